```python
import jax, jax.numpy as jnp
from jax import lax
import numpy as np

D_MODEL = 1024
BATCH = 16
SEQ = 2048
DEPTH = 1

N_META = 16
D_MIX = D_MODEL
MLA_HEADS = 4
QK_NOPE_DIM = 128
QK_ROPE_DIM = 64
V_HEAD_DIM = 128
MLA_WIDTH = MLA_HEADS * V_HEAD_DIM
Q_LORA_RANK = 256
KV_LORA_RANK = 128
ROPE_THETA = 10000.0
ATTN_SCALE = (QK_NOPE_DIM + QK_ROPE_DIM) ** -0.5
Q_BLOCK = 128
NEG_INF = -1e30
CONV_WIDTH = D_MIX - MLA_WIDTH
CONV_GROUPS = 8
CONV_KSIZE = 3
IN_SPLITS = (Q_LORA_RANK, KV_LORA_RANK, QK_ROPE_DIM, MLA_WIDTH,
             CONV_WIDTH, CONV_WIDTH, CONV_WIDTH, CONV_WIDTH)
IN_PROJ_DIM = sum(IN_SPLITS)
EPS = 1e-6

kernel_name = "hymba_mla_shortconv_hybrid"


def rms_norm(x, g):
    xf = x.astype(jnp.float32)
    y = xf * lax.rsqrt(jnp.mean(xf * xf, axis=-1, keepdims=True) + EPS)
    return (y * g.astype(jnp.float32)).astype(x.dtype)


def apply_rope(x, pos):
    half = x.shape[-1] // 2
    inv_freq = 1.0 / (ROPE_THETA ** (jnp.arange(half, dtype=jnp.float32) / half))
    ang = pos.astype(jnp.float32)[:, None] * inv_freq[None, :]
    cos = jnp.cos(ang)[None, :, None, :]
    sin = jnp.sin(ang)[None, :, None, :]
    xf = x.astype(jnp.float32)
    x1, x2 = xf[..., :half], xf[..., half:]
    return jnp.concatenate([x1 * cos - x2 * sin, x2 * cos + x1 * sin], axis=-1).astype(x.dtype)


def _attend_block(q_blk, q_pos, k, v, k_pos):
    s = jnp.einsum('bqhd,bkhd->bhqk', q_blk, k, preferred_element_type=jnp.float32) * ATTN_SCALE
    mask = k_pos[None, :] <= q_pos[:, None]
    s = jnp.where(mask[None, None], s, NEG_INF)
    p = jax.nn.softmax(s, axis=-1)
    return jnp.einsum('bhqk,bkhd->bqhd', p.astype(v.dtype), v)


def causal_attention(q, k, v):
    B, T, H, _ = q.shape
    pos = jnp.arange(T)
    o_meta = _attend_block(q[:, :N_META], pos[:N_META], k[:, :N_META], v[:, :N_META], pos[:N_META])
    n_blk = (T - N_META) // Q_BLOCK
    q_real = q[:, N_META:].reshape(B, n_blk, Q_BLOCK, H, q.shape[-1]).transpose(1, 0, 2, 3, 4)
    q_pos = pos[N_META:].reshape(n_blk, Q_BLOCK)
    o_real = lax.map(lambda a: _attend_block(a[0], a[1], k, v, pos), (q_real, q_pos))
    o_real = o_real.transpose(1, 0, 2, 3, 4).reshape(B, T - N_META, H, v.shape[-1])
    return jnp.concatenate([o_meta, o_real], axis=1)


def causal_dwconv(u, w):
    C = u.shape[-1]
    return lax.conv_general_dilated(
        u, w[:, None, :].astype(u.dtype), window_strides=(1,), padding=[(CONV_KSIZE - 1, 0)],
        dimension_numbers=('NWC', 'WIO', 'NWC'), feature_group_count=C)


def hybrid_layer(h, norm_g, w_in, q_norm_g, w_q_up, kv_norm_g, w_kv_up, conv_w,
                 attn_out_g, conv_out_g, w_out):
    B, T, _ = h.shape
    pos = jnp.arange(T)
    u = rms_norm(h, norm_g)
    p = u @ w_in
    idx = np.cumsum(IN_SPLITS)[:-1].tolist()
    c_q, c_kv, k_rope, z_attn, conv_b, conv_c, conv_h, z_conv = jnp.split(p, idx, axis=-1)

    q = (rms_norm(c_q, q_norm_g) @ w_q_up).reshape(B, T, MLA_HEADS, QK_NOPE_DIM + QK_ROPE_DIM)
    q_nope, q_pe = q[..., :QK_NOPE_DIM], q[..., QK_NOPE_DIM:]
    q_pe = apply_rope(q_pe, pos)
    kv = (rms_norm(c_kv, kv_norm_g) @ w_kv_up).reshape(B, T, MLA_HEADS, QK_NOPE_DIM + V_HEAD_DIM)
    k_nope, v = kv[..., :QK_NOPE_DIM], kv[..., QK_NOPE_DIM:]
    k_pe = apply_rope(k_rope[:, :, None, :], pos)
    k_pe = jnp.broadcast_to(k_pe, (B, T, MLA_HEADS, QK_ROPE_DIM))
    q_full = jnp.concatenate([q_nope, q_pe], axis=-1)
    k_full = jnp.concatenate([k_nope, k_pe], axis=-1)
    o = causal_attention(q_full, k_full, v)
    o = rms_norm(o, attn_out_g.reshape(MLA_HEADS, V_HEAD_DIM)).reshape(B, T, MLA_WIDTH)
    y_attn = o * jax.nn.silu(z_attn)

    yc = conv_b * causal_dwconv(conv_c * conv_h, conv_w)
    yc = rms_norm(yc.reshape(B, T, CONV_GROUPS, CONV_WIDTH // CONV_GROUPS),
                  conv_out_g.reshape(CONV_GROUPS, CONV_WIDTH // CONV_GROUPS)).reshape(B, T, CONV_WIDTH)
    y_conv = yc * jax.nn.silu(z_conv)

    mix = jnp.concatenate([y_attn, y_conv], axis=-1) @ w_out
    return h + mix


def _fwd_setup_inputs(seed: int = 0) -> dict:
    key = jax.random.key(seed)
    ks = jax.random.split(key, 16)
    f32 = jnp.float32

    def w(k, shape, fan_in):
        return jax.random.normal(k, shape, f32) * fan_in ** -0.5

    def gain(k, shape):
        return 1.0 + 0.02 * jax.random.normal(k, shape, f32)

    return {
        "x": jax.random.normal(ks[0], (BATCH, SEQ, D_MODEL), f32),
        "meta_tokens": jax.random.normal(ks[1], (N_META, D_MODEL), f32),
        "norm_g": gain(ks[2], (DEPTH, D_MODEL)),
        "w_in": w(ks[3], (DEPTH, D_MODEL, IN_PROJ_DIM), D_MODEL),
        "q_norm_g": gain(ks[4], (DEPTH, Q_LORA_RANK)),
        "w_q_up": w(ks[5], (DEPTH, Q_LORA_RANK, MLA_HEADS * (QK_NOPE_DIM + QK_ROPE_DIM)), Q_LORA_RANK),
        "kv_norm_g": gain(ks[6], (DEPTH, KV_LORA_RANK)),
        "w_kv_up": w(ks[7], (DEPTH, KV_LORA_RANK, MLA_HEADS * (QK_NOPE_DIM + V_HEAD_DIM)), KV_LORA_RANK),
        "conv_w": w(ks[8], (DEPTH, CONV_KSIZE, CONV_WIDTH), CONV_KSIZE),
        "attn_out_g": gain(ks[9], (DEPTH, MLA_WIDTH)),
        "conv_out_g": gain(ks[10], (DEPTH, CONV_WIDTH)),
        "w_out": w(ks[11], (DEPTH, D_MIX, D_MODEL), D_MIX),
        "final_norm_g": gain(ks[12], (D_MODEL,)),
    }


def _fwd_reference(x, meta_tokens, norm_g, w_in, q_norm_g, w_q_up, kv_norm_g, w_kv_up, conv_w,
              attn_out_g, conv_out_g, w_out, final_norm_g):
    B = x.shape[0]
    meta = jnp.broadcast_to(meta_tokens[None].astype(x.dtype), (B, N_META, x.shape[-1]))
    h = jnp.concatenate([meta, x], axis=1)
    for l in range(DEPTH):
        h = hybrid_layer(h, norm_g[l], w_in[l], q_norm_g[l], w_q_up[l], kv_norm_g[l], w_kv_up[l],
                         conv_w[l], attn_out_g[l], conv_out_g[l], w_out[l])
    return rms_norm(h, final_norm_g)[:, N_META:]


import jax as _jax
import jax.numpy as _jnp

TWIN_FORMAT = 'train_step'
FWD_PARAMS = ['x', 'meta_tokens', 'norm_g', 'w_in', 'q_norm_g', 'w_q_up', 'kv_norm_g', 'w_kv_up', 'conv_w', 'attn_out_g', 'conv_out_g', 'w_out', 'final_norm_g']
TWIN_WEIGHTS = ['meta_tokens', 'norm_g', 'w_in', 'q_norm_g', 'w_q_up', 'kv_norm_g', 'w_kv_up', 'conv_w', 'attn_out_g', 'conv_out_g', 'w_out', 'final_norm_g']
TWIN_DIFF_INPUT = 'x'
TWIN_INPUTS = ['x', 'meta_tokens', 'norm_g', 'w_in', 'q_norm_g', 'w_q_up', 'kv_norm_g', 'w_kv_up', 'conv_w', 'attn_out_g', 'conv_out_g', 'w_out', 'final_norm_g', 'loss_target', 'm_meta_tokens', 'm_norm_g', 'm_w_in', 'm_q_norm_g', 'm_w_q_up', 'm_kv_norm_g', 'm_w_kv_up', 'm_conv_w', 'm_attn_out_g', 'm_conv_out_g', 'm_w_out', 'm_final_norm_g', 'v_meta_tokens', 'v_norm_g', 'v_w_in', 'v_q_norm_g', 'v_w_q_up', 'v_kv_norm_g', 'v_w_kv_up', 'v_conv_w', 'v_attn_out_g', 'v_conv_out_g', 'v_w_out', 'v_final_norm_g']
TWIN_OUTPUTS = ['loss', 'grad_x', 'grad_meta_tokens', 'grad_norm_g', 'grad_w_in', 'grad_q_norm_g', 'grad_w_q_up', 'grad_kv_norm_g', 'grad_w_kv_up', 'grad_conv_w', 'grad_attn_out_g', 'grad_conv_out_g', 'grad_w_out', 'grad_final_norm_g', 'delta_meta_tokens', 'delta_norm_g', 'delta_w_in', 'delta_q_norm_g', 'delta_w_q_up', 'delta_kv_norm_g', 'delta_w_kv_up', 'delta_conv_w', 'delta_attn_out_g', 'delta_conv_out_g', 'delta_w_out', 'delta_final_norm_g', 'new_m_meta_tokens', 'new_m_norm_g', 'new_m_w_in', 'new_m_q_norm_g', 'new_m_w_q_up', 'new_m_kv_norm_g', 'new_m_w_kv_up', 'new_m_conv_w', 'new_m_attn_out_g', 'new_m_conv_out_g', 'new_m_w_out', 'new_m_final_norm_g', 'new_v_meta_tokens', 'new_v_norm_g', 'new_v_w_in', 'new_v_q_norm_g', 'new_v_w_q_up', 'new_v_kv_norm_g', 'new_v_w_kv_up', 'new_v_conv_w', 'new_v_attn_out_g', 'new_v_conv_out_g', 'new_v_w_out', 'new_v_final_norm_g']
TWIN_LEAF_KINDS = {'loss': 'loss', 'grad_x': 'grad_x', 'grad_meta_tokens': 'grad_w', 'grad_norm_g': 'grad_w', 'grad_w_in': 'grad_w', 'grad_q_norm_g': 'grad_w', 'grad_w_q_up': 'grad_w', 'grad_kv_norm_g': 'grad_w', 'grad_w_kv_up': 'grad_w', 'grad_conv_w': 'grad_w', 'grad_attn_out_g': 'grad_w', 'grad_conv_out_g': 'grad_w', 'grad_w_out': 'grad_w', 'grad_final_norm_g': 'grad_w', 'delta_meta_tokens': 'delta_w', 'delta_norm_g': 'delta_w', 'delta_w_in': 'delta_w', 'delta_q_norm_g': 'delta_w', 'delta_w_q_up': 'delta_w', 'delta_kv_norm_g': 'delta_w', 'delta_w_kv_up': 'delta_w', 'delta_conv_w': 'delta_w', 'delta_attn_out_g': 'delta_w', 'delta_conv_out_g': 'delta_w', 'delta_w_out': 'delta_w', 'delta_final_norm_g': 'delta_w', 'new_m_meta_tokens': 'new_m', 'new_m_norm_g': 'new_m', 'new_m_w_in': 'new_m', 'new_m_q_norm_g': 'new_m', 'new_m_w_q_up': 'new_m', 'new_m_kv_norm_g': 'new_m', 'new_m_w_kv_up': 'new_m', 'new_m_conv_w': 'new_m', 'new_m_attn_out_g': 'new_m', 'new_m_conv_out_g': 'new_m', 'new_m_w_out': 'new_m', 'new_m_final_norm_g': 'new_m', 'new_v_meta_tokens': 'new_v', 'new_v_norm_g': 'new_v', 'new_v_w_in': 'new_v', 'new_v_q_norm_g': 'new_v', 'new_v_w_q_up': 'new_v', 'new_v_kv_norm_g': 'new_v', 'new_v_w_kv_up': 'new_v', 'new_v_conv_w': 'new_v', 'new_v_attn_out_g': 'new_v', 'new_v_conv_out_g': 'new_v', 'new_v_w_out': 'new_v', 'new_v_final_norm_g': 'new_v'}


def _forward(args):
    return _fwd_reference(*[args[k] for k in FWD_PARAMS])


def _output_shape():
    out = _jax.eval_shape(lambda: _forward(_fwd_setup_inputs(0)))
    return out.shape, out.dtype

N_MICROBATCH = 1
ADAM_LR = 0.001
ADAM_B1 = 0.9
ADAM_B2 = 0.999
ADAM_EPS = 1e-08
ADAM_WD = 0.01
ADAM_STEP = 10
PER_EXAMPLE_BATCH_AXIS = {'x': 0, 'loss_target': 0}
SHARED_INPUTS = []
_WEIGHT_DTYPES = {'meta_tokens': _jnp.float32, 'norm_g': _jnp.float32, 'w_in': _jnp.float32, 'q_norm_g': _jnp.float32, 'w_q_up': _jnp.float32, 'kv_norm_g': _jnp.float32, 'w_kv_up': _jnp.float32, 'conv_w': _jnp.float32, 'attn_out_g': _jnp.float32, 'conv_out_g': _jnp.float32, 'w_out': _jnp.float32, 'final_norm_g': _jnp.float32}
MOMENT_SCALE = {'meta_tokens': 8.626313e-03, 'norm_g': 1.961738e-01, 'w_in': 1.045862e-01, 'q_norm_g': 1.337332e-01, 'w_q_up': 7.594899e-02, 'kv_norm_g': 3.354338e-01, 'w_kv_up': 8.768146e-02, 'conv_w': 9.268744e-02, 'attn_out_g': 9.379558e-02, 'conv_out_g': 9.157394e-02, 'w_out': 9.071870e-02, 'final_norm_g': 3.203430e+01}


def _to_microbatches(a, axis):
    t = _jnp.moveaxis(a, axis, 0)
    t = t.reshape((N_MICROBATCH, t.shape[0] // N_MICROBATCH) + t.shape[1:])
    return _jnp.moveaxis(t, 1, axis + 1)


def setup_inputs(seed: int = 0) -> dict:
    inp = _fwd_setup_inputs(seed)
    key = _jax.random.fold_in(_jax.random.key(seed), 7919)
    shape, _ = _output_shape()
    out = dict(inp)
    out["loss_target"] = _jax.random.normal(_jax.random.fold_in(key, 0), shape, _jnp.float32)
    for i, name in enumerate(TWIN_WEIGHTS):
        w = inp[name].astype(_jnp.float32)
        if MOMENT_SCALE is None:
            s = _jnp.sqrt(_jnp.mean(_jnp.square(w)) + 1e-30)
        else:
            s = MOMENT_SCALE[name]
        km, kv = _jax.random.split(_jax.random.fold_in(key, i + 1))
        out[name] = w
        out["m_" + name] = s * _jax.random.normal(km, w.shape, _jnp.float32)
        out["v_" + name] = (s * s) * _jax.random.uniform(kv, w.shape, _jnp.float32, 0.5, 1.5)
    if N_MICROBATCH > 1:
        for name, axis in PER_EXAMPLE_BATCH_AXIS.items():
            out[name] = _to_microbatches(out[name], axis)
    return {'x': out['x'], 'meta_tokens': out['meta_tokens'], 'norm_g': out['norm_g'], 'w_in': out['w_in'], 'q_norm_g': out['q_norm_g'], 'w_q_up': out['w_q_up'], 'kv_norm_g': out['kv_norm_g'], 'w_kv_up': out['w_kv_up'], 'conv_w': out['conv_w'], 'attn_out_g': out['attn_out_g'], 'conv_out_g': out['conv_out_g'], 'w_out': out['w_out'], 'final_norm_g': out['final_norm_g'], 'loss_target': out['loss_target'], 'm_meta_tokens': out['m_meta_tokens'], 'm_norm_g': out['m_norm_g'], 'm_w_in': out['m_w_in'], 'm_q_norm_g': out['m_q_norm_g'], 'm_w_q_up': out['m_w_q_up'], 'm_kv_norm_g': out['m_kv_norm_g'], 'm_w_kv_up': out['m_w_kv_up'], 'm_conv_w': out['m_conv_w'], 'm_attn_out_g': out['m_attn_out_g'], 'm_conv_out_g': out['m_conv_out_g'], 'm_w_out': out['m_w_out'], 'm_final_norm_g': out['m_final_norm_g'], 'v_meta_tokens': out['v_meta_tokens'], 'v_norm_g': out['v_norm_g'], 'v_w_in': out['v_w_in'], 'v_q_norm_g': out['v_q_norm_g'], 'v_w_q_up': out['v_w_q_up'], 'v_kv_norm_g': out['v_kv_norm_g'], 'v_w_kv_up': out['v_w_kv_up'], 'v_conv_w': out['v_conv_w'], 'v_attn_out_g': out['v_attn_out_g'], 'v_conv_out_g': out['v_conv_out_g'], 'v_w_out': out['v_w_out'], 'v_final_norm_g': out['v_final_norm_g']}


def _loss(weights, diff, rest, loss_target):
    with _jax.named_scope("forward"):
        args = {**rest, TWIN_DIFF_INPUT: diff, **{k: w.astype(_WEIGHT_DTYPES[k]) for k, w in weights.items()}}
        y = _forward(args)
    with _jax.named_scope("loss_head"):
        err = _jnp.square(y.astype(_jnp.float32) - loss_target)
        return 0.5 * _jnp.sum(_jnp.mean(err, axis=-1)) if err.ndim else 0.5 * err


def _adamw(w, g, m, v):
    m = ADAM_B1 * m + (1.0 - ADAM_B1) * g
    v = ADAM_B2 * v + (1.0 - ADAM_B2) * _jnp.square(g)
    m_hat = m / (1.0 - ADAM_B1 ** ADAM_STEP)
    v_hat = v / (1.0 - ADAM_B2 ** ADAM_STEP)
    delta = -ADAM_LR * (m_hat / (_jnp.sqrt(v_hat) + ADAM_EPS) + ADAM_WD * w)
    return delta, m, v


def reference(x, meta_tokens, norm_g, w_in, q_norm_g, w_q_up, kv_norm_g, w_kv_up, conv_w, attn_out_g, conv_out_g, w_out, final_norm_g, loss_target, m_meta_tokens, m_norm_g, m_w_in, m_q_norm_g, m_w_q_up, m_kv_norm_g, m_w_kv_up, m_conv_w, m_attn_out_g, m_conv_out_g, m_w_out, m_final_norm_g, v_meta_tokens, v_norm_g, v_w_in, v_q_norm_g, v_w_q_up, v_kv_norm_g, v_w_kv_up, v_conv_w, v_attn_out_g, v_conv_out_g, v_w_out, v_final_norm_g):
    given = dict(x=x, meta_tokens=meta_tokens, norm_g=norm_g, w_in=w_in, q_norm_g=q_norm_g, w_q_up=w_q_up, kv_norm_g=kv_norm_g, w_kv_up=w_kv_up, conv_w=conv_w, attn_out_g=attn_out_g, conv_out_g=conv_out_g, w_out=w_out, final_norm_g=final_norm_g, loss_target=loss_target, m_meta_tokens=m_meta_tokens, m_norm_g=m_norm_g, m_w_in=m_w_in, m_q_norm_g=m_q_norm_g, m_w_q_up=m_w_q_up, m_kv_norm_g=m_kv_norm_g, m_w_kv_up=m_w_kv_up, m_conv_w=m_conv_w, m_attn_out_g=m_attn_out_g, m_conv_out_g=m_conv_out_g, m_w_out=m_w_out, m_final_norm_g=m_final_norm_g, v_meta_tokens=v_meta_tokens, v_norm_g=v_norm_g, v_w_in=v_w_in, v_q_norm_g=v_q_norm_g, v_w_q_up=v_w_q_up, v_kv_norm_g=v_kv_norm_g, v_w_kv_up=v_w_kv_up, v_conv_w=v_conv_w, v_attn_out_g=v_attn_out_g, v_conv_out_g=v_conv_out_g, v_w_out=v_w_out, v_final_norm_g=v_final_norm_g)
    weights = {n: given[n] for n in TWIN_WEIGHTS}
    shared = {n: given[n] for n in SHARED_INPUTS}
    per_example = {n: given[n] for n in ['x']}
    grad_fn = _jax.value_and_grad(_loss, argnums=(0, 1))

    def one_microbatch(ex, loss_target):
        ex = dict(ex)
        diff = ex.pop(TWIN_DIFF_INPUT)
        return grad_fn(weights, diff, {**shared, **ex}, loss_target)

    if N_MICROBATCH == 1:
        loss, (grad_w, grad_x) = one_microbatch(per_example, given["loss_target"])
    else:
        def body(carry, xs):
            loss_sum, grad_sum = carry
            l_k, (gw_k, gx_k) = one_microbatch(xs[0], xs[1])
            with _jax.named_scope("update"):
                return (loss_sum + l_k, _jax.tree.map(_jnp.add, grad_sum, gw_k)), gx_k

        init = (_jnp.zeros((), _jnp.float32), _jax.tree.map(_jnp.zeros_like, weights))
        (loss, grad_w), grad_x = _jax.lax.scan(body, init, (per_example, given["loss_target"]))
    with _jax.named_scope("update"):
        delta_w, new_m, new_v = {}, {}, {}
        for n in TWIN_WEIGHTS:
            delta_w[n], new_m[n], new_v[n] = _adamw(weights[n], grad_w[n], given["m_" + n], given["v_" + n])
    return (loss, grad_x, *[grad_w[n] for n in TWIN_WEIGHTS], *[delta_w[n] for n in TWIN_WEIGHTS],
            *[new_m[n] for n in TWIN_WEIGHTS], *[new_v[n] for n in TWIN_WEIGHTS])
```

```python
import functools

import jax
import jax.numpy as jnp
from jax import lax
from jax.experimental import pallas as pl
from jax.experimental.pallas import tpu as pltpu

F32 = jnp.float32
BF16 = jnp.bfloat16

N_META = 16
D_MODEL = 1024
N_HEADS = 4
D_NOPE = 128
D_ROPE = 64
D_V = 128
Q_RANK = 256
KV_RANK = 128
CONV_WIDTH = 512
CONV_GROUP = 64
ROPE_THETA = 10000.0
ATTN_SCALE = (D_NOPE + D_ROPE) ** -0.5
EPS = 1e-6
NEG_INF = -1e30

ADAM_LR = 0.001
ADAM_B1 = 0.9
ADAM_B2 = 0.999
ADAM_EPS = 1e-08
ADAM_WD = 0.01
ADAM_STEP = 10

LANES = 128
PAD_FRONT = LANES - N_META
KV_TILE = 256
N_DEV = 8
VMEM_LIMIT = 56 * 1024 * 1024

IN_PAD = 3072
GRP_A = 512

BIG_ROWS = 544
SMALL_ROWS = 8
GATHER_ROWS = BIG_ROWS + 2 * SMALL_ROWS
N_REPL = 1024 + 256 + 128 + 512 + 512 + 1024
SMALL_USED = 192 + 2048 + N_REPL + 1
ADAM_CHUNK = 32


def _params(*sem):
    return pltpu.CompilerParams(dimension_semantics=sem, vmem_limit_bytes=VMEM_LIMIT)


def _rms_stats(x):
    r = lax.rsqrt(jnp.mean(x * x, axis=-1, keepdims=True) + EPS)
    return x * r, r


def _rms_bwd(gdy, xhat, r):
    return r * (gdy - xhat * jnp.mean(gdy * xhat, axis=-1, keepdims=True))


def _sigmoid(z):
    return 1.0 / (1.0 + jnp.exp(-z))


def _group_mean(x):
    i0 = lax.broadcasted_iota(jnp.int32, (LANES, LANES), 0) // CONV_GROUP
    i1 = lax.broadcasted_iota(jnp.int32, (LANES, LANES), 1) // CONV_GROUP
    m = jnp.where(i0 == i1, 1.0 / CONV_GROUP, 0.0).astype(BF16)
    hi = x.astype(BF16)
    lo = (x - hi.astype(F32)).astype(BF16)
    return jnp.dot(hi, m, preferred_element_type=F32) + jnp.dot(lo, m, preferred_element_type=F32)


_NT = (((1,), (1,)), ((), ()))
_TN = (((0,), (0,)), ((), ()))


def _dot(a, b, dims=None):
    if dims is None:
        return jnp.dot(a, b, preferred_element_type=F32)
    return lax.dot_general(a, b, dims, preferred_element_type=F32)


def _all_gather(shard):
    m_per, n = shard.shape

    def body(x_ref, out_ref, send_sems, recv_sems, local_sem):
        x, y, c = lax.axis_index("x"), lax.axis_index("y"), lax.axis_index("c")
        me, sibling = (x, y, c), (x, y, 1 - c)
        chips = [(1 - x, y), (x, 1 - y), (1 - x, 1 - y)]

        def rows(px, py, pc):
            return out_ref.at[pl.ds((4 * px + 2 * py + pc) * m_per, m_per), :]

        def copy(k, block, to, src=None):
            return pltpu.make_async_remote_copy(
                src_ref=rows(*block) if src is None else src,
                dst_ref=rows(*block),
                send_sem=send_sems.at[k],
                recv_sem=recv_sems.at[k],
                device_id=to,
                device_id_type=pl.DeviceIdType.MESH,
            )

        mine = pltpu.make_async_copy(x_ref, rows(*me), local_sem)
        mine.start()
        first = [copy(0, me, sibling, src=x_ref)]
        first += [copy(1 + j, me, (*chip, c), src=x_ref) for j, chip in enumerate(chips)]
        for cp in first:
            cp.start()
        passed = [copy(4 + j, (*chip, c), sibling) for j, chip in enumerate(chips)]
        for j, chip in enumerate(chips):
            copy(1 + j, (*chip, c), me).wait_recv()
            passed[j].start()
        copy(0, sibling, me).wait_recv()
        for j, chip in enumerate(chips):
            copy(4 + j, (*chip, 1 - c), me).wait_recv()
        for cp in first + passed:
            cp.wait_send()
        mine.wait()

    return pl.pallas_call(
        body,
        name="all_gather_weights",
        out_shape=jax.ShapeDtypeStruct((N_DEV * m_per, n), shard.dtype),
        in_specs=[pl.BlockSpec(memory_space=pltpu.VMEM)],
        out_specs=pl.BlockSpec(memory_space=pltpu.VMEM),
        scratch_shapes=[
            pltpu.SemaphoreType.DMA((7,)),
            pltpu.SemaphoreType.DMA((7,)),
            pltpu.SemaphoreType.DMA,
        ],
        compiler_params=pltpu.CompilerParams(vmem_limit_bytes=VMEM_LIMIT),
    )(shard)


def _reduce_adam(big, small, w, m, v):
    rows = BIG_ROWS + SMALL_ROWS
    n = big.shape[-1]

    def body(big_ref, small_ref, w_ref, m_ref, v_ref, g_out, d_out, m_out, v_out,
             rbig, rsmall, send_sems, recv_sems):
        x, y, c = lax.axis_index("x"), lax.axis_index("y"), lax.axis_index("c")
        me = 4 * x + 2 * y + c

        copies = []
        for r in range(1, N_DEV):
            peer = (1 - x if r & 4 else x, 1 - y if r & 2 else y, 1 - c if r & 1 else c)
            for j, (src, dst) in enumerate(((big_ref, rbig), (small_ref, rsmall))):
                copies.append(pltpu.make_async_remote_copy(
                    src_ref=src.at[4 * peer[0] + 2 * peer[1] + peer[2]],
                    dst_ref=dst.at[r],
                    send_sem=send_sems.at[2 * (r - 1) + j],
                    recv_sem=recv_sems.at[2 * (r - 1) + j],
                    device_id=peer,
                    device_id_type=pl.DeviceIdType.MESH,
                ))
        for cp in copies:
            cp.start()
        rsmall[0] = small_ref[me]
        for cp in copies:
            cp.wait_recv()

        def adam(g, sl):
            w_, m_, v_ = w_ref[sl, :], m_ref[sl, :], v_ref[sl, :]
            m_new = ADAM_B1 * m_ + (1.0 - ADAM_B1) * g
            v_new = ADAM_B2 * v_ + (1.0 - ADAM_B2) * (g * g)
            m_hat = m_new / (1.0 - ADAM_B1 ** ADAM_STEP)
            v_hat = v_new / (1.0 - ADAM_B2 ** ADAM_STEP)
            g_out[sl, :] = g
            d_out[sl, :] = -ADAM_LR * (m_hat / (jnp.sqrt(v_hat) + ADAM_EPS) + ADAM_WD * w_)
            m_out[sl, :] = m_new
            v_out[sl, :] = v_new

        def chunk(i, carry):
            sl = pl.ds(pl.multiple_of(i * ADAM_CHUNK, ADAM_CHUNK), ADAM_CHUNK)
            g = big_ref[me, sl, :].astype(F32)
            for r in range(1, N_DEV):
                g = g + rbig[r, sl, :].astype(F32)
            adam(g, sl)
            return carry

        lax.fori_loop(0, BIG_ROWS // ADAM_CHUNK, chunk, 0)

        gs = rsmall[me]
        for d in range(1, N_DEV):
            gs = gs + rsmall[d ^ me]
        adam(gs, pl.ds(BIG_ROWS, SMALL_ROWS))

        for cp in copies:
            cp.wait_send()

    vm = pl.BlockSpec(memory_space=pltpu.VMEM)
    out = jax.ShapeDtypeStruct((rows, n), F32)
    return pl.pallas_call(
        body,
        name="reduce_grads_adamw",
        out_shape=(out, out, out, out),
        in_specs=[vm] * 5,
        out_specs=(vm, vm, vm, vm),
        scratch_shapes=[
            pltpu.VMEM((N_DEV, BIG_ROWS, n), BF16),
            pltpu.VMEM((N_DEV, SMALL_ROWS, n), F32),
            pltpu.SemaphoreType.DMA((2 * (N_DEV - 1),)),
            pltpu.SemaphoreType.DMA((2 * (N_DEV - 1),)),
        ],
        compiler_params=pltpu.CompilerParams(vmem_limit_bytes=VMEM_LIMIT),
    )(big, small, w, m, v)


def _prep(x, meta, norm_g):
    nb_seq, s, d = x.shape
    nb = s // LANES + 1

    def body(x_ref, meta_ref, g_ref, u_ref):
        j = pl.program_id(1)

        def norm(h):
            hhat, _ = _rms_stats(h)
            return (hhat * g_ref[...]).astype(BF16)

        @pl.when(j == 0)
        def _():
            u_ref[0:PAD_FRONT, :] = jnp.zeros((PAD_FRONT, d), BF16)
            u_ref[PAD_FRONT:LANES, :] = norm(meta_ref[...])

        @pl.when(j > 0)
        def _():
            u_ref[...] = norm(x_ref[0])

    return pl.pallas_call(
        body,
        name="prep_norm",
        grid=(nb_seq, nb),
        in_specs=[
            pl.BlockSpec((1, LANES, d), lambda b, j: (b, jnp.maximum(j - 1, 0), 0)),
            pl.BlockSpec((N_META, d), lambda b, j: (0, 0)),
            pl.BlockSpec((1, d), lambda b, j: (0, 0)),
        ],
        out_specs=pl.BlockSpec((LANES, d), lambda b, j: (b * nb + j, 0)),
        out_shape=jax.ShapeDtypeStruct((nb_seq * nb * LANES, d), BF16),
        compiler_params=_params("parallel", "arbitrary"),
    )(x, meta, norm_g)


def _matmul(a, b, bm, bn, out_dtype, name):
    m, k = a.shape
    _, n = b.shape

    def body(a_ref, b_ref, o_ref):
        o_ref[...] = _dot(a_ref[...], b_ref[...]).astype(out_dtype)

    return pl.pallas_call(
        body,
        name=name,
        grid=(m // bm, n // bn),
        in_specs=[pl.BlockSpec((bm, k), lambda i, j: (i, 0)), pl.BlockSpec((k, bn), lambda i, j: (0, j))],
        out_specs=pl.BlockSpec((bm, bn), lambda i, j: (i, j)),
        out_shape=jax.ShapeDtypeStruct((m, n), out_dtype),
        compiler_params=_params("parallel", "arbitrary"),
    )(a, b)


def _rope_tables(tp):
    half = D_ROPE // 2
    inv_freq = 1.0 / (ROPE_THETA ** (jnp.arange(half, dtype=F32) / half))
    pos = (jnp.arange(tp) - PAD_FRONT).astype(F32)
    ang = pos[:, None] * inv_freq[None, :]
    cos = jnp.tile(jnp.cos(ang), (1, LANES // half))
    sin = jnp.tile(jnp.sin(ang), (1, LANES // half))
    first = (jnp.arange(LANES) % D_ROPE) < half
    return cos, jnp.where(first, -sin, 0.0), jnp.where(first, 0.0, sin)


def _rope(t, cos, sa, sb):
    return t * cos + pltpu.roll(t, LANES - D_ROPE // 2, 1) * sa + pltpu.roll(t, D_ROPE // 2, 1) * sb


def _rope_t(t, cos, sa, sb):
    return t * cos + pltpu.roll(t * sa, D_ROPE // 2, 1) + pltpu.roll(t * sb, LANES - D_ROPE // 2, 1)


def _qkv_fwd(p, wq, wkv, gq, gkv, tables, nb_seq, tp):
    ht = tp // 2

    def body(pa_ref, wq_ref, wkv_ref, gq_ref, gkv_ref, cos_ref, sa_ref, sb_ref, q_ref, k_ref, v_ref):
        pa = pa_ref[...].astype(F32)
        cq_hat, _ = _rms_stats(pa[:, :Q_RANK])
        ckv_hat, _ = _rms_stats(pa[:, Q_RANK:Q_RANK + KV_RANK])
        q = _dot((cq_hat * gq_ref[...]).astype(BF16), wq_ref[...])
        kv = _dot((ckv_hat * gkv_ref[...]).astype(BF16), wkv_ref[...])
        tabs = (cos_ref[...], sa_ref[...], sb_ref[...])
        low = lax.broadcasted_iota(jnp.int32, (ht, LANES), 1) < D_ROPE
        k_pe = _rope(pa[:, Q_RANK + KV_RANK:], *tabs)
        pairs = [_rope(q[:, N_HEADS * D_NOPE + LANES * i:N_HEADS * D_NOPE + LANES * (i + 1)], *tabs) for i in range(2)]
        for h in range(N_HEADS):
            pair = pairs[h // 2]
            if h % 2:
                pair = pltpu.roll(pair, D_ROPE, 1)
            pe = jnp.where(low, pair, 0.0)
            q_ref[0, h] = jnp.concatenate([q[:, D_NOPE * h:D_NOPE * (h + 1)], pe], axis=1).astype(BF16)
            k_ref[0, h] = jnp.concatenate([kv[:, D_NOPE * h:D_NOPE * (h + 1)], k_pe], axis=1).astype(BF16)
            v_ref[0, h] = kv[:, N_HEADS * D_NOPE + D_V * h:N_HEADS * D_NOPE + D_V * (h + 1)].astype(BF16)

    full = lambda a: pl.BlockSpec(a.shape, lambda i: (0,) * a.ndim)
    tab = pl.BlockSpec((ht, LANES), lambda i: (i % 2, 0))
    qk = pl.BlockSpec((1, N_HEADS, ht, 2 * LANES), lambda i: (i // 2, 0, i % 2, 0))
    return pl.pallas_call(
        body,
        name="qkv_fwd",
        grid=(2 * nb_seq,),
        in_specs=[pl.BlockSpec((ht, GRP_A), lambda i: (i, 0)), full(wq), full(wkv), full(gq), full(gkv), tab, tab, tab],
        out_specs=(qk, qk, pl.BlockSpec((1, N_HEADS, ht, D_V), lambda i: (i // 2, 0, i % 2, 0))),
        out_shape=(
            jax.ShapeDtypeStruct((nb_seq, N_HEADS, tp, 2 * LANES), BF16),
            jax.ShapeDtypeStruct((nb_seq, N_HEADS, tp, 2 * LANES), BF16),
            jax.ShapeDtypeStruct((nb_seq, N_HEADS, tp, D_V), BF16),
        ),
        compiler_params=_params("parallel"),
    )(p, wq, wkv, gq, gkv, *tables)


def _attn_fwd(q, k, v, p, g_attn):
    nb_seq, _, tp, _ = q.shape

    def body(q_ref, k_ref, v_ref, z_ref, g_ref, y_ref, o_ref, lse_ref):
        g = g_ref[...]
        for r0 in range(0, tp, KV_TILE):
            nq = min(KV_TILE, tp - r0)
            kend = r0 + nq
            s = _dot(q_ref[0, 0, r0:kend, :], k_ref[0, 0, 0:kend, :], _NT) * ATTN_SCALE
            qpos = r0 + lax.broadcasted_iota(jnp.int32, (nq, kend), 0)
            kpos = lax.broadcasted_iota(jnp.int32, (nq, kend), 1)
            s = jnp.where((kpos <= qpos) & (kpos >= PAD_FRONT), s, NEG_INF)
            m = jnp.max(s, axis=-1, keepdims=True)
            e = jnp.exp(s - m)
            l = jnp.sum(e, axis=-1, keepdims=True)
            o = _dot(e.astype(BF16), v_ref[0, 0, 0:kend, :]) * (1.0 / l)
            o_ref[0, 0, r0:kend, :] = o
            lse_ref[0, 0, r0:kend, :] = jnp.broadcast_to(m + jnp.log(l), (nq, LANES))
            ohat, _ = _rms_stats(o)
            z = z_ref[r0:kend, :].astype(F32)
            y_ref[r0:kend, :] = (ohat * g * (z * _sigmoid(z))).astype(BF16)

    qk = pl.BlockSpec((1, 1, tp, 2 * LANES), lambda b, h: (b, h, 0, 0))
    hv = pl.BlockSpec((1, 1, tp, D_V), lambda b, h: (b, h, 0, 0))
    return pl.pallas_call(
        body,
        name="attn_fwd",
        grid=(nb_seq, N_HEADS),
        in_specs=[qk, qk, hv,
                  pl.BlockSpec((tp, LANES), lambda b, h: (b, GRP_A // LANES + h)),
                  pl.BlockSpec((1, LANES), lambda b, h: (0, h))],
        out_specs=(pl.BlockSpec((tp, LANES), lambda b, h: (b, h)), hv, hv),
        out_shape=(
            jax.ShapeDtypeStruct((nb_seq * tp, N_HEADS * D_V), BF16),
            jax.ShapeDtypeStruct((nb_seq, N_HEADS, tp, D_V), F32),
            jax.ShapeDtypeStruct((nb_seq, N_HEADS, tp, LANES), F32),
        ),
        compiler_params=_params("parallel", "parallel"),
    )(q, k, v, p, g_attn)


_CONV_COL0 = (GRP_A + N_HEADS * D_V) // LANES


def _conv_specs(tp, order):
    cols = CONV_WIDTH // LANES
    return [pl.BlockSpec((tp, LANES), functools.partial(
        lambda a, b, off: order(a, b, off), off=_CONV_COL0 + i * cols)) for i in range(4)]


def _conv_fwd(p, conv_w, g_conv, nb_seq, tp):
    def body(b_ref, c_ref, h_ref, z_ref, w_ref, g_ref, y_ref):
        cc = c_ref[...].astype(F32) * h_ref[...].astype(F32)
        row = lax.broadcasted_iota(jnp.int32, (tp, LANES), 0)
        s1 = jnp.where(row >= 1, pltpu.roll(cc, 1, 0), 0.0)
        s2 = jnp.where(row >= 2, pltpu.roll(cc, 2, 0), 0.0)
        yc = b_ref[...].astype(F32) * (w_ref[0:1, :] * s2 + w_ref[1:2, :] * s1 + w_ref[2:3, :] * cc)
        r = lax.rsqrt(_group_mean(yc * yc) + EPS)
        z = z_ref[...].astype(F32)
        y_ref[...] = (yc * r * g_ref[...] * (z * _sigmoid(z))).astype(BF16)

    return pl.pallas_call(
        body,
        name="conv_fwd",
        grid=(nb_seq, CONV_WIDTH // LANES),
        in_specs=_conv_specs(tp, lambda b, t, off: (b, off + t)) + [
            pl.BlockSpec((3, LANES), lambda b, t: (0, t)),
            pl.BlockSpec((1, LANES), lambda b, t: (0, t))],
        out_specs=pl.BlockSpec((tp, LANES), lambda b, t: (b, t)),
        out_shape=jax.ShapeDtypeStruct((nb_seq * tp, CONV_WIDTH), BF16),
        compiler_params=_params("parallel", "parallel"),
    )(p, p, p, p, conv_w, g_conv)


def _out_proj(ya, yc, w_out, bm):
    r, ka = ya.shape
    d = w_out.shape[1]

    def body(a_ref, c_ref, w_ref, o_ref):
        o_ref[...] = _dot(a_ref[...], w_ref[0:ka, :]) + _dot(c_ref[...], w_ref[ka:, :])

    return pl.pallas_call(
        body,
        name="out_proj",
        grid=(r // bm,),
        in_specs=[pl.BlockSpec((bm, ka), lambda i: (i, 0)), pl.BlockSpec((bm, yc.shape[1]), lambda i: (i, 0)),
                  pl.BlockSpec(w_out.shape, lambda i: (0, 0))],
        out_specs=pl.BlockSpec((bm, d), lambda i: (i, 0)),
        out_shape=jax.ShapeDtypeStruct((r, d), F32),
        compiler_params=_params("parallel"),
    )(ya, yc, w_out)


def _loss_bwd(x, mix, target, g_final):
    nb_seq, s, d = x.shape
    nb = s // LANES + 1

    def body(x_ref, mix_ref, t_ref, g_ref, dh_ref, dhb_ref, dg_ref, loss_ref, acc_ref):
        b, j = pl.program_id(0), pl.program_id(1)

        @pl.when((b == 0) & (j == 0))
        def _():
            acc_ref[...] = jnp.zeros_like(acc_ref)
            dg_ref[...] = jnp.zeros_like(dg_ref)

        @pl.when(j == 0)
        def _():
            dh_ref[...] = jnp.zeros_like(dh_ref)
            dhb_ref[...] = jnp.zeros_like(dhb_ref)

        @pl.when(j > 0)
        def _():
            g = g_ref[...]
            hhat, r = _rms_stats(x_ref[0] + mix_ref[...])
            e = hhat * g - t_ref[0]
            acc_ref[...] += jnp.sum(e * e, axis=0, keepdims=True)
            dy = e * (1.0 / d)
            dg_ref[...] += jnp.sum(dy * hhat, axis=0, keepdims=True)
            dh = _rms_bwd(g * dy, hhat, r)
            dh_ref[...] = dh
            dhb_ref[...] = dh.astype(BF16)

        @pl.when((b == nb_seq - 1) & (j == nb - 1))
        def _():
            total = jnp.sum(acc_ref[...], axis=1, keepdims=True)
            loss_ref[...] = jnp.broadcast_to((0.5 / d) * total, loss_ref.shape)

    tok = pl.BlockSpec((1, LANES, d), lambda b, j: (b, jnp.maximum(j - 1, 0), 0))
    row = pl.BlockSpec((LANES, d), lambda b, j: (b * nb + j, 0))
    vec = pl.BlockSpec((1, d), lambda b, j: (0, 0))
    return pl.pallas_call(
        body,
        name="loss_bwd",
        grid=(nb_seq, nb),
        in_specs=[tok, row, tok, vec],
        out_specs=(row, row, vec, pl.BlockSpec((1, LANES), lambda b, j: (0, 0))),
        out_shape=(
            jax.ShapeDtypeStruct((nb_seq * nb * LANES, d), F32),
            jax.ShapeDtypeStruct((nb_seq * nb * LANES, d), BF16),
            jax.ShapeDtypeStruct((1, d), F32),
            jax.ShapeDtypeStruct((1, LANES), F32),
        ),
        scratch_shapes=[pltpu.VMEM((1, d), F32)],
        compiler_params=_params("arbitrary", "arbitrary"),
    )(x, mix, target, g_final)


def _out_proj_bwd(dhb, w_out_t, ya, yc, bm):
    r, d = dhb.shape
    ka = ya.shape[1]

    def body(dh_ref, wt_ref, a_ref, c_ref, dcat_ref, dw_ref):
        @pl.when(pl.program_id(0) == 0)
        def _():
            dw_ref[...] = jnp.zeros_like(dw_ref)

        dh = dh_ref[...]
        dcat_ref[...] = _dot(dh, wt_ref[...]).astype(BF16)
        dw_ref[0:ka, :] += _dot(a_ref[...], dh, _TN)
        dw_ref[ka:, :] += _dot(c_ref[...], dh, _TN)

    return pl.pallas_call(
        body,
        name="out_proj_bwd",
        grid=(r // bm,),
        in_specs=[pl.BlockSpec((bm, d), lambda i: (i, 0)), pl.BlockSpec(w_out_t.shape, lambda i: (0, 0)),
                  pl.BlockSpec((bm, ka), lambda i: (i, 0)), pl.BlockSpec((bm, yc.shape[1]), lambda i: (i, 0))],
        out_specs=(pl.BlockSpec((bm, w_out_t.shape[1]), lambda i: (i, 0)),
                   pl.BlockSpec((w_out_t.shape[1], d), lambda i: (0, 0))),
        out_shape=(jax.ShapeDtypeStruct((r, w_out_t.shape[1]), BF16),
                   jax.ShapeDtypeStruct((w_out_t.shape[1], d), F32)),
        compiler_params=_params("arbitrary"),
    )(dhb, w_out_t, ya, yc)


def _attn_bwd(q, k, v, o, lse, dcat, p, g_attn):
    nb_seq, _, tp, _ = q.shape

    def body(q_ref, k_ref, v_ref, o_ref, lse_ref, dy_ref, z_ref, g_ref,
             dq_ref, dk_ref, dv_ref, dz_ref, dg_ref, dq_acc):
        @pl.when(pl.program_id(1) == 0)
        def _():
            dg_ref[...] = jnp.zeros_like(dg_ref)

        g = g_ref[...]
        z = z_ref[...].astype(F32)
        o = o_ref[0, 0]
        dy = dy_ref[...].astype(F32)
        sig = _sigmoid(z)
        ohat, r = _rms_stats(o)
        don = dy * (z * sig)
        dz_ref[...] = (dy * (ohat * g) * (sig * (1.0 + z * (1.0 - sig)))).astype(BF16)
        dg_ref[0] += jnp.sum(don * ohat, axis=0, keepdims=True)
        do = _rms_bwd(g * don, ohat, r)
        dvec = jnp.sum(do * o, axis=-1, keepdims=True)
        dob = do.astype(BF16)
        lse_col = lse_ref[0, 0, :, 0:1]
        dq_acc[...] = jnp.zeros_like(dq_acc)
        for k0 in range(0, tp, KV_TILE):
            nk = min(KV_TILE, tp - k0)
            nq = tp - k0
            qq = q_ref[0, 0, k0:, :]
            kk = k_ref[0, 0, k0:k0 + nk, :]
            s = _dot(qq, kk, _NT) * ATTN_SCALE
            qpos = lax.broadcasted_iota(jnp.int32, (nq, nk), 0)
            kpos = lax.broadcasted_iota(jnp.int32, (nq, nk), 1)
            valid = kpos <= qpos
            if k0 == 0:
                valid = valid & (kpos >= PAD_FRONT)
            pr = jnp.where(valid, jnp.exp(s - lse_col[k0:]), 0.0)
            dp = _dot(dob[k0:], v_ref[0, 0, k0:k0 + nk, :], _NT)
            ds = (pr * (dp - dvec[k0:]) * ATTN_SCALE).astype(BF16)
            dv_ref[0, 0, k0:k0 + nk, :] = _dot(pr.astype(BF16), dob[k0:], _TN).astype(BF16)
            dk_ref[0, 0, k0:k0 + nk, :] = _dot(ds, qq, _TN).astype(BF16)
            dq_acc[k0:, :] += _dot(ds, kk)
        dq_ref[0, 0] = dq_acc[...].astype(BF16)

    qk = pl.BlockSpec((1, 1, tp, 2 * LANES), lambda h, b: (b, h, 0, 0))
    hv = pl.BlockSpec((1, 1, tp, D_V), lambda h, b: (b, h, 0, 0))
    col = pl.BlockSpec((tp, LANES), lambda h, b: (b, h))
    return pl.pallas_call(
        body,
        name="attn_bwd",
        grid=(N_HEADS, nb_seq),
        in_specs=[qk, qk, hv, hv, hv, col,
                  pl.BlockSpec((tp, LANES), lambda h, b: (b, GRP_A // LANES + h)),
                  pl.BlockSpec((1, LANES), lambda h, b: (0, h))],
        out_specs=(qk, qk, hv, col, pl.BlockSpec((1, 1, LANES), lambda h, b: (h, 0, 0))),
        out_shape=(
            jax.ShapeDtypeStruct((nb_seq, N_HEADS, tp, 2 * LANES), BF16),
            jax.ShapeDtypeStruct((nb_seq, N_HEADS, tp, 2 * LANES), BF16),
            jax.ShapeDtypeStruct((nb_seq, N_HEADS, tp, D_V), BF16),
            jax.ShapeDtypeStruct((nb_seq * tp, N_HEADS * D_V), BF16),
            jax.ShapeDtypeStruct((N_HEADS, 1, LANES), F32),
        ),
        scratch_shapes=[pltpu.VMEM((tp, 2 * LANES), F32)],
        compiler_params=_params("arbitrary", "arbitrary"),
    )(q, k, v, o, lse, dcat, p, g_attn)


def _qkv_bwd(p, dq, dk, dv, wq_t, wkv_t, gq, gkv, tables):
    nb_seq, _, tp, _ = dq.shape
    ht = tp // 2

    def body(pa_ref, dq_ref, dk_ref, dv_ref, wqt_ref, wkvt_ref, gq_ref, gkv_ref, cos_ref, sa_ref, sb_ref,
             dpa_ref, dwq_ref, dwkv_ref, dgq_ref, dgkv_ref):
        @pl.when(pl.program_id(0) == 0)
        def _():
            dwq_ref[...] = jnp.zeros_like(dwq_ref)
            dwkv_ref[...] = jnp.zeros_like(dwkv_ref)
            dgq_ref[...] = jnp.zeros_like(dgq_ref)
            dgkv_ref[...] = jnp.zeros_like(dgkv_ref)

        pa = pa_ref[...].astype(F32)
        gq, gkv = gq_ref[...], gkv_ref[...]
        cq_hat, rq = _rms_stats(pa[:, :Q_RANK])
        ckv_hat, rkv = _rms_stats(pa[:, Q_RANK:Q_RANK + KV_RANK])
        tabs = (cos_ref[...], sa_ref[...], sb_ref[...])

        pe = [dq_ref[0, h, :, D_NOPE:].astype(F32) for h in range(N_HEADS)]
        pairs = [_rope_t(pe[2 * i] + pltpu.roll(pe[2 * i + 1], D_ROPE, 1), *tabs).astype(BF16) for i in range(2)]
        dq_flat = jnp.concatenate([dq_ref[0, h, :, :D_NOPE] for h in range(N_HEADS)] + pairs, axis=1)
        dwq_ref[...] += _dot((cq_hat * gq).astype(BF16), dq_flat, _TN)
        dcqn = _dot(dq_flat, wqt_ref[...])
        dgq_ref[...] += jnp.sum(dcqn * cq_hat, axis=0, keepdims=True)
        dcq = _rms_bwd(gq * dcqn, cq_hat, rq)

        dkv_flat = jnp.concatenate([dk_ref[0, h, :, :D_NOPE] for h in range(N_HEADS)]
                                   + [dv_ref[0, h] for h in range(N_HEADS)], axis=1)
        dwkv_ref[...] += _dot((ckv_hat * gkv).astype(BF16), dkv_flat, _TN)
        dckvn = _dot(dkv_flat, wkvt_ref[...])
        dgkv_ref[...] += jnp.sum(dckvn * ckv_hat, axis=0, keepdims=True)
        dckv = _rms_bwd(gkv * dckvn, ckv_hat, rkv)

        dk_pe = dk_ref[0, 0, :, D_NOPE:].astype(F32)
        for h in range(1, N_HEADS):
            dk_pe = dk_pe + dk_ref[0, h, :, D_NOPE:].astype(F32)
        dpa_ref[...] = jnp.concatenate([dcq, dckv, _rope_t(dk_pe, *tabs)], axis=1).astype(BF16)

    full = lambda a: pl.BlockSpec(a.shape, lambda i: (0,) * a.ndim)
    tab = pl.BlockSpec((ht, LANES), lambda i: (i % 2, 0))
    qk = pl.BlockSpec((1, N_HEADS, ht, 2 * LANES), lambda i: (i // 2, 0, i % 2, 0))
    acc = lambda shape: pl.BlockSpec(shape, lambda i: (0, 0))
    return pl.pallas_call(
        body,
        name="qkv_bwd",
        grid=(2 * nb_seq,),
        in_specs=[pl.BlockSpec((ht, GRP_A), lambda i: (i, 0)), qk, qk,
                  pl.BlockSpec((1, N_HEADS, ht, D_V), lambda i: (i // 2, 0, i % 2, 0)),
                  full(wq_t), full(wkv_t), full(gq), full(gkv), tab, tab, tab],
        out_specs=(pl.BlockSpec((ht, GRP_A), lambda i: (i, 0)),
                   acc((Q_RANK, wq_t.shape[0])), acc((KV_RANK, wkv_t.shape[0])), acc((1, Q_RANK)), acc((1, KV_RANK))),
        out_shape=(
            jax.ShapeDtypeStruct((nb_seq * tp, GRP_A), BF16),
            jax.ShapeDtypeStruct((Q_RANK, wq_t.shape[0]), F32),
            jax.ShapeDtypeStruct((KV_RANK, wkv_t.shape[0]), F32),
            jax.ShapeDtypeStruct((1, Q_RANK), F32),
            jax.ShapeDtypeStruct((1, KV_RANK), F32),
        ),
        compiler_params=_params("arbitrary"),
    )(p, dq, dk, dv, wq_t, wkv_t, gq, gkv, *tables)


def _conv_bwd(p, dcat, conv_w, g_conv, nb_seq, tp):
    cols = CONV_WIDTH // LANES

    def body(b_ref, c_ref, h_ref, z_ref, dy_ref, w_ref, g_ref,
             db_ref, dc_ref, dh_ref, dz_ref, dw_ref, dg_ref):
        @pl.when(pl.program_id(1) == 0)
        def _():
            dw_ref[...] = jnp.zeros_like(dw_ref)
            dg_ref[...] = jnp.zeros_like(dg_ref)

        cb, c, h = b_ref[...].astype(F32), c_ref[...].astype(F32), h_ref[...].astype(F32)
        z, dy = z_ref[...].astype(F32), dy_ref[...].astype(F32)
        g = g_ref[...]
        w0, w1, w2 = w_ref[0:1, :], w_ref[1:2, :], w_ref[2:3, :]
        cc = c * h
        row = lax.broadcasted_iota(jnp.int32, (tp, LANES), 0)
        s1 = jnp.where(row >= 1, pltpu.roll(cc, 1, 0), 0.0)
        s2 = jnp.where(row >= 2, pltpu.roll(cc, 2, 0), 0.0)
        dwc = w0 * s2 + w1 * s1 + w2 * cc
        yc = cb * dwc
        r = lax.rsqrt(_group_mean(yc * yc) + EPS)
        ychat = yc * r
        sig = _sigmoid(z)
        dz_ref[...] = (dy * (ychat * g) * (sig * (1.0 + z * (1.0 - sig)))).astype(BF16)
        dyn = dy * (z * sig)
        dg_ref[...] += jnp.sum(dyn * ychat, axis=0, keepdims=True)
        gd = g * dyn
        dyc = r * (gd - ychat * _group_mean(gd * ychat))
        db_ref[...] = (dyc * dwc).astype(BF16)
        ddw = dyc * cb
        dw_ref[0:1, :] += jnp.sum(ddw * s2, axis=0, keepdims=True)
        dw_ref[1:2, :] += jnp.sum(ddw * s1, axis=0, keepdims=True)
        dw_ref[2:3, :] += jnp.sum(ddw * cc, axis=0, keepdims=True)
        u1 = jnp.where(row <= tp - 2, pltpu.roll(ddw, tp - 1, 0), 0.0)
        u2 = jnp.where(row <= tp - 3, pltpu.roll(ddw, tp - 2, 0), 0.0)
        dcc = w2 * ddw + w1 * u1 + w0 * u2
        dc_ref[...] = (dcc * h).astype(BF16)
        dh_ref[...] = (dcc * c).astype(BF16)

    col = pl.BlockSpec((tp, LANES), lambda t, b: (b, t))
    out = jax.ShapeDtypeStruct((nb_seq * tp, CONV_WIDTH), BF16)
    return pl.pallas_call(
        body,
        name="conv_bwd",
        grid=(cols, nb_seq),
        in_specs=_conv_specs(tp, lambda t, b, off: (b, off + t)) + [
            pl.BlockSpec((tp, LANES), lambda t, b: (b, N_HEADS * D_V // LANES + t)),
            pl.BlockSpec((3, LANES), lambda t, b: (0, t)),
            pl.BlockSpec((1, LANES), lambda t, b: (0, t))],
        out_specs=(col, col, col, col,
                   pl.BlockSpec((3, LANES), lambda t, b: (0, t)), pl.BlockSpec((1, LANES), lambda t, b: (0, t))),
        out_shape=(out, out, out, out,
                   jax.ShapeDtypeStruct((3, CONV_WIDTH), F32), jax.ShapeDtypeStruct((1, CONV_WIDTH), F32)),
        compiler_params=_params("arbitrary", "arbitrary"),
    )(p, p, p, p, dcat, conv_w, g_conv)


def _in_proj_bwd_x(dps, w_in_t, bm):
    r, kb = dps[0].shape
    d = w_in_t.shape[1]

    def body(*refs):
        dp_refs, wt_ref, o_ref = refs[:len(dps)], refs[len(dps)], refs[len(dps) + 1]
        acc = _dot(dp_refs[0][...], wt_ref[0:kb, :])
        for j in range(1, len(dps)):
            acc = acc + _dot(dp_refs[j][...], wt_ref[kb * j:kb * (j + 1), :])
        o_ref[...] = acc

    return pl.pallas_call(
        body,
        name="in_proj_bwd_x",
        grid=(r // bm,),
        in_specs=[pl.BlockSpec((bm, kb), lambda i: (i, 0)) for _ in dps] + [pl.BlockSpec(w_in_t.shape, lambda i: (0, 0))],
        out_specs=pl.BlockSpec((bm, d), lambda i: (i, 0)),
        out_shape=jax.ShapeDtypeStruct((r, d), F32),
        compiler_params=_params("parallel"),
    )(*dps, w_in_t)


def _in_proj_bwd_w(u, dps, bm, name):
    r, d = u.shape
    kb = dps[0].shape[1]

    def body(*refs):
        u_ref, dp_refs, o_ref = refs[0], refs[1:1 + len(dps)], refs[1 + len(dps)]

        @pl.when(pl.program_id(0) == 0)
        def _():
            o_ref[...] = jnp.zeros_like(o_ref)

        uu = u_ref[...]
        for j in range(len(dps)):
            o_ref[:, kb * j:kb * (j + 1)] += _dot(uu, dp_refs[j][...], _TN)

    return pl.pallas_call(
        body,
        name=name,
        grid=(r // bm,),
        in_specs=[pl.BlockSpec((bm, d), lambda i: (i, 0))] + [pl.BlockSpec((bm, kb), lambda i: (i, 0)) for _ in dps],
        out_specs=pl.BlockSpec((d, kb * len(dps)), lambda i: (0, 0)),
        out_shape=jax.ShapeDtypeStruct((d, kb * len(dps)), F32),
        compiler_params=_params("arbitrary"),
    )(u, *dps)


def _input_bwd(x, meta, du, dh, norm_g):
    nb_seq, s, d = x.shape
    nb = s // LANES + 1

    def body(x_ref, meta_ref, du_ref, dh_ref, g_ref, gx_ref, dmeta_ref, dg_ref):
        b, j = pl.program_id(0), pl.program_id(1)
        g = g_ref[...]

        @pl.when((b == 0) & (j == 0))
        def _():
            dmeta_ref[...] = jnp.zeros_like(dmeta_ref)
            dg_ref[...] = jnp.zeros_like(dg_ref)

        def bwd(h0, du_, dh_):
            hhat, r = _rms_stats(h0)
            dg_ref[...] += jnp.sum(du_ * hhat, axis=0, keepdims=True)
            return _rms_bwd(g * du_, hhat, r) + dh_

        @pl.when(j == 0)
        def _():
            dmeta_ref[...] += bwd(meta_ref[...], du_ref[PAD_FRONT:LANES, :], dh_ref[PAD_FRONT:LANES, :])

        @pl.when(j > 0)
        def _():
            gx_ref[0] = bwd(x_ref[0], du_ref[...], dh_ref[...])

    tok = pl.BlockSpec((1, LANES, d), lambda b, j: (b, jnp.maximum(j - 1, 0), 0))
    row = pl.BlockSpec((LANES, d), lambda b, j: (b * nb + j, 0))
    vec = pl.BlockSpec((1, d), lambda b, j: (0, 0))
    return pl.pallas_call(
        body,
        name="input_bwd",
        grid=(nb_seq, nb),
        in_specs=[tok, pl.BlockSpec((N_META, d), lambda b, j: (0, 0)), row, row, vec],
        out_specs=(tok, pl.BlockSpec((N_META, d), lambda b, j: (0, 0)), vec),
        out_shape=(
            jax.ShapeDtypeStruct((nb_seq, s, d), F32),
            jax.ShapeDtypeStruct((N_META, d), F32),
            jax.ShapeDtypeStruct((1, d), F32),
        ),
        compiler_params=_params("arbitrary", "arbitrary"),
    )(x, meta, du, dh, norm_g)


def _heads_first(w, per_head, first):
    a = [w[:, per_head * h:per_head * h + first] for h in range(N_HEADS)]
    b = [w[:, per_head * h + first:per_head * (h + 1)] for h in range(N_HEADS)]
    return jnp.concatenate(a + b, axis=1)


def _heads_interleaved(w, first, rest):
    base = N_HEADS * first
    parts = []
    for h in range(N_HEADS):
        parts += [w[:, first * h:first * (h + 1)], w[:, base + rest * h:base + rest * (h + 1)]]
    return jnp.concatenate(parts, axis=1)


def _col_shards(w):
    r, c = w.shape
    return w.reshape(r, N_DEV, c // N_DEV).transpose(1, 0, 2).reshape(N_DEV, -1)


def _from_col_shards(flat, r):
    return flat.reshape(N_DEV, r, -1).transpose(1, 0, 2).reshape(r, -1)


def _pad_small(parts):
    flat = jnp.concatenate([a.reshape(-1) for a in parts])
    return jnp.pad(flat, (0, SMALL_ROWS * D_MODEL - flat.shape[0]))


def _pack_shard(w_in, w_q, w_kv, w_out, conv_w, meta, repl):
    big = jnp.concatenate([a.reshape(-1) for a in (w_in, w_q, w_kv, w_out)]).reshape(BIG_ROWS, D_MODEL)
    small = _pad_small([conv_w, meta] + list(repl) + [jnp.zeros((1,), F32)]).reshape(SMALL_ROWS, D_MODEL)
    return jnp.concatenate([big, small], axis=0)


def _unpack_shard(blob):
    big = blob[:BIG_ROWS].reshape(-1)
    small = blob[BIG_ROWS:].reshape(-1)
    out, o = [], 0
    for shape in ((1, D_MODEL, 376), (1, Q_RANK, 96), (1, KV_RANK, 128), (1, 128, D_MODEL)):
        n = shape[1] * shape[2]
        out.append(big[o:o + n].reshape(shape))
        o += n
    o = 0
    for shape in ((1, 3, 64), (N_META, 128), (1, 1024), (1, 256), (1, 128), (1, 512), (1, 512), (1024,), ()):
        n = 1
        for s_ in shape:
            n *= s_
        out.append(small[o:o + n].reshape(shape))
        o += n
    return out


def kernel(x, meta_tokens, norm_g, w_in, q_norm_g, w_q_up, kv_norm_g, w_kv_up, conv_w, attn_out_g, conv_out_g, w_out, final_norm_g, loss_target, m_meta_tokens, m_norm_g, m_w_in, m_q_norm_g, m_w_q_up, m_kv_norm_g, m_w_kv_up, m_conv_w, m_attn_out_g, m_conv_out_g, m_w_out, m_final_norm_g, v_meta_tokens, v_norm_g, v_w_in, v_q_norm_g, v_w_q_up, v_kv_norm_g, v_w_kv_up, v_conv_w, v_attn_out_g, v_conv_out_g, v_w_out, v_final_norm_g):
    nb_seq, s, d = x.shape
    tp = s + LANES
    ht = tp // 2

    big = jnp.concatenate([a.reshape(-1) for a in (w_in[0], w_q_up[0], w_kv_up[0], w_out[0])]).astype(BF16)
    small = lax.bitcast_convert_type(_pad_small([conv_w[0], meta_tokens]), BF16).reshape(-1)
    gathered = _all_gather(jnp.concatenate([big, small]).reshape(GATHER_ROWS, D_MODEL))
    gathered = gathered.reshape(N_DEV, GATHER_ROWS, D_MODEL)
    gbig = gathered[:, :BIG_ROWS].reshape(N_DEV, -1)
    gsmall = lax.bitcast_convert_type(gathered[:, BIG_ROWS:].reshape(N_DEV, -1, 2), F32)
    o1, o2, o3 = D_MODEL * 376, D_MODEL * 376 + Q_RANK * 96, D_MODEL * 376 + Q_RANK * 96 + KV_RANK * 128
    w_in_f = _from_col_shards(gbig[:, :o1], D_MODEL)
    w_q_f = _from_col_shards(gbig[:, o1:o2], Q_RANK)
    w_kv_f = _from_col_shards(gbig[:, o2:o3], KV_RANK)
    w_out_f = gbig[:, o3:].reshape(D_MODEL, D_MODEL)
    conv_w_f = _from_col_shards(gsmall[:, :192], 3)
    meta_f = _from_col_shards(gsmall[:, 192:192 + N_META * 128], N_META)

    (loss_part, grad_x, d_meta, d_norm_g, d_w_in, d_gq, d_w_q, d_gkv, d_w_kv, d_conv_w, d_attn_g, d_conv_g, d_w_out,
     d_final_g) = _local_step(x, loss_target, meta_f, norm_g, w_in_f, q_norm_g, w_q_f, kv_norm_g, w_kv_f, conv_w_f,
                              attn_out_g, conv_out_g, w_out_f, final_norm_g)

    send_big = jnp.concatenate(
        [_col_shards(d_w_in), _col_shards(d_w_q), _col_shards(d_w_kv), d_w_out.reshape(N_DEV, -1)],
        axis=1).astype(BF16).reshape(N_DEV, BIG_ROWS, D_MODEL)
    repl = jnp.concatenate([a.reshape(-1) for a in (
        d_norm_g, d_gq, d_gkv, d_attn_g, d_conv_g, d_final_g, loss_part[0, :1])])
    send_small = jnp.concatenate(
        [_col_shards(d_conv_w), _col_shards(d_meta), jnp.broadcast_to(repl, (N_DEV, repl.shape[0]))], axis=1)
    send_small = jnp.pad(send_small, ((0, 0), (0, SMALL_ROWS * D_MODEL - send_small.shape[1])))
    send_small = send_small.reshape(N_DEV, SMALL_ROWS, D_MODEL)

    gains = lambda pre: [pre[n] for n in ("norm_g", "q_norm_g", "kv_norm_g", "attn_out_g", "conv_out_g", "final_norm_g")]
    w_all = dict(norm_g=norm_g, q_norm_g=q_norm_g, kv_norm_g=kv_norm_g, attn_out_g=attn_out_g,
                 conv_out_g=conv_out_g, final_norm_g=final_norm_g)
    m_all = dict(norm_g=m_norm_g, q_norm_g=m_q_norm_g, kv_norm_g=m_kv_norm_g, attn_out_g=m_attn_out_g,
                 conv_out_g=m_conv_out_g, final_norm_g=m_final_norm_g)
    v_all = dict(norm_g=v_norm_g, q_norm_g=v_q_norm_g, kv_norm_g=v_kv_norm_g, attn_out_g=v_attn_out_g,
                 conv_out_g=v_conv_out_g, final_norm_g=v_final_norm_g)
    w_blob = _pack_shard(w_in[0], w_q_up[0], w_kv_up[0], w_out[0], conv_w[0], meta_tokens, gains(w_all))
    m_blob = _pack_shard(m_w_in[0], m_w_q_up[0], m_w_kv_up[0], m_w_out[0], m_conv_w[0], m_meta_tokens, gains(m_all))
    v_blob = _pack_shard(v_w_in[0], v_w_q_up[0], v_w_kv_up[0], v_w_out[0], v_conv_w[0], v_meta_tokens, gains(v_all))
    g_blob, d_blob, nm_blob, nv_blob = _reduce_adam(send_big, send_small, w_blob, m_blob, v_blob)

    def leaves(blob):
        w_in_, w_q_, w_kv_, w_out_, conv_, meta_, ng, qg, kvg, ag, cg, fg, last = _unpack_shard(blob)
        return [meta_, ng, w_in_, qg, w_q_, kvg, w_kv_, conv_, ag, cg, w_out_, fg], last

    grads, loss = leaves(g_blob)
    deltas, _ = leaves(d_blob)
    new_m, _ = leaves(nm_blob)
    new_v, _ = leaves(nv_blob)
    return (loss, grad_x, *grads, *deltas, *new_m, *new_v)


def _local_step(x, loss_target, meta_f, norm_g, w_in_f, q_norm_g, w_q_f, kv_norm_g, w_kv_f, conv_w_f,
                attn_out_g, conv_out_g, w_out_f, final_norm_g):
    nb_seq, s, d = x.shape
    tp = s + LANES
    ht = tp // 2
    n_a = Q_RANK + KV_RANK + D_ROPE
    w_in_p = jnp.concatenate([w_in_f[:, :n_a], jnp.zeros((D_MODEL, GRP_A - n_a), BF16), w_in_f[:, n_a:]], axis=1)
    w_q_p = _heads_first(w_q_f, D_NOPE + D_ROPE, D_NOPE)
    w_kv_p = _heads_first(w_kv_f, D_NOPE + D_V, D_NOPE)
    tables = _rope_tables(tp)

    u = _prep(x, meta_f, norm_g)
    p = _matmul(u, w_in_p, ht, GRP_A, BF16, "in_proj")
    q, k, v = _qkv_fwd(p, w_q_p, w_kv_p, q_norm_g, kv_norm_g, tables, nb_seq, tp)
    ya, o, lse = _attn_fwd(q, k, v, p, attn_out_g)
    yc = _conv_fwd(p, conv_w_f, conv_out_g, nb_seq, tp)
    mix = _out_proj(ya, yc, w_out_f, ht)
    g_final = final_norm_g.reshape(1, d)
    dh, dhb, d_final_g, loss_part = _loss_bwd(x, mix, loss_target, g_final)

    dcat, d_w_out = _out_proj_bwd(dhb, w_out_f.T, ya, yc, ht)
    dq, dk, dv, dz_attn, d_attn_g = _attn_bwd(q, k, v, o, lse, dcat, p, attn_out_g)
    dpa, d_wq_p, d_wkv_p, d_gq, d_gkv = _qkv_bwd(p, dq, dk, dv, w_q_p.T, w_kv_p.T, q_norm_g, kv_norm_g, tables)
    d_b, d_c, d_h, dz_conv, d_conv_w, d_conv_g = _conv_bwd(p, dcat, conv_w_f, conv_out_g, nb_seq, tp)
    dps = (dpa, dz_attn, d_b, d_c, d_h, dz_conv)
    du = _in_proj_bwd_x(dps, w_in_p.T, ht)
    d_w_in_p = jnp.concatenate([_in_proj_bwd_w(u, dps[:3], ht, "in_proj_bwd_w0"),
                                _in_proj_bwd_w(u, dps[3:], ht, "in_proj_bwd_w1")], axis=1)
    grad_x, d_meta, d_norm_g = _input_bwd(x, meta_f, du, dh, norm_g)

    d_w_in = jnp.concatenate([d_w_in_p[:, :n_a], d_w_in_p[:, GRP_A:]], axis=1)
    d_w_q = _heads_interleaved(d_wq_p, D_NOPE, D_ROPE)
    d_w_kv = _heads_interleaved(d_wkv_p, D_NOPE, D_V)
    return (loss_part, grad_x, d_meta, d_norm_g, d_w_in, d_gq, d_w_q, d_gkv, d_w_kv, d_conv_w,
            d_attn_g.reshape(1, -1), d_conv_g, d_w_out, d_final_g)
```

```python
import functools

import jax
import jax.numpy as jnp
from jax import lax
from jax.experimental import pallas as pl
from jax.experimental.pallas import tpu as pltpu

F32 = jnp.float32
BF16 = jnp.bfloat16

N_META = 16
D_MODEL = 1024
N_HEADS = 4
D_NOPE = 128
D_ROPE = 64
D_V = 128
Q_RANK = 256
KV_RANK = 128
CONV_WIDTH = 512
CONV_GROUP = 64
ROPE_THETA = 10000.0
ATTN_SCALE = (D_NOPE + D_ROPE) ** -0.5
EPS = 1e-6
NEG_INF = -1e30

ADAM_LR = 0.001
ADAM_B1 = 0.9
ADAM_B2 = 0.999
ADAM_EPS = 1e-08
ADAM_WD = 0.01
ADAM_STEP = 10

LANES = 128
PAD_FRONT = LANES - N_META
KV_TILE = 256
N_DEV = 8
VMEM_LIMIT = 56 * 1024 * 1024

IN_PAD = 3072
GRP_A = 512
N_A = Q_RANK + KV_RANK + D_ROPE
IN_PROJ = 3008
SHARD_IN = IN_PROJ // N_DEV
SHARD_IN_PAD = 384
SHARD_Q = 96
SHARD_KV = 128
SHARD_OUT = 128
SHARD_CONV = 64
SHARD_META = 128
Q_COLS = N_HEADS * (D_NOPE + D_ROPE)
KV_COLS = N_HEADS * (D_NOPE + D_V)
GATHER_COLS = 512
GATHER_SMALL_ROWS = 24

ROW_Q, ROW_KV, ROW_META, ROW_CONV = 0, 256, 384, 400
ROW_REPL = 408
ROW_NORM, ROW_FINAL, ROW_GQ, ROW_GKV, ROW_ATTN, ROW_CONVG, ROW_LOSS = 408, 416, 424, 426, 427, 431, 435
SMALL_ROWS = 440
ADAM_CHUNK = 64

PARAM_SHAPES = (
    ("meta_tokens", (N_META, SHARD_META)), ("norm_g", (1, D_MODEL)), ("w_in", (D_MODEL, SHARD_IN)),
    ("q_norm_g", (1, Q_RANK)), ("w_q_up", (Q_RANK, SHARD_Q)), ("kv_norm_g", (1, KV_RANK)),
    ("w_kv_up", (KV_RANK, SHARD_KV)), ("conv_w", (3, SHARD_CONV)), ("attn_out_g", (1, CONV_WIDTH)),
    ("conv_out_g", (1, CONV_WIDTH)), ("w_out", (SHARD_OUT, D_MODEL)), ("final_norm_g", (1, D_MODEL)),
)


def _in_pieces(k):
    lo, hi = SHARD_IN * k, SHARD_IN * (k + 1)
    out = []
    if lo < N_A:
        out.append((0, min(hi, N_A) - lo, lo))
    if hi > N_A:
        s = max(lo, N_A)
        out.append((s - lo, hi - lo, s + GRP_A - N_A))
    return out


def _q_pieces(k):
    lo, hi = SHARD_Q * k, SHARD_Q * (k + 1)
    out = []
    for h in range(N_HEADS):
        base = (D_NOPE + D_ROPE) * h
        s, e = max(lo, base), min(hi, base + D_NOPE)
        if s < e:
            out.append((s - lo, e - lo, D_NOPE * h + s - base))
        s, e = max(lo, base + D_NOPE), min(hi, base + D_NOPE + D_ROPE)
        if s < e:
            out.append((s - lo, e - lo, N_HEADS * D_NOPE + D_ROPE * h + s - base - D_NOPE))
    return out


def _kv_dst(k):
    return D_NOPE * (k // 2) + (N_HEADS * D_NOPE if k % 2 else 0)


def _params(*sem):
    return pltpu.CompilerParams(dimension_semantics=sem, vmem_limit_bytes=VMEM_LIMIT)


def _rms_stats(x):
    r = lax.rsqrt(jnp.mean(x * x, axis=-1, keepdims=True) + EPS)
    return x * r, r


def _rms_bwd(gdy, xhat, r):
    return r * (gdy - xhat * jnp.mean(gdy * xhat, axis=-1, keepdims=True))


def _sigmoid(z):
    return 1.0 / (1.0 + jnp.exp(-z))


def _group_mean(x):
    i0 = lax.broadcasted_iota(jnp.int32, (LANES, LANES), 0) // CONV_GROUP
    i1 = lax.broadcasted_iota(jnp.int32, (LANES, LANES), 1) // CONV_GROUP
    m = jnp.where(i0 == i1, 1.0 / CONV_GROUP, 0.0).astype(BF16)
    hi = x.astype(BF16)
    lo = (x - hi.astype(F32)).astype(BF16)
    return jnp.dot(hi, m, preferred_element_type=F32) + jnp.dot(lo, m, preferred_element_type=F32)


_NT = (((1,), (1,)), ((), ()))
_TN = (((0,), (0,)), ((), ()))


def _dot(a, b, dims=None):
    if dims is None:
        return jnp.dot(a, b, preferred_element_type=F32)
    return lax.dot_general(a, b, dims, preferred_element_type=F32)


def _device_position():
    x, y, c = lax.axis_index("x"), lax.axis_index("y"), lax.axis_index("c")
    return x, y, c, 4 * x + 2 * y + c


def _gather_weights(w_in, w_q, w_kv, w_out, conv_w, meta):
    n_arrays = 3

    def body(win_ref, wq_ref, wkv_ref, wout_ref, conv_ref, meta_ref,
             w_in_p, w_q_p, w_kv_p, w_out_f, conv_f, meta_f,
             sbig, sout, ssmall, gbig, gsmall, send_sems, recv_sems, local_sems):
        x, y, c, me_idx = _device_position()
        me, sibling = (x, y, c), (x, y, 1 - c)
        chips = [(1 - x, y), (x, 1 - y), (1 - x, 1 - y)]

        sbig[...] = jnp.zeros_like(sbig)
        sbig[:, 0:SHARD_IN] = win_ref[...].astype(BF16)
        sbig[0:Q_RANK, SHARD_IN_PAD:SHARD_IN_PAD + SHARD_Q] = wq_ref[...].astype(BF16)
        sbig[Q_RANK:Q_RANK + KV_RANK, SHARD_IN_PAD:] = wkv_ref[...].astype(BF16)
        sout[...] = wout_ref[...].astype(BF16)
        ssmall[...] = jnp.zeros_like(ssmall)
        ssmall[0:N_META, :] = meta_ref[...]
        ssmall[N_META:N_META + 3, 0:SHARD_CONV] = conv_ref[...]
        srcs = (sbig, sout, ssmall)

        def slot(a, px, py, pc):
            idx = 4 * px + 2 * py + pc
            if a == 1:
                return w_out_f.at[pl.ds(pl.multiple_of(idx * SHARD_OUT, SHARD_OUT), SHARD_OUT), :]
            return (gbig, None, gsmall)[a].at[idx]

        def copy(a, k, block, to, own=False):
            return pltpu.make_async_remote_copy(
                src_ref=srcs[a] if own else slot(a, *block),
                dst_ref=slot(a, *block),
                send_sem=send_sems.at[7 * a + k],
                recv_sem=recv_sems.at[7 * a + k],
                device_id=to,
                device_id_type=pl.DeviceIdType.MESH,
            )

        mine = [pltpu.make_async_copy(srcs[a], slot(a, *me), local_sems.at[a]) for a in range(n_arrays)]
        for cp in mine:
            cp.start()
        first = []
        for a in range(n_arrays):
            first.append(copy(a, 0, me, sibling, own=True))
            first += [copy(a, 1 + j, me, (*chip, c), own=True) for j, chip in enumerate(chips)]
        for cp in first:
            cp.start()
        passed = []
        for j, chip in enumerate(chips):
            for a in range(n_arrays):
                copy(a, 1 + j, (*chip, c), me).wait_recv()
                fwd = copy(a, 4 + j, (*chip, c), sibling)
                fwd.start()
                passed.append(fwd)
        for a in range(n_arrays):
            copy(a, 0, sibling, me).wait_recv()
            for j, chip in enumerate(chips):
                copy(a, 4 + j, (*chip, 1 - c), me).wait_recv()
        for cp in first + passed:
            cp.wait_send()
        for cp in mine:
            cp.wait()

        w_in_p[:, N_A:GRP_A] = jnp.zeros((D_MODEL, GRP_A - N_A), BF16)
        conv_f[...] = jnp.zeros_like(conv_f)
        for k in range(N_DEV):
            for s, e, d in _in_pieces(k):
                w_in_p[:, d:d + e - s] = gbig[k, :, s:e]
            for s, e, d in _q_pieces(k):
                w_q_p[:, d:d + e - s] = gbig[k, 0:Q_RANK, SHARD_IN_PAD + s:SHARD_IN_PAD + e]
            w_kv_p[:, _kv_dst(k):_kv_dst(k) + SHARD_KV] = gbig[k, Q_RANK:Q_RANK + KV_RANK, SHARD_IN_PAD:]
            meta_f[:, SHARD_META * k:SHARD_META * (k + 1)] = gsmall[k, 0:N_META, :]
            conv_f[0:3, SHARD_CONV * k:SHARD_CONV * (k + 1)] = gsmall[k, N_META:N_META + 3, 0:SHARD_CONV]

    vm = pl.BlockSpec(memory_space=pltpu.VMEM)
    return pl.pallas_call(
        body,
        name="gather_weights",
        out_shape=(
            jax.ShapeDtypeStruct((D_MODEL, IN_PAD), BF16),
            jax.ShapeDtypeStruct((Q_RANK, Q_COLS), BF16),
            jax.ShapeDtypeStruct((KV_RANK, KV_COLS), BF16),
            jax.ShapeDtypeStruct((D_MODEL, D_MODEL), BF16),
            jax.ShapeDtypeStruct((8, CONV_WIDTH), F32),
            jax.ShapeDtypeStruct((N_META, D_MODEL), F32),
        ),
        in_specs=[vm] * 6,
        out_specs=(vm,) * 6,
        scratch_shapes=[
            pltpu.VMEM((D_MODEL, GATHER_COLS), BF16),
            pltpu.VMEM((SHARD_OUT, D_MODEL), BF16),
            pltpu.VMEM((GATHER_SMALL_ROWS, LANES), F32),
            pltpu.VMEM((N_DEV, D_MODEL, GATHER_COLS), BF16),
            pltpu.VMEM((N_DEV, GATHER_SMALL_ROWS, LANES), F32),
            pltpu.SemaphoreType.DMA((7 * n_arrays,)),
            pltpu.SemaphoreType.DMA((7 * n_arrays,)),
            pltpu.SemaphoreType.DMA((n_arrays,)),
        ],
        compiler_params=pltpu.CompilerParams(vmem_limit_bytes=VMEM_LIMIT),
    )(w_in, w_q, w_kv, w_out, conv_w, meta)


def _adam_update(g, w, m, v):
    m_new = ADAM_B1 * m + (1.0 - ADAM_B1) * g
    v_new = ADAM_B2 * v + (1.0 - ADAM_B2) * (g * g)
    m_hat = m_new / (1.0 - ADAM_B1 ** ADAM_STEP)
    v_hat = v_new / (1.0 - ADAM_B2 ** ADAM_STEP)
    return -ADAM_LR * (m_hat / (jnp.sqrt(v_hat) + ADAM_EPS) + ADAM_WD * w), m_new, v_new


def _reduce_adam(send_in, send_out, small_grads, params):
    n_small = len(small_grads)
    n_p = len(PARAM_SHAPES)
    names = [n for n, _ in PARAM_SHAPES]

    def body(*refs):
        sin_hbm, sout_hbm = refs[0], refs[1]
        (dwq, dwkv, dconv, dmeta, dnorm, dfinal, dgq, dgkv, dattn, dconvg, loss_part) = refs[2:2 + n_small]
        o = 2 + n_small
        p_in = {n: refs[o + 3 * i:o + 3 * i + 3] for i, n in enumerate(names)}
        o += 3 * n_p
        p_out = {n: refs[o + 4 * i:o + 4 * i + 4] for i, n in enumerate(names)}
        o += 4 * n_p
        loss_out = refs[o]
        ssmall, rin, rout, rsmall, gsum, send_sems, recv_sems, local_sems = refs[o + 1:]
        x, y, c, me = _device_position()

        ssmall[...] = jnp.zeros_like(ssmall)
        rep = ssmall.at[0]
        for i in range(D_MODEL // LANES):
            rep[ROW_NORM + i:ROW_NORM + i + 1, :] = dnorm[:, LANES * i:LANES * (i + 1)]
            rep[ROW_FINAL + i:ROW_FINAL + i + 1, :] = dfinal[:, LANES * i:LANES * (i + 1)]
        for i in range(Q_RANK // LANES):
            rep[ROW_GQ + i:ROW_GQ + i + 1, :] = dgq[:, LANES * i:LANES * (i + 1)]
        rep[ROW_GKV:ROW_GKV + 1, :] = dgkv[...]
        for i in range(CONV_WIDTH // LANES):
            rep[ROW_ATTN + i:ROW_ATTN + i + 1, :] = dattn[:, LANES * i:LANES * (i + 1)]
            rep[ROW_CONVG + i:ROW_CONVG + i + 1, :] = dconvg[:, LANES * i:LANES * (i + 1)]
        rep[ROW_LOSS:ROW_LOSS + 1, :] = loss_part[...]
        for k in range(N_DEV):
            if k:
                ssmall[k, ROW_REPL:, :] = ssmall[0, ROW_REPL:, :]
            for s, e, d in _q_pieces(k):
                ssmall[k, ROW_Q:ROW_Q + Q_RANK, s:e] = dwq[:, d:d + e - s]
            ssmall[k, ROW_KV:ROW_KV + KV_RANK, :] = dwkv[:, _kv_dst(k):_kv_dst(k) + SHARD_KV]
            ssmall[k, ROW_META:ROW_META + N_META, :] = dmeta[:, SHARD_META * k:SHARD_META * (k + 1)]
            ssmall[k, ROW_CONV:ROW_CONV + 3, 0:SHARD_CONV] = dconv[0:3, SHARD_CONV * k:SHARD_CONV * (k + 1)]

        copies = []
        for r in range(1, N_DEV):
            peer = (1 - x if r & 4 else x, 1 - y if r & 2 else y, 1 - c if r & 1 else c)
            peer_idx = 4 * peer[0] + 2 * peer[1] + peer[2]
            for j, (src, dst) in enumerate(((sin_hbm, rin), (sout_hbm, rout), (ssmall, rsmall))):
                copies.append(pltpu.make_async_remote_copy(
                    src_ref=src.at[peer_idx],
                    dst_ref=dst.at[r],
                    send_sem=send_sems.at[3 * (r - 1) + j],
                    recv_sem=recv_sems.at[3 * (r - 1) + j],
                    device_id=peer,
                    device_id_type=pl.DeviceIdType.MESH,
                ))
        for cp in copies:
            cp.start()
        own = [pltpu.make_async_copy(sin_hbm.at[me], rin.at[0], local_sems.at[0]),
               pltpu.make_async_copy(sout_hbm.at[me], rout.at[0], local_sems.at[1])]
        for cp in own:
            cp.start()
        rsmall[0] = ssmall[me]
        for cp in own:
            cp.wait()
        for cp in copies:
            cp.wait_recv()

        def apply(name, g, idx):
            w, m, v = (r_[idx] for r_ in p_in[name])
            delta, m_new, v_new = _adam_update(g, w, m, v)
            for r_, val in zip(p_out[name], (g, delta, m_new, v_new)):
                r_[idx] = val

        def chunk(i, carry):
            sl = pl.ds(pl.multiple_of(i * ADAM_CHUNK, ADAM_CHUNK), ADAM_CHUNK)
            g = rin[0, sl, :].astype(F32)
            for r in range(1, N_DEV):
                g = g + rin[r, sl, :].astype(F32)
            apply("w_in", g[:, :SHARD_IN], (sl, slice(None)))
            return carry

        lax.fori_loop(0, D_MODEL // ADAM_CHUNK, chunk, 0)

        g = rout[0].astype(F32)
        for r in range(1, N_DEV):
            g = g + rout[r].astype(F32)
        apply("w_out", g, (slice(None), slice(None)))

        gs = rsmall[me]
        for d in range(1, N_DEV):
            gs = gs + rsmall[d ^ me]
        gsum[...] = gs
        full = (slice(None), slice(None))
        apply("w_q_up", gsum[ROW_Q:ROW_Q + Q_RANK, 0:SHARD_Q], full)
        apply("w_kv_up", gsum[ROW_KV:ROW_KV + KV_RANK, :], full)
        apply("meta_tokens", gsum[ROW_META:ROW_META + N_META, :], full)
        apply("conv_w", gsum[ROW_CONV:ROW_CONV + 3, 0:SHARD_CONV], full)
        for name, row, width in (("norm_g", ROW_NORM, D_MODEL), ("final_norm_g", ROW_FINAL, D_MODEL),
                                 ("q_norm_g", ROW_GQ, Q_RANK), ("kv_norm_g", ROW_GKV, KV_RANK),
                                 ("attn_out_g", ROW_ATTN, CONV_WIDTH), ("conv_out_g", ROW_CONVG, CONV_WIDTH)):
            for i in range(width // LANES):
                apply(name, gsum[row + i:row + i + 1, :], (slice(None), slice(LANES * i, LANES * (i + 1))))
        loss_out[...] = gsum[ROW_LOSS:ROW_LOSS + 1, :]

        for cp in copies:
            cp.wait_send()

    vm = pl.BlockSpec(memory_space=pltpu.VMEM)
    hbm = pl.BlockSpec(memory_space=pl.ANY)
    out_shape = []
    for _, shape in PARAM_SHAPES:
        out_shape += [jax.ShapeDtypeStruct(shape, F32)] * 4
    out_shape.append(jax.ShapeDtypeStruct((1, LANES), F32))
    flat_params = [a for name in names for a in params[name]]
    outs = pl.pallas_call(
        body,
        name="reduce_grads_adamw",
        out_shape=tuple(out_shape),
        in_specs=[hbm, hbm] + [vm] * (n_small + 3 * n_p),
        out_specs=(vm,) * len(out_shape),
        scratch_shapes=[
            pltpu.VMEM((N_DEV, SMALL_ROWS, LANES), F32),
            pltpu.VMEM((N_DEV, D_MODEL, SHARD_IN_PAD), BF16),
            pltpu.VMEM((N_DEV, SHARD_OUT, D_MODEL), BF16),
            pltpu.VMEM((N_DEV, SMALL_ROWS, LANES), F32),
            pltpu.VMEM((SMALL_ROWS, LANES), F32),
            pltpu.SemaphoreType.DMA((3 * (N_DEV - 1),)),
            pltpu.SemaphoreType.DMA((3 * (N_DEV - 1),)),
            pltpu.SemaphoreType.DMA((2,)),
        ],
        compiler_params=pltpu.CompilerParams(vmem_limit_bytes=VMEM_LIMIT),
    )(send_in, send_out, *small_grads, *flat_params)
    return {n: outs[4 * i:4 * i + 4] for i, n in enumerate(names)}, outs[-1]


def _prep(x, meta, norm_g):
    nb_seq, s, d = x.shape
    nb = s // LANES + 1

    def body(x_ref, meta_ref, g_ref, u_ref):
        j = pl.program_id(1)

        def norm(h):
            hhat, _ = _rms_stats(h)
            return (hhat * g_ref[...]).astype(BF16)

        @pl.when(j == 0)
        def _():
            u_ref[0:PAD_FRONT, :] = jnp.zeros((PAD_FRONT, d), BF16)
            u_ref[PAD_FRONT:LANES, :] = norm(meta_ref[...])

        @pl.when(j > 0)
        def _():
            u_ref[...] = norm(x_ref[0])

    return pl.pallas_call(
        body,
        name="prep_norm",
        grid=(nb_seq, nb),
        in_specs=[
            pl.BlockSpec((1, LANES, d), lambda b, j: (b, jnp.maximum(j - 1, 0), 0)),
            pl.BlockSpec((N_META, d), lambda b, j: (0, 0)),
            pl.BlockSpec((1, d), lambda b, j: (0, 0)),
        ],
        out_specs=pl.BlockSpec((LANES, d), lambda b, j: (b * nb + j, 0)),
        out_shape=jax.ShapeDtypeStruct((nb_seq * nb * LANES, d), BF16),
        compiler_params=_params("parallel", "arbitrary"),
    )(x, meta, norm_g)


def _matmul(a, b, bm, bn, out_dtype, name):
    m, k = a.shape
    _, n = b.shape

    def body(a_ref, b_ref, o_ref):
        o_ref[...] = _dot(a_ref[...], b_ref[...]).astype(out_dtype)

    return pl.pallas_call(
        body,
        name=name,
        grid=(m // bm, n // bn),
        in_specs=[pl.BlockSpec((bm, k), lambda i, j: (i, 0)), pl.BlockSpec((k, bn), lambda i, j: (0, j))],
        out_specs=pl.BlockSpec((bm, bn), lambda i, j: (i, j)),
        out_shape=jax.ShapeDtypeStruct((m, n), out_dtype),
        compiler_params=_params("parallel", "arbitrary"),
    )(a, b)


def _rope_tables(tp):
    half = D_ROPE // 2
    inv_freq = 1.0 / (ROPE_THETA ** (jnp.arange(half, dtype=F32) / half))
    pos = (jnp.arange(tp) - PAD_FRONT).astype(F32)
    ang = pos[:, None] * inv_freq[None, :]
    cos = jnp.tile(jnp.cos(ang), (1, LANES // half))
    sin = jnp.tile(jnp.sin(ang), (1, LANES // half))
    first = (jnp.arange(LANES) % D_ROPE) < half
    return cos, jnp.where(first, -sin, 0.0), jnp.where(first, 0.0, sin)


def _rope(t, cos, sa, sb):
    return t * cos + pltpu.roll(t, LANES - D_ROPE // 2, 1) * sa + pltpu.roll(t, D_ROPE // 2, 1) * sb


def _rope_t(t, cos, sa, sb):
    return t * cos + pltpu.roll(t * sa, D_ROPE // 2, 1) + pltpu.roll(t * sb, LANES - D_ROPE // 2, 1)


def _qkv_fwd(p, wq, wkv, gq, gkv, tables, nb_seq, tp):
    ht = tp // 2

    def body(pa_ref, wq_ref, wkv_ref, gq_ref, gkv_ref, cos_ref, sa_ref, sb_ref, q_ref, k_ref, v_ref):
        pa = pa_ref[...].astype(F32)
        cq_hat, _ = _rms_stats(pa[:, :Q_RANK])
        ckv_hat, _ = _rms_stats(pa[:, Q_RANK:Q_RANK + KV_RANK])
        q = _dot((cq_hat * gq_ref[...]).astype(BF16), wq_ref[...])
        kv = _dot((ckv_hat * gkv_ref[...]).astype(BF16), wkv_ref[...])
        tabs = (cos_ref[...], sa_ref[...], sb_ref[...])
        low = lax.broadcasted_iota(jnp.int32, (ht, LANES), 1) < D_ROPE
        k_pe = _rope(pa[:, Q_RANK + KV_RANK:], *tabs)
        pairs = [_rope(q[:, N_HEADS * D_NOPE + LANES * i:N_HEADS * D_NOPE + LANES * (i + 1)], *tabs) for i in range(2)]
        for h in range(N_HEADS):
            pair = pairs[h // 2]
            if h % 2:
                pair = pltpu.roll(pair, D_ROPE, 1)
            pe = jnp.where(low, pair, 0.0)
            q_ref[0, h] = jnp.concatenate([q[:, D_NOPE * h:D_NOPE * (h + 1)], pe], axis=1).astype(BF16)
            k_ref[0, h] = jnp.concatenate([kv[:, D_NOPE * h:D_NOPE * (h + 1)], k_pe], axis=1).astype(BF16)
            v_ref[0, h] = kv[:, N_HEADS * D_NOPE + D_V * h:N_HEADS * D_NOPE + D_V * (h + 1)].astype(BF16)

    full = lambda a: pl.BlockSpec(a.shape, lambda i: (0,) * a.ndim)
    tab = pl.BlockSpec((ht, LANES), lambda i: (i % 2, 0))
    qk = pl.BlockSpec((1, N_HEADS, ht, 2 * LANES), lambda i: (i // 2, 0, i % 2, 0))
    return pl.pallas_call(
        body,
        name="qkv_fwd",
        grid=(2 * nb_seq,),
        in_specs=[pl.BlockSpec((ht, GRP_A), lambda i: (i, 0)), full(wq), full(wkv), full(gq), full(gkv), tab, tab, tab],
        out_specs=(qk, qk, pl.BlockSpec((1, N_HEADS, ht, D_V), lambda i: (i // 2, 0, i % 2, 0))),
        out_shape=(
            jax.ShapeDtypeStruct((nb_seq, N_HEADS, tp, 2 * LANES), BF16),
            jax.ShapeDtypeStruct((nb_seq, N_HEADS, tp, 2 * LANES), BF16),
            jax.ShapeDtypeStruct((nb_seq, N_HEADS, tp, D_V), BF16),
        ),
        compiler_params=_params("parallel"),
    )(p, wq, wkv, gq, gkv, *tables)


def _attn_fwd(q, k, v, p, g_attn):
    nb_seq, _, tp, _ = q.shape

    def body(q_ref, k_ref, v_ref, z_ref, g_ref, y_ref, o_ref, lse_ref):
        g = g_ref[...]
        for r0 in range(0, tp, KV_TILE):
            nq = min(KV_TILE, tp - r0)
            kend = r0 + nq
            s = _dot(q_ref[0, 0, r0:kend, :], k_ref[0, 0, 0:kend, :], _NT) * ATTN_SCALE
            qpos = r0 + lax.broadcasted_iota(jnp.int32, (nq, kend), 0)
            kpos = lax.broadcasted_iota(jnp.int32, (nq, kend), 1)
            s = jnp.where((kpos <= qpos) & (kpos >= PAD_FRONT), s, NEG_INF)
            m = jnp.max(s, axis=-1, keepdims=True)
            e = jnp.exp(s - m)
            l = jnp.sum(e, axis=-1, keepdims=True)
            o = _dot(e.astype(BF16), v_ref[0, 0, 0:kend, :]) * (1.0 / l)
            o_ref[0, 0, r0:kend, :] = o
            lse_ref[0, 0, r0:kend, :] = jnp.broadcast_to(m + jnp.log(l), (nq, LANES))
            ohat, _ = _rms_stats(o)
            z = z_ref[r0:kend, :].astype(F32)
            y_ref[r0:kend, :] = (ohat * g * (z * _sigmoid(z))).astype(BF16)

    qk = pl.BlockSpec((1, 1, tp, 2 * LANES), lambda b, h: (b, h, 0, 0))
    hv = pl.BlockSpec((1, 1, tp, D_V), lambda b, h: (b, h, 0, 0))
    return pl.pallas_call(
        body,
        name="attn_fwd",
        grid=(nb_seq, N_HEADS),
        in_specs=[qk, qk, hv,
                  pl.BlockSpec((tp, LANES), lambda b, h: (b, GRP_A // LANES + h)),
                  pl.BlockSpec((1, LANES), lambda b, h: (0, h))],
        out_specs=(pl.BlockSpec((tp, LANES), lambda b, h: (b, h)), hv, hv),
        out_shape=(
            jax.ShapeDtypeStruct((nb_seq * tp, N_HEADS * D_V), BF16),
            jax.ShapeDtypeStruct((nb_seq, N_HEADS, tp, D_V), F32),
            jax.ShapeDtypeStruct((nb_seq, N_HEADS, tp, LANES), F32),
        ),
        compiler_params=_params("parallel", "parallel"),
    )(q, k, v, p, g_attn)


_CONV_COL0 = (GRP_A + N_HEADS * D_V) // LANES


def _conv_specs(tp, order):
    cols = CONV_WIDTH // LANES
    return [pl.BlockSpec((tp, LANES), functools.partial(
        lambda a, b, off: order(a, b, off), off=_CONV_COL0 + i * cols)) for i in range(4)]


def _conv_fwd(p, conv_w, g_conv, nb_seq, tp):
    def body(b_ref, c_ref, h_ref, z_ref, w_ref, g_ref, y_ref):
        cc = c_ref[...].astype(F32) * h_ref[...].astype(F32)
        row = lax.broadcasted_iota(jnp.int32, (tp, LANES), 0)
        s1 = jnp.where(row >= 1, pltpu.roll(cc, 1, 0), 0.0)
        s2 = jnp.where(row >= 2, pltpu.roll(cc, 2, 0), 0.0)
        yc = b_ref[...].astype(F32) * (w_ref[0:1, :] * s2 + w_ref[1:2, :] * s1 + w_ref[2:3, :] * cc)
        r = lax.rsqrt(_group_mean(yc * yc) + EPS)
        z = z_ref[...].astype(F32)
        y_ref[...] = (yc * r * g_ref[...] * (z * _sigmoid(z))).astype(BF16)

    return pl.pallas_call(
        body,
        name="conv_fwd",
        grid=(nb_seq, CONV_WIDTH // LANES),
        in_specs=_conv_specs(tp, lambda b, t, off: (b, off + t)) + [
            pl.BlockSpec((8, LANES), lambda b, t: (0, t)),
            pl.BlockSpec((1, LANES), lambda b, t: (0, t))],
        out_specs=pl.BlockSpec((tp, LANES), lambda b, t: (b, t)),
        out_shape=jax.ShapeDtypeStruct((nb_seq * tp, CONV_WIDTH), BF16),
        compiler_params=_params("parallel", "parallel"),
    )(p, p, p, p, conv_w, g_conv)


def _out_proj(ya, yc, w_out, bm):
    r, ka = ya.shape
    d = w_out.shape[1]

    def body(a_ref, c_ref, w_ref, o_ref):
        o_ref[...] = _dot(a_ref[...], w_ref[0:ka, :]) + _dot(c_ref[...], w_ref[ka:, :])

    return pl.pallas_call(
        body,
        name="out_proj",
        grid=(r // bm,),
        in_specs=[pl.BlockSpec((bm, ka), lambda i: (i, 0)), pl.BlockSpec((bm, yc.shape[1]), lambda i: (i, 0)),
                  pl.BlockSpec(w_out.shape, lambda i: (0, 0))],
        out_specs=pl.BlockSpec((bm, d), lambda i: (i, 0)),
        out_shape=jax.ShapeDtypeStruct((r, d), F32),
        compiler_params=_params("parallel"),
    )(ya, yc, w_out)


def _loss_bwd(x, mix, target, g_final):
    nb_seq, s, d = x.shape
    nb = s // LANES + 1

    def body(x_ref, mix_ref, t_ref, g_ref, dh_ref, dhb_ref, dg_ref, loss_ref, acc_ref):
        b, j = pl.program_id(0), pl.program_id(1)

        @pl.when((b == 0) & (j == 0))
        def _():
            acc_ref[...] = jnp.zeros_like(acc_ref)
            dg_ref[...] = jnp.zeros_like(dg_ref)

        @pl.when(j == 0)
        def _():
            dh_ref[...] = jnp.zeros_like(dh_ref)
            dhb_ref[...] = jnp.zeros_like(dhb_ref)

        @pl.when(j > 0)
        def _():
            g = g_ref[...]
            hhat, r = _rms_stats(x_ref[0] + mix_ref[...])
            e = hhat * g - t_ref[0]
            acc_ref[...] += jnp.sum(e * e, axis=0, keepdims=True)
            dy = e * (1.0 / d)
            dg_ref[...] += jnp.sum(dy * hhat, axis=0, keepdims=True)
            dh = _rms_bwd(g * dy, hhat, r)
            dh_ref[...] = dh
            dhb_ref[...] = dh.astype(BF16)

        @pl.when((b == nb_seq - 1) & (j == nb - 1))
        def _():
            total = jnp.sum(acc_ref[...], axis=1, keepdims=True)
            loss_ref[...] = jnp.broadcast_to((0.5 / d) * total, loss_ref.shape)

    tok = pl.BlockSpec((1, LANES, d), lambda b, j: (b, jnp.maximum(j - 1, 0), 0))
    row = pl.BlockSpec((LANES, d), lambda b, j: (b * nb + j, 0))
    vec = pl.BlockSpec((1, d), lambda b, j: (0, 0))
    return pl.pallas_call(
        body,
        name="loss_bwd",
        grid=(nb_seq, nb),
        in_specs=[tok, row, tok, vec],
        out_specs=(row, row, vec, pl.BlockSpec((1, LANES), lambda b, j: (0, 0))),
        out_shape=(
            jax.ShapeDtypeStruct((nb_seq * nb * LANES, d), F32),
            jax.ShapeDtypeStruct((nb_seq * nb * LANES, d), BF16),
            jax.ShapeDtypeStruct((1, d), F32),
            jax.ShapeDtypeStruct((1, LANES), F32),
        ),
        scratch_shapes=[pltpu.VMEM((1, d), F32)],
        compiler_params=_params("arbitrary", "arbitrary"),
    )(x, mix, target, g_final)


def _out_proj_bwd(dhb, w_out, ya, yc, bm):
    r, d = dhb.shape
    ka = ya.shape[1]
    n_mix = w_out.shape[0]
    last = r // bm - 1

    def body(dh_ref, w_ref, a_ref, c_ref, dcat_ref, dw_ref, acc_ref):
        @pl.when(pl.program_id(0) == 0)
        def _():
            acc_ref[...] = jnp.zeros_like(acc_ref)

        dh = dh_ref[...]
        dcat_ref[...] = _dot(dh, w_ref[...], _NT).astype(BF16)
        acc_ref[0:ka, :] += _dot(a_ref[...], dh, _TN)
        acc_ref[ka:, :] += _dot(c_ref[...], dh, _TN)

        @pl.when(pl.program_id(0) == last)
        def _():
            dw_ref[...] = acc_ref[...].astype(BF16)

    return pl.pallas_call(
        body,
        name="out_proj_bwd",
        grid=(r // bm,),
        in_specs=[pl.BlockSpec((bm, d), lambda i: (i, 0)), pl.BlockSpec(w_out.shape, lambda i: (0, 0)),
                  pl.BlockSpec((bm, ka), lambda i: (i, 0)), pl.BlockSpec((bm, yc.shape[1]), lambda i: (i, 0))],
        out_specs=(pl.BlockSpec((bm, n_mix), lambda i: (i, 0)),
                   pl.BlockSpec((n_mix, d), lambda i: (0, 0))),
        out_shape=(jax.ShapeDtypeStruct((r, n_mix), BF16),
                   jax.ShapeDtypeStruct((n_mix, d), BF16)),
        scratch_shapes=[pltpu.VMEM((n_mix, d), F32)],
        compiler_params=_params("arbitrary"),
    )(dhb, w_out, ya, yc)


def _attn_bwd(q, k, v, o, lse, dcat, p, g_attn):
    nb_seq, _, tp, _ = q.shape

    def body(q_ref, k_ref, v_ref, o_ref, lse_ref, dy_ref, z_ref, g_ref,
             dq_ref, dk_ref, dv_ref, dz_ref, dg_ref, dq_acc):
        @pl.when(pl.program_id(1) == 0)
        def _():
            dg_ref[...] = jnp.zeros_like(dg_ref)

        g = g_ref[...]
        z = z_ref[...].astype(F32)
        o = o_ref[0, 0]
        dy = dy_ref[...].astype(F32)
        sig = _sigmoid(z)
        ohat, r = _rms_stats(o)
        don = dy * (z * sig)
        dz_ref[...] = (dy * (ohat * g) * (sig * (1.0 + z * (1.0 - sig)))).astype(BF16)
        dg_ref[...] += jnp.sum(don * ohat, axis=0, keepdims=True)
        do = _rms_bwd(g * don, ohat, r)
        dvec = jnp.sum(do * o, axis=-1, keepdims=True)
        dob = do.astype(BF16)
        lse_col = lse_ref[0, 0, :, 0:1]
        dq_acc[...] = jnp.zeros_like(dq_acc)
        for k0 in range(0, tp, KV_TILE):
            nk = min(KV_TILE, tp - k0)
            nq = tp - k0
            qq = q_ref[0, 0, k0:, :]
            kk = k_ref[0, 0, k0:k0 + nk, :]
            s = _dot(qq, kk, _NT) * ATTN_SCALE
            qpos = lax.broadcasted_iota(jnp.int32, (nq, nk), 0)
            kpos = lax.broadcasted_iota(jnp.int32, (nq, nk), 1)
            valid = kpos <= qpos
            if k0 == 0:
                valid = valid & (kpos >= PAD_FRONT)
            pr = jnp.where(valid, jnp.exp(s - lse_col[k0:]), 0.0)
            dp = _dot(dob[k0:], v_ref[0, 0, k0:k0 + nk, :], _NT)
            ds = (pr * (dp - dvec[k0:]) * ATTN_SCALE).astype(BF16)
            dv_ref[0, 0, k0:k0 + nk, :] = _dot(pr.astype(BF16), dob[k0:], _TN).astype(BF16)
            dk_ref[0, 0, k0:k0 + nk, :] = _dot(ds, qq, _TN).astype(BF16)
            dq_acc[k0:, :] += _dot(ds, kk)
        dq_ref[0, 0] = dq_acc[...].astype(BF16)

    qk = pl.BlockSpec((1, 1, tp, 2 * LANES), lambda h, b: (b, h, 0, 0))
    hv = pl.BlockSpec((1, 1, tp, D_V), lambda h, b: (b, h, 0, 0))
    col = pl.BlockSpec((tp, LANES), lambda h, b: (b, h))
    return pl.pallas_call(
        body,
        name="attn_bwd",
        grid=(N_HEADS, nb_seq),
        in_specs=[qk, qk, hv, hv, hv, col,
                  pl.BlockSpec((tp, LANES), lambda h, b: (b, GRP_A // LANES + h)),
                  pl.BlockSpec((1, LANES), lambda h, b: (0, h))],
        out_specs=(qk, qk, hv, col, pl.BlockSpec((1, LANES), lambda h, b: (0, h))),
        out_shape=(
            jax.ShapeDtypeStruct((nb_seq, N_HEADS, tp, 2 * LANES), BF16),
            jax.ShapeDtypeStruct((nb_seq, N_HEADS, tp, 2 * LANES), BF16),
            jax.ShapeDtypeStruct((nb_seq, N_HEADS, tp, D_V), BF16),
            jax.ShapeDtypeStruct((nb_seq * tp, N_HEADS * D_V), BF16),
            jax.ShapeDtypeStruct((1, N_HEADS * D_V), F32),
        ),
        scratch_shapes=[pltpu.VMEM((tp, 2 * LANES), F32)],
        compiler_params=_params("arbitrary", "arbitrary"),
    )(q, k, v, o, lse, dcat, p, g_attn)


def _qkv_bwd(p, dq, dk, dv, wq, wkv, gq, gkv, tables):
    nb_seq, _, tp, _ = dq.shape
    ht = tp // 2

    def body(pa_ref, dq_ref, dk_ref, dv_ref, wq_ref, wkv_ref, gq_ref, gkv_ref, cos_ref, sa_ref, sb_ref,
             dpa_ref, dwq_ref, dwkv_ref, dgq_ref, dgkv_ref):
        @pl.when(pl.program_id(0) == 0)
        def _():
            dwq_ref[...] = jnp.zeros_like(dwq_ref)
            dwkv_ref[...] = jnp.zeros_like(dwkv_ref)
            dgq_ref[...] = jnp.zeros_like(dgq_ref)
            dgkv_ref[...] = jnp.zeros_like(dgkv_ref)

        pa = pa_ref[...].astype(F32)
        gq, gkv = gq_ref[...], gkv_ref[...]
        cq_hat, rq = _rms_stats(pa[:, :Q_RANK])
        ckv_hat, rkv = _rms_stats(pa[:, Q_RANK:Q_RANK + KV_RANK])
        tabs = (cos_ref[...], sa_ref[...], sb_ref[...])

        pe = [dq_ref[0, h, :, D_NOPE:].astype(F32) for h in range(N_HEADS)]
        pairs = [_rope_t(pe[2 * i] + pltpu.roll(pe[2 * i + 1], D_ROPE, 1), *tabs).astype(BF16) for i in range(2)]
        dq_flat = jnp.concatenate([dq_ref[0, h, :, :D_NOPE] for h in range(N_HEADS)] + pairs, axis=1)
        dwq_ref[...] += _dot((cq_hat * gq).astype(BF16), dq_flat, _TN)
        dcqn = _dot(dq_flat, wq_ref[...], _NT)
        dgq_ref[...] += jnp.sum(dcqn * cq_hat, axis=0, keepdims=True)
        dcq = _rms_bwd(gq * dcqn, cq_hat, rq)

        dkv_flat = jnp.concatenate([dk_ref[0, h, :, :D_NOPE] for h in range(N_HEADS)]
                                   + [dv_ref[0, h] for h in range(N_HEADS)], axis=1)
        dwkv_ref[...] += _dot((ckv_hat * gkv).astype(BF16), dkv_flat, _TN)
        dckvn = _dot(dkv_flat, wkv_ref[...], _NT)
        dgkv_ref[...] += jnp.sum(dckvn * ckv_hat, axis=0, keepdims=True)
        dckv = _rms_bwd(gkv * dckvn, ckv_hat, rkv)

        dk_pe = dk_ref[0, 0, :, D_NOPE:].astype(F32)
        for h in range(1, N_HEADS):
            dk_pe = dk_pe + dk_ref[0, h, :, D_NOPE:].astype(F32)
        dpa_ref[...] = jnp.concatenate([dcq, dckv, _rope_t(dk_pe, *tabs)], axis=1).astype(BF16)

    full = lambda a: pl.BlockSpec(a.shape, lambda i: (0,) * a.ndim)
    tab = pl.BlockSpec((ht, LANES), lambda i: (i % 2, 0))
    qk = pl.BlockSpec((1, N_HEADS, ht, 2 * LANES), lambda i: (i // 2, 0, i % 2, 0))
    acc = lambda shape: pl.BlockSpec(shape, lambda i: (0, 0))
    return pl.pallas_call(
        body,
        name="qkv_bwd",
        grid=(2 * nb_seq,),
        in_specs=[pl.BlockSpec((ht, GRP_A), lambda i: (i, 0)), qk, qk,
                  pl.BlockSpec((1, N_HEADS, ht, D_V), lambda i: (i // 2, 0, i % 2, 0)),
                  full(wq), full(wkv), full(gq), full(gkv), tab, tab, tab],
        out_specs=(pl.BlockSpec((ht, GRP_A), lambda i: (i, 0)),
                   acc(wq.shape), acc(wkv.shape), acc((1, Q_RANK)), acc((1, KV_RANK))),
        out_shape=(
            jax.ShapeDtypeStruct((nb_seq * tp, GRP_A), BF16),
            jax.ShapeDtypeStruct(wq.shape, F32),
            jax.ShapeDtypeStruct(wkv.shape, F32),
            jax.ShapeDtypeStruct((1, Q_RANK), F32),
            jax.ShapeDtypeStruct((1, KV_RANK), F32),
        ),
        compiler_params=_params("arbitrary"),
    )(p, dq, dk, dv, wq, wkv, gq, gkv, *tables)


def _conv_bwd(p, dcat, conv_w, g_conv, nb_seq, tp):
    cols = CONV_WIDTH // LANES

    def body(b_ref, c_ref, h_ref, z_ref, dy_ref, w_ref, g_ref,
             db_ref, dc_ref, dh_ref, dz_ref, dw_ref, dg_ref):
        @pl.when(pl.program_id(1) == 0)
        def _():
            dw_ref[...] = jnp.zeros_like(dw_ref)
            dg_ref[...] = jnp.zeros_like(dg_ref)

        cb, c, h = b_ref[...].astype(F32), c_ref[...].astype(F32), h_ref[...].astype(F32)
        z, dy = z_ref[...].astype(F32), dy_ref[...].astype(F32)
        g = g_ref[...]
        w0, w1, w2 = w_ref[0:1, :], w_ref[1:2, :], w_ref[2:3, :]
        cc = c * h
        row = lax.broadcasted_iota(jnp.int32, (tp, LANES), 0)
        s1 = jnp.where(row >= 1, pltpu.roll(cc, 1, 0), 0.0)
        s2 = jnp.where(row >= 2, pltpu.roll(cc, 2, 0), 0.0)
        dwc = w0 * s2 + w1 * s1 + w2 * cc
        yc = cb * dwc
        r = lax.rsqrt(_group_mean(yc * yc) + EPS)
        ychat = yc * r
        sig = _sigmoid(z)
        dz_ref[...] = (dy * (ychat * g) * (sig * (1.0 + z * (1.0 - sig)))).astype(BF16)
        dyn = dy * (z * sig)
        dg_ref[...] += jnp.sum(dyn * ychat, axis=0, keepdims=True)
        gd = g * dyn
        dyc = r * (gd - ychat * _group_mean(gd * ychat))
        db_ref[...] = (dyc * dwc).astype(BF16)
        ddw = dyc * cb
        dw_ref[0:1, :] += jnp.sum(ddw * s2, axis=0, keepdims=True)
        dw_ref[1:2, :] += jnp.sum(ddw * s1, axis=0, keepdims=True)
        dw_ref[2:3, :] += jnp.sum(ddw * cc, axis=0, keepdims=True)
        u1 = jnp.where(row <= tp - 2, pltpu.roll(ddw, tp - 1, 0), 0.0)
        u2 = jnp.where(row <= tp - 3, pltpu.roll(ddw, tp - 2, 0), 0.0)
        dcc = w2 * ddw + w1 * u1 + w0 * u2
        dc_ref[...] = (dcc * h).astype(BF16)
        dh_ref[...] = (dcc * c).astype(BF16)

    col = pl.BlockSpec((tp, LANES), lambda t, b: (b, t))
    out = jax.ShapeDtypeStruct((nb_seq * tp, CONV_WIDTH), BF16)
    return pl.pallas_call(
        body,
        name="conv_bwd",
        grid=(cols, nb_seq),
        in_specs=_conv_specs(tp, lambda t, b, off: (b, off + t)) + [
            pl.BlockSpec((tp, LANES), lambda t, b: (b, N_HEADS * D_V // LANES + t)),
            pl.BlockSpec((8, LANES), lambda t, b: (0, t)),
            pl.BlockSpec((1, LANES), lambda t, b: (0, t))],
        out_specs=(col, col, col, col,
                   pl.BlockSpec((8, LANES), lambda t, b: (0, t)), pl.BlockSpec((1, LANES), lambda t, b: (0, t))),
        out_shape=(out, out, out, out,
                   jax.ShapeDtypeStruct((8, CONV_WIDTH), F32), jax.ShapeDtypeStruct((1, CONV_WIDTH), F32)),
        compiler_params=_params("arbitrary", "arbitrary"),
    )(p, p, p, p, dcat, conv_w, g_conv)


def _in_proj_bwd_x(dps, w_in, bm):
    r, kb = dps[0].shape
    d = w_in.shape[0]

    def body(*refs):
        dp_refs, w_ref, o_ref = refs[:len(dps)], refs[len(dps)], refs[len(dps) + 1]
        acc = _dot(dp_refs[0][...], w_ref[:, 0:kb], _NT)
        for j in range(1, len(dps)):
            acc = acc + _dot(dp_refs[j][...], w_ref[:, kb * j:kb * (j + 1)], _NT)
        o_ref[...] = acc

    return pl.pallas_call(
        body,
        name="in_proj_bwd_x",
        grid=(r // bm,),
        in_specs=[pl.BlockSpec((bm, kb), lambda i: (i, 0)) for _ in dps] + [pl.BlockSpec(w_in.shape, lambda i: (0, 0))],
        out_specs=pl.BlockSpec((bm, d), lambda i: (i, 0)),
        out_shape=jax.ShapeDtypeStruct((r, d), F32),
        compiler_params=_params("parallel"),
    )(*dps, w_in)


def _in_proj_bwd_w(u, dps, bm):
    r, d = u.shape
    kb = dps[0].shape[1]
    last = r // bm - 1

    def body(*refs):
        u_ref, dp_refs, o_ref, acc_ref = refs[0], refs[1:1 + len(dps)], refs[1 + len(dps)], refs[2 + len(dps)]

        @pl.when(pl.program_id(0) == 0)
        def _():
            acc_ref[...] = jnp.zeros_like(acc_ref)

        uu = u_ref[...]
        for j in range(len(dps)):
            acc_ref[:, kb * j:kb * (j + 1)] += _dot(uu, dp_refs[j][...], _TN)

        @pl.when(pl.program_id(0) == last)
        def _():
            for k in range(N_DEV):
                for s, e, c0 in _in_pieces(k):
                    o_ref[k, :, s:e] = acc_ref[:, c0:c0 + e - s].astype(BF16)
                o_ref[k, :, SHARD_IN:] = jnp.zeros((d, SHARD_IN_PAD - SHARD_IN), BF16)

    return pl.pallas_call(
        body,
        name="in_proj_bwd_w",
        grid=(r // bm,),
        in_specs=[pl.BlockSpec((bm, d), lambda i: (i, 0))] + [pl.BlockSpec((bm, kb), lambda i: (i, 0)) for _ in dps],
        out_specs=pl.BlockSpec((N_DEV, d, SHARD_IN_PAD), lambda i: (0, 0, 0)),
        out_shape=jax.ShapeDtypeStruct((N_DEV, d, SHARD_IN_PAD), BF16),
        scratch_shapes=[pltpu.VMEM((d, kb * len(dps)), F32)],
        compiler_params=_params("arbitrary"),
    )(u, *dps)


def _input_bwd(x, meta, du, dh, norm_g):
    nb_seq, s, d = x.shape
    nb = s // LANES + 1

    def body(x_ref, meta_ref, du_ref, dh_ref, g_ref, gx_ref, dmeta_ref, dg_ref):
        b, j = pl.program_id(0), pl.program_id(1)
        g = g_ref[...]

        @pl.when((b == 0) & (j == 0))
        def _():
            dmeta_ref[...] = jnp.zeros_like(dmeta_ref)
            dg_ref[...] = jnp.zeros_like(dg_ref)

        def bwd(h0, du_, dh_):
            hhat, r = _rms_stats(h0)
            dg_ref[...] += jnp.sum(du_ * hhat, axis=0, keepdims=True)
            return _rms_bwd(g * du_, hhat, r) + dh_

        @pl.when(j == 0)
        def _():
            dmeta_ref[...] += bwd(meta_ref[...], du_ref[PAD_FRONT:LANES, :], dh_ref[PAD_FRONT:LANES, :])

        @pl.when(j > 0)
        def _():
            gx_ref[0] = bwd(x_ref[0], du_ref[...], dh_ref[...])

    tok = pl.BlockSpec((1, LANES, d), lambda b, j: (b, jnp.maximum(j - 1, 0), 0))
    row = pl.BlockSpec((LANES, d), lambda b, j: (b * nb + j, 0))
    vec = pl.BlockSpec((1, d), lambda b, j: (0, 0))
    return pl.pallas_call(
        body,
        name="input_bwd",
        grid=(nb_seq, nb),
        in_specs=[tok, pl.BlockSpec((N_META, d), lambda b, j: (0, 0)), row, row, vec],
        out_specs=(tok, pl.BlockSpec((N_META, d), lambda b, j: (0, 0)), vec),
        out_shape=(
            jax.ShapeDtypeStruct((nb_seq, s, d), F32),
            jax.ShapeDtypeStruct((N_META, d), F32),
            jax.ShapeDtypeStruct((1, d), F32),
        ),
        compiler_params=_params("arbitrary", "arbitrary"),
    )(x, meta, du, dh, norm_g)


def _local_step(x, loss_target, meta_f, norm_g, w_in_p, q_norm_g, w_q_p, kv_norm_g, w_kv_p, conv_w_f,
                attn_out_g, conv_out_g, w_out_f, g_final):
    nb_seq, s, d = x.shape
    tp = s + LANES
    ht = tp // 2
    tables = _rope_tables(tp)

    u = _prep(x, meta_f, norm_g)
    p = _matmul(u, w_in_p, ht, GRP_A, BF16, "in_proj")
    q, k, v = _qkv_fwd(p, w_q_p, w_kv_p, q_norm_g, kv_norm_g, tables, nb_seq, tp)
    ya, o, lse = _attn_fwd(q, k, v, p, attn_out_g)
    yc = _conv_fwd(p, conv_w_f, conv_out_g, nb_seq, tp)
    mix = _out_proj(ya, yc, w_out_f, ht)
    dh, dhb, d_final_g, loss_part = _loss_bwd(x, mix, loss_target, g_final)

    dcat, d_w_out = _out_proj_bwd(dhb, w_out_f, ya, yc, ht)
    dq, dk, dv, dz_attn, d_attn_g = _attn_bwd(q, k, v, o, lse, dcat, p, attn_out_g)
    dpa, d_wq_p, d_wkv_p, d_gq, d_gkv = _qkv_bwd(p, dq, dk, dv, w_q_p, w_kv_p, q_norm_g, kv_norm_g, tables)
    d_b, d_c, d_h, dz_conv, d_conv_w, d_conv_g = _conv_bwd(p, dcat, conv_w_f, conv_out_g, nb_seq, tp)
    dps = (dpa, dz_attn, d_b, d_c, d_h, dz_conv)
    du = _in_proj_bwd_x(dps, w_in_p, ht)
    send_in = _in_proj_bwd_w(u, dps, ht // 2)
    grad_x, d_meta, d_norm_g = _input_bwd(x, meta_f, du, dh, norm_g)
    send_out = d_w_out.reshape(N_DEV, SHARD_OUT, d)
    small = (d_wq_p, d_wkv_p, d_conv_w, d_meta, d_norm_g, d_final_g, d_gq, d_gkv, d_attn_g, d_conv_g, loss_part)
    return grad_x, send_in, send_out, small


def kernel(x, meta_tokens, norm_g, w_in, q_norm_g, w_q_up, kv_norm_g, w_kv_up, conv_w, attn_out_g, conv_out_g, w_out, final_norm_g, loss_target, m_meta_tokens, m_norm_g, m_w_in, m_q_norm_g, m_w_q_up, m_kv_norm_g, m_w_kv_up, m_conv_w, m_attn_out_g, m_conv_out_g, m_w_out, m_final_norm_g, v_meta_tokens, v_norm_g, v_w_in, v_q_norm_g, v_w_q_up, v_kv_norm_g, v_w_kv_up, v_conv_w, v_attn_out_g, v_conv_out_g, v_w_out, v_final_norm_g):
    d = x.shape[-1]
    w_in_p, w_q_p, w_kv_p, w_out_f, conv_w_f, meta_f = _gather_weights(
        w_in[0], w_q_up[0], w_kv_up[0], w_out[0], conv_w[0], meta_tokens)
    g_final = final_norm_g.reshape(1, d)
    grad_x, send_in, send_out, small = _local_step(
        x, loss_target, meta_f, norm_g, w_in_p, q_norm_g, w_q_p, kv_norm_g, w_kv_p, conv_w_f,
        attn_out_g, conv_out_g, w_out_f, g_final)

    flat = lambda a: a.reshape(a.shape[-2:]) if a.ndim == 3 else a.reshape(1, -1) if a.ndim == 1 else a
    params = {
        "meta_tokens": (meta_tokens, m_meta_tokens, v_meta_tokens),
        "norm_g": (norm_g, m_norm_g, v_norm_g),
        "w_in": (w_in, m_w_in, v_w_in),
        "q_norm_g": (q_norm_g, m_q_norm_g, v_q_norm_g),
        "w_q_up": (w_q_up, m_w_q_up, v_w_q_up),
        "kv_norm_g": (kv_norm_g, m_kv_norm_g, v_kv_norm_g),
        "w_kv_up": (w_kv_up, m_w_kv_up, v_w_kv_up),
        "conv_w": (conv_w, m_conv_w, v_conv_w),
        "attn_out_g": (attn_out_g, m_attn_out_g, v_attn_out_g),
        "conv_out_g": (conv_out_g, m_conv_out_g, v_conv_out_g),
        "w_out": (w_out, m_w_out, v_w_out),
        "final_norm_g": (final_norm_g, m_final_norm_g, v_final_norm_g),
    }
    updated, loss = _reduce_adam(send_in, send_out, small,
                                 {n: tuple(flat(a) for a in t) for n, t in params.items()})
    outs = [[updated[n][i].reshape(params[n][0].shape) for n, _ in PARAM_SHAPES] for i in range(4)]
    return (loss[0, 0], grad_x, *outs[0], *outs[1], *outs[2], *outs[3])
```

```python
import functools

import jax
import jax.numpy as jnp
from jax import lax
from jax.experimental import pallas as pl
from jax.experimental.pallas import tpu as pltpu

F32 = jnp.float32
BF16 = jnp.bfloat16

N_META = 16
D_MODEL = 1024
N_HEADS = 4
D_NOPE = 128
D_ROPE = 64
D_V = 128
Q_RANK = 256
KV_RANK = 128
CONV_WIDTH = 512
CONV_GROUP = 64
ROPE_THETA = 10000.0
ATTN_SCALE = (D_NOPE + D_ROPE) ** -0.5
EPS = 1e-6
NEG_INF = -1e30

ADAM_LR = 0.001
ADAM_B1 = 0.9
ADAM_B2 = 0.999
ADAM_EPS = 1e-08
ADAM_WD = 0.01
ADAM_STEP = 10

LANES = 128
PAD_FRONT = LANES - N_META
KV_TILE = 256
N_DEV = 8
VMEM_LIMIT = 56 * 1024 * 1024

IN_PAD = 3072
GRP_A = 512
N_A = Q_RANK + KV_RANK + D_ROPE
IN_PROJ = 3008
SHARD_IN = IN_PROJ // N_DEV
SHARD_IN_PAD = 384
SHARD_Q = 96
SHARD_KV = 128
SHARD_OUT = 128
SHARD_CONV = 64
SHARD_META = 128
Q_COLS = N_HEADS * (D_NOPE + D_ROPE)
KV_COLS = N_HEADS * (D_NOPE + D_V)
GATHER_COLS = 512
GATHER_SMALL_ROWS = 24

ROW_Q, ROW_KV, ROW_META, ROW_CONV = 0, 256, 384, 400
ROW_REPL = 408
ROW_NORM, ROW_FINAL, ROW_GQ, ROW_GKV, ROW_ATTN, ROW_CONVG, ROW_LOSS = 408, 416, 424, 426, 427, 431, 435
SMALL_ROWS = 440
ADAM_CHUNK = 64

PARAM_SHAPES = (
    ("meta_tokens", (N_META, SHARD_META)), ("norm_g", (1, D_MODEL)), ("w_in", (D_MODEL, SHARD_IN)),
    ("q_norm_g", (1, Q_RANK)), ("w_q_up", (Q_RANK, SHARD_Q)), ("kv_norm_g", (1, KV_RANK)),
    ("w_kv_up", (KV_RANK, SHARD_KV)), ("conv_w", (3, SHARD_CONV)), ("attn_out_g", (1, CONV_WIDTH)),
    ("conv_out_g", (1, CONV_WIDTH)), ("w_out", (SHARD_OUT, D_MODEL)), ("final_norm_g", (1, D_MODEL)),
)


def _in_pieces(k):
    lo, hi = SHARD_IN * k, SHARD_IN * (k + 1)
    out = []
    if lo < N_A:
        out.append((0, min(hi, N_A) - lo, lo))
    if hi > N_A:
        s = max(lo, N_A)
        out.append((s - lo, hi - lo, s + GRP_A - N_A))
    return out


def _q_pieces(k):
    lo, hi = SHARD_Q * k, SHARD_Q * (k + 1)
    out = []
    for h in range(N_HEADS):
        base = (D_NOPE + D_ROPE) * h
        s, e = max(lo, base), min(hi, base + D_NOPE)
        if s < e:
            out.append((s - lo, e - lo, D_NOPE * h + s - base))
        s, e = max(lo, base + D_NOPE), min(hi, base + D_NOPE + D_ROPE)
        if s < e:
            out.append((s - lo, e - lo, N_HEADS * D_NOPE + D_ROPE * h + s - base - D_NOPE))
    return out


def _kv_dst(k):
    return D_NOPE * (k // 2) + (N_HEADS * D_NOPE if k % 2 else 0)


def _params(*sem):
    return pltpu.CompilerParams(dimension_semantics=sem, vmem_limit_bytes=VMEM_LIMIT)


def _rms_stats(x):
    r = lax.rsqrt(jnp.mean(x * x, axis=-1, keepdims=True) + EPS)
    return x * r, r


def _rms_bwd(gdy, xhat, r):
    return r * (gdy - xhat * jnp.mean(gdy * xhat, axis=-1, keepdims=True))


def _sigmoid(z):
    return 1.0 / (1.0 + jnp.exp(-z))


def _group_mean(x):
    i0 = lax.broadcasted_iota(jnp.int32, (LANES, LANES), 0) // CONV_GROUP
    i1 = lax.broadcasted_iota(jnp.int32, (LANES, LANES), 1) // CONV_GROUP
    m = jnp.where(i0 == i1, 1.0 / CONV_GROUP, 0.0).astype(BF16)
    hi = x.astype(BF16)
    lo = (x - hi.astype(F32)).astype(BF16)
    return jnp.dot(hi, m, preferred_element_type=F32) + jnp.dot(lo, m, preferred_element_type=F32)


_NT = (((1,), (1,)), ((), ()))
_TN = (((0,), (0,)), ((), ()))


def _dot(a, b, dims=None):
    if dims is None:
        return jnp.dot(a, b, preferred_element_type=F32)
    return lax.dot_general(a, b, dims, preferred_element_type=F32)


def _device_position():
    x, y, c = lax.axis_index("x"), lax.axis_index("y"), lax.axis_index("c")
    return x, y, c, 4 * x + 2 * y + c


def _gather_weights(w_in, w_q, w_kv, w_out, conv_w, meta):
    n_arrays = 3

    def body(win_ref, wq_ref, wkv_ref, wout_ref, conv_ref, meta_ref,
             w_in_p, w_q_p, w_kv_p, w_out_f, conv_f, meta_f,
             sbig, sout, ssmall, gbig, gsmall, send_sems, recv_sems, local_sems):
        x, y, c, me_idx = _device_position()
        me, sibling = (x, y, c), (x, y, 1 - c)
        chips = [(1 - x, y), (x, 1 - y), (1 - x, 1 - y)]

        sbig[...] = jnp.zeros_like(sbig)
        sbig[:, 0:SHARD_IN] = win_ref[...].astype(BF16)
        sbig[0:Q_RANK, SHARD_IN_PAD:SHARD_IN_PAD + SHARD_Q] = wq_ref[...].astype(BF16)
        sbig[Q_RANK:Q_RANK + KV_RANK, SHARD_IN_PAD:] = wkv_ref[...].astype(BF16)
        sout[...] = wout_ref[...].astype(BF16)
        ssmall[...] = jnp.zeros_like(ssmall)
        ssmall[0:N_META, :] = meta_ref[...]
        ssmall[N_META:N_META + 3, 0:SHARD_CONV] = conv_ref[...]
        srcs = (sbig, sout, ssmall)

        def slot(a, px, py, pc):
            idx = 4 * px + 2 * py + pc
            if a == 1:
                return w_out_f.at[pl.ds(pl.multiple_of(idx * SHARD_OUT, SHARD_OUT), SHARD_OUT), :]
            return (gbig, None, gsmall)[a].at[idx]

        def copy(a, k, block, to, own=False):
            return pltpu.make_async_remote_copy(
                src_ref=srcs[a] if own else slot(a, *block),
                dst_ref=slot(a, *block),
                send_sem=send_sems.at[7 * a + k],
                recv_sem=recv_sems.at[7 * a + k],
                device_id=to,
                device_id_type=pl.DeviceIdType.MESH,
            )

        mine = [pltpu.make_async_copy(srcs[a], slot(a, *me), local_sems.at[a]) for a in range(n_arrays)]
        for cp in mine:
            cp.start()
        first = []
        for a in range(n_arrays):
            first.append(copy(a, 0, me, sibling, own=True))
            first += [copy(a, 1 + j, me, (*chip, c), own=True) for j, chip in enumerate(chips)]
        for cp in first:
            cp.start()
        passed = []
        for j, chip in enumerate(chips):
            for a in range(n_arrays):
                copy(a, 1 + j, (*chip, c), me).wait_recv()
                fwd = copy(a, 4 + j, (*chip, c), sibling)
                fwd.start()
                passed.append(fwd)
        for a in range(n_arrays):
            copy(a, 0, sibling, me).wait_recv()
            for j, chip in enumerate(chips):
                copy(a, 4 + j, (*chip, 1 - c), me).wait_recv()
        for cp in first + passed:
            cp.wait_send()
        for cp in mine:
            cp.wait()

        w_in_p[:, N_A:GRP_A] = jnp.zeros((D_MODEL, GRP_A - N_A), BF16)
        conv_f[...] = jnp.zeros_like(conv_f)
        for k in range(N_DEV):
            for s, e, d in _in_pieces(k):
                w_in_p[:, d:d + e - s] = gbig[k, :, s:e]
            for s, e, d in _q_pieces(k):
                w_q_p[:, d:d + e - s] = gbig[k, 0:Q_RANK, SHARD_IN_PAD + s:SHARD_IN_PAD + e]
            w_kv_p[:, _kv_dst(k):_kv_dst(k) + SHARD_KV] = gbig[k, Q_RANK:Q_RANK + KV_RANK, SHARD_IN_PAD:]
            meta_f[:, SHARD_META * k:SHARD_META * (k + 1)] = gsmall[k, 0:N_META, :]
            conv_f[0:3, SHARD_CONV * k:SHARD_CONV * (k + 1)] = gsmall[k, N_META:N_META + 3, 0:SHARD_CONV]

    vm = pl.BlockSpec(memory_space=pltpu.VMEM)
    return pl.pallas_call(
        body,
        name="gather_weights",
        out_shape=(
            jax.ShapeDtypeStruct((D_MODEL, IN_PAD), BF16),
            jax.ShapeDtypeStruct((Q_RANK, Q_COLS), BF16),
            jax.ShapeDtypeStruct((KV_RANK, KV_COLS), BF16),
            jax.ShapeDtypeStruct((D_MODEL, D_MODEL), BF16),
            jax.ShapeDtypeStruct((8, CONV_WIDTH), F32),
            jax.ShapeDtypeStruct((N_META, D_MODEL), F32),
        ),
        in_specs=[vm] * 6,
        out_specs=(vm,) * 6,
        scratch_shapes=[
            pltpu.VMEM((D_MODEL, GATHER_COLS), BF16),
            pltpu.VMEM((SHARD_OUT, D_MODEL), BF16),
            pltpu.VMEM((GATHER_SMALL_ROWS, LANES), F32),
            pltpu.VMEM((N_DEV, D_MODEL, GATHER_COLS), BF16),
            pltpu.VMEM((N_DEV, GATHER_SMALL_ROWS, LANES), F32),
            pltpu.SemaphoreType.DMA((7 * n_arrays,)),
            pltpu.SemaphoreType.DMA((7 * n_arrays,)),
            pltpu.SemaphoreType.DMA((n_arrays,)),
        ],
        compiler_params=pltpu.CompilerParams(vmem_limit_bytes=VMEM_LIMIT),
    )(w_in, w_q, w_kv, w_out, conv_w, meta)


def _adam_update(g, w, m, v):
    m_new = ADAM_B1 * m + (1.0 - ADAM_B1) * g
    v_new = ADAM_B2 * v + (1.0 - ADAM_B2) * (g * g)
    m_hat = m_new / (1.0 - ADAM_B1 ** ADAM_STEP)
    v_hat = v_new / (1.0 - ADAM_B2 ** ADAM_STEP)
    return -ADAM_LR * (m_hat / (jnp.sqrt(v_hat) + ADAM_EPS) + ADAM_WD * w), m_new, v_new


def _adamw(grads, params):
    names = [n for n, _ in PARAM_SHAPES]
    n_p = len(names)

    def body(*refs):
        for i in range(n_p):
            g = refs[i][...]
            w, m, v = (refs[n_p + 3 * i + j][...] for j in range(3))
            delta, m_new, v_new = _adam_update(g, w, m, v)
            for j, val in enumerate((g, delta, m_new, v_new)):
                refs[4 * n_p + 4 * i + j][...] = val

    vm = pl.BlockSpec(memory_space=pltpu.VMEM)
    out_shape = []
    for _, shape in PARAM_SHAPES:
        out_shape += [jax.ShapeDtypeStruct(shape, F32)] * 4
    outs = pl.pallas_call(
        body,
        name="adamw",
        out_shape=tuple(out_shape),
        in_specs=[vm] * (4 * n_p),
        out_specs=(vm,) * (4 * n_p),
        compiler_params=pltpu.CompilerParams(vmem_limit_bytes=VMEM_LIMIT),
    )(*[grads[n] for n in names], *[a for n in names for a in params[n]])
    return {n: outs[4 * i:4 * i + 4] for i, n in enumerate(names)}


def _reduce_grads(send_in, send_out, small_grads):
    n_small = len(small_grads)
    n_p = len(PARAM_SHAPES)
    names = [n for n, _ in PARAM_SHAPES]

    def body(*refs):
        sin_hbm, sout_hbm = refs[0], refs[1]
        (dwq, dwkv, dconv, dmeta, dnorm, dfinal, dgq, dgkv, dattn, dconvg, loss_part) = refs[2:2 + n_small]
        o = 2 + n_small
        g_out = {n: refs[o + i] for i, n in enumerate(names)}
        o += n_p
        loss_out = refs[o]
        ssmall, rin, rout, rsmall, gsum, send_sems, recv_sems, local_sems = refs[o + 1:]
        x, y, c, me = _device_position()

        ssmall[...] = jnp.zeros_like(ssmall)
        rep = ssmall.at[0]
        for i in range(D_MODEL // LANES):
            rep[ROW_NORM + i:ROW_NORM + i + 1, :] = dnorm[:, LANES * i:LANES * (i + 1)]
            rep[ROW_FINAL + i:ROW_FINAL + i + 1, :] = dfinal[:, LANES * i:LANES * (i + 1)]
        for i in range(Q_RANK // LANES):
            rep[ROW_GQ + i:ROW_GQ + i + 1, :] = dgq[:, LANES * i:LANES * (i + 1)]
        rep[ROW_GKV:ROW_GKV + 1, :] = dgkv[...]
        for i in range(CONV_WIDTH // LANES):
            rep[ROW_ATTN + i:ROW_ATTN + i + 1, :] = dattn[:, LANES * i:LANES * (i + 1)]
            rep[ROW_CONVG + i:ROW_CONVG + i + 1, :] = dconvg[:, LANES * i:LANES * (i + 1)]
        rep[ROW_LOSS:ROW_LOSS + 1, :] = loss_part[...]
        for k in range(N_DEV):
            if k:
                ssmall[k, ROW_REPL:, :] = ssmall[0, ROW_REPL:, :]
            for s, e, d in _q_pieces(k):
                ssmall[k, ROW_Q:ROW_Q + Q_RANK, s:e] = dwq[:, d:d + e - s]
            ssmall[k, ROW_KV:ROW_KV + KV_RANK, :] = dwkv[:, _kv_dst(k):_kv_dst(k) + SHARD_KV]
            ssmall[k, ROW_META:ROW_META + N_META, :] = dmeta[:, SHARD_META * k:SHARD_META * (k + 1)]
            ssmall[k, ROW_CONV:ROW_CONV + 3, 0:SHARD_CONV] = dconv[0:3, SHARD_CONV * k:SHARD_CONV * (k + 1)]

        copies = []
        for r in range(1, N_DEV):
            peer = (1 - x if r & 4 else x, 1 - y if r & 2 else y, 1 - c if r & 1 else c)
            peer_idx = 4 * peer[0] + 2 * peer[1] + peer[2]
            for j, (src, dst) in enumerate(((sin_hbm, rin), (sout_hbm, rout), (ssmall, rsmall))):
                copies.append(pltpu.make_async_remote_copy(
                    src_ref=src.at[peer_idx],
                    dst_ref=dst.at[r],
                    send_sem=send_sems.at[3 * (r - 1) + j],
                    recv_sem=recv_sems.at[3 * (r - 1) + j],
                    device_id=peer,
                    device_id_type=pl.DeviceIdType.MESH,
                ))
        for cp in copies:
            cp.start()
        own = [pltpu.make_async_copy(sin_hbm.at[me], rin.at[0], local_sems.at[0]),
               pltpu.make_async_copy(sout_hbm.at[me], rout.at[0], local_sems.at[1])]
        for cp in own:
            cp.start()
        rsmall[0] = ssmall[me]
        for cp in own:
            cp.wait()
        for cp in copies:
            cp.wait_recv()

        def apply(name, g, idx):
            g_out[name][idx] = g

        def chunk(i, carry):
            sl = pl.ds(pl.multiple_of(i * ADAM_CHUNK, ADAM_CHUNK), ADAM_CHUNK)
            g = rin[0, sl, :].astype(F32)
            for r in range(1, N_DEV):
                g = g + rin[r, sl, :].astype(F32)
            apply("w_in", g[:, :SHARD_IN], (sl, slice(None)))
            return carry

        lax.fori_loop(0, D_MODEL // ADAM_CHUNK, chunk, 0)

        g = rout[0].astype(F32)
        for r in range(1, N_DEV):
            g = g + rout[r].astype(F32)
        apply("w_out", g, (slice(None), slice(None)))

        gs = rsmall[me]
        for d in range(1, N_DEV):
            gs = gs + rsmall[d ^ me]
        gsum[...] = gs
        full = (slice(None), slice(None))
        apply("w_q_up", gsum[ROW_Q:ROW_Q + Q_RANK, 0:SHARD_Q], full)
        apply("w_kv_up", gsum[ROW_KV:ROW_KV + KV_RANK, :], full)
        apply("meta_tokens", gsum[ROW_META:ROW_META + N_META, :], full)
        apply("conv_w", gsum[ROW_CONV:ROW_CONV + 3, 0:SHARD_CONV], full)
        for name, row, width in (("norm_g", ROW_NORM, D_MODEL), ("final_norm_g", ROW_FINAL, D_MODEL),
                                 ("q_norm_g", ROW_GQ, Q_RANK), ("kv_norm_g", ROW_GKV, KV_RANK),
                                 ("attn_out_g", ROW_ATTN, CONV_WIDTH), ("conv_out_g", ROW_CONVG, CONV_WIDTH)):
            for i in range(width // LANES):
                apply(name, gsum[row + i:row + i + 1, :], (slice(None), slice(LANES * i, LANES * (i + 1))))
        loss_out[...] = gsum[ROW_LOSS:ROW_LOSS + 1, :]

        for cp in copies:
            cp.wait_send()

    vm = pl.BlockSpec(memory_space=pltpu.VMEM)
    hbm = pl.BlockSpec(memory_space=pl.ANY)
    out_shape = [jax.ShapeDtypeStruct(shape, F32) for _, shape in PARAM_SHAPES]
    out_shape.append(jax.ShapeDtypeStruct((1, LANES), F32))
    outs = pl.pallas_call(
        body,
        name="reduce_grads",
        out_shape=tuple(out_shape),
        in_specs=[hbm, hbm] + [vm] * n_small,
        out_specs=(vm,) * len(out_shape),
        scratch_shapes=[
            pltpu.VMEM((N_DEV, SMALL_ROWS, LANES), F32),
            pltpu.VMEM((N_DEV, D_MODEL, SHARD_IN_PAD), BF16),
            pltpu.VMEM((N_DEV, SHARD_OUT, D_MODEL), BF16),
            pltpu.VMEM((N_DEV, SMALL_ROWS, LANES), F32),
            pltpu.VMEM((SMALL_ROWS, LANES), F32),
            pltpu.SemaphoreType.DMA((3 * (N_DEV - 1),)),
            pltpu.SemaphoreType.DMA((3 * (N_DEV - 1),)),
            pltpu.SemaphoreType.DMA((2,)),
        ],
        compiler_params=pltpu.CompilerParams(vmem_limit_bytes=VMEM_LIMIT),
    )(send_in, send_out, *small_grads)
    return {n: outs[i] for i, n in enumerate(names)}, outs[-1]


def _prep(x, meta, norm_g):
    nb_seq, s, d = x.shape
    nb = s // LANES + 1

    def body(x_ref, meta_ref, g_ref, u_ref):
        j = pl.program_id(1)

        def norm(h):
            hhat, _ = _rms_stats(h)
            return (hhat * g_ref[...]).astype(BF16)

        @pl.when(j == 0)
        def _():
            u_ref[0:PAD_FRONT, :] = jnp.zeros((PAD_FRONT, d), BF16)
            u_ref[PAD_FRONT:LANES, :] = norm(meta_ref[...])

        @pl.when(j > 0)
        def _():
            u_ref[...] = norm(x_ref[0])

    return pl.pallas_call(
        body,
        name="prep_norm",
        grid=(nb_seq, nb),
        in_specs=[
            pl.BlockSpec((1, LANES, d), lambda b, j: (b, jnp.maximum(j - 1, 0), 0)),
            pl.BlockSpec((N_META, d), lambda b, j: (0, 0)),
            pl.BlockSpec((1, d), lambda b, j: (0, 0)),
        ],
        out_specs=pl.BlockSpec((LANES, d), lambda b, j: (b * nb + j, 0)),
        out_shape=jax.ShapeDtypeStruct((nb_seq * nb * LANES, d), BF16),
        compiler_params=_params("parallel", "arbitrary"),
    )(x, meta, norm_g)


def _matmul(a, b, bm, bn, out_dtype, name):
    m, k = a.shape
    _, n = b.shape

    def body(a_ref, b_ref, o_ref):
        o_ref[...] = _dot(a_ref[...], b_ref[...]).astype(out_dtype)

    return pl.pallas_call(
        body,
        name=name,
        grid=(m // bm, n // bn),
        in_specs=[pl.BlockSpec((bm, k), lambda i, j: (i, 0)), pl.BlockSpec((k, bn), lambda i, j: (0, j))],
        out_specs=pl.BlockSpec((bm, bn), lambda i, j: (i, j)),
        out_shape=jax.ShapeDtypeStruct((m, n), out_dtype),
        compiler_params=_params("parallel", "arbitrary"),
    )(a, b)


def _rope_tables(tp):
    half = D_ROPE // 2
    inv_freq = 1.0 / (ROPE_THETA ** (jnp.arange(half, dtype=F32) / half))
    pos = (jnp.arange(tp) - PAD_FRONT).astype(F32)
    ang = pos[:, None] * inv_freq[None, :]
    cos = jnp.tile(jnp.cos(ang), (1, LANES // half))
    sin = jnp.tile(jnp.sin(ang), (1, LANES // half))
    first = (jnp.arange(LANES) % D_ROPE) < half
    return cos, jnp.where(first, -sin, 0.0), jnp.where(first, 0.0, sin)


def _rope(t, cos, sa, sb):
    return t * cos + pltpu.roll(t, LANES - D_ROPE // 2, 1) * sa + pltpu.roll(t, D_ROPE // 2, 1) * sb


def _rope_t(t, cos, sa, sb):
    return t * cos + pltpu.roll(t * sa, D_ROPE // 2, 1) + pltpu.roll(t * sb, LANES - D_ROPE // 2, 1)


def _qkv_fwd(p, wq, wkv, gq, gkv, tables, nb_seq, tp):
    ht = tp // 2

    def body(pa_ref, wq_ref, wkv_ref, gq_ref, gkv_ref, cos_ref, sa_ref, sb_ref, q_ref, k_ref, v_ref):
        pa = pa_ref[...].astype(F32)
        cq_hat, _ = _rms_stats(pa[:, :Q_RANK])
        ckv_hat, _ = _rms_stats(pa[:, Q_RANK:Q_RANK + KV_RANK])
        q = _dot((cq_hat * gq_ref[...]).astype(BF16), wq_ref[...])
        kv = _dot((ckv_hat * gkv_ref[...]).astype(BF16), wkv_ref[...])
        tabs = (cos_ref[...], sa_ref[...], sb_ref[...])
        lane = lax.broadcasted_iota(jnp.int32, (ht, LANES), 1)
        low = lane < D_ROPE
        mark = lane == D_ROPE
        row = (pl.program_id(0) % 2) * ht + lax.broadcasted_iota(jnp.int32, (ht, LANES), 0)
        k_pe = jnp.where(mark & (row < PAD_FRONT), NEG_INF, _rope(pa[:, Q_RANK + KV_RANK:], *tabs))
        one = jnp.where(mark & (row >= PAD_FRONT), 1.0, 0.0)
        pairs = [_rope(q[:, N_HEADS * D_NOPE + LANES * i:N_HEADS * D_NOPE + LANES * (i + 1)], *tabs) for i in range(2)]
        for h in range(N_HEADS):
            pair = pairs[h // 2]
            if h % 2:
                pair = pltpu.roll(pair, D_ROPE, 1)
            pe = jnp.where(low, pair, one)
            q_ref[0, h] = jnp.concatenate([q[:, D_NOPE * h:D_NOPE * (h + 1)], pe], axis=1).astype(BF16)
            k_ref[0, h] = jnp.concatenate([kv[:, D_NOPE * h:D_NOPE * (h + 1)], k_pe], axis=1).astype(BF16)
            v_ref[0, h] = kv[:, N_HEADS * D_NOPE + D_V * h:N_HEADS * D_NOPE + D_V * (h + 1)].astype(BF16)

    full = lambda a: pl.BlockSpec(a.shape, lambda i: (0,) * a.ndim)
    tab = pl.BlockSpec((ht, LANES), lambda i: (i % 2, 0))
    qk = pl.BlockSpec((1, N_HEADS, ht, 2 * LANES), lambda i: (i // 2, 0, i % 2, 0))
    return pl.pallas_call(
        body,
        name="qkv_fwd",
        grid=(2 * nb_seq,),
        in_specs=[pl.BlockSpec((ht, GRP_A), lambda i: (i, 0)), full(wq), full(wkv), full(gq), full(gkv), tab, tab, tab],
        out_specs=(qk, qk, pl.BlockSpec((1, N_HEADS, ht, D_V), lambda i: (i // 2, 0, i % 2, 0))),
        out_shape=(
            jax.ShapeDtypeStruct((nb_seq, N_HEADS, tp, 2 * LANES), BF16),
            jax.ShapeDtypeStruct((nb_seq, N_HEADS, tp, 2 * LANES), BF16),
            jax.ShapeDtypeStruct((nb_seq, N_HEADS, tp, D_V), BF16),
        ),
        compiler_params=_params("parallel"),
    )(p, wq, wkv, gq, gkv, *tables)


def _attn_fwd(q, k, v, p, g_attn):
    nb_seq, _, tp, _ = q.shape

    def body(q_ref, k_ref, v_ref, z_ref, g_ref, y_ref, o_ref, lse_ref):
        g = g_ref[...]
        for r0 in range(0, tp, KV_TILE):
            nq = min(KV_TILE, tp - r0)
            kend = r0 + nq
            qq = q_ref[0, 0, r0:kend, :]
            sd = _dot(qq, k_ref[0, 0, r0:kend, :], _NT) * ATTN_SCALE
            causal = (lax.broadcasted_iota(jnp.int32, (nq, nq), 1) <= lax.broadcasted_iota(jnp.int32, (nq, nq), 0))
            sd = jnp.where(causal, sd, NEG_INF)
            m = jnp.max(sd, axis=-1, keepdims=True)
            if r0:
                so = _dot(qq, k_ref[0, 0, 0:r0, :], _NT) * ATTN_SCALE
                m = jnp.maximum(m, jnp.max(so, axis=-1, keepdims=True))
            ed = jnp.exp(sd - m)
            l = jnp.sum(ed, axis=-1, keepdims=True)
            o = _dot(ed.astype(BF16), v_ref[0, 0, r0:kend, :])
            if r0:
                eo = jnp.exp(so - m)
                l = l + jnp.sum(eo, axis=-1, keepdims=True)
                o = o + _dot(eo.astype(BF16), v_ref[0, 0, 0:r0, :])
            o = o * (1.0 / l)
            o_ref[0, 0, r0:kend, :] = o
            lse_ref[0, 0, r0:kend, :] = jnp.broadcast_to(m + jnp.log(l), (nq, LANES))
            ohat, _ = _rms_stats(o)
            z = z_ref[r0:kend, :].astype(F32)
            y_ref[r0:kend, :] = (ohat * g * (z * _sigmoid(z))).astype(BF16)

    qk = pl.BlockSpec((1, 1, tp, 2 * LANES), lambda b, h: (b, h, 0, 0))
    hv = pl.BlockSpec((1, 1, tp, D_V), lambda b, h: (b, h, 0, 0))
    return pl.pallas_call(
        body,
        name="attn_fwd",
        grid=(nb_seq, N_HEADS),
        in_specs=[qk, qk, hv,
                  pl.BlockSpec((tp, LANES), lambda b, h: (b, GRP_A // LANES + h)),
                  pl.BlockSpec((1, LANES), lambda b, h: (0, h))],
        out_specs=(pl.BlockSpec((tp, LANES), lambda b, h: (b, h)), hv, hv),
        out_shape=(
            jax.ShapeDtypeStruct((nb_seq * tp, N_HEADS * D_V), BF16),
            jax.ShapeDtypeStruct((nb_seq, N_HEADS, tp, D_V), F32),
            jax.ShapeDtypeStruct((nb_seq, N_HEADS, tp, LANES), F32),
        ),
        compiler_params=_params("parallel", "parallel"),
    )(q, k, v, p, g_attn)


_CONV_COL0 = (GRP_A + N_HEADS * D_V) // LANES


def _conv_specs(tp, order):
    cols = CONV_WIDTH // LANES
    return [pl.BlockSpec((tp, LANES), functools.partial(
        lambda a, b, off: order(a, b, off), off=_CONV_COL0 + i * cols)) for i in range(4)]


def _conv_fwd(p, conv_w, g_conv, nb_seq, tp):
    def body(b_ref, c_ref, h_ref, z_ref, w_ref, g_ref, y_ref):
        cc = c_ref[...].astype(F32) * h_ref[...].astype(F32)
        row = lax.broadcasted_iota(jnp.int32, (tp, LANES), 0)
        s1 = jnp.where(row >= 1, pltpu.roll(cc, 1, 0), 0.0)
        s2 = jnp.where(row >= 2, pltpu.roll(cc, 2, 0), 0.0)
        yc = b_ref[...].astype(F32) * (w_ref[0:1, :] * s2 + w_ref[1:2, :] * s1 + w_ref[2:3, :] * cc)
        r = lax.rsqrt(_group_mean(yc * yc) + EPS)
        z = z_ref[...].astype(F32)
        y_ref[...] = (yc * r * g_ref[...] * (z * _sigmoid(z))).astype(BF16)

    return pl.pallas_call(
        body,
        name="conv_fwd",
        grid=(nb_seq, CONV_WIDTH // LANES),
        in_specs=_conv_specs(tp, lambda b, t, off: (b, off + t)) + [
            pl.BlockSpec((8, LANES), lambda b, t: (0, t)),
            pl.BlockSpec((1, LANES), lambda b, t: (0, t))],
        out_specs=pl.BlockSpec((tp, LANES), lambda b, t: (b, t)),
        out_shape=jax.ShapeDtypeStruct((nb_seq * tp, CONV_WIDTH), BF16),
        compiler_params=_params("parallel", "parallel"),
    )(p, p, p, p, conv_w, g_conv)


def _out_proj(ya, yc, w_out, bm):
    r, ka = ya.shape
    d = w_out.shape[1]

    def body(a_ref, c_ref, w_ref, o_ref):
        o_ref[...] = _dot(a_ref[...], w_ref[0:ka, :]) + _dot(c_ref[...], w_ref[ka:, :])

    return pl.pallas_call(
        body,
        name="out_proj",
        grid=(r // bm,),
        in_specs=[pl.BlockSpec((bm, ka), lambda i: (i, 0)), pl.BlockSpec((bm, yc.shape[1]), lambda i: (i, 0)),
                  pl.BlockSpec(w_out.shape, lambda i: (0, 0))],
        out_specs=pl.BlockSpec((bm, d), lambda i: (i, 0)),
        out_shape=jax.ShapeDtypeStruct((r, d), F32),
        compiler_params=_params("parallel"),
    )(ya, yc, w_out)


def _loss_bwd(x, mix, target, g_final):
    nb_seq, s, d = x.shape
    nb = s // LANES + 1

    def body(x_ref, mix_ref, t_ref, g_ref, dh_ref, dhb_ref, dg_ref, loss_ref, acc_ref):
        b, j = pl.program_id(0), pl.program_id(1)

        @pl.when((b == 0) & (j == 0))
        def _():
            acc_ref[...] = jnp.zeros_like(acc_ref)
            dg_ref[...] = jnp.zeros_like(dg_ref)

        @pl.when(j == 0)
        def _():
            dh_ref[...] = jnp.zeros_like(dh_ref)
            dhb_ref[...] = jnp.zeros_like(dhb_ref)

        @pl.when(j > 0)
        def _():
            g = g_ref[...]
            hhat, r = _rms_stats(x_ref[0] + mix_ref[...])
            e = hhat * g - t_ref[0]
            acc_ref[...] += jnp.sum(e * e, axis=0, keepdims=True)
            dy = e * (1.0 / d)
            dg_ref[...] += jnp.sum(dy * hhat, axis=0, keepdims=True)
            dh = _rms_bwd(g * dy, hhat, r)
            dh_ref[...] = dh
            dhb_ref[...] = dh.astype(BF16)

        @pl.when((b == nb_seq - 1) & (j == nb - 1))
        def _():
            total = jnp.sum(acc_ref[...], axis=1, keepdims=True)
            loss_ref[...] = jnp.broadcast_to((0.5 / d) * total, loss_ref.shape)

    tok = pl.BlockSpec((1, LANES, d), lambda b, j: (b, jnp.maximum(j - 1, 0), 0))
    row = pl.BlockSpec((LANES, d), lambda b, j: (b * nb + j, 0))
    vec = pl.BlockSpec((1, d), lambda b, j: (0, 0))
    return pl.pallas_call(
        body,
        name="loss_bwd",
        grid=(nb_seq, nb),
        in_specs=[tok, row, tok, vec],
        out_specs=(row, row, vec, pl.BlockSpec((1, LANES), lambda b, j: (0, 0))),
        out_shape=(
            jax.ShapeDtypeStruct((nb_seq * nb * LANES, d), F32),
            jax.ShapeDtypeStruct((nb_seq * nb * LANES, d), BF16),
            jax.ShapeDtypeStruct((1, d), F32),
            jax.ShapeDtypeStruct((1, LANES), F32),
        ),
        scratch_shapes=[pltpu.VMEM((1, d), F32)],
        compiler_params=_params("arbitrary", "arbitrary"),
    )(x, mix, target, g_final)


def _out_proj_bwd(dhb, w_out, ya, yc, bm):
    r, d = dhb.shape
    ka = ya.shape[1]
    n_mix = w_out.shape[0]
    last = r // bm - 1

    def body(dh_ref, w_ref, a_ref, c_ref, dcat_ref, dw_ref, acc_ref):
        @pl.when(pl.program_id(0) == 0)
        def _():
            acc_ref[...] = jnp.zeros_like(acc_ref)

        dh = dh_ref[...]
        dcat_ref[...] = _dot(dh, w_ref[...], _NT).astype(BF16)
        acc_ref[0:ka, :] += _dot(a_ref[...], dh, _TN)
        acc_ref[ka:, :] += _dot(c_ref[...], dh, _TN)

        @pl.when(pl.program_id(0) == last)
        def _():
            dw_ref[...] = acc_ref[...].astype(BF16)

    return pl.pallas_call(
        body,
        name="out_proj_bwd",
        grid=(r // bm,),
        in_specs=[pl.BlockSpec((bm, d), lambda i: (i, 0)), pl.BlockSpec(w_out.shape, lambda i: (0, 0)),
                  pl.BlockSpec((bm, ka), lambda i: (i, 0)), pl.BlockSpec((bm, yc.shape[1]), lambda i: (i, 0))],
        out_specs=(pl.BlockSpec((bm, n_mix), lambda i: (i, 0)),
                   pl.BlockSpec((n_mix, d), lambda i: (0, 0))),
        out_shape=(jax.ShapeDtypeStruct((r, n_mix), BF16),
                   jax.ShapeDtypeStruct((n_mix, d), BF16)),
        scratch_shapes=[pltpu.VMEM((n_mix, d), F32)],
        compiler_params=_params("arbitrary"),
    )(dhb, w_out, ya, yc)


def _attn_bwd(q, k, v, o, lse, dcat, p, g_attn):
    nb_seq, _, tp, _ = q.shape

    def body(q_ref, k_ref, v_ref, o_ref, lse_ref, dy_ref, z_ref, g_ref,
             dq_ref, dk_ref, dv_ref, dz_ref, dg_ref, dq_acc):
        @pl.when(pl.program_id(1) == 0)
        def _():
            dg_ref[...] = jnp.zeros_like(dg_ref)

        g = g_ref[...]
        z = z_ref[...].astype(F32)
        o = o_ref[0, 0]
        dy = dy_ref[...].astype(F32)
        sig = _sigmoid(z)
        ohat, r = _rms_stats(o)
        don = dy * (z * sig)
        dz_ref[...] = (dy * (ohat * g) * (sig * (1.0 + z * (1.0 - sig)))).astype(BF16)
        dg_ref[...] += jnp.sum(don * ohat, axis=0, keepdims=True)
        do = _rms_bwd(g * don, ohat, r)
        dvec = jnp.sum(do * o, axis=-1, keepdims=True)
        dob = do.astype(BF16)
        lse_col = lse_ref[0, 0, :, 0:1]
        dq_acc[...] = jnp.zeros_like(dq_acc)
        for k0 in range(0, tp, KV_TILE):
            nk = min(KV_TILE, tp - k0)
            kk = k_ref[0, 0, k0:k0 + nk, :]
            vv = v_ref[0, 0, k0:k0 + nk, :]
            parts = [(k0, k0 + nk, True)] + ([(k0 + nk, tp, False)] if k0 + nk < tp else [])
            dk_t, dv_t = None, None
            for q0, q1, diagonal in parts:
                qq = q_ref[0, 0, q0:q1, :]
                pr = jnp.exp(_dot(qq, kk, _NT) * ATTN_SCALE - lse_col[q0:q1])
                if diagonal:
                    causal = (lax.broadcasted_iota(jnp.int32, (nk, nk), 1)
                              <= lax.broadcasted_iota(jnp.int32, (nk, nk), 0))
                    pr = jnp.where(causal, pr, 0.0)
                dp = _dot(dob[q0:q1], vv, _NT)
                ds = (pr * (dp - dvec[q0:q1]) * ATTN_SCALE).astype(BF16)
                dv_p = _dot(pr.astype(BF16), dob[q0:q1], _TN)
                dk_p = _dot(ds, qq, _TN)
                dv_t = dv_p if dv_t is None else dv_t + dv_p
                dk_t = dk_p if dk_t is None else dk_t + dk_p
                dq_acc[q0:q1, :] += _dot(ds, kk)
            dv_ref[0, 0, k0:k0 + nk, :] = dv_t.astype(BF16)
            dk_ref[0, 0, k0:k0 + nk, :] = dk_t.astype(BF16)
        dq_ref[0, 0] = dq_acc[...].astype(BF16)

    qk = pl.BlockSpec((1, 1, tp, 2 * LANES), lambda h, b: (b, h, 0, 0))
    hv = pl.BlockSpec((1, 1, tp, D_V), lambda h, b: (b, h, 0, 0))
    col = pl.BlockSpec((tp, LANES), lambda h, b: (b, h))
    return pl.pallas_call(
        body,
        name="attn_bwd",
        grid=(N_HEADS, nb_seq),
        in_specs=[qk, qk, hv, hv, hv, col,
                  pl.BlockSpec((tp, LANES), lambda h, b: (b, GRP_A // LANES + h)),
                  pl.BlockSpec((1, LANES), lambda h, b: (0, h))],
        out_specs=(qk, qk, hv, col, pl.BlockSpec((1, LANES), lambda h, b: (0, h))),
        out_shape=(
            jax.ShapeDtypeStruct((nb_seq, N_HEADS, tp, 2 * LANES), BF16),
            jax.ShapeDtypeStruct((nb_seq, N_HEADS, tp, 2 * LANES), BF16),
            jax.ShapeDtypeStruct((nb_seq, N_HEADS, tp, D_V), BF16),
            jax.ShapeDtypeStruct((nb_seq * tp, N_HEADS * D_V), BF16),
            jax.ShapeDtypeStruct((1, N_HEADS * D_V), F32),
        ),
        scratch_shapes=[pltpu.VMEM((tp, 2 * LANES), F32)],
        compiler_params=_params("arbitrary", "arbitrary"),
    )(q, k, v, o, lse, dcat, p, g_attn)


def _qkv_bwd(p, dq, dk, dv, wq, wkv, gq, gkv, tables):
    nb_seq, _, tp, _ = dq.shape
    ht = tp // 2

    def body(pa_ref, dq_ref, dk_ref, dv_ref, wq_ref, wkv_ref, gq_ref, gkv_ref, cos_ref, sa_ref, sb_ref,
             dpa_ref, dwq_ref, dwkv_ref, dgq_ref, dgkv_ref):
        @pl.when(pl.program_id(0) == 0)
        def _():
            dwq_ref[...] = jnp.zeros_like(dwq_ref)
            dwkv_ref[...] = jnp.zeros_like(dwkv_ref)
            dgq_ref[...] = jnp.zeros_like(dgq_ref)
            dgkv_ref[...] = jnp.zeros_like(dgkv_ref)

        pa = pa_ref[...].astype(F32)
        gq, gkv = gq_ref[...], gkv_ref[...]
        cq_hat, rq = _rms_stats(pa[:, :Q_RANK])
        ckv_hat, rkv = _rms_stats(pa[:, Q_RANK:Q_RANK + KV_RANK])
        tabs = (cos_ref[...], sa_ref[...], sb_ref[...])

        pe = [dq_ref[0, h, :, D_NOPE:].astype(F32) for h in range(N_HEADS)]
        pairs = [_rope_t(pe[2 * i] + pltpu.roll(pe[2 * i + 1], D_ROPE, 1), *tabs).astype(BF16) for i in range(2)]
        dq_flat = jnp.concatenate([dq_ref[0, h, :, :D_NOPE] for h in range(N_HEADS)] + pairs, axis=1)
        dwq_ref[...] += _dot((cq_hat * gq).astype(BF16), dq_flat, _TN)
        dcqn = _dot(dq_flat, wq_ref[...], _NT)
        dgq_ref[...] += jnp.sum(dcqn * cq_hat, axis=0, keepdims=True)
        dcq = _rms_bwd(gq * dcqn, cq_hat, rq)

        dkv_flat = jnp.concatenate([dk_ref[0, h, :, :D_NOPE] for h in range(N_HEADS)]
                                   + [dv_ref[0, h] for h in range(N_HEADS)], axis=1)
        dwkv_ref[...] += _dot((ckv_hat * gkv).astype(BF16), dkv_flat, _TN)
        dckvn = _dot(dkv_flat, wkv_ref[...], _NT)
        dgkv_ref[...] += jnp.sum(dckvn * ckv_hat, axis=0, keepdims=True)
        dckv = _rms_bwd(gkv * dckvn, ckv_hat, rkv)

        dk_pe = dk_ref[0, 0, :, D_NOPE:].astype(F32)
        for h in range(1, N_HEADS):
            dk_pe = dk_pe + dk_ref[0, h, :, D_NOPE:].astype(F32)
        dk_pe = jnp.where(lax.broadcasted_iota(jnp.int32, (ht, LANES), 1) < D_ROPE, dk_pe, 0.0)
        dpa_ref[...] = jnp.concatenate([dcq, dckv, _rope_t(dk_pe, *tabs)], axis=1).astype(BF16)

    full = lambda a: pl.BlockSpec(a.shape, lambda i: (0,) * a.ndim)
    tab = pl.BlockSpec((ht, LANES), lambda i: (i % 2, 0))
    qk = pl.BlockSpec((1, N_HEADS, ht, 2 * LANES), lambda i: (i // 2, 0, i % 2, 0))
    acc = lambda shape: pl.BlockSpec(shape, lambda i: (0, 0))
    return pl.pallas_call(
        body,
        name="qkv_bwd",
        grid=(2 * nb_seq,),
        in_specs=[pl.BlockSpec((ht, GRP_A), lambda i: (i, 0)), qk, qk,
                  pl.BlockSpec((1, N_HEADS, ht, D_V), lambda i: (i // 2, 0, i % 2, 0)),
                  full(wq), full(wkv), full(gq), full(gkv), tab, tab, tab],
        out_specs=(pl.BlockSpec((ht, GRP_A), lambda i: (i, 0)),
                   acc(wq.shape), acc(wkv.shape), acc((1, Q_RANK)), acc((1, KV_RANK))),
        out_shape=(
            jax.ShapeDtypeStruct((nb_seq * tp, GRP_A), BF16),
            jax.ShapeDtypeStruct(wq.shape, F32),
            jax.ShapeDtypeStruct(wkv.shape, F32),
            jax.ShapeDtypeStruct((1, Q_RANK), F32),
            jax.ShapeDtypeStruct((1, KV_RANK), F32),
        ),
        compiler_params=_params("arbitrary"),
    )(p, dq, dk, dv, wq, wkv, gq, gkv, *tables)


def _conv_bwd(p, dcat, conv_w, g_conv, nb_seq, tp):
    cols = CONV_WIDTH // LANES

    def body(b_ref, c_ref, h_ref, z_ref, dy_ref, w_ref, g_ref,
             db_ref, dc_ref, dh_ref, dz_ref, dw_ref, dg_ref):
        @pl.when(pl.program_id(1) == 0)
        def _():
            dw_ref[...] = jnp.zeros_like(dw_ref)
            dg_ref[...] = jnp.zeros_like(dg_ref)

        cb, c, h = b_ref[...].astype(F32), c_ref[...].astype(F32), h_ref[...].astype(F32)
        z, dy = z_ref[...].astype(F32), dy_ref[...].astype(F32)
        g = g_ref[...]
        w0, w1, w2 = w_ref[0:1, :], w_ref[1:2, :], w_ref[2:3, :]
        cc = c * h
        row = lax.broadcasted_iota(jnp.int32, (tp, LANES), 0)
        s1 = jnp.where(row >= 1, pltpu.roll(cc, 1, 0), 0.0)
        s2 = jnp.where(row >= 2, pltpu.roll(cc, 2, 0), 0.0)
        dwc = w0 * s2 + w1 * s1 + w2 * cc
        yc = cb * dwc
        r = lax.rsqrt(_group_mean(yc * yc) + EPS)
        ychat = yc * r
        sig = _sigmoid(z)
        dz_ref[...] = (dy * (ychat * g) * (sig * (1.0 + z * (1.0 - sig)))).astype(BF16)
        dyn = dy * (z * sig)
        dg_ref[...] += jnp.sum(dyn * ychat, axis=0, keepdims=True)
        gd = g * dyn
        dyc = r * (gd - ychat * _group_mean(gd * ychat))
        db_ref[...] = (dyc * dwc).astype(BF16)
        ddw = dyc * cb
        dw_ref[0:1, :] += jnp.sum(ddw * s2, axis=0, keepdims=True)
        dw_ref[1:2, :] += jnp.sum(ddw * s1, axis=0, keepdims=True)
        dw_ref[2:3, :] += jnp.sum(ddw * cc, axis=0, keepdims=True)
        u1 = jnp.where(row <= tp - 2, pltpu.roll(ddw, tp - 1, 0), 0.0)
        u2 = jnp.where(row <= tp - 3, pltpu.roll(ddw, tp - 2, 0), 0.0)
        dcc = w2 * ddw + w1 * u1 + w0 * u2
        dc_ref[...] = (dcc * h).astype(BF16)
        dh_ref[...] = (dcc * c).astype(BF16)

    col = pl.BlockSpec((tp, LANES), lambda t, b: (b, t))
    out = jax.ShapeDtypeStruct((nb_seq * tp, CONV_WIDTH), BF16)
    return pl.pallas_call(
        body,
        name="conv_bwd",
        grid=(cols, nb_seq),
        in_specs=_conv_specs(tp, lambda t, b, off: (b, off + t)) + [
            pl.BlockSpec((tp, LANES), lambda t, b: (b, N_HEADS * D_V // LANES + t)),
            pl.BlockSpec((8, LANES), lambda t, b: (0, t)),
            pl.BlockSpec((1, LANES), lambda t, b: (0, t))],
        out_specs=(col, col, col, col,
                   pl.BlockSpec((8, LANES), lambda t, b: (0, t)), pl.BlockSpec((1, LANES), lambda t, b: (0, t))),
        out_shape=(out, out, out, out,
                   jax.ShapeDtypeStruct((8, CONV_WIDTH), F32), jax.ShapeDtypeStruct((1, CONV_WIDTH), F32)),
        compiler_params=_params("arbitrary", "arbitrary"),
    )(p, p, p, p, dcat, conv_w, g_conv)


def _in_proj_bwd_x(dps, w_in, bm):
    r, kb = dps[0].shape
    d = w_in.shape[0]

    def body(*refs):
        dp_refs, w_ref, o_ref = refs[:len(dps)], refs[len(dps)], refs[len(dps) + 1]
        acc = _dot(dp_refs[0][...], w_ref[:, 0:kb], _NT)
        for j in range(1, len(dps)):
            acc = acc + _dot(dp_refs[j][...], w_ref[:, kb * j:kb * (j + 1)], _NT)
        o_ref[...] = acc

    return pl.pallas_call(
        body,
        name="in_proj_bwd_x",
        grid=(r // bm,),
        in_specs=[pl.BlockSpec((bm, kb), lambda i: (i, 0)) for _ in dps] + [pl.BlockSpec(w_in.shape, lambda i: (0, 0))],
        out_specs=pl.BlockSpec((bm, d), lambda i: (i, 0)),
        out_shape=jax.ShapeDtypeStruct((r, d), F32),
        compiler_params=_params("parallel"),
    )(*dps, w_in)


def _in_proj_bwd_w(u, dps, bm):
    r, d = u.shape
    kb = dps[0].shape[1]
    last = r // bm - 1

    def body(*refs):
        u_ref, dp_refs, o_ref, acc_ref = refs[0], refs[1:1 + len(dps)], refs[1 + len(dps)], refs[2 + len(dps)]

        @pl.when(pl.program_id(0) == 0)
        def _():
            acc_ref[...] = jnp.zeros_like(acc_ref)

        uu = u_ref[...]
        for j in range(len(dps)):
            acc_ref[:, kb * j:kb * (j + 1)] += _dot(uu, dp_refs[j][...], _TN)

        @pl.when(pl.program_id(0) == last)
        def _():
            for k in range(N_DEV):
                for s, e, c0 in _in_pieces(k):
                    o_ref[k, :, s:e] = acc_ref[:, c0:c0 + e - s].astype(BF16)
                o_ref[k, :, SHARD_IN:] = jnp.zeros((d, SHARD_IN_PAD - SHARD_IN), BF16)

    return pl.pallas_call(
        body,
        name="in_proj_bwd_w",
        grid=(r // bm,),
        in_specs=[pl.BlockSpec((bm, d), lambda i: (i, 0))] + [pl.BlockSpec((bm, kb), lambda i: (i, 0)) for _ in dps],
        out_specs=pl.BlockSpec((N_DEV, d, SHARD_IN_PAD), lambda i: (0, 0, 0)),
        out_shape=jax.ShapeDtypeStruct((N_DEV, d, SHARD_IN_PAD), BF16),
        scratch_shapes=[pltpu.VMEM((d, kb * len(dps)), F32)],
        compiler_params=_params("arbitrary"),
    )(u, *dps)


def _input_bwd(x, meta, du, dh, norm_g):
    nb_seq, s, d = x.shape
    nb = s // LANES + 1

    def body(x_ref, meta_ref, du_ref, dh_ref, g_ref, gx_ref, dmeta_ref, dg_ref):
        b, j = pl.program_id(0), pl.program_id(1)
        g = g_ref[...]

        @pl.when((b == 0) & (j == 0))
        def _():
            dmeta_ref[...] = jnp.zeros_like(dmeta_ref)
            dg_ref[...] = jnp.zeros_like(dg_ref)

        def bwd(h0, du_, dh_):
            hhat, r = _rms_stats(h0)
            dg_ref[...] += jnp.sum(du_ * hhat, axis=0, keepdims=True)
            return _rms_bwd(g * du_, hhat, r) + dh_

        @pl.when(j == 0)
        def _():
            dmeta_ref[...] += bwd(meta_ref[...], du_ref[PAD_FRONT:LANES, :], dh_ref[PAD_FRONT:LANES, :])

        @pl.when(j > 0)
        def _():
            gx_ref[0] = bwd(x_ref[0], du_ref[...], dh_ref[...])

    tok = pl.BlockSpec((1, LANES, d), lambda b, j: (b, jnp.maximum(j - 1, 0), 0))
    row = pl.BlockSpec((LANES, d), lambda b, j: (b * nb + j, 0))
    vec = pl.BlockSpec((1, d), lambda b, j: (0, 0))
    return pl.pallas_call(
        body,
        name="input_bwd",
        grid=(nb_seq, nb),
        in_specs=[tok, pl.BlockSpec((N_META, d), lambda b, j: (0, 0)), row, row, vec],
        out_specs=(tok, pl.BlockSpec((N_META, d), lambda b, j: (0, 0)), vec),
        out_shape=(
            jax.ShapeDtypeStruct((nb_seq, s, d), F32),
            jax.ShapeDtypeStruct((N_META, d), F32),
            jax.ShapeDtypeStruct((1, d), F32),
        ),
        compiler_params=_params("arbitrary", "arbitrary"),
    )(x, meta, du, dh, norm_g)


def _local_step(x, loss_target, meta_f, norm_g, w_in_p, q_norm_g, w_q_p, kv_norm_g, w_kv_p, conv_w_f,
                attn_out_g, conv_out_g, w_out_f, g_final):
    nb_seq, s, d = x.shape
    tp = s + LANES
    ht = tp // 2
    tables = _rope_tables(tp)

    u = _prep(x, meta_f, norm_g)
    p = _matmul(u, w_in_p, ht, GRP_A, BF16, "in_proj")
    q, k, v = _qkv_fwd(p, w_q_p, w_kv_p, q_norm_g, kv_norm_g, tables, nb_seq, tp)
    ya, o, lse = _attn_fwd(q, k, v, p, attn_out_g)
    yc = _conv_fwd(p, conv_w_f, conv_out_g, nb_seq, tp)
    mix = _out_proj(ya, yc, w_out_f, ht)
    dh, dhb, d_final_g, loss_part = _loss_bwd(x, mix, loss_target, g_final)

    dcat, d_w_out = _out_proj_bwd(dhb, w_out_f, ya, yc, ht)
    dq, dk, dv, dz_attn, d_attn_g = _attn_bwd(q, k, v, o, lse, dcat, p, attn_out_g)
    dpa, d_wq_p, d_wkv_p, d_gq, d_gkv = _qkv_bwd(p, dq, dk, dv, w_q_p, w_kv_p, q_norm_g, kv_norm_g, tables)
    d_b, d_c, d_h, dz_conv, d_conv_w, d_conv_g = _conv_bwd(p, dcat, conv_w_f, conv_out_g, nb_seq, tp)
    dps = (dpa, dz_attn, d_b, d_c, d_h, dz_conv)
    du = _in_proj_bwd_x(dps, w_in_p, ht)
    send_in = _in_proj_bwd_w(u, dps, ht // 2)
    grad_x, d_meta, d_norm_g = _input_bwd(x, meta_f, du, dh, norm_g)
    send_out = d_w_out.reshape(N_DEV, SHARD_OUT, d)
    small = (d_wq_p, d_wkv_p, d_conv_w, d_meta, d_norm_g, d_final_g, d_gq, d_gkv, d_attn_g, d_conv_g, loss_part)
    return grad_x, send_in, send_out, small


def kernel(x, meta_tokens, norm_g, w_in, q_norm_g, w_q_up, kv_norm_g, w_kv_up, conv_w, attn_out_g, conv_out_g, w_out, final_norm_g, loss_target, m_meta_tokens, m_norm_g, m_w_in, m_q_norm_g, m_w_q_up, m_kv_norm_g, m_w_kv_up, m_conv_w, m_attn_out_g, m_conv_out_g, m_w_out, m_final_norm_g, v_meta_tokens, v_norm_g, v_w_in, v_q_norm_g, v_w_q_up, v_kv_norm_g, v_w_kv_up, v_conv_w, v_attn_out_g, v_conv_out_g, v_w_out, v_final_norm_g):
    d = x.shape[-1]
    w_in_p, w_q_p, w_kv_p, w_out_f, conv_w_f, meta_f = _gather_weights(
        w_in[0], w_q_up[0], w_kv_up[0], w_out[0], conv_w[0], meta_tokens)
    g_final = final_norm_g.reshape(1, d)
    grad_x, send_in, send_out, small = _local_step(
        x, loss_target, meta_f, norm_g, w_in_p, q_norm_g, w_q_p, kv_norm_g, w_kv_p, conv_w_f,
        attn_out_g, conv_out_g, w_out_f, g_final)

    flat = lambda a: a.reshape(a.shape[-2:]) if a.ndim == 3 else a.reshape(1, -1) if a.ndim == 1 else a
    params = {
        "meta_tokens": (meta_tokens, m_meta_tokens, v_meta_tokens),
        "norm_g": (norm_g, m_norm_g, v_norm_g),
        "w_in": (w_in, m_w_in, v_w_in),
        "q_norm_g": (q_norm_g, m_q_norm_g, v_q_norm_g),
        "w_q_up": (w_q_up, m_w_q_up, v_w_q_up),
        "kv_norm_g": (kv_norm_g, m_kv_norm_g, v_kv_norm_g),
        "w_kv_up": (w_kv_up, m_w_kv_up, v_w_kv_up),
        "conv_w": (conv_w, m_conv_w, v_conv_w),
        "attn_out_g": (attn_out_g, m_attn_out_g, v_attn_out_g),
        "conv_out_g": (conv_out_g, m_conv_out_g, v_conv_out_g),
        "w_out": (w_out, m_w_out, v_w_out),
        "final_norm_g": (final_norm_g, m_final_norm_g, v_final_norm_g),
    }
    grads, loss = _reduce_grads(send_in, send_out, small)
    updated = _adamw(grads, {n: tuple(flat(a) for a in t) for n, t in params.items()})
    outs = [[updated[n][i].reshape(params[n][0].shape) for n, _ in PARAM_SHAPES] for i in range(4)]
    return (loss[0, 0], grad_x, *outs[0], *outs[1], *outs[2], *outs[3])
```

```python
import functools

import jax
import jax.numpy as jnp
from jax import lax
from jax.experimental import pallas as pl
from jax.experimental.pallas import tpu as pltpu

F32 = jnp.float32
BF16 = jnp.bfloat16

N_META = 16
D_MODEL = 1024
N_HEADS = 4
D_NOPE = 128
D_ROPE = 64
D_V = 128
Q_RANK = 256
KV_RANK = 128
CONV_WIDTH = 512
CONV_GROUP = 64
ROPE_THETA = 10000.0
ATTN_SCALE = (D_NOPE + D_ROPE) ** -0.5
EPS = 1e-6
NEG_INF = -1e30

ADAM_LR = 0.001
ADAM_B1 = 0.9
ADAM_B2 = 0.999
ADAM_EPS = 1e-08
ADAM_WD = 0.01
ADAM_STEP = 10

LANES = 128
PAD_FRONT = LANES - N_META
KV_TILE = 256
N_DEV = 8
VMEM_LIMIT = 56 * 1024 * 1024

IN_PAD = 3072
GRP_A = 512
N_A = Q_RANK + KV_RANK + D_ROPE
IN_PROJ = 3008
SHARD_IN = IN_PROJ // N_DEV
SHARD_IN_PAD = 384
SHARD_Q = 96
SHARD_KV = 128
SHARD_OUT = 128
SHARD_CONV = 64
SHARD_META = 128
Q_COLS = N_HEADS * (D_NOPE + D_ROPE)
KV_COLS = N_HEADS * (D_NOPE + D_V)

ROW_Q, ROW_KV, ROW_META, ROW_CONV = 0, 256, 384, 400
ROW_REPL = 408
ROW_NORM, ROW_FINAL, ROW_GQ, ROW_GKV, ROW_ATTN, ROW_CONVG, ROW_LOSS = 408, 416, 424, 426, 427, 431, 435
SMALL_ROWS = 440
ADAM_CHUNK = 64

PARAM_SHAPES = (
    ("meta_tokens", (N_META, SHARD_META)), ("norm_g", (1, D_MODEL)), ("w_in", (D_MODEL, SHARD_IN)),
    ("q_norm_g", (1, Q_RANK)), ("w_q_up", (Q_RANK, SHARD_Q)), ("kv_norm_g", (1, KV_RANK)),
    ("w_kv_up", (KV_RANK, SHARD_KV)), ("conv_w", (3, SHARD_CONV)), ("attn_out_g", (1, CONV_WIDTH)),
    ("conv_out_g", (1, CONV_WIDTH)), ("w_out", (SHARD_OUT, D_MODEL)), ("final_norm_g", (1, D_MODEL)),
)


def _in_pieces(k):
    lo, hi = SHARD_IN * k, SHARD_IN * (k + 1)
    out = []
    if lo < N_A:
        out.append((0, min(hi, N_A) - lo, lo))
    if hi > N_A:
        s = max(lo, N_A)
        out.append((s - lo, hi - lo, s + GRP_A - N_A))
    return out


def _q_pieces(k):
    lo, hi = SHARD_Q * k, SHARD_Q * (k + 1)
    out = []
    for h in range(N_HEADS):
        base = (D_NOPE + D_ROPE) * h
        s, e = max(lo, base), min(hi, base + D_NOPE)
        if s < e:
            out.append((s - lo, e - lo, D_NOPE * h + s - base))
        s, e = max(lo, base + D_NOPE), min(hi, base + D_NOPE + D_ROPE)
        if s < e:
            out.append((s - lo, e - lo, N_HEADS * D_NOPE + D_ROPE * h + s - base - D_NOPE))
    return out


def _kv_dst(k):
    return D_NOPE * (k // 2) + (N_HEADS * D_NOPE if k % 2 else 0)


def _params(*sem):
    return pltpu.CompilerParams(dimension_semantics=sem, vmem_limit_bytes=VMEM_LIMIT)


def _rms_stats(x):
    r = lax.rsqrt(jnp.mean(x * x, axis=-1, keepdims=True) + EPS)
    return x * r, r


def _rms_bwd(gdy, xhat, r):
    return r * (gdy - xhat * jnp.mean(gdy * xhat, axis=-1, keepdims=True))


def _sigmoid(z):
    return 1.0 / (1.0 + jnp.exp(-z))


def _group_mean(x):
    i0 = lax.broadcasted_iota(jnp.int32, (LANES, LANES), 0) // CONV_GROUP
    i1 = lax.broadcasted_iota(jnp.int32, (LANES, LANES), 1) // CONV_GROUP
    m = jnp.where(i0 == i1, 1.0 / CONV_GROUP, 0.0).astype(BF16)
    hi = x.astype(BF16)
    lo = (x - hi.astype(F32)).astype(BF16)
    return jnp.dot(hi, m, preferred_element_type=F32) + jnp.dot(lo, m, preferred_element_type=F32)


_NT = (((1,), (1,)), ((), ()))
_TN = (((0,), (0,)), ((), ()))


def _dot(a, b, dims=None):
    if dims is None:
        return jnp.dot(a, b, preferred_element_type=F32)
    return lax.dot_general(a, b, dims, preferred_element_type=F32)


def _device_position():
    x, y, c = lax.axis_index("x"), lax.axis_index("y"), lax.axis_index("c")
    return x, y, c, 4 * x + 2 * y + c


def _gather_plan(srcs, slots, send_sems, recv_sems, local_sems):
    x, y, c, _ = _device_position()
    me, sibling = (x, y, c), (x, y, 1 - c)
    chips = [(1 - x, y), (x, 1 - y), (1 - x, 1 - y)]
    n = len(srcs)

    def slot(a, px, py, pc):
        return slots[a].at[4 * px + 2 * py + pc]

    def copy(a, k, block, to, own=False):
        return pltpu.make_async_remote_copy(
            src_ref=srcs[a] if own else slot(a, *block),
            dst_ref=slot(a, *block),
            send_sem=send_sems.at[7 * a + k],
            recv_sem=recv_sems.at[7 * a + k],
            device_id=to,
            device_id_type=pl.DeviceIdType.MESH,
        )

    def local(a):
        return pltpu.make_async_copy(srcs[a], slot(a, *me), local_sems.at[a])

    def firsts(a):
        return [copy(a, 0, me, sibling, own=True)] + [
            copy(a, 1 + j, me, (*chip, c), own=True) for j, chip in enumerate(chips)]

    def start():
        for a in range(n):
            local(a).start()
            for cp in firsts(a):
                cp.start()

    def forward():
        for j, chip in enumerate(chips):
            for a in range(n):
                copy(a, 1 + j, (*chip, c), me).wait_recv()
                copy(a, 4 + j, (*chip, c), sibling).start()

    def finish():
        for a in range(n):
            copy(a, 0, sibling, me).wait_recv()
            for j, chip in enumerate(chips):
                copy(a, 4 + j, (*chip, 1 - c), me).wait_recv()
        for a in range(n):
            for cp in firsts(a) + [copy(a, 4 + j, (*chip, c), sibling) for j, chip in enumerate(chips)]:
                cp.wait_send()
            local(a).wait()

    return start, forward, finish


def _adam_update(g, w, m, v):
    m_new = ADAM_B1 * m + (1.0 - ADAM_B1) * g
    v_new = ADAM_B2 * v + (1.0 - ADAM_B2) * (g * g)
    m_hat = m_new / (1.0 - ADAM_B1 ** ADAM_STEP)
    v_hat = v_new / (1.0 - ADAM_B2 ** ADAM_STEP)
    return -ADAM_LR * (m_hat / (jnp.sqrt(v_hat) + ADAM_EPS) + ADAM_WD * w), m_new, v_new


def _adamw(grads, params):
    names = [n for n, _ in PARAM_SHAPES]
    n_p = len(names)

    def body(*refs):
        for i in range(n_p):
            g = refs[i][...]
            w, m, v = (refs[n_p + 3 * i + j][...] for j in range(3))
            delta, m_new, v_new = _adam_update(g, w, m, v)
            for j, val in enumerate((g, delta, m_new, v_new)):
                refs[4 * n_p + 4 * i + j][...] = val

    vm = pl.BlockSpec(memory_space=pltpu.VMEM)
    out_shape = []
    for _, shape in PARAM_SHAPES:
        out_shape += [jax.ShapeDtypeStruct(shape, F32)] * 4
    outs = pl.pallas_call(
        body,
        name="adamw",
        out_shape=tuple(out_shape),
        in_specs=[vm] * (4 * n_p),
        out_specs=(vm,) * (4 * n_p),
        compiler_params=pltpu.CompilerParams(vmem_limit_bytes=VMEM_LIMIT),
    )(*[grads[n] for n in names], *[a for n in names for a in params[n]])
    return {n: outs[4 * i:4 * i + 4] for i, n in enumerate(names)}


def _reduce_grads(send_in, send_out, small_grads):
    n_small = len(small_grads)
    n_p = len(PARAM_SHAPES)
    names = [n for n, _ in PARAM_SHAPES]

    def body(*refs):
        sin_hbm, sout_hbm = refs[0], refs[1]
        (dwq, dwkv, dconv, dmeta, dnorm, dfinal, dgq, dgkv, dattn, dconvg, loss_part) = refs[2:2 + n_small]
        o = 2 + n_small
        g_out = {n: refs[o + i] for i, n in enumerate(names)}
        o += n_p
        loss_out = refs[o]
        ssmall, rin, rout, rsmall, gsum, send_sems, recv_sems, local_sems = refs[o + 1:]
        x, y, c, me = _device_position()

        ssmall[...] = jnp.zeros_like(ssmall)
        rep = ssmall.at[0]
        for i in range(D_MODEL // LANES):
            rep[ROW_NORM + i:ROW_NORM + i + 1, :] = dnorm[:, LANES * i:LANES * (i + 1)]
            rep[ROW_FINAL + i:ROW_FINAL + i + 1, :] = dfinal[:, LANES * i:LANES * (i + 1)]
        for i in range(Q_RANK // LANES):
            rep[ROW_GQ + i:ROW_GQ + i + 1, :] = dgq[:, LANES * i:LANES * (i + 1)]
        rep[ROW_GKV:ROW_GKV + 1, :] = dgkv[...]
        for i in range(CONV_WIDTH // LANES):
            rep[ROW_ATTN + i:ROW_ATTN + i + 1, :] = dattn[:, LANES * i:LANES * (i + 1)]
            rep[ROW_CONVG + i:ROW_CONVG + i + 1, :] = dconvg[:, LANES * i:LANES * (i + 1)]
        rep[ROW_LOSS:ROW_LOSS + 1, :] = loss_part[...]
        for k in range(N_DEV):
            if k:
                ssmall[k, ROW_REPL:, :] = ssmall[0, ROW_REPL:, :]
            for s, e, d in _q_pieces(k):
                ssmall[k, ROW_Q:ROW_Q + Q_RANK, s:e] = dwq[:, d:d + e - s]
            ssmall[k, ROW_KV:ROW_KV + KV_RANK, :] = dwkv[:, _kv_dst(k):_kv_dst(k) + SHARD_KV]
            ssmall[k, ROW_META:ROW_META + N_META, :] = dmeta[:, SHARD_META * k:SHARD_META * (k + 1)]
            ssmall[k, ROW_CONV:ROW_CONV + 3, 0:SHARD_CONV] = dconv[0:3, SHARD_CONV * k:SHARD_CONV * (k + 1)]

        copies = []
        for r in range(1, N_DEV):
            peer = (1 - x if r & 4 else x, 1 - y if r & 2 else y, 1 - c if r & 1 else c)
            peer_idx = 4 * peer[0] + 2 * peer[1] + peer[2]
            for j, (src, dst) in enumerate(((sin_hbm, rin), (sout_hbm, rout), (ssmall, rsmall))):
                copies.append(pltpu.make_async_remote_copy(
                    src_ref=src.at[peer_idx],
                    dst_ref=dst.at[r],
                    send_sem=send_sems.at[3 * (r - 1) + j],
                    recv_sem=recv_sems.at[3 * (r - 1) + j],
                    device_id=peer,
                    device_id_type=pl.DeviceIdType.MESH,
                ))
        for cp in copies:
            cp.start()
        own = [pltpu.make_async_copy(sin_hbm.at[me], rin.at[0], local_sems.at[0]),
               pltpu.make_async_copy(sout_hbm.at[me], rout.at[0], local_sems.at[1])]
        for cp in own:
            cp.start()
        rsmall[0] = ssmall[me]
        for cp in own:
            cp.wait()
        for cp in copies:
            cp.wait_recv()

        def apply(name, g, idx):
            g_out[name][idx] = g

        def chunk(i, carry):
            sl = pl.ds(pl.multiple_of(i * ADAM_CHUNK, ADAM_CHUNK), ADAM_CHUNK)
            g = rin[0, sl, :].astype(F32)
            for r in range(1, N_DEV):
                g = g + rin[r, sl, :].astype(F32)
            apply("w_in", g[:, :SHARD_IN], (sl, slice(None)))
            return carry

        lax.fori_loop(0, D_MODEL // ADAM_CHUNK, chunk, 0)

        g = rout[0].astype(F32)
        for r in range(1, N_DEV):
            g = g + rout[r].astype(F32)
        apply("w_out", g, (slice(None), slice(None)))

        gs = rsmall[me]
        for d in range(1, N_DEV):
            gs = gs + rsmall[d ^ me]
        gsum[...] = gs
        full = (slice(None), slice(None))
        apply("w_q_up", gsum[ROW_Q:ROW_Q + Q_RANK, 0:SHARD_Q], full)
        apply("w_kv_up", gsum[ROW_KV:ROW_KV + KV_RANK, :], full)
        apply("meta_tokens", gsum[ROW_META:ROW_META + N_META, :], full)
        apply("conv_w", gsum[ROW_CONV:ROW_CONV + 3, 0:SHARD_CONV], full)
        for name, row, width in (("norm_g", ROW_NORM, D_MODEL), ("final_norm_g", ROW_FINAL, D_MODEL),
                                 ("q_norm_g", ROW_GQ, Q_RANK), ("kv_norm_g", ROW_GKV, KV_RANK),
                                 ("attn_out_g", ROW_ATTN, CONV_WIDTH), ("conv_out_g", ROW_CONVG, CONV_WIDTH)):
            for i in range(width // LANES):
                apply(name, gsum[row + i:row + i + 1, :], (slice(None), slice(LANES * i, LANES * (i + 1))))
        loss_out[...] = gsum[ROW_LOSS:ROW_LOSS + 1, :]

        for cp in copies:
            cp.wait_send()

    vm = pl.BlockSpec(memory_space=pltpu.VMEM)
    hbm = pl.BlockSpec(memory_space=pl.ANY)
    out_shape = [jax.ShapeDtypeStruct(shape, F32) for _, shape in PARAM_SHAPES]
    out_shape.append(jax.ShapeDtypeStruct((1, LANES), F32))
    outs = pl.pallas_call(
        body,
        name="reduce_grads",
        out_shape=tuple(out_shape),
        in_specs=[hbm, hbm] + [vm] * n_small,
        out_specs=(vm,) * len(out_shape),
        scratch_shapes=[
            pltpu.VMEM((N_DEV, SMALL_ROWS, LANES), F32),
            pltpu.VMEM((N_DEV, D_MODEL, SHARD_IN_PAD), BF16),
            pltpu.VMEM((N_DEV, SHARD_OUT, D_MODEL), BF16),
            pltpu.VMEM((N_DEV, SMALL_ROWS, LANES), F32),
            pltpu.VMEM((SMALL_ROWS, LANES), F32),
            pltpu.SemaphoreType.DMA((3 * (N_DEV - 1),)),
            pltpu.SemaphoreType.DMA((3 * (N_DEV - 1),)),
            pltpu.SemaphoreType.DMA((2,)),
        ],
        compiler_params=pltpu.CompilerParams(vmem_limit_bytes=VMEM_LIMIT),
    )(send_in, send_out, *small_grads)
    return {n: outs[i] for i, n in enumerate(names)}, outs[-1]


def _prep_gather(x, meta, norm_g, w_in):
    nb_seq, s, d = x.shape
    nb = s // LANES + 1
    forward_step = nb_seq * (nb // 3)
    finish_step = nb_seq * (nb - 1)

    def body(x_ref, meta_ref, g_ref, win_ref, u_ref, w_in_p, meta_f,
             sbig, ssmall, gbig, gsmall, send_sems, recv_sems, local_sems):
        jj, b = pl.program_id(0), pl.program_id(1)
        t = jj * nb_seq + b

        def plan():
            return _gather_plan((sbig, ssmall), (gbig, gsmall), send_sems, recv_sems, local_sems)

        @pl.when(t == 0)
        def _():
            sbig[:, 0:SHARD_IN] = win_ref[...].astype(BF16)
            sbig[:, SHARD_IN:] = jnp.zeros((d, SHARD_IN_PAD - SHARD_IN), BF16)
            ssmall[...] = meta_ref[...]
            plan()[0]()

        @pl.when(t == forward_step)
        def _():
            plan()[1]()

        @pl.when(t == finish_step)
        def _():
            plan()[2]()
            w_in_p[:, N_A:GRP_A] = jnp.zeros((d, GRP_A - N_A), BF16)
            for k in range(N_DEV):
                for s0, e0, d0 in _in_pieces(k):
                    w_in_p[:, d0:d0 + e0 - s0] = gbig[k, :, s0:e0]
                meta_f[:, SHARD_META * k:SHARD_META * (k + 1)] = gsmall[k]

        def norm(h):
            hhat, _ = _rms_stats(h)
            return (hhat * g_ref[...]).astype(BF16)

        @pl.when(jj < nb - 1)
        def _():
            u_ref[...] = norm(x_ref[0])

        @pl.when(jj == nb - 1)
        def _():
            u_ref[0:PAD_FRONT, :] = jnp.zeros((PAD_FRONT, d), BF16)
            u_ref[PAD_FRONT:LANES, :] = norm(meta_f[...])

    whole = lambda shape: pl.BlockSpec(shape, lambda jj, b: (0,) * len(shape))
    return pl.pallas_call(
        body,
        name="prep_norm_gather",
        grid=(nb, nb_seq),
        in_specs=[
            pl.BlockSpec((1, LANES, d), lambda jj, b: (b, jnp.minimum(jj, nb - 2), 0)),
            whole(meta.shape), whole(norm_g.shape), whole(w_in.shape),
        ],
        out_specs=(pl.BlockSpec((LANES, d), lambda jj, b: (b * nb + (jj + 1) % nb, 0)),
                   whole((d, IN_PAD)), whole((N_META, d))),
        out_shape=(jax.ShapeDtypeStruct((nb_seq * nb * LANES, d), BF16),
                   jax.ShapeDtypeStruct((d, IN_PAD), BF16),
                   jax.ShapeDtypeStruct((N_META, d), F32)),
        scratch_shapes=[
            pltpu.VMEM((d, SHARD_IN_PAD), BF16),
            pltpu.VMEM((N_META, SHARD_META), F32),
            pltpu.VMEM((N_DEV, d, SHARD_IN_PAD), BF16),
            pltpu.VMEM((N_DEV, N_META, SHARD_META), F32),
            pltpu.SemaphoreType.DMA((14,)),
            pltpu.SemaphoreType.DMA((14,)),
            pltpu.SemaphoreType.DMA((2,)),
        ],
        compiler_params=_params("arbitrary", "arbitrary"),
    )(x, meta, norm_g, w_in)


def _in_proj_gather(u, w_in_p, w_q, w_kv, w_out, conv_w, bm, bn):
    m, k_dim = u.shape
    n = w_in_p.shape[1]
    steps = (m // bm) * (n // bn)
    forward_step = steps // 3
    qkv_rows = Q_RANK + KV_RANK

    def body(a_ref, b_ref, wq_ref, wkv_ref, wout_ref, conv_ref, o_ref, w_q_p, w_kv_p, w_out_f, conv_f,
             sqkv, sout, sconv, gqkv, gout, gconv, send_sems, recv_sems, local_sems):
        t = pl.program_id(0) * (n // bn) + pl.program_id(1)

        def plan():
            return _gather_plan((sqkv, sout, sconv), (gqkv, gout, gconv), send_sems, recv_sems, local_sems)

        @pl.when(t == 0)
        def _():
            sqkv[...] = jnp.zeros_like(sqkv)
            sqkv[0:Q_RANK, 0:SHARD_Q] = wq_ref[...].astype(BF16)
            sqkv[Q_RANK:, :] = wkv_ref[...].astype(BF16)
            sout[...] = wout_ref[...].astype(BF16)
            sconv[...] = jnp.zeros_like(sconv)
            sconv[0:3, 0:SHARD_CONV] = conv_ref[...]
            plan()[0]()

        @pl.when(t == forward_step)
        def _():
            plan()[1]()

        o_ref[...] = _dot(a_ref[...], b_ref[...]).astype(o_ref.dtype)

        @pl.when(t == steps - 1)
        def _():
            plan()[2]()
            conv_f[...] = jnp.zeros_like(conv_f)
            for k in range(N_DEV):
                for s0, e0, d0 in _q_pieces(k):
                    w_q_p[:, d0:d0 + e0 - s0] = gqkv[k, 0:Q_RANK, s0:e0]
                w_kv_p[:, _kv_dst(k):_kv_dst(k) + SHARD_KV] = gqkv[k, Q_RANK:, :]
                w_out_f[SHARD_OUT * k:SHARD_OUT * (k + 1), :] = gout[k]
                conv_f[0:3, SHARD_CONV * k:SHARD_CONV * (k + 1)] = gconv[k, 0:3, 0:SHARD_CONV]

    whole = lambda shape: pl.BlockSpec(shape, lambda i, j: (0,) * len(shape))
    return pl.pallas_call(
        body,
        name="in_proj_gather",
        grid=(m // bm, n // bn),
        in_specs=[pl.BlockSpec((bm, k_dim), lambda i, j: (i, 0)), pl.BlockSpec((k_dim, bn), lambda i, j: (0, j)),
                  whole(w_q.shape), whole(w_kv.shape), whole(w_out.shape), whole(conv_w.shape)],
        out_specs=(pl.BlockSpec((bm, bn), lambda i, j: (i, j)),
                   whole((Q_RANK, Q_COLS)), whole((KV_RANK, KV_COLS)), whole((D_MODEL, D_MODEL)),
                   whole((8, CONV_WIDTH))),
        out_shape=(jax.ShapeDtypeStruct((m, n), BF16),
                   jax.ShapeDtypeStruct((Q_RANK, Q_COLS), BF16),
                   jax.ShapeDtypeStruct((KV_RANK, KV_COLS), BF16),
                   jax.ShapeDtypeStruct((D_MODEL, D_MODEL), BF16),
                   jax.ShapeDtypeStruct((8, CONV_WIDTH), F32)),
        scratch_shapes=[
            pltpu.VMEM((qkv_rows, LANES), BF16),
            pltpu.VMEM((SHARD_OUT, D_MODEL), BF16),
            pltpu.VMEM((8, LANES), F32),
            pltpu.VMEM((N_DEV, qkv_rows, LANES), BF16),
            pltpu.VMEM((N_DEV, SHARD_OUT, D_MODEL), BF16),
            pltpu.VMEM((N_DEV, 8, LANES), F32),
            pltpu.SemaphoreType.DMA((21,)),
            pltpu.SemaphoreType.DMA((21,)),
            pltpu.SemaphoreType.DMA((3,)),
        ],
        compiler_params=_params("arbitrary", "arbitrary"),
    )(u, w_in_p, w_q, w_kv, w_out, conv_w)


def _rope_tables(tp):
    half = D_ROPE // 2
    inv_freq = 1.0 / (ROPE_THETA ** (jnp.arange(half, dtype=F32) / half))
    pos = (jnp.arange(tp) - PAD_FRONT).astype(F32)
    ang = pos[:, None] * inv_freq[None, :]
    cos = jnp.tile(jnp.cos(ang), (1, LANES // half))
    sin = jnp.tile(jnp.sin(ang), (1, LANES // half))
    first = (jnp.arange(LANES) % D_ROPE) < half
    return cos, jnp.where(first, -sin, 0.0), jnp.where(first, 0.0, sin)


def _rope(t, cos, sa, sb):
    return t * cos + pltpu.roll(t, LANES - D_ROPE // 2, 1) * sa + pltpu.roll(t, D_ROPE // 2, 1) * sb


def _rope_t(t, cos, sa, sb):
    return t * cos + pltpu.roll(t * sa, D_ROPE // 2, 1) + pltpu.roll(t * sb, LANES - D_ROPE // 2, 1)


def _qkv_fwd(p, wq, wkv, gq, gkv, tables, nb_seq, tp):
    ht = tp // 2

    def body(pa_ref, wq_ref, wkv_ref, gq_ref, gkv_ref, cos_ref, sa_ref, sb_ref, q_ref, k_ref, v_ref):
        pa = pa_ref[...].astype(F32)
        cq_hat, _ = _rms_stats(pa[:, :Q_RANK])
        ckv_hat, _ = _rms_stats(pa[:, Q_RANK:Q_RANK + KV_RANK])
        q = _dot((cq_hat * gq_ref[...]).astype(BF16), wq_ref[...])
        kv = _dot((ckv_hat * gkv_ref[...]).astype(BF16), wkv_ref[...])
        tabs = (cos_ref[...], sa_ref[...], sb_ref[...])
        lane = lax.broadcasted_iota(jnp.int32, (ht, LANES), 1)
        low = lane < D_ROPE
        mark = lane == D_ROPE
        row = (pl.program_id(0) % 2) * ht + lax.broadcasted_iota(jnp.int32, (ht, LANES), 0)
        k_pe = jnp.where(mark & (row < PAD_FRONT), NEG_INF, _rope(pa[:, Q_RANK + KV_RANK:], *tabs))
        one = jnp.where(mark & (row >= PAD_FRONT), 1.0, 0.0)
        pairs = [_rope(q[:, N_HEADS * D_NOPE + LANES * i:N_HEADS * D_NOPE + LANES * (i + 1)], *tabs) for i in range(2)]
        for h in range(N_HEADS):
            pair = pairs[h // 2]
            if h % 2:
                pair = pltpu.roll(pair, D_ROPE, 1)
            pe = jnp.where(low, pair, one)
            q_ref[0, h] = jnp.concatenate([q[:, D_NOPE * h:D_NOPE * (h + 1)], pe], axis=1).astype(BF16)
            k_ref[0, h] = jnp.concatenate([kv[:, D_NOPE * h:D_NOPE * (h + 1)], k_pe], axis=1).astype(BF16)
            v_ref[0, h] = kv[:, N_HEADS * D_NOPE + D_V * h:N_HEADS * D_NOPE + D_V * (h + 1)].astype(BF16)

    full = lambda a: pl.BlockSpec(a.shape, lambda i: (0,) * a.ndim)
    tab = pl.BlockSpec((ht, LANES), lambda i: (i % 2, 0))
    qk = pl.BlockSpec((1, N_HEADS, ht, 2 * LANES), lambda i: (i // 2, 0, i % 2, 0))
    return pl.pallas_call(
        body,
        name="qkv_fwd",
        grid=(2 * nb_seq,),
        in_specs=[pl.BlockSpec((ht, GRP_A), lambda i: (i, 0)), full(wq), full(wkv), full(gq), full(gkv), tab, tab, tab],
        out_specs=(qk, qk, pl.BlockSpec((1, N_HEADS, ht, D_V), lambda i: (i // 2, 0, i % 2, 0))),
        out_shape=(
            jax.ShapeDtypeStruct((nb_seq, N_HEADS, tp, 2 * LANES), BF16),
            jax.ShapeDtypeStruct((nb_seq, N_HEADS, tp, 2 * LANES), BF16),
            jax.ShapeDtypeStruct((nb_seq, N_HEADS, tp, D_V), BF16),
        ),
        compiler_params=_params("parallel"),
    )(p, wq, wkv, gq, gkv, *tables)


def _attn_fwd(q, k, v, p, g_attn):
    nb_seq, _, tp, _ = q.shape

    def body(q_ref, k_ref, v_ref, z_ref, g_ref, y_ref, o_ref, lse_ref):
        g = g_ref[...]
        for r0 in range(0, tp, KV_TILE):
            nq = min(KV_TILE, tp - r0)
            kend = r0 + nq
            qq = q_ref[0, 0, r0:kend, :]
            sd = _dot(qq, k_ref[0, 0, r0:kend, :], _NT) * ATTN_SCALE
            causal = (lax.broadcasted_iota(jnp.int32, (nq, nq), 1) <= lax.broadcasted_iota(jnp.int32, (nq, nq), 0))
            sd = jnp.where(causal, sd, NEG_INF)
            m = jnp.max(sd, axis=-1, keepdims=True)
            if r0:
                so = _dot(qq, k_ref[0, 0, 0:r0, :], _NT) * ATTN_SCALE
                m = jnp.maximum(m, jnp.max(so, axis=-1, keepdims=True))
            ed = jnp.exp(sd - m)
            l = jnp.sum(ed, axis=-1, keepdims=True)
            o = _dot(ed.astype(BF16), v_ref[0, 0, r0:kend, :])
            if r0:
                eo = jnp.exp(so - m)
                l = l + jnp.sum(eo, axis=-1, keepdims=True)
                o = o + _dot(eo.astype(BF16), v_ref[0, 0, 0:r0, :])
            o = o * (1.0 / l)
            o_ref[0, 0, r0:kend, :] = o
            lse_ref[0, 0, r0:kend, :] = jnp.broadcast_to(m + jnp.log(l), (nq, LANES))
            ohat, _ = _rms_stats(o)
            z = z_ref[r0:kend, :].astype(F32)
            y_ref[r0:kend, :] = (ohat * g * (z * _sigmoid(z))).astype(BF16)

    qk = pl.BlockSpec((1, 1, tp, 2 * LANES), lambda b, h: (b, h, 0, 0))
    hv = pl.BlockSpec((1, 1, tp, D_V), lambda b, h: (b, h, 0, 0))
    return pl.pallas_call(
        body,
        name="attn_fwd",
        grid=(nb_seq, N_HEADS),
        in_specs=[qk, qk, hv,
                  pl.BlockSpec((tp, LANES), lambda b, h: (b, GRP_A // LANES + h)),
                  pl.BlockSpec((1, LANES), lambda b, h: (0, h))],
        out_specs=(pl.BlockSpec((tp, LANES), lambda b, h: (b, h)), hv, hv),
        out_shape=(
            jax.ShapeDtypeStruct((nb_seq * tp, N_HEADS * D_V), BF16),
            jax.ShapeDtypeStruct((nb_seq, N_HEADS, tp, D_V), F32),
            jax.ShapeDtypeStruct((nb_seq, N_HEADS, tp, LANES), F32),
        ),
        compiler_params=_params("parallel", "parallel"),
    )(q, k, v, p, g_attn)


_CONV_COL0 = (GRP_A + N_HEADS * D_V) // LANES


def _conv_specs(tp, order):
    cols = CONV_WIDTH // LANES
    return [pl.BlockSpec((tp, LANES), functools.partial(
        lambda a, b, off: order(a, b, off), off=_CONV_COL0 + i * cols)) for i in range(4)]


def _conv_fwd(p, conv_w, g_conv, nb_seq, tp):
    def body(b_ref, c_ref, h_ref, z_ref, w_ref, g_ref, y_ref):
        cc = c_ref[...].astype(F32) * h_ref[...].astype(F32)
        row = lax.broadcasted_iota(jnp.int32, (tp, LANES), 0)
        s1 = jnp.where(row >= 1, pltpu.roll(cc, 1, 0), 0.0)
        s2 = jnp.where(row >= 2, pltpu.roll(cc, 2, 0), 0.0)
        yc = b_ref[...].astype(F32) * (w_ref[0:1, :] * s2 + w_ref[1:2, :] * s1 + w_ref[2:3, :] * cc)
        r = lax.rsqrt(_group_mean(yc * yc) + EPS)
        z = z_ref[...].astype(F32)
        y_ref[...] = (yc * r * g_ref[...] * (z * _sigmoid(z))).astype(BF16)

    return pl.pallas_call(
        body,
        name="conv_fwd",
        grid=(nb_seq, CONV_WIDTH // LANES),
        in_specs=_conv_specs(tp, lambda b, t, off: (b, off + t)) + [
            pl.BlockSpec((8, LANES), lambda b, t: (0, t)),
            pl.BlockSpec((1, LANES), lambda b, t: (0, t))],
        out_specs=pl.BlockSpec((tp, LANES), lambda b, t: (b, t)),
        out_shape=jax.ShapeDtypeStruct((nb_seq * tp, CONV_WIDTH), BF16),
        compiler_params=_params("parallel", "parallel"),
    )(p, p, p, p, conv_w, g_conv)


def _out_proj(ya, yc, w_out, bm):
    r, ka = ya.shape
    d = w_out.shape[1]

    def body(a_ref, c_ref, w_ref, o_ref):
        o_ref[...] = _dot(a_ref[...], w_ref[0:ka, :]) + _dot(c_ref[...], w_ref[ka:, :])

    return pl.pallas_call(
        body,
        name="out_proj",
        grid=(r // bm,),
        in_specs=[pl.BlockSpec((bm, ka), lambda i: (i, 0)), pl.BlockSpec((bm, yc.shape[1]), lambda i: (i, 0)),
                  pl.BlockSpec(w_out.shape, lambda i: (0, 0))],
        out_specs=pl.BlockSpec((bm, d), lambda i: (i, 0)),
        out_shape=jax.ShapeDtypeStruct((r, d), F32),
        compiler_params=_params("parallel"),
    )(ya, yc, w_out)


def _loss_bwd(x, mix, target, g_final):
    nb_seq, s, d = x.shape
    nb = s // LANES + 1

    def body(x_ref, mix_ref, t_ref, g_ref, dh_ref, dhb_ref, dg_ref, loss_ref, acc_ref):
        b, j = pl.program_id(0), pl.program_id(1)

        @pl.when((b == 0) & (j == 0))
        def _():
            acc_ref[...] = jnp.zeros_like(acc_ref)
            dg_ref[...] = jnp.zeros_like(dg_ref)

        @pl.when(j == 0)
        def _():
            dh_ref[...] = jnp.zeros_like(dh_ref)
            dhb_ref[...] = jnp.zeros_like(dhb_ref)

        @pl.when(j > 0)
        def _():
            g = g_ref[...]
            hhat, r = _rms_stats(x_ref[0] + mix_ref[...])
            e = hhat * g - t_ref[0]
            acc_ref[...] += jnp.sum(e * e, axis=0, keepdims=True)
            dy = e * (1.0 / d)
            dg_ref[...] += jnp.sum(dy * hhat, axis=0, keepdims=True)
            dh = _rms_bwd(g * dy, hhat, r)
            dh_ref[...] = dh
            dhb_ref[...] = dh.astype(BF16)

        @pl.when((b == nb_seq - 1) & (j == nb - 1))
        def _():
            total = jnp.sum(acc_ref[...], axis=1, keepdims=True)
            loss_ref[...] = jnp.broadcast_to((0.5 / d) * total, loss_ref.shape)

    tok = pl.BlockSpec((1, LANES, d), lambda b, j: (b, jnp.maximum(j - 1, 0), 0))
    row = pl.BlockSpec((LANES, d), lambda b, j: (b * nb + j, 0))
    vec = pl.BlockSpec((1, d), lambda b, j: (0, 0))
    return pl.pallas_call(
        body,
        name="loss_bwd",
        grid=(nb_seq, nb),
        in_specs=[tok, row, tok, vec],
        out_specs=(row, row, vec, pl.BlockSpec((1, LANES), lambda b, j: (0, 0))),
        out_shape=(
            jax.ShapeDtypeStruct((nb_seq * nb * LANES, d), F32),
            jax.ShapeDtypeStruct((nb_seq * nb * LANES, d), BF16),
            jax.ShapeDtypeStruct((1, d), F32),
            jax.ShapeDtypeStruct((1, LANES), F32),
        ),
        scratch_shapes=[pltpu.VMEM((1, d), F32)],
        compiler_params=_params("arbitrary", "arbitrary"),
    )(x, mix, target, g_final)


def _out_proj_bwd(dhb, w_out, ya, yc, bm):
    r, d = dhb.shape
    ka = ya.shape[1]
    n_mix = w_out.shape[0]
    last = r // bm - 1

    def body(dh_ref, w_ref, a_ref, c_ref, dcat_ref, dw_ref, acc_ref):
        @pl.when(pl.program_id(0) == 0)
        def _():
            acc_ref[...] = jnp.zeros_like(acc_ref)

        dh = dh_ref[...]
        dcat_ref[...] = _dot(dh, w_ref[...], _NT).astype(BF16)
        acc_ref[0:ka, :] += _dot(a_ref[...], dh, _TN)
        acc_ref[ka:, :] += _dot(c_ref[...], dh, _TN)

        @pl.when(pl.program_id(0) == last)
        def _():
            dw_ref[...] = acc_ref[...].astype(BF16)

    return pl.pallas_call(
        body,
        name="out_proj_bwd",
        grid=(r // bm,),
        in_specs=[pl.BlockSpec((bm, d), lambda i: (i, 0)), pl.BlockSpec(w_out.shape, lambda i: (0, 0)),
                  pl.BlockSpec((bm, ka), lambda i: (i, 0)), pl.BlockSpec((bm, yc.shape[1]), lambda i: (i, 0))],
        out_specs=(pl.BlockSpec((bm, n_mix), lambda i: (i, 0)),
                   pl.BlockSpec((n_mix, d), lambda i: (0, 0))),
        out_shape=(jax.ShapeDtypeStruct((r, n_mix), BF16),
                   jax.ShapeDtypeStruct((n_mix, d), BF16)),
        scratch_shapes=[pltpu.VMEM((n_mix, d), F32)],
        compiler_params=_params("arbitrary"),
    )(dhb, w_out, ya, yc)


def _attn_bwd(q, k, v, o, lse, dcat, p, g_attn):
    nb_seq, _, tp, _ = q.shape

    def body(q_ref, k_ref, v_ref, o_ref, lse_ref, dy_ref, z_ref, g_ref,
             dq_ref, dk_ref, dv_ref, dz_ref, dg_ref, dq_acc):
        @pl.when(pl.program_id(1) == 0)
        def _():
            dg_ref[...] = jnp.zeros_like(dg_ref)

        g = g_ref[...]
        z = z_ref[...].astype(F32)
        o = o_ref[0, 0]
        dy = dy_ref[...].astype(F32)
        sig = _sigmoid(z)
        ohat, r = _rms_stats(o)
        don = dy * (z * sig)
        dz_ref[...] = (dy * (ohat * g) * (sig * (1.0 + z * (1.0 - sig)))).astype(BF16)
        dg_ref[...] += jnp.sum(don * ohat, axis=0, keepdims=True)
        do = _rms_bwd(g * don, ohat, r)
        dvec = jnp.sum(do * o, axis=-1, keepdims=True)
        dob = do.astype(BF16)
        lse_col = lse_ref[0, 0, :, 0:1]
        dq_acc[...] = jnp.zeros_like(dq_acc)
        for k0 in range(0, tp, KV_TILE):
            nk = min(KV_TILE, tp - k0)
            nq = tp - k0
            qq = q_ref[0, 0, k0:, :]
            kk = k_ref[0, 0, k0:k0 + nk, :]
            causal = (lax.broadcasted_iota(jnp.int32, (nq, nk), 1) <= lax.broadcasted_iota(jnp.int32, (nq, nk), 0))
            pr = jnp.where(causal, jnp.exp(_dot(qq, kk, _NT) * ATTN_SCALE - lse_col[k0:]), 0.0)
            dp = _dot(dob[k0:], v_ref[0, 0, k0:k0 + nk, :], _NT)
            ds = (pr * (dp - dvec[k0:]) * ATTN_SCALE).astype(BF16)
            dv_ref[0, 0, k0:k0 + nk, :] = _dot(pr.astype(BF16), dob[k0:], _TN).astype(BF16)
            dk_ref[0, 0, k0:k0 + nk, :] = _dot(ds, qq, _TN).astype(BF16)
            dq_acc[k0:, :] += _dot(ds, kk)
        dq_ref[0, 0] = dq_acc[...].astype(BF16)

    qk = pl.BlockSpec((1, 1, tp, 2 * LANES), lambda h, b: (b, h, 0, 0))
    hv = pl.BlockSpec((1, 1, tp, D_V), lambda h, b: (b, h, 0, 0))
    col = pl.BlockSpec((tp, LANES), lambda h, b: (b, h))
    return pl.pallas_call(
        body,
        name="attn_bwd",
        grid=(N_HEADS, nb_seq),
        in_specs=[qk, qk, hv, hv, hv, col,
                  pl.BlockSpec((tp, LANES), lambda h, b: (b, GRP_A // LANES + h)),
                  pl.BlockSpec((1, LANES), lambda h, b: (0, h))],
        out_specs=(qk, qk, hv, col, pl.BlockSpec((1, LANES), lambda h, b: (0, h))),
        out_shape=(
            jax.ShapeDtypeStruct((nb_seq, N_HEADS, tp, 2 * LANES), BF16),
            jax.ShapeDtypeStruct((nb_seq, N_HEADS, tp, 2 * LANES), BF16),
            jax.ShapeDtypeStruct((nb_seq, N_HEADS, tp, D_V), BF16),
            jax.ShapeDtypeStruct((nb_seq * tp, N_HEADS * D_V), BF16),
            jax.ShapeDtypeStruct((1, N_HEADS * D_V), F32),
        ),
        scratch_shapes=[pltpu.VMEM((tp, 2 * LANES), F32)],
        compiler_params=_params("arbitrary", "arbitrary"),
    )(q, k, v, o, lse, dcat, p, g_attn)


def _qkv_bwd(p, dq, dk, dv, wq, wkv, gq, gkv, tables):
    nb_seq, _, tp, _ = dq.shape
    ht = tp // 2

    def body(pa_ref, dq_ref, dk_ref, dv_ref, wq_ref, wkv_ref, gq_ref, gkv_ref, cos_ref, sa_ref, sb_ref,
             dpa_ref, dwq_ref, dwkv_ref, dgq_ref, dgkv_ref):
        @pl.when(pl.program_id(0) == 0)
        def _():
            dwq_ref[...] = jnp.zeros_like(dwq_ref)
            dwkv_ref[...] = jnp.zeros_like(dwkv_ref)
            dgq_ref[...] = jnp.zeros_like(dgq_ref)
            dgkv_ref[...] = jnp.zeros_like(dgkv_ref)

        pa = pa_ref[...].astype(F32)
        gq, gkv = gq_ref[...], gkv_ref[...]
        cq_hat, rq = _rms_stats(pa[:, :Q_RANK])
        ckv_hat, rkv = _rms_stats(pa[:, Q_RANK:Q_RANK + KV_RANK])
        tabs = (cos_ref[...], sa_ref[...], sb_ref[...])

        pe = [dq_ref[0, h, :, D_NOPE:].astype(F32) for h in range(N_HEADS)]
        pairs = [_rope_t(pe[2 * i] + pltpu.roll(pe[2 * i + 1], D_ROPE, 1), *tabs).astype(BF16) for i in range(2)]
        dq_flat = jnp.concatenate([dq_ref[0, h, :, :D_NOPE] for h in range(N_HEADS)] + pairs, axis=1)
        dwq_ref[...] += _dot((cq_hat * gq).astype(BF16), dq_flat, _TN)
        dcqn = _dot(dq_flat, wq_ref[...], _NT)
        dgq_ref[...] += jnp.sum(dcqn * cq_hat, axis=0, keepdims=True)
        dcq = _rms_bwd(gq * dcqn, cq_hat, rq)

        dkv_flat = jnp.concatenate([dk_ref[0, h, :, :D_NOPE] for h in range(N_HEADS)]
                                   + [dv_ref[0, h] for h in range(N_HEADS)], axis=1)
        dwkv_ref[...] += _dot((ckv_hat * gkv).astype(BF16), dkv_flat, _TN)
        dckvn = _dot(dkv_flat, wkv_ref[...], _NT)
        dgkv_ref[...] += jnp.sum(dckvn * ckv_hat, axis=0, keepdims=True)
        dckv = _rms_bwd(gkv * dckvn, ckv_hat, rkv)

        dk_pe = dk_ref[0, 0, :, D_NOPE:].astype(F32)
        for h in range(1, N_HEADS):
            dk_pe = dk_pe + dk_ref[0, h, :, D_NOPE:].astype(F32)
        dk_pe = jnp.where(lax.broadcasted_iota(jnp.int32, (ht, LANES), 1) < D_ROPE, dk_pe, 0.0)
        dpa_ref[...] = jnp.concatenate([dcq, dckv, _rope_t(dk_pe, *tabs)], axis=1).astype(BF16)

    full = lambda a: pl.BlockSpec(a.shape, lambda i: (0,) * a.ndim)
    tab = pl.BlockSpec((ht, LANES), lambda i: (i % 2, 0))
    qk = pl.BlockSpec((1, N_HEADS, ht, 2 * LANES), lambda i: (i // 2, 0, i % 2, 0))
    acc = lambda shape: pl.BlockSpec(shape, lambda i: (0, 0))
    return pl.pallas_call(
        body,
        name="qkv_bwd",
        grid=(2 * nb_seq,),
        in_specs=[pl.BlockSpec((ht, GRP_A), lambda i: (i, 0)), qk, qk,
                  pl.BlockSpec((1, N_HEADS, ht, D_V), lambda i: (i // 2, 0, i % 2, 0)),
                  full(wq), full(wkv), full(gq), full(gkv), tab, tab, tab],
        out_specs=(pl.BlockSpec((ht, GRP_A), lambda i: (i, 0)),
                   acc(wq.shape), acc(wkv.shape), acc((1, Q_RANK)), acc((1, KV_RANK))),
        out_shape=(
            jax.ShapeDtypeStruct((nb_seq * tp, GRP_A), BF16),
            jax.ShapeDtypeStruct(wq.shape, F32),
            jax.ShapeDtypeStruct(wkv.shape, F32),
            jax.ShapeDtypeStruct((1, Q_RANK), F32),
            jax.ShapeDtypeStruct((1, KV_RANK), F32),
        ),
        compiler_params=_params("arbitrary"),
    )(p, dq, dk, dv, wq, wkv, gq, gkv, *tables)


def _conv_bwd(p, dcat, conv_w, g_conv, nb_seq, tp):
    cols = CONV_WIDTH // LANES

    def body(b_ref, c_ref, h_ref, z_ref, dy_ref, w_ref, g_ref,
             db_ref, dc_ref, dh_ref, dz_ref, dw_ref, dg_ref):
        @pl.when(pl.program_id(1) == 0)
        def _():
            dw_ref[...] = jnp.zeros_like(dw_ref)
            dg_ref[...] = jnp.zeros_like(dg_ref)

        cb, c, h = b_ref[...].astype(F32), c_ref[...].astype(F32), h_ref[...].astype(F32)
        z, dy = z_ref[...].astype(F32), dy_ref[...].astype(F32)
        g = g_ref[...]
        w0, w1, w2 = w_ref[0:1, :], w_ref[1:2, :], w_ref[2:3, :]
        cc = c * h
        row = lax.broadcasted_iota(jnp.int32, (tp, LANES), 0)
        s1 = jnp.where(row >= 1, pltpu.roll(cc, 1, 0), 0.0)
        s2 = jnp.where(row >= 2, pltpu.roll(cc, 2, 0), 0.0)
        dwc = w0 * s2 + w1 * s1 + w2 * cc
        yc = cb * dwc
        r = lax.rsqrt(_group_mean(yc * yc) + EPS)
        ychat = yc * r
        sig = _sigmoid(z)
        dz_ref[...] = (dy * (ychat * g) * (sig * (1.0 + z * (1.0 - sig)))).astype(BF16)
        dyn = dy * (z * sig)
        dg_ref[...] += jnp.sum(dyn * ychat, axis=0, keepdims=True)
        gd = g * dyn
        dyc = r * (gd - ychat * _group_mean(gd * ychat))
        db_ref[...] = (dyc * dwc).astype(BF16)
        ddw = dyc * cb
        dw_ref[0:1, :] += jnp.sum(ddw * s2, axis=0, keepdims=True)
        dw_ref[1:2, :] += jnp.sum(ddw * s1, axis=0, keepdims=True)
        dw_ref[2:3, :] += jnp.sum(ddw * cc, axis=0, keepdims=True)
        u1 = jnp.where(row <= tp - 2, pltpu.roll(ddw, tp - 1, 0), 0.0)
        u2 = jnp.where(row <= tp - 3, pltpu.roll(ddw, tp - 2, 0), 0.0)
        dcc = w2 * ddw + w1 * u1 + w0 * u2
        dc_ref[...] = (dcc * h).astype(BF16)
        dh_ref[...] = (dcc * c).astype(BF16)

    col = pl.BlockSpec((tp, LANES), lambda t, b: (b, t))
    out = jax.ShapeDtypeStruct((nb_seq * tp, CONV_WIDTH), BF16)
    return pl.pallas_call(
        body,
        name="conv_bwd",
        grid=(cols, nb_seq),
        in_specs=_conv_specs(tp, lambda t, b, off: (b, off + t)) + [
            pl.BlockSpec((tp, LANES), lambda t, b: (b, N_HEADS * D_V // LANES + t)),
            pl.BlockSpec((8, LANES), lambda t, b: (0, t)),
            pl.BlockSpec((1, LANES), lambda t, b: (0, t))],
        out_specs=(col, col, col, col,
                   pl.BlockSpec((8, LANES), lambda t, b: (0, t)), pl.BlockSpec((1, LANES), lambda t, b: (0, t))),
        out_shape=(out, out, out, out,
                   jax.ShapeDtypeStruct((8, CONV_WIDTH), F32), jax.ShapeDtypeStruct((1, CONV_WIDTH), F32)),
        compiler_params=_params("arbitrary", "arbitrary"),
    )(p, p, p, p, dcat, conv_w, g_conv)


def _in_proj_bwd_x(dps, w_in, bm):
    r, kb = dps[0].shape
    d = w_in.shape[0]

    def body(*refs):
        dp_refs, w_ref, o_ref = refs[:len(dps)], refs[len(dps)], refs[len(dps) + 1]
        acc = _dot(dp_refs[0][...], w_ref[:, 0:kb], _NT)
        for j in range(1, len(dps)):
            acc = acc + _dot(dp_refs[j][...], w_ref[:, kb * j:kb * (j + 1)], _NT)
        o_ref[...] = acc

    return pl.pallas_call(
        body,
        name="in_proj_bwd_x",
        grid=(r // bm,),
        in_specs=[pl.BlockSpec((bm, kb), lambda i: (i, 0)) for _ in dps] + [pl.BlockSpec(w_in.shape, lambda i: (0, 0))],
        out_specs=pl.BlockSpec((bm, d), lambda i: (i, 0)),
        out_shape=jax.ShapeDtypeStruct((r, d), F32),
        compiler_params=_params("parallel"),
    )(*dps, w_in)


def _in_proj_bwd_w(u, dps, bm):
    r, d = u.shape
    kb = dps[0].shape[1]
    last = r // bm - 1

    def body(*refs):
        u_ref, dp_refs, o_ref, acc_ref = refs[0], refs[1:1 + len(dps)], refs[1 + len(dps)], refs[2 + len(dps)]

        @pl.when(pl.program_id(0) == 0)
        def _():
            acc_ref[...] = jnp.zeros_like(acc_ref)

        uu = u_ref[...]
        for j in range(len(dps)):
            acc_ref[:, kb * j:kb * (j + 1)] += _dot(uu, dp_refs[j][...], _TN)

        @pl.when(pl.program_id(0) == last)
        def _():
            for k in range(N_DEV):
                for s, e, c0 in _in_pieces(k):
                    o_ref[k, :, s:e] = acc_ref[:, c0:c0 + e - s].astype(BF16)
                o_ref[k, :, SHARD_IN:] = jnp.zeros((d, SHARD_IN_PAD - SHARD_IN), BF16)

    return pl.pallas_call(
        body,
        name="in_proj_bwd_w",
        grid=(r // bm,),
        in_specs=[pl.BlockSpec((bm, d), lambda i: (i, 0))] + [pl.BlockSpec((bm, kb), lambda i: (i, 0)) for _ in dps],
        out_specs=pl.BlockSpec((N_DEV, d, SHARD_IN_PAD), lambda i: (0, 0, 0)),
        out_shape=jax.ShapeDtypeStruct((N_DEV, d, SHARD_IN_PAD), BF16),
        scratch_shapes=[pltpu.VMEM((d, kb * len(dps)), F32)],
        compiler_params=_params("arbitrary"),
    )(u, *dps)


def _input_bwd(x, meta, du, dh, norm_g):
    nb_seq, s, d = x.shape
    nb = s // LANES + 1

    def body(x_ref, meta_ref, du_ref, dh_ref, g_ref, gx_ref, dmeta_ref, dg_ref):
        b, j = pl.program_id(0), pl.program_id(1)
        g = g_ref[...]

        @pl.when((b == 0) & (j == 0))
        def _():
            dmeta_ref[...] = jnp.zeros_like(dmeta_ref)
            dg_ref[...] = jnp.zeros_like(dg_ref)

        def bwd(h0, du_, dh_):
            hhat, r = _rms_stats(h0)
            dg_ref[...] += jnp.sum(du_ * hhat, axis=0, keepdims=True)
            return _rms_bwd(g * du_, hhat, r) + dh_

        @pl.when(j == 0)
        def _():
            dmeta_ref[...] += bwd(meta_ref[...], du_ref[PAD_FRONT:LANES, :], dh_ref[PAD_FRONT:LANES, :])

        @pl.when(j > 0)
        def _():
            gx_ref[0] = bwd(x_ref[0], du_ref[...], dh_ref[...])

    tok = pl.BlockSpec((1, LANES, d), lambda b, j: (b, jnp.maximum(j - 1, 0), 0))
    row = pl.BlockSpec((LANES, d), lambda b, j: (b * nb + j, 0))
    vec = pl.BlockSpec((1, d), lambda b, j: (0, 0))
    return pl.pallas_call(
        body,
        name="input_bwd",
        grid=(nb_seq, nb),
        in_specs=[tok, pl.BlockSpec((N_META, d), lambda b, j: (0, 0)), row, row, vec],
        out_specs=(tok, pl.BlockSpec((N_META, d), lambda b, j: (0, 0)), vec),
        out_shape=(
            jax.ShapeDtypeStruct((nb_seq, s, d), F32),
            jax.ShapeDtypeStruct((N_META, d), F32),
            jax.ShapeDtypeStruct((1, d), F32),
        ),
        compiler_params=_params("arbitrary", "arbitrary"),
    )(x, meta, du, dh, norm_g)


def _local_step(x, loss_target, u, p, meta_f, norm_g, w_in_p, q_norm_g, w_q_p, kv_norm_g, w_kv_p, conv_w_f,
                attn_out_g, conv_out_g, w_out_f, g_final):
    nb_seq, s, d = x.shape
    tp = s + LANES
    ht = tp // 2
    tables = _rope_tables(tp)

    q, k, v = _qkv_fwd(p, w_q_p, w_kv_p, q_norm_g, kv_norm_g, tables, nb_seq, tp)
    ya, o, lse = _attn_fwd(q, k, v, p, attn_out_g)
    yc = _conv_fwd(p, conv_w_f, conv_out_g, nb_seq, tp)
    mix = _out_proj(ya, yc, w_out_f, ht)
    dh, dhb, d_final_g, loss_part = _loss_bwd(x, mix, loss_target, g_final)

    dcat, d_w_out = _out_proj_bwd(dhb, w_out_f, ya, yc, ht)
    dq, dk, dv, dz_attn, d_attn_g = _attn_bwd(q, k, v, o, lse, dcat, p, attn_out_g)
    dpa, d_wq_p, d_wkv_p, d_gq, d_gkv = _qkv_bwd(p, dq, dk, dv, w_q_p, w_kv_p, q_norm_g, kv_norm_g, tables)
    d_b, d_c, d_h, dz_conv, d_conv_w, d_conv_g = _conv_bwd(p, dcat, conv_w_f, conv_out_g, nb_seq, tp)
    dps = (dpa, dz_attn, d_b, d_c, d_h, dz_conv)
    du = _in_proj_bwd_x(dps, w_in_p, ht)
    send_in = _in_proj_bwd_w(u, dps, ht // 2)
    grad_x, d_meta, d_norm_g = _input_bwd(x, meta_f, du, dh, norm_g)
    send_out = d_w_out.reshape(N_DEV, SHARD_OUT, d)
    small = (d_wq_p, d_wkv_p, d_conv_w, d_meta, d_norm_g, d_final_g, d_gq, d_gkv, d_attn_g, d_conv_g, loss_part)
    return grad_x, send_in, send_out, small


def kernel(x, meta_tokens, norm_g, w_in, q_norm_g, w_q_up, kv_norm_g, w_kv_up, conv_w, attn_out_g, conv_out_g, w_out, final_norm_g, loss_target, m_meta_tokens, m_norm_g, m_w_in, m_q_norm_g, m_w_q_up, m_kv_norm_g, m_w_kv_up, m_conv_w, m_attn_out_g, m_conv_out_g, m_w_out, m_final_norm_g, v_meta_tokens, v_norm_g, v_w_in, v_q_norm_g, v_w_q_up, v_kv_norm_g, v_w_kv_up, v_conv_w, v_attn_out_g, v_conv_out_g, v_w_out, v_final_norm_g):
    d = x.shape[-1]
    ht = (x.shape[1] + LANES) // 2
    u, w_in_p, meta_f = _prep_gather(x, meta_tokens, norm_g, w_in[0])
    p, w_q_p, w_kv_p, w_out_f, conv_w_f = _in_proj_gather(
        u, w_in_p, w_q_up[0], w_kv_up[0], w_out[0], conv_w[0], ht, GRP_A)
    g_final = final_norm_g.reshape(1, d)
    grad_x, send_in, send_out, small = _local_step(
        x, loss_target, u, p, meta_f, norm_g, w_in_p, q_norm_g, w_q_p, kv_norm_g, w_kv_p, conv_w_f,
        attn_out_g, conv_out_g, w_out_f, g_final)

    flat = lambda a: a.reshape(a.shape[-2:]) if a.ndim == 3 else a.reshape(1, -1) if a.ndim == 1 else a
    params = {
        "meta_tokens": (meta_tokens, m_meta_tokens, v_meta_tokens),
        "norm_g": (norm_g, m_norm_g, v_norm_g),
        "w_in": (w_in, m_w_in, v_w_in),
        "q_norm_g": (q_norm_g, m_q_norm_g, v_q_norm_g),
        "w_q_up": (w_q_up, m_w_q_up, v_w_q_up),
        "kv_norm_g": (kv_norm_g, m_kv_norm_g, v_kv_norm_g),
        "w_kv_up": (w_kv_up, m_w_kv_up, v_w_kv_up),
        "conv_w": (conv_w, m_conv_w, v_conv_w),
        "attn_out_g": (attn_out_g, m_attn_out_g, v_attn_out_g),
        "conv_out_g": (conv_out_g, m_conv_out_g, v_conv_out_g),
        "w_out": (w_out, m_w_out, v_w_out),
        "final_norm_g": (final_norm_g, m_final_norm_g, v_final_norm_g),
    }
    grads, loss = _reduce_grads(send_in, send_out, small)
    updated = _adamw(grads, {n: tuple(flat(a) for a in t) for n, t in params.items()})
    outs = [[updated[n][i].reshape(params[n][0].shape) for n, _ in PARAM_SHAPES] for i in range(4)]
    return (loss[0, 0], grad_x, *outs[0], *outs[1], *outs[2], *outs[3])
```

```python
import functools

import jax
import jax.numpy as jnp
from jax import lax
from jax.experimental import pallas as pl
from jax.experimental.pallas import tpu as pltpu

F32 = jnp.float32
BF16 = jnp.bfloat16

N_META = 16
D_MODEL = 1024
N_HEADS = 4
D_NOPE = 128
D_ROPE = 64
D_V = 128
Q_RANK = 256
KV_RANK = 128
CONV_WIDTH = 512
CONV_GROUP = 64
ROPE_THETA = 10000.0
ATTN_SCALE = (D_NOPE + D_ROPE) ** -0.5
EPS = 1e-6
NEG_INF = -1e30

ADAM_LR = 0.001
ADAM_B1 = 0.9
ADAM_B2 = 0.999
ADAM_EPS = 1e-08
ADAM_WD = 0.01
ADAM_STEP = 10

LANES = 128
PAD_FRONT = LANES - N_META
KV_TILE = 256
N_DEV = 8
VMEM_LIMIT = 56 * 1024 * 1024

IN_PAD = 3072
GRP_A = 512
N_A = Q_RANK + KV_RANK + D_ROPE
IN_PROJ = 3008
SHARD_IN = IN_PROJ // N_DEV
SHARD_IN_PAD = 384
SHARD_Q = 96
SHARD_KV = 128
SHARD_OUT = 128
SHARD_CONV = 64
SHARD_META = 128
Q_COLS = N_HEADS * (D_NOPE + D_ROPE)
KV_COLS = N_HEADS * (D_NOPE + D_V)

ROW_Q, ROW_KV, ROW_META, ROW_CONV = 0, 256, 384, 400
ROW_REPL = 408
ROW_NORM, ROW_FINAL, ROW_GQ, ROW_GKV, ROW_ATTN, ROW_CONVG, ROW_LOSS = 408, 416, 424, 426, 427, 431, 435
SMALL_ROWS = 440
ADAM_CHUNK = 64

PARAM_SHAPES = (
    ("meta_tokens", (N_META, SHARD_META)), ("norm_g", (1, D_MODEL)), ("w_in", (D_MODEL, SHARD_IN)),
    ("q_norm_g", (1, Q_RANK)), ("w_q_up", (Q_RANK, SHARD_Q)), ("kv_norm_g", (1, KV_RANK)),
    ("w_kv_up", (KV_RANK, SHARD_KV)), ("conv_w", (3, SHARD_CONV)), ("attn_out_g", (1, CONV_WIDTH)),
    ("conv_out_g", (1, CONV_WIDTH)), ("w_out", (SHARD_OUT, D_MODEL)), ("final_norm_g", (1, D_MODEL)),
)


def _in_pieces(k):
    lo, hi = SHARD_IN * k, SHARD_IN * (k + 1)
    out = []
    if lo < N_A:
        out.append((0, min(hi, N_A) - lo, lo))
    if hi > N_A:
        s = max(lo, N_A)
        out.append((s - lo, hi - lo, s + GRP_A - N_A))
    return out


def _q_pieces(k):
    lo, hi = SHARD_Q * k, SHARD_Q * (k + 1)
    out = []
    for h in range(N_HEADS):
        base = (D_NOPE + D_ROPE) * h
        s, e = max(lo, base), min(hi, base + D_NOPE)
        if s < e:
            out.append((s - lo, e - lo, D_NOPE * h + s - base))
        s, e = max(lo, base + D_NOPE), min(hi, base + D_NOPE + D_ROPE)
        if s < e:
            out.append((s - lo, e - lo, N_HEADS * D_NOPE + D_ROPE * h + s - base - D_NOPE))
    return out


def _kv_dst(k):
    return D_NOPE * (k // 2) + (N_HEADS * D_NOPE if k % 2 else 0)


def _params(*sem):
    return pltpu.CompilerParams(dimension_semantics=sem, vmem_limit_bytes=VMEM_LIMIT)


def _rms_stats(x):
    r = lax.rsqrt(jnp.mean(x * x, axis=-1, keepdims=True) + EPS)
    return x * r, r


def _rms_bwd(gdy, xhat, r):
    return r * (gdy - xhat * jnp.mean(gdy * xhat, axis=-1, keepdims=True))


def _sigmoid(z):
    return 1.0 / (1.0 + jnp.exp(-z))


def _group_mean(x):
    i0 = lax.broadcasted_iota(jnp.int32, (LANES, LANES), 0) // CONV_GROUP
    i1 = lax.broadcasted_iota(jnp.int32, (LANES, LANES), 1) // CONV_GROUP
    m = jnp.where(i0 == i1, 1.0 / CONV_GROUP, 0.0).astype(BF16)
    hi = x.astype(BF16)
    lo = (x - hi.astype(F32)).astype(BF16)
    return jnp.dot(hi, m, preferred_element_type=F32) + jnp.dot(lo, m, preferred_element_type=F32)


_NT = (((1,), (1,)), ((), ()))
_TN = (((0,), (0,)), ((), ()))


def _dot(a, b, dims=None):
    if dims is None:
        return jnp.dot(a, b, preferred_element_type=F32)
    return lax.dot_general(a, b, dims, preferred_element_type=F32)


def _device_position():
    x, y, c = lax.axis_index("x"), lax.axis_index("y"), lax.axis_index("c")
    return x, y, c, 4 * x + 2 * y + c


def _gather_plan(srcs, slots, send_sems, recv_sems, local_sems):
    x, y, c, _ = _device_position()
    me, sibling = (x, y, c), (x, y, 1 - c)
    chips = [(1 - x, y), (x, 1 - y), (1 - x, 1 - y)]
    n = len(srcs)

    def slot(a, px, py, pc):
        return slots[a].at[4 * px + 2 * py + pc]

    def copy(a, k, block, to, own=False):
        return pltpu.make_async_remote_copy(
            src_ref=srcs[a] if own else slot(a, *block),
            dst_ref=slot(a, *block),
            send_sem=send_sems.at[7 * a + k],
            recv_sem=recv_sems.at[7 * a + k],
            device_id=to,
            device_id_type=pl.DeviceIdType.MESH,
        )

    def local(a):
        return pltpu.make_async_copy(srcs[a], slot(a, *me), local_sems.at[a])

    def firsts(a):
        return [copy(a, 0, me, sibling, own=True)] + [
            copy(a, 1 + j, me, (*chip, c), own=True) for j, chip in enumerate(chips)]

    def start():
        for a in range(n):
            local(a).start()
            for cp in firsts(a):
                cp.start()

    def forward():
        for j, chip in enumerate(chips):
            for a in range(n):
                copy(a, 1 + j, (*chip, c), me).wait_recv()
                copy(a, 4 + j, (*chip, c), sibling).start()

    def finish():
        for a in range(n):
            copy(a, 0, sibling, me).wait_recv()
            for j, chip in enumerate(chips):
                copy(a, 4 + j, (*chip, 1 - c), me).wait_recv()
        for a in range(n):
            for cp in firsts(a) + [copy(a, 4 + j, (*chip, c), sibling) for j, chip in enumerate(chips)]:
                cp.wait_send()
            local(a).wait()

    return start, forward, finish


def _adam_update(g, w, m, v):
    m_new = ADAM_B1 * m + (1.0 - ADAM_B1) * g
    v_new = ADAM_B2 * v + (1.0 - ADAM_B2) * (g * g)
    m_hat = m_new / (1.0 - ADAM_B1 ** ADAM_STEP)
    v_hat = v_new / (1.0 - ADAM_B2 ** ADAM_STEP)
    return -ADAM_LR * (m_hat / (jnp.sqrt(v_hat) + ADAM_EPS) + ADAM_WD * w), m_new, v_new


def _adamw(grads, params):
    names = [n for n, _ in PARAM_SHAPES]
    n_p = len(names)

    def body(*refs):
        for i in range(n_p):
            g = refs[i][...]
            w, m, v = (refs[n_p + 3 * i + j][...] for j in range(3))
            delta, m_new, v_new = _adam_update(g, w, m, v)
            for j, val in enumerate((g, delta, m_new, v_new)):
                refs[4 * n_p + 4 * i + j][...] = val

    vm = pl.BlockSpec(memory_space=pltpu.VMEM)
    out_shape = []
    for _, shape in PARAM_SHAPES:
        out_shape += [jax.ShapeDtypeStruct(shape, F32)] * 4
    outs = pl.pallas_call(
        body,
        name="adamw",
        out_shape=tuple(out_shape),
        in_specs=[vm] * (4 * n_p),
        out_specs=(vm,) * (4 * n_p),
        compiler_params=pltpu.CompilerParams(vmem_limit_bytes=VMEM_LIMIT),
    )(*[grads[n] for n in names], *[a for n in names for a in params[n]])
    return {n: outs[4 * i:4 * i + 4] for i, n in enumerate(names)}


N_CHIPS = 4


def _reduce_plan(pays, owns, r1s, sums, r2s, send1, recv1, send2, recv2, local_sems):
    x, y, c, _ = _device_position()
    sibling = (x, y, 1 - c)
    chips = [((1 - x if rj & 2 else x), (1 - y if rj & 1 else y)) for rj in range(N_CHIPS)]
    n = len(pays)

    def slot_of(rj, core):
        return 4 * chips[rj][0] + 2 * chips[rj][1] + core

    def to_sibling(a, rj):
        return pltpu.make_async_remote_copy(
            src_ref=pays[a].at[slot_of(rj, 1 - c)], dst_ref=r1s[a].at[rj],
            send_sem=send1.at[N_CHIPS * a + rj], recv_sem=recv1.at[N_CHIPS * a + rj],
            device_id=sibling, device_id_type=pl.DeviceIdType.MESH)

    def load_own(a, rj):
        return pltpu.make_async_copy(pays[a].at[slot_of(rj, c)], owns[a].at[rj], local_sems.at[2 * N_CHIPS * a + rj])

    def to_chip(a, rj):
        return pltpu.make_async_remote_copy(
            src_ref=sums[a].at[rj], dst_ref=r2s[a].at[rj],
            send_sem=send2.at[N_CHIPS * a + rj], recv_sem=recv2.at[N_CHIPS * a + rj],
            device_id=(*chips[rj], c), device_id_type=pl.DeviceIdType.MESH)

    def keep(a):
        return pltpu.make_async_copy(sums[a].at[0], r2s[a].at[0], local_sems.at[2 * N_CHIPS * a + N_CHIPS])

    def start():
        for a in range(n):
            for rj in range(N_CHIPS):
                to_sibling(a, rj).start()
                if owns[a] is not None:
                    load_own(a, rj).start()

    def combine():
        for a in range(n):
            for rj in range(N_CHIPS):
                to_sibling(a, rj).wait_recv()
                if owns[a] is not None:
                    load_own(a, rj).wait()
                    mine = owns[a][rj]
                else:
                    mine = pays[a][slot_of(rj, c)]
                sums[a][rj] = (mine.astype(F32) + r1s[a][rj].astype(F32)).astype(sums[a].dtype)
            keep(a).start()
            for rj in range(1, N_CHIPS):
                to_chip(a, rj).start()

    def finish():
        for a in range(n):
            for rj in range(1, N_CHIPS):
                to_chip(a, rj).wait_recv()
            for rj in range(N_CHIPS):
                to_sibling(a, rj).wait_send()
            for rj in range(1, N_CHIPS):
                to_chip(a, rj).wait_send()
            keep(a).wait()

    return start, combine, finish


def _reduce_scratch(shapes_dtypes, own_flags):
    out = []
    for (shape, dtype), own in zip(shapes_dtypes, own_flags):
        if own:
            out.append(pltpu.VMEM((N_CHIPS,) + shape, dtype))
        out += [pltpu.VMEM((N_CHIPS,) + shape, dtype), pltpu.VMEM((N_CHIPS,) + shape, dtype)]
    n = len(shapes_dtypes)
    out += [pltpu.SemaphoreType.DMA((N_CHIPS * n,))] * 4 + [pltpu.SemaphoreType.DMA((2 * N_CHIPS * n,))]
    return out


def _pack_small(ssmall, dwq, dwkv, dconv, dfinal, dgq, dgkv, dattn, dconvg, loss_part):
    ssmall[...] = jnp.zeros_like(ssmall)
    rep = ssmall.at[0]
    for i in range(D_MODEL // LANES):
        rep[ROW_FINAL + i:ROW_FINAL + i + 1, :] = dfinal[:, LANES * i:LANES * (i + 1)]
    for i in range(Q_RANK // LANES):
        rep[ROW_GQ + i:ROW_GQ + i + 1, :] = dgq[:, LANES * i:LANES * (i + 1)]
    rep[ROW_GKV:ROW_GKV + 1, :] = dgkv[...]
    for i in range(CONV_WIDTH // LANES):
        rep[ROW_ATTN + i:ROW_ATTN + i + 1, :] = dattn[:, LANES * i:LANES * (i + 1)]
        rep[ROW_CONVG + i:ROW_CONVG + i + 1, :] = dconvg[:, LANES * i:LANES * (i + 1)]
    rep[ROW_LOSS:ROW_LOSS + 1, :] = loss_part[...]
    for k in range(N_DEV):
        if k:
            ssmall[k, ROW_REPL:, :] = ssmall[0, ROW_REPL:, :]
        for s, e, d in _q_pieces(k):
            ssmall[k, ROW_Q:ROW_Q + Q_RANK, s:e] = dwq[:, d:d + e - s]
        ssmall[k, ROW_KV:ROW_KV + KV_RANK, :] = dwkv[:, _kv_dst(k):_kv_dst(k) + SHARD_KV]
        ssmall[k, ROW_CONV:ROW_CONV + 3, 0:SHARD_CONV] = dconv[0:3, SHARD_CONV * k:SHARD_CONV * (k + 1)]


TAIL_ROWS = N_META + D_MODEL // LANES


def _reduce_tail(r_in, r_out, r_small, d_meta, d_norm):
    n_p = len(PARAM_SHAPES)
    names = [n for n, _ in PARAM_SHAPES]

    def body(*refs):
        rin, rout, rsmall, dmeta, dnorm = refs[:5]
        g_out = {n: refs[5 + i] for i, n in enumerate(names)}
        loss_out = refs[5 + n_p]
        stail, rtail, gsum, gtail, send_sems, recv_sems = refs[6 + n_p:]
        x, y, c, me = _device_position()
        my_chip = 2 * x + y

        for k in range(N_DEV):
            stail[k, 0:N_META, :] = dmeta[:, SHARD_META * k:SHARD_META * (k + 1)]
            for i in range(D_MODEL // LANES):
                stail[k, N_META + i:N_META + i + 1, :] = dnorm[:, LANES * i:LANES * (i + 1)]
        copies = []
        for r in range(1, N_DEV):
            peer = (1 - x if r & 4 else x, 1 - y if r & 2 else y, 1 - c if r & 1 else c)
            copies.append(pltpu.make_async_remote_copy(
                src_ref=stail.at[4 * peer[0] + 2 * peer[1] + peer[2]],
                dst_ref=rtail.at[r],
                send_sem=send_sems.at[r - 1],
                recv_sem=recv_sems.at[r - 1],
                device_id=peer,
                device_id_type=pl.DeviceIdType.MESH,
            ))
        for cp in copies:
            cp.start()
        rtail[0] = stail[me]

        def chunk(i, carry):
            sl = pl.ds(pl.multiple_of(i * ADAM_CHUNK, ADAM_CHUNK), ADAM_CHUNK)
            g = rin[my_chip, sl, :].astype(F32)
            for ch in range(1, N_CHIPS):
                g = g + rin[ch ^ my_chip, sl, :].astype(F32)
            g_out["w_in"][sl, :] = g[:, :SHARD_IN]
            return carry

        lax.fori_loop(0, D_MODEL // ADAM_CHUNK, chunk, 0)

        g = rout[my_chip].astype(F32)
        gs = rsmall[my_chip]
        for ch in range(1, N_CHIPS):
            g = g + rout[ch ^ my_chip].astype(F32)
            gs = gs + rsmall[ch ^ my_chip]
        g_out["w_out"][...] = g
        gsum[...] = gs
        g_out["w_q_up"][...] = gsum[ROW_Q:ROW_Q + Q_RANK, 0:SHARD_Q]
        g_out["w_kv_up"][...] = gsum[ROW_KV:ROW_KV + KV_RANK, :]
        g_out["conv_w"][...] = gsum[ROW_CONV:ROW_CONV + 3, 0:SHARD_CONV]
        for name, row, width in (("final_norm_g", ROW_FINAL, D_MODEL), ("q_norm_g", ROW_GQ, Q_RANK),
                                 ("kv_norm_g", ROW_GKV, KV_RANK), ("attn_out_g", ROW_ATTN, CONV_WIDTH),
                                 ("conv_out_g", ROW_CONVG, CONV_WIDTH)):
            for i in range(width // LANES):
                g_out[name][:, LANES * i:LANES * (i + 1)] = gsum[row + i:row + i + 1, :]
        loss_out[...] = gsum[ROW_LOSS:ROW_LOSS + 1, :]

        for cp in copies:
            cp.wait_recv()
        gt = rtail[me]
        for d in range(1, N_DEV):
            gt = gt + rtail[d ^ me]
        gtail[...] = gt
        g_out["meta_tokens"][...] = gtail[0:N_META, :]
        for i in range(D_MODEL // LANES):
            g_out["norm_g"][:, LANES * i:LANES * (i + 1)] = gtail[N_META + i:N_META + i + 1, :]
        for cp in copies:
            cp.wait_send()

    vm = pl.BlockSpec(memory_space=pltpu.VMEM)
    out_shape = [jax.ShapeDtypeStruct(shape, F32) for _, shape in PARAM_SHAPES]
    out_shape.append(jax.ShapeDtypeStruct((1, LANES), F32))
    outs = pl.pallas_call(
        body,
        name="reduce_tail",
        out_shape=tuple(out_shape),
        in_specs=[vm] * 5,
        out_specs=(vm,) * len(out_shape),
        scratch_shapes=[
            pltpu.VMEM((N_DEV, TAIL_ROWS, LANES), F32),
            pltpu.VMEM((N_DEV, TAIL_ROWS, LANES), F32),
            pltpu.VMEM((SMALL_ROWS, LANES), F32),
            pltpu.VMEM((TAIL_ROWS, LANES), F32),
            pltpu.SemaphoreType.DMA((N_DEV - 1,)),
            pltpu.SemaphoreType.DMA((N_DEV - 1,)),
        ],
        compiler_params=pltpu.CompilerParams(vmem_limit_bytes=VMEM_LIMIT),
    )(r_in, r_out, r_small, d_meta, d_norm)
    return {n: outs[i] for i, n in enumerate(names)}, outs[-1]


def _prep_gather(x, meta, norm_g, w_in):
    nb_seq, s, d = x.shape
    nb = s // LANES + 1
    forward_step = nb_seq * (nb // 3)
    finish_step = nb_seq * (nb - 1)

    def body(x_ref, meta_ref, g_ref, win_ref, u_ref, w_in_p, meta_f,
             sbig, ssmall, gbig, gsmall, send_sems, recv_sems, local_sems):
        jj, b = pl.program_id(0), pl.program_id(1)
        t = jj * nb_seq + b

        def plan():
            return _gather_plan((sbig, ssmall), (gbig, gsmall), send_sems, recv_sems, local_sems)

        @pl.when(t == 0)
        def _():
            sbig[:, 0:SHARD_IN] = win_ref[...].astype(BF16)
            sbig[:, SHARD_IN:] = jnp.zeros((d, SHARD_IN_PAD - SHARD_IN), BF16)
            ssmall[...] = meta_ref[...]
            plan()[0]()

        @pl.when(t == forward_step)
        def _():
            plan()[1]()

        @pl.when(t == finish_step)
        def _():
            plan()[2]()
            w_in_p[:, N_A:GRP_A] = jnp.zeros((d, GRP_A - N_A), BF16)
            for k in range(N_DEV):
                for s0, e0, d0 in _in_pieces(k):
                    w_in_p[:, d0:d0 + e0 - s0] = gbig[k, :, s0:e0]
                meta_f[:, SHARD_META * k:SHARD_META * (k + 1)] = gsmall[k]

        def norm(h):
            hhat, _ = _rms_stats(h)
            return (hhat * g_ref[...]).astype(BF16)

        @pl.when(jj < nb - 1)
        def _():
            u_ref[...] = norm(x_ref[0])

        @pl.when(jj == nb - 1)
        def _():
            u_ref[0:PAD_FRONT, :] = jnp.zeros((PAD_FRONT, d), BF16)
            u_ref[PAD_FRONT:LANES, :] = norm(meta_f[...])

    whole = lambda shape: pl.BlockSpec(shape, lambda jj, b: (0,) * len(shape))
    return pl.pallas_call(
        body,
        name="prep_norm_gather",
        grid=(nb, nb_seq),
        in_specs=[
            pl.BlockSpec((1, LANES, d), lambda jj, b: (b, jnp.minimum(jj, nb - 2), 0)),
            whole(meta.shape), whole(norm_g.shape), whole(w_in.shape),
        ],
        out_specs=(pl.BlockSpec((LANES, d), lambda jj, b: (b * nb + (jj + 1) % nb, 0)),
                   whole((d, IN_PAD)), whole((N_META, d))),
        out_shape=(jax.ShapeDtypeStruct((nb_seq * nb * LANES, d), BF16),
                   jax.ShapeDtypeStruct((d, IN_PAD), BF16),
                   jax.ShapeDtypeStruct((N_META, d), F32)),
        scratch_shapes=[
            pltpu.VMEM((d, SHARD_IN_PAD), BF16),
            pltpu.VMEM((N_META, SHARD_META), F32),
            pltpu.VMEM((N_DEV, d, SHARD_IN_PAD), BF16),
            pltpu.VMEM((N_DEV, N_META, SHARD_META), F32),
            pltpu.SemaphoreType.DMA((14,)),
            pltpu.SemaphoreType.DMA((14,)),
            pltpu.SemaphoreType.DMA((2,)),
        ],
        compiler_params=_params("arbitrary", "arbitrary"),
    )(x, meta, norm_g, w_in)


def _in_proj_gather(u, w_in_p, w_q, w_kv, w_out, conv_w, bm, bn):
    m, k_dim = u.shape
    n = w_in_p.shape[1]
    steps = (m // bm) * (n // bn)
    forward_step = steps // 3
    qkv_rows = Q_RANK + KV_RANK

    def body(a_ref, b_ref, wq_ref, wkv_ref, wout_ref, conv_ref, o_ref, w_q_p, w_kv_p, w_out_f, conv_f,
             sqkv, sout, sconv, gqkv, gout, gconv, send_sems, recv_sems, local_sems):
        t = pl.program_id(0) * (n // bn) + pl.program_id(1)

        def plan():
            return _gather_plan((sqkv, sout, sconv), (gqkv, gout, gconv), send_sems, recv_sems, local_sems)

        @pl.when(t == 0)
        def _():
            sqkv[...] = jnp.zeros_like(sqkv)
            sqkv[0:Q_RANK, 0:SHARD_Q] = wq_ref[...].astype(BF16)
            sqkv[Q_RANK:, :] = wkv_ref[...].astype(BF16)
            sout[...] = wout_ref[...].astype(BF16)
            sconv[...] = jnp.zeros_like(sconv)
            sconv[0:3, 0:SHARD_CONV] = conv_ref[...]
            plan()[0]()

        @pl.when(t == forward_step)
        def _():
            plan()[1]()

        o_ref[...] = _dot(a_ref[...], b_ref[...]).astype(o_ref.dtype)

        @pl.when(t == steps - 1)
        def _():
            plan()[2]()
            conv_f[...] = jnp.zeros_like(conv_f)
            for k in range(N_DEV):
                for s0, e0, d0 in _q_pieces(k):
                    w_q_p[:, d0:d0 + e0 - s0] = gqkv[k, 0:Q_RANK, s0:e0]
                w_kv_p[:, _kv_dst(k):_kv_dst(k) + SHARD_KV] = gqkv[k, Q_RANK:, :]
                w_out_f[SHARD_OUT * k:SHARD_OUT * (k + 1), :] = gout[k]
                conv_f[0:3, SHARD_CONV * k:SHARD_CONV * (k + 1)] = gconv[k, 0:3, 0:SHARD_CONV]

    whole = lambda shape: pl.BlockSpec(shape, lambda i, j: (0,) * len(shape))
    return pl.pallas_call(
        body,
        name="in_proj_gather",
        grid=(m // bm, n // bn),
        in_specs=[pl.BlockSpec((bm, k_dim), lambda i, j: (i, 0)), pl.BlockSpec((k_dim, bn), lambda i, j: (0, j)),
                  whole(w_q.shape), whole(w_kv.shape), whole(w_out.shape), whole(conv_w.shape)],
        out_specs=(pl.BlockSpec((bm, bn), lambda i, j: (i, j)),
                   whole((Q_RANK, Q_COLS)), whole((KV_RANK, KV_COLS)), whole((D_MODEL, D_MODEL)),
                   whole((8, CONV_WIDTH))),
        out_shape=(jax.ShapeDtypeStruct((m, n), BF16),
                   jax.ShapeDtypeStruct((Q_RANK, Q_COLS), BF16),
                   jax.ShapeDtypeStruct((KV_RANK, KV_COLS), BF16),
                   jax.ShapeDtypeStruct((D_MODEL, D_MODEL), BF16),
                   jax.ShapeDtypeStruct((8, CONV_WIDTH), F32)),
        scratch_shapes=[
            pltpu.VMEM((qkv_rows, LANES), BF16),
            pltpu.VMEM((SHARD_OUT, D_MODEL), BF16),
            pltpu.VMEM((8, LANES), F32),
            pltpu.VMEM((N_DEV, qkv_rows, LANES), BF16),
            pltpu.VMEM((N_DEV, SHARD_OUT, D_MODEL), BF16),
            pltpu.VMEM((N_DEV, 8, LANES), F32),
            pltpu.SemaphoreType.DMA((21,)),
            pltpu.SemaphoreType.DMA((21,)),
            pltpu.SemaphoreType.DMA((3,)),
        ],
        compiler_params=_params("arbitrary", "arbitrary"),
    )(u, w_in_p, w_q, w_kv, w_out, conv_w)


def _rope_tables(tp):
    half = D_ROPE // 2
    inv_freq = 1.0 / (ROPE_THETA ** (jnp.arange(half, dtype=F32) / half))
    pos = (jnp.arange(tp) - PAD_FRONT).astype(F32)
    ang = pos[:, None] * inv_freq[None, :]
    cos = jnp.tile(jnp.cos(ang), (1, LANES // half))
    sin = jnp.tile(jnp.sin(ang), (1, LANES // half))
    first = (jnp.arange(LANES) % D_ROPE) < half
    return cos, jnp.where(first, -sin, 0.0), jnp.where(first, 0.0, sin)


def _rope(t, cos, sa, sb):
    return t * cos + pltpu.roll(t, LANES - D_ROPE // 2, 1) * sa + pltpu.roll(t, D_ROPE // 2, 1) * sb


def _rope_t(t, cos, sa, sb):
    return t * cos + pltpu.roll(t * sa, D_ROPE // 2, 1) + pltpu.roll(t * sb, LANES - D_ROPE // 2, 1)


def _qkv_fwd(p, wq, wkv, gq, gkv, tables, nb_seq, tp):
    ht = tp // 2

    def body(pa_ref, wq_ref, wkv_ref, gq_ref, gkv_ref, cos_ref, sa_ref, sb_ref, q_ref, k_ref, v_ref):
        pa = pa_ref[...].astype(F32)
        cq_hat, _ = _rms_stats(pa[:, :Q_RANK])
        ckv_hat, _ = _rms_stats(pa[:, Q_RANK:Q_RANK + KV_RANK])
        q = _dot((cq_hat * gq_ref[...]).astype(BF16), wq_ref[...])
        kv = _dot((ckv_hat * gkv_ref[...]).astype(BF16), wkv_ref[...])
        tabs = (cos_ref[...], sa_ref[...], sb_ref[...])
        lane = lax.broadcasted_iota(jnp.int32, (ht, LANES), 1)
        low = lane < D_ROPE
        mark = lane == D_ROPE
        row = (pl.program_id(0) % 2) * ht + lax.broadcasted_iota(jnp.int32, (ht, LANES), 0)
        k_pe = jnp.where(mark & (row < PAD_FRONT), NEG_INF, _rope(pa[:, Q_RANK + KV_RANK:], *tabs))
        one = jnp.where(mark & (row >= PAD_FRONT), 1.0, 0.0)
        pairs = [_rope(q[:, N_HEADS * D_NOPE + LANES * i:N_HEADS * D_NOPE + LANES * (i + 1)], *tabs) for i in range(2)]
        for h in range(N_HEADS):
            pair = pairs[h // 2]
            if h % 2:
                pair = pltpu.roll(pair, D_ROPE, 1)
            pe = jnp.where(low, pair, one)
            q_ref[0, h] = jnp.concatenate([q[:, D_NOPE * h:D_NOPE * (h + 1)], pe], axis=1).astype(BF16)
            k_ref[0, h] = jnp.concatenate([kv[:, D_NOPE * h:D_NOPE * (h + 1)], k_pe], axis=1).astype(BF16)
            v_ref[0, h] = kv[:, N_HEADS * D_NOPE + D_V * h:N_HEADS * D_NOPE + D_V * (h + 1)].astype(BF16)

    full = lambda a: pl.BlockSpec(a.shape, lambda i: (0,) * a.ndim)
    tab = pl.BlockSpec((ht, LANES), lambda i: (i % 2, 0))
    qk = pl.BlockSpec((1, N_HEADS, ht, 2 * LANES), lambda i: (i // 2, 0, i % 2, 0))
    return pl.pallas_call(
        body,
        name="qkv_fwd",
        grid=(2 * nb_seq,),
        in_specs=[pl.BlockSpec((ht, GRP_A), lambda i: (i, 0)), full(wq), full(wkv), full(gq), full(gkv), tab, tab, tab],
        out_specs=(qk, qk, pl.BlockSpec((1, N_HEADS, ht, D_V), lambda i: (i // 2, 0, i % 2, 0))),
        out_shape=(
            jax.ShapeDtypeStruct((nb_seq, N_HEADS, tp, 2 * LANES), BF16),
            jax.ShapeDtypeStruct((nb_seq, N_HEADS, tp, 2 * LANES), BF16),
            jax.ShapeDtypeStruct((nb_seq, N_HEADS, tp, D_V), BF16),
        ),
        compiler_params=_params("parallel"),
    )(p, wq, wkv, gq, gkv, *tables)


def _attn_fwd(q, k, v, p, g_attn):
    nb_seq, _, tp, _ = q.shape

    def body(q_ref, k_ref, v_ref, z_ref, g_ref, y_ref, o_ref, lse_ref):
        g = g_ref[...]
        for r0 in range(0, tp, KV_TILE):
            nq = min(KV_TILE, tp - r0)
            kend = r0 + nq
            qq = q_ref[0, 0, r0:kend, :]
            sd = _dot(qq, k_ref[0, 0, r0:kend, :], _NT) * ATTN_SCALE
            causal = (lax.broadcasted_iota(jnp.int32, (nq, nq), 1) <= lax.broadcasted_iota(jnp.int32, (nq, nq), 0))
            sd = jnp.where(causal, sd, NEG_INF)
            m = jnp.max(sd, axis=-1, keepdims=True)
            if r0:
                so = _dot(qq, k_ref[0, 0, 0:r0, :], _NT) * ATTN_SCALE
                m = jnp.maximum(m, jnp.max(so, axis=-1, keepdims=True))
            ed = jnp.exp(sd - m)
            l = jnp.sum(ed, axis=-1, keepdims=True)
            o = _dot(ed.astype(BF16), v_ref[0, 0, r0:kend, :])
            if r0:
                eo = jnp.exp(so - m)
                l = l + jnp.sum(eo, axis=-1, keepdims=True)
                o = o + _dot(eo.astype(BF16), v_ref[0, 0, 0:r0, :])
            o = o * (1.0 / l)
            o_ref[0, 0, r0:kend, :] = o
            lse_ref[0, 0, r0:kend, :] = jnp.broadcast_to(m + jnp.log(l), (nq, LANES))
            ohat, _ = _rms_stats(o)
            z = z_ref[r0:kend, :].astype(F32)
            y_ref[r0:kend, :] = (ohat * g * (z * _sigmoid(z))).astype(BF16)

    qk = pl.BlockSpec((1, 1, tp, 2 * LANES), lambda b, h: (b, h, 0, 0))
    hv = pl.BlockSpec((1, 1, tp, D_V), lambda b, h: (b, h, 0, 0))
    return pl.pallas_call(
        body,
        name="attn_fwd",
        grid=(nb_seq, N_HEADS),
        in_specs=[qk, qk, hv,
                  pl.BlockSpec((tp, LANES), lambda b, h: (b, GRP_A // LANES + h)),
                  pl.BlockSpec((1, LANES), lambda b, h: (0, h))],
        out_specs=(pl.BlockSpec((tp, LANES), lambda b, h: (b, h)), hv, hv),
        out_shape=(
            jax.ShapeDtypeStruct((nb_seq * tp, N_HEADS * D_V), BF16),
            jax.ShapeDtypeStruct((nb_seq, N_HEADS, tp, D_V), F32),
            jax.ShapeDtypeStruct((nb_seq, N_HEADS, tp, LANES), F32),
        ),
        compiler_params=_params("parallel", "parallel"),
    )(q, k, v, p, g_attn)


_CONV_COL0 = (GRP_A + N_HEADS * D_V) // LANES


def _conv_specs(tp, order):
    cols = CONV_WIDTH // LANES
    return [pl.BlockSpec((tp, LANES), functools.partial(
        lambda a, b, off: order(a, b, off), off=_CONV_COL0 + i * cols)) for i in range(4)]


def _conv_fwd(p, conv_w, g_conv, nb_seq, tp):
    def body(b_ref, c_ref, h_ref, z_ref, w_ref, g_ref, y_ref):
        cc = c_ref[...].astype(F32) * h_ref[...].astype(F32)
        row = lax.broadcasted_iota(jnp.int32, (tp, LANES), 0)
        s1 = jnp.where(row >= 1, pltpu.roll(cc, 1, 0), 0.0)
        s2 = jnp.where(row >= 2, pltpu.roll(cc, 2, 0), 0.0)
        yc = b_ref[...].astype(F32) * (w_ref[0:1, :] * s2 + w_ref[1:2, :] * s1 + w_ref[2:3, :] * cc)
        r = lax.rsqrt(_group_mean(yc * yc) + EPS)
        z = z_ref[...].astype(F32)
        y_ref[...] = (yc * r * g_ref[...] * (z * _sigmoid(z))).astype(BF16)

    return pl.pallas_call(
        body,
        name="conv_fwd",
        grid=(nb_seq, CONV_WIDTH // LANES),
        in_specs=_conv_specs(tp, lambda b, t, off: (b, off + t)) + [
            pl.BlockSpec((8, LANES), lambda b, t: (0, t)),
            pl.BlockSpec((1, LANES), lambda b, t: (0, t))],
        out_specs=pl.BlockSpec((tp, LANES), lambda b, t: (b, t)),
        out_shape=jax.ShapeDtypeStruct((nb_seq * tp, CONV_WIDTH), BF16),
        compiler_params=_params("parallel", "parallel"),
    )(p, p, p, p, conv_w, g_conv)


def _out_proj(ya, yc, w_out, bm):
    r, ka = ya.shape
    d = w_out.shape[1]

    def body(a_ref, c_ref, w_ref, o_ref):
        o_ref[...] = _dot(a_ref[...], w_ref[0:ka, :]) + _dot(c_ref[...], w_ref[ka:, :])

    return pl.pallas_call(
        body,
        name="out_proj",
        grid=(r // bm,),
        in_specs=[pl.BlockSpec((bm, ka), lambda i: (i, 0)), pl.BlockSpec((bm, yc.shape[1]), lambda i: (i, 0)),
                  pl.BlockSpec(w_out.shape, lambda i: (0, 0))],
        out_specs=pl.BlockSpec((bm, d), lambda i: (i, 0)),
        out_shape=jax.ShapeDtypeStruct((r, d), F32),
        compiler_params=_params("parallel"),
    )(ya, yc, w_out)


def _loss_bwd(x, mix, target, g_final):
    nb_seq, s, d = x.shape
    nb = s // LANES + 1

    def body(x_ref, mix_ref, t_ref, g_ref, dh_ref, dhb_ref, dg_ref, loss_ref, acc_ref):
        b, j = pl.program_id(0), pl.program_id(1)

        @pl.when((b == 0) & (j == 0))
        def _():
            acc_ref[...] = jnp.zeros_like(acc_ref)
            dg_ref[...] = jnp.zeros_like(dg_ref)

        @pl.when(j == 0)
        def _():
            dh_ref[...] = jnp.zeros_like(dh_ref)
            dhb_ref[...] = jnp.zeros_like(dhb_ref)

        @pl.when(j > 0)
        def _():
            g = g_ref[...]
            hhat, r = _rms_stats(x_ref[0] + mix_ref[...])
            e = hhat * g - t_ref[0]
            acc_ref[...] += jnp.sum(e * e, axis=0, keepdims=True)
            dy = e * (1.0 / d)
            dg_ref[...] += jnp.sum(dy * hhat, axis=0, keepdims=True)
            dh = _rms_bwd(g * dy, hhat, r)
            dh_ref[...] = dh
            dhb_ref[...] = dh.astype(BF16)

        @pl.when((b == nb_seq - 1) & (j == nb - 1))
        def _():
            total = jnp.sum(acc_ref[...], axis=1, keepdims=True)
            loss_ref[...] = jnp.broadcast_to((0.5 / d) * total, loss_ref.shape)

    tok = pl.BlockSpec((1, LANES, d), lambda b, j: (b, jnp.maximum(j - 1, 0), 0))
    row = pl.BlockSpec((LANES, d), lambda b, j: (b * nb + j, 0))
    vec = pl.BlockSpec((1, d), lambda b, j: (0, 0))
    return pl.pallas_call(
        body,
        name="loss_bwd",
        grid=(nb_seq, nb),
        in_specs=[tok, row, tok, vec],
        out_specs=(row, row, vec, pl.BlockSpec((1, LANES), lambda b, j: (0, 0))),
        out_shape=(
            jax.ShapeDtypeStruct((nb_seq * nb * LANES, d), F32),
            jax.ShapeDtypeStruct((nb_seq * nb * LANES, d), BF16),
            jax.ShapeDtypeStruct((1, d), F32),
            jax.ShapeDtypeStruct((1, LANES), F32),
        ),
        scratch_shapes=[pltpu.VMEM((1, d), F32)],
        compiler_params=_params("arbitrary", "arbitrary"),
    )(x, mix, target, g_final)


def _out_proj_bwd(dhb, w_out, ya, yc, bm):
    r, d = dhb.shape
    ka = ya.shape[1]
    n_mix = w_out.shape[0]
    last = r // bm - 1

    def body(dh_ref, w_ref, a_ref, c_ref, dcat_ref, dw_ref, acc_ref):
        @pl.when(pl.program_id(0) == 0)
        def _():
            acc_ref[...] = jnp.zeros_like(acc_ref)

        dh = dh_ref[...]
        dcat_ref[...] = _dot(dh, w_ref[...], _NT).astype(BF16)
        acc_ref[0:ka, :] += _dot(a_ref[...], dh, _TN)
        acc_ref[ka:, :] += _dot(c_ref[...], dh, _TN)

        @pl.when(pl.program_id(0) == last)
        def _():
            dw_ref[...] = acc_ref[...].astype(BF16)

    return pl.pallas_call(
        body,
        name="out_proj_bwd",
        grid=(r // bm,),
        in_specs=[pl.BlockSpec((bm, d), lambda i: (i, 0)), pl.BlockSpec(w_out.shape, lambda i: (0, 0)),
                  pl.BlockSpec((bm, ka), lambda i: (i, 0)), pl.BlockSpec((bm, yc.shape[1]), lambda i: (i, 0))],
        out_specs=(pl.BlockSpec((bm, n_mix), lambda i: (i, 0)),
                   pl.BlockSpec((n_mix, d), lambda i: (0, 0))),
        out_shape=(jax.ShapeDtypeStruct((r, n_mix), BF16),
                   jax.ShapeDtypeStruct((n_mix, d), BF16)),
        scratch_shapes=[pltpu.VMEM((n_mix, d), F32)],
        compiler_params=_params("arbitrary"),
    )(dhb, w_out, ya, yc)


def _attn_bwd(q, k, v, o, lse, dcat, p, g_attn, send_out):
    nb_seq, _, tp, _ = q.shape
    steps = N_HEADS * nb_seq

    def body(q_ref, k_ref, v_ref, o_ref, lse_ref, dy_ref, z_ref, g_ref, pay_ref,
             dq_ref, dk_ref, dv_ref, dz_ref, dg_ref, r2_ref, dq_acc, r1, sums, *sems):
        t = pl.program_id(0) * nb_seq + pl.program_id(1)

        def plan():
            return _reduce_plan((pay_ref,), (None,), (r1,), (sums,), (r2_ref,), *sems)

        @pl.when(t == 0)
        def _():
            plan()[0]()

        @pl.when(t == 1)
        def _():
            plan()[1]()

        @pl.when(pl.program_id(1) == 0)
        def _():
            dg_ref[...] = jnp.zeros_like(dg_ref)

        g = g_ref[...]
        z = z_ref[...].astype(F32)
        o = o_ref[0, 0]
        dy = dy_ref[...].astype(F32)
        sig = _sigmoid(z)
        ohat, r = _rms_stats(o)
        don = dy * (z * sig)
        dz_ref[...] = (dy * (ohat * g) * (sig * (1.0 + z * (1.0 - sig)))).astype(BF16)
        dg_ref[...] += jnp.sum(don * ohat, axis=0, keepdims=True)
        do = _rms_bwd(g * don, ohat, r)
        dvec = jnp.sum(do * o, axis=-1, keepdims=True)
        dob = do.astype(BF16)
        lse_col = lse_ref[0, 0, :, 0:1]
        dq_acc[...] = jnp.zeros_like(dq_acc)
        for k0 in range(0, tp, KV_TILE):
            nk = min(KV_TILE, tp - k0)
            nq = tp - k0
            qq = q_ref[0, 0, k0:, :]
            kk = k_ref[0, 0, k0:k0 + nk, :]
            causal = (lax.broadcasted_iota(jnp.int32, (nq, nk), 1) <= lax.broadcasted_iota(jnp.int32, (nq, nk), 0))
            pr = jnp.where(causal, jnp.exp(_dot(qq, kk, _NT) * ATTN_SCALE - lse_col[k0:]), 0.0)
            dp = _dot(dob[k0:], v_ref[0, 0, k0:k0 + nk, :], _NT)
            ds = (pr * (dp - dvec[k0:]) * ATTN_SCALE).astype(BF16)
            dv_ref[0, 0, k0:k0 + nk, :] = _dot(pr.astype(BF16), dob[k0:], _TN).astype(BF16)
            dk_ref[0, 0, k0:k0 + nk, :] = _dot(ds, qq, _TN).astype(BF16)
            dq_acc[k0:, :] += _dot(ds, kk)
        dq_ref[0, 0] = dq_acc[...].astype(BF16)

        @pl.when(t == steps - 1)
        def _():
            plan()[2]()

    qk = pl.BlockSpec((1, 1, tp, 2 * LANES), lambda h, b: (b, h, 0, 0))
    hv = pl.BlockSpec((1, 1, tp, D_V), lambda h, b: (b, h, 0, 0))
    col = pl.BlockSpec((tp, LANES), lambda h, b: (b, h))
    slot = send_out.shape[1:]
    return pl.pallas_call(
        body,
        name="attn_bwd",
        grid=(N_HEADS, nb_seq),
        in_specs=[qk, qk, hv, hv, hv, col,
                  pl.BlockSpec((tp, LANES), lambda h, b: (b, GRP_A // LANES + h)),
                  pl.BlockSpec((1, LANES), lambda h, b: (0, h)),
                  pl.BlockSpec(send_out.shape, lambda h, b: (0, 0, 0))],
        out_specs=(qk, qk, hv, col, pl.BlockSpec((1, LANES), lambda h, b: (0, h)),
                   pl.BlockSpec(memory_space=pl.ANY)),
        out_shape=(
            jax.ShapeDtypeStruct((nb_seq, N_HEADS, tp, 2 * LANES), BF16),
            jax.ShapeDtypeStruct((nb_seq, N_HEADS, tp, 2 * LANES), BF16),
            jax.ShapeDtypeStruct((nb_seq, N_HEADS, tp, D_V), BF16),
            jax.ShapeDtypeStruct((nb_seq * tp, N_HEADS * D_V), BF16),
            jax.ShapeDtypeStruct((1, N_HEADS * D_V), F32),
            jax.ShapeDtypeStruct((N_CHIPS,) + slot, BF16),
        ),
        scratch_shapes=[pltpu.VMEM((tp, 2 * LANES), F32)] + _reduce_scratch([(slot, BF16)], [False]),
        compiler_params=_params("arbitrary", "arbitrary"),
    )(q, k, v, o, lse, dcat, p, g_attn, send_out)


def _qkv_bwd(p, dq, dk, dv, wq, wkv, gq, gkv, tables):
    nb_seq, _, tp, _ = dq.shape
    ht = tp // 2

    def body(pa_ref, dq_ref, dk_ref, dv_ref, wq_ref, wkv_ref, gq_ref, gkv_ref, cos_ref, sa_ref, sb_ref,
             dpa_ref, dwq_ref, dwkv_ref, dgq_ref, dgkv_ref):
        @pl.when(pl.program_id(0) == 0)
        def _():
            dwq_ref[...] = jnp.zeros_like(dwq_ref)
            dwkv_ref[...] = jnp.zeros_like(dwkv_ref)
            dgq_ref[...] = jnp.zeros_like(dgq_ref)
            dgkv_ref[...] = jnp.zeros_like(dgkv_ref)

        pa = pa_ref[...].astype(F32)
        gq, gkv = gq_ref[...], gkv_ref[...]
        cq_hat, rq = _rms_stats(pa[:, :Q_RANK])
        ckv_hat, rkv = _rms_stats(pa[:, Q_RANK:Q_RANK + KV_RANK])
        tabs = (cos_ref[...], sa_ref[...], sb_ref[...])

        pe = [dq_ref[0, h, :, D_NOPE:].astype(F32) for h in range(N_HEADS)]
        pairs = [_rope_t(pe[2 * i] + pltpu.roll(pe[2 * i + 1], D_ROPE, 1), *tabs).astype(BF16) for i in range(2)]
        dq_flat = jnp.concatenate([dq_ref[0, h, :, :D_NOPE] for h in range(N_HEADS)] + pairs, axis=1)
        dwq_ref[...] += _dot((cq_hat * gq).astype(BF16), dq_flat, _TN)
        dcqn = _dot(dq_flat, wq_ref[...], _NT)
        dgq_ref[...] += jnp.sum(dcqn * cq_hat, axis=0, keepdims=True)
        dcq = _rms_bwd(gq * dcqn, cq_hat, rq)

        dkv_flat = jnp.concatenate([dk_ref[0, h, :, :D_NOPE] for h in range(N_HEADS)]
                                   + [dv_ref[0, h] for h in range(N_HEADS)], axis=1)
        dwkv_ref[...] += _dot((ckv_hat * gkv).astype(BF16), dkv_flat, _TN)
        dckvn = _dot(dkv_flat, wkv_ref[...], _NT)
        dgkv_ref[...] += jnp.sum(dckvn * ckv_hat, axis=0, keepdims=True)
        dckv = _rms_bwd(gkv * dckvn, ckv_hat, rkv)

        dk_pe = dk_ref[0, 0, :, D_NOPE:].astype(F32)
        for h in range(1, N_HEADS):
            dk_pe = dk_pe + dk_ref[0, h, :, D_NOPE:].astype(F32)
        dk_pe = jnp.where(lax.broadcasted_iota(jnp.int32, (ht, LANES), 1) < D_ROPE, dk_pe, 0.0)
        dpa_ref[...] = jnp.concatenate([dcq, dckv, _rope_t(dk_pe, *tabs)], axis=1).astype(BF16)

    full = lambda a: pl.BlockSpec(a.shape, lambda i: (0,) * a.ndim)
    tab = pl.BlockSpec((ht, LANES), lambda i: (i % 2, 0))
    qk = pl.BlockSpec((1, N_HEADS, ht, 2 * LANES), lambda i: (i // 2, 0, i % 2, 0))
    acc = lambda shape: pl.BlockSpec(shape, lambda i: (0, 0))
    return pl.pallas_call(
        body,
        name="qkv_bwd",
        grid=(2 * nb_seq,),
        in_specs=[pl.BlockSpec((ht, GRP_A), lambda i: (i, 0)), qk, qk,
                  pl.BlockSpec((1, N_HEADS, ht, D_V), lambda i: (i // 2, 0, i % 2, 0)),
                  full(wq), full(wkv), full(gq), full(gkv), tab, tab, tab],
        out_specs=(pl.BlockSpec((ht, GRP_A), lambda i: (i, 0)),
                   acc(wq.shape), acc(wkv.shape), acc((1, Q_RANK)), acc((1, KV_RANK))),
        out_shape=(
            jax.ShapeDtypeStruct((nb_seq * tp, GRP_A), BF16),
            jax.ShapeDtypeStruct(wq.shape, F32),
            jax.ShapeDtypeStruct(wkv.shape, F32),
            jax.ShapeDtypeStruct((1, Q_RANK), F32),
            jax.ShapeDtypeStruct((1, KV_RANK), F32),
        ),
        compiler_params=_params("arbitrary"),
    )(p, dq, dk, dv, wq, wkv, gq, gkv, *tables)


def _conv_bwd(p, dcat, conv_w, g_conv, nb_seq, tp):
    cols = CONV_WIDTH // LANES

    def body(b_ref, c_ref, h_ref, z_ref, dy_ref, w_ref, g_ref,
             db_ref, dc_ref, dh_ref, dz_ref, dw_ref, dg_ref):
        @pl.when(pl.program_id(1) == 0)
        def _():
            dw_ref[...] = jnp.zeros_like(dw_ref)
            dg_ref[...] = jnp.zeros_like(dg_ref)

        cb, c, h = b_ref[...].astype(F32), c_ref[...].astype(F32), h_ref[...].astype(F32)
        z, dy = z_ref[...].astype(F32), dy_ref[...].astype(F32)
        g = g_ref[...]
        w0, w1, w2 = w_ref[0:1, :], w_ref[1:2, :], w_ref[2:3, :]
        cc = c * h
        row = lax.broadcasted_iota(jnp.int32, (tp, LANES), 0)
        s1 = jnp.where(row >= 1, pltpu.roll(cc, 1, 0), 0.0)
        s2 = jnp.where(row >= 2, pltpu.roll(cc, 2, 0), 0.0)
        dwc = w0 * s2 + w1 * s1 + w2 * cc
        yc = cb * dwc
        r = lax.rsqrt(_group_mean(yc * yc) + EPS)
        ychat = yc * r
        sig = _sigmoid(z)
        dz_ref[...] = (dy * (ychat * g) * (sig * (1.0 + z * (1.0 - sig)))).astype(BF16)
        dyn = dy * (z * sig)
        dg_ref[...] += jnp.sum(dyn * ychat, axis=0, keepdims=True)
        gd = g * dyn
        dyc = r * (gd - ychat * _group_mean(gd * ychat))
        db_ref[...] = (dyc * dwc).astype(BF16)
        ddw = dyc * cb
        dw_ref[0:1, :] += jnp.sum(ddw * s2, axis=0, keepdims=True)
        dw_ref[1:2, :] += jnp.sum(ddw * s1, axis=0, keepdims=True)
        dw_ref[2:3, :] += jnp.sum(ddw * cc, axis=0, keepdims=True)
        u1 = jnp.where(row <= tp - 2, pltpu.roll(ddw, tp - 1, 0), 0.0)
        u2 = jnp.where(row <= tp - 3, pltpu.roll(ddw, tp - 2, 0), 0.0)
        dcc = w2 * ddw + w1 * u1 + w0 * u2
        dc_ref[...] = (dcc * h).astype(BF16)
        dh_ref[...] = (dcc * c).astype(BF16)

    col = pl.BlockSpec((tp, LANES), lambda t, b: (b, t))
    out = jax.ShapeDtypeStruct((nb_seq * tp, CONV_WIDTH), BF16)
    return pl.pallas_call(
        body,
        name="conv_bwd",
        grid=(cols, nb_seq),
        in_specs=_conv_specs(tp, lambda t, b, off: (b, off + t)) + [
            pl.BlockSpec((tp, LANES), lambda t, b: (b, N_HEADS * D_V // LANES + t)),
            pl.BlockSpec((8, LANES), lambda t, b: (0, t)),
            pl.BlockSpec((1, LANES), lambda t, b: (0, t))],
        out_specs=(col, col, col, col,
                   pl.BlockSpec((8, LANES), lambda t, b: (0, t)), pl.BlockSpec((1, LANES), lambda t, b: (0, t))),
        out_shape=(out, out, out, out,
                   jax.ShapeDtypeStruct((8, CONV_WIDTH), F32), jax.ShapeDtypeStruct((1, CONV_WIDTH), F32)),
        compiler_params=_params("arbitrary", "arbitrary"),
    )(p, p, p, p, dcat, conv_w, g_conv)


def _in_proj_bwd_x(dps, w_in, bm, send_in, small_grads):
    r, kb = dps[0].shape
    d = w_in.shape[0]
    steps = r // bm
    n_dp, n_small = len(dps), len(small_grads)
    in_slot, small_slot = send_in.shape[1:], (SMALL_ROWS, LANES)

    def body(*refs):
        dp_refs, w_ref, pay_ref = refs[:n_dp], refs[n_dp], refs[n_dp + 1]
        small_refs = refs[n_dp + 2:n_dp + 2 + n_small]
        o_ref, r2_in, r2_small = refs[n_dp + 2 + n_small:n_dp + 5 + n_small]
        ssmall, own_in, r1_in, sum_in, r1_small, sum_small = refs[n_dp + 5 + n_small:n_dp + 11 + n_small]
        sems = refs[n_dp + 11 + n_small:]
        t = pl.program_id(0)

        def plan():
            return _reduce_plan((pay_ref, ssmall), (own_in, None), (r1_in, r1_small), (sum_in, sum_small),
                                (r2_in, r2_small), *sems)

        @pl.when(t == 0)
        def _():
            _pack_small(ssmall, *small_refs)
            plan()[0]()

        @pl.when(t == 1)
        def _():
            plan()[1]()

        acc = _dot(dp_refs[0][...], w_ref[:, 0:kb], _NT)
        for j in range(1, n_dp):
            acc = acc + _dot(dp_refs[j][...], w_ref[:, kb * j:kb * (j + 1)], _NT)
        o_ref[...] = acc

        @pl.when(t == steps - 1)
        def _():
            plan()[2]()

    whole = lambda a: pl.BlockSpec(a.shape, lambda i: (0,) * a.ndim)
    hbm = pl.BlockSpec(memory_space=pl.ANY)
    return pl.pallas_call(
        body,
        name="in_proj_bwd_x",
        grid=(steps,),
        in_specs=[pl.BlockSpec((bm, kb), lambda i: (i, 0)) for _ in dps] + [whole(w_in), hbm]
        + [whole(a) for a in small_grads],
        out_specs=(pl.BlockSpec((bm, d), lambda i: (i, 0)), hbm, hbm),
        out_shape=(jax.ShapeDtypeStruct((r, d), F32),
                   jax.ShapeDtypeStruct((N_CHIPS,) + in_slot, BF16),
                   jax.ShapeDtypeStruct((N_CHIPS,) + small_slot, F32)),
        scratch_shapes=[pltpu.VMEM((N_DEV,) + small_slot, F32)]
        + _reduce_scratch([(in_slot, BF16), (small_slot, F32)], [True, False]),
        compiler_params=_params("arbitrary"),
    )(*dps, w_in, send_in, *small_grads)


def _in_proj_bwd_w(u, dps, bm):
    r, d = u.shape
    kb = dps[0].shape[1]
    last = r // bm - 1

    def body(*refs):
        u_ref, dp_refs, o_ref, acc_ref = refs[0], refs[1:1 + len(dps)], refs[1 + len(dps)], refs[2 + len(dps)]

        @pl.when(pl.program_id(0) == 0)
        def _():
            acc_ref[...] = jnp.zeros_like(acc_ref)

        uu = u_ref[...]
        for j in range(len(dps)):
            acc_ref[:, kb * j:kb * (j + 1)] += _dot(uu, dp_refs[j][...], _TN)

        @pl.when(pl.program_id(0) == last)
        def _():
            for k in range(N_DEV):
                for s, e, c0 in _in_pieces(k):
                    o_ref[k, :, s:e] = acc_ref[:, c0:c0 + e - s].astype(BF16)
                o_ref[k, :, SHARD_IN:] = jnp.zeros((d, SHARD_IN_PAD - SHARD_IN), BF16)

    return pl.pallas_call(
        body,
        name="in_proj_bwd_w",
        grid=(r // bm,),
        in_specs=[pl.BlockSpec((bm, d), lambda i: (i, 0))] + [pl.BlockSpec((bm, kb), lambda i: (i, 0)) for _ in dps],
        out_specs=pl.BlockSpec((N_DEV, d, SHARD_IN_PAD), lambda i: (0, 0, 0)),
        out_shape=jax.ShapeDtypeStruct((N_DEV, d, SHARD_IN_PAD), BF16),
        scratch_shapes=[pltpu.VMEM((d, kb * len(dps)), F32)],
        compiler_params=_params("arbitrary"),
    )(u, *dps)


def _input_bwd(x, meta, du, dh, norm_g):
    nb_seq, s, d = x.shape
    nb = s // LANES + 1

    def body(x_ref, meta_ref, du_ref, dh_ref, g_ref, gx_ref, dmeta_ref, dg_ref):
        b, j = pl.program_id(0), pl.program_id(1)
        g = g_ref[...]

        @pl.when((b == 0) & (j == 0))
        def _():
            dmeta_ref[...] = jnp.zeros_like(dmeta_ref)
            dg_ref[...] = jnp.zeros_like(dg_ref)

        def bwd(h0, du_, dh_):
            hhat, r = _rms_stats(h0)
            dg_ref[...] += jnp.sum(du_ * hhat, axis=0, keepdims=True)
            return _rms_bwd(g * du_, hhat, r) + dh_

        @pl.when(j == 0)
        def _():
            dmeta_ref[...] += bwd(meta_ref[...], du_ref[PAD_FRONT:LANES, :], dh_ref[PAD_FRONT:LANES, :])

        @pl.when(j > 0)
        def _():
            gx_ref[0] = bwd(x_ref[0], du_ref[...], dh_ref[...])

    tok = pl.BlockSpec((1, LANES, d), lambda b, j: (b, jnp.maximum(j - 1, 0), 0))
    row = pl.BlockSpec((LANES, d), lambda b, j: (b * nb + j, 0))
    vec = pl.BlockSpec((1, d), lambda b, j: (0, 0))
    return pl.pallas_call(
        body,
        name="input_bwd",
        grid=(nb_seq, nb),
        in_specs=[tok, pl.BlockSpec((N_META, d), lambda b, j: (0, 0)), row, row, vec],
        out_specs=(tok, pl.BlockSpec((N_META, d), lambda b, j: (0, 0)), vec),
        out_shape=(
            jax.ShapeDtypeStruct((nb_seq, s, d), F32),
            jax.ShapeDtypeStruct((N_META, d), F32),
            jax.ShapeDtypeStruct((1, d), F32),
        ),
        compiler_params=_params("arbitrary", "arbitrary"),
    )(x, meta, du, dh, norm_g)


def _local_step(x, loss_target, u, p, meta_f, norm_g, w_in_p, q_norm_g, w_q_p, kv_norm_g, w_kv_p, conv_w_f,
                attn_out_g, conv_out_g, w_out_f, g_final):
    nb_seq, s, d = x.shape
    tp = s + LANES
    ht = tp // 2
    tables = _rope_tables(tp)

    q, k, v = _qkv_fwd(p, w_q_p, w_kv_p, q_norm_g, kv_norm_g, tables, nb_seq, tp)
    ya, o, lse = _attn_fwd(q, k, v, p, attn_out_g)
    yc = _conv_fwd(p, conv_w_f, conv_out_g, nb_seq, tp)
    mix = _out_proj(ya, yc, w_out_f, ht)
    dh, dhb, d_final_g, loss_part = _loss_bwd(x, mix, loss_target, g_final)

    dcat, d_w_out = _out_proj_bwd(dhb, w_out_f, ya, yc, ht)
    send_out = d_w_out.reshape(N_DEV, SHARD_OUT, d)
    dq, dk, dv, dz_attn, d_attn_g, r_out = _attn_bwd(q, k, v, o, lse, dcat, p, attn_out_g, send_out)
    dpa, d_wq_p, d_wkv_p, d_gq, d_gkv = _qkv_bwd(p, dq, dk, dv, w_q_p, w_kv_p, q_norm_g, kv_norm_g, tables)
    d_b, d_c, d_h, dz_conv, d_conv_w, d_conv_g = _conv_bwd(p, dcat, conv_w_f, conv_out_g, nb_seq, tp)
    dps = (dpa, dz_attn, d_b, d_c, d_h, dz_conv)
    send_in = _in_proj_bwd_w(u, dps, ht // 2)
    small = (d_wq_p, d_wkv_p, d_conv_w, d_final_g, d_gq, d_gkv, d_attn_g, d_conv_g, loss_part)
    du, r_in, r_small = _in_proj_bwd_x(dps, w_in_p, ht // 2, send_in, small)
    grad_x, d_meta, d_norm_g = _input_bwd(x, meta_f, du, dh, norm_g)
    return grad_x, r_in, r_out, r_small, d_meta, d_norm_g


def kernel(x, meta_tokens, norm_g, w_in, q_norm_g, w_q_up, kv_norm_g, w_kv_up, conv_w, attn_out_g, conv_out_g, w_out, final_norm_g, loss_target, m_meta_tokens, m_norm_g, m_w_in, m_q_norm_g, m_w_q_up, m_kv_norm_g, m_w_kv_up, m_conv_w, m_attn_out_g, m_conv_out_g, m_w_out, m_final_norm_g, v_meta_tokens, v_norm_g, v_w_in, v_q_norm_g, v_w_q_up, v_kv_norm_g, v_w_kv_up, v_conv_w, v_attn_out_g, v_conv_out_g, v_w_out, v_final_norm_g):
    d = x.shape[-1]
    ht = (x.shape[1] + LANES) // 2
    u, w_in_p, meta_f = _prep_gather(x, meta_tokens, norm_g, w_in[0])
    p, w_q_p, w_kv_p, w_out_f, conv_w_f = _in_proj_gather(
        u, w_in_p, w_q_up[0], w_kv_up[0], w_out[0], conv_w[0], ht, GRP_A)
    g_final = final_norm_g.reshape(1, d)
    grad_x, r_in, r_out, r_small, d_meta, d_norm_g = _local_step(
        x, loss_target, u, p, meta_f, norm_g, w_in_p, q_norm_g, w_q_p, kv_norm_g, w_kv_p, conv_w_f,
        attn_out_g, conv_out_g, w_out_f, g_final)

    flat = lambda a: a.reshape(a.shape[-2:]) if a.ndim == 3 else a.reshape(1, -1) if a.ndim == 1 else a
    params = {
        "meta_tokens": (meta_tokens, m_meta_tokens, v_meta_tokens),
        "norm_g": (norm_g, m_norm_g, v_norm_g),
        "w_in": (w_in, m_w_in, v_w_in),
        "q_norm_g": (q_norm_g, m_q_norm_g, v_q_norm_g),
        "w_q_up": (w_q_up, m_w_q_up, v_w_q_up),
        "kv_norm_g": (kv_norm_g, m_kv_norm_g, v_kv_norm_g),
        "w_kv_up": (w_kv_up, m_w_kv_up, v_w_kv_up),
        "conv_w": (conv_w, m_conv_w, v_conv_w),
        "attn_out_g": (attn_out_g, m_attn_out_g, v_attn_out_g),
        "conv_out_g": (conv_out_g, m_conv_out_g, v_conv_out_g),
        "w_out": (w_out, m_w_out, v_w_out),
        "final_norm_g": (final_norm_g, m_final_norm_g, v_final_norm_g),
    }
    grads, loss = _reduce_tail(r_in, r_out, r_small, d_meta, d_norm_g)
    updated = _adamw(grads, {n: tuple(flat(a) for a in t) for n, t in params.items()})
    outs = [[updated[n][i].reshape(params[n][0].shape) for n, _ in PARAM_SHAPES] for i in range(4)]
    return (loss[0, 0], grad_x, *outs[0], *outs[1], *outs[2], *outs[3])
```

```python
import functools

import jax
import jax.numpy as jnp
from jax import lax
from jax.experimental import pallas as pl
from jax.experimental.pallas import tpu as pltpu

F32 = jnp.float32
BF16 = jnp.bfloat16

N_META = 16
D_MODEL = 1024
N_HEADS = 4
D_NOPE = 128
D_ROPE = 64
D_V = 128
Q_RANK = 256
KV_RANK = 128
CONV_WIDTH = 512
CONV_GROUP = 64
ROPE_THETA = 10000.0
ATTN_SCALE = (D_NOPE + D_ROPE) ** -0.5
EPS = 1e-6
NEG_INF = -1e30

ADAM_LR = 0.001
ADAM_B1 = 0.9
ADAM_B2 = 0.999
ADAM_EPS = 1e-08
ADAM_WD = 0.01
ADAM_STEP = 10

LANES = 128
PAD_FRONT = LANES - N_META
KV_TILE = 256
N_DEV = 8
VMEM_LIMIT = 56 * 1024 * 1024

IN_PAD = 3072
GRP_A = 512
N_A = Q_RANK + KV_RANK + D_ROPE
IN_PROJ = 3008
SHARD_IN = IN_PROJ // N_DEV
SHARD_IN_PAD = 384
SHARD_Q = 96
SHARD_KV = 128
SHARD_OUT = 128
SHARD_CONV = 64
SHARD_META = 128
Q_COLS = N_HEADS * (D_NOPE + D_ROPE)
KV_COLS = N_HEADS * (D_NOPE + D_V)

ROW_Q, ROW_KV, ROW_META, ROW_CONV = 0, 256, 384, 400
ROW_REPL = 408
ROW_NORM, ROW_FINAL, ROW_GQ, ROW_GKV, ROW_ATTN, ROW_CONVG, ROW_LOSS = 408, 416, 424, 426, 427, 431, 435
SMALL_ROWS = 440
ADAM_CHUNK = 64

PARAM_SHAPES = (
    ("meta_tokens", (N_META, SHARD_META)), ("norm_g", (1, D_MODEL)), ("w_in", (D_MODEL, SHARD_IN)),
    ("q_norm_g", (1, Q_RANK)), ("w_q_up", (Q_RANK, SHARD_Q)), ("kv_norm_g", (1, KV_RANK)),
    ("w_kv_up", (KV_RANK, SHARD_KV)), ("conv_w", (3, SHARD_CONV)), ("attn_out_g", (1, CONV_WIDTH)),
    ("conv_out_g", (1, CONV_WIDTH)), ("w_out", (SHARD_OUT, D_MODEL)), ("final_norm_g", (1, D_MODEL)),
)


def _in_pieces(k):
    lo, hi = SHARD_IN * k, SHARD_IN * (k + 1)
    out = []
    if lo < N_A:
        out.append((0, min(hi, N_A) - lo, lo))
    if hi > N_A:
        s = max(lo, N_A)
        out.append((s - lo, hi - lo, s + GRP_A - N_A))
    return out


def _q_pieces(k):
    lo, hi = SHARD_Q * k, SHARD_Q * (k + 1)
    out = []
    for h in range(N_HEADS):
        base = (D_NOPE + D_ROPE) * h
        s, e = max(lo, base), min(hi, base + D_NOPE)
        if s < e:
            out.append((s - lo, e - lo, D_NOPE * h + s - base))
        s, e = max(lo, base + D_NOPE), min(hi, base + D_NOPE + D_ROPE)
        if s < e:
            out.append((s - lo, e - lo, N_HEADS * D_NOPE + D_ROPE * h + s - base - D_NOPE))
    return out


def _kv_dst(k):
    return D_NOPE * (k // 2) + (N_HEADS * D_NOPE if k % 2 else 0)


def _params(*sem):
    return pltpu.CompilerParams(dimension_semantics=sem, vmem_limit_bytes=VMEM_LIMIT)


def _rms_stats(x):
    r = lax.rsqrt(jnp.mean(x * x, axis=-1, keepdims=True) + EPS)
    return x * r, r


def _rms_bwd(gdy, xhat, r):
    return r * (gdy - xhat * jnp.mean(gdy * xhat, axis=-1, keepdims=True))


def _sigmoid(z):
    return 1.0 / (1.0 + jnp.exp(-z))


def _group_mean(x):
    i0 = lax.broadcasted_iota(jnp.int32, (LANES, LANES), 0) // CONV_GROUP
    i1 = lax.broadcasted_iota(jnp.int32, (LANES, LANES), 1) // CONV_GROUP
    m = jnp.where(i0 == i1, 1.0 / CONV_GROUP, 0.0).astype(BF16)
    hi = x.astype(BF16)
    lo = (x - hi.astype(F32)).astype(BF16)
    return jnp.dot(hi, m, preferred_element_type=F32) + jnp.dot(lo, m, preferred_element_type=F32)


_NT = (((1,), (1,)), ((), ()))
_TN = (((0,), (0,)), ((), ()))


def _dot(a, b, dims=None):
    if dims is None:
        return jnp.dot(a, b, preferred_element_type=F32)
    return lax.dot_general(a, b, dims, preferred_element_type=F32)


def _device_position():
    x, y, c = lax.axis_index("x"), lax.axis_index("y"), lax.axis_index("c")
    return x, y, c, 4 * x + 2 * y + c


def _gather_plan(srcs, slots, send_sems, recv_sems, local_sems):
    x, y, c, _ = _device_position()
    me, sibling = (x, y, c), (x, y, 1 - c)
    chips = [(1 - x, y), (x, 1 - y), (1 - x, 1 - y)]
    n = len(srcs)

    def slot(a, px, py, pc):
        return slots[a].at[4 * px + 2 * py + pc]

    def copy(a, k, block, to, own=False):
        return pltpu.make_async_remote_copy(
            src_ref=srcs[a] if own else slot(a, *block),
            dst_ref=slot(a, *block),
            send_sem=send_sems.at[7 * a + k],
            recv_sem=recv_sems.at[7 * a + k],
            device_id=to,
            device_id_type=pl.DeviceIdType.MESH,
        )

    def local(a):
        return pltpu.make_async_copy(srcs[a], slot(a, *me), local_sems.at[a])

    def firsts(a):
        return [copy(a, 0, me, sibling, own=True)] + [
            copy(a, 1 + j, me, (*chip, c), own=True) for j, chip in enumerate(chips)]

    def start():
        for a in range(n):
            local(a).start()
            for cp in firsts(a):
                cp.start()

    def forward():
        for j, chip in enumerate(chips):
            for a in range(n):
                copy(a, 1 + j, (*chip, c), me).wait_recv()
                copy(a, 4 + j, (*chip, c), sibling).start()

    def finish():
        for a in range(n):
            copy(a, 0, sibling, me).wait_recv()
            for j, chip in enumerate(chips):
                copy(a, 4 + j, (*chip, 1 - c), me).wait_recv()
        for a in range(n):
            for cp in firsts(a) + [copy(a, 4 + j, (*chip, c), sibling) for j, chip in enumerate(chips)]:
                cp.wait_send()
            local(a).wait()

    return start, forward, finish


def _adam_update(g, w, m, v):
    m_new = ADAM_B1 * m + (1.0 - ADAM_B1) * g
    v_new = ADAM_B2 * v + (1.0 - ADAM_B2) * (g * g)
    m_hat = m_new / (1.0 - ADAM_B1 ** ADAM_STEP)
    v_hat = v_new / (1.0 - ADAM_B2 ** ADAM_STEP)
    return -ADAM_LR * (m_hat / (jnp.sqrt(v_hat) + ADAM_EPS) + ADAM_WD * w), m_new, v_new


def _adamw(grads, params):
    names = [n for n, _ in PARAM_SHAPES]
    n_p = len(names)

    def body(*refs):
        for i in range(n_p):
            g = refs[i][...]
            w, m, v = (refs[n_p + 3 * i + j][...] for j in range(3))
            delta, m_new, v_new = _adam_update(g, w, m, v)
            for j, val in enumerate((g, delta, m_new, v_new)):
                refs[4 * n_p + 4 * i + j][...] = val

    vm = pl.BlockSpec(memory_space=pltpu.VMEM)
    out_shape = []
    for _, shape in PARAM_SHAPES:
        out_shape += [jax.ShapeDtypeStruct(shape, F32)] * 4
    outs = pl.pallas_call(
        body,
        name="adamw",
        out_shape=tuple(out_shape),
        in_specs=[vm] * (4 * n_p),
        out_specs=(vm,) * (4 * n_p),
        compiler_params=pltpu.CompilerParams(vmem_limit_bytes=VMEM_LIMIT),
    )(*[grads[n] for n in names], *[a for n in names for a in params[n]])
    return {n: outs[4 * i:4 * i + 4] for i, n in enumerate(names)}


N_CHIPS = 4


def _reduce_plan(pays, owns, r1s, sums, r2s, send1, recv1, send2, recv2, local_sems):
    x, y, c, _ = _device_position()
    sibling = (x, y, 1 - c)
    chips = [((1 - x if rj & 2 else x), (1 - y if rj & 1 else y)) for rj in range(N_CHIPS)]
    n = len(pays)

    def slot_of(rj, core):
        return 4 * chips[rj][0] + 2 * chips[rj][1] + core

    def to_sibling(a, rj):
        return pltpu.make_async_remote_copy(
            src_ref=pays[a].at[slot_of(rj, 1 - c)], dst_ref=r1s[a].at[rj],
            send_sem=send1.at[N_CHIPS * a + rj], recv_sem=recv1.at[N_CHIPS * a + rj],
            device_id=sibling, device_id_type=pl.DeviceIdType.MESH)

    def load_own(a, rj):
        return pltpu.make_async_copy(pays[a].at[slot_of(rj, c)], owns[a].at[rj], local_sems.at[2 * N_CHIPS * a + rj])

    def to_chip(a, rj):
        return pltpu.make_async_remote_copy(
            src_ref=sums[a].at[rj], dst_ref=r2s[a].at[rj],
            send_sem=send2.at[N_CHIPS * a + rj], recv_sem=recv2.at[N_CHIPS * a + rj],
            device_id=(*chips[rj], c), device_id_type=pl.DeviceIdType.MESH)

    def keep(a):
        return pltpu.make_async_copy(sums[a].at[0], r2s[a].at[0], local_sems.at[2 * N_CHIPS * a + N_CHIPS])

    def start():
        for a in range(n):
            for rj in range(N_CHIPS):
                to_sibling(a, rj).start()
                if owns[a] is not None:
                    load_own(a, rj).start()

    def combine():
        for a in range(n):
            for rj in range(N_CHIPS):
                to_sibling(a, rj).wait_recv()
                if owns[a] is not None:
                    load_own(a, rj).wait()
                    mine = owns[a][rj]
                else:
                    mine = pays[a][slot_of(rj, c)]
                sums[a][rj] = (mine.astype(F32) + r1s[a][rj].astype(F32)).astype(sums[a].dtype)
            keep(a).start()
            for rj in range(1, N_CHIPS):
                to_chip(a, rj).start()

    def finish():
        for a in range(n):
            for rj in range(1, N_CHIPS):
                to_chip(a, rj).wait_recv()
            for rj in range(N_CHIPS):
                to_sibling(a, rj).wait_send()
            for rj in range(1, N_CHIPS):
                to_chip(a, rj).wait_send()
            keep(a).wait()

    return start, combine, finish


def _reduce_scratch(shapes_dtypes, own_flags):
    out = []
    for (shape, dtype), own in zip(shapes_dtypes, own_flags):
        if own:
            out.append(pltpu.VMEM((N_CHIPS,) + shape, dtype))
        out += [pltpu.VMEM((N_CHIPS,) + shape, dtype), pltpu.VMEM((N_CHIPS,) + shape, dtype)]
    n = len(shapes_dtypes)
    out += [pltpu.SemaphoreType.DMA((N_CHIPS * n,))] * 4 + [pltpu.SemaphoreType.DMA((2 * N_CHIPS * n,))]
    return out


def _pack_small(ssmall, dwq, dwkv, dconv, dfinal, dgq, dgkv, dattn, dconvg, loss_part):
    ssmall[...] = jnp.zeros_like(ssmall)
    rep = ssmall.at[0]
    for i in range(D_MODEL // LANES):
        rep[ROW_FINAL + i:ROW_FINAL + i + 1, :] = dfinal[:, LANES * i:LANES * (i + 1)]
    for i in range(Q_RANK // LANES):
        rep[ROW_GQ + i:ROW_GQ + i + 1, :] = dgq[:, LANES * i:LANES * (i + 1)]
    rep[ROW_GKV:ROW_GKV + 1, :] = dgkv[...]
    for i in range(CONV_WIDTH // LANES):
        rep[ROW_ATTN + i:ROW_ATTN + i + 1, :] = dattn[:, LANES * i:LANES * (i + 1)]
        rep[ROW_CONVG + i:ROW_CONVG + i + 1, :] = dconvg[:, LANES * i:LANES * (i + 1)]
    rep[ROW_LOSS:ROW_LOSS + 1, :] = loss_part[...]
    for k in range(N_DEV):
        if k:
            ssmall[k, ROW_REPL:, :] = ssmall[0, ROW_REPL:, :]
        for s, e, d in _q_pieces(k):
            ssmall[k, ROW_Q:ROW_Q + Q_RANK, s:e] = dwq[:, d:d + e - s]
        ssmall[k, ROW_KV:ROW_KV + KV_RANK, :] = dwkv[:, _kv_dst(k):_kv_dst(k) + SHARD_KV]
        ssmall[k, ROW_CONV:ROW_CONV + 3, 0:SHARD_CONV] = dconv[0:3, SHARD_CONV * k:SHARD_CONV * (k + 1)]


TOKEN_TILES = 4
TAIL_ROWS = N_META + D_MODEL // LANES


def _reduce_tail(r_in, r_out, r_small, d_meta, d_norm):
    n_p = len(PARAM_SHAPES)
    names = [n for n, _ in PARAM_SHAPES]

    def body(*refs):
        rin, rout, rsmall, dmeta, dnorm = refs[:5]
        g_out = {n: refs[5 + i] for i, n in enumerate(names)}
        loss_out = refs[5 + n_p]
        stail, rtail, gsum, gtail, send_sems, recv_sems = refs[6 + n_p:]
        x, y, c, me = _device_position()
        my_chip = 2 * x + y

        for k in range(N_DEV):
            stail[k, 0:N_META, :] = dmeta[:, SHARD_META * k:SHARD_META * (k + 1)]
            for i in range(D_MODEL // LANES):
                stail[k, N_META + i:N_META + i + 1, :] = dnorm[:, LANES * i:LANES * (i + 1)]
        copies = []
        for r in range(1, N_DEV):
            peer = (1 - x if r & 4 else x, 1 - y if r & 2 else y, 1 - c if r & 1 else c)
            copies.append(pltpu.make_async_remote_copy(
                src_ref=stail.at[4 * peer[0] + 2 * peer[1] + peer[2]],
                dst_ref=rtail.at[r],
                send_sem=send_sems.at[r - 1],
                recv_sem=recv_sems.at[r - 1],
                device_id=peer,
                device_id_type=pl.DeviceIdType.MESH,
            ))
        for cp in copies:
            cp.start()
        rtail[0] = stail[me]

        def chunk(i, carry):
            sl = pl.ds(pl.multiple_of(i * ADAM_CHUNK, ADAM_CHUNK), ADAM_CHUNK)
            g = rin[my_chip, sl, :].astype(F32)
            for ch in range(1, N_CHIPS):
                g = g + rin[ch ^ my_chip, sl, :].astype(F32)
            g_out["w_in"][sl, :] = g[:, :SHARD_IN]
            return carry

        lax.fori_loop(0, D_MODEL // ADAM_CHUNK, chunk, 0)

        g = rout[my_chip].astype(F32)
        gs = rsmall[my_chip]
        for ch in range(1, N_CHIPS):
            g = g + rout[ch ^ my_chip].astype(F32)
            gs = gs + rsmall[ch ^ my_chip]
        g_out["w_out"][...] = g
        gsum[...] = gs
        g_out["w_q_up"][...] = gsum[ROW_Q:ROW_Q + Q_RANK, 0:SHARD_Q]
        g_out["w_kv_up"][...] = gsum[ROW_KV:ROW_KV + KV_RANK, :]
        g_out["conv_w"][...] = gsum[ROW_CONV:ROW_CONV + 3, 0:SHARD_CONV]
        for name, row, width in (("final_norm_g", ROW_FINAL, D_MODEL), ("q_norm_g", ROW_GQ, Q_RANK),
                                 ("kv_norm_g", ROW_GKV, KV_RANK), ("attn_out_g", ROW_ATTN, CONV_WIDTH),
                                 ("conv_out_g", ROW_CONVG, CONV_WIDTH)):
            for i in range(width // LANES):
                g_out[name][:, LANES * i:LANES * (i + 1)] = gsum[row + i:row + i + 1, :]
        loss_out[...] = gsum[ROW_LOSS:ROW_LOSS + 1, :]

        for cp in copies:
            cp.wait_recv()
        gt = rtail[me]
        for d in range(1, N_DEV):
            gt = gt + rtail[d ^ me]
        gtail[...] = gt
        g_out["meta_tokens"][...] = gtail[0:N_META, :]
        for i in range(D_MODEL // LANES):
            g_out["norm_g"][:, LANES * i:LANES * (i + 1)] = gtail[N_META + i:N_META + i + 1, :]
        for cp in copies:
            cp.wait_send()

    vm = pl.BlockSpec(memory_space=pltpu.VMEM)
    out_shape = [jax.ShapeDtypeStruct(shape, F32) for _, shape in PARAM_SHAPES]
    out_shape.append(jax.ShapeDtypeStruct((1, LANES), F32))
    outs = pl.pallas_call(
        body,
        name="reduce_tail",
        out_shape=tuple(out_shape),
        in_specs=[vm] * 5,
        out_specs=(vm,) * len(out_shape),
        scratch_shapes=[
            pltpu.VMEM((N_DEV, TAIL_ROWS, LANES), F32),
            pltpu.VMEM((N_DEV, TAIL_ROWS, LANES), F32),
            pltpu.VMEM((SMALL_ROWS, LANES), F32),
            pltpu.VMEM((TAIL_ROWS, LANES), F32),
            pltpu.SemaphoreType.DMA((N_DEV - 1,)),
            pltpu.SemaphoreType.DMA((N_DEV - 1,)),
        ],
        compiler_params=pltpu.CompilerParams(vmem_limit_bytes=VMEM_LIMIT),
    )(r_in, r_out, r_small, d_meta, d_norm)
    return {n: outs[i] for i, n in enumerate(names)}, outs[-1]


def _prep_gather(x, meta, norm_g, w_in):
    nb_seq, s, d = x.shape
    nb = s // LANES + 1
    forward_step = nb_seq * (nb // 3)
    finish_step = nb_seq * (nb - 1)

    def body(x_ref, meta_ref, g_ref, win_ref, u_ref, w_in_p, meta_f,
             sbig, ssmall, gbig, gsmall, send_sems, recv_sems, local_sems):
        jj, b = pl.program_id(0), pl.program_id(1)
        t = jj * nb_seq + b

        def plan():
            return _gather_plan((sbig, ssmall), (gbig, gsmall), send_sems, recv_sems, local_sems)

        @pl.when(t == 0)
        def _():
            sbig[:, 0:SHARD_IN] = win_ref[...].astype(BF16)
            sbig[:, SHARD_IN:] = jnp.zeros((d, SHARD_IN_PAD - SHARD_IN), BF16)
            ssmall[...] = meta_ref[...]
            plan()[0]()

        @pl.when(t == forward_step)
        def _():
            plan()[1]()

        @pl.when(t == finish_step)
        def _():
            plan()[2]()
            w_in_p[:, N_A:GRP_A] = jnp.zeros((d, GRP_A - N_A), BF16)
            for k in range(N_DEV):
                for s0, e0, d0 in _in_pieces(k):
                    w_in_p[:, d0:d0 + e0 - s0] = gbig[k, :, s0:e0]
                meta_f[:, SHARD_META * k:SHARD_META * (k + 1)] = gsmall[k]

        def norm(h):
            hhat, _ = _rms_stats(h)
            return (hhat * g_ref[...]).astype(BF16)

        @pl.when(jj < nb - 1)
        def _():
            u_ref[...] = norm(x_ref[0])

        @pl.when(jj == nb - 1)
        def _():
            u_ref[0:PAD_FRONT, :] = jnp.zeros((PAD_FRONT, d), BF16)
            u_ref[PAD_FRONT:LANES, :] = norm(meta_f[...])

    whole = lambda shape: pl.BlockSpec(shape, lambda jj, b: (0,) * len(shape))
    return pl.pallas_call(
        body,
        name="prep_norm_gather",
        grid=(nb, nb_seq),
        in_specs=[
            pl.BlockSpec((1, LANES, d), lambda jj, b: (b, jnp.minimum(jj, nb - 2), 0)),
            whole(meta.shape), whole(norm_g.shape), whole(w_in.shape),
        ],
        out_specs=(pl.BlockSpec((LANES, d), lambda jj, b: (b * nb + (jj + 1) % nb, 0)),
                   whole((d, IN_PAD)), whole((N_META, d))),
        out_shape=(jax.ShapeDtypeStruct((nb_seq * nb * LANES, d), BF16),
                   jax.ShapeDtypeStruct((d, IN_PAD), BF16),
                   jax.ShapeDtypeStruct((N_META, d), F32)),
        scratch_shapes=[
            pltpu.VMEM((d, SHARD_IN_PAD), BF16),
            pltpu.VMEM((N_META, SHARD_META), F32),
            pltpu.VMEM((N_DEV, d, SHARD_IN_PAD), BF16),
            pltpu.VMEM((N_DEV, N_META, SHARD_META), F32),
            pltpu.SemaphoreType.DMA((14,)),
            pltpu.SemaphoreType.DMA((14,)),
            pltpu.SemaphoreType.DMA((2,)),
        ],
        compiler_params=_params("arbitrary", "arbitrary"),
    )(x, meta, norm_g, w_in)


def _in_proj_gather(u, w_in_p, w_q, w_kv, w_out, conv_w, bm, bn):
    m, k_dim = u.shape
    n = w_in_p.shape[1]
    steps = (m // bm) * (n // bn)
    forward_step = steps // 3
    qkv_rows = Q_RANK + KV_RANK

    def body(a_ref, b_ref, wq_ref, wkv_ref, wout_ref, conv_ref, o_ref, w_q_p, w_kv_p, w_out_f, conv_f,
             sqkv, sout, sconv, gqkv, gout, gconv, send_sems, recv_sems, local_sems):
        t = pl.program_id(0) * (n // bn) + pl.program_id(1)

        def plan():
            return _gather_plan((sqkv, sout, sconv), (gqkv, gout, gconv), send_sems, recv_sems, local_sems)

        @pl.when(t == 0)
        def _():
            sqkv[...] = jnp.zeros_like(sqkv)
            sqkv[0:Q_RANK, 0:SHARD_Q] = wq_ref[...].astype(BF16)
            sqkv[Q_RANK:, :] = wkv_ref[...].astype(BF16)
            sout[...] = wout_ref[...].astype(BF16)
            sconv[...] = jnp.zeros_like(sconv)
            sconv[0:3, 0:SHARD_CONV] = conv_ref[...]
            plan()[0]()

        @pl.when(t == forward_step)
        def _():
            plan()[1]()

        o_ref[...] = _dot(a_ref[...], b_ref[...]).astype(o_ref.dtype)

        @pl.when(t == steps - 1)
        def _():
            plan()[2]()
            conv_f[...] = jnp.zeros_like(conv_f)
            for k in range(N_DEV):
                for s0, e0, d0 in _q_pieces(k):
                    w_q_p[:, d0:d0 + e0 - s0] = gqkv[k, 0:Q_RANK, s0:e0]
                w_kv_p[:, _kv_dst(k):_kv_dst(k) + SHARD_KV] = gqkv[k, Q_RANK:, :]
                w_out_f[SHARD_OUT * k:SHARD_OUT * (k + 1), :] = gout[k]
                conv_f[0:3, SHARD_CONV * k:SHARD_CONV * (k + 1)] = gconv[k, 0:3, 0:SHARD_CONV]

    whole = lambda shape: pl.BlockSpec(shape, lambda i, j: (0,) * len(shape))
    return pl.pallas_call(
        body,
        name="in_proj_gather",
        grid=(m // bm, n // bn),
        in_specs=[pl.BlockSpec((bm, k_dim), lambda i, j: (i, 0)), pl.BlockSpec((k_dim, bn), lambda i, j: (0, j)),
                  whole(w_q.shape), whole(w_kv.shape), whole(w_out.shape), whole(conv_w.shape)],
        out_specs=(pl.BlockSpec((bm, bn), lambda i, j: (i, j)),
                   whole((Q_RANK, Q_COLS)), whole((KV_RANK, KV_COLS)), whole((D_MODEL, D_MODEL)),
                   whole((8, CONV_WIDTH))),
        out_shape=(jax.ShapeDtypeStruct((m, n), BF16),
                   jax.ShapeDtypeStruct((Q_RANK, Q_COLS), BF16),
                   jax.ShapeDtypeStruct((KV_RANK, KV_COLS), BF16),
                   jax.ShapeDtypeStruct((D_MODEL, D_MODEL), BF16),
                   jax.ShapeDtypeStruct((8, CONV_WIDTH), F32)),
        scratch_shapes=[
            pltpu.VMEM((qkv_rows, LANES), BF16),
            pltpu.VMEM((SHARD_OUT, D_MODEL), BF16),
            pltpu.VMEM((8, LANES), F32),
            pltpu.VMEM((N_DEV, qkv_rows, LANES), BF16),
            pltpu.VMEM((N_DEV, SHARD_OUT, D_MODEL), BF16),
            pltpu.VMEM((N_DEV, 8, LANES), F32),
            pltpu.SemaphoreType.DMA((21,)),
            pltpu.SemaphoreType.DMA((21,)),
            pltpu.SemaphoreType.DMA((3,)),
        ],
        compiler_params=_params("arbitrary", "arbitrary"),
    )(u, w_in_p, w_q, w_kv, w_out, conv_w)


def _rope_tables(tp):
    half = D_ROPE // 2
    inv_freq = 1.0 / (ROPE_THETA ** (jnp.arange(half, dtype=F32) / half))
    pos = (jnp.arange(tp) - PAD_FRONT).astype(F32)
    ang = pos[:, None] * inv_freq[None, :]
    cos = jnp.tile(jnp.cos(ang), (1, LANES // half))
    sin = jnp.tile(jnp.sin(ang), (1, LANES // half))
    first = (jnp.arange(LANES) % D_ROPE) < half
    return cos, jnp.where(first, -sin, 0.0), jnp.where(first, 0.0, sin)


def _rope(t, cos, sa, sb):
    return t * cos + pltpu.roll(t, LANES - D_ROPE // 2, 1) * sa + pltpu.roll(t, D_ROPE // 2, 1) * sb


def _rope_t(t, cos, sa, sb):
    return t * cos + pltpu.roll(t * sa, D_ROPE // 2, 1) + pltpu.roll(t * sb, LANES - D_ROPE // 2, 1)


def _qkv_fwd(p, wq, wkv, gq, gkv, tables, nb_seq, tp):
    ht = tp // 2

    def body(pa_ref, wq_ref, wkv_ref, gq_ref, gkv_ref, cos_ref, sa_ref, sb_ref, q_ref, k_ref, v_ref):
        pa = pa_ref[...].astype(F32)
        cq_hat, _ = _rms_stats(pa[:, :Q_RANK])
        ckv_hat, _ = _rms_stats(pa[:, Q_RANK:Q_RANK + KV_RANK])
        q = _dot((cq_hat * gq_ref[...]).astype(BF16), wq_ref[...])
        kv = _dot((ckv_hat * gkv_ref[...]).astype(BF16), wkv_ref[...])
        tabs = (cos_ref[...], sa_ref[...], sb_ref[...])
        lane = lax.broadcasted_iota(jnp.int32, (ht, LANES), 1)
        low = lane < D_ROPE
        mark = lane == D_ROPE
        row = (pl.program_id(0) % 2) * ht + lax.broadcasted_iota(jnp.int32, (ht, LANES), 0)
        k_pe = jnp.where(mark & (row < PAD_FRONT), NEG_INF, _rope(pa[:, Q_RANK + KV_RANK:], *tabs))
        one = jnp.where(mark & (row >= PAD_FRONT), 1.0, 0.0)
        pairs = [_rope(q[:, N_HEADS * D_NOPE + LANES * i:N_HEADS * D_NOPE + LANES * (i + 1)], *tabs) for i in range(2)]
        for h in range(N_HEADS):
            pair = pairs[h // 2]
            if h % 2:
                pair = pltpu.roll(pair, D_ROPE, 1)
            pe = jnp.where(low, pair, one)
            q_ref[0, h] = jnp.concatenate([q[:, D_NOPE * h:D_NOPE * (h + 1)], pe], axis=1).astype(BF16)
            k_ref[0, h] = jnp.concatenate([kv[:, D_NOPE * h:D_NOPE * (h + 1)], k_pe], axis=1).astype(BF16)
            v_ref[0, h] = kv[:, N_HEADS * D_NOPE + D_V * h:N_HEADS * D_NOPE + D_V * (h + 1)].astype(BF16)

    full = lambda a: pl.BlockSpec(a.shape, lambda i: (0,) * a.ndim)
    tab = pl.BlockSpec((ht, LANES), lambda i: (i % 2, 0))
    qk = pl.BlockSpec((1, N_HEADS, ht, 2 * LANES), lambda i: (i // 2, 0, i % 2, 0))
    return pl.pallas_call(
        body,
        name="qkv_fwd",
        grid=(2 * nb_seq,),
        in_specs=[pl.BlockSpec((ht, GRP_A), lambda i: (i, 0)), full(wq), full(wkv), full(gq), full(gkv), tab, tab, tab],
        out_specs=(qk, qk, pl.BlockSpec((1, N_HEADS, ht, D_V), lambda i: (i // 2, 0, i % 2, 0))),
        out_shape=(
            jax.ShapeDtypeStruct((nb_seq, N_HEADS, tp, 2 * LANES), BF16),
            jax.ShapeDtypeStruct((nb_seq, N_HEADS, tp, 2 * LANES), BF16),
            jax.ShapeDtypeStruct((nb_seq, N_HEADS, tp, D_V), BF16),
        ),
        compiler_params=_params("parallel"),
    )(p, wq, wkv, gq, gkv, *tables)


def _attn_fwd(q, k, v, p, g_attn):
    nb_seq, _, tp, _ = q.shape

    def body(q_ref, k_ref, v_ref, z_ref, g_ref, y_ref, o_ref, lse_ref):
        g = g_ref[...]
        for r0 in range(0, tp, KV_TILE):
            nq = min(KV_TILE, tp - r0)
            kend = r0 + nq
            qq = q_ref[0, 0, r0:kend, :]
            sd = _dot(qq, k_ref[0, 0, r0:kend, :], _NT) * ATTN_SCALE
            causal = (lax.broadcasted_iota(jnp.int32, (nq, nq), 1) <= lax.broadcasted_iota(jnp.int32, (nq, nq), 0))
            sd = jnp.where(causal, sd, NEG_INF)
            m = jnp.max(sd, axis=-1, keepdims=True)
            if r0:
                so = _dot(qq, k_ref[0, 0, 0:r0, :], _NT) * ATTN_SCALE
                m = jnp.maximum(m, jnp.max(so, axis=-1, keepdims=True))
            ed = jnp.exp(sd - m)
            l = jnp.sum(ed, axis=-1, keepdims=True)
            o = _dot(ed.astype(BF16), v_ref[0, 0, r0:kend, :])
            if r0:
                eo = jnp.exp(so - m)
                l = l + jnp.sum(eo, axis=-1, keepdims=True)
                o = o + _dot(eo.astype(BF16), v_ref[0, 0, 0:r0, :])
            o = o * (1.0 / l)
            o_ref[0, 0, r0:kend, :] = o
            lse_ref[0, 0, r0:kend, :] = jnp.broadcast_to(m + jnp.log(l), (nq, LANES))
            ohat, _ = _rms_stats(o)
            z = z_ref[r0:kend, :].astype(F32)
            y_ref[r0:kend, :] = (ohat * g * (z * _sigmoid(z))).astype(BF16)

    qk = pl.BlockSpec((1, 1, tp, 2 * LANES), lambda b, h: (b, h, 0, 0))
    hv = pl.BlockSpec((1, 1, tp, D_V), lambda b, h: (b, h, 0, 0))
    return pl.pallas_call(
        body,
        name="attn_fwd",
        grid=(nb_seq, N_HEADS),
        in_specs=[qk, qk, hv,
                  pl.BlockSpec((tp, LANES), lambda b, h: (b, GRP_A // LANES + h)),
                  pl.BlockSpec((1, LANES), lambda b, h: (0, h))],
        out_specs=(pl.BlockSpec((tp, LANES), lambda b, h: (b, h)), hv, hv),
        out_shape=(
            jax.ShapeDtypeStruct((nb_seq * tp, N_HEADS * D_V), BF16),
            jax.ShapeDtypeStruct((nb_seq, N_HEADS, tp, D_V), F32),
            jax.ShapeDtypeStruct((nb_seq, N_HEADS, tp, LANES), F32),
        ),
        compiler_params=_params("parallel", "parallel"),
    )(q, k, v, p, g_attn)


_CONV_COL0 = (GRP_A + N_HEADS * D_V) // LANES


def _conv_specs(tp, order):
    cols = CONV_WIDTH // LANES
    return [pl.BlockSpec((tp, LANES), functools.partial(
        lambda a, b, off: order(a, b, off), off=_CONV_COL0 + i * cols)) for i in range(4)]


def _conv_fwd(p, conv_w, g_conv, nb_seq, tp):
    def body(b_ref, c_ref, h_ref, z_ref, w_ref, g_ref, y_ref):
        cc = c_ref[...].astype(F32) * h_ref[...].astype(F32)
        row = lax.broadcasted_iota(jnp.int32, (tp, LANES), 0)
        s1 = jnp.where(row >= 1, pltpu.roll(cc, 1, 0), 0.0)
        s2 = jnp.where(row >= 2, pltpu.roll(cc, 2, 0), 0.0)
        yc = b_ref[...].astype(F32) * (w_ref[0:1, :] * s2 + w_ref[1:2, :] * s1 + w_ref[2:3, :] * cc)
        r = lax.rsqrt(_group_mean(yc * yc) + EPS)
        z = z_ref[...].astype(F32)
        y_ref[...] = (yc * r * g_ref[...] * (z * _sigmoid(z))).astype(BF16)

    return pl.pallas_call(
        body,
        name="conv_fwd",
        grid=(nb_seq, CONV_WIDTH // LANES),
        in_specs=_conv_specs(tp, lambda b, t, off: (b, off + t)) + [
            pl.BlockSpec((8, LANES), lambda b, t: (0, t)),
            pl.BlockSpec((1, LANES), lambda b, t: (0, t))],
        out_specs=pl.BlockSpec((tp, LANES), lambda b, t: (b, t)),
        out_shape=jax.ShapeDtypeStruct((nb_seq * tp, CONV_WIDTH), BF16),
        compiler_params=_params("parallel", "parallel"),
    )(p, p, p, p, conv_w, g_conv)


def _token_copy(hbm, b, k, ts, buf, sem, to_hbm=False):
    lo, hi = max(k * ts - LANES, 0), (k + 1) * ts - LANES
    off = lo - (k * ts - LANES)
    src, dst = hbm.at[b, pl.ds(lo, hi - lo)], buf.at[pl.ds(off, hi - lo)]
    if to_hbm:
        src, dst = dst, src
    return pltpu.make_async_copy(src, dst, sem)


def _for_tile(k, nt, fn):
    for kk in range(nt):
        @pl.when(k == kk)
        def _(kk=kk):
            fn(kk)


def _out_proj_loss(ya, yc, w_out, x, target, g_final, nt):
    nb_seq, s, d = x.shape
    r, ka = ya.shape
    ts = (s + LANES) // nt
    steps = nb_seq * nt

    def body(a_ref, c_ref, w_ref, x_hbm, t_hbm, g_ref, dh_ref, dhb_ref, dg_ref, loss_ref,
             xbuf, tbuf, acc_ref, sems):
        i = pl.program_id(0)
        b, k = i // nt, i % nt

        @pl.when(i == 0)
        def _():
            acc_ref[...] = jnp.zeros_like(acc_ref)
            dg_ref[...] = jnp.zeros_like(dg_ref)

        def fetch(kk):
            return [_token_copy(x_hbm, b, kk, ts, xbuf, sems.at[0]), _token_copy(t_hbm, b, kk, ts, tbuf, sems.at[1])]

        def start(kk):
            if kk == 0:
                xbuf[0:LANES, :] = jnp.zeros((LANES, d), F32)
                tbuf[0:LANES, :] = jnp.zeros((LANES, d), F32)
            for cp in fetch(kk):
                cp.start()

        _for_tile(k, nt, start)
        mix = _dot(a_ref[...], w_ref[0:ka, :]) + _dot(c_ref[...], w_ref[ka:, :])
        _for_tile(k, nt, lambda kk: [cp.wait() for cp in fetch(kk)])

        real = (lax.broadcasted_iota(jnp.int32, (ts, d), 0) >= LANES) | (k > 0)
        g = g_ref[...]
        hhat, rstd = _rms_stats(xbuf[...] + mix)
        e = jnp.where(real, hhat * g - tbuf[...], 0.0)
        acc_ref[...] += jnp.sum(e * e, axis=0, keepdims=True)
        dy = e * (1.0 / d)
        dg_ref[...] += jnp.sum(dy * hhat, axis=0, keepdims=True)
        dh = _rms_bwd(g * dy, hhat, rstd)
        dh_ref[...] = dh
        dhb_ref[...] = dh.astype(BF16)

        @pl.when(i == steps - 1)
        def _():
            total = jnp.sum(acc_ref[...], axis=1, keepdims=True)
            loss_ref[...] = jnp.broadcast_to((0.5 / d) * total, loss_ref.shape)

    hbm = pl.BlockSpec(memory_space=pl.ANY)
    row = pl.BlockSpec((ts, d), lambda i: (i, 0))
    vec = pl.BlockSpec((1, d), lambda i: (0, 0))
    return pl.pallas_call(
        body,
        name="out_proj_loss",
        grid=(steps,),
        in_specs=[pl.BlockSpec((ts, ka), lambda i: (i, 0)), pl.BlockSpec((ts, yc.shape[1]), lambda i: (i, 0)),
                  pl.BlockSpec(w_out.shape, lambda i: (0, 0)), hbm, hbm, vec],
        out_specs=(row, row, vec, pl.BlockSpec((1, LANES), lambda i: (0, 0))),
        out_shape=(
            jax.ShapeDtypeStruct((r, d), F32),
            jax.ShapeDtypeStruct((r, d), BF16),
            jax.ShapeDtypeStruct((1, d), F32),
            jax.ShapeDtypeStruct((1, LANES), F32),
        ),
        scratch_shapes=[pltpu.VMEM((ts, d), F32), pltpu.VMEM((ts, d), F32), pltpu.VMEM((1, d), F32),
                        pltpu.SemaphoreType.DMA((2,))],
        compiler_params=_params("arbitrary"),
    )(ya, yc, w_out, x, target, g_final)


def _out_proj_bwd(dhb, w_out, ya, yc, bm):
    r, d = dhb.shape
    ka = ya.shape[1]
    n_mix = w_out.shape[0]
    last = r // bm - 1

    def body(dh_ref, w_ref, a_ref, c_ref, dcat_ref, dw_ref, acc_ref):
        @pl.when(pl.program_id(0) == 0)
        def _():
            acc_ref[...] = jnp.zeros_like(acc_ref)

        dh = dh_ref[...]
        dcat_ref[...] = _dot(dh, w_ref[...], _NT).astype(BF16)
        acc_ref[0:ka, :] += _dot(a_ref[...], dh, _TN)
        acc_ref[ka:, :] += _dot(c_ref[...], dh, _TN)

        @pl.when(pl.program_id(0) == last)
        def _():
            dw_ref[...] = acc_ref[...].astype(BF16)

    return pl.pallas_call(
        body,
        name="out_proj_bwd",
        grid=(r // bm,),
        in_specs=[pl.BlockSpec((bm, d), lambda i: (i, 0)), pl.BlockSpec(w_out.shape, lambda i: (0, 0)),
                  pl.BlockSpec((bm, ka), lambda i: (i, 0)), pl.BlockSpec((bm, yc.shape[1]), lambda i: (i, 0))],
        out_specs=(pl.BlockSpec((bm, n_mix), lambda i: (i, 0)),
                   pl.BlockSpec((n_mix, d), lambda i: (0, 0))),
        out_shape=(jax.ShapeDtypeStruct((r, n_mix), BF16),
                   jax.ShapeDtypeStruct((n_mix, d), BF16)),
        scratch_shapes=[pltpu.VMEM((n_mix, d), F32)],
        compiler_params=_params("arbitrary"),
    )(dhb, w_out, ya, yc)


def _attn_bwd(q, k, v, o, lse, dcat, p, g_attn, send_out):
    nb_seq, _, tp, _ = q.shape
    steps = N_HEADS * nb_seq

    def body(q_ref, k_ref, v_ref, o_ref, lse_ref, dy_ref, z_ref, g_ref, pay_ref,
             dq_ref, dk_ref, dv_ref, dz_ref, dg_ref, r2_ref, dq_acc, r1, sums, *sems):
        t = pl.program_id(0) * nb_seq + pl.program_id(1)

        def plan():
            return _reduce_plan((pay_ref,), (None,), (r1,), (sums,), (r2_ref,), *sems)

        @pl.when(t == 0)
        def _():
            plan()[0]()

        @pl.when(t == 1)
        def _():
            plan()[1]()

        @pl.when(pl.program_id(1) == 0)
        def _():
            dg_ref[...] = jnp.zeros_like(dg_ref)

        g = g_ref[...]
        z = z_ref[...].astype(F32)
        o = o_ref[0, 0]
        dy = dy_ref[...].astype(F32)
        sig = _sigmoid(z)
        ohat, r = _rms_stats(o)
        don = dy * (z * sig)
        dz_ref[...] = (dy * (ohat * g) * (sig * (1.0 + z * (1.0 - sig)))).astype(BF16)
        dg_ref[...] += jnp.sum(don * ohat, axis=0, keepdims=True)
        do = _rms_bwd(g * don, ohat, r)
        dvec = jnp.sum(do * o, axis=-1, keepdims=True)
        dob = do.astype(BF16)
        lse_col = lse_ref[0, 0, :, 0:1]
        dq_acc[...] = jnp.zeros_like(dq_acc)
        for k0 in range(0, tp, KV_TILE):
            nk = min(KV_TILE, tp - k0)
            nq = tp - k0
            qq = q_ref[0, 0, k0:, :]
            kk = k_ref[0, 0, k0:k0 + nk, :]
            causal = (lax.broadcasted_iota(jnp.int32, (nq, nk), 1) <= lax.broadcasted_iota(jnp.int32, (nq, nk), 0))
            pr = jnp.where(causal, jnp.exp(_dot(qq, kk, _NT) * ATTN_SCALE - lse_col[k0:]), 0.0)
            dp = _dot(dob[k0:], v_ref[0, 0, k0:k0 + nk, :], _NT)
            ds = (pr * (dp - dvec[k0:]) * ATTN_SCALE).astype(BF16)
            dv_ref[0, 0, k0:k0 + nk, :] = _dot(pr.astype(BF16), dob[k0:], _TN).astype(BF16)
            dk_ref[0, 0, k0:k0 + nk, :] = _dot(ds, qq, _TN).astype(BF16)
            dq_acc[k0:, :] += _dot(ds, kk)
        dq_ref[0, 0] = dq_acc[...].astype(BF16)

        @pl.when(t == steps - 1)
        def _():
            plan()[2]()

    qk = pl.BlockSpec((1, 1, tp, 2 * LANES), lambda h, b: (b, h, 0, 0))
    hv = pl.BlockSpec((1, 1, tp, D_V), lambda h, b: (b, h, 0, 0))
    col = pl.BlockSpec((tp, LANES), lambda h, b: (b, h))
    slot = send_out.shape[1:]
    return pl.pallas_call(
        body,
        name="attn_bwd",
        grid=(N_HEADS, nb_seq),
        in_specs=[qk, qk, hv, hv, hv, col,
                  pl.BlockSpec((tp, LANES), lambda h, b: (b, GRP_A // LANES + h)),
                  pl.BlockSpec((1, LANES), lambda h, b: (0, h)),
                  pl.BlockSpec(send_out.shape, lambda h, b: (0, 0, 0))],
        out_specs=(qk, qk, hv, col, pl.BlockSpec((1, LANES), lambda h, b: (0, h)),
                   pl.BlockSpec(memory_space=pl.ANY)),
        out_shape=(
            jax.ShapeDtypeStruct((nb_seq, N_HEADS, tp, 2 * LANES), BF16),
            jax.ShapeDtypeStruct((nb_seq, N_HEADS, tp, 2 * LANES), BF16),
            jax.ShapeDtypeStruct((nb_seq, N_HEADS, tp, D_V), BF16),
            jax.ShapeDtypeStruct((nb_seq * tp, N_HEADS * D_V), BF16),
            jax.ShapeDtypeStruct((1, N_HEADS * D_V), F32),
            jax.ShapeDtypeStruct((N_CHIPS,) + slot, BF16),
        ),
        scratch_shapes=[pltpu.VMEM((tp, 2 * LANES), F32)] + _reduce_scratch([(slot, BF16)], [False]),
        compiler_params=_params("arbitrary", "arbitrary"),
    )(q, k, v, o, lse, dcat, p, g_attn, send_out)


def _qkv_bwd(p, dq, dk, dv, wq, wkv, gq, gkv, tables):
    nb_seq, _, tp, _ = dq.shape
    ht = tp // 2

    def body(pa_ref, dq_ref, dk_ref, dv_ref, wq_ref, wkv_ref, gq_ref, gkv_ref, cos_ref, sa_ref, sb_ref,
             dpa_ref, dwq_ref, dwkv_ref, dgq_ref, dgkv_ref):
        @pl.when(pl.program_id(0) == 0)
        def _():
            dwq_ref[...] = jnp.zeros_like(dwq_ref)
            dwkv_ref[...] = jnp.zeros_like(dwkv_ref)
            dgq_ref[...] = jnp.zeros_like(dgq_ref)
            dgkv_ref[...] = jnp.zeros_like(dgkv_ref)

        pa = pa_ref[...].astype(F32)
        gq, gkv = gq_ref[...], gkv_ref[...]
        cq_hat, rq = _rms_stats(pa[:, :Q_RANK])
        ckv_hat, rkv = _rms_stats(pa[:, Q_RANK:Q_RANK + KV_RANK])
        tabs = (cos_ref[...], sa_ref[...], sb_ref[...])

        pe = [dq_ref[0, h, :, D_NOPE:].astype(F32) for h in range(N_HEADS)]
        pairs = [_rope_t(pe[2 * i] + pltpu.roll(pe[2 * i + 1], D_ROPE, 1), *tabs).astype(BF16) for i in range(2)]
        dq_flat = jnp.concatenate([dq_ref[0, h, :, :D_NOPE] for h in range(N_HEADS)] + pairs, axis=1)
        dwq_ref[...] += _dot((cq_hat * gq).astype(BF16), dq_flat, _TN)
        dcqn = _dot(dq_flat, wq_ref[...], _NT)
        dgq_ref[...] += jnp.sum(dcqn * cq_hat, axis=0, keepdims=True)
        dcq = _rms_bwd(gq * dcqn, cq_hat, rq)

        dkv_flat = jnp.concatenate([dk_ref[0, h, :, :D_NOPE] for h in range(N_HEADS)]
                                   + [dv_ref[0, h] for h in range(N_HEADS)], axis=1)
        dwkv_ref[...] += _dot((ckv_hat * gkv).astype(BF16), dkv_flat, _TN)
        dckvn = _dot(dkv_flat, wkv_ref[...], _NT)
        dgkv_ref[...] += jnp.sum(dckvn * ckv_hat, axis=0, keepdims=True)
        dckv = _rms_bwd(gkv * dckvn, ckv_hat, rkv)

        dk_pe = dk_ref[0, 0, :, D_NOPE:].astype(F32)
        for h in range(1, N_HEADS):
            dk_pe = dk_pe + dk_ref[0, h, :, D_NOPE:].astype(F32)
        dk_pe = jnp.where(lax.broadcasted_iota(jnp.int32, (ht, LANES), 1) < D_ROPE, dk_pe, 0.0)
        dpa_ref[...] = jnp.concatenate([dcq, dckv, _rope_t(dk_pe, *tabs)], axis=1).astype(BF16)

    full = lambda a: pl.BlockSpec(a.shape, lambda i: (0,) * a.ndim)
    tab = pl.BlockSpec((ht, LANES), lambda i: (i % 2, 0))
    qk = pl.BlockSpec((1, N_HEADS, ht, 2 * LANES), lambda i: (i // 2, 0, i % 2, 0))
    acc = lambda shape: pl.BlockSpec(shape, lambda i: (0, 0))
    return pl.pallas_call(
        body,
        name="qkv_bwd",
        grid=(2 * nb_seq,),
        in_specs=[pl.BlockSpec((ht, GRP_A), lambda i: (i, 0)), qk, qk,
                  pl.BlockSpec((1, N_HEADS, ht, D_V), lambda i: (i // 2, 0, i % 2, 0)),
                  full(wq), full(wkv), full(gq), full(gkv), tab, tab, tab],
        out_specs=(pl.BlockSpec((ht, GRP_A), lambda i: (i, 0)),
                   acc(wq.shape), acc(wkv.shape), acc((1, Q_RANK)), acc((1, KV_RANK))),
        out_shape=(
            jax.ShapeDtypeStruct((nb_seq * tp, GRP_A), BF16),
            jax.ShapeDtypeStruct(wq.shape, F32),
            jax.ShapeDtypeStruct(wkv.shape, F32),
            jax.ShapeDtypeStruct((1, Q_RANK), F32),
            jax.ShapeDtypeStruct((1, KV_RANK), F32),
        ),
        compiler_params=_params("arbitrary"),
    )(p, dq, dk, dv, wq, wkv, gq, gkv, *tables)


def _conv_bwd(p, dcat, conv_w, g_conv, nb_seq, tp):
    cols = CONV_WIDTH // LANES

    def body(b_ref, c_ref, h_ref, z_ref, dy_ref, w_ref, g_ref,
             db_ref, dc_ref, dh_ref, dz_ref, dw_ref, dg_ref):
        @pl.when(pl.program_id(1) == 0)
        def _():
            dw_ref[...] = jnp.zeros_like(dw_ref)
            dg_ref[...] = jnp.zeros_like(dg_ref)

        cb, c, h = b_ref[...].astype(F32), c_ref[...].astype(F32), h_ref[...].astype(F32)
        z, dy = z_ref[...].astype(F32), dy_ref[...].astype(F32)
        g = g_ref[...]
        w0, w1, w2 = w_ref[0:1, :], w_ref[1:2, :], w_ref[2:3, :]
        cc = c * h
        row = lax.broadcasted_iota(jnp.int32, (tp, LANES), 0)
        s1 = jnp.where(row >= 1, pltpu.roll(cc, 1, 0), 0.0)
        s2 = jnp.where(row >= 2, pltpu.roll(cc, 2, 0), 0.0)
        dwc = w0 * s2 + w1 * s1 + w2 * cc
        yc = cb * dwc
        r = lax.rsqrt(_group_mean(yc * yc) + EPS)
        ychat = yc * r
        sig = _sigmoid(z)
        dz_ref[...] = (dy * (ychat * g) * (sig * (1.0 + z * (1.0 - sig)))).astype(BF16)
        dyn = dy * (z * sig)
        dg_ref[...] += jnp.sum(dyn * ychat, axis=0, keepdims=True)
        gd = g * dyn
        dyc = r * (gd - ychat * _group_mean(gd * ychat))
        db_ref[...] = (dyc * dwc).astype(BF16)
        ddw = dyc * cb
        dw_ref[0:1, :] += jnp.sum(ddw * s2, axis=0, keepdims=True)
        dw_ref[1:2, :] += jnp.sum(ddw * s1, axis=0, keepdims=True)
        dw_ref[2:3, :] += jnp.sum(ddw * cc, axis=0, keepdims=True)
        u1 = jnp.where(row <= tp - 2, pltpu.roll(ddw, tp - 1, 0), 0.0)
        u2 = jnp.where(row <= tp - 3, pltpu.roll(ddw, tp - 2, 0), 0.0)
        dcc = w2 * ddw + w1 * u1 + w0 * u2
        dc_ref[...] = (dcc * h).astype(BF16)
        dh_ref[...] = (dcc * c).astype(BF16)

    col = pl.BlockSpec((tp, LANES), lambda t, b: (b, t))
    out = jax.ShapeDtypeStruct((nb_seq * tp, CONV_WIDTH), BF16)
    return pl.pallas_call(
        body,
        name="conv_bwd",
        grid=(cols, nb_seq),
        in_specs=_conv_specs(tp, lambda t, b, off: (b, off + t)) + [
            pl.BlockSpec((tp, LANES), lambda t, b: (b, N_HEADS * D_V // LANES + t)),
            pl.BlockSpec((8, LANES), lambda t, b: (0, t)),
            pl.BlockSpec((1, LANES), lambda t, b: (0, t))],
        out_specs=(col, col, col, col,
                   pl.BlockSpec((8, LANES), lambda t, b: (0, t)), pl.BlockSpec((1, LANES), lambda t, b: (0, t))),
        out_shape=(out, out, out, out,
                   jax.ShapeDtypeStruct((8, CONV_WIDTH), F32), jax.ShapeDtypeStruct((1, CONV_WIDTH), F32)),
        compiler_params=_params("arbitrary", "arbitrary"),
    )(p, p, p, p, dcat, conv_w, g_conv)


def _input_bwd(dps, w_in, x, meta, dh, norm_g, nt, send_in, small_grads):
    nb_seq, s, d = x.shape
    r, kb = dps[0].shape
    ts = (s + LANES) // nt
    steps = nb_seq * nt
    n_dp, n_small = len(dps), len(small_grads)
    in_slot, small_slot = send_in.shape[1:], (SMALL_ROWS, LANES)

    def body(*refs):
        dp_refs, w_ref, x_hbm, meta_ref, dh_ref, g_ref, pay_ref = refs[:n_dp], *refs[n_dp:n_dp + 6]
        o = n_dp + 6
        small_refs = refs[o:o + n_small]
        o += n_small
        gx_hbm, dmeta_ref, dg_ref, r2_in, r2_small = refs[o:o + 5]
        xbuf, gxbuf, tok_sems, ssmall, own_in, r1_in, sum_in, r1_small, sum_small = refs[o + 5:o + 14]
        sems = refs[o + 14:]
        i = pl.program_id(0)
        b, k = i // nt, i % nt

        def plan():
            return _reduce_plan((pay_ref, ssmall), (own_in, None), (r1_in, r1_small), (sum_in, sum_small),
                                (r2_in, r2_small), *sems)

        @pl.when(i == 0)
        def _():
            dmeta_ref[...] = jnp.zeros_like(dmeta_ref)
            dg_ref[...] = jnp.zeros_like(dg_ref)
            _pack_small(ssmall, *small_refs)
            plan()[0]()

        @pl.when(i == 1)
        def _():
            plan()[1]()

        def start(kk):
            if kk == 0:
                xbuf[0:PAD_FRONT, :] = jnp.zeros((PAD_FRONT, d), F32)
                xbuf[PAD_FRONT:LANES, :] = meta_ref[...]
            _token_copy(x_hbm, b, kk, ts, xbuf, tok_sems.at[0]).start()

        _for_tile(k, nt, start)
        du = _dot(dp_refs[0][...], w_ref[:, 0:kb], _NT)
        for j in range(1, n_dp):
            du = du + _dot(dp_refs[j][...], w_ref[:, kb * j:kb * (j + 1)], _NT)
        _for_tile(k, nt, lambda kk: _token_copy(x_hbm, b, kk, ts, xbuf, tok_sems.at[0]).wait())

        g = g_ref[...]
        hhat, rstd = _rms_stats(xbuf[...])
        dg_ref[...] += jnp.sum(du * hhat, axis=0, keepdims=True)
        res = _rms_bwd(g * du, hhat, rstd) + dh_ref[...]

        @pl.when(i > 0)
        def _():
            _for_tile(k, nt, lambda kk: _token_copy(gx_hbm, b, (kk - 1) % nt, ts, gxbuf, tok_sems.at[1], True).wait())

        gxbuf[...] = res

        @pl.when(k == 0)
        def _():
            dmeta_ref[...] += gxbuf[PAD_FRONT:LANES, :]

        _for_tile(k, nt, lambda kk: _token_copy(gx_hbm, b, kk, ts, gxbuf, tok_sems.at[1], True).start())

        @pl.when(i == steps - 1)
        def _():
            _token_copy(gx_hbm, b, nt - 1, ts, gxbuf, tok_sems.at[1], True).wait()
            plan()[2]()

    whole = lambda a: pl.BlockSpec(a.shape, lambda i: (0,) * a.ndim)
    hbm = pl.BlockSpec(memory_space=pl.ANY)
    return pl.pallas_call(
        body,
        name="input_bwd",
        grid=(steps,),
        in_specs=[pl.BlockSpec((ts, kb), lambda i: (i, 0)) for _ in dps]
        + [whole(w_in), hbm, whole(meta), pl.BlockSpec((ts, d), lambda i: (i, 0)), whole(norm_g), hbm]
        + [whole(a) for a in small_grads],
        out_specs=(hbm, pl.BlockSpec((N_META, d), lambda i: (0, 0)), pl.BlockSpec((1, d), lambda i: (0, 0)), hbm, hbm),
        out_shape=(jax.ShapeDtypeStruct((nb_seq, s, d), F32),
                   jax.ShapeDtypeStruct((N_META, d), F32),
                   jax.ShapeDtypeStruct((1, d), F32),
                   jax.ShapeDtypeStruct((N_CHIPS,) + in_slot, BF16),
                   jax.ShapeDtypeStruct((N_CHIPS,) + small_slot, F32)),
        scratch_shapes=[pltpu.VMEM((ts, d), F32), pltpu.VMEM((ts, d), F32), pltpu.SemaphoreType.DMA((2,)),
                        pltpu.VMEM((N_DEV,) + small_slot, F32)]
        + _reduce_scratch([(in_slot, BF16), (small_slot, F32)], [True, False]),
        compiler_params=_params("arbitrary"),
    )(*dps, w_in, x, meta, dh, norm_g, send_in, *small_grads)


def _in_proj_bwd_w(u, dps, bm):
    r, d = u.shape
    kb = dps[0].shape[1]
    last = r // bm - 1

    def body(*refs):
        u_ref, dp_refs, o_ref, acc_ref = refs[0], refs[1:1 + len(dps)], refs[1 + len(dps)], refs[2 + len(dps)]

        @pl.when(pl.program_id(0) == 0)
        def _():
            acc_ref[...] = jnp.zeros_like(acc_ref)

        uu = u_ref[...]
        for j in range(len(dps)):
            acc_ref[:, kb * j:kb * (j + 1)] += _dot(uu, dp_refs[j][...], _TN)

        @pl.when(pl.program_id(0) == last)
        def _():
            for k in range(N_DEV):
                for s, e, c0 in _in_pieces(k):
                    o_ref[k, :, s:e] = acc_ref[:, c0:c0 + e - s].astype(BF16)
                o_ref[k, :, SHARD_IN:] = jnp.zeros((d, SHARD_IN_PAD - SHARD_IN), BF16)

    return pl.pallas_call(
        body,
        name="in_proj_bwd_w",
        grid=(r // bm,),
        in_specs=[pl.BlockSpec((bm, d), lambda i: (i, 0))] + [pl.BlockSpec((bm, kb), lambda i: (i, 0)) for _ in dps],
        out_specs=pl.BlockSpec((N_DEV, d, SHARD_IN_PAD), lambda i: (0, 0, 0)),
        out_shape=jax.ShapeDtypeStruct((N_DEV, d, SHARD_IN_PAD), BF16),
        scratch_shapes=[pltpu.VMEM((d, kb * len(dps)), F32)],
        compiler_params=_params("arbitrary"),
    )(u, *dps)


def _local_step(x, loss_target, u, p, meta_f, norm_g, w_in_p, q_norm_g, w_q_p, kv_norm_g, w_kv_p, conv_w_f,
                attn_out_g, conv_out_g, w_out_f, g_final):
    nb_seq, s, d = x.shape
    tp = s + LANES
    ht = tp // 2
    tables = _rope_tables(tp)

    q, k, v = _qkv_fwd(p, w_q_p, w_kv_p, q_norm_g, kv_norm_g, tables, nb_seq, tp)
    ya, o, lse = _attn_fwd(q, k, v, p, attn_out_g)
    yc = _conv_fwd(p, conv_w_f, conv_out_g, nb_seq, tp)
    dh, dhb, d_final_g, loss_part = _out_proj_loss(ya, yc, w_out_f, x, loss_target, g_final, TOKEN_TILES)

    dcat, d_w_out = _out_proj_bwd(dhb, w_out_f, ya, yc, ht)
    send_out = d_w_out.reshape(N_DEV, SHARD_OUT, d)
    dq, dk, dv, dz_attn, d_attn_g, r_out = _attn_bwd(q, k, v, o, lse, dcat, p, attn_out_g, send_out)
    dpa, d_wq_p, d_wkv_p, d_gq, d_gkv = _qkv_bwd(p, dq, dk, dv, w_q_p, w_kv_p, q_norm_g, kv_norm_g, tables)
    d_b, d_c, d_h, dz_conv, d_conv_w, d_conv_g = _conv_bwd(p, dcat, conv_w_f, conv_out_g, nb_seq, tp)
    dps = (dpa, dz_attn, d_b, d_c, d_h, dz_conv)
    send_in = _in_proj_bwd_w(u, dps, ht // 2)
    small = (d_wq_p, d_wkv_p, d_conv_w, d_final_g, d_gq, d_gkv, d_attn_g, d_conv_g, loss_part)
    grad_x, d_meta, d_norm_g, r_in, r_small = _input_bwd(
        dps, w_in_p, x, meta_f, dh, norm_g, TOKEN_TILES, send_in, small)
    return grad_x, r_in, r_out, r_small, d_meta, d_norm_g


def kernel(x, meta_tokens, norm_g, w_in, q_norm_g, w_q_up, kv_norm_g, w_kv_up, conv_w, attn_out_g, conv_out_g, w_out, final_norm_g, loss_target, m_meta_tokens, m_norm_g, m_w_in, m_q_norm_g, m_w_q_up, m_kv_norm_g, m_w_kv_up, m_conv_w, m_attn_out_g, m_conv_out_g, m_w_out, m_final_norm_g, v_meta_tokens, v_norm_g, v_w_in, v_q_norm_g, v_w_q_up, v_kv_norm_g, v_w_kv_up, v_conv_w, v_attn_out_g, v_conv_out_g, v_w_out, v_final_norm_g):
    d = x.shape[-1]
    ht = (x.shape[1] + LANES) // 2
    u, w_in_p, meta_f = _prep_gather(x, meta_tokens, norm_g, w_in[0])
    p, w_q_p, w_kv_p, w_out_f, conv_w_f = _in_proj_gather(
        u, w_in_p, w_q_up[0], w_kv_up[0], w_out[0], conv_w[0], ht, GRP_A)
    g_final = final_norm_g.reshape(1, d)
    grad_x, r_in, r_out, r_small, d_meta, d_norm_g = _local_step(
        x, loss_target, u, p, meta_f, norm_g, w_in_p, q_norm_g, w_q_p, kv_norm_g, w_kv_p, conv_w_f,
        attn_out_g, conv_out_g, w_out_f, g_final)

    flat = lambda a: a.reshape(a.shape[-2:]) if a.ndim == 3 else a.reshape(1, -1) if a.ndim == 1 else a
    params = {
        "meta_tokens": (meta_tokens, m_meta_tokens, v_meta_tokens),
        "norm_g": (norm_g, m_norm_g, v_norm_g),
        "w_in": (w_in, m_w_in, v_w_in),
        "q_norm_g": (q_norm_g, m_q_norm_g, v_q_norm_g),
        "w_q_up": (w_q_up, m_w_q_up, v_w_q_up),
        "kv_norm_g": (kv_norm_g, m_kv_norm_g, v_kv_norm_g),
        "w_kv_up": (w_kv_up, m_w_kv_up, v_w_kv_up),
        "conv_w": (conv_w, m_conv_w, v_conv_w),
        "attn_out_g": (attn_out_g, m_attn_out_g, v_attn_out_g),
        "conv_out_g": (conv_out_g, m_conv_out_g, v_conv_out_g),
        "w_out": (w_out, m_w_out, v_w_out),
        "final_norm_g": (final_norm_g, m_final_norm_g, v_final_norm_g),
    }
    grads, loss = _reduce_tail(r_in, r_out, r_small, d_meta, d_norm_g)
    updated = _adamw(grads, {n: tuple(flat(a) for a in t) for n, t in params.items()})
    outs = [[updated[n][i].reshape(params[n][0].shape) for n, _ in PARAM_SHAPES] for i in range(4)]
    return (loss[0, 0], grad_x, *outs[0], *outs[1], *outs[2], *outs[3])
```

```python
import functools

import jax
import jax.numpy as jnp
from jax import lax
from jax.experimental import pallas as pl
from jax.experimental.pallas import tpu as pltpu

F32 = jnp.float32
BF16 = jnp.bfloat16

N_META = 16
D_MODEL = 1024
N_HEADS = 4
D_NOPE = 128
D_ROPE = 64
D_V = 128
Q_RANK = 256
KV_RANK = 128
CONV_WIDTH = 512
CONV_GROUP = 64
ROPE_THETA = 10000.0
ATTN_SCALE = (D_NOPE + D_ROPE) ** -0.5
EPS = 1e-6
NEG_INF = -1e30

ADAM_LR = 0.001
ADAM_B1 = 0.9
ADAM_B2 = 0.999
ADAM_EPS = 1e-08
ADAM_WD = 0.01
ADAM_STEP = 10

LANES = 128
PAD_FRONT = LANES - N_META
KV_TILE = 256
N_DEV = 8
VMEM_LIMIT = 56 * 1024 * 1024

IN_PAD = 3072
GRP_A = 512
N_A = Q_RANK + KV_RANK + D_ROPE
IN_PROJ = 3008
SHARD_IN = IN_PROJ // N_DEV
SHARD_IN_PAD = 384
SHARD_Q = 96
SHARD_KV = 128
SHARD_OUT = 128
SHARD_CONV = 64
SHARD_META = 128
Q_COLS = N_HEADS * (D_NOPE + D_ROPE)
KV_COLS = N_HEADS * (D_NOPE + D_V)

ROW_Q, ROW_KV, ROW_META, ROW_CONV = 0, 256, 384, 400
ROW_REPL = 408
ROW_NORM, ROW_FINAL, ROW_GQ, ROW_GKV, ROW_ATTN, ROW_CONVG, ROW_LOSS = 408, 416, 424, 426, 427, 431, 435
SMALL_ROWS = 440
ADAM_CHUNK = 64

PARAM_SHAPES = (
    ("meta_tokens", (N_META, SHARD_META)), ("norm_g", (1, D_MODEL)), ("w_in", (D_MODEL, SHARD_IN)),
    ("q_norm_g", (1, Q_RANK)), ("w_q_up", (Q_RANK, SHARD_Q)), ("kv_norm_g", (1, KV_RANK)),
    ("w_kv_up", (KV_RANK, SHARD_KV)), ("conv_w", (3, SHARD_CONV)), ("attn_out_g", (1, CONV_WIDTH)),
    ("conv_out_g", (1, CONV_WIDTH)), ("w_out", (SHARD_OUT, D_MODEL)), ("final_norm_g", (1, D_MODEL)),
)


def _in_pieces(k):
    lo, hi = SHARD_IN * k, SHARD_IN * (k + 1)
    out = []
    if lo < N_A:
        out.append((0, min(hi, N_A) - lo, lo))
    if hi > N_A:
        s = max(lo, N_A)
        out.append((s - lo, hi - lo, s + GRP_A - N_A))
    return out


def _q_pieces(k):
    lo, hi = SHARD_Q * k, SHARD_Q * (k + 1)
    out = []
    for h in range(N_HEADS):
        base = (D_NOPE + D_ROPE) * h
        s, e = max(lo, base), min(hi, base + D_NOPE)
        if s < e:
            out.append((s - lo, e - lo, D_NOPE * h + s - base))
        s, e = max(lo, base + D_NOPE), min(hi, base + D_NOPE + D_ROPE)
        if s < e:
            out.append((s - lo, e - lo, N_HEADS * D_NOPE + D_ROPE * h + s - base - D_NOPE))
    return out


def _kv_dst(k):
    return D_NOPE * (k // 2) + (N_HEADS * D_NOPE if k % 2 else 0)


def _params(*sem):
    return pltpu.CompilerParams(dimension_semantics=sem, vmem_limit_bytes=VMEM_LIMIT)


def _rms_stats(x):
    r = lax.rsqrt(jnp.mean(x * x, axis=-1, keepdims=True) + EPS)
    return x * r, r


def _rms_bwd(gdy, xhat, r):
    return r * (gdy - xhat * jnp.mean(gdy * xhat, axis=-1, keepdims=True))


def _sigmoid(z):
    return 1.0 / (1.0 + jnp.exp(-z))


def _group_mean(x):
    i0 = lax.broadcasted_iota(jnp.int32, (LANES, LANES), 0) // CONV_GROUP
    i1 = lax.broadcasted_iota(jnp.int32, (LANES, LANES), 1) // CONV_GROUP
    m = jnp.where(i0 == i1, 1.0 / CONV_GROUP, 0.0).astype(BF16)
    hi = x.astype(BF16)
    lo = (x - hi.astype(F32)).astype(BF16)
    return jnp.dot(hi, m, preferred_element_type=F32) + jnp.dot(lo, m, preferred_element_type=F32)


_NT = (((1,), (1,)), ((), ()))
_TN = (((0,), (0,)), ((), ()))


def _dot(a, b, dims=None):
    if dims is None:
        return jnp.dot(a, b, preferred_element_type=F32)
    return lax.dot_general(a, b, dims, preferred_element_type=F32)


def _device_position():
    x, y, c = lax.axis_index("x"), lax.axis_index("y"), lax.axis_index("c")
    return x, y, c, 4 * x + 2 * y + c


def _gather_plan(srcs, slots, send_sems, recv_sems, local_sems):
    x, y, c, _ = _device_position()
    me, sibling = (x, y, c), (x, y, 1 - c)
    chips = [(1 - x, y), (x, 1 - y), (1 - x, 1 - y)]
    n = len(srcs)

    def slot(a, px, py, pc):
        return slots[a].at[4 * px + 2 * py + pc]

    def copy(a, k, block, to, own=False):
        return pltpu.make_async_remote_copy(
            src_ref=srcs[a] if own else slot(a, *block),
            dst_ref=slot(a, *block),
            send_sem=send_sems.at[7 * a + k],
            recv_sem=recv_sems.at[7 * a + k],
            device_id=to,
            device_id_type=pl.DeviceIdType.MESH,
        )

    def local(a):
        return pltpu.make_async_copy(srcs[a], slot(a, *me), local_sems.at[a])

    def firsts(a):
        return [copy(a, 0, me, sibling, own=True)] + [
            copy(a, 1 + j, me, (*chip, c), own=True) for j, chip in enumerate(chips)]

    def start():
        for a in range(n):
            local(a).start()
            for cp in firsts(a):
                cp.start()

    def forward():
        for j, chip in enumerate(chips):
            for a in range(n):
                copy(a, 1 + j, (*chip, c), me).wait_recv()
                copy(a, 4 + j, (*chip, c), sibling).start()

    def finish():
        for a in range(n):
            copy(a, 0, sibling, me).wait_recv()
            for j, chip in enumerate(chips):
                copy(a, 4 + j, (*chip, 1 - c), me).wait_recv()
        for a in range(n):
            for cp in firsts(a) + [copy(a, 4 + j, (*chip, c), sibling) for j, chip in enumerate(chips)]:
                cp.wait_send()
            local(a).wait()

    return start, forward, finish


def _adam_update(g, w, m, v):
    m_new = ADAM_B1 * m + (1.0 - ADAM_B1) * g
    v_new = ADAM_B2 * v + (1.0 - ADAM_B2) * (g * g)
    m_hat = m_new / (1.0 - ADAM_B1 ** ADAM_STEP)
    v_hat = v_new / (1.0 - ADAM_B2 ** ADAM_STEP)
    return -ADAM_LR * (m_hat / (jnp.sqrt(v_hat) + ADAM_EPS) + ADAM_WD * w), m_new, v_new


def _adamw(grads, params):
    names = [n for n, _ in PARAM_SHAPES]
    n_p = len(names)

    def body(*refs):
        for i in range(n_p):
            g = refs[i][...]
            w, m, v = (refs[n_p + 3 * i + j][...] for j in range(3))
            delta, m_new, v_new = _adam_update(g, w, m, v)
            for j, val in enumerate((g, delta, m_new, v_new)):
                refs[4 * n_p + 4 * i + j][...] = val

    vm = pl.BlockSpec(memory_space=pltpu.VMEM)
    out_shape = []
    for _, shape in PARAM_SHAPES:
        out_shape += [jax.ShapeDtypeStruct(shape, F32)] * 4
    outs = pl.pallas_call(
        body,
        name="adamw",
        out_shape=tuple(out_shape),
        in_specs=[vm] * (4 * n_p),
        out_specs=(vm,) * (4 * n_p),
        compiler_params=pltpu.CompilerParams(vmem_limit_bytes=VMEM_LIMIT),
    )(*[grads[n] for n in names], *[a for n in names for a in params[n]])
    return {n: outs[4 * i:4 * i + 4] for i, n in enumerate(names)}


N_CHIPS = 4


def _reduce_plan(pays, owns, r1s, sums, r2s, send1, recv1, send2, recv2, local_sems):
    x, y, c, _ = _device_position()
    sibling = (x, y, 1 - c)
    chips = [((1 - x if rj & 2 else x), (1 - y if rj & 1 else y)) for rj in range(N_CHIPS)]
    n = len(pays)

    def slot_of(rj, core):
        return 4 * chips[rj][0] + 2 * chips[rj][1] + core

    def to_sibling(a, rj):
        return pltpu.make_async_remote_copy(
            src_ref=pays[a].at[slot_of(rj, 1 - c)], dst_ref=r1s[a].at[rj],
            send_sem=send1.at[N_CHIPS * a + rj], recv_sem=recv1.at[N_CHIPS * a + rj],
            device_id=sibling, device_id_type=pl.DeviceIdType.MESH)

    def load_own(a, rj):
        return pltpu.make_async_copy(pays[a].at[slot_of(rj, c)], owns[a].at[rj], local_sems.at[2 * N_CHIPS * a + rj])

    def to_chip(a, rj):
        return pltpu.make_async_remote_copy(
            src_ref=sums[a].at[rj], dst_ref=r2s[a].at[rj],
            send_sem=send2.at[N_CHIPS * a + rj], recv_sem=recv2.at[N_CHIPS * a + rj],
            device_id=(*chips[rj], c), device_id_type=pl.DeviceIdType.MESH)

    def keep(a):
        return pltpu.make_async_copy(sums[a].at[0], r2s[a].at[0], local_sems.at[2 * N_CHIPS * a + N_CHIPS])

    def start():
        for a in range(n):
            for rj in range(N_CHIPS):
                to_sibling(a, rj).start()
                if owns[a] is not None:
                    load_own(a, rj).start()

    def combine():
        for a in range(n):
            for rj in range(N_CHIPS):
                to_sibling(a, rj).wait_recv()
                if owns[a] is not None:
                    load_own(a, rj).wait()
                    mine = owns[a][rj]
                else:
                    mine = pays[a][slot_of(rj, c)]
                sums[a][rj] = (mine.astype(F32) + r1s[a][rj].astype(F32)).astype(sums[a].dtype)
            keep(a).start()
            for rj in range(1, N_CHIPS):
                to_chip(a, rj).start()

    def finish():
        for a in range(n):
            for rj in range(1, N_CHIPS):
                to_chip(a, rj).wait_recv()
            for rj in range(N_CHIPS):
                to_sibling(a, rj).wait_send()
            for rj in range(1, N_CHIPS):
                to_chip(a, rj).wait_send()
            keep(a).wait()

    return start, combine, finish


def _reduce_scratch(shapes_dtypes, own_flags):
    out = []
    for (shape, dtype), own in zip(shapes_dtypes, own_flags):
        if own:
            out.append(pltpu.VMEM((N_CHIPS,) + shape, dtype))
        out += [pltpu.VMEM((N_CHIPS,) + shape, dtype), pltpu.VMEM((N_CHIPS,) + shape, dtype)]
    n = len(shapes_dtypes)
    out += [pltpu.SemaphoreType.DMA((N_CHIPS * n,))] * 4 + [pltpu.SemaphoreType.DMA((2 * N_CHIPS * n,))]
    return out


def _pack_small(ssmall, dwq, dwkv, dconv, dfinal, dgq, dgkv, dattn, dconvg, loss_part):
    ssmall[...] = jnp.zeros_like(ssmall)
    rep = ssmall.at[0]
    for i in range(D_MODEL // LANES):
        rep[ROW_FINAL + i:ROW_FINAL + i + 1, :] = dfinal[:, LANES * i:LANES * (i + 1)]
    for i in range(Q_RANK // LANES):
        rep[ROW_GQ + i:ROW_GQ + i + 1, :] = dgq[:, LANES * i:LANES * (i + 1)]
    rep[ROW_GKV:ROW_GKV + 1, :] = dgkv[...]
    for i in range(CONV_WIDTH // LANES):
        rep[ROW_ATTN + i:ROW_ATTN + i + 1, :] = dattn[:, LANES * i:LANES * (i + 1)]
        rep[ROW_CONVG + i:ROW_CONVG + i + 1, :] = dconvg[:, LANES * i:LANES * (i + 1)]
    rep[ROW_LOSS:ROW_LOSS + 1, :] = loss_part[...]
    for k in range(N_DEV):
        if k:
            ssmall[k, ROW_REPL:, :] = ssmall[0, ROW_REPL:, :]
        for s, e, d in _q_pieces(k):
            ssmall[k, ROW_Q:ROW_Q + Q_RANK, s:e] = dwq[:, d:d + e - s]
        ssmall[k, ROW_KV:ROW_KV + KV_RANK, :] = dwkv[:, _kv_dst(k):_kv_dst(k) + SHARD_KV]
        ssmall[k, ROW_CONV:ROW_CONV + 3, 0:SHARD_CONV] = dconv[0:3, SHARD_CONV * k:SHARD_CONV * (k + 1)]


TOKEN_TILES = 4
TAIL_ROWS = N_META + D_MODEL // LANES


def _reduce_tail(r_in, r_out, r_small, d_meta, d_norm):
    n_p = len(PARAM_SHAPES)
    names = [n for n, _ in PARAM_SHAPES]

    def body(*refs):
        rin, rout, rsmall, dmeta, dnorm = refs[:5]
        g_out = {n: refs[5 + i] for i, n in enumerate(names)}
        loss_out = refs[5 + n_p]
        stail, rtail, gsum, gtail, send_sems, recv_sems = refs[6 + n_p:]
        x, y, c, me = _device_position()
        my_chip = 2 * x + y

        for k in range(N_DEV):
            stail[k, 0:N_META, :] = dmeta[:, SHARD_META * k:SHARD_META * (k + 1)]
            for i in range(D_MODEL // LANES):
                stail[k, N_META + i:N_META + i + 1, :] = dnorm[:, LANES * i:LANES * (i + 1)]
        copies = []
        for r in range(1, N_DEV):
            peer = (1 - x if r & 4 else x, 1 - y if r & 2 else y, 1 - c if r & 1 else c)
            copies.append(pltpu.make_async_remote_copy(
                src_ref=stail.at[4 * peer[0] + 2 * peer[1] + peer[2]],
                dst_ref=rtail.at[r],
                send_sem=send_sems.at[r - 1],
                recv_sem=recv_sems.at[r - 1],
                device_id=peer,
                device_id_type=pl.DeviceIdType.MESH,
            ))
        for cp in copies:
            cp.start()
        rtail[0] = stail[me]

        def chunk(i, carry):
            sl = pl.ds(pl.multiple_of(i * ADAM_CHUNK, ADAM_CHUNK), ADAM_CHUNK)
            g = rin[my_chip, sl, :].astype(F32)
            for ch in range(1, N_CHIPS):
                g = g + rin[ch ^ my_chip, sl, :].astype(F32)
            g_out["w_in"][sl, :] = g[:, :SHARD_IN]
            return carry

        lax.fori_loop(0, D_MODEL // ADAM_CHUNK, chunk, 0)

        g = rout[my_chip].astype(F32)
        gs = rsmall[my_chip]
        for ch in range(1, N_CHIPS):
            g = g + rout[ch ^ my_chip].astype(F32)
            gs = gs + rsmall[ch ^ my_chip]
        g_out["w_out"][...] = g
        gsum[...] = gs
        g_out["w_q_up"][...] = gsum[ROW_Q:ROW_Q + Q_RANK, 0:SHARD_Q]
        g_out["w_kv_up"][...] = gsum[ROW_KV:ROW_KV + KV_RANK, :]
        g_out["conv_w"][...] = gsum[ROW_CONV:ROW_CONV + 3, 0:SHARD_CONV]
        for name, row, width in (("final_norm_g", ROW_FINAL, D_MODEL), ("q_norm_g", ROW_GQ, Q_RANK),
                                 ("kv_norm_g", ROW_GKV, KV_RANK), ("attn_out_g", ROW_ATTN, CONV_WIDTH),
                                 ("conv_out_g", ROW_CONVG, CONV_WIDTH)):
            for i in range(width // LANES):
                g_out[name][:, LANES * i:LANES * (i + 1)] = gsum[row + i:row + i + 1, :]
        loss_out[...] = gsum[ROW_LOSS:ROW_LOSS + 1, :]

        for cp in copies:
            cp.wait_recv()
        gt = rtail[me]
        for d in range(1, N_DEV):
            gt = gt + rtail[d ^ me]
        gtail[...] = gt
        g_out["meta_tokens"][...] = gtail[0:N_META, :]
        for i in range(D_MODEL // LANES):
            g_out["norm_g"][:, LANES * i:LANES * (i + 1)] = gtail[N_META + i:N_META + i + 1, :]
        for cp in copies:
            cp.wait_send()

    vm = pl.BlockSpec(memory_space=pltpu.VMEM)
    out_shape = [jax.ShapeDtypeStruct(shape, F32) for _, shape in PARAM_SHAPES]
    out_shape.append(jax.ShapeDtypeStruct((1, LANES), F32))
    outs = pl.pallas_call(
        body,
        name="reduce_tail",
        out_shape=tuple(out_shape),
        in_specs=[vm] * 5,
        out_specs=(vm,) * len(out_shape),
        scratch_shapes=[
            pltpu.VMEM((N_DEV, TAIL_ROWS, LANES), F32),
            pltpu.VMEM((N_DEV, TAIL_ROWS, LANES), F32),
            pltpu.VMEM((SMALL_ROWS, LANES), F32),
            pltpu.VMEM((TAIL_ROWS, LANES), F32),
            pltpu.SemaphoreType.DMA((N_DEV - 1,)),
            pltpu.SemaphoreType.DMA((N_DEV - 1,)),
        ],
        compiler_params=pltpu.CompilerParams(vmem_limit_bytes=VMEM_LIMIT),
    )(r_in, r_out, r_small, d_meta, d_norm)
    return {n: outs[i] for i, n in enumerate(names)}, outs[-1]


def _prep_gather(x, meta, norm_g, w_in):
    nb_seq, s, d = x.shape
    nb = s // LANES + 1
    forward_step = nb_seq * (nb // 3)
    finish_step = nb_seq * (nb - 1)

    def body(x_ref, meta_ref, g_ref, win_ref, u_ref, w_in_p, meta_f,
             sbig, ssmall, gbig, gsmall, send_sems, recv_sems, local_sems):
        jj, b = pl.program_id(0), pl.program_id(1)
        t = jj * nb_seq + b

        def plan():
            return _gather_plan((sbig, ssmall), (gbig, gsmall), send_sems, recv_sems, local_sems)

        @pl.when(t == 0)
        def _():
            sbig[:, 0:SHARD_IN] = win_ref[...].astype(BF16)
            sbig[:, SHARD_IN:] = jnp.zeros((d, SHARD_IN_PAD - SHARD_IN), BF16)
            ssmall[...] = meta_ref[...]
            plan()[0]()

        @pl.when(t == forward_step)
        def _():
            plan()[1]()

        @pl.when(t == finish_step)
        def _():
            plan()[2]()
            w_in_p[:, N_A:GRP_A] = jnp.zeros((d, GRP_A - N_A), BF16)
            for k in range(N_DEV):
                for s0, e0, d0 in _in_pieces(k):
                    w_in_p[:, d0:d0 + e0 - s0] = gbig[k, :, s0:e0]
                meta_f[:, SHARD_META * k:SHARD_META * (k + 1)] = gsmall[k]

        def norm(h):
            hhat, _ = _rms_stats(h)
            return (hhat * g_ref[...]).astype(BF16)

        @pl.when(jj < nb - 1)
        def _():
            u_ref[...] = norm(x_ref[0])

        @pl.when(jj == nb - 1)
        def _():
            u_ref[0:PAD_FRONT, :] = jnp.zeros((PAD_FRONT, d), BF16)
            u_ref[PAD_FRONT:LANES, :] = norm(meta_f[...])

    whole = lambda shape: pl.BlockSpec(shape, lambda jj, b: (0,) * len(shape))
    return pl.pallas_call(
        body,
        name="prep_norm_gather",
        grid=(nb, nb_seq),
        in_specs=[
            pl.BlockSpec((1, LANES, d), lambda jj, b: (b, jnp.minimum(jj, nb - 2), 0)),
            whole(meta.shape), whole(norm_g.shape), whole(w_in.shape),
        ],
        out_specs=(pl.BlockSpec((LANES, d), lambda jj, b: (b * nb + (jj + 1) % nb, 0)),
                   whole((d, IN_PAD)), whole((N_META, d))),
        out_shape=(jax.ShapeDtypeStruct((nb_seq * nb * LANES, d), BF16),
                   jax.ShapeDtypeStruct((d, IN_PAD), BF16),
                   jax.ShapeDtypeStruct((N_META, d), F32)),
        scratch_shapes=[
            pltpu.VMEM((d, SHARD_IN_PAD), BF16),
            pltpu.VMEM((N_META, SHARD_META), F32),
            pltpu.VMEM((N_DEV, d, SHARD_IN_PAD), BF16),
            pltpu.VMEM((N_DEV, N_META, SHARD_META), F32),
            pltpu.SemaphoreType.DMA((14,)),
            pltpu.SemaphoreType.DMA((14,)),
            pltpu.SemaphoreType.DMA((2,)),
        ],
        compiler_params=_params("arbitrary", "arbitrary"),
    )(x, meta, norm_g, w_in)


def _in_proj_gather(u, w_in_p, w_q, w_kv, w_out, conv_w, bm, bn):
    m, k_dim = u.shape
    n = w_in_p.shape[1]
    steps = (m // bm) * (n // bn)
    forward_step = steps // 3
    qkv_rows = Q_RANK + KV_RANK

    def body(a_ref, b_ref, wq_ref, wkv_ref, wout_ref, conv_ref, o_ref, w_q_p, w_kv_p, w_out_f, conv_f,
             sqkv, sout, sconv, gqkv, gout, gconv, send_sems, recv_sems, local_sems):
        t = pl.program_id(0) * (n // bn) + pl.program_id(1)

        def plan():
            return _gather_plan((sqkv, sout, sconv), (gqkv, gout, gconv), send_sems, recv_sems, local_sems)

        @pl.when(t == 0)
        def _():
            sqkv[...] = jnp.zeros_like(sqkv)
            sqkv[0:Q_RANK, 0:SHARD_Q] = wq_ref[...].astype(BF16)
            sqkv[Q_RANK:, :] = wkv_ref[...].astype(BF16)
            sout[...] = wout_ref[...].astype(BF16)
            sconv[...] = jnp.zeros_like(sconv)
            sconv[0:3, 0:SHARD_CONV] = conv_ref[...]
            plan()[0]()

        @pl.when(t == forward_step)
        def _():
            plan()[1]()

        o_ref[...] = _dot(a_ref[...], b_ref[...]).astype(o_ref.dtype)

        @pl.when(t == steps - 1)
        def _():
            plan()[2]()
            conv_f[...] = jnp.zeros_like(conv_f)
            for k in range(N_DEV):
                for s0, e0, d0 in _q_pieces(k):
                    w_q_p[:, d0:d0 + e0 - s0] = gqkv[k, 0:Q_RANK, s0:e0]
                w_kv_p[:, _kv_dst(k):_kv_dst(k) + SHARD_KV] = gqkv[k, Q_RANK:, :]
                w_out_f[SHARD_OUT * k:SHARD_OUT * (k + 1), :] = gout[k]
                conv_f[0:3, SHARD_CONV * k:SHARD_CONV * (k + 1)] = gconv[k, 0:3, 0:SHARD_CONV]

    whole = lambda shape: pl.BlockSpec(shape, lambda i, j: (0,) * len(shape))
    return pl.pallas_call(
        body,
        name="in_proj_gather",
        grid=(m // bm, n // bn),
        in_specs=[pl.BlockSpec((bm, k_dim), lambda i, j: (i, 0)), pl.BlockSpec((k_dim, bn), lambda i, j: (0, j)),
                  whole(w_q.shape), whole(w_kv.shape), whole(w_out.shape), whole(conv_w.shape)],
        out_specs=(pl.BlockSpec((bm, bn), lambda i, j: (i, j)),
                   whole((Q_RANK, Q_COLS)), whole((KV_RANK, KV_COLS)), whole((D_MODEL, D_MODEL)),
                   whole((8, CONV_WIDTH))),
        out_shape=(jax.ShapeDtypeStruct((m, n), BF16),
                   jax.ShapeDtypeStruct((Q_RANK, Q_COLS), BF16),
                   jax.ShapeDtypeStruct((KV_RANK, KV_COLS), BF16),
                   jax.ShapeDtypeStruct((D_MODEL, D_MODEL), BF16),
                   jax.ShapeDtypeStruct((8, CONV_WIDTH), F32)),
        scratch_shapes=[
            pltpu.VMEM((qkv_rows, LANES), BF16),
            pltpu.VMEM((SHARD_OUT, D_MODEL), BF16),
            pltpu.VMEM((8, LANES), F32),
            pltpu.VMEM((N_DEV, qkv_rows, LANES), BF16),
            pltpu.VMEM((N_DEV, SHARD_OUT, D_MODEL), BF16),
            pltpu.VMEM((N_DEV, 8, LANES), F32),
            pltpu.SemaphoreType.DMA((21,)),
            pltpu.SemaphoreType.DMA((21,)),
            pltpu.SemaphoreType.DMA((3,)),
        ],
        compiler_params=_params("arbitrary", "arbitrary"),
    )(u, w_in_p, w_q, w_kv, w_out, conv_w)


def _rope_tables(tp):
    half = D_ROPE // 2
    inv_freq = 1.0 / (ROPE_THETA ** (jnp.arange(half, dtype=F32) / half))
    pos = (jnp.arange(tp) - PAD_FRONT).astype(F32)
    ang = pos[:, None] * inv_freq[None, :]
    cos = jnp.tile(jnp.cos(ang), (1, LANES // half))
    sin = jnp.tile(jnp.sin(ang), (1, LANES // half))
    first = (jnp.arange(LANES) % D_ROPE) < half
    return cos, jnp.where(first, -sin, 0.0), jnp.where(first, 0.0, sin)


def _rope(t, cos, sa, sb):
    return t * cos + pltpu.roll(t, LANES - D_ROPE // 2, 1) * sa + pltpu.roll(t, D_ROPE // 2, 1) * sb


def _rope_t(t, cos, sa, sb):
    return t * cos + pltpu.roll(t * sa, D_ROPE // 2, 1) + pltpu.roll(t * sb, LANES - D_ROPE // 2, 1)


def _qkv_fwd(p, wq, wkv, gq, gkv, tables, nb_seq, tp):
    ht = tp // 2

    def body(pa_ref, wq_ref, wkv_ref, gq_ref, gkv_ref, cos_ref, sa_ref, sb_ref, q_ref, k_ref, v_ref):
        pa = pa_ref[...].astype(F32)
        cq_hat, _ = _rms_stats(pa[:, :Q_RANK])
        ckv_hat, _ = _rms_stats(pa[:, Q_RANK:Q_RANK + KV_RANK])
        q = _dot((cq_hat * gq_ref[...]).astype(BF16), wq_ref[...])
        kv = _dot((ckv_hat * gkv_ref[...]).astype(BF16), wkv_ref[...])
        tabs = (cos_ref[...], sa_ref[...], sb_ref[...])
        lane = lax.broadcasted_iota(jnp.int32, (ht, LANES), 1)
        low = lane < D_ROPE
        mark = lane == D_ROPE
        row = (pl.program_id(0) % 2) * ht + lax.broadcasted_iota(jnp.int32, (ht, LANES), 0)
        k_pe = jnp.where(mark & (row < PAD_FRONT), NEG_INF, _rope(pa[:, Q_RANK + KV_RANK:], *tabs))
        one = jnp.where(mark & (row >= PAD_FRONT), 1.0, 0.0)
        pairs = [_rope(q[:, N_HEADS * D_NOPE + LANES * i:N_HEADS * D_NOPE + LANES * (i + 1)], *tabs) for i in range(2)]
        for h in range(N_HEADS):
            pair = pairs[h // 2]
            if h % 2:
                pair = pltpu.roll(pair, D_ROPE, 1)
            pe = jnp.where(low, pair, one)
            q_ref[0, h] = jnp.concatenate([q[:, D_NOPE * h:D_NOPE * (h + 1)], pe], axis=1).astype(BF16)
            k_ref[0, h] = jnp.concatenate([kv[:, D_NOPE * h:D_NOPE * (h + 1)], k_pe], axis=1).astype(BF16)
            v_ref[0, h] = kv[:, N_HEADS * D_NOPE + D_V * h:N_HEADS * D_NOPE + D_V * (h + 1)].astype(BF16)

    full = lambda a: pl.BlockSpec(a.shape, lambda i: (0,) * a.ndim)
    tab = pl.BlockSpec((ht, LANES), lambda i: (i % 2, 0))
    qk = pl.BlockSpec((1, N_HEADS, ht, 2 * LANES), lambda i: (i // 2, 0, i % 2, 0))
    return pl.pallas_call(
        body,
        name="qkv_fwd",
        grid=(2 * nb_seq,),
        in_specs=[pl.BlockSpec((ht, GRP_A), lambda i: (i, 0)), full(wq), full(wkv), full(gq), full(gkv), tab, tab, tab],
        out_specs=(qk, qk, pl.BlockSpec((1, N_HEADS, ht, D_V), lambda i: (i // 2, 0, i % 2, 0))),
        out_shape=(
            jax.ShapeDtypeStruct((nb_seq, N_HEADS, tp, 2 * LANES), BF16),
            jax.ShapeDtypeStruct((nb_seq, N_HEADS, tp, 2 * LANES), BF16),
            jax.ShapeDtypeStruct((nb_seq, N_HEADS, tp, D_V), BF16),
        ),
        compiler_params=_params("parallel"),
    )(p, wq, wkv, gq, gkv, *tables)


def _attn_fwd(q, k, v, p, g_attn):
    nb_seq, _, tp, _ = q.shape

    def body(q_ref, k_ref, v_ref, z_ref, g_ref, y_ref, o_ref, lse_ref):
        g = g_ref[...]
        for r0 in range(0, tp, KV_TILE):
            nq = min(KV_TILE, tp - r0)
            kend = r0 + nq
            qq = q_ref[0, 0, r0:kend, :]
            sd = _dot(qq, k_ref[0, 0, r0:kend, :], _NT) * ATTN_SCALE
            causal = (lax.broadcasted_iota(jnp.int32, (nq, nq), 1) <= lax.broadcasted_iota(jnp.int32, (nq, nq), 0))
            sd = jnp.where(causal, sd, NEG_INF)
            m = jnp.max(sd, axis=-1, keepdims=True)
            if r0:
                so = _dot(qq, k_ref[0, 0, 0:r0, :], _NT) * ATTN_SCALE
                m = jnp.maximum(m, jnp.max(so, axis=-1, keepdims=True))
            ed = jnp.exp(sd - m)
            l = jnp.sum(ed, axis=-1, keepdims=True)
            o = _dot(ed.astype(BF16), v_ref[0, 0, r0:kend, :])
            if r0:
                eo = jnp.exp(so - m)
                l = l + jnp.sum(eo, axis=-1, keepdims=True)
                o = o + _dot(eo.astype(BF16), v_ref[0, 0, 0:r0, :])
            o = o * (1.0 / l)
            o_ref[0, 0, r0:kend, :] = o
            lse_ref[0, 0, r0:kend, :] = jnp.broadcast_to(m + jnp.log(l), (nq, LANES))
            ohat, _ = _rms_stats(o)
            z = z_ref[r0:kend, :].astype(F32)
            y_ref[r0:kend, :] = (ohat * g * (z * _sigmoid(z))).astype(BF16)

    qk = pl.BlockSpec((1, 1, tp, 2 * LANES), lambda b, h: (b, h, 0, 0))
    hv = pl.BlockSpec((1, 1, tp, D_V), lambda b, h: (b, h, 0, 0))
    return pl.pallas_call(
        body,
        name="attn_fwd",
        grid=(nb_seq, N_HEADS),
        in_specs=[qk, qk, hv,
                  pl.BlockSpec((tp, LANES), lambda b, h: (b, GRP_A // LANES + h)),
                  pl.BlockSpec((1, LANES), lambda b, h: (0, h))],
        out_specs=(pl.BlockSpec((tp, LANES), lambda b, h: (b, h)), hv, hv),
        out_shape=(
            jax.ShapeDtypeStruct((nb_seq * tp, N_HEADS * D_V), BF16),
            jax.ShapeDtypeStruct((nb_seq, N_HEADS, tp, D_V), F32),
            jax.ShapeDtypeStruct((nb_seq, N_HEADS, tp, LANES), F32),
        ),
        compiler_params=_params("parallel", "parallel"),
    )(q, k, v, p, g_attn)


_CONV_COL0 = (GRP_A + N_HEADS * D_V) // LANES


def _conv_specs(tp, order):
    cols = CONV_WIDTH // LANES
    return [pl.BlockSpec((tp, LANES), functools.partial(
        lambda a, b, off: order(a, b, off), off=_CONV_COL0 + i * cols)) for i in range(4)]


def _conv_fwd(p, conv_w, g_conv, nb_seq, tp):
    def body(b_ref, c_ref, h_ref, z_ref, w_ref, g_ref, y_ref):
        cc = c_ref[...].astype(F32) * h_ref[...].astype(F32)
        row = lax.broadcasted_iota(jnp.int32, (tp, LANES), 0)
        s1 = jnp.where(row >= 1, pltpu.roll(cc, 1, 0), 0.0)
        s2 = jnp.where(row >= 2, pltpu.roll(cc, 2, 0), 0.0)
        yc = b_ref[...].astype(F32) * (w_ref[0:1, :] * s2 + w_ref[1:2, :] * s1 + w_ref[2:3, :] * cc)
        r = lax.rsqrt(_group_mean(yc * yc) + EPS)
        z = z_ref[...].astype(F32)
        y_ref[...] = (yc * r * g_ref[...] * (z * _sigmoid(z))).astype(BF16)

    return pl.pallas_call(
        body,
        name="conv_fwd",
        grid=(nb_seq, CONV_WIDTH // LANES),
        in_specs=_conv_specs(tp, lambda b, t, off: (b, off + t)) + [
            pl.BlockSpec((8, LANES), lambda b, t: (0, t)),
            pl.BlockSpec((1, LANES), lambda b, t: (0, t))],
        out_specs=pl.BlockSpec((tp, LANES), lambda b, t: (b, t)),
        out_shape=jax.ShapeDtypeStruct((nb_seq * tp, CONV_WIDTH), BF16),
        compiler_params=_params("parallel", "parallel"),
    )(p, p, p, p, conv_w, g_conv)


def _token_copy(hbm, b, k, ts, buf, sem, to_hbm=False):
    lo, hi = max(k * ts - LANES, 0), (k + 1) * ts - LANES
    off = lo - (k * ts - LANES)
    src, dst = hbm.at[b, pl.ds(lo, hi - lo)], buf.at[pl.ds(off, hi - lo)]
    if to_hbm:
        src, dst = dst, src
    return pltpu.make_async_copy(src, dst, sem)


def _for_tile(k, nt, fn):
    for kk in range(nt):
        @pl.when(k == kk)
        def _(kk=kk):
            fn(kk)


def _out_proj_loss(ya, yc, w_out, x, target, g_final, nt):
    nb_seq, s, d = x.shape
    r, ka = ya.shape
    ts = (s + LANES) // nt
    steps = nb_seq * nt

    def body(a_ref, c_ref, w_ref, x_hbm, t_hbm, g_ref, dh_ref, dhb_ref, dg_ref, loss_ref,
             xbuf, tbuf, acc_ref, sems):
        i = pl.program_id(0)
        b, k = i // nt, i % nt

        @pl.when(i == 0)
        def _():
            acc_ref[...] = jnp.zeros_like(acc_ref)
            dg_ref[...] = jnp.zeros_like(dg_ref)

        slot = i % 2

        def fetch(seq, kk, sl):
            return [_token_copy(x_hbm, seq, kk, ts, xbuf.at[sl], sems.at[sl, 0]),
                    _token_copy(t_hbm, seq, kk, ts, tbuf.at[sl], sems.at[sl, 1])]

        def start(seq, sl, kk):
            if kk == 0:
                xbuf[sl, 0:LANES, :] = jnp.zeros((LANES, d), F32)
                tbuf[sl, 0:LANES, :] = jnp.zeros((LANES, d), F32)
            for cp in fetch(seq, kk, sl):
                cp.start()

        @pl.when(i == 0)
        def _():
            start(0, 0, 0)

        @pl.when(i + 1 < steps)
        def _():
            _for_tile((i + 1) % nt, nt, functools.partial(start, (i + 1) // nt, 1 - slot))

        mix = _dot(a_ref[...], w_ref[0:ka, :]) + _dot(c_ref[...], w_ref[ka:, :])
        _for_tile(k, nt, lambda kk: [cp.wait() for cp in fetch(b, kk, slot)])

        real = (lax.broadcasted_iota(jnp.int32, (ts, d), 0) >= LANES) | (k > 0)
        g = g_ref[...]
        hhat, rstd = _rms_stats(xbuf[slot] + mix)
        e = jnp.where(real, hhat * g - tbuf[slot], 0.0)
        acc_ref[...] += jnp.sum(e * e, axis=0, keepdims=True)
        dy = e * (1.0 / d)
        dg_ref[...] += jnp.sum(dy * hhat, axis=0, keepdims=True)
        dh = _rms_bwd(g * dy, hhat, rstd)
        dh_ref[...] = dh
        dhb_ref[...] = dh.astype(BF16)

        @pl.when(i == steps - 1)
        def _():
            total = jnp.sum(acc_ref[...], axis=1, keepdims=True)
            loss_ref[...] = jnp.broadcast_to((0.5 / d) * total, loss_ref.shape)

    hbm = pl.BlockSpec(memory_space=pl.ANY)
    row = pl.BlockSpec((ts, d), lambda i: (i, 0))
    vec = pl.BlockSpec((1, d), lambda i: (0, 0))
    return pl.pallas_call(
        body,
        name="out_proj_loss",
        grid=(steps,),
        in_specs=[pl.BlockSpec((ts, ka), lambda i: (i, 0)), pl.BlockSpec((ts, yc.shape[1]), lambda i: (i, 0)),
                  pl.BlockSpec(w_out.shape, lambda i: (0, 0)), hbm, hbm, vec],
        out_specs=(row, row, vec, pl.BlockSpec((1, LANES), lambda i: (0, 0))),
        out_shape=(
            jax.ShapeDtypeStruct((r, d), F32),
            jax.ShapeDtypeStruct((r, d), BF16),
            jax.ShapeDtypeStruct((1, d), F32),
            jax.ShapeDtypeStruct((1, LANES), F32),
        ),
        scratch_shapes=[pltpu.VMEM((2, ts, d), F32), pltpu.VMEM((2, ts, d), F32), pltpu.VMEM((1, d), F32),
                        pltpu.SemaphoreType.DMA((2, 2))],
        compiler_params=_params("arbitrary"),
    )(ya, yc, w_out, x, target, g_final)


def _out_proj_bwd(dhb, w_out, ya, yc, bm):
    r, d = dhb.shape
    ka = ya.shape[1]
    n_mix = w_out.shape[0]
    last = r // bm - 1

    def body(dh_ref, w_ref, a_ref, c_ref, dcat_ref, dw_ref, acc_ref):
        @pl.when(pl.program_id(0) == 0)
        def _():
            acc_ref[...] = jnp.zeros_like(acc_ref)

        dh = dh_ref[...]
        dcat_ref[...] = _dot(dh, w_ref[...], _NT).astype(BF16)
        acc_ref[0:ka, :] += _dot(a_ref[...], dh, _TN)
        acc_ref[ka:, :] += _dot(c_ref[...], dh, _TN)

        @pl.when(pl.program_id(0) == last)
        def _():
            dw_ref[...] = acc_ref[...].astype(BF16)

    return pl.pallas_call(
        body,
        name="out_proj_bwd",
        grid=(r // bm,),
        in_specs=[pl.BlockSpec((bm, d), lambda i: (i, 0)), pl.BlockSpec(w_out.shape, lambda i: (0, 0)),
                  pl.BlockSpec((bm, ka), lambda i: (i, 0)), pl.BlockSpec((bm, yc.shape[1]), lambda i: (i, 0))],
        out_specs=(pl.BlockSpec((bm, n_mix), lambda i: (i, 0)),
                   pl.BlockSpec((n_mix, d), lambda i: (0, 0))),
        out_shape=(jax.ShapeDtypeStruct((r, n_mix), BF16),
                   jax.ShapeDtypeStruct((n_mix, d), BF16)),
        scratch_shapes=[pltpu.VMEM((n_mix, d), F32)],
        compiler_params=_params("arbitrary"),
    )(dhb, w_out, ya, yc)


def _attn_bwd(q, k, v, o, lse, dcat, p, g_attn, send_out):
    nb_seq, _, tp, _ = q.shape
    steps = N_HEADS * nb_seq

    def body(q_ref, k_ref, v_ref, o_ref, lse_ref, dy_ref, z_ref, g_ref, pay_ref,
             dq_ref, dk_ref, dv_ref, dz_ref, dg_ref, r2_ref, dq_acc, r1, sums, *sems):
        t = pl.program_id(0) * nb_seq + pl.program_id(1)

        def plan():
            return _reduce_plan((pay_ref,), (None,), (r1,), (sums,), (r2_ref,), *sems)

        @pl.when(t == 0)
        def _():
            plan()[0]()

        @pl.when(t == 1)
        def _():
            plan()[1]()

        @pl.when(pl.program_id(1) == 0)
        def _():
            dg_ref[...] = jnp.zeros_like(dg_ref)

        g = g_ref[...]
        z = z_ref[...].astype(F32)
        o = o_ref[0, 0]
        dy = dy_ref[...].astype(F32)
        sig = _sigmoid(z)
        ohat, r = _rms_stats(o)
        don = dy * (z * sig)
        dz_ref[...] = (dy * (ohat * g) * (sig * (1.0 + z * (1.0 - sig)))).astype(BF16)
        dg_ref[...] += jnp.sum(don * ohat, axis=0, keepdims=True)
        do = _rms_bwd(g * don, ohat, r)
        dvec = jnp.sum(do * o, axis=-1, keepdims=True)
        dob = do.astype(BF16)
        lse_col = lse_ref[0, 0, :, 0:1]
        dq_acc[...] = jnp.zeros_like(dq_acc)
        for k0 in range(0, tp, KV_TILE):
            nk = min(KV_TILE, tp - k0)
            nq = tp - k0
            qq = q_ref[0, 0, k0:, :]
            kk = k_ref[0, 0, k0:k0 + nk, :]
            causal = (lax.broadcasted_iota(jnp.int32, (nq, nk), 1) <= lax.broadcasted_iota(jnp.int32, (nq, nk), 0))
            pr = jnp.where(causal, jnp.exp(_dot(qq, kk, _NT) * ATTN_SCALE - lse_col[k0:]), 0.0)
            dp = _dot(dob[k0:], v_ref[0, 0, k0:k0 + nk, :], _NT)
            ds = (pr * (dp - dvec[k0:]) * ATTN_SCALE).astype(BF16)
            dv_ref[0, 0, k0:k0 + nk, :] = _dot(pr.astype(BF16), dob[k0:], _TN).astype(BF16)
            dk_ref[0, 0, k0:k0 + nk, :] = _dot(ds, qq, _TN).astype(BF16)
            dq_acc[k0:, :] += _dot(ds, kk)
        dq_ref[0, 0] = dq_acc[...].astype(BF16)

        @pl.when(t == steps - 1)
        def _():
            plan()[2]()

    qk = pl.BlockSpec((1, 1, tp, 2 * LANES), lambda h, b: (b, h, 0, 0))
    hv = pl.BlockSpec((1, 1, tp, D_V), lambda h, b: (b, h, 0, 0))
    col = pl.BlockSpec((tp, LANES), lambda h, b: (b, h))
    slot = send_out.shape[1:]
    return pl.pallas_call(
        body,
        name="attn_bwd",
        grid=(N_HEADS, nb_seq),
        in_specs=[qk, qk, hv, hv, hv, col,
                  pl.BlockSpec((tp, LANES), lambda h, b: (b, GRP_A // LANES + h)),
                  pl.BlockSpec((1, LANES), lambda h, b: (0, h)),
                  pl.BlockSpec(send_out.shape, lambda h, b: (0, 0, 0))],
        out_specs=(qk, qk, hv, col, pl.BlockSpec((1, LANES), lambda h, b: (0, h)),
                   pl.BlockSpec(memory_space=pl.ANY)),
        out_shape=(
            jax.ShapeDtypeStruct((nb_seq, N_HEADS, tp, 2 * LANES), BF16),
            jax.ShapeDtypeStruct((nb_seq, N_HEADS, tp, 2 * LANES), BF16),
            jax.ShapeDtypeStruct((nb_seq, N_HEADS, tp, D_V), BF16),
            jax.ShapeDtypeStruct((nb_seq * tp, N_HEADS * D_V), BF16),
            jax.ShapeDtypeStruct((1, N_HEADS * D_V), F32),
            jax.ShapeDtypeStruct((N_CHIPS,) + slot, BF16),
        ),
        scratch_shapes=[pltpu.VMEM((tp, 2 * LANES), F32)] + _reduce_scratch([(slot, BF16)], [False]),
        compiler_params=_params("arbitrary", "arbitrary"),
    )(q, k, v, o, lse, dcat, p, g_attn, send_out)


def _qkv_bwd(p, dq, dk, dv, wq, wkv, gq, gkv, tables):
    nb_seq, _, tp, _ = dq.shape
    ht = tp // 2

    def body(pa_ref, dq_ref, dk_ref, dv_ref, wq_ref, wkv_ref, gq_ref, gkv_ref, cos_ref, sa_ref, sb_ref,
             dpa_ref, dwq_ref, dwkv_ref, dgq_ref, dgkv_ref):
        @pl.when(pl.program_id(0) == 0)
        def _():
            dwq_ref[...] = jnp.zeros_like(dwq_ref)
            dwkv_ref[...] = jnp.zeros_like(dwkv_ref)
            dgq_ref[...] = jnp.zeros_like(dgq_ref)
            dgkv_ref[...] = jnp.zeros_like(dgkv_ref)

        pa = pa_ref[...].astype(F32)
        gq, gkv = gq_ref[...], gkv_ref[...]
        cq_hat, rq = _rms_stats(pa[:, :Q_RANK])
        ckv_hat, rkv = _rms_stats(pa[:, Q_RANK:Q_RANK + KV_RANK])
        tabs = (cos_ref[...], sa_ref[...], sb_ref[...])

        pe = [dq_ref[0, h, :, D_NOPE:].astype(F32) for h in range(N_HEADS)]
        pairs = [_rope_t(pe[2 * i] + pltpu.roll(pe[2 * i + 1], D_ROPE, 1), *tabs).astype(BF16) for i in range(2)]
        dq_flat = jnp.concatenate([dq_ref[0, h, :, :D_NOPE] for h in range(N_HEADS)] + pairs, axis=1)
        dwq_ref[...] += _dot((cq_hat * gq).astype(BF16), dq_flat, _TN)
        dcqn = _dot(dq_flat, wq_ref[...], _NT)
        dgq_ref[...] += jnp.sum(dcqn * cq_hat, axis=0, keepdims=True)
        dcq = _rms_bwd(gq * dcqn, cq_hat, rq)

        dkv_flat = jnp.concatenate([dk_ref[0, h, :, :D_NOPE] for h in range(N_HEADS)]
                                   + [dv_ref[0, h] for h in range(N_HEADS)], axis=1)
        dwkv_ref[...] += _dot((ckv_hat * gkv).astype(BF16), dkv_flat, _TN)
        dckvn = _dot(dkv_flat, wkv_ref[...], _NT)
        dgkv_ref[...] += jnp.sum(dckvn * ckv_hat, axis=0, keepdims=True)
        dckv = _rms_bwd(gkv * dckvn, ckv_hat, rkv)

        dk_pe = dk_ref[0, 0, :, D_NOPE:].astype(F32)
        for h in range(1, N_HEADS):
            dk_pe = dk_pe + dk_ref[0, h, :, D_NOPE:].astype(F32)
        dk_pe = jnp.where(lax.broadcasted_iota(jnp.int32, (ht, LANES), 1) < D_ROPE, dk_pe, 0.0)
        dpa_ref[...] = jnp.concatenate([dcq, dckv, _rope_t(dk_pe, *tabs)], axis=1).astype(BF16)

    full = lambda a: pl.BlockSpec(a.shape, lambda i: (0,) * a.ndim)
    tab = pl.BlockSpec((ht, LANES), lambda i: (i % 2, 0))
    qk = pl.BlockSpec((1, N_HEADS, ht, 2 * LANES), lambda i: (i // 2, 0, i % 2, 0))
    acc = lambda shape: pl.BlockSpec(shape, lambda i: (0, 0))
    return pl.pallas_call(
        body,
        name="qkv_bwd",
        grid=(2 * nb_seq,),
        in_specs=[pl.BlockSpec((ht, GRP_A), lambda i: (i, 0)), qk, qk,
                  pl.BlockSpec((1, N_HEADS, ht, D_V), lambda i: (i // 2, 0, i % 2, 0)),
                  full(wq), full(wkv), full(gq), full(gkv), tab, tab, tab],
        out_specs=(pl.BlockSpec((ht, GRP_A), lambda i: (i, 0)),
                   acc(wq.shape), acc(wkv.shape), acc((1, Q_RANK)), acc((1, KV_RANK))),
        out_shape=(
            jax.ShapeDtypeStruct((nb_seq * tp, GRP_A), BF16),
            jax.ShapeDtypeStruct(wq.shape, F32),
            jax.ShapeDtypeStruct(wkv.shape, F32),
            jax.ShapeDtypeStruct((1, Q_RANK), F32),
            jax.ShapeDtypeStruct((1, KV_RANK), F32),
        ),
        compiler_params=_params("arbitrary"),
    )(p, dq, dk, dv, wq, wkv, gq, gkv, *tables)


def _conv_bwd(p, dcat, conv_w, g_conv, nb_seq, tp):
    cols = CONV_WIDTH // LANES

    def body(b_ref, c_ref, h_ref, z_ref, dy_ref, w_ref, g_ref,
             db_ref, dc_ref, dh_ref, dz_ref, dw_ref, dg_ref):
        @pl.when(pl.program_id(1) == 0)
        def _():
            dw_ref[...] = jnp.zeros_like(dw_ref)
            dg_ref[...] = jnp.zeros_like(dg_ref)

        cb, c, h = b_ref[...].astype(F32), c_ref[...].astype(F32), h_ref[...].astype(F32)
        z, dy = z_ref[...].astype(F32), dy_ref[...].astype(F32)
        g = g_ref[...]
        w0, w1, w2 = w_ref[0:1, :], w_ref[1:2, :], w_ref[2:3, :]
        cc = c * h
        row = lax.broadcasted_iota(jnp.int32, (tp, LANES), 0)
        s1 = jnp.where(row >= 1, pltpu.roll(cc, 1, 0), 0.0)
        s2 = jnp.where(row >= 2, pltpu.roll(cc, 2, 0), 0.0)
        dwc = w0 * s2 + w1 * s1 + w2 * cc
        yc = cb * dwc
        r = lax.rsqrt(_group_mean(yc * yc) + EPS)
        ychat = yc * r
        sig = _sigmoid(z)
        dz_ref[...] = (dy * (ychat * g) * (sig * (1.0 + z * (1.0 - sig)))).astype(BF16)
        dyn = dy * (z * sig)
        dg_ref[...] += jnp.sum(dyn * ychat, axis=0, keepdims=True)
        gd = g * dyn
        dyc = r * (gd - ychat * _group_mean(gd * ychat))
        db_ref[...] = (dyc * dwc).astype(BF16)
        ddw = dyc * cb
        dw_ref[0:1, :] += jnp.sum(ddw * s2, axis=0, keepdims=True)
        dw_ref[1:2, :] += jnp.sum(ddw * s1, axis=0, keepdims=True)
        dw_ref[2:3, :] += jnp.sum(ddw * cc, axis=0, keepdims=True)
        u1 = jnp.where(row <= tp - 2, pltpu.roll(ddw, tp - 1, 0), 0.0)
        u2 = jnp.where(row <= tp - 3, pltpu.roll(ddw, tp - 2, 0), 0.0)
        dcc = w2 * ddw + w1 * u1 + w0 * u2
        dc_ref[...] = (dcc * h).astype(BF16)
        dh_ref[...] = (dcc * c).astype(BF16)

    col = pl.BlockSpec((tp, LANES), lambda t, b: (b, t))
    out = jax.ShapeDtypeStruct((nb_seq * tp, CONV_WIDTH), BF16)
    return pl.pallas_call(
        body,
        name="conv_bwd",
        grid=(cols, nb_seq),
        in_specs=_conv_specs(tp, lambda t, b, off: (b, off + t)) + [
            pl.BlockSpec((tp, LANES), lambda t, b: (b, N_HEADS * D_V // LANES + t)),
            pl.BlockSpec((8, LANES), lambda t, b: (0, t)),
            pl.BlockSpec((1, LANES), lambda t, b: (0, t))],
        out_specs=(col, col, col, col,
                   pl.BlockSpec((8, LANES), lambda t, b: (0, t)), pl.BlockSpec((1, LANES), lambda t, b: (0, t))),
        out_shape=(out, out, out, out,
                   jax.ShapeDtypeStruct((8, CONV_WIDTH), F32), jax.ShapeDtypeStruct((1, CONV_WIDTH), F32)),
        compiler_params=_params("arbitrary", "arbitrary"),
    )(p, p, p, p, dcat, conv_w, g_conv)


def _input_bwd(dps, w_in, x, meta, dh, norm_g, nt, send_in, small_grads):
    nb_seq, s, d = x.shape
    r, kb = dps[0].shape
    ts = (s + LANES) // nt
    steps = nb_seq * nt
    n_dp, n_small = len(dps), len(small_grads)
    in_slot, small_slot = send_in.shape[1:], (SMALL_ROWS, LANES)

    def body(*refs):
        dp_refs, w_ref, x_hbm, meta_ref, dh_ref, g_ref, pay_ref = refs[:n_dp], *refs[n_dp:n_dp + 6]
        o = n_dp + 6
        small_refs = refs[o:o + n_small]
        o += n_small
        gx_hbm, dmeta_ref, dg_ref, r2_in, r2_small = refs[o:o + 5]
        xbuf, gxbuf, tok_sems, ssmall, own_in, r1_in, sum_in, r1_small, sum_small = refs[o + 5:o + 14]
        sems = refs[o + 14:]
        i = pl.program_id(0)
        b, k = i // nt, i % nt

        def plan():
            return _reduce_plan((pay_ref, ssmall), (own_in, None), (r1_in, r1_small), (sum_in, sum_small),
                                (r2_in, r2_small), *sems)

        @pl.when(i == 0)
        def _():
            dmeta_ref[...] = jnp.zeros_like(dmeta_ref)
            dg_ref[...] = jnp.zeros_like(dg_ref)
            _pack_small(ssmall, *small_refs)
            plan()[0]()

        @pl.when(i == 1)
        def _():
            plan()[1]()

        def start(kk):
            if kk == 0:
                xbuf[0:PAD_FRONT, :] = jnp.zeros((PAD_FRONT, d), F32)
                xbuf[PAD_FRONT:LANES, :] = meta_ref[...]
            _token_copy(x_hbm, b, kk, ts, xbuf, tok_sems.at[0]).start()

        _for_tile(k, nt, start)
        du = _dot(dp_refs[0][...], w_ref[:, 0:kb], _NT)
        for j in range(1, n_dp):
            du = du + _dot(dp_refs[j][...], w_ref[:, kb * j:kb * (j + 1)], _NT)
        _for_tile(k, nt, lambda kk: _token_copy(x_hbm, b, kk, ts, xbuf, tok_sems.at[0]).wait())

        g = g_ref[...]
        hhat, rstd = _rms_stats(xbuf[...])
        dg_ref[...] += jnp.sum(du * hhat, axis=0, keepdims=True)
        res = _rms_bwd(g * du, hhat, rstd) + dh_ref[...]

        @pl.when(i > 0)
        def _():
            _for_tile(k, nt, lambda kk: _token_copy(gx_hbm, b, (kk - 1) % nt, ts, gxbuf, tok_sems.at[1], True).wait())

        gxbuf[...] = res

        @pl.when(k == 0)
        def _():
            dmeta_ref[...] += gxbuf[PAD_FRONT:LANES, :]

        _for_tile(k, nt, lambda kk: _token_copy(gx_hbm, b, kk, ts, gxbuf, tok_sems.at[1], True).start())

        @pl.when(i == steps - 1)
        def _():
            _token_copy(gx_hbm, b, nt - 1, ts, gxbuf, tok_sems.at[1], True).wait()
            plan()[2]()

    whole = lambda a: pl.BlockSpec(a.shape, lambda i: (0,) * a.ndim)
    hbm = pl.BlockSpec(memory_space=pl.ANY)
    return pl.pallas_call(
        body,
        name="input_bwd",
        grid=(steps,),
        in_specs=[pl.BlockSpec((ts, kb), lambda i: (i, 0)) for _ in dps]
        + [whole(w_in), hbm, whole(meta), pl.BlockSpec((ts, d), lambda i: (i, 0)), whole(norm_g), hbm]
        + [whole(a) for a in small_grads],
        out_specs=(hbm, pl.BlockSpec((N_META, d), lambda i: (0, 0)), pl.BlockSpec((1, d), lambda i: (0, 0)), hbm, hbm),
        out_shape=(jax.ShapeDtypeStruct((nb_seq, s, d), F32),
                   jax.ShapeDtypeStruct((N_META, d), F32),
                   jax.ShapeDtypeStruct((1, d), F32),
                   jax.ShapeDtypeStruct((N_CHIPS,) + in_slot, BF16),
                   jax.ShapeDtypeStruct((N_CHIPS,) + small_slot, F32)),
        scratch_shapes=[pltpu.VMEM((ts, d), F32), pltpu.VMEM((ts, d), F32), pltpu.SemaphoreType.DMA((2,)),
                        pltpu.VMEM((N_DEV,) + small_slot, F32)]
        + _reduce_scratch([(in_slot, BF16), (small_slot, F32)], [True, False]),
        compiler_params=_params("arbitrary"),
    )(*dps, w_in, x, meta, dh, norm_g, send_in, *small_grads)


def _in_proj_bwd_w(u, dps, bm):
    r, d = u.shape
    kb = dps[0].shape[1]
    last = r // bm - 1

    def body(*refs):
        u_ref, dp_refs, o_ref, acc_ref = refs[0], refs[1:1 + len(dps)], refs[1 + len(dps)], refs[2 + len(dps)]

        @pl.when(pl.program_id(0) == 0)
        def _():
            acc_ref[...] = jnp.zeros_like(acc_ref)

        uu = u_ref[...]
        for j in range(len(dps)):
            acc_ref[:, kb * j:kb * (j + 1)] += _dot(uu, dp_refs[j][...], _TN)

        @pl.when(pl.program_id(0) == last)
        def _():
            for k in range(N_DEV):
                for s, e, c0 in _in_pieces(k):
                    o_ref[k, :, s:e] = acc_ref[:, c0:c0 + e - s].astype(BF16)
                o_ref[k, :, SHARD_IN:] = jnp.zeros((d, SHARD_IN_PAD - SHARD_IN), BF16)

    return pl.pallas_call(
        body,
        name="in_proj_bwd_w",
        grid=(r // bm,),
        in_specs=[pl.BlockSpec((bm, d), lambda i: (i, 0))] + [pl.BlockSpec((bm, kb), lambda i: (i, 0)) for _ in dps],
        out_specs=pl.BlockSpec((N_DEV, d, SHARD_IN_PAD), lambda i: (0, 0, 0)),
        out_shape=jax.ShapeDtypeStruct((N_DEV, d, SHARD_IN_PAD), BF16),
        scratch_shapes=[pltpu.VMEM((d, kb * len(dps)), F32)],
        compiler_params=_params("arbitrary"),
    )(u, *dps)


def _local_step(x, loss_target, u, p, meta_f, norm_g, w_in_p, q_norm_g, w_q_p, kv_norm_g, w_kv_p, conv_w_f,
                attn_out_g, conv_out_g, w_out_f, g_final):
    nb_seq, s, d = x.shape
    tp = s + LANES
    ht = tp // 2
    tables = _rope_tables(tp)

    q, k, v = _qkv_fwd(p, w_q_p, w_kv_p, q_norm_g, kv_norm_g, tables, nb_seq, tp)
    ya, o, lse = _attn_fwd(q, k, v, p, attn_out_g)
    yc = _conv_fwd(p, conv_w_f, conv_out_g, nb_seq, tp)
    dh, dhb, d_final_g, loss_part = _out_proj_loss(ya, yc, w_out_f, x, loss_target, g_final, TOKEN_TILES)

    dcat, d_w_out = _out_proj_bwd(dhb, w_out_f, ya, yc, ht)
    send_out = d_w_out.reshape(N_DEV, SHARD_OUT, d)
    dq, dk, dv, dz_attn, d_attn_g, r_out = _attn_bwd(q, k, v, o, lse, dcat, p, attn_out_g, send_out)
    dpa, d_wq_p, d_wkv_p, d_gq, d_gkv = _qkv_bwd(p, dq, dk, dv, w_q_p, w_kv_p, q_norm_g, kv_norm_g, tables)
    d_b, d_c, d_h, dz_conv, d_conv_w, d_conv_g = _conv_bwd(p, dcat, conv_w_f, conv_out_g, nb_seq, tp)
    dps = (dpa, dz_attn, d_b, d_c, d_h, dz_conv)
    send_in = _in_proj_bwd_w(u, dps, ht // 2)
    small = (d_wq_p, d_wkv_p, d_conv_w, d_final_g, d_gq, d_gkv, d_attn_g, d_conv_g, loss_part)
    grad_x, d_meta, d_norm_g, r_in, r_small = _input_bwd(
        dps, w_in_p, x, meta_f, dh, norm_g, TOKEN_TILES, send_in, small)
    return grad_x, r_in, r_out, r_small, d_meta, d_norm_g


def kernel(x, meta_tokens, norm_g, w_in, q_norm_g, w_q_up, kv_norm_g, w_kv_up, conv_w, attn_out_g, conv_out_g, w_out, final_norm_g, loss_target, m_meta_tokens, m_norm_g, m_w_in, m_q_norm_g, m_w_q_up, m_kv_norm_g, m_w_kv_up, m_conv_w, m_attn_out_g, m_conv_out_g, m_w_out, m_final_norm_g, v_meta_tokens, v_norm_g, v_w_in, v_q_norm_g, v_w_q_up, v_kv_norm_g, v_w_kv_up, v_conv_w, v_attn_out_g, v_conv_out_g, v_w_out, v_final_norm_g):
    d = x.shape[-1]
    ht = (x.shape[1] + LANES) // 2
    u, w_in_p, meta_f = _prep_gather(x, meta_tokens, norm_g, w_in[0])
    p, w_q_p, w_kv_p, w_out_f, conv_w_f = _in_proj_gather(
        u, w_in_p, w_q_up[0], w_kv_up[0], w_out[0], conv_w[0], ht, GRP_A)
    g_final = final_norm_g.reshape(1, d)
    grad_x, r_in, r_out, r_small, d_meta, d_norm_g = _local_step(
        x, loss_target, u, p, meta_f, norm_g, w_in_p, q_norm_g, w_q_p, kv_norm_g, w_kv_p, conv_w_f,
        attn_out_g, conv_out_g, w_out_f, g_final)

    flat = lambda a: a.reshape(a.shape[-2:]) if a.ndim == 3 else a.reshape(1, -1) if a.ndim == 1 else a
    params = {
        "meta_tokens": (meta_tokens, m_meta_tokens, v_meta_tokens),
        "norm_g": (norm_g, m_norm_g, v_norm_g),
        "w_in": (w_in, m_w_in, v_w_in),
        "q_norm_g": (q_norm_g, m_q_norm_g, v_q_norm_g),
        "w_q_up": (w_q_up, m_w_q_up, v_w_q_up),
        "kv_norm_g": (kv_norm_g, m_kv_norm_g, v_kv_norm_g),
        "w_kv_up": (w_kv_up, m_w_kv_up, v_w_kv_up),
        "conv_w": (conv_w, m_conv_w, v_conv_w),
        "attn_out_g": (attn_out_g, m_attn_out_g, v_attn_out_g),
        "conv_out_g": (conv_out_g, m_conv_out_g, v_conv_out_g),
        "w_out": (w_out, m_w_out, v_w_out),
        "final_norm_g": (final_norm_g, m_final_norm_g, v_final_norm_g),
    }
    grads, loss = _reduce_tail(r_in, r_out, r_small, d_meta, d_norm_g)
    updated = _adamw(grads, {n: tuple(flat(a) for a in t) for n, t in params.items()})
    outs = [[updated[n][i].reshape(params[n][0].shape) for n, _ in PARAM_SHAPES] for i in range(4)]
    return (loss[0, 0], grad_x, *outs[0], *outs[1], *outs[2], *outs[3])
```

```python
import functools

import jax
import jax.numpy as jnp
from jax import lax
from jax.experimental import pallas as pl
from jax.experimental.pallas import tpu as pltpu

F32 = jnp.float32
BF16 = jnp.bfloat16

N_META = 16
D_MODEL = 1024
N_HEADS = 4
D_NOPE = 128
D_ROPE = 64
D_V = 128
Q_RANK = 256
KV_RANK = 128
CONV_WIDTH = 512
CONV_GROUP = 64
ROPE_THETA = 10000.0
ATTN_SCALE = (D_NOPE + D_ROPE) ** -0.5
EPS = 1e-6
NEG_INF = -1e30

ADAM_LR = 0.001
ADAM_B1 = 0.9
ADAM_B2 = 0.999
ADAM_EPS = 1e-08
ADAM_WD = 0.01
ADAM_STEP = 10

LANES = 128
PAD_FRONT = LANES - N_META
KV_TILE = 256
N_DEV = 8
VMEM_LIMIT = 56 * 1024 * 1024

IN_PAD = 3072
GRP_A = 512
N_A = Q_RANK + KV_RANK + D_ROPE
IN_PROJ = 3008
SHARD_IN = IN_PROJ // N_DEV
SHARD_IN_PAD = 384
SHARD_Q = 96
SHARD_KV = 128
SHARD_OUT = 128
SHARD_CONV = 64
SHARD_META = 128
Q_COLS = N_HEADS * (D_NOPE + D_ROPE)
KV_COLS = N_HEADS * (D_NOPE + D_V)

ROW_Q, ROW_KV, ROW_META, ROW_CONV = 0, 256, 384, 400
ROW_REPL = 408
ROW_NORM, ROW_FINAL, ROW_GQ, ROW_GKV, ROW_ATTN, ROW_CONVG, ROW_LOSS = 408, 416, 424, 426, 427, 431, 435
SMALL_ROWS = 440

PARAM_SHAPES = (
    ("meta_tokens", (N_META, SHARD_META)), ("norm_g", (1, D_MODEL)), ("w_in", (SHARD_IN, D_MODEL)),
    ("q_norm_g", (1, Q_RANK)), ("w_q_up", (Q_RANK, SHARD_Q)), ("kv_norm_g", (1, KV_RANK)),
    ("w_kv_up", (KV_RANK, SHARD_KV)), ("conv_w", (3, SHARD_CONV)), ("attn_out_g", (1, CONV_WIDTH)),
    ("conv_out_g", (1, CONV_WIDTH)), ("w_out", (SHARD_OUT, D_MODEL)), ("final_norm_g", (1, D_MODEL)),
)


def _in_pieces(k):
    lo, hi = SHARD_IN * k, SHARD_IN * (k + 1)
    out = []
    if lo < N_A:
        out.append((0, min(hi, N_A) - lo, lo))
    if hi > N_A:
        s = max(lo, N_A)
        out.append((s - lo, hi - lo, s + GRP_A - N_A))
    return out


def _q_pieces(k):
    lo, hi = SHARD_Q * k, SHARD_Q * (k + 1)
    out = []
    for h in range(N_HEADS):
        base = (D_NOPE + D_ROPE) * h
        s, e = max(lo, base), min(hi, base + D_NOPE)
        if s < e:
            out.append((s - lo, e - lo, D_NOPE * h + s - base))
        s, e = max(lo, base + D_NOPE), min(hi, base + D_NOPE + D_ROPE)
        if s < e:
            out.append((s - lo, e - lo, N_HEADS * D_NOPE + D_ROPE * h + s - base - D_NOPE))
    return out


def _kv_dst(k):
    return D_NOPE * (k // 2) + (N_HEADS * D_NOPE if k % 2 else 0)


def _params(*sem):
    return pltpu.CompilerParams(dimension_semantics=sem, vmem_limit_bytes=VMEM_LIMIT)


def _rms_stats(x):
    r = lax.rsqrt(jnp.mean(x * x, axis=-1, keepdims=True) + EPS)
    return x * r, r


def _rms_bwd(gdy, xhat, r):
    return r * (gdy - xhat * jnp.mean(gdy * xhat, axis=-1, keepdims=True))


def _sigmoid(z):
    return 1.0 / (1.0 + jnp.exp(-z))


def _group_mean(x):
    i0 = lax.broadcasted_iota(jnp.int32, (LANES, LANES), 0) // CONV_GROUP
    i1 = lax.broadcasted_iota(jnp.int32, (LANES, LANES), 1) // CONV_GROUP
    m = jnp.where(i0 == i1, 1.0 / CONV_GROUP, 0.0).astype(BF16)
    hi = x.astype(BF16)
    lo = (x - hi.astype(F32)).astype(BF16)
    return jnp.dot(hi, m, preferred_element_type=F32) + jnp.dot(lo, m, preferred_element_type=F32)


_NT = (((1,), (1,)), ((), ()))
_TN = (((0,), (0,)), ((), ()))


def _dot(a, b, dims=None):
    if dims is None:
        return jnp.dot(a, b, preferred_element_type=F32)
    return lax.dot_general(a, b, dims, preferred_element_type=F32)


def _device_position():
    x, y, c = lax.axis_index("x"), lax.axis_index("y"), lax.axis_index("c")
    return x, y, c, 4 * x + 2 * y + c


def _gather_plan(srcs, slots, send_sems, recv_sems, local_sems):
    x, y, c, _ = _device_position()
    me, sibling = (x, y, c), (x, y, 1 - c)
    chips = [(1 - x, y), (x, 1 - y), (1 - x, 1 - y)]
    n = len(srcs)

    def slot(a, px, py, pc):
        return slots[a].at[4 * px + 2 * py + pc]

    def copy(a, k, block, to, own=False):
        return pltpu.make_async_remote_copy(
            src_ref=srcs[a] if own else slot(a, *block),
            dst_ref=slot(a, *block),
            send_sem=send_sems.at[7 * a + k],
            recv_sem=recv_sems.at[7 * a + k],
            device_id=to,
            device_id_type=pl.DeviceIdType.MESH,
        )

    def local(a):
        return pltpu.make_async_copy(srcs[a], slot(a, *me), local_sems.at[a])

    def firsts(a):
        return [copy(a, 0, me, sibling, own=True)] + [
            copy(a, 1 + j, me, (*chip, c), own=True) for j, chip in enumerate(chips)]

    def start():
        for a in range(n):
            local(a).start()
            for cp in firsts(a):
                cp.start()

    def forward():
        for j, chip in enumerate(chips):
            for a in range(n):
                copy(a, 1 + j, (*chip, c), me).wait_recv()
                copy(a, 4 + j, (*chip, c), sibling).start()

    def finish():
        for a in range(n):
            copy(a, 0, sibling, me).wait_recv()
            for j, chip in enumerate(chips):
                copy(a, 4 + j, (*chip, 1 - c), me).wait_recv()
        for a in range(n):
            for cp in firsts(a) + [copy(a, 4 + j, (*chip, c), sibling) for j, chip in enumerate(chips)]:
                cp.wait_send()
            local(a).wait()

    return start, forward, finish


def _adam_update(g, w, m, v):
    m_new = ADAM_B1 * m + (1.0 - ADAM_B1) * g
    v_new = ADAM_B2 * v + (1.0 - ADAM_B2) * (g * g)
    m_hat = m_new / (1.0 - ADAM_B1 ** ADAM_STEP)
    v_hat = v_new / (1.0 - ADAM_B2 ** ADAM_STEP)
    return -ADAM_LR * (m_hat / (jnp.sqrt(v_hat) + ADAM_EPS) + ADAM_WD * w), m_new, v_new


def _adamw(grads, params):
    names = [n for n, _ in PARAM_SHAPES]
    n_p = len(names)

    def body(*refs):
        for i in range(n_p):
            g = refs[i][...]
            w, m, v = (refs[n_p + 3 * i + j][...] for j in range(3))
            delta, m_new, v_new = _adam_update(g, w, m, v)
            for j, val in enumerate((g, delta, m_new, v_new)):
                refs[4 * n_p + 4 * i + j][...] = val

    vm = pl.BlockSpec(memory_space=pltpu.VMEM)
    out_shape = []
    for _, shape in PARAM_SHAPES:
        out_shape += [jax.ShapeDtypeStruct(shape, F32)] * 4
    outs = pl.pallas_call(
        body,
        name="adamw",
        out_shape=tuple(out_shape),
        in_specs=[vm] * (4 * n_p),
        out_specs=(vm,) * (4 * n_p),
        compiler_params=pltpu.CompilerParams(vmem_limit_bytes=VMEM_LIMIT),
    )(*[grads[n] for n in names], *[a for n in names for a in params[n]])
    return {n: outs[4 * i:4 * i + 4] for i, n in enumerate(names)}


N_CHIPS = 4


def _reduce_plan(pays, owns, r1s, sums, r2s, send1, recv1, send2, recv2, local_sems):
    x, y, c, _ = _device_position()
    sibling = (x, y, 1 - c)
    chips = [((1 - x if rj & 2 else x), (1 - y if rj & 1 else y)) for rj in range(N_CHIPS)]
    n = len(pays)

    def slot_of(rj, core):
        return 4 * chips[rj][0] + 2 * chips[rj][1] + core

    def to_sibling(a, rj):
        return pltpu.make_async_remote_copy(
            src_ref=pays[a].at[slot_of(rj, 1 - c)], dst_ref=r1s[a].at[rj],
            send_sem=send1.at[N_CHIPS * a + rj], recv_sem=recv1.at[N_CHIPS * a + rj],
            device_id=sibling, device_id_type=pl.DeviceIdType.MESH)

    def load_own(a, rj):
        return pltpu.make_async_copy(pays[a].at[slot_of(rj, c)], owns[a].at[rj], local_sems.at[2 * N_CHIPS * a + rj])

    def to_chip(a, rj):
        return pltpu.make_async_remote_copy(
            src_ref=sums[a].at[rj], dst_ref=r2s[a].at[rj],
            send_sem=send2.at[N_CHIPS * a + rj], recv_sem=recv2.at[N_CHIPS * a + rj],
            device_id=(*chips[rj], c), device_id_type=pl.DeviceIdType.MESH)

    def keep(a):
        return pltpu.make_async_copy(sums[a].at[0], r2s[a].at[0], local_sems.at[2 * N_CHIPS * a + N_CHIPS])

    def start():
        for a in range(n):
            for rj in range(N_CHIPS):
                to_sibling(a, rj).start()
                if owns[a] is not None:
                    load_own(a, rj).start()

    def combine():
        for a in range(n):
            for rj in range(N_CHIPS):
                to_sibling(a, rj).wait_recv()
                if owns[a] is not None:
                    load_own(a, rj).wait()
                    mine = owns[a][rj]
                else:
                    mine = pays[a][slot_of(rj, c)]
                sums[a][rj] = (mine.astype(F32) + r1s[a][rj].astype(F32)).astype(sums[a].dtype)
            keep(a).start()
            for rj in range(1, N_CHIPS):
                to_chip(a, rj).start()

    def finish():
        for a in range(n):
            for rj in range(1, N_CHIPS):
                to_chip(a, rj).wait_recv()
            for rj in range(N_CHIPS):
                to_sibling(a, rj).wait_send()
            for rj in range(1, N_CHIPS):
                to_chip(a, rj).wait_send()
            keep(a).wait()

    return start, combine, finish


def _reduce_scratch(shapes_dtypes, own_flags):
    out = []
    for (shape, dtype), own in zip(shapes_dtypes, own_flags):
        if own:
            out.append(pltpu.VMEM((N_CHIPS,) + shape, dtype))
        out += [pltpu.VMEM((N_CHIPS,) + shape, dtype), pltpu.VMEM((N_CHIPS,) + shape, dtype)]
    n = len(shapes_dtypes)
    out += [pltpu.SemaphoreType.DMA((N_CHIPS * n,))] * 4 + [pltpu.SemaphoreType.DMA((2 * N_CHIPS * n,))]
    return out


def _pack_small(ssmall, dwq, dwkv, dconv, dfinal, dgq, dgkv, dattn, dconvg, loss_part):
    ssmall[...] = jnp.zeros_like(ssmall)
    rep = ssmall.at[0]
    for i in range(D_MODEL // LANES):
        rep[ROW_FINAL + i:ROW_FINAL + i + 1, :] = dfinal[:, LANES * i:LANES * (i + 1)]
    for i in range(Q_RANK // LANES):
        rep[ROW_GQ + i:ROW_GQ + i + 1, :] = dgq[:, LANES * i:LANES * (i + 1)]
    rep[ROW_GKV:ROW_GKV + 1, :] = dgkv[...]
    for i in range(CONV_WIDTH // LANES):
        rep[ROW_ATTN + i:ROW_ATTN + i + 1, :] = dattn[:, LANES * i:LANES * (i + 1)]
        rep[ROW_CONVG + i:ROW_CONVG + i + 1, :] = dconvg[:, LANES * i:LANES * (i + 1)]
    rep[ROW_LOSS:ROW_LOSS + 1, :] = loss_part[...]
    for k in range(N_DEV):
        if k:
            ssmall[k, ROW_REPL:, :] = ssmall[0, ROW_REPL:, :]
        for s, e, d in _q_pieces(k):
            ssmall[k, ROW_Q:ROW_Q + Q_RANK, s:e] = dwq[:, d:d + e - s]
        ssmall[k, ROW_KV:ROW_KV + KV_RANK, :] = dwkv[:, _kv_dst(k):_kv_dst(k) + SHARD_KV]
        ssmall[k, ROW_CONV:ROW_CONV + 3, 0:SHARD_CONV] = dconv[0:3, SHARD_CONV * k:SHARD_CONV * (k + 1)]


TOKEN_TILES = 4
TAIL_ROWS = N_META + D_MODEL // LANES


def _reduce_tail(r_in, r_out, r_small, d_meta, d_norm):
    n_p = len(PARAM_SHAPES)
    names = [n for n, _ in PARAM_SHAPES]

    def body(*refs):
        rin, rout, rsmall, dmeta, dnorm = refs[:5]
        g_out = {n: refs[5 + i] for i, n in enumerate(names)}
        loss_out = refs[5 + n_p]
        stail, rtail, gsum, gtail, send_sems, recv_sems = refs[6 + n_p:]
        x, y, c, me = _device_position()
        my_chip = 2 * x + y

        for k in range(N_DEV):
            stail[k, 0:N_META, :] = dmeta[:, SHARD_META * k:SHARD_META * (k + 1)]
            for i in range(D_MODEL // LANES):
                stail[k, N_META + i:N_META + i + 1, :] = dnorm[:, LANES * i:LANES * (i + 1)]
        copies = []
        for r in range(1, N_DEV):
            peer = (1 - x if r & 4 else x, 1 - y if r & 2 else y, 1 - c if r & 1 else c)
            copies.append(pltpu.make_async_remote_copy(
                src_ref=stail.at[4 * peer[0] + 2 * peer[1] + peer[2]],
                dst_ref=rtail.at[r],
                send_sem=send_sems.at[r - 1],
                recv_sem=recv_sems.at[r - 1],
                device_id=peer,
                device_id_type=pl.DeviceIdType.MESH,
            ))
        for cp in copies:
            cp.start()
        rtail[0] = stail[me]

        g = rin[my_chip].astype(F32)
        for ch in range(1, N_CHIPS):
            g = g + rin[ch ^ my_chip].astype(F32)
        g_out["w_in"][...] = g[:SHARD_IN, :]

        g = rout[my_chip].astype(F32)
        gs = rsmall[my_chip]
        for ch in range(1, N_CHIPS):
            g = g + rout[ch ^ my_chip].astype(F32)
            gs = gs + rsmall[ch ^ my_chip]
        g_out["w_out"][...] = g
        gsum[...] = gs
        g_out["w_q_up"][...] = gsum[ROW_Q:ROW_Q + Q_RANK, 0:SHARD_Q]
        g_out["w_kv_up"][...] = gsum[ROW_KV:ROW_KV + KV_RANK, :]
        g_out["conv_w"][...] = gsum[ROW_CONV:ROW_CONV + 3, 0:SHARD_CONV]
        for name, row, width in (("final_norm_g", ROW_FINAL, D_MODEL), ("q_norm_g", ROW_GQ, Q_RANK),
                                 ("kv_norm_g", ROW_GKV, KV_RANK), ("attn_out_g", ROW_ATTN, CONV_WIDTH),
                                 ("conv_out_g", ROW_CONVG, CONV_WIDTH)):
            for i in range(width // LANES):
                g_out[name][:, LANES * i:LANES * (i + 1)] = gsum[row + i:row + i + 1, :]
        loss_out[...] = gsum[ROW_LOSS:ROW_LOSS + 1, :]

        for cp in copies:
            cp.wait_recv()
        gt = rtail[me]
        for d in range(1, N_DEV):
            gt = gt + rtail[d ^ me]
        gtail[...] = gt
        g_out["meta_tokens"][...] = gtail[0:N_META, :]
        for i in range(D_MODEL // LANES):
            g_out["norm_g"][:, LANES * i:LANES * (i + 1)] = gtail[N_META + i:N_META + i + 1, :]
        for cp in copies:
            cp.wait_send()

    vm = pl.BlockSpec(memory_space=pltpu.VMEM)
    out_shape = [jax.ShapeDtypeStruct(shape, F32) for _, shape in PARAM_SHAPES]
    out_shape.append(jax.ShapeDtypeStruct((1, LANES), F32))
    outs = pl.pallas_call(
        body,
        name="reduce_tail",
        out_shape=tuple(out_shape),
        in_specs=[vm] * 5,
        out_specs=(vm,) * len(out_shape),
        scratch_shapes=[
            pltpu.VMEM((N_DEV, TAIL_ROWS, LANES), F32),
            pltpu.VMEM((N_DEV, TAIL_ROWS, LANES), F32),
            pltpu.VMEM((SMALL_ROWS, LANES), F32),
            pltpu.VMEM((TAIL_ROWS, LANES), F32),
            pltpu.SemaphoreType.DMA((N_DEV - 1,)),
            pltpu.SemaphoreType.DMA((N_DEV - 1,)),
        ],
        compiler_params=pltpu.CompilerParams(vmem_limit_bytes=VMEM_LIMIT),
    )(r_in, r_out, r_small, d_meta, d_norm)
    return {n: outs[i] for i, n in enumerate(names)}, outs[-1]


def _prep_gather(x, meta, norm_g, w_in_t):
    nb_seq, s, d = x.shape
    nb = s // LANES + 1
    forward_step = nb_seq * (nb // 3)
    finish_step = nb_seq * (nb - 1)

    def body(x_ref, meta_ref, g_ref, win_ref, u_ref, w_in_p, meta_f,
             sbig, ssmall, gbig, gsmall, send_sems, recv_sems, local_sems):
        jj, b = pl.program_id(0), pl.program_id(1)
        t = jj * nb_seq + b

        def plan():
            return _gather_plan((sbig, ssmall), (gbig, gsmall), send_sems, recv_sems, local_sems)

        @pl.when(t == 0)
        def _():
            sbig[0:SHARD_IN, :] = win_ref[...].astype(BF16)
            sbig[SHARD_IN:, :] = jnp.zeros((SHARD_IN_PAD - SHARD_IN, d), BF16)
            ssmall[...] = meta_ref[...]
            plan()[0]()

        @pl.when(t == forward_step)
        def _():
            plan()[1]()

        @pl.when(t == finish_step)
        def _():
            plan()[2]()
            w_in_p[N_A:GRP_A, :] = jnp.zeros((GRP_A - N_A, d), BF16)
            for k in range(N_DEV):
                for s0, e0, d0 in _in_pieces(k):
                    w_in_p[d0:d0 + e0 - s0, :] = gbig[k, s0:e0, :]
                meta_f[:, SHARD_META * k:SHARD_META * (k + 1)] = gsmall[k]

        def norm(h):
            hhat, _ = _rms_stats(h)
            return (hhat * g_ref[...]).astype(BF16)

        @pl.when(jj < nb - 1)
        def _():
            u_ref[...] = norm(x_ref[0])

        @pl.when(jj == nb - 1)
        def _():
            u_ref[0:PAD_FRONT, :] = jnp.zeros((PAD_FRONT, d), BF16)
            u_ref[PAD_FRONT:LANES, :] = norm(meta_f[...])

    whole = lambda shape: pl.BlockSpec(shape, lambda jj, b: (0,) * len(shape))
    return pl.pallas_call(
        body,
        name="prep_norm_gather",
        grid=(nb, nb_seq),
        in_specs=[
            pl.BlockSpec((1, LANES, d), lambda jj, b: (b, jnp.minimum(jj, nb - 2), 0)),
            whole(meta.shape), whole(norm_g.shape), whole(w_in_t.shape),
        ],
        out_specs=(pl.BlockSpec((LANES, d), lambda jj, b: (b * nb + (jj + 1) % nb, 0)),
                   whole((IN_PAD, d)), whole((N_META, d))),
        out_shape=(jax.ShapeDtypeStruct((nb_seq * nb * LANES, d), BF16),
                   jax.ShapeDtypeStruct((IN_PAD, d), BF16),
                   jax.ShapeDtypeStruct((N_META, d), F32)),
        scratch_shapes=[
            pltpu.VMEM((SHARD_IN_PAD, d), BF16),
            pltpu.VMEM((N_META, SHARD_META), F32),
            pltpu.VMEM((N_DEV, SHARD_IN_PAD, d), BF16),
            pltpu.VMEM((N_DEV, N_META, SHARD_META), F32),
            pltpu.SemaphoreType.DMA((14,)),
            pltpu.SemaphoreType.DMA((14,)),
            pltpu.SemaphoreType.DMA((2,)),
        ],
        compiler_params=_params("arbitrary", "arbitrary"),
    )(x, meta, norm_g, w_in_t)


def _in_proj_gather(u, w_in_p, w_q, w_kv, w_out, conv_w, bm, bn):
    m, k_dim = u.shape
    n = w_in_p.shape[0]
    steps = (m // bm) * (n // bn)
    forward_step = steps // 3
    qkv_rows = Q_RANK + KV_RANK

    def body(a_ref, b_ref, wq_ref, wkv_ref, wout_ref, conv_ref, o_ref, w_q_p, w_kv_p, w_out_f, conv_f,
             sqkv, sout, sconv, gqkv, gout, gconv, send_sems, recv_sems, local_sems):
        t = pl.program_id(0) * (n // bn) + pl.program_id(1)

        def plan():
            return _gather_plan((sqkv, sout, sconv), (gqkv, gout, gconv), send_sems, recv_sems, local_sems)

        @pl.when(t == 0)
        def _():
            sqkv[...] = jnp.zeros_like(sqkv)
            sqkv[0:Q_RANK, 0:SHARD_Q] = wq_ref[...].astype(BF16)
            sqkv[Q_RANK:, :] = wkv_ref[...].astype(BF16)
            sout[...] = wout_ref[...].astype(BF16)
            sconv[...] = jnp.zeros_like(sconv)
            sconv[0:3, 0:SHARD_CONV] = conv_ref[...]
            plan()[0]()

        @pl.when(t == forward_step)
        def _():
            plan()[1]()

        o_ref[...] = _dot(a_ref[...], b_ref[...], _NT).astype(o_ref.dtype)

        @pl.when(t == steps - 1)
        def _():
            plan()[2]()
            conv_f[...] = jnp.zeros_like(conv_f)
            for k in range(N_DEV):
                for s0, e0, d0 in _q_pieces(k):
                    w_q_p[:, d0:d0 + e0 - s0] = gqkv[k, 0:Q_RANK, s0:e0]
                w_kv_p[:, _kv_dst(k):_kv_dst(k) + SHARD_KV] = gqkv[k, Q_RANK:, :]
                w_out_f[SHARD_OUT * k:SHARD_OUT * (k + 1), :] = gout[k]
                conv_f[0:3, SHARD_CONV * k:SHARD_CONV * (k + 1)] = gconv[k, 0:3, 0:SHARD_CONV]

    whole = lambda shape: pl.BlockSpec(shape, lambda i, j: (0,) * len(shape))
    return pl.pallas_call(
        body,
        name="in_proj_gather",
        grid=(m // bm, n // bn),
        in_specs=[pl.BlockSpec((bm, k_dim), lambda i, j: (i, 0)), pl.BlockSpec((bn, k_dim), lambda i, j: (j, 0)),
                  whole(w_q.shape), whole(w_kv.shape), whole(w_out.shape), whole(conv_w.shape)],
        out_specs=(pl.BlockSpec((bm, bn), lambda i, j: (i, j)),
                   whole((Q_RANK, Q_COLS)), whole((KV_RANK, KV_COLS)), whole((D_MODEL, D_MODEL)),
                   whole((8, CONV_WIDTH))),
        out_shape=(jax.ShapeDtypeStruct((m, n), BF16),
                   jax.ShapeDtypeStruct((Q_RANK, Q_COLS), BF16),
                   jax.ShapeDtypeStruct((KV_RANK, KV_COLS), BF16),
                   jax.ShapeDtypeStruct((D_MODEL, D_MODEL), BF16),
                   jax.ShapeDtypeStruct((8, CONV_WIDTH), F32)),
        scratch_shapes=[
            pltpu.VMEM((qkv_rows, LANES), BF16),
            pltpu.VMEM((SHARD_OUT, D_MODEL), BF16),
            pltpu.VMEM((8, LANES), F32),
            pltpu.VMEM((N_DEV, qkv_rows, LANES), BF16),
            pltpu.VMEM((N_DEV, SHARD_OUT, D_MODEL), BF16),
            pltpu.VMEM((N_DEV, 8, LANES), F32),
            pltpu.SemaphoreType.DMA((21,)),
            pltpu.SemaphoreType.DMA((21,)),
            pltpu.SemaphoreType.DMA((3,)),
        ],
        compiler_params=_params("arbitrary", "arbitrary"),
    )(u, w_in_p, w_q, w_kv, w_out, conv_w)


def _rope_tables(tp):
    half = D_ROPE // 2
    inv_freq = 1.0 / (ROPE_THETA ** (jnp.arange(half, dtype=F32) / half))
    pos = (jnp.arange(tp) - PAD_FRONT).astype(F32)
    ang = pos[:, None] * inv_freq[None, :]
    cos = jnp.tile(jnp.cos(ang), (1, LANES // half))
    sin = jnp.tile(jnp.sin(ang), (1, LANES // half))
    first = (jnp.arange(LANES) % D_ROPE) < half
    return cos, jnp.where(first, -sin, 0.0), jnp.where(first, 0.0, sin)


def _rope(t, cos, sa, sb):
    return t * cos + pltpu.roll(t, LANES - D_ROPE // 2, 1) * sa + pltpu.roll(t, D_ROPE // 2, 1) * sb


def _rope_t(t, cos, sa, sb):
    return t * cos + pltpu.roll(t * sa, D_ROPE // 2, 1) + pltpu.roll(t * sb, LANES - D_ROPE // 2, 1)


def _qkv_fwd(p, wq, wkv, gq, gkv, tables, nb_seq, tp):
    ht = tp // 2

    def body(pa_ref, wq_ref, wkv_ref, gq_ref, gkv_ref, cos_ref, sa_ref, sb_ref, q_ref, k_ref, v_ref):
        pa = pa_ref[...].astype(F32)
        cq_hat, _ = _rms_stats(pa[:, :Q_RANK])
        ckv_hat, _ = _rms_stats(pa[:, Q_RANK:Q_RANK + KV_RANK])
        q = _dot((cq_hat * gq_ref[...]).astype(BF16), wq_ref[...])
        kv = _dot((ckv_hat * gkv_ref[...]).astype(BF16), wkv_ref[...])
        tabs = (cos_ref[...], sa_ref[...], sb_ref[...])
        lane = lax.broadcasted_iota(jnp.int32, (ht, LANES), 1)
        low = lane < D_ROPE
        mark = lane == D_ROPE
        row = (pl.program_id(0) % 2) * ht + lax.broadcasted_iota(jnp.int32, (ht, LANES), 0)
        k_pe = jnp.where(mark & (row < PAD_FRONT), NEG_INF, _rope(pa[:, Q_RANK + KV_RANK:], *tabs))
        one = jnp.where(mark & (row >= PAD_FRONT), 1.0, 0.0)
        pairs = [_rope(q[:, N_HEADS * D_NOPE + LANES * i:N_HEADS * D_NOPE + LANES * (i + 1)], *tabs) for i in range(2)]
        for h in range(N_HEADS):
            pair = pairs[h // 2]
            if h % 2:
                pair = pltpu.roll(pair, D_ROPE, 1)
            pe = jnp.where(low, pair, one)
            q_ref[0, h] = jnp.concatenate([q[:, D_NOPE * h:D_NOPE * (h + 1)], pe], axis=1).astype(BF16)
            k_ref[0, h] = jnp.concatenate([kv[:, D_NOPE * h:D_NOPE * (h + 1)], k_pe], axis=1).astype(BF16)
            v_ref[0, h] = kv[:, N_HEADS * D_NOPE + D_V * h:N_HEADS * D_NOPE + D_V * (h + 1)].astype(BF16)

    full = lambda a: pl.BlockSpec(a.shape, lambda i: (0,) * a.ndim)
    tab = pl.BlockSpec((ht, LANES), lambda i: (i % 2, 0))
    qk = pl.BlockSpec((1, N_HEADS, ht, 2 * LANES), lambda i: (i // 2, 0, i % 2, 0))
    return pl.pallas_call(
        body,
        name="qkv_fwd",
        grid=(2 * nb_seq,),
        in_specs=[pl.BlockSpec((ht, GRP_A), lambda i: (i, 0)), full(wq), full(wkv), full(gq), full(gkv), tab, tab, tab],
        out_specs=(qk, qk, pl.BlockSpec((1, N_HEADS, ht, D_V), lambda i: (i // 2, 0, i % 2, 0))),
        out_shape=(
            jax.ShapeDtypeStruct((nb_seq, N_HEADS, tp, 2 * LANES), BF16),
            jax.ShapeDtypeStruct((nb_seq, N_HEADS, tp, 2 * LANES), BF16),
            jax.ShapeDtypeStruct((nb_seq, N_HEADS, tp, D_V), BF16),
        ),
        compiler_params=_params("parallel"),
    )(p, wq, wkv, gq, gkv, *tables)


def _attn_fwd(q, k, v, p, g_attn):
    nb_seq, _, tp, _ = q.shape

    def body(q_ref, k_ref, v_ref, z_ref, g_ref, y_ref, o_ref, lse_ref):
        g = g_ref[...]
        for r0 in range(0, tp, KV_TILE):
            nq = min(KV_TILE, tp - r0)
            kend = r0 + nq
            qq = q_ref[0, 0, r0:kend, :]
            sd = _dot(qq, k_ref[0, 0, r0:kend, :], _NT) * ATTN_SCALE
            causal = (lax.broadcasted_iota(jnp.int32, (nq, nq), 1) <= lax.broadcasted_iota(jnp.int32, (nq, nq), 0))
            sd = jnp.where(causal, sd, NEG_INF)
            m = jnp.max(sd, axis=-1, keepdims=True)
            if r0:
                so = _dot(qq, k_ref[0, 0, 0:r0, :], _NT) * ATTN_SCALE
                m = jnp.maximum(m, jnp.max(so, axis=-1, keepdims=True))
            ed = jnp.exp(sd - m)
            l = jnp.sum(ed, axis=-1, keepdims=True)
            o = _dot(ed.astype(BF16), v_ref[0, 0, r0:kend, :])
            if r0:
                eo = jnp.exp(so - m)
                l = l + jnp.sum(eo, axis=-1, keepdims=True)
                o = o + _dot(eo.astype(BF16), v_ref[0, 0, 0:r0, :])
            o = o * (1.0 / l)
            o_ref[0, 0, r0:kend, :] = o
            lse_ref[0, 0, r0:kend, :] = jnp.broadcast_to(m + jnp.log(l), (nq, LANES))
            ohat, _ = _rms_stats(o)
            z = z_ref[r0:kend, :].astype(F32)
            y_ref[r0:kend, :] = (ohat * g * (z * _sigmoid(z))).astype(BF16)

    qk = pl.BlockSpec((1, 1, tp, 2 * LANES), lambda b, h: (b, h, 0, 0))
    hv = pl.BlockSpec((1, 1, tp, D_V), lambda b, h: (b, h, 0, 0))
    return pl.pallas_call(
        body,
        name="attn_fwd",
        grid=(nb_seq, N_HEADS),
        in_specs=[qk, qk, hv,
                  pl.BlockSpec((tp, LANES), lambda b, h: (b, GRP_A // LANES + h)),
                  pl.BlockSpec((1, LANES), lambda b, h: (0, h))],
        out_specs=(pl.BlockSpec((tp, LANES), lambda b, h: (b, h)), hv, hv),
        out_shape=(
            jax.ShapeDtypeStruct((nb_seq * tp, N_HEADS * D_V), BF16),
            jax.ShapeDtypeStruct((nb_seq, N_HEADS, tp, D_V), F32),
            jax.ShapeDtypeStruct((nb_seq, N_HEADS, tp, LANES), F32),
        ),
        compiler_params=_params("parallel", "parallel"),
    )(q, k, v, p, g_attn)


_CONV_COL0 = (GRP_A + N_HEADS * D_V) // LANES


def _conv_specs(tp, order):
    cols = CONV_WIDTH // LANES
    return [pl.BlockSpec((tp, LANES), functools.partial(
        lambda a, b, off: order(a, b, off), off=_CONV_COL0 + i * cols)) for i in range(4)]


def _conv_fwd(p, conv_w, g_conv, nb_seq, tp):
    def body(b_ref, c_ref, h_ref, z_ref, w_ref, g_ref, y_ref):
        cc = c_ref[...].astype(F32) * h_ref[...].astype(F32)
        row = lax.broadcasted_iota(jnp.int32, (tp, LANES), 0)
        s1 = jnp.where(row >= 1, pltpu.roll(cc, 1, 0), 0.0)
        s2 = jnp.where(row >= 2, pltpu.roll(cc, 2, 0), 0.0)
        yc = b_ref[...].astype(F32) * (w_ref[0:1, :] * s2 + w_ref[1:2, :] * s1 + w_ref[2:3, :] * cc)
        r = lax.rsqrt(_group_mean(yc * yc) + EPS)
        z = z_ref[...].astype(F32)
        y_ref[...] = (yc * r * g_ref[...] * (z * _sigmoid(z))).astype(BF16)

    return pl.pallas_call(
        body,
        name="conv_fwd",
        grid=(nb_seq, CONV_WIDTH // LANES),
        in_specs=_conv_specs(tp, lambda b, t, off: (b, off + t)) + [
            pl.BlockSpec((8, LANES), lambda b, t: (0, t)),
            pl.BlockSpec((1, LANES), lambda b, t: (0, t))],
        out_specs=pl.BlockSpec((tp, LANES), lambda b, t: (b, t)),
        out_shape=jax.ShapeDtypeStruct((nb_seq * tp, CONV_WIDTH), BF16),
        compiler_params=_params("parallel", "parallel"),
    )(p, p, p, p, conv_w, g_conv)


def _token_copy(hbm, b, k, ts, buf, sem, to_hbm=False):
    lo, hi = max(k * ts - LANES, 0), (k + 1) * ts - LANES
    off = lo - (k * ts - LANES)
    src, dst = hbm.at[b, pl.ds(lo, hi - lo)], buf.at[pl.ds(off, hi - lo)]
    if to_hbm:
        src, dst = dst, src
    return pltpu.make_async_copy(src, dst, sem)


def _for_tile(k, nt, fn):
    for kk in range(nt):
        @pl.when(k == kk)
        def _(kk=kk):
            fn(kk)


def _out_proj_loss(ya, yc, w_out, x, target, g_final, nt):
    nb_seq, s, d = x.shape
    r, ka = ya.shape
    ts = (s + LANES) // nt
    steps = nb_seq * nt

    def body(a_ref, c_ref, w_ref, x_hbm, t_hbm, g_ref, dh_ref, dhb_ref, dg_ref, loss_ref,
             xbuf, tbuf, acc_ref, sems):
        i = pl.program_id(0)
        b, k = i // nt, i % nt

        @pl.when(i == 0)
        def _():
            acc_ref[...] = jnp.zeros_like(acc_ref)
            dg_ref[...] = jnp.zeros_like(dg_ref)

        slot = i % 2

        def fetch(seq, kk, sl):
            return [_token_copy(x_hbm, seq, kk, ts, xbuf.at[sl], sems.at[sl, 0]),
                    _token_copy(t_hbm, seq, kk, ts, tbuf.at[sl], sems.at[sl, 1])]

        def start(seq, sl, kk):
            if kk == 0:
                xbuf[sl, 0:LANES, :] = jnp.zeros((LANES, d), F32)
                tbuf[sl, 0:LANES, :] = jnp.zeros((LANES, d), F32)
            for cp in fetch(seq, kk, sl):
                cp.start()

        @pl.when(i == 0)
        def _():
            start(0, 0, 0)

        @pl.when(i + 1 < steps)
        def _():
            _for_tile((i + 1) % nt, nt, functools.partial(start, (i + 1) // nt, 1 - slot))

        mix = _dot(a_ref[...], w_ref[0:ka, :]) + _dot(c_ref[...], w_ref[ka:, :])
        _for_tile(k, nt, lambda kk: [cp.wait() for cp in fetch(b, kk, slot)])

        real = (lax.broadcasted_iota(jnp.int32, (ts, d), 0) >= LANES) | (k > 0)
        g = g_ref[...]
        hhat, rstd = _rms_stats(xbuf[slot] + mix)
        e = jnp.where(real, hhat * g - tbuf[slot], 0.0)
        acc_ref[...] += jnp.sum(e * e, axis=0, keepdims=True)
        dy = e * (1.0 / d)
        dg_ref[...] += jnp.sum(dy * hhat, axis=0, keepdims=True)
        dh = _rms_bwd(g * dy, hhat, rstd)
        dh_ref[...] = dh
        dhb_ref[...] = dh.astype(BF16)

        @pl.when(i == steps - 1)
        def _():
            total = jnp.sum(acc_ref[...], axis=1, keepdims=True)
            loss_ref[...] = jnp.broadcast_to((0.5 / d) * total, loss_ref.shape)

    hbm = pl.BlockSpec(memory_space=pl.ANY)
    row = pl.BlockSpec((ts, d), lambda i: (i, 0))
    vec = pl.BlockSpec((1, d), lambda i: (0, 0))
    return pl.pallas_call(
        body,
        name="out_proj_loss",
        grid=(steps,),
        in_specs=[pl.BlockSpec((ts, ka), lambda i: (i, 0)), pl.BlockSpec((ts, yc.shape[1]), lambda i: (i, 0)),
                  pl.BlockSpec(w_out.shape, lambda i: (0, 0)), hbm, hbm, vec],
        out_specs=(row, row, vec, pl.BlockSpec((1, LANES), lambda i: (0, 0))),
        out_shape=(
            jax.ShapeDtypeStruct((r, d), F32),
            jax.ShapeDtypeStruct((r, d), BF16),
            jax.ShapeDtypeStruct((1, d), F32),
            jax.ShapeDtypeStruct((1, LANES), F32),
        ),
        scratch_shapes=[pltpu.VMEM((2, ts, d), F32), pltpu.VMEM((2, ts, d), F32), pltpu.VMEM((1, d), F32),
                        pltpu.SemaphoreType.DMA((2, 2))],
        compiler_params=_params("arbitrary"),
    )(ya, yc, w_out, x, target, g_final)


def _out_proj_bwd(dhb, w_out, ya, yc, bm):
    r, d = dhb.shape
    ka = ya.shape[1]
    n_mix = w_out.shape[0]
    last = r // bm - 1

    def body(dh_ref, w_ref, a_ref, c_ref, dcat_ref, dw_ref, acc_ref):
        @pl.when(pl.program_id(0) == 0)
        def _():
            acc_ref[...] = jnp.zeros_like(acc_ref)

        dh = dh_ref[...]
        dcat_ref[...] = _dot(dh, w_ref[...], _NT).astype(BF16)
        acc_ref[0:ka, :] += _dot(a_ref[...], dh, _TN)
        acc_ref[ka:, :] += _dot(c_ref[...], dh, _TN)

        @pl.when(pl.program_id(0) == last)
        def _():
            dw_ref[...] = acc_ref[...].astype(BF16)

    return pl.pallas_call(
        body,
        name="out_proj_bwd",
        grid=(r // bm,),
        in_specs=[pl.BlockSpec((bm, d), lambda i: (i, 0)), pl.BlockSpec(w_out.shape, lambda i: (0, 0)),
                  pl.BlockSpec((bm, ka), lambda i: (i, 0)), pl.BlockSpec((bm, yc.shape[1]), lambda i: (i, 0))],
        out_specs=(pl.BlockSpec((bm, n_mix), lambda i: (i, 0)),
                   pl.BlockSpec((n_mix, d), lambda i: (0, 0))),
        out_shape=(jax.ShapeDtypeStruct((r, n_mix), BF16),
                   jax.ShapeDtypeStruct((n_mix, d), BF16)),
        scratch_shapes=[pltpu.VMEM((n_mix, d), F32)],
        compiler_params=_params("arbitrary"),
    )(dhb, w_out, ya, yc)


def _attn_bwd(q, k, v, o, lse, dcat, p, g_attn, send_out):
    nb_seq, _, tp, _ = q.shape
    steps = N_HEADS * nb_seq

    def body(q_ref, k_ref, v_ref, o_ref, lse_ref, dy_ref, z_ref, g_ref, pay_ref,
             dq_ref, dk_ref, dv_ref, dz_ref, dg_ref, r2_ref, dq_acc, r1, sums, *sems):
        t = pl.program_id(0) * nb_seq + pl.program_id(1)

        def plan():
            return _reduce_plan((pay_ref,), (None,), (r1,), (sums,), (r2_ref,), *sems)

        @pl.when(t == 0)
        def _():
            plan()[0]()

        @pl.when(t == 1)
        def _():
            plan()[1]()

        @pl.when(pl.program_id(1) == 0)
        def _():
            dg_ref[...] = jnp.zeros_like(dg_ref)

        g = g_ref[...]
        z = z_ref[...].astype(F32)
        o = o_ref[0, 0]
        dy = dy_ref[...].astype(F32)
        sig = _sigmoid(z)
        ohat, r = _rms_stats(o)
        don = dy * (z * sig)
        dz_ref[...] = (dy * (ohat * g) * (sig * (1.0 + z * (1.0 - sig)))).astype(BF16)
        dg_ref[...] += jnp.sum(don * ohat, axis=0, keepdims=True)
        do = _rms_bwd(g * don, ohat, r)
        dvec = jnp.sum(do * o, axis=-1, keepdims=True)
        dob = do.astype(BF16)
        lse_col = lse_ref[0, 0, :, 0:1]
        dq_acc[...] = jnp.zeros_like(dq_acc)
        for k0 in range(0, tp, KV_TILE):
            nk = min(KV_TILE, tp - k0)
            nq = tp - k0
            qq = q_ref[0, 0, k0:, :]
            kk = k_ref[0, 0, k0:k0 + nk, :]
            causal = (lax.broadcasted_iota(jnp.int32, (nq, nk), 1) <= lax.broadcasted_iota(jnp.int32, (nq, nk), 0))
            pr = jnp.where(causal, jnp.exp(_dot(qq, kk, _NT) * ATTN_SCALE - lse_col[k0:]), 0.0)
            dp = _dot(dob[k0:], v_ref[0, 0, k0:k0 + nk, :], _NT)
            ds = (pr * (dp - dvec[k0:]) * ATTN_SCALE).astype(BF16)
            dv_ref[0, 0, k0:k0 + nk, :] = _dot(pr.astype(BF16), dob[k0:], _TN).astype(BF16)
            dk_ref[0, 0, k0:k0 + nk, :] = _dot(ds, qq, _TN).astype(BF16)
            dq_acc[k0:, :] += _dot(ds, kk)
        dq_ref[0, 0] = dq_acc[...].astype(BF16)

        @pl.when(t == steps - 1)
        def _():
            plan()[2]()

    qk = pl.BlockSpec((1, 1, tp, 2 * LANES), lambda h, b: (b, h, 0, 0))
    hv = pl.BlockSpec((1, 1, tp, D_V), lambda h, b: (b, h, 0, 0))
    col = pl.BlockSpec((tp, LANES), lambda h, b: (b, h))
    slot = send_out.shape[1:]
    return pl.pallas_call(
        body,
        name="attn_bwd",
        grid=(N_HEADS, nb_seq),
        in_specs=[qk, qk, hv, hv, hv, col,
                  pl.BlockSpec((tp, LANES), lambda h, b: (b, GRP_A // LANES + h)),
                  pl.BlockSpec((1, LANES), lambda h, b: (0, h)),
                  pl.BlockSpec(send_out.shape, lambda h, b: (0, 0, 0))],
        out_specs=(qk, qk, hv, col, pl.BlockSpec((1, LANES), lambda h, b: (0, h)),
                   pl.BlockSpec(memory_space=pl.ANY)),
        out_shape=(
            jax.ShapeDtypeStruct((nb_seq, N_HEADS, tp, 2 * LANES), BF16),
            jax.ShapeDtypeStruct((nb_seq, N_HEADS, tp, 2 * LANES), BF16),
            jax.ShapeDtypeStruct((nb_seq, N_HEADS, tp, D_V), BF16),
            jax.ShapeDtypeStruct((nb_seq * tp, N_HEADS * D_V), BF16),
            jax.ShapeDtypeStruct((1, N_HEADS * D_V), F32),
            jax.ShapeDtypeStruct((N_CHIPS,) + slot, BF16),
        ),
        scratch_shapes=[pltpu.VMEM((tp, 2 * LANES), F32)] + _reduce_scratch([(slot, BF16)], [False]),
        compiler_params=_params("arbitrary", "arbitrary"),
    )(q, k, v, o, lse, dcat, p, g_attn, send_out)


def _qkv_bwd(p, dq, dk, dv, wq, wkv, gq, gkv, tables):
    nb_seq, _, tp, _ = dq.shape
    ht = tp // 2

    def body(pa_ref, dq_ref, dk_ref, dv_ref, wq_ref, wkv_ref, gq_ref, gkv_ref, cos_ref, sa_ref, sb_ref,
             dpa_ref, dwq_ref, dwkv_ref, dgq_ref, dgkv_ref):
        @pl.when(pl.program_id(0) == 0)
        def _():
            dwq_ref[...] = jnp.zeros_like(dwq_ref)
            dwkv_ref[...] = jnp.zeros_like(dwkv_ref)
            dgq_ref[...] = jnp.zeros_like(dgq_ref)
            dgkv_ref[...] = jnp.zeros_like(dgkv_ref)

        pa = pa_ref[...].astype(F32)
        gq, gkv = gq_ref[...], gkv_ref[...]
        cq_hat, rq = _rms_stats(pa[:, :Q_RANK])
        ckv_hat, rkv = _rms_stats(pa[:, Q_RANK:Q_RANK + KV_RANK])
        tabs = (cos_ref[...], sa_ref[...], sb_ref[...])

        pe = [dq_ref[0, h, :, D_NOPE:].astype(F32) for h in range(N_HEADS)]
        pairs = [_rope_t(pe[2 * i] + pltpu.roll(pe[2 * i + 1], D_ROPE, 1), *tabs).astype(BF16) for i in range(2)]
        dq_flat = jnp.concatenate([dq_ref[0, h, :, :D_NOPE] for h in range(N_HEADS)] + pairs, axis=1)
        dwq_ref[...] += _dot((cq_hat * gq).astype(BF16), dq_flat, _TN)
        dcqn = _dot(dq_flat, wq_ref[...], _NT)
        dgq_ref[...] += jnp.sum(dcqn * cq_hat, axis=0, keepdims=True)
        dcq = _rms_bwd(gq * dcqn, cq_hat, rq)

        dkv_flat = jnp.concatenate([dk_ref[0, h, :, :D_NOPE] for h in range(N_HEADS)]
                                   + [dv_ref[0, h] for h in range(N_HEADS)], axis=1)
        dwkv_ref[...] += _dot((ckv_hat * gkv).astype(BF16), dkv_flat, _TN)
        dckvn = _dot(dkv_flat, wkv_ref[...], _NT)
        dgkv_ref[...] += jnp.sum(dckvn * ckv_hat, axis=0, keepdims=True)
        dckv = _rms_bwd(gkv * dckvn, ckv_hat, rkv)

        dk_pe = dk_ref[0, 0, :, D_NOPE:].astype(F32)
        for h in range(1, N_HEADS):
            dk_pe = dk_pe + dk_ref[0, h, :, D_NOPE:].astype(F32)
        dk_pe = jnp.where(lax.broadcasted_iota(jnp.int32, (ht, LANES), 1) < D_ROPE, dk_pe, 0.0)
        dpa_ref[...] = jnp.concatenate([dcq, dckv, _rope_t(dk_pe, *tabs)], axis=1).astype(BF16)

    full = lambda a: pl.BlockSpec(a.shape, lambda i: (0,) * a.ndim)
    tab = pl.BlockSpec((ht, LANES), lambda i: (i % 2, 0))
    qk = pl.BlockSpec((1, N_HEADS, ht, 2 * LANES), lambda i: (i // 2, 0, i % 2, 0))
    acc = lambda shape: pl.BlockSpec(shape, lambda i: (0, 0))
    return pl.pallas_call(
        body,
        name="qkv_bwd",
        grid=(2 * nb_seq,),
        in_specs=[pl.BlockSpec((ht, GRP_A), lambda i: (i, 0)), qk, qk,
                  pl.BlockSpec((1, N_HEADS, ht, D_V), lambda i: (i // 2, 0, i % 2, 0)),
                  full(wq), full(wkv), full(gq), full(gkv), tab, tab, tab],
        out_specs=(pl.BlockSpec((ht, GRP_A), lambda i: (i, 0)),
                   acc(wq.shape), acc(wkv.shape), acc((1, Q_RANK)), acc((1, KV_RANK))),
        out_shape=(
            jax.ShapeDtypeStruct((nb_seq * tp, GRP_A), BF16),
            jax.ShapeDtypeStruct(wq.shape, F32),
            jax.ShapeDtypeStruct(wkv.shape, F32),
            jax.ShapeDtypeStruct((1, Q_RANK), F32),
            jax.ShapeDtypeStruct((1, KV_RANK), F32),
        ),
        compiler_params=_params("arbitrary"),
    )(p, dq, dk, dv, wq, wkv, gq, gkv, *tables)


def _conv_bwd(p, dcat, conv_w, g_conv, nb_seq, tp):
    cols = CONV_WIDTH // LANES

    def body(b_ref, c_ref, h_ref, z_ref, dy_ref, w_ref, g_ref,
             db_ref, dc_ref, dh_ref, dz_ref, dw_ref, dg_ref):
        @pl.when(pl.program_id(1) == 0)
        def _():
            dw_ref[...] = jnp.zeros_like(dw_ref)
            dg_ref[...] = jnp.zeros_like(dg_ref)

        cb, c, h = b_ref[...].astype(F32), c_ref[...].astype(F32), h_ref[...].astype(F32)
        z, dy = z_ref[...].astype(F32), dy_ref[...].astype(F32)
        g = g_ref[...]
        w0, w1, w2 = w_ref[0:1, :], w_ref[1:2, :], w_ref[2:3, :]
        cc = c * h
        row = lax.broadcasted_iota(jnp.int32, (tp, LANES), 0)
        s1 = jnp.where(row >= 1, pltpu.roll(cc, 1, 0), 0.0)
        s2 = jnp.where(row >= 2, pltpu.roll(cc, 2, 0), 0.0)
        dwc = w0 * s2 + w1 * s1 + w2 * cc
        yc = cb * dwc
        r = lax.rsqrt(_group_mean(yc * yc) + EPS)
        ychat = yc * r
        sig = _sigmoid(z)
        dz_ref[...] = (dy * (ychat * g) * (sig * (1.0 + z * (1.0 - sig)))).astype(BF16)
        dyn = dy * (z * sig)
        dg_ref[...] += jnp.sum(dyn * ychat, axis=0, keepdims=True)
        gd = g * dyn
        dyc = r * (gd - ychat * _group_mean(gd * ychat))
        db_ref[...] = (dyc * dwc).astype(BF16)
        ddw = dyc * cb
        dw_ref[0:1, :] += jnp.sum(ddw * s2, axis=0, keepdims=True)
        dw_ref[1:2, :] += jnp.sum(ddw * s1, axis=0, keepdims=True)
        dw_ref[2:3, :] += jnp.sum(ddw * cc, axis=0, keepdims=True)
        u1 = jnp.where(row <= tp - 2, pltpu.roll(ddw, tp - 1, 0), 0.0)
        u2 = jnp.where(row <= tp - 3, pltpu.roll(ddw, tp - 2, 0), 0.0)
        dcc = w2 * ddw + w1 * u1 + w0 * u2
        dc_ref[...] = (dcc * h).astype(BF16)
        dh_ref[...] = (dcc * c).astype(BF16)

    col = pl.BlockSpec((tp, LANES), lambda t, b: (b, t))
    out = jax.ShapeDtypeStruct((nb_seq * tp, CONV_WIDTH), BF16)
    return pl.pallas_call(
        body,
        name="conv_bwd",
        grid=(cols, nb_seq),
        in_specs=_conv_specs(tp, lambda t, b, off: (b, off + t)) + [
            pl.BlockSpec((tp, LANES), lambda t, b: (b, N_HEADS * D_V // LANES + t)),
            pl.BlockSpec((8, LANES), lambda t, b: (0, t)),
            pl.BlockSpec((1, LANES), lambda t, b: (0, t))],
        out_specs=(col, col, col, col,
                   pl.BlockSpec((8, LANES), lambda t, b: (0, t)), pl.BlockSpec((1, LANES), lambda t, b: (0, t))),
        out_shape=(out, out, out, out,
                   jax.ShapeDtypeStruct((8, CONV_WIDTH), F32), jax.ShapeDtypeStruct((1, CONV_WIDTH), F32)),
        compiler_params=_params("arbitrary", "arbitrary"),
    )(p, p, p, p, dcat, conv_w, g_conv)


def _input_bwd(dps, w_in, x, meta, dh, norm_g, nt, send_in, small_grads):
    nb_seq, s, d = x.shape
    r, kb = dps[0].shape
    ts = (s + LANES) // nt
    steps = nb_seq * nt
    n_dp, n_small = len(dps), len(small_grads)
    in_slot, small_slot = send_in.shape[1:], (SMALL_ROWS, LANES)

    def body(*refs):
        dp_refs, w_ref, x_hbm, meta_ref, dh_ref, g_ref, pay_ref = refs[:n_dp], *refs[n_dp:n_dp + 6]
        o = n_dp + 6
        small_refs = refs[o:o + n_small]
        o += n_small
        gx_hbm, dmeta_ref, dg_ref, r2_in, r2_small = refs[o:o + 5]
        xbuf, gxbuf, tok_sems, ssmall, own_in, r1_in, sum_in, r1_small, sum_small = refs[o + 5:o + 14]
        sems = refs[o + 14:]
        i = pl.program_id(0)
        b, k = i // nt, i % nt

        def plan():
            return _reduce_plan((pay_ref, ssmall), (own_in, None), (r1_in, r1_small), (sum_in, sum_small),
                                (r2_in, r2_small), *sems)

        @pl.when(i == 0)
        def _():
            dmeta_ref[...] = jnp.zeros_like(dmeta_ref)
            dg_ref[...] = jnp.zeros_like(dg_ref)
            _pack_small(ssmall, *small_refs)
            plan()[0]()

        @pl.when(i == 1)
        def _():
            plan()[1]()

        def start(kk):
            if kk == 0:
                xbuf[0:PAD_FRONT, :] = jnp.zeros((PAD_FRONT, d), F32)
                xbuf[PAD_FRONT:LANES, :] = meta_ref[...]
            _token_copy(x_hbm, b, kk, ts, xbuf, tok_sems.at[0]).start()

        _for_tile(k, nt, start)
        du = _dot(dp_refs[0][...], w_ref[0:kb, :])
        for j in range(1, n_dp):
            du = du + _dot(dp_refs[j][...], w_ref[kb * j:kb * (j + 1), :])
        _for_tile(k, nt, lambda kk: _token_copy(x_hbm, b, kk, ts, xbuf, tok_sems.at[0]).wait())

        g = g_ref[...]
        hhat, rstd = _rms_stats(xbuf[...])
        dg_ref[...] += jnp.sum(du * hhat, axis=0, keepdims=True)
        res = _rms_bwd(g * du, hhat, rstd) + dh_ref[...]

        @pl.when(i > 0)
        def _():
            _for_tile(k, nt, lambda kk: _token_copy(gx_hbm, b, (kk - 1) % nt, ts, gxbuf, tok_sems.at[1], True).wait())

        gxbuf[...] = res

        @pl.when(k == 0)
        def _():
            dmeta_ref[...] += gxbuf[PAD_FRONT:LANES, :]

        _for_tile(k, nt, lambda kk: _token_copy(gx_hbm, b, kk, ts, gxbuf, tok_sems.at[1], True).start())

        @pl.when(i == steps - 1)
        def _():
            _token_copy(gx_hbm, b, nt - 1, ts, gxbuf, tok_sems.at[1], True).wait()
            plan()[2]()

    whole = lambda a: pl.BlockSpec(a.shape, lambda i: (0,) * a.ndim)
    hbm = pl.BlockSpec(memory_space=pl.ANY)
    return pl.pallas_call(
        body,
        name="input_bwd",
        grid=(steps,),
        in_specs=[pl.BlockSpec((ts, kb), lambda i: (i, 0)) for _ in dps]
        + [whole(w_in), hbm, whole(meta), pl.BlockSpec((ts, d), lambda i: (i, 0)), whole(norm_g), hbm]
        + [whole(a) for a in small_grads],
        out_specs=(hbm, pl.BlockSpec((N_META, d), lambda i: (0, 0)), pl.BlockSpec((1, d), lambda i: (0, 0)), hbm, hbm),
        out_shape=(jax.ShapeDtypeStruct((nb_seq, s, d), F32),
                   jax.ShapeDtypeStruct((N_META, d), F32),
                   jax.ShapeDtypeStruct((1, d), F32),
                   jax.ShapeDtypeStruct((N_CHIPS,) + in_slot, BF16),
                   jax.ShapeDtypeStruct((N_CHIPS,) + small_slot, F32)),
        scratch_shapes=[pltpu.VMEM((ts, d), F32), pltpu.VMEM((ts, d), F32), pltpu.SemaphoreType.DMA((2,)),
                        pltpu.VMEM((N_DEV,) + small_slot, F32)]
        + _reduce_scratch([(in_slot, BF16), (small_slot, F32)], [True, False]),
        compiler_params=_params("arbitrary"),
    )(*dps, w_in, x, meta, dh, norm_g, send_in, *small_grads)


def _in_proj_bwd_w(u, dps, bm):
    r, d = u.shape
    kb = dps[0].shape[1]
    last = r // bm - 1

    def body(*refs):
        u_ref, dp_refs, o_ref, acc_ref = refs[0], refs[1:1 + len(dps)], refs[1 + len(dps)], refs[2 + len(dps)]

        @pl.when(pl.program_id(0) == 0)
        def _():
            acc_ref[...] = jnp.zeros_like(acc_ref)

        uu = u_ref[...]
        for j in range(len(dps)):
            acc_ref[kb * j:kb * (j + 1), :] += _dot(dp_refs[j][...], uu, _TN)

        @pl.when(pl.program_id(0) == last)
        def _():
            for k in range(N_DEV):
                for s, e, c0 in _in_pieces(k):
                    o_ref[k, s:e, :] = acc_ref[c0:c0 + e - s, :].astype(BF16)
                o_ref[k, SHARD_IN:, :] = jnp.zeros((SHARD_IN_PAD - SHARD_IN, d), BF16)

    return pl.pallas_call(
        body,
        name="in_proj_bwd_w",
        grid=(r // bm,),
        in_specs=[pl.BlockSpec((bm, d), lambda i: (i, 0))] + [pl.BlockSpec((bm, kb), lambda i: (i, 0)) for _ in dps],
        out_specs=pl.BlockSpec((N_DEV, SHARD_IN_PAD, d), lambda i: (0, 0, 0)),
        out_shape=jax.ShapeDtypeStruct((N_DEV, SHARD_IN_PAD, d), BF16),
        scratch_shapes=[pltpu.VMEM((kb * len(dps), d), F32)],
        compiler_params=_params("arbitrary"),
    )(u, *dps)


def _local_step(x, loss_target, u, p, meta_f, norm_g, w_in_p, q_norm_g, w_q_p, kv_norm_g, w_kv_p, conv_w_f,
                attn_out_g, conv_out_g, w_out_f, g_final):
    nb_seq, s, d = x.shape
    tp = s + LANES
    ht = tp // 2
    tables = _rope_tables(tp)

    q, k, v = _qkv_fwd(p, w_q_p, w_kv_p, q_norm_g, kv_norm_g, tables, nb_seq, tp)
    ya, o, lse = _attn_fwd(q, k, v, p, attn_out_g)
    yc = _conv_fwd(p, conv_w_f, conv_out_g, nb_seq, tp)
    dh, dhb, d_final_g, loss_part = _out_proj_loss(ya, yc, w_out_f, x, loss_target, g_final, TOKEN_TILES)

    dcat, d_w_out = _out_proj_bwd(dhb, w_out_f, ya, yc, ht)
    send_out = d_w_out.reshape(N_DEV, SHARD_OUT, d)
    dq, dk, dv, dz_attn, d_attn_g, r_out = _attn_bwd(q, k, v, o, lse, dcat, p, attn_out_g, send_out)
    dpa, d_wq_p, d_wkv_p, d_gq, d_gkv = _qkv_bwd(p, dq, dk, dv, w_q_p, w_kv_p, q_norm_g, kv_norm_g, tables)
    d_b, d_c, d_h, dz_conv, d_conv_w, d_conv_g = _conv_bwd(p, dcat, conv_w_f, conv_out_g, nb_seq, tp)
    dps = (dpa, dz_attn, d_b, d_c, d_h, dz_conv)
    send_in = _in_proj_bwd_w(u, dps, ht // 2)
    small = (d_wq_p, d_wkv_p, d_conv_w, d_final_g, d_gq, d_gkv, d_attn_g, d_conv_g, loss_part)
    grad_x, d_meta, d_norm_g, r_in, r_small = _input_bwd(
        dps, w_in_p, x, meta_f, dh, norm_g, TOKEN_TILES, send_in, small)
    return grad_x, r_in, r_out, r_small, d_meta, d_norm_g


def kernel(x, meta_tokens, norm_g, w_in, q_norm_g, w_q_up, kv_norm_g, w_kv_up, conv_w, attn_out_g, conv_out_g, w_out, final_norm_g, loss_target, m_meta_tokens, m_norm_g, m_w_in, m_q_norm_g, m_w_q_up, m_kv_norm_g, m_w_kv_up, m_conv_w, m_attn_out_g, m_conv_out_g, m_w_out, m_final_norm_g, v_meta_tokens, v_norm_g, v_w_in, v_q_norm_g, v_w_q_up, v_kv_norm_g, v_w_kv_up, v_conv_w, v_attn_out_g, v_conv_out_g, v_w_out, v_final_norm_g):
    d = x.shape[-1]
    ht = (x.shape[1] + LANES) // 2
    u, w_in_p, meta_f = _prep_gather(x, meta_tokens, norm_g, w_in[0].T)
    p, w_q_p, w_kv_p, w_out_f, conv_w_f = _in_proj_gather(
        u, w_in_p, w_q_up[0], w_kv_up[0], w_out[0], conv_w[0], ht, GRP_A)
    g_final = final_norm_g.reshape(1, d)
    grad_x, r_in, r_out, r_small, d_meta, d_norm_g = _local_step(
        x, loss_target, u, p, meta_f, norm_g, w_in_p, q_norm_g, w_q_p, kv_norm_g, w_kv_p, conv_w_f,
        attn_out_g, conv_out_g, w_out_f, g_final)

    flat = lambda a: a.reshape(a.shape[-2:]) if a.ndim == 3 else a.reshape(1, -1) if a.ndim == 1 else a
    transposed = ("w_in",)
    to_kernel = lambda n, a: flat(a).T if n in transposed else flat(a)
    from_kernel = lambda n, a, shape: (a.T if n in transposed else a).reshape(shape)
    params = {
        "meta_tokens": (meta_tokens, m_meta_tokens, v_meta_tokens),
        "norm_g": (norm_g, m_norm_g, v_norm_g),
        "w_in": (w_in, m_w_in, v_w_in),
        "q_norm_g": (q_norm_g, m_q_norm_g, v_q_norm_g),
        "w_q_up": (w_q_up, m_w_q_up, v_w_q_up),
        "kv_norm_g": (kv_norm_g, m_kv_norm_g, v_kv_norm_g),
        "w_kv_up": (w_kv_up, m_w_kv_up, v_w_kv_up),
        "conv_w": (conv_w, m_conv_w, v_conv_w),
        "attn_out_g": (attn_out_g, m_attn_out_g, v_attn_out_g),
        "conv_out_g": (conv_out_g, m_conv_out_g, v_conv_out_g),
        "w_out": (w_out, m_w_out, v_w_out),
        "final_norm_g": (final_norm_g, m_final_norm_g, v_final_norm_g),
    }
    grads, loss = _reduce_tail(r_in, r_out, r_small, d_meta, d_norm_g)
    updated = _adamw(grads, {n: tuple(to_kernel(n, a) for a in t) for n, t in params.items()})
    outs = [[from_kernel(n, updated[n][i], params[n][0].shape) for n, _ in PARAM_SHAPES] for i in range(4)]
    return (loss[0, 0], grad_x, *outs[0], *outs[1], *outs[2], *outs[3])
```

```python
import functools

import jax
import jax.numpy as jnp
from jax import lax
from jax.experimental import pallas as pl
from jax.experimental.pallas import tpu as pltpu

F32 = jnp.float32
BF16 = jnp.bfloat16

N_META = 16
D_MODEL = 1024
N_HEADS = 4
D_NOPE = 128
D_ROPE = 64
D_V = 128
Q_RANK = 256
KV_RANK = 128
CONV_WIDTH = 512
CONV_GROUP = 64
ROPE_THETA = 10000.0
ATTN_SCALE = (D_NOPE + D_ROPE) ** -0.5
EPS = 1e-6
NEG_INF = -1e30

ADAM_LR = 0.001
ADAM_B1 = 0.9
ADAM_B2 = 0.999
ADAM_EPS = 1e-08
ADAM_WD = 0.01
ADAM_STEP = 10

LANES = 128
PAD_FRONT = LANES - N_META
KV_TILE = 256
N_DEV = 8
VMEM_LIMIT = 56 * 1024 * 1024

IN_PAD = 3072
GRP_A = 512
N_A = Q_RANK + KV_RANK + D_ROPE
IN_PROJ = 3008
SHARD_IN = IN_PROJ // N_DEV
SHARD_IN_PAD = 384
SHARD_Q = 96
SHARD_KV = 128
SHARD_OUT = 128
SHARD_CONV = 64
SHARD_META = 128
Q_COLS = N_HEADS * (D_NOPE + D_ROPE)
KV_COLS = N_HEADS * (D_NOPE + D_V)

ROW_Q, ROW_KV, ROW_META, ROW_CONV = 0, 256, 384, 400
ROW_REPL = 408
ROW_NORM, ROW_FINAL, ROW_GQ, ROW_GKV, ROW_ATTN, ROW_CONVG, ROW_LOSS = 408, 416, 424, 426, 427, 431, 435
SMALL_ROWS = 440

PARAM_SHAPES = (
    ("meta_tokens", (N_META, SHARD_META)), ("norm_g", (1, D_MODEL)), ("w_in", (SHARD_IN, D_MODEL)),
    ("q_norm_g", (1, Q_RANK)), ("w_q_up", (Q_RANK, SHARD_Q)), ("kv_norm_g", (1, KV_RANK)),
    ("w_kv_up", (KV_RANK, SHARD_KV)), ("conv_w", (3, SHARD_CONV)), ("attn_out_g", (1, CONV_WIDTH)),
    ("conv_out_g", (1, CONV_WIDTH)), ("w_out", (SHARD_OUT, D_MODEL)), ("final_norm_g", (1, D_MODEL)),
)


def _in_pieces(k):
    lo, hi = SHARD_IN * k, SHARD_IN * (k + 1)
    out = []
    if lo < N_A:
        out.append((0, min(hi, N_A) - lo, lo))
    if hi > N_A:
        s = max(lo, N_A)
        out.append((s - lo, hi - lo, s + GRP_A - N_A))
    return out


def _q_pieces(k):
    lo, hi = SHARD_Q * k, SHARD_Q * (k + 1)
    out = []
    for h in range(N_HEADS):
        base = (D_NOPE + D_ROPE) * h
        s, e = max(lo, base), min(hi, base + D_NOPE)
        if s < e:
            out.append((s - lo, e - lo, D_NOPE * h + s - base))
        s, e = max(lo, base + D_NOPE), min(hi, base + D_NOPE + D_ROPE)
        if s < e:
            out.append((s - lo, e - lo, N_HEADS * D_NOPE + D_ROPE * h + s - base - D_NOPE))
    return out


def _kv_dst(k):
    return D_NOPE * (k // 2) + (N_HEADS * D_NOPE if k % 2 else 0)


def _params(*sem):
    return pltpu.CompilerParams(dimension_semantics=sem, vmem_limit_bytes=VMEM_LIMIT)


def _rms_stats(x):
    r = lax.rsqrt(jnp.mean(x * x, axis=-1, keepdims=True) + EPS)
    return x * r, r


def _rms_bwd(gdy, xhat, r):
    return r * (gdy - xhat * jnp.mean(gdy * xhat, axis=-1, keepdims=True))


def _sigmoid(z):
    return 1.0 / (1.0 + jnp.exp(-z))


def _group_mean(x):
    i0 = lax.broadcasted_iota(jnp.int32, (LANES, LANES), 0) // CONV_GROUP
    i1 = lax.broadcasted_iota(jnp.int32, (LANES, LANES), 1) // CONV_GROUP
    m = jnp.where(i0 == i1, 1.0 / CONV_GROUP, 0.0).astype(BF16)
    hi = x.astype(BF16)
    lo = (x - hi.astype(F32)).astype(BF16)
    return jnp.dot(hi, m, preferred_element_type=F32) + jnp.dot(lo, m, preferred_element_type=F32)


_NT = (((1,), (1,)), ((), ()))
_TN = (((0,), (0,)), ((), ()))


def _dot(a, b, dims=None):
    if dims is None:
        return jnp.dot(a, b, preferred_element_type=F32)
    return lax.dot_general(a, b, dims, preferred_element_type=F32)


def _device_position():
    x, y, c = lax.axis_index("x"), lax.axis_index("y"), lax.axis_index("c")
    return x, y, c, 4 * x + 2 * y + c


def _gather_plan(srcs, slots, send_sems, recv_sems, local_sems):
    x, y, c, _ = _device_position()
    me, sibling = (x, y, c), (x, y, 1 - c)
    chips = [(1 - x, y), (x, 1 - y), (1 - x, 1 - y)]
    n = len(srcs)

    def slot(a, px, py, pc):
        return slots[a].at[4 * px + 2 * py + pc]

    def copy(a, k, block, to, own=False):
        return pltpu.make_async_remote_copy(
            src_ref=srcs[a] if own else slot(a, *block),
            dst_ref=slot(a, *block),
            send_sem=send_sems.at[7 * a + k],
            recv_sem=recv_sems.at[7 * a + k],
            device_id=to,
            device_id_type=pl.DeviceIdType.MESH,
        )

    def local(a):
        return pltpu.make_async_copy(srcs[a], slot(a, *me), local_sems.at[a])

    def firsts(a):
        return [copy(a, 0, me, sibling, own=True)] + [
            copy(a, 1 + j, me, (*chip, c), own=True) for j, chip in enumerate(chips)]

    def start():
        for a in range(n):
            local(a).start()
            for cp in firsts(a):
                cp.start()

    def forward():
        for j, chip in enumerate(chips):
            for a in range(n):
                copy(a, 1 + j, (*chip, c), me).wait_recv()
                copy(a, 4 + j, (*chip, c), sibling).start()

    def finish():
        for a in range(n):
            copy(a, 0, sibling, me).wait_recv()
            for j, chip in enumerate(chips):
                copy(a, 4 + j, (*chip, 1 - c), me).wait_recv()
        for a in range(n):
            for cp in firsts(a) + [copy(a, 4 + j, (*chip, c), sibling) for j, chip in enumerate(chips)]:
                cp.wait_send()
            local(a).wait()

    return start, forward, finish


def _adam_update(g, w, m, v):
    m_new = ADAM_B1 * m + (1.0 - ADAM_B1) * g
    v_new = ADAM_B2 * v + (1.0 - ADAM_B2) * (g * g)
    m_hat = m_new / (1.0 - ADAM_B1 ** ADAM_STEP)
    v_hat = v_new / (1.0 - ADAM_B2 ** ADAM_STEP)
    return -ADAM_LR * (m_hat / (jnp.sqrt(v_hat) + ADAM_EPS) + ADAM_WD * w), m_new, v_new


def _adamw(grads, params):
    names = [n for n, _ in PARAM_SHAPES]
    n_p = len(names)

    def body(*refs):
        for i in range(n_p):
            g = refs[i][...]
            w, m, v = (refs[n_p + 3 * i + j][...] for j in range(3))
            delta, m_new, v_new = _adam_update(g, w, m, v)
            for j, val in enumerate((g, delta, m_new, v_new)):
                refs[4 * n_p + 4 * i + j][...] = val

    vm = pl.BlockSpec(memory_space=pltpu.VMEM)
    out_shape = []
    for _, shape in PARAM_SHAPES:
        out_shape += [jax.ShapeDtypeStruct(shape, F32)] * 4
    outs = pl.pallas_call(
        body,
        name="adamw",
        out_shape=tuple(out_shape),
        in_specs=[vm] * (4 * n_p),
        out_specs=(vm,) * (4 * n_p),
        compiler_params=pltpu.CompilerParams(vmem_limit_bytes=VMEM_LIMIT),
    )(*[grads[n] for n in names], *[a for n in names for a in params[n]])
    return {n: outs[4 * i:4 * i + 4] for i, n in enumerate(names)}


N_CHIPS = 4


def _reduce_plan(pays, owns, r1s, sums, r2s, send1, recv1, send2, recv2, local_sems):
    x, y, c, _ = _device_position()
    sibling = (x, y, 1 - c)
    chips = [((1 - x if rj & 2 else x), (1 - y if rj & 1 else y)) for rj in range(N_CHIPS)]
    n = len(pays)

    def slot_of(rj, core):
        return 4 * chips[rj][0] + 2 * chips[rj][1] + core

    def to_sibling(a, rj):
        return pltpu.make_async_remote_copy(
            src_ref=pays[a].at[slot_of(rj, 1 - c)], dst_ref=r1s[a].at[rj],
            send_sem=send1.at[N_CHIPS * a + rj], recv_sem=recv1.at[N_CHIPS * a + rj],
            device_id=sibling, device_id_type=pl.DeviceIdType.MESH)

    def load_own(a, rj):
        return pltpu.make_async_copy(pays[a].at[slot_of(rj, c)], owns[a].at[rj], local_sems.at[2 * N_CHIPS * a + rj])

    def to_chip(a, rj):
        return pltpu.make_async_remote_copy(
            src_ref=sums[a].at[rj], dst_ref=r2s[a].at[rj],
            send_sem=send2.at[N_CHIPS * a + rj], recv_sem=recv2.at[N_CHIPS * a + rj],
            device_id=(*chips[rj], c), device_id_type=pl.DeviceIdType.MESH)

    def keep(a):
        return pltpu.make_async_copy(sums[a].at[0], r2s[a].at[0], local_sems.at[2 * N_CHIPS * a + N_CHIPS])

    def start():
        for a in range(n):
            for rj in range(N_CHIPS):
                to_sibling(a, rj).start()
                if owns[a] is not None:
                    load_own(a, rj).start()

    def combine():
        for a in range(n):
            for rj in range(N_CHIPS):
                to_sibling(a, rj).wait_recv()
                if owns[a] is not None:
                    load_own(a, rj).wait()
                    mine = owns[a][rj]
                else:
                    mine = pays[a][slot_of(rj, c)]
                sums[a][rj] = (mine.astype(F32) + r1s[a][rj].astype(F32)).astype(sums[a].dtype)
            keep(a).start()
            for rj in range(1, N_CHIPS):
                to_chip(a, rj).start()

    def finish():
        for a in range(n):
            for rj in range(1, N_CHIPS):
                to_chip(a, rj).wait_recv()
            for rj in range(N_CHIPS):
                to_sibling(a, rj).wait_send()
            for rj in range(1, N_CHIPS):
                to_chip(a, rj).wait_send()
            keep(a).wait()

    return start, combine, finish


def _reduce_scratch(shapes_dtypes, own_flags):
    out = []
    for (shape, dtype), own in zip(shapes_dtypes, own_flags):
        if own:
            out.append(pltpu.VMEM((N_CHIPS,) + shape, dtype))
        out += [pltpu.VMEM((N_CHIPS,) + shape, dtype), pltpu.VMEM((N_CHIPS,) + shape, dtype)]
    n = len(shapes_dtypes)
    out += [pltpu.SemaphoreType.DMA((N_CHIPS * n,))] * 4 + [pltpu.SemaphoreType.DMA((2 * N_CHIPS * n,))]
    return out


def _pack_small(ssmall, dwq, dwkv, dconv, dfinal, dgq, dgkv, dattn, dconvg, loss_part):
    ssmall[...] = jnp.zeros_like(ssmall)
    rep = ssmall.at[0]
    for i in range(D_MODEL // LANES):
        rep[ROW_FINAL + i:ROW_FINAL + i + 1, :] = dfinal[:, LANES * i:LANES * (i + 1)]
    for i in range(Q_RANK // LANES):
        rep[ROW_GQ + i:ROW_GQ + i + 1, :] = dgq[:, LANES * i:LANES * (i + 1)]
    rep[ROW_GKV:ROW_GKV + 1, :] = dgkv[...]
    for i in range(CONV_WIDTH // LANES):
        rep[ROW_ATTN + i:ROW_ATTN + i + 1, :] = dattn[:, LANES * i:LANES * (i + 1)]
        rep[ROW_CONVG + i:ROW_CONVG + i + 1, :] = dconvg[:, LANES * i:LANES * (i + 1)]
    rep[ROW_LOSS:ROW_LOSS + 1, :] = loss_part[...]
    for k in range(N_DEV):
        if k:
            ssmall[k, ROW_REPL:, :] = ssmall[0, ROW_REPL:, :]
        for s, e, d in _q_pieces(k):
            ssmall[k, ROW_Q:ROW_Q + Q_RANK, s:e] = dwq[:, d:d + e - s]
        ssmall[k, ROW_KV:ROW_KV + KV_RANK, :] = dwkv[:, _kv_dst(k):_kv_dst(k) + SHARD_KV]
        ssmall[k, ROW_CONV:ROW_CONV + 3, 0:SHARD_CONV] = dconv[0:3, SHARD_CONV * k:SHARD_CONV * (k + 1)]


TOKEN_TILES = 4
TAIL_ROWS = N_META + D_MODEL // LANES


def _reduce_tail(r_in_a, r_in_b, r_out, r_small, d_meta, d_norm):
    n_p = len(PARAM_SHAPES)
    names = [n for n, _ in PARAM_SHAPES]

    def body(*refs):
        rin_a, rin_b, rout, rsmall, dmeta, dnorm = refs[:6]
        g_out = {n: refs[6 + i] for i, n in enumerate(names)}
        loss_out = refs[6 + n_p]
        stail, rtail, gsum, gtail, send_sems, recv_sems = refs[7 + n_p:]
        half = rin_a.shape[-1]
        x, y, c, me = _device_position()
        my_chip = 2 * x + y

        for k in range(N_DEV):
            stail[k, 0:N_META, :] = dmeta[:, SHARD_META * k:SHARD_META * (k + 1)]
            for i in range(D_MODEL // LANES):
                stail[k, N_META + i:N_META + i + 1, :] = dnorm[:, LANES * i:LANES * (i + 1)]
        copies = []
        for r in range(1, N_DEV):
            peer = (1 - x if r & 4 else x, 1 - y if r & 2 else y, 1 - c if r & 1 else c)
            copies.append(pltpu.make_async_remote_copy(
                src_ref=stail.at[4 * peer[0] + 2 * peer[1] + peer[2]],
                dst_ref=rtail.at[r],
                send_sem=send_sems.at[r - 1],
                recv_sem=recv_sems.at[r - 1],
                device_id=peer,
                device_id_type=pl.DeviceIdType.MESH,
            ))
        for cp in copies:
            cp.start()
        rtail[0] = stail[me]

        for h, rin in enumerate((rin_a, rin_b)):
            g = rin[my_chip].astype(F32)
            for ch in range(1, N_CHIPS):
                g = g + rin[ch ^ my_chip].astype(F32)
            g_out["w_in"][:, h * half:(h + 1) * half] = g[:SHARD_IN, :]

        g = rout[my_chip].astype(F32)
        gs = rsmall[my_chip]
        for ch in range(1, N_CHIPS):
            g = g + rout[ch ^ my_chip].astype(F32)
            gs = gs + rsmall[ch ^ my_chip]
        g_out["w_out"][...] = g
        gsum[...] = gs
        g_out["w_q_up"][...] = gsum[ROW_Q:ROW_Q + Q_RANK, 0:SHARD_Q]
        g_out["w_kv_up"][...] = gsum[ROW_KV:ROW_KV + KV_RANK, :]
        g_out["conv_w"][...] = gsum[ROW_CONV:ROW_CONV + 3, 0:SHARD_CONV]
        for name, row, width in (("final_norm_g", ROW_FINAL, D_MODEL), ("q_norm_g", ROW_GQ, Q_RANK),
                                 ("kv_norm_g", ROW_GKV, KV_RANK), ("attn_out_g", ROW_ATTN, CONV_WIDTH),
                                 ("conv_out_g", ROW_CONVG, CONV_WIDTH)):
            for i in range(width // LANES):
                g_out[name][:, LANES * i:LANES * (i + 1)] = gsum[row + i:row + i + 1, :]
        loss_out[...] = gsum[ROW_LOSS:ROW_LOSS + 1, :]

        for cp in copies:
            cp.wait_recv()
        gt = rtail[me]
        for d in range(1, N_DEV):
            gt = gt + rtail[d ^ me]
        gtail[...] = gt
        g_out["meta_tokens"][...] = gtail[0:N_META, :]
        for i in range(D_MODEL // LANES):
            g_out["norm_g"][:, LANES * i:LANES * (i + 1)] = gtail[N_META + i:N_META + i + 1, :]
        for cp in copies:
            cp.wait_send()

    vm = pl.BlockSpec(memory_space=pltpu.VMEM)
    out_shape = [jax.ShapeDtypeStruct(shape, F32) for _, shape in PARAM_SHAPES]
    out_shape.append(jax.ShapeDtypeStruct((1, LANES), F32))
    outs = pl.pallas_call(
        body,
        name="reduce_tail",
        out_shape=tuple(out_shape),
        in_specs=[vm] * 6,
        out_specs=(vm,) * len(out_shape),
        scratch_shapes=[
            pltpu.VMEM((N_DEV, TAIL_ROWS, LANES), F32),
            pltpu.VMEM((N_DEV, TAIL_ROWS, LANES), F32),
            pltpu.VMEM((SMALL_ROWS, LANES), F32),
            pltpu.VMEM((TAIL_ROWS, LANES), F32),
            pltpu.SemaphoreType.DMA((N_DEV - 1,)),
            pltpu.SemaphoreType.DMA((N_DEV - 1,)),
        ],
        compiler_params=pltpu.CompilerParams(vmem_limit_bytes=VMEM_LIMIT),
    )(r_in_a, r_in_b, r_out, r_small, d_meta, d_norm)
    return {n: outs[i] for i, n in enumerate(names)}, outs[-1]


def _prep_gather(x, meta, norm_g, w_in_t):
    nb_seq, s, d = x.shape
    nb = s // LANES + 1
    forward_step = nb_seq * (nb // 3)
    finish_step = nb_seq * (nb - 1)

    def body(x_ref, meta_ref, g_ref, win_ref, u_ref, w_in_p, meta_f,
             sbig, ssmall, gbig, gsmall, send_sems, recv_sems, local_sems):
        jj, b = pl.program_id(0), pl.program_id(1)
        t = jj * nb_seq + b

        def plan():
            return _gather_plan((sbig, ssmall), (gbig, gsmall), send_sems, recv_sems, local_sems)

        @pl.when(t == 0)
        def _():
            sbig[0:SHARD_IN, :] = win_ref[...].astype(BF16)
            sbig[SHARD_IN:, :] = jnp.zeros((SHARD_IN_PAD - SHARD_IN, d), BF16)
            ssmall[...] = meta_ref[...]
            plan()[0]()

        @pl.when(t == forward_step)
        def _():
            plan()[1]()

        @pl.when(t == finish_step)
        def _():
            plan()[2]()
            w_in_p[N_A:GRP_A, :] = jnp.zeros((GRP_A - N_A, d), BF16)
            for k in range(N_DEV):
                for s0, e0, d0 in _in_pieces(k):
                    w_in_p[d0:d0 + e0 - s0, :] = gbig[k, s0:e0, :]
                meta_f[:, SHARD_META * k:SHARD_META * (k + 1)] = gsmall[k]

        def norm(h):
            hhat, _ = _rms_stats(h)
            return (hhat * g_ref[...]).astype(BF16)

        @pl.when(jj < nb - 1)
        def _():
            u_ref[...] = norm(x_ref[0])

        @pl.when(jj == nb - 1)
        def _():
            u_ref[0:PAD_FRONT, :] = jnp.zeros((PAD_FRONT, d), BF16)
            u_ref[PAD_FRONT:LANES, :] = norm(meta_f[...])

    whole = lambda shape: pl.BlockSpec(shape, lambda jj, b: (0,) * len(shape))
    return pl.pallas_call(
        body,
        name="prep_norm_gather",
        grid=(nb, nb_seq),
        in_specs=[
            pl.BlockSpec((1, LANES, d), lambda jj, b: (b, jnp.minimum(jj, nb - 2), 0)),
            whole(meta.shape), whole(norm_g.shape), whole(w_in_t.shape),
        ],
        out_specs=(pl.BlockSpec((LANES, d), lambda jj, b: (b * nb + (jj + 1) % nb, 0)),
                   whole((IN_PAD, d)), whole((N_META, d))),
        out_shape=(jax.ShapeDtypeStruct((nb_seq * nb * LANES, d), BF16),
                   jax.ShapeDtypeStruct((IN_PAD, d), BF16),
                   jax.ShapeDtypeStruct((N_META, d), F32)),
        scratch_shapes=[
            pltpu.VMEM((SHARD_IN_PAD, d), BF16),
            pltpu.VMEM((N_META, SHARD_META), F32),
            pltpu.VMEM((N_DEV, SHARD_IN_PAD, d), BF16),
            pltpu.VMEM((N_DEV, N_META, SHARD_META), F32),
            pltpu.SemaphoreType.DMA((14,)),
            pltpu.SemaphoreType.DMA((14,)),
            pltpu.SemaphoreType.DMA((2,)),
        ],
        compiler_params=_params("arbitrary", "arbitrary"),
    )(x, meta, norm_g, w_in_t)


def _in_proj_gather(u, w_in_p, w_q, w_kv, w_out, conv_w, bm, bn):
    m, k_dim = u.shape
    n = w_in_p.shape[0]
    steps = (m // bm) * (n // bn)
    forward_step = steps // 3
    qkv_rows = Q_RANK + KV_RANK

    def body(a_ref, b_ref, wq_ref, wkv_ref, wout_ref, conv_ref, o_ref, w_q_p, w_kv_p, w_out_f, conv_f,
             sqkv, sout, sconv, gqkv, gout, gconv, send_sems, recv_sems, local_sems):
        t = pl.program_id(0) * (n // bn) + pl.program_id(1)

        def plan():
            return _gather_plan((sqkv, sout, sconv), (gqkv, gout, gconv), send_sems, recv_sems, local_sems)

        @pl.when(t == 0)
        def _():
            sqkv[...] = jnp.zeros_like(sqkv)
            sqkv[0:Q_RANK, 0:SHARD_Q] = wq_ref[...].astype(BF16)
            sqkv[Q_RANK:, :] = wkv_ref[...].astype(BF16)
            sout[...] = wout_ref[...].astype(BF16)
            sconv[...] = jnp.zeros_like(sconv)
            sconv[0:3, 0:SHARD_CONV] = conv_ref[...]
            plan()[0]()

        @pl.when(t == forward_step)
        def _():
            plan()[1]()

        o_ref[...] = _dot(a_ref[...], b_ref[...], _NT).astype(o_ref.dtype)

        @pl.when(t == steps - 1)
        def _():
            plan()[2]()
            conv_f[...] = jnp.zeros_like(conv_f)
            for k in range(N_DEV):
                for s0, e0, d0 in _q_pieces(k):
                    w_q_p[:, d0:d0 + e0 - s0] = gqkv[k, 0:Q_RANK, s0:e0]
                w_kv_p[:, _kv_dst(k):_kv_dst(k) + SHARD_KV] = gqkv[k, Q_RANK:, :]
                w_out_f[SHARD_OUT * k:SHARD_OUT * (k + 1), :] = gout[k]
                conv_f[0:3, SHARD_CONV * k:SHARD_CONV * (k + 1)] = gconv[k, 0:3, 0:SHARD_CONV]

    whole = lambda shape: pl.BlockSpec(shape, lambda i, j: (0,) * len(shape))
    return pl.pallas_call(
        body,
        name="in_proj_gather",
        grid=(m // bm, n // bn),
        in_specs=[pl.BlockSpec((bm, k_dim), lambda i, j: (i, 0)), pl.BlockSpec((bn, k_dim), lambda i, j: (j, 0)),
                  whole(w_q.shape), whole(w_kv.shape), whole(w_out.shape), whole(conv_w.shape)],
        out_specs=(pl.BlockSpec((bm, bn), lambda i, j: (i, j)),
                   whole((Q_RANK, Q_COLS)), whole((KV_RANK, KV_COLS)), whole((D_MODEL, D_MODEL)),
                   whole((8, CONV_WIDTH))),
        out_shape=(jax.ShapeDtypeStruct((m, n), BF16),
                   jax.ShapeDtypeStruct((Q_RANK, Q_COLS), BF16),
                   jax.ShapeDtypeStruct((KV_RANK, KV_COLS), BF16),
                   jax.ShapeDtypeStruct((D_MODEL, D_MODEL), BF16),
                   jax.ShapeDtypeStruct((8, CONV_WIDTH), F32)),
        scratch_shapes=[
            pltpu.VMEM((qkv_rows, LANES), BF16),
            pltpu.VMEM((SHARD_OUT, D_MODEL), BF16),
            pltpu.VMEM((8, LANES), F32),
            pltpu.VMEM((N_DEV, qkv_rows, LANES), BF16),
            pltpu.VMEM((N_DEV, SHARD_OUT, D_MODEL), BF16),
            pltpu.VMEM((N_DEV, 8, LANES), F32),
            pltpu.SemaphoreType.DMA((21,)),
            pltpu.SemaphoreType.DMA((21,)),
            pltpu.SemaphoreType.DMA((3,)),
        ],
        compiler_params=_params("arbitrary", "arbitrary"),
    )(u, w_in_p, w_q, w_kv, w_out, conv_w)


def _rope_tables(tp):
    half = D_ROPE // 2
    inv_freq = 1.0 / (ROPE_THETA ** (jnp.arange(half, dtype=F32) / half))
    pos = (jnp.arange(tp) - PAD_FRONT).astype(F32)
    ang = pos[:, None] * inv_freq[None, :]
    cos = jnp.tile(jnp.cos(ang), (1, LANES // half))
    sin = jnp.tile(jnp.sin(ang), (1, LANES // half))
    first = (jnp.arange(LANES) % D_ROPE) < half
    return cos, jnp.where(first, -sin, 0.0), jnp.where(first, 0.0, sin)


def _rope(t, cos, sa, sb):
    return t * cos + pltpu.roll(t, LANES - D_ROPE // 2, 1) * sa + pltpu.roll(t, D_ROPE // 2, 1) * sb


def _rope_t(t, cos, sa, sb):
    return t * cos + pltpu.roll(t * sa, D_ROPE // 2, 1) + pltpu.roll(t * sb, LANES - D_ROPE // 2, 1)


def _qkv_fwd(p, wq, wkv, gq, gkv, tables, nb_seq, tp):
    ht = tp // 2

    def body(pa_ref, wq_ref, wkv_ref, gq_ref, gkv_ref, cos_ref, sa_ref, sb_ref, q_ref, k_ref, v_ref):
        pa = pa_ref[...].astype(F32)
        cq_hat, _ = _rms_stats(pa[:, :Q_RANK])
        ckv_hat, _ = _rms_stats(pa[:, Q_RANK:Q_RANK + KV_RANK])
        q = _dot((cq_hat * gq_ref[...]).astype(BF16), wq_ref[...])
        kv = _dot((ckv_hat * gkv_ref[...]).astype(BF16), wkv_ref[...])
        tabs = (cos_ref[...], sa_ref[...], sb_ref[...])
        lane = lax.broadcasted_iota(jnp.int32, (ht, LANES), 1)
        low = lane < D_ROPE
        mark = lane == D_ROPE
        row = (pl.program_id(0) % 2) * ht + lax.broadcasted_iota(jnp.int32, (ht, LANES), 0)
        k_pe = jnp.where(mark & (row < PAD_FRONT), NEG_INF, _rope(pa[:, Q_RANK + KV_RANK:], *tabs))
        one = jnp.where(mark & (row >= PAD_FRONT), 1.0, 0.0)
        pairs = [_rope(q[:, N_HEADS * D_NOPE + LANES * i:N_HEADS * D_NOPE + LANES * (i + 1)], *tabs) for i in range(2)]
        for h in range(N_HEADS):
            pair = pairs[h // 2]
            if h % 2:
                pair = pltpu.roll(pair, D_ROPE, 1)
            pe = jnp.where(low, pair, one)
            q_ref[0, h] = jnp.concatenate([q[:, D_NOPE * h:D_NOPE * (h + 1)], pe], axis=1).astype(BF16)
            k_ref[0, h] = jnp.concatenate([kv[:, D_NOPE * h:D_NOPE * (h + 1)], k_pe], axis=1).astype(BF16)
            v_ref[0, h] = kv[:, N_HEADS * D_NOPE + D_V * h:N_HEADS * D_NOPE + D_V * (h + 1)].astype(BF16)

    full = lambda a: pl.BlockSpec(a.shape, lambda i: (0,) * a.ndim)
    tab = pl.BlockSpec((ht, LANES), lambda i: (i % 2, 0))
    qk = pl.BlockSpec((1, N_HEADS, ht, 2 * LANES), lambda i: (i // 2, 0, i % 2, 0))
    return pl.pallas_call(
        body,
        name="qkv_fwd",
        grid=(2 * nb_seq,),
        in_specs=[pl.BlockSpec((ht, GRP_A), lambda i: (i, 0)), full(wq), full(wkv), full(gq), full(gkv), tab, tab, tab],
        out_specs=(qk, qk, pl.BlockSpec((1, N_HEADS, ht, D_V), lambda i: (i // 2, 0, i % 2, 0))),
        out_shape=(
            jax.ShapeDtypeStruct((nb_seq, N_HEADS, tp, 2 * LANES), BF16),
            jax.ShapeDtypeStruct((nb_seq, N_HEADS, tp, 2 * LANES), BF16),
            jax.ShapeDtypeStruct((nb_seq, N_HEADS, tp, D_V), BF16),
        ),
        compiler_params=_params("parallel"),
    )(p, wq, wkv, gq, gkv, *tables)


def _attn_fwd(q, k, v, p, g_attn):
    nb_seq, _, tp, _ = q.shape

    def body(q_ref, k_ref, v_ref, z_ref, g_ref, y_ref, o_ref, lse_ref):
        g = g_ref[...]
        for r0 in range(0, tp, KV_TILE):
            nq = min(KV_TILE, tp - r0)
            kend = r0 + nq
            qq = q_ref[0, 0, r0:kend, :]
            sd = _dot(qq, k_ref[0, 0, r0:kend, :], _NT) * ATTN_SCALE
            causal = (lax.broadcasted_iota(jnp.int32, (nq, nq), 1) <= lax.broadcasted_iota(jnp.int32, (nq, nq), 0))
            sd = jnp.where(causal, sd, NEG_INF)
            m = jnp.max(sd, axis=-1, keepdims=True)
            if r0:
                so = _dot(qq, k_ref[0, 0, 0:r0, :], _NT) * ATTN_SCALE
                m = jnp.maximum(m, jnp.max(so, axis=-1, keepdims=True))
            ed = jnp.exp(sd - m)
            l = jnp.sum(ed, axis=-1, keepdims=True)
            o = _dot(ed.astype(BF16), v_ref[0, 0, r0:kend, :])
            if r0:
                eo = jnp.exp(so - m)
                l = l + jnp.sum(eo, axis=-1, keepdims=True)
                o = o + _dot(eo.astype(BF16), v_ref[0, 0, 0:r0, :])
            o = o * (1.0 / l)
            o_ref[0, 0, r0:kend, :] = o
            lse_ref[0, 0, r0:kend, :] = jnp.broadcast_to(m + jnp.log(l), (nq, LANES))
            ohat, _ = _rms_stats(o)
            z = z_ref[r0:kend, :].astype(F32)
            y_ref[r0:kend, :] = (ohat * g * (z * _sigmoid(z))).astype(BF16)

    qk = pl.BlockSpec((1, 1, tp, 2 * LANES), lambda b, h: (b, h, 0, 0))
    hv = pl.BlockSpec((1, 1, tp, D_V), lambda b, h: (b, h, 0, 0))
    return pl.pallas_call(
        body,
        name="attn_fwd",
        grid=(nb_seq, N_HEADS),
        in_specs=[qk, qk, hv,
                  pl.BlockSpec((tp, LANES), lambda b, h: (b, GRP_A // LANES + h)),
                  pl.BlockSpec((1, LANES), lambda b, h: (0, h))],
        out_specs=(pl.BlockSpec((tp, LANES), lambda b, h: (b, h)), hv, hv),
        out_shape=(
            jax.ShapeDtypeStruct((nb_seq * tp, N_HEADS * D_V), BF16),
            jax.ShapeDtypeStruct((nb_seq, N_HEADS, tp, D_V), F32),
            jax.ShapeDtypeStruct((nb_seq, N_HEADS, tp, LANES), F32),
        ),
        compiler_params=_params("parallel", "parallel"),
    )(q, k, v, p, g_attn)


_CONV_COL0 = (GRP_A + N_HEADS * D_V) // LANES


def _conv_specs(tp, order):
    cols = CONV_WIDTH // LANES
    return [pl.BlockSpec((tp, LANES), functools.partial(
        lambda a, b, off: order(a, b, off), off=_CONV_COL0 + i * cols)) for i in range(4)]


def _conv_fwd(p, conv_w, g_conv, nb_seq, tp):
    def body(b_ref, c_ref, h_ref, z_ref, w_ref, g_ref, y_ref):
        cc = c_ref[...].astype(F32) * h_ref[...].astype(F32)
        row = lax.broadcasted_iota(jnp.int32, (tp, LANES), 0)
        s1 = jnp.where(row >= 1, pltpu.roll(cc, 1, 0), 0.0)
        s2 = jnp.where(row >= 2, pltpu.roll(cc, 2, 0), 0.0)
        yc = b_ref[...].astype(F32) * (w_ref[0:1, :] * s2 + w_ref[1:2, :] * s1 + w_ref[2:3, :] * cc)
        r = lax.rsqrt(_group_mean(yc * yc) + EPS)
        z = z_ref[...].astype(F32)
        y_ref[...] = (yc * r * g_ref[...] * (z * _sigmoid(z))).astype(BF16)

    return pl.pallas_call(
        body,
        name="conv_fwd",
        grid=(nb_seq, CONV_WIDTH // LANES),
        in_specs=_conv_specs(tp, lambda b, t, off: (b, off + t)) + [
            pl.BlockSpec((8, LANES), lambda b, t: (0, t)),
            pl.BlockSpec((1, LANES), lambda b, t: (0, t))],
        out_specs=pl.BlockSpec((tp, LANES), lambda b, t: (b, t)),
        out_shape=jax.ShapeDtypeStruct((nb_seq * tp, CONV_WIDTH), BF16),
        compiler_params=_params("parallel", "parallel"),
    )(p, p, p, p, conv_w, g_conv)


def _token_copy(hbm, b, k, ts, buf, sem, to_hbm=False):
    lo, hi = max(k * ts - LANES, 0), (k + 1) * ts - LANES
    off = lo - (k * ts - LANES)
    src, dst = hbm.at[b, pl.ds(lo, hi - lo)], buf.at[pl.ds(off, hi - lo)]
    if to_hbm:
        src, dst = dst, src
    return pltpu.make_async_copy(src, dst, sem)


def _for_tile(k, nt, fn):
    for kk in range(nt):
        @pl.when(k == kk)
        def _(kk=kk):
            fn(kk)


def _out_proj_loss(ya, yc, w_out, x, target, g_final, nt):
    nb_seq, s, d = x.shape
    r, ka = ya.shape
    ts = (s + LANES) // nt
    steps = nb_seq * nt

    def body(a_ref, c_ref, w_ref, x_hbm, t_hbm, g_ref, dh_ref, dhb_ref, dg_ref, loss_ref,
             xbuf, tbuf, acc_ref, sems):
        i = pl.program_id(0)
        b, k = i // nt, i % nt

        @pl.when(i == 0)
        def _():
            acc_ref[...] = jnp.zeros_like(acc_ref)
            dg_ref[...] = jnp.zeros_like(dg_ref)

        slot = i % 2

        def fetch(seq, kk, sl):
            return [_token_copy(x_hbm, seq, kk, ts, xbuf.at[sl], sems.at[sl, 0]),
                    _token_copy(t_hbm, seq, kk, ts, tbuf.at[sl], sems.at[sl, 1])]

        def start(seq, sl, kk):
            if kk == 0:
                xbuf[sl, 0:LANES, :] = jnp.zeros((LANES, d), F32)
                tbuf[sl, 0:LANES, :] = jnp.zeros((LANES, d), F32)
            for cp in fetch(seq, kk, sl):
                cp.start()

        @pl.when(i == 0)
        def _():
            start(0, 0, 0)

        @pl.when(i + 1 < steps)
        def _():
            _for_tile((i + 1) % nt, nt, functools.partial(start, (i + 1) // nt, 1 - slot))

        mix = _dot(a_ref[...], w_ref[0:ka, :]) + _dot(c_ref[...], w_ref[ka:, :])
        _for_tile(k, nt, lambda kk: [cp.wait() for cp in fetch(b, kk, slot)])

        real = (lax.broadcasted_iota(jnp.int32, (ts, d), 0) >= LANES) | (k > 0)
        g = g_ref[...]
        hhat, rstd = _rms_stats(xbuf[slot] + mix)
        e = jnp.where(real, hhat * g - tbuf[slot], 0.0)
        acc_ref[...] += jnp.sum(e * e, axis=0, keepdims=True)
        dy = e * (1.0 / d)
        dg_ref[...] += jnp.sum(dy * hhat, axis=0, keepdims=True)
        dh = _rms_bwd(g * dy, hhat, rstd)
        dh_ref[...] = dh
        dhb_ref[...] = dh.astype(BF16)

        @pl.when(i == steps - 1)
        def _():
            total = jnp.sum(acc_ref[...], axis=1, keepdims=True)
            loss_ref[...] = jnp.broadcast_to((0.5 / d) * total, loss_ref.shape)

    hbm = pl.BlockSpec(memory_space=pl.ANY)
    row = pl.BlockSpec((ts, d), lambda i: (i, 0))
    vec = pl.BlockSpec((1, d), lambda i: (0, 0))
    return pl.pallas_call(
        body,
        name="out_proj_loss",
        grid=(steps,),
        in_specs=[pl.BlockSpec((ts, ka), lambda i: (i, 0)), pl.BlockSpec((ts, yc.shape[1]), lambda i: (i, 0)),
                  pl.BlockSpec(w_out.shape, lambda i: (0, 0)), hbm, hbm, vec],
        out_specs=(row, row, vec, pl.BlockSpec((1, LANES), lambda i: (0, 0))),
        out_shape=(
            jax.ShapeDtypeStruct((r, d), F32),
            jax.ShapeDtypeStruct((r, d), BF16),
            jax.ShapeDtypeStruct((1, d), F32),
            jax.ShapeDtypeStruct((1, LANES), F32),
        ),
        scratch_shapes=[pltpu.VMEM((2, ts, d), F32), pltpu.VMEM((2, ts, d), F32), pltpu.VMEM((1, d), F32),
                        pltpu.SemaphoreType.DMA((2, 2))],
        compiler_params=_params("arbitrary"),
    )(ya, yc, w_out, x, target, g_final)


def _out_proj_bwd(dhb, w_out, ya, yc, bm):
    r, d = dhb.shape
    ka = ya.shape[1]
    n_mix = w_out.shape[0]
    last = r // bm - 1

    def body(dh_ref, w_ref, a_ref, c_ref, dcat_ref, dw_ref, acc_ref):
        @pl.when(pl.program_id(0) == 0)
        def _():
            acc_ref[...] = jnp.zeros_like(acc_ref)

        dh = dh_ref[...]
        dcat_ref[...] = _dot(dh, w_ref[...], _NT).astype(BF16)
        acc_ref[0:ka, :] += _dot(a_ref[...], dh, _TN)
        acc_ref[ka:, :] += _dot(c_ref[...], dh, _TN)

        @pl.when(pl.program_id(0) == last)
        def _():
            dw_ref[...] = acc_ref[...].astype(BF16)

    return pl.pallas_call(
        body,
        name="out_proj_bwd",
        grid=(r // bm,),
        in_specs=[pl.BlockSpec((bm, d), lambda i: (i, 0)), pl.BlockSpec(w_out.shape, lambda i: (0, 0)),
                  pl.BlockSpec((bm, ka), lambda i: (i, 0)), pl.BlockSpec((bm, yc.shape[1]), lambda i: (i, 0))],
        out_specs=(pl.BlockSpec((bm, n_mix), lambda i: (i, 0)),
                   pl.BlockSpec((n_mix, d), lambda i: (0, 0))),
        out_shape=(jax.ShapeDtypeStruct((r, n_mix), BF16),
                   jax.ShapeDtypeStruct((n_mix, d), BF16)),
        scratch_shapes=[pltpu.VMEM((n_mix, d), F32)],
        compiler_params=_params("arbitrary"),
    )(dhb, w_out, ya, yc)


def _attn_bwd(q, k, v, o, lse, dcat, p, g_attn, send_out):
    nb_seq, _, tp, _ = q.shape
    steps = N_HEADS * nb_seq

    def body(q_ref, k_ref, v_ref, o_ref, lse_ref, dy_ref, z_ref, g_ref, pay_ref,
             dq_ref, dk_ref, dv_ref, dz_ref, dg_ref, r2_ref, dq_acc, r1, sums, *sems):
        t = pl.program_id(0) * nb_seq + pl.program_id(1)

        def plan():
            return _reduce_plan((pay_ref,), (None,), (r1,), (sums,), (r2_ref,), *sems)

        @pl.when(t == 0)
        def _():
            plan()[0]()

        @pl.when(t == 1)
        def _():
            plan()[1]()

        @pl.when(pl.program_id(1) == 0)
        def _():
            dg_ref[...] = jnp.zeros_like(dg_ref)

        g = g_ref[...]
        z = z_ref[...].astype(F32)
        o = o_ref[0, 0]
        dy = dy_ref[...].astype(F32)
        sig = _sigmoid(z)
        ohat, r = _rms_stats(o)
        don = dy * (z * sig)
        dz_ref[...] = (dy * (ohat * g) * (sig * (1.0 + z * (1.0 - sig)))).astype(BF16)
        dg_ref[...] += jnp.sum(don * ohat, axis=0, keepdims=True)
        do = _rms_bwd(g * don, ohat, r)
        dvec = jnp.sum(do * o, axis=-1, keepdims=True)
        dob = do.astype(BF16)
        lse_col = lse_ref[0, 0, :, 0:1]
        dq_acc[...] = jnp.zeros_like(dq_acc)
        for k0 in range(0, tp, KV_TILE):
            nk = min(KV_TILE, tp - k0)
            nq = tp - k0
            qq = q_ref[0, 0, k0:, :]
            kk = k_ref[0, 0, k0:k0 + nk, :]
            causal = (lax.broadcasted_iota(jnp.int32, (nq, nk), 1) <= lax.broadcasted_iota(jnp.int32, (nq, nk), 0))
            pr = jnp.where(causal, jnp.exp(_dot(qq, kk, _NT) * ATTN_SCALE - lse_col[k0:]), 0.0)
            dp = _dot(dob[k0:], v_ref[0, 0, k0:k0 + nk, :], _NT)
            ds = (pr * (dp - dvec[k0:]) * ATTN_SCALE).astype(BF16)
            dv_ref[0, 0, k0:k0 + nk, :] = _dot(pr.astype(BF16), dob[k0:], _TN).astype(BF16)
            dk_ref[0, 0, k0:k0 + nk, :] = _dot(ds, qq, _TN).astype(BF16)
            dq_acc[k0:, :] += _dot(ds, kk)
        dq_ref[0, 0] = dq_acc[...].astype(BF16)

        @pl.when(t == steps - 1)
        def _():
            plan()[2]()

    qk = pl.BlockSpec((1, 1, tp, 2 * LANES), lambda h, b: (b, h, 0, 0))
    hv = pl.BlockSpec((1, 1, tp, D_V), lambda h, b: (b, h, 0, 0))
    col = pl.BlockSpec((tp, LANES), lambda h, b: (b, h))
    slot = send_out.shape[1:]
    return pl.pallas_call(
        body,
        name="attn_bwd",
        grid=(N_HEADS, nb_seq),
        in_specs=[qk, qk, hv, hv, hv, col,
                  pl.BlockSpec((tp, LANES), lambda h, b: (b, GRP_A // LANES + h)),
                  pl.BlockSpec((1, LANES), lambda h, b: (0, h)),
                  pl.BlockSpec(send_out.shape, lambda h, b: (0, 0, 0))],
        out_specs=(qk, qk, hv, col, pl.BlockSpec((1, LANES), lambda h, b: (0, h)),
                   pl.BlockSpec(memory_space=pl.ANY)),
        out_shape=(
            jax.ShapeDtypeStruct((nb_seq, N_HEADS, tp, 2 * LANES), BF16),
            jax.ShapeDtypeStruct((nb_seq, N_HEADS, tp, 2 * LANES), BF16),
            jax.ShapeDtypeStruct((nb_seq, N_HEADS, tp, D_V), BF16),
            jax.ShapeDtypeStruct((nb_seq * tp, N_HEADS * D_V), BF16),
            jax.ShapeDtypeStruct((1, N_HEADS * D_V), F32),
            jax.ShapeDtypeStruct((N_CHIPS,) + slot, BF16),
        ),
        scratch_shapes=[pltpu.VMEM((tp, 2 * LANES), F32)] + _reduce_scratch([(slot, BF16)], [False]),
        compiler_params=_params("arbitrary", "arbitrary"),
    )(q, k, v, o, lse, dcat, p, g_attn, send_out)


def _qkv_bwd(p, dq, dk, dv, wq, wkv, gq, gkv, tables):
    nb_seq, _, tp, _ = dq.shape
    ht = tp // 2

    def body(pa_ref, dq_ref, dk_ref, dv_ref, wq_ref, wkv_ref, gq_ref, gkv_ref, cos_ref, sa_ref, sb_ref,
             dpa_ref, dwq_ref, dwkv_ref, dgq_ref, dgkv_ref):
        @pl.when(pl.program_id(0) == 0)
        def _():
            dwq_ref[...] = jnp.zeros_like(dwq_ref)
            dwkv_ref[...] = jnp.zeros_like(dwkv_ref)
            dgq_ref[...] = jnp.zeros_like(dgq_ref)
            dgkv_ref[...] = jnp.zeros_like(dgkv_ref)

        pa = pa_ref[...].astype(F32)
        gq, gkv = gq_ref[...], gkv_ref[...]
        cq_hat, rq = _rms_stats(pa[:, :Q_RANK])
        ckv_hat, rkv = _rms_stats(pa[:, Q_RANK:Q_RANK + KV_RANK])
        tabs = (cos_ref[...], sa_ref[...], sb_ref[...])

        pe = [dq_ref[0, h, :, D_NOPE:].astype(F32) for h in range(N_HEADS)]
        pairs = [_rope_t(pe[2 * i] + pltpu.roll(pe[2 * i + 1], D_ROPE, 1), *tabs).astype(BF16) for i in range(2)]
        dq_flat = jnp.concatenate([dq_ref[0, h, :, :D_NOPE] for h in range(N_HEADS)] + pairs, axis=1)
        dwq_ref[...] += _dot((cq_hat * gq).astype(BF16), dq_flat, _TN)
        dcqn = _dot(dq_flat, wq_ref[...], _NT)
        dgq_ref[...] += jnp.sum(dcqn * cq_hat, axis=0, keepdims=True)
        dcq = _rms_bwd(gq * dcqn, cq_hat, rq)

        dkv_flat = jnp.concatenate([dk_ref[0, h, :, :D_NOPE] for h in range(N_HEADS)]
                                   + [dv_ref[0, h] for h in range(N_HEADS)], axis=1)
        dwkv_ref[...] += _dot((ckv_hat * gkv).astype(BF16), dkv_flat, _TN)
        dckvn = _dot(dkv_flat, wkv_ref[...], _NT)
        dgkv_ref[...] += jnp.sum(dckvn * ckv_hat, axis=0, keepdims=True)
        dckv = _rms_bwd(gkv * dckvn, ckv_hat, rkv)

        dk_pe = dk_ref[0, 0, :, D_NOPE:].astype(F32)
        for h in range(1, N_HEADS):
            dk_pe = dk_pe + dk_ref[0, h, :, D_NOPE:].astype(F32)
        dk_pe = jnp.where(lax.broadcasted_iota(jnp.int32, (ht, LANES), 1) < D_ROPE, dk_pe, 0.0)
        dpa_ref[...] = jnp.concatenate([dcq, dckv, _rope_t(dk_pe, *tabs)], axis=1).astype(BF16)

    full = lambda a: pl.BlockSpec(a.shape, lambda i: (0,) * a.ndim)
    tab = pl.BlockSpec((ht, LANES), lambda i: (i % 2, 0))
    qk = pl.BlockSpec((1, N_HEADS, ht, 2 * LANES), lambda i: (i // 2, 0, i % 2, 0))
    acc = lambda shape: pl.BlockSpec(shape, lambda i: (0, 0))
    return pl.pallas_call(
        body,
        name="qkv_bwd",
        grid=(2 * nb_seq,),
        in_specs=[pl.BlockSpec((ht, GRP_A), lambda i: (i, 0)), qk, qk,
                  pl.BlockSpec((1, N_HEADS, ht, D_V), lambda i: (i // 2, 0, i % 2, 0)),
                  full(wq), full(wkv), full(gq), full(gkv), tab, tab, tab],
        out_specs=(pl.BlockSpec((ht, GRP_A), lambda i: (i, 0)),
                   acc(wq.shape), acc(wkv.shape), acc((1, Q_RANK)), acc((1, KV_RANK))),
        out_shape=(
            jax.ShapeDtypeStruct((nb_seq * tp, GRP_A), BF16),
            jax.ShapeDtypeStruct(wq.shape, F32),
            jax.ShapeDtypeStruct(wkv.shape, F32),
            jax.ShapeDtypeStruct((1, Q_RANK), F32),
            jax.ShapeDtypeStruct((1, KV_RANK), F32),
        ),
        compiler_params=_params("arbitrary"),
    )(p, dq, dk, dv, wq, wkv, gq, gkv, *tables)


def _conv_bwd(p, dcat, conv_w, g_conv, nb_seq, tp):
    cols = CONV_WIDTH // LANES

    def body(b_ref, c_ref, h_ref, z_ref, dy_ref, w_ref, g_ref,
             db_ref, dc_ref, dh_ref, dz_ref, dw_ref, dg_ref):
        @pl.when(pl.program_id(1) == 0)
        def _():
            dw_ref[...] = jnp.zeros_like(dw_ref)
            dg_ref[...] = jnp.zeros_like(dg_ref)

        cb, c, h = b_ref[...].astype(F32), c_ref[...].astype(F32), h_ref[...].astype(F32)
        z, dy = z_ref[...].astype(F32), dy_ref[...].astype(F32)
        g = g_ref[...]
        w0, w1, w2 = w_ref[0:1, :], w_ref[1:2, :], w_ref[2:3, :]
        cc = c * h
        row = lax.broadcasted_iota(jnp.int32, (tp, LANES), 0)
        s1 = jnp.where(row >= 1, pltpu.roll(cc, 1, 0), 0.0)
        s2 = jnp.where(row >= 2, pltpu.roll(cc, 2, 0), 0.0)
        dwc = w0 * s2 + w1 * s1 + w2 * cc
        yc = cb * dwc
        r = lax.rsqrt(_group_mean(yc * yc) + EPS)
        ychat = yc * r
        sig = _sigmoid(z)
        dz_ref[...] = (dy * (ychat * g) * (sig * (1.0 + z * (1.0 - sig)))).astype(BF16)
        dyn = dy * (z * sig)
        dg_ref[...] += jnp.sum(dyn * ychat, axis=0, keepdims=True)
        gd = g * dyn
        dyc = r * (gd - ychat * _group_mean(gd * ychat))
        db_ref[...] = (dyc * dwc).astype(BF16)
        ddw = dyc * cb
        dw_ref[0:1, :] += jnp.sum(ddw * s2, axis=0, keepdims=True)
        dw_ref[1:2, :] += jnp.sum(ddw * s1, axis=0, keepdims=True)
        dw_ref[2:3, :] += jnp.sum(ddw * cc, axis=0, keepdims=True)
        u1 = jnp.where(row <= tp - 2, pltpu.roll(ddw, tp - 1, 0), 0.0)
        u2 = jnp.where(row <= tp - 3, pltpu.roll(ddw, tp - 2, 0), 0.0)
        dcc = w2 * ddw + w1 * u1 + w0 * u2
        dc_ref[...] = (dcc * h).astype(BF16)
        dh_ref[...] = (dcc * c).astype(BF16)

    col = pl.BlockSpec((tp, LANES), lambda t, b: (b, t))
    out = jax.ShapeDtypeStruct((nb_seq * tp, CONV_WIDTH), BF16)
    return pl.pallas_call(
        body,
        name="conv_bwd",
        grid=(cols, nb_seq),
        in_specs=_conv_specs(tp, lambda t, b, off: (b, off + t)) + [
            pl.BlockSpec((tp, LANES), lambda t, b: (b, N_HEADS * D_V // LANES + t)),
            pl.BlockSpec((8, LANES), lambda t, b: (0, t)),
            pl.BlockSpec((1, LANES), lambda t, b: (0, t))],
        out_specs=(col, col, col, col,
                   pl.BlockSpec((8, LANES), lambda t, b: (0, t)), pl.BlockSpec((1, LANES), lambda t, b: (0, t))),
        out_shape=(out, out, out, out,
                   jax.ShapeDtypeStruct((8, CONV_WIDTH), F32), jax.ShapeDtypeStruct((1, CONV_WIDTH), F32)),
        compiler_params=_params("arbitrary", "arbitrary"),
    )(p, p, p, p, dcat, conv_w, g_conv)


def _input_bwd(dps, w_in, x, meta, dh, norm_g, nt, send_in):
    nb_seq, s, d = x.shape
    r, kb = dps[0].shape
    ts = (s + LANES) // nt
    steps = nb_seq * nt
    n_dp = len(dps)
    in_slot = send_in.shape[1:]

    def body(*refs):
        dp_refs, w_ref, x_hbm, meta_ref, dh_ref, g_ref, pay_ref = refs[:n_dp], *refs[n_dp:n_dp + 6]
        o = n_dp + 6
        gx_hbm, dmeta_ref, dg_ref, r2_in = refs[o:o + 4]
        xbuf, gxbuf, tok_sems, own_in, r1_in, sum_in = refs[o + 4:o + 10]
        sems = refs[o + 10:]
        i = pl.program_id(0)
        b, k = i // nt, i % nt

        def plan():
            return _reduce_plan((pay_ref,), (own_in,), (r1_in,), (sum_in,), (r2_in,), *sems)

        @pl.when(i == 0)
        def _():
            dmeta_ref[...] = jnp.zeros_like(dmeta_ref)
            dg_ref[...] = jnp.zeros_like(dg_ref)
            plan()[0]()

        @pl.when(i == 1)
        def _():
            plan()[1]()

        def start(kk):
            if kk == 0:
                xbuf[0:PAD_FRONT, :] = jnp.zeros((PAD_FRONT, d), F32)
                xbuf[PAD_FRONT:LANES, :] = meta_ref[...]
            _token_copy(x_hbm, b, kk, ts, xbuf, tok_sems.at[0]).start()

        _for_tile(k, nt, start)
        du = _dot(dp_refs[0][...], w_ref[0:kb, :])
        for j in range(1, n_dp):
            du = du + _dot(dp_refs[j][...], w_ref[kb * j:kb * (j + 1), :])
        _for_tile(k, nt, lambda kk: _token_copy(x_hbm, b, kk, ts, xbuf, tok_sems.at[0]).wait())

        g = g_ref[...]
        hhat, rstd = _rms_stats(xbuf[...])
        dg_ref[...] += jnp.sum(du * hhat, axis=0, keepdims=True)
        res = _rms_bwd(g * du, hhat, rstd) + dh_ref[...]

        @pl.when(i > 0)
        def _():
            _for_tile(k, nt, lambda kk: _token_copy(gx_hbm, b, (kk - 1) % nt, ts, gxbuf, tok_sems.at[1], True).wait())

        gxbuf[...] = res

        @pl.when(k == 0)
        def _():
            dmeta_ref[...] += gxbuf[PAD_FRONT:LANES, :]

        _for_tile(k, nt, lambda kk: _token_copy(gx_hbm, b, kk, ts, gxbuf, tok_sems.at[1], True).start())

        @pl.when(i == steps - 1)
        def _():
            _token_copy(gx_hbm, b, nt - 1, ts, gxbuf, tok_sems.at[1], True).wait()
            plan()[2]()

    whole = lambda a: pl.BlockSpec(a.shape, lambda i: (0,) * a.ndim)
    hbm = pl.BlockSpec(memory_space=pl.ANY)
    return pl.pallas_call(
        body,
        name="input_bwd",
        grid=(steps,),
        in_specs=[pl.BlockSpec((ts, kb), lambda i: (i, 0)) for _ in dps]
        + [whole(w_in), hbm, whole(meta), pl.BlockSpec((ts, d), lambda i: (i, 0)), whole(norm_g), hbm],
        out_specs=(hbm, pl.BlockSpec((N_META, d), lambda i: (0, 0)), pl.BlockSpec((1, d), lambda i: (0, 0)), hbm),
        out_shape=(jax.ShapeDtypeStruct((nb_seq, s, d), F32),
                   jax.ShapeDtypeStruct((N_META, d), F32),
                   jax.ShapeDtypeStruct((1, d), F32),
                   jax.ShapeDtypeStruct((N_CHIPS,) + in_slot, BF16)),
        scratch_shapes=[pltpu.VMEM((ts, d), F32), pltpu.VMEM((ts, d), F32), pltpu.SemaphoreType.DMA((2,))]
        + _reduce_scratch([(in_slot, BF16)], [True]),
        compiler_params=_params("arbitrary"),
    )(*dps, w_in, x, meta, dh, norm_g, send_in)


def _in_proj_bwd_w(u, dps, bm, small_grads):
    r, d = u.shape
    kb = dps[0].shape[1]
    steps = r // bm
    half = d // 2
    n_dp, n_small = len(dps), len(small_grads)
    slot, small_slot = (SHARD_IN_PAD, half), (SMALL_ROWS, LANES)

    def body(*refs):
        u_ref, dp_refs = refs[0], refs[1:1 + n_dp]
        small_refs = refs[1 + n_dp:1 + n_dp + n_small]
        o = 1 + n_dp + n_small
        pay_b, r2_a, r2_small = refs[o:o + 3]
        acc_ref, pay_a, ssmall, r1_a, sum_a, r1_small, sum_small = refs[o + 3:o + 10]
        sems = refs[o + 10:]
        h, i = pl.program_id(0), pl.program_id(1)

        def plan():
            return _reduce_plan((pay_a, ssmall), (None, None), (r1_a, r1_small), (sum_a, sum_small),
                                (r2_a, r2_small), *sems)

        @pl.when(i == 0)
        def _():
            acc_ref[...] = jnp.zeros_like(acc_ref)

        @pl.when((h == 1) & (i == 0))
        def _():
            _pack_small(ssmall, *small_refs)
            plan()[0]()

        @pl.when((h == 1) & (i == 1))
        def _():
            plan()[1]()

        uu = u_ref[...]
        for j in range(n_dp):
            acc_ref[kb * j:kb * (j + 1), :] += _dot(dp_refs[j][...], uu, _TN)

        def payload(dst):
            for k in range(N_DEV):
                for s, e, c0 in _in_pieces(k):
                    dst[k, s:e, :] = acc_ref[c0:c0 + e - s, :].astype(BF16)
                dst[k, SHARD_IN:, :] = jnp.zeros((SHARD_IN_PAD - SHARD_IN, half), BF16)

        @pl.when((h == 0) & (i == steps - 1))
        def _():
            payload(pay_a)

        @pl.when((h == 1) & (i == steps - 1))
        def _():
            payload(pay_b)
            plan()[2]()

    whole = lambda a: pl.BlockSpec(a.shape, lambda h, i: (0,) * a.ndim)
    hbm = pl.BlockSpec(memory_space=pl.ANY)
    return pl.pallas_call(
        body,
        name="in_proj_bwd_w",
        grid=(2, steps),
        in_specs=[pl.BlockSpec((bm, half), lambda h, i: (i, h))]
        + [pl.BlockSpec((bm, kb), lambda h, i: (i, 0)) for _ in dps] + [whole(a) for a in small_grads],
        out_specs=(pl.BlockSpec((N_DEV,) + slot, lambda h, i: (0, 0, 0)), hbm, hbm),
        out_shape=(jax.ShapeDtypeStruct((N_DEV,) + slot, BF16),
                   jax.ShapeDtypeStruct((N_CHIPS,) + slot, BF16),
                   jax.ShapeDtypeStruct((N_CHIPS,) + small_slot, F32)),
        scratch_shapes=[pltpu.VMEM((kb * n_dp, half), F32), pltpu.VMEM((N_DEV,) + slot, BF16),
                        pltpu.VMEM((N_DEV,) + small_slot, F32)]
        + _reduce_scratch([(slot, BF16), (small_slot, F32)], [False, False]),
        compiler_params=_params("arbitrary", "arbitrary"),
    )(u, *dps, *small_grads)


def _local_step(x, loss_target, u, p, meta_f, norm_g, w_in_p, q_norm_g, w_q_p, kv_norm_g, w_kv_p, conv_w_f,
                attn_out_g, conv_out_g, w_out_f, g_final):
    nb_seq, s, d = x.shape
    tp = s + LANES
    ht = tp // 2
    tables = _rope_tables(tp)

    q, k, v = _qkv_fwd(p, w_q_p, w_kv_p, q_norm_g, kv_norm_g, tables, nb_seq, tp)
    ya, o, lse = _attn_fwd(q, k, v, p, attn_out_g)
    yc = _conv_fwd(p, conv_w_f, conv_out_g, nb_seq, tp)
    dh, dhb, d_final_g, loss_part = _out_proj_loss(ya, yc, w_out_f, x, loss_target, g_final, TOKEN_TILES)

    dcat, d_w_out = _out_proj_bwd(dhb, w_out_f, ya, yc, ht)
    send_out = d_w_out.reshape(N_DEV, SHARD_OUT, d)
    dq, dk, dv, dz_attn, d_attn_g, r_out = _attn_bwd(q, k, v, o, lse, dcat, p, attn_out_g, send_out)
    dpa, d_wq_p, d_wkv_p, d_gq, d_gkv = _qkv_bwd(p, dq, dk, dv, w_q_p, w_kv_p, q_norm_g, kv_norm_g, tables)
    d_b, d_c, d_h, dz_conv, d_conv_w, d_conv_g = _conv_bwd(p, dcat, conv_w_f, conv_out_g, nb_seq, tp)
    dps = (dpa, dz_attn, d_b, d_c, d_h, dz_conv)
    small = (d_wq_p, d_wkv_p, d_conv_w, d_final_g, d_gq, d_gkv, d_attn_g, d_conv_g, loss_part)
    send_in_b, r_in_a, r_small = _in_proj_bwd_w(u, dps, ht // 2, small)
    grad_x, d_meta, d_norm_g, r_in_b = _input_bwd(dps, w_in_p, x, meta_f, dh, norm_g, TOKEN_TILES, send_in_b)
    return grad_x, r_in_a, r_in_b, r_out, r_small, d_meta, d_norm_g


def kernel(x, meta_tokens, norm_g, w_in, q_norm_g, w_q_up, kv_norm_g, w_kv_up, conv_w, attn_out_g, conv_out_g, w_out, final_norm_g, loss_target, m_meta_tokens, m_norm_g, m_w_in, m_q_norm_g, m_w_q_up, m_kv_norm_g, m_w_kv_up, m_conv_w, m_attn_out_g, m_conv_out_g, m_w_out, m_final_norm_g, v_meta_tokens, v_norm_g, v_w_in, v_q_norm_g, v_w_q_up, v_kv_norm_g, v_w_kv_up, v_conv_w, v_attn_out_g, v_conv_out_g, v_w_out, v_final_norm_g):
    d = x.shape[-1]
    ht = (x.shape[1] + LANES) // 2
    u, w_in_p, meta_f = _prep_gather(x, meta_tokens, norm_g, w_in[0].T)
    p, w_q_p, w_kv_p, w_out_f, conv_w_f = _in_proj_gather(
        u, w_in_p, w_q_up[0], w_kv_up[0], w_out[0], conv_w[0], ht, GRP_A)
    g_final = final_norm_g.reshape(1, d)
    grad_x, r_in_a, r_in_b, r_out, r_small, d_meta, d_norm_g = _local_step(
        x, loss_target, u, p, meta_f, norm_g, w_in_p, q_norm_g, w_q_p, kv_norm_g, w_kv_p, conv_w_f,
        attn_out_g, conv_out_g, w_out_f, g_final)

    flat = lambda a: a.reshape(a.shape[-2:]) if a.ndim == 3 else a.reshape(1, -1) if a.ndim == 1 else a
    transposed = ("w_in",)
    to_kernel = lambda n, a: flat(a).T if n in transposed else flat(a)
    from_kernel = lambda n, a, shape: (a.T if n in transposed else a).reshape(shape)
    params = {
        "meta_tokens": (meta_tokens, m_meta_tokens, v_meta_tokens),
        "norm_g": (norm_g, m_norm_g, v_norm_g),
        "w_in": (w_in, m_w_in, v_w_in),
        "q_norm_g": (q_norm_g, m_q_norm_g, v_q_norm_g),
        "w_q_up": (w_q_up, m_w_q_up, v_w_q_up),
        "kv_norm_g": (kv_norm_g, m_kv_norm_g, v_kv_norm_g),
        "w_kv_up": (w_kv_up, m_w_kv_up, v_w_kv_up),
        "conv_w": (conv_w, m_conv_w, v_conv_w),
        "attn_out_g": (attn_out_g, m_attn_out_g, v_attn_out_g),
        "conv_out_g": (conv_out_g, m_conv_out_g, v_conv_out_g),
        "w_out": (w_out, m_w_out, v_w_out),
        "final_norm_g": (final_norm_g, m_final_norm_g, v_final_norm_g),
    }
    grads, loss = _reduce_tail(r_in_a, r_in_b, r_out, r_small, d_meta, d_norm_g)
    updated = _adamw(grads, {n: tuple(to_kernel(n, a) for a in t) for n, t in params.items()})
    outs = [[from_kernel(n, updated[n][i], params[n][0].shape) for n, _ in PARAM_SHAPES] for i in range(4)]
    return (loss[0, 0], grad_x, *outs[0], *outs[1], *outs[2], *outs[3])
```

```python
import functools

import jax
import jax.numpy as jnp
from jax import lax
from jax.experimental import pallas as pl
from jax.experimental.pallas import tpu as pltpu

F32 = jnp.float32
BF16 = jnp.bfloat16

N_META = 16
D_MODEL = 1024
N_HEADS = 4
D_NOPE = 128
D_ROPE = 64
D_V = 128
Q_RANK = 256
KV_RANK = 128
CONV_WIDTH = 512
CONV_GROUP = 64
ROPE_THETA = 10000.0
ATTN_SCALE = (D_NOPE + D_ROPE) ** -0.5
EPS = 1e-6
NEG_INF = -1e30

ADAM_LR = 0.001
ADAM_B1 = 0.9
ADAM_B2 = 0.999
ADAM_EPS = 1e-08
ADAM_WD = 0.01
ADAM_STEP = 10

LANES = 128
PAD_FRONT = LANES - N_META
KV_TILE = 256
N_DEV = 8
VMEM_LIMIT = 56 * 1024 * 1024

IN_PAD = 3072
GRP_A = 512
N_A = Q_RANK + KV_RANK + D_ROPE
IN_PROJ = 3008
SHARD_IN = IN_PROJ // N_DEV
SHARD_IN_PAD = 384
SHARD_Q = 96
SHARD_KV = 128
SHARD_OUT = 128
SHARD_CONV = 64
SHARD_META = 128
Q_COLS = N_HEADS * (D_NOPE + D_ROPE)
KV_COLS = N_HEADS * (D_NOPE + D_V)

ROW_Q, ROW_KV, ROW_META, ROW_CONV = 0, 256, 384, 400
ROW_REPL = 408
ROW_NORM, ROW_FINAL, ROW_GQ, ROW_GKV, ROW_ATTN, ROW_CONVG, ROW_LOSS = 408, 416, 424, 426, 427, 431, 435
SMALL_ROWS = 440

PARAM_SHAPES = (
    ("meta_tokens", (N_META, SHARD_META)), ("norm_g", (1, D_MODEL)), ("w_in", (SHARD_IN, D_MODEL)),
    ("q_norm_g", (1, Q_RANK)), ("w_q_up", (Q_RANK, SHARD_Q)), ("kv_norm_g", (1, KV_RANK)),
    ("w_kv_up", (KV_RANK, SHARD_KV)), ("conv_w", (3, SHARD_CONV)), ("attn_out_g", (1, CONV_WIDTH)),
    ("conv_out_g", (1, CONV_WIDTH)), ("w_out", (SHARD_OUT, D_MODEL)), ("final_norm_g", (1, D_MODEL)),
)


def _in_pieces(k):
    lo, hi = SHARD_IN * k, SHARD_IN * (k + 1)
    out = []
    if lo < N_A:
        out.append((0, min(hi, N_A) - lo, lo))
    if hi > N_A:
        s = max(lo, N_A)
        out.append((s - lo, hi - lo, s + GRP_A - N_A))
    return out


def _q_pieces(k):
    lo, hi = SHARD_Q * k, SHARD_Q * (k + 1)
    out = []
    for h in range(N_HEADS):
        base = (D_NOPE + D_ROPE) * h
        s, e = max(lo, base), min(hi, base + D_NOPE)
        if s < e:
            out.append((s - lo, e - lo, D_NOPE * h + s - base))
        s, e = max(lo, base + D_NOPE), min(hi, base + D_NOPE + D_ROPE)
        if s < e:
            out.append((s - lo, e - lo, N_HEADS * D_NOPE + D_ROPE * h + s - base - D_NOPE))
    return out


def _kv_dst(k):
    return D_NOPE * (k // 2) + (N_HEADS * D_NOPE if k % 2 else 0)


def _params(*sem):
    return pltpu.CompilerParams(dimension_semantics=sem, vmem_limit_bytes=VMEM_LIMIT)


def _rms_stats(x):
    r = lax.rsqrt(jnp.mean(x * x, axis=-1, keepdims=True) + EPS)
    return x * r, r


def _rms_bwd(gdy, xhat, r):
    return r * (gdy - xhat * jnp.mean(gdy * xhat, axis=-1, keepdims=True))


def _sigmoid(z):
    return 1.0 / (1.0 + jnp.exp(-z))


def _group_mean(x):
    i0 = lax.broadcasted_iota(jnp.int32, (LANES, LANES), 0) // CONV_GROUP
    i1 = lax.broadcasted_iota(jnp.int32, (LANES, LANES), 1) // CONV_GROUP
    m = jnp.where(i0 == i1, 1.0 / CONV_GROUP, 0.0).astype(BF16)
    hi = x.astype(BF16)
    lo = (x - hi.astype(F32)).astype(BF16)
    return jnp.dot(hi, m, preferred_element_type=F32) + jnp.dot(lo, m, preferred_element_type=F32)


_NT = (((1,), (1,)), ((), ()))
_TN = (((0,), (0,)), ((), ()))


def _dot(a, b, dims=None):
    if dims is None:
        return jnp.dot(a, b, preferred_element_type=F32)
    return lax.dot_general(a, b, dims, preferred_element_type=F32)


def _device_position():
    x, y, c = lax.axis_index("x"), lax.axis_index("y"), lax.axis_index("c")
    return x, y, c, 4 * x + 2 * y + c


def _gather_plan(srcs, slots, send_sems, recv_sems, local_sems):
    x, y, c, _ = _device_position()
    me, sibling = (x, y, c), (x, y, 1 - c)
    chips = [(1 - x, y), (x, 1 - y), (1 - x, 1 - y)]
    n = len(srcs)

    def slot(a, px, py, pc):
        return slots[a].at[4 * px + 2 * py + pc]

    def copy(a, k, block, to, own=False):
        return pltpu.make_async_remote_copy(
            src_ref=srcs[a] if own else slot(a, *block),
            dst_ref=slot(a, *block),
            send_sem=send_sems.at[7 * a + k],
            recv_sem=recv_sems.at[7 * a + k],
            device_id=to,
            device_id_type=pl.DeviceIdType.MESH,
        )

    def local(a):
        return pltpu.make_async_copy(srcs[a], slot(a, *me), local_sems.at[a])

    def firsts(a):
        return [copy(a, 0, me, sibling, own=True)] + [
            copy(a, 1 + j, me, (*chip, c), own=True) for j, chip in enumerate(chips)]

    def start():
        for a in range(n):
            local(a).start()
            for cp in firsts(a):
                cp.start()

    def forward():
        for j, chip in enumerate(chips):
            for a in range(n):
                copy(a, 1 + j, (*chip, c), me).wait_recv()
                copy(a, 4 + j, (*chip, c), sibling).start()

    def finish():
        for a in range(n):
            copy(a, 0, sibling, me).wait_recv()
            for j, chip in enumerate(chips):
                copy(a, 4 + j, (*chip, 1 - c), me).wait_recv()
        for a in range(n):
            for cp in firsts(a) + [copy(a, 4 + j, (*chip, c), sibling) for j, chip in enumerate(chips)]:
                cp.wait_send()
            local(a).wait()

    return start, forward, finish


def _adam_update(g, w, m, v):
    m_new = ADAM_B1 * m + (1.0 - ADAM_B1) * g
    v_new = ADAM_B2 * v + (1.0 - ADAM_B2) * (g * g)
    m_hat = m_new / (1.0 - ADAM_B1 ** ADAM_STEP)
    v_hat = v_new / (1.0 - ADAM_B2 ** ADAM_STEP)
    return -ADAM_LR * (m_hat / (jnp.sqrt(v_hat) + ADAM_EPS) + ADAM_WD * w), m_new, v_new


def _adamw(grads, params):
    names = [n for n, _ in PARAM_SHAPES]
    n_p = len(names)

    def body(*refs):
        for i in range(n_p):
            g = refs[i][...]
            w, m, v = (refs[n_p + 3 * i + j][...] for j in range(3))
            delta, m_new, v_new = _adam_update(g, w, m, v)
            for j, val in enumerate((g, delta, m_new, v_new)):
                refs[4 * n_p + 4 * i + j][...] = val

    vm = pl.BlockSpec(memory_space=pltpu.VMEM)
    out_shape = []
    for _, shape in PARAM_SHAPES:
        out_shape += [jax.ShapeDtypeStruct(shape, F32)] * 4
    outs = pl.pallas_call(
        body,
        name="adamw",
        out_shape=tuple(out_shape),
        in_specs=[vm] * (4 * n_p),
        out_specs=(vm,) * (4 * n_p),
        compiler_params=pltpu.CompilerParams(vmem_limit_bytes=VMEM_LIMIT),
    )(*[grads[n] for n in names], *[a for n in names for a in params[n]])
    return {n: outs[4 * i:4 * i + 4] for i, n in enumerate(names)}


N_CHIPS = 4


def _reduce_plan(pays, owns, r1s, sums, r2s, send1, recv1, send2, recv2, local_sems):
    x, y, c, _ = _device_position()
    sibling = (x, y, 1 - c)
    chips = [((1 - x if rj & 2 else x), (1 - y if rj & 1 else y)) for rj in range(N_CHIPS)]
    n = len(pays)

    def slot_of(rj, core):
        return 4 * chips[rj][0] + 2 * chips[rj][1] + core

    def to_sibling(a, rj):
        return pltpu.make_async_remote_copy(
            src_ref=pays[a].at[slot_of(rj, 1 - c)], dst_ref=r1s[a].at[rj],
            send_sem=send1.at[N_CHIPS * a + rj], recv_sem=recv1.at[N_CHIPS * a + rj],
            device_id=sibling, device_id_type=pl.DeviceIdType.MESH)

    def load_own(a, rj):
        return pltpu.make_async_copy(pays[a].at[slot_of(rj, c)], owns[a].at[rj], local_sems.at[2 * N_CHIPS * a + rj])

    def to_chip(a, rj):
        return pltpu.make_async_remote_copy(
            src_ref=sums[a].at[rj], dst_ref=r2s[a].at[rj],
            send_sem=send2.at[N_CHIPS * a + rj], recv_sem=recv2.at[N_CHIPS * a + rj],
            device_id=(*chips[rj], c), device_id_type=pl.DeviceIdType.MESH)

    def keep(a):
        return pltpu.make_async_copy(sums[a].at[0], r2s[a].at[0], local_sems.at[2 * N_CHIPS * a + N_CHIPS])

    def start():
        for a in range(n):
            for rj in range(N_CHIPS):
                to_sibling(a, rj).start()
                if owns[a] is not None:
                    load_own(a, rj).start()

    def combine():
        for a in range(n):
            for rj in range(N_CHIPS):
                to_sibling(a, rj).wait_recv()
                if owns[a] is not None:
                    load_own(a, rj).wait()
                    mine = owns[a][rj]
                else:
                    mine = pays[a][slot_of(rj, c)]
                sums[a][rj] = (mine.astype(F32) + r1s[a][rj].astype(F32)).astype(sums[a].dtype)
            keep(a).start()
            for rj in range(1, N_CHIPS):
                to_chip(a, rj).start()

    def finish():
        for a in range(n):
            for rj in range(1, N_CHIPS):
                to_chip(a, rj).wait_recv()
            for rj in range(N_CHIPS):
                to_sibling(a, rj).wait_send()
            for rj in range(1, N_CHIPS):
                to_chip(a, rj).wait_send()
            keep(a).wait()

    return start, combine, finish


def _reduce_scratch(shapes_dtypes, own_flags):
    out = []
    for (shape, dtype), own in zip(shapes_dtypes, own_flags):
        if own:
            out.append(pltpu.VMEM((N_CHIPS,) + shape, dtype))
        out += [pltpu.VMEM((N_CHIPS,) + shape, dtype), pltpu.VMEM((N_CHIPS,) + shape, dtype)]
    n = len(shapes_dtypes)
    out += [pltpu.SemaphoreType.DMA((N_CHIPS * n,))] * 4 + [pltpu.SemaphoreType.DMA((2 * N_CHIPS * n,))]
    return out


def _pack_small(ssmall, dwq, dwkv, dconv, dfinal, dgq, dgkv, dattn, dconvg, loss_part):
    ssmall[...] = jnp.zeros_like(ssmall)
    rep = ssmall.at[0]
    for i in range(D_MODEL // LANES):
        rep[ROW_FINAL + i:ROW_FINAL + i + 1, :] = dfinal[:, LANES * i:LANES * (i + 1)]
    for i in range(Q_RANK // LANES):
        rep[ROW_GQ + i:ROW_GQ + i + 1, :] = dgq[:, LANES * i:LANES * (i + 1)]
    rep[ROW_GKV:ROW_GKV + 1, :] = dgkv[...]
    for i in range(CONV_WIDTH // LANES):
        rep[ROW_ATTN + i:ROW_ATTN + i + 1, :] = dattn[:, LANES * i:LANES * (i + 1)]
        rep[ROW_CONVG + i:ROW_CONVG + i + 1, :] = dconvg[:, LANES * i:LANES * (i + 1)]
    rep[ROW_LOSS:ROW_LOSS + 1, :] = loss_part[...]
    for k in range(N_DEV):
        if k:
            ssmall[k, ROW_REPL:, :] = ssmall[0, ROW_REPL:, :]
        for s, e, d in _q_pieces(k):
            ssmall[k, ROW_Q:ROW_Q + Q_RANK, s:e] = dwq[:, d:d + e - s]
        ssmall[k, ROW_KV:ROW_KV + KV_RANK, :] = dwkv[:, _kv_dst(k):_kv_dst(k) + SHARD_KV]
        ssmall[k, ROW_CONV:ROW_CONV + 3, 0:SHARD_CONV] = dconv[0:3, SHARD_CONV * k:SHARD_CONV * (k + 1)]


TOKEN_TILES = 4
TAIL_ROWS = N_META + D_MODEL // LANES


def _reduce_tail(r_in, r_out, r_small, d_meta, d_norm):
    n_p = len(PARAM_SHAPES)
    names = [n for n, _ in PARAM_SHAPES]

    def body(*refs):
        rin, rout, rsmall, dmeta, dnorm = refs[:5]
        g_out = {n: refs[5 + i] for i, n in enumerate(names)}
        loss_out = refs[5 + n_p]
        stail, rtail, gsum, gtail, send_sems, recv_sems = refs[6 + n_p:]
        x, y, c, me = _device_position()
        my_chip = 2 * x + y

        for k in range(N_DEV):
            stail[k, 0:N_META, :] = dmeta[:, SHARD_META * k:SHARD_META * (k + 1)]
            for i in range(D_MODEL // LANES):
                stail[k, N_META + i:N_META + i + 1, :] = dnorm[:, LANES * i:LANES * (i + 1)]
        copies = []
        for r in range(1, N_DEV):
            peer = (1 - x if r & 4 else x, 1 - y if r & 2 else y, 1 - c if r & 1 else c)
            copies.append(pltpu.make_async_remote_copy(
                src_ref=stail.at[4 * peer[0] + 2 * peer[1] + peer[2]],
                dst_ref=rtail.at[r],
                send_sem=send_sems.at[r - 1],
                recv_sem=recv_sems.at[r - 1],
                device_id=peer,
                device_id_type=pl.DeviceIdType.MESH,
            ))
        for cp in copies:
            cp.start()
        rtail[0] = stail[me]

        g = rin[my_chip].astype(F32)
        for ch in range(1, N_CHIPS):
            g = g + rin[ch ^ my_chip].astype(F32)
        g_out["w_in"][...] = g[:SHARD_IN, :]

        g = rout[my_chip].astype(F32)
        gs = rsmall[my_chip]
        for ch in range(1, N_CHIPS):
            g = g + rout[ch ^ my_chip].astype(F32)
            gs = gs + rsmall[ch ^ my_chip]
        g_out["w_out"][...] = g
        gsum[...] = gs
        g_out["w_q_up"][...] = gsum[ROW_Q:ROW_Q + Q_RANK, 0:SHARD_Q]
        g_out["w_kv_up"][...] = gsum[ROW_KV:ROW_KV + KV_RANK, :]
        g_out["conv_w"][...] = gsum[ROW_CONV:ROW_CONV + 3, 0:SHARD_CONV]
        for name, row, width in (("final_norm_g", ROW_FINAL, D_MODEL), ("q_norm_g", ROW_GQ, Q_RANK),
                                 ("kv_norm_g", ROW_GKV, KV_RANK), ("attn_out_g", ROW_ATTN, CONV_WIDTH),
                                 ("conv_out_g", ROW_CONVG, CONV_WIDTH)):
            for i in range(width // LANES):
                g_out[name][:, LANES * i:LANES * (i + 1)] = gsum[row + i:row + i + 1, :]
        loss_out[...] = gsum[ROW_LOSS:ROW_LOSS + 1, :]

        for cp in copies:
            cp.wait_recv()
        gt = rtail[me]
        for d in range(1, N_DEV):
            gt = gt + rtail[d ^ me]
        gtail[...] = gt
        g_out["meta_tokens"][...] = gtail[0:N_META, :]
        for i in range(D_MODEL // LANES):
            g_out["norm_g"][:, LANES * i:LANES * (i + 1)] = gtail[N_META + i:N_META + i + 1, :]
        for cp in copies:
            cp.wait_send()

    vm = pl.BlockSpec(memory_space=pltpu.VMEM)
    out_shape = [jax.ShapeDtypeStruct(shape, F32) for _, shape in PARAM_SHAPES]
    out_shape.append(jax.ShapeDtypeStruct((1, LANES), F32))
    outs = pl.pallas_call(
        body,
        name="reduce_tail",
        out_shape=tuple(out_shape),
        in_specs=[vm] * 5,
        out_specs=(vm,) * len(out_shape),
        scratch_shapes=[
            pltpu.VMEM((N_DEV, TAIL_ROWS, LANES), F32),
            pltpu.VMEM((N_DEV, TAIL_ROWS, LANES), F32),
            pltpu.VMEM((SMALL_ROWS, LANES), F32),
            pltpu.VMEM((TAIL_ROWS, LANES), F32),
            pltpu.SemaphoreType.DMA((N_DEV - 1,)),
            pltpu.SemaphoreType.DMA((N_DEV - 1,)),
        ],
        compiler_params=pltpu.CompilerParams(vmem_limit_bytes=VMEM_LIMIT),
    )(r_in, r_out, r_small, d_meta, d_norm)
    return {n: outs[i] for i, n in enumerate(names)}, outs[-1]


def _prep_gather(x, meta, norm_g, w_in_t):
    nb_seq, s, d = x.shape
    nb = s // LANES + 1
    forward_step = nb_seq * (nb // 3)
    finish_step = nb_seq * (nb - 1)

    def body(x_ref, meta_ref, g_ref, win_ref, u_ref, w_in_p, meta_f,
             sbig, ssmall, gbig, gsmall, send_sems, recv_sems, local_sems):
        jj, b = pl.program_id(0), pl.program_id(1)
        t = jj * nb_seq + b

        def plan():
            return _gather_plan((sbig, ssmall), (gbig, gsmall), send_sems, recv_sems, local_sems)

        @pl.when(t == 0)
        def _():
            sbig[0:SHARD_IN, :] = win_ref[...].astype(BF16)
            sbig[SHARD_IN:, :] = jnp.zeros((SHARD_IN_PAD - SHARD_IN, d), BF16)
            ssmall[...] = meta_ref[...]
            plan()[0]()

        @pl.when(t == forward_step)
        def _():
            plan()[1]()

        @pl.when(t == finish_step)
        def _():
            plan()[2]()
            w_in_p[N_A:GRP_A, :] = jnp.zeros((GRP_A - N_A, d), BF16)
            for k in range(N_DEV):
                for s0, e0, d0 in _in_pieces(k):
                    w_in_p[d0:d0 + e0 - s0, :] = gbig[k, s0:e0, :]
                meta_f[:, SHARD_META * k:SHARD_META * (k + 1)] = gsmall[k]

        def norm(h):
            hhat, _ = _rms_stats(h)
            return (hhat * g_ref[...]).astype(BF16)

        @pl.when(jj < nb - 1)
        def _():
            u_ref[...] = norm(x_ref[0])

        @pl.when(jj == nb - 1)
        def _():
            u_ref[0:PAD_FRONT, :] = jnp.zeros((PAD_FRONT, d), BF16)
            u_ref[PAD_FRONT:LANES, :] = norm(meta_f[...])

    whole = lambda shape: pl.BlockSpec(shape, lambda jj, b: (0,) * len(shape))
    return pl.pallas_call(
        body,
        name="prep_norm_gather",
        grid=(nb, nb_seq),
        in_specs=[
            pl.BlockSpec((1, LANES, d), lambda jj, b: (b, jnp.minimum(jj, nb - 2), 0)),
            whole(meta.shape), whole(norm_g.shape), whole(w_in_t.shape),
        ],
        out_specs=(pl.BlockSpec((LANES, d), lambda jj, b: (b * nb + (jj + 1) % nb, 0)),
                   whole((IN_PAD, d)), whole((N_META, d))),
        out_shape=(jax.ShapeDtypeStruct((nb_seq * nb * LANES, d), BF16),
                   jax.ShapeDtypeStruct((IN_PAD, d), BF16),
                   jax.ShapeDtypeStruct((N_META, d), F32)),
        scratch_shapes=[
            pltpu.VMEM((SHARD_IN_PAD, d), BF16),
            pltpu.VMEM((N_META, SHARD_META), F32),
            pltpu.VMEM((N_DEV, SHARD_IN_PAD, d), BF16),
            pltpu.VMEM((N_DEV, N_META, SHARD_META), F32),
            pltpu.SemaphoreType.DMA((14,)),
            pltpu.SemaphoreType.DMA((14,)),
            pltpu.SemaphoreType.DMA((2,)),
        ],
        compiler_params=_params("arbitrary", "arbitrary"),
    )(x, meta, norm_g, w_in_t)


def _in_proj_gather(u, w_in_p, w_q, w_kv, w_out, conv_w, bm, bn):
    m, k_dim = u.shape
    n = w_in_p.shape[0]
    steps = (m // bm) * (n // bn)
    forward_step = steps // 3
    qkv_rows = Q_RANK + KV_RANK

    def body(a_ref, b_ref, wq_ref, wkv_ref, wout_ref, conv_ref, o_ref, w_q_p, w_kv_p, w_out_f, conv_f,
             sqkv, sout, sconv, gqkv, gout, gconv, send_sems, recv_sems, local_sems):
        t = pl.program_id(0) * (n // bn) + pl.program_id(1)

        def plan():
            return _gather_plan((sqkv, sout, sconv), (gqkv, gout, gconv), send_sems, recv_sems, local_sems)

        @pl.when(t == 0)
        def _():
            sqkv[...] = jnp.zeros_like(sqkv)
            sqkv[0:Q_RANK, 0:SHARD_Q] = wq_ref[...].astype(BF16)
            sqkv[Q_RANK:, :] = wkv_ref[...].astype(BF16)
            sout[...] = wout_ref[...].astype(BF16)
            sconv[...] = jnp.zeros_like(sconv)
            sconv[0:3, 0:SHARD_CONV] = conv_ref[...]
            plan()[0]()

        @pl.when(t == forward_step)
        def _():
            plan()[1]()

        o_ref[...] = _dot(a_ref[...], b_ref[...], _NT).astype(o_ref.dtype)

        @pl.when(t == steps - 1)
        def _():
            plan()[2]()
            conv_f[...] = jnp.zeros_like(conv_f)
            for k in range(N_DEV):
                for s0, e0, d0 in _q_pieces(k):
                    w_q_p[:, d0:d0 + e0 - s0] = gqkv[k, 0:Q_RANK, s0:e0]
                w_kv_p[:, _kv_dst(k):_kv_dst(k) + SHARD_KV] = gqkv[k, Q_RANK:, :]
                w_out_f[SHARD_OUT * k:SHARD_OUT * (k + 1), :] = gout[k]
                conv_f[0:3, SHARD_CONV * k:SHARD_CONV * (k + 1)] = gconv[k, 0:3, 0:SHARD_CONV]

    whole = lambda shape: pl.BlockSpec(shape, lambda i, j: (0,) * len(shape))
    return pl.pallas_call(
        body,
        name="in_proj_gather",
        grid=(m // bm, n // bn),
        in_specs=[pl.BlockSpec((bm, k_dim), lambda i, j: (i, 0)), pl.BlockSpec((bn, k_dim), lambda i, j: (j, 0)),
                  whole(w_q.shape), whole(w_kv.shape), whole(w_out.shape), whole(conv_w.shape)],
        out_specs=(pl.BlockSpec((bm, bn), lambda i, j: (i, j)),
                   whole((Q_RANK, Q_COLS)), whole((KV_RANK, KV_COLS)), whole((D_MODEL, D_MODEL)),
                   whole((8, CONV_WIDTH))),
        out_shape=(jax.ShapeDtypeStruct((m, n), BF16),
                   jax.ShapeDtypeStruct((Q_RANK, Q_COLS), BF16),
                   jax.ShapeDtypeStruct((KV_RANK, KV_COLS), BF16),
                   jax.ShapeDtypeStruct((D_MODEL, D_MODEL), BF16),
                   jax.ShapeDtypeStruct((8, CONV_WIDTH), F32)),
        scratch_shapes=[
            pltpu.VMEM((qkv_rows, LANES), BF16),
            pltpu.VMEM((SHARD_OUT, D_MODEL), BF16),
            pltpu.VMEM((8, LANES), F32),
            pltpu.VMEM((N_DEV, qkv_rows, LANES), BF16),
            pltpu.VMEM((N_DEV, SHARD_OUT, D_MODEL), BF16),
            pltpu.VMEM((N_DEV, 8, LANES), F32),
            pltpu.SemaphoreType.DMA((21,)),
            pltpu.SemaphoreType.DMA((21,)),
            pltpu.SemaphoreType.DMA((3,)),
        ],
        compiler_params=_params("arbitrary", "arbitrary"),
    )(u, w_in_p, w_q, w_kv, w_out, conv_w)


def _rope_tables(tp):
    half = D_ROPE // 2
    inv_freq = 1.0 / (ROPE_THETA ** (jnp.arange(half, dtype=F32) / half))
    pos = (jnp.arange(tp) - PAD_FRONT).astype(F32)
    ang = pos[:, None] * inv_freq[None, :]
    cos = jnp.tile(jnp.cos(ang), (1, LANES // half))
    sin = jnp.tile(jnp.sin(ang), (1, LANES // half))
    first = (jnp.arange(LANES) % D_ROPE) < half
    return cos, jnp.where(first, -sin, 0.0), jnp.where(first, 0.0, sin)


def _rope(t, cos, sa, sb):
    return t * cos + pltpu.roll(t, LANES - D_ROPE // 2, 1) * sa + pltpu.roll(t, D_ROPE // 2, 1) * sb


def _rope_t(t, cos, sa, sb):
    return t * cos + pltpu.roll(t * sa, D_ROPE // 2, 1) + pltpu.roll(t * sb, LANES - D_ROPE // 2, 1)


def _qkv_fwd(p, wq, wkv, gq, gkv, tables, nb_seq, tp):
    ht = tp // 2

    def body(pa_ref, wq_ref, wkv_ref, gq_ref, gkv_ref, cos_ref, sa_ref, sb_ref, q_ref, k_ref, v_ref):
        pa = pa_ref[...].astype(F32)
        cq_hat, _ = _rms_stats(pa[:, :Q_RANK])
        ckv_hat, _ = _rms_stats(pa[:, Q_RANK:Q_RANK + KV_RANK])
        q = _dot((cq_hat * gq_ref[...]).astype(BF16), wq_ref[...])
        kv = _dot((ckv_hat * gkv_ref[...]).astype(BF16), wkv_ref[...])
        tabs = (cos_ref[...], sa_ref[...], sb_ref[...])
        lane = lax.broadcasted_iota(jnp.int32, (ht, LANES), 1)
        low = lane < D_ROPE
        mark = lane == D_ROPE
        row = (pl.program_id(0) % 2) * ht + lax.broadcasted_iota(jnp.int32, (ht, LANES), 0)
        k_pe = jnp.where(mark & (row < PAD_FRONT), NEG_INF, _rope(pa[:, Q_RANK + KV_RANK:], *tabs))
        one = jnp.where(mark & (row >= PAD_FRONT), 1.0, 0.0)
        pairs = [_rope(q[:, N_HEADS * D_NOPE + LANES * i:N_HEADS * D_NOPE + LANES * (i + 1)], *tabs) for i in range(2)]
        for h in range(N_HEADS):
            pair = pairs[h // 2]
            if h % 2:
                pair = pltpu.roll(pair, D_ROPE, 1)
            pe = jnp.where(low, pair, one)
            q_ref[0, h] = jnp.concatenate([q[:, D_NOPE * h:D_NOPE * (h + 1)], pe], axis=1).astype(BF16)
            k_ref[0, h] = jnp.concatenate([kv[:, D_NOPE * h:D_NOPE * (h + 1)], k_pe], axis=1).astype(BF16)
            v_ref[0, h] = kv[:, N_HEADS * D_NOPE + D_V * h:N_HEADS * D_NOPE + D_V * (h + 1)].astype(BF16)

    full = lambda a: pl.BlockSpec(a.shape, lambda i: (0,) * a.ndim)
    tab = pl.BlockSpec((ht, LANES), lambda i: (i % 2, 0))
    qk = pl.BlockSpec((1, N_HEADS, ht, 2 * LANES), lambda i: (i // 2, 0, i % 2, 0))
    return pl.pallas_call(
        body,
        name="qkv_fwd",
        grid=(2 * nb_seq,),
        in_specs=[pl.BlockSpec((ht, GRP_A), lambda i: (i, 0)), full(wq), full(wkv), full(gq), full(gkv), tab, tab, tab],
        out_specs=(qk, qk, pl.BlockSpec((1, N_HEADS, ht, D_V), lambda i: (i // 2, 0, i % 2, 0))),
        out_shape=(
            jax.ShapeDtypeStruct((nb_seq, N_HEADS, tp, 2 * LANES), BF16),
            jax.ShapeDtypeStruct((nb_seq, N_HEADS, tp, 2 * LANES), BF16),
            jax.ShapeDtypeStruct((nb_seq, N_HEADS, tp, D_V), BF16),
        ),
        compiler_params=_params("parallel"),
    )(p, wq, wkv, gq, gkv, *tables)


def _attn_fwd(q, k, v, p, g_attn):
    nb_seq, _, tp, _ = q.shape

    def body(q_ref, k_ref, v_ref, z_ref, g_ref, y_ref, o_ref, lse_ref):
        g = g_ref[...]
        for r0 in range(0, tp, KV_TILE):
            nq = min(KV_TILE, tp - r0)
            kend = r0 + nq
            qq = q_ref[0, 0, r0:kend, :]
            sd = _dot(qq, k_ref[0, 0, r0:kend, :], _NT) * ATTN_SCALE
            causal = (lax.broadcasted_iota(jnp.int32, (nq, nq), 1) <= lax.broadcasted_iota(jnp.int32, (nq, nq), 0))
            sd = jnp.where(causal, sd, NEG_INF)
            m = jnp.max(sd, axis=-1, keepdims=True)
            if r0:
                so = _dot(qq, k_ref[0, 0, 0:r0, :], _NT) * ATTN_SCALE
                m = jnp.maximum(m, jnp.max(so, axis=-1, keepdims=True))
            ed = jnp.exp(sd - m)
            l = jnp.sum(ed, axis=-1, keepdims=True)
            o = _dot(ed.astype(BF16), v_ref[0, 0, r0:kend, :])
            if r0:
                eo = jnp.exp(so - m)
                l = l + jnp.sum(eo, axis=-1, keepdims=True)
                o = o + _dot(eo.astype(BF16), v_ref[0, 0, 0:r0, :])
            o = o * (1.0 / l)
            o_ref[0, 0, r0:kend, :] = o
            lse_ref[0, 0, r0:kend, :] = jnp.broadcast_to(m + jnp.log(l), (nq, LANES))
            ohat, _ = _rms_stats(o)
            z = z_ref[r0:kend, :].astype(F32)
            y_ref[r0:kend, :] = (ohat * g * (z * _sigmoid(z))).astype(BF16)

    qk = pl.BlockSpec((1, 1, tp, 2 * LANES), lambda b, h: (b, h, 0, 0))
    hv = pl.BlockSpec((1, 1, tp, D_V), lambda b, h: (b, h, 0, 0))
    return pl.pallas_call(
        body,
        name="attn_fwd",
        grid=(nb_seq, N_HEADS),
        in_specs=[qk, qk, hv,
                  pl.BlockSpec((tp, LANES), lambda b, h: (b, GRP_A // LANES + h)),
                  pl.BlockSpec((1, LANES), lambda b, h: (0, h))],
        out_specs=(pl.BlockSpec((tp, LANES), lambda b, h: (b, h)), hv, hv),
        out_shape=(
            jax.ShapeDtypeStruct((nb_seq * tp, N_HEADS * D_V), BF16),
            jax.ShapeDtypeStruct((nb_seq, N_HEADS, tp, D_V), F32),
            jax.ShapeDtypeStruct((nb_seq, N_HEADS, tp, LANES), F32),
        ),
        compiler_params=_params("parallel", "parallel"),
    )(q, k, v, p, g_attn)


_CONV_COL0 = (GRP_A + N_HEADS * D_V) // LANES


def _conv_specs(tp, order):
    cols = CONV_WIDTH // LANES
    return [pl.BlockSpec((tp, LANES), functools.partial(
        lambda a, b, off: order(a, b, off), off=_CONV_COL0 + i * cols)) for i in range(4)]


def _conv_fwd(p, conv_w, g_conv, nb_seq, tp):
    def body(b_ref, c_ref, h_ref, z_ref, w_ref, g_ref, y_ref):
        cc = c_ref[...].astype(F32) * h_ref[...].astype(F32)
        row = lax.broadcasted_iota(jnp.int32, (tp, LANES), 0)
        s1 = jnp.where(row >= 1, pltpu.roll(cc, 1, 0), 0.0)
        s2 = jnp.where(row >= 2, pltpu.roll(cc, 2, 0), 0.0)
        yc = b_ref[...].astype(F32) * (w_ref[0:1, :] * s2 + w_ref[1:2, :] * s1 + w_ref[2:3, :] * cc)
        r = lax.rsqrt(_group_mean(yc * yc) + EPS)
        z = z_ref[...].astype(F32)
        y_ref[...] = (yc * r * g_ref[...] * (z * _sigmoid(z))).astype(BF16)

    return pl.pallas_call(
        body,
        name="conv_fwd",
        grid=(nb_seq, CONV_WIDTH // LANES),
        in_specs=_conv_specs(tp, lambda b, t, off: (b, off + t)) + [
            pl.BlockSpec((8, LANES), lambda b, t: (0, t)),
            pl.BlockSpec((1, LANES), lambda b, t: (0, t))],
        out_specs=pl.BlockSpec((tp, LANES), lambda b, t: (b, t)),
        out_shape=jax.ShapeDtypeStruct((nb_seq * tp, CONV_WIDTH), BF16),
        compiler_params=_params("parallel", "parallel"),
    )(p, p, p, p, conv_w, g_conv)


def _token_copy(hbm, b, k, ts, buf, sem, to_hbm=False):
    lo, hi = max(k * ts - LANES, 0), (k + 1) * ts - LANES
    off = lo - (k * ts - LANES)
    src, dst = hbm.at[b, pl.ds(lo, hi - lo)], buf.at[pl.ds(off, hi - lo)]
    if to_hbm:
        src, dst = dst, src
    return pltpu.make_async_copy(src, dst, sem)


def _for_tile(k, nt, fn):
    for kk in range(nt):
        @pl.when(k == kk)
        def _(kk=kk):
            fn(kk)


def _out_proj_loss(ya, yc, w_out, x, target, g_final, nt):
    nb_seq, s, d = x.shape
    r, ka = ya.shape
    ts = (s + LANES) // nt
    steps = nb_seq * nt

    def body(a_ref, c_ref, w_ref, x_hbm, t_hbm, g_ref, dh_ref, dhb_ref, dg_ref, loss_ref,
             xbuf, tbuf, acc_ref, sems):
        i = pl.program_id(0)
        b, k = i // nt, i % nt

        @pl.when(i == 0)
        def _():
            acc_ref[...] = jnp.zeros_like(acc_ref)
            dg_ref[...] = jnp.zeros_like(dg_ref)

        slot = i % 2

        def fetch(seq, kk, sl):
            return [_token_copy(x_hbm, seq, kk, ts, xbuf.at[sl], sems.at[sl, 0]),
                    _token_copy(t_hbm, seq, kk, ts, tbuf.at[sl], sems.at[sl, 1])]

        def start(seq, sl, kk):
            if kk == 0:
                xbuf[sl, 0:LANES, :] = jnp.zeros((LANES, d), F32)
                tbuf[sl, 0:LANES, :] = jnp.zeros((LANES, d), F32)
            for cp in fetch(seq, kk, sl):
                cp.start()

        @pl.when(i == 0)
        def _():
            start(0, 0, 0)

        @pl.when(i + 1 < steps)
        def _():
            _for_tile((i + 1) % nt, nt, functools.partial(start, (i + 1) // nt, 1 - slot))

        mix = _dot(a_ref[...], w_ref[0:ka, :]) + _dot(c_ref[...], w_ref[ka:, :])
        _for_tile(k, nt, lambda kk: [cp.wait() for cp in fetch(b, kk, slot)])

        real = (lax.broadcasted_iota(jnp.int32, (ts, d), 0) >= LANES) | (k > 0)
        g = g_ref[...]
        hhat, rstd = _rms_stats(xbuf[slot] + mix)
        e = jnp.where(real, hhat * g - tbuf[slot], 0.0)
        acc_ref[...] += jnp.sum(e * e, axis=0, keepdims=True)
        dy = e * (1.0 / d)
        dg_ref[...] += jnp.sum(dy * hhat, axis=0, keepdims=True)
        dh = _rms_bwd(g * dy, hhat, rstd)
        dh_ref[...] = dh
        dhb_ref[...] = dh.astype(BF16)

        @pl.when(i == steps - 1)
        def _():
            total = jnp.sum(acc_ref[...], axis=1, keepdims=True)
            loss_ref[...] = jnp.broadcast_to((0.5 / d) * total, loss_ref.shape)

    hbm = pl.BlockSpec(memory_space=pl.ANY)
    row = pl.BlockSpec((ts, d), lambda i: (i, 0))
    vec = pl.BlockSpec((1, d), lambda i: (0, 0))
    return pl.pallas_call(
        body,
        name="out_proj_loss",
        grid=(steps,),
        in_specs=[pl.BlockSpec((ts, ka), lambda i: (i, 0)), pl.BlockSpec((ts, yc.shape[1]), lambda i: (i, 0)),
                  pl.BlockSpec(w_out.shape, lambda i: (0, 0)), hbm, hbm, vec],
        out_specs=(row, row, vec, pl.BlockSpec((1, LANES), lambda i: (0, 0))),
        out_shape=(
            jax.ShapeDtypeStruct((r, d), F32),
            jax.ShapeDtypeStruct((r, d), BF16),
            jax.ShapeDtypeStruct((1, d), F32),
            jax.ShapeDtypeStruct((1, LANES), F32),
        ),
        scratch_shapes=[pltpu.VMEM((2, ts, d), F32), pltpu.VMEM((2, ts, d), F32), pltpu.VMEM((1, d), F32),
                        pltpu.SemaphoreType.DMA((2, 2))],
        compiler_params=_params("arbitrary"),
    )(ya, yc, w_out, x, target, g_final)


def _out_proj_bwd(dhb, w_out, ya, yc, bm):
    r, d = dhb.shape
    ka = ya.shape[1]
    n_mix = w_out.shape[0]
    last = r // bm - 1

    def body(dh_ref, w_ref, a_ref, c_ref, dcat_ref, dw_ref, acc_ref):
        @pl.when(pl.program_id(0) == 0)
        def _():
            acc_ref[...] = jnp.zeros_like(acc_ref)

        dh = dh_ref[...]
        dcat_ref[...] = _dot(dh, w_ref[...], _NT).astype(BF16)
        acc_ref[0:ka, :] += _dot(a_ref[...], dh, _TN)
        acc_ref[ka:, :] += _dot(c_ref[...], dh, _TN)

        @pl.when(pl.program_id(0) == last)
        def _():
            dw_ref[...] = acc_ref[...].astype(BF16)

    return pl.pallas_call(
        body,
        name="out_proj_bwd",
        grid=(r // bm,),
        in_specs=[pl.BlockSpec((bm, d), lambda i: (i, 0)), pl.BlockSpec(w_out.shape, lambda i: (0, 0)),
                  pl.BlockSpec((bm, ka), lambda i: (i, 0)), pl.BlockSpec((bm, yc.shape[1]), lambda i: (i, 0))],
        out_specs=(pl.BlockSpec((bm, n_mix), lambda i: (i, 0)),
                   pl.BlockSpec((n_mix, d), lambda i: (0, 0))),
        out_shape=(jax.ShapeDtypeStruct((r, n_mix), BF16),
                   jax.ShapeDtypeStruct((n_mix, d), BF16)),
        scratch_shapes=[pltpu.VMEM((n_mix, d), F32)],
        compiler_params=_params("arbitrary"),
    )(dhb, w_out, ya, yc)


def _attn_bwd(q, k, v, o, lse, dcat, p, g_attn, send_out):
    nb_seq, _, tp, _ = q.shape
    steps = N_HEADS * nb_seq

    def body(q_ref, k_ref, v_ref, o_ref, lse_ref, dy_ref, z_ref, g_ref, pay_ref,
             dq_ref, dk_ref, dv_ref, dz_ref, dg_ref, r2_ref, dq_acc, r1, sums, *sems):
        t = pl.program_id(0) * nb_seq + pl.program_id(1)

        def plan():
            return _reduce_plan((pay_ref,), (None,), (r1,), (sums,), (r2_ref,), *sems)

        @pl.when(t == 0)
        def _():
            plan()[0]()

        @pl.when(t == 1)
        def _():
            plan()[1]()

        @pl.when(pl.program_id(1) == 0)
        def _():
            dg_ref[...] = jnp.zeros_like(dg_ref)

        g = g_ref[...]
        z = z_ref[...].astype(F32)
        o = o_ref[0, 0]
        dy = dy_ref[...].astype(F32)
        sig = _sigmoid(z)
        ohat, r = _rms_stats(o)
        don = dy * (z * sig)
        dz_ref[...] = (dy * (ohat * g) * (sig * (1.0 + z * (1.0 - sig)))).astype(BF16)
        dg_ref[...] += jnp.sum(don * ohat, axis=0, keepdims=True)
        do = _rms_bwd(g * don, ohat, r)
        dvec = jnp.sum(do * o, axis=-1, keepdims=True)
        dob = do.astype(BF16)
        lse_col = lse_ref[0, 0, :, 0:1]
        dq_acc[...] = jnp.zeros_like(dq_acc)
        for k0 in range(0, tp, KV_TILE):
            nk = min(KV_TILE, tp - k0)
            nq = tp - k0
            qq = q_ref[0, 0, k0:, :]
            kk = k_ref[0, 0, k0:k0 + nk, :]
            causal = (lax.broadcasted_iota(jnp.int32, (nq, nk), 1) <= lax.broadcasted_iota(jnp.int32, (nq, nk), 0))
            pr = jnp.where(causal, jnp.exp(_dot(qq, kk, _NT) * ATTN_SCALE - lse_col[k0:]), 0.0)
            dp = _dot(dob[k0:], v_ref[0, 0, k0:k0 + nk, :], _NT)
            ds = (pr * (dp - dvec[k0:]) * ATTN_SCALE).astype(BF16)
            dv_ref[0, 0, k0:k0 + nk, :] = _dot(pr.astype(BF16), dob[k0:], _TN).astype(BF16)
            dk_ref[0, 0, k0:k0 + nk, :] = _dot(ds, qq, _TN).astype(BF16)
            dq_acc[k0:, :] += _dot(ds, kk)
        dq_ref[0, 0] = dq_acc[...].astype(BF16)

        @pl.when(t == steps - 1)
        def _():
            plan()[2]()

    qk = pl.BlockSpec((1, 1, tp, 2 * LANES), lambda h, b: (b, h, 0, 0))
    hv = pl.BlockSpec((1, 1, tp, D_V), lambda h, b: (b, h, 0, 0))
    col = pl.BlockSpec((tp, LANES), lambda h, b: (b, h))
    slot = send_out.shape[1:]
    return pl.pallas_call(
        body,
        name="attn_bwd",
        grid=(N_HEADS, nb_seq),
        in_specs=[qk, qk, hv, hv, hv, col,
                  pl.BlockSpec((tp, LANES), lambda h, b: (b, GRP_A // LANES + h)),
                  pl.BlockSpec((1, LANES), lambda h, b: (0, h)),
                  pl.BlockSpec(send_out.shape, lambda h, b: (0, 0, 0))],
        out_specs=(qk, qk, hv, col, pl.BlockSpec((1, LANES), lambda h, b: (0, h)),
                   pl.BlockSpec(memory_space=pl.ANY)),
        out_shape=(
            jax.ShapeDtypeStruct((nb_seq, N_HEADS, tp, 2 * LANES), BF16),
            jax.ShapeDtypeStruct((nb_seq, N_HEADS, tp, 2 * LANES), BF16),
            jax.ShapeDtypeStruct((nb_seq, N_HEADS, tp, D_V), BF16),
            jax.ShapeDtypeStruct((nb_seq * tp, N_HEADS * D_V), BF16),
            jax.ShapeDtypeStruct((1, N_HEADS * D_V), F32),
            jax.ShapeDtypeStruct((N_CHIPS,) + slot, BF16),
        ),
        scratch_shapes=[pltpu.VMEM((tp, 2 * LANES), F32)] + _reduce_scratch([(slot, BF16)], [False]),
        compiler_params=_params("arbitrary", "arbitrary"),
    )(q, k, v, o, lse, dcat, p, g_attn, send_out)


def _qkv_bwd(p, dq, dk, dv, wq, wkv, gq, gkv, tables):
    nb_seq, _, tp, _ = dq.shape
    ht = tp // 2

    def body(pa_ref, dq_ref, dk_ref, dv_ref, wq_ref, wkv_ref, gq_ref, gkv_ref, cos_ref, sa_ref, sb_ref,
             dpa_ref, dwq_ref, dwkv_ref, dgq_ref, dgkv_ref):
        @pl.when(pl.program_id(0) == 0)
        def _():
            dwq_ref[...] = jnp.zeros_like(dwq_ref)
            dwkv_ref[...] = jnp.zeros_like(dwkv_ref)
            dgq_ref[...] = jnp.zeros_like(dgq_ref)
            dgkv_ref[...] = jnp.zeros_like(dgkv_ref)

        pa = pa_ref[...].astype(F32)
        gq, gkv = gq_ref[...], gkv_ref[...]
        cq_hat, rq = _rms_stats(pa[:, :Q_RANK])
        ckv_hat, rkv = _rms_stats(pa[:, Q_RANK:Q_RANK + KV_RANK])
        tabs = (cos_ref[...], sa_ref[...], sb_ref[...])

        pe = [dq_ref[0, h, :, D_NOPE:].astype(F32) for h in range(N_HEADS)]
        pairs = [_rope_t(pe[2 * i] + pltpu.roll(pe[2 * i + 1], D_ROPE, 1), *tabs).astype(BF16) for i in range(2)]
        dq_flat = jnp.concatenate([dq_ref[0, h, :, :D_NOPE] for h in range(N_HEADS)] + pairs, axis=1)
        dwq_ref[...] += _dot((cq_hat * gq).astype(BF16), dq_flat, _TN)
        dcqn = _dot(dq_flat, wq_ref[...], _NT)
        dgq_ref[...] += jnp.sum(dcqn * cq_hat, axis=0, keepdims=True)
        dcq = _rms_bwd(gq * dcqn, cq_hat, rq)

        dkv_flat = jnp.concatenate([dk_ref[0, h, :, :D_NOPE] for h in range(N_HEADS)]
                                   + [dv_ref[0, h] for h in range(N_HEADS)], axis=1)
        dwkv_ref[...] += _dot((ckv_hat * gkv).astype(BF16), dkv_flat, _TN)
        dckvn = _dot(dkv_flat, wkv_ref[...], _NT)
        dgkv_ref[...] += jnp.sum(dckvn * ckv_hat, axis=0, keepdims=True)
        dckv = _rms_bwd(gkv * dckvn, ckv_hat, rkv)

        dk_pe = dk_ref[0, 0, :, D_NOPE:].astype(F32)
        for h in range(1, N_HEADS):
            dk_pe = dk_pe + dk_ref[0, h, :, D_NOPE:].astype(F32)
        dk_pe = jnp.where(lax.broadcasted_iota(jnp.int32, (ht, LANES), 1) < D_ROPE, dk_pe, 0.0)
        dpa_ref[...] = jnp.concatenate([dcq, dckv, _rope_t(dk_pe, *tabs)], axis=1).astype(BF16)

    full = lambda a: pl.BlockSpec(a.shape, lambda i: (0,) * a.ndim)
    tab = pl.BlockSpec((ht, LANES), lambda i: (i % 2, 0))
    qk = pl.BlockSpec((1, N_HEADS, ht, 2 * LANES), lambda i: (i // 2, 0, i % 2, 0))
    acc = lambda shape: pl.BlockSpec(shape, lambda i: (0, 0))
    return pl.pallas_call(
        body,
        name="qkv_bwd",
        grid=(2 * nb_seq,),
        in_specs=[pl.BlockSpec((ht, GRP_A), lambda i: (i, 0)), qk, qk,
                  pl.BlockSpec((1, N_HEADS, ht, D_V), lambda i: (i // 2, 0, i % 2, 0)),
                  full(wq), full(wkv), full(gq), full(gkv), tab, tab, tab],
        out_specs=(pl.BlockSpec((ht, GRP_A), lambda i: (i, 0)),
                   acc(wq.shape), acc(wkv.shape), acc((1, Q_RANK)), acc((1, KV_RANK))),
        out_shape=(
            jax.ShapeDtypeStruct((nb_seq * tp, GRP_A), BF16),
            jax.ShapeDtypeStruct(wq.shape, F32),
            jax.ShapeDtypeStruct(wkv.shape, F32),
            jax.ShapeDtypeStruct((1, Q_RANK), F32),
            jax.ShapeDtypeStruct((1, KV_RANK), F32),
        ),
        compiler_params=_params("arbitrary"),
    )(p, dq, dk, dv, wq, wkv, gq, gkv, *tables)


def _conv_bwd(p, dcat, conv_w, g_conv, nb_seq, tp):
    cols = CONV_WIDTH // LANES

    def body(b_ref, c_ref, h_ref, z_ref, dy_ref, w_ref, g_ref,
             db_ref, dc_ref, dh_ref, dz_ref, dw_ref, dg_ref):
        @pl.when(pl.program_id(1) == 0)
        def _():
            dw_ref[...] = jnp.zeros_like(dw_ref)
            dg_ref[...] = jnp.zeros_like(dg_ref)

        cb, c, h = b_ref[...].astype(F32), c_ref[...].astype(F32), h_ref[...].astype(F32)
        z, dy = z_ref[...].astype(F32), dy_ref[...].astype(F32)
        g = g_ref[...]
        w0, w1, w2 = w_ref[0:1, :], w_ref[1:2, :], w_ref[2:3, :]
        cc = c * h
        row = lax.broadcasted_iota(jnp.int32, (tp, LANES), 0)
        s1 = jnp.where(row >= 1, pltpu.roll(cc, 1, 0), 0.0)
        s2 = jnp.where(row >= 2, pltpu.roll(cc, 2, 0), 0.0)
        dwc = w0 * s2 + w1 * s1 + w2 * cc
        yc = cb * dwc
        r = lax.rsqrt(_group_mean(yc * yc) + EPS)
        ychat = yc * r
        sig = _sigmoid(z)
        dz_ref[...] = (dy * (ychat * g) * (sig * (1.0 + z * (1.0 - sig)))).astype(BF16)
        dyn = dy * (z * sig)
        dg_ref[...] += jnp.sum(dyn * ychat, axis=0, keepdims=True)
        gd = g * dyn
        dyc = r * (gd - ychat * _group_mean(gd * ychat))
        db_ref[...] = (dyc * dwc).astype(BF16)
        ddw = dyc * cb
        dw_ref[0:1, :] += jnp.sum(ddw * s2, axis=0, keepdims=True)
        dw_ref[1:2, :] += jnp.sum(ddw * s1, axis=0, keepdims=True)
        dw_ref[2:3, :] += jnp.sum(ddw * cc, axis=0, keepdims=True)
        u1 = jnp.where(row <= tp - 2, pltpu.roll(ddw, tp - 1, 0), 0.0)
        u2 = jnp.where(row <= tp - 3, pltpu.roll(ddw, tp - 2, 0), 0.0)
        dcc = w2 * ddw + w1 * u1 + w0 * u2
        dc_ref[...] = (dcc * h).astype(BF16)
        dh_ref[...] = (dcc * c).astype(BF16)

    col = pl.BlockSpec((tp, LANES), lambda t, b: (b, t))
    out = jax.ShapeDtypeStruct((nb_seq * tp, CONV_WIDTH), BF16)
    return pl.pallas_call(
        body,
        name="conv_bwd",
        grid=(cols, nb_seq),
        in_specs=_conv_specs(tp, lambda t, b, off: (b, off + t)) + [
            pl.BlockSpec((tp, LANES), lambda t, b: (b, N_HEADS * D_V // LANES + t)),
            pl.BlockSpec((8, LANES), lambda t, b: (0, t)),
            pl.BlockSpec((1, LANES), lambda t, b: (0, t))],
        out_specs=(col, col, col, col,
                   pl.BlockSpec((8, LANES), lambda t, b: (0, t)), pl.BlockSpec((1, LANES), lambda t, b: (0, t))),
        out_shape=(out, out, out, out,
                   jax.ShapeDtypeStruct((8, CONV_WIDTH), F32), jax.ShapeDtypeStruct((1, CONV_WIDTH), F32)),
        compiler_params=_params("arbitrary", "arbitrary"),
    )(p, p, p, p, dcat, conv_w, g_conv)


def _input_bwd(dps, w_in, x, meta, dh, norm_g, nt, send_in):
    nb_seq, s, d = x.shape
    r, kb = dps[0].shape
    ts = (s + LANES) // nt
    steps = nb_seq * nt
    n_dp = len(dps)
    in_slot = send_in.shape[1:]

    def body(*refs):
        dp_refs, w_ref, x_hbm, meta_ref, dh_ref, g_ref, pay_ref = refs[:n_dp], *refs[n_dp:n_dp + 6]
        o = n_dp + 6
        gx_hbm, dmeta_ref, dg_ref, r2_in = refs[o:o + 4]
        xbuf, gxbuf, tok_sems, own_in, r1_in, sum_in = refs[o + 4:o + 10]
        sems = refs[o + 10:]
        i = pl.program_id(0)
        b, k = i // nt, i % nt

        def plan():
            return _reduce_plan((pay_ref,), (own_in,), (r1_in,), (sum_in,), (r2_in,), *sems)

        @pl.when(i == 0)
        def _():
            dmeta_ref[...] = jnp.zeros_like(dmeta_ref)
            dg_ref[...] = jnp.zeros_like(dg_ref)
            plan()[0]()

        @pl.when(i == 1)
        def _():
            plan()[1]()

        def start(kk):
            if kk == 0:
                xbuf[0:PAD_FRONT, :] = jnp.zeros((PAD_FRONT, d), F32)
                xbuf[PAD_FRONT:LANES, :] = meta_ref[...]
            _token_copy(x_hbm, b, kk, ts, xbuf, tok_sems.at[0]).start()

        _for_tile(k, nt, start)
        du = _dot(dp_refs[0][...], w_ref[0:kb, :])
        for j in range(1, n_dp):
            du = du + _dot(dp_refs[j][...], w_ref[kb * j:kb * (j + 1), :])
        _for_tile(k, nt, lambda kk: _token_copy(x_hbm, b, kk, ts, xbuf, tok_sems.at[0]).wait())

        g = g_ref[...]
        hhat, rstd = _rms_stats(xbuf[...])
        dg_ref[...] += jnp.sum(du * hhat, axis=0, keepdims=True)
        res = _rms_bwd(g * du, hhat, rstd) + dh_ref[...]

        @pl.when(i > 0)
        def _():
            _for_tile(k, nt, lambda kk: _token_copy(gx_hbm, b, (kk - 1) % nt, ts, gxbuf, tok_sems.at[1], True).wait())

        gxbuf[...] = res

        @pl.when(k == 0)
        def _():
            dmeta_ref[...] += gxbuf[PAD_FRONT:LANES, :]

        _for_tile(k, nt, lambda kk: _token_copy(gx_hbm, b, kk, ts, gxbuf, tok_sems.at[1], True).start())

        @pl.when(i == steps - 1)
        def _():
            _token_copy(gx_hbm, b, nt - 1, ts, gxbuf, tok_sems.at[1], True).wait()
            plan()[2]()

    whole = lambda a: pl.BlockSpec(a.shape, lambda i: (0,) * a.ndim)
    hbm = pl.BlockSpec(memory_space=pl.ANY)
    return pl.pallas_call(
        body,
        name="input_bwd",
        grid=(steps,),
        in_specs=[pl.BlockSpec((ts, kb), lambda i: (i, 0)) for _ in dps]
        + [whole(w_in), hbm, whole(meta), pl.BlockSpec((ts, d), lambda i: (i, 0)), whole(norm_g), hbm],
        out_specs=(hbm, pl.BlockSpec((N_META, d), lambda i: (0, 0)), pl.BlockSpec((1, d), lambda i: (0, 0)), hbm),
        out_shape=(jax.ShapeDtypeStruct((nb_seq, s, d), F32),
                   jax.ShapeDtypeStruct((N_META, d), F32),
                   jax.ShapeDtypeStruct((1, d), F32),
                   jax.ShapeDtypeStruct((N_CHIPS,) + in_slot, BF16)),
        scratch_shapes=[pltpu.VMEM((ts, d), F32), pltpu.VMEM((ts, d), F32), pltpu.SemaphoreType.DMA((2,))]
        + _reduce_scratch([(in_slot, BF16)], [True]),
        compiler_params=_params("arbitrary"),
    )(*dps, w_in, x, meta, dh, norm_g, send_in)


def _in_proj_bwd_w(u, dps, bm, small_grads):
    r, d = u.shape
    kb = dps[0].shape[1]
    steps = r // bm
    n_dp, n_small = len(dps), len(small_grads)
    small_slot = (SMALL_ROWS, LANES)

    def body(*refs):
        u_ref, dp_refs = refs[0], refs[1:1 + n_dp]
        small_refs = refs[1 + n_dp:1 + n_dp + n_small]
        o = 1 + n_dp + n_small
        o_ref, r2_small = refs[o:o + 2]
        acc_ref, ssmall, r1_small, sum_small = refs[o + 2:o + 6]
        sems = refs[o + 6:]
        i = pl.program_id(0)

        def plan():
            return _reduce_plan((ssmall,), (None,), (r1_small,), (sum_small,), (r2_small,), *sems)

        @pl.when(i == 0)
        def _():
            acc_ref[...] = jnp.zeros_like(acc_ref)
            _pack_small(ssmall, *small_refs)
            plan()[0]()

        @pl.when(i == 1)
        def _():
            plan()[1]()

        uu = u_ref[...]
        for j in range(n_dp):
            acc_ref[kb * j:kb * (j + 1), :] += _dot(dp_refs[j][...], uu, _TN)

        @pl.when(i == steps - 1)
        def _():
            for k in range(N_DEV):
                for s, e, c0 in _in_pieces(k):
                    o_ref[k, s:e, :] = acc_ref[c0:c0 + e - s, :].astype(BF16)
                o_ref[k, SHARD_IN:, :] = jnp.zeros((SHARD_IN_PAD - SHARD_IN, d), BF16)
            plan()[2]()

    whole = lambda a: pl.BlockSpec(a.shape, lambda i: (0,) * a.ndim)
    return pl.pallas_call(
        body,
        name="in_proj_bwd_w",
        grid=(steps,),
        in_specs=[pl.BlockSpec((bm, d), lambda i: (i, 0))]
        + [pl.BlockSpec((bm, kb), lambda i: (i, 0)) for _ in dps] + [whole(a) for a in small_grads],
        out_specs=(pl.BlockSpec((N_DEV, SHARD_IN_PAD, d), lambda i: (0, 0, 0)), pl.BlockSpec(memory_space=pl.ANY)),
        out_shape=(jax.ShapeDtypeStruct((N_DEV, SHARD_IN_PAD, d), BF16),
                   jax.ShapeDtypeStruct((N_CHIPS,) + small_slot, F32)),
        scratch_shapes=[pltpu.VMEM((kb * n_dp, d), F32), pltpu.VMEM((N_DEV,) + small_slot, F32)]
        + _reduce_scratch([(small_slot, F32)], [False]),
        compiler_params=_params("arbitrary"),
    )(u, *dps, *small_grads)


def _local_step(x, loss_target, u, p, meta_f, norm_g, w_in_p, q_norm_g, w_q_p, kv_norm_g, w_kv_p, conv_w_f,
                attn_out_g, conv_out_g, w_out_f, g_final):
    nb_seq, s, d = x.shape
    tp = s + LANES
    ht = tp // 2
    tables = _rope_tables(tp)

    q, k, v = _qkv_fwd(p, w_q_p, w_kv_p, q_norm_g, kv_norm_g, tables, nb_seq, tp)
    ya, o, lse = _attn_fwd(q, k, v, p, attn_out_g)
    yc = _conv_fwd(p, conv_w_f, conv_out_g, nb_seq, tp)
    dh, dhb, d_final_g, loss_part = _out_proj_loss(ya, yc, w_out_f, x, loss_target, g_final, TOKEN_TILES)

    dcat, d_w_out = _out_proj_bwd(dhb, w_out_f, ya, yc, ht)
    send_out = d_w_out.reshape(N_DEV, SHARD_OUT, d)
    dq, dk, dv, dz_attn, d_attn_g, r_out = _attn_bwd(q, k, v, o, lse, dcat, p, attn_out_g, send_out)
    dpa, d_wq_p, d_wkv_p, d_gq, d_gkv = _qkv_bwd(p, dq, dk, dv, w_q_p, w_kv_p, q_norm_g, kv_norm_g, tables)
    d_b, d_c, d_h, dz_conv, d_conv_w, d_conv_g = _conv_bwd(p, dcat, conv_w_f, conv_out_g, nb_seq, tp)
    dps = (dpa, dz_attn, d_b, d_c, d_h, dz_conv)
    small = (d_wq_p, d_wkv_p, d_conv_w, d_final_g, d_gq, d_gkv, d_attn_g, d_conv_g, loss_part)
    send_in, r_small = _in_proj_bwd_w(u, dps, ht // 2, small)
    grad_x, d_meta, d_norm_g, r_in = _input_bwd(dps, w_in_p, x, meta_f, dh, norm_g, TOKEN_TILES, send_in)
    return grad_x, r_in, r_out, r_small, d_meta, d_norm_g


def kernel(x, meta_tokens, norm_g, w_in, q_norm_g, w_q_up, kv_norm_g, w_kv_up, conv_w, attn_out_g, conv_out_g, w_out, final_norm_g, loss_target, m_meta_tokens, m_norm_g, m_w_in, m_q_norm_g, m_w_q_up, m_kv_norm_g, m_w_kv_up, m_conv_w, m_attn_out_g, m_conv_out_g, m_w_out, m_final_norm_g, v_meta_tokens, v_norm_g, v_w_in, v_q_norm_g, v_w_q_up, v_kv_norm_g, v_w_kv_up, v_conv_w, v_attn_out_g, v_conv_out_g, v_w_out, v_final_norm_g):
    d = x.shape[-1]
    ht = (x.shape[1] + LANES) // 2
    u, w_in_p, meta_f = _prep_gather(x, meta_tokens, norm_g, w_in[0].T)
    p, w_q_p, w_kv_p, w_out_f, conv_w_f = _in_proj_gather(
        u, w_in_p, w_q_up[0], w_kv_up[0], w_out[0], conv_w[0], ht, GRP_A)
    g_final = final_norm_g.reshape(1, d)
    grad_x, r_in, r_out, r_small, d_meta, d_norm_g = _local_step(
        x, loss_target, u, p, meta_f, norm_g, w_in_p, q_norm_g, w_q_p, kv_norm_g, w_kv_p, conv_w_f,
        attn_out_g, conv_out_g, w_out_f, g_final)

    flat = lambda a: a.reshape(a.shape[-2:]) if a.ndim == 3 else a.reshape(1, -1) if a.ndim == 1 else a
    transposed = ("w_in",)
    to_kernel = lambda n, a: flat(a).T if n in transposed else flat(a)
    from_kernel = lambda n, a, shape: (a.T if n in transposed else a).reshape(shape)
    params = {
        "meta_tokens": (meta_tokens, m_meta_tokens, v_meta_tokens),
        "norm_g": (norm_g, m_norm_g, v_norm_g),
        "w_in": (w_in, m_w_in, v_w_in),
        "q_norm_g": (q_norm_g, m_q_norm_g, v_q_norm_g),
        "w_q_up": (w_q_up, m_w_q_up, v_w_q_up),
        "kv_norm_g": (kv_norm_g, m_kv_norm_g, v_kv_norm_g),
        "w_kv_up": (w_kv_up, m_w_kv_up, v_w_kv_up),
        "conv_w": (conv_w, m_conv_w, v_conv_w),
        "attn_out_g": (attn_out_g, m_attn_out_g, v_attn_out_g),
        "conv_out_g": (conv_out_g, m_conv_out_g, v_conv_out_g),
        "w_out": (w_out, m_w_out, v_w_out),
        "final_norm_g": (final_norm_g, m_final_norm_g, v_final_norm_g),
    }
    grads, loss = _reduce_tail(r_in, r_out, r_small, d_meta, d_norm_g)
    updated = _adamw(grads, {n: tuple(to_kernel(n, a) for a in t) for n, t in params.items()})
    outs = [[from_kernel(n, updated[n][i], params[n][0].shape) for n, _ in PARAM_SHAPES] for i in range(4)]
    return (loss[0, 0], grad_x, *outs[0], *outs[1], *outs[2], *outs[3])
```

```python
import functools

import jax
import jax.numpy as jnp
from jax import lax
from jax.experimental import pallas as pl
from jax.experimental.pallas import tpu as pltpu

F32 = jnp.float32
BF16 = jnp.bfloat16

N_META = 16
D_MODEL = 1024
N_HEADS = 4
D_NOPE = 128
D_ROPE = 64
D_V = 128
Q_RANK = 256
KV_RANK = 128
CONV_WIDTH = 512
CONV_GROUP = 64
ROPE_THETA = 10000.0
ATTN_SCALE = (D_NOPE + D_ROPE) ** -0.5
Q_SCALE = ATTN_SCALE * 1.4426950408889634
EPS = 1e-6
NEG_INF = -1e30

ADAM_LR = 0.001
ADAM_B1 = 0.9
ADAM_B2 = 0.999
ADAM_EPS = 1e-08
ADAM_WD = 0.01
ADAM_STEP = 10

LANES = 128
PAD_FRONT = LANES - N_META
KV_TILE = 256
N_DEV = 8
VMEM_LIMIT = 56 * 1024 * 1024

IN_PAD = 3072
GRP_A = 512
N_A = Q_RANK + KV_RANK + D_ROPE
IN_PROJ = 3008
SHARD_IN = IN_PROJ // N_DEV
SHARD_IN_PAD = 384
SHARD_Q = 96
SHARD_KV = 128
SHARD_OUT = 128
SHARD_CONV = 64
SHARD_META = 128
Q_COLS = N_HEADS * (D_NOPE + D_ROPE)
KV_COLS = N_HEADS * (D_NOPE + D_V)

ROW_Q, ROW_KV, ROW_META, ROW_CONV = 0, 256, 384, 400
ROW_REPL = 408
ROW_NORM, ROW_FINAL, ROW_GQ, ROW_GKV, ROW_ATTN, ROW_CONVG, ROW_LOSS = 408, 416, 424, 426, 427, 431, 435
SMALL_ROWS = 440

PARAM_SHAPES = (
    ("meta_tokens", (N_META, SHARD_META)), ("norm_g", (1, D_MODEL)), ("w_in", (SHARD_IN, D_MODEL)),
    ("q_norm_g", (1, Q_RANK)), ("w_q_up", (Q_RANK, SHARD_Q)), ("kv_norm_g", (1, KV_RANK)),
    ("w_kv_up", (KV_RANK, SHARD_KV)), ("conv_w", (3, SHARD_CONV)), ("attn_out_g", (1, CONV_WIDTH)),
    ("conv_out_g", (1, CONV_WIDTH)), ("w_out", (SHARD_OUT, D_MODEL)), ("final_norm_g", (1, D_MODEL)),
)


def _in_pieces(k):
    lo, hi = SHARD_IN * k, SHARD_IN * (k + 1)
    out = []
    if lo < N_A:
        out.append((0, min(hi, N_A) - lo, lo))
    if hi > N_A:
        s = max(lo, N_A)
        out.append((s - lo, hi - lo, s + GRP_A - N_A))
    return out


def _q_pieces(k):
    lo, hi = SHARD_Q * k, SHARD_Q * (k + 1)
    out = []
    for h in range(N_HEADS):
        base = (D_NOPE + D_ROPE) * h
        s, e = max(lo, base), min(hi, base + D_NOPE)
        if s < e:
            out.append((s - lo, e - lo, D_NOPE * h + s - base))
        s, e = max(lo, base + D_NOPE), min(hi, base + D_NOPE + D_ROPE)
        if s < e:
            out.append((s - lo, e - lo, N_HEADS * D_NOPE + D_ROPE * h + s - base - D_NOPE))
    return out


def _kv_dst(k):
    return D_NOPE * (k // 2) + (N_HEADS * D_NOPE if k % 2 else 0)


def _params(*sem):
    return pltpu.CompilerParams(dimension_semantics=sem, vmem_limit_bytes=VMEM_LIMIT)


def _rms_stats(x):
    r = lax.rsqrt(jnp.mean(x * x, axis=-1, keepdims=True) + EPS)
    return x * r, r


def _rms_bwd(gdy, xhat, r):
    return r * (gdy - xhat * jnp.mean(gdy * xhat, axis=-1, keepdims=True))


def _sigmoid(z):
    return 1.0 / (1.0 + jnp.exp(-z))


def _group_mean(x):
    i0 = lax.broadcasted_iota(jnp.int32, (LANES, LANES), 0) // CONV_GROUP
    i1 = lax.broadcasted_iota(jnp.int32, (LANES, LANES), 1) // CONV_GROUP
    m = jnp.where(i0 == i1, 1.0 / CONV_GROUP, 0.0).astype(BF16)
    hi = x.astype(BF16)
    lo = (x - hi.astype(F32)).astype(BF16)
    return jnp.dot(hi, m, preferred_element_type=F32) + jnp.dot(lo, m, preferred_element_type=F32)


_NT = (((1,), (1,)), ((), ()))
_TN = (((0,), (0,)), ((), ()))


def _dot(a, b, dims=None):
    if dims is None:
        return jnp.dot(a, b, preferred_element_type=F32)
    return lax.dot_general(a, b, dims, preferred_element_type=F32)


def _device_position():
    x, y, c = lax.axis_index("x"), lax.axis_index("y"), lax.axis_index("c")
    return x, y, c, 4 * x + 2 * y + c


def _gather_plan(srcs, slots, send_sems, recv_sems, local_sems):
    x, y, c, _ = _device_position()
    me, sibling = (x, y, c), (x, y, 1 - c)
    chips = [(1 - x, y), (x, 1 - y), (1 - x, 1 - y)]
    n = len(srcs)

    def slot(a, px, py, pc):
        return slots[a].at[4 * px + 2 * py + pc]

    def copy(a, k, block, to, own=False):
        return pltpu.make_async_remote_copy(
            src_ref=srcs[a] if own else slot(a, *block),
            dst_ref=slot(a, *block),
            send_sem=send_sems.at[7 * a + k],
            recv_sem=recv_sems.at[7 * a + k],
            device_id=to,
            device_id_type=pl.DeviceIdType.MESH,
        )

    def local(a):
        return pltpu.make_async_copy(srcs[a], slot(a, *me), local_sems.at[a])

    def firsts(a):
        return [copy(a, 0, me, sibling, own=True)] + [
            copy(a, 1 + j, me, (*chip, c), own=True) for j, chip in enumerate(chips)]

    def start():
        for a in range(n):
            local(a).start()
            for cp in firsts(a):
                cp.start()

    def forward():
        for j, chip in enumerate(chips):
            for a in range(n):
                copy(a, 1 + j, (*chip, c), me).wait_recv()
                copy(a, 4 + j, (*chip, c), sibling).start()

    def finish():
        for a in range(n):
            copy(a, 0, sibling, me).wait_recv()
            for j, chip in enumerate(chips):
                copy(a, 4 + j, (*chip, 1 - c), me).wait_recv()
        for a in range(n):
            for cp in firsts(a) + [copy(a, 4 + j, (*chip, c), sibling) for j, chip in enumerate(chips)]:
                cp.wait_send()
            local(a).wait()

    return start, forward, finish


def _adam_update(g, w, m, v):
    m_new = ADAM_B1 * m + (1.0 - ADAM_B1) * g
    v_new = ADAM_B2 * v + (1.0 - ADAM_B2) * (g * g)
    m_hat = m_new / (1.0 - ADAM_B1 ** ADAM_STEP)
    v_hat = v_new / (1.0 - ADAM_B2 ** ADAM_STEP)
    return -ADAM_LR * (m_hat / (jnp.sqrt(v_hat) + ADAM_EPS) + ADAM_WD * w), m_new, v_new


def _adamw(grads, params):
    names = [n for n, _ in PARAM_SHAPES]
    n_p = len(names)

    def body(*refs):
        for i in range(n_p):
            g = refs[i][...]
            w, m, v = (refs[n_p + 3 * i + j][...] for j in range(3))
            delta, m_new, v_new = _adam_update(g, w, m, v)
            for j, val in enumerate((g, delta, m_new, v_new)):
                refs[4 * n_p + 4 * i + j][...] = val

    vm = pl.BlockSpec(memory_space=pltpu.VMEM)
    out_shape = []
    for _, shape in PARAM_SHAPES:
        out_shape += [jax.ShapeDtypeStruct(shape, F32)] * 4
    outs = pl.pallas_call(
        body,
        name="adamw",
        out_shape=tuple(out_shape),
        in_specs=[vm] * (4 * n_p),
        out_specs=(vm,) * (4 * n_p),
        compiler_params=pltpu.CompilerParams(vmem_limit_bytes=VMEM_LIMIT),
    )(*[grads[n] for n in names], *[a for n in names for a in params[n]])
    return {n: outs[4 * i:4 * i + 4] for i, n in enumerate(names)}


N_CHIPS = 4


def _reduce_plan(pays, owns, r1s, sums, r2s, send1, recv1, send2, recv2, local_sems):
    x, y, c, _ = _device_position()
    sibling = (x, y, 1 - c)
    chips = [((1 - x if rj & 2 else x), (1 - y if rj & 1 else y)) for rj in range(N_CHIPS)]
    n = len(pays)

    def slot_of(rj, core):
        return 4 * chips[rj][0] + 2 * chips[rj][1] + core

    def to_sibling(a, rj):
        return pltpu.make_async_remote_copy(
            src_ref=pays[a].at[slot_of(rj, 1 - c)], dst_ref=r1s[a].at[rj],
            send_sem=send1.at[N_CHIPS * a + rj], recv_sem=recv1.at[N_CHIPS * a + rj],
            device_id=sibling, device_id_type=pl.DeviceIdType.MESH)

    def load_own(a, rj):
        return pltpu.make_async_copy(pays[a].at[slot_of(rj, c)], owns[a].at[rj], local_sems.at[2 * N_CHIPS * a + rj])

    def to_chip(a, rj):
        return pltpu.make_async_remote_copy(
            src_ref=sums[a].at[rj], dst_ref=r2s[a].at[rj],
            send_sem=send2.at[N_CHIPS * a + rj], recv_sem=recv2.at[N_CHIPS * a + rj],
            device_id=(*chips[rj], c), device_id_type=pl.DeviceIdType.MESH)

    def keep(a):
        return pltpu.make_async_copy(sums[a].at[0], r2s[a].at[0], local_sems.at[2 * N_CHIPS * a + N_CHIPS])

    def start():
        for a in range(n):
            for rj in range(N_CHIPS):
                to_sibling(a, rj).start()
                if owns[a] is not None:
                    load_own(a, rj).start()

    def combine():
        for a in range(n):
            for rj in range(N_CHIPS):
                to_sibling(a, rj).wait_recv()
                if owns[a] is not None:
                    load_own(a, rj).wait()
                    mine = owns[a][rj]
                else:
                    mine = pays[a][slot_of(rj, c)]
                sums[a][rj] = (mine.astype(F32) + r1s[a][rj].astype(F32)).astype(sums[a].dtype)
            keep(a).start()
            for rj in range(1, N_CHIPS):
                to_chip(a, rj).start()

    def finish():
        for a in range(n):
            for rj in range(1, N_CHIPS):
                to_chip(a, rj).wait_recv()
            for rj in range(N_CHIPS):
                to_sibling(a, rj).wait_send()
            for rj in range(1, N_CHIPS):
                to_chip(a, rj).wait_send()
            keep(a).wait()

    return start, combine, finish


def _reduce_scratch(shapes_dtypes, own_flags):
    out = []
    for (shape, dtype), own in zip(shapes_dtypes, own_flags):
        if own:
            out.append(pltpu.VMEM((N_CHIPS,) + shape, dtype))
        out += [pltpu.VMEM((N_CHIPS,) + shape, dtype), pltpu.VMEM((N_CHIPS,) + shape, dtype)]
    n = len(shapes_dtypes)
    out += [pltpu.SemaphoreType.DMA((N_CHIPS * n,))] * 4 + [pltpu.SemaphoreType.DMA((2 * N_CHIPS * n,))]
    return out


def _pack_small(ssmall, dwq, dwkv, dconv, dfinal, dgq, dgkv, dattn, dconvg, loss_part):
    ssmall[...] = jnp.zeros_like(ssmall)
    rep = ssmall.at[0]
    for i in range(D_MODEL // LANES):
        rep[ROW_FINAL + i:ROW_FINAL + i + 1, :] = dfinal[:, LANES * i:LANES * (i + 1)]
    for i in range(Q_RANK // LANES):
        rep[ROW_GQ + i:ROW_GQ + i + 1, :] = dgq[:, LANES * i:LANES * (i + 1)]
    rep[ROW_GKV:ROW_GKV + 1, :] = dgkv[...]
    for i in range(CONV_WIDTH // LANES):
        rep[ROW_ATTN + i:ROW_ATTN + i + 1, :] = dattn[:, LANES * i:LANES * (i + 1)]
        rep[ROW_CONVG + i:ROW_CONVG + i + 1, :] = dconvg[:, LANES * i:LANES * (i + 1)]
    rep[ROW_LOSS:ROW_LOSS + 1, :] = loss_part[...]
    for k in range(N_DEV):
        if k:
            ssmall[k, ROW_REPL:, :] = ssmall[0, ROW_REPL:, :]
        for s, e, d in _q_pieces(k):
            ssmall[k, ROW_Q:ROW_Q + Q_RANK, s:e] = dwq[:, d:d + e - s]
        ssmall[k, ROW_KV:ROW_KV + KV_RANK, :] = dwkv[:, _kv_dst(k):_kv_dst(k) + SHARD_KV]
        ssmall[k, ROW_CONV:ROW_CONV + 3, 0:SHARD_CONV] = dconv[0:3, SHARD_CONV * k:SHARD_CONV * (k + 1)]


TOKEN_TILES = 4
TAIL_ROWS = N_META + D_MODEL // LANES


def _reduce_tail(r_in, r_out, r_small, d_meta, d_norm):
    n_p = len(PARAM_SHAPES)
    names = [n for n, _ in PARAM_SHAPES]

    def body(*refs):
        rin, rout, rsmall, dmeta, dnorm = refs[:5]
        g_out = {n: refs[5 + i] for i, n in enumerate(names)}
        loss_out = refs[5 + n_p]
        stail, rtail, gsum, gtail, send_sems, recv_sems = refs[6 + n_p:]
        x, y, c, me = _device_position()
        my_chip = 2 * x + y

        for k in range(N_DEV):
            stail[k, 0:N_META, :] = dmeta[:, SHARD_META * k:SHARD_META * (k + 1)]
            for i in range(D_MODEL // LANES):
                stail[k, N_META + i:N_META + i + 1, :] = dnorm[:, LANES * i:LANES * (i + 1)]
        copies = []
        for r in range(1, N_DEV):
            peer = (1 - x if r & 4 else x, 1 - y if r & 2 else y, 1 - c if r & 1 else c)
            copies.append(pltpu.make_async_remote_copy(
                src_ref=stail.at[4 * peer[0] + 2 * peer[1] + peer[2]],
                dst_ref=rtail.at[r],
                send_sem=send_sems.at[r - 1],
                recv_sem=recv_sems.at[r - 1],
                device_id=peer,
                device_id_type=pl.DeviceIdType.MESH,
            ))
        for cp in copies:
            cp.start()
        rtail[0] = stail[me]

        g = rin[my_chip].astype(F32)
        for ch in range(1, N_CHIPS):
            g = g + rin[ch ^ my_chip].astype(F32)
        g_out["w_in"][...] = g[:SHARD_IN, :]

        g = rout[my_chip].astype(F32)
        gs = rsmall[my_chip]
        for ch in range(1, N_CHIPS):
            g = g + rout[ch ^ my_chip].astype(F32)
            gs = gs + rsmall[ch ^ my_chip]
        g_out["w_out"][...] = g
        gsum[...] = gs
        g_out["w_q_up"][...] = gsum[ROW_Q:ROW_Q + Q_RANK, 0:SHARD_Q]
        g_out["w_kv_up"][...] = gsum[ROW_KV:ROW_KV + KV_RANK, :]
        g_out["conv_w"][...] = gsum[ROW_CONV:ROW_CONV + 3, 0:SHARD_CONV]
        for name, row, width in (("final_norm_g", ROW_FINAL, D_MODEL), ("q_norm_g", ROW_GQ, Q_RANK),
                                 ("kv_norm_g", ROW_GKV, KV_RANK), ("attn_out_g", ROW_ATTN, CONV_WIDTH),
                                 ("conv_out_g", ROW_CONVG, CONV_WIDTH)):
            for i in range(width // LANES):
                g_out[name][:, LANES * i:LANES * (i + 1)] = gsum[row + i:row + i + 1, :]
        loss_out[...] = gsum[ROW_LOSS:ROW_LOSS + 1, :]

        for cp in copies:
            cp.wait_recv()
        gt = rtail[me]
        for d in range(1, N_DEV):
            gt = gt + rtail[d ^ me]
        gtail[...] = gt
        g_out["meta_tokens"][...] = gtail[0:N_META, :]
        for i in range(D_MODEL // LANES):
            g_out["norm_g"][:, LANES * i:LANES * (i + 1)] = gtail[N_META + i:N_META + i + 1, :]
        for cp in copies:
            cp.wait_send()

    vm = pl.BlockSpec(memory_space=pltpu.VMEM)
    out_shape = [jax.ShapeDtypeStruct(shape, F32) for _, shape in PARAM_SHAPES]
    out_shape.append(jax.ShapeDtypeStruct((1, LANES), F32))
    outs = pl.pallas_call(
        body,
        name="reduce_tail",
        out_shape=tuple(out_shape),
        in_specs=[vm] * 5,
        out_specs=(vm,) * len(out_shape),
        scratch_shapes=[
            pltpu.VMEM((N_DEV, TAIL_ROWS, LANES), F32),
            pltpu.VMEM((N_DEV, TAIL_ROWS, LANES), F32),
            pltpu.VMEM((SMALL_ROWS, LANES), F32),
            pltpu.VMEM((TAIL_ROWS, LANES), F32),
            pltpu.SemaphoreType.DMA((N_DEV - 1,)),
            pltpu.SemaphoreType.DMA((N_DEV - 1,)),
        ],
        compiler_params=pltpu.CompilerParams(vmem_limit_bytes=VMEM_LIMIT),
    )(r_in, r_out, r_small, d_meta, d_norm)
    return {n: outs[i] for i, n in enumerate(names)}, outs[-1]


def _prep_gather(x, meta, norm_g, w_in_t):
    nb_seq, s, d = x.shape
    nb = s // LANES + 1
    forward_step = nb_seq * (nb // 3)
    finish_step = nb_seq * (nb - 1)

    def body(x_ref, meta_ref, g_ref, win_ref, u_ref, w_in_p, meta_f,
             sbig, ssmall, gbig, gsmall, send_sems, recv_sems, local_sems):
        jj, b = pl.program_id(0), pl.program_id(1)
        t = jj * nb_seq + b

        def plan():
            return _gather_plan((sbig, ssmall), (gbig, gsmall), send_sems, recv_sems, local_sems)

        @pl.when(t == 0)
        def _():
            sbig[0:SHARD_IN, :] = win_ref[...].astype(BF16)
            sbig[SHARD_IN:, :] = jnp.zeros((SHARD_IN_PAD - SHARD_IN, d), BF16)
            ssmall[...] = meta_ref[...]
            plan()[0]()

        @pl.when(t == forward_step)
        def _():
            plan()[1]()

        @pl.when(t == finish_step)
        def _():
            plan()[2]()
            w_in_p[N_A:GRP_A, :] = jnp.zeros((GRP_A - N_A, d), BF16)
            for k in range(N_DEV):
                for s0, e0, d0 in _in_pieces(k):
                    w_in_p[d0:d0 + e0 - s0, :] = gbig[k, s0:e0, :]
                meta_f[:, SHARD_META * k:SHARD_META * (k + 1)] = gsmall[k]

        def norm(h):
            hhat, _ = _rms_stats(h)
            return (hhat * g_ref[...]).astype(BF16)

        @pl.when(jj < nb - 1)
        def _():
            u_ref[...] = norm(x_ref[0])

        @pl.when(jj == nb - 1)
        def _():
            u_ref[0:PAD_FRONT, :] = jnp.zeros((PAD_FRONT, d), BF16)
            u_ref[PAD_FRONT:LANES, :] = norm(meta_f[...])

    whole = lambda shape: pl.BlockSpec(shape, lambda jj, b: (0,) * len(shape))
    return pl.pallas_call(
        body,
        name="prep_norm_gather",
        grid=(nb, nb_seq),
        in_specs=[
            pl.BlockSpec((1, LANES, d), lambda jj, b: (b, jnp.minimum(jj, nb - 2), 0)),
            whole(meta.shape), whole(norm_g.shape), whole(w_in_t.shape),
        ],
        out_specs=(pl.BlockSpec((LANES, d), lambda jj, b: (b * nb + (jj + 1) % nb, 0)),
                   whole((IN_PAD, d)), whole((N_META, d))),
        out_shape=(jax.ShapeDtypeStruct((nb_seq * nb * LANES, d), BF16),
                   jax.ShapeDtypeStruct((IN_PAD, d), BF16),
                   jax.ShapeDtypeStruct((N_META, d), F32)),
        scratch_shapes=[
            pltpu.VMEM((SHARD_IN_PAD, d), BF16),
            pltpu.VMEM((N_META, SHARD_META), F32),
            pltpu.VMEM((N_DEV, SHARD_IN_PAD, d), BF16),
            pltpu.VMEM((N_DEV, N_META, SHARD_META), F32),
            pltpu.SemaphoreType.DMA((14,)),
            pltpu.SemaphoreType.DMA((14,)),
            pltpu.SemaphoreType.DMA((2,)),
        ],
        compiler_params=_params("arbitrary", "arbitrary"),
    )(x, meta, norm_g, w_in_t)


def _in_proj_gather(u, w_in_p, w_q, w_kv, w_out, conv_w, bm, bn):
    m, k_dim = u.shape
    n = w_in_p.shape[0]
    steps = (m // bm) * (n // bn)
    forward_step = steps // 3
    qkv_rows = Q_RANK + KV_RANK

    def body(a_ref, b_ref, wq_ref, wkv_ref, wout_ref, conv_ref, o_ref, w_q_p, w_kv_p, w_out_f, conv_f,
             sqkv, sout, sconv, gqkv, gout, gconv, send_sems, recv_sems, local_sems):
        t = pl.program_id(0) * (n // bn) + pl.program_id(1)

        def plan():
            return _gather_plan((sqkv, sout, sconv), (gqkv, gout, gconv), send_sems, recv_sems, local_sems)

        @pl.when(t == 0)
        def _():
            sqkv[...] = jnp.zeros_like(sqkv)
            sqkv[0:Q_RANK, 0:SHARD_Q] = wq_ref[...].astype(BF16)
            sqkv[Q_RANK:, :] = wkv_ref[...].astype(BF16)
            sout[...] = wout_ref[...].astype(BF16)
            sconv[...] = jnp.zeros_like(sconv)
            sconv[0:3, 0:SHARD_CONV] = conv_ref[...]
            plan()[0]()

        @pl.when(t == forward_step)
        def _():
            plan()[1]()

        o_ref[...] = _dot(a_ref[...], b_ref[...], _NT).astype(o_ref.dtype)

        @pl.when(t == steps - 1)
        def _():
            plan()[2]()
            conv_f[...] = jnp.zeros_like(conv_f)
            for k in range(N_DEV):
                for s0, e0, d0 in _q_pieces(k):
                    w_q_p[:, d0:d0 + e0 - s0] = gqkv[k, 0:Q_RANK, s0:e0]
                w_kv_p[:, _kv_dst(k):_kv_dst(k) + SHARD_KV] = gqkv[k, Q_RANK:, :]
                w_out_f[SHARD_OUT * k:SHARD_OUT * (k + 1), :] = gout[k]
                conv_f[0:3, SHARD_CONV * k:SHARD_CONV * (k + 1)] = gconv[k, 0:3, 0:SHARD_CONV]

    whole = lambda shape: pl.BlockSpec(shape, lambda i, j: (0,) * len(shape))
    return pl.pallas_call(
        body,
        name="in_proj_gather",
        grid=(m // bm, n // bn),
        in_specs=[pl.BlockSpec((bm, k_dim), lambda i, j: (i, 0)), pl.BlockSpec((bn, k_dim), lambda i, j: (j, 0)),
                  whole(w_q.shape), whole(w_kv.shape), whole(w_out.shape), whole(conv_w.shape)],
        out_specs=(pl.BlockSpec((bm, bn), lambda i, j: (i, j)),
                   whole((Q_RANK, Q_COLS)), whole((KV_RANK, KV_COLS)), whole((D_MODEL, D_MODEL)),
                   whole((8, CONV_WIDTH))),
        out_shape=(jax.ShapeDtypeStruct((m, n), BF16),
                   jax.ShapeDtypeStruct((Q_RANK, Q_COLS), BF16),
                   jax.ShapeDtypeStruct((KV_RANK, KV_COLS), BF16),
                   jax.ShapeDtypeStruct((D_MODEL, D_MODEL), BF16),
                   jax.ShapeDtypeStruct((8, CONV_WIDTH), F32)),
        scratch_shapes=[
            pltpu.VMEM((qkv_rows, LANES), BF16),
            pltpu.VMEM((SHARD_OUT, D_MODEL), BF16),
            pltpu.VMEM((8, LANES), F32),
            pltpu.VMEM((N_DEV, qkv_rows, LANES), BF16),
            pltpu.VMEM((N_DEV, SHARD_OUT, D_MODEL), BF16),
            pltpu.VMEM((N_DEV, 8, LANES), F32),
            pltpu.SemaphoreType.DMA((21,)),
            pltpu.SemaphoreType.DMA((21,)),
            pltpu.SemaphoreType.DMA((3,)),
        ],
        compiler_params=_params("arbitrary", "arbitrary"),
    )(u, w_in_p, w_q, w_kv, w_out, conv_w)


def _rope_tables(tp):
    half = D_ROPE // 2
    inv_freq = 1.0 / (ROPE_THETA ** (jnp.arange(half, dtype=F32) / half))
    pos = (jnp.arange(tp) - PAD_FRONT).astype(F32)
    ang = pos[:, None] * inv_freq[None, :]
    cos = jnp.tile(jnp.cos(ang), (1, LANES // half))
    sin = jnp.tile(jnp.sin(ang), (1, LANES // half))
    first = (jnp.arange(LANES) % D_ROPE) < half
    return cos, jnp.where(first, -sin, 0.0), jnp.where(first, 0.0, sin)


def _rope(t, cos, sa, sb):
    return t * cos + pltpu.roll(t, LANES - D_ROPE // 2, 1) * sa + pltpu.roll(t, D_ROPE // 2, 1) * sb


def _rope_t(t, cos, sa, sb):
    return t * cos + pltpu.roll(t * sa, D_ROPE // 2, 1) + pltpu.roll(t * sb, LANES - D_ROPE // 2, 1)


def _qkv_fwd(p, wq, wkv, gq, gkv, tables, nb_seq, tp):
    ht = tp // 2

    def body(pa_ref, wq_ref, wkv_ref, gq_ref, gkv_ref, cos_ref, sa_ref, sb_ref, q_ref, k_ref, v_ref):
        pa = pa_ref[...].astype(F32)
        cq_hat, _ = _rms_stats(pa[:, :Q_RANK])
        ckv_hat, _ = _rms_stats(pa[:, Q_RANK:Q_RANK + KV_RANK])
        q = _dot((cq_hat * gq_ref[...]).astype(BF16), wq_ref[...]) * Q_SCALE
        kv = _dot((ckv_hat * gkv_ref[...]).astype(BF16), wkv_ref[...])
        tabs = (cos_ref[...], sa_ref[...], sb_ref[...])
        lane = lax.broadcasted_iota(jnp.int32, (ht, LANES), 1)
        low = lane < D_ROPE
        mark = lane == D_ROPE
        row = (pl.program_id(0) % 2) * ht + lax.broadcasted_iota(jnp.int32, (ht, LANES), 0)
        k_pe = jnp.where(mark & (row < PAD_FRONT), NEG_INF, _rope(pa[:, Q_RANK + KV_RANK:], *tabs))
        one = jnp.where(mark & (row >= PAD_FRONT), 1.0, 0.0)
        pairs = [_rope(q[:, N_HEADS * D_NOPE + LANES * i:N_HEADS * D_NOPE + LANES * (i + 1)], *tabs) for i in range(2)]
        for h in range(N_HEADS):
            pair = pairs[h // 2]
            if h % 2:
                pair = pltpu.roll(pair, D_ROPE, 1)
            pe = jnp.where(low, pair, one)
            q_ref[0, h] = jnp.concatenate([q[:, D_NOPE * h:D_NOPE * (h + 1)], pe], axis=1).astype(BF16)
            k_ref[0, h] = jnp.concatenate([kv[:, D_NOPE * h:D_NOPE * (h + 1)], k_pe], axis=1).astype(BF16)
            v_ref[0, h] = kv[:, N_HEADS * D_NOPE + D_V * h:N_HEADS * D_NOPE + D_V * (h + 1)].astype(BF16)

    full = lambda a: pl.BlockSpec(a.shape, lambda i: (0,) * a.ndim)
    tab = pl.BlockSpec((ht, LANES), lambda i: (i % 2, 0))
    qk = pl.BlockSpec((1, N_HEADS, ht, 2 * LANES), lambda i: (i // 2, 0, i % 2, 0))
    return pl.pallas_call(
        body,
        name="qkv_fwd",
        grid=(2 * nb_seq,),
        in_specs=[pl.BlockSpec((ht, GRP_A), lambda i: (i, 0)), full(wq), full(wkv), full(gq), full(gkv), tab, tab, tab],
        out_specs=(qk, qk, pl.BlockSpec((1, N_HEADS, ht, D_V), lambda i: (i // 2, 0, i % 2, 0))),
        out_shape=(
            jax.ShapeDtypeStruct((nb_seq, N_HEADS, tp, 2 * LANES), BF16),
            jax.ShapeDtypeStruct((nb_seq, N_HEADS, tp, 2 * LANES), BF16),
            jax.ShapeDtypeStruct((nb_seq, N_HEADS, tp, D_V), BF16),
        ),
        compiler_params=_params("parallel"),
    )(p, wq, wkv, gq, gkv, *tables)


def _attn_fwd(q, k, v, p, g_attn):
    nb_seq, _, tp, _ = q.shape

    def body(q_ref, k_ref, v_ref, z_ref, g_ref, y_ref, o_ref, lse_ref):
        g = g_ref[...]
        for r0 in range(0, tp, KV_TILE):
            nq = min(KV_TILE, tp - r0)
            kend = r0 + nq
            qq = q_ref[0, 0, r0:kend, :]
            sd = _dot(qq, k_ref[0, 0, r0:kend, :], _NT)
            causal = (lax.broadcasted_iota(jnp.int32, (nq, nq), 1) <= lax.broadcasted_iota(jnp.int32, (nq, nq), 0))
            sd = jnp.where(causal, sd, NEG_INF)
            m = jnp.max(sd, axis=-1, keepdims=True)
            if r0:
                so = _dot(qq, k_ref[0, 0, 0:r0, :], _NT)
                m = jnp.maximum(m, jnp.max(so, axis=-1, keepdims=True))
            ed = jnp.exp2(sd - m)
            l = jnp.sum(ed, axis=-1, keepdims=True)
            o = _dot(ed.astype(BF16), v_ref[0, 0, r0:kend, :])
            if r0:
                eo = jnp.exp2(so - m)
                l = l + jnp.sum(eo, axis=-1, keepdims=True)
                o = o + _dot(eo.astype(BF16), v_ref[0, 0, 0:r0, :])
            o = o * (1.0 / l)
            o_ref[0, 0, r0:kend, :] = o
            lse_ref[0, 0, r0:kend, :] = jnp.broadcast_to(m + jnp.log2(l), (nq, LANES))
            ohat, _ = _rms_stats(o)
            z = z_ref[r0:kend, :].astype(F32)
            y_ref[r0:kend, :] = (ohat * g * (z * _sigmoid(z))).astype(BF16)

    qk = pl.BlockSpec((1, 1, tp, 2 * LANES), lambda b, h: (b, h, 0, 0))
    hv = pl.BlockSpec((1, 1, tp, D_V), lambda b, h: (b, h, 0, 0))
    return pl.pallas_call(
        body,
        name="attn_fwd",
        grid=(nb_seq, N_HEADS),
        in_specs=[qk, qk, hv,
                  pl.BlockSpec((tp, LANES), lambda b, h: (b, GRP_A // LANES + h)),
                  pl.BlockSpec((1, LANES), lambda b, h: (0, h))],
        out_specs=(pl.BlockSpec((tp, LANES), lambda b, h: (b, h)), hv, hv),
        out_shape=(
            jax.ShapeDtypeStruct((nb_seq * tp, N_HEADS * D_V), BF16),
            jax.ShapeDtypeStruct((nb_seq, N_HEADS, tp, D_V), F32),
            jax.ShapeDtypeStruct((nb_seq, N_HEADS, tp, LANES), F32),
        ),
        compiler_params=_params("parallel", "parallel"),
    )(q, k, v, p, g_attn)


_CONV_COL0 = (GRP_A + N_HEADS * D_V) // LANES


def _conv_specs(tp, order):
    cols = CONV_WIDTH // LANES
    return [pl.BlockSpec((tp, LANES), functools.partial(
        lambda a, b, off: order(a, b, off), off=_CONV_COL0 + i * cols)) for i in range(4)]


def _conv_fwd(p, conv_w, g_conv, nb_seq, tp):
    def body(b_ref, c_ref, h_ref, z_ref, w_ref, g_ref, y_ref):
        cc = c_ref[...].astype(F32) * h_ref[...].astype(F32)
        row = lax.broadcasted_iota(jnp.int32, (tp, LANES), 0)
        s1 = jnp.where(row >= 1, pltpu.roll(cc, 1, 0), 0.0)
        s2 = jnp.where(row >= 2, pltpu.roll(cc, 2, 0), 0.0)
        yc = b_ref[...].astype(F32) * (w_ref[0:1, :] * s2 + w_ref[1:2, :] * s1 + w_ref[2:3, :] * cc)
        r = lax.rsqrt(_group_mean(yc * yc) + EPS)
        z = z_ref[...].astype(F32)
        y_ref[...] = (yc * r * g_ref[...] * (z * _sigmoid(z))).astype(BF16)

    return pl.pallas_call(
        body,
        name="conv_fwd",
        grid=(nb_seq, CONV_WIDTH // LANES),
        in_specs=_conv_specs(tp, lambda b, t, off: (b, off + t)) + [
            pl.BlockSpec((8, LANES), lambda b, t: (0, t)),
            pl.BlockSpec((1, LANES), lambda b, t: (0, t))],
        out_specs=pl.BlockSpec((tp, LANES), lambda b, t: (b, t)),
        out_shape=jax.ShapeDtypeStruct((nb_seq * tp, CONV_WIDTH), BF16),
        compiler_params=_params("parallel", "parallel"),
    )(p, p, p, p, conv_w, g_conv)


def _token_copy(hbm, b, k, ts, buf, sem, to_hbm=False):
    lo, hi = max(k * ts - LANES, 0), (k + 1) * ts - LANES
    off = lo - (k * ts - LANES)
    src, dst = hbm.at[b, pl.ds(lo, hi - lo)], buf.at[pl.ds(off, hi - lo)]
    if to_hbm:
        src, dst = dst, src
    return pltpu.make_async_copy(src, dst, sem)


def _for_tile(k, nt, fn):
    for kk in range(nt):
        @pl.when(k == kk)
        def _(kk=kk):
            fn(kk)


def _out_proj_loss(ya, yc, w_out, x, target, g_final, nt):
    nb_seq, s, d = x.shape
    r, ka = ya.shape
    ts = (s + LANES) // nt
    steps = nb_seq * nt

    def body(a_ref, c_ref, w_ref, x_hbm, t_hbm, g_ref, dh_ref, dhb_ref, dg_ref, loss_ref,
             xbuf, tbuf, acc_ref, sems):
        i = pl.program_id(0)
        b, k = i // nt, i % nt

        @pl.when(i == 0)
        def _():
            acc_ref[...] = jnp.zeros_like(acc_ref)
            dg_ref[...] = jnp.zeros_like(dg_ref)

        slot = i % 2

        def fetch(seq, kk, sl):
            return [_token_copy(x_hbm, seq, kk, ts, xbuf.at[sl], sems.at[sl, 0]),
                    _token_copy(t_hbm, seq, kk, ts, tbuf.at[sl], sems.at[sl, 1])]

        def start(seq, sl, kk):
            if kk == 0:
                xbuf[sl, 0:LANES, :] = jnp.zeros((LANES, d), F32)
                tbuf[sl, 0:LANES, :] = jnp.zeros((LANES, d), F32)
            for cp in fetch(seq, kk, sl):
                cp.start()

        @pl.when(i == 0)
        def _():
            start(0, 0, 0)

        @pl.when(i + 1 < steps)
        def _():
            _for_tile((i + 1) % nt, nt, functools.partial(start, (i + 1) // nt, 1 - slot))

        mix = _dot(a_ref[...], w_ref[0:ka, :]) + _dot(c_ref[...], w_ref[ka:, :])
        _for_tile(k, nt, lambda kk: [cp.wait() for cp in fetch(b, kk, slot)])

        real = (lax.broadcasted_iota(jnp.int32, (ts, d), 0) >= LANES) | (k > 0)
        g = g_ref[...]
        hhat, rstd = _rms_stats(xbuf[slot] + mix)
        e = jnp.where(real, hhat * g - tbuf[slot], 0.0)
        acc_ref[...] += jnp.sum(e * e, axis=0, keepdims=True)
        dy = e * (1.0 / d)
        dg_ref[...] += jnp.sum(dy * hhat, axis=0, keepdims=True)
        dh = _rms_bwd(g * dy, hhat, rstd)
        dh_ref[...] = dh
        dhb_ref[...] = dh.astype(BF16)

        @pl.when(i == steps - 1)
        def _():
            total = jnp.sum(acc_ref[...], axis=1, keepdims=True)
            loss_ref[...] = jnp.broadcast_to((0.5 / d) * total, loss_ref.shape)

    hbm = pl.BlockSpec(memory_space=pl.ANY)
    row = pl.BlockSpec((ts, d), lambda i: (i, 0))
    vec = pl.BlockSpec((1, d), lambda i: (0, 0))
    return pl.pallas_call(
        body,
        name="out_proj_loss",
        grid=(steps,),
        in_specs=[pl.BlockSpec((ts, ka), lambda i: (i, 0)), pl.BlockSpec((ts, yc.shape[1]), lambda i: (i, 0)),
                  pl.BlockSpec(w_out.shape, lambda i: (0, 0)), hbm, hbm, vec],
        out_specs=(row, row, vec, pl.BlockSpec((1, LANES), lambda i: (0, 0))),
        out_shape=(
            jax.ShapeDtypeStruct((r, d), F32),
            jax.ShapeDtypeStruct((r, d), BF16),
            jax.ShapeDtypeStruct((1, d), F32),
            jax.ShapeDtypeStruct((1, LANES), F32),
        ),
        scratch_shapes=[pltpu.VMEM((2, ts, d), F32), pltpu.VMEM((2, ts, d), F32), pltpu.VMEM((1, d), F32),
                        pltpu.SemaphoreType.DMA((2, 2))],
        compiler_params=_params("arbitrary"),
    )(ya, yc, w_out, x, target, g_final)


def _out_proj_bwd(dhb, w_out, ya, yc, bm):
    r, d = dhb.shape
    ka = ya.shape[1]
    n_mix = w_out.shape[0]
    last = r // bm - 1

    def body(dh_ref, w_ref, a_ref, c_ref, dcat_ref, dw_ref, acc_ref):
        @pl.when(pl.program_id(0) == 0)
        def _():
            acc_ref[...] = jnp.zeros_like(acc_ref)

        dh = dh_ref[...]
        dcat_ref[...] = _dot(dh, w_ref[...], _NT).astype(BF16)
        acc_ref[0:ka, :] += _dot(a_ref[...], dh, _TN)
        acc_ref[ka:, :] += _dot(c_ref[...], dh, _TN)

        @pl.when(pl.program_id(0) == last)
        def _():
            dw_ref[...] = acc_ref[...].astype(BF16)

    return pl.pallas_call(
        body,
        name="out_proj_bwd",
        grid=(r // bm,),
        in_specs=[pl.BlockSpec((bm, d), lambda i: (i, 0)), pl.BlockSpec(w_out.shape, lambda i: (0, 0)),
                  pl.BlockSpec((bm, ka), lambda i: (i, 0)), pl.BlockSpec((bm, yc.shape[1]), lambda i: (i, 0))],
        out_specs=(pl.BlockSpec((bm, n_mix), lambda i: (i, 0)),
                   pl.BlockSpec((n_mix, d), lambda i: (0, 0))),
        out_shape=(jax.ShapeDtypeStruct((r, n_mix), BF16),
                   jax.ShapeDtypeStruct((n_mix, d), BF16)),
        scratch_shapes=[pltpu.VMEM((n_mix, d), F32)],
        compiler_params=_params("arbitrary"),
    )(dhb, w_out, ya, yc)


def _attn_bwd(q, k, v, o, lse, dcat, p, g_attn, send_out):
    nb_seq, _, tp, _ = q.shape
    steps = N_HEADS * nb_seq

    def body(q_ref, k_ref, v_ref, o_ref, lse_ref, dy_ref, z_ref, g_ref, pay_ref,
             dq_ref, dk_ref, dv_ref, dz_ref, dg_ref, r2_ref, dq_acc, r1, sums, *sems):
        t = pl.program_id(0) * nb_seq + pl.program_id(1)

        def plan():
            return _reduce_plan((pay_ref,), (None,), (r1,), (sums,), (r2_ref,), *sems)

        @pl.when(t == 0)
        def _():
            plan()[0]()

        @pl.when(t == 1)
        def _():
            plan()[1]()

        @pl.when(pl.program_id(1) == 0)
        def _():
            dg_ref[...] = jnp.zeros_like(dg_ref)

        g = g_ref[...]
        z = z_ref[...].astype(F32)
        o = o_ref[0, 0]
        dy = dy_ref[...].astype(F32)
        sig = _sigmoid(z)
        ohat, r = _rms_stats(o)
        don = dy * (z * sig)
        dz_ref[...] = (dy * (ohat * g) * (sig * (1.0 + z * (1.0 - sig)))).astype(BF16)
        dg_ref[...] += jnp.sum(don * ohat, axis=0, keepdims=True)
        do = _rms_bwd(g * don, ohat, r)
        dvec = jnp.sum(do * o, axis=-1, keepdims=True)
        dob = do.astype(BF16)
        lse_col = lse_ref[0, 0, :, 0:1]
        dq_acc[...] = jnp.zeros_like(dq_acc)
        for k0 in range(0, tp, KV_TILE):
            nk = min(KV_TILE, tp - k0)
            nq = tp - k0
            qq = q_ref[0, 0, k0:, :]
            kk = k_ref[0, 0, k0:k0 + nk, :]
            causal = (lax.broadcasted_iota(jnp.int32, (nq, nk), 1) <= lax.broadcasted_iota(jnp.int32, (nq, nk), 0))
            pr = jnp.where(causal, jnp.exp2(_dot(qq, kk, _NT) - lse_col[k0:]), 0.0)
            dp = _dot(dob[k0:], v_ref[0, 0, k0:k0 + nk, :], _NT)
            ds = (pr * (dp - dvec[k0:])).astype(BF16)
            dv_ref[0, 0, k0:k0 + nk, :] = _dot(pr.astype(BF16), dob[k0:], _TN).astype(BF16)
            dk_ref[0, 0, k0:k0 + nk, :] = (_dot(ds, qq, _TN) * (ATTN_SCALE / Q_SCALE)).astype(BF16)
            dq_acc[k0:, :] += _dot(ds, kk)
        dq_ref[0, 0] = (dq_acc[...] * ATTN_SCALE).astype(BF16)

        @pl.when(t == steps - 1)
        def _():
            plan()[2]()

    qk = pl.BlockSpec((1, 1, tp, 2 * LANES), lambda h, b: (b, h, 0, 0))
    hv = pl.BlockSpec((1, 1, tp, D_V), lambda h, b: (b, h, 0, 0))
    col = pl.BlockSpec((tp, LANES), lambda h, b: (b, h))
    slot = send_out.shape[1:]
    return pl.pallas_call(
        body,
        name="attn_bwd",
        grid=(N_HEADS, nb_seq),
        in_specs=[qk, qk, hv, hv, hv, col,
                  pl.BlockSpec((tp, LANES), lambda h, b: (b, GRP_A // LANES + h)),
                  pl.BlockSpec((1, LANES), lambda h, b: (0, h)),
                  pl.BlockSpec(send_out.shape, lambda h, b: (0, 0, 0))],
        out_specs=(qk, qk, hv, col, pl.BlockSpec((1, LANES), lambda h, b: (0, h)),
                   pl.BlockSpec(memory_space=pl.ANY)),
        out_shape=(
            jax.ShapeDtypeStruct((nb_seq, N_HEADS, tp, 2 * LANES), BF16),
            jax.ShapeDtypeStruct((nb_seq, N_HEADS, tp, 2 * LANES), BF16),
            jax.ShapeDtypeStruct((nb_seq, N_HEADS, tp, D_V), BF16),
            jax.ShapeDtypeStruct((nb_seq * tp, N_HEADS * D_V), BF16),
            jax.ShapeDtypeStruct((1, N_HEADS * D_V), F32),
            jax.ShapeDtypeStruct((N_CHIPS,) + slot, BF16),
        ),
        scratch_shapes=[pltpu.VMEM((tp, 2 * LANES), F32)] + _reduce_scratch([(slot, BF16)], [False]),
        compiler_params=_params("arbitrary", "arbitrary"),
    )(q, k, v, o, lse, dcat, p, g_attn, send_out)


def _qkv_bwd(p, dq, dk, dv, wq, wkv, gq, gkv, tables):
    nb_seq, _, tp, _ = dq.shape
    ht = tp // 2

    def body(pa_ref, dq_ref, dk_ref, dv_ref, wq_ref, wkv_ref, gq_ref, gkv_ref, cos_ref, sa_ref, sb_ref,
             dpa_ref, dwq_ref, dwkv_ref, dgq_ref, dgkv_ref):
        @pl.when(pl.program_id(0) == 0)
        def _():
            dwq_ref[...] = jnp.zeros_like(dwq_ref)
            dwkv_ref[...] = jnp.zeros_like(dwkv_ref)
            dgq_ref[...] = jnp.zeros_like(dgq_ref)
            dgkv_ref[...] = jnp.zeros_like(dgkv_ref)

        pa = pa_ref[...].astype(F32)
        gq, gkv = gq_ref[...], gkv_ref[...]
        cq_hat, rq = _rms_stats(pa[:, :Q_RANK])
        ckv_hat, rkv = _rms_stats(pa[:, Q_RANK:Q_RANK + KV_RANK])
        tabs = (cos_ref[...], sa_ref[...], sb_ref[...])

        pe = [dq_ref[0, h, :, D_NOPE:].astype(F32) for h in range(N_HEADS)]
        pairs = [_rope_t(pe[2 * i] + pltpu.roll(pe[2 * i + 1], D_ROPE, 1), *tabs).astype(BF16) for i in range(2)]
        dq_flat = jnp.concatenate([dq_ref[0, h, :, :D_NOPE] for h in range(N_HEADS)] + pairs, axis=1)
        dwq_ref[...] += _dot((cq_hat * gq).astype(BF16), dq_flat, _TN)
        dcqn = _dot(dq_flat, wq_ref[...], _NT)
        dgq_ref[...] += jnp.sum(dcqn * cq_hat, axis=0, keepdims=True)
        dcq = _rms_bwd(gq * dcqn, cq_hat, rq)

        dkv_flat = jnp.concatenate([dk_ref[0, h, :, :D_NOPE] for h in range(N_HEADS)]
                                   + [dv_ref[0, h] for h in range(N_HEADS)], axis=1)
        dwkv_ref[...] += _dot((ckv_hat * gkv).astype(BF16), dkv_flat, _TN)
        dckvn = _dot(dkv_flat, wkv_ref[...], _NT)
        dgkv_ref[...] += jnp.sum(dckvn * ckv_hat, axis=0, keepdims=True)
        dckv = _rms_bwd(gkv * dckvn, ckv_hat, rkv)

        dk_pe = dk_ref[0, 0, :, D_NOPE:].astype(F32)
        for h in range(1, N_HEADS):
            dk_pe = dk_pe + dk_ref[0, h, :, D_NOPE:].astype(F32)
        dk_pe = jnp.where(lax.broadcasted_iota(jnp.int32, (ht, LANES), 1) < D_ROPE, dk_pe, 0.0)
        dpa_ref[...] = jnp.concatenate([dcq, dckv, _rope_t(dk_pe, *tabs)], axis=1).astype(BF16)

    full = lambda a: pl.BlockSpec(a.shape, lambda i: (0,) * a.ndim)
    tab = pl.BlockSpec((ht, LANES), lambda i: (i % 2, 0))
    qk = pl.BlockSpec((1, N_HEADS, ht, 2 * LANES), lambda i: (i // 2, 0, i % 2, 0))
    acc = lambda shape: pl.BlockSpec(shape, lambda i: (0, 0))
    return pl.pallas_call(
        body,
        name="qkv_bwd",
        grid=(2 * nb_seq,),
        in_specs=[pl.BlockSpec((ht, GRP_A), lambda i: (i, 0)), qk, qk,
                  pl.BlockSpec((1, N_HEADS, ht, D_V), lambda i: (i // 2, 0, i % 2, 0)),
                  full(wq), full(wkv), full(gq), full(gkv), tab, tab, tab],
        out_specs=(pl.BlockSpec((ht, GRP_A), lambda i: (i, 0)),
                   acc(wq.shape), acc(wkv.shape), acc((1, Q_RANK)), acc((1, KV_RANK))),
        out_shape=(
            jax.ShapeDtypeStruct((nb_seq * tp, GRP_A), BF16),
            jax.ShapeDtypeStruct(wq.shape, F32),
            jax.ShapeDtypeStruct(wkv.shape, F32),
            jax.ShapeDtypeStruct((1, Q_RANK), F32),
            jax.ShapeDtypeStruct((1, KV_RANK), F32),
        ),
        compiler_params=_params("arbitrary"),
    )(p, dq, dk, dv, wq, wkv, gq, gkv, *tables)


def _conv_bwd(p, dcat, conv_w, g_conv, nb_seq, tp):
    cols = CONV_WIDTH // LANES

    def body(b_ref, c_ref, h_ref, z_ref, dy_ref, w_ref, g_ref,
             db_ref, dc_ref, dh_ref, dz_ref, dw_ref, dg_ref):
        @pl.when(pl.program_id(1) == 0)
        def _():
            dw_ref[...] = jnp.zeros_like(dw_ref)
            dg_ref[...] = jnp.zeros_like(dg_ref)

        cb, c, h = b_ref[...].astype(F32), c_ref[...].astype(F32), h_ref[...].astype(F32)
        z, dy = z_ref[...].astype(F32), dy_ref[...].astype(F32)
        g = g_ref[...]
        w0, w1, w2 = w_ref[0:1, :], w_ref[1:2, :], w_ref[2:3, :]
        cc = c * h
        row = lax.broadcasted_iota(jnp.int32, (tp, LANES), 0)
        s1 = jnp.where(row >= 1, pltpu.roll(cc, 1, 0), 0.0)
        s2 = jnp.where(row >= 2, pltpu.roll(cc, 2, 0), 0.0)
        dwc = w0 * s2 + w1 * s1 + w2 * cc
        yc = cb * dwc
        r = lax.rsqrt(_group_mean(yc * yc) + EPS)
        ychat = yc * r
        sig = _sigmoid(z)
        dz_ref[...] = (dy * (ychat * g) * (sig * (1.0 + z * (1.0 - sig)))).astype(BF16)
        dyn = dy * (z * sig)
        dg_ref[...] += jnp.sum(dyn * ychat, axis=0, keepdims=True)
        gd = g * dyn
        dyc = r * (gd - ychat * _group_mean(gd * ychat))
        db_ref[...] = (dyc * dwc).astype(BF16)
        ddw = dyc * cb
        dw_ref[0:1, :] += jnp.sum(ddw * s2, axis=0, keepdims=True)
        dw_ref[1:2, :] += jnp.sum(ddw * s1, axis=0, keepdims=True)
        dw_ref[2:3, :] += jnp.sum(ddw * cc, axis=0, keepdims=True)
        u1 = jnp.where(row <= tp - 2, pltpu.roll(ddw, tp - 1, 0), 0.0)
        u2 = jnp.where(row <= tp - 3, pltpu.roll(ddw, tp - 2, 0), 0.0)
        dcc = w2 * ddw + w1 * u1 + w0 * u2
        dc_ref[...] = (dcc * h).astype(BF16)
        dh_ref[...] = (dcc * c).astype(BF16)

    col = pl.BlockSpec((tp, LANES), lambda t, b: (b, t))
    out = jax.ShapeDtypeStruct((nb_seq * tp, CONV_WIDTH), BF16)
    return pl.pallas_call(
        body,
        name="conv_bwd",
        grid=(cols, nb_seq),
        in_specs=_conv_specs(tp, lambda t, b, off: (b, off + t)) + [
            pl.BlockSpec((tp, LANES), lambda t, b: (b, N_HEADS * D_V // LANES + t)),
            pl.BlockSpec((8, LANES), lambda t, b: (0, t)),
            pl.BlockSpec((1, LANES), lambda t, b: (0, t))],
        out_specs=(col, col, col, col,
                   pl.BlockSpec((8, LANES), lambda t, b: (0, t)), pl.BlockSpec((1, LANES), lambda t, b: (0, t))),
        out_shape=(out, out, out, out,
                   jax.ShapeDtypeStruct((8, CONV_WIDTH), F32), jax.ShapeDtypeStruct((1, CONV_WIDTH), F32)),
        compiler_params=_params("arbitrary", "arbitrary"),
    )(p, p, p, p, dcat, conv_w, g_conv)


def _input_bwd(dps, w_in, x, meta, dh, norm_g, nt, send_in):
    nb_seq, s, d = x.shape
    r, kb = dps[0].shape
    ts = (s + LANES) // nt
    steps = nb_seq * nt
    n_dp = len(dps)
    in_slot = send_in.shape[1:]

    def body(*refs):
        dp_refs, w_ref, x_hbm, meta_ref, dh_ref, g_ref, pay_ref = refs[:n_dp], *refs[n_dp:n_dp + 6]
        o = n_dp + 6
        gx_hbm, dmeta_ref, dg_ref, r2_in = refs[o:o + 4]
        xbuf, gxbuf, tok_sems, own_in, r1_in, sum_in = refs[o + 4:o + 10]
        sems = refs[o + 10:]
        i = pl.program_id(0)
        b, k = i // nt, i % nt

        def plan():
            return _reduce_plan((pay_ref,), (own_in,), (r1_in,), (sum_in,), (r2_in,), *sems)

        @pl.when(i == 0)
        def _():
            dmeta_ref[...] = jnp.zeros_like(dmeta_ref)
            dg_ref[...] = jnp.zeros_like(dg_ref)
            plan()[0]()

        @pl.when(i == 1)
        def _():
            plan()[1]()

        def start(kk):
            if kk == 0:
                xbuf[0:PAD_FRONT, :] = jnp.zeros((PAD_FRONT, d), F32)
                xbuf[PAD_FRONT:LANES, :] = meta_ref[...]
            _token_copy(x_hbm, b, kk, ts, xbuf, tok_sems.at[0]).start()

        _for_tile(k, nt, start)
        du = _dot(dp_refs[0][...], w_ref[0:kb, :])
        for j in range(1, n_dp):
            du = du + _dot(dp_refs[j][...], w_ref[kb * j:kb * (j + 1), :])
        _for_tile(k, nt, lambda kk: _token_copy(x_hbm, b, kk, ts, xbuf, tok_sems.at[0]).wait())

        g = g_ref[...]
        hhat, rstd = _rms_stats(xbuf[...])
        dg_ref[...] += jnp.sum(du * hhat, axis=0, keepdims=True)
        res = _rms_bwd(g * du, hhat, rstd) + dh_ref[...]

        @pl.when(i > 0)
        def _():
            _for_tile(k, nt, lambda kk: _token_copy(gx_hbm, b, (kk - 1) % nt, ts, gxbuf, tok_sems.at[1], True).wait())

        gxbuf[...] = res

        @pl.when(k == 0)
        def _():
            dmeta_ref[...] += gxbuf[PAD_FRONT:LANES, :]

        _for_tile(k, nt, lambda kk: _token_copy(gx_hbm, b, kk, ts, gxbuf, tok_sems.at[1], True).start())

        @pl.when(i == steps - 1)
        def _():
            _token_copy(gx_hbm, b, nt - 1, ts, gxbuf, tok_sems.at[1], True).wait()
            plan()[2]()

    whole = lambda a: pl.BlockSpec(a.shape, lambda i: (0,) * a.ndim)
    hbm = pl.BlockSpec(memory_space=pl.ANY)
    return pl.pallas_call(
        body,
        name="input_bwd",
        grid=(steps,),
        in_specs=[pl.BlockSpec((ts, kb), lambda i: (i, 0)) for _ in dps]
        + [whole(w_in), hbm, whole(meta), pl.BlockSpec((ts, d), lambda i: (i, 0)), whole(norm_g), hbm],
        out_specs=(hbm, pl.BlockSpec((N_META, d), lambda i: (0, 0)), pl.BlockSpec((1, d), lambda i: (0, 0)), hbm),
        out_shape=(jax.ShapeDtypeStruct((nb_seq, s, d), F32),
                   jax.ShapeDtypeStruct((N_META, d), F32),
                   jax.ShapeDtypeStruct((1, d), F32),
                   jax.ShapeDtypeStruct((N_CHIPS,) + in_slot, BF16)),
        scratch_shapes=[pltpu.VMEM((ts, d), F32), pltpu.VMEM((ts, d), F32), pltpu.SemaphoreType.DMA((2,))]
        + _reduce_scratch([(in_slot, BF16)], [True]),
        compiler_params=_params("arbitrary"),
    )(*dps, w_in, x, meta, dh, norm_g, send_in)


def _in_proj_bwd_w(u, dps, bm, small_grads):
    r, d = u.shape
    kb = dps[0].shape[1]
    steps = r // bm
    n_dp, n_small = len(dps), len(small_grads)
    small_slot = (SMALL_ROWS, LANES)

    def body(*refs):
        u_ref, dp_refs = refs[0], refs[1:1 + n_dp]
        small_refs = refs[1 + n_dp:1 + n_dp + n_small]
        o = 1 + n_dp + n_small
        o_ref, r2_small = refs[o:o + 2]
        acc_ref, ssmall, r1_small, sum_small = refs[o + 2:o + 6]
        sems = refs[o + 6:]
        i = pl.program_id(0)

        def plan():
            return _reduce_plan((ssmall,), (None,), (r1_small,), (sum_small,), (r2_small,), *sems)

        @pl.when(i == 0)
        def _():
            acc_ref[...] = jnp.zeros_like(acc_ref)
            _pack_small(ssmall, *small_refs)
            plan()[0]()

        @pl.when(i == 1)
        def _():
            plan()[1]()

        uu = u_ref[...]
        for j in range(n_dp):
            acc_ref[kb * j:kb * (j + 1), :] += _dot(dp_refs[j][...], uu, _TN)

        @pl.when(i == steps - 1)
        def _():
            for k in range(N_DEV):
                for s, e, c0 in _in_pieces(k):
                    o_ref[k, s:e, :] = acc_ref[c0:c0 + e - s, :].astype(BF16)
                o_ref[k, SHARD_IN:, :] = jnp.zeros((SHARD_IN_PAD - SHARD_IN, d), BF16)
            plan()[2]()

    whole = lambda a: pl.BlockSpec(a.shape, lambda i: (0,) * a.ndim)
    return pl.pallas_call(
        body,
        name="in_proj_bwd_w",
        grid=(steps,),
        in_specs=[pl.BlockSpec((bm, d), lambda i: (i, 0))]
        + [pl.BlockSpec((bm, kb), lambda i: (i, 0)) for _ in dps] + [whole(a) for a in small_grads],
        out_specs=(pl.BlockSpec((N_DEV, SHARD_IN_PAD, d), lambda i: (0, 0, 0)), pl.BlockSpec(memory_space=pl.ANY)),
        out_shape=(jax.ShapeDtypeStruct((N_DEV, SHARD_IN_PAD, d), BF16),
                   jax.ShapeDtypeStruct((N_CHIPS,) + small_slot, F32)),
        scratch_shapes=[pltpu.VMEM((kb * n_dp, d), F32), pltpu.VMEM((N_DEV,) + small_slot, F32)]
        + _reduce_scratch([(small_slot, F32)], [False]),
        compiler_params=_params("arbitrary"),
    )(u, *dps, *small_grads)


def _local_step(x, loss_target, u, p, meta_f, norm_g, w_in_p, q_norm_g, w_q_p, kv_norm_g, w_kv_p, conv_w_f,
                attn_out_g, conv_out_g, w_out_f, g_final):
    nb_seq, s, d = x.shape
    tp = s + LANES
    ht = tp // 2
    tables = _rope_tables(tp)

    q, k, v = _qkv_fwd(p, w_q_p, w_kv_p, q_norm_g, kv_norm_g, tables, nb_seq, tp)
    ya, o, lse = _attn_fwd(q, k, v, p, attn_out_g)
    yc = _conv_fwd(p, conv_w_f, conv_out_g, nb_seq, tp)
    dh, dhb, d_final_g, loss_part = _out_proj_loss(ya, yc, w_out_f, x, loss_target, g_final, TOKEN_TILES)

    dcat, d_w_out = _out_proj_bwd(dhb, w_out_f, ya, yc, ht)
    send_out = d_w_out.reshape(N_DEV, SHARD_OUT, d)
    dq, dk, dv, dz_attn, d_attn_g, r_out = _attn_bwd(q, k, v, o, lse, dcat, p, attn_out_g, send_out)
    dpa, d_wq_p, d_wkv_p, d_gq, d_gkv = _qkv_bwd(p, dq, dk, dv, w_q_p, w_kv_p, q_norm_g, kv_norm_g, tables)
    d_b, d_c, d_h, dz_conv, d_conv_w, d_conv_g = _conv_bwd(p, dcat, conv_w_f, conv_out_g, nb_seq, tp)
    dps = (dpa, dz_attn, d_b, d_c, d_h, dz_conv)
    small = (d_wq_p, d_wkv_p, d_conv_w, d_final_g, d_gq, d_gkv, d_attn_g, d_conv_g, loss_part)
    send_in, r_small = _in_proj_bwd_w(u, dps, ht // 2, small)
    grad_x, d_meta, d_norm_g, r_in = _input_bwd(dps, w_in_p, x, meta_f, dh, norm_g, TOKEN_TILES, send_in)
    return grad_x, r_in, r_out, r_small, d_meta, d_norm_g


def kernel(x, meta_tokens, norm_g, w_in, q_norm_g, w_q_up, kv_norm_g, w_kv_up, conv_w, attn_out_g, conv_out_g, w_out, final_norm_g, loss_target, m_meta_tokens, m_norm_g, m_w_in, m_q_norm_g, m_w_q_up, m_kv_norm_g, m_w_kv_up, m_conv_w, m_attn_out_g, m_conv_out_g, m_w_out, m_final_norm_g, v_meta_tokens, v_norm_g, v_w_in, v_q_norm_g, v_w_q_up, v_kv_norm_g, v_w_kv_up, v_conv_w, v_attn_out_g, v_conv_out_g, v_w_out, v_final_norm_g):
    d = x.shape[-1]
    ht = (x.shape[1] + LANES) // 2
    u, w_in_p, meta_f = _prep_gather(x, meta_tokens, norm_g, w_in[0].T)
    p, w_q_p, w_kv_p, w_out_f, conv_w_f = _in_proj_gather(
        u, w_in_p, w_q_up[0], w_kv_up[0], w_out[0], conv_w[0], ht, GRP_A)
    g_final = final_norm_g.reshape(1, d)
    grad_x, r_in, r_out, r_small, d_meta, d_norm_g = _local_step(
        x, loss_target, u, p, meta_f, norm_g, w_in_p, q_norm_g, w_q_p, kv_norm_g, w_kv_p, conv_w_f,
        attn_out_g, conv_out_g, w_out_f, g_final)

    flat = lambda a: a.reshape(a.shape[-2:]) if a.ndim == 3 else a.reshape(1, -1) if a.ndim == 1 else a
    transposed = ("w_in",)
    to_kernel = lambda n, a: flat(a).T if n in transposed else flat(a)
    from_kernel = lambda n, a, shape: (a.T if n in transposed else a).reshape(shape)
    params = {
        "meta_tokens": (meta_tokens, m_meta_tokens, v_meta_tokens),
        "norm_g": (norm_g, m_norm_g, v_norm_g),
        "w_in": (w_in, m_w_in, v_w_in),
        "q_norm_g": (q_norm_g, m_q_norm_g, v_q_norm_g),
        "w_q_up": (w_q_up, m_w_q_up, v_w_q_up),
        "kv_norm_g": (kv_norm_g, m_kv_norm_g, v_kv_norm_g),
        "w_kv_up": (w_kv_up, m_w_kv_up, v_w_kv_up),
        "conv_w": (conv_w, m_conv_w, v_conv_w),
        "attn_out_g": (attn_out_g, m_attn_out_g, v_attn_out_g),
        "conv_out_g": (conv_out_g, m_conv_out_g, v_conv_out_g),
        "w_out": (w_out, m_w_out, v_w_out),
        "final_norm_g": (final_norm_g, m_final_norm_g, v_final_norm_g),
    }
    grads, loss = _reduce_tail(r_in, r_out, r_small, d_meta, d_norm_g)
    updated = _adamw(grads, {n: tuple(to_kernel(n, a) for a in t) for n, t in params.items()})
    outs = [[from_kernel(n, updated[n][i], params[n][0].shape) for n, _ in PARAM_SHAPES] for i in range(4)]
    return (loss[0, 0], grad_x, *outs[0], *outs[1], *outs[2], *outs[3])
```

```python
import functools

import jax
import jax.numpy as jnp
from jax import lax
from jax.experimental import pallas as pl
from jax.experimental.pallas import tpu as pltpu

F32 = jnp.float32
BF16 = jnp.bfloat16

N_META = 16
D_MODEL = 1024
N_HEADS = 4
D_NOPE = 128
D_ROPE = 64
D_V = 128
Q_RANK = 256
KV_RANK = 128
CONV_WIDTH = 512
CONV_GROUP = 64
ROPE_THETA = 10000.0
ATTN_SCALE = (D_NOPE + D_ROPE) ** -0.5
Q_SCALE = ATTN_SCALE * 1.4426950408889634
EPS = 1e-6
NEG_INF = -1e30

ADAM_LR = 0.001
ADAM_B1 = 0.9
ADAM_B2 = 0.999
ADAM_EPS = 1e-08
ADAM_WD = 0.01
ADAM_STEP = 10

LANES = 128
PAD_FRONT = LANES - N_META
KV_TILE = 256
N_DEV = 8
VMEM_LIMIT = 56 * 1024 * 1024

IN_PAD = 3072
GRP_A = 512
N_A = Q_RANK + KV_RANK + D_ROPE
IN_PROJ = 3008
SHARD_IN = IN_PROJ // N_DEV
SHARD_IN_PAD = 384
SHARD_Q = 96
SHARD_KV = 128
SHARD_OUT = 128
SHARD_CONV = 64
SHARD_META = 128
Q_COLS = N_HEADS * (D_NOPE + D_ROPE)
KV_COLS = N_HEADS * (D_NOPE + D_V)

ROW_Q, ROW_KV, ROW_META, ROW_CONV = 0, 256, 384, 400
ROW_REPL = 408
ROW_NORM, ROW_FINAL, ROW_GQ, ROW_GKV, ROW_ATTN, ROW_CONVG, ROW_LOSS = 408, 416, 424, 426, 427, 431, 435
SMALL_ROWS = 440

PARAM_SHAPES = (
    ("meta_tokens", (N_META, SHARD_META)), ("norm_g", (1, D_MODEL)), ("w_in", (SHARD_IN, D_MODEL)),
    ("q_norm_g", (1, Q_RANK)), ("w_q_up", (Q_RANK, SHARD_Q)), ("kv_norm_g", (1, KV_RANK)),
    ("w_kv_up", (KV_RANK, SHARD_KV)), ("conv_w", (3, SHARD_CONV)), ("attn_out_g", (1, CONV_WIDTH)),
    ("conv_out_g", (1, CONV_WIDTH)), ("w_out", (SHARD_OUT, D_MODEL)), ("final_norm_g", (1, D_MODEL)),
)


def _in_pieces(k):
    lo, hi = SHARD_IN * k, SHARD_IN * (k + 1)
    out = []
    if lo < N_A:
        out.append((0, min(hi, N_A) - lo, lo))
    if hi > N_A:
        s = max(lo, N_A)
        out.append((s - lo, hi - lo, s + GRP_A - N_A))
    return out


def _q_pieces(k):
    lo, hi = SHARD_Q * k, SHARD_Q * (k + 1)
    out = []
    for h in range(N_HEADS):
        base = (D_NOPE + D_ROPE) * h
        s, e = max(lo, base), min(hi, base + D_NOPE)
        if s < e:
            out.append((s - lo, e - lo, D_NOPE * h + s - base))
        s, e = max(lo, base + D_NOPE), min(hi, base + D_NOPE + D_ROPE)
        if s < e:
            out.append((s - lo, e - lo, N_HEADS * D_NOPE + D_ROPE * h + s - base - D_NOPE))
    return out


def _kv_dst(k):
    return D_NOPE * (k // 2) + (N_HEADS * D_NOPE if k % 2 else 0)


def _params(*sem):
    return pltpu.CompilerParams(dimension_semantics=sem, vmem_limit_bytes=VMEM_LIMIT)


def _rms_stats(x):
    r = lax.rsqrt(jnp.mean(x * x, axis=-1, keepdims=True) + EPS)
    return x * r, r


def _rms_bwd(gdy, xhat, r):
    return r * (gdy - xhat * jnp.mean(gdy * xhat, axis=-1, keepdims=True))


def _sigmoid(z):
    return 1.0 / (1.0 + jnp.exp(-z))


def _group_mean(x):
    i0 = lax.broadcasted_iota(jnp.int32, (LANES, LANES), 0) // CONV_GROUP
    i1 = lax.broadcasted_iota(jnp.int32, (LANES, LANES), 1) // CONV_GROUP
    m = jnp.where(i0 == i1, 1.0 / CONV_GROUP, 0.0).astype(BF16)
    hi = x.astype(BF16)
    lo = (x - hi.astype(F32)).astype(BF16)
    return jnp.dot(hi, m, preferred_element_type=F32) + jnp.dot(lo, m, preferred_element_type=F32)


_NT = (((1,), (1,)), ((), ()))
_TN = (((0,), (0,)), ((), ()))


def _dot(a, b, dims=None):
    if dims is None:
        return jnp.dot(a, b, preferred_element_type=F32)
    return lax.dot_general(a, b, dims, preferred_element_type=F32)


def _device_position():
    x, y, c = lax.axis_index("x"), lax.axis_index("y"), lax.axis_index("c")
    return x, y, c, 4 * x + 2 * y + c


def _gather_plan(srcs, slots, send_sems, recv_sems, local_sems):
    x, y, c, _ = _device_position()
    me, sibling = (x, y, c), (x, y, 1 - c)
    chips = [(1 - x, y), (x, 1 - y), (1 - x, 1 - y)]
    n = len(srcs)

    def slot(a, px, py, pc):
        return slots[a].at[4 * px + 2 * py + pc]

    def copy(a, k, block, to, own=False):
        return pltpu.make_async_remote_copy(
            src_ref=srcs[a] if own else slot(a, *block),
            dst_ref=slot(a, *block),
            send_sem=send_sems.at[7 * a + k],
            recv_sem=recv_sems.at[7 * a + k],
            device_id=to,
            device_id_type=pl.DeviceIdType.MESH,
        )

    def local(a):
        return pltpu.make_async_copy(srcs[a], slot(a, *me), local_sems.at[a])

    def firsts(a):
        return [copy(a, 0, me, sibling, own=True)] + [
            copy(a, 1 + j, me, (*chip, c), own=True) for j, chip in enumerate(chips)]

    def start():
        for a in range(n):
            local(a).start()
            for cp in firsts(a):
                cp.start()

    def forward():
        for j, chip in enumerate(chips):
            for a in range(n):
                copy(a, 1 + j, (*chip, c), me).wait_recv()
                copy(a, 4 + j, (*chip, c), sibling).start()

    def finish():
        for a in range(n):
            copy(a, 0, sibling, me).wait_recv()
            for j, chip in enumerate(chips):
                copy(a, 4 + j, (*chip, 1 - c), me).wait_recv()
        for a in range(n):
            for cp in firsts(a) + [copy(a, 4 + j, (*chip, c), sibling) for j, chip in enumerate(chips)]:
                cp.wait_send()
            local(a).wait()

    return start, forward, finish


def _adam_update(g, w, m, v):
    m_new = ADAM_B1 * m + (1.0 - ADAM_B1) * g
    v_new = ADAM_B2 * v + (1.0 - ADAM_B2) * (g * g)
    m_hat = m_new / (1.0 - ADAM_B1 ** ADAM_STEP)
    v_hat = v_new / (1.0 - ADAM_B2 ** ADAM_STEP)
    return -ADAM_LR * (m_hat / (jnp.sqrt(v_hat) + ADAM_EPS) + ADAM_WD * w), m_new, v_new


def _adamw(grads, params):
    names = [n for n, _ in PARAM_SHAPES]
    n_p = len(names)

    def body(*refs):
        for i in range(n_p):
            g = refs[i][...]
            w, m, v = (refs[n_p + 3 * i + j][...] for j in range(3))
            delta, m_new, v_new = _adam_update(g, w, m, v)
            for j, val in enumerate((g, delta, m_new, v_new)):
                refs[4 * n_p + 4 * i + j][...] = val

    vm = pl.BlockSpec(memory_space=pltpu.VMEM)
    out_shape = []
    for _, shape in PARAM_SHAPES:
        out_shape += [jax.ShapeDtypeStruct(shape, F32)] * 4
    outs = pl.pallas_call(
        body,
        name="adamw",
        out_shape=tuple(out_shape),
        in_specs=[vm] * (4 * n_p),
        out_specs=(vm,) * (4 * n_p),
        compiler_params=pltpu.CompilerParams(vmem_limit_bytes=VMEM_LIMIT),
    )(*[grads[n] for n in names], *[a for n in names for a in params[n]])
    return {n: outs[4 * i:4 * i + 4] for i, n in enumerate(names)}


N_CHIPS = 4


def _reduce_plan(pays, owns, r1s, sums, r2s, send1, recv1, send2, recv2, local_sems):
    x, y, c, _ = _device_position()
    sibling = (x, y, 1 - c)
    chips = [((1 - x if rj & 2 else x), (1 - y if rj & 1 else y)) for rj in range(N_CHIPS)]
    n = len(pays)

    def slot_of(rj, core):
        return 4 * chips[rj][0] + 2 * chips[rj][1] + core

    def to_sibling(a, rj):
        return pltpu.make_async_remote_copy(
            src_ref=pays[a].at[slot_of(rj, 1 - c)], dst_ref=r1s[a].at[rj],
            send_sem=send1.at[N_CHIPS * a + rj], recv_sem=recv1.at[N_CHIPS * a + rj],
            device_id=sibling, device_id_type=pl.DeviceIdType.MESH)

    def load_own(a, rj):
        return pltpu.make_async_copy(pays[a].at[slot_of(rj, c)], owns[a].at[rj], local_sems.at[2 * N_CHIPS * a + rj])

    def to_chip(a, rj):
        return pltpu.make_async_remote_copy(
            src_ref=sums[a].at[rj], dst_ref=r2s[a].at[rj],
            send_sem=send2.at[N_CHIPS * a + rj], recv_sem=recv2.at[N_CHIPS * a + rj],
            device_id=(*chips[rj], c), device_id_type=pl.DeviceIdType.MESH)

    def keep(a):
        return pltpu.make_async_copy(sums[a].at[0], r2s[a].at[0], local_sems.at[2 * N_CHIPS * a + N_CHIPS])

    def start():
        for a in range(n):
            for rj in range(N_CHIPS):
                to_sibling(a, rj).start()
                if owns[a] is not None:
                    load_own(a, rj).start()

    def combine():
        for a in range(n):
            for rj in range(N_CHIPS):
                to_sibling(a, rj).wait_recv()
                if owns[a] is not None:
                    load_own(a, rj).wait()
                    mine = owns[a][rj]
                else:
                    mine = pays[a][slot_of(rj, c)]
                sums[a][rj] = (mine.astype(F32) + r1s[a][rj].astype(F32)).astype(sums[a].dtype)
            keep(a).start()
            for rj in range(1, N_CHIPS):
                to_chip(a, rj).start()

    def finish():
        for a in range(n):
            for rj in range(1, N_CHIPS):
                to_chip(a, rj).wait_recv()
            for rj in range(N_CHIPS):
                to_sibling(a, rj).wait_send()
            for rj in range(1, N_CHIPS):
                to_chip(a, rj).wait_send()
            keep(a).wait()

    return start, combine, finish


def _reduce_scratch(shapes_dtypes, own_flags):
    out = []
    for (shape, dtype), own in zip(shapes_dtypes, own_flags):
        if own:
            out.append(pltpu.VMEM((N_CHIPS,) + shape, dtype))
        out += [pltpu.VMEM((N_CHIPS,) + shape, dtype), pltpu.VMEM((N_CHIPS,) + shape, dtype)]
    n = len(shapes_dtypes)
    out += [pltpu.SemaphoreType.DMA((N_CHIPS * n,))] * 4 + [pltpu.SemaphoreType.DMA((2 * N_CHIPS * n,))]
    return out


def _pack_small(ssmall, dwq, dwkv, dconv, dfinal, dgq, dgkv, dattn, dconvg, loss_part):
    ssmall[...] = jnp.zeros_like(ssmall)
    rep = ssmall.at[0]
    for i in range(D_MODEL // LANES):
        rep[ROW_FINAL + i:ROW_FINAL + i + 1, :] = dfinal[:, LANES * i:LANES * (i + 1)]
    for i in range(Q_RANK // LANES):
        rep[ROW_GQ + i:ROW_GQ + i + 1, :] = dgq[:, LANES * i:LANES * (i + 1)]
    rep[ROW_GKV:ROW_GKV + 1, :] = dgkv[...]
    for i in range(CONV_WIDTH // LANES):
        rep[ROW_ATTN + i:ROW_ATTN + i + 1, :] = dattn[:, LANES * i:LANES * (i + 1)]
        rep[ROW_CONVG + i:ROW_CONVG + i + 1, :] = dconvg[:, LANES * i:LANES * (i + 1)]
    rep[ROW_LOSS:ROW_LOSS + 1, :] = loss_part[...]
    for k in range(N_DEV):
        if k:
            ssmall[k, ROW_REPL:, :] = ssmall[0, ROW_REPL:, :]
        for s, e, d in _q_pieces(k):
            ssmall[k, ROW_Q:ROW_Q + Q_RANK, s:e] = dwq[:, d:d + e - s]
        ssmall[k, ROW_KV:ROW_KV + KV_RANK, :] = dwkv[:, _kv_dst(k):_kv_dst(k) + SHARD_KV]
        ssmall[k, ROW_CONV:ROW_CONV + 3, 0:SHARD_CONV] = dconv[0:3, SHARD_CONV * k:SHARD_CONV * (k + 1)]


TOKEN_TILES = 4
TAIL_ROWS = N_META + D_MODEL // LANES


def _reduce_tail(r_in, r_out, r_small, d_meta, d_norm):
    n_p = len(PARAM_SHAPES)
    names = [n for n, _ in PARAM_SHAPES]

    def body(*refs):
        rin, rout, rsmall, dmeta, dnorm = refs[:5]
        g_out = {n: refs[5 + i] for i, n in enumerate(names)}
        loss_out = refs[5 + n_p]
        stail, rtail, gsum, gtail, send_sems, recv_sems = refs[6 + n_p:]
        x, y, c, me = _device_position()
        my_chip = 2 * x + y

        for k in range(N_DEV):
            stail[k, 0:N_META, :] = dmeta[:, SHARD_META * k:SHARD_META * (k + 1)]
            for i in range(D_MODEL // LANES):
                stail[k, N_META + i:N_META + i + 1, :] = dnorm[:, LANES * i:LANES * (i + 1)]
        copies = []
        for r in range(1, N_DEV):
            peer = (1 - x if r & 4 else x, 1 - y if r & 2 else y, 1 - c if r & 1 else c)
            copies.append(pltpu.make_async_remote_copy(
                src_ref=stail.at[4 * peer[0] + 2 * peer[1] + peer[2]],
                dst_ref=rtail.at[r],
                send_sem=send_sems.at[r - 1],
                recv_sem=recv_sems.at[r - 1],
                device_id=peer,
                device_id_type=pl.DeviceIdType.MESH,
            ))
        for cp in copies:
            cp.start()
        rtail[0] = stail[me]

        g = rin[my_chip].astype(F32)
        for ch in range(1, N_CHIPS):
            g = g + rin[ch ^ my_chip].astype(F32)
        g_out["w_in"][...] = g[:SHARD_IN, :]

        g = rout[my_chip].astype(F32)
        gs = rsmall[my_chip]
        for ch in range(1, N_CHIPS):
            g = g + rout[ch ^ my_chip].astype(F32)
            gs = gs + rsmall[ch ^ my_chip]
        g_out["w_out"][...] = g
        gsum[...] = gs
        g_out["w_q_up"][...] = gsum[ROW_Q:ROW_Q + Q_RANK, 0:SHARD_Q]
        g_out["w_kv_up"][...] = gsum[ROW_KV:ROW_KV + KV_RANK, :]
        g_out["conv_w"][...] = gsum[ROW_CONV:ROW_CONV + 3, 0:SHARD_CONV]
        for name, row, width in (("final_norm_g", ROW_FINAL, D_MODEL), ("q_norm_g", ROW_GQ, Q_RANK),
                                 ("kv_norm_g", ROW_GKV, KV_RANK), ("attn_out_g", ROW_ATTN, CONV_WIDTH),
                                 ("conv_out_g", ROW_CONVG, CONV_WIDTH)):
            for i in range(width // LANES):
                g_out[name][:, LANES * i:LANES * (i + 1)] = gsum[row + i:row + i + 1, :]
        loss_out[...] = gsum[ROW_LOSS:ROW_LOSS + 1, :]

        for cp in copies:
            cp.wait_recv()
        gt = rtail[me]
        for d in range(1, N_DEV):
            gt = gt + rtail[d ^ me]
        gtail[...] = gt
        g_out["meta_tokens"][...] = gtail[0:N_META, :]
        for i in range(D_MODEL // LANES):
            g_out["norm_g"][:, LANES * i:LANES * (i + 1)] = gtail[N_META + i:N_META + i + 1, :]
        for cp in copies:
            cp.wait_send()

    vm = pl.BlockSpec(memory_space=pltpu.VMEM)
    out_shape = [jax.ShapeDtypeStruct(shape, F32) for _, shape in PARAM_SHAPES]
    out_shape.append(jax.ShapeDtypeStruct((1, LANES), F32))
    outs = pl.pallas_call(
        body,
        name="reduce_tail",
        out_shape=tuple(out_shape),
        in_specs=[vm] * 5,
        out_specs=(vm,) * len(out_shape),
        scratch_shapes=[
            pltpu.VMEM((N_DEV, TAIL_ROWS, LANES), F32),
            pltpu.VMEM((N_DEV, TAIL_ROWS, LANES), F32),
            pltpu.VMEM((SMALL_ROWS, LANES), F32),
            pltpu.VMEM((TAIL_ROWS, LANES), F32),
            pltpu.SemaphoreType.DMA((N_DEV - 1,)),
            pltpu.SemaphoreType.DMA((N_DEV - 1,)),
        ],
        compiler_params=pltpu.CompilerParams(vmem_limit_bytes=VMEM_LIMIT),
    )(r_in, r_out, r_small, d_meta, d_norm)
    return {n: outs[i] for i, n in enumerate(names)}, outs[-1]


def _prep_gather(x, meta, norm_g, w_in_t):
    nb_seq, s, d = x.shape
    nb = s // LANES + 1
    forward_step = nb_seq * (nb - 1) - 1
    finish_step = nb_seq * (nb - 1)

    def body(x_ref, meta_ref, g_ref, win_ref, u_ref, w_in_p, meta_f,
             sbig, ssmall, gbig, gsmall, send_sems, recv_sems, local_sems):
        jj, b = pl.program_id(0), pl.program_id(1)
        t = jj * nb_seq + b

        def plan():
            return _gather_plan((sbig, ssmall), (gbig, gsmall), send_sems, recv_sems, local_sems)

        @pl.when(t == 0)
        def _():
            sbig[0:SHARD_IN, :] = win_ref[...].astype(BF16)
            sbig[SHARD_IN:, :] = jnp.zeros((SHARD_IN_PAD - SHARD_IN, d), BF16)
            ssmall[...] = meta_ref[...]
            plan()[0]()

        @pl.when(t == forward_step)
        def _():
            plan()[1]()

        @pl.when(t == finish_step)
        def _():
            plan()[2]()
            w_in_p[N_A:GRP_A, :] = jnp.zeros((GRP_A - N_A, d), BF16)
            for k in range(N_DEV):
                for s0, e0, d0 in _in_pieces(k):
                    w_in_p[d0:d0 + e0 - s0, :] = gbig[k, s0:e0, :]
                meta_f[:, SHARD_META * k:SHARD_META * (k + 1)] = gsmall[k]

        def norm(h):
            hhat, _ = _rms_stats(h)
            return (hhat * g_ref[...]).astype(BF16)

        @pl.when(jj < nb - 1)
        def _():
            u_ref[...] = norm(x_ref[0])

        @pl.when(jj == nb - 1)
        def _():
            u_ref[0:PAD_FRONT, :] = jnp.zeros((PAD_FRONT, d), BF16)
            u_ref[PAD_FRONT:LANES, :] = norm(meta_f[...])

    whole = lambda shape: pl.BlockSpec(shape, lambda jj, b: (0,) * len(shape))
    return pl.pallas_call(
        body,
        name="prep_norm_gather",
        grid=(nb, nb_seq),
        in_specs=[
            pl.BlockSpec((1, LANES, d), lambda jj, b: (b, jnp.minimum(jj, nb - 2), 0)),
            whole(meta.shape), whole(norm_g.shape), whole(w_in_t.shape),
        ],
        out_specs=(pl.BlockSpec((LANES, d), lambda jj, b: (b * nb + (jj + 1) % nb, 0)),
                   whole((IN_PAD, d)), whole((N_META, d))),
        out_shape=(jax.ShapeDtypeStruct((nb_seq * nb * LANES, d), BF16),
                   jax.ShapeDtypeStruct((IN_PAD, d), BF16),
                   jax.ShapeDtypeStruct((N_META, d), F32)),
        scratch_shapes=[
            pltpu.VMEM((SHARD_IN_PAD, d), BF16),
            pltpu.VMEM((N_META, SHARD_META), F32),
            pltpu.VMEM((N_DEV, SHARD_IN_PAD, d), BF16),
            pltpu.VMEM((N_DEV, N_META, SHARD_META), F32),
            pltpu.SemaphoreType.DMA((14,)),
            pltpu.SemaphoreType.DMA((14,)),
            pltpu.SemaphoreType.DMA((2,)),
        ],
        compiler_params=_params("arbitrary", "arbitrary"),
    )(x, meta, norm_g, w_in_t)


def _in_proj_gather(u, w_in_p, w_q, w_kv, w_out, conv_w, bm, bn):
    m, k_dim = u.shape
    n = w_in_p.shape[0]
    steps = (m // bm) * (n // bn)
    forward_step = steps // 2
    qkv_rows = Q_RANK + KV_RANK

    def body(a_ref, b_ref, wq_ref, wkv_ref, wout_ref, conv_ref, o_ref, w_q_p, w_kv_p, w_out_f, conv_f,
             sqkv, sout, sconv, gqkv, gout, gconv, send_sems, recv_sems, local_sems):
        t = pl.program_id(0) * (n // bn) + pl.program_id(1)

        def plan():
            return _gather_plan((sqkv, sout, sconv), (gqkv, gout, gconv), send_sems, recv_sems, local_sems)

        @pl.when(t == 0)
        def _():
            sqkv[...] = jnp.zeros_like(sqkv)
            sqkv[0:Q_RANK, 0:SHARD_Q] = wq_ref[...].astype(BF16)
            sqkv[Q_RANK:, :] = wkv_ref[...].astype(BF16)
            sout[...] = wout_ref[...].astype(BF16)
            sconv[...] = jnp.zeros_like(sconv)
            sconv[0:3, 0:SHARD_CONV] = conv_ref[...]
            plan()[0]()

        @pl.when(t == forward_step)
        def _():
            plan()[1]()

        o_ref[...] = _dot(a_ref[...], b_ref[...], _NT).astype(o_ref.dtype)

        @pl.when(t == steps - 1)
        def _():
            plan()[2]()
            conv_f[...] = jnp.zeros_like(conv_f)
            for k in range(N_DEV):
                for s0, e0, d0 in _q_pieces(k):
                    w_q_p[:, d0:d0 + e0 - s0] = gqkv[k, 0:Q_RANK, s0:e0]
                w_kv_p[:, _kv_dst(k):_kv_dst(k) + SHARD_KV] = gqkv[k, Q_RANK:, :]
                w_out_f[SHARD_OUT * k:SHARD_OUT * (k + 1), :] = gout[k]
                conv_f[0:3, SHARD_CONV * k:SHARD_CONV * (k + 1)] = gconv[k, 0:3, 0:SHARD_CONV]

    whole = lambda shape: pl.BlockSpec(shape, lambda i, j: (0,) * len(shape))
    return pl.pallas_call(
        body,
        name="in_proj_gather",
        grid=(m // bm, n // bn),
        in_specs=[pl.BlockSpec((bm, k_dim), lambda i, j: (i, 0)), pl.BlockSpec((bn, k_dim), lambda i, j: (j, 0)),
                  whole(w_q.shape), whole(w_kv.shape), whole(w_out.shape), whole(conv_w.shape)],
        out_specs=(pl.BlockSpec((bm, bn), lambda i, j: (i, j)),
                   whole((Q_RANK, Q_COLS)), whole((KV_RANK, KV_COLS)), whole((D_MODEL, D_MODEL)),
                   whole((8, CONV_WIDTH))),
        out_shape=(jax.ShapeDtypeStruct((m, n), BF16),
                   jax.ShapeDtypeStruct((Q_RANK, Q_COLS), BF16),
                   jax.ShapeDtypeStruct((KV_RANK, KV_COLS), BF16),
                   jax.ShapeDtypeStruct((D_MODEL, D_MODEL), BF16),
                   jax.ShapeDtypeStruct((8, CONV_WIDTH), F32)),
        scratch_shapes=[
            pltpu.VMEM((qkv_rows, LANES), BF16),
            pltpu.VMEM((SHARD_OUT, D_MODEL), BF16),
            pltpu.VMEM((8, LANES), F32),
            pltpu.VMEM((N_DEV, qkv_rows, LANES), BF16),
            pltpu.VMEM((N_DEV, SHARD_OUT, D_MODEL), BF16),
            pltpu.VMEM((N_DEV, 8, LANES), F32),
            pltpu.SemaphoreType.DMA((21,)),
            pltpu.SemaphoreType.DMA((21,)),
            pltpu.SemaphoreType.DMA((3,)),
        ],
        compiler_params=_params("arbitrary", "arbitrary"),
    )(u, w_in_p, w_q, w_kv, w_out, conv_w)


def _rope_tables(tp):
    half = D_ROPE // 2
    inv_freq = 1.0 / (ROPE_THETA ** (jnp.arange(half, dtype=F32) / half))
    pos = (jnp.arange(tp) - PAD_FRONT).astype(F32)
    ang = pos[:, None] * inv_freq[None, :]
    cos = jnp.tile(jnp.cos(ang), (1, LANES // half))
    sin = jnp.tile(jnp.sin(ang), (1, LANES // half))
    first = (jnp.arange(LANES) % D_ROPE) < half
    return cos, jnp.where(first, -sin, 0.0), jnp.where(first, 0.0, sin)


def _rope(t, cos, sa, sb):
    return t * cos + pltpu.roll(t, LANES - D_ROPE // 2, 1) * sa + pltpu.roll(t, D_ROPE // 2, 1) * sb


def _rope_t(t, cos, sa, sb):
    return t * cos + pltpu.roll(t * sa, D_ROPE // 2, 1) + pltpu.roll(t * sb, LANES - D_ROPE // 2, 1)


def _qkv_fwd(p, wq, wkv, gq, gkv, tables, nb_seq, tp):
    ht = tp // 2

    def body(pa_ref, wq_ref, wkv_ref, gq_ref, gkv_ref, cos_ref, sa_ref, sb_ref, q_ref, k_ref, v_ref):
        pa = pa_ref[...].astype(F32)
        cq_hat, _ = _rms_stats(pa[:, :Q_RANK])
        ckv_hat, _ = _rms_stats(pa[:, Q_RANK:Q_RANK + KV_RANK])
        q = _dot((cq_hat * gq_ref[...]).astype(BF16), wq_ref[...]) * Q_SCALE
        kv = _dot((ckv_hat * gkv_ref[...]).astype(BF16), wkv_ref[...])
        tabs = (cos_ref[...], sa_ref[...], sb_ref[...])
        lane = lax.broadcasted_iota(jnp.int32, (ht, LANES), 1)
        low = lane < D_ROPE
        mark = lane == D_ROPE
        row = (pl.program_id(0) % 2) * ht + lax.broadcasted_iota(jnp.int32, (ht, LANES), 0)
        k_pe = jnp.where(mark & (row < PAD_FRONT), NEG_INF, _rope(pa[:, Q_RANK + KV_RANK:], *tabs))
        one = jnp.where(mark & (row >= PAD_FRONT), 1.0, 0.0)
        pairs = [_rope(q[:, N_HEADS * D_NOPE + LANES * i:N_HEADS * D_NOPE + LANES * (i + 1)], *tabs) for i in range(2)]
        for h in range(N_HEADS):
            pair = pairs[h // 2]
            if h % 2:
                pair = pltpu.roll(pair, D_ROPE, 1)
            pe = jnp.where(low, pair, one)
            q_ref[0, h] = jnp.concatenate([q[:, D_NOPE * h:D_NOPE * (h + 1)], pe], axis=1).astype(BF16)
            k_ref[0, h] = jnp.concatenate([kv[:, D_NOPE * h:D_NOPE * (h + 1)], k_pe], axis=1).astype(BF16)
            v_ref[0, h] = kv[:, N_HEADS * D_NOPE + D_V * h:N_HEADS * D_NOPE + D_V * (h + 1)].astype(BF16)

    full = lambda a: pl.BlockSpec(a.shape, lambda i: (0,) * a.ndim)
    tab = pl.BlockSpec((ht, LANES), lambda i: (i % 2, 0))
    qk = pl.BlockSpec((1, N_HEADS, ht, 2 * LANES), lambda i: (i // 2, 0, i % 2, 0))
    return pl.pallas_call(
        body,
        name="qkv_fwd",
        grid=(2 * nb_seq,),
        in_specs=[pl.BlockSpec((ht, GRP_A), lambda i: (i, 0)), full(wq), full(wkv), full(gq), full(gkv), tab, tab, tab],
        out_specs=(qk, qk, pl.BlockSpec((1, N_HEADS, ht, D_V), lambda i: (i // 2, 0, i % 2, 0))),
        out_shape=(
            jax.ShapeDtypeStruct((nb_seq, N_HEADS, tp, 2 * LANES), BF16),
            jax.ShapeDtypeStruct((nb_seq, N_HEADS, tp, 2 * LANES), BF16),
            jax.ShapeDtypeStruct((nb_seq, N_HEADS, tp, D_V), BF16),
        ),
        compiler_params=_params("parallel"),
    )(p, wq, wkv, gq, gkv, *tables)


def _attn_fwd(q, k, v, p, g_attn):
    nb_seq, _, tp, _ = q.shape

    def body(q_ref, k_ref, v_ref, z_ref, g_ref, y_ref, o_ref, lse_ref):
        g = g_ref[...]
        for r0 in range(0, tp, KV_TILE):
            nq = min(KV_TILE, tp - r0)
            kend = r0 + nq
            qq = q_ref[0, 0, r0:kend, :]
            sd = _dot(qq, k_ref[0, 0, r0:kend, :], _NT)
            causal = (lax.broadcasted_iota(jnp.int32, (nq, nq), 1) <= lax.broadcasted_iota(jnp.int32, (nq, nq), 0))
            sd = jnp.where(causal, sd, NEG_INF)
            m = jnp.max(sd, axis=-1, keepdims=True)
            if r0:
                so = _dot(qq, k_ref[0, 0, 0:r0, :], _NT)
                m = jnp.maximum(m, jnp.max(so, axis=-1, keepdims=True))
            ed = jnp.exp2(sd - m)
            l = jnp.sum(ed, axis=-1, keepdims=True)
            o = _dot(ed.astype(BF16), v_ref[0, 0, r0:kend, :])
            if r0:
                eo = jnp.exp2(so - m)
                l = l + jnp.sum(eo, axis=-1, keepdims=True)
                o = o + _dot(eo.astype(BF16), v_ref[0, 0, 0:r0, :])
            o = o * (1.0 / l)
            o_ref[0, 0, r0:kend, :] = o
            lse_ref[0, 0, r0:kend, :] = jnp.broadcast_to(m + jnp.log2(l), (nq, LANES))
            ohat, _ = _rms_stats(o)
            z = z_ref[r0:kend, :].astype(F32)
            y_ref[r0:kend, :] = (ohat * g * (z * _sigmoid(z))).astype(BF16)

    qk = pl.BlockSpec((1, 1, tp, 2 * LANES), lambda b, h: (b, h, 0, 0))
    hv = pl.BlockSpec((1, 1, tp, D_V), lambda b, h: (b, h, 0, 0))
    return pl.pallas_call(
        body,
        name="attn_fwd",
        grid=(nb_seq, N_HEADS),
        in_specs=[qk, qk, hv,
                  pl.BlockSpec((tp, LANES), lambda b, h: (b, GRP_A // LANES + h)),
                  pl.BlockSpec((1, LANES), lambda b, h: (0, h))],
        out_specs=(pl.BlockSpec((tp, LANES), lambda b, h: (b, h)), hv, hv),
        out_shape=(
            jax.ShapeDtypeStruct((nb_seq * tp, N_HEADS * D_V), BF16),
            jax.ShapeDtypeStruct((nb_seq, N_HEADS, tp, D_V), F32),
            jax.ShapeDtypeStruct((nb_seq, N_HEADS, tp, LANES), F32),
        ),
        compiler_params=_params("parallel", "parallel"),
    )(q, k, v, p, g_attn)


_CONV_COL0 = (GRP_A + N_HEADS * D_V) // LANES


def _conv_specs(tp, order):
    cols = CONV_WIDTH // LANES
    return [pl.BlockSpec((tp, LANES), functools.partial(
        lambda a, b, off: order(a, b, off), off=_CONV_COL0 + i * cols)) for i in range(4)]


def _conv_fwd(p, conv_w, g_conv, nb_seq, tp):
    def body(b_ref, c_ref, h_ref, z_ref, w_ref, g_ref, y_ref):
        cc = c_ref[...].astype(F32) * h_ref[...].astype(F32)
        row = lax.broadcasted_iota(jnp.int32, (tp, LANES), 0)
        s1 = jnp.where(row >= 1, pltpu.roll(cc, 1, 0), 0.0)
        s2 = jnp.where(row >= 2, pltpu.roll(cc, 2, 0), 0.0)
        yc = b_ref[...].astype(F32) * (w_ref[0:1, :] * s2 + w_ref[1:2, :] * s1 + w_ref[2:3, :] * cc)
        r = lax.rsqrt(_group_mean(yc * yc) + EPS)
        z = z_ref[...].astype(F32)
        y_ref[...] = (yc * r * g_ref[...] * (z * _sigmoid(z))).astype(BF16)

    return pl.pallas_call(
        body,
        name="conv_fwd",
        grid=(nb_seq, CONV_WIDTH // LANES),
        in_specs=_conv_specs(tp, lambda b, t, off: (b, off + t)) + [
            pl.BlockSpec((8, LANES), lambda b, t: (0, t)),
            pl.BlockSpec((1, LANES), lambda b, t: (0, t))],
        out_specs=pl.BlockSpec((tp, LANES), lambda b, t: (b, t)),
        out_shape=jax.ShapeDtypeStruct((nb_seq * tp, CONV_WIDTH), BF16),
        compiler_params=_params("parallel", "parallel"),
    )(p, p, p, p, conv_w, g_conv)


def _token_copy(hbm, b, k, ts, buf, sem, to_hbm=False):
    lo, hi = max(k * ts - LANES, 0), (k + 1) * ts - LANES
    off = lo - (k * ts - LANES)
    src, dst = hbm.at[b, pl.ds(lo, hi - lo)], buf.at[pl.ds(off, hi - lo)]
    if to_hbm:
        src, dst = dst, src
    return pltpu.make_async_copy(src, dst, sem)


def _for_tile(k, nt, fn):
    for kk in range(nt):
        @pl.when(k == kk)
        def _(kk=kk):
            fn(kk)


def _out_proj_loss(ya, yc, w_out, x, target, g_final, nt):
    nb_seq, s, d = x.shape
    r, ka = ya.shape
    ts = (s + LANES) // nt
    steps = nb_seq * nt

    def body(a_ref, c_ref, w_ref, x_hbm, t_hbm, g_ref, dh_ref, dhb_ref, dg_ref, loss_ref,
             xbuf, tbuf, acc_ref, sems):
        i = pl.program_id(0)
        b, k = i // nt, i % nt

        @pl.when(i == 0)
        def _():
            acc_ref[...] = jnp.zeros_like(acc_ref)
            dg_ref[...] = jnp.zeros_like(dg_ref)

        slot = i % 2

        def fetch(seq, kk, sl):
            return [_token_copy(x_hbm, seq, kk, ts, xbuf.at[sl], sems.at[sl, 0]),
                    _token_copy(t_hbm, seq, kk, ts, tbuf.at[sl], sems.at[sl, 1])]

        def start(seq, sl, kk):
            if kk == 0:
                xbuf[sl, 0:LANES, :] = jnp.zeros((LANES, d), F32)
                tbuf[sl, 0:LANES, :] = jnp.zeros((LANES, d), F32)
            for cp in fetch(seq, kk, sl):
                cp.start()

        @pl.when(i == 0)
        def _():
            start(0, 0, 0)

        @pl.when(i + 1 < steps)
        def _():
            _for_tile((i + 1) % nt, nt, functools.partial(start, (i + 1) // nt, 1 - slot))

        mix = _dot(a_ref[...], w_ref[0:ka, :]) + _dot(c_ref[...], w_ref[ka:, :])
        _for_tile(k, nt, lambda kk: [cp.wait() for cp in fetch(b, kk, slot)])

        real = (lax.broadcasted_iota(jnp.int32, (ts, d), 0) >= LANES) | (k > 0)
        g = g_ref[...]
        hhat, rstd = _rms_stats(xbuf[slot] + mix)
        e = jnp.where(real, hhat * g - tbuf[slot], 0.0)
        acc_ref[...] += jnp.sum(e * e, axis=0, keepdims=True)
        dy = e * (1.0 / d)
        dg_ref[...] += jnp.sum(dy * hhat, axis=0, keepdims=True)
        dh = _rms_bwd(g * dy, hhat, rstd)
        dh_ref[...] = dh
        dhb_ref[...] = dh.astype(BF16)

        @pl.when(i == steps - 1)
        def _():
            total = jnp.sum(acc_ref[...], axis=1, keepdims=True)
            loss_ref[...] = jnp.broadcast_to((0.5 / d) * total, loss_ref.shape)

    hbm = pl.BlockSpec(memory_space=pl.ANY)
    row = pl.BlockSpec((ts, d), lambda i: (i, 0))
    vec = pl.BlockSpec((1, d), lambda i: (0, 0))
    return pl.pallas_call(
        body,
        name="out_proj_loss",
        grid=(steps,),
        in_specs=[pl.BlockSpec((ts, ka), lambda i: (i, 0)), pl.BlockSpec((ts, yc.shape[1]), lambda i: (i, 0)),
                  pl.BlockSpec(w_out.shape, lambda i: (0, 0)), hbm, hbm, vec],
        out_specs=(row, row, vec, pl.BlockSpec((1, LANES), lambda i: (0, 0))),
        out_shape=(
            jax.ShapeDtypeStruct((r, d), F32),
            jax.ShapeDtypeStruct((r, d), BF16),
            jax.ShapeDtypeStruct((1, d), F32),
            jax.ShapeDtypeStruct((1, LANES), F32),
        ),
        scratch_shapes=[pltpu.VMEM((2, ts, d), F32), pltpu.VMEM((2, ts, d), F32), pltpu.VMEM((1, d), F32),
                        pltpu.SemaphoreType.DMA((2, 2))],
        compiler_params=_params("arbitrary"),
    )(ya, yc, w_out, x, target, g_final)


def _out_proj_bwd(dhb, w_out, ya, yc, bm):
    r, d = dhb.shape
    ka = ya.shape[1]
    n_mix = w_out.shape[0]
    last = r // bm - 1

    def body(dh_ref, w_ref, a_ref, c_ref, dcat_ref, dw_ref, acc_ref):
        @pl.when(pl.program_id(0) == 0)
        def _():
            acc_ref[...] = jnp.zeros_like(acc_ref)

        dh = dh_ref[...]
        dcat_ref[...] = _dot(dh, w_ref[...], _NT).astype(BF16)
        acc_ref[0:ka, :] += _dot(a_ref[...], dh, _TN)
        acc_ref[ka:, :] += _dot(c_ref[...], dh, _TN)

        @pl.when(pl.program_id(0) == last)
        def _():
            dw_ref[...] = acc_ref[...].astype(BF16)

    return pl.pallas_call(
        body,
        name="out_proj_bwd",
        grid=(r // bm,),
        in_specs=[pl.BlockSpec((bm, d), lambda i: (i, 0)), pl.BlockSpec(w_out.shape, lambda i: (0, 0)),
                  pl.BlockSpec((bm, ka), lambda i: (i, 0)), pl.BlockSpec((bm, yc.shape[1]), lambda i: (i, 0))],
        out_specs=(pl.BlockSpec((bm, n_mix), lambda i: (i, 0)),
                   pl.BlockSpec((n_mix, d), lambda i: (0, 0))),
        out_shape=(jax.ShapeDtypeStruct((r, n_mix), BF16),
                   jax.ShapeDtypeStruct((n_mix, d), BF16)),
        scratch_shapes=[pltpu.VMEM((n_mix, d), F32)],
        compiler_params=_params("arbitrary"),
    )(dhb, w_out, ya, yc)


def _attn_bwd(q, k, v, o, lse, dcat, p, g_attn, send_out):
    nb_seq, _, tp, _ = q.shape
    steps = N_HEADS * nb_seq

    def body(q_ref, k_ref, v_ref, o_ref, lse_ref, dy_ref, z_ref, g_ref, pay_ref,
             dq_ref, dk_ref, dv_ref, dz_ref, dg_ref, r2_ref, dq_acc, r1, sums, *sems):
        t = pl.program_id(0) * nb_seq + pl.program_id(1)

        def plan():
            return _reduce_plan((pay_ref,), (None,), (r1,), (sums,), (r2_ref,), *sems)

        @pl.when(t == 0)
        def _():
            plan()[0]()

        @pl.when(t == 1)
        def _():
            plan()[1]()

        @pl.when(pl.program_id(1) == 0)
        def _():
            dg_ref[...] = jnp.zeros_like(dg_ref)

        g = g_ref[...]
        z = z_ref[...].astype(F32)
        o = o_ref[0, 0]
        dy = dy_ref[...].astype(F32)
        sig = _sigmoid(z)
        ohat, r = _rms_stats(o)
        don = dy * (z * sig)
        dz_ref[...] = (dy * (ohat * g) * (sig * (1.0 + z * (1.0 - sig)))).astype(BF16)
        dg_ref[...] += jnp.sum(don * ohat, axis=0, keepdims=True)
        do = _rms_bwd(g * don, ohat, r)
        dvec = jnp.sum(do * o, axis=-1, keepdims=True)
        dob = do.astype(BF16)
        lse_col = lse_ref[0, 0, :, 0:1]
        dq_acc[...] = jnp.zeros_like(dq_acc)
        for k0 in range(0, tp, KV_TILE):
            nk = min(KV_TILE, tp - k0)
            nq = tp - k0
            qq = q_ref[0, 0, k0:, :]
            kk = k_ref[0, 0, k0:k0 + nk, :]
            causal = (lax.broadcasted_iota(jnp.int32, (nq, nk), 1) <= lax.broadcasted_iota(jnp.int32, (nq, nk), 0))
            pr = jnp.where(causal, jnp.exp2(_dot(qq, kk, _NT) - lse_col[k0:]), 0.0)
            dp = _dot(dob[k0:], v_ref[0, 0, k0:k0 + nk, :], _NT)
            ds = (pr * (dp - dvec[k0:])).astype(BF16)
            dv_ref[0, 0, k0:k0 + nk, :] = _dot(pr.astype(BF16), dob[k0:], _TN).astype(BF16)
            dk_ref[0, 0, k0:k0 + nk, :] = (_dot(ds, qq, _TN) * (ATTN_SCALE / Q_SCALE)).astype(BF16)
            dq_acc[k0:, :] += _dot(ds, kk)
        dq_ref[0, 0] = (dq_acc[...] * ATTN_SCALE).astype(BF16)

        @pl.when(t == steps - 1)
        def _():
            plan()[2]()

    qk = pl.BlockSpec((1, 1, tp, 2 * LANES), lambda h, b: (b, h, 0, 0))
    hv = pl.BlockSpec((1, 1, tp, D_V), lambda h, b: (b, h, 0, 0))
    col = pl.BlockSpec((tp, LANES), lambda h, b: (b, h))
    slot = send_out.shape[1:]
    return pl.pallas_call(
        body,
        name="attn_bwd",
        grid=(N_HEADS, nb_seq),
        in_specs=[qk, qk, hv, hv, hv, col,
                  pl.BlockSpec((tp, LANES), lambda h, b: (b, GRP_A // LANES + h)),
                  pl.BlockSpec((1, LANES), lambda h, b: (0, h)),
                  pl.BlockSpec(send_out.shape, lambda h, b: (0, 0, 0))],
        out_specs=(qk, qk, hv, col, pl.BlockSpec((1, LANES), lambda h, b: (0, h)),
                   pl.BlockSpec(memory_space=pl.ANY)),
        out_shape=(
            jax.ShapeDtypeStruct((nb_seq, N_HEADS, tp, 2 * LANES), BF16),
            jax.ShapeDtypeStruct((nb_seq, N_HEADS, tp, 2 * LANES), BF16),
            jax.ShapeDtypeStruct((nb_seq, N_HEADS, tp, D_V), BF16),
            jax.ShapeDtypeStruct((nb_seq * tp, N_HEADS * D_V), BF16),
            jax.ShapeDtypeStruct((1, N_HEADS * D_V), F32),
            jax.ShapeDtypeStruct((N_CHIPS,) + slot, BF16),
        ),
        scratch_shapes=[pltpu.VMEM((tp, 2 * LANES), F32)] + _reduce_scratch([(slot, BF16)], [False]),
        compiler_params=_params("arbitrary", "arbitrary"),
    )(q, k, v, o, lse, dcat, p, g_attn, send_out)


def _qkv_bwd(p, dq, dk, dv, wq, wkv, gq, gkv, tables):
    nb_seq, _, tp, _ = dq.shape
    ht = tp // 2

    def body(pa_ref, dq_ref, dk_ref, dv_ref, wq_ref, wkv_ref, gq_ref, gkv_ref, cos_ref, sa_ref, sb_ref,
             dpa_ref, dwq_ref, dwkv_ref, dgq_ref, dgkv_ref):
        @pl.when(pl.program_id(0) == 0)
        def _():
            dwq_ref[...] = jnp.zeros_like(dwq_ref)
            dwkv_ref[...] = jnp.zeros_like(dwkv_ref)
            dgq_ref[...] = jnp.zeros_like(dgq_ref)
            dgkv_ref[...] = jnp.zeros_like(dgkv_ref)

        pa = pa_ref[...].astype(F32)
        gq, gkv = gq_ref[...], gkv_ref[...]
        cq_hat, rq = _rms_stats(pa[:, :Q_RANK])
        ckv_hat, rkv = _rms_stats(pa[:, Q_RANK:Q_RANK + KV_RANK])
        tabs = (cos_ref[...], sa_ref[...], sb_ref[...])

        pe = [dq_ref[0, h, :, D_NOPE:].astype(F32) for h in range(N_HEADS)]
        pairs = [_rope_t(pe[2 * i] + pltpu.roll(pe[2 * i + 1], D_ROPE, 1), *tabs).astype(BF16) for i in range(2)]
        dq_flat = jnp.concatenate([dq_ref[0, h, :, :D_NOPE] for h in range(N_HEADS)] + pairs, axis=1)
        dwq_ref[...] += _dot((cq_hat * gq).astype(BF16), dq_flat, _TN)
        dcqn = _dot(dq_flat, wq_ref[...], _NT)
        dgq_ref[...] += jnp.sum(dcqn * cq_hat, axis=0, keepdims=True)
        dcq = _rms_bwd(gq * dcqn, cq_hat, rq)

        dkv_flat = jnp.concatenate([dk_ref[0, h, :, :D_NOPE] for h in range(N_HEADS)]
                                   + [dv_ref[0, h] for h in range(N_HEADS)], axis=1)
        dwkv_ref[...] += _dot((ckv_hat * gkv).astype(BF16), dkv_flat, _TN)
        dckvn = _dot(dkv_flat, wkv_ref[...], _NT)
        dgkv_ref[...] += jnp.sum(dckvn * ckv_hat, axis=0, keepdims=True)
        dckv = _rms_bwd(gkv * dckvn, ckv_hat, rkv)

        dk_pe = dk_ref[0, 0, :, D_NOPE:].astype(F32)
        for h in range(1, N_HEADS):
            dk_pe = dk_pe + dk_ref[0, h, :, D_NOPE:].astype(F32)
        dk_pe = jnp.where(lax.broadcasted_iota(jnp.int32, (ht, LANES), 1) < D_ROPE, dk_pe, 0.0)
        dpa_ref[...] = jnp.concatenate([dcq, dckv, _rope_t(dk_pe, *tabs)], axis=1).astype(BF16)

    full = lambda a: pl.BlockSpec(a.shape, lambda i: (0,) * a.ndim)
    tab = pl.BlockSpec((ht, LANES), lambda i: (i % 2, 0))
    qk = pl.BlockSpec((1, N_HEADS, ht, 2 * LANES), lambda i: (i // 2, 0, i % 2, 0))
    acc = lambda shape: pl.BlockSpec(shape, lambda i: (0, 0))
    return pl.pallas_call(
        body,
        name="qkv_bwd",
        grid=(2 * nb_seq,),
        in_specs=[pl.BlockSpec((ht, GRP_A), lambda i: (i, 0)), qk, qk,
                  pl.BlockSpec((1, N_HEADS, ht, D_V), lambda i: (i // 2, 0, i % 2, 0)),
                  full(wq), full(wkv), full(gq), full(gkv), tab, tab, tab],
        out_specs=(pl.BlockSpec((ht, GRP_A), lambda i: (i, 0)),
                   acc(wq.shape), acc(wkv.shape), acc((1, Q_RANK)), acc((1, KV_RANK))),
        out_shape=(
            jax.ShapeDtypeStruct((nb_seq * tp, GRP_A), BF16),
            jax.ShapeDtypeStruct(wq.shape, F32),
            jax.ShapeDtypeStruct(wkv.shape, F32),
            jax.ShapeDtypeStruct((1, Q_RANK), F32),
            jax.ShapeDtypeStruct((1, KV_RANK), F32),
        ),
        compiler_params=_params("arbitrary"),
    )(p, dq, dk, dv, wq, wkv, gq, gkv, *tables)


def _conv_bwd(p, dcat, conv_w, g_conv, nb_seq, tp):
    cols = CONV_WIDTH // LANES

    def body(b_ref, c_ref, h_ref, z_ref, dy_ref, w_ref, g_ref,
             db_ref, dc_ref, dh_ref, dz_ref, dw_ref, dg_ref):
        @pl.when(pl.program_id(1) == 0)
        def _():
            dw_ref[...] = jnp.zeros_like(dw_ref)
            dg_ref[...] = jnp.zeros_like(dg_ref)

        cb, c, h = b_ref[...].astype(F32), c_ref[...].astype(F32), h_ref[...].astype(F32)
        z, dy = z_ref[...].astype(F32), dy_ref[...].astype(F32)
        g = g_ref[...]
        w0, w1, w2 = w_ref[0:1, :], w_ref[1:2, :], w_ref[2:3, :]
        cc = c * h
        row = lax.broadcasted_iota(jnp.int32, (tp, LANES), 0)
        s1 = jnp.where(row >= 1, pltpu.roll(cc, 1, 0), 0.0)
        s2 = jnp.where(row >= 2, pltpu.roll(cc, 2, 0), 0.0)
        dwc = w0 * s2 + w1 * s1 + w2 * cc
        yc = cb * dwc
        r = lax.rsqrt(_group_mean(yc * yc) + EPS)
        ychat = yc * r
        sig = _sigmoid(z)
        dz_ref[...] = (dy * (ychat * g) * (sig * (1.0 + z * (1.0 - sig)))).astype(BF16)
        dyn = dy * (z * sig)
        dg_ref[...] += jnp.sum(dyn * ychat, axis=0, keepdims=True)
        gd = g * dyn
        dyc = r * (gd - ychat * _group_mean(gd * ychat))
        db_ref[...] = (dyc * dwc).astype(BF16)
        ddw = dyc * cb
        dw_ref[0:1, :] += jnp.sum(ddw * s2, axis=0, keepdims=True)
        dw_ref[1:2, :] += jnp.sum(ddw * s1, axis=0, keepdims=True)
        dw_ref[2:3, :] += jnp.sum(ddw * cc, axis=0, keepdims=True)
        u1 = jnp.where(row <= tp - 2, pltpu.roll(ddw, tp - 1, 0), 0.0)
        u2 = jnp.where(row <= tp - 3, pltpu.roll(ddw, tp - 2, 0), 0.0)
        dcc = w2 * ddw + w1 * u1 + w0 * u2
        dc_ref[...] = (dcc * h).astype(BF16)
        dh_ref[...] = (dcc * c).astype(BF16)

    col = pl.BlockSpec((tp, LANES), lambda t, b: (b, t))
    out = jax.ShapeDtypeStruct((nb_seq * tp, CONV_WIDTH), BF16)
    return pl.pallas_call(
        body,
        name="conv_bwd",
        grid=(cols, nb_seq),
        in_specs=_conv_specs(tp, lambda t, b, off: (b, off + t)) + [
            pl.BlockSpec((tp, LANES), lambda t, b: (b, N_HEADS * D_V // LANES + t)),
            pl.BlockSpec((8, LANES), lambda t, b: (0, t)),
            pl.BlockSpec((1, LANES), lambda t, b: (0, t))],
        out_specs=(col, col, col, col,
                   pl.BlockSpec((8, LANES), lambda t, b: (0, t)), pl.BlockSpec((1, LANES), lambda t, b: (0, t))),
        out_shape=(out, out, out, out,
                   jax.ShapeDtypeStruct((8, CONV_WIDTH), F32), jax.ShapeDtypeStruct((1, CONV_WIDTH), F32)),
        compiler_params=_params("arbitrary", "arbitrary"),
    )(p, p, p, p, dcat, conv_w, g_conv)


def _input_bwd(dps, w_in, x, meta, dh, norm_g, nt, send_in):
    nb_seq, s, d = x.shape
    r, kb = dps[0].shape
    ts = (s + LANES) // nt
    steps = nb_seq * nt
    n_dp = len(dps)
    in_slot = send_in.shape[1:]

    def body(*refs):
        dp_refs, w_ref, x_hbm, meta_ref, dh_ref, g_ref, pay_ref = refs[:n_dp], *refs[n_dp:n_dp + 6]
        o = n_dp + 6
        gx_hbm, dmeta_ref, dg_ref, r2_in = refs[o:o + 4]
        xbuf, gxbuf, tok_sems, own_in, r1_in, sum_in = refs[o + 4:o + 10]
        sems = refs[o + 10:]
        i = pl.program_id(0)
        b, k = i // nt, i % nt

        def plan():
            return _reduce_plan((pay_ref,), (own_in,), (r1_in,), (sum_in,), (r2_in,), *sems)

        @pl.when(i == 0)
        def _():
            dmeta_ref[...] = jnp.zeros_like(dmeta_ref)
            dg_ref[...] = jnp.zeros_like(dg_ref)
            plan()[0]()

        @pl.when(i == 1)
        def _():
            plan()[1]()

        def start(kk):
            if kk == 0:
                xbuf[0:PAD_FRONT, :] = jnp.zeros((PAD_FRONT, d), F32)
                xbuf[PAD_FRONT:LANES, :] = meta_ref[...]
            _token_copy(x_hbm, b, kk, ts, xbuf, tok_sems.at[0]).start()

        _for_tile(k, nt, start)
        du = _dot(dp_refs[0][...], w_ref[0:kb, :])
        for j in range(1, n_dp):
            du = du + _dot(dp_refs[j][...], w_ref[kb * j:kb * (j + 1), :])
        _for_tile(k, nt, lambda kk: _token_copy(x_hbm, b, kk, ts, xbuf, tok_sems.at[0]).wait())

        g = g_ref[...]
        hhat, rstd = _rms_stats(xbuf[...])
        dg_ref[...] += jnp.sum(du * hhat, axis=0, keepdims=True)
        res = _rms_bwd(g * du, hhat, rstd) + dh_ref[...]

        @pl.when(i > 0)
        def _():
            _for_tile(k, nt, lambda kk: _token_copy(gx_hbm, b, (kk - 1) % nt, ts, gxbuf, tok_sems.at[1], True).wait())

        gxbuf[...] = res

        @pl.when(k == 0)
        def _():
            dmeta_ref[...] += gxbuf[PAD_FRONT:LANES, :]

        _for_tile(k, nt, lambda kk: _token_copy(gx_hbm, b, kk, ts, gxbuf, tok_sems.at[1], True).start())

        @pl.when(i == steps - 1)
        def _():
            _token_copy(gx_hbm, b, nt - 1, ts, gxbuf, tok_sems.at[1], True).wait()
            plan()[2]()

    whole = lambda a: pl.BlockSpec(a.shape, lambda i: (0,) * a.ndim)
    hbm = pl.BlockSpec(memory_space=pl.ANY)
    return pl.pallas_call(
        body,
        name="input_bwd",
        grid=(steps,),
        in_specs=[pl.BlockSpec((ts, kb), lambda i: (i, 0)) for _ in dps]
        + [whole(w_in), hbm, whole(meta), pl.BlockSpec((ts, d), lambda i: (i, 0)), whole(norm_g), hbm],
        out_specs=(hbm, pl.BlockSpec((N_META, d), lambda i: (0, 0)), pl.BlockSpec((1, d), lambda i: (0, 0)), hbm),
        out_shape=(jax.ShapeDtypeStruct((nb_seq, s, d), F32),
                   jax.ShapeDtypeStruct((N_META, d), F32),
                   jax.ShapeDtypeStruct((1, d), F32),
                   jax.ShapeDtypeStruct((N_CHIPS,) + in_slot, BF16)),
        scratch_shapes=[pltpu.VMEM((ts, d), F32), pltpu.VMEM((ts, d), F32), pltpu.SemaphoreType.DMA((2,))]
        + _reduce_scratch([(in_slot, BF16)], [True]),
        compiler_params=_params("arbitrary"),
    )(*dps, w_in, x, meta, dh, norm_g, send_in)


def _in_proj_bwd_w(u, dps, bm, small_grads):
    r, d = u.shape
    kb = dps[0].shape[1]
    steps = r // bm
    n_dp, n_small = len(dps), len(small_grads)
    small_slot = (SMALL_ROWS, LANES)

    def body(*refs):
        u_ref, dp_refs = refs[0], refs[1:1 + n_dp]
        small_refs = refs[1 + n_dp:1 + n_dp + n_small]
        o = 1 + n_dp + n_small
        o_ref, r2_small = refs[o:o + 2]
        acc_ref, ssmall, r1_small, sum_small = refs[o + 2:o + 6]
        sems = refs[o + 6:]
        i = pl.program_id(0)

        def plan():
            return _reduce_plan((ssmall,), (None,), (r1_small,), (sum_small,), (r2_small,), *sems)

        @pl.when(i == 0)
        def _():
            acc_ref[...] = jnp.zeros_like(acc_ref)
            _pack_small(ssmall, *small_refs)
            plan()[0]()

        @pl.when(i == 1)
        def _():
            plan()[1]()

        uu = u_ref[...]
        for j in range(n_dp):
            acc_ref[kb * j:kb * (j + 1), :] += _dot(dp_refs[j][...], uu, _TN)

        @pl.when(i == steps - 1)
        def _():
            for k in range(N_DEV):
                for s, e, c0 in _in_pieces(k):
                    o_ref[k, s:e, :] = acc_ref[c0:c0 + e - s, :].astype(BF16)
                o_ref[k, SHARD_IN:, :] = jnp.zeros((SHARD_IN_PAD - SHARD_IN, d), BF16)
            plan()[2]()

    whole = lambda a: pl.BlockSpec(a.shape, lambda i: (0,) * a.ndim)
    return pl.pallas_call(
        body,
        name="in_proj_bwd_w",
        grid=(steps,),
        in_specs=[pl.BlockSpec((bm, d), lambda i: (i, 0))]
        + [pl.BlockSpec((bm, kb), lambda i: (i, 0)) for _ in dps] + [whole(a) for a in small_grads],
        out_specs=(pl.BlockSpec((N_DEV, SHARD_IN_PAD, d), lambda i: (0, 0, 0)), pl.BlockSpec(memory_space=pl.ANY)),
        out_shape=(jax.ShapeDtypeStruct((N_DEV, SHARD_IN_PAD, d), BF16),
                   jax.ShapeDtypeStruct((N_CHIPS,) + small_slot, F32)),
        scratch_shapes=[pltpu.VMEM((kb * n_dp, d), F32), pltpu.VMEM((N_DEV,) + small_slot, F32)]
        + _reduce_scratch([(small_slot, F32)], [False]),
        compiler_params=_params("arbitrary"),
    )(u, *dps, *small_grads)


def _local_step(x, loss_target, u, p, meta_f, norm_g, w_in_p, q_norm_g, w_q_p, kv_norm_g, w_kv_p, conv_w_f,
                attn_out_g, conv_out_g, w_out_f, g_final):
    nb_seq, s, d = x.shape
    tp = s + LANES
    ht = tp // 2
    tables = _rope_tables(tp)

    q, k, v = _qkv_fwd(p, w_q_p, w_kv_p, q_norm_g, kv_norm_g, tables, nb_seq, tp)
    ya, o, lse = _attn_fwd(q, k, v, p, attn_out_g)
    yc = _conv_fwd(p, conv_w_f, conv_out_g, nb_seq, tp)
    dh, dhb, d_final_g, loss_part = _out_proj_loss(ya, yc, w_out_f, x, loss_target, g_final, TOKEN_TILES)

    dcat, d_w_out = _out_proj_bwd(dhb, w_out_f, ya, yc, ht)
    send_out = d_w_out.reshape(N_DEV, SHARD_OUT, d)
    dq, dk, dv, dz_attn, d_attn_g, r_out = _attn_bwd(q, k, v, o, lse, dcat, p, attn_out_g, send_out)
    dpa, d_wq_p, d_wkv_p, d_gq, d_gkv = _qkv_bwd(p, dq, dk, dv, w_q_p, w_kv_p, q_norm_g, kv_norm_g, tables)
    d_b, d_c, d_h, dz_conv, d_conv_w, d_conv_g = _conv_bwd(p, dcat, conv_w_f, conv_out_g, nb_seq, tp)
    dps = (dpa, dz_attn, d_b, d_c, d_h, dz_conv)
    small = (d_wq_p, d_wkv_p, d_conv_w, d_final_g, d_gq, d_gkv, d_attn_g, d_conv_g, loss_part)
    send_in, r_small = _in_proj_bwd_w(u, dps, ht // 2, small)
    grad_x, d_meta, d_norm_g, r_in = _input_bwd(dps, w_in_p, x, meta_f, dh, norm_g, TOKEN_TILES, send_in)
    return grad_x, r_in, r_out, r_small, d_meta, d_norm_g


def kernel(x, meta_tokens, norm_g, w_in, q_norm_g, w_q_up, kv_norm_g, w_kv_up, conv_w, attn_out_g, conv_out_g, w_out, final_norm_g, loss_target, m_meta_tokens, m_norm_g, m_w_in, m_q_norm_g, m_w_q_up, m_kv_norm_g, m_w_kv_up, m_conv_w, m_attn_out_g, m_conv_out_g, m_w_out, m_final_norm_g, v_meta_tokens, v_norm_g, v_w_in, v_q_norm_g, v_w_q_up, v_kv_norm_g, v_w_kv_up, v_conv_w, v_attn_out_g, v_conv_out_g, v_w_out, v_final_norm_g):
    d = x.shape[-1]
    ht = (x.shape[1] + LANES) // 2
    u, w_in_p, meta_f = _prep_gather(x, meta_tokens, norm_g, w_in[0].T)
    p, w_q_p, w_kv_p, w_out_f, conv_w_f = _in_proj_gather(
        u, w_in_p, w_q_up[0], w_kv_up[0], w_out[0], conv_w[0], ht, GRP_A)
    g_final = final_norm_g.reshape(1, d)
    grad_x, r_in, r_out, r_small, d_meta, d_norm_g = _local_step(
        x, loss_target, u, p, meta_f, norm_g, w_in_p, q_norm_g, w_q_p, kv_norm_g, w_kv_p, conv_w_f,
        attn_out_g, conv_out_g, w_out_f, g_final)

    flat = lambda a: a.reshape(a.shape[-2:]) if a.ndim == 3 else a.reshape(1, -1) if a.ndim == 1 else a
    transposed = ("w_in",)
    to_kernel = lambda n, a: flat(a).T if n in transposed else flat(a)
    from_kernel = lambda n, a, shape: (a.T if n in transposed else a).reshape(shape)
    params = {
        "meta_tokens": (meta_tokens, m_meta_tokens, v_meta_tokens),
        "norm_g": (norm_g, m_norm_g, v_norm_g),
        "w_in": (w_in, m_w_in, v_w_in),
        "q_norm_g": (q_norm_g, m_q_norm_g, v_q_norm_g),
        "w_q_up": (w_q_up, m_w_q_up, v_w_q_up),
        "kv_norm_g": (kv_norm_g, m_kv_norm_g, v_kv_norm_g),
        "w_kv_up": (w_kv_up, m_w_kv_up, v_w_kv_up),
        "conv_w": (conv_w, m_conv_w, v_conv_w),
        "attn_out_g": (attn_out_g, m_attn_out_g, v_attn_out_g),
        "conv_out_g": (conv_out_g, m_conv_out_g, v_conv_out_g),
        "w_out": (w_out, m_w_out, v_w_out),
        "final_norm_g": (final_norm_g, m_final_norm_g, v_final_norm_g),
    }
    grads, loss = _reduce_tail(r_in, r_out, r_small, d_meta, d_norm_g)
    updated = _adamw(grads, {n: tuple(to_kernel(n, a) for a in t) for n, t in params.items()})
    outs = [[from_kernel(n, updated[n][i], params[n][0].shape) for n, _ in PARAM_SHAPES] for i in range(4)]
    return (loss[0, 0], grad_x, *outs[0], *outs[1], *outs[2], *outs[3])
```

```python
import functools

import jax
import jax.numpy as jnp
from jax import lax
from jax.experimental import pallas as pl
from jax.experimental.pallas import tpu as pltpu

F32 = jnp.float32
BF16 = jnp.bfloat16

N_META = 16
D_MODEL = 1024
N_HEADS = 4
D_NOPE = 128
D_ROPE = 64
D_V = 128
Q_RANK = 256
KV_RANK = 128
CONV_WIDTH = 512
CONV_GROUP = 64
ROPE_THETA = 10000.0
ATTN_SCALE = (D_NOPE + D_ROPE) ** -0.5
Q_SCALE = ATTN_SCALE * 1.4426950408889634
EPS = 1e-6
NEG_INF = -1e30

ADAM_LR = 0.001
ADAM_B1 = 0.9
ADAM_B2 = 0.999
ADAM_EPS = 1e-08
ADAM_WD = 0.01
ADAM_STEP = 10

LANES = 128
PAD_FRONT = LANES - N_META
KV_TILE = 256
N_DEV = 8
VMEM_LIMIT = 56 * 1024 * 1024

IN_PAD = 3072
GRP_A = 512
N_A = Q_RANK + KV_RANK + D_ROPE
IN_PROJ = 3008
SHARD_IN = IN_PROJ // N_DEV
SHARD_IN_PAD = 384
SHARD_Q = 96
SHARD_KV = 128
SHARD_OUT = 128
SHARD_CONV = 64
SHARD_META = 128
Q_COLS = N_HEADS * (D_NOPE + D_ROPE)
KV_COLS = N_HEADS * (D_NOPE + D_V)

ROW_Q, ROW_KV, ROW_META, ROW_CONV = 0, 256, 384, 400
ROW_REPL = 408
ROW_NORM, ROW_FINAL, ROW_GQ, ROW_GKV, ROW_ATTN, ROW_CONVG, ROW_LOSS = 408, 416, 424, 426, 427, 431, 435
SMALL_ROWS = 440

PARAM_SHAPES = (
    ("meta_tokens", (N_META, SHARD_META)), ("norm_g", (1, D_MODEL)), ("w_in", (SHARD_IN, D_MODEL)),
    ("q_norm_g", (1, Q_RANK)), ("w_q_up", (Q_RANK, SHARD_Q)), ("kv_norm_g", (1, KV_RANK)),
    ("w_kv_up", (KV_RANK, SHARD_KV)), ("conv_w", (3, SHARD_CONV)), ("attn_out_g", (1, CONV_WIDTH)),
    ("conv_out_g", (1, CONV_WIDTH)), ("w_out", (SHARD_OUT, D_MODEL)), ("final_norm_g", (1, D_MODEL)),
)


def _in_pieces(k):
    lo, hi = SHARD_IN * k, SHARD_IN * (k + 1)
    out = []
    if lo < N_A:
        out.append((0, min(hi, N_A) - lo, lo))
    if hi > N_A:
        s = max(lo, N_A)
        out.append((s - lo, hi - lo, s + GRP_A - N_A))
    return out


def _q_pieces(k):
    lo, hi = SHARD_Q * k, SHARD_Q * (k + 1)
    out = []
    for h in range(N_HEADS):
        base = (D_NOPE + D_ROPE) * h
        s, e = max(lo, base), min(hi, base + D_NOPE)
        if s < e:
            out.append((s - lo, e - lo, D_NOPE * h + s - base))
        s, e = max(lo, base + D_NOPE), min(hi, base + D_NOPE + D_ROPE)
        if s < e:
            out.append((s - lo, e - lo, N_HEADS * D_NOPE + D_ROPE * h + s - base - D_NOPE))
    return out


def _kv_dst(k):
    return D_NOPE * (k // 2) + (N_HEADS * D_NOPE if k % 2 else 0)


def _params(*sem):
    return pltpu.CompilerParams(dimension_semantics=sem, vmem_limit_bytes=VMEM_LIMIT)


def _rms_stats(x):
    r = lax.rsqrt(jnp.mean(x * x, axis=-1, keepdims=True) + EPS)
    return x * r, r


def _rms_bwd(gdy, xhat, r):
    return r * (gdy - xhat * jnp.mean(gdy * xhat, axis=-1, keepdims=True))


def _sigmoid(z):
    return 1.0 / (1.0 + jnp.exp(-z))


def _group_mean(x):
    i0 = lax.broadcasted_iota(jnp.int32, (LANES, LANES), 0) // CONV_GROUP
    i1 = lax.broadcasted_iota(jnp.int32, (LANES, LANES), 1) // CONV_GROUP
    m = jnp.where(i0 == i1, 1.0 / CONV_GROUP, 0.0).astype(BF16)
    hi = x.astype(BF16)
    lo = (x - hi.astype(F32)).astype(BF16)
    return jnp.dot(hi, m, preferred_element_type=F32) + jnp.dot(lo, m, preferred_element_type=F32)


_NT = (((1,), (1,)), ((), ()))
_TN = (((0,), (0,)), ((), ()))


def _dot(a, b, dims=None):
    if dims is None:
        return jnp.dot(a, b, preferred_element_type=F32)
    return lax.dot_general(a, b, dims, preferred_element_type=F32)


def _device_position():
    x, y, c = lax.axis_index("x"), lax.axis_index("y"), lax.axis_index("c")
    return x, y, c, 4 * x + 2 * y + c


def _gather_plan(srcs, slots, send_sems, recv_sems, local_sems):
    x, y, c, _ = _device_position()
    me, sibling = (x, y, c), (x, y, 1 - c)
    flip = lambda v, on: v + on - 2 * v * on
    near = (flip(x, 1 - c), flip(y, c))
    far = (flip(x, c), flip(y, 1 - c))
    diag = (1 - x, 1 - y)
    n = len(srcs)

    def slot(a, px, py, pc):
        return slots[a].at[4 * px + 2 * py + pc]

    def copy(a, k, block, to, own=False):
        return pltpu.make_async_remote_copy(
            src_ref=srcs[a] if own else slot(a, *block),
            dst_ref=slot(a, *block),
            send_sem=send_sems.at[7 * a + k],
            recv_sem=recv_sems.at[7 * a + k],
            device_id=to,
            device_id_type=pl.DeviceIdType.MESH,
        )

    def local(a):
        return pltpu.make_async_copy(srcs[a], slot(a, *me), local_sems.at[a])

    def sends(a):
        return [copy(a, 0, me, sibling, own=True), copy(a, 1, me, (*near, c), own=True),
                copy(a, 2, me, (*far, c), own=True), copy(a, 3, (*near, c), (*far, c)),
                copy(a, 4, (*near, c), sibling), copy(a, 5, (*far, c), sibling), copy(a, 6, (*diag, c), sibling)]

    def arrivals(a):
        return [copy(a, 0, sibling, me), copy(a, 1, (*near, c), me), copy(a, 2, (*far, c), me),
                copy(a, 3, (*diag, c), me), copy(a, 4, (*far, 1 - c), me), copy(a, 5, (*near, 1 - c), me),
                copy(a, 6, (*diag, 1 - c), me)]

    def start():
        for a in range(n):
            local(a).start()
            for cp in sends(a)[:3]:
                cp.start()

    def relay():
        for a in range(n):
            arrivals(a)[1].wait_recv()
            sends(a)[3].start()
            sends(a)[4].start()

    def forward():
        for k in (2, 3):
            for a in range(n):
                arrivals(a)[k].wait_recv()
                sends(a)[k + 3].start()

    def finish():
        for a in range(n):
            for k in (0, 4, 5, 6):
                arrivals(a)[k].wait_recv()
        for a in range(n):
            for cp in sends(a):
                cp.wait_send()
            local(a).wait()

    return start, relay, forward, finish


def _adam_update(g, w, m, v):
    m_new = ADAM_B1 * m + (1.0 - ADAM_B1) * g
    v_new = ADAM_B2 * v + (1.0 - ADAM_B2) * (g * g)
    m_hat = m_new / (1.0 - ADAM_B1 ** ADAM_STEP)
    v_hat = v_new / (1.0 - ADAM_B2 ** ADAM_STEP)
    return -ADAM_LR * (m_hat / (jnp.sqrt(v_hat) + ADAM_EPS) + ADAM_WD * w), m_new, v_new


def _adamw(grads, params):
    names = [n for n, _ in PARAM_SHAPES]
    n_p = len(names)

    def body(*refs):
        for i in range(n_p):
            g = refs[i][...]
            w, m, v = (refs[n_p + 3 * i + j][...] for j in range(3))
            delta, m_new, v_new = _adam_update(g, w, m, v)
            for j, val in enumerate((g, delta, m_new, v_new)):
                refs[4 * n_p + 4 * i + j][...] = val

    vm = pl.BlockSpec(memory_space=pltpu.VMEM)
    out_shape = []
    for _, shape in PARAM_SHAPES:
        out_shape += [jax.ShapeDtypeStruct(shape, F32)] * 4
    outs = pl.pallas_call(
        body,
        name="adamw",
        out_shape=tuple(out_shape),
        in_specs=[vm] * (4 * n_p),
        out_specs=(vm,) * (4 * n_p),
        compiler_params=pltpu.CompilerParams(vmem_limit_bytes=VMEM_LIMIT),
    )(*[grads[n] for n in names], *[a for n in names for a in params[n]])
    return {n: outs[4 * i:4 * i + 4] for i, n in enumerate(names)}


N_CHIPS = 4


def _reduce_plan(pays, owns, r1s, sums, r2s, send1, recv1, send2, recv2, local_sems):
    x, y, c, _ = _device_position()
    sibling = (x, y, 1 - c)
    chips = [((1 - x if rj & 2 else x), (1 - y if rj & 1 else y)) for rj in range(N_CHIPS)]
    n = len(pays)

    def slot_of(rj, core):
        return 4 * chips[rj][0] + 2 * chips[rj][1] + core

    def to_sibling(a, rj):
        return pltpu.make_async_remote_copy(
            src_ref=pays[a].at[slot_of(rj, 1 - c)], dst_ref=r1s[a].at[rj],
            send_sem=send1.at[N_CHIPS * a + rj], recv_sem=recv1.at[N_CHIPS * a + rj],
            device_id=sibling, device_id_type=pl.DeviceIdType.MESH)

    def load_own(a, rj):
        return pltpu.make_async_copy(pays[a].at[slot_of(rj, c)], owns[a].at[rj], local_sems.at[2 * N_CHIPS * a + rj])

    def to_chip(a, rj):
        return pltpu.make_async_remote_copy(
            src_ref=sums[a].at[rj], dst_ref=r2s[a].at[rj],
            send_sem=send2.at[N_CHIPS * a + rj], recv_sem=recv2.at[N_CHIPS * a + rj],
            device_id=(*chips[rj], c), device_id_type=pl.DeviceIdType.MESH)

    def keep(a):
        return pltpu.make_async_copy(sums[a].at[0], r2s[a].at[0], local_sems.at[2 * N_CHIPS * a + N_CHIPS])

    def start():
        for a in range(n):
            for rj in range(N_CHIPS):
                to_sibling(a, rj).start()
                if owns[a] is not None:
                    load_own(a, rj).start()

    def combine():
        for a in range(n):
            for rj in range(N_CHIPS):
                to_sibling(a, rj).wait_recv()
                if owns[a] is not None:
                    load_own(a, rj).wait()
                    mine = owns[a][rj]
                else:
                    mine = pays[a][slot_of(rj, c)]
                sums[a][rj] = (mine.astype(F32) + r1s[a][rj].astype(F32)).astype(sums[a].dtype)
            keep(a).start()
            for rj in range(1, N_CHIPS):
                to_chip(a, rj).start()

    def finish():
        for a in range(n):
            for rj in range(1, N_CHIPS):
                to_chip(a, rj).wait_recv()
            for rj in range(N_CHIPS):
                to_sibling(a, rj).wait_send()
            for rj in range(1, N_CHIPS):
                to_chip(a, rj).wait_send()
            keep(a).wait()

    return start, combine, finish


def _reduce_scratch(shapes_dtypes, own_flags):
    out = []
    for (shape, dtype), own in zip(shapes_dtypes, own_flags):
        if own:
            out.append(pltpu.VMEM((N_CHIPS,) + shape, dtype))
        out += [pltpu.VMEM((N_CHIPS,) + shape, dtype), pltpu.VMEM((N_CHIPS,) + shape, dtype)]
    n = len(shapes_dtypes)
    out += [pltpu.SemaphoreType.DMA((N_CHIPS * n,))] * 4 + [pltpu.SemaphoreType.DMA((2 * N_CHIPS * n,))]
    return out


def _pack_small(ssmall, dwq, dwkv, dconv, dfinal, dgq, dgkv, dattn, dconvg, loss_part):
    ssmall[...] = jnp.zeros_like(ssmall)
    rep = ssmall.at[0]
    for i in range(D_MODEL // LANES):
        rep[ROW_FINAL + i:ROW_FINAL + i + 1, :] = dfinal[:, LANES * i:LANES * (i + 1)]
    for i in range(Q_RANK // LANES):
        rep[ROW_GQ + i:ROW_GQ + i + 1, :] = dgq[:, LANES * i:LANES * (i + 1)]
    rep[ROW_GKV:ROW_GKV + 1, :] = dgkv[...]
    for i in range(CONV_WIDTH // LANES):
        rep[ROW_ATTN + i:ROW_ATTN + i + 1, :] = dattn[:, LANES * i:LANES * (i + 1)]
        rep[ROW_CONVG + i:ROW_CONVG + i + 1, :] = dconvg[:, LANES * i:LANES * (i + 1)]
    rep[ROW_LOSS:ROW_LOSS + 1, :] = loss_part[...]
    for k in range(N_DEV):
        if k:
            ssmall[k, ROW_REPL:, :] = ssmall[0, ROW_REPL:, :]
        for s, e, d in _q_pieces(k):
            ssmall[k, ROW_Q:ROW_Q + Q_RANK, s:e] = dwq[:, d:d + e - s]
        ssmall[k, ROW_KV:ROW_KV + KV_RANK, :] = dwkv[:, _kv_dst(k):_kv_dst(k) + SHARD_KV]
        ssmall[k, ROW_CONV:ROW_CONV + 3, 0:SHARD_CONV] = dconv[0:3, SHARD_CONV * k:SHARD_CONV * (k + 1)]


TOKEN_TILES = 4
TAIL_ROWS = N_META + D_MODEL // LANES


def _reduce_tail(r_in, r_out, r_small, d_meta, d_norm):
    n_p = len(PARAM_SHAPES)
    names = [n for n, _ in PARAM_SHAPES]

    def body(*refs):
        rin, rout, rsmall, dmeta, dnorm = refs[:5]
        g_out = {n: refs[5 + i] for i, n in enumerate(names)}
        loss_out = refs[5 + n_p]
        stail, rtail, gsum, gtail, send_sems, recv_sems = refs[6 + n_p:]
        x, y, c, me = _device_position()
        my_chip = 2 * x + y

        for k in range(N_DEV):
            stail[k, 0:N_META, :] = dmeta[:, SHARD_META * k:SHARD_META * (k + 1)]
            for i in range(D_MODEL // LANES):
                stail[k, N_META + i:N_META + i + 1, :] = dnorm[:, LANES * i:LANES * (i + 1)]
        copies = []
        for r in range(1, N_DEV):
            peer = (1 - x if r & 4 else x, 1 - y if r & 2 else y, 1 - c if r & 1 else c)
            copies.append(pltpu.make_async_remote_copy(
                src_ref=stail.at[4 * peer[0] + 2 * peer[1] + peer[2]],
                dst_ref=rtail.at[r],
                send_sem=send_sems.at[r - 1],
                recv_sem=recv_sems.at[r - 1],
                device_id=peer,
                device_id_type=pl.DeviceIdType.MESH,
            ))
        for cp in copies:
            cp.start()
        rtail[0] = stail[me]

        g = rin[my_chip].astype(F32)
        for ch in range(1, N_CHIPS):
            g = g + rin[ch ^ my_chip].astype(F32)
        g_out["w_in"][...] = g[:SHARD_IN, :]

        g = rout[my_chip].astype(F32)
        gs = rsmall[my_chip]
        for ch in range(1, N_CHIPS):
            g = g + rout[ch ^ my_chip].astype(F32)
            gs = gs + rsmall[ch ^ my_chip]
        g_out["w_out"][...] = g
        gsum[...] = gs
        g_out["w_q_up"][...] = gsum[ROW_Q:ROW_Q + Q_RANK, 0:SHARD_Q]
        g_out["w_kv_up"][...] = gsum[ROW_KV:ROW_KV + KV_RANK, :]
        g_out["conv_w"][...] = gsum[ROW_CONV:ROW_CONV + 3, 0:SHARD_CONV]
        for name, row, width in (("final_norm_g", ROW_FINAL, D_MODEL), ("q_norm_g", ROW_GQ, Q_RANK),
                                 ("kv_norm_g", ROW_GKV, KV_RANK), ("attn_out_g", ROW_ATTN, CONV_WIDTH),
                                 ("conv_out_g", ROW_CONVG, CONV_WIDTH)):
            for i in range(width // LANES):
                g_out[name][:, LANES * i:LANES * (i + 1)] = gsum[row + i:row + i + 1, :]
        loss_out[...] = gsum[ROW_LOSS:ROW_LOSS + 1, :]

        for cp in copies:
            cp.wait_recv()
        gt = rtail[me]
        for d in range(1, N_DEV):
            gt = gt + rtail[d ^ me]
        gtail[...] = gt
        g_out["meta_tokens"][...] = gtail[0:N_META, :]
        for i in range(D_MODEL // LANES):
            g_out["norm_g"][:, LANES * i:LANES * (i + 1)] = gtail[N_META + i:N_META + i + 1, :]
        for cp in copies:
            cp.wait_send()

    vm = pl.BlockSpec(memory_space=pltpu.VMEM)
    out_shape = [jax.ShapeDtypeStruct(shape, F32) for _, shape in PARAM_SHAPES]
    out_shape.append(jax.ShapeDtypeStruct((1, LANES), F32))
    outs = pl.pallas_call(
        body,
        name="reduce_tail",
        out_shape=tuple(out_shape),
        in_specs=[vm] * 5,
        out_specs=(vm,) * len(out_shape),
        scratch_shapes=[
            pltpu.VMEM((N_DEV, TAIL_ROWS, LANES), F32),
            pltpu.VMEM((N_DEV, TAIL_ROWS, LANES), F32),
            pltpu.VMEM((SMALL_ROWS, LANES), F32),
            pltpu.VMEM((TAIL_ROWS, LANES), F32),
            pltpu.SemaphoreType.DMA((N_DEV - 1,)),
            pltpu.SemaphoreType.DMA((N_DEV - 1,)),
        ],
        compiler_params=pltpu.CompilerParams(vmem_limit_bytes=VMEM_LIMIT),
    )(r_in, r_out, r_small, d_meta, d_norm)
    return {n: outs[i] for i, n in enumerate(names)}, outs[-1]


def _prep_gather(x, meta, norm_g, w_in_t):
    nb_seq, s, d = x.shape
    nb = s // LANES + 1
    relay_step = nb_seq * (nb - 1) // 2
    forward_step = nb_seq * (nb - 1) - 1
    finish_step = nb_seq * (nb - 1)

    def body(x_ref, meta_ref, g_ref, win_ref, u_ref, w_in_p, meta_f,
             sbig, ssmall, gbig, gsmall, send_sems, recv_sems, local_sems):
        jj, b = pl.program_id(0), pl.program_id(1)
        t = jj * nb_seq + b

        def plan():
            return _gather_plan((sbig, ssmall), (gbig, gsmall), send_sems, recv_sems, local_sems)

        @pl.when(t == 0)
        def _():
            sbig[0:SHARD_IN, :] = win_ref[...].astype(BF16)
            sbig[SHARD_IN:, :] = jnp.zeros((SHARD_IN_PAD - SHARD_IN, d), BF16)
            ssmall[...] = meta_ref[...]
            plan()[0]()

        @pl.when(t == relay_step)
        def _():
            plan()[1]()

        @pl.when(t == forward_step)
        def _():
            plan()[2]()

        @pl.when(t == finish_step)
        def _():
            plan()[3]()
            w_in_p[N_A:GRP_A, :] = jnp.zeros((GRP_A - N_A, d), BF16)
            for k in range(N_DEV):
                for s0, e0, d0 in _in_pieces(k):
                    w_in_p[d0:d0 + e0 - s0, :] = gbig[k, s0:e0, :]
                meta_f[:, SHARD_META * k:SHARD_META * (k + 1)] = gsmall[k]

        def norm(h):
            hhat, _ = _rms_stats(h)
            return (hhat * g_ref[...]).astype(BF16)

        @pl.when(jj < nb - 1)
        def _():
            u_ref[...] = norm(x_ref[0])

        @pl.when(jj == nb - 1)
        def _():
            u_ref[0:PAD_FRONT, :] = jnp.zeros((PAD_FRONT, d), BF16)
            u_ref[PAD_FRONT:LANES, :] = norm(meta_f[...])

    whole = lambda shape: pl.BlockSpec(shape, lambda jj, b: (0,) * len(shape))
    return pl.pallas_call(
        body,
        name="prep_norm_gather",
        grid=(nb, nb_seq),
        in_specs=[
            pl.BlockSpec((1, LANES, d), lambda jj, b: (b, jnp.minimum(jj, nb - 2), 0)),
            whole(meta.shape), whole(norm_g.shape), whole(w_in_t.shape),
        ],
        out_specs=(pl.BlockSpec((LANES, d), lambda jj, b: (b * nb + (jj + 1) % nb, 0)),
                   whole((IN_PAD, d)), whole((N_META, d))),
        out_shape=(jax.ShapeDtypeStruct((nb_seq * nb * LANES, d), BF16),
                   jax.ShapeDtypeStruct((IN_PAD, d), BF16),
                   jax.ShapeDtypeStruct((N_META, d), F32)),
        scratch_shapes=[
            pltpu.VMEM((SHARD_IN_PAD, d), BF16),
            pltpu.VMEM((N_META, SHARD_META), F32),
            pltpu.VMEM((N_DEV, SHARD_IN_PAD, d), BF16),
            pltpu.VMEM((N_DEV, N_META, SHARD_META), F32),
            pltpu.SemaphoreType.DMA((14,)),
            pltpu.SemaphoreType.DMA((14,)),
            pltpu.SemaphoreType.DMA((2,)),
        ],
        compiler_params=_params("arbitrary", "arbitrary"),
    )(x, meta, norm_g, w_in_t)


def _in_proj_gather(u, w_in_p, w_q, w_kv, w_out, conv_w, bm, bn):
    m, k_dim = u.shape
    n = w_in_p.shape[0]
    steps = (m // bm) * (n // bn)
    relay_step, forward_step = steps // 6, steps // 2
    qkv_rows = Q_RANK + KV_RANK

    def body(a_ref, b_ref, wq_ref, wkv_ref, wout_ref, conv_ref, o_ref, w_q_p, w_kv_p, w_out_f, conv_f,
             sqkv, sout, sconv, gqkv, gout, gconv, send_sems, recv_sems, local_sems):
        t = pl.program_id(0) * (n // bn) + pl.program_id(1)

        def plan():
            return _gather_plan((sqkv, sout, sconv), (gqkv, gout, gconv), send_sems, recv_sems, local_sems)

        @pl.when(t == 0)
        def _():
            sqkv[...] = jnp.zeros_like(sqkv)
            sqkv[0:Q_RANK, 0:SHARD_Q] = wq_ref[...].astype(BF16)
            sqkv[Q_RANK:, :] = wkv_ref[...].astype(BF16)
            sout[...] = wout_ref[...].astype(BF16)
            sconv[...] = jnp.zeros_like(sconv)
            sconv[0:3, 0:SHARD_CONV] = conv_ref[...]
            plan()[0]()

        @pl.when(t == relay_step)
        def _():
            plan()[1]()

        @pl.when(t == forward_step)
        def _():
            plan()[2]()

        o_ref[...] = _dot(a_ref[...], b_ref[...], _NT).astype(o_ref.dtype)

        @pl.when(t == steps - 1)
        def _():
            plan()[3]()
            conv_f[...] = jnp.zeros_like(conv_f)
            for k in range(N_DEV):
                for s0, e0, d0 in _q_pieces(k):
                    w_q_p[:, d0:d0 + e0 - s0] = gqkv[k, 0:Q_RANK, s0:e0]
                w_kv_p[:, _kv_dst(k):_kv_dst(k) + SHARD_KV] = gqkv[k, Q_RANK:, :]
                w_out_f[SHARD_OUT * k:SHARD_OUT * (k + 1), :] = gout[k]
                conv_f[0:3, SHARD_CONV * k:SHARD_CONV * (k + 1)] = gconv[k, 0:3, 0:SHARD_CONV]

    whole = lambda shape: pl.BlockSpec(shape, lambda i, j: (0,) * len(shape))
    return pl.pallas_call(
        body,
        name="in_proj_gather",
        grid=(m // bm, n // bn),
        in_specs=[pl.BlockSpec((bm, k_dim), lambda i, j: (i, 0)), pl.BlockSpec((bn, k_dim), lambda i, j: (j, 0)),
                  whole(w_q.shape), whole(w_kv.shape), whole(w_out.shape), whole(conv_w.shape)],
        out_specs=(pl.BlockSpec((bm, bn), lambda i, j: (i, j)),
                   whole((Q_RANK, Q_COLS)), whole((KV_RANK, KV_COLS)), whole((D_MODEL, D_MODEL)),
                   whole((8, CONV_WIDTH))),
        out_shape=(jax.ShapeDtypeStruct((m, n), BF16),
                   jax.ShapeDtypeStruct((Q_RANK, Q_COLS), BF16),
                   jax.ShapeDtypeStruct((KV_RANK, KV_COLS), BF16),
                   jax.ShapeDtypeStruct((D_MODEL, D_MODEL), BF16),
                   jax.ShapeDtypeStruct((8, CONV_WIDTH), F32)),
        scratch_shapes=[
            pltpu.VMEM((qkv_rows, LANES), BF16),
            pltpu.VMEM((SHARD_OUT, D_MODEL), BF16),
            pltpu.VMEM((8, LANES), F32),
            pltpu.VMEM((N_DEV, qkv_rows, LANES), BF16),
            pltpu.VMEM((N_DEV, SHARD_OUT, D_MODEL), BF16),
            pltpu.VMEM((N_DEV, 8, LANES), F32),
            pltpu.SemaphoreType.DMA((21,)),
            pltpu.SemaphoreType.DMA((21,)),
            pltpu.SemaphoreType.DMA((3,)),
        ],
        compiler_params=_params("arbitrary", "arbitrary"),
    )(u, w_in_p, w_q, w_kv, w_out, conv_w)


def _rope_tables(tp):
    half = D_ROPE // 2
    inv_freq = 1.0 / (ROPE_THETA ** (jnp.arange(half, dtype=F32) / half))
    pos = (jnp.arange(tp) - PAD_FRONT).astype(F32)
    ang = pos[:, None] * inv_freq[None, :]
    cos = jnp.tile(jnp.cos(ang), (1, LANES // half))
    sin = jnp.tile(jnp.sin(ang), (1, LANES // half))
    first = (jnp.arange(LANES) % D_ROPE) < half
    return cos, jnp.where(first, -sin, 0.0), jnp.where(first, 0.0, sin)


def _rope(t, cos, sa, sb):
    return t * cos + pltpu.roll(t, LANES - D_ROPE // 2, 1) * sa + pltpu.roll(t, D_ROPE // 2, 1) * sb


def _rope_t(t, cos, sa, sb):
    return t * cos + pltpu.roll(t * sa, D_ROPE // 2, 1) + pltpu.roll(t * sb, LANES - D_ROPE // 2, 1)


def _qkv_fwd(p, wq, wkv, gq, gkv, tables, nb_seq, tp):
    ht = tp // 2

    def body(pa_ref, wq_ref, wkv_ref, gq_ref, gkv_ref, cos_ref, sa_ref, sb_ref, q_ref, k_ref, v_ref):
        pa = pa_ref[...].astype(F32)
        cq_hat, _ = _rms_stats(pa[:, :Q_RANK])
        ckv_hat, _ = _rms_stats(pa[:, Q_RANK:Q_RANK + KV_RANK])
        q = _dot((cq_hat * gq_ref[...]).astype(BF16), wq_ref[...]) * Q_SCALE
        kv = _dot((ckv_hat * gkv_ref[...]).astype(BF16), wkv_ref[...])
        tabs = (cos_ref[...], sa_ref[...], sb_ref[...])
        lane = lax.broadcasted_iota(jnp.int32, (ht, LANES), 1)
        low = lane < D_ROPE
        mark = lane == D_ROPE
        row = (pl.program_id(0) % 2) * ht + lax.broadcasted_iota(jnp.int32, (ht, LANES), 0)
        k_pe = jnp.where(mark & (row < PAD_FRONT), NEG_INF, _rope(pa[:, Q_RANK + KV_RANK:], *tabs))
        one = jnp.where(mark & (row >= PAD_FRONT), 1.0, 0.0)
        pairs = [_rope(q[:, N_HEADS * D_NOPE + LANES * i:N_HEADS * D_NOPE + LANES * (i + 1)], *tabs) for i in range(2)]
        for h in range(N_HEADS):
            pair = pairs[h // 2]
            if h % 2:
                pair = pltpu.roll(pair, D_ROPE, 1)
            pe = jnp.where(low, pair, one)
            q_ref[0, h] = jnp.concatenate([q[:, D_NOPE * h:D_NOPE * (h + 1)], pe], axis=1).astype(BF16)
            k_ref[0, h] = jnp.concatenate([kv[:, D_NOPE * h:D_NOPE * (h + 1)], k_pe], axis=1).astype(BF16)
            v_ref[0, h] = kv[:, N_HEADS * D_NOPE + D_V * h:N_HEADS * D_NOPE + D_V * (h + 1)].astype(BF16)

    full = lambda a: pl.BlockSpec(a.shape, lambda i: (0,) * a.ndim)
    tab = pl.BlockSpec((ht, LANES), lambda i: (i % 2, 0))
    qk = pl.BlockSpec((1, N_HEADS, ht, 2 * LANES), lambda i: (i // 2, 0, i % 2, 0))
    return pl.pallas_call(
        body,
        name="qkv_fwd",
        grid=(2 * nb_seq,),
        in_specs=[pl.BlockSpec((ht, GRP_A), lambda i: (i, 0)), full(wq), full(wkv), full(gq), full(gkv), tab, tab, tab],
        out_specs=(qk, qk, pl.BlockSpec((1, N_HEADS, ht, D_V), lambda i: (i // 2, 0, i % 2, 0))),
        out_shape=(
            jax.ShapeDtypeStruct((nb_seq, N_HEADS, tp, 2 * LANES), BF16),
            jax.ShapeDtypeStruct((nb_seq, N_HEADS, tp, 2 * LANES), BF16),
            jax.ShapeDtypeStruct((nb_seq, N_HEADS, tp, D_V), BF16),
        ),
        compiler_params=_params("parallel"),
    )(p, wq, wkv, gq, gkv, *tables)


def _attn_fwd(q, k, v, p, g_attn):
    nb_seq, _, tp, _ = q.shape

    def body(q_ref, k_ref, v_ref, z_ref, g_ref, y_ref, o_ref, lse_ref):
        g = g_ref[...]
        for r0 in range(0, tp, KV_TILE):
            nq = min(KV_TILE, tp - r0)
            kend = r0 + nq
            qq = q_ref[0, 0, r0:kend, :]
            sd = _dot(qq, k_ref[0, 0, r0:kend, :], _NT)
            causal = (lax.broadcasted_iota(jnp.int32, (nq, nq), 1) <= lax.broadcasted_iota(jnp.int32, (nq, nq), 0))
            sd = jnp.where(causal, sd, NEG_INF)
            m = jnp.max(sd, axis=-1, keepdims=True)
            if r0:
                so = _dot(qq, k_ref[0, 0, 0:r0, :], _NT)
                m = jnp.maximum(m, jnp.max(so, axis=-1, keepdims=True))
            ed = jnp.exp2(sd - m)
            l = jnp.sum(ed, axis=-1, keepdims=True)
            o = _dot(ed.astype(BF16), v_ref[0, 0, r0:kend, :])
            if r0:
                eo = jnp.exp2(so - m)
                l = l + jnp.sum(eo, axis=-1, keepdims=True)
                o = o + _dot(eo.astype(BF16), v_ref[0, 0, 0:r0, :])
            o = o * (1.0 / l)
            o_ref[0, 0, r0:kend, :] = o
            lse_ref[0, 0, r0:kend, :] = jnp.broadcast_to(m + jnp.log2(l), (nq, LANES))
            ohat, _ = _rms_stats(o)
            z = z_ref[r0:kend, :].astype(F32)
            y_ref[r0:kend, :] = (ohat * g * (z * _sigmoid(z))).astype(BF16)

    qk = pl.BlockSpec((1, 1, tp, 2 * LANES), lambda b, h: (b, h, 0, 0))
    hv = pl.BlockSpec((1, 1, tp, D_V), lambda b, h: (b, h, 0, 0))
    return pl.pallas_call(
        body,
        name="attn_fwd",
        grid=(nb_seq, N_HEADS),
        in_specs=[qk, qk, hv,
                  pl.BlockSpec((tp, LANES), lambda b, h: (b, GRP_A // LANES + h)),
                  pl.BlockSpec((1, LANES), lambda b, h: (0, h))],
        out_specs=(pl.BlockSpec((tp, LANES), lambda b, h: (b, h)), hv, hv),
        out_shape=(
            jax.ShapeDtypeStruct((nb_seq * tp, N_HEADS * D_V), BF16),
            jax.ShapeDtypeStruct((nb_seq, N_HEADS, tp, D_V), F32),
            jax.ShapeDtypeStruct((nb_seq, N_HEADS, tp, LANES), F32),
        ),
        compiler_params=_params("parallel", "parallel"),
    )(q, k, v, p, g_attn)


_CONV_COL0 = (GRP_A + N_HEADS * D_V) // LANES


def _conv_specs(tp, order):
    cols = CONV_WIDTH // LANES
    return [pl.BlockSpec((tp, LANES), functools.partial(
        lambda a, b, off: order(a, b, off), off=_CONV_COL0 + i * cols)) for i in range(4)]


def _conv_fwd(p, conv_w, g_conv, nb_seq, tp):
    def body(b_ref, c_ref, h_ref, z_ref, w_ref, g_ref, y_ref):
        cc = c_ref[...].astype(F32) * h_ref[...].astype(F32)
        row = lax.broadcasted_iota(jnp.int32, (tp, LANES), 0)
        s1 = jnp.where(row >= 1, pltpu.roll(cc, 1, 0), 0.0)
        s2 = jnp.where(row >= 2, pltpu.roll(cc, 2, 0), 0.0)
        yc = b_ref[...].astype(F32) * (w_ref[0:1, :] * s2 + w_ref[1:2, :] * s1 + w_ref[2:3, :] * cc)
        r = lax.rsqrt(_group_mean(yc * yc) + EPS)
        z = z_ref[...].astype(F32)
        y_ref[...] = (yc * r * g_ref[...] * (z * _sigmoid(z))).astype(BF16)

    return pl.pallas_call(
        body,
        name="conv_fwd",
        grid=(nb_seq, CONV_WIDTH // LANES),
        in_specs=_conv_specs(tp, lambda b, t, off: (b, off + t)) + [
            pl.BlockSpec((8, LANES), lambda b, t: (0, t)),
            pl.BlockSpec((1, LANES), lambda b, t: (0, t))],
        out_specs=pl.BlockSpec((tp, LANES), lambda b, t: (b, t)),
        out_shape=jax.ShapeDtypeStruct((nb_seq * tp, CONV_WIDTH), BF16),
        compiler_params=_params("parallel", "parallel"),
    )(p, p, p, p, conv_w, g_conv)


def _token_copy(hbm, b, k, ts, buf, sem, to_hbm=False):
    lo, hi = max(k * ts - LANES, 0), (k + 1) * ts - LANES
    off = lo - (k * ts - LANES)
    src, dst = hbm.at[b, pl.ds(lo, hi - lo)], buf.at[pl.ds(off, hi - lo)]
    if to_hbm:
        src, dst = dst, src
    return pltpu.make_async_copy(src, dst, sem)


def _for_tile(k, nt, fn):
    for kk in range(nt):
        @pl.when(k == kk)
        def _(kk=kk):
            fn(kk)


def _out_proj_loss(ya, yc, w_out, x, target, g_final, nt):
    nb_seq, s, d = x.shape
    r, ka = ya.shape
    ts = (s + LANES) // nt
    steps = nb_seq * nt

    def body(a_ref, c_ref, w_ref, x_hbm, t_hbm, g_ref, dh_ref, dhb_ref, dg_ref, loss_ref,
             xbuf, tbuf, acc_ref, sems):
        i = pl.program_id(0)
        b, k = i // nt, i % nt

        @pl.when(i == 0)
        def _():
            acc_ref[...] = jnp.zeros_like(acc_ref)
            dg_ref[...] = jnp.zeros_like(dg_ref)

        slot = i % 2

        def fetch(seq, kk, sl):
            return [_token_copy(x_hbm, seq, kk, ts, xbuf.at[sl], sems.at[sl, 0]),
                    _token_copy(t_hbm, seq, kk, ts, tbuf.at[sl], sems.at[sl, 1])]

        def start(seq, sl, kk):
            if kk == 0:
                xbuf[sl, 0:LANES, :] = jnp.zeros((LANES, d), F32)
                tbuf[sl, 0:LANES, :] = jnp.zeros((LANES, d), F32)
            for cp in fetch(seq, kk, sl):
                cp.start()

        @pl.when(i == 0)
        def _():
            start(0, 0, 0)

        @pl.when(i + 1 < steps)
        def _():
            _for_tile((i + 1) % nt, nt, functools.partial(start, (i + 1) // nt, 1 - slot))

        mix = _dot(a_ref[...], w_ref[0:ka, :]) + _dot(c_ref[...], w_ref[ka:, :])
        _for_tile(k, nt, lambda kk: [cp.wait() for cp in fetch(b, kk, slot)])

        real = (lax.broadcasted_iota(jnp.int32, (ts, d), 0) >= LANES) | (k > 0)
        g = g_ref[...]
        hhat, rstd = _rms_stats(xbuf[slot] + mix)
        e = jnp.where(real, hhat * g - tbuf[slot], 0.0)
        acc_ref[...] += jnp.sum(e * e, axis=0, keepdims=True)
        dy = e * (1.0 / d)
        dg_ref[...] += jnp.sum(dy * hhat, axis=0, keepdims=True)
        dh = _rms_bwd(g * dy, hhat, rstd)
        dh_ref[...] = dh
        dhb_ref[...] = dh.astype(BF16)

        @pl.when(i == steps - 1)
        def _():
            total = jnp.sum(acc_ref[...], axis=1, keepdims=True)
            loss_ref[...] = jnp.broadcast_to((0.5 / d) * total, loss_ref.shape)

    hbm = pl.BlockSpec(memory_space=pl.ANY)
    row = pl.BlockSpec((ts, d), lambda i: (i, 0))
    vec = pl.BlockSpec((1, d), lambda i: (0, 0))
    return pl.pallas_call(
        body,
        name="out_proj_loss",
        grid=(steps,),
        in_specs=[pl.BlockSpec((ts, ka), lambda i: (i, 0)), pl.BlockSpec((ts, yc.shape[1]), lambda i: (i, 0)),
                  pl.BlockSpec(w_out.shape, lambda i: (0, 0)), hbm, hbm, vec],
        out_specs=(row, row, vec, pl.BlockSpec((1, LANES), lambda i: (0, 0))),
        out_shape=(
            jax.ShapeDtypeStruct((r, d), F32),
            jax.ShapeDtypeStruct((r, d), BF16),
            jax.ShapeDtypeStruct((1, d), F32),
            jax.ShapeDtypeStruct((1, LANES), F32),
        ),
        scratch_shapes=[pltpu.VMEM((2, ts, d), F32), pltpu.VMEM((2, ts, d), F32), pltpu.VMEM((1, d), F32),
                        pltpu.SemaphoreType.DMA((2, 2))],
        compiler_params=_params("arbitrary"),
    )(ya, yc, w_out, x, target, g_final)


def _out_proj_bwd(dhb, w_out, ya, yc, bm):
    r, d = dhb.shape
    ka = ya.shape[1]
    n_mix = w_out.shape[0]
    last = r // bm - 1

    def body(dh_ref, w_ref, a_ref, c_ref, dcat_ref, dw_ref, acc_ref):
        @pl.when(pl.program_id(0) == 0)
        def _():
            acc_ref[...] = jnp.zeros_like(acc_ref)

        dh = dh_ref[...]
        dcat_ref[...] = _dot(dh, w_ref[...], _NT).astype(BF16)
        acc_ref[0:ka, :] += _dot(a_ref[...], dh, _TN)
        acc_ref[ka:, :] += _dot(c_ref[...], dh, _TN)

        @pl.when(pl.program_id(0) == last)
        def _():
            dw_ref[...] = acc_ref[...].astype(BF16)

    return pl.pallas_call(
        body,
        name="out_proj_bwd",
        grid=(r // bm,),
        in_specs=[pl.BlockSpec((bm, d), lambda i: (i, 0)), pl.BlockSpec(w_out.shape, lambda i: (0, 0)),
                  pl.BlockSpec((bm, ka), lambda i: (i, 0)), pl.BlockSpec((bm, yc.shape[1]), lambda i: (i, 0))],
        out_specs=(pl.BlockSpec((bm, n_mix), lambda i: (i, 0)),
                   pl.BlockSpec((n_mix, d), lambda i: (0, 0))),
        out_shape=(jax.ShapeDtypeStruct((r, n_mix), BF16),
                   jax.ShapeDtypeStruct((n_mix, d), BF16)),
        scratch_shapes=[pltpu.VMEM((n_mix, d), F32)],
        compiler_params=_params("arbitrary"),
    )(dhb, w_out, ya, yc)


def _attn_bwd(q, k, v, o, lse, dcat, p, g_attn, send_out):
    nb_seq, _, tp, _ = q.shape
    steps = N_HEADS * nb_seq

    def body(q_ref, k_ref, v_ref, o_ref, lse_ref, dy_ref, z_ref, g_ref, pay_ref,
             dq_ref, dk_ref, dv_ref, dz_ref, dg_ref, r2_ref, dq_acc, r1, sums, *sems):
        t = pl.program_id(0) * nb_seq + pl.program_id(1)

        def plan():
            return _reduce_plan((pay_ref,), (None,), (r1,), (sums,), (r2_ref,), *sems)

        @pl.when(t == 0)
        def _():
            plan()[0]()

        @pl.when(t == 1)
        def _():
            plan()[1]()

        @pl.when(pl.program_id(1) == 0)
        def _():
            dg_ref[...] = jnp.zeros_like(dg_ref)

        g = g_ref[...]
        z = z_ref[...].astype(F32)
        o = o_ref[0, 0]
        dy = dy_ref[...].astype(F32)
        sig = _sigmoid(z)
        ohat, r = _rms_stats(o)
        don = dy * (z * sig)
        dz_ref[...] = (dy * (ohat * g) * (sig * (1.0 + z * (1.0 - sig)))).astype(BF16)
        dg_ref[...] += jnp.sum(don * ohat, axis=0, keepdims=True)
        do = _rms_bwd(g * don, ohat, r)
        dvec = jnp.sum(do * o, axis=-1, keepdims=True)
        dob = do.astype(BF16)
        lse_col = lse_ref[0, 0, :, 0:1]
        dq_acc[...] = jnp.zeros_like(dq_acc)
        for k0 in range(0, tp, KV_TILE):
            nk = min(KV_TILE, tp - k0)
            nq = tp - k0
            qq = q_ref[0, 0, k0:, :]
            kk = k_ref[0, 0, k0:k0 + nk, :]
            causal = (lax.broadcasted_iota(jnp.int32, (nq, nk), 1) <= lax.broadcasted_iota(jnp.int32, (nq, nk), 0))
            pr = jnp.where(causal, jnp.exp2(_dot(qq, kk, _NT) - lse_col[k0:]), 0.0)
            dp = _dot(dob[k0:], v_ref[0, 0, k0:k0 + nk, :], _NT)
            ds = (pr * (dp - dvec[k0:])).astype(BF16)
            dv_ref[0, 0, k0:k0 + nk, :] = _dot(pr.astype(BF16), dob[k0:], _TN).astype(BF16)
            dk_ref[0, 0, k0:k0 + nk, :] = (_dot(ds, qq, _TN) * (ATTN_SCALE / Q_SCALE)).astype(BF16)
            dq_acc[k0:, :] += _dot(ds, kk)
        dq_ref[0, 0] = (dq_acc[...] * ATTN_SCALE).astype(BF16)

        @pl.when(t == steps - 1)
        def _():
            plan()[2]()

    qk = pl.BlockSpec((1, 1, tp, 2 * LANES), lambda h, b: (b, h, 0, 0))
    hv = pl.BlockSpec((1, 1, tp, D_V), lambda h, b: (b, h, 0, 0))
    col = pl.BlockSpec((tp, LANES), lambda h, b: (b, h))
    slot = send_out.shape[1:]
    return pl.pallas_call(
        body,
        name="attn_bwd",
        grid=(N_HEADS, nb_seq),
        in_specs=[qk, qk, hv, hv, hv, col,
                  pl.BlockSpec((tp, LANES), lambda h, b: (b, GRP_A // LANES + h)),
                  pl.BlockSpec((1, LANES), lambda h, b: (0, h)),
                  pl.BlockSpec(send_out.shape, lambda h, b: (0, 0, 0))],
        out_specs=(qk, qk, hv, col, pl.BlockSpec((1, LANES), lambda h, b: (0, h)),
                   pl.BlockSpec(memory_space=pl.ANY)),
        out_shape=(
            jax.ShapeDtypeStruct((nb_seq, N_HEADS, tp, 2 * LANES), BF16),
            jax.ShapeDtypeStruct((nb_seq, N_HEADS, tp, 2 * LANES), BF16),
            jax.ShapeDtypeStruct((nb_seq, N_HEADS, tp, D_V), BF16),
            jax.ShapeDtypeStruct((nb_seq * tp, N_HEADS * D_V), BF16),
            jax.ShapeDtypeStruct((1, N_HEADS * D_V), F32),
            jax.ShapeDtypeStruct((N_CHIPS,) + slot, BF16),
        ),
        scratch_shapes=[pltpu.VMEM((tp, 2 * LANES), F32)] + _reduce_scratch([(slot, BF16)], [False]),
        compiler_params=_params("arbitrary", "arbitrary"),
    )(q, k, v, o, lse, dcat, p, g_attn, send_out)


def _qkv_bwd(p, dq, dk, dv, wq, wkv, gq, gkv, tables):
    nb_seq, _, tp, _ = dq.shape
    ht = tp // 2

    def body(pa_ref, dq_ref, dk_ref, dv_ref, wq_ref, wkv_ref, gq_ref, gkv_ref, cos_ref, sa_ref, sb_ref,
             dpa_ref, dwq_ref, dwkv_ref, dgq_ref, dgkv_ref):
        @pl.when(pl.program_id(0) == 0)
        def _():
            dwq_ref[...] = jnp.zeros_like(dwq_ref)
            dwkv_ref[...] = jnp.zeros_like(dwkv_ref)
            dgq_ref[...] = jnp.zeros_like(dgq_ref)
            dgkv_ref[...] = jnp.zeros_like(dgkv_ref)

        pa = pa_ref[...].astype(F32)
        gq, gkv = gq_ref[...], gkv_ref[...]
        cq_hat, rq = _rms_stats(pa[:, :Q_RANK])
        ckv_hat, rkv = _rms_stats(pa[:, Q_RANK:Q_RANK + KV_RANK])
        tabs = (cos_ref[...], sa_ref[...], sb_ref[...])

        pe = [dq_ref[0, h, :, D_NOPE:].astype(F32) for h in range(N_HEADS)]
        pairs = [_rope_t(pe[2 * i] + pltpu.roll(pe[2 * i + 1], D_ROPE, 1), *tabs).astype(BF16) for i in range(2)]
        dq_flat = jnp.concatenate([dq_ref[0, h, :, :D_NOPE] for h in range(N_HEADS)] + pairs, axis=1)
        dwq_ref[...] += _dot((cq_hat * gq).astype(BF16), dq_flat, _TN)
        dcqn = _dot(dq_flat, wq_ref[...], _NT)
        dgq_ref[...] += jnp.sum(dcqn * cq_hat, axis=0, keepdims=True)
        dcq = _rms_bwd(gq * dcqn, cq_hat, rq)

        dkv_flat = jnp.concatenate([dk_ref[0, h, :, :D_NOPE] for h in range(N_HEADS)]
                                   + [dv_ref[0, h] for h in range(N_HEADS)], axis=1)
        dwkv_ref[...] += _dot((ckv_hat * gkv).astype(BF16), dkv_flat, _TN)
        dckvn = _dot(dkv_flat, wkv_ref[...], _NT)
        dgkv_ref[...] += jnp.sum(dckvn * ckv_hat, axis=0, keepdims=True)
        dckv = _rms_bwd(gkv * dckvn, ckv_hat, rkv)

        dk_pe = dk_ref[0, 0, :, D_NOPE:].astype(F32)
        for h in range(1, N_HEADS):
            dk_pe = dk_pe + dk_ref[0, h, :, D_NOPE:].astype(F32)
        dk_pe = jnp.where(lax.broadcasted_iota(jnp.int32, (ht, LANES), 1) < D_ROPE, dk_pe, 0.0)
        dpa_ref[...] = jnp.concatenate([dcq, dckv, _rope_t(dk_pe, *tabs)], axis=1).astype(BF16)

    full = lambda a: pl.BlockSpec(a.shape, lambda i: (0,) * a.ndim)
    tab = pl.BlockSpec((ht, LANES), lambda i: (i % 2, 0))
    qk = pl.BlockSpec((1, N_HEADS, ht, 2 * LANES), lambda i: (i // 2, 0, i % 2, 0))
    acc = lambda shape: pl.BlockSpec(shape, lambda i: (0, 0))
    return pl.pallas_call(
        body,
        name="qkv_bwd",
        grid=(2 * nb_seq,),
        in_specs=[pl.BlockSpec((ht, GRP_A), lambda i: (i, 0)), qk, qk,
                  pl.BlockSpec((1, N_HEADS, ht, D_V), lambda i: (i // 2, 0, i % 2, 0)),
                  full(wq), full(wkv), full(gq), full(gkv), tab, tab, tab],
        out_specs=(pl.BlockSpec((ht, GRP_A), lambda i: (i, 0)),
                   acc(wq.shape), acc(wkv.shape), acc((1, Q_RANK)), acc((1, KV_RANK))),
        out_shape=(
            jax.ShapeDtypeStruct((nb_seq * tp, GRP_A), BF16),
            jax.ShapeDtypeStruct(wq.shape, F32),
            jax.ShapeDtypeStruct(wkv.shape, F32),
            jax.ShapeDtypeStruct((1, Q_RANK), F32),
            jax.ShapeDtypeStruct((1, KV_RANK), F32),
        ),
        compiler_params=_params("arbitrary"),
    )(p, dq, dk, dv, wq, wkv, gq, gkv, *tables)


def _conv_bwd(p, dcat, conv_w, g_conv, nb_seq, tp):
    cols = CONV_WIDTH // LANES

    def body(b_ref, c_ref, h_ref, z_ref, dy_ref, w_ref, g_ref,
             db_ref, dc_ref, dh_ref, dz_ref, dw_ref, dg_ref):
        @pl.when(pl.program_id(1) == 0)
        def _():
            dw_ref[...] = jnp.zeros_like(dw_ref)
            dg_ref[...] = jnp.zeros_like(dg_ref)

        cb, c, h = b_ref[...].astype(F32), c_ref[...].astype(F32), h_ref[...].astype(F32)
        z, dy = z_ref[...].astype(F32), dy_ref[...].astype(F32)
        g = g_ref[...]
        w0, w1, w2 = w_ref[0:1, :], w_ref[1:2, :], w_ref[2:3, :]
        cc = c * h
        row = lax.broadcasted_iota(jnp.int32, (tp, LANES), 0)
        s1 = jnp.where(row >= 1, pltpu.roll(cc, 1, 0), 0.0)
        s2 = jnp.where(row >= 2, pltpu.roll(cc, 2, 0), 0.0)
        dwc = w0 * s2 + w1 * s1 + w2 * cc
        yc = cb * dwc
        r = lax.rsqrt(_group_mean(yc * yc) + EPS)
        ychat = yc * r
        sig = _sigmoid(z)
        dz_ref[...] = (dy * (ychat * g) * (sig * (1.0 + z * (1.0 - sig)))).astype(BF16)
        dyn = dy * (z * sig)
        dg_ref[...] += jnp.sum(dyn * ychat, axis=0, keepdims=True)
        gd = g * dyn
        dyc = r * (gd - ychat * _group_mean(gd * ychat))
        db_ref[...] = (dyc * dwc).astype(BF16)
        ddw = dyc * cb
        dw_ref[0:1, :] += jnp.sum(ddw * s2, axis=0, keepdims=True)
        dw_ref[1:2, :] += jnp.sum(ddw * s1, axis=0, keepdims=True)
        dw_ref[2:3, :] += jnp.sum(ddw * cc, axis=0, keepdims=True)
        u1 = jnp.where(row <= tp - 2, pltpu.roll(ddw, tp - 1, 0), 0.0)
        u2 = jnp.where(row <= tp - 3, pltpu.roll(ddw, tp - 2, 0), 0.0)
        dcc = w2 * ddw + w1 * u1 + w0 * u2
        dc_ref[...] = (dcc * h).astype(BF16)
        dh_ref[...] = (dcc * c).astype(BF16)

    col = pl.BlockSpec((tp, LANES), lambda t, b: (b, t))
    out = jax.ShapeDtypeStruct((nb_seq * tp, CONV_WIDTH), BF16)
    return pl.pallas_call(
        body,
        name="conv_bwd",
        grid=(cols, nb_seq),
        in_specs=_conv_specs(tp, lambda t, b, off: (b, off + t)) + [
            pl.BlockSpec((tp, LANES), lambda t, b: (b, N_HEADS * D_V // LANES + t)),
            pl.BlockSpec((8, LANES), lambda t, b: (0, t)),
            pl.BlockSpec((1, LANES), lambda t, b: (0, t))],
        out_specs=(col, col, col, col,
                   pl.BlockSpec((8, LANES), lambda t, b: (0, t)), pl.BlockSpec((1, LANES), lambda t, b: (0, t))),
        out_shape=(out, out, out, out,
                   jax.ShapeDtypeStruct((8, CONV_WIDTH), F32), jax.ShapeDtypeStruct((1, CONV_WIDTH), F32)),
        compiler_params=_params("arbitrary", "arbitrary"),
    )(p, p, p, p, dcat, conv_w, g_conv)


def _input_bwd(dps, w_in, x, meta, dh, norm_g, nt, send_in):
    nb_seq, s, d = x.shape
    r, kb = dps[0].shape
    ts = (s + LANES) // nt
    steps = nb_seq * nt
    n_dp = len(dps)
    in_slot = send_in.shape[1:]

    def body(*refs):
        dp_refs, w_ref, x_hbm, meta_ref, dh_ref, g_ref, pay_ref = refs[:n_dp], *refs[n_dp:n_dp + 6]
        o = n_dp + 6
        gx_hbm, dmeta_ref, dg_ref, r2_in = refs[o:o + 4]
        xbuf, gxbuf, tok_sems, own_in, r1_in, sum_in = refs[o + 4:o + 10]
        sems = refs[o + 10:]
        i = pl.program_id(0)
        b, k = i // nt, i % nt

        def plan():
            return _reduce_plan((pay_ref,), (own_in,), (r1_in,), (sum_in,), (r2_in,), *sems)

        @pl.when(i == 0)
        def _():
            dmeta_ref[...] = jnp.zeros_like(dmeta_ref)
            dg_ref[...] = jnp.zeros_like(dg_ref)
            plan()[0]()

        @pl.when(i == 1)
        def _():
            plan()[1]()

        def start(kk):
            if kk == 0:
                xbuf[0:PAD_FRONT, :] = jnp.zeros((PAD_FRONT, d), F32)
                xbuf[PAD_FRONT:LANES, :] = meta_ref[...]
            _token_copy(x_hbm, b, kk, ts, xbuf, tok_sems.at[0]).start()

        _for_tile(k, nt, start)
        du = _dot(dp_refs[0][...], w_ref[0:kb, :])
        for j in range(1, n_dp):
            du = du + _dot(dp_refs[j][...], w_ref[kb * j:kb * (j + 1), :])
        _for_tile(k, nt, lambda kk: _token_copy(x_hbm, b, kk, ts, xbuf, tok_sems.at[0]).wait())

        g = g_ref[...]
        hhat, rstd = _rms_stats(xbuf[...])
        dg_ref[...] += jnp.sum(du * hhat, axis=0, keepdims=True)
        res = _rms_bwd(g * du, hhat, rstd) + dh_ref[...]

        @pl.when(i > 0)
        def _():
            _for_tile(k, nt, lambda kk: _token_copy(gx_hbm, b, (kk - 1) % nt, ts, gxbuf, tok_sems.at[1], True).wait())

        gxbuf[...] = res

        @pl.when(k == 0)
        def _():
            dmeta_ref[...] += gxbuf[PAD_FRONT:LANES, :]

        _for_tile(k, nt, lambda kk: _token_copy(gx_hbm, b, kk, ts, gxbuf, tok_sems.at[1], True).start())

        @pl.when(i == steps - 1)
        def _():
            _token_copy(gx_hbm, b, nt - 1, ts, gxbuf, tok_sems.at[1], True).wait()
            plan()[2]()

    whole = lambda a: pl.BlockSpec(a.shape, lambda i: (0,) * a.ndim)
    hbm = pl.BlockSpec(memory_space=pl.ANY)
    return pl.pallas_call(
        body,
        name="input_bwd",
        grid=(steps,),
        in_specs=[pl.BlockSpec((ts, kb), lambda i: (i, 0)) for _ in dps]
        + [whole(w_in), hbm, whole(meta), pl.BlockSpec((ts, d), lambda i: (i, 0)), whole(norm_g), hbm],
        out_specs=(hbm, pl.BlockSpec((N_META, d), lambda i: (0, 0)), pl.BlockSpec((1, d), lambda i: (0, 0)), hbm),
        out_shape=(jax.ShapeDtypeStruct((nb_seq, s, d), F32),
                   jax.ShapeDtypeStruct((N_META, d), F32),
                   jax.ShapeDtypeStruct((1, d), F32),
                   jax.ShapeDtypeStruct((N_CHIPS,) + in_slot, BF16)),
        scratch_shapes=[pltpu.VMEM((ts, d), F32), pltpu.VMEM((ts, d), F32), pltpu.SemaphoreType.DMA((2,))]
        + _reduce_scratch([(in_slot, BF16)], [True]),
        compiler_params=_params("arbitrary"),
    )(*dps, w_in, x, meta, dh, norm_g, send_in)


def _in_proj_bwd_w(u, dps, bm, small_grads):
    r, d = u.shape
    kb = dps[0].shape[1]
    steps = r // bm
    n_dp, n_small = len(dps), len(small_grads)
    small_slot = (SMALL_ROWS, LANES)

    def body(*refs):
        u_ref, dp_refs = refs[0], refs[1:1 + n_dp]
        small_refs = refs[1 + n_dp:1 + n_dp + n_small]
        o = 1 + n_dp + n_small
        o_ref, r2_small = refs[o:o + 2]
        acc_ref, ssmall, r1_small, sum_small = refs[o + 2:o + 6]
        sems = refs[o + 6:]
        i = pl.program_id(0)

        def plan():
            return _reduce_plan((ssmall,), (None,), (r1_small,), (sum_small,), (r2_small,), *sems)

        @pl.when(i == 0)
        def _():
            acc_ref[...] = jnp.zeros_like(acc_ref)
            _pack_small(ssmall, *small_refs)
            plan()[0]()

        @pl.when(i == 1)
        def _():
            plan()[1]()

        uu = u_ref[...]
        for j in range(n_dp):
            acc_ref[kb * j:kb * (j + 1), :] += _dot(dp_refs[j][...], uu, _TN)

        @pl.when(i == steps - 1)
        def _():
            for k in range(N_DEV):
                for s, e, c0 in _in_pieces(k):
                    o_ref[k, s:e, :] = acc_ref[c0:c0 + e - s, :].astype(BF16)
                o_ref[k, SHARD_IN:, :] = jnp.zeros((SHARD_IN_PAD - SHARD_IN, d), BF16)
            plan()[2]()

    whole = lambda a: pl.BlockSpec(a.shape, lambda i: (0,) * a.ndim)
    return pl.pallas_call(
        body,
        name="in_proj_bwd_w",
        grid=(steps,),
        in_specs=[pl.BlockSpec((bm, d), lambda i: (i, 0))]
        + [pl.BlockSpec((bm, kb), lambda i: (i, 0)) for _ in dps] + [whole(a) for a in small_grads],
        out_specs=(pl.BlockSpec((N_DEV, SHARD_IN_PAD, d), lambda i: (0, 0, 0)), pl.BlockSpec(memory_space=pl.ANY)),
        out_shape=(jax.ShapeDtypeStruct((N_DEV, SHARD_IN_PAD, d), BF16),
                   jax.ShapeDtypeStruct((N_CHIPS,) + small_slot, F32)),
        scratch_shapes=[pltpu.VMEM((kb * n_dp, d), F32), pltpu.VMEM((N_DEV,) + small_slot, F32)]
        + _reduce_scratch([(small_slot, F32)], [False]),
        compiler_params=_params("arbitrary"),
    )(u, *dps, *small_grads)


def _local_step(x, loss_target, u, p, meta_f, norm_g, w_in_p, q_norm_g, w_q_p, kv_norm_g, w_kv_p, conv_w_f,
                attn_out_g, conv_out_g, w_out_f, g_final):
    nb_seq, s, d = x.shape
    tp = s + LANES
    ht = tp // 2
    tables = _rope_tables(tp)

    q, k, v = _qkv_fwd(p, w_q_p, w_kv_p, q_norm_g, kv_norm_g, tables, nb_seq, tp)
    ya, o, lse = _attn_fwd(q, k, v, p, attn_out_g)
    yc = _conv_fwd(p, conv_w_f, conv_out_g, nb_seq, tp)
    dh, dhb, d_final_g, loss_part = _out_proj_loss(ya, yc, w_out_f, x, loss_target, g_final, TOKEN_TILES)

    dcat, d_w_out = _out_proj_bwd(dhb, w_out_f, ya, yc, ht)
    send_out = d_w_out.reshape(N_DEV, SHARD_OUT, d)
    dq, dk, dv, dz_attn, d_attn_g, r_out = _attn_bwd(q, k, v, o, lse, dcat, p, attn_out_g, send_out)
    dpa, d_wq_p, d_wkv_p, d_gq, d_gkv = _qkv_bwd(p, dq, dk, dv, w_q_p, w_kv_p, q_norm_g, kv_norm_g, tables)
    d_b, d_c, d_h, dz_conv, d_conv_w, d_conv_g = _conv_bwd(p, dcat, conv_w_f, conv_out_g, nb_seq, tp)
    dps = (dpa, dz_attn, d_b, d_c, d_h, dz_conv)
    small = (d_wq_p, d_wkv_p, d_conv_w, d_final_g, d_gq, d_gkv, d_attn_g, d_conv_g, loss_part)
    send_in, r_small = _in_proj_bwd_w(u, dps, ht // 2, small)
    grad_x, d_meta, d_norm_g, r_in = _input_bwd(dps, w_in_p, x, meta_f, dh, norm_g, TOKEN_TILES, send_in)
    return grad_x, r_in, r_out, r_small, d_meta, d_norm_g


def kernel(x, meta_tokens, norm_g, w_in, q_norm_g, w_q_up, kv_norm_g, w_kv_up, conv_w, attn_out_g, conv_out_g, w_out, final_norm_g, loss_target, m_meta_tokens, m_norm_g, m_w_in, m_q_norm_g, m_w_q_up, m_kv_norm_g, m_w_kv_up, m_conv_w, m_attn_out_g, m_conv_out_g, m_w_out, m_final_norm_g, v_meta_tokens, v_norm_g, v_w_in, v_q_norm_g, v_w_q_up, v_kv_norm_g, v_w_kv_up, v_conv_w, v_attn_out_g, v_conv_out_g, v_w_out, v_final_norm_g):
    d = x.shape[-1]
    ht = (x.shape[1] + LANES) // 2
    u, w_in_p, meta_f = _prep_gather(x, meta_tokens, norm_g, w_in[0].T)
    p, w_q_p, w_kv_p, w_out_f, conv_w_f = _in_proj_gather(
        u, w_in_p, w_q_up[0], w_kv_up[0], w_out[0], conv_w[0], ht, GRP_A)
    g_final = final_norm_g.reshape(1, d)
    grad_x, r_in, r_out, r_small, d_meta, d_norm_g = _local_step(
        x, loss_target, u, p, meta_f, norm_g, w_in_p, q_norm_g, w_q_p, kv_norm_g, w_kv_p, conv_w_f,
        attn_out_g, conv_out_g, w_out_f, g_final)

    flat = lambda a: a.reshape(a.shape[-2:]) if a.ndim == 3 else a.reshape(1, -1) if a.ndim == 1 else a
    transposed = ("w_in",)
    to_kernel = lambda n, a: flat(a).T if n in transposed else flat(a)
    from_kernel = lambda n, a, shape: (a.T if n in transposed else a).reshape(shape)
    params = {
        "meta_tokens": (meta_tokens, m_meta_tokens, v_meta_tokens),
        "norm_g": (norm_g, m_norm_g, v_norm_g),
        "w_in": (w_in, m_w_in, v_w_in),
        "q_norm_g": (q_norm_g, m_q_norm_g, v_q_norm_g),
        "w_q_up": (w_q_up, m_w_q_up, v_w_q_up),
        "kv_norm_g": (kv_norm_g, m_kv_norm_g, v_kv_norm_g),
        "w_kv_up": (w_kv_up, m_w_kv_up, v_w_kv_up),
        "conv_w": (conv_w, m_conv_w, v_conv_w),
        "attn_out_g": (attn_out_g, m_attn_out_g, v_attn_out_g),
        "conv_out_g": (conv_out_g, m_conv_out_g, v_conv_out_g),
        "w_out": (w_out, m_w_out, v_w_out),
        "final_norm_g": (final_norm_g, m_final_norm_g, v_final_norm_g),
    }
    grads, loss = _reduce_tail(r_in, r_out, r_small, d_meta, d_norm_g)
    updated = _adamw(grads, {n: tuple(to_kernel(n, a) for a in t) for n, t in params.items()})
    outs = [[from_kernel(n, updated[n][i], params[n][0].shape) for n, _ in PARAM_SHAPES] for i in range(4)]
    return (loss[0, 0], grad_x, *outs[0], *outs[1], *outs[2], *outs[3])
```

```python
import functools

import jax
import jax.numpy as jnp
from jax import lax
from jax.experimental import pallas as pl
from jax.experimental.pallas import tpu as pltpu

F32 = jnp.float32
BF16 = jnp.bfloat16

N_META = 16
D_MODEL = 1024
N_HEADS = 4
D_NOPE = 128
D_ROPE = 64
D_V = 128
Q_RANK = 256
KV_RANK = 128
CONV_WIDTH = 512
CONV_GROUP = 64
ROPE_THETA = 10000.0
ATTN_SCALE = (D_NOPE + D_ROPE) ** -0.5
Q_SCALE = ATTN_SCALE * 1.4426950408889634
EPS = 1e-6
NEG_INF = -1e30

ADAM_LR = 0.001
ADAM_B1 = 0.9
ADAM_B2 = 0.999
ADAM_EPS = 1e-08
ADAM_WD = 0.01
ADAM_STEP = 10

LANES = 128
PAD_FRONT = LANES - N_META
KV_TILE = 256
N_DEV = 8
VMEM_LIMIT = 56 * 1024 * 1024

IN_PAD = 3072
GRP_A = 512
N_A = Q_RANK + KV_RANK + D_ROPE
IN_PROJ = 3008
SHARD_IN = IN_PROJ // N_DEV
SHARD_IN_PAD = 384
SHARD_Q = 96
SHARD_KV = 128
SHARD_OUT = 128
SHARD_CONV = 64
SHARD_META = 128
Q_COLS = N_HEADS * (D_NOPE + D_ROPE)
KV_COLS = N_HEADS * (D_NOPE + D_V)

ROW_Q, ROW_KV, ROW_META, ROW_CONV = 0, 256, 384, 400
ROW_REPL = 408
ROW_NORM, ROW_FINAL, ROW_GQ, ROW_GKV, ROW_ATTN, ROW_CONVG, ROW_LOSS = 408, 416, 424, 426, 427, 431, 435
SMALL_ROWS = 440

PARAM_SHAPES = (
    ("meta_tokens", (N_META, SHARD_META)), ("norm_g", (1, D_MODEL)), ("w_in", (SHARD_IN, D_MODEL)),
    ("q_norm_g", (1, Q_RANK)), ("w_q_up", (Q_RANK, SHARD_Q)), ("kv_norm_g", (1, KV_RANK)),
    ("w_kv_up", (KV_RANK, SHARD_KV)), ("conv_w", (3, SHARD_CONV)), ("attn_out_g", (1, CONV_WIDTH)),
    ("conv_out_g", (1, CONV_WIDTH)), ("w_out", (SHARD_OUT, D_MODEL)), ("final_norm_g", (1, D_MODEL)),
)


def _in_pieces(k):
    lo, hi = SHARD_IN * k, SHARD_IN * (k + 1)
    out = []
    if lo < N_A:
        out.append((0, min(hi, N_A) - lo, lo))
    if hi > N_A:
        s = max(lo, N_A)
        out.append((s - lo, hi - lo, s + GRP_A - N_A))
    return out


def _q_pieces(k):
    lo, hi = SHARD_Q * k, SHARD_Q * (k + 1)
    out = []
    for h in range(N_HEADS):
        base = (D_NOPE + D_ROPE) * h
        s, e = max(lo, base), min(hi, base + D_NOPE)
        if s < e:
            out.append((s - lo, e - lo, D_NOPE * h + s - base))
        s, e = max(lo, base + D_NOPE), min(hi, base + D_NOPE + D_ROPE)
        if s < e:
            out.append((s - lo, e - lo, N_HEADS * D_NOPE + D_ROPE * h + s - base - D_NOPE))
    return out


def _kv_dst(k):
    return D_NOPE * (k // 2) + (N_HEADS * D_NOPE if k % 2 else 0)


def _params(*sem):
    return pltpu.CompilerParams(dimension_semantics=sem, vmem_limit_bytes=VMEM_LIMIT)


def _rms_stats(x):
    r = lax.rsqrt(jnp.mean(x * x, axis=-1, keepdims=True) + EPS)
    return x * r, r


def _rms_bwd(gdy, xhat, r):
    return r * (gdy - xhat * jnp.mean(gdy * xhat, axis=-1, keepdims=True))


def _sigmoid(z):
    return 1.0 / (1.0 + jnp.exp(-z))


def _group_mean(x):
    i0 = lax.broadcasted_iota(jnp.int32, (LANES, LANES), 0) // CONV_GROUP
    i1 = lax.broadcasted_iota(jnp.int32, (LANES, LANES), 1) // CONV_GROUP
    m = jnp.where(i0 == i1, 1.0 / CONV_GROUP, 0.0).astype(BF16)
    hi = x.astype(BF16)
    lo = (x - hi.astype(F32)).astype(BF16)
    return jnp.dot(hi, m, preferred_element_type=F32) + jnp.dot(lo, m, preferred_element_type=F32)


_NT = (((1,), (1,)), ((), ()))
_TN = (((0,), (0,)), ((), ()))


def _dot(a, b, dims=None):
    if dims is None:
        return jnp.dot(a, b, preferred_element_type=F32)
    return lax.dot_general(a, b, dims, preferred_element_type=F32)


def _device_position():
    x, y, c = lax.axis_index("x"), lax.axis_index("y"), lax.axis_index("c")
    return x, y, c, 4 * x + 2 * y + c


def _gather_plan(srcs, slots, send_sems, recv_sems, local_sems):
    x, y, c, _ = _device_position()
    me, sibling = (x, y, c), (x, y, 1 - c)
    flip = lambda v, on: v + on - 2 * v * on
    near = (flip(x, 1 - c), flip(y, c))
    far = (flip(x, c), flip(y, 1 - c))
    diag = (1 - x, 1 - y)
    n = len(srcs)

    def slot(a, px, py, pc):
        return slots[a].at[4 * px + 2 * py + pc]

    def copy(a, k, block, to, own=False):
        return pltpu.make_async_remote_copy(
            src_ref=srcs[a] if own else slot(a, *block),
            dst_ref=slot(a, *block),
            send_sem=send_sems.at[7 * a + k],
            recv_sem=recv_sems.at[7 * a + k],
            device_id=to,
            device_id_type=pl.DeviceIdType.MESH,
        )

    def local(a):
        return pltpu.make_async_copy(srcs[a], slot(a, *me), local_sems.at[a])

    def sends(a):
        return [copy(a, 0, me, sibling, own=True), copy(a, 1, me, (*near, c), own=True),
                copy(a, 2, me, (*far, c), own=True), copy(a, 3, (*near, c), (*far, c)),
                copy(a, 4, (*near, c), sibling), copy(a, 5, (*far, c), sibling), copy(a, 6, (*diag, c), sibling)]

    def arrivals(a):
        return [copy(a, 0, sibling, me), copy(a, 1, (*near, c), me), copy(a, 2, (*far, c), me),
                copy(a, 3, (*diag, c), me), copy(a, 4, (*far, 1 - c), me), copy(a, 5, (*near, 1 - c), me),
                copy(a, 6, (*diag, 1 - c), me)]

    def start():
        for a in range(n):
            local(a).start()
            for cp in sends(a)[:3]:
                cp.start()

    def relay():
        for a in range(n):
            arrivals(a)[1].wait_recv()
            sends(a)[3].start()
            sends(a)[4].start()

    def forward():
        for k in (2, 3):
            for a in range(n):
                arrivals(a)[k].wait_recv()
                sends(a)[k + 3].start()

    def finish():
        for a in range(n):
            for k in (0, 4, 5, 6):
                arrivals(a)[k].wait_recv()
        for a in range(n):
            for cp in sends(a):
                cp.wait_send()
            local(a).wait()

    return start, relay, forward, finish


def _adam_update(g, w, m, v):
    m_new = ADAM_B1 * m + (1.0 - ADAM_B1) * g
    v_new = ADAM_B2 * v + (1.0 - ADAM_B2) * (g * g)
    m_hat = m_new / (1.0 - ADAM_B1 ** ADAM_STEP)
    v_hat = v_new / (1.0 - ADAM_B2 ** ADAM_STEP)
    return -ADAM_LR * (m_hat / (jnp.sqrt(v_hat) + ADAM_EPS) + ADAM_WD * w), m_new, v_new


def _adamw(grads, params):
    names = [n for n, _ in PARAM_SHAPES]
    n_p = len(names)

    def body(*refs):
        for i in range(n_p):
            g = refs[i][...]
            w, m, v = (refs[n_p + 3 * i + j][...] for j in range(3))
            delta, m_new, v_new = _adam_update(g, w, m, v)
            for j, val in enumerate((g, delta, m_new, v_new)):
                refs[4 * n_p + 4 * i + j][...] = val

    vm = pl.BlockSpec(memory_space=pltpu.VMEM)
    out_shape = []
    for _, shape in PARAM_SHAPES:
        out_shape += [jax.ShapeDtypeStruct(shape, F32)] * 4
    outs = pl.pallas_call(
        body,
        name="adamw",
        out_shape=tuple(out_shape),
        in_specs=[vm] * (4 * n_p),
        out_specs=(vm,) * (4 * n_p),
        compiler_params=pltpu.CompilerParams(vmem_limit_bytes=VMEM_LIMIT),
    )(*[grads[n] for n in names], *[a for n in names for a in params[n]])
    return {n: outs[4 * i:4 * i + 4] for i, n in enumerate(names)}


N_CHIPS = 4


def _reduce_plan(pays, owns, r1s, sums, r2s, send1, recv1, send2, recv2, local_sems):
    x, y, c, _ = _device_position()
    sibling = (x, y, 1 - c)
    chips = [((1 - x if rj & 2 else x), (1 - y if rj & 1 else y)) for rj in range(N_CHIPS)]
    n = len(pays)

    def slot_of(rj, core):
        return 4 * chips[rj][0] + 2 * chips[rj][1] + core

    def to_sibling(a, rj):
        return pltpu.make_async_remote_copy(
            src_ref=pays[a].at[slot_of(rj, 1 - c)], dst_ref=r1s[a].at[rj],
            send_sem=send1.at[N_CHIPS * a + rj], recv_sem=recv1.at[N_CHIPS * a + rj],
            device_id=sibling, device_id_type=pl.DeviceIdType.MESH)

    def load_own(a, rj):
        return pltpu.make_async_copy(pays[a].at[slot_of(rj, c)], owns[a].at[rj], local_sems.at[2 * N_CHIPS * a + rj])

    def to_chip(a, rj):
        return pltpu.make_async_remote_copy(
            src_ref=sums[a].at[rj], dst_ref=r2s[a].at[rj],
            send_sem=send2.at[N_CHIPS * a + rj], recv_sem=recv2.at[N_CHIPS * a + rj],
            device_id=(*chips[rj], c), device_id_type=pl.DeviceIdType.MESH)

    def keep(a):
        return pltpu.make_async_copy(sums[a].at[0], r2s[a].at[0], local_sems.at[2 * N_CHIPS * a + N_CHIPS])

    def start():
        for a in range(n):
            for rj in range(N_CHIPS):
                to_sibling(a, rj).start()
                if owns[a] is not None:
                    load_own(a, rj).start()

    def combine():
        for a in range(n):
            for rj in range(N_CHIPS):
                to_sibling(a, rj).wait_recv()
                if owns[a] is not None:
                    load_own(a, rj).wait()
                    mine = owns[a][rj]
                else:
                    mine = pays[a][slot_of(rj, c)]
                sums[a][rj] = (mine.astype(F32) + r1s[a][rj].astype(F32)).astype(sums[a].dtype)
            keep(a).start()
            for rj in range(1, N_CHIPS):
                to_chip(a, rj).start()

    def finish():
        for a in range(n):
            for rj in range(1, N_CHIPS):
                to_chip(a, rj).wait_recv()
            for rj in range(N_CHIPS):
                to_sibling(a, rj).wait_send()
            for rj in range(1, N_CHIPS):
                to_chip(a, rj).wait_send()
            keep(a).wait()

    return start, combine, finish


def _reduce_scratch(shapes_dtypes, own_flags):
    out = []
    for (shape, dtype), own in zip(shapes_dtypes, own_flags):
        if own:
            out.append(pltpu.VMEM((N_CHIPS,) + shape, dtype))
        out += [pltpu.VMEM((N_CHIPS,) + shape, dtype), pltpu.VMEM((N_CHIPS,) + shape, dtype)]
    n = len(shapes_dtypes)
    out += [pltpu.SemaphoreType.DMA((N_CHIPS * n,))] * 4 + [pltpu.SemaphoreType.DMA((2 * N_CHIPS * n,))]
    return out


def _pack_small(ssmall, dwq, dwkv, dconv, dfinal, dgq, dgkv, dattn, dconvg, loss_part):
    ssmall[...] = jnp.zeros_like(ssmall)
    rep = ssmall.at[0]
    for i in range(D_MODEL // LANES):
        rep[ROW_FINAL + i:ROW_FINAL + i + 1, :] = dfinal[:, LANES * i:LANES * (i + 1)]
    for i in range(Q_RANK // LANES):
        rep[ROW_GQ + i:ROW_GQ + i + 1, :] = dgq[:, LANES * i:LANES * (i + 1)]
    rep[ROW_GKV:ROW_GKV + 1, :] = dgkv[...]
    for i in range(CONV_WIDTH // LANES):
        rep[ROW_ATTN + i:ROW_ATTN + i + 1, :] = dattn[:, LANES * i:LANES * (i + 1)]
        rep[ROW_CONVG + i:ROW_CONVG + i + 1, :] = dconvg[:, LANES * i:LANES * (i + 1)]
    rep[ROW_LOSS:ROW_LOSS + 1, :] = loss_part[...]
    for k in range(N_DEV):
        if k:
            ssmall[k, ROW_REPL:, :] = ssmall[0, ROW_REPL:, :]
        for s, e, d in _q_pieces(k):
            ssmall[k, ROW_Q:ROW_Q + Q_RANK, s:e] = dwq[:, d:d + e - s]
        ssmall[k, ROW_KV:ROW_KV + KV_RANK, :] = dwkv[:, _kv_dst(k):_kv_dst(k) + SHARD_KV]
        ssmall[k, ROW_CONV:ROW_CONV + 3, 0:SHARD_CONV] = dconv[0:3, SHARD_CONV * k:SHARD_CONV * (k + 1)]


TOKEN_TILES = 4
TAIL_ROWS = N_META + D_MODEL // LANES


def _reduce_tail(r_in, r_out, r_small, d_meta, d_norm):
    n_p = len(PARAM_SHAPES)
    names = [n for n, _ in PARAM_SHAPES]

    def body(*refs):
        rin, rout, rsmall, dmeta, dnorm = refs[:5]
        g_out = {n: refs[5 + i] for i, n in enumerate(names)}
        loss_out = refs[5 + n_p]
        stail, rtail, gsum, gtail, send_sems, recv_sems = refs[6 + n_p:]
        x, y, c, me = _device_position()
        my_chip = 2 * x + y

        for k in range(N_DEV):
            stail[k, 0:N_META, :] = dmeta[:, SHARD_META * k:SHARD_META * (k + 1)]
            for i in range(D_MODEL // LANES):
                stail[k, N_META + i:N_META + i + 1, :] = dnorm[:, LANES * i:LANES * (i + 1)]
        copies = []
        for r in range(1, N_DEV):
            peer = (1 - x if r & 4 else x, 1 - y if r & 2 else y, 1 - c if r & 1 else c)
            copies.append(pltpu.make_async_remote_copy(
                src_ref=stail.at[4 * peer[0] + 2 * peer[1] + peer[2]],
                dst_ref=rtail.at[r],
                send_sem=send_sems.at[r - 1],
                recv_sem=recv_sems.at[r - 1],
                device_id=peer,
                device_id_type=pl.DeviceIdType.MESH,
            ))
        for cp in copies:
            cp.start()
        rtail[0] = stail[me]

        g = rin[my_chip].astype(F32)
        for ch in range(1, N_CHIPS):
            g = g + rin[ch ^ my_chip].astype(F32)
        g_out["w_in"][...] = g[:SHARD_IN, :]

        g = rout[my_chip].astype(F32)
        gs = rsmall[my_chip]
        for ch in range(1, N_CHIPS):
            g = g + rout[ch ^ my_chip].astype(F32)
            gs = gs + rsmall[ch ^ my_chip]
        g_out["w_out"][...] = g
        gsum[...] = gs
        g_out["w_q_up"][...] = gsum[ROW_Q:ROW_Q + Q_RANK, 0:SHARD_Q]
        g_out["w_kv_up"][...] = gsum[ROW_KV:ROW_KV + KV_RANK, :]
        g_out["conv_w"][...] = gsum[ROW_CONV:ROW_CONV + 3, 0:SHARD_CONV]
        for name, row, width in (("final_norm_g", ROW_FINAL, D_MODEL), ("q_norm_g", ROW_GQ, Q_RANK),
                                 ("kv_norm_g", ROW_GKV, KV_RANK), ("attn_out_g", ROW_ATTN, CONV_WIDTH),
                                 ("conv_out_g", ROW_CONVG, CONV_WIDTH)):
            for i in range(width // LANES):
                g_out[name][:, LANES * i:LANES * (i + 1)] = gsum[row + i:row + i + 1, :]
        loss_out[...] = gsum[ROW_LOSS:ROW_LOSS + 1, :]

        for cp in copies:
            cp.wait_recv()
        gt = rtail[me]
        for d in range(1, N_DEV):
            gt = gt + rtail[d ^ me]
        gtail[...] = gt
        g_out["meta_tokens"][...] = gtail[0:N_META, :]
        for i in range(D_MODEL // LANES):
            g_out["norm_g"][:, LANES * i:LANES * (i + 1)] = gtail[N_META + i:N_META + i + 1, :]
        for cp in copies:
            cp.wait_send()

    vm = pl.BlockSpec(memory_space=pltpu.VMEM)
    out_shape = [jax.ShapeDtypeStruct(shape, F32) for _, shape in PARAM_SHAPES]
    out_shape.append(jax.ShapeDtypeStruct((1, LANES), F32))
    outs = pl.pallas_call(
        body,
        name="reduce_tail",
        out_shape=tuple(out_shape),
        in_specs=[vm] * 5,
        out_specs=(vm,) * len(out_shape),
        scratch_shapes=[
            pltpu.VMEM((N_DEV, TAIL_ROWS, LANES), F32),
            pltpu.VMEM((N_DEV, TAIL_ROWS, LANES), F32),
            pltpu.VMEM((SMALL_ROWS, LANES), F32),
            pltpu.VMEM((TAIL_ROWS, LANES), F32),
            pltpu.SemaphoreType.DMA((N_DEV - 1,)),
            pltpu.SemaphoreType.DMA((N_DEV - 1,)),
        ],
        compiler_params=pltpu.CompilerParams(vmem_limit_bytes=VMEM_LIMIT),
    )(r_in, r_out, r_small, d_meta, d_norm)
    return {n: outs[i] for i, n in enumerate(names)}, outs[-1]


def _prep_gather(x, meta, norm_g, w_in_t):
    nb_seq, s, d = x.shape
    nb = s // LANES + 1
    relay_step = nb_seq * (nb - 1) // 2
    forward_step = nb_seq * (nb - 1) - 1
    finish_step = nb_seq * (nb - 1)

    def body(x_ref, meta_ref, g_ref, win_ref, u_ref, w_in_p, meta_f,
             sbig, ssmall, gbig, gsmall, send_sems, recv_sems, local_sems):
        jj, b = pl.program_id(0), pl.program_id(1)
        t = jj * nb_seq + b

        def plan():
            return _gather_plan((sbig, ssmall), (gbig, gsmall), send_sems, recv_sems, local_sems)

        @pl.when(t == 0)
        def _():
            sbig[0:SHARD_IN, :] = win_ref[...].astype(BF16)
            sbig[SHARD_IN:, :] = jnp.zeros((SHARD_IN_PAD - SHARD_IN, d), BF16)
            ssmall[...] = meta_ref[...]
            plan()[0]()

        @pl.when(t == relay_step)
        def _():
            plan()[1]()

        @pl.when(t == forward_step)
        def _():
            plan()[2]()

        @pl.when(t == finish_step)
        def _():
            plan()[3]()
            w_in_p[N_A:GRP_A, :] = jnp.zeros((GRP_A - N_A, d), BF16)
            for k in range(N_DEV):
                for s0, e0, d0 in _in_pieces(k):
                    w_in_p[d0:d0 + e0 - s0, :] = gbig[k, s0:e0, :]
                meta_f[:, SHARD_META * k:SHARD_META * (k + 1)] = gsmall[k]

        def norm(h):
            hhat, _ = _rms_stats(h)
            return (hhat * g_ref[...]).astype(BF16)

        @pl.when(jj < nb - 1)
        def _():
            u_ref[...] = norm(x_ref[0])

        @pl.when(jj == nb - 1)
        def _():
            u_ref[0:PAD_FRONT, :] = jnp.zeros((PAD_FRONT, d), BF16)
            u_ref[PAD_FRONT:LANES, :] = norm(meta_f[...])

    whole = lambda shape: pl.BlockSpec(shape, lambda jj, b: (0,) * len(shape))
    return pl.pallas_call(
        body,
        name="prep_norm_gather",
        grid=(nb, nb_seq),
        in_specs=[
            pl.BlockSpec((1, LANES, d), lambda jj, b: (b, jnp.minimum(jj, nb - 2), 0)),
            whole(meta.shape), whole(norm_g.shape), whole(w_in_t.shape),
        ],
        out_specs=(pl.BlockSpec((LANES, d), lambda jj, b: (b * nb + (jj + 1) % nb, 0)),
                   whole((IN_PAD, d)), whole((N_META, d))),
        out_shape=(jax.ShapeDtypeStruct((nb_seq * nb * LANES, d), BF16),
                   jax.ShapeDtypeStruct((IN_PAD, d), BF16),
                   jax.ShapeDtypeStruct((N_META, d), F32)),
        scratch_shapes=[
            pltpu.VMEM((SHARD_IN_PAD, d), BF16),
            pltpu.VMEM((N_META, SHARD_META), F32),
            pltpu.VMEM((N_DEV, SHARD_IN_PAD, d), BF16),
            pltpu.VMEM((N_DEV, N_META, SHARD_META), F32),
            pltpu.SemaphoreType.DMA((14,)),
            pltpu.SemaphoreType.DMA((14,)),
            pltpu.SemaphoreType.DMA((2,)),
        ],
        compiler_params=_params("arbitrary", "arbitrary"),
    )(x, meta, norm_g, w_in_t)


def _in_proj_gather(u, w_in_p, w_q, w_kv, w_out, conv_w, bm, bn):
    m, k_dim = u.shape
    n = w_in_p.shape[0]
    steps = (m // bm) * (n // bn)
    relay_step, forward_step = steps // 3, 2 * steps // 3
    qkv_rows = Q_RANK + KV_RANK

    def body(a_ref, b_ref, wq_ref, wkv_ref, wout_ref, conv_ref, o_ref, w_q_p, w_kv_p, w_out_f, conv_f,
             sqkv, sout, sconv, gqkv, gout, gconv, send_sems, recv_sems, local_sems):
        t = pl.program_id(0) * (n // bn) + pl.program_id(1)

        def plan():
            return _gather_plan((sqkv, sout, sconv), (gqkv, gout, gconv), send_sems, recv_sems, local_sems)

        @pl.when(t == 0)
        def _():
            sqkv[...] = jnp.zeros_like(sqkv)
            sqkv[0:Q_RANK, 0:SHARD_Q] = wq_ref[...].astype(BF16)
            sqkv[Q_RANK:, :] = wkv_ref[...].astype(BF16)
            sout[...] = wout_ref[...].astype(BF16)
            sconv[...] = jnp.zeros_like(sconv)
            sconv[0:3, 0:SHARD_CONV] = conv_ref[...]
            plan()[0]()

        @pl.when(t == relay_step)
        def _():
            plan()[1]()

        @pl.when(t == forward_step)
        def _():
            plan()[2]()

        o_ref[...] = _dot(a_ref[...], b_ref[...], _NT).astype(o_ref.dtype)

        @pl.when(t == steps - 1)
        def _():
            plan()[3]()
            conv_f[...] = jnp.zeros_like(conv_f)
            for k in range(N_DEV):
                for s0, e0, d0 in _q_pieces(k):
                    w_q_p[:, d0:d0 + e0 - s0] = gqkv[k, 0:Q_RANK, s0:e0]
                w_kv_p[:, _kv_dst(k):_kv_dst(k) + SHARD_KV] = gqkv[k, Q_RANK:, :]
                w_out_f[SHARD_OUT * k:SHARD_OUT * (k + 1), :] = gout[k]
                conv_f[0:3, SHARD_CONV * k:SHARD_CONV * (k + 1)] = gconv[k, 0:3, 0:SHARD_CONV]

    whole = lambda shape: pl.BlockSpec(shape, lambda i, j: (0,) * len(shape))
    return pl.pallas_call(
        body,
        name="in_proj_gather",
        grid=(m // bm, n // bn),
        in_specs=[pl.BlockSpec((bm, k_dim), lambda i, j: (i, 0)), pl.BlockSpec((bn, k_dim), lambda i, j: (j, 0)),
                  whole(w_q.shape), whole(w_kv.shape), whole(w_out.shape), whole(conv_w.shape)],
        out_specs=(pl.BlockSpec((bm, bn), lambda i, j: (i, j)),
                   whole((Q_RANK, Q_COLS)), whole((KV_RANK, KV_COLS)), whole((D_MODEL, D_MODEL)),
                   whole((8, CONV_WIDTH))),
        out_shape=(jax.ShapeDtypeStruct((m, n), BF16),
                   jax.ShapeDtypeStruct((Q_RANK, Q_COLS), BF16),
                   jax.ShapeDtypeStruct((KV_RANK, KV_COLS), BF16),
                   jax.ShapeDtypeStruct((D_MODEL, D_MODEL), BF16),
                   jax.ShapeDtypeStruct((8, CONV_WIDTH), F32)),
        scratch_shapes=[
            pltpu.VMEM((qkv_rows, LANES), BF16),
            pltpu.VMEM((SHARD_OUT, D_MODEL), BF16),
            pltpu.VMEM((8, LANES), F32),
            pltpu.VMEM((N_DEV, qkv_rows, LANES), BF16),
            pltpu.VMEM((N_DEV, SHARD_OUT, D_MODEL), BF16),
            pltpu.VMEM((N_DEV, 8, LANES), F32),
            pltpu.SemaphoreType.DMA((21,)),
            pltpu.SemaphoreType.DMA((21,)),
            pltpu.SemaphoreType.DMA((3,)),
        ],
        compiler_params=_params("arbitrary", "arbitrary"),
    )(u, w_in_p, w_q, w_kv, w_out, conv_w)


def _rope_tables(tp):
    half = D_ROPE // 2
    inv_freq = 1.0 / (ROPE_THETA ** (jnp.arange(half, dtype=F32) / half))
    pos = (jnp.arange(tp) - PAD_FRONT).astype(F32)
    ang = pos[:, None] * inv_freq[None, :]
    cos = jnp.tile(jnp.cos(ang), (1, LANES // half))
    sin = jnp.tile(jnp.sin(ang), (1, LANES // half))
    first = (jnp.arange(LANES) % D_ROPE) < half
    return cos, jnp.where(first, -sin, 0.0), jnp.where(first, 0.0, sin)


def _rope(t, cos, sa, sb):
    return t * cos + pltpu.roll(t, LANES - D_ROPE // 2, 1) * sa + pltpu.roll(t, D_ROPE // 2, 1) * sb


def _rope_t(t, cos, sa, sb):
    return t * cos + pltpu.roll(t * sa, D_ROPE // 2, 1) + pltpu.roll(t * sb, LANES - D_ROPE // 2, 1)


def _qkv_fwd(p, wq, wkv, gq, gkv, tables, nb_seq, tp):
    ht = tp // 2

    def body(pa_ref, wq_ref, wkv_ref, gq_ref, gkv_ref, cos_ref, sa_ref, sb_ref, q_ref, k_ref, v_ref):
        pa = pa_ref[...].astype(F32)
        cq_hat, _ = _rms_stats(pa[:, :Q_RANK])
        ckv_hat, _ = _rms_stats(pa[:, Q_RANK:Q_RANK + KV_RANK])
        q = _dot((cq_hat * gq_ref[...]).astype(BF16), wq_ref[...]) * Q_SCALE
        kv = _dot((ckv_hat * gkv_ref[...]).astype(BF16), wkv_ref[...])
        tabs = (cos_ref[...], sa_ref[...], sb_ref[...])
        lane = lax.broadcasted_iota(jnp.int32, (ht, LANES), 1)
        low = lane < D_ROPE
        mark = lane == D_ROPE
        row = (pl.program_id(0) % 2) * ht + lax.broadcasted_iota(jnp.int32, (ht, LANES), 0)
        k_pe = jnp.where(mark & (row < PAD_FRONT), NEG_INF, _rope(pa[:, Q_RANK + KV_RANK:], *tabs))
        one = jnp.where(mark & (row >= PAD_FRONT), 1.0, 0.0)
        pairs = [_rope(q[:, N_HEADS * D_NOPE + LANES * i:N_HEADS * D_NOPE + LANES * (i + 1)], *tabs) for i in range(2)]
        for h in range(N_HEADS):
            pair = pairs[h // 2]
            if h % 2:
                pair = pltpu.roll(pair, D_ROPE, 1)
            pe = jnp.where(low, pair, one)
            q_ref[0, h] = jnp.concatenate([q[:, D_NOPE * h:D_NOPE * (h + 1)], pe], axis=1).astype(BF16)
            k_ref[0, h] = jnp.concatenate([kv[:, D_NOPE * h:D_NOPE * (h + 1)], k_pe], axis=1).astype(BF16)
            v_ref[0, h] = kv[:, N_HEADS * D_NOPE + D_V * h:N_HEADS * D_NOPE + D_V * (h + 1)].astype(BF16)

    full = lambda a: pl.BlockSpec(a.shape, lambda i: (0,) * a.ndim)
    tab = pl.BlockSpec((ht, LANES), lambda i: (i % 2, 0))
    qk = pl.BlockSpec((1, N_HEADS, ht, 2 * LANES), lambda i: (i // 2, 0, i % 2, 0))
    return pl.pallas_call(
        body,
        name="qkv_fwd",
        grid=(2 * nb_seq,),
        in_specs=[pl.BlockSpec((ht, GRP_A), lambda i: (i, 0)), full(wq), full(wkv), full(gq), full(gkv), tab, tab, tab],
        out_specs=(qk, qk, pl.BlockSpec((1, N_HEADS, ht, D_V), lambda i: (i // 2, 0, i % 2, 0))),
        out_shape=(
            jax.ShapeDtypeStruct((nb_seq, N_HEADS, tp, 2 * LANES), BF16),
            jax.ShapeDtypeStruct((nb_seq, N_HEADS, tp, 2 * LANES), BF16),
            jax.ShapeDtypeStruct((nb_seq, N_HEADS, tp, D_V), BF16),
        ),
        compiler_params=_params("parallel"),
    )(p, wq, wkv, gq, gkv, *tables)


def _attn_fwd(q, k, v, p, g_attn):
    nb_seq, _, tp, _ = q.shape

    def body(q_ref, k_ref, v_ref, z_ref, g_ref, y_ref, o_ref, lse_ref):
        g = g_ref[...]
        for r0 in range(0, tp, KV_TILE):
            nq = min(KV_TILE, tp - r0)
            kend = r0 + nq
            qq = q_ref[0, 0, r0:kend, :]
            sd = _dot(qq, k_ref[0, 0, r0:kend, :], _NT)
            causal = (lax.broadcasted_iota(jnp.int32, (nq, nq), 1) <= lax.broadcasted_iota(jnp.int32, (nq, nq), 0))
            sd = jnp.where(causal, sd, NEG_INF)
            m = jnp.max(sd, axis=-1, keepdims=True)
            if r0:
                so = _dot(qq, k_ref[0, 0, 0:r0, :], _NT)
                m = jnp.maximum(m, jnp.max(so, axis=-1, keepdims=True))
            ed = jnp.exp2(sd - m)
            l = jnp.sum(ed, axis=-1, keepdims=True)
            o = _dot(ed.astype(BF16), v_ref[0, 0, r0:kend, :])
            if r0:
                eo = jnp.exp2(so - m)
                l = l + jnp.sum(eo, axis=-1, keepdims=True)
                o = o + _dot(eo.astype(BF16), v_ref[0, 0, 0:r0, :])
            o = o * (1.0 / l)
            o_ref[0, 0, r0:kend, :] = o
            lse_ref[0, 0, r0:kend, :] = jnp.broadcast_to(m + jnp.log2(l), (nq, LANES))
            ohat, _ = _rms_stats(o)
            z = z_ref[r0:kend, :].astype(F32)
            y_ref[r0:kend, :] = (ohat * g * (z * _sigmoid(z))).astype(BF16)

    qk = pl.BlockSpec((1, 1, tp, 2 * LANES), lambda b, h: (b, h, 0, 0))
    hv = pl.BlockSpec((1, 1, tp, D_V), lambda b, h: (b, h, 0, 0))
    return pl.pallas_call(
        body,
        name="attn_fwd",
        grid=(nb_seq, N_HEADS),
        in_specs=[qk, qk, hv,
                  pl.BlockSpec((tp, LANES), lambda b, h: (b, GRP_A // LANES + h)),
                  pl.BlockSpec((1, LANES), lambda b, h: (0, h))],
        out_specs=(pl.BlockSpec((tp, LANES), lambda b, h: (b, h)), hv, hv),
        out_shape=(
            jax.ShapeDtypeStruct((nb_seq * tp, N_HEADS * D_V), BF16),
            jax.ShapeDtypeStruct((nb_seq, N_HEADS, tp, D_V), F32),
            jax.ShapeDtypeStruct((nb_seq, N_HEADS, tp, LANES), F32),
        ),
        compiler_params=_params("parallel", "parallel"),
    )(q, k, v, p, g_attn)


_CONV_COL0 = (GRP_A + N_HEADS * D_V) // LANES


def _conv_specs(tp, order):
    cols = CONV_WIDTH // LANES
    return [pl.BlockSpec((tp, LANES), functools.partial(
        lambda a, b, off: order(a, b, off), off=_CONV_COL0 + i * cols)) for i in range(4)]


def _conv_fwd(p, conv_w, g_conv, nb_seq, tp):
    def body(b_ref, c_ref, h_ref, z_ref, w_ref, g_ref, y_ref):
        cc = c_ref[...].astype(F32) * h_ref[...].astype(F32)
        row = lax.broadcasted_iota(jnp.int32, (tp, LANES), 0)
        s1 = jnp.where(row >= 1, pltpu.roll(cc, 1, 0), 0.0)
        s2 = jnp.where(row >= 2, pltpu.roll(cc, 2, 0), 0.0)
        yc = b_ref[...].astype(F32) * (w_ref[0:1, :] * s2 + w_ref[1:2, :] * s1 + w_ref[2:3, :] * cc)
        r = lax.rsqrt(_group_mean(yc * yc) + EPS)
        z = z_ref[...].astype(F32)
        y_ref[...] = (yc * r * g_ref[...] * (z * _sigmoid(z))).astype(BF16)

    return pl.pallas_call(
        body,
        name="conv_fwd",
        grid=(nb_seq, CONV_WIDTH // LANES),
        in_specs=_conv_specs(tp, lambda b, t, off: (b, off + t)) + [
            pl.BlockSpec((8, LANES), lambda b, t: (0, t)),
            pl.BlockSpec((1, LANES), lambda b, t: (0, t))],
        out_specs=pl.BlockSpec((tp, LANES), lambda b, t: (b, t)),
        out_shape=jax.ShapeDtypeStruct((nb_seq * tp, CONV_WIDTH), BF16),
        compiler_params=_params("parallel", "parallel"),
    )(p, p, p, p, conv_w, g_conv)


def _token_copy(hbm, b, k, ts, buf, sem, to_hbm=False):
    lo, hi = max(k * ts - LANES, 0), (k + 1) * ts - LANES
    off = lo - (k * ts - LANES)
    src, dst = hbm.at[b, pl.ds(lo, hi - lo)], buf.at[pl.ds(off, hi - lo)]
    if to_hbm:
        src, dst = dst, src
    return pltpu.make_async_copy(src, dst, sem)


def _for_tile(k, nt, fn):
    for kk in range(nt):
        @pl.when(k == kk)
        def _(kk=kk):
            fn(kk)


def _out_proj_loss(ya, yc, w_out, x, target, g_final, nt):
    nb_seq, s, d = x.shape
    r, ka = ya.shape
    ts = (s + LANES) // nt
    steps = nb_seq * nt

    def body(a_ref, c_ref, w_ref, x_hbm, t_hbm, g_ref, dh_ref, dhb_ref, dg_ref, loss_ref,
             xbuf, tbuf, acc_ref, sems):
        i = pl.program_id(0)
        b, k = i // nt, i % nt

        @pl.when(i == 0)
        def _():
            acc_ref[...] = jnp.zeros_like(acc_ref)
            dg_ref[...] = jnp.zeros_like(dg_ref)

        slot = i % 2

        def fetch(seq, kk, sl):
            return [_token_copy(x_hbm, seq, kk, ts, xbuf.at[sl], sems.at[sl, 0]),
                    _token_copy(t_hbm, seq, kk, ts, tbuf.at[sl], sems.at[sl, 1])]

        def start(seq, sl, kk):
            if kk == 0:
                xbuf[sl, 0:LANES, :] = jnp.zeros((LANES, d), F32)
                tbuf[sl, 0:LANES, :] = jnp.zeros((LANES, d), F32)
            for cp in fetch(seq, kk, sl):
                cp.start()

        @pl.when(i == 0)
        def _():
            start(0, 0, 0)

        @pl.when(i + 1 < steps)
        def _():
            _for_tile((i + 1) % nt, nt, functools.partial(start, (i + 1) // nt, 1 - slot))

        mix = _dot(a_ref[...], w_ref[0:ka, :]) + _dot(c_ref[...], w_ref[ka:, :])
        _for_tile(k, nt, lambda kk: [cp.wait() for cp in fetch(b, kk, slot)])

        real = (lax.broadcasted_iota(jnp.int32, (ts, d), 0) >= LANES) | (k > 0)
        g = g_ref[...]
        hhat, rstd = _rms_stats(xbuf[slot] + mix)
        e = jnp.where(real, hhat * g - tbuf[slot], 0.0)
        acc_ref[...] += jnp.sum(e * e, axis=0, keepdims=True)
        dy = e * (1.0 / d)
        dg_ref[...] += jnp.sum(dy * hhat, axis=0, keepdims=True)
        dh = _rms_bwd(g * dy, hhat, rstd)
        dh_ref[...] = dh
        dhb_ref[...] = dh.astype(BF16)

        @pl.when(i == steps - 1)
        def _():
            total = jnp.sum(acc_ref[...], axis=1, keepdims=True)
            loss_ref[...] = jnp.broadcast_to((0.5 / d) * total, loss_ref.shape)

    hbm = pl.BlockSpec(memory_space=pl.ANY)
    row = pl.BlockSpec((ts, d), lambda i: (i, 0))
    vec = pl.BlockSpec((1, d), lambda i: (0, 0))
    return pl.pallas_call(
        body,
        name="out_proj_loss",
        grid=(steps,),
        in_specs=[pl.BlockSpec((ts, ka), lambda i: (i, 0)), pl.BlockSpec((ts, yc.shape[1]), lambda i: (i, 0)),
                  pl.BlockSpec(w_out.shape, lambda i: (0, 0)), hbm, hbm, vec],
        out_specs=(row, row, vec, pl.BlockSpec((1, LANES), lambda i: (0, 0))),
        out_shape=(
            jax.ShapeDtypeStruct((r, d), F32),
            jax.ShapeDtypeStruct((r, d), BF16),
            jax.ShapeDtypeStruct((1, d), F32),
            jax.ShapeDtypeStruct((1, LANES), F32),
        ),
        scratch_shapes=[pltpu.VMEM((2, ts, d), F32), pltpu.VMEM((2, ts, d), F32), pltpu.VMEM((1, d), F32),
                        pltpu.SemaphoreType.DMA((2, 2))],
        compiler_params=_params("arbitrary"),
    )(ya, yc, w_out, x, target, g_final)


def _out_proj_bwd(dhb, w_out, ya, yc, bm):
    r, d = dhb.shape
    ka = ya.shape[1]
    n_mix = w_out.shape[0]
    last = r // bm - 1

    def body(dh_ref, w_ref, a_ref, c_ref, dcat_ref, dw_ref, acc_ref):
        @pl.when(pl.program_id(0) == 0)
        def _():
            acc_ref[...] = jnp.zeros_like(acc_ref)

        dh = dh_ref[...]
        dcat_ref[...] = _dot(dh, w_ref[...], _NT).astype(BF16)
        acc_ref[0:ka, :] += _dot(a_ref[...], dh, _TN)
        acc_ref[ka:, :] += _dot(c_ref[...], dh, _TN)

        @pl.when(pl.program_id(0) == last)
        def _():
            dw_ref[...] = acc_ref[...].astype(BF16)

    return pl.pallas_call(
        body,
        name="out_proj_bwd",
        grid=(r // bm,),
        in_specs=[pl.BlockSpec((bm, d), lambda i: (i, 0)), pl.BlockSpec(w_out.shape, lambda i: (0, 0)),
                  pl.BlockSpec((bm, ka), lambda i: (i, 0)), pl.BlockSpec((bm, yc.shape[1]), lambda i: (i, 0))],
        out_specs=(pl.BlockSpec((bm, n_mix), lambda i: (i, 0)),
                   pl.BlockSpec((n_mix, d), lambda i: (0, 0))),
        out_shape=(jax.ShapeDtypeStruct((r, n_mix), BF16),
                   jax.ShapeDtypeStruct((n_mix, d), BF16)),
        scratch_shapes=[pltpu.VMEM((n_mix, d), F32)],
        compiler_params=_params("arbitrary"),
    )(dhb, w_out, ya, yc)


def _attn_bwd(q, k, v, o, lse, dcat, p, g_attn, send_out):
    nb_seq, _, tp, _ = q.shape
    steps = N_HEADS * nb_seq

    def body(q_ref, k_ref, v_ref, o_ref, lse_ref, dy_ref, z_ref, g_ref, pay_ref,
             dq_ref, dk_ref, dv_ref, dz_ref, dg_ref, r2_ref, dq_acc, r1, sums, *sems):
        t = pl.program_id(0) * nb_seq + pl.program_id(1)

        def plan():
            return _reduce_plan((pay_ref,), (None,), (r1,), (sums,), (r2_ref,), *sems)

        @pl.when(t == 0)
        def _():
            plan()[0]()

        @pl.when(t == 1)
        def _():
            plan()[1]()

        @pl.when(pl.program_id(1) == 0)
        def _():
            dg_ref[...] = jnp.zeros_like(dg_ref)

        g = g_ref[...]
        z = z_ref[...].astype(F32)
        o = o_ref[0, 0]
        dy = dy_ref[...].astype(F32)
        sig = _sigmoid(z)
        ohat, r = _rms_stats(o)
        don = dy * (z * sig)
        dz_ref[...] = (dy * (ohat * g) * (sig * (1.0 + z * (1.0 - sig)))).astype(BF16)
        dg_ref[...] += jnp.sum(don * ohat, axis=0, keepdims=True)
        do = _rms_bwd(g * don, ohat, r)
        dvec = jnp.sum(do * o, axis=-1, keepdims=True)
        dob = do.astype(BF16)
        lse_col = lse_ref[0, 0, :, 0:1]
        dq_acc[...] = jnp.zeros_like(dq_acc)
        for k0 in range(0, tp, KV_TILE):
            nk = min(KV_TILE, tp - k0)
            nq = tp - k0
            qq = q_ref[0, 0, k0:, :]
            kk = k_ref[0, 0, k0:k0 + nk, :]
            causal = (lax.broadcasted_iota(jnp.int32, (nq, nk), 1) <= lax.broadcasted_iota(jnp.int32, (nq, nk), 0))
            pr = jnp.where(causal, jnp.exp2(_dot(qq, kk, _NT) - lse_col[k0:]), 0.0)
            dp = _dot(dob[k0:], v_ref[0, 0, k0:k0 + nk, :], _NT)
            ds = (pr * (dp - dvec[k0:])).astype(BF16)
            dv_ref[0, 0, k0:k0 + nk, :] = _dot(pr.astype(BF16), dob[k0:], _TN).astype(BF16)
            dk_ref[0, 0, k0:k0 + nk, :] = (_dot(ds, qq, _TN) * (ATTN_SCALE / Q_SCALE)).astype(BF16)
            dq_acc[k0:, :] += _dot(ds, kk)
        dq_ref[0, 0] = (dq_acc[...] * ATTN_SCALE).astype(BF16)

        @pl.when(t == steps - 1)
        def _():
            plan()[2]()

    qk = pl.BlockSpec((1, 1, tp, 2 * LANES), lambda h, b: (b, h, 0, 0))
    hv = pl.BlockSpec((1, 1, tp, D_V), lambda h, b: (b, h, 0, 0))
    col = pl.BlockSpec((tp, LANES), lambda h, b: (b, h))
    slot = send_out.shape[1:]
    return pl.pallas_call(
        body,
        name="attn_bwd",
        grid=(N_HEADS, nb_seq),
        in_specs=[qk, qk, hv, hv, hv, col,
                  pl.BlockSpec((tp, LANES), lambda h, b: (b, GRP_A // LANES + h)),
                  pl.BlockSpec((1, LANES), lambda h, b: (0, h)),
                  pl.BlockSpec(send_out.shape, lambda h, b: (0, 0, 0))],
        out_specs=(qk, qk, hv, col, pl.BlockSpec((1, LANES), lambda h, b: (0, h)),
                   pl.BlockSpec(memory_space=pl.ANY)),
        out_shape=(
            jax.ShapeDtypeStruct((nb_seq, N_HEADS, tp, 2 * LANES), BF16),
            jax.ShapeDtypeStruct((nb_seq, N_HEADS, tp, 2 * LANES), BF16),
            jax.ShapeDtypeStruct((nb_seq, N_HEADS, tp, D_V), BF16),
            jax.ShapeDtypeStruct((nb_seq * tp, N_HEADS * D_V), BF16),
            jax.ShapeDtypeStruct((1, N_HEADS * D_V), F32),
            jax.ShapeDtypeStruct((N_CHIPS,) + slot, BF16),
        ),
        scratch_shapes=[pltpu.VMEM((tp, 2 * LANES), F32)] + _reduce_scratch([(slot, BF16)], [False]),
        compiler_params=_params("arbitrary", "arbitrary"),
    )(q, k, v, o, lse, dcat, p, g_attn, send_out)


def _qkv_bwd(p, dq, dk, dv, wq, wkv, gq, gkv, tables):
    nb_seq, _, tp, _ = dq.shape
    ht = tp // 2

    def body(pa_ref, dq_ref, dk_ref, dv_ref, wq_ref, wkv_ref, gq_ref, gkv_ref, cos_ref, sa_ref, sb_ref,
             dpa_ref, dwq_ref, dwkv_ref, dgq_ref, dgkv_ref):
        @pl.when(pl.program_id(0) == 0)
        def _():
            dwq_ref[...] = jnp.zeros_like(dwq_ref)
            dwkv_ref[...] = jnp.zeros_like(dwkv_ref)
            dgq_ref[...] = jnp.zeros_like(dgq_ref)
            dgkv_ref[...] = jnp.zeros_like(dgkv_ref)

        pa = pa_ref[...].astype(F32)
        gq, gkv = gq_ref[...], gkv_ref[...]
        cq_hat, rq = _rms_stats(pa[:, :Q_RANK])
        ckv_hat, rkv = _rms_stats(pa[:, Q_RANK:Q_RANK + KV_RANK])
        tabs = (cos_ref[...], sa_ref[...], sb_ref[...])

        pe = [dq_ref[0, h, :, D_NOPE:].astype(F32) for h in range(N_HEADS)]
        pairs = [_rope_t(pe[2 * i] + pltpu.roll(pe[2 * i + 1], D_ROPE, 1), *tabs).astype(BF16) for i in range(2)]
        dq_flat = jnp.concatenate([dq_ref[0, h, :, :D_NOPE] for h in range(N_HEADS)] + pairs, axis=1)
        dwq_ref[...] += _dot((cq_hat * gq).astype(BF16), dq_flat, _TN)
        dcqn = _dot(dq_flat, wq_ref[...], _NT)
        dgq_ref[...] += jnp.sum(dcqn * cq_hat, axis=0, keepdims=True)
        dcq = _rms_bwd(gq * dcqn, cq_hat, rq)

        dkv_flat = jnp.concatenate([dk_ref[0, h, :, :D_NOPE] for h in range(N_HEADS)]
                                   + [dv_ref[0, h] for h in range(N_HEADS)], axis=1)
        dwkv_ref[...] += _dot((ckv_hat * gkv).astype(BF16), dkv_flat, _TN)
        dckvn = _dot(dkv_flat, wkv_ref[...], _NT)
        dgkv_ref[...] += jnp.sum(dckvn * ckv_hat, axis=0, keepdims=True)
        dckv = _rms_bwd(gkv * dckvn, ckv_hat, rkv)

        dk_pe = dk_ref[0, 0, :, D_NOPE:].astype(F32)
        for h in range(1, N_HEADS):
            dk_pe = dk_pe + dk_ref[0, h, :, D_NOPE:].astype(F32)
        dk_pe = jnp.where(lax.broadcasted_iota(jnp.int32, (ht, LANES), 1) < D_ROPE, dk_pe, 0.0)
        dpa_ref[...] = jnp.concatenate([dcq, dckv, _rope_t(dk_pe, *tabs)], axis=1).astype(BF16)

    full = lambda a: pl.BlockSpec(a.shape, lambda i: (0,) * a.ndim)
    tab = pl.BlockSpec((ht, LANES), lambda i: (i % 2, 0))
    qk = pl.BlockSpec((1, N_HEADS, ht, 2 * LANES), lambda i: (i // 2, 0, i % 2, 0))
    acc = lambda shape: pl.BlockSpec(shape, lambda i: (0, 0))
    return pl.pallas_call(
        body,
        name="qkv_bwd",
        grid=(2 * nb_seq,),
        in_specs=[pl.BlockSpec((ht, GRP_A), lambda i: (i, 0)), qk, qk,
                  pl.BlockSpec((1, N_HEADS, ht, D_V), lambda i: (i // 2, 0, i % 2, 0)),
                  full(wq), full(wkv), full(gq), full(gkv), tab, tab, tab],
        out_specs=(pl.BlockSpec((ht, GRP_A), lambda i: (i, 0)),
                   acc(wq.shape), acc(wkv.shape), acc((1, Q_RANK)), acc((1, KV_RANK))),
        out_shape=(
            jax.ShapeDtypeStruct((nb_seq * tp, GRP_A), BF16),
            jax.ShapeDtypeStruct(wq.shape, F32),
            jax.ShapeDtypeStruct(wkv.shape, F32),
            jax.ShapeDtypeStruct((1, Q_RANK), F32),
            jax.ShapeDtypeStruct((1, KV_RANK), F32),
        ),
        compiler_params=_params("arbitrary"),
    )(p, dq, dk, dv, wq, wkv, gq, gkv, *tables)


def _conv_bwd(p, dcat, conv_w, g_conv, nb_seq, tp):
    cols = CONV_WIDTH // LANES

    def body(b_ref, c_ref, h_ref, z_ref, dy_ref, w_ref, g_ref,
             db_ref, dc_ref, dh_ref, dz_ref, dw_ref, dg_ref):
        @pl.when(pl.program_id(1) == 0)
        def _():
            dw_ref[...] = jnp.zeros_like(dw_ref)
            dg_ref[...] = jnp.zeros_like(dg_ref)

        cb, c, h = b_ref[...].astype(F32), c_ref[...].astype(F32), h_ref[...].astype(F32)
        z, dy = z_ref[...].astype(F32), dy_ref[...].astype(F32)
        g = g_ref[...]
        w0, w1, w2 = w_ref[0:1, :], w_ref[1:2, :], w_ref[2:3, :]
        cc = c * h
        row = lax.broadcasted_iota(jnp.int32, (tp, LANES), 0)
        s1 = jnp.where(row >= 1, pltpu.roll(cc, 1, 0), 0.0)
        s2 = jnp.where(row >= 2, pltpu.roll(cc, 2, 0), 0.0)
        dwc = w0 * s2 + w1 * s1 + w2 * cc
        yc = cb * dwc
        r = lax.rsqrt(_group_mean(yc * yc) + EPS)
        ychat = yc * r
        sig = _sigmoid(z)
        dz_ref[...] = (dy * (ychat * g) * (sig * (1.0 + z * (1.0 - sig)))).astype(BF16)
        dyn = dy * (z * sig)
        dg_ref[...] += jnp.sum(dyn * ychat, axis=0, keepdims=True)
        gd = g * dyn
        dyc = r * (gd - ychat * _group_mean(gd * ychat))
        db_ref[...] = (dyc * dwc).astype(BF16)
        ddw = dyc * cb
        dw_ref[0:1, :] += jnp.sum(ddw * s2, axis=0, keepdims=True)
        dw_ref[1:2, :] += jnp.sum(ddw * s1, axis=0, keepdims=True)
        dw_ref[2:3, :] += jnp.sum(ddw * cc, axis=0, keepdims=True)
        u1 = jnp.where(row <= tp - 2, pltpu.roll(ddw, tp - 1, 0), 0.0)
        u2 = jnp.where(row <= tp - 3, pltpu.roll(ddw, tp - 2, 0), 0.0)
        dcc = w2 * ddw + w1 * u1 + w0 * u2
        dc_ref[...] = (dcc * h).astype(BF16)
        dh_ref[...] = (dcc * c).astype(BF16)

    col = pl.BlockSpec((tp, LANES), lambda t, b: (b, t))
    out = jax.ShapeDtypeStruct((nb_seq * tp, CONV_WIDTH), BF16)
    return pl.pallas_call(
        body,
        name="conv_bwd",
        grid=(cols, nb_seq),
        in_specs=_conv_specs(tp, lambda t, b, off: (b, off + t)) + [
            pl.BlockSpec((tp, LANES), lambda t, b: (b, N_HEADS * D_V // LANES + t)),
            pl.BlockSpec((8, LANES), lambda t, b: (0, t)),
            pl.BlockSpec((1, LANES), lambda t, b: (0, t))],
        out_specs=(col, col, col, col,
                   pl.BlockSpec((8, LANES), lambda t, b: (0, t)), pl.BlockSpec((1, LANES), lambda t, b: (0, t))),
        out_shape=(out, out, out, out,
                   jax.ShapeDtypeStruct((8, CONV_WIDTH), F32), jax.ShapeDtypeStruct((1, CONV_WIDTH), F32)),
        compiler_params=_params("arbitrary", "arbitrary"),
    )(p, p, p, p, dcat, conv_w, g_conv)


def _input_bwd(dps, w_in, x, meta, dh, norm_g, nt, send_in):
    nb_seq, s, d = x.shape
    r, kb = dps[0].shape
    ts = (s + LANES) // nt
    steps = nb_seq * nt
    n_dp = len(dps)
    in_slot = send_in.shape[1:]

    def body(*refs):
        dp_refs, w_ref, x_hbm, meta_ref, dh_ref, g_ref, pay_ref = refs[:n_dp], *refs[n_dp:n_dp + 6]
        o = n_dp + 6
        gx_hbm, dmeta_ref, dg_ref, r2_in = refs[o:o + 4]
        xbuf, gxbuf, tok_sems, own_in, r1_in, sum_in = refs[o + 4:o + 10]
        sems = refs[o + 10:]
        i = pl.program_id(0)
        b, k = i // nt, i % nt

        def plan():
            return _reduce_plan((pay_ref,), (own_in,), (r1_in,), (sum_in,), (r2_in,), *sems)

        @pl.when(i == 0)
        def _():
            dmeta_ref[...] = jnp.zeros_like(dmeta_ref)
            dg_ref[...] = jnp.zeros_like(dg_ref)
            plan()[0]()

        @pl.when(i == 1)
        def _():
            plan()[1]()

        def start(kk):
            if kk == 0:
                xbuf[0:PAD_FRONT, :] = jnp.zeros((PAD_FRONT, d), F32)
                xbuf[PAD_FRONT:LANES, :] = meta_ref[...]
            _token_copy(x_hbm, b, kk, ts, xbuf, tok_sems.at[0]).start()

        _for_tile(k, nt, start)
        du = _dot(dp_refs[0][...], w_ref[0:kb, :])
        for j in range(1, n_dp):
            du = du + _dot(dp_refs[j][...], w_ref[kb * j:kb * (j + 1), :])
        _for_tile(k, nt, lambda kk: _token_copy(x_hbm, b, kk, ts, xbuf, tok_sems.at[0]).wait())

        g = g_ref[...]
        hhat, rstd = _rms_stats(xbuf[...])
        dg_ref[...] += jnp.sum(du * hhat, axis=0, keepdims=True)
        res = _rms_bwd(g * du, hhat, rstd) + dh_ref[...]

        @pl.when(i > 0)
        def _():
            _for_tile(k, nt, lambda kk: _token_copy(gx_hbm, b, (kk - 1) % nt, ts, gxbuf, tok_sems.at[1], True).wait())

        gxbuf[...] = res

        @pl.when(k == 0)
        def _():
            dmeta_ref[...] += gxbuf[PAD_FRONT:LANES, :]

        _for_tile(k, nt, lambda kk: _token_copy(gx_hbm, b, kk, ts, gxbuf, tok_sems.at[1], True).start())

        @pl.when(i == steps - 1)
        def _():
            _token_copy(gx_hbm, b, nt - 1, ts, gxbuf, tok_sems.at[1], True).wait()
            plan()[2]()

    whole = lambda a: pl.BlockSpec(a.shape, lambda i: (0,) * a.ndim)
    hbm = pl.BlockSpec(memory_space=pl.ANY)
    return pl.pallas_call(
        body,
        name="input_bwd",
        grid=(steps,),
        in_specs=[pl.BlockSpec((ts, kb), lambda i: (i, 0)) for _ in dps]
        + [whole(w_in), hbm, whole(meta), pl.BlockSpec((ts, d), lambda i: (i, 0)), whole(norm_g), hbm],
        out_specs=(hbm, pl.BlockSpec((N_META, d), lambda i: (0, 0)), pl.BlockSpec((1, d), lambda i: (0, 0)), hbm),
        out_shape=(jax.ShapeDtypeStruct((nb_seq, s, d), F32),
                   jax.ShapeDtypeStruct((N_META, d), F32),
                   jax.ShapeDtypeStruct((1, d), F32),
                   jax.ShapeDtypeStruct((N_CHIPS,) + in_slot, BF16)),
        scratch_shapes=[pltpu.VMEM((ts, d), F32), pltpu.VMEM((ts, d), F32), pltpu.SemaphoreType.DMA((2,))]
        + _reduce_scratch([(in_slot, BF16)], [True]),
        compiler_params=_params("arbitrary"),
    )(*dps, w_in, x, meta, dh, norm_g, send_in)


def _in_proj_bwd_w(u, dps, bm, small_grads):
    r, d = u.shape
    kb = dps[0].shape[1]
    steps = r // bm
    n_dp, n_small = len(dps), len(small_grads)
    small_slot = (SMALL_ROWS, LANES)

    def body(*refs):
        u_ref, dp_refs = refs[0], refs[1:1 + n_dp]
        small_refs = refs[1 + n_dp:1 + n_dp + n_small]
        o = 1 + n_dp + n_small
        o_ref, r2_small = refs[o:o + 2]
        acc_ref, ssmall, r1_small, sum_small = refs[o + 2:o + 6]
        sems = refs[o + 6:]
        i = pl.program_id(0)

        def plan():
            return _reduce_plan((ssmall,), (None,), (r1_small,), (sum_small,), (r2_small,), *sems)

        @pl.when(i == 0)
        def _():
            acc_ref[...] = jnp.zeros_like(acc_ref)
            _pack_small(ssmall, *small_refs)
            plan()[0]()

        @pl.when(i == 1)
        def _():
            plan()[1]()

        uu = u_ref[...]
        for j in range(n_dp):
            acc_ref[kb * j:kb * (j + 1), :] += _dot(dp_refs[j][...], uu, _TN)

        @pl.when(i == steps - 1)
        def _():
            for k in range(N_DEV):
                for s, e, c0 in _in_pieces(k):
                    o_ref[k, s:e, :] = acc_ref[c0:c0 + e - s, :].astype(BF16)
                o_ref[k, SHARD_IN:, :] = jnp.zeros((SHARD_IN_PAD - SHARD_IN, d), BF16)
            plan()[2]()

    whole = lambda a: pl.BlockSpec(a.shape, lambda i: (0,) * a.ndim)
    return pl.pallas_call(
        body,
        name="in_proj_bwd_w",
        grid=(steps,),
        in_specs=[pl.BlockSpec((bm, d), lambda i: (i, 0))]
        + [pl.BlockSpec((bm, kb), lambda i: (i, 0)) for _ in dps] + [whole(a) for a in small_grads],
        out_specs=(pl.BlockSpec((N_DEV, SHARD_IN_PAD, d), lambda i: (0, 0, 0)), pl.BlockSpec(memory_space=pl.ANY)),
        out_shape=(jax.ShapeDtypeStruct((N_DEV, SHARD_IN_PAD, d), BF16),
                   jax.ShapeDtypeStruct((N_CHIPS,) + small_slot, F32)),
        scratch_shapes=[pltpu.VMEM((kb * n_dp, d), F32), pltpu.VMEM((N_DEV,) + small_slot, F32)]
        + _reduce_scratch([(small_slot, F32)], [False]),
        compiler_params=_params("arbitrary"),
    )(u, *dps, *small_grads)


def _local_step(x, loss_target, u, p, meta_f, norm_g, w_in_p, q_norm_g, w_q_p, kv_norm_g, w_kv_p, conv_w_f,
                attn_out_g, conv_out_g, w_out_f, g_final):
    nb_seq, s, d = x.shape
    tp = s + LANES
    ht = tp // 2
    tables = _rope_tables(tp)

    q, k, v = _qkv_fwd(p, w_q_p, w_kv_p, q_norm_g, kv_norm_g, tables, nb_seq, tp)
    ya, o, lse = _attn_fwd(q, k, v, p, attn_out_g)
    yc = _conv_fwd(p, conv_w_f, conv_out_g, nb_seq, tp)
    dh, dhb, d_final_g, loss_part = _out_proj_loss(ya, yc, w_out_f, x, loss_target, g_final, TOKEN_TILES)

    dcat, d_w_out = _out_proj_bwd(dhb, w_out_f, ya, yc, ht)
    send_out = d_w_out.reshape(N_DEV, SHARD_OUT, d)
    dq, dk, dv, dz_attn, d_attn_g, r_out = _attn_bwd(q, k, v, o, lse, dcat, p, attn_out_g, send_out)
    dpa, d_wq_p, d_wkv_p, d_gq, d_gkv = _qkv_bwd(p, dq, dk, dv, w_q_p, w_kv_p, q_norm_g, kv_norm_g, tables)
    d_b, d_c, d_h, dz_conv, d_conv_w, d_conv_g = _conv_bwd(p, dcat, conv_w_f, conv_out_g, nb_seq, tp)
    dps = (dpa, dz_attn, d_b, d_c, d_h, dz_conv)
    small = (d_wq_p, d_wkv_p, d_conv_w, d_final_g, d_gq, d_gkv, d_attn_g, d_conv_g, loss_part)
    send_in, r_small = _in_proj_bwd_w(u, dps, ht // 2, small)
    grad_x, d_meta, d_norm_g, r_in = _input_bwd(dps, w_in_p, x, meta_f, dh, norm_g, TOKEN_TILES, send_in)
    return grad_x, r_in, r_out, r_small, d_meta, d_norm_g


def kernel(x, meta_tokens, norm_g, w_in, q_norm_g, w_q_up, kv_norm_g, w_kv_up, conv_w, attn_out_g, conv_out_g, w_out, final_norm_g, loss_target, m_meta_tokens, m_norm_g, m_w_in, m_q_norm_g, m_w_q_up, m_kv_norm_g, m_w_kv_up, m_conv_w, m_attn_out_g, m_conv_out_g, m_w_out, m_final_norm_g, v_meta_tokens, v_norm_g, v_w_in, v_q_norm_g, v_w_q_up, v_kv_norm_g, v_w_kv_up, v_conv_w, v_attn_out_g, v_conv_out_g, v_w_out, v_final_norm_g):
    d = x.shape[-1]
    ht = (x.shape[1] + LANES) // 2
    u, w_in_p, meta_f = _prep_gather(x, meta_tokens, norm_g, w_in[0].T)
    p, w_q_p, w_kv_p, w_out_f, conv_w_f = _in_proj_gather(
        u, w_in_p, w_q_up[0], w_kv_up[0], w_out[0], conv_w[0], ht, GRP_A)
    g_final = final_norm_g.reshape(1, d)
    grad_x, r_in, r_out, r_small, d_meta, d_norm_g = _local_step(
        x, loss_target, u, p, meta_f, norm_g, w_in_p, q_norm_g, w_q_p, kv_norm_g, w_kv_p, conv_w_f,
        attn_out_g, conv_out_g, w_out_f, g_final)

    flat = lambda a: a.reshape(a.shape[-2:]) if a.ndim == 3 else a.reshape(1, -1) if a.ndim == 1 else a
    transposed = ("w_in",)
    to_kernel = lambda n, a: flat(a).T if n in transposed else flat(a)
    from_kernel = lambda n, a, shape: (a.T if n in transposed else a).reshape(shape)
    params = {
        "meta_tokens": (meta_tokens, m_meta_tokens, v_meta_tokens),
        "norm_g": (norm_g, m_norm_g, v_norm_g),
        "w_in": (w_in, m_w_in, v_w_in),
        "q_norm_g": (q_norm_g, m_q_norm_g, v_q_norm_g),
        "w_q_up": (w_q_up, m_w_q_up, v_w_q_up),
        "kv_norm_g": (kv_norm_g, m_kv_norm_g, v_kv_norm_g),
        "w_kv_up": (w_kv_up, m_w_kv_up, v_w_kv_up),
        "conv_w": (conv_w, m_conv_w, v_conv_w),
        "attn_out_g": (attn_out_g, m_attn_out_g, v_attn_out_g),
        "conv_out_g": (conv_out_g, m_conv_out_g, v_conv_out_g),
        "w_out": (w_out, m_w_out, v_w_out),
        "final_norm_g": (final_norm_g, m_final_norm_g, v_final_norm_g),
    }
    grads, loss = _reduce_tail(r_in, r_out, r_small, d_meta, d_norm_g)
    updated = _adamw(grads, {n: tuple(to_kernel(n, a) for a in t) for n, t in params.items()})
    outs = [[from_kernel(n, updated[n][i], params[n][0].shape) for n, _ in PARAM_SHAPES] for i in range(4)]
    return (loss[0, 0], grad_x, *outs[0], *outs[1], *outs[2], *outs[3])
```

```python
import functools

import jax
import jax.numpy as jnp
from jax import lax
from jax.experimental import pallas as pl
from jax.experimental.pallas import tpu as pltpu

F32 = jnp.float32
BF16 = jnp.bfloat16

N_META = 16
D_MODEL = 1024
N_HEADS = 4
D_NOPE = 128
D_ROPE = 64
D_V = 128
Q_RANK = 256
KV_RANK = 128
CONV_WIDTH = 512
CONV_GROUP = 64
ROPE_THETA = 10000.0
ATTN_SCALE = (D_NOPE + D_ROPE) ** -0.5
Q_SCALE = ATTN_SCALE * 1.4426950408889634
EPS = 1e-6
NEG_INF = -1e30

ADAM_LR = 0.001
ADAM_B1 = 0.9
ADAM_B2 = 0.999
ADAM_EPS = 1e-08
ADAM_WD = 0.01
ADAM_STEP = 10

LANES = 128
PAD_FRONT = LANES - N_META
K_TILE = 256
Q_TILE = 512
N_DEV = 8
VMEM_LIMIT = 56 * 1024 * 1024

IN_PAD = 3072
GRP_A = 512
N_A = Q_RANK + KV_RANK + D_ROPE
IN_PROJ = 3008
SHARD_IN = IN_PROJ // N_DEV
SHARD_IN_PAD = 384
SHARD_Q = 96
SHARD_KV = 128
SHARD_OUT = 128
SHARD_CONV = 64
SHARD_META = 128
Q_COLS = N_HEADS * (D_NOPE + D_ROPE)
KV_COLS = N_HEADS * (D_NOPE + D_V)

ROW_Q, ROW_KV, ROW_META, ROW_CONV = 0, 256, 384, 400
ROW_REPL = 408
ROW_NORM, ROW_FINAL, ROW_GQ, ROW_GKV, ROW_ATTN, ROW_CONVG, ROW_LOSS = 408, 416, 424, 426, 427, 431, 435
SMALL_ROWS = 440

PARAM_SHAPES = (
    ("meta_tokens", (N_META, SHARD_META)), ("norm_g", (1, D_MODEL)), ("w_in", (SHARD_IN, D_MODEL)),
    ("q_norm_g", (1, Q_RANK)), ("w_q_up", (Q_RANK, SHARD_Q)), ("kv_norm_g", (1, KV_RANK)),
    ("w_kv_up", (KV_RANK, SHARD_KV)), ("conv_w", (3, SHARD_CONV)), ("attn_out_g", (1, CONV_WIDTH)),
    ("conv_out_g", (1, CONV_WIDTH)), ("w_out", (SHARD_OUT, D_MODEL)), ("final_norm_g", (1, D_MODEL)),
)


def _in_pieces(k):
    lo, hi = SHARD_IN * k, SHARD_IN * (k + 1)
    out = []
    if lo < N_A:
        out.append((0, min(hi, N_A) - lo, lo))
    if hi > N_A:
        s = max(lo, N_A)
        out.append((s - lo, hi - lo, s + GRP_A - N_A))
    return out


def _q_pieces(k):
    lo, hi = SHARD_Q * k, SHARD_Q * (k + 1)
    out = []
    for h in range(N_HEADS):
        base = (D_NOPE + D_ROPE) * h
        s, e = max(lo, base), min(hi, base + D_NOPE)
        if s < e:
            out.append((s - lo, e - lo, D_NOPE * h + s - base))
        s, e = max(lo, base + D_NOPE), min(hi, base + D_NOPE + D_ROPE)
        if s < e:
            out.append((s - lo, e - lo, N_HEADS * D_NOPE + D_ROPE * h + s - base - D_NOPE))
    return out


def _kv_dst(k):
    return D_NOPE * (k // 2) + (N_HEADS * D_NOPE if k % 2 else 0)


def _params(*sem):
    return pltpu.CompilerParams(dimension_semantics=sem, vmem_limit_bytes=VMEM_LIMIT)


def _rms_stats(x):
    r = lax.rsqrt(jnp.mean(x * x, axis=-1, keepdims=True) + EPS)
    return x * r, r


def _rms_bwd(gdy, xhat, r):
    return r * (gdy - xhat * jnp.mean(gdy * xhat, axis=-1, keepdims=True))


def _sigmoid(z):
    return 1.0 / (1.0 + jnp.exp(-z))


def _group_mean(x):
    i0 = lax.broadcasted_iota(jnp.int32, (LANES, LANES), 0) // CONV_GROUP
    i1 = lax.broadcasted_iota(jnp.int32, (LANES, LANES), 1) // CONV_GROUP
    m = jnp.where(i0 == i1, 1.0 / CONV_GROUP, 0.0).astype(BF16)
    hi = x.astype(BF16)
    lo = (x - hi.astype(F32)).astype(BF16)
    return jnp.dot(hi, m, preferred_element_type=F32) + jnp.dot(lo, m, preferred_element_type=F32)


_NT = (((1,), (1,)), ((), ()))
_TN = (((0,), (0,)), ((), ()))


def _dot(a, b, dims=None):
    if dims is None:
        return jnp.dot(a, b, preferred_element_type=F32)
    return lax.dot_general(a, b, dims, preferred_element_type=F32)


def _device_position():
    x, y, c = lax.axis_index("x"), lax.axis_index("y"), lax.axis_index("c")
    return x, y, c, 4 * x + 2 * y + c


def _gather_plan(srcs, slots, send_sems, recv_sems, local_sems):
    x, y, c, _ = _device_position()
    me, sibling = (x, y, c), (x, y, 1 - c)
    flip = lambda v, on: v + on - 2 * v * on
    near = (flip(x, 1 - c), flip(y, c))
    far = (flip(x, c), flip(y, 1 - c))
    diag = (1 - x, 1 - y)
    n = len(srcs)

    def slot(a, px, py, pc):
        return slots[a].at[4 * px + 2 * py + pc]

    def copy(a, k, block, to, own=False):
        return pltpu.make_async_remote_copy(
            src_ref=srcs[a] if own else slot(a, *block),
            dst_ref=slot(a, *block),
            send_sem=send_sems.at[7 * a + k],
            recv_sem=recv_sems.at[7 * a + k],
            device_id=to,
            device_id_type=pl.DeviceIdType.MESH,
        )

    def local(a):
        return pltpu.make_async_copy(srcs[a], slot(a, *me), local_sems.at[a])

    sent = [(me, sibling), (me, (*near, c)), (me, (*far, c)), ((*near, c), (*far, c)),
            ((*near, c), sibling), ((*far, c), sibling), ((*diag, c), sibling)]
    landed = [sibling, (*near, c), (*far, c), (*diag, c), (*far, 1 - c), (*near, 1 - c), (*diag, 1 - c)]

    def send(a, k):
        return copy(a, k, *sent[k], own=k < 3)

    def arrival(a, k):
        return copy(a, k, landed[k], me)

    def start():
        for a in range(n):
            local(a).start()
            for k in range(3):
                send(a, k).start()

    def relay():
        for a in range(n):
            arrival(a, 1).wait_recv()
            send(a, 3).start()
            send(a, 4).start()

    def forward():
        for k in (2, 3):
            for a in range(n):
                arrival(a, k).wait_recv()
                send(a, k + 3).start()

    def finish():
        for a in range(n):
            for k in (0, 4, 5, 6):
                arrival(a, k).wait_recv()
        for a in range(n):
            for k in range(7):
                send(a, k).wait_send()
            local(a).wait()

    return start, relay, forward, finish


def _adam_update(g, w, m, v):
    m_new = ADAM_B1 * m + (1.0 - ADAM_B1) * g
    v_new = ADAM_B2 * v + (1.0 - ADAM_B2) * (g * g)
    m_hat = m_new / (1.0 - ADAM_B1 ** ADAM_STEP)
    v_hat = v_new / (1.0 - ADAM_B2 ** ADAM_STEP)
    return -ADAM_LR * (m_hat / (jnp.sqrt(v_hat) + ADAM_EPS) + ADAM_WD * w), m_new, v_new


def _adamw(grads, params):
    names = [n for n, _ in PARAM_SHAPES]
    n_p = len(names)

    def body(*refs):
        for i in range(n_p):
            g = refs[i][...]
            w, m, v = (refs[n_p + 3 * i + j][...] for j in range(3))
            delta, m_new, v_new = _adam_update(g, w, m, v)
            for j, val in enumerate((g, delta, m_new, v_new)):
                refs[4 * n_p + 4 * i + j][...] = val

    vm = pl.BlockSpec(memory_space=pltpu.VMEM)
    out_shape = []
    for _, shape in PARAM_SHAPES:
        out_shape += [jax.ShapeDtypeStruct(shape, F32)] * 4
    outs = pl.pallas_call(
        body,
        name="adamw",
        out_shape=tuple(out_shape),
        in_specs=[vm] * (4 * n_p),
        out_specs=(vm,) * (4 * n_p),
        compiler_params=pltpu.CompilerParams(vmem_limit_bytes=VMEM_LIMIT),
    )(*[grads[n] for n in names], *[a for n in names for a in params[n]])
    return {n: outs[4 * i:4 * i + 4] for i, n in enumerate(names)}


N_CHIPS = 4


def _reduce_plan(pays, owns, r1s, sums, r2s, send1, recv1, send2, recv2, local_sems):
    x, y, c, _ = _device_position()
    sibling = (x, y, 1 - c)
    chips = [((1 - x if rj & 2 else x), (1 - y if rj & 1 else y)) for rj in range(N_CHIPS)]
    n = len(pays)

    def slot_of(rj, core):
        return 4 * chips[rj][0] + 2 * chips[rj][1] + core

    def to_sibling(a, rj):
        return pltpu.make_async_remote_copy(
            src_ref=pays[a].at[slot_of(rj, 1 - c)], dst_ref=r1s[a].at[rj],
            send_sem=send1.at[N_CHIPS * a + rj], recv_sem=recv1.at[N_CHIPS * a + rj],
            device_id=sibling, device_id_type=pl.DeviceIdType.MESH)

    def load_own(a, rj):
        return pltpu.make_async_copy(pays[a].at[slot_of(rj, c)], owns[a].at[rj], local_sems.at[2 * N_CHIPS * a + rj])

    def to_chip(a, rj):
        return pltpu.make_async_remote_copy(
            src_ref=sums[a].at[rj], dst_ref=r2s[a].at[rj],
            send_sem=send2.at[N_CHIPS * a + rj], recv_sem=recv2.at[N_CHIPS * a + rj],
            device_id=(*chips[rj], c), device_id_type=pl.DeviceIdType.MESH)

    def keep(a):
        return pltpu.make_async_copy(sums[a].at[0], r2s[a].at[0], local_sems.at[2 * N_CHIPS * a + N_CHIPS])

    def start():
        for a in range(n):
            for rj in range(N_CHIPS):
                to_sibling(a, rj).start()
                if owns[a] is not None:
                    load_own(a, rj).start()

    def combine():
        for a in range(n):
            for rj in range(N_CHIPS):
                to_sibling(a, rj).wait_recv()
                if owns[a] is not None:
                    load_own(a, rj).wait()
                    mine = owns[a][rj]
                else:
                    mine = pays[a][slot_of(rj, c)]
                sums[a][rj] = (mine.astype(F32) + r1s[a][rj].astype(F32)).astype(sums[a].dtype)
            keep(a).start()
            for rj in range(1, N_CHIPS):
                to_chip(a, rj).start()

    def finish():
        for a in range(n):
            for rj in range(1, N_CHIPS):
                to_chip(a, rj).wait_recv()
            for rj in range(N_CHIPS):
                to_sibling(a, rj).wait_send()
            for rj in range(1, N_CHIPS):
                to_chip(a, rj).wait_send()
            keep(a).wait()

    return start, combine, finish


def _reduce_scratch(shapes_dtypes, own_flags):
    out = []
    for (shape, dtype), own in zip(shapes_dtypes, own_flags):
        if own:
            out.append(pltpu.VMEM((N_CHIPS,) + shape, dtype))
        out += [pltpu.VMEM((N_CHIPS,) + shape, dtype), pltpu.VMEM((N_CHIPS,) + shape, dtype)]
    n = len(shapes_dtypes)
    out += [pltpu.SemaphoreType.DMA((N_CHIPS * n,))] * 4 + [pltpu.SemaphoreType.DMA((2 * N_CHIPS * n,))]
    return out


def _pack_small(ssmall, dwq, dwkv, dconv, dfinal, dgq, dgkv, dattn, dconvg, loss_part):
    ssmall[...] = jnp.zeros_like(ssmall)
    rep = ssmall.at[0]
    for i in range(D_MODEL // LANES):
        rep[ROW_FINAL + i:ROW_FINAL + i + 1, :] = dfinal[:, LANES * i:LANES * (i + 1)]
    for i in range(Q_RANK // LANES):
        rep[ROW_GQ + i:ROW_GQ + i + 1, :] = dgq[:, LANES * i:LANES * (i + 1)]
    rep[ROW_GKV:ROW_GKV + 1, :] = dgkv[...]
    for i in range(CONV_WIDTH // LANES):
        rep[ROW_ATTN + i:ROW_ATTN + i + 1, :] = dattn[:, LANES * i:LANES * (i + 1)]
        rep[ROW_CONVG + i:ROW_CONVG + i + 1, :] = dconvg[:, LANES * i:LANES * (i + 1)]
    rep[ROW_LOSS:ROW_LOSS + 1, :] = loss_part[...]
    for k in range(N_DEV):
        if k:
            ssmall[k, ROW_REPL:, :] = ssmall[0, ROW_REPL:, :]
        for s, e, d in _q_pieces(k):
            ssmall[k, ROW_Q:ROW_Q + Q_RANK, s:e] = dwq[:, d:d + e - s]
        ssmall[k, ROW_KV:ROW_KV + KV_RANK, :] = dwkv[:, _kv_dst(k):_kv_dst(k) + SHARD_KV]
        ssmall[k, ROW_CONV:ROW_CONV + 3, 0:SHARD_CONV] = dconv[0:3, SHARD_CONV * k:SHARD_CONV * (k + 1)]


TOKEN_TILES = 4
TAIL_ROWS = N_META + D_MODEL // LANES


def _reduce_tail(r_in, r_out, r_small, d_meta, d_norm):
    n_p = len(PARAM_SHAPES)
    names = [n for n, _ in PARAM_SHAPES]

    def body(*refs):
        rin, rout, rsmall, dmeta, dnorm = refs[:5]
        g_out = {n: refs[5 + i] for i, n in enumerate(names)}
        loss_out = refs[5 + n_p]
        stail, rtail, gsum, gtail, send_sems, recv_sems = refs[6 + n_p:]
        x, y, c, me = _device_position()
        my_chip = 2 * x + y

        for k in range(N_DEV):
            stail[k, 0:N_META, :] = dmeta[:, SHARD_META * k:SHARD_META * (k + 1)]
            for i in range(D_MODEL // LANES):
                stail[k, N_META + i:N_META + i + 1, :] = dnorm[:, LANES * i:LANES * (i + 1)]
        copies = []
        for r in range(1, N_DEV):
            peer = (1 - x if r & 4 else x, 1 - y if r & 2 else y, 1 - c if r & 1 else c)
            copies.append(pltpu.make_async_remote_copy(
                src_ref=stail.at[4 * peer[0] + 2 * peer[1] + peer[2]],
                dst_ref=rtail.at[r],
                send_sem=send_sems.at[r - 1],
                recv_sem=recv_sems.at[r - 1],
                device_id=peer,
                device_id_type=pl.DeviceIdType.MESH,
            ))
        for cp in copies:
            cp.start()
        rtail[0] = stail[me]

        g = rin[my_chip].astype(F32)
        for ch in range(1, N_CHIPS):
            g = g + rin[ch ^ my_chip].astype(F32)
        g_out["w_in"][...] = g[:SHARD_IN, :]

        g = rout[my_chip].astype(F32)
        gs = rsmall[my_chip]
        for ch in range(1, N_CHIPS):
            g = g + rout[ch ^ my_chip].astype(F32)
            gs = gs + rsmall[ch ^ my_chip]
        g_out["w_out"][...] = g
        gsum[...] = gs
        g_out["w_q_up"][...] = gsum[ROW_Q:ROW_Q + Q_RANK, 0:SHARD_Q]
        g_out["w_kv_up"][...] = gsum[ROW_KV:ROW_KV + KV_RANK, :]
        g_out["conv_w"][...] = gsum[ROW_CONV:ROW_CONV + 3, 0:SHARD_CONV]
        for name, row, width in (("final_norm_g", ROW_FINAL, D_MODEL), ("q_norm_g", ROW_GQ, Q_RANK),
                                 ("kv_norm_g", ROW_GKV, KV_RANK), ("attn_out_g", ROW_ATTN, CONV_WIDTH),
                                 ("conv_out_g", ROW_CONVG, CONV_WIDTH)):
            for i in range(width // LANES):
                g_out[name][:, LANES * i:LANES * (i + 1)] = gsum[row + i:row + i + 1, :]
        loss_out[...] = gsum[ROW_LOSS:ROW_LOSS + 1, :]

        for cp in copies:
            cp.wait_recv()
        gt = rtail[me]
        for d in range(1, N_DEV):
            gt = gt + rtail[d ^ me]
        gtail[...] = gt
        g_out["meta_tokens"][...] = gtail[0:N_META, :]
        for i in range(D_MODEL // LANES):
            g_out["norm_g"][:, LANES * i:LANES * (i + 1)] = gtail[N_META + i:N_META + i + 1, :]
        for cp in copies:
            cp.wait_send()

    vm = pl.BlockSpec(memory_space=pltpu.VMEM)
    out_shape = [jax.ShapeDtypeStruct(shape, F32) for _, shape in PARAM_SHAPES]
    out_shape.append(jax.ShapeDtypeStruct((1, LANES), F32))
    outs = pl.pallas_call(
        body,
        name="reduce_tail",
        out_shape=tuple(out_shape),
        in_specs=[vm] * 5,
        out_specs=(vm,) * len(out_shape),
        scratch_shapes=[
            pltpu.VMEM((N_DEV, TAIL_ROWS, LANES), F32),
            pltpu.VMEM((N_DEV, TAIL_ROWS, LANES), F32),
            pltpu.VMEM((SMALL_ROWS, LANES), F32),
            pltpu.VMEM((TAIL_ROWS, LANES), F32),
            pltpu.SemaphoreType.DMA((N_DEV - 1,)),
            pltpu.SemaphoreType.DMA((N_DEV - 1,)),
        ],
        compiler_params=pltpu.CompilerParams(vmem_limit_bytes=VMEM_LIMIT),
    )(r_in, r_out, r_small, d_meta, d_norm)
    return {n: outs[i] for i, n in enumerate(names)}, outs[-1]


def _prep_gather(x, meta, norm_g, w_in_t):
    nb_seq, s, d = x.shape
    nb = s // LANES + 1
    relay_step = nb_seq * (nb - 1) // 2
    forward_step = nb_seq * (nb - 1) - 1
    finish_step = nb_seq * (nb - 1)

    def body(x_ref, meta_ref, g_ref, win_ref, u_ref, w_in_p, meta_f,
             sbig, ssmall, gbig, gsmall, send_sems, recv_sems, local_sems):
        jj, b = pl.program_id(0), pl.program_id(1)
        t = jj * nb_seq + b

        def plan():
            return _gather_plan((sbig, ssmall), (gbig, gsmall), send_sems, recv_sems, local_sems)

        @pl.when(t == 0)
        def _():
            sbig[0:SHARD_IN, :] = win_ref[...].astype(BF16)
            sbig[SHARD_IN:, :] = jnp.zeros((SHARD_IN_PAD - SHARD_IN, d), BF16)
            ssmall[...] = meta_ref[...]
            plan()[0]()

        @pl.when(t == relay_step)
        def _():
            plan()[1]()

        @pl.when(t == forward_step)
        def _():
            plan()[2]()

        @pl.when(t == finish_step)
        def _():
            plan()[3]()
            w_in_p[N_A:GRP_A, :] = jnp.zeros((GRP_A - N_A, d), BF16)
            for k in range(N_DEV):
                for s0, e0, d0 in _in_pieces(k):
                    w_in_p[d0:d0 + e0 - s0, :] = gbig[k, s0:e0, :]
                meta_f[:, SHARD_META * k:SHARD_META * (k + 1)] = gsmall[k]

        def norm(h):
            hhat, _ = _rms_stats(h)
            return (hhat * g_ref[...]).astype(BF16)

        @pl.when(jj < nb - 1)
        def _():
            u_ref[...] = norm(x_ref[0])

        @pl.when(jj == nb - 1)
        def _():
            u_ref[0:PAD_FRONT, :] = jnp.zeros((PAD_FRONT, d), BF16)
            u_ref[PAD_FRONT:LANES, :] = norm(meta_f[...])

    whole = lambda shape: pl.BlockSpec(shape, lambda jj, b: (0,) * len(shape))
    return pl.pallas_call(
        body,
        name="prep_norm_gather",
        grid=(nb, nb_seq),
        in_specs=[
            pl.BlockSpec((1, LANES, d), lambda jj, b: (b, jnp.minimum(jj, nb - 2), 0)),
            whole(meta.shape), whole(norm_g.shape), whole(w_in_t.shape),
        ],
        out_specs=(pl.BlockSpec((LANES, d), lambda jj, b: (b * nb + (jj + 1) % nb, 0)),
                   whole((IN_PAD, d)), whole((N_META, d))),
        out_shape=(jax.ShapeDtypeStruct((nb_seq * nb * LANES, d), BF16),
                   jax.ShapeDtypeStruct((IN_PAD, d), BF16),
                   jax.ShapeDtypeStruct((N_META, d), F32)),
        scratch_shapes=[
            pltpu.VMEM((SHARD_IN_PAD, d), BF16),
            pltpu.VMEM((N_META, SHARD_META), F32),
            pltpu.VMEM((N_DEV, SHARD_IN_PAD, d), BF16),
            pltpu.VMEM((N_DEV, N_META, SHARD_META), F32),
            pltpu.SemaphoreType.DMA((14,)),
            pltpu.SemaphoreType.DMA((14,)),
            pltpu.SemaphoreType.DMA((2,)),
        ],
        compiler_params=_params("arbitrary", "arbitrary"),
    )(x, meta, norm_g, w_in_t)


def _in_proj_gather(u, w_in_p, w_q, w_kv, w_out, conv_w, bm, bn):
    m, k_dim = u.shape
    n = w_in_p.shape[0]
    steps = (m // bm) * (n // bn)
    relay_step, forward_step = steps // 3, 2 * steps // 3
    qkv_rows = Q_RANK + KV_RANK

    def body(a_ref, b_ref, wq_ref, wkv_ref, wout_ref, conv_ref, o_ref, w_q_p, w_kv_p, w_out_f, conv_f,
             sqkv, sout, sconv, gqkv, gout, gconv, send_sems, recv_sems, local_sems):
        t = pl.program_id(0) * (n // bn) + pl.program_id(1)

        def plan():
            return _gather_plan((sqkv, sout, sconv), (gqkv, gout, gconv), send_sems, recv_sems, local_sems)

        @pl.when(t == 0)
        def _():
            sqkv[...] = jnp.zeros_like(sqkv)
            sqkv[0:Q_RANK, 0:SHARD_Q] = wq_ref[...].astype(BF16)
            sqkv[Q_RANK:, :] = wkv_ref[...].astype(BF16)
            sout[...] = wout_ref[...].astype(BF16)
            sconv[...] = jnp.zeros_like(sconv)
            sconv[0:3, 0:SHARD_CONV] = conv_ref[...]
            plan()[0]()

        @pl.when(t == relay_step)
        def _():
            plan()[1]()

        @pl.when(t == forward_step)
        def _():
            plan()[2]()

        o_ref[...] = _dot(a_ref[...], b_ref[...], _NT).astype(o_ref.dtype)

        @pl.when(t == steps - 1)
        def _():
            plan()[3]()
            conv_f[...] = jnp.zeros_like(conv_f)
            for k in range(N_DEV):
                for s0, e0, d0 in _q_pieces(k):
                    w_q_p[:, d0:d0 + e0 - s0] = gqkv[k, 0:Q_RANK, s0:e0]
                w_kv_p[:, _kv_dst(k):_kv_dst(k) + SHARD_KV] = gqkv[k, Q_RANK:, :]
                w_out_f[SHARD_OUT * k:SHARD_OUT * (k + 1), :] = gout[k]
                conv_f[0:3, SHARD_CONV * k:SHARD_CONV * (k + 1)] = gconv[k, 0:3, 0:SHARD_CONV]

    whole = lambda shape: pl.BlockSpec(shape, lambda i, j: (0,) * len(shape))
    return pl.pallas_call(
        body,
        name="in_proj_gather",
        grid=(m // bm, n // bn),
        in_specs=[pl.BlockSpec((bm, k_dim), lambda i, j: (i, 0)), pl.BlockSpec((bn, k_dim), lambda i, j: (j, 0)),
                  whole(w_q.shape), whole(w_kv.shape), whole(w_out.shape), whole(conv_w.shape)],
        out_specs=(pl.BlockSpec((bm, bn), lambda i, j: (i, j)),
                   whole((Q_RANK, Q_COLS)), whole((KV_RANK, KV_COLS)), whole((D_MODEL, D_MODEL)),
                   whole((8, CONV_WIDTH))),
        out_shape=(jax.ShapeDtypeStruct((m, n), BF16),
                   jax.ShapeDtypeStruct((Q_RANK, Q_COLS), BF16),
                   jax.ShapeDtypeStruct((KV_RANK, KV_COLS), BF16),
                   jax.ShapeDtypeStruct((D_MODEL, D_MODEL), BF16),
                   jax.ShapeDtypeStruct((8, CONV_WIDTH), F32)),
        scratch_shapes=[
            pltpu.VMEM((qkv_rows, LANES), BF16),
            pltpu.VMEM((SHARD_OUT, D_MODEL), BF16),
            pltpu.VMEM((8, LANES), F32),
            pltpu.VMEM((N_DEV, qkv_rows, LANES), BF16),
            pltpu.VMEM((N_DEV, SHARD_OUT, D_MODEL), BF16),
            pltpu.VMEM((N_DEV, 8, LANES), F32),
            pltpu.SemaphoreType.DMA((21,)),
            pltpu.SemaphoreType.DMA((21,)),
            pltpu.SemaphoreType.DMA((3,)),
        ],
        compiler_params=_params("arbitrary", "arbitrary"),
    )(u, w_in_p, w_q, w_kv, w_out, conv_w)


def _rope_tables(tp):
    half = D_ROPE // 2
    inv_freq = 1.0 / (ROPE_THETA ** (jnp.arange(half, dtype=F32) / half))
    pos = (jnp.arange(tp) - PAD_FRONT).astype(F32)
    ang = pos[:, None] * inv_freq[None, :]
    cos = jnp.tile(jnp.cos(ang), (1, LANES // half))
    sin = jnp.tile(jnp.sin(ang), (1, LANES // half))
    first = (jnp.arange(LANES) % D_ROPE) < half
    return cos, jnp.where(first, -sin, 0.0), jnp.where(first, 0.0, sin)


def _rope(t, cos, sa, sb):
    return t * cos + pltpu.roll(t, LANES - D_ROPE // 2, 1) * sa + pltpu.roll(t, D_ROPE // 2, 1) * sb


def _rope_t(t, cos, sa, sb):
    return t * cos + pltpu.roll(t * sa, D_ROPE // 2, 1) + pltpu.roll(t * sb, LANES - D_ROPE // 2, 1)


def _qkv_fwd(p, wq, wkv, gq, gkv, tables, nb_seq, tp):
    ht = tp // 2

    def body(pa_ref, wq_ref, wkv_ref, gq_ref, gkv_ref, cos_ref, sa_ref, sb_ref, q_ref, k_ref, v_ref):
        pa = pa_ref[...].astype(F32)
        cq_hat, _ = _rms_stats(pa[:, :Q_RANK])
        ckv_hat, _ = _rms_stats(pa[:, Q_RANK:Q_RANK + KV_RANK])
        q = _dot((cq_hat * gq_ref[...]).astype(BF16), wq_ref[...]) * Q_SCALE
        kv = _dot((ckv_hat * gkv_ref[...]).astype(BF16), wkv_ref[...])
        tabs = (cos_ref[...], sa_ref[...], sb_ref[...])
        lane = lax.broadcasted_iota(jnp.int32, (ht, LANES), 1)
        low = lane < D_ROPE
        mark = lane == D_ROPE
        row = (pl.program_id(0) % 2) * ht + lax.broadcasted_iota(jnp.int32, (ht, LANES), 0)
        k_pe = jnp.where(mark & (row < PAD_FRONT), NEG_INF, _rope(pa[:, Q_RANK + KV_RANK:], *tabs))
        one = jnp.where(mark & (row >= PAD_FRONT), 1.0, 0.0)
        pairs = [_rope(q[:, N_HEADS * D_NOPE + LANES * i:N_HEADS * D_NOPE + LANES * (i + 1)], *tabs) for i in range(2)]
        for h in range(N_HEADS):
            pair = pairs[h // 2]
            if h % 2:
                pair = pltpu.roll(pair, D_ROPE, 1)
            pe = jnp.where(low, pair, one)
            q_ref[0, h] = jnp.concatenate([q[:, D_NOPE * h:D_NOPE * (h + 1)], pe], axis=1).astype(BF16)
            k_ref[0, h] = jnp.concatenate([kv[:, D_NOPE * h:D_NOPE * (h + 1)], k_pe], axis=1).astype(BF16)
            v_ref[0, h] = kv[:, N_HEADS * D_NOPE + D_V * h:N_HEADS * D_NOPE + D_V * (h + 1)].astype(BF16)

    full = lambda a: pl.BlockSpec(a.shape, lambda i: (0,) * a.ndim)
    tab = pl.BlockSpec((ht, LANES), lambda i: (i % 2, 0))
    qk = pl.BlockSpec((1, N_HEADS, ht, 2 * LANES), lambda i: (i // 2, 0, i % 2, 0))
    return pl.pallas_call(
        body,
        name="qkv_fwd",
        grid=(2 * nb_seq,),
        in_specs=[pl.BlockSpec((ht, GRP_A), lambda i: (i, 0)), full(wq), full(wkv), full(gq), full(gkv), tab, tab, tab],
        out_specs=(qk, qk, pl.BlockSpec((1, N_HEADS, ht, D_V), lambda i: (i // 2, 0, i % 2, 0))),
        out_shape=(
            jax.ShapeDtypeStruct((nb_seq, N_HEADS, tp, 2 * LANES), BF16),
            jax.ShapeDtypeStruct((nb_seq, N_HEADS, tp, 2 * LANES), BF16),
            jax.ShapeDtypeStruct((nb_seq, N_HEADS, tp, D_V), BF16),
        ),
        compiler_params=_params("parallel"),
    )(p, wq, wkv, gq, gkv, *tables)


def _attn_fwd(q, k, v, p, g_attn):
    nb_seq, _, tp, _ = q.shape

    def body(q_ref, k_ref, v_ref, z_ref, g_ref, y_ref, o_ref, lse_ref):
        g = g_ref[...]
        for r0 in range(0, tp, Q_TILE):
            nq = min(Q_TILE, tp - r0)
            kend = r0 + nq
            qq = q_ref[0, 0, r0:kend, :]
            sd = _dot(qq, k_ref[0, 0, r0:kend, :], _NT)
            causal = (lax.broadcasted_iota(jnp.int32, (nq, nq), 1) <= lax.broadcasted_iota(jnp.int32, (nq, nq), 0))
            sd = jnp.where(causal, sd, NEG_INF)
            m = jnp.max(sd, axis=-1, keepdims=True)
            if r0:
                so = _dot(qq, k_ref[0, 0, 0:r0, :], _NT)
                m = jnp.maximum(m, jnp.max(so, axis=-1, keepdims=True))
            ed = jnp.exp2(sd - m)
            l = jnp.sum(ed, axis=-1, keepdims=True)
            o = _dot(ed.astype(BF16), v_ref[0, 0, r0:kend, :])
            if r0:
                eo = jnp.exp2(so - m)
                l = l + jnp.sum(eo, axis=-1, keepdims=True)
                o = o + _dot(eo.astype(BF16), v_ref[0, 0, 0:r0, :])
            o = o * (1.0 / l)
            o_ref[0, 0, r0:kend, :] = o
            lse_ref[0, 0, r0:kend, :] = jnp.broadcast_to(m + jnp.log2(l), (nq, LANES))
            ohat, _ = _rms_stats(o)
            z = z_ref[r0:kend, :].astype(F32)
            y_ref[r0:kend, :] = (ohat * g * (z * _sigmoid(z))).astype(BF16)

    qk = pl.BlockSpec((1, 1, tp, 2 * LANES), lambda b, h: (b, h, 0, 0))
    hv = pl.BlockSpec((1, 1, tp, D_V), lambda b, h: (b, h, 0, 0))
    return pl.pallas_call(
        body,
        name="attn_fwd",
        grid=(nb_seq, N_HEADS),
        in_specs=[qk, qk, hv,
                  pl.BlockSpec((tp, LANES), lambda b, h: (b, GRP_A // LANES + h)),
                  pl.BlockSpec((1, LANES), lambda b, h: (0, h))],
        out_specs=(pl.BlockSpec((tp, LANES), lambda b, h: (b, h)), hv, hv),
        out_shape=(
            jax.ShapeDtypeStruct((nb_seq * tp, N_HEADS * D_V), BF16),
            jax.ShapeDtypeStruct((nb_seq, N_HEADS, tp, D_V), F32),
            jax.ShapeDtypeStruct((nb_seq, N_HEADS, tp, LANES), F32),
        ),
        compiler_params=_params("parallel", "parallel"),
    )(q, k, v, p, g_attn)


_CONV_COL0 = (GRP_A + N_HEADS * D_V) // LANES


def _conv_specs(tp, order):
    cols = CONV_WIDTH // LANES
    return [pl.BlockSpec((tp, LANES), functools.partial(
        lambda a, b, off: order(a, b, off), off=_CONV_COL0 + i * cols)) for i in range(4)]


def _conv_fwd(p, conv_w, g_conv, nb_seq, tp):
    def body(b_ref, c_ref, h_ref, z_ref, w_ref, g_ref, y_ref):
        cc = c_ref[...].astype(F32) * h_ref[...].astype(F32)
        row = lax.broadcasted_iota(jnp.int32, (tp, LANES), 0)
        s1 = jnp.where(row >= 1, pltpu.roll(cc, 1, 0), 0.0)
        s2 = jnp.where(row >= 2, pltpu.roll(cc, 2, 0), 0.0)
        yc = b_ref[...].astype(F32) * (w_ref[0:1, :] * s2 + w_ref[1:2, :] * s1 + w_ref[2:3, :] * cc)
        r = lax.rsqrt(_group_mean(yc * yc) + EPS)
        z = z_ref[...].astype(F32)
        y_ref[...] = (yc * r * g_ref[...] * (z * _sigmoid(z))).astype(BF16)

    return pl.pallas_call(
        body,
        name="conv_fwd",
        grid=(nb_seq, CONV_WIDTH // LANES),
        in_specs=_conv_specs(tp, lambda b, t, off: (b, off + t)) + [
            pl.BlockSpec((8, LANES), lambda b, t: (0, t)),
            pl.BlockSpec((1, LANES), lambda b, t: (0, t))],
        out_specs=pl.BlockSpec((tp, LANES), lambda b, t: (b, t)),
        out_shape=jax.ShapeDtypeStruct((nb_seq * tp, CONV_WIDTH), BF16),
        compiler_params=_params("parallel", "parallel"),
    )(p, p, p, p, conv_w, g_conv)


def _token_copy(hbm, b, k, ts, buf, sem, to_hbm=False):
    lo, hi = max(k * ts - LANES, 0), (k + 1) * ts - LANES
    off = lo - (k * ts - LANES)
    src, dst = hbm.at[b, pl.ds(lo, hi - lo)], buf.at[pl.ds(off, hi - lo)]
    if to_hbm:
        src, dst = dst, src
    return pltpu.make_async_copy(src, dst, sem)


def _for_tile(k, nt, fn):
    for kk in range(nt):
        @pl.when(k == kk)
        def _(kk=kk):
            fn(kk)


def _out_proj_loss(ya, yc, w_out, x, target, g_final, nt):
    nb_seq, s, d = x.shape
    r, ka = ya.shape
    ts = (s + LANES) // nt
    steps = nb_seq * nt

    def body(a_ref, c_ref, w_ref, x_hbm, t_hbm, g_ref, dh_ref, dhb_ref, dg_ref, loss_ref,
             xbuf, tbuf, acc_ref, sems):
        i = pl.program_id(0)
        b, k = i // nt, i % nt

        @pl.when(i == 0)
        def _():
            acc_ref[...] = jnp.zeros_like(acc_ref)
            dg_ref[...] = jnp.zeros_like(dg_ref)

        slot = i % 2

        def fetch(seq, kk, sl):
            return [_token_copy(x_hbm, seq, kk, ts, xbuf.at[sl], sems.at[sl, 0]),
                    _token_copy(t_hbm, seq, kk, ts, tbuf.at[sl], sems.at[sl, 1])]

        def start(seq, sl, kk):
            if kk == 0:
                xbuf[sl, 0:LANES, :] = jnp.zeros((LANES, d), F32)
                tbuf[sl, 0:LANES, :] = jnp.zeros((LANES, d), F32)
            for cp in fetch(seq, kk, sl):
                cp.start()

        @pl.when(i == 0)
        def _():
            start(0, 0, 0)

        @pl.when(i + 1 < steps)
        def _():
            _for_tile((i + 1) % nt, nt, functools.partial(start, (i + 1) // nt, 1 - slot))

        mix = _dot(a_ref[...], w_ref[0:ka, :]) + _dot(c_ref[...], w_ref[ka:, :])
        _for_tile(k, nt, lambda kk: [cp.wait() for cp in fetch(b, kk, slot)])

        real = (lax.broadcasted_iota(jnp.int32, (ts, d), 0) >= LANES) | (k > 0)
        g = g_ref[...]
        hhat, rstd = _rms_stats(xbuf[slot] + mix)
        e = jnp.where(real, hhat * g - tbuf[slot], 0.0)
        acc_ref[...] += jnp.sum(e * e, axis=0, keepdims=True)
        dy = e * (1.0 / d)
        dg_ref[...] += jnp.sum(dy * hhat, axis=0, keepdims=True)
        dh = _rms_bwd(g * dy, hhat, rstd)
        dh_ref[...] = dh
        dhb_ref[...] = dh.astype(BF16)

        @pl.when(i == steps - 1)
        def _():
            total = jnp.sum(acc_ref[...], axis=1, keepdims=True)
            loss_ref[...] = jnp.broadcast_to((0.5 / d) * total, loss_ref.shape)

    hbm = pl.BlockSpec(memory_space=pl.ANY)
    row = pl.BlockSpec((ts, d), lambda i: (i, 0))
    vec = pl.BlockSpec((1, d), lambda i: (0, 0))
    return pl.pallas_call(
        body,
        name="out_proj_loss",
        grid=(steps,),
        in_specs=[pl.BlockSpec((ts, ka), lambda i: (i, 0)), pl.BlockSpec((ts, yc.shape[1]), lambda i: (i, 0)),
                  pl.BlockSpec(w_out.shape, lambda i: (0, 0)), hbm, hbm, vec],
        out_specs=(row, row, vec, pl.BlockSpec((1, LANES), lambda i: (0, 0))),
        out_shape=(
            jax.ShapeDtypeStruct((r, d), F32),
            jax.ShapeDtypeStruct((r, d), BF16),
            jax.ShapeDtypeStruct((1, d), F32),
            jax.ShapeDtypeStruct((1, LANES), F32),
        ),
        scratch_shapes=[pltpu.VMEM((2, ts, d), F32), pltpu.VMEM((2, ts, d), F32), pltpu.VMEM((1, d), F32),
                        pltpu.SemaphoreType.DMA((2, 2))],
        compiler_params=_params("arbitrary"),
    )(ya, yc, w_out, x, target, g_final)


def _out_proj_bwd(dhb, w_out, ya, yc, bm):
    r, d = dhb.shape
    ka = ya.shape[1]
    n_mix = w_out.shape[0]
    last = r // bm - 1

    def body(dh_ref, w_ref, a_ref, c_ref, dcat_ref, dw_ref, acc_ref):
        @pl.when(pl.program_id(0) == 0)
        def _():
            acc_ref[...] = jnp.zeros_like(acc_ref)

        dh = dh_ref[...]
        dcat_ref[...] = _dot(dh, w_ref[...], _NT).astype(BF16)
        acc_ref[0:ka, :] += _dot(a_ref[...], dh, _TN)
        acc_ref[ka:, :] += _dot(c_ref[...], dh, _TN)

        @pl.when(pl.program_id(0) == last)
        def _():
            dw_ref[...] = acc_ref[...].astype(BF16)

    return pl.pallas_call(
        body,
        name="out_proj_bwd",
        grid=(r // bm,),
        in_specs=[pl.BlockSpec((bm, d), lambda i: (i, 0)), pl.BlockSpec(w_out.shape, lambda i: (0, 0)),
                  pl.BlockSpec((bm, ka), lambda i: (i, 0)), pl.BlockSpec((bm, yc.shape[1]), lambda i: (i, 0))],
        out_specs=(pl.BlockSpec((bm, n_mix), lambda i: (i, 0)),
                   pl.BlockSpec((n_mix, d), lambda i: (0, 0))),
        out_shape=(jax.ShapeDtypeStruct((r, n_mix), BF16),
                   jax.ShapeDtypeStruct((n_mix, d), BF16)),
        scratch_shapes=[pltpu.VMEM((n_mix, d), F32)],
        compiler_params=_params("arbitrary"),
    )(dhb, w_out, ya, yc)


def _attn_bwd(q, k, v, o, lse, dcat, p, g_attn, send_out):
    nb_seq, _, tp, _ = q.shape
    steps = N_HEADS * nb_seq

    def body(q_ref, k_ref, v_ref, o_ref, lse_ref, dy_ref, z_ref, g_ref, pay_ref,
             dq_ref, dk_ref, dv_ref, dz_ref, dg_ref, r2_ref, dq_acc, r1, sums, *sems):
        t = pl.program_id(0) * nb_seq + pl.program_id(1)

        def plan():
            return _reduce_plan((pay_ref,), (None,), (r1,), (sums,), (r2_ref,), *sems)

        @pl.when(t == 0)
        def _():
            plan()[0]()

        @pl.when(t == 1)
        def _():
            plan()[1]()

        @pl.when(pl.program_id(1) == 0)
        def _():
            dg_ref[...] = jnp.zeros_like(dg_ref)

        g = g_ref[...]
        z = z_ref[...].astype(F32)
        o = o_ref[0, 0]
        dy = dy_ref[...].astype(F32)
        sig = _sigmoid(z)
        ohat, r = _rms_stats(o)
        don = dy * (z * sig)
        dz_ref[...] = (dy * (ohat * g) * (sig * (1.0 + z * (1.0 - sig)))).astype(BF16)
        dg_ref[...] += jnp.sum(don * ohat, axis=0, keepdims=True)
        do = _rms_bwd(g * don, ohat, r)
        dvec = jnp.sum(do * o, axis=-1, keepdims=True)
        dob = do.astype(BF16)
        lse_col = lse_ref[0, 0, :, 0:1]
        dq_acc[...] = jnp.zeros_like(dq_acc)
        for k0 in range(0, tp, K_TILE):
            nk = min(K_TILE, tp - k0)
            nq = tp - k0
            qq = q_ref[0, 0, k0:, :]
            kk = k_ref[0, 0, k0:k0 + nk, :]
            causal = (lax.broadcasted_iota(jnp.int32, (nq, nk), 1) <= lax.broadcasted_iota(jnp.int32, (nq, nk), 0))
            pr = jnp.where(causal, jnp.exp2(_dot(qq, kk, _NT) - lse_col[k0:]), 0.0)
            dp = _dot(dob[k0:], v_ref[0, 0, k0:k0 + nk, :], _NT)
            ds = (pr * (dp - dvec[k0:])).astype(BF16)
            dv_ref[0, 0, k0:k0 + nk, :] = _dot(pr.astype(BF16), dob[k0:], _TN).astype(BF16)
            dk_ref[0, 0, k0:k0 + nk, :] = (_dot(ds, qq, _TN) * (ATTN_SCALE / Q_SCALE)).astype(BF16)
            dq_acc[k0:, :] += _dot(ds, kk)
        dq_ref[0, 0] = (dq_acc[...] * ATTN_SCALE).astype(BF16)

        @pl.when(t == steps - 1)
        def _():
            plan()[2]()

    qk = pl.BlockSpec((1, 1, tp, 2 * LANES), lambda h, b: (b, h, 0, 0))
    hv = pl.BlockSpec((1, 1, tp, D_V), lambda h, b: (b, h, 0, 0))
    col = pl.BlockSpec((tp, LANES), lambda h, b: (b, h))
    slot = send_out.shape[1:]
    return pl.pallas_call(
        body,
        name="attn_bwd",
        grid=(N_HEADS, nb_seq),
        in_specs=[qk, qk, hv, hv, hv, col,
                  pl.BlockSpec((tp, LANES), lambda h, b: (b, GRP_A // LANES + h)),
                  pl.BlockSpec((1, LANES), lambda h, b: (0, h)),
                  pl.BlockSpec(send_out.shape, lambda h, b: (0, 0, 0))],
        out_specs=(qk, qk, hv, col, pl.BlockSpec((1, LANES), lambda h, b: (0, h)),
                   pl.BlockSpec(memory_space=pl.ANY)),
        out_shape=(
            jax.ShapeDtypeStruct((nb_seq, N_HEADS, tp, 2 * LANES), BF16),
            jax.ShapeDtypeStruct((nb_seq, N_HEADS, tp, 2 * LANES), BF16),
            jax.ShapeDtypeStruct((nb_seq, N_HEADS, tp, D_V), BF16),
            jax.ShapeDtypeStruct((nb_seq * tp, N_HEADS * D_V), BF16),
            jax.ShapeDtypeStruct((1, N_HEADS * D_V), F32),
            jax.ShapeDtypeStruct((N_CHIPS,) + slot, BF16),
        ),
        scratch_shapes=[pltpu.VMEM((tp, 2 * LANES), F32)] + _reduce_scratch([(slot, BF16)], [False]),
        compiler_params=_params("arbitrary", "arbitrary"),
    )(q, k, v, o, lse, dcat, p, g_attn, send_out)


def _qkv_bwd(p, dq, dk, dv, wq, wkv, gq, gkv, tables):
    nb_seq, _, tp, _ = dq.shape
    ht = tp // 2

    def body(pa_ref, dq_ref, dk_ref, dv_ref, wq_ref, wkv_ref, gq_ref, gkv_ref, cos_ref, sa_ref, sb_ref,
             dpa_ref, dwq_ref, dwkv_ref, dgq_ref, dgkv_ref):
        @pl.when(pl.program_id(0) == 0)
        def _():
            dwq_ref[...] = jnp.zeros_like(dwq_ref)
            dwkv_ref[...] = jnp.zeros_like(dwkv_ref)
            dgq_ref[...] = jnp.zeros_like(dgq_ref)
            dgkv_ref[...] = jnp.zeros_like(dgkv_ref)

        pa = pa_ref[...].astype(F32)
        gq, gkv = gq_ref[...], gkv_ref[...]
        cq_hat, rq = _rms_stats(pa[:, :Q_RANK])
        ckv_hat, rkv = _rms_stats(pa[:, Q_RANK:Q_RANK + KV_RANK])
        tabs = (cos_ref[...], sa_ref[...], sb_ref[...])

        pe = [dq_ref[0, h, :, D_NOPE:].astype(F32) for h in range(N_HEADS)]
        pairs = [_rope_t(pe[2 * i] + pltpu.roll(pe[2 * i + 1], D_ROPE, 1), *tabs).astype(BF16) for i in range(2)]
        dq_flat = jnp.concatenate([dq_ref[0, h, :, :D_NOPE] for h in range(N_HEADS)] + pairs, axis=1)
        dwq_ref[...] += _dot((cq_hat * gq).astype(BF16), dq_flat, _TN)
        dcqn = _dot(dq_flat, wq_ref[...], _NT)
        dgq_ref[...] += jnp.sum(dcqn * cq_hat, axis=0, keepdims=True)
        dcq = _rms_bwd(gq * dcqn, cq_hat, rq)

        dkv_flat = jnp.concatenate([dk_ref[0, h, :, :D_NOPE] for h in range(N_HEADS)]
                                   + [dv_ref[0, h] for h in range(N_HEADS)], axis=1)
        dwkv_ref[...] += _dot((ckv_hat * gkv).astype(BF16), dkv_flat, _TN)
        dckvn = _dot(dkv_flat, wkv_ref[...], _NT)
        dgkv_ref[...] += jnp.sum(dckvn * ckv_hat, axis=0, keepdims=True)
        dckv = _rms_bwd(gkv * dckvn, ckv_hat, rkv)

        dk_pe = dk_ref[0, 0, :, D_NOPE:].astype(F32)
        for h in range(1, N_HEADS):
            dk_pe = dk_pe + dk_ref[0, h, :, D_NOPE:].astype(F32)
        dk_pe = jnp.where(lax.broadcasted_iota(jnp.int32, (ht, LANES), 1) < D_ROPE, dk_pe, 0.0)
        dpa_ref[...] = jnp.concatenate([dcq, dckv, _rope_t(dk_pe, *tabs)], axis=1).astype(BF16)

    full = lambda a: pl.BlockSpec(a.shape, lambda i: (0,) * a.ndim)
    tab = pl.BlockSpec((ht, LANES), lambda i: (i % 2, 0))
    qk = pl.BlockSpec((1, N_HEADS, ht, 2 * LANES), lambda i: (i // 2, 0, i % 2, 0))
    acc = lambda shape: pl.BlockSpec(shape, lambda i: (0, 0))
    return pl.pallas_call(
        body,
        name="qkv_bwd",
        grid=(2 * nb_seq,),
        in_specs=[pl.BlockSpec((ht, GRP_A), lambda i: (i, 0)), qk, qk,
                  pl.BlockSpec((1, N_HEADS, ht, D_V), lambda i: (i // 2, 0, i % 2, 0)),
                  full(wq), full(wkv), full(gq), full(gkv), tab, tab, tab],
        out_specs=(pl.BlockSpec((ht, GRP_A), lambda i: (i, 0)),
                   acc(wq.shape), acc(wkv.shape), acc((1, Q_RANK)), acc((1, KV_RANK))),
        out_shape=(
            jax.ShapeDtypeStruct((nb_seq * tp, GRP_A), BF16),
            jax.ShapeDtypeStruct(wq.shape, F32),
            jax.ShapeDtypeStruct(wkv.shape, F32),
            jax.ShapeDtypeStruct((1, Q_RANK), F32),
            jax.ShapeDtypeStruct((1, KV_RANK), F32),
        ),
        compiler_params=_params("arbitrary"),
    )(p, dq, dk, dv, wq, wkv, gq, gkv, *tables)


def _conv_bwd(p, dcat, conv_w, g_conv, nb_seq, tp):
    cols = CONV_WIDTH // LANES

    def body(b_ref, c_ref, h_ref, z_ref, dy_ref, w_ref, g_ref,
             db_ref, dc_ref, dh_ref, dz_ref, dw_ref, dg_ref):
        @pl.when(pl.program_id(1) == 0)
        def _():
            dw_ref[...] = jnp.zeros_like(dw_ref)
            dg_ref[...] = jnp.zeros_like(dg_ref)

        cb, c, h = b_ref[...].astype(F32), c_ref[...].astype(F32), h_ref[...].astype(F32)
        z, dy = z_ref[...].astype(F32), dy_ref[...].astype(F32)
        g = g_ref[...]
        w0, w1, w2 = w_ref[0:1, :], w_ref[1:2, :], w_ref[2:3, :]
        cc = c * h
        row = lax.broadcasted_iota(jnp.int32, (tp, LANES), 0)
        s1 = jnp.where(row >= 1, pltpu.roll(cc, 1, 0), 0.0)
        s2 = jnp.where(row >= 2, pltpu.roll(cc, 2, 0), 0.0)
        dwc = w0 * s2 + w1 * s1 + w2 * cc
        yc = cb * dwc
        r = lax.rsqrt(_group_mean(yc * yc) + EPS)
        ychat = yc * r
        sig = _sigmoid(z)
        dz_ref[...] = (dy * (ychat * g) * (sig * (1.0 + z * (1.0 - sig)))).astype(BF16)
        dyn = dy * (z * sig)
        dg_ref[...] += jnp.sum(dyn * ychat, axis=0, keepdims=True)
        gd = g * dyn
        dyc = r * (gd - ychat * _group_mean(gd * ychat))
        db_ref[...] = (dyc * dwc).astype(BF16)
        ddw = dyc * cb
        dw_ref[0:1, :] += jnp.sum(ddw * s2, axis=0, keepdims=True)
        dw_ref[1:2, :] += jnp.sum(ddw * s1, axis=0, keepdims=True)
        dw_ref[2:3, :] += jnp.sum(ddw * cc, axis=0, keepdims=True)
        u1 = jnp.where(row <= tp - 2, pltpu.roll(ddw, tp - 1, 0), 0.0)
        u2 = jnp.where(row <= tp - 3, pltpu.roll(ddw, tp - 2, 0), 0.0)
        dcc = w2 * ddw + w1 * u1 + w0 * u2
        dc_ref[...] = (dcc * h).astype(BF16)
        dh_ref[...] = (dcc * c).astype(BF16)

    col = pl.BlockSpec((tp, LANES), lambda t, b: (b, t))
    out = jax.ShapeDtypeStruct((nb_seq * tp, CONV_WIDTH), BF16)
    return pl.pallas_call(
        body,
        name="conv_bwd",
        grid=(cols, nb_seq),
        in_specs=_conv_specs(tp, lambda t, b, off: (b, off + t)) + [
            pl.BlockSpec((tp, LANES), lambda t, b: (b, N_HEADS * D_V // LANES + t)),
            pl.BlockSpec((8, LANES), lambda t, b: (0, t)),
            pl.BlockSpec((1, LANES), lambda t, b: (0, t))],
        out_specs=(col, col, col, col,
                   pl.BlockSpec((8, LANES), lambda t, b: (0, t)), pl.BlockSpec((1, LANES), lambda t, b: (0, t))),
        out_shape=(out, out, out, out,
                   jax.ShapeDtypeStruct((8, CONV_WIDTH), F32), jax.ShapeDtypeStruct((1, CONV_WIDTH), F32)),
        compiler_params=_params("arbitrary", "arbitrary"),
    )(p, p, p, p, dcat, conv_w, g_conv)


def _input_bwd(dps, w_in, x, meta, dh, norm_g, nt, send_in):
    nb_seq, s, d = x.shape
    r, kb = dps[0].shape
    ts = (s + LANES) // nt
    steps = nb_seq * nt
    n_dp = len(dps)
    in_slot = send_in.shape[1:]

    def body(*refs):
        dp_refs, w_ref, x_hbm, meta_ref, dh_ref, g_ref, pay_ref = refs[:n_dp], *refs[n_dp:n_dp + 6]
        o = n_dp + 6
        gx_hbm, dmeta_ref, dg_ref, r2_in = refs[o:o + 4]
        xbuf, gxbuf, tok_sems, own_in, r1_in, sum_in = refs[o + 4:o + 10]
        sems = refs[o + 10:]
        i = pl.program_id(0)
        b, k = i // nt, i % nt

        def plan():
            return _reduce_plan((pay_ref,), (own_in,), (r1_in,), (sum_in,), (r2_in,), *sems)

        @pl.when(i == 0)
        def _():
            dmeta_ref[...] = jnp.zeros_like(dmeta_ref)
            dg_ref[...] = jnp.zeros_like(dg_ref)
            plan()[0]()

        @pl.when(i == 1)
        def _():
            plan()[1]()

        def start(kk):
            if kk == 0:
                xbuf[0:PAD_FRONT, :] = jnp.zeros((PAD_FRONT, d), F32)
                xbuf[PAD_FRONT:LANES, :] = meta_ref[...]
            _token_copy(x_hbm, b, kk, ts, xbuf, tok_sems.at[0]).start()

        _for_tile(k, nt, start)
        du = _dot(dp_refs[0][...], w_ref[0:kb, :])
        for j in range(1, n_dp):
            du = du + _dot(dp_refs[j][...], w_ref[kb * j:kb * (j + 1), :])
        _for_tile(k, nt, lambda kk: _token_copy(x_hbm, b, kk, ts, xbuf, tok_sems.at[0]).wait())

        g = g_ref[...]
        hhat, rstd = _rms_stats(xbuf[...])
        dg_ref[...] += jnp.sum(du * hhat, axis=0, keepdims=True)
        res = _rms_bwd(g * du, hhat, rstd) + dh_ref[...]

        @pl.when(i > 0)
        def _():
            _for_tile(k, nt, lambda kk: _token_copy(gx_hbm, b, (kk - 1) % nt, ts, gxbuf, tok_sems.at[1], True).wait())

        gxbuf[...] = res

        @pl.when(k == 0)
        def _():
            dmeta_ref[...] += gxbuf[PAD_FRONT:LANES, :]

        _for_tile(k, nt, lambda kk: _token_copy(gx_hbm, b, kk, ts, gxbuf, tok_sems.at[1], True).start())

        @pl.when(i == steps - 1)
        def _():
            _token_copy(gx_hbm, b, nt - 1, ts, gxbuf, tok_sems.at[1], True).wait()
            plan()[2]()

    whole = lambda a: pl.BlockSpec(a.shape, lambda i: (0,) * a.ndim)
    hbm = pl.BlockSpec(memory_space=pl.ANY)
    return pl.pallas_call(
        body,
        name="input_bwd",
        grid=(steps,),
        in_specs=[pl.BlockSpec((ts, kb), lambda i: (i, 0)) for _ in dps]
        + [whole(w_in), hbm, whole(meta), pl.BlockSpec((ts, d), lambda i: (i, 0)), whole(norm_g), hbm],
        out_specs=(hbm, pl.BlockSpec((N_META, d), lambda i: (0, 0)), pl.BlockSpec((1, d), lambda i: (0, 0)), hbm),
        out_shape=(jax.ShapeDtypeStruct((nb_seq, s, d), F32),
                   jax.ShapeDtypeStruct((N_META, d), F32),
                   jax.ShapeDtypeStruct((1, d), F32),
                   jax.ShapeDtypeStruct((N_CHIPS,) + in_slot, BF16)),
        scratch_shapes=[pltpu.VMEM((ts, d), F32), pltpu.VMEM((ts, d), F32), pltpu.SemaphoreType.DMA((2,))]
        + _reduce_scratch([(in_slot, BF16)], [True]),
        compiler_params=_params("arbitrary"),
    )(*dps, w_in, x, meta, dh, norm_g, send_in)


def _in_proj_bwd_w(u, dps, bm, small_grads):
    r, d = u.shape
    kb = dps[0].shape[1]
    steps = r // bm
    n_dp, n_small = len(dps), len(small_grads)
    small_slot = (SMALL_ROWS, LANES)

    def body(*refs):
        u_ref, dp_refs = refs[0], refs[1:1 + n_dp]
        small_refs = refs[1 + n_dp:1 + n_dp + n_small]
        o = 1 + n_dp + n_small
        o_ref, r2_small = refs[o:o + 2]
        acc_ref, ssmall, r1_small, sum_small = refs[o + 2:o + 6]
        sems = refs[o + 6:]
        i = pl.program_id(0)

        def plan():
            return _reduce_plan((ssmall,), (None,), (r1_small,), (sum_small,), (r2_small,), *sems)

        @pl.when(i == 0)
        def _():
            acc_ref[...] = jnp.zeros_like(acc_ref)
            _pack_small(ssmall, *small_refs)
            plan()[0]()

        @pl.when(i == 1)
        def _():
            plan()[1]()

        uu = u_ref[...]
        for j in range(n_dp):
            acc_ref[kb * j:kb * (j + 1), :] += _dot(dp_refs[j][...], uu, _TN)

        @pl.when(i == steps - 1)
        def _():
            for k in range(N_DEV):
                for s, e, c0 in _in_pieces(k):
                    o_ref[k, s:e, :] = acc_ref[c0:c0 + e - s, :].astype(BF16)
                o_ref[k, SHARD_IN:, :] = jnp.zeros((SHARD_IN_PAD - SHARD_IN, d), BF16)
            plan()[2]()

    whole = lambda a: pl.BlockSpec(a.shape, lambda i: (0,) * a.ndim)
    return pl.pallas_call(
        body,
        name="in_proj_bwd_w",
        grid=(steps,),
        in_specs=[pl.BlockSpec((bm, d), lambda i: (i, 0))]
        + [pl.BlockSpec((bm, kb), lambda i: (i, 0)) for _ in dps] + [whole(a) for a in small_grads],
        out_specs=(pl.BlockSpec((N_DEV, SHARD_IN_PAD, d), lambda i: (0, 0, 0)), pl.BlockSpec(memory_space=pl.ANY)),
        out_shape=(jax.ShapeDtypeStruct((N_DEV, SHARD_IN_PAD, d), BF16),
                   jax.ShapeDtypeStruct((N_CHIPS,) + small_slot, F32)),
        scratch_shapes=[pltpu.VMEM((kb * n_dp, d), F32), pltpu.VMEM((N_DEV,) + small_slot, F32)]
        + _reduce_scratch([(small_slot, F32)], [False]),
        compiler_params=_params("arbitrary"),
    )(u, *dps, *small_grads)


def _local_step(x, loss_target, u, p, meta_f, norm_g, w_in_p, q_norm_g, w_q_p, kv_norm_g, w_kv_p, conv_w_f,
                attn_out_g, conv_out_g, w_out_f, g_final):
    nb_seq, s, d = x.shape
    tp = s + LANES
    ht = tp // 2
    tables = _rope_tables(tp)

    q, k, v = _qkv_fwd(p, w_q_p, w_kv_p, q_norm_g, kv_norm_g, tables, nb_seq, tp)
    ya, o, lse = _attn_fwd(q, k, v, p, attn_out_g)
    yc = _conv_fwd(p, conv_w_f, conv_out_g, nb_seq, tp)
    dh, dhb, d_final_g, loss_part = _out_proj_loss(ya, yc, w_out_f, x, loss_target, g_final, TOKEN_TILES)

    dcat, d_w_out = _out_proj_bwd(dhb, w_out_f, ya, yc, ht)
    send_out = d_w_out.reshape(N_DEV, SHARD_OUT, d)
    dq, dk, dv, dz_attn, d_attn_g, r_out = _attn_bwd(q, k, v, o, lse, dcat, p, attn_out_g, send_out)
    dpa, d_wq_p, d_wkv_p, d_gq, d_gkv = _qkv_bwd(p, dq, dk, dv, w_q_p, w_kv_p, q_norm_g, kv_norm_g, tables)
    d_b, d_c, d_h, dz_conv, d_conv_w, d_conv_g = _conv_bwd(p, dcat, conv_w_f, conv_out_g, nb_seq, tp)
    dps = (dpa, dz_attn, d_b, d_c, d_h, dz_conv)
    small = (d_wq_p, d_wkv_p, d_conv_w, d_final_g, d_gq, d_gkv, d_attn_g, d_conv_g, loss_part)
    send_in, r_small = _in_proj_bwd_w(u, dps, ht // 2, small)
    grad_x, d_meta, d_norm_g, r_in = _input_bwd(dps, w_in_p, x, meta_f, dh, norm_g, TOKEN_TILES, send_in)
    return grad_x, r_in, r_out, r_small, d_meta, d_norm_g


def kernel(x, meta_tokens, norm_g, w_in, q_norm_g, w_q_up, kv_norm_g, w_kv_up, conv_w, attn_out_g, conv_out_g, w_out, final_norm_g, loss_target, m_meta_tokens, m_norm_g, m_w_in, m_q_norm_g, m_w_q_up, m_kv_norm_g, m_w_kv_up, m_conv_w, m_attn_out_g, m_conv_out_g, m_w_out, m_final_norm_g, v_meta_tokens, v_norm_g, v_w_in, v_q_norm_g, v_w_q_up, v_kv_norm_g, v_w_kv_up, v_conv_w, v_attn_out_g, v_conv_out_g, v_w_out, v_final_norm_g):
    d = x.shape[-1]
    ht = (x.shape[1] + LANES) // 2
    u, w_in_p, meta_f = _prep_gather(x, meta_tokens, norm_g, w_in[0].T)
    p, w_q_p, w_kv_p, w_out_f, conv_w_f = _in_proj_gather(
        u, w_in_p, w_q_up[0], w_kv_up[0], w_out[0], conv_w[0], ht, 3 * GRP_A)
    g_final = final_norm_g.reshape(1, d)
    grad_x, r_in, r_out, r_small, d_meta, d_norm_g = _local_step(
        x, loss_target, u, p, meta_f, norm_g, w_in_p, q_norm_g, w_q_p, kv_norm_g, w_kv_p, conv_w_f,
        attn_out_g, conv_out_g, w_out_f, g_final)

    flat = lambda a: a.reshape(a.shape[-2:]) if a.ndim == 3 else a.reshape(1, -1) if a.ndim == 1 else a
    transposed = ("w_in",)
    to_kernel = lambda n, a: flat(a).T if n in transposed else flat(a)
    from_kernel = lambda n, a, shape: (a.T if n in transposed else a).reshape(shape)
    params = {
        "meta_tokens": (meta_tokens, m_meta_tokens, v_meta_tokens),
        "norm_g": (norm_g, m_norm_g, v_norm_g),
        "w_in": (w_in, m_w_in, v_w_in),
        "q_norm_g": (q_norm_g, m_q_norm_g, v_q_norm_g),
        "w_q_up": (w_q_up, m_w_q_up, v_w_q_up),
        "kv_norm_g": (kv_norm_g, m_kv_norm_g, v_kv_norm_g),
        "w_kv_up": (w_kv_up, m_w_kv_up, v_w_kv_up),
        "conv_w": (conv_w, m_conv_w, v_conv_w),
        "attn_out_g": (attn_out_g, m_attn_out_g, v_attn_out_g),
        "conv_out_g": (conv_out_g, m_conv_out_g, v_conv_out_g),
        "w_out": (w_out, m_w_out, v_w_out),
        "final_norm_g": (final_norm_g, m_final_norm_g, v_final_norm_g),
    }
    grads, loss = _reduce_tail(r_in, r_out, r_small, d_meta, d_norm_g)
    updated = _adamw(grads, {n: tuple(to_kernel(n, a) for a in t) for n, t in params.items()})
    outs = [[from_kernel(n, updated[n][i], params[n][0].shape) for n, _ in PARAM_SHAPES] for i in range(4)]
    return (loss[0, 0], grad_x, *outs[0], *outs[1], *outs[2], *outs[3])
```

```python
import functools

import jax
import jax.numpy as jnp
from jax import lax
from jax.experimental import pallas as pl
from jax.experimental.pallas import tpu as pltpu

F32 = jnp.float32
BF16 = jnp.bfloat16

N_META = 16
D_MODEL = 1024
N_HEADS = 4
D_NOPE = 128
D_ROPE = 64
D_V = 128
Q_RANK = 256
KV_RANK = 128
CONV_WIDTH = 512
CONV_GROUP = 64
ROPE_THETA = 10000.0
ATTN_SCALE = (D_NOPE + D_ROPE) ** -0.5
Q_SCALE = ATTN_SCALE * 1.4426950408889634
EPS = 1e-6
NEG_INF = -1e30

ADAM_LR = 0.001
ADAM_B1 = 0.9
ADAM_B2 = 0.999
ADAM_EPS = 1e-08
ADAM_WD = 0.01
ADAM_STEP = 10

LANES = 128
PAD_FRONT = LANES - N_META
K_TILE = 256
Q_TILE = 512
N_DEV = 8
VMEM_LIMIT = 56 * 1024 * 1024

IN_PAD = 3072
GRP_A = 512
N_A = Q_RANK + KV_RANK + D_ROPE
IN_PROJ = 3008
SHARD_IN = IN_PROJ // N_DEV
SHARD_IN_PAD = 384
SHARD_Q = 96
SHARD_KV = 128
SHARD_OUT = 128
SHARD_CONV = 64
SHARD_META = 128
Q_COLS = N_HEADS * (D_NOPE + D_ROPE)
KV_COLS = N_HEADS * (D_NOPE + D_V)

ROW_Q, ROW_KV, ROW_META, ROW_CONV = 0, 256, 384, 400
ROW_REPL = 408
ROW_NORM, ROW_FINAL, ROW_GQ, ROW_GKV, ROW_ATTN, ROW_CONVG, ROW_LOSS = 408, 416, 424, 426, 427, 431, 435
SMALL_ROWS = 440

PARAM_SHAPES = (
    ("meta_tokens", (N_META, SHARD_META)), ("norm_g", (1, D_MODEL)), ("w_in", (SHARD_IN, D_MODEL)),
    ("q_norm_g", (1, Q_RANK)), ("w_q_up", (Q_RANK, SHARD_Q)), ("kv_norm_g", (1, KV_RANK)),
    ("w_kv_up", (KV_RANK, SHARD_KV)), ("conv_w", (3, SHARD_CONV)), ("attn_out_g", (1, CONV_WIDTH)),
    ("conv_out_g", (1, CONV_WIDTH)), ("w_out", (SHARD_OUT, D_MODEL)), ("final_norm_g", (1, D_MODEL)),
)


def _in_pieces(k):
    lo, hi = SHARD_IN * k, SHARD_IN * (k + 1)
    out = []
    if lo < N_A:
        out.append((0, min(hi, N_A) - lo, lo))
    if hi > N_A:
        s = max(lo, N_A)
        out.append((s - lo, hi - lo, s + GRP_A - N_A))
    return out


def _q_pieces(k):
    lo, hi = SHARD_Q * k, SHARD_Q * (k + 1)
    out = []
    for h in range(N_HEADS):
        base = (D_NOPE + D_ROPE) * h
        s, e = max(lo, base), min(hi, base + D_NOPE)
        if s < e:
            out.append((s - lo, e - lo, D_NOPE * h + s - base))
        s, e = max(lo, base + D_NOPE), min(hi, base + D_NOPE + D_ROPE)
        if s < e:
            out.append((s - lo, e - lo, N_HEADS * D_NOPE + D_ROPE * h + s - base - D_NOPE))
    return out


def _kv_dst(k):
    return D_NOPE * (k // 2) + (N_HEADS * D_NOPE if k % 2 else 0)


def _params(*sem):
    return pltpu.CompilerParams(dimension_semantics=sem, vmem_limit_bytes=VMEM_LIMIT)


def _rms_stats(x):
    r = lax.rsqrt(jnp.mean(x * x, axis=-1, keepdims=True) + EPS)
    return x * r, r


def _rms_bwd(gdy, xhat, r):
    return r * (gdy - xhat * jnp.mean(gdy * xhat, axis=-1, keepdims=True))


def _sigmoid(z):
    return 1.0 / (1.0 + jnp.exp(-z))


def _group_mean(x):
    i0 = lax.broadcasted_iota(jnp.int32, (LANES, LANES), 0) // CONV_GROUP
    i1 = lax.broadcasted_iota(jnp.int32, (LANES, LANES), 1) // CONV_GROUP
    m = jnp.where(i0 == i1, 1.0 / CONV_GROUP, 0.0).astype(BF16)
    hi = x.astype(BF16)
    lo = (x - hi.astype(F32)).astype(BF16)
    return jnp.dot(hi, m, preferred_element_type=F32) + jnp.dot(lo, m, preferred_element_type=F32)


_NT = (((1,), (1,)), ((), ()))
_TN = (((0,), (0,)), ((), ()))


def _dot(a, b, dims=None):
    if dims is None:
        return jnp.dot(a, b, preferred_element_type=F32)
    return lax.dot_general(a, b, dims, preferred_element_type=F32)


def _device_position():
    x, y, c = lax.axis_index("x"), lax.axis_index("y"), lax.axis_index("c")
    return x, y, c, 4 * x + 2 * y + c


def _gather_plan(srcs, slots, send_sems, recv_sems, local_sems):
    x, y, c, _ = _device_position()
    me, sibling = (x, y, c), (x, y, 1 - c)
    flip = lambda v, on: v + on - 2 * v * on
    near = (flip(x, 1 - c), flip(y, c))
    far = (flip(x, c), flip(y, 1 - c))
    diag = (1 - x, 1 - y)
    n = len(srcs)

    def slot(a, px, py, pc):
        return slots[a].at[4 * px + 2 * py + pc]

    def copy(a, k, block, to, own=False):
        return pltpu.make_async_remote_copy(
            src_ref=srcs[a] if own else slot(a, *block),
            dst_ref=slot(a, *block),
            send_sem=send_sems.at[7 * a + k],
            recv_sem=recv_sems.at[7 * a + k],
            device_id=to,
            device_id_type=pl.DeviceIdType.MESH,
        )

    def local(a):
        return pltpu.make_async_copy(srcs[a], slot(a, *me), local_sems.at[a])

    sent = [(me, sibling), (me, (*near, c)), (me, (*far, c)), ((*near, c), (*far, c)),
            ((*near, c), sibling), ((*far, c), sibling), ((*diag, c), sibling)]
    landed = [sibling, (*near, c), (*far, c), (*diag, c), (*far, 1 - c), (*near, 1 - c), (*diag, 1 - c)]

    def send(a, k):
        return copy(a, k, *sent[k], own=k < 3)

    def arrival(a, k):
        return copy(a, k, landed[k], me)

    def start():
        for a in range(n):
            local(a).start()
            for k in range(3):
                send(a, k).start()

    def relay():
        for a in range(n):
            arrival(a, 1).wait_recv()
            send(a, 3).start()
            send(a, 4).start()

    def forward():
        for k in (2, 3):
            for a in range(n):
                arrival(a, k).wait_recv()
                send(a, k + 3).start()

    def finish():
        for a in range(n):
            for k in (0, 4, 5, 6):
                arrival(a, k).wait_recv()
        for a in range(n):
            for k in range(7):
                send(a, k).wait_send()
            local(a).wait()

    return start, relay, forward, finish


def _adam_update(g, w, m, v):
    m_new = ADAM_B1 * m + (1.0 - ADAM_B1) * g
    v_new = ADAM_B2 * v + (1.0 - ADAM_B2) * (g * g)
    m_hat = m_new / (1.0 - ADAM_B1 ** ADAM_STEP)
    v_hat = v_new / (1.0 - ADAM_B2 ** ADAM_STEP)
    return -ADAM_LR * (m_hat / (jnp.sqrt(v_hat) + ADAM_EPS) + ADAM_WD * w), m_new, v_new


def _adamw(grads, params):
    names = [n for n, _ in PARAM_SHAPES]
    n_p = len(names)

    def body(*refs):
        for i in range(n_p):
            g = refs[i][...]
            w, m, v = (refs[n_p + 3 * i + j][...] for j in range(3))
            delta, m_new, v_new = _adam_update(g, w, m, v)
            for j, val in enumerate((g, delta, m_new, v_new)):
                refs[4 * n_p + 4 * i + j][...] = val

    vm = pl.BlockSpec(memory_space=pltpu.VMEM)
    out_shape = []
    for _, shape in PARAM_SHAPES:
        out_shape += [jax.ShapeDtypeStruct(shape, F32)] * 4
    outs = pl.pallas_call(
        body,
        name="adamw",
        out_shape=tuple(out_shape),
        in_specs=[vm] * (4 * n_p),
        out_specs=(vm,) * (4 * n_p),
        compiler_params=pltpu.CompilerParams(vmem_limit_bytes=VMEM_LIMIT),
    )(*[grads[n] for n in names], *[a for n in names for a in params[n]])
    return {n: outs[4 * i:4 * i + 4] for i, n in enumerate(names)}


N_CHIPS = 4


def _reduce_plan(pays, owns, r1s, sums, r2s, send1, recv1, send2, recv2, local_sems):
    x, y, c, _ = _device_position()
    sibling = (x, y, 1 - c)
    chips = [((1 - x if rj & 2 else x), (1 - y if rj & 1 else y)) for rj in range(N_CHIPS)]
    n = len(pays)

    def slot_of(rj, core):
        return 4 * chips[rj][0] + 2 * chips[rj][1] + core

    def to_sibling(a, rj):
        return pltpu.make_async_remote_copy(
            src_ref=pays[a].at[slot_of(rj, 1 - c)], dst_ref=r1s[a].at[rj],
            send_sem=send1.at[N_CHIPS * a + rj], recv_sem=recv1.at[N_CHIPS * a + rj],
            device_id=sibling, device_id_type=pl.DeviceIdType.MESH)

    def load_own(a, rj):
        return pltpu.make_async_copy(pays[a].at[slot_of(rj, c)], owns[a].at[rj], local_sems.at[2 * N_CHIPS * a + rj])

    def to_chip(a, rj):
        return pltpu.make_async_remote_copy(
            src_ref=sums[a].at[rj], dst_ref=r2s[a].at[rj],
            send_sem=send2.at[N_CHIPS * a + rj], recv_sem=recv2.at[N_CHIPS * a + rj],
            device_id=(*chips[rj], c), device_id_type=pl.DeviceIdType.MESH)

    def keep(a):
        return pltpu.make_async_copy(sums[a].at[0], r2s[a].at[0], local_sems.at[2 * N_CHIPS * a + N_CHIPS])

    def start():
        for a in range(n):
            for rj in range(N_CHIPS):
                to_sibling(a, rj).start()
                if owns[a] is not None:
                    load_own(a, rj).start()

    def combine():
        for a in range(n):
            for rj in range(N_CHIPS):
                to_sibling(a, rj).wait_recv()
                if owns[a] is not None:
                    load_own(a, rj).wait()
                    mine = owns[a][rj]
                else:
                    mine = pays[a][slot_of(rj, c)]
                sums[a][rj] = (mine.astype(F32) + r1s[a][rj].astype(F32)).astype(sums[a].dtype)
            keep(a).start()
            for rj in range(1, N_CHIPS):
                to_chip(a, rj).start()

    def finish():
        for a in range(n):
            for rj in range(1, N_CHIPS):
                to_chip(a, rj).wait_recv()
            for rj in range(N_CHIPS):
                to_sibling(a, rj).wait_send()
            for rj in range(1, N_CHIPS):
                to_chip(a, rj).wait_send()
            keep(a).wait()

    return start, combine, finish


def _reduce_scratch(shapes_dtypes, own_flags):
    out = []
    for (shape, dtype), own in zip(shapes_dtypes, own_flags):
        if own:
            out.append(pltpu.VMEM((N_CHIPS,) + shape, dtype))
        out += [pltpu.VMEM((N_CHIPS,) + shape, dtype), pltpu.VMEM((N_CHIPS,) + shape, dtype)]
    n = len(shapes_dtypes)
    out += [pltpu.SemaphoreType.DMA((N_CHIPS * n,))] * 4 + [pltpu.SemaphoreType.DMA((2 * N_CHIPS * n,))]
    return out


def _pack_small(ssmall, dwq, dwkv, dconv, dfinal, dgq, dgkv, dattn, dconvg, loss_part):
    ssmall[...] = jnp.zeros_like(ssmall)
    rep = ssmall.at[0]
    for i in range(D_MODEL // LANES):
        rep[ROW_FINAL + i:ROW_FINAL + i + 1, :] = dfinal[:, LANES * i:LANES * (i + 1)]
    for i in range(Q_RANK // LANES):
        rep[ROW_GQ + i:ROW_GQ + i + 1, :] = dgq[:, LANES * i:LANES * (i + 1)]
    rep[ROW_GKV:ROW_GKV + 1, :] = dgkv[...]
    for i in range(CONV_WIDTH // LANES):
        rep[ROW_ATTN + i:ROW_ATTN + i + 1, :] = dattn[:, LANES * i:LANES * (i + 1)]
        rep[ROW_CONVG + i:ROW_CONVG + i + 1, :] = dconvg[:, LANES * i:LANES * (i + 1)]
    rep[ROW_LOSS:ROW_LOSS + 1, :] = loss_part[...]
    for k in range(N_DEV):
        if k:
            ssmall[k, ROW_REPL:, :] = ssmall[0, ROW_REPL:, :]
        for s, e, d in _q_pieces(k):
            ssmall[k, ROW_Q:ROW_Q + Q_RANK, s:e] = dwq[:, d:d + e - s]
        ssmall[k, ROW_KV:ROW_KV + KV_RANK, :] = dwkv[:, _kv_dst(k):_kv_dst(k) + SHARD_KV]
        ssmall[k, ROW_CONV:ROW_CONV + 3, 0:SHARD_CONV] = dconv[0:3, SHARD_CONV * k:SHARD_CONV * (k + 1)]


TOKEN_TILES = 4
TAIL_ROWS = N_META + D_MODEL // LANES


def _reduce_tail(r_in, r_out, r_small, d_meta, d_norm):
    n_p = len(PARAM_SHAPES)
    names = [n for n, _ in PARAM_SHAPES]

    def body(*refs):
        rin, rout, rsmall, dmeta, dnorm = refs[:5]
        g_out = {n: refs[5 + i] for i, n in enumerate(names)}
        loss_out = refs[5 + n_p]
        stail, rtail, gsum, gtail, send_sems, recv_sems = refs[6 + n_p:]
        x, y, c, me = _device_position()
        my_chip = 2 * x + y

        for k in range(N_DEV):
            stail[k, 0:N_META, :] = dmeta[:, SHARD_META * k:SHARD_META * (k + 1)]
            for i in range(D_MODEL // LANES):
                stail[k, N_META + i:N_META + i + 1, :] = dnorm[:, LANES * i:LANES * (i + 1)]
        copies = []
        for r in range(1, N_DEV):
            peer = (1 - x if r & 4 else x, 1 - y if r & 2 else y, 1 - c if r & 1 else c)
            copies.append(pltpu.make_async_remote_copy(
                src_ref=stail.at[4 * peer[0] + 2 * peer[1] + peer[2]],
                dst_ref=rtail.at[r],
                send_sem=send_sems.at[r - 1],
                recv_sem=recv_sems.at[r - 1],
                device_id=peer,
                device_id_type=pl.DeviceIdType.MESH,
            ))
        for cp in copies:
            cp.start()
        rtail[0] = stail[me]

        g = rin[my_chip].astype(F32)
        for ch in range(1, N_CHIPS):
            g = g + rin[ch ^ my_chip].astype(F32)
        g_out["w_in"][...] = g[:SHARD_IN, :]

        g = rout[my_chip].astype(F32)
        gs = rsmall[my_chip]
        for ch in range(1, N_CHIPS):
            g = g + rout[ch ^ my_chip].astype(F32)
            gs = gs + rsmall[ch ^ my_chip]
        g_out["w_out"][...] = g
        gsum[...] = gs
        g_out["w_q_up"][...] = gsum[ROW_Q:ROW_Q + Q_RANK, 0:SHARD_Q]
        g_out["w_kv_up"][...] = gsum[ROW_KV:ROW_KV + KV_RANK, :]
        g_out["conv_w"][...] = gsum[ROW_CONV:ROW_CONV + 3, 0:SHARD_CONV]
        for name, row, width in (("final_norm_g", ROW_FINAL, D_MODEL), ("q_norm_g", ROW_GQ, Q_RANK),
                                 ("kv_norm_g", ROW_GKV, KV_RANK), ("attn_out_g", ROW_ATTN, CONV_WIDTH),
                                 ("conv_out_g", ROW_CONVG, CONV_WIDTH)):
            for i in range(width // LANES):
                g_out[name][:, LANES * i:LANES * (i + 1)] = gsum[row + i:row + i + 1, :]
        loss_out[...] = gsum[ROW_LOSS:ROW_LOSS + 1, :]

        for cp in copies:
            cp.wait_recv()
        gt = rtail[me]
        for d in range(1, N_DEV):
            gt = gt + rtail[d ^ me]
        gtail[...] = gt
        g_out["meta_tokens"][...] = gtail[0:N_META, :]
        for i in range(D_MODEL // LANES):
            g_out["norm_g"][:, LANES * i:LANES * (i + 1)] = gtail[N_META + i:N_META + i + 1, :]
        for cp in copies:
            cp.wait_send()

    vm = pl.BlockSpec(memory_space=pltpu.VMEM)
    out_shape = [jax.ShapeDtypeStruct(shape, F32) for _, shape in PARAM_SHAPES]
    out_shape.append(jax.ShapeDtypeStruct((1, LANES), F32))
    outs = pl.pallas_call(
        body,
        name="reduce_tail",
        out_shape=tuple(out_shape),
        in_specs=[vm] * 5,
        out_specs=(vm,) * len(out_shape),
        scratch_shapes=[
            pltpu.VMEM((N_DEV, TAIL_ROWS, LANES), F32),
            pltpu.VMEM((N_DEV, TAIL_ROWS, LANES), F32),
            pltpu.VMEM((SMALL_ROWS, LANES), F32),
            pltpu.VMEM((TAIL_ROWS, LANES), F32),
            pltpu.SemaphoreType.DMA((N_DEV - 1,)),
            pltpu.SemaphoreType.DMA((N_DEV - 1,)),
        ],
        compiler_params=pltpu.CompilerParams(vmem_limit_bytes=VMEM_LIMIT),
    )(r_in, r_out, r_small, d_meta, d_norm)
    return {n: outs[i] for i, n in enumerate(names)}, outs[-1]


def _prep_gather(x, meta, norm_g, w_in_t):
    nb_seq, s, d = x.shape
    nb = s // LANES + 1
    relay_step = nb_seq * (nb - 1) // 2
    forward_step = nb_seq * (nb - 1) - 1
    finish_step = nb_seq * (nb - 1)

    def body(x_ref, meta_ref, g_ref, win_ref, u_ref, w_in_p, meta_f,
             sbig, ssmall, gbig, gsmall, send_sems, recv_sems, local_sems):
        jj, b = pl.program_id(0), pl.program_id(1)
        t = jj * nb_seq + b

        def plan():
            return _gather_plan((sbig, ssmall), (gbig, gsmall), send_sems, recv_sems, local_sems)

        @pl.when(t == 0)
        def _():
            sbig[0:SHARD_IN, :] = win_ref[...].astype(BF16)
            sbig[SHARD_IN:, :] = jnp.zeros((SHARD_IN_PAD - SHARD_IN, d), BF16)
            ssmall[...] = meta_ref[...]
            plan()[0]()

        @pl.when(t == relay_step)
        def _():
            plan()[1]()

        @pl.when(t == forward_step)
        def _():
            plan()[2]()

        @pl.when(t == finish_step)
        def _():
            plan()[3]()
            w_in_p[N_A:GRP_A, :] = jnp.zeros((GRP_A - N_A, d), BF16)
            for k in range(N_DEV):
                for s0, e0, d0 in _in_pieces(k):
                    w_in_p[d0:d0 + e0 - s0, :] = gbig[k, s0:e0, :]
                meta_f[:, SHARD_META * k:SHARD_META * (k + 1)] = gsmall[k]

        def norm(h):
            hhat, _ = _rms_stats(h)
            return (hhat * g_ref[...]).astype(BF16)

        @pl.when(jj < nb - 1)
        def _():
            u_ref[...] = norm(x_ref[0])

        @pl.when(jj == nb - 1)
        def _():
            u_ref[0:PAD_FRONT, :] = jnp.zeros((PAD_FRONT, d), BF16)
            u_ref[PAD_FRONT:LANES, :] = norm(meta_f[...])

    whole = lambda shape: pl.BlockSpec(shape, lambda jj, b: (0,) * len(shape))
    return pl.pallas_call(
        body,
        name="prep_norm_gather",
        grid=(nb, nb_seq),
        in_specs=[
            pl.BlockSpec((1, LANES, d), lambda jj, b: (b, jnp.minimum(jj, nb - 2), 0)),
            whole(meta.shape), whole(norm_g.shape), whole(w_in_t.shape),
        ],
        out_specs=(pl.BlockSpec((LANES, d), lambda jj, b: (b * nb + (jj + 1) % nb, 0)),
                   whole((IN_PAD, d)), whole((N_META, d))),
        out_shape=(jax.ShapeDtypeStruct((nb_seq * nb * LANES, d), BF16),
                   jax.ShapeDtypeStruct((IN_PAD, d), BF16),
                   jax.ShapeDtypeStruct((N_META, d), F32)),
        scratch_shapes=[
            pltpu.VMEM((SHARD_IN_PAD, d), BF16),
            pltpu.VMEM((N_META, SHARD_META), F32),
            pltpu.VMEM((N_DEV, SHARD_IN_PAD, d), BF16),
            pltpu.VMEM((N_DEV, N_META, SHARD_META), F32),
            pltpu.SemaphoreType.DMA((14,)),
            pltpu.SemaphoreType.DMA((14,)),
            pltpu.SemaphoreType.DMA((2,)),
        ],
        compiler_params=_params("arbitrary", "arbitrary"),
    )(x, meta, norm_g, w_in_t)


def _in_proj_gather(u, w_in_p, w_q, w_kv, w_out, conv_w, bm, bn):
    m, k_dim = u.shape
    n = w_in_p.shape[0]
    steps = (m // bm) * (n // bn)
    relay_step, forward_step = steps // 3, 2 * steps // 3
    qkv_rows = Q_RANK + KV_RANK

    def body(a_ref, b_ref, wq_ref, wkv_ref, wout_ref, conv_ref, o_ref, w_q_p, w_kv_p, w_out_f, conv_f,
             sqkv, sout, sconv, gqkv, gout, gconv, send_sems, recv_sems, local_sems):
        t = pl.program_id(0) * (n // bn) + pl.program_id(1)

        def plan():
            return _gather_plan((sqkv, sout, sconv), (gqkv, gout, gconv), send_sems, recv_sems, local_sems)

        @pl.when(t == 0)
        def _():
            sqkv[...] = jnp.zeros_like(sqkv)
            sqkv[0:Q_RANK, 0:SHARD_Q] = wq_ref[...].astype(BF16)
            sqkv[Q_RANK:, :] = wkv_ref[...].astype(BF16)
            sout[...] = wout_ref[...].astype(BF16)
            sconv[...] = jnp.zeros_like(sconv)
            sconv[0:3, 0:SHARD_CONV] = conv_ref[...]
            plan()[0]()

        @pl.when(t == relay_step)
        def _():
            plan()[1]()

        @pl.when(t == forward_step)
        def _():
            plan()[2]()

        o_ref[...] = _dot(a_ref[...], b_ref[...], _NT).astype(o_ref.dtype)

        @pl.when(t == steps - 1)
        def _():
            plan()[3]()
            conv_f[...] = jnp.zeros_like(conv_f)
            for k in range(N_DEV):
                for s0, e0, d0 in _q_pieces(k):
                    w_q_p[:, d0:d0 + e0 - s0] = gqkv[k, 0:Q_RANK, s0:e0]
                w_kv_p[:, _kv_dst(k):_kv_dst(k) + SHARD_KV] = gqkv[k, Q_RANK:, :]
                w_out_f[SHARD_OUT * k:SHARD_OUT * (k + 1), :] = gout[k]
                conv_f[0:3, SHARD_CONV * k:SHARD_CONV * (k + 1)] = gconv[k, 0:3, 0:SHARD_CONV]

    whole = lambda shape: pl.BlockSpec(shape, lambda i, j: (0,) * len(shape))
    return pl.pallas_call(
        body,
        name="in_proj_gather",
        grid=(m // bm, n // bn),
        in_specs=[pl.BlockSpec((bm, k_dim), lambda i, j: (i, 0)), pl.BlockSpec((bn, k_dim), lambda i, j: (j, 0)),
                  whole(w_q.shape), whole(w_kv.shape), whole(w_out.shape), whole(conv_w.shape)],
        out_specs=(pl.BlockSpec((bm, bn), lambda i, j: (i, j)),
                   whole((Q_RANK, Q_COLS)), whole((KV_RANK, KV_COLS)), whole((D_MODEL, D_MODEL)),
                   whole((8, CONV_WIDTH))),
        out_shape=(jax.ShapeDtypeStruct((m, n), BF16),
                   jax.ShapeDtypeStruct((Q_RANK, Q_COLS), BF16),
                   jax.ShapeDtypeStruct((KV_RANK, KV_COLS), BF16),
                   jax.ShapeDtypeStruct((D_MODEL, D_MODEL), BF16),
                   jax.ShapeDtypeStruct((8, CONV_WIDTH), F32)),
        scratch_shapes=[
            pltpu.VMEM((qkv_rows, LANES), BF16),
            pltpu.VMEM((SHARD_OUT, D_MODEL), BF16),
            pltpu.VMEM((8, LANES), F32),
            pltpu.VMEM((N_DEV, qkv_rows, LANES), BF16),
            pltpu.VMEM((N_DEV, SHARD_OUT, D_MODEL), BF16),
            pltpu.VMEM((N_DEV, 8, LANES), F32),
            pltpu.SemaphoreType.DMA((21,)),
            pltpu.SemaphoreType.DMA((21,)),
            pltpu.SemaphoreType.DMA((3,)),
        ],
        compiler_params=_params("arbitrary", "arbitrary"),
    )(u, w_in_p, w_q, w_kv, w_out, conv_w)


def _rope_tables(tp):
    half = D_ROPE // 2
    lane = jnp.arange(LANES)
    inv_freq = 1.0 / (ROPE_THETA ** ((lane % half).astype(F32) / half))
    pos = (jnp.arange(tp) - PAD_FRONT).astype(F32)
    ang = pos[:, None] * inv_freq[None, :]
    cos, sin = jnp.cos(ang), jnp.sin(ang)
    first = (lane % D_ROPE) < half
    return cos, jnp.where(first, -sin, 0.0), jnp.where(first, 0.0, sin)


def _rope(t, cos, sa, sb):
    return t * cos + pltpu.roll(t, LANES - D_ROPE // 2, 1) * sa + pltpu.roll(t, D_ROPE // 2, 1) * sb


def _rope_t(t, cos, sa, sb):
    return t * cos + pltpu.roll(t * sa, D_ROPE // 2, 1) + pltpu.roll(t * sb, LANES - D_ROPE // 2, 1)


def _qkv_fwd(p, wq, wkv, gq, gkv, tables, nb_seq, tp):
    ht = tp // 2

    def body(pa_ref, wq_ref, wkv_ref, gq_ref, gkv_ref, cos_ref, sa_ref, sb_ref, q_ref, k_ref, v_ref):
        pa = pa_ref[...].astype(F32)
        cq_hat, _ = _rms_stats(pa[:, :Q_RANK])
        ckv_hat, _ = _rms_stats(pa[:, Q_RANK:Q_RANK + KV_RANK])
        q = _dot((cq_hat * gq_ref[...]).astype(BF16), wq_ref[...]) * Q_SCALE
        kv = _dot((ckv_hat * gkv_ref[...]).astype(BF16), wkv_ref[...])
        tabs = (cos_ref[...], sa_ref[...], sb_ref[...])
        lane = lax.broadcasted_iota(jnp.int32, (ht, LANES), 1)
        low = lane < D_ROPE
        mark = lane == D_ROPE
        row = (pl.program_id(0) % 2) * ht + lax.broadcasted_iota(jnp.int32, (ht, LANES), 0)
        k_pe = jnp.where(mark & (row < PAD_FRONT), NEG_INF, _rope(pa[:, Q_RANK + KV_RANK:], *tabs))
        one = jnp.where(mark & (row >= PAD_FRONT), 1.0, 0.0)
        pairs = [_rope(q[:, N_HEADS * D_NOPE + LANES * i:N_HEADS * D_NOPE + LANES * (i + 1)], *tabs) for i in range(2)]
        for h in range(N_HEADS):
            pair = pairs[h // 2]
            if h % 2:
                pair = pltpu.roll(pair, D_ROPE, 1)
            pe = jnp.where(low, pair, one)
            q_ref[0, h] = jnp.concatenate([q[:, D_NOPE * h:D_NOPE * (h + 1)], pe], axis=1).astype(BF16)
            k_ref[0, h] = jnp.concatenate([kv[:, D_NOPE * h:D_NOPE * (h + 1)], k_pe], axis=1).astype(BF16)
            v_ref[0, h] = kv[:, N_HEADS * D_NOPE + D_V * h:N_HEADS * D_NOPE + D_V * (h + 1)].astype(BF16)

    full = lambda a: pl.BlockSpec(a.shape, lambda i: (0,) * a.ndim)
    tab = pl.BlockSpec((ht, LANES), lambda i: (i % 2, 0))
    qk = pl.BlockSpec((1, N_HEADS, ht, 2 * LANES), lambda i: (i // 2, 0, i % 2, 0))
    return pl.pallas_call(
        body,
        name="qkv_fwd",
        grid=(2 * nb_seq,),
        in_specs=[pl.BlockSpec((ht, GRP_A), lambda i: (i, 0)), full(wq), full(wkv), full(gq), full(gkv), tab, tab, tab],
        out_specs=(qk, qk, pl.BlockSpec((1, N_HEADS, ht, D_V), lambda i: (i // 2, 0, i % 2, 0))),
        out_shape=(
            jax.ShapeDtypeStruct((nb_seq, N_HEADS, tp, 2 * LANES), BF16),
            jax.ShapeDtypeStruct((nb_seq, N_HEADS, tp, 2 * LANES), BF16),
            jax.ShapeDtypeStruct((nb_seq, N_HEADS, tp, D_V), BF16),
        ),
        compiler_params=_params("parallel"),
    )(p, wq, wkv, gq, gkv, *tables)


def _attn_fwd(q, k, v, p, g_attn):
    nb_seq, _, tp, _ = q.shape

    def body(q_ref, k_ref, v_ref, z_ref, g_ref, y_ref, o_ref, lse_ref):
        g = g_ref[...]
        for r0 in range(0, tp, Q_TILE):
            nq = min(Q_TILE, tp - r0)
            kend = r0 + nq
            qq = q_ref[0, 0, r0:kend, :]
            sd = _dot(qq, k_ref[0, 0, r0:kend, :], _NT)
            causal = (lax.broadcasted_iota(jnp.int32, (nq, nq), 1) <= lax.broadcasted_iota(jnp.int32, (nq, nq), 0))
            sd = jnp.where(causal, sd, NEG_INF)
            m = jnp.max(sd, axis=-1, keepdims=True)
            if r0:
                so = _dot(qq, k_ref[0, 0, 0:r0, :], _NT)
                m = jnp.maximum(m, jnp.max(so, axis=-1, keepdims=True))
            ed = jnp.exp2(sd - m)
            l = jnp.sum(ed, axis=-1, keepdims=True)
            o = _dot(ed.astype(BF16), v_ref[0, 0, r0:kend, :])
            if r0:
                eo = jnp.exp2(so - m)
                l = l + jnp.sum(eo, axis=-1, keepdims=True)
                o = o + _dot(eo.astype(BF16), v_ref[0, 0, 0:r0, :])
            o = o * (1.0 / l)
            o_ref[0, 0, r0:kend, :] = o
            lse_ref[0, 0, r0:kend, :] = jnp.broadcast_to(m + jnp.log2(l), (nq, LANES))
            ohat, _ = _rms_stats(o)
            z = z_ref[r0:kend, :].astype(F32)
            y_ref[r0:kend, :] = (ohat * g * (z * _sigmoid(z))).astype(BF16)

    qk = pl.BlockSpec((1, 1, tp, 2 * LANES), lambda b, h: (b, h, 0, 0))
    hv = pl.BlockSpec((1, 1, tp, D_V), lambda b, h: (b, h, 0, 0))
    return pl.pallas_call(
        body,
        name="attn_fwd",
        grid=(nb_seq, N_HEADS),
        in_specs=[qk, qk, hv,
                  pl.BlockSpec((tp, LANES), lambda b, h: (b, GRP_A // LANES + h)),
                  pl.BlockSpec((1, LANES), lambda b, h: (0, h))],
        out_specs=(pl.BlockSpec((tp, LANES), lambda b, h: (b, h)), hv, hv),
        out_shape=(
            jax.ShapeDtypeStruct((nb_seq * tp, N_HEADS * D_V), BF16),
            jax.ShapeDtypeStruct((nb_seq, N_HEADS, tp, D_V), F32),
            jax.ShapeDtypeStruct((nb_seq, N_HEADS, tp, LANES), F32),
        ),
        compiler_params=_params("parallel", "parallel"),
    )(q, k, v, p, g_attn)


_CONV_COL0 = (GRP_A + N_HEADS * D_V) // LANES


def _conv_specs(tp, order):
    cols = CONV_WIDTH // LANES
    return [pl.BlockSpec((tp, LANES), functools.partial(
        lambda a, b, off: order(a, b, off), off=_CONV_COL0 + i * cols)) for i in range(4)]


def _conv_fwd(p, conv_w, g_conv, nb_seq, tp):
    def body(b_ref, c_ref, h_ref, z_ref, w_ref, g_ref, y_ref):
        cc = c_ref[...].astype(F32) * h_ref[...].astype(F32)
        row = lax.broadcasted_iota(jnp.int32, (tp, LANES), 0)
        s1 = jnp.where(row >= 1, pltpu.roll(cc, 1, 0), 0.0)
        s2 = jnp.where(row >= 2, pltpu.roll(cc, 2, 0), 0.0)
        yc = b_ref[...].astype(F32) * (w_ref[0:1, :] * s2 + w_ref[1:2, :] * s1 + w_ref[2:3, :] * cc)
        r = lax.rsqrt(_group_mean(yc * yc) + EPS)
        z = z_ref[...].astype(F32)
        y_ref[...] = (yc * r * g_ref[...] * (z * _sigmoid(z))).astype(BF16)

    return pl.pallas_call(
        body,
        name="conv_fwd",
        grid=(nb_seq, CONV_WIDTH // LANES),
        in_specs=_conv_specs(tp, lambda b, t, off: (b, off + t)) + [
            pl.BlockSpec((8, LANES), lambda b, t: (0, t)),
            pl.BlockSpec((1, LANES), lambda b, t: (0, t))],
        out_specs=pl.BlockSpec((tp, LANES), lambda b, t: (b, t)),
        out_shape=jax.ShapeDtypeStruct((nb_seq * tp, CONV_WIDTH), BF16),
        compiler_params=_params("parallel", "parallel"),
    )(p, p, p, p, conv_w, g_conv)


def _token_copy(hbm, b, k, ts, buf, sem, to_hbm=False):
    lo, hi = max(k * ts - LANES, 0), (k + 1) * ts - LANES
    off = lo - (k * ts - LANES)
    src, dst = hbm.at[b, pl.ds(lo, hi - lo)], buf.at[pl.ds(off, hi - lo)]
    if to_hbm:
        src, dst = dst, src
    return pltpu.make_async_copy(src, dst, sem)


def _for_tile(k, nt, fn):
    for kk in range(nt):
        @pl.when(k == kk)
        def _(kk=kk):
            fn(kk)


def _out_proj_loss(ya, yc, w_out, x, target, g_final, nt):
    nb_seq, s, d = x.shape
    r, ka = ya.shape
    ts = (s + LANES) // nt
    steps = nb_seq * nt

    def body(a_ref, c_ref, w_ref, x_hbm, t_hbm, g_ref, dh_ref, dhb_ref, dg_ref, loss_ref,
             xbuf, tbuf, acc_ref, sems):
        i = pl.program_id(0)
        b, k = i // nt, i % nt

        @pl.when(i == 0)
        def _():
            acc_ref[...] = jnp.zeros_like(acc_ref)
            dg_ref[...] = jnp.zeros_like(dg_ref)

        slot = i % 2

        def fetch(seq, kk, sl):
            return [_token_copy(x_hbm, seq, kk, ts, xbuf.at[sl], sems.at[sl, 0]),
                    _token_copy(t_hbm, seq, kk, ts, tbuf.at[sl], sems.at[sl, 1])]

        def start(seq, sl, kk):
            if kk == 0:
                xbuf[sl, 0:LANES, :] = jnp.zeros((LANES, d), F32)
                tbuf[sl, 0:LANES, :] = jnp.zeros((LANES, d), F32)
            for cp in fetch(seq, kk, sl):
                cp.start()

        @pl.when(i == 0)
        def _():
            start(0, 0, 0)

        @pl.when(i + 1 < steps)
        def _():
            _for_tile((i + 1) % nt, nt, functools.partial(start, (i + 1) // nt, 1 - slot))

        mix = _dot(a_ref[...], w_ref[0:ka, :]) + _dot(c_ref[...], w_ref[ka:, :])
        _for_tile(k, nt, lambda kk: [cp.wait() for cp in fetch(b, kk, slot)])

        real = (lax.broadcasted_iota(jnp.int32, (ts, d), 0) >= LANES) | (k > 0)
        g = g_ref[...]
        hhat, rstd = _rms_stats(xbuf[slot] + mix)
        e = jnp.where(real, hhat * g - tbuf[slot], 0.0)
        acc_ref[...] += jnp.sum(e * e, axis=0, keepdims=True)
        dy = e * (1.0 / d)
        dg_ref[...] += jnp.sum(dy * hhat, axis=0, keepdims=True)
        dh = _rms_bwd(g * dy, hhat, rstd)
        dh_ref[...] = dh
        dhb_ref[...] = dh.astype(BF16)

        @pl.when(i == steps - 1)
        def _():
            total = jnp.sum(acc_ref[...], axis=1, keepdims=True)
            loss_ref[...] = jnp.broadcast_to((0.5 / d) * total, loss_ref.shape)

    hbm = pl.BlockSpec(memory_space=pl.ANY)
    row = pl.BlockSpec((ts, d), lambda i: (i, 0))
    vec = pl.BlockSpec((1, d), lambda i: (0, 0))
    return pl.pallas_call(
        body,
        name="out_proj_loss",
        grid=(steps,),
        in_specs=[pl.BlockSpec((ts, ka), lambda i: (i, 0)), pl.BlockSpec((ts, yc.shape[1]), lambda i: (i, 0)),
                  pl.BlockSpec(w_out.shape, lambda i: (0, 0)), hbm, hbm, vec],
        out_specs=(row, row, vec, pl.BlockSpec((1, LANES), lambda i: (0, 0))),
        out_shape=(
            jax.ShapeDtypeStruct((r, d), F32),
            jax.ShapeDtypeStruct((r, d), BF16),
            jax.ShapeDtypeStruct((1, d), F32),
            jax.ShapeDtypeStruct((1, LANES), F32),
        ),
        scratch_shapes=[pltpu.VMEM((2, ts, d), F32), pltpu.VMEM((2, ts, d), F32), pltpu.VMEM((1, d), F32),
                        pltpu.SemaphoreType.DMA((2, 2))],
        compiler_params=_params("arbitrary"),
    )(ya, yc, w_out, x, target, g_final)


def _out_proj_bwd(dhb, w_out, ya, yc, bm):
    r, d = dhb.shape
    ka = ya.shape[1]
    n_mix = w_out.shape[0]
    last = r // bm - 1

    def body(dh_ref, w_ref, a_ref, c_ref, dcat_ref, dw_ref, acc_ref):
        @pl.when(pl.program_id(0) == 0)
        def _():
            acc_ref[...] = jnp.zeros_like(acc_ref)

        dh = dh_ref[...]
        dcat_ref[...] = _dot(dh, w_ref[...], _NT).astype(BF16)
        acc_ref[0:ka, :] += _dot(a_ref[...], dh, _TN)
        acc_ref[ka:, :] += _dot(c_ref[...], dh, _TN)

        @pl.when(pl.program_id(0) == last)
        def _():
            dw_ref[...] = acc_ref[...].astype(BF16)

    return pl.pallas_call(
        body,
        name="out_proj_bwd",
        grid=(r // bm,),
        in_specs=[pl.BlockSpec((bm, d), lambda i: (i, 0)), pl.BlockSpec(w_out.shape, lambda i: (0, 0)),
                  pl.BlockSpec((bm, ka), lambda i: (i, 0)), pl.BlockSpec((bm, yc.shape[1]), lambda i: (i, 0))],
        out_specs=(pl.BlockSpec((bm, n_mix), lambda i: (i, 0)),
                   pl.BlockSpec((n_mix, d), lambda i: (0, 0))),
        out_shape=(jax.ShapeDtypeStruct((r, n_mix), BF16),
                   jax.ShapeDtypeStruct((n_mix, d), BF16)),
        scratch_shapes=[pltpu.VMEM((n_mix, d), F32)],
        compiler_params=_params("arbitrary"),
    )(dhb, w_out, ya, yc)


def _attn_bwd(q, k, v, o, lse, dcat, p, g_attn):
    nb_seq, _, tp, _ = q.shape

    def body(q_ref, k_ref, v_ref, o_ref, lse_ref, dy_ref, z_ref, g_ref,
             dq_ref, dk_ref, dv_ref, dz_ref, dg_ref, dq_acc):
        @pl.when(pl.program_id(1) == 0)
        def _():
            dg_ref[...] = jnp.zeros_like(dg_ref)

        g = g_ref[...]
        z = z_ref[...].astype(F32)
        o = o_ref[0, 0]
        dy = dy_ref[...].astype(F32)
        sig = _sigmoid(z)
        ohat, r = _rms_stats(o)
        don = dy * (z * sig)
        dz_ref[...] = (dy * (ohat * g) * (sig * (1.0 + z * (1.0 - sig)))).astype(BF16)
        dg_ref[...] += jnp.sum(don * ohat, axis=0, keepdims=True)
        do = _rms_bwd(g * don, ohat, r)
        dvec = jnp.sum(do * o, axis=-1, keepdims=True)
        dob = do.astype(BF16)
        lse_col = lse_ref[0, 0, :, 0:1]
        dq_acc[...] = jnp.zeros_like(dq_acc)
        for k0 in range(0, tp, K_TILE):
            nk = min(K_TILE, tp - k0)
            nq = tp - k0
            qq = q_ref[0, 0, k0:, :]
            kk = k_ref[0, 0, k0:k0 + nk, :]
            causal = (lax.broadcasted_iota(jnp.int32, (nq, nk), 1) <= lax.broadcasted_iota(jnp.int32, (nq, nk), 0))
            pr = jnp.where(causal, jnp.exp2(_dot(qq, kk, _NT) - lse_col[k0:]), 0.0)
            dp = _dot(dob[k0:], v_ref[0, 0, k0:k0 + nk, :], _NT)
            ds = (pr * (dp - dvec[k0:])).astype(BF16)
            dv_ref[0, 0, k0:k0 + nk, :] = _dot(pr.astype(BF16), dob[k0:], _TN).astype(BF16)
            dk_ref[0, 0, k0:k0 + nk, :] = (_dot(ds, qq, _TN) * (ATTN_SCALE / Q_SCALE)).astype(BF16)
            dq_acc[k0:, :] += _dot(ds, kk)
        dq_ref[0, 0] = (dq_acc[...] * ATTN_SCALE).astype(BF16)

    qk = pl.BlockSpec((1, 1, tp, 2 * LANES), lambda h, b: (b, h, 0, 0))
    hv = pl.BlockSpec((1, 1, tp, D_V), lambda h, b: (b, h, 0, 0))
    col = pl.BlockSpec((tp, LANES), lambda h, b: (b, h))
    return pl.pallas_call(
        body,
        name="attn_bwd",
        grid=(N_HEADS, nb_seq),
        in_specs=[qk, qk, hv, hv, hv, col,
                  pl.BlockSpec((tp, LANES), lambda h, b: (b, GRP_A // LANES + h)),
                  pl.BlockSpec((1, LANES), lambda h, b: (0, h))],
        out_specs=(qk, qk, hv, col, pl.BlockSpec((1, LANES), lambda h, b: (0, h))),
        out_shape=(
            jax.ShapeDtypeStruct((nb_seq, N_HEADS, tp, 2 * LANES), BF16),
            jax.ShapeDtypeStruct((nb_seq, N_HEADS, tp, 2 * LANES), BF16),
            jax.ShapeDtypeStruct((nb_seq, N_HEADS, tp, D_V), BF16),
            jax.ShapeDtypeStruct((nb_seq * tp, N_HEADS * D_V), BF16),
            jax.ShapeDtypeStruct((1, N_HEADS * D_V), F32),
        ),
        scratch_shapes=[pltpu.VMEM((tp, 2 * LANES), F32)],
        compiler_params=_params("arbitrary", "arbitrary"),
    )(q, k, v, o, lse, dcat, p, g_attn)


def _qkv_bwd(p, dq, dk, dv, wq, wkv, gq, gkv, tables):
    nb_seq, _, tp, _ = dq.shape
    ht = tp // 2

    def body(pa_ref, dq_ref, dk_ref, dv_ref, wq_ref, wkv_ref, gq_ref, gkv_ref, cos_ref, sa_ref, sb_ref,
             dpa_ref, dwq_ref, dwkv_ref, dgq_ref, dgkv_ref):
        @pl.when(pl.program_id(0) == 0)
        def _():
            dwq_ref[...] = jnp.zeros_like(dwq_ref)
            dwkv_ref[...] = jnp.zeros_like(dwkv_ref)
            dgq_ref[...] = jnp.zeros_like(dgq_ref)
            dgkv_ref[...] = jnp.zeros_like(dgkv_ref)

        pa = pa_ref[...].astype(F32)
        gq, gkv = gq_ref[...], gkv_ref[...]
        cq_hat, rq = _rms_stats(pa[:, :Q_RANK])
        ckv_hat, rkv = _rms_stats(pa[:, Q_RANK:Q_RANK + KV_RANK])
        tabs = (cos_ref[...], sa_ref[...], sb_ref[...])

        pe = [dq_ref[0, h, :, D_NOPE:].astype(F32) for h in range(N_HEADS)]
        pairs = [_rope_t(pe[2 * i] + pltpu.roll(pe[2 * i + 1], D_ROPE, 1), *tabs).astype(BF16) for i in range(2)]
        dq_flat = jnp.concatenate([dq_ref[0, h, :, :D_NOPE] for h in range(N_HEADS)] + pairs, axis=1)
        dwq_ref[...] += _dot((cq_hat * gq).astype(BF16), dq_flat, _TN)
        dcqn = _dot(dq_flat, wq_ref[...], _NT)
        dgq_ref[...] += jnp.sum(dcqn * cq_hat, axis=0, keepdims=True)
        dcq = _rms_bwd(gq * dcqn, cq_hat, rq)

        dkv_flat = jnp.concatenate([dk_ref[0, h, :, :D_NOPE] for h in range(N_HEADS)]
                                   + [dv_ref[0, h] for h in range(N_HEADS)], axis=1)
        dwkv_ref[...] += _dot((ckv_hat * gkv).astype(BF16), dkv_flat, _TN)
        dckvn = _dot(dkv_flat, wkv_ref[...], _NT)
        dgkv_ref[...] += jnp.sum(dckvn * ckv_hat, axis=0, keepdims=True)
        dckv = _rms_bwd(gkv * dckvn, ckv_hat, rkv)

        dk_pe = dk_ref[0, 0, :, D_NOPE:].astype(F32)
        for h in range(1, N_HEADS):
            dk_pe = dk_pe + dk_ref[0, h, :, D_NOPE:].astype(F32)
        dk_pe = jnp.where(lax.broadcasted_iota(jnp.int32, (ht, LANES), 1) < D_ROPE, dk_pe, 0.0)
        dpa_ref[...] = jnp.concatenate([dcq, dckv, _rope_t(dk_pe, *tabs)], axis=1).astype(BF16)

    full = lambda a: pl.BlockSpec(a.shape, lambda i: (0,) * a.ndim)
    tab = pl.BlockSpec((ht, LANES), lambda i: (i % 2, 0))
    qk = pl.BlockSpec((1, N_HEADS, ht, 2 * LANES), lambda i: (i // 2, 0, i % 2, 0))
    acc = lambda shape: pl.BlockSpec(shape, lambda i: (0, 0))
    return pl.pallas_call(
        body,
        name="qkv_bwd",
        grid=(2 * nb_seq,),
        in_specs=[pl.BlockSpec((ht, GRP_A), lambda i: (i, 0)), qk, qk,
                  pl.BlockSpec((1, N_HEADS, ht, D_V), lambda i: (i // 2, 0, i % 2, 0)),
                  full(wq), full(wkv), full(gq), full(gkv), tab, tab, tab],
        out_specs=(pl.BlockSpec((ht, GRP_A), lambda i: (i, 0)),
                   acc(wq.shape), acc(wkv.shape), acc((1, Q_RANK)), acc((1, KV_RANK))),
        out_shape=(
            jax.ShapeDtypeStruct((nb_seq * tp, GRP_A), BF16),
            jax.ShapeDtypeStruct(wq.shape, F32),
            jax.ShapeDtypeStruct(wkv.shape, F32),
            jax.ShapeDtypeStruct((1, Q_RANK), F32),
            jax.ShapeDtypeStruct((1, KV_RANK), F32),
        ),
        compiler_params=_params("arbitrary"),
    )(p, dq, dk, dv, wq, wkv, gq, gkv, *tables)


def _conv_bwd(p, dcat, conv_w, g_conv, nb_seq, tp):
    cols = CONV_WIDTH // LANES

    def body(b_ref, c_ref, h_ref, z_ref, dy_ref, w_ref, g_ref,
             db_ref, dc_ref, dh_ref, dz_ref, dw_ref, dg_ref):
        @pl.when(pl.program_id(1) == 0)
        def _():
            dw_ref[...] = jnp.zeros_like(dw_ref)
            dg_ref[...] = jnp.zeros_like(dg_ref)

        cb, c, h = b_ref[...].astype(F32), c_ref[...].astype(F32), h_ref[...].astype(F32)
        z, dy = z_ref[...].astype(F32), dy_ref[...].astype(F32)
        g = g_ref[...]
        w0, w1, w2 = w_ref[0:1, :], w_ref[1:2, :], w_ref[2:3, :]
        cc = c * h
        row = lax.broadcasted_iota(jnp.int32, (tp, LANES), 0)
        s1 = jnp.where(row >= 1, pltpu.roll(cc, 1, 0), 0.0)
        s2 = jnp.where(row >= 2, pltpu.roll(cc, 2, 0), 0.0)
        dwc = w0 * s2 + w1 * s1 + w2 * cc
        yc = cb * dwc
        r = lax.rsqrt(_group_mean(yc * yc) + EPS)
        ychat = yc * r
        sig = _sigmoid(z)
        dz_ref[...] = (dy * (ychat * g) * (sig * (1.0 + z * (1.0 - sig)))).astype(BF16)
        dyn = dy * (z * sig)
        dg_ref[...] += jnp.sum(dyn * ychat, axis=0, keepdims=True)
        gd = g * dyn
        dyc = r * (gd - ychat * _group_mean(gd * ychat))
        db_ref[...] = (dyc * dwc).astype(BF16)
        ddw = dyc * cb
        dw_ref[0:1, :] += jnp.sum(ddw * s2, axis=0, keepdims=True)
        dw_ref[1:2, :] += jnp.sum(ddw * s1, axis=0, keepdims=True)
        dw_ref[2:3, :] += jnp.sum(ddw * cc, axis=0, keepdims=True)
        u1 = jnp.where(row <= tp - 2, pltpu.roll(ddw, tp - 1, 0), 0.0)
        u2 = jnp.where(row <= tp - 3, pltpu.roll(ddw, tp - 2, 0), 0.0)
        dcc = w2 * ddw + w1 * u1 + w0 * u2
        dc_ref[...] = (dcc * h).astype(BF16)
        dh_ref[...] = (dcc * c).astype(BF16)

    col = pl.BlockSpec((tp, LANES), lambda t, b: (b, t))
    out = jax.ShapeDtypeStruct((nb_seq * tp, CONV_WIDTH), BF16)
    return pl.pallas_call(
        body,
        name="conv_bwd",
        grid=(cols, nb_seq),
        in_specs=_conv_specs(tp, lambda t, b, off: (b, off + t)) + [
            pl.BlockSpec((tp, LANES), lambda t, b: (b, N_HEADS * D_V // LANES + t)),
            pl.BlockSpec((8, LANES), lambda t, b: (0, t)),
            pl.BlockSpec((1, LANES), lambda t, b: (0, t))],
        out_specs=(col, col, col, col,
                   pl.BlockSpec((8, LANES), lambda t, b: (0, t)), pl.BlockSpec((1, LANES), lambda t, b: (0, t))),
        out_shape=(out, out, out, out,
                   jax.ShapeDtypeStruct((8, CONV_WIDTH), F32), jax.ShapeDtypeStruct((1, CONV_WIDTH), F32)),
        compiler_params=_params("arbitrary", "arbitrary"),
    )(p, p, p, p, dcat, conv_w, g_conv)


def _input_bwd(dps, w_in, x, meta, dh, norm_g, nt, send_in):
    nb_seq, s, d = x.shape
    r, kb = dps[0].shape
    ts = (s + LANES) // nt
    steps = nb_seq * nt
    n_dp = len(dps)
    in_slot = send_in.shape[1:]

    def body(*refs):
        dp_refs, w_ref, x_hbm, meta_ref, dh_ref, g_ref, pay_ref = refs[:n_dp], *refs[n_dp:n_dp + 6]
        o = n_dp + 6
        gx_hbm, dmeta_ref, dg_ref, r2_in = refs[o:o + 4]
        xbuf, gxbuf, tok_sems, own_in, r1_in, sum_in = refs[o + 4:o + 10]
        sems = refs[o + 10:]
        i = pl.program_id(0)
        b, k = i // nt, i % nt

        def plan():
            return _reduce_plan((pay_ref,), (own_in,), (r1_in,), (sum_in,), (r2_in,), *sems)

        @pl.when(i == 0)
        def _():
            dmeta_ref[...] = jnp.zeros_like(dmeta_ref)
            dg_ref[...] = jnp.zeros_like(dg_ref)
            plan()[0]()

        @pl.when(i == 1)
        def _():
            plan()[1]()

        def start(kk):
            if kk == 0:
                xbuf[0:PAD_FRONT, :] = jnp.zeros((PAD_FRONT, d), F32)
                xbuf[PAD_FRONT:LANES, :] = meta_ref[...]
            _token_copy(x_hbm, b, kk, ts, xbuf, tok_sems.at[0]).start()

        _for_tile(k, nt, start)
        du = _dot(dp_refs[0][...], w_ref[0:kb, :])
        for j in range(1, n_dp):
            du = du + _dot(dp_refs[j][...], w_ref[kb * j:kb * (j + 1), :])
        _for_tile(k, nt, lambda kk: _token_copy(x_hbm, b, kk, ts, xbuf, tok_sems.at[0]).wait())

        g = g_ref[...]
        hhat, rstd = _rms_stats(xbuf[...])
        dg_ref[...] += jnp.sum(du * hhat, axis=0, keepdims=True)
        res = _rms_bwd(g * du, hhat, rstd) + dh_ref[...]

        @pl.when(i > 0)
        def _():
            _for_tile(k, nt, lambda kk: _token_copy(gx_hbm, b, (kk - 1) % nt, ts, gxbuf, tok_sems.at[1], True).wait())

        gxbuf[...] = res

        @pl.when(k == 0)
        def _():
            dmeta_ref[...] += gxbuf[PAD_FRONT:LANES, :]

        _for_tile(k, nt, lambda kk: _token_copy(gx_hbm, b, kk, ts, gxbuf, tok_sems.at[1], True).start())

        @pl.when(i == steps - 1)
        def _():
            _token_copy(gx_hbm, b, nt - 1, ts, gxbuf, tok_sems.at[1], True).wait()
            plan()[2]()

    whole = lambda a: pl.BlockSpec(a.shape, lambda i: (0,) * a.ndim)
    hbm = pl.BlockSpec(memory_space=pl.ANY)
    return pl.pallas_call(
        body,
        name="input_bwd",
        grid=(steps,),
        in_specs=[pl.BlockSpec((ts, kb), lambda i: (i, 0)) for _ in dps]
        + [whole(w_in), hbm, whole(meta), pl.BlockSpec((ts, d), lambda i: (i, 0)), whole(norm_g), hbm],
        out_specs=(hbm, pl.BlockSpec((N_META, d), lambda i: (0, 0)), pl.BlockSpec((1, d), lambda i: (0, 0)), hbm),
        out_shape=(jax.ShapeDtypeStruct((nb_seq, s, d), F32),
                   jax.ShapeDtypeStruct((N_META, d), F32),
                   jax.ShapeDtypeStruct((1, d), F32),
                   jax.ShapeDtypeStruct((N_CHIPS,) + in_slot, BF16)),
        scratch_shapes=[pltpu.VMEM((ts, d), F32), pltpu.VMEM((ts, d), F32), pltpu.SemaphoreType.DMA((2,))]
        + _reduce_scratch([(in_slot, BF16)], [True]),
        compiler_params=_params("arbitrary"),
    )(*dps, w_in, x, meta, dh, norm_g, send_in)


def _in_proj_bwd_w(u, dps, bm, small_grads, send_out):
    r, d = u.shape
    kb = dps[0].shape[1]
    steps = r // bm
    n_dp, n_small = len(dps), len(small_grads)
    out_slot, small_slot = send_out.shape[1:], (SMALL_ROWS, LANES)

    def body(*refs):
        u_ref, dp_refs = refs[0], refs[1:1 + n_dp]
        small_refs = refs[1 + n_dp:1 + n_dp + n_small]
        o = 1 + n_dp + n_small
        pay_out, o_ref, r2_out, r2_small = refs[o:o + 4]
        acc_ref, ssmall, r1_out, sum_out, r1_small, sum_small = refs[o + 4:o + 10]
        sems = refs[o + 10:]
        i = pl.program_id(0)

        def plan():
            return _reduce_plan((pay_out, ssmall), (None, None), (r1_out, r1_small), (sum_out, sum_small),
                                (r2_out, r2_small), *sems)

        @pl.when(i == 0)
        def _():
            acc_ref[...] = jnp.zeros_like(acc_ref)
            _pack_small(ssmall, *small_refs)
            plan()[0]()

        @pl.when(i == 1)
        def _():
            plan()[1]()

        uu = u_ref[...]
        for j in range(n_dp):
            acc_ref[kb * j:kb * (j + 1), :] += _dot(dp_refs[j][...], uu, _TN)

        @pl.when(i == steps - 1)
        def _():
            for k in range(N_DEV):
                for s, e, c0 in _in_pieces(k):
                    o_ref[k, s:e, :] = acc_ref[c0:c0 + e - s, :].astype(BF16)
                o_ref[k, SHARD_IN:, :] = jnp.zeros((SHARD_IN_PAD - SHARD_IN, d), BF16)
            plan()[2]()

    whole = lambda a: pl.BlockSpec(a.shape, lambda i: (0,) * a.ndim)
    hbm = pl.BlockSpec(memory_space=pl.ANY)
    return pl.pallas_call(
        body,
        name="in_proj_bwd_w",
        grid=(steps,),
        in_specs=[pl.BlockSpec((bm, d), lambda i: (i, 0))]
        + [pl.BlockSpec((bm, kb), lambda i: (i, 0)) for _ in dps] + [whole(a) for a in small_grads]
        + [whole(send_out)],
        out_specs=(pl.BlockSpec((N_DEV, SHARD_IN_PAD, d), lambda i: (0, 0, 0)), hbm, hbm),
        out_shape=(jax.ShapeDtypeStruct((N_DEV, SHARD_IN_PAD, d), BF16),
                   jax.ShapeDtypeStruct((N_CHIPS,) + out_slot, BF16),
                   jax.ShapeDtypeStruct((N_CHIPS,) + small_slot, F32)),
        scratch_shapes=[pltpu.VMEM((kb * n_dp, d), F32), pltpu.VMEM((N_DEV,) + small_slot, F32)]
        + _reduce_scratch([(out_slot, BF16), (small_slot, F32)], [False, False]),
        compiler_params=_params("arbitrary"),
    )(u, *dps, *small_grads, send_out)


def _local_step(x, loss_target, u, p, meta_f, norm_g, w_in_p, q_norm_g, w_q_p, kv_norm_g, w_kv_p, conv_w_f,
                attn_out_g, conv_out_g, w_out_f, g_final):
    nb_seq, s, d = x.shape
    tp = s + LANES
    ht = tp // 2
    tables = _rope_tables(tp)

    q, k, v = _qkv_fwd(p, w_q_p, w_kv_p, q_norm_g, kv_norm_g, tables, nb_seq, tp)
    ya, o, lse = _attn_fwd(q, k, v, p, attn_out_g)
    yc = _conv_fwd(p, conv_w_f, conv_out_g, nb_seq, tp)
    dh, dhb, d_final_g, loss_part = _out_proj_loss(ya, yc, w_out_f, x, loss_target, g_final, TOKEN_TILES)

    dcat, d_w_out = _out_proj_bwd(dhb, w_out_f, ya, yc, ht)
    send_out = d_w_out.reshape(N_DEV, SHARD_OUT, d)
    dq, dk, dv, dz_attn, d_attn_g = _attn_bwd(q, k, v, o, lse, dcat, p, attn_out_g)
    dpa, d_wq_p, d_wkv_p, d_gq, d_gkv = _qkv_bwd(p, dq, dk, dv, w_q_p, w_kv_p, q_norm_g, kv_norm_g, tables)
    d_b, d_c, d_h, dz_conv, d_conv_w, d_conv_g = _conv_bwd(p, dcat, conv_w_f, conv_out_g, nb_seq, tp)
    dps = (dpa, dz_attn, d_b, d_c, d_h, dz_conv)
    small = (d_wq_p, d_wkv_p, d_conv_w, d_final_g, d_gq, d_gkv, d_attn_g, d_conv_g, loss_part)
    send_in, r_out, r_small = _in_proj_bwd_w(u, dps, ht // 2, small, send_out)
    grad_x, d_meta, d_norm_g, r_in = _input_bwd(dps, w_in_p, x, meta_f, dh, norm_g, TOKEN_TILES, send_in)
    return grad_x, r_in, r_out, r_small, d_meta, d_norm_g


def kernel(x, meta_tokens, norm_g, w_in, q_norm_g, w_q_up, kv_norm_g, w_kv_up, conv_w, attn_out_g, conv_out_g, w_out, final_norm_g, loss_target, m_meta_tokens, m_norm_g, m_w_in, m_q_norm_g, m_w_q_up, m_kv_norm_g, m_w_kv_up, m_conv_w, m_attn_out_g, m_conv_out_g, m_w_out, m_final_norm_g, v_meta_tokens, v_norm_g, v_w_in, v_q_norm_g, v_w_q_up, v_kv_norm_g, v_w_kv_up, v_conv_w, v_attn_out_g, v_conv_out_g, v_w_out, v_final_norm_g):
    d = x.shape[-1]
    ht = (x.shape[1] + LANES) // 2
    u, w_in_p, meta_f = _prep_gather(x, meta_tokens, norm_g, w_in[0].T)
    p, w_q_p, w_kv_p, w_out_f, conv_w_f = _in_proj_gather(
        u, w_in_p, w_q_up[0], w_kv_up[0], w_out[0], conv_w[0], ht, 3 * GRP_A)
    g_final = final_norm_g.reshape(1, d)
    grad_x, r_in, r_out, r_small, d_meta, d_norm_g = _local_step(
        x, loss_target, u, p, meta_f, norm_g, w_in_p, q_norm_g, w_q_p, kv_norm_g, w_kv_p, conv_w_f,
        attn_out_g, conv_out_g, w_out_f, g_final)

    flat = lambda a: a.reshape(a.shape[-2:]) if a.ndim == 3 else a.reshape(1, -1) if a.ndim == 1 else a
    transposed = ("w_in",)
    to_kernel = lambda n, a: flat(a).T if n in transposed else flat(a)
    from_kernel = lambda n, a, shape: (a.T if n in transposed else a).reshape(shape)
    params = {
        "meta_tokens": (meta_tokens, m_meta_tokens, v_meta_tokens),
        "norm_g": (norm_g, m_norm_g, v_norm_g),
        "w_in": (w_in, m_w_in, v_w_in),
        "q_norm_g": (q_norm_g, m_q_norm_g, v_q_norm_g),
        "w_q_up": (w_q_up, m_w_q_up, v_w_q_up),
        "kv_norm_g": (kv_norm_g, m_kv_norm_g, v_kv_norm_g),
        "w_kv_up": (w_kv_up, m_w_kv_up, v_w_kv_up),
        "conv_w": (conv_w, m_conv_w, v_conv_w),
        "attn_out_g": (attn_out_g, m_attn_out_g, v_attn_out_g),
        "conv_out_g": (conv_out_g, m_conv_out_g, v_conv_out_g),
        "w_out": (w_out, m_w_out, v_w_out),
        "final_norm_g": (final_norm_g, m_final_norm_g, v_final_norm_g),
    }
    grads, loss = _reduce_tail(r_in, r_out, r_small, d_meta, d_norm_g)
    updated = _adamw(grads, {n: tuple(to_kernel(n, a) for a in t) for n, t in params.items()})
    outs = [[from_kernel(n, updated[n][i], params[n][0].shape) for n, _ in PARAM_SHAPES] for i in range(4)]
    return (loss[0, 0], grad_x, *outs[0], *outs[1], *outs[2], *outs[3])
```

```python
import functools

import jax
import jax.numpy as jnp
from jax import lax
from jax.experimental import pallas as pl
from jax.experimental.pallas import tpu as pltpu

F32 = jnp.float32
BF16 = jnp.bfloat16

N_META = 16
D_MODEL = 1024
N_HEADS = 4
D_NOPE = 128
D_ROPE = 64
D_V = 128
Q_RANK = 256
KV_RANK = 128
CONV_WIDTH = 512
CONV_GROUP = 64
ROPE_THETA = 10000.0
ATTN_SCALE = (D_NOPE + D_ROPE) ** -0.5
Q_SCALE = ATTN_SCALE * 1.4426950408889634
EPS = 1e-6
NEG_INF = -1e30

ADAM_LR = 0.001
ADAM_B1 = 0.9
ADAM_B2 = 0.999
ADAM_EPS = 1e-08
ADAM_WD = 0.01
ADAM_STEP = 10

LANES = 128
PAD_FRONT = LANES - N_META
K_TILE = 256
Q_TILE = 512
N_DEV = 8
VMEM_LIMIT = 56 * 1024 * 1024

IN_PAD = 3072
GRP_A = 512
N_A = Q_RANK + KV_RANK + D_ROPE
IN_PROJ = 3008
SHARD_IN = IN_PROJ // N_DEV
SHARD_IN_PAD = 384
SHARD_Q = 96
SHARD_KV = 128
SHARD_OUT = 128
SHARD_CONV = 64
SHARD_META = 128
Q_COLS = N_HEADS * (D_NOPE + D_ROPE)
KV_COLS = N_HEADS * (D_NOPE + D_V)

ROW_Q, ROW_KV, ROW_META, ROW_CONV = 0, 256, 384, 400
ROW_REPL = 408
ROW_NORM, ROW_FINAL, ROW_GQ, ROW_GKV, ROW_ATTN, ROW_CONVG, ROW_LOSS = 408, 416, 424, 426, 427, 431, 435
SMALL_ROWS = 440

PARAM_SHAPES = (
    ("meta_tokens", (N_META, SHARD_META)), ("norm_g", (1, D_MODEL)), ("w_in", (SHARD_IN, D_MODEL)),
    ("q_norm_g", (1, Q_RANK)), ("w_q_up", (SHARD_Q, Q_RANK)), ("kv_norm_g", (1, KV_RANK)),
    ("w_kv_up", (KV_RANK, SHARD_KV)), ("conv_w", (3, SHARD_CONV)), ("attn_out_g", (1, CONV_WIDTH)),
    ("conv_out_g", (1, CONV_WIDTH)), ("w_out", (SHARD_OUT, D_MODEL)), ("final_norm_g", (1, D_MODEL)),
)


def _in_pieces(k):
    lo, hi = SHARD_IN * k, SHARD_IN * (k + 1)
    out = []
    if lo < N_A:
        out.append((0, min(hi, N_A) - lo, lo))
    if hi > N_A:
        s = max(lo, N_A)
        out.append((s - lo, hi - lo, s + GRP_A - N_A))
    return out


def _q_pieces(k):
    lo, hi = SHARD_Q * k, SHARD_Q * (k + 1)
    out = []
    for h in range(N_HEADS):
        base = (D_NOPE + D_ROPE) * h
        s, e = max(lo, base), min(hi, base + D_NOPE)
        if s < e:
            out.append((s - lo, e - lo, D_NOPE * h + s - base))
        s, e = max(lo, base + D_NOPE), min(hi, base + D_NOPE + D_ROPE)
        if s < e:
            out.append((s - lo, e - lo, N_HEADS * D_NOPE + D_ROPE * h + s - base - D_NOPE))
    return out


def _kv_dst(k):
    return D_NOPE * (k // 2) + (N_HEADS * D_NOPE if k % 2 else 0)


def _params(*sem):
    return pltpu.CompilerParams(dimension_semantics=sem, vmem_limit_bytes=VMEM_LIMIT)


def _rms_stats(x):
    r = lax.rsqrt(jnp.mean(x * x, axis=-1, keepdims=True) + EPS)
    return x * r, r


def _rms_bwd(gdy, xhat, r):
    return r * (gdy - xhat * jnp.mean(gdy * xhat, axis=-1, keepdims=True))


def _sigmoid(z):
    return 1.0 / (1.0 + jnp.exp(-z))


def _group_mean(x):
    i0 = lax.broadcasted_iota(jnp.int32, (LANES, LANES), 0) // CONV_GROUP
    i1 = lax.broadcasted_iota(jnp.int32, (LANES, LANES), 1) // CONV_GROUP
    m = jnp.where(i0 == i1, 1.0 / CONV_GROUP, 0.0).astype(BF16)
    hi = x.astype(BF16)
    lo = (x - hi.astype(F32)).astype(BF16)
    return jnp.dot(hi, m, preferred_element_type=F32) + jnp.dot(lo, m, preferred_element_type=F32)


_NT = (((1,), (1,)), ((), ()))
_TN = (((0,), (0,)), ((), ()))


def _dot(a, b, dims=None):
    if dims is None:
        return jnp.dot(a, b, preferred_element_type=F32)
    return lax.dot_general(a, b, dims, preferred_element_type=F32)


def _device_position():
    x, y, c = lax.axis_index("x"), lax.axis_index("y"), lax.axis_index("c")
    return x, y, c, 4 * x + 2 * y + c


def _gather_plan(srcs, slots, send_sems, recv_sems, local_sems):
    x, y, c, _ = _device_position()
    me, sibling = (x, y, c), (x, y, 1 - c)
    flip = lambda v, on: v + on - 2 * v * on
    near = (flip(x, 1 - c), flip(y, c))
    far = (flip(x, c), flip(y, 1 - c))
    diag = (1 - x, 1 - y)
    n = len(srcs)

    def slot(a, px, py, pc):
        return slots[a].at[4 * px + 2 * py + pc]

    def copy(a, k, block, to, own=False):
        return pltpu.make_async_remote_copy(
            src_ref=srcs[a] if own else slot(a, *block),
            dst_ref=slot(a, *block),
            send_sem=send_sems.at[7 * a + k],
            recv_sem=recv_sems.at[7 * a + k],
            device_id=to,
            device_id_type=pl.DeviceIdType.MESH,
        )

    def local(a):
        return pltpu.make_async_copy(srcs[a], slot(a, *me), local_sems.at[a])

    sent = [(me, sibling), (me, (*near, c)), (me, (*far, c)), ((*near, c), (*far, c)),
            ((*near, c), sibling), ((*far, c), sibling), ((*diag, c), sibling)]
    landed = [sibling, (*near, c), (*far, c), (*diag, c), (*far, 1 - c), (*near, 1 - c), (*diag, 1 - c)]

    def send(a, k):
        return copy(a, k, *sent[k], own=k < 3)

    def arrival(a, k):
        return copy(a, k, landed[k], me)

    def start():
        for a in range(n):
            local(a).start()
            for k in range(3):
                send(a, k).start()

    def relay():
        for a in range(n):
            arrival(a, 1).wait_recv()
            send(a, 3).start()
            send(a, 4).start()

    def forward():
        for k in (2, 3):
            for a in range(n):
                arrival(a, k).wait_recv()
                send(a, k + 3).start()

    def finish():
        for a in range(n):
            for k in (0, 4, 5, 6):
                arrival(a, k).wait_recv()
        for a in range(n):
            for k in range(7):
                send(a, k).wait_send()
            local(a).wait()

    return start, relay, forward, finish


def _adam_update(g, w, m, v):
    m_new = ADAM_B1 * m + (1.0 - ADAM_B1) * g
    v_new = ADAM_B2 * v + (1.0 - ADAM_B2) * (g * g)
    m_hat = m_new / (1.0 - ADAM_B1 ** ADAM_STEP)
    v_hat = v_new / (1.0 - ADAM_B2 ** ADAM_STEP)
    return -ADAM_LR * (m_hat / (jnp.sqrt(v_hat) + ADAM_EPS) + ADAM_WD * w), m_new, v_new


def _adamw(grads, params):
    names = [n for n, _ in PARAM_SHAPES]
    n_p = len(names)

    def body(*refs):
        for i in range(n_p):
            g = refs[i][...]
            w, m, v = (refs[n_p + 3 * i + j][...] for j in range(3))
            delta, m_new, v_new = _adam_update(g, w, m, v)
            for j, val in enumerate((g, delta, m_new, v_new)):
                refs[4 * n_p + 4 * i + j][...] = val

    vm = pl.BlockSpec(memory_space=pltpu.VMEM)
    out_shape = []
    for _, shape in PARAM_SHAPES:
        out_shape += [jax.ShapeDtypeStruct(shape, F32)] * 4
    outs = pl.pallas_call(
        body,
        name="adamw",
        out_shape=tuple(out_shape),
        in_specs=[vm] * (4 * n_p),
        out_specs=(vm,) * (4 * n_p),
        compiler_params=pltpu.CompilerParams(vmem_limit_bytes=VMEM_LIMIT),
    )(*[grads[n] for n in names], *[a for n in names for a in params[n]])
    return {n: outs[4 * i:4 * i + 4] for i, n in enumerate(names)}


N_CHIPS = 4


def _reduce_plan(pays, owns, r1s, sums, r2s, send1, recv1, send2, recv2, local_sems):
    x, y, c, _ = _device_position()
    sibling = (x, y, 1 - c)
    chips = [((1 - x if rj & 2 else x), (1 - y if rj & 1 else y)) for rj in range(N_CHIPS)]
    n = len(pays)

    def slot_of(rj, core):
        return 4 * chips[rj][0] + 2 * chips[rj][1] + core

    def to_sibling(a, rj):
        return pltpu.make_async_remote_copy(
            src_ref=pays[a].at[slot_of(rj, 1 - c)], dst_ref=r1s[a].at[rj],
            send_sem=send1.at[N_CHIPS * a + rj], recv_sem=recv1.at[N_CHIPS * a + rj],
            device_id=sibling, device_id_type=pl.DeviceIdType.MESH)

    def load_own(a, rj):
        return pltpu.make_async_copy(pays[a].at[slot_of(rj, c)], owns[a].at[rj], local_sems.at[2 * N_CHIPS * a + rj])

    def to_chip(a, rj):
        return pltpu.make_async_remote_copy(
            src_ref=sums[a].at[rj], dst_ref=r2s[a].at[rj],
            send_sem=send2.at[N_CHIPS * a + rj], recv_sem=recv2.at[N_CHIPS * a + rj],
            device_id=(*chips[rj], c), device_id_type=pl.DeviceIdType.MESH)

    def keep(a):
        return pltpu.make_async_copy(sums[a].at[0], r2s[a].at[0], local_sems.at[2 * N_CHIPS * a + N_CHIPS])

    def start():
        for a in range(n):
            for rj in range(N_CHIPS):
                to_sibling(a, rj).start()
                if owns[a] is not None:
                    load_own(a, rj).start()

    def combine():
        for a in range(n):
            for rj in range(N_CHIPS):
                to_sibling(a, rj).wait_recv()
                if owns[a] is not None:
                    load_own(a, rj).wait()
                    mine = owns[a][rj]
                else:
                    mine = pays[a][slot_of(rj, c)]
                sums[a][rj] = (mine.astype(F32) + r1s[a][rj].astype(F32)).astype(sums[a].dtype)
            keep(a).start()
            for rj in range(1, N_CHIPS):
                to_chip(a, rj).start()

    def finish():
        for a in range(n):
            for rj in range(1, N_CHIPS):
                to_chip(a, rj).wait_recv()
            for rj in range(N_CHIPS):
                to_sibling(a, rj).wait_send()
            for rj in range(1, N_CHIPS):
                to_chip(a, rj).wait_send()
            keep(a).wait()

    return start, combine, finish


def _reduce_scratch(shapes_dtypes, own_flags):
    out = []
    for (shape, dtype), own in zip(shapes_dtypes, own_flags):
        if own:
            out.append(pltpu.VMEM((N_CHIPS,) + shape, dtype))
        out += [pltpu.VMEM((N_CHIPS,) + shape, dtype), pltpu.VMEM((N_CHIPS,) + shape, dtype)]
    n = len(shapes_dtypes)
    out += [pltpu.SemaphoreType.DMA((N_CHIPS * n,))] * 4 + [pltpu.SemaphoreType.DMA((2 * N_CHIPS * n,))]
    return out


def _pack_small(ssmall, dwq, dwkv, dconv, dfinal, dgq, dgkv, dattn, dconvg, loss_part):
    ssmall[...] = jnp.zeros_like(ssmall)
    rep = ssmall.at[0]
    for i in range(D_MODEL // LANES):
        rep[ROW_FINAL + i:ROW_FINAL + i + 1, :] = dfinal[:, LANES * i:LANES * (i + 1)]
    for i in range(Q_RANK // LANES):
        rep[ROW_GQ + i:ROW_GQ + i + 1, :] = dgq[:, LANES * i:LANES * (i + 1)]
    rep[ROW_GKV:ROW_GKV + 1, :] = dgkv[...]
    for i in range(CONV_WIDTH // LANES):
        rep[ROW_ATTN + i:ROW_ATTN + i + 1, :] = dattn[:, LANES * i:LANES * (i + 1)]
        rep[ROW_CONVG + i:ROW_CONVG + i + 1, :] = dconvg[:, LANES * i:LANES * (i + 1)]
    rep[ROW_LOSS:ROW_LOSS + 1, :] = loss_part[...]
    for k in range(N_DEV):
        if k:
            ssmall[k, ROW_REPL:, :] = ssmall[0, ROW_REPL:, :]
        for s, e, d in _q_pieces(k):
            for i in range(Q_RANK // LANES):
                ssmall[k, ROW_Q + SHARD_Q * i + s:ROW_Q + SHARD_Q * i + e, :] = dwq[d:d + e - s, LANES * i:LANES * (i + 1)]
        ssmall[k, ROW_KV:ROW_KV + KV_RANK, :] = dwkv[:, _kv_dst(k):_kv_dst(k) + SHARD_KV]
        ssmall[k, ROW_CONV:ROW_CONV + 3, 0:SHARD_CONV] = dconv[0:3, SHARD_CONV * k:SHARD_CONV * (k + 1)]


TOKEN_TILES = 4
TAIL_ROWS = N_META + D_MODEL // LANES


def _reduce_tail(r_in, r_out, r_small, d_meta, d_norm):
    n_p = len(PARAM_SHAPES)
    names = [n for n, _ in PARAM_SHAPES]

    def body(*refs):
        rin, rout, rsmall, dmeta, dnorm = refs[:5]
        g_out = {n: refs[5 + i] for i, n in enumerate(names)}
        loss_out = refs[5 + n_p]
        stail, rtail, gsum, gtail, send_sems, recv_sems = refs[6 + n_p:]
        x, y, c, me = _device_position()
        my_chip = 2 * x + y

        for k in range(N_DEV):
            stail[k, 0:N_META, :] = dmeta[:, SHARD_META * k:SHARD_META * (k + 1)]
            for i in range(D_MODEL // LANES):
                stail[k, N_META + i:N_META + i + 1, :] = dnorm[:, LANES * i:LANES * (i + 1)]
        copies = []
        for r in range(1, N_DEV):
            peer = (1 - x if r & 4 else x, 1 - y if r & 2 else y, 1 - c if r & 1 else c)
            copies.append(pltpu.make_async_remote_copy(
                src_ref=stail.at[4 * peer[0] + 2 * peer[1] + peer[2]],
                dst_ref=rtail.at[r],
                send_sem=send_sems.at[r - 1],
                recv_sem=recv_sems.at[r - 1],
                device_id=peer,
                device_id_type=pl.DeviceIdType.MESH,
            ))
        for cp in copies:
            cp.start()
        rtail[0] = stail[me]

        g = rin[my_chip].astype(F32)
        for ch in range(1, N_CHIPS):
            g = g + rin[ch ^ my_chip].astype(F32)
        g_out["w_in"][...] = g[:SHARD_IN, :]

        g = rout[my_chip].astype(F32)
        gs = rsmall[my_chip]
        for ch in range(1, N_CHIPS):
            g = g + rout[ch ^ my_chip].astype(F32)
            gs = gs + rsmall[ch ^ my_chip]
        g_out["w_out"][...] = g
        gsum[...] = gs
        for i in range(Q_RANK // LANES):
            g_out["w_q_up"][:, LANES * i:LANES * (i + 1)] = gsum[ROW_Q + SHARD_Q * i:ROW_Q + SHARD_Q * (i + 1), :]
        g_out["w_kv_up"][...] = gsum[ROW_KV:ROW_KV + KV_RANK, :]
        g_out["conv_w"][...] = gsum[ROW_CONV:ROW_CONV + 3, 0:SHARD_CONV]
        for name, row, width in (("final_norm_g", ROW_FINAL, D_MODEL), ("q_norm_g", ROW_GQ, Q_RANK),
                                 ("kv_norm_g", ROW_GKV, KV_RANK), ("attn_out_g", ROW_ATTN, CONV_WIDTH),
                                 ("conv_out_g", ROW_CONVG, CONV_WIDTH)):
            for i in range(width // LANES):
                g_out[name][:, LANES * i:LANES * (i + 1)] = gsum[row + i:row + i + 1, :]
        loss_out[...] = gsum[ROW_LOSS:ROW_LOSS + 1, :]

        for cp in copies:
            cp.wait_recv()
        gt = rtail[me]
        for d in range(1, N_DEV):
            gt = gt + rtail[d ^ me]
        gtail[...] = gt
        g_out["meta_tokens"][...] = gtail[0:N_META, :]
        for i in range(D_MODEL // LANES):
            g_out["norm_g"][:, LANES * i:LANES * (i + 1)] = gtail[N_META + i:N_META + i + 1, :]
        for cp in copies:
            cp.wait_send()

    vm = pl.BlockSpec(memory_space=pltpu.VMEM)
    out_shape = [jax.ShapeDtypeStruct(shape, F32) for _, shape in PARAM_SHAPES]
    out_shape.append(jax.ShapeDtypeStruct((1, LANES), F32))
    outs = pl.pallas_call(
        body,
        name="reduce_tail",
        out_shape=tuple(out_shape),
        in_specs=[vm] * 5,
        out_specs=(vm,) * len(out_shape),
        scratch_shapes=[
            pltpu.VMEM((N_DEV, TAIL_ROWS, LANES), F32),
            pltpu.VMEM((N_DEV, TAIL_ROWS, LANES), F32),
            pltpu.VMEM((SMALL_ROWS, LANES), F32),
            pltpu.VMEM((TAIL_ROWS, LANES), F32),
            pltpu.SemaphoreType.DMA((N_DEV - 1,)),
            pltpu.SemaphoreType.DMA((N_DEV - 1,)),
        ],
        compiler_params=pltpu.CompilerParams(vmem_limit_bytes=VMEM_LIMIT),
    )(r_in, r_out, r_small, d_meta, d_norm)
    return {n: outs[i] for i, n in enumerate(names)}, outs[-1]


def _prep_gather(x, meta, norm_g, w_in_t):
    nb_seq, s, d = x.shape
    nb = s // LANES + 1
    relay_step = nb_seq * (nb - 1) // 2
    forward_step = nb_seq * (nb - 1) - 1
    finish_step = nb_seq * (nb - 1)

    def body(x_ref, meta_ref, g_ref, win_ref, u_ref, w_in_p, meta_f,
             sbig, ssmall, gbig, gsmall, send_sems, recv_sems, local_sems):
        jj, b = pl.program_id(0), pl.program_id(1)
        t = jj * nb_seq + b

        def plan():
            return _gather_plan((sbig, ssmall), (gbig, gsmall), send_sems, recv_sems, local_sems)

        @pl.when(t == 0)
        def _():
            sbig[0:SHARD_IN, :] = win_ref[...].astype(BF16)
            sbig[SHARD_IN:, :] = jnp.zeros((SHARD_IN_PAD - SHARD_IN, d), BF16)
            ssmall[...] = meta_ref[...]
            plan()[0]()

        @pl.when(t == relay_step)
        def _():
            plan()[1]()

        @pl.when(t == forward_step)
        def _():
            plan()[2]()

        @pl.when(t == finish_step)
        def _():
            plan()[3]()
            w_in_p[N_A:GRP_A, :] = jnp.zeros((GRP_A - N_A, d), BF16)
            for k in range(N_DEV):
                for s0, e0, d0 in _in_pieces(k):
                    w_in_p[d0:d0 + e0 - s0, :] = gbig[k, s0:e0, :]
                meta_f[:, SHARD_META * k:SHARD_META * (k + 1)] = gsmall[k]

        def norm(h):
            hhat, _ = _rms_stats(h)
            return (hhat * g_ref[...]).astype(BF16)

        @pl.when(jj < nb - 1)
        def _():
            u_ref[...] = norm(x_ref[0])

        @pl.when(jj == nb - 1)
        def _():
            u_ref[0:PAD_FRONT, :] = jnp.zeros((PAD_FRONT, d), BF16)
            u_ref[PAD_FRONT:LANES, :] = norm(meta_f[...])

    whole = lambda shape: pl.BlockSpec(shape, lambda jj, b: (0,) * len(shape))
    return pl.pallas_call(
        body,
        name="prep_norm_gather",
        grid=(nb, nb_seq),
        in_specs=[
            pl.BlockSpec((1, LANES, d), lambda jj, b: (b, jnp.minimum(jj, nb - 2), 0)),
            whole(meta.shape), whole(norm_g.shape), whole(w_in_t.shape),
        ],
        out_specs=(pl.BlockSpec((LANES, d), lambda jj, b: (b * nb + (jj + 1) % nb, 0)),
                   whole((IN_PAD, d)), whole((N_META, d))),
        out_shape=(jax.ShapeDtypeStruct((nb_seq * nb * LANES, d), BF16),
                   jax.ShapeDtypeStruct((IN_PAD, d), BF16),
                   jax.ShapeDtypeStruct((N_META, d), F32)),
        scratch_shapes=[
            pltpu.VMEM((SHARD_IN_PAD, d), BF16),
            pltpu.VMEM((N_META, SHARD_META), F32),
            pltpu.VMEM((N_DEV, SHARD_IN_PAD, d), BF16),
            pltpu.VMEM((N_DEV, N_META, SHARD_META), F32),
            pltpu.SemaphoreType.DMA((14,)),
            pltpu.SemaphoreType.DMA((14,)),
            pltpu.SemaphoreType.DMA((2,)),
        ],
        compiler_params=_params("arbitrary", "arbitrary"),
    )(x, meta, norm_g, w_in_t)


def _in_proj_gather(u, w_in_p, w_q, w_kv, w_out, conv_w, bm, bn):
    m, k_dim = u.shape
    n = w_in_p.shape[0]
    steps = (m // bm) * (n // bn)
    relay_step, forward_step = steps // 3, 2 * steps // 3
    qkv_shape = (SHARD_Q + KV_RANK, Q_RANK)

    def body(a_ref, b_ref, wq_ref, wkv_ref, wout_ref, conv_ref, o_ref, w_q_p, w_kv_p, w_out_f, conv_f,
             sqkv, sout, sconv, gqkv, gout, gconv, send_sems, recv_sems, local_sems):
        t = pl.program_id(0) * (n // bn) + pl.program_id(1)

        def plan():
            return _gather_plan((sqkv, sout, sconv), (gqkv, gout, gconv), send_sems, recv_sems, local_sems)

        @pl.when(t == 0)
        def _():
            sqkv[...] = jnp.zeros_like(sqkv)
            sqkv[0:SHARD_Q, :] = wq_ref[...].astype(BF16)
            sqkv[SHARD_Q:, 0:SHARD_KV] = wkv_ref[...].astype(BF16)
            sout[...] = wout_ref[...].astype(BF16)
            sconv[...] = jnp.zeros_like(sconv)
            sconv[0:3, 0:SHARD_CONV] = conv_ref[...]
            plan()[0]()

        @pl.when(t == relay_step)
        def _():
            plan()[1]()

        @pl.when(t == forward_step)
        def _():
            plan()[2]()

        o_ref[...] = _dot(a_ref[...], b_ref[...], _NT).astype(o_ref.dtype)

        @pl.when(t == steps - 1)
        def _():
            plan()[3]()
            conv_f[...] = jnp.zeros_like(conv_f)
            for k in range(N_DEV):
                for s0, e0, d0 in _q_pieces(k):
                    w_q_p[d0:d0 + e0 - s0, :] = gqkv[k, s0:e0, :]
                w_kv_p[:, _kv_dst(k):_kv_dst(k) + SHARD_KV] = gqkv[k, SHARD_Q:, 0:SHARD_KV]
                w_out_f[SHARD_OUT * k:SHARD_OUT * (k + 1), :] = gout[k]
                conv_f[0:3, SHARD_CONV * k:SHARD_CONV * (k + 1)] = gconv[k, 0:3, 0:SHARD_CONV]

    whole = lambda shape: pl.BlockSpec(shape, lambda i, j: (0,) * len(shape))
    return pl.pallas_call(
        body,
        name="in_proj_gather",
        grid=(m // bm, n // bn),
        in_specs=[pl.BlockSpec((bm, k_dim), lambda i, j: (i, 0)), pl.BlockSpec((bn, k_dim), lambda i, j: (j, 0)),
                  whole(w_q.shape), whole(w_kv.shape), whole(w_out.shape), whole(conv_w.shape)],
        out_specs=(pl.BlockSpec((bm, bn), lambda i, j: (i, j)),
                   whole((Q_COLS, Q_RANK)), whole((KV_RANK, KV_COLS)), whole((D_MODEL, D_MODEL)),
                   whole((8, CONV_WIDTH))),
        out_shape=(jax.ShapeDtypeStruct((m, n), BF16),
                   jax.ShapeDtypeStruct((Q_COLS, Q_RANK), BF16),
                   jax.ShapeDtypeStruct((KV_RANK, KV_COLS), BF16),
                   jax.ShapeDtypeStruct((D_MODEL, D_MODEL), BF16),
                   jax.ShapeDtypeStruct((8, CONV_WIDTH), F32)),
        scratch_shapes=[
            pltpu.VMEM(qkv_shape, BF16),
            pltpu.VMEM((SHARD_OUT, D_MODEL), BF16),
            pltpu.VMEM((8, LANES), F32),
            pltpu.VMEM((N_DEV,) + qkv_shape, BF16),
            pltpu.VMEM((N_DEV, SHARD_OUT, D_MODEL), BF16),
            pltpu.VMEM((N_DEV, 8, LANES), F32),
            pltpu.SemaphoreType.DMA((21,)),
            pltpu.SemaphoreType.DMA((21,)),
            pltpu.SemaphoreType.DMA((3,)),
        ],
        compiler_params=_params("arbitrary", "arbitrary"),
    )(u, w_in_p, w_q, w_kv, w_out, conv_w)


def _rope_tables(tp):
    half = D_ROPE // 2
    inv_freq = 1.0 / (ROPE_THETA ** (jnp.arange(half, dtype=F32) / half))
    pos = (jnp.arange(tp) - PAD_FRONT).astype(F32)
    ang = pos[:, None] * inv_freq[None, :]
    cos = jnp.tile(jnp.cos(ang), (1, LANES // half))
    sin = jnp.tile(jnp.sin(ang), (1, LANES // half))
    first = (jnp.arange(LANES) % D_ROPE) < half
    return cos, jnp.where(first, -sin, 0.0), jnp.where(first, 0.0, sin)


def _rope(t, cos, sa, sb):
    return t * cos + pltpu.roll(t, LANES - D_ROPE // 2, 1) * sa + pltpu.roll(t, D_ROPE // 2, 1) * sb


def _rope_t(t, cos, sa, sb):
    return t * cos + pltpu.roll(t * sa, D_ROPE // 2, 1) + pltpu.roll(t * sb, LANES - D_ROPE // 2, 1)


def _qkv_fwd(p, wq, wkv, gq, gkv, tables, nb_seq, tp):
    ht = tp // 2

    def body(pa_ref, wq_ref, wkv_ref, gq_ref, gkv_ref, cos_ref, sa_ref, sb_ref, q_ref, k_ref, v_ref):
        pa = pa_ref[...].astype(F32)
        cq_hat, _ = _rms_stats(pa[:, :Q_RANK])
        ckv_hat, _ = _rms_stats(pa[:, Q_RANK:Q_RANK + KV_RANK])
        q = _dot((cq_hat * gq_ref[...]).astype(BF16), wq_ref[...], _NT) * Q_SCALE
        kv = _dot((ckv_hat * gkv_ref[...]).astype(BF16), wkv_ref[...])
        tabs = (cos_ref[...], sa_ref[...], sb_ref[...])
        lane = lax.broadcasted_iota(jnp.int32, (ht, LANES), 1)
        low = lane < D_ROPE
        mark = lane == D_ROPE
        row = (pl.program_id(0) % 2) * ht + lax.broadcasted_iota(jnp.int32, (ht, LANES), 0)
        k_pe = jnp.where(mark & (row < PAD_FRONT), NEG_INF, _rope(pa[:, Q_RANK + KV_RANK:], *tabs))
        one = jnp.where(mark & (row >= PAD_FRONT), 1.0, 0.0)
        pairs = [_rope(q[:, N_HEADS * D_NOPE + LANES * i:N_HEADS * D_NOPE + LANES * (i + 1)], *tabs) for i in range(2)]
        for h in range(N_HEADS):
            pair = pairs[h // 2]
            if h % 2:
                pair = pltpu.roll(pair, D_ROPE, 1)
            pe = jnp.where(low, pair, one)
            q_ref[0, h] = jnp.concatenate([q[:, D_NOPE * h:D_NOPE * (h + 1)], pe], axis=1).astype(BF16)
            k_ref[0, h] = jnp.concatenate([kv[:, D_NOPE * h:D_NOPE * (h + 1)], k_pe], axis=1).astype(BF16)
            v_ref[0, h] = kv[:, N_HEADS * D_NOPE + D_V * h:N_HEADS * D_NOPE + D_V * (h + 1)].astype(BF16)

    full = lambda a: pl.BlockSpec(a.shape, lambda i: (0,) * a.ndim)
    tab = pl.BlockSpec((ht, LANES), lambda i: (i % 2, 0))
    qk = pl.BlockSpec((1, N_HEADS, ht, 2 * LANES), lambda i: (i // 2, 0, i % 2, 0))
    return pl.pallas_call(
        body,
        name="qkv_fwd",
        grid=(2 * nb_seq,),
        in_specs=[pl.BlockSpec((ht, GRP_A), lambda i: (i, 0)), full(wq), full(wkv), full(gq), full(gkv), tab, tab, tab],
        out_specs=(qk, qk, pl.BlockSpec((1, N_HEADS, ht, D_V), lambda i: (i // 2, 0, i % 2, 0))),
        out_shape=(
            jax.ShapeDtypeStruct((nb_seq, N_HEADS, tp, 2 * LANES), BF16),
            jax.ShapeDtypeStruct((nb_seq, N_HEADS, tp, 2 * LANES), BF16),
            jax.ShapeDtypeStruct((nb_seq, N_HEADS, tp, D_V), BF16),
        ),
        compiler_params=_params("parallel"),
    )(p, wq, wkv, gq, gkv, *tables)


def _attn_fwd(q, k, v, p, g_attn):
    nb_seq, _, tp, _ = q.shape

    def body(q_ref, k_ref, v_ref, z_ref, g_ref, y_ref, o_ref, lse_ref):
        g = g_ref[...]
        for r0 in range(0, tp, Q_TILE):
            nq = min(Q_TILE, tp - r0)
            kend = r0 + nq
            qq = q_ref[0, 0, r0:kend, :]
            sd = _dot(qq, k_ref[0, 0, r0:kend, :], _NT)
            causal = (lax.broadcasted_iota(jnp.int32, (nq, nq), 1) <= lax.broadcasted_iota(jnp.int32, (nq, nq), 0))
            sd = jnp.where(causal, sd, NEG_INF)
            m = jnp.max(sd, axis=-1, keepdims=True)
            if r0:
                so = _dot(qq, k_ref[0, 0, 0:r0, :], _NT)
                m = jnp.maximum(m, jnp.max(so, axis=-1, keepdims=True))
            ed = jnp.exp2(sd - m)
            l = jnp.sum(ed, axis=-1, keepdims=True)
            o = _dot(ed.astype(BF16), v_ref[0, 0, r0:kend, :])
            if r0:
                eo = jnp.exp2(so - m)
                l = l + jnp.sum(eo, axis=-1, keepdims=True)
                o = o + _dot(eo.astype(BF16), v_ref[0, 0, 0:r0, :])
            o = o * (1.0 / l)
            o_ref[0, 0, r0:kend, :] = o
            lse_ref[0, 0, r0:kend, :] = jnp.broadcast_to(m + jnp.log2(l), (nq, LANES))
            ohat, _ = _rms_stats(o)
            z = z_ref[r0:kend, :].astype(F32)
            y_ref[r0:kend, :] = (ohat * g * (z * _sigmoid(z))).astype(BF16)

    qk = pl.BlockSpec((1, 1, tp, 2 * LANES), lambda b, h: (b, h, 0, 0))
    hv = pl.BlockSpec((1, 1, tp, D_V), lambda b, h: (b, h, 0, 0))
    return pl.pallas_call(
        body,
        name="attn_fwd",
        grid=(nb_seq, N_HEADS),
        in_specs=[qk, qk, hv,
                  pl.BlockSpec((tp, LANES), lambda b, h: (b, GRP_A // LANES + h)),
                  pl.BlockSpec((1, LANES), lambda b, h: (0, h))],
        out_specs=(pl.BlockSpec((tp, LANES), lambda b, h: (b, h)), hv, hv),
        out_shape=(
            jax.ShapeDtypeStruct((nb_seq * tp, N_HEADS * D_V), BF16),
            jax.ShapeDtypeStruct((nb_seq, N_HEADS, tp, D_V), F32),
            jax.ShapeDtypeStruct((nb_seq, N_HEADS, tp, LANES), F32),
        ),
        compiler_params=_params("parallel", "parallel"),
    )(q, k, v, p, g_attn)


_CONV_COL0 = (GRP_A + N_HEADS * D_V) // LANES


def _conv_specs(tp, order):
    cols = CONV_WIDTH // LANES
    return [pl.BlockSpec((tp, LANES), functools.partial(
        lambda a, b, off: order(a, b, off), off=_CONV_COL0 + i * cols)) for i in range(4)]


def _conv_fwd(p, conv_w, g_conv, nb_seq, tp):
    def body(b_ref, c_ref, h_ref, z_ref, w_ref, g_ref, y_ref):
        cc = c_ref[...].astype(F32) * h_ref[...].astype(F32)
        row = lax.broadcasted_iota(jnp.int32, (tp, LANES), 0)
        s1 = jnp.where(row >= 1, pltpu.roll(cc, 1, 0), 0.0)
        s2 = jnp.where(row >= 2, pltpu.roll(cc, 2, 0), 0.0)
        yc = b_ref[...].astype(F32) * (w_ref[0:1, :] * s2 + w_ref[1:2, :] * s1 + w_ref[2:3, :] * cc)
        r = lax.rsqrt(_group_mean(yc * yc) + EPS)
        z = z_ref[...].astype(F32)
        y_ref[...] = (yc * r * g_ref[...] * (z * _sigmoid(z))).astype(BF16)

    return pl.pallas_call(
        body,
        name="conv_fwd",
        grid=(nb_seq, CONV_WIDTH // LANES),
        in_specs=_conv_specs(tp, lambda b, t, off: (b, off + t)) + [
            pl.BlockSpec((8, LANES), lambda b, t: (0, t)),
            pl.BlockSpec((1, LANES), lambda b, t: (0, t))],
        out_specs=pl.BlockSpec((tp, LANES), lambda b, t: (b, t)),
        out_shape=jax.ShapeDtypeStruct((nb_seq * tp, CONV_WIDTH), BF16),
        compiler_params=_params("parallel", "parallel"),
    )(p, p, p, p, conv_w, g_conv)


def _token_copy(hbm, b, k, ts, buf, sem, to_hbm=False):
    lo, hi = max(k * ts - LANES, 0), (k + 1) * ts - LANES
    off = lo - (k * ts - LANES)
    src, dst = hbm.at[b, pl.ds(lo, hi - lo)], buf.at[pl.ds(off, hi - lo)]
    if to_hbm:
        src, dst = dst, src
    return pltpu.make_async_copy(src, dst, sem)


def _for_tile(k, nt, fn):
    for kk in range(nt):
        @pl.when(k == kk)
        def _(kk=kk):
            fn(kk)


def _out_proj_loss(ya, yc, w_out, x, target, g_final, nt):
    nb_seq, s, d = x.shape
    r, ka = ya.shape
    ts = (s + LANES) // nt
    steps = nb_seq * nt

    def body(a_ref, c_ref, w_ref, x_hbm, t_hbm, g_ref, dh_ref, dhb_ref, dg_ref, loss_ref,
             xbuf, tbuf, acc_ref, sems):
        i = pl.program_id(0)
        b, k = i // nt, i % nt

        @pl.when(i == 0)
        def _():
            acc_ref[...] = jnp.zeros_like(acc_ref)
            dg_ref[...] = jnp.zeros_like(dg_ref)

        slot = i % 2

        def fetch(seq, kk, sl):
            return [_token_copy(x_hbm, seq, kk, ts, xbuf.at[sl], sems.at[sl, 0]),
                    _token_copy(t_hbm, seq, kk, ts, tbuf.at[sl], sems.at[sl, 1])]

        def start(seq, sl, kk):
            if kk == 0:
                xbuf[sl, 0:LANES, :] = jnp.zeros((LANES, d), F32)
                tbuf[sl, 0:LANES, :] = jnp.zeros((LANES, d), F32)
            for cp in fetch(seq, kk, sl):
                cp.start()

        @pl.when(i == 0)
        def _():
            start(0, 0, 0)

        @pl.when(i + 1 < steps)
        def _():
            _for_tile((i + 1) % nt, nt, functools.partial(start, (i + 1) // nt, 1 - slot))

        mix = _dot(a_ref[...], w_ref[0:ka, :]) + _dot(c_ref[...], w_ref[ka:, :])
        _for_tile(k, nt, lambda kk: [cp.wait() for cp in fetch(b, kk, slot)])

        real = (lax.broadcasted_iota(jnp.int32, (ts, d), 0) >= LANES) | (k > 0)
        g = g_ref[...]
        hhat, rstd = _rms_stats(xbuf[slot] + mix)
        e = jnp.where(real, hhat * g - tbuf[slot], 0.0)
        acc_ref[...] += jnp.sum(e * e, axis=0, keepdims=True)
        dy = e * (1.0 / d)
        dg_ref[...] += jnp.sum(dy * hhat, axis=0, keepdims=True)
        dh = _rms_bwd(g * dy, hhat, rstd)
        dh_ref[...] = dh
        dhb_ref[...] = dh.astype(BF16)

        @pl.when(i == steps - 1)
        def _():
            total = jnp.sum(acc_ref[...], axis=1, keepdims=True)
            loss_ref[...] = jnp.broadcast_to((0.5 / d) * total, loss_ref.shape)

    hbm = pl.BlockSpec(memory_space=pl.ANY)
    row = pl.BlockSpec((ts, d), lambda i: (i, 0))
    vec = pl.BlockSpec((1, d), lambda i: (0, 0))
    return pl.pallas_call(
        body,
        name="out_proj_loss",
        grid=(steps,),
        in_specs=[pl.BlockSpec((ts, ka), lambda i: (i, 0)), pl.BlockSpec((ts, yc.shape[1]), lambda i: (i, 0)),
                  pl.BlockSpec(w_out.shape, lambda i: (0, 0)), hbm, hbm, vec],
        out_specs=(row, row, vec, pl.BlockSpec((1, LANES), lambda i: (0, 0))),
        out_shape=(
            jax.ShapeDtypeStruct((r, d), F32),
            jax.ShapeDtypeStruct((r, d), BF16),
            jax.ShapeDtypeStruct((1, d), F32),
            jax.ShapeDtypeStruct((1, LANES), F32),
        ),
        scratch_shapes=[pltpu.VMEM((2, ts, d), F32), pltpu.VMEM((2, ts, d), F32), pltpu.VMEM((1, d), F32),
                        pltpu.SemaphoreType.DMA((2, 2))],
        compiler_params=_params("arbitrary"),
    )(ya, yc, w_out, x, target, g_final)


def _out_proj_bwd(dhb, w_out, ya, yc, bm):
    r, d = dhb.shape
    ka = ya.shape[1]
    n_mix = w_out.shape[0]
    last = r // bm - 1

    def body(dh_ref, w_ref, a_ref, c_ref, dcat_ref, dw_ref, acc_ref):
        @pl.when(pl.program_id(0) == 0)
        def _():
            acc_ref[...] = jnp.zeros_like(acc_ref)

        dh = dh_ref[...]
        dcat_ref[...] = _dot(dh, w_ref[...], _NT).astype(BF16)
        acc_ref[0:ka, :] += _dot(a_ref[...], dh, _TN)
        acc_ref[ka:, :] += _dot(c_ref[...], dh, _TN)

        @pl.when(pl.program_id(0) == last)
        def _():
            dw_ref[...] = acc_ref[...].astype(BF16)

    return pl.pallas_call(
        body,
        name="out_proj_bwd",
        grid=(r // bm,),
        in_specs=[pl.BlockSpec((bm, d), lambda i: (i, 0)), pl.BlockSpec(w_out.shape, lambda i: (0, 0)),
                  pl.BlockSpec((bm, ka), lambda i: (i, 0)), pl.BlockSpec((bm, yc.shape[1]), lambda i: (i, 0))],
        out_specs=(pl.BlockSpec((bm, n_mix), lambda i: (i, 0)),
                   pl.BlockSpec((n_mix, d), lambda i: (0, 0))),
        out_shape=(jax.ShapeDtypeStruct((r, n_mix), BF16),
                   jax.ShapeDtypeStruct((n_mix, d), BF16)),
        scratch_shapes=[pltpu.VMEM((n_mix, d), F32)],
        compiler_params=_params("arbitrary"),
    )(dhb, w_out, ya, yc)


def _attn_bwd(q, k, v, o, lse, dcat, p, g_attn):
    nb_seq, _, tp, _ = q.shape

    def body(q_ref, k_ref, v_ref, o_ref, lse_ref, dy_ref, z_ref, g_ref,
             dq_ref, dk_ref, dv_ref, dz_ref, dg_ref, dq_acc):
        @pl.when(pl.program_id(1) == 0)
        def _():
            dg_ref[...] = jnp.zeros_like(dg_ref)

        g = g_ref[...]
        z = z_ref[...].astype(F32)
        o = o_ref[0, 0]
        dy = dy_ref[...].astype(F32)
        sig = _sigmoid(z)
        ohat, r = _rms_stats(o)
        don = dy * (z * sig)
        dz_ref[...] = (dy * (ohat * g) * (sig * (1.0 + z * (1.0 - sig)))).astype(BF16)
        dg_ref[...] += jnp.sum(don * ohat, axis=0, keepdims=True)
        do = _rms_bwd(g * don, ohat, r)
        dvec = jnp.sum(do * o, axis=-1, keepdims=True)
        dob = do.astype(BF16)
        lse_col = lse_ref[0, 0, :, 0:1]
        dq_acc[...] = jnp.zeros_like(dq_acc)
        for k0 in range(0, tp, K_TILE):
            nk = min(K_TILE, tp - k0)
            nq = tp - k0
            qq = q_ref[0, 0, k0:, :]
            kk = k_ref[0, 0, k0:k0 + nk, :]
            causal = (lax.broadcasted_iota(jnp.int32, (nq, nk), 1) <= lax.broadcasted_iota(jnp.int32, (nq, nk), 0))
            pr = jnp.where(causal, jnp.exp2(_dot(qq, kk, _NT) - lse_col[k0:]), 0.0)
            dp = _dot(dob[k0:], v_ref[0, 0, k0:k0 + nk, :], _NT)
            ds = (pr * (dp - dvec[k0:])).astype(BF16)
            dv_ref[0, 0, k0:k0 + nk, :] = _dot(pr.astype(BF16), dob[k0:], _TN).astype(BF16)
            dk_ref[0, 0, k0:k0 + nk, :] = (_dot(ds, qq, _TN) * (ATTN_SCALE / Q_SCALE)).astype(BF16)
            dq_acc[k0:, :] += _dot(ds, kk)
        dq_ref[0, 0] = (dq_acc[...] * ATTN_SCALE).astype(BF16)

    qk = pl.BlockSpec((1, 1, tp, 2 * LANES), lambda h, b: (b, h, 0, 0))
    hv = pl.BlockSpec((1, 1, tp, D_V), lambda h, b: (b, h, 0, 0))
    col = pl.BlockSpec((tp, LANES), lambda h, b: (b, h))
    return pl.pallas_call(
        body,
        name="attn_bwd",
        grid=(N_HEADS, nb_seq),
        in_specs=[qk, qk, hv, hv, hv, col,
                  pl.BlockSpec((tp, LANES), lambda h, b: (b, GRP_A // LANES + h)),
                  pl.BlockSpec((1, LANES), lambda h, b: (0, h))],
        out_specs=(qk, qk, hv, col, pl.BlockSpec((1, LANES), lambda h, b: (0, h))),
        out_shape=(
            jax.ShapeDtypeStruct((nb_seq, N_HEADS, tp, 2 * LANES), BF16),
            jax.ShapeDtypeStruct((nb_seq, N_HEADS, tp, 2 * LANES), BF16),
            jax.ShapeDtypeStruct((nb_seq, N_HEADS, tp, D_V), BF16),
            jax.ShapeDtypeStruct((nb_seq * tp, N_HEADS * D_V), BF16),
            jax.ShapeDtypeStruct((1, N_HEADS * D_V), F32),
        ),
        scratch_shapes=[pltpu.VMEM((tp, 2 * LANES), F32)],
        compiler_params=_params("arbitrary", "arbitrary"),
    )(q, k, v, o, lse, dcat, p, g_attn)


def _qkv_bwd(p, dq, dk, dv, wq, wkv, gq, gkv, tables):
    nb_seq, _, tp, _ = dq.shape
    ht = tp // 2

    def body(pa_ref, dq_ref, dk_ref, dv_ref, wq_ref, wkv_ref, gq_ref, gkv_ref, cos_ref, sa_ref, sb_ref,
             dpa_ref, dwq_ref, dwkv_ref, dgq_ref, dgkv_ref):
        @pl.when(pl.program_id(0) == 0)
        def _():
            dwq_ref[...] = jnp.zeros_like(dwq_ref)
            dwkv_ref[...] = jnp.zeros_like(dwkv_ref)
            dgq_ref[...] = jnp.zeros_like(dgq_ref)
            dgkv_ref[...] = jnp.zeros_like(dgkv_ref)

        pa = pa_ref[...].astype(F32)
        gq, gkv = gq_ref[...], gkv_ref[...]
        cq_hat, rq = _rms_stats(pa[:, :Q_RANK])
        ckv_hat, rkv = _rms_stats(pa[:, Q_RANK:Q_RANK + KV_RANK])
        tabs = (cos_ref[...], sa_ref[...], sb_ref[...])

        pe = [dq_ref[0, h, :, D_NOPE:].astype(F32) for h in range(N_HEADS)]
        pairs = [_rope_t(pe[2 * i] + pltpu.roll(pe[2 * i + 1], D_ROPE, 1), *tabs).astype(BF16) for i in range(2)]
        dq_flat = jnp.concatenate([dq_ref[0, h, :, :D_NOPE] for h in range(N_HEADS)] + pairs, axis=1)
        dwq_ref[...] += _dot(dq_flat, (cq_hat * gq).astype(BF16), _TN)
        dcqn = _dot(dq_flat, wq_ref[...])
        dgq_ref[...] += jnp.sum(dcqn * cq_hat, axis=0, keepdims=True)
        dcq = _rms_bwd(gq * dcqn, cq_hat, rq)

        dkv_flat = jnp.concatenate([dk_ref[0, h, :, :D_NOPE] for h in range(N_HEADS)]
                                   + [dv_ref[0, h] for h in range(N_HEADS)], axis=1)
        dwkv_ref[...] += _dot((ckv_hat * gkv).astype(BF16), dkv_flat, _TN)
        dckvn = _dot(dkv_flat, wkv_ref[...], _NT)
        dgkv_ref[...] += jnp.sum(dckvn * ckv_hat, axis=0, keepdims=True)
        dckv = _rms_bwd(gkv * dckvn, ckv_hat, rkv)

        dk_pe = dk_ref[0, 0, :, D_NOPE:].astype(F32)
        for h in range(1, N_HEADS):
            dk_pe = dk_pe + dk_ref[0, h, :, D_NOPE:].astype(F32)
        dk_pe = jnp.where(lax.broadcasted_iota(jnp.int32, (ht, LANES), 1) < D_ROPE, dk_pe, 0.0)
        dpa_ref[...] = jnp.concatenate([dcq, dckv, _rope_t(dk_pe, *tabs)], axis=1).astype(BF16)

    full = lambda a: pl.BlockSpec(a.shape, lambda i: (0,) * a.ndim)
    tab = pl.BlockSpec((ht, LANES), lambda i: (i % 2, 0))
    qk = pl.BlockSpec((1, N_HEADS, ht, 2 * LANES), lambda i: (i // 2, 0, i % 2, 0))
    acc = lambda shape: pl.BlockSpec(shape, lambda i: (0, 0))
    return pl.pallas_call(
        body,
        name="qkv_bwd",
        grid=(2 * nb_seq,),
        in_specs=[pl.BlockSpec((ht, GRP_A), lambda i: (i, 0)), qk, qk,
                  pl.BlockSpec((1, N_HEADS, ht, D_V), lambda i: (i // 2, 0, i % 2, 0)),
                  full(wq), full(wkv), full(gq), full(gkv), tab, tab, tab],
        out_specs=(pl.BlockSpec((ht, GRP_A), lambda i: (i, 0)),
                   acc(wq.shape), acc(wkv.shape), acc((1, Q_RANK)), acc((1, KV_RANK))),
        out_shape=(
            jax.ShapeDtypeStruct((nb_seq * tp, GRP_A), BF16),
            jax.ShapeDtypeStruct(wq.shape, F32),
            jax.ShapeDtypeStruct(wkv.shape, F32),
            jax.ShapeDtypeStruct((1, Q_RANK), F32),
            jax.ShapeDtypeStruct((1, KV_RANK), F32),
        ),
        compiler_params=_params("arbitrary"),
    )(p, dq, dk, dv, wq, wkv, gq, gkv, *tables)


def _conv_bwd(p, dcat, conv_w, g_conv, nb_seq, tp):
    cols = CONV_WIDTH // LANES

    def body(b_ref, c_ref, h_ref, z_ref, dy_ref, w_ref, g_ref,
             db_ref, dc_ref, dh_ref, dz_ref, dw_ref, dg_ref):
        @pl.when(pl.program_id(1) == 0)
        def _():
            dw_ref[...] = jnp.zeros_like(dw_ref)
            dg_ref[...] = jnp.zeros_like(dg_ref)

        cb, c, h = b_ref[...].astype(F32), c_ref[...].astype(F32), h_ref[...].astype(F32)
        z, dy = z_ref[...].astype(F32), dy_ref[...].astype(F32)
        g = g_ref[...]
        w0, w1, w2 = w_ref[0:1, :], w_ref[1:2, :], w_ref[2:3, :]
        cc = c * h
        row = lax.broadcasted_iota(jnp.int32, (tp, LANES), 0)
        s1 = jnp.where(row >= 1, pltpu.roll(cc, 1, 0), 0.0)
        s2 = jnp.where(row >= 2, pltpu.roll(cc, 2, 0), 0.0)
        dwc = w0 * s2 + w1 * s1 + w2 * cc
        yc = cb * dwc
        r = lax.rsqrt(_group_mean(yc * yc) + EPS)
        ychat = yc * r
        sig = _sigmoid(z)
        dz_ref[...] = (dy * (ychat * g) * (sig * (1.0 + z * (1.0 - sig)))).astype(BF16)
        dyn = dy * (z * sig)
        dg_ref[...] += jnp.sum(dyn * ychat, axis=0, keepdims=True)
        gd = g * dyn
        dyc = r * (gd - ychat * _group_mean(gd * ychat))
        db_ref[...] = (dyc * dwc).astype(BF16)
        ddw = dyc * cb
        dw_ref[0:1, :] += jnp.sum(ddw * s2, axis=0, keepdims=True)
        dw_ref[1:2, :] += jnp.sum(ddw * s1, axis=0, keepdims=True)
        dw_ref[2:3, :] += jnp.sum(ddw * cc, axis=0, keepdims=True)
        u1 = jnp.where(row <= tp - 2, pltpu.roll(ddw, tp - 1, 0), 0.0)
        u2 = jnp.where(row <= tp - 3, pltpu.roll(ddw, tp - 2, 0), 0.0)
        dcc = w2 * ddw + w1 * u1 + w0 * u2
        dc_ref[...] = (dcc * h).astype(BF16)
        dh_ref[...] = (dcc * c).astype(BF16)

    col = pl.BlockSpec((tp, LANES), lambda t, b: (b, t))
    out = jax.ShapeDtypeStruct((nb_seq * tp, CONV_WIDTH), BF16)
    return pl.pallas_call(
        body,
        name="conv_bwd",
        grid=(cols, nb_seq),
        in_specs=_conv_specs(tp, lambda t, b, off: (b, off + t)) + [
            pl.BlockSpec((tp, LANES), lambda t, b: (b, N_HEADS * D_V // LANES + t)),
            pl.BlockSpec((8, LANES), lambda t, b: (0, t)),
            pl.BlockSpec((1, LANES), lambda t, b: (0, t))],
        out_specs=(col, col, col, col,
                   pl.BlockSpec((8, LANES), lambda t, b: (0, t)), pl.BlockSpec((1, LANES), lambda t, b: (0, t))),
        out_shape=(out, out, out, out,
                   jax.ShapeDtypeStruct((8, CONV_WIDTH), F32), jax.ShapeDtypeStruct((1, CONV_WIDTH), F32)),
        compiler_params=_params("arbitrary", "arbitrary"),
    )(p, p, p, p, dcat, conv_w, g_conv)


def _input_bwd(dps, w_in, x, meta, dh, norm_g, nt, send_in):
    nb_seq, s, d = x.shape
    r, kb = dps[0].shape
    ts = (s + LANES) // nt
    steps = nb_seq * nt
    n_dp = len(dps)
    in_slot = send_in.shape[1:]

    def body(*refs):
        dp_refs, w_ref, x_hbm, meta_ref, dh_ref, g_ref, pay_ref = refs[:n_dp], *refs[n_dp:n_dp + 6]
        o = n_dp + 6
        gx_hbm, dmeta_ref, dg_ref, r2_in = refs[o:o + 4]
        xbuf, gxbuf, tok_sems, own_in, r1_in, sum_in = refs[o + 4:o + 10]
        sems = refs[o + 10:]
        i = pl.program_id(0)
        b, k = i // nt, i % nt

        def plan():
            return _reduce_plan((pay_ref,), (own_in,), (r1_in,), (sum_in,), (r2_in,), *sems)

        @pl.when(i == 0)
        def _():
            dmeta_ref[...] = jnp.zeros_like(dmeta_ref)
            dg_ref[...] = jnp.zeros_like(dg_ref)
            plan()[0]()

        @pl.when(i == 1)
        def _():
            plan()[1]()

        def start(kk):
            if kk == 0:
                xbuf[0:PAD_FRONT, :] = jnp.zeros((PAD_FRONT, d), F32)
                xbuf[PAD_FRONT:LANES, :] = meta_ref[...]
            _token_copy(x_hbm, b, kk, ts, xbuf, tok_sems.at[0]).start()

        _for_tile(k, nt, start)
        du = _dot(dp_refs[0][...], w_ref[0:kb, :])
        for j in range(1, n_dp):
            du = du + _dot(dp_refs[j][...], w_ref[kb * j:kb * (j + 1), :])
        _for_tile(k, nt, lambda kk: _token_copy(x_hbm, b, kk, ts, xbuf, tok_sems.at[0]).wait())

        g = g_ref[...]
        hhat, rstd = _rms_stats(xbuf[...])
        dg_ref[...] += jnp.sum(du * hhat, axis=0, keepdims=True)
        res = _rms_bwd(g * du, hhat, rstd) + dh_ref[...]

        @pl.when(i > 0)
        def _():
            _for_tile(k, nt, lambda kk: _token_copy(gx_hbm, b, (kk - 1) % nt, ts, gxbuf, tok_sems.at[1], True).wait())

        gxbuf[...] = res

        @pl.when(k == 0)
        def _():
            dmeta_ref[...] += gxbuf[PAD_FRONT:LANES, :]

        _for_tile(k, nt, lambda kk: _token_copy(gx_hbm, b, kk, ts, gxbuf, tok_sems.at[1], True).start())

        @pl.when(i == steps - 1)
        def _():
            _token_copy(gx_hbm, b, nt - 1, ts, gxbuf, tok_sems.at[1], True).wait()
            plan()[2]()

    whole = lambda a: pl.BlockSpec(a.shape, lambda i: (0,) * a.ndim)
    hbm = pl.BlockSpec(memory_space=pl.ANY)
    return pl.pallas_call(
        body,
        name="input_bwd",
        grid=(steps,),
        in_specs=[pl.BlockSpec((ts, kb), lambda i: (i, 0)) for _ in dps]
        + [whole(w_in), hbm, whole(meta), pl.BlockSpec((ts, d), lambda i: (i, 0)), whole(norm_g), hbm],
        out_specs=(hbm, pl.BlockSpec((N_META, d), lambda i: (0, 0)), pl.BlockSpec((1, d), lambda i: (0, 0)), hbm),
        out_shape=(jax.ShapeDtypeStruct((nb_seq, s, d), F32),
                   jax.ShapeDtypeStruct((N_META, d), F32),
                   jax.ShapeDtypeStruct((1, d), F32),
                   jax.ShapeDtypeStruct((N_CHIPS,) + in_slot, BF16)),
        scratch_shapes=[pltpu.VMEM((ts, d), F32), pltpu.VMEM((ts, d), F32), pltpu.SemaphoreType.DMA((2,))]
        + _reduce_scratch([(in_slot, BF16)], [True]),
        compiler_params=_params("arbitrary"),
    )(*dps, w_in, x, meta, dh, norm_g, send_in)


def _in_proj_bwd_w(u, dps, bm, small_grads, send_out):
    r, d = u.shape
    kb = dps[0].shape[1]
    steps = r // bm
    n_dp, n_small = len(dps), len(small_grads)
    out_slot, small_slot = send_out.shape[1:], (SMALL_ROWS, LANES)

    def body(*refs):
        u_ref, dp_refs = refs[0], refs[1:1 + n_dp]
        small_refs = refs[1 + n_dp:1 + n_dp + n_small]
        o = 1 + n_dp + n_small
        pay_out, o_ref, r2_out, r2_small = refs[o:o + 4]
        acc_ref, ssmall, r1_out, sum_out, r1_small, sum_small = refs[o + 4:o + 10]
        sems = refs[o + 10:]
        i = pl.program_id(0)

        def plan():
            return _reduce_plan((pay_out, ssmall), (None, None), (r1_out, r1_small), (sum_out, sum_small),
                                (r2_out, r2_small), *sems)

        @pl.when(i == 0)
        def _():
            acc_ref[...] = jnp.zeros_like(acc_ref)
            _pack_small(ssmall, *small_refs)
            plan()[0]()

        @pl.when(i == 1)
        def _():
            plan()[1]()

        uu = u_ref[...]
        for j in range(n_dp):
            acc_ref[kb * j:kb * (j + 1), :] += _dot(dp_refs[j][...], uu, _TN)

        @pl.when(i == steps - 1)
        def _():
            for k in range(N_DEV):
                for s, e, c0 in _in_pieces(k):
                    o_ref[k, s:e, :] = acc_ref[c0:c0 + e - s, :].astype(BF16)
                o_ref[k, SHARD_IN:, :] = jnp.zeros((SHARD_IN_PAD - SHARD_IN, d), BF16)
            plan()[2]()

    whole = lambda a: pl.BlockSpec(a.shape, lambda i: (0,) * a.ndim)
    hbm = pl.BlockSpec(memory_space=pl.ANY)
    return pl.pallas_call(
        body,
        name="in_proj_bwd_w",
        grid=(steps,),
        in_specs=[pl.BlockSpec((bm, d), lambda i: (i, 0))]
        + [pl.BlockSpec((bm, kb), lambda i: (i, 0)) for _ in dps] + [whole(a) for a in small_grads]
        + [whole(send_out)],
        out_specs=(pl.BlockSpec((N_DEV, SHARD_IN_PAD, d), lambda i: (0, 0, 0)), hbm, hbm),
        out_shape=(jax.ShapeDtypeStruct((N_DEV, SHARD_IN_PAD, d), BF16),
                   jax.ShapeDtypeStruct((N_CHIPS,) + out_slot, BF16),
                   jax.ShapeDtypeStruct((N_CHIPS,) + small_slot, F32)),
        scratch_shapes=[pltpu.VMEM((kb * n_dp, d), F32), pltpu.VMEM((N_DEV,) + small_slot, F32)]
        + _reduce_scratch([(out_slot, BF16), (small_slot, F32)], [False, False]),
        compiler_params=_params("arbitrary"),
    )(u, *dps, *small_grads, send_out)


def _local_step(x, loss_target, u, p, meta_f, norm_g, w_in_p, q_norm_g, w_q_p, kv_norm_g, w_kv_p, conv_w_f,
                attn_out_g, conv_out_g, w_out_f, g_final):
    nb_seq, s, d = x.shape
    tp = s + LANES
    ht = tp // 2
    tables = _rope_tables(tp)

    q, k, v = _qkv_fwd(p, w_q_p, w_kv_p, q_norm_g, kv_norm_g, tables, nb_seq, tp)
    ya, o, lse = _attn_fwd(q, k, v, p, attn_out_g)
    yc = _conv_fwd(p, conv_w_f, conv_out_g, nb_seq, tp)
    dh, dhb, d_final_g, loss_part = _out_proj_loss(ya, yc, w_out_f, x, loss_target, g_final, TOKEN_TILES)

    dcat, d_w_out = _out_proj_bwd(dhb, w_out_f, ya, yc, ht)
    send_out = d_w_out.reshape(N_DEV, SHARD_OUT, d)
    dq, dk, dv, dz_attn, d_attn_g = _attn_bwd(q, k, v, o, lse, dcat, p, attn_out_g)
    dpa, d_wq_p, d_wkv_p, d_gq, d_gkv = _qkv_bwd(p, dq, dk, dv, w_q_p, w_kv_p, q_norm_g, kv_norm_g, tables)
    d_b, d_c, d_h, dz_conv, d_conv_w, d_conv_g = _conv_bwd(p, dcat, conv_w_f, conv_out_g, nb_seq, tp)
    dps = (dpa, dz_attn, d_b, d_c, d_h, dz_conv)
    small = (d_wq_p, d_wkv_p, d_conv_w, d_final_g, d_gq, d_gkv, d_attn_g, d_conv_g, loss_part)
    send_in, r_out, r_small = _in_proj_bwd_w(u, dps, ht // 2, small, send_out)
    grad_x, d_meta, d_norm_g, r_in = _input_bwd(dps, w_in_p, x, meta_f, dh, norm_g, TOKEN_TILES, send_in)
    return grad_x, r_in, r_out, r_small, d_meta, d_norm_g


def kernel(x, meta_tokens, norm_g, w_in, q_norm_g, w_q_up, kv_norm_g, w_kv_up, conv_w, attn_out_g, conv_out_g, w_out, final_norm_g, loss_target, m_meta_tokens, m_norm_g, m_w_in, m_q_norm_g, m_w_q_up, m_kv_norm_g, m_w_kv_up, m_conv_w, m_attn_out_g, m_conv_out_g, m_w_out, m_final_norm_g, v_meta_tokens, v_norm_g, v_w_in, v_q_norm_g, v_w_q_up, v_kv_norm_g, v_w_kv_up, v_conv_w, v_attn_out_g, v_conv_out_g, v_w_out, v_final_norm_g):
    d = x.shape[-1]
    ht = (x.shape[1] + LANES) // 2
    u, w_in_p, meta_f = _prep_gather(x, meta_tokens, norm_g, w_in[0].T)
    p, w_q_p, w_kv_p, w_out_f, conv_w_f = _in_proj_gather(
        u, w_in_p, w_q_up[0].T, w_kv_up[0], w_out[0], conv_w[0], ht, 3 * GRP_A)
    g_final = final_norm_g.reshape(1, d)
    grad_x, r_in, r_out, r_small, d_meta, d_norm_g = _local_step(
        x, loss_target, u, p, meta_f, norm_g, w_in_p, q_norm_g, w_q_p, kv_norm_g, w_kv_p, conv_w_f,
        attn_out_g, conv_out_g, w_out_f, g_final)

    flat = lambda a: a.reshape(a.shape[-2:]) if a.ndim == 3 else a.reshape(1, -1) if a.ndim == 1 else a
    transposed = ("w_in", "w_q_up")
    to_kernel = lambda n, a: flat(a).T if n in transposed else flat(a)
    from_kernel = lambda n, a, shape: (a.T if n in transposed else a).reshape(shape)
    params = {
        "meta_tokens": (meta_tokens, m_meta_tokens, v_meta_tokens),
        "norm_g": (norm_g, m_norm_g, v_norm_g),
        "w_in": (w_in, m_w_in, v_w_in),
        "q_norm_g": (q_norm_g, m_q_norm_g, v_q_norm_g),
        "w_q_up": (w_q_up, m_w_q_up, v_w_q_up),
        "kv_norm_g": (kv_norm_g, m_kv_norm_g, v_kv_norm_g),
        "w_kv_up": (w_kv_up, m_w_kv_up, v_w_kv_up),
        "conv_w": (conv_w, m_conv_w, v_conv_w),
        "attn_out_g": (attn_out_g, m_attn_out_g, v_attn_out_g),
        "conv_out_g": (conv_out_g, m_conv_out_g, v_conv_out_g),
        "w_out": (w_out, m_w_out, v_w_out),
        "final_norm_g": (final_norm_g, m_final_norm_g, v_final_norm_g),
    }
    grads, loss = _reduce_tail(r_in, r_out, r_small, d_meta, d_norm_g)
    updated = _adamw(grads, {n: tuple(to_kernel(n, a) for a in t) for n, t in params.items()})
    outs = [[from_kernel(n, updated[n][i], params[n][0].shape) for n, _ in PARAM_SHAPES] for i in range(4)]
    return (loss[0, 0], grad_x, *outs[0], *outs[1], *outs[2], *outs[3])
```

```python
import functools

import jax
import jax.numpy as jnp
from jax import lax
from jax.experimental import pallas as pl
from jax.experimental.pallas import tpu as pltpu

F32 = jnp.float32
BF16 = jnp.bfloat16

N_META = 16
D_MODEL = 1024
N_HEADS = 4
D_NOPE = 128
D_ROPE = 64
D_V = 128
Q_RANK = 256
KV_RANK = 128
CONV_WIDTH = 512
CONV_GROUP = 64
ROPE_THETA = 10000.0
ATTN_SCALE = (D_NOPE + D_ROPE) ** -0.5
Q_SCALE = ATTN_SCALE * 1.4426950408889634
EPS = 1e-6
NEG_INF = -1e30

ADAM_LR = 0.001
ADAM_B1 = 0.9
ADAM_B2 = 0.999
ADAM_EPS = 1e-08
ADAM_WD = 0.01
ADAM_STEP = 10

LANES = 128
PAD_FRONT = LANES - N_META
K_TILE = 256
Q_TILE = 512
N_DEV = 8
VMEM_LIMIT = 56 * 1024 * 1024

IN_PAD = 3072
GRP_A = 512
N_A = Q_RANK + KV_RANK + D_ROPE
IN_PROJ = 3008
SHARD_IN = IN_PROJ // N_DEV
SHARD_IN_PAD = 384
SHARD_Q = 96
SHARD_KV = 128
SHARD_OUT = 128
SHARD_CONV = 64
SHARD_META = 128
Q_COLS = N_HEADS * (D_NOPE + D_ROPE)
KV_COLS = N_HEADS * (D_NOPE + D_V)

ROW_Q, ROW_KV, ROW_META, ROW_CONV = 0, 256, 384, 400
ROW_REPL = 408
ROW_NORM, ROW_FINAL, ROW_GQ, ROW_GKV, ROW_ATTN, ROW_CONVG, ROW_LOSS = 408, 416, 424, 426, 427, 431, 435
SMALL_ROWS = 440

PARAM_SHAPES = (
    ("meta_tokens", (N_META, SHARD_META)), ("norm_g", (1, D_MODEL)), ("w_in", (SHARD_IN, D_MODEL)),
    ("q_norm_g", (1, Q_RANK)), ("w_q_up", (SHARD_Q, Q_RANK)), ("kv_norm_g", (1, KV_RANK)),
    ("w_kv_up", (KV_RANK, SHARD_KV)), ("conv_w", (3, SHARD_CONV)), ("attn_out_g", (1, CONV_WIDTH)),
    ("conv_out_g", (1, CONV_WIDTH)), ("w_out", (SHARD_OUT, D_MODEL)), ("final_norm_g", (1, D_MODEL)),
)


def _in_pieces(k):
    lo, hi = SHARD_IN * k, SHARD_IN * (k + 1)
    out = []
    if lo < N_A:
        out.append((0, min(hi, N_A) - lo, lo))
    if hi > N_A:
        s = max(lo, N_A)
        out.append((s - lo, hi - lo, s + GRP_A - N_A))
    return out


def _q_pieces(k):
    lo, hi = SHARD_Q * k, SHARD_Q * (k + 1)
    out = []
    for h in range(N_HEADS):
        base = (D_NOPE + D_ROPE) * h
        s, e = max(lo, base), min(hi, base + D_NOPE)
        if s < e:
            out.append((s - lo, e - lo, D_NOPE * h + s - base))
        s, e = max(lo, base + D_NOPE), min(hi, base + D_NOPE + D_ROPE)
        if s < e:
            out.append((s - lo, e - lo, N_HEADS * D_NOPE + D_ROPE * h + s - base - D_NOPE))
    return out


def _kv_dst(k):
    return D_NOPE * (k // 2) + (N_HEADS * D_NOPE if k % 2 else 0)


def _params(*sem):
    return pltpu.CompilerParams(dimension_semantics=sem, vmem_limit_bytes=VMEM_LIMIT)


def _rms_stats(x):
    r = lax.rsqrt(jnp.mean(x * x, axis=-1, keepdims=True) + EPS)
    return x * r, r


def _rms_bwd(gdy, xhat, r):
    return r * (gdy - xhat * jnp.mean(gdy * xhat, axis=-1, keepdims=True))


def _sigmoid(z):
    return 1.0 / (1.0 + jnp.exp(-z))


def _group_mean(x):
    i0 = lax.broadcasted_iota(jnp.int32, (LANES, LANES), 0) // CONV_GROUP
    i1 = lax.broadcasted_iota(jnp.int32, (LANES, LANES), 1) // CONV_GROUP
    m = jnp.where(i0 == i1, 1.0 / CONV_GROUP, 0.0).astype(BF16)
    hi = x.astype(BF16)
    lo = (x - hi.astype(F32)).astype(BF16)
    return jnp.dot(hi, m, preferred_element_type=F32) + jnp.dot(lo, m, preferred_element_type=F32)


_NT = (((1,), (1,)), ((), ()))
_TN = (((0,), (0,)), ((), ()))


def _dot(a, b, dims=None):
    if dims is None:
        return jnp.dot(a, b, preferred_element_type=F32)
    return lax.dot_general(a, b, dims, preferred_element_type=F32)


def _device_position():
    x, y, c = lax.axis_index("x"), lax.axis_index("y"), lax.axis_index("c")
    return x, y, c, 4 * x + 2 * y + c


def _gather_plan(srcs, slots, send_sems, recv_sems, local_sems):
    x, y, c, _ = _device_position()
    me, sibling = (x, y, c), (x, y, 1 - c)
    flip = lambda v, on: v + on - 2 * v * on
    near = (flip(x, 1 - c), flip(y, c))
    far = (flip(x, c), flip(y, 1 - c))
    diag = (1 - x, 1 - y)
    n = len(srcs)

    def slot(a, px, py, pc):
        return slots[a].at[4 * px + 2 * py + pc]

    def copy(a, k, block, to, own=False):
        return pltpu.make_async_remote_copy(
            src_ref=srcs[a] if own else slot(a, *block),
            dst_ref=slot(a, *block),
            send_sem=send_sems.at[7 * a + k],
            recv_sem=recv_sems.at[7 * a + k],
            device_id=to,
            device_id_type=pl.DeviceIdType.MESH,
        )

    def local(a):
        return pltpu.make_async_copy(srcs[a], slot(a, *me), local_sems.at[a])

    sent = [(me, sibling), (me, (*near, c)), (me, (*far, c)), ((*near, c), (*far, c)),
            ((*near, c), sibling), ((*far, c), sibling), ((*diag, c), sibling)]
    landed = [sibling, (*near, c), (*far, c), (*diag, c), (*far, 1 - c), (*near, 1 - c), (*diag, 1 - c)]

    def send(a, k):
        return copy(a, k, *sent[k], own=k < 3)

    def arrival(a, k):
        return copy(a, k, landed[k], me)

    def start():
        for a in range(n):
            local(a).start()
            for k in range(3):
                send(a, k).start()

    def relay():
        for a in range(n):
            arrival(a, 1).wait_recv()
            send(a, 3).start()
            send(a, 4).start()

    def forward():
        for k in (2, 3):
            for a in range(n):
                arrival(a, k).wait_recv()
                send(a, k + 3).start()

    def finish():
        for a in range(n):
            for k in (0, 4, 5, 6):
                arrival(a, k).wait_recv()
        for a in range(n):
            for k in range(7):
                send(a, k).wait_send()
            local(a).wait()

    return start, relay, forward, finish


def _adam_update(g, w, m, v):
    m_new = ADAM_B1 * m + (1.0 - ADAM_B1) * g
    v_new = ADAM_B2 * v + (1.0 - ADAM_B2) * (g * g)
    m_hat = m_new / (1.0 - ADAM_B1 ** ADAM_STEP)
    v_hat = v_new / (1.0 - ADAM_B2 ** ADAM_STEP)
    return -ADAM_LR * (m_hat / (jnp.sqrt(v_hat) + ADAM_EPS) + ADAM_WD * w), m_new, v_new


def _adamw(grads, params):
    names = [n for n, _ in PARAM_SHAPES]
    n_p = len(names)

    def body(*refs):
        for i in range(n_p):
            g = refs[i][...]
            w, m, v = (refs[n_p + 3 * i + j][...] for j in range(3))
            delta, m_new, v_new = _adam_update(g, w, m, v)
            for j, val in enumerate((g, delta, m_new, v_new)):
                refs[4 * n_p + 4 * i + j][...] = val

    vm = pl.BlockSpec(memory_space=pltpu.VMEM)
    out_shape = []
    for _, shape in PARAM_SHAPES:
        out_shape += [jax.ShapeDtypeStruct(shape, F32)] * 4
    outs = pl.pallas_call(
        body,
        name="adamw",
        out_shape=tuple(out_shape),
        in_specs=[vm] * (4 * n_p),
        out_specs=(vm,) * (4 * n_p),
        compiler_params=pltpu.CompilerParams(vmem_limit_bytes=VMEM_LIMIT),
    )(*[grads[n] for n in names], *[a for n in names for a in params[n]])
    return {n: outs[4 * i:4 * i + 4] for i, n in enumerate(names)}


N_CHIPS = 4


def _reduce_plan(pays, owns, r1s, sums, r2s, send1, recv1, send2, recv2, local_sems):
    x, y, c, _ = _device_position()
    sibling = (x, y, 1 - c)
    chips = [((1 - x if rj & 2 else x), (1 - y if rj & 1 else y)) for rj in range(N_CHIPS)]
    n = len(pays)

    def slot_of(rj, core):
        return 4 * chips[rj][0] + 2 * chips[rj][1] + core

    def to_sibling(a, rj):
        return pltpu.make_async_remote_copy(
            src_ref=pays[a].at[slot_of(rj, 1 - c)], dst_ref=r1s[a].at[rj],
            send_sem=send1.at[N_CHIPS * a + rj], recv_sem=recv1.at[N_CHIPS * a + rj],
            device_id=sibling, device_id_type=pl.DeviceIdType.MESH)

    def load_own(a, rj):
        return pltpu.make_async_copy(pays[a].at[slot_of(rj, c)], owns[a].at[rj], local_sems.at[2 * N_CHIPS * a + rj])

    def to_chip(a, rj):
        return pltpu.make_async_remote_copy(
            src_ref=sums[a].at[rj], dst_ref=r2s[a].at[rj],
            send_sem=send2.at[N_CHIPS * a + rj], recv_sem=recv2.at[N_CHIPS * a + rj],
            device_id=(*chips[rj], c), device_id_type=pl.DeviceIdType.MESH)

    def keep(a):
        return pltpu.make_async_copy(sums[a].at[0], r2s[a].at[0], local_sems.at[2 * N_CHIPS * a + N_CHIPS])

    def start():
        for a in range(n):
            for rj in range(N_CHIPS):
                to_sibling(a, rj).start()
                if owns[a] is not None:
                    load_own(a, rj).start()

    def combine():
        for a in range(n):
            for rj in range(N_CHIPS):
                to_sibling(a, rj).wait_recv()
                if owns[a] is not None:
                    load_own(a, rj).wait()
                    mine = owns[a][rj]
                else:
                    mine = pays[a][slot_of(rj, c)]
                sums[a][rj] = (mine.astype(F32) + r1s[a][rj].astype(F32)).astype(sums[a].dtype)
            keep(a).start()
            for rj in range(1, N_CHIPS):
                to_chip(a, rj).start()

    def finish():
        for a in range(n):
            for rj in range(1, N_CHIPS):
                to_chip(a, rj).wait_recv()
            for rj in range(N_CHIPS):
                to_sibling(a, rj).wait_send()
            for rj in range(1, N_CHIPS):
                to_chip(a, rj).wait_send()
            keep(a).wait()

    return start, combine, finish


def _reduce_scratch(shapes_dtypes, own_flags):
    out = []
    for (shape, dtype), own in zip(shapes_dtypes, own_flags):
        if own:
            out.append(pltpu.VMEM((N_CHIPS,) + shape, dtype))
        out += [pltpu.VMEM((N_CHIPS,) + shape, dtype), pltpu.VMEM((N_CHIPS,) + shape, dtype)]
    n = len(shapes_dtypes)
    out += [pltpu.SemaphoreType.DMA((N_CHIPS * n,))] * 4 + [pltpu.SemaphoreType.DMA((2 * N_CHIPS * n,))]
    return out


def _pack_small(ssmall, dwq, dwkv, dconv, dfinal, dgq, dgkv, dattn, dconvg, loss_part):
    ssmall[...] = jnp.zeros_like(ssmall)
    rep = ssmall.at[0]
    for i in range(D_MODEL // LANES):
        rep[ROW_FINAL + i:ROW_FINAL + i + 1, :] = dfinal[:, LANES * i:LANES * (i + 1)]
    for i in range(Q_RANK // LANES):
        rep[ROW_GQ + i:ROW_GQ + i + 1, :] = dgq[:, LANES * i:LANES * (i + 1)]
    rep[ROW_GKV:ROW_GKV + 1, :] = dgkv[...]
    for i in range(CONV_WIDTH // LANES):
        rep[ROW_ATTN + i:ROW_ATTN + i + 1, :] = dattn[:, LANES * i:LANES * (i + 1)]
        rep[ROW_CONVG + i:ROW_CONVG + i + 1, :] = dconvg[:, LANES * i:LANES * (i + 1)]
    rep[ROW_LOSS:ROW_LOSS + 1, :] = loss_part[...]
    for k in range(N_DEV):
        if k:
            ssmall[k, ROW_REPL:, :] = ssmall[0, ROW_REPL:, :]
        for s, e, d in _q_pieces(k):
            for i in range(Q_RANK // LANES):
                ssmall[k, ROW_Q + SHARD_Q * i + s:ROW_Q + SHARD_Q * i + e, :] = dwq[d:d + e - s, LANES * i:LANES * (i + 1)]
        ssmall[k, ROW_KV:ROW_KV + KV_RANK, :] = dwkv[:, _kv_dst(k):_kv_dst(k) + SHARD_KV]
        ssmall[k, ROW_CONV:ROW_CONV + 3, 0:SHARD_CONV] = dconv[0:3, SHARD_CONV * k:SHARD_CONV * (k + 1)]


TOKEN_TILES = 4
TAIL_ROWS = N_META + D_MODEL // LANES


def _reduce_tail(r_in, r_out, r_small, d_meta, d_norm):
    n_p = len(PARAM_SHAPES)
    names = [n for n, _ in PARAM_SHAPES]

    def body(*refs):
        rin, rout, rsmall, dmeta, dnorm = refs[:5]
        g_out = {n: refs[5 + i] for i, n in enumerate(names)}
        loss_out = refs[5 + n_p]
        stail, rtail, gsum, gtail, send_sems, recv_sems = refs[6 + n_p:]
        x, y, c, me = _device_position()
        my_chip = 2 * x + y

        for k in range(N_DEV):
            stail[k, 0:N_META, :] = dmeta[:, SHARD_META * k:SHARD_META * (k + 1)]
            for i in range(D_MODEL // LANES):
                stail[k, N_META + i:N_META + i + 1, :] = dnorm[:, LANES * i:LANES * (i + 1)]
        copies = []
        for r in range(1, N_DEV):
            peer = (1 - x if r & 4 else x, 1 - y if r & 2 else y, 1 - c if r & 1 else c)
            copies.append(pltpu.make_async_remote_copy(
                src_ref=stail.at[4 * peer[0] + 2 * peer[1] + peer[2]],
                dst_ref=rtail.at[r],
                send_sem=send_sems.at[r - 1],
                recv_sem=recv_sems.at[r - 1],
                device_id=peer,
                device_id_type=pl.DeviceIdType.MESH,
            ))
        for cp in copies:
            cp.start()
        rtail[0] = stail[me]

        g = rin[my_chip].astype(F32)
        for ch in range(1, N_CHIPS):
            g = g + rin[ch ^ my_chip].astype(F32)
        g_out["w_in"][...] = g[:SHARD_IN, :]

        g = rout[my_chip].astype(F32)
        gs = rsmall[my_chip]
        for ch in range(1, N_CHIPS):
            g = g + rout[ch ^ my_chip].astype(F32)
            gs = gs + rsmall[ch ^ my_chip]
        g_out["w_out"][...] = g
        gsum[...] = gs
        for i in range(Q_RANK // LANES):
            g_out["w_q_up"][:, LANES * i:LANES * (i + 1)] = gsum[ROW_Q + SHARD_Q * i:ROW_Q + SHARD_Q * (i + 1), :]
        g_out["w_kv_up"][...] = gsum[ROW_KV:ROW_KV + KV_RANK, :]
        g_out["conv_w"][...] = gsum[ROW_CONV:ROW_CONV + 3, 0:SHARD_CONV]
        for name, row, width in (("final_norm_g", ROW_FINAL, D_MODEL), ("q_norm_g", ROW_GQ, Q_RANK),
                                 ("kv_norm_g", ROW_GKV, KV_RANK), ("attn_out_g", ROW_ATTN, CONV_WIDTH),
                                 ("conv_out_g", ROW_CONVG, CONV_WIDTH)):
            for i in range(width // LANES):
                g_out[name][:, LANES * i:LANES * (i + 1)] = gsum[row + i:row + i + 1, :]
        loss_out[...] = gsum[ROW_LOSS:ROW_LOSS + 1, :]

        for cp in copies:
            cp.wait_recv()
        gt = rtail[me]
        for d in range(1, N_DEV):
            gt = gt + rtail[d ^ me]
        gtail[...] = gt
        g_out["meta_tokens"][...] = gtail[0:N_META, :]
        for i in range(D_MODEL // LANES):
            g_out["norm_g"][:, LANES * i:LANES * (i + 1)] = gtail[N_META + i:N_META + i + 1, :]
        for cp in copies:
            cp.wait_send()

    vm = pl.BlockSpec(memory_space=pltpu.VMEM)
    out_shape = [jax.ShapeDtypeStruct(shape, F32) for _, shape in PARAM_SHAPES]
    out_shape.append(jax.ShapeDtypeStruct((1, LANES), F32))
    outs = pl.pallas_call(
        body,
        name="reduce_tail",
        out_shape=tuple(out_shape),
        in_specs=[vm] * 5,
        out_specs=(vm,) * len(out_shape),
        scratch_shapes=[
            pltpu.VMEM((N_DEV, TAIL_ROWS, LANES), F32),
            pltpu.VMEM((N_DEV, TAIL_ROWS, LANES), F32),
            pltpu.VMEM((SMALL_ROWS, LANES), F32),
            pltpu.VMEM((TAIL_ROWS, LANES), F32),
            pltpu.SemaphoreType.DMA((N_DEV - 1,)),
            pltpu.SemaphoreType.DMA((N_DEV - 1,)),
        ],
        compiler_params=pltpu.CompilerParams(vmem_limit_bytes=VMEM_LIMIT),
    )(r_in, r_out, r_small, d_meta, d_norm)
    return {n: outs[i] for i, n in enumerate(names)}, outs[-1]


def _prep_gather(x, meta, norm_g, w_in_t):
    nb_seq, s, d = x.shape
    nb = s // LANES + 1
    relay_step = nb_seq * (nb - 1) // 2
    forward_step = nb_seq * (nb - 1) - 1
    finish_step = nb_seq * (nb - 1)

    def body(x_ref, meta_ref, g_ref, win_ref, u_ref, w_in_p, meta_f,
             sbig, ssmall, gbig, gsmall, send_sems, recv_sems, local_sems):
        jj, b = pl.program_id(0), pl.program_id(1)
        t = jj * nb_seq + b

        def plan():
            return _gather_plan((sbig, ssmall), (gbig, gsmall), send_sems, recv_sems, local_sems)

        @pl.when(t == 0)
        def _():
            sbig[0:SHARD_IN, :] = win_ref[...].astype(BF16)
            sbig[SHARD_IN:, :] = jnp.zeros((SHARD_IN_PAD - SHARD_IN, d), BF16)
            ssmall[...] = meta_ref[...]
            plan()[0]()

        @pl.when(t == relay_step)
        def _():
            plan()[1]()

        @pl.when(t == forward_step)
        def _():
            plan()[2]()

        @pl.when(t == finish_step)
        def _():
            plan()[3]()
            w_in_p[N_A:GRP_A, :] = jnp.zeros((GRP_A - N_A, d), BF16)
            for k in range(N_DEV):
                for s0, e0, d0 in _in_pieces(k):
                    w_in_p[d0:d0 + e0 - s0, :] = gbig[k, s0:e0, :]
                meta_f[:, SHARD_META * k:SHARD_META * (k + 1)] = gsmall[k]

        def norm(h):
            hhat, _ = _rms_stats(h)
            return (hhat * g_ref[...]).astype(BF16)

        @pl.when(jj < nb - 1)
        def _():
            u_ref[...] = norm(x_ref[0])

        @pl.when(jj == nb - 1)
        def _():
            u_ref[0:PAD_FRONT, :] = jnp.zeros((PAD_FRONT, d), BF16)
            u_ref[PAD_FRONT:LANES, :] = norm(meta_f[...])

    whole = lambda shape: pl.BlockSpec(shape, lambda jj, b: (0,) * len(shape))
    return pl.pallas_call(
        body,
        name="prep_norm_gather",
        grid=(nb, nb_seq),
        in_specs=[
            pl.BlockSpec((1, LANES, d), lambda jj, b: (b, jnp.minimum(jj, nb - 2), 0)),
            whole(meta.shape), whole(norm_g.shape), whole(w_in_t.shape),
        ],
        out_specs=(pl.BlockSpec((LANES, d), lambda jj, b: (b * nb + (jj + 1) % nb, 0)),
                   whole((IN_PAD, d)), whole((N_META, d))),
        out_shape=(jax.ShapeDtypeStruct((nb_seq * nb * LANES, d), BF16),
                   jax.ShapeDtypeStruct((IN_PAD, d), BF16),
                   jax.ShapeDtypeStruct((N_META, d), F32)),
        scratch_shapes=[
            pltpu.VMEM((SHARD_IN_PAD, d), BF16),
            pltpu.VMEM((N_META, SHARD_META), F32),
            pltpu.VMEM((N_DEV, SHARD_IN_PAD, d), BF16),
            pltpu.VMEM((N_DEV, N_META, SHARD_META), F32),
            pltpu.SemaphoreType.DMA((14,)),
            pltpu.SemaphoreType.DMA((14,)),
            pltpu.SemaphoreType.DMA((2,)),
        ],
        compiler_params=_params("arbitrary", "arbitrary"),
    )(x, meta, norm_g, w_in_t)


def _in_proj_gather(u, w_in_p, w_q, w_kv, w_out, conv_w, bm, bn):
    m, k_dim = u.shape
    n = w_in_p.shape[0]
    steps = (m // bm) * (n // bn)
    relay_step, forward_step = steps // 3, 2 * steps // 3
    qkv_shape = (SHARD_Q + KV_RANK, Q_RANK)

    def body(a_ref, b_ref, wq_ref, wkv_ref, wout_ref, conv_ref, o_ref, w_q_p, w_kv_p, w_out_f, conv_f,
             sqkv, sout, sconv, gqkv, gout, gconv, send_sems, recv_sems, local_sems):
        t = pl.program_id(0) * (n // bn) + pl.program_id(1)

        def plan():
            return _gather_plan((sqkv, sout, sconv), (gqkv, gout, gconv), send_sems, recv_sems, local_sems)

        @pl.when(t == 0)
        def _():
            sqkv[...] = jnp.zeros_like(sqkv)
            sqkv[0:SHARD_Q, :] = wq_ref[...].astype(BF16)
            sqkv[SHARD_Q:, 0:SHARD_KV] = wkv_ref[...].astype(BF16)
            sout[...] = wout_ref[...].astype(BF16)
            sconv[...] = jnp.zeros_like(sconv)
            sconv[0:3, 0:SHARD_CONV] = conv_ref[...]
            plan()[0]()

        @pl.when(t == relay_step)
        def _():
            plan()[1]()

        @pl.when(t == forward_step)
        def _():
            plan()[2]()

        o_ref[...] = _dot(a_ref[...], b_ref[...], _NT).astype(o_ref.dtype)

        @pl.when(t == steps - 1)
        def _():
            plan()[3]()
            conv_f[...] = jnp.zeros_like(conv_f)
            for k in range(N_DEV):
                for s0, e0, d0 in _q_pieces(k):
                    w_q_p[d0:d0 + e0 - s0, :] = gqkv[k, s0:e0, :]
                w_kv_p[:, _kv_dst(k):_kv_dst(k) + SHARD_KV] = gqkv[k, SHARD_Q:, 0:SHARD_KV]
                w_out_f[SHARD_OUT * k:SHARD_OUT * (k + 1), :] = gout[k]
                conv_f[0:3, SHARD_CONV * k:SHARD_CONV * (k + 1)] = gconv[k, 0:3, 0:SHARD_CONV]

    whole = lambda shape: pl.BlockSpec(shape, lambda i, j: (0,) * len(shape))
    return pl.pallas_call(
        body,
        name="in_proj_gather",
        grid=(m // bm, n // bn),
        in_specs=[pl.BlockSpec((bm, k_dim), lambda i, j: (i, 0)), pl.BlockSpec((bn, k_dim), lambda i, j: (j, 0)),
                  whole(w_q.shape), whole(w_kv.shape), whole(w_out.shape), whole(conv_w.shape)],
        out_specs=(pl.BlockSpec((bm, bn), lambda i, j: (i, j)),
                   whole((Q_COLS, Q_RANK)), whole((KV_RANK, KV_COLS)), whole((D_MODEL, D_MODEL)),
                   whole((8, CONV_WIDTH))),
        out_shape=(jax.ShapeDtypeStruct((m, n), BF16),
                   jax.ShapeDtypeStruct((Q_COLS, Q_RANK), BF16),
                   jax.ShapeDtypeStruct((KV_RANK, KV_COLS), BF16),
                   jax.ShapeDtypeStruct((D_MODEL, D_MODEL), BF16),
                   jax.ShapeDtypeStruct((8, CONV_WIDTH), F32)),
        scratch_shapes=[
            pltpu.VMEM(qkv_shape, BF16),
            pltpu.VMEM((SHARD_OUT, D_MODEL), BF16),
            pltpu.VMEM((8, LANES), F32),
            pltpu.VMEM((N_DEV,) + qkv_shape, BF16),
            pltpu.VMEM((N_DEV, SHARD_OUT, D_MODEL), BF16),
            pltpu.VMEM((N_DEV, 8, LANES), F32),
            pltpu.SemaphoreType.DMA((21,)),
            pltpu.SemaphoreType.DMA((21,)),
            pltpu.SemaphoreType.DMA((3,)),
        ],
        compiler_params=_params("arbitrary", "arbitrary"),
    )(u, w_in_p, w_q, w_kv, w_out, conv_w)


def _rope_tables(tp):
    half = D_ROPE // 2
    inv_freq = 1.0 / (ROPE_THETA ** (jnp.arange(half, dtype=F32) / half))
    pos = (jnp.arange(tp) - PAD_FRONT).astype(F32)
    ang = pos[:, None] * inv_freq[None, :]
    cos = jnp.tile(jnp.cos(ang), (1, LANES // half))
    sin = jnp.tile(jnp.sin(ang), (1, LANES // half))
    first = (jnp.arange(LANES) % D_ROPE) < half
    return cos, jnp.where(first, -sin, 0.0), jnp.where(first, 0.0, sin)


def _rope(t, cos, sa, sb):
    return t * cos + pltpu.roll(t, LANES - D_ROPE // 2, 1) * sa + pltpu.roll(t, D_ROPE // 2, 1) * sb


def _rope_t(t, cos, sa, sb):
    return t * cos + pltpu.roll(t * sa, D_ROPE // 2, 1) + pltpu.roll(t * sb, LANES - D_ROPE // 2, 1)


def _qkv_fwd(p, wq, wkv, gq, gkv, tables, nb_seq, tp):
    ht = tp // 2

    def body(pa_ref, wq_ref, wkv_ref, gq_ref, gkv_ref, cos_ref, sa_ref, sb_ref, q_ref, k_ref, v_ref):
        pa = pa_ref[...].astype(F32)
        cq_hat, _ = _rms_stats(pa[:, :Q_RANK])
        ckv_hat, _ = _rms_stats(pa[:, Q_RANK:Q_RANK + KV_RANK])
        q = _dot((cq_hat * gq_ref[...]).astype(BF16), wq_ref[...], _NT) * Q_SCALE
        kv = _dot((ckv_hat * gkv_ref[...]).astype(BF16), wkv_ref[...])
        tabs = (cos_ref[...], sa_ref[...], sb_ref[...])
        lane = lax.broadcasted_iota(jnp.int32, (ht, LANES), 1)
        low = lane < D_ROPE
        mark = lane == D_ROPE
        row = (pl.program_id(0) % 2) * ht + lax.broadcasted_iota(jnp.int32, (ht, LANES), 0)
        k_pe = jnp.where(mark & (row < PAD_FRONT), NEG_INF, _rope(pa[:, Q_RANK + KV_RANK:], *tabs))
        one = jnp.where(mark & (row >= PAD_FRONT), 1.0, 0.0)
        pairs = [_rope(q[:, N_HEADS * D_NOPE + LANES * i:N_HEADS * D_NOPE + LANES * (i + 1)], *tabs) for i in range(2)]
        for h in range(N_HEADS):
            pair = pairs[h // 2]
            if h % 2:
                pair = pltpu.roll(pair, D_ROPE, 1)
            pe = jnp.where(low, pair, one)
            q_ref[0, h] = jnp.concatenate([q[:, D_NOPE * h:D_NOPE * (h + 1)], pe], axis=1).astype(BF16)
            k_ref[0, h] = jnp.concatenate([kv[:, D_NOPE * h:D_NOPE * (h + 1)], k_pe], axis=1).astype(BF16)
            v_ref[0, h] = kv[:, N_HEADS * D_NOPE + D_V * h:N_HEADS * D_NOPE + D_V * (h + 1)].astype(BF16)

    full = lambda a: pl.BlockSpec(a.shape, lambda i: (0,) * a.ndim)
    tab = pl.BlockSpec((ht, LANES), lambda i: (i % 2, 0))
    qk = pl.BlockSpec((1, N_HEADS, ht, 2 * LANES), lambda i: (i // 2, 0, i % 2, 0))
    return pl.pallas_call(
        body,
        name="qkv_fwd",
        grid=(2 * nb_seq,),
        in_specs=[pl.BlockSpec((ht, GRP_A), lambda i: (i, 0)), full(wq), full(wkv), full(gq), full(gkv), tab, tab, tab],
        out_specs=(qk, qk, pl.BlockSpec((1, N_HEADS, ht, D_V), lambda i: (i // 2, 0, i % 2, 0))),
        out_shape=(
            jax.ShapeDtypeStruct((nb_seq, N_HEADS, tp, 2 * LANES), BF16),
            jax.ShapeDtypeStruct((nb_seq, N_HEADS, tp, 2 * LANES), BF16),
            jax.ShapeDtypeStruct((nb_seq, N_HEADS, tp, D_V), BF16),
        ),
        compiler_params=_params("parallel"),
    )(p, wq, wkv, gq, gkv, *tables)


def _attn_fwd(q, k, v, p, g_attn):
    nb_seq, _, tp, _ = q.shape

    def body(q_ref, k_ref, v_ref, z_ref, g_ref, y_ref, o_ref, lse_ref):
        g = g_ref[...]
        for r0 in range(0, tp, Q_TILE):
            nq = min(Q_TILE, tp - r0)
            kend = r0 + nq
            qq = q_ref[0, 0, r0:kend, :]
            sd = _dot(qq, k_ref[0, 0, r0:kend, :], _NT)
            causal = (lax.broadcasted_iota(jnp.int32, (nq, nq), 1) <= lax.broadcasted_iota(jnp.int32, (nq, nq), 0))
            sd = jnp.where(causal, sd, NEG_INF)
            m = jnp.max(sd, axis=-1, keepdims=True)
            if r0:
                so = _dot(qq, k_ref[0, 0, 0:r0, :], _NT)
                m = jnp.maximum(m, jnp.max(so, axis=-1, keepdims=True))
            ed = jnp.exp2(sd - m)
            l = jnp.sum(ed, axis=-1, keepdims=True)
            o = _dot(ed.astype(BF16), v_ref[0, 0, r0:kend, :])
            if r0:
                eo = jnp.exp2(so - m)
                l = l + jnp.sum(eo, axis=-1, keepdims=True)
                o = o + _dot(eo.astype(BF16), v_ref[0, 0, 0:r0, :])
            o = o * (1.0 / l)
            o_ref[0, 0, r0:kend, :] = o
            lse_ref[0, 0, r0:kend, :] = jnp.broadcast_to(m + jnp.log2(l), (nq, LANES))
            ohat, _ = _rms_stats(o)
            z = z_ref[r0:kend, :].astype(F32)
            y_ref[r0:kend, :] = (ohat * g * (z * _sigmoid(z))).astype(BF16)

    qk = pl.BlockSpec((1, 1, tp, 2 * LANES), lambda b, h: (b, h, 0, 0))
    hv = pl.BlockSpec((1, 1, tp, D_V), lambda b, h: (b, h, 0, 0))
    return pl.pallas_call(
        body,
        name="attn_fwd",
        grid=(nb_seq, N_HEADS),
        in_specs=[qk, qk, hv,
                  pl.BlockSpec((tp, LANES), lambda b, h: (b, GRP_A // LANES + h)),
                  pl.BlockSpec((1, LANES), lambda b, h: (0, h))],
        out_specs=(pl.BlockSpec((tp, LANES), lambda b, h: (b, h)), hv, hv),
        out_shape=(
            jax.ShapeDtypeStruct((nb_seq * tp, N_HEADS * D_V), BF16),
            jax.ShapeDtypeStruct((nb_seq, N_HEADS, tp, D_V), F32),
            jax.ShapeDtypeStruct((nb_seq, N_HEADS, tp, LANES), F32),
        ),
        compiler_params=_params("parallel", "parallel"),
    )(q, k, v, p, g_attn)


_CONV_COL0 = (GRP_A + N_HEADS * D_V) // LANES


def _conv_specs(tp, order):
    cols = CONV_WIDTH // LANES
    return [pl.BlockSpec((tp, LANES), functools.partial(
        lambda a, b, off: order(a, b, off), off=_CONV_COL0 + i * cols)) for i in range(4)]


def _conv_fwd(p, conv_w, g_conv, nb_seq, tp):
    def body(b_ref, c_ref, h_ref, z_ref, w_ref, g_ref, y_ref):
        cc = c_ref[...].astype(F32) * h_ref[...].astype(F32)
        row = lax.broadcasted_iota(jnp.int32, (tp, LANES), 0)
        s1 = jnp.where(row >= 1, pltpu.roll(cc, 1, 0), 0.0)
        s2 = jnp.where(row >= 2, pltpu.roll(cc, 2, 0), 0.0)
        yc = b_ref[...].astype(F32) * (w_ref[0:1, :] * s2 + w_ref[1:2, :] * s1 + w_ref[2:3, :] * cc)
        r = lax.rsqrt(_group_mean(yc * yc) + EPS)
        z = z_ref[...].astype(F32)
        y_ref[...] = (yc * r * g_ref[...] * (z * _sigmoid(z))).astype(BF16)

    return pl.pallas_call(
        body,
        name="conv_fwd",
        grid=(nb_seq, CONV_WIDTH // LANES),
        in_specs=_conv_specs(tp, lambda b, t, off: (b, off + t)) + [
            pl.BlockSpec((8, LANES), lambda b, t: (0, t)),
            pl.BlockSpec((1, LANES), lambda b, t: (0, t))],
        out_specs=pl.BlockSpec((tp, LANES), lambda b, t: (b, t)),
        out_shape=jax.ShapeDtypeStruct((nb_seq * tp, CONV_WIDTH), BF16),
        compiler_params=_params("parallel", "parallel"),
    )(p, p, p, p, conv_w, g_conv)


def _token_copy(hbm, b, k, ts, buf, sem, to_hbm=False):
    lo, hi = max(k * ts - LANES, 0), (k + 1) * ts - LANES
    off = lo - (k * ts - LANES)
    src, dst = hbm.at[b, pl.ds(lo, hi - lo)], buf.at[pl.ds(off, hi - lo)]
    if to_hbm:
        src, dst = dst, src
    return pltpu.make_async_copy(src, dst, sem)


def _for_tile(k, nt, fn):
    for kk in range(nt):
        @pl.when(k == kk)
        def _(kk=kk):
            fn(kk)


def _out_proj_loss(ya, yc, w_out, x, target, g_final, nt):
    nb_seq, s, d = x.shape
    r, ka = ya.shape
    ts = (s + LANES) // nt
    steps = nb_seq * nt

    def body(a_ref, c_ref, w_ref, x_hbm, t_hbm, g_ref, dhb_ref, dg_ref, loss_ref,
             xbuf, tbuf, acc_ref, sems):
        i = pl.program_id(0)
        b, k = i // nt, i % nt

        @pl.when(i == 0)
        def _():
            acc_ref[...] = jnp.zeros_like(acc_ref)
            dg_ref[...] = jnp.zeros_like(dg_ref)

        slot = i % 2

        def fetch(seq, kk, sl):
            return [_token_copy(x_hbm, seq, kk, ts, xbuf.at[sl], sems.at[sl, 0]),
                    _token_copy(t_hbm, seq, kk, ts, tbuf.at[sl], sems.at[sl, 1])]

        def start(seq, sl, kk):
            if kk == 0:
                xbuf[sl, 0:LANES, :] = jnp.zeros((LANES, d), F32)
                tbuf[sl, 0:LANES, :] = jnp.zeros((LANES, d), F32)
            for cp in fetch(seq, kk, sl):
                cp.start()

        @pl.when(i == 0)
        def _():
            start(0, 0, 0)

        @pl.when(i + 1 < steps)
        def _():
            _for_tile((i + 1) % nt, nt, functools.partial(start, (i + 1) // nt, 1 - slot))

        mix = _dot(a_ref[...], w_ref[0:ka, :]) + _dot(c_ref[...], w_ref[ka:, :])
        _for_tile(k, nt, lambda kk: [cp.wait() for cp in fetch(b, kk, slot)])

        real = (lax.broadcasted_iota(jnp.int32, (ts, d), 0) >= LANES) | (k > 0)
        g = g_ref[...]
        hhat, rstd = _rms_stats(xbuf[slot] + mix)
        e = jnp.where(real, hhat * g - tbuf[slot], 0.0)
        acc_ref[...] += jnp.sum(e * e, axis=0, keepdims=True)
        dy = e * (1.0 / d)
        dg_ref[...] += jnp.sum(dy * hhat, axis=0, keepdims=True)
        dh = _rms_bwd(g * dy, hhat, rstd)
        dhb_ref[...] = dh.astype(BF16)

        @pl.when(i == steps - 1)
        def _():
            total = jnp.sum(acc_ref[...], axis=1, keepdims=True)
            loss_ref[...] = jnp.broadcast_to((0.5 / d) * total, loss_ref.shape)

    hbm = pl.BlockSpec(memory_space=pl.ANY)
    row = pl.BlockSpec((ts, d), lambda i: (i, 0))
    vec = pl.BlockSpec((1, d), lambda i: (0, 0))
    return pl.pallas_call(
        body,
        name="out_proj_loss",
        grid=(steps,),
        in_specs=[pl.BlockSpec((ts, ka), lambda i: (i, 0)), pl.BlockSpec((ts, yc.shape[1]), lambda i: (i, 0)),
                  pl.BlockSpec(w_out.shape, lambda i: (0, 0)), hbm, hbm, vec],
        out_specs=(row, vec, pl.BlockSpec((1, LANES), lambda i: (0, 0))),
        out_shape=(
            jax.ShapeDtypeStruct((r, d), BF16),
            jax.ShapeDtypeStruct((1, d), F32),
            jax.ShapeDtypeStruct((1, LANES), F32),
        ),
        scratch_shapes=[pltpu.VMEM((2, ts, d), F32), pltpu.VMEM((2, ts, d), F32), pltpu.VMEM((1, d), F32),
                        pltpu.SemaphoreType.DMA((2, 2))],
        compiler_params=_params("arbitrary"),
    )(ya, yc, w_out, x, target, g_final)


def _out_proj_bwd(dhb, w_out, ya, yc, bm):
    r, d = dhb.shape
    ka = ya.shape[1]
    n_mix = w_out.shape[0]
    last = r // bm - 1

    def body(dh_ref, w_ref, a_ref, c_ref, dcat_ref, dw_ref, acc_ref):
        @pl.when(pl.program_id(0) == 0)
        def _():
            acc_ref[...] = jnp.zeros_like(acc_ref)

        dh = dh_ref[...]
        dcat_ref[...] = _dot(dh, w_ref[...], _NT).astype(BF16)
        acc_ref[0:ka, :] += _dot(a_ref[...], dh, _TN)
        acc_ref[ka:, :] += _dot(c_ref[...], dh, _TN)

        @pl.when(pl.program_id(0) == last)
        def _():
            dw_ref[...] = acc_ref[...].astype(BF16)

    return pl.pallas_call(
        body,
        name="out_proj_bwd",
        grid=(r // bm,),
        in_specs=[pl.BlockSpec((bm, d), lambda i: (i, 0)), pl.BlockSpec(w_out.shape, lambda i: (0, 0)),
                  pl.BlockSpec((bm, ka), lambda i: (i, 0)), pl.BlockSpec((bm, yc.shape[1]), lambda i: (i, 0))],
        out_specs=(pl.BlockSpec((bm, n_mix), lambda i: (i, 0)),
                   pl.BlockSpec((n_mix, d), lambda i: (0, 0))),
        out_shape=(jax.ShapeDtypeStruct((r, n_mix), BF16),
                   jax.ShapeDtypeStruct((n_mix, d), BF16)),
        scratch_shapes=[pltpu.VMEM((n_mix, d), F32)],
        compiler_params=_params("arbitrary"),
    )(dhb, w_out, ya, yc)


def _attn_bwd(q, k, v, o, lse, dcat, p, g_attn):
    nb_seq, _, tp, _ = q.shape

    def body(q_ref, k_ref, v_ref, o_ref, lse_ref, dy_ref, z_ref, g_ref,
             dq_ref, dk_ref, dv_ref, dz_ref, dg_ref, dq_acc):
        @pl.when(pl.program_id(1) == 0)
        def _():
            dg_ref[...] = jnp.zeros_like(dg_ref)

        g = g_ref[...]
        z = z_ref[...].astype(F32)
        o = o_ref[0, 0]
        dy = dy_ref[...].astype(F32)
        sig = _sigmoid(z)
        ohat, r = _rms_stats(o)
        don = dy * (z * sig)
        dz_ref[...] = (dy * (ohat * g) * (sig * (1.0 + z * (1.0 - sig)))).astype(BF16)
        dg_ref[...] += jnp.sum(don * ohat, axis=0, keepdims=True)
        do = _rms_bwd(g * don, ohat, r)
        dvec = jnp.sum(do * o, axis=-1, keepdims=True)
        dob = do.astype(BF16)
        lse_col = lse_ref[0, 0, :, 0:1]
        dq_acc[...] = jnp.zeros_like(dq_acc)
        for k0 in range(0, tp, K_TILE):
            nk = min(K_TILE, tp - k0)
            nq = tp - k0
            qq = q_ref[0, 0, k0:, :]
            kk = k_ref[0, 0, k0:k0 + nk, :]
            causal = (lax.broadcasted_iota(jnp.int32, (nq, nk), 1) <= lax.broadcasted_iota(jnp.int32, (nq, nk), 0))
            pr = jnp.where(causal, jnp.exp2(_dot(qq, kk, _NT) - lse_col[k0:]), 0.0)
            dp = _dot(dob[k0:], v_ref[0, 0, k0:k0 + nk, :], _NT)
            ds = (pr * (dp - dvec[k0:])).astype(BF16)
            dv_ref[0, 0, k0:k0 + nk, :] = _dot(pr.astype(BF16), dob[k0:], _TN).astype(BF16)
            dk_ref[0, 0, k0:k0 + nk, :] = (_dot(ds, qq, _TN) * (ATTN_SCALE / Q_SCALE)).astype(BF16)
            dq_acc[k0:, :] += _dot(ds, kk)
        dq_ref[0, 0] = (dq_acc[...] * ATTN_SCALE).astype(BF16)

    qk = pl.BlockSpec((1, 1, tp, 2 * LANES), lambda h, b: (b, h, 0, 0))
    hv = pl.BlockSpec((1, 1, tp, D_V), lambda h, b: (b, h, 0, 0))
    col = pl.BlockSpec((tp, LANES), lambda h, b: (b, h))
    return pl.pallas_call(
        body,
        name="attn_bwd",
        grid=(N_HEADS, nb_seq),
        in_specs=[qk, qk, hv, hv, hv, col,
                  pl.BlockSpec((tp, LANES), lambda h, b: (b, GRP_A // LANES + h)),
                  pl.BlockSpec((1, LANES), lambda h, b: (0, h))],
        out_specs=(qk, qk, hv, col, pl.BlockSpec((1, LANES), lambda h, b: (0, h))),
        out_shape=(
            jax.ShapeDtypeStruct((nb_seq, N_HEADS, tp, 2 * LANES), BF16),
            jax.ShapeDtypeStruct((nb_seq, N_HEADS, tp, 2 * LANES), BF16),
            jax.ShapeDtypeStruct((nb_seq, N_HEADS, tp, D_V), BF16),
            jax.ShapeDtypeStruct((nb_seq * tp, N_HEADS * D_V), BF16),
            jax.ShapeDtypeStruct((1, N_HEADS * D_V), F32),
        ),
        scratch_shapes=[pltpu.VMEM((tp, 2 * LANES), F32)],
        compiler_params=_params("arbitrary", "arbitrary"),
    )(q, k, v, o, lse, dcat, p, g_attn)


def _qkv_bwd(p, dq, dk, dv, wq, wkv, gq, gkv, tables):
    nb_seq, _, tp, _ = dq.shape
    ht = tp // 2

    def body(pa_ref, dq_ref, dk_ref, dv_ref, wq_ref, wkv_ref, gq_ref, gkv_ref, cos_ref, sa_ref, sb_ref,
             dpa_ref, dwq_ref, dwkv_ref, dgq_ref, dgkv_ref):
        @pl.when(pl.program_id(0) == 0)
        def _():
            dwq_ref[...] = jnp.zeros_like(dwq_ref)
            dwkv_ref[...] = jnp.zeros_like(dwkv_ref)
            dgq_ref[...] = jnp.zeros_like(dgq_ref)
            dgkv_ref[...] = jnp.zeros_like(dgkv_ref)

        pa = pa_ref[...].astype(F32)
        gq, gkv = gq_ref[...], gkv_ref[...]
        cq_hat, rq = _rms_stats(pa[:, :Q_RANK])
        ckv_hat, rkv = _rms_stats(pa[:, Q_RANK:Q_RANK + KV_RANK])
        tabs = (cos_ref[...], sa_ref[...], sb_ref[...])

        pe = [dq_ref[0, h, :, D_NOPE:].astype(F32) for h in range(N_HEADS)]
        pairs = [_rope_t(pe[2 * i] + pltpu.roll(pe[2 * i + 1], D_ROPE, 1), *tabs).astype(BF16) for i in range(2)]
        dq_flat = jnp.concatenate([dq_ref[0, h, :, :D_NOPE] for h in range(N_HEADS)] + pairs, axis=1)
        dwq_ref[...] += _dot(dq_flat, (cq_hat * gq).astype(BF16), _TN)
        dcqn = _dot(dq_flat, wq_ref[...])
        dgq_ref[...] += jnp.sum(dcqn * cq_hat, axis=0, keepdims=True)
        dcq = _rms_bwd(gq * dcqn, cq_hat, rq)

        dkv_flat = jnp.concatenate([dk_ref[0, h, :, :D_NOPE] for h in range(N_HEADS)]
                                   + [dv_ref[0, h] for h in range(N_HEADS)], axis=1)
        dwkv_ref[...] += _dot((ckv_hat * gkv).astype(BF16), dkv_flat, _TN)
        dckvn = _dot(dkv_flat, wkv_ref[...], _NT)
        dgkv_ref[...] += jnp.sum(dckvn * ckv_hat, axis=0, keepdims=True)
        dckv = _rms_bwd(gkv * dckvn, ckv_hat, rkv)

        dk_pe = dk_ref[0, 0, :, D_NOPE:].astype(F32)
        for h in range(1, N_HEADS):
            dk_pe = dk_pe + dk_ref[0, h, :, D_NOPE:].astype(F32)
        dk_pe = jnp.where(lax.broadcasted_iota(jnp.int32, (ht, LANES), 1) < D_ROPE, dk_pe, 0.0)
        dpa_ref[...] = jnp.concatenate([dcq, dckv, _rope_t(dk_pe, *tabs)], axis=1).astype(BF16)

    full = lambda a: pl.BlockSpec(a.shape, lambda i: (0,) * a.ndim)
    tab = pl.BlockSpec((ht, LANES), lambda i: (i % 2, 0))
    qk = pl.BlockSpec((1, N_HEADS, ht, 2 * LANES), lambda i: (i // 2, 0, i % 2, 0))
    acc = lambda shape: pl.BlockSpec(shape, lambda i: (0, 0))
    return pl.pallas_call(
        body,
        name="qkv_bwd",
        grid=(2 * nb_seq,),
        in_specs=[pl.BlockSpec((ht, GRP_A), lambda i: (i, 0)), qk, qk,
                  pl.BlockSpec((1, N_HEADS, ht, D_V), lambda i: (i // 2, 0, i % 2, 0)),
                  full(wq), full(wkv), full(gq), full(gkv), tab, tab, tab],
        out_specs=(pl.BlockSpec((ht, GRP_A), lambda i: (i, 0)),
                   acc(wq.shape), acc(wkv.shape), acc((1, Q_RANK)), acc((1, KV_RANK))),
        out_shape=(
            jax.ShapeDtypeStruct((nb_seq * tp, GRP_A), BF16),
            jax.ShapeDtypeStruct(wq.shape, F32),
            jax.ShapeDtypeStruct(wkv.shape, F32),
            jax.ShapeDtypeStruct((1, Q_RANK), F32),
            jax.ShapeDtypeStruct((1, KV_RANK), F32),
        ),
        compiler_params=_params("arbitrary"),
    )(p, dq, dk, dv, wq, wkv, gq, gkv, *tables)


def _conv_bwd(p, dcat, conv_w, g_conv, nb_seq, tp):
    cols = CONV_WIDTH // LANES

    def body(b_ref, c_ref, h_ref, z_ref, dy_ref, w_ref, g_ref,
             db_ref, dc_ref, dh_ref, dz_ref, dw_ref, dg_ref):
        @pl.when(pl.program_id(1) == 0)
        def _():
            dw_ref[...] = jnp.zeros_like(dw_ref)
            dg_ref[...] = jnp.zeros_like(dg_ref)

        cb, c, h = b_ref[...].astype(F32), c_ref[...].astype(F32), h_ref[...].astype(F32)
        z, dy = z_ref[...].astype(F32), dy_ref[...].astype(F32)
        g = g_ref[...]
        w0, w1, w2 = w_ref[0:1, :], w_ref[1:2, :], w_ref[2:3, :]
        cc = c * h
        row = lax.broadcasted_iota(jnp.int32, (tp, LANES), 0)
        s1 = jnp.where(row >= 1, pltpu.roll(cc, 1, 0), 0.0)
        s2 = jnp.where(row >= 2, pltpu.roll(cc, 2, 0), 0.0)
        dwc = w0 * s2 + w1 * s1 + w2 * cc
        yc = cb * dwc
        r = lax.rsqrt(_group_mean(yc * yc) + EPS)
        ychat = yc * r
        sig = _sigmoid(z)
        dz_ref[...] = (dy * (ychat * g) * (sig * (1.0 + z * (1.0 - sig)))).astype(BF16)
        dyn = dy * (z * sig)
        dg_ref[...] += jnp.sum(dyn * ychat, axis=0, keepdims=True)
        gd = g * dyn
        dyc = r * (gd - ychat * _group_mean(gd * ychat))
        db_ref[...] = (dyc * dwc).astype(BF16)
        ddw = dyc * cb
        dw_ref[0:1, :] += jnp.sum(ddw * s2, axis=0, keepdims=True)
        dw_ref[1:2, :] += jnp.sum(ddw * s1, axis=0, keepdims=True)
        dw_ref[2:3, :] += jnp.sum(ddw * cc, axis=0, keepdims=True)
        u1 = jnp.where(row <= tp - 2, pltpu.roll(ddw, tp - 1, 0), 0.0)
        u2 = jnp.where(row <= tp - 3, pltpu.roll(ddw, tp - 2, 0), 0.0)
        dcc = w2 * ddw + w1 * u1 + w0 * u2
        dc_ref[...] = (dcc * h).astype(BF16)
        dh_ref[...] = (dcc * c).astype(BF16)

    col = pl.BlockSpec((tp, LANES), lambda t, b: (b, t))
    out = jax.ShapeDtypeStruct((nb_seq * tp, CONV_WIDTH), BF16)
    return pl.pallas_call(
        body,
        name="conv_bwd",
        grid=(cols, nb_seq),
        in_specs=_conv_specs(tp, lambda t, b, off: (b, off + t)) + [
            pl.BlockSpec((tp, LANES), lambda t, b: (b, N_HEADS * D_V // LANES + t)),
            pl.BlockSpec((8, LANES), lambda t, b: (0, t)),
            pl.BlockSpec((1, LANES), lambda t, b: (0, t))],
        out_specs=(col, col, col, col,
                   pl.BlockSpec((8, LANES), lambda t, b: (0, t)), pl.BlockSpec((1, LANES), lambda t, b: (0, t))),
        out_shape=(out, out, out, out,
                   jax.ShapeDtypeStruct((8, CONV_WIDTH), F32), jax.ShapeDtypeStruct((1, CONV_WIDTH), F32)),
        compiler_params=_params("arbitrary", "arbitrary"),
    )(p, p, p, p, dcat, conv_w, g_conv)


def _input_bwd(dps, w_in, x, meta, dh, norm_g, nt, send_in):
    nb_seq, s, d = x.shape
    r, kb = dps[0].shape
    ts = (s + LANES) // nt
    steps = nb_seq * nt
    n_dp = len(dps)
    in_slot = send_in.shape[1:]

    def body(*refs):
        dp_refs, w_ref, x_hbm, meta_ref, dh_ref, g_ref, pay_ref = refs[:n_dp], *refs[n_dp:n_dp + 6]
        o = n_dp + 6
        gx_hbm, dmeta_ref, dg_ref, r2_in = refs[o:o + 4]
        xbuf, gxbuf, tok_sems, own_in, r1_in, sum_in = refs[o + 4:o + 10]
        sems = refs[o + 10:]
        i = pl.program_id(0)
        b, k = i // nt, i % nt

        def plan():
            return _reduce_plan((pay_ref,), (own_in,), (r1_in,), (sum_in,), (r2_in,), *sems)

        @pl.when(i == 0)
        def _():
            dmeta_ref[...] = jnp.zeros_like(dmeta_ref)
            dg_ref[...] = jnp.zeros_like(dg_ref)
            plan()[0]()

        @pl.when(i == 1)
        def _():
            plan()[1]()

        def start(kk):
            if kk == 0:
                xbuf[0:PAD_FRONT, :] = jnp.zeros((PAD_FRONT, d), F32)
                xbuf[PAD_FRONT:LANES, :] = meta_ref[...]
            _token_copy(x_hbm, b, kk, ts, xbuf, tok_sems.at[0]).start()

        _for_tile(k, nt, start)
        du = _dot(dp_refs[0][...], w_ref[0:kb, :])
        for j in range(1, n_dp):
            du = du + _dot(dp_refs[j][...], w_ref[kb * j:kb * (j + 1), :])
        _for_tile(k, nt, lambda kk: _token_copy(x_hbm, b, kk, ts, xbuf, tok_sems.at[0]).wait())

        g = g_ref[...]
        hhat, rstd = _rms_stats(xbuf[...])
        dg_ref[...] += jnp.sum(du * hhat, axis=0, keepdims=True)
        res = _rms_bwd(g * du, hhat, rstd) + dh_ref[...].astype(F32)

        @pl.when(i > 0)
        def _():
            _for_tile(k, nt, lambda kk: _token_copy(gx_hbm, b, (kk - 1) % nt, ts, gxbuf, tok_sems.at[1], True).wait())

        gxbuf[...] = res

        @pl.when(k == 0)
        def _():
            dmeta_ref[...] += gxbuf[PAD_FRONT:LANES, :]

        _for_tile(k, nt, lambda kk: _token_copy(gx_hbm, b, kk, ts, gxbuf, tok_sems.at[1], True).start())

        @pl.when(i == steps - 1)
        def _():
            _token_copy(gx_hbm, b, nt - 1, ts, gxbuf, tok_sems.at[1], True).wait()
            plan()[2]()

    whole = lambda a: pl.BlockSpec(a.shape, lambda i: (0,) * a.ndim)
    hbm = pl.BlockSpec(memory_space=pl.ANY)
    return pl.pallas_call(
        body,
        name="input_bwd",
        grid=(steps,),
        in_specs=[pl.BlockSpec((ts, kb), lambda i: (i, 0)) for _ in dps]
        + [whole(w_in), hbm, whole(meta), pl.BlockSpec((ts, d), lambda i: (i, 0)), whole(norm_g), hbm],
        out_specs=(hbm, pl.BlockSpec((N_META, d), lambda i: (0, 0)), pl.BlockSpec((1, d), lambda i: (0, 0)), hbm),
        out_shape=(jax.ShapeDtypeStruct((nb_seq, s, d), F32),
                   jax.ShapeDtypeStruct((N_META, d), F32),
                   jax.ShapeDtypeStruct((1, d), F32),
                   jax.ShapeDtypeStruct((N_CHIPS,) + in_slot, BF16)),
        scratch_shapes=[pltpu.VMEM((ts, d), F32), pltpu.VMEM((ts, d), F32), pltpu.SemaphoreType.DMA((2,))]
        + _reduce_scratch([(in_slot, BF16)], [True]),
        compiler_params=_params("arbitrary"),
    )(*dps, w_in, x, meta, dh, norm_g, send_in)


def _in_proj_bwd_w(u, dps, bm, small_grads, send_out):
    r, d = u.shape
    kb = dps[0].shape[1]
    steps = r // bm
    n_dp, n_small = len(dps), len(small_grads)
    out_slot, small_slot = send_out.shape[1:], (SMALL_ROWS, LANES)

    def body(*refs):
        u_ref, dp_refs = refs[0], refs[1:1 + n_dp]
        small_refs = refs[1 + n_dp:1 + n_dp + n_small]
        o = 1 + n_dp + n_small
        pay_out, o_ref, r2_out, r2_small = refs[o:o + 4]
        acc_ref, ssmall, r1_out, sum_out, r1_small, sum_small = refs[o + 4:o + 10]
        sems = refs[o + 10:]
        i = pl.program_id(0)

        def plan():
            return _reduce_plan((pay_out, ssmall), (None, None), (r1_out, r1_small), (sum_out, sum_small),
                                (r2_out, r2_small), *sems)

        @pl.when(i == 0)
        def _():
            acc_ref[...] = jnp.zeros_like(acc_ref)
            _pack_small(ssmall, *small_refs)
            plan()[0]()

        @pl.when(i == 1)
        def _():
            plan()[1]()

        uu = u_ref[...]
        for j in range(n_dp):
            acc_ref[kb * j:kb * (j + 1), :] += _dot(dp_refs[j][...], uu, _TN)

        @pl.when(i == steps - 1)
        def _():
            for k in range(N_DEV):
                for s, e, c0 in _in_pieces(k):
                    o_ref[k, s:e, :] = acc_ref[c0:c0 + e - s, :].astype(BF16)
                o_ref[k, SHARD_IN:, :] = jnp.zeros((SHARD_IN_PAD - SHARD_IN, d), BF16)
            plan()[2]()

    whole = lambda a: pl.BlockSpec(a.shape, lambda i: (0,) * a.ndim)
    hbm = pl.BlockSpec(memory_space=pl.ANY)
    return pl.pallas_call(
        body,
        name="in_proj_bwd_w",
        grid=(steps,),
        in_specs=[pl.BlockSpec((bm, d), lambda i: (i, 0))]
        + [pl.BlockSpec((bm, kb), lambda i: (i, 0)) for _ in dps] + [whole(a) for a in small_grads]
        + [whole(send_out)],
        out_specs=(pl.BlockSpec((N_DEV, SHARD_IN_PAD, d), lambda i: (0, 0, 0)), hbm, hbm),
        out_shape=(jax.ShapeDtypeStruct((N_DEV, SHARD_IN_PAD, d), BF16),
                   jax.ShapeDtypeStruct((N_CHIPS,) + out_slot, BF16),
                   jax.ShapeDtypeStruct((N_CHIPS,) + small_slot, F32)),
        scratch_shapes=[pltpu.VMEM((kb * n_dp, d), F32), pltpu.VMEM((N_DEV,) + small_slot, F32)]
        + _reduce_scratch([(out_slot, BF16), (small_slot, F32)], [False, False]),
        compiler_params=_params("arbitrary"),
    )(u, *dps, *small_grads, send_out)


def _local_step(x, loss_target, u, p, meta_f, norm_g, w_in_p, q_norm_g, w_q_p, kv_norm_g, w_kv_p, conv_w_f,
                attn_out_g, conv_out_g, w_out_f, g_final):
    nb_seq, s, d = x.shape
    tp = s + LANES
    ht = tp // 2
    tables = _rope_tables(tp)

    q, k, v = _qkv_fwd(p, w_q_p, w_kv_p, q_norm_g, kv_norm_g, tables, nb_seq, tp)
    ya, o, lse = _attn_fwd(q, k, v, p, attn_out_g)
    yc = _conv_fwd(p, conv_w_f, conv_out_g, nb_seq, tp)
    dhb, d_final_g, loss_part = _out_proj_loss(ya, yc, w_out_f, x, loss_target, g_final, TOKEN_TILES)

    dcat, d_w_out = _out_proj_bwd(dhb, w_out_f, ya, yc, ht)
    send_out = d_w_out.reshape(N_DEV, SHARD_OUT, d)
    dq, dk, dv, dz_attn, d_attn_g = _attn_bwd(q, k, v, o, lse, dcat, p, attn_out_g)
    dpa, d_wq_p, d_wkv_p, d_gq, d_gkv = _qkv_bwd(p, dq, dk, dv, w_q_p, w_kv_p, q_norm_g, kv_norm_g, tables)
    d_b, d_c, d_h, dz_conv, d_conv_w, d_conv_g = _conv_bwd(p, dcat, conv_w_f, conv_out_g, nb_seq, tp)
    dps = (dpa, dz_attn, d_b, d_c, d_h, dz_conv)
    small = (d_wq_p, d_wkv_p, d_conv_w, d_final_g, d_gq, d_gkv, d_attn_g, d_conv_g, loss_part)
    send_in, r_out, r_small = _in_proj_bwd_w(u, dps, ht, small, send_out)
    grad_x, d_meta, d_norm_g, r_in = _input_bwd(dps, w_in_p, x, meta_f, dhb, norm_g, TOKEN_TILES, send_in)
    return grad_x, r_in, r_out, r_small, d_meta, d_norm_g


def kernel(x, meta_tokens, norm_g, w_in, q_norm_g, w_q_up, kv_norm_g, w_kv_up, conv_w, attn_out_g, conv_out_g, w_out, final_norm_g, loss_target, m_meta_tokens, m_norm_g, m_w_in, m_q_norm_g, m_w_q_up, m_kv_norm_g, m_w_kv_up, m_conv_w, m_attn_out_g, m_conv_out_g, m_w_out, m_final_norm_g, v_meta_tokens, v_norm_g, v_w_in, v_q_norm_g, v_w_q_up, v_kv_norm_g, v_w_kv_up, v_conv_w, v_attn_out_g, v_conv_out_g, v_w_out, v_final_norm_g):
    d = x.shape[-1]
    ht = (x.shape[1] + LANES) // 2
    u, w_in_p, meta_f = _prep_gather(x, meta_tokens, norm_g, w_in[0].T)
    p, w_q_p, w_kv_p, w_out_f, conv_w_f = _in_proj_gather(
        u, w_in_p, w_q_up[0].T, w_kv_up[0], w_out[0], conv_w[0], ht, 3 * GRP_A)
    g_final = final_norm_g.reshape(1, d)
    grad_x, r_in, r_out, r_small, d_meta, d_norm_g = _local_step(
        x, loss_target, u, p, meta_f, norm_g, w_in_p, q_norm_g, w_q_p, kv_norm_g, w_kv_p, conv_w_f,
        attn_out_g, conv_out_g, w_out_f, g_final)

    flat = lambda a: a.reshape(a.shape[-2:]) if a.ndim == 3 else a.reshape(1, -1) if a.ndim == 1 else a
    transposed = ("w_in", "w_q_up")
    to_kernel = lambda n, a: flat(a).T if n in transposed else flat(a)
    from_kernel = lambda n, a, shape: (a.T if n in transposed else a).reshape(shape)
    params = {
        "meta_tokens": (meta_tokens, m_meta_tokens, v_meta_tokens),
        "norm_g": (norm_g, m_norm_g, v_norm_g),
        "w_in": (w_in, m_w_in, v_w_in),
        "q_norm_g": (q_norm_g, m_q_norm_g, v_q_norm_g),
        "w_q_up": (w_q_up, m_w_q_up, v_w_q_up),
        "kv_norm_g": (kv_norm_g, m_kv_norm_g, v_kv_norm_g),
        "w_kv_up": (w_kv_up, m_w_kv_up, v_w_kv_up),
        "conv_w": (conv_w, m_conv_w, v_conv_w),
        "attn_out_g": (attn_out_g, m_attn_out_g, v_attn_out_g),
        "conv_out_g": (conv_out_g, m_conv_out_g, v_conv_out_g),
        "w_out": (w_out, m_w_out, v_w_out),
        "final_norm_g": (final_norm_g, m_final_norm_g, v_final_norm_g),
    }
    grads, loss = _reduce_tail(r_in, r_out, r_small, d_meta, d_norm_g)
    updated = _adamw(grads, {n: tuple(to_kernel(n, a) for a in t) for n, t in params.items()})
    outs = [[from_kernel(n, updated[n][i], params[n][0].shape) for n, _ in PARAM_SHAPES] for i in range(4)]
    return (loss[0, 0], grad_x, *outs[0], *outs[1], *outs[2], *outs[3])
```

```python
import functools

import jax
import jax.numpy as jnp
from jax import lax
from jax.experimental import pallas as pl
from jax.experimental.pallas import tpu as pltpu

F32 = jnp.float32
BF16 = jnp.bfloat16

N_META = 16
D_MODEL = 1024
N_HEADS = 4
D_NOPE = 128
D_ROPE = 64
D_V = 128
Q_RANK = 256
KV_RANK = 128
CONV_WIDTH = 512
CONV_GROUP = 64
ROPE_THETA = 10000.0
ATTN_SCALE = (D_NOPE + D_ROPE) ** -0.5
Q_SCALE = ATTN_SCALE * 1.4426950408889634
EPS = 1e-6
NEG_INF = -1e30

ADAM_LR = 0.001
ADAM_B1 = 0.9
ADAM_B2 = 0.999
ADAM_EPS = 1e-08
ADAM_WD = 0.01
ADAM_STEP = 10

LANES = 128
PAD_FRONT = LANES - N_META
K_TILE = 256
Q_TILE = 512
N_DEV = 8
VMEM_LIMIT = 56 * 1024 * 1024

IN_PAD = 3072
GRP_A = 512
N_A = Q_RANK + KV_RANK + D_ROPE
IN_PROJ = 3008
SHARD_IN = IN_PROJ // N_DEV
SHARD_IN_PAD = 384
SHARD_Q = 96
SHARD_KV = 128
SHARD_OUT = 128
SHARD_CONV = 64
SHARD_META = 128
Q_COLS = N_HEADS * (D_NOPE + D_ROPE)
KV_COLS = N_HEADS * (D_NOPE + D_V)

ROW_Q, ROW_KV, ROW_META, ROW_CONV = 0, 256, 384, 400
ROW_REPL = 408
ROW_NORM, ROW_FINAL, ROW_GQ, ROW_GKV, ROW_ATTN, ROW_CONVG, ROW_LOSS = 408, 416, 424, 426, 427, 431, 435
SMALL_ROWS = 440

PARAM_SHAPES = (
    ("meta_tokens", (N_META, SHARD_META)), ("norm_g", (1, D_MODEL)), ("w_in", (SHARD_IN, D_MODEL)),
    ("q_norm_g", (1, Q_RANK)), ("w_q_up", (SHARD_Q, Q_RANK)), ("kv_norm_g", (1, KV_RANK)),
    ("w_kv_up", (KV_RANK, SHARD_KV)), ("conv_w", (3, SHARD_CONV)), ("attn_out_g", (1, CONV_WIDTH)),
    ("conv_out_g", (1, CONV_WIDTH)), ("w_out", (SHARD_OUT, D_MODEL)), ("final_norm_g", (1, D_MODEL)),
)


def _in_pieces(k):
    lo, hi = SHARD_IN * k, SHARD_IN * (k + 1)
    out = []
    if lo < N_A:
        out.append((0, min(hi, N_A) - lo, lo))
    if hi > N_A:
        s = max(lo, N_A)
        out.append((s - lo, hi - lo, s + GRP_A - N_A))
    return out


def _q_pieces(k):
    lo, hi = SHARD_Q * k, SHARD_Q * (k + 1)
    out = []
    for h in range(N_HEADS):
        base = (D_NOPE + D_ROPE) * h
        s, e = max(lo, base), min(hi, base + D_NOPE)
        if s < e:
            out.append((s - lo, e - lo, D_NOPE * h + s - base))
        s, e = max(lo, base + D_NOPE), min(hi, base + D_NOPE + D_ROPE)
        if s < e:
            out.append((s - lo, e - lo, N_HEADS * D_NOPE + D_ROPE * h + s - base - D_NOPE))
    return out


def _kv_dst(k):
    return D_NOPE * (k // 2) + (N_HEADS * D_NOPE if k % 2 else 0)


def _params(*sem):
    return pltpu.CompilerParams(dimension_semantics=sem, vmem_limit_bytes=VMEM_LIMIT)


def _rms_stats(x):
    r = lax.rsqrt(jnp.mean(x * x, axis=-1, keepdims=True) + EPS)
    return x * r, r


def _rms_bwd(gdy, xhat, r):
    return r * (gdy - xhat * jnp.mean(gdy * xhat, axis=-1, keepdims=True))


def _sigmoid(z):
    return 1.0 / (1.0 + jnp.exp(-z))


def _group_mean(x):
    i0 = lax.broadcasted_iota(jnp.int32, (LANES, LANES), 0) // CONV_GROUP
    i1 = lax.broadcasted_iota(jnp.int32, (LANES, LANES), 1) // CONV_GROUP
    m = jnp.where(i0 == i1, 1.0 / CONV_GROUP, 0.0).astype(BF16)
    hi = x.astype(BF16)
    lo = (x - hi.astype(F32)).astype(BF16)
    return jnp.dot(hi, m, preferred_element_type=F32) + jnp.dot(lo, m, preferred_element_type=F32)


_NT = (((1,), (1,)), ((), ()))
_TN = (((0,), (0,)), ((), ()))


def _dot(a, b, dims=None):
    if dims is None:
        return jnp.dot(a, b, preferred_element_type=F32)
    return lax.dot_general(a, b, dims, preferred_element_type=F32)


def _device_position():
    x, y, c = lax.axis_index("x"), lax.axis_index("y"), lax.axis_index("c")
    return x, y, c, 4 * x + 2 * y + c


def _gather_plan(srcs, slots, send_sems, recv_sems, local_sems):
    x, y, c, _ = _device_position()
    me, sibling = (x, y, c), (x, y, 1 - c)
    flip = lambda v, on: v + on - 2 * v * on
    near = (flip(x, 1 - c), flip(y, c))
    far = (flip(x, c), flip(y, 1 - c))
    diag = (1 - x, 1 - y)
    n = len(srcs)

    def slot(a, px, py, pc):
        return slots[a].at[4 * px + 2 * py + pc]

    def copy(a, k, block, to, own=False):
        return pltpu.make_async_remote_copy(
            src_ref=srcs[a] if own else slot(a, *block),
            dst_ref=slot(a, *block),
            send_sem=send_sems.at[7 * a + k],
            recv_sem=recv_sems.at[7 * a + k],
            device_id=to,
            device_id_type=pl.DeviceIdType.MESH,
        )

    def local(a):
        return pltpu.make_async_copy(srcs[a], slot(a, *me), local_sems.at[a])

    sent = [(me, sibling), (me, (*near, c)), (me, (*far, c)), ((*near, c), (*far, c)),
            ((*near, c), sibling), ((*far, c), sibling), ((*diag, c), sibling)]
    landed = [sibling, (*near, c), (*far, c), (*diag, c), (*far, 1 - c), (*near, 1 - c), (*diag, 1 - c)]

    def send(a, k):
        return copy(a, k, *sent[k], own=k < 3)

    def arrival(a, k):
        return copy(a, k, landed[k], me)

    def start():
        for a in range(n):
            local(a).start()
            for k in range(3):
                send(a, k).start()

    def relay():
        for a in range(n):
            arrival(a, 1).wait_recv()
            send(a, 3).start()
            send(a, 4).start()

    def forward():
        for k in (2, 3):
            for a in range(n):
                arrival(a, k).wait_recv()
                send(a, k + 3).start()

    def finish():
        for a in range(n):
            for k in (0, 4, 5, 6):
                arrival(a, k).wait_recv()
        for a in range(n):
            for k in range(7):
                send(a, k).wait_send()
            local(a).wait()

    return start, relay, forward, finish


def _adam_update(g, w, m, v):
    m_new = ADAM_B1 * m + (1.0 - ADAM_B1) * g
    v_new = ADAM_B2 * v + (1.0 - ADAM_B2) * (g * g)
    m_hat = m_new / (1.0 - ADAM_B1 ** ADAM_STEP)
    v_hat = v_new / (1.0 - ADAM_B2 ** ADAM_STEP)
    return -ADAM_LR * (m_hat / (jnp.sqrt(v_hat) + ADAM_EPS) + ADAM_WD * w), m_new, v_new


def _adamw(grads, params):
    names = [n for n, _ in PARAM_SHAPES]
    n_p = len(names)

    def body(*refs):
        for i in range(n_p):
            g = refs[i][...]
            w, m, v = (refs[n_p + 3 * i + j][...] for j in range(3))
            delta, m_new, v_new = _adam_update(g, w, m, v)
            for j, val in enumerate((g, delta, m_new, v_new)):
                refs[4 * n_p + 4 * i + j][...] = val

    vm = pl.BlockSpec(memory_space=pltpu.VMEM)
    out_shape = []
    for _, shape in PARAM_SHAPES:
        out_shape += [jax.ShapeDtypeStruct(shape, F32)] * 4
    outs = pl.pallas_call(
        body,
        name="adamw",
        out_shape=tuple(out_shape),
        in_specs=[vm] * (4 * n_p),
        out_specs=(vm,) * (4 * n_p),
        compiler_params=pltpu.CompilerParams(vmem_limit_bytes=VMEM_LIMIT),
    )(*[grads[n] for n in names], *[a for n in names for a in params[n]])
    return {n: outs[4 * i:4 * i + 4] for i, n in enumerate(names)}


N_CHIPS = 4


def _reduce_plan(pays, owns, r1s, sums, r2s, send1, recv1, send2, recv2, local_sems):
    x, y, c, _ = _device_position()
    sibling = (x, y, 1 - c)
    chips = [((1 - x if rj & 2 else x), (1 - y if rj & 1 else y)) for rj in range(N_CHIPS)]
    n = len(pays)

    def slot_of(rj, core):
        return 4 * chips[rj][0] + 2 * chips[rj][1] + core

    def to_sibling(a, rj):
        return pltpu.make_async_remote_copy(
            src_ref=pays[a].at[slot_of(rj, 1 - c)], dst_ref=r1s[a].at[rj],
            send_sem=send1.at[N_CHIPS * a + rj], recv_sem=recv1.at[N_CHIPS * a + rj],
            device_id=sibling, device_id_type=pl.DeviceIdType.MESH)

    def load_own(a, rj):
        return pltpu.make_async_copy(pays[a].at[slot_of(rj, c)], owns[a].at[rj], local_sems.at[2 * N_CHIPS * a + rj])

    def to_chip(a, rj):
        return pltpu.make_async_remote_copy(
            src_ref=sums[a].at[rj], dst_ref=r2s[a].at[rj],
            send_sem=send2.at[N_CHIPS * a + rj], recv_sem=recv2.at[N_CHIPS * a + rj],
            device_id=(*chips[rj], c), device_id_type=pl.DeviceIdType.MESH)

    def keep(a):
        return pltpu.make_async_copy(sums[a].at[0], r2s[a].at[0], local_sems.at[2 * N_CHIPS * a + N_CHIPS])

    def start():
        for a in range(n):
            for rj in range(N_CHIPS):
                to_sibling(a, rj).start()
                if owns[a] is not None:
                    load_own(a, rj).start()

    def combine():
        for a in range(n):
            for rj in range(N_CHIPS):
                to_sibling(a, rj).wait_recv()
                if owns[a] is not None:
                    load_own(a, rj).wait()
                    mine = owns[a][rj]
                else:
                    mine = pays[a][slot_of(rj, c)]
                sums[a][rj] = (mine.astype(F32) + r1s[a][rj].astype(F32)).astype(sums[a].dtype)
            keep(a).start()
            for rj in range(1, N_CHIPS):
                to_chip(a, rj).start()

    def finish():
        for a in range(n):
            for rj in range(1, N_CHIPS):
                to_chip(a, rj).wait_recv()
            for rj in range(N_CHIPS):
                to_sibling(a, rj).wait_send()
            for rj in range(1, N_CHIPS):
                to_chip(a, rj).wait_send()
            keep(a).wait()

    return start, combine, finish


def _reduce_scratch(shapes_dtypes, own_flags):
    out = []
    for (shape, dtype), own in zip(shapes_dtypes, own_flags):
        if own:
            out.append(pltpu.VMEM((N_CHIPS,) + shape, dtype))
        out += [pltpu.VMEM((N_CHIPS,) + shape, dtype), pltpu.VMEM((N_CHIPS,) + shape, dtype)]
    n = len(shapes_dtypes)
    out += [pltpu.SemaphoreType.DMA((N_CHIPS * n,))] * 4 + [pltpu.SemaphoreType.DMA((2 * N_CHIPS * n,))]
    return out


def _pack_small(ssmall, dwq, dwkv, dconv, dfinal, dgq, dgkv, dattn, dconvg, loss_part):
    ssmall[...] = jnp.zeros_like(ssmall)
    rep = ssmall.at[0]
    for i in range(D_MODEL // LANES):
        rep[ROW_FINAL + i:ROW_FINAL + i + 1, :] = dfinal[:, LANES * i:LANES * (i + 1)]
    for i in range(Q_RANK // LANES):
        rep[ROW_GQ + i:ROW_GQ + i + 1, :] = dgq[:, LANES * i:LANES * (i + 1)]
    rep[ROW_GKV:ROW_GKV + 1, :] = dgkv[...]
    for i in range(CONV_WIDTH // LANES):
        rep[ROW_ATTN + i:ROW_ATTN + i + 1, :] = dattn[:, LANES * i:LANES * (i + 1)]
        rep[ROW_CONVG + i:ROW_CONVG + i + 1, :] = dconvg[:, LANES * i:LANES * (i + 1)]
    rep[ROW_LOSS:ROW_LOSS + 1, :] = loss_part[...]
    for k in range(N_DEV):
        if k:
            ssmall[k, ROW_REPL:, :] = ssmall[0, ROW_REPL:, :]
        for s, e, d in _q_pieces(k):
            for i in range(Q_RANK // LANES):
                ssmall[k, ROW_Q + SHARD_Q * i + s:ROW_Q + SHARD_Q * i + e, :] = dwq[d:d + e - s, LANES * i:LANES * (i + 1)]
        ssmall[k, ROW_KV:ROW_KV + KV_RANK, :] = dwkv[:, _kv_dst(k):_kv_dst(k) + SHARD_KV]
        ssmall[k, ROW_CONV:ROW_CONV + 3, 0:SHARD_CONV] = dconv[0:3, SHARD_CONV * k:SHARD_CONV * (k + 1)]


TOKEN_TILES = 4
TAIL_ROWS = N_META + D_MODEL // LANES


def _reduce_tail(r_in, r_out, r_small, d_meta, d_norm):
    n_p = len(PARAM_SHAPES)
    names = [n for n, _ in PARAM_SHAPES]

    def body(*refs):
        rin, rout, rsmall, dmeta, dnorm = refs[:5]
        g_out = {n: refs[5 + i] for i, n in enumerate(names)}
        loss_out = refs[5 + n_p]
        stail, rtail, gsum, gtail, send_sems, recv_sems = refs[6 + n_p:]
        x, y, c, me = _device_position()
        my_chip = 2 * x + y

        for k in range(N_DEV):
            stail[k, 0:N_META, :] = dmeta[:, SHARD_META * k:SHARD_META * (k + 1)]
            for i in range(D_MODEL // LANES):
                stail[k, N_META + i:N_META + i + 1, :] = dnorm[:, LANES * i:LANES * (i + 1)]
        copies = []
        for r in range(1, N_DEV):
            peer = (1 - x if r & 4 else x, 1 - y if r & 2 else y, 1 - c if r & 1 else c)
            copies.append(pltpu.make_async_remote_copy(
                src_ref=stail.at[4 * peer[0] + 2 * peer[1] + peer[2]],
                dst_ref=rtail.at[r],
                send_sem=send_sems.at[r - 1],
                recv_sem=recv_sems.at[r - 1],
                device_id=peer,
                device_id_type=pl.DeviceIdType.MESH,
            ))
        for cp in copies:
            cp.start()
        rtail[0] = stail[me]

        g = rin[my_chip].astype(F32)
        for ch in range(1, N_CHIPS):
            g = g + rin[ch ^ my_chip].astype(F32)
        g_out["w_in"][...] = g[:SHARD_IN, :]

        g = rout[my_chip].astype(F32)
        gs = rsmall[my_chip]
        for ch in range(1, N_CHIPS):
            g = g + rout[ch ^ my_chip].astype(F32)
            gs = gs + rsmall[ch ^ my_chip]
        g_out["w_out"][...] = g
        gsum[...] = gs
        for i in range(Q_RANK // LANES):
            g_out["w_q_up"][:, LANES * i:LANES * (i + 1)] = gsum[ROW_Q + SHARD_Q * i:ROW_Q + SHARD_Q * (i + 1), :]
        g_out["w_kv_up"][...] = gsum[ROW_KV:ROW_KV + KV_RANK, :]
        g_out["conv_w"][...] = gsum[ROW_CONV:ROW_CONV + 3, 0:SHARD_CONV]
        for name, row, width in (("final_norm_g", ROW_FINAL, D_MODEL), ("q_norm_g", ROW_GQ, Q_RANK),
                                 ("kv_norm_g", ROW_GKV, KV_RANK), ("attn_out_g", ROW_ATTN, CONV_WIDTH),
                                 ("conv_out_g", ROW_CONVG, CONV_WIDTH)):
            for i in range(width // LANES):
                g_out[name][:, LANES * i:LANES * (i + 1)] = gsum[row + i:row + i + 1, :]
        loss_out[...] = gsum[ROW_LOSS:ROW_LOSS + 1, :]

        for cp in copies:
            cp.wait_recv()
        gt = rtail[me]
        for d in range(1, N_DEV):
            gt = gt + rtail[d ^ me]
        gtail[...] = gt
        g_out["meta_tokens"][...] = gtail[0:N_META, :]
        for i in range(D_MODEL // LANES):
            g_out["norm_g"][:, LANES * i:LANES * (i + 1)] = gtail[N_META + i:N_META + i + 1, :]
        for cp in copies:
            cp.wait_send()

    vm = pl.BlockSpec(memory_space=pltpu.VMEM)
    out_shape = [jax.ShapeDtypeStruct(shape, F32) for _, shape in PARAM_SHAPES]
    out_shape.append(jax.ShapeDtypeStruct((1, LANES), F32))
    outs = pl.pallas_call(
        body,
        name="reduce_tail",
        out_shape=tuple(out_shape),
        in_specs=[vm] * 5,
        out_specs=(vm,) * len(out_shape),
        scratch_shapes=[
            pltpu.VMEM((N_DEV, TAIL_ROWS, LANES), F32),
            pltpu.VMEM((N_DEV, TAIL_ROWS, LANES), F32),
            pltpu.VMEM((SMALL_ROWS, LANES), F32),
            pltpu.VMEM((TAIL_ROWS, LANES), F32),
            pltpu.SemaphoreType.DMA((N_DEV - 1,)),
            pltpu.SemaphoreType.DMA((N_DEV - 1,)),
        ],
        compiler_params=pltpu.CompilerParams(vmem_limit_bytes=VMEM_LIMIT),
    )(r_in, r_out, r_small, d_meta, d_norm)
    return {n: outs[i] for i, n in enumerate(names)}, outs[-1]


def _prep_gather(x, meta, norm_g, w_in_t):
    nb_seq, s, d = x.shape
    nb = s // LANES + 1
    relay_step = nb_seq * (nb - 1) // 2
    forward_step = nb_seq * (nb - 1) - 1
    finish_step = nb_seq * (nb - 1)

    def body(x_ref, meta_ref, g_ref, win_ref, u_ref, w_in_p, meta_f,
             sbig, ssmall, gbig, gsmall, send_sems, recv_sems, local_sems):
        jj, b = pl.program_id(0), pl.program_id(1)
        t = jj * nb_seq + b

        def plan():
            return _gather_plan((sbig, ssmall), (gbig, gsmall), send_sems, recv_sems, local_sems)

        @pl.when(t == 0)
        def _():
            sbig[0:SHARD_IN, :] = win_ref[...].astype(BF16)
            sbig[SHARD_IN:, :] = jnp.zeros((SHARD_IN_PAD - SHARD_IN, d), BF16)
            ssmall[...] = meta_ref[...]
            plan()[0]()

        @pl.when(t == relay_step)
        def _():
            plan()[1]()

        @pl.when(t == forward_step)
        def _():
            plan()[2]()

        @pl.when(t == finish_step)
        def _():
            plan()[3]()
            w_in_p[N_A:GRP_A, :] = jnp.zeros((GRP_A - N_A, d), BF16)
            for k in range(N_DEV):
                for s0, e0, d0 in _in_pieces(k):
                    w_in_p[d0:d0 + e0 - s0, :] = gbig[k, s0:e0, :]
                meta_f[:, SHARD_META * k:SHARD_META * (k + 1)] = gsmall[k]

        def norm(h):
            hhat, _ = _rms_stats(h)
            return (hhat * g_ref[...]).astype(BF16)

        @pl.when(jj < nb - 1)
        def _():
            u_ref[...] = norm(x_ref[0])

        @pl.when(jj == nb - 1)
        def _():
            u_ref[0:PAD_FRONT, :] = jnp.zeros((PAD_FRONT, d), BF16)
            u_ref[PAD_FRONT:LANES, :] = norm(meta_f[...])

    whole = lambda shape: pl.BlockSpec(shape, lambda jj, b: (0,) * len(shape))
    return pl.pallas_call(
        body,
        name="prep_norm_gather",
        grid=(nb, nb_seq),
        in_specs=[
            pl.BlockSpec((1, LANES, d), lambda jj, b: (b, jnp.minimum(jj, nb - 2), 0)),
            whole(meta.shape), whole(norm_g.shape), whole(w_in_t.shape),
        ],
        out_specs=(pl.BlockSpec((LANES, d), lambda jj, b: (b * nb + (jj + 1) % nb, 0)),
                   whole((IN_PAD, d)), whole((N_META, d))),
        out_shape=(jax.ShapeDtypeStruct((nb_seq * nb * LANES, d), BF16),
                   jax.ShapeDtypeStruct((IN_PAD, d), BF16),
                   jax.ShapeDtypeStruct((N_META, d), F32)),
        scratch_shapes=[
            pltpu.VMEM((SHARD_IN_PAD, d), BF16),
            pltpu.VMEM((N_META, SHARD_META), F32),
            pltpu.VMEM((N_DEV, SHARD_IN_PAD, d), BF16),
            pltpu.VMEM((N_DEV, N_META, SHARD_META), F32),
            pltpu.SemaphoreType.DMA((14,)),
            pltpu.SemaphoreType.DMA((14,)),
            pltpu.SemaphoreType.DMA((2,)),
        ],
        compiler_params=_params("arbitrary", "arbitrary"),
    )(x, meta, norm_g, w_in_t)


def _in_proj_gather(u, w_in_p, w_q, w_kv, w_out, conv_w, bm, bn):
    m, k_dim = u.shape
    n = w_in_p.shape[0]
    steps = (m // bm) * (n // bn)
    relay_step, forward_step = steps // 3, 2 * steps // 3
    qkv_shape = (SHARD_Q + KV_RANK, Q_RANK)

    def body(a_ref, b_ref, wq_ref, wkv_ref, wout_ref, conv_ref, o_ref, w_q_p, w_kv_p, w_out_f, conv_f,
             sqkv, sout, sconv, gqkv, gout, gconv, send_sems, recv_sems, local_sems):
        t = pl.program_id(0) * (n // bn) + pl.program_id(1)

        def plan():
            return _gather_plan((sqkv, sout, sconv), (gqkv, gout, gconv), send_sems, recv_sems, local_sems)

        @pl.when(t == 0)
        def _():
            sqkv[...] = jnp.zeros_like(sqkv)
            sqkv[0:SHARD_Q, :] = wq_ref[...].astype(BF16)
            sqkv[SHARD_Q:, 0:SHARD_KV] = wkv_ref[...].astype(BF16)
            sout[...] = wout_ref[...].astype(BF16)
            sconv[...] = jnp.zeros_like(sconv)
            sconv[0:3, 0:SHARD_CONV] = conv_ref[...]
            plan()[0]()

        @pl.when(t == relay_step)
        def _():
            plan()[1]()

        @pl.when(t == forward_step)
        def _():
            plan()[2]()

        o_ref[...] = _dot(a_ref[...], b_ref[...], _NT).astype(o_ref.dtype)

        @pl.when(t == steps - 1)
        def _():
            plan()[3]()
            conv_f[...] = jnp.zeros_like(conv_f)
            for k in range(N_DEV):
                for s0, e0, d0 in _q_pieces(k):
                    w_q_p[d0:d0 + e0 - s0, :] = gqkv[k, s0:e0, :]
                w_kv_p[:, _kv_dst(k):_kv_dst(k) + SHARD_KV] = gqkv[k, SHARD_Q:, 0:SHARD_KV]
                w_out_f[SHARD_OUT * k:SHARD_OUT * (k + 1), :] = gout[k]
                conv_f[0:3, SHARD_CONV * k:SHARD_CONV * (k + 1)] = gconv[k, 0:3, 0:SHARD_CONV]

    whole = lambda shape: pl.BlockSpec(shape, lambda i, j: (0,) * len(shape))
    return pl.pallas_call(
        body,
        name="in_proj_gather",
        grid=(m // bm, n // bn),
        in_specs=[pl.BlockSpec((bm, k_dim), lambda i, j: (i, 0)), pl.BlockSpec((bn, k_dim), lambda i, j: (j, 0)),
                  whole(w_q.shape), whole(w_kv.shape), whole(w_out.shape), whole(conv_w.shape)],
        out_specs=(pl.BlockSpec((bm, bn), lambda i, j: (i, j)),
                   whole((Q_COLS, Q_RANK)), whole((KV_RANK, KV_COLS)), whole((D_MODEL, D_MODEL)),
                   whole((8, CONV_WIDTH))),
        out_shape=(jax.ShapeDtypeStruct((m, n), BF16),
                   jax.ShapeDtypeStruct((Q_COLS, Q_RANK), BF16),
                   jax.ShapeDtypeStruct((KV_RANK, KV_COLS), BF16),
                   jax.ShapeDtypeStruct((D_MODEL, D_MODEL), BF16),
                   jax.ShapeDtypeStruct((8, CONV_WIDTH), F32)),
        scratch_shapes=[
            pltpu.VMEM(qkv_shape, BF16),
            pltpu.VMEM((SHARD_OUT, D_MODEL), BF16),
            pltpu.VMEM((8, LANES), F32),
            pltpu.VMEM((N_DEV,) + qkv_shape, BF16),
            pltpu.VMEM((N_DEV, SHARD_OUT, D_MODEL), BF16),
            pltpu.VMEM((N_DEV, 8, LANES), F32),
            pltpu.SemaphoreType.DMA((21,)),
            pltpu.SemaphoreType.DMA((21,)),
            pltpu.SemaphoreType.DMA((3,)),
        ],
        compiler_params=_params("arbitrary", "arbitrary"),
    )(u, w_in_p, w_q, w_kv, w_out, conv_w)


def _rope_tables(tp):
    half = D_ROPE // 2
    inv_freq = 1.0 / (ROPE_THETA ** (jnp.arange(half, dtype=F32) / half))
    pos = (jnp.arange(tp) - PAD_FRONT).astype(F32)
    ang = pos[:, None] * inv_freq[None, :]
    cos = jnp.tile(jnp.cos(ang), (1, LANES // half))
    sin = jnp.tile(jnp.sin(ang), (1, LANES // half))
    first = (jnp.arange(LANES) % D_ROPE) < half
    return cos, jnp.where(first, -sin, 0.0), jnp.where(first, 0.0, sin)


def _rope(t, cos, sa, sb):
    return t * cos + pltpu.roll(t, LANES - D_ROPE // 2, 1) * sa + pltpu.roll(t, D_ROPE // 2, 1) * sb


def _rope_t(t, cos, sa, sb):
    return t * cos + pltpu.roll(t * sa, D_ROPE // 2, 1) + pltpu.roll(t * sb, LANES - D_ROPE // 2, 1)


def _qkv_fwd(p, wq, wkv, gq, gkv, tables, nb_seq, tp):
    ht = tp // 2

    def body(pa_ref, wq_ref, wkv_ref, gq_ref, gkv_ref, cos_ref, sa_ref, sb_ref, q_ref, k_ref, v_ref):
        pa = pa_ref[...].astype(F32)
        cq_hat, _ = _rms_stats(pa[:, :Q_RANK])
        ckv_hat, _ = _rms_stats(pa[:, Q_RANK:Q_RANK + KV_RANK])
        q = _dot((cq_hat * gq_ref[...]).astype(BF16), wq_ref[...], _NT) * Q_SCALE
        kv = _dot((ckv_hat * gkv_ref[...]).astype(BF16), wkv_ref[...])
        tabs = (cos_ref[...], sa_ref[...], sb_ref[...])
        lane = lax.broadcasted_iota(jnp.int32, (ht, LANES), 1)
        low = lane < D_ROPE
        mark = lane == D_ROPE
        row = (pl.program_id(0) % 2) * ht + lax.broadcasted_iota(jnp.int32, (ht, LANES), 0)
        k_pe = jnp.where(mark & (row < PAD_FRONT), NEG_INF, _rope(pa[:, Q_RANK + KV_RANK:], *tabs))
        one = jnp.where(mark & (row >= PAD_FRONT), 1.0, 0.0)
        pairs = [_rope(q[:, N_HEADS * D_NOPE + LANES * i:N_HEADS * D_NOPE + LANES * (i + 1)], *tabs) for i in range(2)]
        for h in range(N_HEADS):
            pair = pairs[h // 2]
            if h % 2:
                pair = pltpu.roll(pair, D_ROPE, 1)
            pe = jnp.where(low, pair, one)
            q_ref[0, h] = jnp.concatenate([q[:, D_NOPE * h:D_NOPE * (h + 1)], pe], axis=1).astype(BF16)
            k_ref[0, h] = jnp.concatenate([kv[:, D_NOPE * h:D_NOPE * (h + 1)], k_pe], axis=1).astype(BF16)
            v_ref[0, h] = kv[:, N_HEADS * D_NOPE + D_V * h:N_HEADS * D_NOPE + D_V * (h + 1)].astype(BF16)

    full = lambda a: pl.BlockSpec(a.shape, lambda i: (0,) * a.ndim)
    tab = pl.BlockSpec((ht, LANES), lambda i: (i % 2, 0))
    qk = pl.BlockSpec((1, N_HEADS, ht, 2 * LANES), lambda i: (i // 2, 0, i % 2, 0))
    return pl.pallas_call(
        body,
        name="qkv_fwd",
        grid=(2 * nb_seq,),
        in_specs=[pl.BlockSpec((ht, GRP_A), lambda i: (i, 0)), full(wq), full(wkv), full(gq), full(gkv), tab, tab, tab],
        out_specs=(qk, qk, pl.BlockSpec((1, N_HEADS, ht, D_V), lambda i: (i // 2, 0, i % 2, 0))),
        out_shape=(
            jax.ShapeDtypeStruct((nb_seq, N_HEADS, tp, 2 * LANES), BF16),
            jax.ShapeDtypeStruct((nb_seq, N_HEADS, tp, 2 * LANES), BF16),
            jax.ShapeDtypeStruct((nb_seq, N_HEADS, tp, D_V), BF16),
        ),
        compiler_params=_params("parallel"),
    )(p, wq, wkv, gq, gkv, *tables)


def _attn_fwd(q, k, v, p, g_attn):
    nb_seq, _, tp, _ = q.shape

    def body(q_ref, k_ref, v_ref, z_ref, g_ref, y_ref, o_ref, lse_ref):
        g = g_ref[...]
        for r0 in range(0, tp, Q_TILE):
            nq = min(Q_TILE, tp - r0)
            kend = r0 + nq
            qq = q_ref[0, 0, r0:kend, :]
            sd = _dot(qq, k_ref[0, 0, r0:kend, :], _NT)
            causal = (lax.broadcasted_iota(jnp.int32, (nq, nq), 1) <= lax.broadcasted_iota(jnp.int32, (nq, nq), 0))
            sd = jnp.where(causal, sd, NEG_INF)
            m = jnp.max(sd, axis=-1, keepdims=True)
            if r0:
                so = _dot(qq, k_ref[0, 0, 0:r0, :], _NT)
                m = jnp.maximum(m, jnp.max(so, axis=-1, keepdims=True))
            ed = jnp.exp2(sd - m)
            l = jnp.sum(ed, axis=-1, keepdims=True)
            o = _dot(ed.astype(BF16), v_ref[0, 0, r0:kend, :])
            if r0:
                eo = jnp.exp2(so - m)
                l = l + jnp.sum(eo, axis=-1, keepdims=True)
                o = o + _dot(eo.astype(BF16), v_ref[0, 0, 0:r0, :])
            o = o * (1.0 / l)
            o_ref[0, 0, r0:kend, :] = o
            lse_ref[0, 0, r0:kend, :] = jnp.broadcast_to(m + jnp.log2(l), (nq, LANES))
            ohat, _ = _rms_stats(o)
            z = z_ref[r0:kend, :].astype(F32)
            y_ref[r0:kend, :] = (ohat * g * (z * _sigmoid(z))).astype(BF16)

    qk = pl.BlockSpec((1, 1, tp, 2 * LANES), lambda b, h: (b, h, 0, 0))
    hv = pl.BlockSpec((1, 1, tp, D_V), lambda b, h: (b, h, 0, 0))
    return pl.pallas_call(
        body,
        name="attn_fwd",
        grid=(nb_seq, N_HEADS),
        in_specs=[qk, qk, hv,
                  pl.BlockSpec((tp, LANES), lambda b, h: (b, GRP_A // LANES + h)),
                  pl.BlockSpec((1, LANES), lambda b, h: (0, h))],
        out_specs=(pl.BlockSpec((tp, LANES), lambda b, h: (b, h)), hv, hv),
        out_shape=(
            jax.ShapeDtypeStruct((nb_seq * tp, N_HEADS * D_V), BF16),
            jax.ShapeDtypeStruct((nb_seq, N_HEADS, tp, D_V), F32),
            jax.ShapeDtypeStruct((nb_seq, N_HEADS, tp, LANES), F32),
        ),
        compiler_params=_params("parallel", "parallel"),
    )(q, k, v, p, g_attn)


_CONV_COL0 = (GRP_A + N_HEADS * D_V) // LANES


def _conv_specs(tp, order):
    cols = CONV_WIDTH // LANES
    return [pl.BlockSpec((tp, LANES), functools.partial(
        lambda a, b, off: order(a, b, off), off=_CONV_COL0 + i * cols)) for i in range(4)]


def _conv_fwd(p, conv_w, g_conv, nb_seq, tp):
    def body(b_ref, c_ref, h_ref, z_ref, w_ref, g_ref, y_ref):
        cc = c_ref[...].astype(F32) * h_ref[...].astype(F32)
        row = lax.broadcasted_iota(jnp.int32, (tp, LANES), 0)
        s1 = jnp.where(row >= 1, pltpu.roll(cc, 1, 0), 0.0)
        s2 = jnp.where(row >= 2, pltpu.roll(cc, 2, 0), 0.0)
        yc = b_ref[...].astype(F32) * (w_ref[0:1, :] * s2 + w_ref[1:2, :] * s1 + w_ref[2:3, :] * cc)
        r = lax.rsqrt(_group_mean(yc * yc) + EPS)
        z = z_ref[...].astype(F32)
        y_ref[...] = (yc * r * g_ref[...] * (z * _sigmoid(z))).astype(BF16)

    return pl.pallas_call(
        body,
        name="conv_fwd",
        grid=(nb_seq, CONV_WIDTH // LANES),
        in_specs=_conv_specs(tp, lambda b, t, off: (b, off + t)) + [
            pl.BlockSpec((8, LANES), lambda b, t: (0, t)),
            pl.BlockSpec((1, LANES), lambda b, t: (0, t))],
        out_specs=pl.BlockSpec((tp, LANES), lambda b, t: (b, t)),
        out_shape=jax.ShapeDtypeStruct((nb_seq * tp, CONV_WIDTH), BF16),
        compiler_params=_params("parallel", "parallel"),
    )(p, p, p, p, conv_w, g_conv)


def _token_copy(hbm, b, k, ts, buf, sem, to_hbm=False):
    lo, hi = max(k * ts - LANES, 0), (k + 1) * ts - LANES
    off = lo - (k * ts - LANES)
    src, dst = hbm.at[b, pl.ds(lo, hi - lo)], buf.at[pl.ds(off, hi - lo)]
    if to_hbm:
        src, dst = dst, src
    return pltpu.make_async_copy(src, dst, sem)


def _for_tile(k, nt, fn):
    for kk in range(nt):
        @pl.when(k == kk)
        def _(kk=kk):
            fn(kk)


def _out_proj_loss(ya, yc, w_out, x, target, g_final, nt):
    nb_seq, s, d = x.shape
    r, ka = ya.shape
    ts = (s + LANES) // nt
    steps = nb_seq * nt

    def body(a_ref, c_ref, w_ref, x_hbm, t_hbm, g_ref, dhb_ref, dg_ref, loss_ref,
             xbuf, tbuf, acc_ref, sems):
        i = pl.program_id(0)
        b, k = i // nt, i % nt

        @pl.when(i == 0)
        def _():
            acc_ref[...] = jnp.zeros_like(acc_ref)
            dg_ref[...] = jnp.zeros_like(dg_ref)

        slot = i % 2

        def fetch(seq, kk, sl):
            return [_token_copy(x_hbm, seq, kk, ts, xbuf.at[sl], sems.at[sl, 0]),
                    _token_copy(t_hbm, seq, kk, ts, tbuf.at[sl], sems.at[sl, 1])]

        def start(seq, sl, kk):
            if kk == 0:
                xbuf[sl, 0:LANES, :] = jnp.zeros((LANES, d), F32)
                tbuf[sl, 0:LANES, :] = jnp.zeros((LANES, d), F32)
            for cp in fetch(seq, kk, sl):
                cp.start()

        @pl.when(i == 0)
        def _():
            start(0, 0, 0)

        @pl.when(i + 1 < steps)
        def _():
            _for_tile((i + 1) % nt, nt, functools.partial(start, (i + 1) // nt, 1 - slot))

        mix = _dot(a_ref[...], w_ref[0:ka, :]) + _dot(c_ref[...], w_ref[ka:, :])
        _for_tile(k, nt, lambda kk: [cp.wait() for cp in fetch(b, kk, slot)])

        real = (lax.broadcasted_iota(jnp.int32, (ts, d), 0) >= LANES) | (k > 0)
        g = g_ref[...]
        hhat, rstd = _rms_stats(xbuf[slot] + mix)
        e = jnp.where(real, hhat * g - tbuf[slot], 0.0)
        acc_ref[...] += jnp.sum(e * e, axis=0, keepdims=True)
        dy = e * (1.0 / d)
        dg_ref[...] += jnp.sum(dy * hhat, axis=0, keepdims=True)
        dh = _rms_bwd(g * dy, hhat, rstd)
        dhb_ref[...] = dh.astype(BF16)

        @pl.when(i == steps - 1)
        def _():
            total = jnp.sum(acc_ref[...], axis=1, keepdims=True)
            loss_ref[...] = jnp.broadcast_to((0.5 / d) * total, loss_ref.shape)

    hbm = pl.BlockSpec(memory_space=pl.ANY)
    row = pl.BlockSpec((ts, d), lambda i: (i, 0))
    vec = pl.BlockSpec((1, d), lambda i: (0, 0))
    return pl.pallas_call(
        body,
        name="out_proj_loss",
        grid=(steps,),
        in_specs=[pl.BlockSpec((ts, ka), lambda i: (i, 0)), pl.BlockSpec((ts, yc.shape[1]), lambda i: (i, 0)),
                  pl.BlockSpec(w_out.shape, lambda i: (0, 0)), hbm, hbm, vec],
        out_specs=(row, vec, pl.BlockSpec((1, LANES), lambda i: (0, 0))),
        out_shape=(
            jax.ShapeDtypeStruct((r, d), BF16),
            jax.ShapeDtypeStruct((1, d), F32),
            jax.ShapeDtypeStruct((1, LANES), F32),
        ),
        scratch_shapes=[pltpu.VMEM((2, ts, d), F32), pltpu.VMEM((2, ts, d), F32), pltpu.VMEM((1, d), F32),
                        pltpu.SemaphoreType.DMA((2, 2))],
        compiler_params=_params("arbitrary"),
    )(ya, yc, w_out, x, target, g_final)


def _out_proj_bwd(dhb, w_out, ya, yc, bm):
    r, d = dhb.shape
    ka = ya.shape[1]
    n_mix = w_out.shape[0]
    last = r // bm - 1

    def body(dh_ref, w_ref, a_ref, c_ref, dcat_ref, dw_ref, acc_ref):
        @pl.when(pl.program_id(0) == 0)
        def _():
            acc_ref[...] = jnp.zeros_like(acc_ref)

        dh = dh_ref[...]
        dcat_ref[...] = _dot(dh, w_ref[...], _NT).astype(BF16)
        acc_ref[0:ka, :] += _dot(a_ref[...], dh, _TN)
        acc_ref[ka:, :] += _dot(c_ref[...], dh, _TN)

        @pl.when(pl.program_id(0) == last)
        def _():
            dw_ref[...] = acc_ref[...].astype(BF16)

    return pl.pallas_call(
        body,
        name="out_proj_bwd",
        grid=(r // bm,),
        in_specs=[pl.BlockSpec((bm, d), lambda i: (i, 0)), pl.BlockSpec(w_out.shape, lambda i: (0, 0)),
                  pl.BlockSpec((bm, ka), lambda i: (i, 0)), pl.BlockSpec((bm, yc.shape[1]), lambda i: (i, 0))],
        out_specs=(pl.BlockSpec((bm, n_mix), lambda i: (i, 0)),
                   pl.BlockSpec((n_mix, d), lambda i: (0, 0))),
        out_shape=(jax.ShapeDtypeStruct((r, n_mix), BF16),
                   jax.ShapeDtypeStruct((n_mix, d), BF16)),
        scratch_shapes=[pltpu.VMEM((n_mix, d), F32)],
        compiler_params=_params("arbitrary"),
    )(dhb, w_out, ya, yc)


def _attn_bwd(q, k, v, o, lse, dcat, p, g_attn):
    nb_seq, _, tp, _ = q.shape

    def body(q_ref, k_ref, v_ref, o_ref, lse_ref, dy_ref, z_ref, g_ref,
             dq_ref, dk_ref, dv_ref, dz_ref, dg_ref, dq_acc):
        @pl.when(pl.program_id(1) == 0)
        def _():
            dg_ref[...] = jnp.zeros_like(dg_ref)

        g = g_ref[...]
        z = z_ref[...].astype(F32)
        o = o_ref[0, 0]
        dy = dy_ref[...].astype(F32)
        sig = _sigmoid(z)
        ohat, r = _rms_stats(o)
        don = dy * (z * sig)
        dz_ref[...] = (dy * (ohat * g) * (sig * (1.0 + z * (1.0 - sig)))).astype(BF16)
        dg_ref[...] += jnp.sum(don * ohat, axis=0, keepdims=True)
        do = _rms_bwd(g * don, ohat, r)
        dvec = jnp.sum(do * o, axis=-1, keepdims=True)
        dob = do.astype(BF16)
        lse_col = lse_ref[0, 0, :, 0:1]
        dq_acc[...] = jnp.zeros_like(dq_acc)
        for k0 in range(0, tp, K_TILE):
            nk = min(K_TILE, tp - k0)
            nq = tp - k0
            qq = q_ref[0, 0, k0:, :]
            kk = k_ref[0, 0, k0:k0 + nk, :]
            causal = (lax.broadcasted_iota(jnp.int32, (nq, nk), 1) <= lax.broadcasted_iota(jnp.int32, (nq, nk), 0))
            pr = jnp.where(causal, jnp.exp2(_dot(qq, kk, _NT) - lse_col[k0:]), 0.0)
            dp = _dot(dob[k0:], v_ref[0, 0, k0:k0 + nk, :], _NT)
            ds = (pr * (dp - dvec[k0:])).astype(BF16)
            dv_ref[0, 0, k0:k0 + nk, :] = _dot(pr.astype(BF16), dob[k0:], _TN).astype(BF16)
            dk_ref[0, 0, k0:k0 + nk, :] = (_dot(ds, qq, _TN) * (ATTN_SCALE / Q_SCALE)).astype(BF16)
            dq_acc[k0:, :] += _dot(ds, kk)
        dq_ref[0, 0] = (dq_acc[...] * ATTN_SCALE).astype(BF16)

    qk = pl.BlockSpec((1, 1, tp, 2 * LANES), lambda h, b: (b, h, 0, 0))
    hv = pl.BlockSpec((1, 1, tp, D_V), lambda h, b: (b, h, 0, 0))
    col = pl.BlockSpec((tp, LANES), lambda h, b: (b, h))
    return pl.pallas_call(
        body,
        name="attn_bwd",
        grid=(N_HEADS, nb_seq),
        in_specs=[qk, qk, hv, hv, hv, col,
                  pl.BlockSpec((tp, LANES), lambda h, b: (b, GRP_A // LANES + h)),
                  pl.BlockSpec((1, LANES), lambda h, b: (0, h))],
        out_specs=(qk, qk, hv, col, pl.BlockSpec((1, LANES), lambda h, b: (0, h))),
        out_shape=(
            jax.ShapeDtypeStruct((nb_seq, N_HEADS, tp, 2 * LANES), BF16),
            jax.ShapeDtypeStruct((nb_seq, N_HEADS, tp, 2 * LANES), BF16),
            jax.ShapeDtypeStruct((nb_seq, N_HEADS, tp, D_V), BF16),
            jax.ShapeDtypeStruct((nb_seq * tp, N_HEADS * D_V), BF16),
            jax.ShapeDtypeStruct((1, N_HEADS * D_V), F32),
        ),
        scratch_shapes=[pltpu.VMEM((tp, 2 * LANES), F32)],
        compiler_params=_params("arbitrary", "arbitrary"),
    )(q, k, v, o, lse, dcat, p, g_attn)


def _qkv_bwd(p, dq, dk, dv, wq, wkv, gq, gkv, tables):
    nb_seq, _, tp, _ = dq.shape
    ht = tp // 2

    def body(pa_ref, dq_ref, dk_ref, dv_ref, wq_ref, wkv_ref, gq_ref, gkv_ref, cos_ref, sa_ref, sb_ref,
             dpa_ref, dwq_ref, dwkv_ref, dgq_ref, dgkv_ref):
        @pl.when(pl.program_id(0) == 0)
        def _():
            dwq_ref[...] = jnp.zeros_like(dwq_ref)
            dwkv_ref[...] = jnp.zeros_like(dwkv_ref)
            dgq_ref[...] = jnp.zeros_like(dgq_ref)
            dgkv_ref[...] = jnp.zeros_like(dgkv_ref)

        pa = pa_ref[...].astype(F32)
        gq, gkv = gq_ref[...], gkv_ref[...]
        cq_hat, rq = _rms_stats(pa[:, :Q_RANK])
        ckv_hat, rkv = _rms_stats(pa[:, Q_RANK:Q_RANK + KV_RANK])
        tabs = (cos_ref[...], sa_ref[...], sb_ref[...])

        pe = [dq_ref[0, h, :, D_NOPE:].astype(F32) for h in range(N_HEADS)]
        pairs = [_rope_t(pe[2 * i] + pltpu.roll(pe[2 * i + 1], D_ROPE, 1), *tabs).astype(BF16) for i in range(2)]
        dq_flat = jnp.concatenate([dq_ref[0, h, :, :D_NOPE] for h in range(N_HEADS)] + pairs, axis=1)
        dwq_ref[...] += _dot(dq_flat, (cq_hat * gq).astype(BF16), _TN)
        dcqn = _dot(dq_flat, wq_ref[...])
        dgq_ref[...] += jnp.sum(dcqn * cq_hat, axis=0, keepdims=True)
        dcq = _rms_bwd(gq * dcqn, cq_hat, rq)

        dkv_flat = jnp.concatenate([dk_ref[0, h, :, :D_NOPE] for h in range(N_HEADS)]
                                   + [dv_ref[0, h] for h in range(N_HEADS)], axis=1)
        dwkv_ref[...] += _dot((ckv_hat * gkv).astype(BF16), dkv_flat, _TN)
        dckvn = _dot(dkv_flat, wkv_ref[...], _NT)
        dgkv_ref[...] += jnp.sum(dckvn * ckv_hat, axis=0, keepdims=True)
        dckv = _rms_bwd(gkv * dckvn, ckv_hat, rkv)

        dk_pe = dk_ref[0, 0, :, D_NOPE:].astype(F32)
        for h in range(1, N_HEADS):
            dk_pe = dk_pe + dk_ref[0, h, :, D_NOPE:].astype(F32)
        dk_pe = jnp.where(lax.broadcasted_iota(jnp.int32, (ht, LANES), 1) < D_ROPE, dk_pe, 0.0)
        dpa_ref[...] = jnp.concatenate([dcq, dckv, _rope_t(dk_pe, *tabs)], axis=1).astype(BF16)

    full = lambda a: pl.BlockSpec(a.shape, lambda i: (0,) * a.ndim)
    tab = pl.BlockSpec((ht, LANES), lambda i: (i % 2, 0))
    qk = pl.BlockSpec((1, N_HEADS, ht, 2 * LANES), lambda i: (i // 2, 0, i % 2, 0))
    acc = lambda shape: pl.BlockSpec(shape, lambda i: (0, 0))
    return pl.pallas_call(
        body,
        name="qkv_bwd",
        grid=(2 * nb_seq,),
        in_specs=[pl.BlockSpec((ht, GRP_A), lambda i: (i, 0)), qk, qk,
                  pl.BlockSpec((1, N_HEADS, ht, D_V), lambda i: (i // 2, 0, i % 2, 0)),
                  full(wq), full(wkv), full(gq), full(gkv), tab, tab, tab],
        out_specs=(pl.BlockSpec((ht, GRP_A), lambda i: (i, 0)),
                   acc(wq.shape), acc(wkv.shape), acc((1, Q_RANK)), acc((1, KV_RANK))),
        out_shape=(
            jax.ShapeDtypeStruct((nb_seq * tp, GRP_A), BF16),
            jax.ShapeDtypeStruct(wq.shape, F32),
            jax.ShapeDtypeStruct(wkv.shape, F32),
            jax.ShapeDtypeStruct((1, Q_RANK), F32),
            jax.ShapeDtypeStruct((1, KV_RANK), F32),
        ),
        compiler_params=_params("arbitrary"),
    )(p, dq, dk, dv, wq, wkv, gq, gkv, *tables)


def _conv_bwd(p, dcat, conv_w, g_conv, nb_seq, tp):
    cols = CONV_WIDTH // LANES

    def body(b_ref, c_ref, h_ref, z_ref, dy_ref, w_ref, g_ref,
             db_ref, dc_ref, dh_ref, dz_ref, dw_ref, dg_ref):
        @pl.when(pl.program_id(1) == 0)
        def _():
            dw_ref[...] = jnp.zeros_like(dw_ref)
            dg_ref[...] = jnp.zeros_like(dg_ref)

        cb, c, h = b_ref[...].astype(F32), c_ref[...].astype(F32), h_ref[...].astype(F32)
        z, dy = z_ref[...].astype(F32), dy_ref[...].astype(F32)
        g = g_ref[...]
        w0, w1, w2 = w_ref[0:1, :], w_ref[1:2, :], w_ref[2:3, :]
        cc = c * h
        row = lax.broadcasted_iota(jnp.int32, (tp, LANES), 0)
        s1 = jnp.where(row >= 1, pltpu.roll(cc, 1, 0), 0.0)
        s2 = jnp.where(row >= 2, pltpu.roll(cc, 2, 0), 0.0)
        dwc = w0 * s2 + w1 * s1 + w2 * cc
        yc = cb * dwc
        r = lax.rsqrt(_group_mean(yc * yc) + EPS)
        ychat = yc * r
        sig = _sigmoid(z)
        dz_ref[...] = (dy * (ychat * g) * (sig * (1.0 + z * (1.0 - sig)))).astype(BF16)
        dyn = dy * (z * sig)
        dg_ref[...] += jnp.sum(dyn * ychat, axis=0, keepdims=True)
        gd = g * dyn
        dyc = r * (gd - ychat * _group_mean(gd * ychat))
        db_ref[...] = (dyc * dwc).astype(BF16)
        ddw = dyc * cb
        dw_ref[0:1, :] += jnp.sum(ddw * s2, axis=0, keepdims=True)
        dw_ref[1:2, :] += jnp.sum(ddw * s1, axis=0, keepdims=True)
        dw_ref[2:3, :] += jnp.sum(ddw * cc, axis=0, keepdims=True)
        u1 = jnp.where(row <= tp - 2, pltpu.roll(ddw, tp - 1, 0), 0.0)
        u2 = jnp.where(row <= tp - 3, pltpu.roll(ddw, tp - 2, 0), 0.0)
        dcc = w2 * ddw + w1 * u1 + w0 * u2
        dc_ref[...] = (dcc * h).astype(BF16)
        dh_ref[...] = (dcc * c).astype(BF16)

    col = pl.BlockSpec((tp, LANES), lambda t, b: (b, t))
    out = jax.ShapeDtypeStruct((nb_seq * tp, CONV_WIDTH), BF16)
    return pl.pallas_call(
        body,
        name="conv_bwd",
        grid=(cols, nb_seq),
        in_specs=_conv_specs(tp, lambda t, b, off: (b, off + t)) + [
            pl.BlockSpec((tp, LANES), lambda t, b: (b, N_HEADS * D_V // LANES + t)),
            pl.BlockSpec((8, LANES), lambda t, b: (0, t)),
            pl.BlockSpec((1, LANES), lambda t, b: (0, t))],
        out_specs=(col, col, col, col,
                   pl.BlockSpec((8, LANES), lambda t, b: (0, t)), pl.BlockSpec((1, LANES), lambda t, b: (0, t))),
        out_shape=(out, out, out, out,
                   jax.ShapeDtypeStruct((8, CONV_WIDTH), F32), jax.ShapeDtypeStruct((1, CONV_WIDTH), F32)),
        compiler_params=_params("arbitrary", "arbitrary"),
    )(p, p, p, p, dcat, conv_w, g_conv)


def _input_bwd(dps, w_in, x, meta, dh, norm_g, nt, send_in):
    nb_seq, s, d = x.shape
    r, kb = dps[0].shape
    ts = (s + LANES) // nt
    steps = nb_seq * nt
    n_dp = len(dps)
    in_slot = send_in.shape[1:]

    def body(*refs):
        dp_refs, w_ref, x_hbm, meta_ref, dh_ref, g_ref, pay_ref = refs[:n_dp], *refs[n_dp:n_dp + 6]
        o = n_dp + 6
        gx_hbm, dmeta_ref, dg_ref, r2_in = refs[o:o + 4]
        xbuf, gxbuf, tok_sems, own_in, r1_in, sum_in = refs[o + 4:o + 10]
        sems = refs[o + 10:]
        i = pl.program_id(0)
        b, k = i // nt, i % nt

        def plan():
            return _reduce_plan((pay_ref,), (own_in,), (r1_in,), (sum_in,), (r2_in,), *sems)

        @pl.when(i == 0)
        def _():
            dmeta_ref[...] = jnp.zeros_like(dmeta_ref)
            dg_ref[...] = jnp.zeros_like(dg_ref)
            plan()[0]()

        @pl.when(i == 1)
        def _():
            plan()[1]()

        def start(kk):
            if kk == 0:
                xbuf[0:PAD_FRONT, :] = jnp.zeros((PAD_FRONT, d), F32)
                xbuf[PAD_FRONT:LANES, :] = meta_ref[...]
            _token_copy(x_hbm, b, kk, ts, xbuf, tok_sems.at[0]).start()

        _for_tile(k, nt, start)
        du = _dot(dp_refs[0][...], w_ref[0:kb, :])
        for j in range(1, n_dp):
            du = du + _dot(dp_refs[j][...], w_ref[kb * j:kb * (j + 1), :])
        _for_tile(k, nt, lambda kk: _token_copy(x_hbm, b, kk, ts, xbuf, tok_sems.at[0]).wait())

        g = g_ref[...]
        hhat, rstd = _rms_stats(xbuf[...])
        dg_ref[...] += jnp.sum(du * hhat, axis=0, keepdims=True)
        res = _rms_bwd(g * du, hhat, rstd) + dh_ref[...].astype(F32)

        @pl.when(i > 0)
        def _():
            _for_tile(k, nt, lambda kk: _token_copy(gx_hbm, b, (kk - 1) % nt, ts, gxbuf, tok_sems.at[1], True).wait())

        gxbuf[...] = res

        @pl.when(k == 0)
        def _():
            dmeta_ref[...] += gxbuf[PAD_FRONT:LANES, :]

        _for_tile(k, nt, lambda kk: _token_copy(gx_hbm, b, kk, ts, gxbuf, tok_sems.at[1], True).start())

        @pl.when(i == steps - 1)
        def _():
            _token_copy(gx_hbm, b, nt - 1, ts, gxbuf, tok_sems.at[1], True).wait()
            plan()[2]()

    whole = lambda a: pl.BlockSpec(a.shape, lambda i: (0,) * a.ndim)
    hbm = pl.BlockSpec(memory_space=pl.ANY)
    return pl.pallas_call(
        body,
        name="input_bwd",
        grid=(steps,),
        in_specs=[pl.BlockSpec((ts, kb), lambda i: (i, 0)) for _ in dps]
        + [whole(w_in), hbm, whole(meta), pl.BlockSpec((ts, d), lambda i: (i, 0)), whole(norm_g), hbm],
        out_specs=(hbm, pl.BlockSpec((N_META, d), lambda i: (0, 0)), pl.BlockSpec((1, d), lambda i: (0, 0)), hbm),
        out_shape=(jax.ShapeDtypeStruct((nb_seq, s, d), F32),
                   jax.ShapeDtypeStruct((N_META, d), F32),
                   jax.ShapeDtypeStruct((1, d), F32),
                   jax.ShapeDtypeStruct((N_CHIPS,) + in_slot, BF16)),
        scratch_shapes=[pltpu.VMEM((ts, d), F32), pltpu.VMEM((ts, d), F32), pltpu.SemaphoreType.DMA((2,))]
        + _reduce_scratch([(in_slot, BF16)], [True]),
        compiler_params=_params("arbitrary"),
    )(*dps, w_in, x, meta, dh, norm_g, send_in)


def _in_proj_bwd_w(u, dps, bm, small_grads, send_out):
    r, d = u.shape
    kb = dps[0].shape[1]
    steps = r // bm
    n_dp, n_small = len(dps), len(small_grads)
    out_slot, small_slot = send_out.shape[1:], (SMALL_ROWS, LANES)

    def body(*refs):
        u_ref, dp_refs = refs[0], refs[1:1 + n_dp]
        small_refs = refs[1 + n_dp:1 + n_dp + n_small]
        o = 1 + n_dp + n_small
        pay_out, o_ref, r2_out, r2_small = refs[o:o + 4]
        acc_ref, ssmall, r1_out, sum_out, r1_small, sum_small = refs[o + 4:o + 10]
        sems = refs[o + 10:]
        i = pl.program_id(0)

        def plan():
            return _reduce_plan((pay_out, ssmall), (None, None), (r1_out, r1_small), (sum_out, sum_small),
                                (r2_out, r2_small), *sems)

        @pl.when(i == 0)
        def _():
            acc_ref[...] = jnp.zeros_like(acc_ref)
            _pack_small(ssmall, *small_refs)
            plan()[0]()

        @pl.when(i == 1)
        def _():
            plan()[1]()

        uu = u_ref[...]
        for j in range(n_dp):
            acc_ref[kb * j:kb * (j + 1), :] += _dot(dp_refs[j][...], uu, _TN)

        @pl.when(i == steps - 1)
        def _():
            for k in range(N_DEV):
                for s, e, c0 in _in_pieces(k):
                    o_ref[k, s:e, :] = acc_ref[c0:c0 + e - s, :].astype(BF16)
                o_ref[k, SHARD_IN:, :] = jnp.zeros((SHARD_IN_PAD - SHARD_IN, d), BF16)
            plan()[2]()

    whole = lambda a: pl.BlockSpec(a.shape, lambda i: (0,) * a.ndim)
    hbm = pl.BlockSpec(memory_space=pl.ANY)
    return pl.pallas_call(
        body,
        name="in_proj_bwd_w",
        grid=(steps,),
        in_specs=[pl.BlockSpec((bm, d), lambda i: (i, 0))]
        + [pl.BlockSpec((bm, kb), lambda i: (i, 0)) for _ in dps] + [whole(a) for a in small_grads]
        + [whole(send_out)],
        out_specs=(pl.BlockSpec((N_DEV, SHARD_IN_PAD, d), lambda i: (0, 0, 0)), hbm, hbm),
        out_shape=(jax.ShapeDtypeStruct((N_DEV, SHARD_IN_PAD, d), BF16),
                   jax.ShapeDtypeStruct((N_CHIPS,) + out_slot, BF16),
                   jax.ShapeDtypeStruct((N_CHIPS,) + small_slot, F32)),
        scratch_shapes=[pltpu.VMEM((kb * n_dp, d), F32), pltpu.VMEM((N_DEV,) + small_slot, F32)]
        + _reduce_scratch([(out_slot, BF16), (small_slot, F32)], [False, False]),
        compiler_params=_params("arbitrary"),
    )(u, *dps, *small_grads, send_out)


def _local_step(x, loss_target, u, p, meta_f, norm_g, w_in_p, q_norm_g, w_q_p, kv_norm_g, w_kv_p, conv_w_f,
                attn_out_g, conv_out_g, w_out_f, g_final):
    nb_seq, s, d = x.shape
    tp = s + LANES
    ht = tp // 2
    tables = _rope_tables(tp)

    q, k, v = _qkv_fwd(p, w_q_p, w_kv_p, q_norm_g, kv_norm_g, tables, nb_seq, tp)
    ya, o, lse = _attn_fwd(q, k, v, p, attn_out_g)
    yc = _conv_fwd(p, conv_w_f, conv_out_g, nb_seq, tp)
    dhb, d_final_g, loss_part = _out_proj_loss(ya, yc, w_out_f, x, loss_target, g_final, TOKEN_TILES)

    dcat, d_w_out = _out_proj_bwd(dhb, w_out_f, ya, yc, ht)
    send_out = d_w_out.reshape(N_DEV, SHARD_OUT, d)
    dq, dk, dv, dz_attn, d_attn_g = _attn_bwd(q, k, v, o, lse, dcat, p, attn_out_g)
    dpa, d_wq_p, d_wkv_p, d_gq, d_gkv = _qkv_bwd(p, dq, dk, dv, w_q_p, w_kv_p, q_norm_g, kv_norm_g, tables)
    d_b, d_c, d_h, dz_conv, d_conv_w, d_conv_g = _conv_bwd(p, dcat, conv_w_f, conv_out_g, nb_seq, tp)
    dps = (dpa, dz_attn, d_b, d_c, d_h, dz_conv)
    small = (d_wq_p, d_wkv_p, d_conv_w, d_final_g, d_gq, d_gkv, d_attn_g, d_conv_g, loss_part)
    send_in, r_out, r_small = _in_proj_bwd_w(u, dps, ht, small, send_out)
    grad_x, d_meta, d_norm_g, r_in = _input_bwd(dps, w_in_p, x, meta_f, dhb, norm_g, TOKEN_TILES // 2, send_in)
    return grad_x, r_in, r_out, r_small, d_meta, d_norm_g


def kernel(x, meta_tokens, norm_g, w_in, q_norm_g, w_q_up, kv_norm_g, w_kv_up, conv_w, attn_out_g, conv_out_g, w_out, final_norm_g, loss_target, m_meta_tokens, m_norm_g, m_w_in, m_q_norm_g, m_w_q_up, m_kv_norm_g, m_w_kv_up, m_conv_w, m_attn_out_g, m_conv_out_g, m_w_out, m_final_norm_g, v_meta_tokens, v_norm_g, v_w_in, v_q_norm_g, v_w_q_up, v_kv_norm_g, v_w_kv_up, v_conv_w, v_attn_out_g, v_conv_out_g, v_w_out, v_final_norm_g):
    d = x.shape[-1]
    ht = (x.shape[1] + LANES) // 2
    u, w_in_p, meta_f = _prep_gather(x, meta_tokens, norm_g, w_in[0].T)
    p, w_q_p, w_kv_p, w_out_f, conv_w_f = _in_proj_gather(
        u, w_in_p, w_q_up[0].T, w_kv_up[0], w_out[0], conv_w[0], ht, 3 * GRP_A)
    g_final = final_norm_g.reshape(1, d)
    grad_x, r_in, r_out, r_small, d_meta, d_norm_g = _local_step(
        x, loss_target, u, p, meta_f, norm_g, w_in_p, q_norm_g, w_q_p, kv_norm_g, w_kv_p, conv_w_f,
        attn_out_g, conv_out_g, w_out_f, g_final)

    flat = lambda a: a.reshape(a.shape[-2:]) if a.ndim == 3 else a.reshape(1, -1) if a.ndim == 1 else a
    transposed = ("w_in", "w_q_up")
    to_kernel = lambda n, a: flat(a).T if n in transposed else flat(a)
    from_kernel = lambda n, a, shape: (a.T if n in transposed else a).reshape(shape)
    params = {
        "meta_tokens": (meta_tokens, m_meta_tokens, v_meta_tokens),
        "norm_g": (norm_g, m_norm_g, v_norm_g),
        "w_in": (w_in, m_w_in, v_w_in),
        "q_norm_g": (q_norm_g, m_q_norm_g, v_q_norm_g),
        "w_q_up": (w_q_up, m_w_q_up, v_w_q_up),
        "kv_norm_g": (kv_norm_g, m_kv_norm_g, v_kv_norm_g),
        "w_kv_up": (w_kv_up, m_w_kv_up, v_w_kv_up),
        "conv_w": (conv_w, m_conv_w, v_conv_w),
        "attn_out_g": (attn_out_g, m_attn_out_g, v_attn_out_g),
        "conv_out_g": (conv_out_g, m_conv_out_g, v_conv_out_g),
        "w_out": (w_out, m_w_out, v_w_out),
        "final_norm_g": (final_norm_g, m_final_norm_g, v_final_norm_g),
    }
    grads, loss = _reduce_tail(r_in, r_out, r_small, d_meta, d_norm_g)
    updated = _adamw(grads, {n: tuple(to_kernel(n, a) for a in t) for n, t in params.items()})
    outs = [[from_kernel(n, updated[n][i], params[n][0].shape) for n, _ in PARAM_SHAPES] for i in range(4)]
    return (loss[0, 0], grad_x, *outs[0], *outs[1], *outs[2], *outs[3])
```

```python
import functools

import jax
import jax.numpy as jnp
from jax import lax
from jax.experimental import pallas as pl
from jax.experimental.pallas import tpu as pltpu

F32 = jnp.float32
BF16 = jnp.bfloat16

N_META = 16
D_MODEL = 1024
N_HEADS = 4
D_NOPE = 128
D_ROPE = 64
D_V = 128
Q_RANK = 256
KV_RANK = 128
CONV_WIDTH = 512
CONV_GROUP = 64
ROPE_THETA = 10000.0
ATTN_SCALE = (D_NOPE + D_ROPE) ** -0.5
Q_SCALE = ATTN_SCALE * 1.4426950408889634
EPS = 1e-6
NEG_INF = -1e30

ADAM_LR = 0.001
ADAM_B1 = 0.9
ADAM_B2 = 0.999
ADAM_EPS = 1e-08
ADAM_WD = 0.01
ADAM_STEP = 10

LANES = 128
PAD_FRONT = LANES - N_META
K_TILE = 256
Q_TILE = 512
N_DEV = 8
VMEM_LIMIT = 56 * 1024 * 1024

IN_PAD = 3072
GRP_A = 512
N_A = Q_RANK + KV_RANK + D_ROPE
IN_PROJ = 3008
SHARD_IN = IN_PROJ // N_DEV
SHARD_IN_PAD = 384
SHARD_Q = 96
SHARD_KV = 128
SHARD_OUT = 128
SHARD_CONV = 64
SHARD_META = 128
Q_COLS = N_HEADS * (D_NOPE + D_ROPE)
KV_COLS = N_HEADS * (D_NOPE + D_V)

ROW_Q, ROW_KV, ROW_META, ROW_CONV = 0, 256, 384, 400
ROW_REPL = 408
ROW_NORM, ROW_FINAL, ROW_GQ, ROW_GKV, ROW_ATTN, ROW_CONVG, ROW_LOSS = 408, 416, 424, 426, 427, 431, 435
SMALL_ROWS = 440

PARAM_SHAPES = (
    ("meta_tokens", (N_META, SHARD_META)), ("norm_g", (1, D_MODEL)), ("w_in", (SHARD_IN, D_MODEL)),
    ("q_norm_g", (1, Q_RANK)), ("w_q_up", (SHARD_Q, Q_RANK)), ("kv_norm_g", (1, KV_RANK)),
    ("w_kv_up", (KV_RANK, SHARD_KV)), ("conv_w", (3, SHARD_CONV)), ("attn_out_g", (1, CONV_WIDTH)),
    ("conv_out_g", (1, CONV_WIDTH)), ("w_out", (SHARD_OUT, D_MODEL)), ("final_norm_g", (1, D_MODEL)),
)


def _in_pieces(k):
    lo, hi = SHARD_IN * k, SHARD_IN * (k + 1)
    out = []
    if lo < N_A:
        out.append((0, min(hi, N_A) - lo, lo))
    if hi > N_A:
        s = max(lo, N_A)
        out.append((s - lo, hi - lo, s + GRP_A - N_A))
    return out


def _q_pieces(k):
    lo, hi = SHARD_Q * k, SHARD_Q * (k + 1)
    out = []
    for h in range(N_HEADS):
        base = (D_NOPE + D_ROPE) * h
        s, e = max(lo, base), min(hi, base + D_NOPE)
        if s < e:
            out.append((s - lo, e - lo, D_NOPE * h + s - base))
        s, e = max(lo, base + D_NOPE), min(hi, base + D_NOPE + D_ROPE)
        if s < e:
            out.append((s - lo, e - lo, N_HEADS * D_NOPE + D_ROPE * h + s - base - D_NOPE))
    return out


def _kv_dst(k):
    return D_NOPE * (k // 2) + (N_HEADS * D_NOPE if k % 2 else 0)


def _params(*sem):
    return pltpu.CompilerParams(dimension_semantics=sem, vmem_limit_bytes=VMEM_LIMIT)


def _rms_stats(x):
    r = lax.rsqrt(jnp.mean(x * x, axis=-1, keepdims=True) + EPS)
    return x * r, r


def _rms_bwd(gdy, xhat, r):
    return r * (gdy - xhat * jnp.mean(gdy * xhat, axis=-1, keepdims=True))


def _sigmoid(z):
    return 1.0 / (1.0 + jnp.exp(-z))


def _group_mean(x):
    i0 = lax.broadcasted_iota(jnp.int32, (LANES, LANES), 0) // CONV_GROUP
    i1 = lax.broadcasted_iota(jnp.int32, (LANES, LANES), 1) // CONV_GROUP
    m = jnp.where(i0 == i1, 1.0 / CONV_GROUP, 0.0).astype(BF16)
    hi = x.astype(BF16)
    lo = (x - hi.astype(F32)).astype(BF16)
    return jnp.dot(hi, m, preferred_element_type=F32) + jnp.dot(lo, m, preferred_element_type=F32)


_NT = (((1,), (1,)), ((), ()))
_TN = (((0,), (0,)), ((), ()))


def _dot(a, b, dims=None):
    if dims is None:
        return jnp.dot(a, b, preferred_element_type=F32)
    return lax.dot_general(a, b, dims, preferred_element_type=F32)


def _device_position():
    x, y, c = lax.axis_index("x"), lax.axis_index("y"), lax.axis_index("c")
    return x, y, c, 4 * x + 2 * y + c


def _gather_plan(srcs, slots, send_sems, recv_sems, local_sems):
    x, y, c, _ = _device_position()
    me, sibling = (x, y, c), (x, y, 1 - c)
    flip = lambda v, on: v + on - 2 * v * on
    near = (flip(x, 1 - c), flip(y, c))
    far = (flip(x, c), flip(y, 1 - c))
    diag = (1 - x, 1 - y)
    n = len(srcs)

    def slot(a, px, py, pc):
        return slots[a].at[4 * px + 2 * py + pc]

    def copy(a, k, block, to, own=False):
        return pltpu.make_async_remote_copy(
            src_ref=srcs[a] if own else slot(a, *block),
            dst_ref=slot(a, *block),
            send_sem=send_sems.at[7 * a + k],
            recv_sem=recv_sems.at[7 * a + k],
            device_id=to,
            device_id_type=pl.DeviceIdType.MESH,
        )

    def local(a):
        return pltpu.make_async_copy(srcs[a], slot(a, *me), local_sems.at[a])

    sent = [(me, sibling), (me, (*near, c)), (me, (*far, c)), ((*near, c), (*far, c)),
            ((*near, c), sibling), ((*far, c), sibling), ((*diag, c), sibling)]
    landed = [sibling, (*near, c), (*far, c), (*diag, c), (*far, 1 - c), (*near, 1 - c), (*diag, 1 - c)]

    def send(a, k):
        return copy(a, k, *sent[k], own=k < 3)

    def arrival(a, k):
        return copy(a, k, landed[k], me)

    def start():
        for a in range(n):
            local(a).start()
            for k in range(3):
                send(a, k).start()

    def relay():
        for a in range(n):
            arrival(a, 1).wait_recv()
            send(a, 3).start()
            send(a, 4).start()

    def forward():
        for k in (2, 3):
            for a in range(n):
                arrival(a, k).wait_recv()
                send(a, k + 3).start()

    def finish():
        for a in range(n):
            for k in (0, 4, 5, 6):
                arrival(a, k).wait_recv()
        for a in range(n):
            for k in range(7):
                send(a, k).wait_send()
            local(a).wait()

    return start, relay, forward, finish


def _adam_update(g, w, m, v):
    m_new = ADAM_B1 * m + (1.0 - ADAM_B1) * g
    v_new = ADAM_B2 * v + (1.0 - ADAM_B2) * (g * g)
    m_hat = m_new / (1.0 - ADAM_B1 ** ADAM_STEP)
    v_hat = v_new / (1.0 - ADAM_B2 ** ADAM_STEP)
    return -ADAM_LR * (m_hat / (jnp.sqrt(v_hat) + ADAM_EPS) + ADAM_WD * w), m_new, v_new


def _adamw(grads, params):
    names = [n for n, _ in PARAM_SHAPES]
    n_p = len(names)

    def body(*refs):
        for i in range(n_p):
            g = refs[i][...]
            w, m, v = (refs[n_p + 3 * i + j][...] for j in range(3))
            delta, m_new, v_new = _adam_update(g, w, m, v)
            for j, val in enumerate((g, delta, m_new, v_new)):
                refs[4 * n_p + 4 * i + j][...] = val

    vm = pl.BlockSpec(memory_space=pltpu.VMEM)
    out_shape = []
    for _, shape in PARAM_SHAPES:
        out_shape += [jax.ShapeDtypeStruct(shape, F32)] * 4
    outs = pl.pallas_call(
        body,
        name="adamw",
        out_shape=tuple(out_shape),
        in_specs=[vm] * (4 * n_p),
        out_specs=(vm,) * (4 * n_p),
        compiler_params=pltpu.CompilerParams(vmem_limit_bytes=VMEM_LIMIT),
    )(*[grads[n] for n in names], *[a for n in names for a in params[n]])
    return {n: outs[4 * i:4 * i + 4] for i, n in enumerate(names)}


N_CHIPS = 4


def _reduce_plan(pays, owns, r1s, sums, r2s, send1, recv1, send2, recv2, local_sems):
    x, y, c, _ = _device_position()
    sibling = (x, y, 1 - c)
    chips = [((1 - x if rj & 2 else x), (1 - y if rj & 1 else y)) for rj in range(N_CHIPS)]
    n = len(pays)

    def slot_of(rj, core):
        return 4 * chips[rj][0] + 2 * chips[rj][1] + core

    def to_sibling(a, rj):
        return pltpu.make_async_remote_copy(
            src_ref=pays[a].at[slot_of(rj, 1 - c)], dst_ref=r1s[a].at[rj],
            send_sem=send1.at[N_CHIPS * a + rj], recv_sem=recv1.at[N_CHIPS * a + rj],
            device_id=sibling, device_id_type=pl.DeviceIdType.MESH)

    def load_own(a, rj):
        return pltpu.make_async_copy(pays[a].at[slot_of(rj, c)], owns[a].at[rj], local_sems.at[2 * N_CHIPS * a + rj])

    def to_chip(a, rj):
        return pltpu.make_async_remote_copy(
            src_ref=sums[a].at[rj], dst_ref=r2s[a].at[rj],
            send_sem=send2.at[N_CHIPS * a + rj], recv_sem=recv2.at[N_CHIPS * a + rj],
            device_id=(*chips[rj], c), device_id_type=pl.DeviceIdType.MESH)

    def keep(a):
        return pltpu.make_async_copy(sums[a].at[0], r2s[a].at[0], local_sems.at[2 * N_CHIPS * a + N_CHIPS])

    def start():
        for a in range(n):
            for rj in range(N_CHIPS):
                to_sibling(a, rj).start()
                if owns[a] is not None:
                    load_own(a, rj).start()

    def combine():
        for a in range(n):
            for rj in range(N_CHIPS):
                to_sibling(a, rj).wait_recv()
                if owns[a] is not None:
                    load_own(a, rj).wait()
                    mine = owns[a][rj]
                else:
                    mine = pays[a][slot_of(rj, c)]
                sums[a][rj] = (mine.astype(F32) + r1s[a][rj].astype(F32)).astype(sums[a].dtype)
            keep(a).start()
            for rj in range(1, N_CHIPS):
                to_chip(a, rj).start()

    def finish():
        for a in range(n):
            for rj in range(1, N_CHIPS):
                to_chip(a, rj).wait_recv()
            for rj in range(N_CHIPS):
                to_sibling(a, rj).wait_send()
            for rj in range(1, N_CHIPS):
                to_chip(a, rj).wait_send()
            keep(a).wait()

    return start, combine, finish


def _reduce_scratch(shapes_dtypes, own_flags):
    out = []
    for (shape, dtype), own in zip(shapes_dtypes, own_flags):
        if own:
            out.append(pltpu.VMEM((N_CHIPS,) + shape, dtype))
        out += [pltpu.VMEM((N_CHIPS,) + shape, dtype), pltpu.VMEM((N_CHIPS,) + shape, dtype)]
    n = len(shapes_dtypes)
    out += [pltpu.SemaphoreType.DMA((N_CHIPS * n,))] * 4 + [pltpu.SemaphoreType.DMA((2 * N_CHIPS * n,))]
    return out


def _pack_small(ssmall, dwq, dwkv, dconv, dfinal, dgq, dgkv, dattn, dconvg, loss_part):
    ssmall[...] = jnp.zeros_like(ssmall)
    rep = ssmall.at[0]
    for i in range(D_MODEL // LANES):
        rep[ROW_FINAL + i:ROW_FINAL + i + 1, :] = dfinal[:, LANES * i:LANES * (i + 1)]
    for i in range(Q_RANK // LANES):
        rep[ROW_GQ + i:ROW_GQ + i + 1, :] = dgq[:, LANES * i:LANES * (i + 1)]
    rep[ROW_GKV:ROW_GKV + 1, :] = dgkv[...]
    for i in range(CONV_WIDTH // LANES):
        rep[ROW_ATTN + i:ROW_ATTN + i + 1, :] = dattn[:, LANES * i:LANES * (i + 1)]
        rep[ROW_CONVG + i:ROW_CONVG + i + 1, :] = dconvg[:, LANES * i:LANES * (i + 1)]
    rep[ROW_LOSS:ROW_LOSS + 1, :] = loss_part[...]
    for k in range(N_DEV):
        if k:
            ssmall[k, ROW_REPL:, :] = ssmall[0, ROW_REPL:, :]
        for s, e, d in _q_pieces(k):
            for i in range(Q_RANK // LANES):
                ssmall[k, ROW_Q + SHARD_Q * i + s:ROW_Q + SHARD_Q * i + e, :] = dwq[d:d + e - s, LANES * i:LANES * (i + 1)]
        ssmall[k, ROW_KV:ROW_KV + KV_RANK, :] = dwkv[:, _kv_dst(k):_kv_dst(k) + SHARD_KV]
        ssmall[k, ROW_CONV:ROW_CONV + 3, 0:SHARD_CONV] = dconv[0:3, SHARD_CONV * k:SHARD_CONV * (k + 1)]


TOKEN_TILES = 4
TAIL_ROWS = N_META + D_MODEL // LANES


def _reduce_tail(r_in, r_out, r_small, d_meta, d_norm):
    n_p = len(PARAM_SHAPES)
    names = [n for n, _ in PARAM_SHAPES]

    def body(*refs):
        rin, rout, rsmall, dmeta, dnorm = refs[:5]
        g_out = {n: refs[5 + i] for i, n in enumerate(names)}
        loss_out = refs[5 + n_p]
        stail, rtail, gsum, gtail, send_sems, recv_sems = refs[6 + n_p:]
        x, y, c, me = _device_position()
        my_chip = 2 * x + y

        for k in range(N_DEV):
            stail[k, 0:N_META, :] = dmeta[:, SHARD_META * k:SHARD_META * (k + 1)]
            for i in range(D_MODEL // LANES):
                stail[k, N_META + i:N_META + i + 1, :] = dnorm[:, LANES * i:LANES * (i + 1)]
        copies = []
        for r in range(1, N_DEV):
            peer = (1 - x if r & 4 else x, 1 - y if r & 2 else y, 1 - c if r & 1 else c)
            copies.append(pltpu.make_async_remote_copy(
                src_ref=stail.at[4 * peer[0] + 2 * peer[1] + peer[2]],
                dst_ref=rtail.at[r],
                send_sem=send_sems.at[r - 1],
                recv_sem=recv_sems.at[r - 1],
                device_id=peer,
                device_id_type=pl.DeviceIdType.MESH,
            ))
        for cp in copies:
            cp.start()
        rtail[0] = stail[me]

        g = rin[my_chip].astype(F32)
        for ch in range(1, N_CHIPS):
            g = g + rin[ch ^ my_chip].astype(F32)
        g_out["w_in"][...] = g[:SHARD_IN, :]

        g = rout[my_chip].astype(F32)
        gs = rsmall[my_chip]
        for ch in range(1, N_CHIPS):
            g = g + rout[ch ^ my_chip].astype(F32)
            gs = gs + rsmall[ch ^ my_chip]
        g_out["w_out"][...] = g
        gsum[...] = gs
        for i in range(Q_RANK // LANES):
            g_out["w_q_up"][:, LANES * i:LANES * (i + 1)] = gsum[ROW_Q + SHARD_Q * i:ROW_Q + SHARD_Q * (i + 1), :]
        g_out["w_kv_up"][...] = gsum[ROW_KV:ROW_KV + KV_RANK, :]
        g_out["conv_w"][...] = gsum[ROW_CONV:ROW_CONV + 3, 0:SHARD_CONV]
        for name, row, width in (("final_norm_g", ROW_FINAL, D_MODEL), ("q_norm_g", ROW_GQ, Q_RANK),
                                 ("kv_norm_g", ROW_GKV, KV_RANK), ("attn_out_g", ROW_ATTN, CONV_WIDTH),
                                 ("conv_out_g", ROW_CONVG, CONV_WIDTH)):
            for i in range(width // LANES):
                g_out[name][:, LANES * i:LANES * (i + 1)] = gsum[row + i:row + i + 1, :]
        loss_out[...] = gsum[ROW_LOSS:ROW_LOSS + 1, :]

        for cp in copies:
            cp.wait_recv()
        gt = rtail[me]
        for d in range(1, N_DEV):
            gt = gt + rtail[d ^ me]
        gtail[...] = gt
        g_out["meta_tokens"][...] = gtail[0:N_META, :]
        for i in range(D_MODEL // LANES):
            g_out["norm_g"][:, LANES * i:LANES * (i + 1)] = gtail[N_META + i:N_META + i + 1, :]
        for cp in copies:
            cp.wait_send()

    vm = pl.BlockSpec(memory_space=pltpu.VMEM)
    out_shape = [jax.ShapeDtypeStruct(shape, F32) for _, shape in PARAM_SHAPES]
    out_shape.append(jax.ShapeDtypeStruct((1, LANES), F32))
    outs = pl.pallas_call(
        body,
        name="reduce_tail",
        out_shape=tuple(out_shape),
        in_specs=[vm] * 5,
        out_specs=(vm,) * len(out_shape),
        scratch_shapes=[
            pltpu.VMEM((N_DEV, TAIL_ROWS, LANES), F32),
            pltpu.VMEM((N_DEV, TAIL_ROWS, LANES), F32),
            pltpu.VMEM((SMALL_ROWS, LANES), F32),
            pltpu.VMEM((TAIL_ROWS, LANES), F32),
            pltpu.SemaphoreType.DMA((N_DEV - 1,)),
            pltpu.SemaphoreType.DMA((N_DEV - 1,)),
        ],
        compiler_params=pltpu.CompilerParams(vmem_limit_bytes=VMEM_LIMIT),
    )(r_in, r_out, r_small, d_meta, d_norm)
    return {n: outs[i] for i, n in enumerate(names)}, outs[-1]


def _prep_gather(x, meta, norm_g, w_in_t):
    nb_seq, s, d = x.shape
    nb = s // LANES + 1
    relay_step = nb_seq * (nb - 1) // 2
    forward_step = nb_seq * (nb - 1) - 1
    finish_step = nb_seq * (nb - 1)

    def body(x_ref, meta_ref, g_ref, win_ref, u_ref, w_in_p, meta_f,
             sbig, ssmall, gbig, gsmall, send_sems, recv_sems, local_sems):
        jj, b = pl.program_id(0), pl.program_id(1)
        t = jj * nb_seq + b

        def plan():
            return _gather_plan((sbig, ssmall), (gbig, gsmall), send_sems, recv_sems, local_sems)

        @pl.when(t == 0)
        def _():
            sbig[0:SHARD_IN, :] = win_ref[...].astype(BF16)
            sbig[SHARD_IN:, :] = jnp.zeros((SHARD_IN_PAD - SHARD_IN, d), BF16)
            ssmall[...] = meta_ref[...]
            plan()[0]()

        @pl.when(t == relay_step)
        def _():
            plan()[1]()

        @pl.when(t == forward_step)
        def _():
            plan()[2]()

        @pl.when(t == finish_step)
        def _():
            plan()[3]()
            w_in_p[N_A:GRP_A, :] = jnp.zeros((GRP_A - N_A, d), BF16)
            for k in range(N_DEV):
                for s0, e0, d0 in _in_pieces(k):
                    w_in_p[d0:d0 + e0 - s0, :] = gbig[k, s0:e0, :]
                meta_f[:, SHARD_META * k:SHARD_META * (k + 1)] = gsmall[k]

        def norm(h):
            hhat, _ = _rms_stats(h)
            return (hhat * g_ref[...]).astype(BF16)

        @pl.when(jj < nb - 1)
        def _():
            u_ref[...] = norm(x_ref[0])

        @pl.when(jj == nb - 1)
        def _():
            u_ref[0:PAD_FRONT, :] = jnp.zeros((PAD_FRONT, d), BF16)
            u_ref[PAD_FRONT:LANES, :] = norm(meta_f[...])

    whole = lambda shape: pl.BlockSpec(shape, lambda jj, b: (0,) * len(shape))
    return pl.pallas_call(
        body,
        name="prep_norm_gather",
        grid=(nb, nb_seq),
        in_specs=[
            pl.BlockSpec((1, LANES, d), lambda jj, b: (b, jnp.minimum(jj, nb - 2), 0)),
            whole(meta.shape), whole(norm_g.shape), whole(w_in_t.shape),
        ],
        out_specs=(pl.BlockSpec((LANES, d), lambda jj, b: (b * nb + (jj + 1) % nb, 0)),
                   whole((IN_PAD, d)), whole((N_META, d))),
        out_shape=(jax.ShapeDtypeStruct((nb_seq * nb * LANES, d), BF16),
                   jax.ShapeDtypeStruct((IN_PAD, d), BF16),
                   jax.ShapeDtypeStruct((N_META, d), F32)),
        scratch_shapes=[
            pltpu.VMEM((SHARD_IN_PAD, d), BF16),
            pltpu.VMEM((N_META, SHARD_META), F32),
            pltpu.VMEM((N_DEV, SHARD_IN_PAD, d), BF16),
            pltpu.VMEM((N_DEV, N_META, SHARD_META), F32),
            pltpu.SemaphoreType.DMA((14,)),
            pltpu.SemaphoreType.DMA((14,)),
            pltpu.SemaphoreType.DMA((2,)),
        ],
        compiler_params=_params("arbitrary", "arbitrary"),
    )(x, meta, norm_g, w_in_t)


def _in_proj_gather(u, w_in_p, w_q, w_kv, w_out, conv_w, bm, bn):
    m, k_dim = u.shape
    n = w_in_p.shape[0]
    steps = (m // bm) * (n // bn)
    relay_step, forward_step = steps // 3, 2 * steps // 3
    qkv_shape = (SHARD_Q + KV_RANK, Q_RANK)

    def body(a_ref, b_ref, wq_ref, wkv_ref, wout_ref, conv_ref, o_ref, w_q_p, w_kv_p, w_out_f, conv_f,
             sqkv, sout, sconv, gqkv, gout, gconv, send_sems, recv_sems, local_sems):
        t = pl.program_id(0) * (n // bn) + pl.program_id(1)

        def plan():
            return _gather_plan((sqkv, sout, sconv), (gqkv, gout, gconv), send_sems, recv_sems, local_sems)

        @pl.when(t == 0)
        def _():
            sqkv[...] = jnp.zeros_like(sqkv)
            sqkv[0:SHARD_Q, :] = wq_ref[...].astype(BF16)
            sqkv[SHARD_Q:, 0:SHARD_KV] = wkv_ref[...].astype(BF16)
            sout[...] = wout_ref[...].astype(BF16)
            sconv[...] = jnp.zeros_like(sconv)
            sconv[0:3, 0:SHARD_CONV] = conv_ref[...]
            plan()[0]()

        @pl.when(t == relay_step)
        def _():
            plan()[1]()

        @pl.when(t == forward_step)
        def _():
            plan()[2]()

        o_ref[...] = _dot(a_ref[...], b_ref[...], _NT).astype(o_ref.dtype)

        @pl.when(t == steps - 1)
        def _():
            plan()[3]()
            conv_f[...] = jnp.zeros_like(conv_f)
            for k in range(N_DEV):
                for s0, e0, d0 in _q_pieces(k):
                    w_q_p[d0:d0 + e0 - s0, :] = gqkv[k, s0:e0, :]
                w_kv_p[:, _kv_dst(k):_kv_dst(k) + SHARD_KV] = gqkv[k, SHARD_Q:, 0:SHARD_KV]
                w_out_f[SHARD_OUT * k:SHARD_OUT * (k + 1), :] = gout[k]
                conv_f[0:3, SHARD_CONV * k:SHARD_CONV * (k + 1)] = gconv[k, 0:3, 0:SHARD_CONV]

    whole = lambda shape: pl.BlockSpec(shape, lambda i, j: (0,) * len(shape))
    return pl.pallas_call(
        body,
        name="in_proj_gather",
        grid=(m // bm, n // bn),
        in_specs=[pl.BlockSpec((bm, k_dim), lambda i, j: (i, 0)), pl.BlockSpec((bn, k_dim), lambda i, j: (j, 0)),
                  whole(w_q.shape), whole(w_kv.shape), whole(w_out.shape), whole(conv_w.shape)],
        out_specs=(pl.BlockSpec((bm, bn), lambda i, j: (i, j)),
                   whole((Q_COLS, Q_RANK)), whole((KV_RANK, KV_COLS)), whole((D_MODEL, D_MODEL)),
                   whole((8, CONV_WIDTH))),
        out_shape=(jax.ShapeDtypeStruct((m, n), BF16),
                   jax.ShapeDtypeStruct((Q_COLS, Q_RANK), BF16),
                   jax.ShapeDtypeStruct((KV_RANK, KV_COLS), BF16),
                   jax.ShapeDtypeStruct((D_MODEL, D_MODEL), BF16),
                   jax.ShapeDtypeStruct((8, CONV_WIDTH), F32)),
        scratch_shapes=[
            pltpu.VMEM(qkv_shape, BF16),
            pltpu.VMEM((SHARD_OUT, D_MODEL), BF16),
            pltpu.VMEM((8, LANES), F32),
            pltpu.VMEM((N_DEV,) + qkv_shape, BF16),
            pltpu.VMEM((N_DEV, SHARD_OUT, D_MODEL), BF16),
            pltpu.VMEM((N_DEV, 8, LANES), F32),
            pltpu.SemaphoreType.DMA((21,)),
            pltpu.SemaphoreType.DMA((21,)),
            pltpu.SemaphoreType.DMA((3,)),
        ],
        compiler_params=_params("arbitrary", "arbitrary"),
    )(u, w_in_p, w_q, w_kv, w_out, conv_w)


def _rope_tables(tp):
    half = D_ROPE // 2
    inv_freq = 1.0 / (ROPE_THETA ** (jnp.arange(half, dtype=F32) / half))
    pos = (jnp.arange(tp) - PAD_FRONT).astype(F32)
    ang = pos[:, None] * inv_freq[None, :]
    cos = jnp.tile(jnp.cos(ang), (1, LANES // half))
    sin = jnp.tile(jnp.sin(ang), (1, LANES // half))
    first = (jnp.arange(LANES) % D_ROPE) < half
    return cos, jnp.where(first, -sin, 0.0), jnp.where(first, 0.0, sin)


def _rope(t, cos, sa, sb):
    return t * cos + pltpu.roll(t, LANES - D_ROPE // 2, 1) * sa + pltpu.roll(t, D_ROPE // 2, 1) * sb


def _rope_t(t, cos, sa, sb):
    return t * cos + pltpu.roll(t * sa, D_ROPE // 2, 1) + pltpu.roll(t * sb, LANES - D_ROPE // 2, 1)


def _qkv_fwd(p, wq, wkv, gq, gkv, tables, nb_seq, tp):
    ht = tp // 2

    def body(pa_ref, wq_ref, wkv_ref, gq_ref, gkv_ref, cos_ref, sa_ref, sb_ref, q_ref, k_ref, v_ref):
        pa = pa_ref[...].astype(F32)
        cq_hat, _ = _rms_stats(pa[:, :Q_RANK])
        ckv_hat, _ = _rms_stats(pa[:, Q_RANK:Q_RANK + KV_RANK])
        q = _dot((cq_hat * gq_ref[...]).astype(BF16), wq_ref[...], _NT) * Q_SCALE
        kv = _dot((ckv_hat * gkv_ref[...]).astype(BF16), wkv_ref[...])
        tabs = (cos_ref[...], sa_ref[...], sb_ref[...])
        lane = lax.broadcasted_iota(jnp.int32, (ht, LANES), 1)
        low = lane < D_ROPE
        mark = lane == D_ROPE
        row = (pl.program_id(0) % 2) * ht + lax.broadcasted_iota(jnp.int32, (ht, LANES), 0)
        k_pe = jnp.where(mark & (row < PAD_FRONT), NEG_INF, _rope(pa[:, Q_RANK + KV_RANK:], *tabs))
        one = jnp.where(mark & (row >= PAD_FRONT), 1.0, 0.0)
        pairs = [_rope(q[:, N_HEADS * D_NOPE + LANES * i:N_HEADS * D_NOPE + LANES * (i + 1)], *tabs) for i in range(2)]
        for h in range(N_HEADS):
            pair = pairs[h // 2]
            if h % 2:
                pair = pltpu.roll(pair, D_ROPE, 1)
            pe = jnp.where(low, pair, one)
            q_ref[0, h] = jnp.concatenate([q[:, D_NOPE * h:D_NOPE * (h + 1)], pe], axis=1).astype(BF16)
            k_ref[0, h] = jnp.concatenate([kv[:, D_NOPE * h:D_NOPE * (h + 1)], k_pe], axis=1).astype(BF16)
            v_ref[0, h] = kv[:, N_HEADS * D_NOPE + D_V * h:N_HEADS * D_NOPE + D_V * (h + 1)].astype(BF16)

    full = lambda a: pl.BlockSpec(a.shape, lambda i: (0,) * a.ndim)
    tab = pl.BlockSpec((ht, LANES), lambda i: (i % 2, 0))
    qk = pl.BlockSpec((1, N_HEADS, ht, 2 * LANES), lambda i: (i // 2, 0, i % 2, 0))
    return pl.pallas_call(
        body,
        name="qkv_fwd",
        grid=(2 * nb_seq,),
        in_specs=[pl.BlockSpec((ht, GRP_A), lambda i: (i, 0)), full(wq), full(wkv), full(gq), full(gkv), tab, tab, tab],
        out_specs=(qk, qk, pl.BlockSpec((1, N_HEADS, ht, D_V), lambda i: (i // 2, 0, i % 2, 0))),
        out_shape=(
            jax.ShapeDtypeStruct((nb_seq, N_HEADS, tp, 2 * LANES), BF16),
            jax.ShapeDtypeStruct((nb_seq, N_HEADS, tp, 2 * LANES), BF16),
            jax.ShapeDtypeStruct((nb_seq, N_HEADS, tp, D_V), BF16),
        ),
        compiler_params=_params("parallel"),
    )(p, wq, wkv, gq, gkv, *tables)


def _attn_fwd(q, k, v, p, g_attn):
    nb_seq, _, tp, _ = q.shape

    def body(q_ref, k_ref, v_ref, z_ref, g_ref, y_ref, o_ref, lse_ref):
        g = g_ref[...]
        for r0 in range(0, tp, Q_TILE):
            nq = min(Q_TILE, tp - r0)
            kend = r0 + nq
            qq = q_ref[0, 0, r0:kend, :]
            sd = _dot(qq, k_ref[0, 0, r0:kend, :], _NT)
            causal = (lax.broadcasted_iota(jnp.int32, (nq, nq), 1) <= lax.broadcasted_iota(jnp.int32, (nq, nq), 0))
            sd = jnp.where(causal, sd, NEG_INF)
            m = jnp.max(sd, axis=-1, keepdims=True)
            if r0:
                so = _dot(qq, k_ref[0, 0, 0:r0, :], _NT)
                m = jnp.maximum(m, jnp.max(so, axis=-1, keepdims=True))
            ed = jnp.exp2(sd - m)
            l = jnp.sum(ed, axis=-1, keepdims=True)
            o = _dot(ed.astype(BF16), v_ref[0, 0, r0:kend, :])
            if r0:
                eo = jnp.exp2(so - m)
                l = l + jnp.sum(eo, axis=-1, keepdims=True)
                o = o + _dot(eo.astype(BF16), v_ref[0, 0, 0:r0, :])
            o = o * (1.0 / l)
            o_ref[0, 0, r0:kend, :] = o
            lse_ref[0, 0, r0:kend, :] = jnp.broadcast_to(m + jnp.log2(l), (nq, LANES))
            ohat, _ = _rms_stats(o)
            z = z_ref[r0:kend, :].astype(F32)
            y_ref[r0:kend, :] = (ohat * g * (z * _sigmoid(z))).astype(BF16)

    qk = pl.BlockSpec((1, 1, tp, 2 * LANES), lambda b, h: (b, h, 0, 0))
    hv = pl.BlockSpec((1, 1, tp, D_V), lambda b, h: (b, h, 0, 0))
    return pl.pallas_call(
        body,
        name="attn_fwd",
        grid=(nb_seq, N_HEADS),
        in_specs=[qk, qk, hv,
                  pl.BlockSpec((tp, LANES), lambda b, h: (b, GRP_A // LANES + h)),
                  pl.BlockSpec((1, LANES), lambda b, h: (0, h))],
        out_specs=(pl.BlockSpec((tp, LANES), lambda b, h: (b, h)), hv, hv),
        out_shape=(
            jax.ShapeDtypeStruct((nb_seq * tp, N_HEADS * D_V), BF16),
            jax.ShapeDtypeStruct((nb_seq, N_HEADS, tp, D_V), F32),
            jax.ShapeDtypeStruct((nb_seq, N_HEADS, tp, LANES), F32),
        ),
        compiler_params=_params("parallel", "parallel"),
    )(q, k, v, p, g_attn)


_CONV_COL0 = (GRP_A + N_HEADS * D_V) // LANES


def _conv_specs(tp, order):
    cols = CONV_WIDTH // LANES
    return [pl.BlockSpec((tp, LANES), functools.partial(
        lambda a, b, off: order(a, b, off), off=_CONV_COL0 + i * cols)) for i in range(4)]


def _conv_fwd(p, conv_w, g_conv, nb_seq, tp):
    def body(b_ref, c_ref, h_ref, z_ref, w_ref, g_ref, y_ref):
        cc = c_ref[...].astype(F32) * h_ref[...].astype(F32)
        row = lax.broadcasted_iota(jnp.int32, (tp, LANES), 0)
        s1 = jnp.where(row >= 1, pltpu.roll(cc, 1, 0), 0.0)
        s2 = jnp.where(row >= 2, pltpu.roll(cc, 2, 0), 0.0)
        yc = b_ref[...].astype(F32) * (w_ref[0:1, :] * s2 + w_ref[1:2, :] * s1 + w_ref[2:3, :] * cc)
        r = lax.rsqrt(_group_mean(yc * yc) + EPS)
        z = z_ref[...].astype(F32)
        y_ref[...] = (yc * r * g_ref[...] * (z * _sigmoid(z))).astype(BF16)

    return pl.pallas_call(
        body,
        name="conv_fwd",
        grid=(nb_seq, CONV_WIDTH // LANES),
        in_specs=_conv_specs(tp, lambda b, t, off: (b, off + t)) + [
            pl.BlockSpec((8, LANES), lambda b, t: (0, t)),
            pl.BlockSpec((1, LANES), lambda b, t: (0, t))],
        out_specs=pl.BlockSpec((tp, LANES), lambda b, t: (b, t)),
        out_shape=jax.ShapeDtypeStruct((nb_seq * tp, CONV_WIDTH), BF16),
        compiler_params=_params("parallel", "parallel"),
    )(p, p, p, p, conv_w, g_conv)


def _token_copy(hbm, b, k, ts, buf, sem, to_hbm=False):
    lo, hi = max(k * ts - LANES, 0), (k + 1) * ts - LANES
    off = lo - (k * ts - LANES)
    src, dst = hbm.at[b, pl.ds(lo, hi - lo)], buf.at[pl.ds(off, hi - lo)]
    if to_hbm:
        src, dst = dst, src
    return pltpu.make_async_copy(src, dst, sem)


def _for_tile(k, nt, fn):
    for kk in range(nt):
        @pl.when(k == kk)
        def _(kk=kk):
            fn(kk)


def _out_proj_loss(ya, yc, w_out, x, target, g_final, nt):
    nb_seq, s, d = x.shape
    r, ka = ya.shape
    n_mix = w_out.shape[0]
    ts = (s + LANES) // nt
    steps = nb_seq * nt

    def body(a_ref, c_ref, w_ref, x_hbm, t_hbm, g_ref, dhb_ref, dcat_ref, dw_ref, dg_ref, loss_ref,
             xbuf, tbuf, acc_ref, dw_acc, sems):
        i = pl.program_id(0)
        b, k = i // nt, i % nt

        @pl.when(i == 0)
        def _():
            acc_ref[...] = jnp.zeros_like(acc_ref)
            dg_ref[...] = jnp.zeros_like(dg_ref)
            dw_acc[...] = jnp.zeros_like(dw_acc)

        slot = i % 2

        def fetch(seq, kk, sl):
            return [_token_copy(x_hbm, seq, kk, ts, xbuf.at[sl], sems.at[sl, 0]),
                    _token_copy(t_hbm, seq, kk, ts, tbuf.at[sl], sems.at[sl, 1])]

        def start(seq, sl, kk):
            if kk == 0:
                xbuf[sl, 0:LANES, :] = jnp.zeros((LANES, d), F32)
                tbuf[sl, 0:LANES, :] = jnp.zeros((LANES, d), F32)
            for cp in fetch(seq, kk, sl):
                cp.start()

        @pl.when(i == 0)
        def _():
            start(0, 0, 0)

        @pl.when(i + 1 < steps)
        def _():
            _for_tile((i + 1) % nt, nt, functools.partial(start, (i + 1) // nt, 1 - slot))

        mix = _dot(a_ref[...], w_ref[0:ka, :]) + _dot(c_ref[...], w_ref[ka:, :])
        _for_tile(k, nt, lambda kk: [cp.wait() for cp in fetch(b, kk, slot)])

        real = (lax.broadcasted_iota(jnp.int32, (ts, d), 0) >= LANES) | (k > 0)
        g = g_ref[...]
        hhat, rstd = _rms_stats(xbuf[slot] + mix)
        e = jnp.where(real, hhat * g - tbuf[slot], 0.0)
        acc_ref[...] += jnp.sum(e * e, axis=0, keepdims=True)
        dy = e * (1.0 / d)
        dg_ref[...] += jnp.sum(dy * hhat, axis=0, keepdims=True)
        dhb = _rms_bwd(g * dy, hhat, rstd).astype(BF16)
        dhb_ref[...] = dhb
        dcat_ref[...] = _dot(dhb, w_ref[...], _NT).astype(BF16)
        dw_acc[0:ka, :] += _dot(a_ref[...], dhb, _TN)
        dw_acc[ka:, :] += _dot(c_ref[...], dhb, _TN)

        @pl.when(i == steps - 1)
        def _():
            total = jnp.sum(acc_ref[...], axis=1, keepdims=True)
            loss_ref[...] = jnp.broadcast_to((0.5 / d) * total, loss_ref.shape)
            dw_ref[...] = dw_acc[...].astype(BF16)

    hbm = pl.BlockSpec(memory_space=pl.ANY)
    row = pl.BlockSpec((ts, d), lambda i: (i, 0))
    vec = pl.BlockSpec((1, d), lambda i: (0, 0))
    return pl.pallas_call(
        body,
        name="out_proj_loss",
        grid=(steps,),
        in_specs=[pl.BlockSpec((ts, ka), lambda i: (i, 0)), pl.BlockSpec((ts, yc.shape[1]), lambda i: (i, 0)),
                  pl.BlockSpec(w_out.shape, lambda i: (0, 0)), hbm, hbm, vec],
        out_specs=(row, pl.BlockSpec((ts, n_mix), lambda i: (i, 0)), pl.BlockSpec((n_mix, d), lambda i: (0, 0)),
                   vec, pl.BlockSpec((1, LANES), lambda i: (0, 0))),
        out_shape=(
            jax.ShapeDtypeStruct((r, d), BF16),
            jax.ShapeDtypeStruct((r, n_mix), BF16),
            jax.ShapeDtypeStruct((n_mix, d), BF16),
            jax.ShapeDtypeStruct((1, d), F32),
            jax.ShapeDtypeStruct((1, LANES), F32),
        ),
        scratch_shapes=[pltpu.VMEM((2, ts, d), F32), pltpu.VMEM((2, ts, d), F32), pltpu.VMEM((1, d), F32),
                        pltpu.VMEM((n_mix, d), F32), pltpu.SemaphoreType.DMA((2, 2))],
        compiler_params=_params("arbitrary"),
    )(ya, yc, w_out, x, target, g_final)


def _attn_bwd(q, k, v, o, lse, dcat, p, g_attn):
    nb_seq, _, tp, _ = q.shape

    def body(q_ref, k_ref, v_ref, o_ref, lse_ref, dy_ref, z_ref, g_ref,
             dq_ref, dk_ref, dv_ref, dz_ref, dg_ref, dq_acc):
        @pl.when(pl.program_id(1) == 0)
        def _():
            dg_ref[...] = jnp.zeros_like(dg_ref)

        g = g_ref[...]
        z = z_ref[...].astype(F32)
        o = o_ref[0, 0]
        dy = dy_ref[...].astype(F32)
        sig = _sigmoid(z)
        ohat, r = _rms_stats(o)
        don = dy * (z * sig)
        dz_ref[...] = (dy * (ohat * g) * (sig * (1.0 + z * (1.0 - sig)))).astype(BF16)
        dg_ref[...] += jnp.sum(don * ohat, axis=0, keepdims=True)
        do = _rms_bwd(g * don, ohat, r)
        dvec = jnp.sum(do * o, axis=-1, keepdims=True)
        dob = do.astype(BF16)
        lse_col = lse_ref[0, 0, :, 0:1]
        dq_acc[...] = jnp.zeros_like(dq_acc)
        for k0 in range(0, tp, K_TILE):
            nk = min(K_TILE, tp - k0)
            nq = tp - k0
            qq = q_ref[0, 0, k0:, :]
            kk = k_ref[0, 0, k0:k0 + nk, :]
            causal = (lax.broadcasted_iota(jnp.int32, (nq, nk), 1) <= lax.broadcasted_iota(jnp.int32, (nq, nk), 0))
            pr = jnp.where(causal, jnp.exp2(_dot(qq, kk, _NT) - lse_col[k0:]), 0.0)
            dp = _dot(dob[k0:], v_ref[0, 0, k0:k0 + nk, :], _NT)
            ds = (pr * (dp - dvec[k0:])).astype(BF16)
            dv_ref[0, 0, k0:k0 + nk, :] = _dot(pr.astype(BF16), dob[k0:], _TN).astype(BF16)
            dk_ref[0, 0, k0:k0 + nk, :] = (_dot(ds, qq, _TN) * (ATTN_SCALE / Q_SCALE)).astype(BF16)
            dq_acc[k0:, :] += _dot(ds, kk)
        dq_ref[0, 0] = (dq_acc[...] * ATTN_SCALE).astype(BF16)

    qk = pl.BlockSpec((1, 1, tp, 2 * LANES), lambda h, b: (b, h, 0, 0))
    hv = pl.BlockSpec((1, 1, tp, D_V), lambda h, b: (b, h, 0, 0))
    col = pl.BlockSpec((tp, LANES), lambda h, b: (b, h))
    return pl.pallas_call(
        body,
        name="attn_bwd",
        grid=(N_HEADS, nb_seq),
        in_specs=[qk, qk, hv, hv, hv, col,
                  pl.BlockSpec((tp, LANES), lambda h, b: (b, GRP_A // LANES + h)),
                  pl.BlockSpec((1, LANES), lambda h, b: (0, h))],
        out_specs=(qk, qk, hv, col, pl.BlockSpec((1, LANES), lambda h, b: (0, h))),
        out_shape=(
            jax.ShapeDtypeStruct((nb_seq, N_HEADS, tp, 2 * LANES), BF16),
            jax.ShapeDtypeStruct((nb_seq, N_HEADS, tp, 2 * LANES), BF16),
            jax.ShapeDtypeStruct((nb_seq, N_HEADS, tp, D_V), BF16),
            jax.ShapeDtypeStruct((nb_seq * tp, N_HEADS * D_V), BF16),
            jax.ShapeDtypeStruct((1, N_HEADS * D_V), F32),
        ),
        scratch_shapes=[pltpu.VMEM((tp, 2 * LANES), F32)],
        compiler_params=_params("arbitrary", "arbitrary"),
    )(q, k, v, o, lse, dcat, p, g_attn)


def _qkv_bwd(p, dq, dk, dv, wq, wkv, gq, gkv, tables):
    nb_seq, _, tp, _ = dq.shape
    ht = tp // 2

    def body(pa_ref, dq_ref, dk_ref, dv_ref, wq_ref, wkv_ref, gq_ref, gkv_ref, cos_ref, sa_ref, sb_ref,
             dpa_ref, dwq_ref, dwkv_ref, dgq_ref, dgkv_ref):
        @pl.when(pl.program_id(0) == 0)
        def _():
            dwq_ref[...] = jnp.zeros_like(dwq_ref)
            dwkv_ref[...] = jnp.zeros_like(dwkv_ref)
            dgq_ref[...] = jnp.zeros_like(dgq_ref)
            dgkv_ref[...] = jnp.zeros_like(dgkv_ref)

        pa = pa_ref[...].astype(F32)
        gq, gkv = gq_ref[...], gkv_ref[...]
        cq_hat, rq = _rms_stats(pa[:, :Q_RANK])
        ckv_hat, rkv = _rms_stats(pa[:, Q_RANK:Q_RANK + KV_RANK])
        tabs = (cos_ref[...], sa_ref[...], sb_ref[...])

        pe = [dq_ref[0, h, :, D_NOPE:].astype(F32) for h in range(N_HEADS)]
        pairs = [_rope_t(pe[2 * i] + pltpu.roll(pe[2 * i + 1], D_ROPE, 1), *tabs).astype(BF16) for i in range(2)]
        dq_flat = jnp.concatenate([dq_ref[0, h, :, :D_NOPE] for h in range(N_HEADS)] + pairs, axis=1)
        dwq_ref[...] += _dot(dq_flat, (cq_hat * gq).astype(BF16), _TN)
        dcqn = _dot(dq_flat, wq_ref[...])
        dgq_ref[...] += jnp.sum(dcqn * cq_hat, axis=0, keepdims=True)
        dcq = _rms_bwd(gq * dcqn, cq_hat, rq)

        dkv_flat = jnp.concatenate([dk_ref[0, h, :, :D_NOPE] for h in range(N_HEADS)]
                                   + [dv_ref[0, h] for h in range(N_HEADS)], axis=1)
        dwkv_ref[...] += _dot((ckv_hat * gkv).astype(BF16), dkv_flat, _TN)
        dckvn = _dot(dkv_flat, wkv_ref[...], _NT)
        dgkv_ref[...] += jnp.sum(dckvn * ckv_hat, axis=0, keepdims=True)
        dckv = _rms_bwd(gkv * dckvn, ckv_hat, rkv)

        dk_pe = dk_ref[0, 0, :, D_NOPE:].astype(F32)
        for h in range(1, N_HEADS):
            dk_pe = dk_pe + dk_ref[0, h, :, D_NOPE:].astype(F32)
        dk_pe = jnp.where(lax.broadcasted_iota(jnp.int32, (ht, LANES), 1) < D_ROPE, dk_pe, 0.0)
        dpa_ref[...] = jnp.concatenate([dcq, dckv, _rope_t(dk_pe, *tabs)], axis=1).astype(BF16)

    full = lambda a: pl.BlockSpec(a.shape, lambda i: (0,) * a.ndim)
    tab = pl.BlockSpec((ht, LANES), lambda i: (i % 2, 0))
    qk = pl.BlockSpec((1, N_HEADS, ht, 2 * LANES), lambda i: (i // 2, 0, i % 2, 0))
    acc = lambda shape: pl.BlockSpec(shape, lambda i: (0, 0))
    return pl.pallas_call(
        body,
        name="qkv_bwd",
        grid=(2 * nb_seq,),
        in_specs=[pl.BlockSpec((ht, GRP_A), lambda i: (i, 0)), qk, qk,
                  pl.BlockSpec((1, N_HEADS, ht, D_V), lambda i: (i // 2, 0, i % 2, 0)),
                  full(wq), full(wkv), full(gq), full(gkv), tab, tab, tab],
        out_specs=(pl.BlockSpec((ht, GRP_A), lambda i: (i, 0)),
                   acc(wq.shape), acc(wkv.shape), acc((1, Q_RANK)), acc((1, KV_RANK))),
        out_shape=(
            jax.ShapeDtypeStruct((nb_seq * tp, GRP_A), BF16),
            jax.ShapeDtypeStruct(wq.shape, F32),
            jax.ShapeDtypeStruct(wkv.shape, F32),
            jax.ShapeDtypeStruct((1, Q_RANK), F32),
            jax.ShapeDtypeStruct((1, KV_RANK), F32),
        ),
        compiler_params=_params("arbitrary"),
    )(p, dq, dk, dv, wq, wkv, gq, gkv, *tables)


def _conv_bwd(p, dcat, conv_w, g_conv, nb_seq, tp):
    cols = CONV_WIDTH // LANES

    def body(b_ref, c_ref, h_ref, z_ref, dy_ref, w_ref, g_ref,
             db_ref, dc_ref, dh_ref, dz_ref, dw_ref, dg_ref):
        @pl.when(pl.program_id(1) == 0)
        def _():
            dw_ref[...] = jnp.zeros_like(dw_ref)
            dg_ref[...] = jnp.zeros_like(dg_ref)

        cb, c, h = b_ref[...].astype(F32), c_ref[...].astype(F32), h_ref[...].astype(F32)
        z, dy = z_ref[...].astype(F32), dy_ref[...].astype(F32)
        g = g_ref[...]
        w0, w1, w2 = w_ref[0:1, :], w_ref[1:2, :], w_ref[2:3, :]
        cc = c * h
        row = lax.broadcasted_iota(jnp.int32, (tp, LANES), 0)
        s1 = jnp.where(row >= 1, pltpu.roll(cc, 1, 0), 0.0)
        s2 = jnp.where(row >= 2, pltpu.roll(cc, 2, 0), 0.0)
        dwc = w0 * s2 + w1 * s1 + w2 * cc
        yc = cb * dwc
        r = lax.rsqrt(_group_mean(yc * yc) + EPS)
        ychat = yc * r
        sig = _sigmoid(z)
        dz_ref[...] = (dy * (ychat * g) * (sig * (1.0 + z * (1.0 - sig)))).astype(BF16)
        dyn = dy * (z * sig)
        dg_ref[...] += jnp.sum(dyn * ychat, axis=0, keepdims=True)
        gd = g * dyn
        dyc = r * (gd - ychat * _group_mean(gd * ychat))
        db_ref[...] = (dyc * dwc).astype(BF16)
        ddw = dyc * cb
        dw_ref[0:1, :] += jnp.sum(ddw * s2, axis=0, keepdims=True)
        dw_ref[1:2, :] += jnp.sum(ddw * s1, axis=0, keepdims=True)
        dw_ref[2:3, :] += jnp.sum(ddw * cc, axis=0, keepdims=True)
        u1 = jnp.where(row <= tp - 2, pltpu.roll(ddw, tp - 1, 0), 0.0)
        u2 = jnp.where(row <= tp - 3, pltpu.roll(ddw, tp - 2, 0), 0.0)
        dcc = w2 * ddw + w1 * u1 + w0 * u2
        dc_ref[...] = (dcc * h).astype(BF16)
        dh_ref[...] = (dcc * c).astype(BF16)

    col = pl.BlockSpec((tp, LANES), lambda t, b: (b, t))
    out = jax.ShapeDtypeStruct((nb_seq * tp, CONV_WIDTH), BF16)
    return pl.pallas_call(
        body,
        name="conv_bwd",
        grid=(cols, nb_seq),
        in_specs=_conv_specs(tp, lambda t, b, off: (b, off + t)) + [
            pl.BlockSpec((tp, LANES), lambda t, b: (b, N_HEADS * D_V // LANES + t)),
            pl.BlockSpec((8, LANES), lambda t, b: (0, t)),
            pl.BlockSpec((1, LANES), lambda t, b: (0, t))],
        out_specs=(col, col, col, col,
                   pl.BlockSpec((8, LANES), lambda t, b: (0, t)), pl.BlockSpec((1, LANES), lambda t, b: (0, t))),
        out_shape=(out, out, out, out,
                   jax.ShapeDtypeStruct((8, CONV_WIDTH), F32), jax.ShapeDtypeStruct((1, CONV_WIDTH), F32)),
        compiler_params=_params("arbitrary", "arbitrary"),
    )(p, p, p, p, dcat, conv_w, g_conv)


def _input_bwd(dps, w_in, x, meta, dh, norm_g, nt, send_in):
    nb_seq, s, d = x.shape
    r, kb = dps[0].shape
    ts = (s + LANES) // nt
    steps = nb_seq * nt
    n_dp = len(dps)
    in_slot = send_in.shape[1:]

    def body(*refs):
        dp_refs, w_ref, x_hbm, meta_ref, dh_ref, g_ref, pay_ref = refs[:n_dp], *refs[n_dp:n_dp + 6]
        o = n_dp + 6
        gx_hbm, dmeta_ref, dg_ref, r2_in = refs[o:o + 4]
        xbuf, gxbuf, tok_sems, own_in, r1_in, sum_in = refs[o + 4:o + 10]
        sems = refs[o + 10:]
        i = pl.program_id(0)
        b, k = i // nt, i % nt

        def plan():
            return _reduce_plan((pay_ref,), (own_in,), (r1_in,), (sum_in,), (r2_in,), *sems)

        @pl.when(i == 0)
        def _():
            dmeta_ref[...] = jnp.zeros_like(dmeta_ref)
            dg_ref[...] = jnp.zeros_like(dg_ref)
            plan()[0]()

        @pl.when(i == 1)
        def _():
            plan()[1]()

        def start(kk):
            if kk == 0:
                xbuf[0:PAD_FRONT, :] = jnp.zeros((PAD_FRONT, d), F32)
                xbuf[PAD_FRONT:LANES, :] = meta_ref[...]
            _token_copy(x_hbm, b, kk, ts, xbuf, tok_sems.at[0]).start()

        _for_tile(k, nt, start)
        du = _dot(dp_refs[0][...], w_ref[0:kb, :])
        for j in range(1, n_dp):
            du = du + _dot(dp_refs[j][...], w_ref[kb * j:kb * (j + 1), :])
        _for_tile(k, nt, lambda kk: _token_copy(x_hbm, b, kk, ts, xbuf, tok_sems.at[0]).wait())

        g = g_ref[...]
        hhat, rstd = _rms_stats(xbuf[...])
        dg_ref[...] += jnp.sum(du * hhat, axis=0, keepdims=True)
        res = _rms_bwd(g * du, hhat, rstd) + dh_ref[...].astype(F32)

        @pl.when(i > 0)
        def _():
            _for_tile(k, nt, lambda kk: _token_copy(gx_hbm, b, (kk - 1) % nt, ts, gxbuf, tok_sems.at[1], True).wait())

        gxbuf[...] = res

        @pl.when(k == 0)
        def _():
            dmeta_ref[...] += gxbuf[PAD_FRONT:LANES, :]

        _for_tile(k, nt, lambda kk: _token_copy(gx_hbm, b, kk, ts, gxbuf, tok_sems.at[1], True).start())

        @pl.when(i == steps - 1)
        def _():
            _token_copy(gx_hbm, b, nt - 1, ts, gxbuf, tok_sems.at[1], True).wait()
            plan()[2]()

    whole = lambda a: pl.BlockSpec(a.shape, lambda i: (0,) * a.ndim)
    hbm = pl.BlockSpec(memory_space=pl.ANY)
    return pl.pallas_call(
        body,
        name="input_bwd",
        grid=(steps,),
        in_specs=[pl.BlockSpec((ts, kb), lambda i: (i, 0)) for _ in dps]
        + [whole(w_in), hbm, whole(meta), pl.BlockSpec((ts, d), lambda i: (i, 0)), whole(norm_g), hbm],
        out_specs=(hbm, pl.BlockSpec((N_META, d), lambda i: (0, 0)), pl.BlockSpec((1, d), lambda i: (0, 0)), hbm),
        out_shape=(jax.ShapeDtypeStruct((nb_seq, s, d), F32),
                   jax.ShapeDtypeStruct((N_META, d), F32),
                   jax.ShapeDtypeStruct((1, d), F32),
                   jax.ShapeDtypeStruct((N_CHIPS,) + in_slot, BF16)),
        scratch_shapes=[pltpu.VMEM((ts, d), F32), pltpu.VMEM((ts, d), F32), pltpu.SemaphoreType.DMA((2,))]
        + _reduce_scratch([(in_slot, BF16)], [True]),
        compiler_params=_params("arbitrary"),
    )(*dps, w_in, x, meta, dh, norm_g, send_in)


def _in_proj_bwd_w(u, dps, bm, small_grads, send_out):
    r, d = u.shape
    kb = dps[0].shape[1]
    steps = r // bm
    n_dp, n_small = len(dps), len(small_grads)
    out_slot, small_slot = send_out.shape[1:], (SMALL_ROWS, LANES)

    def body(*refs):
        u_ref, dp_refs = refs[0], refs[1:1 + n_dp]
        small_refs = refs[1 + n_dp:1 + n_dp + n_small]
        o = 1 + n_dp + n_small
        pay_out, o_ref, r2_out, r2_small = refs[o:o + 4]
        acc_ref, ssmall, r1_out, sum_out, r1_small, sum_small = refs[o + 4:o + 10]
        sems = refs[o + 10:]
        i = pl.program_id(0)

        def plan():
            return _reduce_plan((pay_out, ssmall), (None, None), (r1_out, r1_small), (sum_out, sum_small),
                                (r2_out, r2_small), *sems)

        @pl.when(i == 0)
        def _():
            acc_ref[...] = jnp.zeros_like(acc_ref)
            _pack_small(ssmall, *small_refs)
            plan()[0]()

        @pl.when(i == 1)
        def _():
            plan()[1]()

        uu = u_ref[...]
        for j in range(n_dp):
            acc_ref[kb * j:kb * (j + 1), :] += _dot(dp_refs[j][...], uu, _TN)

        @pl.when(i == steps - 1)
        def _():
            for k in range(N_DEV):
                for s, e, c0 in _in_pieces(k):
                    o_ref[k, s:e, :] = acc_ref[c0:c0 + e - s, :].astype(BF16)
                o_ref[k, SHARD_IN:, :] = jnp.zeros((SHARD_IN_PAD - SHARD_IN, d), BF16)
            plan()[2]()

    whole = lambda a: pl.BlockSpec(a.shape, lambda i: (0,) * a.ndim)
    hbm = pl.BlockSpec(memory_space=pl.ANY)
    return pl.pallas_call(
        body,
        name="in_proj_bwd_w",
        grid=(steps,),
        in_specs=[pl.BlockSpec((bm, d), lambda i: (i, 0))]
        + [pl.BlockSpec((bm, kb), lambda i: (i, 0)) for _ in dps] + [whole(a) for a in small_grads]
        + [whole(send_out)],
        out_specs=(pl.BlockSpec((N_DEV, SHARD_IN_PAD, d), lambda i: (0, 0, 0)), hbm, hbm),
        out_shape=(jax.ShapeDtypeStruct((N_DEV, SHARD_IN_PAD, d), BF16),
                   jax.ShapeDtypeStruct((N_CHIPS,) + out_slot, BF16),
                   jax.ShapeDtypeStruct((N_CHIPS,) + small_slot, F32)),
        scratch_shapes=[pltpu.VMEM((kb * n_dp, d), F32), pltpu.VMEM((N_DEV,) + small_slot, F32)]
        + _reduce_scratch([(out_slot, BF16), (small_slot, F32)], [False, False]),
        compiler_params=_params("arbitrary"),
    )(u, *dps, *small_grads, send_out)


def _local_step(x, loss_target, u, p, meta_f, norm_g, w_in_p, q_norm_g, w_q_p, kv_norm_g, w_kv_p, conv_w_f,
                attn_out_g, conv_out_g, w_out_f, g_final):
    nb_seq, s, d = x.shape
    tp = s + LANES
    ht = tp // 2
    tables = _rope_tables(tp)

    q, k, v = _qkv_fwd(p, w_q_p, w_kv_p, q_norm_g, kv_norm_g, tables, nb_seq, tp)
    ya, o, lse = _attn_fwd(q, k, v, p, attn_out_g)
    yc = _conv_fwd(p, conv_w_f, conv_out_g, nb_seq, tp)
    dhb, dcat, d_w_out, d_final_g, loss_part = _out_proj_loss(
        ya, yc, w_out_f, x, loss_target, g_final, TOKEN_TILES)
    send_out = d_w_out.reshape(N_DEV, SHARD_OUT, d)
    dq, dk, dv, dz_attn, d_attn_g = _attn_bwd(q, k, v, o, lse, dcat, p, attn_out_g)
    dpa, d_wq_p, d_wkv_p, d_gq, d_gkv = _qkv_bwd(p, dq, dk, dv, w_q_p, w_kv_p, q_norm_g, kv_norm_g, tables)
    d_b, d_c, d_h, dz_conv, d_conv_w, d_conv_g = _conv_bwd(p, dcat, conv_w_f, conv_out_g, nb_seq, tp)
    dps = (dpa, dz_attn, d_b, d_c, d_h, dz_conv)
    small = (d_wq_p, d_wkv_p, d_conv_w, d_final_g, d_gq, d_gkv, d_attn_g, d_conv_g, loss_part)
    send_in, r_out, r_small = _in_proj_bwd_w(u, dps, ht, small, send_out)
    grad_x, d_meta, d_norm_g, r_in = _input_bwd(dps, w_in_p, x, meta_f, dhb, norm_g, TOKEN_TILES, send_in)
    return grad_x, r_in, r_out, r_small, d_meta, d_norm_g


def kernel(x, meta_tokens, norm_g, w_in, q_norm_g, w_q_up, kv_norm_g, w_kv_up, conv_w, attn_out_g, conv_out_g, w_out, final_norm_g, loss_target, m_meta_tokens, m_norm_g, m_w_in, m_q_norm_g, m_w_q_up, m_kv_norm_g, m_w_kv_up, m_conv_w, m_attn_out_g, m_conv_out_g, m_w_out, m_final_norm_g, v_meta_tokens, v_norm_g, v_w_in, v_q_norm_g, v_w_q_up, v_kv_norm_g, v_w_kv_up, v_conv_w, v_attn_out_g, v_conv_out_g, v_w_out, v_final_norm_g):
    d = x.shape[-1]
    ht = (x.shape[1] + LANES) // 2
    u, w_in_p, meta_f = _prep_gather(x, meta_tokens, norm_g, w_in[0].T)
    p, w_q_p, w_kv_p, w_out_f, conv_w_f = _in_proj_gather(
        u, w_in_p, w_q_up[0].T, w_kv_up[0], w_out[0], conv_w[0], ht, 3 * GRP_A)
    g_final = final_norm_g.reshape(1, d)
    grad_x, r_in, r_out, r_small, d_meta, d_norm_g = _local_step(
        x, loss_target, u, p, meta_f, norm_g, w_in_p, q_norm_g, w_q_p, kv_norm_g, w_kv_p, conv_w_f,
        attn_out_g, conv_out_g, w_out_f, g_final)

    flat = lambda a: a.reshape(a.shape[-2:]) if a.ndim == 3 else a.reshape(1, -1) if a.ndim == 1 else a
    transposed = ("w_in", "w_q_up")
    to_kernel = lambda n, a: flat(a).T if n in transposed else flat(a)
    from_kernel = lambda n, a, shape: (a.T if n in transposed else a).reshape(shape)
    params = {
        "meta_tokens": (meta_tokens, m_meta_tokens, v_meta_tokens),
        "norm_g": (norm_g, m_norm_g, v_norm_g),
        "w_in": (w_in, m_w_in, v_w_in),
        "q_norm_g": (q_norm_g, m_q_norm_g, v_q_norm_g),
        "w_q_up": (w_q_up, m_w_q_up, v_w_q_up),
        "kv_norm_g": (kv_norm_g, m_kv_norm_g, v_kv_norm_g),
        "w_kv_up": (w_kv_up, m_w_kv_up, v_w_kv_up),
        "conv_w": (conv_w, m_conv_w, v_conv_w),
        "attn_out_g": (attn_out_g, m_attn_out_g, v_attn_out_g),
        "conv_out_g": (conv_out_g, m_conv_out_g, v_conv_out_g),
        "w_out": (w_out, m_w_out, v_w_out),
        "final_norm_g": (final_norm_g, m_final_norm_g, v_final_norm_g),
    }
    grads, loss = _reduce_tail(r_in, r_out, r_small, d_meta, d_norm_g)
    updated = _adamw(grads, {n: tuple(to_kernel(n, a) for a in t) for n, t in params.items()})
    outs = [[from_kernel(n, updated[n][i], params[n][0].shape) for n, _ in PARAM_SHAPES] for i in range(4)]
    return (loss[0, 0], grad_x, *outs[0], *outs[1], *outs[2], *outs[3])
```

```python
import functools

import jax
import jax.numpy as jnp
from jax import lax
from jax.experimental import pallas as pl
from jax.experimental.pallas import tpu as pltpu

F32 = jnp.float32
BF16 = jnp.bfloat16

N_META = 16
D_MODEL = 1024
N_HEADS = 4
D_NOPE = 128
D_ROPE = 64
D_V = 128
Q_RANK = 256
KV_RANK = 128
CONV_WIDTH = 512
CONV_GROUP = 64
ROPE_THETA = 10000.0
ATTN_SCALE = (D_NOPE + D_ROPE) ** -0.5
Q_SCALE = ATTN_SCALE * 1.4426950408889634
EPS = 1e-6
NEG_INF = -1e30

ADAM_LR = 0.001
ADAM_B1 = 0.9
ADAM_B2 = 0.999
ADAM_EPS = 1e-08
ADAM_WD = 0.01
ADAM_STEP = 10

LANES = 128
PAD_FRONT = LANES - N_META
K_TILE = 256
Q_TILE = 512
N_DEV = 8
VMEM_LIMIT = 56 * 1024 * 1024

IN_PAD = 3072
GRP_A = 512
N_A = Q_RANK + KV_RANK + D_ROPE
IN_PROJ = 3008
SHARD_IN = IN_PROJ // N_DEV
SHARD_IN_PAD = 384
SHARD_Q = 96
SHARD_KV = 128
SHARD_OUT = 128
SHARD_CONV = 64
SHARD_META = 128
Q_COLS = N_HEADS * (D_NOPE + D_ROPE)
KV_COLS = N_HEADS * (D_NOPE + D_V)

ROW_Q, ROW_KV, ROW_META, ROW_CONV = 0, 256, 384, 400
ROW_REPL = 408
ROW_NORM, ROW_FINAL, ROW_GQ, ROW_GKV, ROW_ATTN, ROW_CONVG, ROW_LOSS = 408, 416, 424, 426, 427, 431, 435
SMALL_ROWS = 440

PARAM_SHAPES = (
    ("meta_tokens", (N_META, SHARD_META)), ("norm_g", (1, D_MODEL)), ("w_in", (SHARD_IN, D_MODEL)),
    ("q_norm_g", (1, Q_RANK)), ("w_q_up", (SHARD_Q, Q_RANK)), ("kv_norm_g", (1, KV_RANK)),
    ("w_kv_up", (KV_RANK, SHARD_KV)), ("conv_w", (3, SHARD_CONV)), ("attn_out_g", (1, CONV_WIDTH)),
    ("conv_out_g", (1, CONV_WIDTH)), ("w_out", (SHARD_OUT, D_MODEL)), ("final_norm_g", (1, D_MODEL)),
)


def _in_pieces(k):
    lo, hi = SHARD_IN * k, SHARD_IN * (k + 1)
    out = []
    if lo < N_A:
        out.append((0, min(hi, N_A) - lo, lo))
    if hi > N_A:
        s = max(lo, N_A)
        out.append((s - lo, hi - lo, s + GRP_A - N_A))
    return out


def _q_pieces(k):
    lo, hi = SHARD_Q * k, SHARD_Q * (k + 1)
    out = []
    for h in range(N_HEADS):
        base = (D_NOPE + D_ROPE) * h
        s, e = max(lo, base), min(hi, base + D_NOPE)
        if s < e:
            out.append((s - lo, e - lo, D_NOPE * h + s - base))
        s, e = max(lo, base + D_NOPE), min(hi, base + D_NOPE + D_ROPE)
        if s < e:
            out.append((s - lo, e - lo, N_HEADS * D_NOPE + D_ROPE * h + s - base - D_NOPE))
    return out


def _kv_dst(k):
    return D_NOPE * (k // 2) + (N_HEADS * D_NOPE if k % 2 else 0)


def _params(*sem):
    return pltpu.CompilerParams(dimension_semantics=sem, vmem_limit_bytes=VMEM_LIMIT)


def _rms_stats(x):
    r = lax.rsqrt(jnp.mean(x * x, axis=-1, keepdims=True) + EPS)
    return x * r, r


def _rms_bwd(gdy, xhat, r):
    return r * (gdy - xhat * jnp.mean(gdy * xhat, axis=-1, keepdims=True))


def _sigmoid(z):
    return 1.0 / (1.0 + jnp.exp(-z))


def _group_mean(x):
    i0 = lax.broadcasted_iota(jnp.int32, (LANES, LANES), 0) // CONV_GROUP
    i1 = lax.broadcasted_iota(jnp.int32, (LANES, LANES), 1) // CONV_GROUP
    m = jnp.where(i0 == i1, 1.0 / CONV_GROUP, 0.0).astype(BF16)
    hi = x.astype(BF16)
    lo = (x - hi.astype(F32)).astype(BF16)
    return jnp.dot(hi, m, preferred_element_type=F32) + jnp.dot(lo, m, preferred_element_type=F32)


_NT = (((1,), (1,)), ((), ()))
_TN = (((0,), (0,)), ((), ()))


def _dot(a, b, dims=None):
    if dims is None:
        return jnp.dot(a, b, preferred_element_type=F32)
    return lax.dot_general(a, b, dims, preferred_element_type=F32)


def _device_position():
    x, y, c = lax.axis_index("x"), lax.axis_index("y"), lax.axis_index("c")
    return x, y, c, 4 * x + 2 * y + c


def _gather_plan(srcs, slots, send_sems, recv_sems, local_sems):
    x, y, c, _ = _device_position()
    me, sibling = (x, y, c), (x, y, 1 - c)
    flip = lambda v, on: v + on - 2 * v * on
    near = (flip(x, 1 - c), flip(y, c))
    far = (flip(x, c), flip(y, 1 - c))
    diag = (1 - x, 1 - y)
    n = len(srcs)

    def slot(a, px, py, pc):
        return slots[a].at[4 * px + 2 * py + pc]

    def copy(a, k, block, to, own=False):
        return pltpu.make_async_remote_copy(
            src_ref=srcs[a] if own else slot(a, *block),
            dst_ref=slot(a, *block),
            send_sem=send_sems.at[7 * a + k],
            recv_sem=recv_sems.at[7 * a + k],
            device_id=to,
            device_id_type=pl.DeviceIdType.MESH,
        )

    def local(a):
        return pltpu.make_async_copy(srcs[a], slot(a, *me), local_sems.at[a])

    sent = [(me, sibling), (me, (*near, c)), (me, (*far, c)), ((*near, c), (*far, c)),
            ((*near, c), sibling), ((*far, c), sibling), ((*diag, c), sibling)]
    landed = [sibling, (*near, c), (*far, c), (*diag, c), (*far, 1 - c), (*near, 1 - c), (*diag, 1 - c)]

    def send(a, k):
        return copy(a, k, *sent[k], own=k < 3)

    def arrival(a, k):
        return copy(a, k, landed[k], me)

    def start():
        for a in range(n):
            local(a).start()
            for k in range(3):
                send(a, k).start()

    def relay():
        for a in range(n):
            arrival(a, 1).wait_recv()
            send(a, 3).start()
            send(a, 4).start()

    def forward():
        for k in (2, 3):
            for a in range(n):
                arrival(a, k).wait_recv()
                send(a, k + 3).start()

    def finish():
        for a in range(n):
            for k in (0, 4, 5, 6):
                arrival(a, k).wait_recv()
        for a in range(n):
            for k in range(7):
                send(a, k).wait_send()
            local(a).wait()

    return start, relay, forward, finish


def _adam_update(g, w, m, v):
    m_new = ADAM_B1 * m + (1.0 - ADAM_B1) * g
    v_new = ADAM_B2 * v + (1.0 - ADAM_B2) * (g * g)
    m_hat = m_new / (1.0 - ADAM_B1 ** ADAM_STEP)
    v_hat = v_new / (1.0 - ADAM_B2 ** ADAM_STEP)
    return -ADAM_LR * (m_hat / (jnp.sqrt(v_hat) + ADAM_EPS) + ADAM_WD * w), m_new, v_new


def _adamw(grads, params):
    names = [n for n, _ in PARAM_SHAPES]
    n_p = len(names)

    def body(*refs):
        for i in range(n_p):
            g = refs[i][...]
            w, m, v = (refs[n_p + 3 * i + j][...] for j in range(3))
            delta, m_new, v_new = _adam_update(g, w, m, v)
            for j, val in enumerate((g, delta, m_new, v_new)):
                refs[4 * n_p + 4 * i + j][...] = val

    vm = pl.BlockSpec(memory_space=pltpu.VMEM)
    out_shape = []
    for _, shape in PARAM_SHAPES:
        out_shape += [jax.ShapeDtypeStruct(shape, F32)] * 4
    outs = pl.pallas_call(
        body,
        name="adamw",
        out_shape=tuple(out_shape),
        in_specs=[vm] * (4 * n_p),
        out_specs=(vm,) * (4 * n_p),
        compiler_params=pltpu.CompilerParams(vmem_limit_bytes=VMEM_LIMIT),
    )(*[grads[n] for n in names], *[a for n in names for a in params[n]])
    return {n: outs[4 * i:4 * i + 4] for i, n in enumerate(names)}


N_CHIPS = 4


def _reduce_plan(pays, owns, r1s, sums, r2s, send1, recv1, send2, recv2, local_sems):
    x, y, c, _ = _device_position()
    sibling = (x, y, 1 - c)
    chips = [((1 - x if rj & 2 else x), (1 - y if rj & 1 else y)) for rj in range(N_CHIPS)]
    n = len(pays)

    def slot_of(rj, core):
        return 4 * chips[rj][0] + 2 * chips[rj][1] + core

    def to_sibling(a, rj):
        return pltpu.make_async_remote_copy(
            src_ref=pays[a].at[slot_of(rj, 1 - c)], dst_ref=r1s[a].at[rj],
            send_sem=send1.at[N_CHIPS * a + rj], recv_sem=recv1.at[N_CHIPS * a + rj],
            device_id=sibling, device_id_type=pl.DeviceIdType.MESH)

    def load_own(a, rj):
        return pltpu.make_async_copy(pays[a].at[slot_of(rj, c)], owns[a].at[rj], local_sems.at[2 * N_CHIPS * a + rj])

    def to_chip(a, rj):
        return pltpu.make_async_remote_copy(
            src_ref=sums[a].at[rj], dst_ref=r2s[a].at[rj],
            send_sem=send2.at[N_CHIPS * a + rj], recv_sem=recv2.at[N_CHIPS * a + rj],
            device_id=(*chips[rj], c), device_id_type=pl.DeviceIdType.MESH)

    def keep(a):
        return pltpu.make_async_copy(sums[a].at[0], r2s[a].at[0], local_sems.at[2 * N_CHIPS * a + N_CHIPS])

    def start():
        for a in range(n):
            for rj in range(N_CHIPS):
                to_sibling(a, rj).start()
                if owns[a] is not None:
                    load_own(a, rj).start()

    def combine():
        for a in range(n):
            for rj in range(N_CHIPS):
                to_sibling(a, rj).wait_recv()
                if owns[a] is not None:
                    load_own(a, rj).wait()
                    mine = owns[a][rj]
                else:
                    mine = pays[a][slot_of(rj, c)]
                sums[a][rj] = (mine.astype(F32) + r1s[a][rj].astype(F32)).astype(sums[a].dtype)
            keep(a).start()
            for rj in range(1, N_CHIPS):
                to_chip(a, rj).start()

    def finish():
        for a in range(n):
            for rj in range(1, N_CHIPS):
                to_chip(a, rj).wait_recv()
            for rj in range(N_CHIPS):
                to_sibling(a, rj).wait_send()
            for rj in range(1, N_CHIPS):
                to_chip(a, rj).wait_send()
            keep(a).wait()

    return start, combine, finish


def _reduce_scratch(shapes_dtypes, own_flags):
    out = []
    for (shape, dtype), own in zip(shapes_dtypes, own_flags):
        if own:
            out.append(pltpu.VMEM((N_CHIPS,) + shape, dtype))
        out += [pltpu.VMEM((N_CHIPS,) + shape, dtype), pltpu.VMEM((N_CHIPS,) + shape, dtype)]
    n = len(shapes_dtypes)
    out += [pltpu.SemaphoreType.DMA((N_CHIPS * n,))] * 4 + [pltpu.SemaphoreType.DMA((2 * N_CHIPS * n,))]
    return out


def _pack_small(ssmall, dwq, dwkv, dconv, dfinal, dgq, dgkv, dattn, dconvg, loss_part):
    ssmall[...] = jnp.zeros_like(ssmall)
    rep = ssmall.at[0]
    for i in range(D_MODEL // LANES):
        rep[ROW_FINAL + i:ROW_FINAL + i + 1, :] = dfinal[:, LANES * i:LANES * (i + 1)]
    for i in range(Q_RANK // LANES):
        rep[ROW_GQ + i:ROW_GQ + i + 1, :] = dgq[:, LANES * i:LANES * (i + 1)]
    rep[ROW_GKV:ROW_GKV + 1, :] = dgkv[...]
    for i in range(CONV_WIDTH // LANES):
        rep[ROW_ATTN + i:ROW_ATTN + i + 1, :] = dattn[:, LANES * i:LANES * (i + 1)]
        rep[ROW_CONVG + i:ROW_CONVG + i + 1, :] = dconvg[:, LANES * i:LANES * (i + 1)]
    rep[ROW_LOSS:ROW_LOSS + 1, :] = loss_part[...]
    for k in range(N_DEV):
        if k:
            ssmall[k, ROW_REPL:, :] = ssmall[0, ROW_REPL:, :]
        for s, e, d in _q_pieces(k):
            for i in range(Q_RANK // LANES):
                ssmall[k, ROW_Q + SHARD_Q * i + s:ROW_Q + SHARD_Q * i + e, :] = dwq[d:d + e - s, LANES * i:LANES * (i + 1)]
        ssmall[k, ROW_KV:ROW_KV + KV_RANK, :] = dwkv[:, _kv_dst(k):_kv_dst(k) + SHARD_KV]
        ssmall[k, ROW_CONV:ROW_CONV + 3, 0:SHARD_CONV] = dconv[0:3, SHARD_CONV * k:SHARD_CONV * (k + 1)]


TOKEN_TILES = 4
TAIL_ROWS = N_META + D_MODEL // LANES


def _reduce_tail(r_in, r_out, r_small, d_meta, d_norm):
    n_p = len(PARAM_SHAPES)
    names = [n for n, _ in PARAM_SHAPES]

    def body(*refs):
        rin, rout, rsmall, dmeta, dnorm = refs[:5]
        g_out = {n: refs[5 + i] for i, n in enumerate(names)}
        loss_out = refs[5 + n_p]
        stail, rtail, gsum, gtail, send_sems, recv_sems = refs[6 + n_p:]
        x, y, c, me = _device_position()
        my_chip = 2 * x + y

        for k in range(N_DEV):
            stail[k, 0:N_META, :] = dmeta[:, SHARD_META * k:SHARD_META * (k + 1)]
            for i in range(D_MODEL // LANES):
                stail[k, N_META + i:N_META + i + 1, :] = dnorm[:, LANES * i:LANES * (i + 1)]
        copies = []
        for r in range(1, N_DEV):
            peer = (1 - x if r & 4 else x, 1 - y if r & 2 else y, 1 - c if r & 1 else c)
            copies.append(pltpu.make_async_remote_copy(
                src_ref=stail.at[4 * peer[0] + 2 * peer[1] + peer[2]],
                dst_ref=rtail.at[r],
                send_sem=send_sems.at[r - 1],
                recv_sem=recv_sems.at[r - 1],
                device_id=peer,
                device_id_type=pl.DeviceIdType.MESH,
            ))
        for cp in copies:
            cp.start()
        rtail[0] = stail[me]

        g = rin[my_chip].astype(F32)
        for ch in range(1, N_CHIPS):
            g = g + rin[ch ^ my_chip].astype(F32)
        g_out["w_in"][...] = g[:SHARD_IN, :]

        g = rout[my_chip].astype(F32)
        gs = rsmall[my_chip]
        for ch in range(1, N_CHIPS):
            g = g + rout[ch ^ my_chip].astype(F32)
            gs = gs + rsmall[ch ^ my_chip]
        g_out["w_out"][...] = g
        gsum[...] = gs
        for i in range(Q_RANK // LANES):
            g_out["w_q_up"][:, LANES * i:LANES * (i + 1)] = gsum[ROW_Q + SHARD_Q * i:ROW_Q + SHARD_Q * (i + 1), :]
        g_out["w_kv_up"][...] = gsum[ROW_KV:ROW_KV + KV_RANK, :]
        g_out["conv_w"][...] = gsum[ROW_CONV:ROW_CONV + 3, 0:SHARD_CONV]
        for name, row, width in (("final_norm_g", ROW_FINAL, D_MODEL), ("q_norm_g", ROW_GQ, Q_RANK),
                                 ("kv_norm_g", ROW_GKV, KV_RANK), ("attn_out_g", ROW_ATTN, CONV_WIDTH),
                                 ("conv_out_g", ROW_CONVG, CONV_WIDTH)):
            for i in range(width // LANES):
                g_out[name][:, LANES * i:LANES * (i + 1)] = gsum[row + i:row + i + 1, :]
        loss_out[...] = gsum[ROW_LOSS:ROW_LOSS + 1, :]

        for cp in copies:
            cp.wait_recv()
        gt = rtail[me]
        for d in range(1, N_DEV):
            gt = gt + rtail[d ^ me]
        gtail[...] = gt
        g_out["meta_tokens"][...] = gtail[0:N_META, :]
        for i in range(D_MODEL // LANES):
            g_out["norm_g"][:, LANES * i:LANES * (i + 1)] = gtail[N_META + i:N_META + i + 1, :]
        for cp in copies:
            cp.wait_send()

    vm = pl.BlockSpec(memory_space=pltpu.VMEM)
    out_shape = [jax.ShapeDtypeStruct(shape, F32) for _, shape in PARAM_SHAPES]
    out_shape.append(jax.ShapeDtypeStruct((1, LANES), F32))
    outs = pl.pallas_call(
        body,
        name="reduce_tail",
        out_shape=tuple(out_shape),
        in_specs=[vm] * 5,
        out_specs=(vm,) * len(out_shape),
        scratch_shapes=[
            pltpu.VMEM((N_DEV, TAIL_ROWS, LANES), F32),
            pltpu.VMEM((N_DEV, TAIL_ROWS, LANES), F32),
            pltpu.VMEM((SMALL_ROWS, LANES), F32),
            pltpu.VMEM((TAIL_ROWS, LANES), F32),
            pltpu.SemaphoreType.DMA((N_DEV - 1,)),
            pltpu.SemaphoreType.DMA((N_DEV - 1,)),
        ],
        compiler_params=pltpu.CompilerParams(vmem_limit_bytes=VMEM_LIMIT),
    )(r_in, r_out, r_small, d_meta, d_norm)
    return {n: outs[i] for i, n in enumerate(names)}, outs[-1]


def _prep_gather(x, meta, norm_g, w_in_t):
    nb_seq, s, d = x.shape
    nb = s // LANES + 1
    relay_step = nb_seq * (nb - 1) // 2
    forward_step = nb_seq * (nb - 1) - 1
    finish_step = nb_seq * (nb - 1)

    def body(x_ref, meta_ref, g_ref, win_ref, u_ref, w_in_p, meta_f,
             sbig, ssmall, gbig, gsmall, send_sems, recv_sems, local_sems):
        jj, b = pl.program_id(0), pl.program_id(1)
        t = jj * nb_seq + b

        def plan():
            return _gather_plan((sbig, ssmall), (gbig, gsmall), send_sems, recv_sems, local_sems)

        @pl.when(t == 0)
        def _():
            sbig[0:SHARD_IN, :] = win_ref[...].astype(BF16)
            sbig[SHARD_IN:, :] = jnp.zeros((SHARD_IN_PAD - SHARD_IN, d), BF16)
            ssmall[...] = meta_ref[...]
            plan()[0]()

        @pl.when(t == relay_step)
        def _():
            plan()[1]()

        @pl.when(t == forward_step)
        def _():
            plan()[2]()

        @pl.when(t == finish_step)
        def _():
            plan()[3]()
            w_in_p[N_A:GRP_A, :] = jnp.zeros((GRP_A - N_A, d), BF16)
            for k in range(N_DEV):
                for s0, e0, d0 in _in_pieces(k):
                    w_in_p[d0:d0 + e0 - s0, :] = gbig[k, s0:e0, :]
                meta_f[:, SHARD_META * k:SHARD_META * (k + 1)] = gsmall[k]

        def norm(h):
            hhat, _ = _rms_stats(h)
            return (hhat * g_ref[...]).astype(BF16)

        @pl.when(jj < nb - 1)
        def _():
            u_ref[...] = norm(x_ref[0])

        @pl.when(jj == nb - 1)
        def _():
            u_ref[0:PAD_FRONT, :] = jnp.zeros((PAD_FRONT, d), BF16)
            u_ref[PAD_FRONT:LANES, :] = norm(meta_f[...])

    whole = lambda shape: pl.BlockSpec(shape, lambda jj, b: (0,) * len(shape))
    return pl.pallas_call(
        body,
        name="prep_norm_gather",
        grid=(nb, nb_seq),
        in_specs=[
            pl.BlockSpec((1, LANES, d), lambda jj, b: (b, jnp.minimum(jj, nb - 2), 0)),
            whole(meta.shape), whole(norm_g.shape), whole(w_in_t.shape),
        ],
        out_specs=(pl.BlockSpec((LANES, d), lambda jj, b: (b * nb + (jj + 1) % nb, 0)),
                   whole((IN_PAD, d)), whole((N_META, d))),
        out_shape=(jax.ShapeDtypeStruct((nb_seq * nb * LANES, d), BF16),
                   jax.ShapeDtypeStruct((IN_PAD, d), BF16),
                   jax.ShapeDtypeStruct((N_META, d), F32)),
        scratch_shapes=[
            pltpu.VMEM((SHARD_IN_PAD, d), BF16),
            pltpu.VMEM((N_META, SHARD_META), F32),
            pltpu.VMEM((N_DEV, SHARD_IN_PAD, d), BF16),
            pltpu.VMEM((N_DEV, N_META, SHARD_META), F32),
            pltpu.SemaphoreType.DMA((14,)),
            pltpu.SemaphoreType.DMA((14,)),
            pltpu.SemaphoreType.DMA((2,)),
        ],
        compiler_params=_params("arbitrary", "arbitrary"),
    )(x, meta, norm_g, w_in_t)


def _in_proj_gather(u, w_in_p, w_q, w_kv, w_out, conv_w, bm, bn):
    m, k_dim = u.shape
    n = w_in_p.shape[0]
    steps = (m // bm) * (n // bn)
    relay_step, forward_step = steps // 3, 2 * steps // 3
    qkv_shape = (SHARD_Q + KV_RANK, Q_RANK)

    def body(a_ref, b_ref, wq_ref, wkv_ref, wout_ref, conv_ref, o_ref, w_q_p, w_kv_p, w_out_f, conv_f,
             sqkv, sout, sconv, gqkv, gout, gconv, send_sems, recv_sems, local_sems):
        t = pl.program_id(0) * (n // bn) + pl.program_id(1)

        def plan():
            return _gather_plan((sqkv, sout, sconv), (gqkv, gout, gconv), send_sems, recv_sems, local_sems)

        @pl.when(t == 0)
        def _():
            sqkv[...] = jnp.zeros_like(sqkv)
            sqkv[0:SHARD_Q, :] = wq_ref[...].astype(BF16)
            sqkv[SHARD_Q:, 0:SHARD_KV] = wkv_ref[...].astype(BF16)
            sout[...] = wout_ref[...].astype(BF16)
            sconv[...] = jnp.zeros_like(sconv)
            sconv[0:3, 0:SHARD_CONV] = conv_ref[...]
            plan()[0]()

        @pl.when(t == relay_step)
        def _():
            plan()[1]()

        @pl.when(t == forward_step)
        def _():
            plan()[2]()

        o_ref[...] = _dot(a_ref[...], b_ref[...], _NT).astype(o_ref.dtype)

        @pl.when(t == steps - 1)
        def _():
            plan()[3]()
            conv_f[...] = jnp.zeros_like(conv_f)
            for k in range(N_DEV):
                for s0, e0, d0 in _q_pieces(k):
                    w_q_p[d0:d0 + e0 - s0, :] = gqkv[k, s0:e0, :]
                w_kv_p[:, _kv_dst(k):_kv_dst(k) + SHARD_KV] = gqkv[k, SHARD_Q:, 0:SHARD_KV]
                w_out_f[SHARD_OUT * k:SHARD_OUT * (k + 1), :] = gout[k]
                conv_f[0:3, SHARD_CONV * k:SHARD_CONV * (k + 1)] = gconv[k, 0:3, 0:SHARD_CONV]

    whole = lambda shape: pl.BlockSpec(shape, lambda i, j: (0,) * len(shape))
    return pl.pallas_call(
        body,
        name="in_proj_gather",
        grid=(m // bm, n // bn),
        in_specs=[pl.BlockSpec((bm, k_dim), lambda i, j: (i, 0)), pl.BlockSpec((bn, k_dim), lambda i, j: (j, 0)),
                  whole(w_q.shape), whole(w_kv.shape), whole(w_out.shape), whole(conv_w.shape)],
        out_specs=(pl.BlockSpec((bm, bn), lambda i, j: (i, j)),
                   whole((Q_COLS, Q_RANK)), whole((KV_RANK, KV_COLS)), whole((D_MODEL, D_MODEL)),
                   whole((8, CONV_WIDTH))),
        out_shape=(jax.ShapeDtypeStruct((m, n), BF16),
                   jax.ShapeDtypeStruct((Q_COLS, Q_RANK), BF16),
                   jax.ShapeDtypeStruct((KV_RANK, KV_COLS), BF16),
                   jax.ShapeDtypeStruct((D_MODEL, D_MODEL), BF16),
                   jax.ShapeDtypeStruct((8, CONV_WIDTH), F32)),
        scratch_shapes=[
            pltpu.VMEM(qkv_shape, BF16),
            pltpu.VMEM((SHARD_OUT, D_MODEL), BF16),
            pltpu.VMEM((8, LANES), F32),
            pltpu.VMEM((N_DEV,) + qkv_shape, BF16),
            pltpu.VMEM((N_DEV, SHARD_OUT, D_MODEL), BF16),
            pltpu.VMEM((N_DEV, 8, LANES), F32),
            pltpu.SemaphoreType.DMA((21,)),
            pltpu.SemaphoreType.DMA((21,)),
            pltpu.SemaphoreType.DMA((3,)),
        ],
        compiler_params=_params("arbitrary", "arbitrary"),
    )(u, w_in_p, w_q, w_kv, w_out, conv_w)


def _rope_tables(tp):
    half = D_ROPE // 2
    inv_freq = 1.0 / (ROPE_THETA ** (jnp.arange(half, dtype=F32) / half))
    pos = (jnp.arange(tp) - PAD_FRONT).astype(F32)
    ang = pos[:, None] * inv_freq[None, :]
    cos = jnp.tile(jnp.cos(ang), (1, LANES // half))
    sin = jnp.tile(jnp.sin(ang), (1, LANES // half))
    first = (jnp.arange(LANES) % D_ROPE) < half
    return cos, jnp.where(first, -sin, 0.0), jnp.where(first, 0.0, sin)


def _rope(t, cos, sa, sb):
    return t * cos + pltpu.roll(t, LANES - D_ROPE // 2, 1) * sa + pltpu.roll(t, D_ROPE // 2, 1) * sb


def _rope_t(t, cos, sa, sb):
    return t * cos + pltpu.roll(t * sa, D_ROPE // 2, 1) + pltpu.roll(t * sb, LANES - D_ROPE // 2, 1)


def _qkv_fwd(p, wq, wkv, gq, gkv, tables, nb_seq, tp):
    ht = tp // 2

    def body(pa_ref, wq_ref, wkv_ref, gq_ref, gkv_ref, cos_ref, sa_ref, sb_ref, q_ref, k_ref, v_ref):
        pa = pa_ref[...].astype(F32)
        cq_hat, _ = _rms_stats(pa[:, :Q_RANK])
        ckv_hat, _ = _rms_stats(pa[:, Q_RANK:Q_RANK + KV_RANK])
        q = _dot((cq_hat * gq_ref[...]).astype(BF16), wq_ref[...], _NT) * Q_SCALE
        kv = _dot((ckv_hat * gkv_ref[...]).astype(BF16), wkv_ref[...])
        tabs = (cos_ref[...], sa_ref[...], sb_ref[...])
        lane = lax.broadcasted_iota(jnp.int32, (ht, LANES), 1)
        low = lane < D_ROPE
        mark = lane == D_ROPE
        row = (pl.program_id(0) % 2) * ht + lax.broadcasted_iota(jnp.int32, (ht, LANES), 0)
        k_pe = jnp.where(mark & (row < PAD_FRONT), NEG_INF, _rope(pa[:, Q_RANK + KV_RANK:], *tabs))
        one = jnp.where(mark & (row >= PAD_FRONT), 1.0, 0.0)
        pairs = [_rope(q[:, N_HEADS * D_NOPE + LANES * i:N_HEADS * D_NOPE + LANES * (i + 1)], *tabs) for i in range(2)]
        for h in range(N_HEADS):
            pair = pairs[h // 2]
            if h % 2:
                pair = pltpu.roll(pair, D_ROPE, 1)
            pe = jnp.where(low, pair, one)
            q_ref[0, h] = jnp.concatenate([q[:, D_NOPE * h:D_NOPE * (h + 1)], pe], axis=1).astype(BF16)
            k_ref[0, h] = jnp.concatenate([kv[:, D_NOPE * h:D_NOPE * (h + 1)], k_pe], axis=1).astype(BF16)
            v_ref[0, h] = kv[:, N_HEADS * D_NOPE + D_V * h:N_HEADS * D_NOPE + D_V * (h + 1)].astype(BF16)

    full = lambda a: pl.BlockSpec(a.shape, lambda i: (0,) * a.ndim)
    tab = pl.BlockSpec((ht, LANES), lambda i: (i % 2, 0))
    qk = pl.BlockSpec((1, N_HEADS, ht, 2 * LANES), lambda i: (i // 2, 0, i % 2, 0))
    return pl.pallas_call(
        body,
        name="qkv_fwd",
        grid=(2 * nb_seq,),
        in_specs=[pl.BlockSpec((ht, GRP_A), lambda i: (i, 0)), full(wq), full(wkv), full(gq), full(gkv), tab, tab, tab],
        out_specs=(qk, qk, pl.BlockSpec((1, N_HEADS, ht, D_V), lambda i: (i // 2, 0, i % 2, 0))),
        out_shape=(
            jax.ShapeDtypeStruct((nb_seq, N_HEADS, tp, 2 * LANES), BF16),
            jax.ShapeDtypeStruct((nb_seq, N_HEADS, tp, 2 * LANES), BF16),
            jax.ShapeDtypeStruct((nb_seq, N_HEADS, tp, D_V), BF16),
        ),
        compiler_params=_params("parallel"),
    )(p, wq, wkv, gq, gkv, *tables)


def _attn_fwd(q, k, v, p, g_attn):
    nb_seq, _, tp, _ = q.shape

    def body(q_ref, k_ref, v_ref, z_ref, g_ref, y_ref, o_ref, lse_ref):
        g = g_ref[...]
        for r0 in range(0, tp, Q_TILE):
            nq = min(Q_TILE, tp - r0)
            kend = r0 + nq
            qq = q_ref[0, 0, r0:kend, :]
            sd = _dot(qq, k_ref[0, 0, r0:kend, :], _NT)
            causal = (lax.broadcasted_iota(jnp.int32, (nq, nq), 1) <= lax.broadcasted_iota(jnp.int32, (nq, nq), 0))
            sd = jnp.where(causal, sd, NEG_INF)
            m = jnp.max(sd, axis=-1, keepdims=True)
            if r0:
                so = _dot(qq, k_ref[0, 0, 0:r0, :], _NT)
                m = jnp.maximum(m, jnp.max(so, axis=-1, keepdims=True))
            ed = jnp.exp2(sd - m)
            l = jnp.sum(ed, axis=-1, keepdims=True)
            o = _dot(ed.astype(BF16), v_ref[0, 0, r0:kend, :])
            if r0:
                eo = jnp.exp2(so - m)
                l = l + jnp.sum(eo, axis=-1, keepdims=True)
                o = o + _dot(eo.astype(BF16), v_ref[0, 0, 0:r0, :])
            o = o * (1.0 / l)
            o_ref[0, 0, r0:kend, :] = o
            lse_ref[0, 0, r0:kend, :] = jnp.broadcast_to(m + jnp.log2(l), (nq, LANES))
            ohat, _ = _rms_stats(o)
            z = z_ref[r0:kend, :].astype(F32)
            y_ref[r0:kend, :] = (ohat * g * (z * _sigmoid(z))).astype(BF16)

    qk = pl.BlockSpec((1, 1, tp, 2 * LANES), lambda b, h: (b, h, 0, 0))
    hv = pl.BlockSpec((1, 1, tp, D_V), lambda b, h: (b, h, 0, 0))
    return pl.pallas_call(
        body,
        name="attn_fwd",
        grid=(nb_seq, N_HEADS),
        in_specs=[qk, qk, hv,
                  pl.BlockSpec((tp, LANES), lambda b, h: (b, GRP_A // LANES + h)),
                  pl.BlockSpec((1, LANES), lambda b, h: (0, h))],
        out_specs=(pl.BlockSpec((tp, LANES), lambda b, h: (b, h)), hv, hv),
        out_shape=(
            jax.ShapeDtypeStruct((nb_seq * tp, N_HEADS * D_V), BF16),
            jax.ShapeDtypeStruct((nb_seq, N_HEADS, tp, D_V), F32),
            jax.ShapeDtypeStruct((nb_seq, N_HEADS, tp, LANES), F32),
        ),
        compiler_params=_params("parallel", "parallel"),
    )(q, k, v, p, g_attn)


_CONV_COL0 = (GRP_A + N_HEADS * D_V) // LANES


def _conv_specs(tp, order):
    cols = CONV_WIDTH // LANES
    return [pl.BlockSpec((tp, LANES), functools.partial(
        lambda a, b, off: order(a, b, off), off=_CONV_COL0 + i * cols)) for i in range(4)]


def _conv_fwd(p, conv_w, g_conv, nb_seq, tp):
    def body(b_ref, c_ref, h_ref, z_ref, w_ref, g_ref, y_ref):
        cc = c_ref[...].astype(F32) * h_ref[...].astype(F32)
        row = lax.broadcasted_iota(jnp.int32, (tp, LANES), 0)
        s1 = jnp.where(row >= 1, pltpu.roll(cc, 1, 0), 0.0)
        s2 = jnp.where(row >= 2, pltpu.roll(cc, 2, 0), 0.0)
        yc = b_ref[...].astype(F32) * (w_ref[0:1, :] * s2 + w_ref[1:2, :] * s1 + w_ref[2:3, :] * cc)
        r = lax.rsqrt(_group_mean(yc * yc) + EPS)
        z = z_ref[...].astype(F32)
        y_ref[...] = (yc * r * g_ref[...] * (z * _sigmoid(z))).astype(BF16)

    return pl.pallas_call(
        body,
        name="conv_fwd",
        grid=(nb_seq, CONV_WIDTH // LANES),
        in_specs=_conv_specs(tp, lambda b, t, off: (b, off + t)) + [
            pl.BlockSpec((8, LANES), lambda b, t: (0, t)),
            pl.BlockSpec((1, LANES), lambda b, t: (0, t))],
        out_specs=pl.BlockSpec((tp, LANES), lambda b, t: (b, t)),
        out_shape=jax.ShapeDtypeStruct((nb_seq * tp, CONV_WIDTH), BF16),
        compiler_params=_params("parallel", "parallel"),
    )(p, p, p, p, conv_w, g_conv)


def _token_copy(hbm, b, k, ts, buf, sem, to_hbm=False):
    lo, hi = max(k * ts - LANES, 0), (k + 1) * ts - LANES
    off = lo - (k * ts - LANES)
    src, dst = hbm.at[b, pl.ds(lo, hi - lo)], buf.at[pl.ds(off, hi - lo)]
    if to_hbm:
        src, dst = dst, src
    return pltpu.make_async_copy(src, dst, sem)


def _for_tile(k, nt, fn):
    for kk in range(nt):
        @pl.when(k == kk)
        def _(kk=kk):
            fn(kk)


def _out_proj_loss(ya, yc, w_out, x, target, g_final, nt):
    nb_seq, s, d = x.shape
    r, ka = ya.shape
    n_mix = w_out.shape[0]
    ts = (s + LANES) // nt
    steps = nb_seq * nt

    def body(a_ref, c_ref, ap_ref, cp_ref, w_ref, x_hbm, t_hbm, g_ref, dhb_ref, dcat_ref, dw_ref, dg_ref, loss_ref,
             xbuf, tbuf, dh_prev, acc_ref, dw_acc, sems):
        i = pl.program_id(0)
        cur = jnp.minimum(i, steps - 1)
        b, k = cur // nt, cur % nt
        slot = cur % 2

        def fetch(seq, kk, sl):
            return [_token_copy(x_hbm, seq, kk, ts, xbuf.at[sl], sems.at[sl, 0]),
                    _token_copy(t_hbm, seq, kk, ts, tbuf.at[sl], sems.at[sl, 1])]

        def start(seq, sl, kk):
            if kk == 0:
                xbuf[sl, 0:LANES, :] = jnp.zeros((LANES, d), F32)
                tbuf[sl, 0:LANES, :] = jnp.zeros((LANES, d), F32)
            for cp in fetch(seq, kk, sl):
                cp.start()

        @pl.when(i == 0)
        def _():
            acc_ref[...] = jnp.zeros_like(acc_ref)
            dg_ref[...] = jnp.zeros_like(dg_ref)
            dw_acc[...] = jnp.zeros_like(dw_acc)
            dh_prev[...] = jnp.zeros_like(dh_prev)
            start(0, 0, 0)

        @pl.when(i + 1 < steps)
        def _():
            _for_tile((i + 1) % nt, nt, functools.partial(start, (i + 1) // nt, 1 - slot))

        @pl.when(i < steps)
        def _():
            _for_tile(k, nt, lambda kk: [cp.wait() for cp in fetch(b, kk, slot)])

        live = jnp.where(i < steps, 1.0, 0.0)
        mix = _dot(a_ref[...], w_ref[0:ka, :]) + _dot(c_ref[...], w_ref[ka:, :])
        real = (lax.broadcasted_iota(jnp.int32, (ts, d), 0) >= LANES) | (k > 0)
        g = g_ref[...]
        hhat, rstd = _rms_stats(xbuf[slot] + mix)
        e = jnp.where(real, hhat * g - tbuf[slot], 0.0)
        acc_ref[...] += live * jnp.sum(e * e, axis=0, keepdims=True)
        dy = e * (1.0 / d)
        dg_ref[...] += live * jnp.sum(dy * hhat, axis=0, keepdims=True)
        dhb = _rms_bwd(g * dy, hhat, rstd).astype(BF16)
        dhb_ref[...] = dhb

        dprev = dh_prev[...]
        dcat_ref[...] = _dot(dprev, w_ref[...], _NT).astype(BF16)
        dw_acc[0:ka, :] += _dot(ap_ref[...], dprev, _TN)
        dw_acc[ka:, :] += _dot(cp_ref[...], dprev, _TN)
        dh_prev[...] = dhb

        @pl.when(i == steps)
        def _():
            total = jnp.sum(acc_ref[...], axis=1, keepdims=True)
            loss_ref[...] = jnp.broadcast_to((0.5 / d) * total, loss_ref.shape)
            dw_ref[...] = dw_acc[...].astype(BF16)

    hbm = pl.BlockSpec(memory_space=pl.ANY)
    cur_row = lambda i: (jnp.minimum(i, steps - 1), 0)
    prev_row = lambda i: (jnp.maximum(i - 1, 0), 0)
    vec = pl.BlockSpec((1, d), lambda i: (0, 0))
    kc = yc.shape[1]
    return pl.pallas_call(
        body,
        name="out_proj_loss",
        grid=(steps + 1,),
        in_specs=[pl.BlockSpec((ts, ka), cur_row), pl.BlockSpec((ts, kc), cur_row),
                  pl.BlockSpec((ts, ka), prev_row), pl.BlockSpec((ts, kc), prev_row),
                  pl.BlockSpec(w_out.shape, lambda i: (0, 0)), hbm, hbm, vec],
        out_specs=(pl.BlockSpec((ts, d), cur_row), pl.BlockSpec((ts, n_mix), prev_row),
                   pl.BlockSpec((n_mix, d), lambda i: (0, 0)), vec, pl.BlockSpec((1, LANES), lambda i: (0, 0))),
        out_shape=(
            jax.ShapeDtypeStruct((r, d), BF16),
            jax.ShapeDtypeStruct((r, n_mix), BF16),
            jax.ShapeDtypeStruct((n_mix, d), BF16),
            jax.ShapeDtypeStruct((1, d), F32),
            jax.ShapeDtypeStruct((1, LANES), F32),
        ),
        scratch_shapes=[pltpu.VMEM((2, ts, d), F32), pltpu.VMEM((2, ts, d), F32), pltpu.VMEM((ts, d), BF16),
                        pltpu.VMEM((1, d), F32), pltpu.VMEM((n_mix, d), F32), pltpu.SemaphoreType.DMA((2, 2))],
        compiler_params=_params("arbitrary"),
    )(ya, yc, ya, yc, w_out, x, target, g_final)


def _attn_bwd(q, k, v, o, lse, dcat, p, g_attn):
    nb_seq, _, tp, _ = q.shape

    def body(q_ref, k_ref, v_ref, o_ref, lse_ref, dy_ref, z_ref, g_ref,
             dq_ref, dk_ref, dv_ref, dz_ref, dg_ref, dq_acc):
        @pl.when(pl.program_id(1) == 0)
        def _():
            dg_ref[...] = jnp.zeros_like(dg_ref)

        g = g_ref[...]
        z = z_ref[...].astype(F32)
        o = o_ref[0, 0]
        dy = dy_ref[...].astype(F32)
        sig = _sigmoid(z)
        ohat, r = _rms_stats(o)
        don = dy * (z * sig)
        dz_ref[...] = (dy * (ohat * g) * (sig * (1.0 + z * (1.0 - sig)))).astype(BF16)
        dg_ref[...] += jnp.sum(don * ohat, axis=0, keepdims=True)
        do = _rms_bwd(g * don, ohat, r)
        dvec = jnp.sum(do * o, axis=-1, keepdims=True)
        dob = do.astype(BF16)
        lse_col = lse_ref[0, 0, :, 0:1]
        dq_acc[...] = jnp.zeros_like(dq_acc)
        for k0 in range(0, tp, K_TILE):
            nk = min(K_TILE, tp - k0)
            nq = tp - k0
            qq = q_ref[0, 0, k0:, :]
            kk = k_ref[0, 0, k0:k0 + nk, :]
            causal = (lax.broadcasted_iota(jnp.int32, (nq, nk), 1) <= lax.broadcasted_iota(jnp.int32, (nq, nk), 0))
            pr = jnp.where(causal, jnp.exp2(_dot(qq, kk, _NT) - lse_col[k0:]), 0.0)
            dp = _dot(dob[k0:], v_ref[0, 0, k0:k0 + nk, :], _NT)
            ds = (pr * (dp - dvec[k0:])).astype(BF16)
            dv_ref[0, 0, k0:k0 + nk, :] = _dot(pr.astype(BF16), dob[k0:], _TN).astype(BF16)
            dk_ref[0, 0, k0:k0 + nk, :] = (_dot(ds, qq, _TN) * (ATTN_SCALE / Q_SCALE)).astype(BF16)
            dq_acc[k0:, :] += _dot(ds, kk)
        dq_ref[0, 0] = (dq_acc[...] * ATTN_SCALE).astype(BF16)

    qk = pl.BlockSpec((1, 1, tp, 2 * LANES), lambda h, b: (b, h, 0, 0))
    hv = pl.BlockSpec((1, 1, tp, D_V), lambda h, b: (b, h, 0, 0))
    col = pl.BlockSpec((tp, LANES), lambda h, b: (b, h))
    return pl.pallas_call(
        body,
        name="attn_bwd",
        grid=(N_HEADS, nb_seq),
        in_specs=[qk, qk, hv, hv, hv, col,
                  pl.BlockSpec((tp, LANES), lambda h, b: (b, GRP_A // LANES + h)),
                  pl.BlockSpec((1, LANES), lambda h, b: (0, h))],
        out_specs=(qk, qk, hv, col, pl.BlockSpec((1, LANES), lambda h, b: (0, h))),
        out_shape=(
            jax.ShapeDtypeStruct((nb_seq, N_HEADS, tp, 2 * LANES), BF16),
            jax.ShapeDtypeStruct((nb_seq, N_HEADS, tp, 2 * LANES), BF16),
            jax.ShapeDtypeStruct((nb_seq, N_HEADS, tp, D_V), BF16),
            jax.ShapeDtypeStruct((nb_seq * tp, N_HEADS * D_V), BF16),
            jax.ShapeDtypeStruct((1, N_HEADS * D_V), F32),
        ),
        scratch_shapes=[pltpu.VMEM((tp, 2 * LANES), F32)],
        compiler_params=_params("arbitrary", "arbitrary"),
    )(q, k, v, o, lse, dcat, p, g_attn)


def _qkv_bwd(p, dq, dk, dv, wq, wkv, gq, gkv, tables):
    nb_seq, _, tp, _ = dq.shape
    ht = tp // 2

    def body(pa_ref, dq_ref, dk_ref, dv_ref, wq_ref, wkv_ref, gq_ref, gkv_ref, cos_ref, sa_ref, sb_ref,
             dpa_ref, dwq_ref, dwkv_ref, dgq_ref, dgkv_ref):
        @pl.when(pl.program_id(0) == 0)
        def _():
            dwq_ref[...] = jnp.zeros_like(dwq_ref)
            dwkv_ref[...] = jnp.zeros_like(dwkv_ref)
            dgq_ref[...] = jnp.zeros_like(dgq_ref)
            dgkv_ref[...] = jnp.zeros_like(dgkv_ref)

        pa = pa_ref[...].astype(F32)
        gq, gkv = gq_ref[...], gkv_ref[...]
        cq_hat, rq = _rms_stats(pa[:, :Q_RANK])
        ckv_hat, rkv = _rms_stats(pa[:, Q_RANK:Q_RANK + KV_RANK])
        tabs = (cos_ref[...], sa_ref[...], sb_ref[...])

        pe = [dq_ref[0, h, :, D_NOPE:].astype(F32) for h in range(N_HEADS)]
        pairs = [_rope_t(pe[2 * i] + pltpu.roll(pe[2 * i + 1], D_ROPE, 1), *tabs).astype(BF16) for i in range(2)]
        dq_flat = jnp.concatenate([dq_ref[0, h, :, :D_NOPE] for h in range(N_HEADS)] + pairs, axis=1)
        dwq_ref[...] += _dot(dq_flat, (cq_hat * gq).astype(BF16), _TN)
        dcqn = _dot(dq_flat, wq_ref[...])
        dgq_ref[...] += jnp.sum(dcqn * cq_hat, axis=0, keepdims=True)
        dcq = _rms_bwd(gq * dcqn, cq_hat, rq)

        dkv_flat = jnp.concatenate([dk_ref[0, h, :, :D_NOPE] for h in range(N_HEADS)]
                                   + [dv_ref[0, h] for h in range(N_HEADS)], axis=1)
        dwkv_ref[...] += _dot((ckv_hat * gkv).astype(BF16), dkv_flat, _TN)
        dckvn = _dot(dkv_flat, wkv_ref[...], _NT)
        dgkv_ref[...] += jnp.sum(dckvn * ckv_hat, axis=0, keepdims=True)
        dckv = _rms_bwd(gkv * dckvn, ckv_hat, rkv)

        dk_pe = dk_ref[0, 0, :, D_NOPE:].astype(F32)
        for h in range(1, N_HEADS):
            dk_pe = dk_pe + dk_ref[0, h, :, D_NOPE:].astype(F32)
        dk_pe = jnp.where(lax.broadcasted_iota(jnp.int32, (ht, LANES), 1) < D_ROPE, dk_pe, 0.0)
        dpa_ref[...] = jnp.concatenate([dcq, dckv, _rope_t(dk_pe, *tabs)], axis=1).astype(BF16)

    full = lambda a: pl.BlockSpec(a.shape, lambda i: (0,) * a.ndim)
    tab = pl.BlockSpec((ht, LANES), lambda i: (i % 2, 0))
    qk = pl.BlockSpec((1, N_HEADS, ht, 2 * LANES), lambda i: (i // 2, 0, i % 2, 0))
    acc = lambda shape: pl.BlockSpec(shape, lambda i: (0, 0))
    return pl.pallas_call(
        body,
        name="qkv_bwd",
        grid=(2 * nb_seq,),
        in_specs=[pl.BlockSpec((ht, GRP_A), lambda i: (i, 0)), qk, qk,
                  pl.BlockSpec((1, N_HEADS, ht, D_V), lambda i: (i // 2, 0, i % 2, 0)),
                  full(wq), full(wkv), full(gq), full(gkv), tab, tab, tab],
        out_specs=(pl.BlockSpec((ht, GRP_A), lambda i: (i, 0)),
                   acc(wq.shape), acc(wkv.shape), acc((1, Q_RANK)), acc((1, KV_RANK))),
        out_shape=(
            jax.ShapeDtypeStruct((nb_seq * tp, GRP_A), BF16),
            jax.ShapeDtypeStruct(wq.shape, F32),
            jax.ShapeDtypeStruct(wkv.shape, F32),
            jax.ShapeDtypeStruct((1, Q_RANK), F32),
            jax.ShapeDtypeStruct((1, KV_RANK), F32),
        ),
        compiler_params=_params("arbitrary"),
    )(p, dq, dk, dv, wq, wkv, gq, gkv, *tables)


def _conv_bwd(p, dcat, conv_w, g_conv, nb_seq, tp):
    cols = CONV_WIDTH // LANES

    def body(b_ref, c_ref, h_ref, z_ref, dy_ref, w_ref, g_ref,
             db_ref, dc_ref, dh_ref, dz_ref, dw_ref, dg_ref):
        @pl.when(pl.program_id(1) == 0)
        def _():
            dw_ref[...] = jnp.zeros_like(dw_ref)
            dg_ref[...] = jnp.zeros_like(dg_ref)

        cb, c, h = b_ref[...].astype(F32), c_ref[...].astype(F32), h_ref[...].astype(F32)
        z, dy = z_ref[...].astype(F32), dy_ref[...].astype(F32)
        g = g_ref[...]
        w0, w1, w2 = w_ref[0:1, :], w_ref[1:2, :], w_ref[2:3, :]
        cc = c * h
        row = lax.broadcasted_iota(jnp.int32, (tp, LANES), 0)
        s1 = jnp.where(row >= 1, pltpu.roll(cc, 1, 0), 0.0)
        s2 = jnp.where(row >= 2, pltpu.roll(cc, 2, 0), 0.0)
        dwc = w0 * s2 + w1 * s1 + w2 * cc
        yc = cb * dwc
        r = lax.rsqrt(_group_mean(yc * yc) + EPS)
        ychat = yc * r
        sig = _sigmoid(z)
        dz_ref[...] = (dy * (ychat * g) * (sig * (1.0 + z * (1.0 - sig)))).astype(BF16)
        dyn = dy * (z * sig)
        dg_ref[...] += jnp.sum(dyn * ychat, axis=0, keepdims=True)
        gd = g * dyn
        dyc = r * (gd - ychat * _group_mean(gd * ychat))
        db_ref[...] = (dyc * dwc).astype(BF16)
        ddw = dyc * cb
        dw_ref[0:1, :] += jnp.sum(ddw * s2, axis=0, keepdims=True)
        dw_ref[1:2, :] += jnp.sum(ddw * s1, axis=0, keepdims=True)
        dw_ref[2:3, :] += jnp.sum(ddw * cc, axis=0, keepdims=True)
        u1 = jnp.where(row <= tp - 2, pltpu.roll(ddw, tp - 1, 0), 0.0)
        u2 = jnp.where(row <= tp - 3, pltpu.roll(ddw, tp - 2, 0), 0.0)
        dcc = w2 * ddw + w1 * u1 + w0 * u2
        dc_ref[...] = (dcc * h).astype(BF16)
        dh_ref[...] = (dcc * c).astype(BF16)

    col = pl.BlockSpec((tp, LANES), lambda t, b: (b, t))
    out = jax.ShapeDtypeStruct((nb_seq * tp, CONV_WIDTH), BF16)
    return pl.pallas_call(
        body,
        name="conv_bwd",
        grid=(cols, nb_seq),
        in_specs=_conv_specs(tp, lambda t, b, off: (b, off + t)) + [
            pl.BlockSpec((tp, LANES), lambda t, b: (b, N_HEADS * D_V // LANES + t)),
            pl.BlockSpec((8, LANES), lambda t, b: (0, t)),
            pl.BlockSpec((1, LANES), lambda t, b: (0, t))],
        out_specs=(col, col, col, col,
                   pl.BlockSpec((8, LANES), lambda t, b: (0, t)), pl.BlockSpec((1, LANES), lambda t, b: (0, t))),
        out_shape=(out, out, out, out,
                   jax.ShapeDtypeStruct((8, CONV_WIDTH), F32), jax.ShapeDtypeStruct((1, CONV_WIDTH), F32)),
        compiler_params=_params("arbitrary", "arbitrary"),
    )(p, p, p, p, dcat, conv_w, g_conv)


def _input_bwd(dps, w_in, x, meta, dh, norm_g, nt, send_in):
    nb_seq, s, d = x.shape
    r, kb = dps[0].shape
    ts = (s + LANES) // nt
    steps = nb_seq * nt
    n_dp = len(dps)
    in_slot = send_in.shape[1:]

    def body(*refs):
        dp_refs, w_ref, x_hbm, meta_ref, dh_ref, g_ref, pay_ref = refs[:n_dp], *refs[n_dp:n_dp + 6]
        o = n_dp + 6
        gx_hbm, dmeta_ref, dg_ref, r2_in = refs[o:o + 4]
        xbuf, gxbuf, tok_sems, own_in, r1_in, sum_in = refs[o + 4:o + 10]
        sems = refs[o + 10:]
        i = pl.program_id(0)
        b, k = i // nt, i % nt

        def plan():
            return _reduce_plan((pay_ref,), (own_in,), (r1_in,), (sum_in,), (r2_in,), *sems)

        @pl.when(i == 0)
        def _():
            dmeta_ref[...] = jnp.zeros_like(dmeta_ref)
            dg_ref[...] = jnp.zeros_like(dg_ref)
            plan()[0]()

        @pl.when(i == 1)
        def _():
            plan()[1]()

        def start(kk):
            if kk == 0:
                xbuf[0:PAD_FRONT, :] = jnp.zeros((PAD_FRONT, d), F32)
                xbuf[PAD_FRONT:LANES, :] = meta_ref[...]
            _token_copy(x_hbm, b, kk, ts, xbuf, tok_sems.at[0]).start()

        _for_tile(k, nt, start)
        du = _dot(dp_refs[0][...], w_ref[0:kb, :])
        for j in range(1, n_dp):
            du = du + _dot(dp_refs[j][...], w_ref[kb * j:kb * (j + 1), :])
        _for_tile(k, nt, lambda kk: _token_copy(x_hbm, b, kk, ts, xbuf, tok_sems.at[0]).wait())

        g = g_ref[...]
        hhat, rstd = _rms_stats(xbuf[...])
        dg_ref[...] += jnp.sum(du * hhat, axis=0, keepdims=True)
        res = _rms_bwd(g * du, hhat, rstd) + dh_ref[...].astype(F32)

        @pl.when(i > 0)
        def _():
            _for_tile(k, nt, lambda kk: _token_copy(gx_hbm, b, (kk - 1) % nt, ts, gxbuf, tok_sems.at[1], True).wait())

        gxbuf[...] = res

        @pl.when(k == 0)
        def _():
            dmeta_ref[...] += gxbuf[PAD_FRONT:LANES, :]

        _for_tile(k, nt, lambda kk: _token_copy(gx_hbm, b, kk, ts, gxbuf, tok_sems.at[1], True).start())

        @pl.when(i == steps - 1)
        def _():
            _token_copy(gx_hbm, b, nt - 1, ts, gxbuf, tok_sems.at[1], True).wait()
            plan()[2]()

    whole = lambda a: pl.BlockSpec(a.shape, lambda i: (0,) * a.ndim)
    hbm = pl.BlockSpec(memory_space=pl.ANY)
    return pl.pallas_call(
        body,
        name="input_bwd",
        grid=(steps,),
        in_specs=[pl.BlockSpec((ts, kb), lambda i: (i, 0)) for _ in dps]
        + [whole(w_in), hbm, whole(meta), pl.BlockSpec((ts, d), lambda i: (i, 0)), whole(norm_g), hbm],
        out_specs=(hbm, pl.BlockSpec((N_META, d), lambda i: (0, 0)), pl.BlockSpec((1, d), lambda i: (0, 0)), hbm),
        out_shape=(jax.ShapeDtypeStruct((nb_seq, s, d), F32),
                   jax.ShapeDtypeStruct((N_META, d), F32),
                   jax.ShapeDtypeStruct((1, d), F32),
                   jax.ShapeDtypeStruct((N_CHIPS,) + in_slot, BF16)),
        scratch_shapes=[pltpu.VMEM((ts, d), F32), pltpu.VMEM((ts, d), F32), pltpu.SemaphoreType.DMA((2,))]
        + _reduce_scratch([(in_slot, BF16)], [True]),
        compiler_params=_params("arbitrary"),
    )(*dps, w_in, x, meta, dh, norm_g, send_in)


def _in_proj_bwd_w(u, dps, bm, small_grads, send_out):
    r, d = u.shape
    kb = dps[0].shape[1]
    steps = r // bm
    n_dp, n_small = len(dps), len(small_grads)
    out_slot, small_slot = send_out.shape[1:], (SMALL_ROWS, LANES)

    def body(*refs):
        u_ref, dp_refs = refs[0], refs[1:1 + n_dp]
        small_refs = refs[1 + n_dp:1 + n_dp + n_small]
        o = 1 + n_dp + n_small
        pay_out, o_ref, r2_out, r2_small = refs[o:o + 4]
        acc_ref, ssmall, r1_out, sum_out, r1_small, sum_small = refs[o + 4:o + 10]
        sems = refs[o + 10:]
        i = pl.program_id(0)

        def plan():
            return _reduce_plan((pay_out, ssmall), (None, None), (r1_out, r1_small), (sum_out, sum_small),
                                (r2_out, r2_small), *sems)

        @pl.when(i == 0)
        def _():
            acc_ref[...] = jnp.zeros_like(acc_ref)
            _pack_small(ssmall, *small_refs)
            plan()[0]()

        @pl.when(i == 1)
        def _():
            plan()[1]()

        uu = u_ref[...]
        for j in range(n_dp):
            acc_ref[kb * j:kb * (j + 1), :] += _dot(dp_refs[j][...], uu, _TN)

        @pl.when(i == steps - 1)
        def _():
            for k in range(N_DEV):
                for s, e, c0 in _in_pieces(k):
                    o_ref[k, s:e, :] = acc_ref[c0:c0 + e - s, :].astype(BF16)
                o_ref[k, SHARD_IN:, :] = jnp.zeros((SHARD_IN_PAD - SHARD_IN, d), BF16)
            plan()[2]()

    whole = lambda a: pl.BlockSpec(a.shape, lambda i: (0,) * a.ndim)
    hbm = pl.BlockSpec(memory_space=pl.ANY)
    return pl.pallas_call(
        body,
        name="in_proj_bwd_w",
        grid=(steps,),
        in_specs=[pl.BlockSpec((bm, d), lambda i: (i, 0))]
        + [pl.BlockSpec((bm, kb), lambda i: (i, 0)) for _ in dps] + [whole(a) for a in small_grads]
        + [whole(send_out)],
        out_specs=(pl.BlockSpec((N_DEV, SHARD_IN_PAD, d), lambda i: (0, 0, 0)), hbm, hbm),
        out_shape=(jax.ShapeDtypeStruct((N_DEV, SHARD_IN_PAD, d), BF16),
                   jax.ShapeDtypeStruct((N_CHIPS,) + out_slot, BF16),
                   jax.ShapeDtypeStruct((N_CHIPS,) + small_slot, F32)),
        scratch_shapes=[pltpu.VMEM((kb * n_dp, d), F32), pltpu.VMEM((N_DEV,) + small_slot, F32)]
        + _reduce_scratch([(out_slot, BF16), (small_slot, F32)], [False, False]),
        compiler_params=_params("arbitrary"),
    )(u, *dps, *small_grads, send_out)


def _local_step(x, loss_target, u, p, meta_f, norm_g, w_in_p, q_norm_g, w_q_p, kv_norm_g, w_kv_p, conv_w_f,
                attn_out_g, conv_out_g, w_out_f, g_final):
    nb_seq, s, d = x.shape
    tp = s + LANES
    ht = tp // 2
    tables = _rope_tables(tp)

    q, k, v = _qkv_fwd(p, w_q_p, w_kv_p, q_norm_g, kv_norm_g, tables, nb_seq, tp)
    ya, o, lse = _attn_fwd(q, k, v, p, attn_out_g)
    yc = _conv_fwd(p, conv_w_f, conv_out_g, nb_seq, tp)
    dhb, dcat, d_w_out, d_final_g, loss_part = _out_proj_loss(
        ya, yc, w_out_f, x, loss_target, g_final, TOKEN_TILES)
    send_out = d_w_out.reshape(N_DEV, SHARD_OUT, d)
    dq, dk, dv, dz_attn, d_attn_g = _attn_bwd(q, k, v, o, lse, dcat, p, attn_out_g)
    dpa, d_wq_p, d_wkv_p, d_gq, d_gkv = _qkv_bwd(p, dq, dk, dv, w_q_p, w_kv_p, q_norm_g, kv_norm_g, tables)
    d_b, d_c, d_h, dz_conv, d_conv_w, d_conv_g = _conv_bwd(p, dcat, conv_w_f, conv_out_g, nb_seq, tp)
    dps = (dpa, dz_attn, d_b, d_c, d_h, dz_conv)
    small = (d_wq_p, d_wkv_p, d_conv_w, d_final_g, d_gq, d_gkv, d_attn_g, d_conv_g, loss_part)
    send_in, r_out, r_small = _in_proj_bwd_w(u, dps, ht, small, send_out)
    grad_x, d_meta, d_norm_g, r_in = _input_bwd(dps, w_in_p, x, meta_f, dhb, norm_g, TOKEN_TILES, send_in)
    return grad_x, r_in, r_out, r_small, d_meta, d_norm_g


def kernel(x, meta_tokens, norm_g, w_in, q_norm_g, w_q_up, kv_norm_g, w_kv_up, conv_w, attn_out_g, conv_out_g, w_out, final_norm_g, loss_target, m_meta_tokens, m_norm_g, m_w_in, m_q_norm_g, m_w_q_up, m_kv_norm_g, m_w_kv_up, m_conv_w, m_attn_out_g, m_conv_out_g, m_w_out, m_final_norm_g, v_meta_tokens, v_norm_g, v_w_in, v_q_norm_g, v_w_q_up, v_kv_norm_g, v_w_kv_up, v_conv_w, v_attn_out_g, v_conv_out_g, v_w_out, v_final_norm_g):
    d = x.shape[-1]
    ht = (x.shape[1] + LANES) // 2
    u, w_in_p, meta_f = _prep_gather(x, meta_tokens, norm_g, w_in[0].T)
    p, w_q_p, w_kv_p, w_out_f, conv_w_f = _in_proj_gather(
        u, w_in_p, w_q_up[0].T, w_kv_up[0], w_out[0], conv_w[0], ht, 3 * GRP_A)
    g_final = final_norm_g.reshape(1, d)
    grad_x, r_in, r_out, r_small, d_meta, d_norm_g = _local_step(
        x, loss_target, u, p, meta_f, norm_g, w_in_p, q_norm_g, w_q_p, kv_norm_g, w_kv_p, conv_w_f,
        attn_out_g, conv_out_g, w_out_f, g_final)

    flat = lambda a: a.reshape(a.shape[-2:]) if a.ndim == 3 else a.reshape(1, -1) if a.ndim == 1 else a
    transposed = ("w_in", "w_q_up")
    to_kernel = lambda n, a: flat(a).T if n in transposed else flat(a)
    from_kernel = lambda n, a, shape: (a.T if n in transposed else a).reshape(shape)
    params = {
        "meta_tokens": (meta_tokens, m_meta_tokens, v_meta_tokens),
        "norm_g": (norm_g, m_norm_g, v_norm_g),
        "w_in": (w_in, m_w_in, v_w_in),
        "q_norm_g": (q_norm_g, m_q_norm_g, v_q_norm_g),
        "w_q_up": (w_q_up, m_w_q_up, v_w_q_up),
        "kv_norm_g": (kv_norm_g, m_kv_norm_g, v_kv_norm_g),
        "w_kv_up": (w_kv_up, m_w_kv_up, v_w_kv_up),
        "conv_w": (conv_w, m_conv_w, v_conv_w),
        "attn_out_g": (attn_out_g, m_attn_out_g, v_attn_out_g),
        "conv_out_g": (conv_out_g, m_conv_out_g, v_conv_out_g),
        "w_out": (w_out, m_w_out, v_w_out),
        "final_norm_g": (final_norm_g, m_final_norm_g, v_final_norm_g),
    }
    grads, loss = _reduce_tail(r_in, r_out, r_small, d_meta, d_norm_g)
    updated = _adamw(grads, {n: tuple(to_kernel(n, a) for a in t) for n, t in params.items()})
    outs = [[from_kernel(n, updated[n][i], params[n][0].shape) for n, _ in PARAM_SHAPES] for i in range(4)]
    return (loss[0, 0], grad_x, *outs[0], *outs[1], *outs[2], *outs[3])
```

```python
import functools

import jax
import jax.numpy as jnp
from jax import lax
from jax.experimental import pallas as pl
from jax.experimental.pallas import tpu as pltpu

F32 = jnp.float32
BF16 = jnp.bfloat16

N_META = 16
D_MODEL = 1024
N_HEADS = 4
D_NOPE = 128
D_ROPE = 64
D_V = 128
Q_RANK = 256
KV_RANK = 128
CONV_WIDTH = 512
CONV_GROUP = 64
ROPE_THETA = 10000.0
ATTN_SCALE = (D_NOPE + D_ROPE) ** -0.5
Q_SCALE = ATTN_SCALE * 1.4426950408889634
EPS = 1e-6
NEG_INF = -1e30

ADAM_LR = 0.001
ADAM_B1 = 0.9
ADAM_B2 = 0.999
ADAM_EPS = 1e-08
ADAM_WD = 0.01
ADAM_STEP = 10

LANES = 128
PAD_FRONT = LANES - N_META
K_TILE = 256
Q_TILE = 512
N_DEV = 8
VMEM_LIMIT = 56 * 1024 * 1024

IN_PAD = 3072
GRP_A = 512
N_A = Q_RANK + KV_RANK + D_ROPE
IN_PROJ = 3008
SHARD_IN = IN_PROJ // N_DEV
SHARD_IN_PAD = 384
SHARD_Q = 96
SHARD_KV = 128
SHARD_OUT = 128
SHARD_CONV = 64
SHARD_META = 128
Q_COLS = N_HEADS * (D_NOPE + D_ROPE)
KV_COLS = N_HEADS * (D_NOPE + D_V)

ROW_Q, ROW_KV, ROW_META, ROW_CONV = 0, 256, 384, 400
ROW_REPL = 408
ROW_NORM, ROW_FINAL, ROW_GQ, ROW_GKV, ROW_ATTN, ROW_CONVG, ROW_LOSS = 408, 416, 424, 426, 427, 431, 435
SMALL_ROWS = 440

PARAM_SHAPES = (
    ("meta_tokens", (N_META, SHARD_META)), ("norm_g", (1, D_MODEL)), ("w_in", (SHARD_IN, D_MODEL)),
    ("q_norm_g", (1, Q_RANK)), ("w_q_up", (SHARD_Q, Q_RANK)), ("kv_norm_g", (1, KV_RANK)),
    ("w_kv_up", (KV_RANK, SHARD_KV)), ("conv_w", (3, 1, SHARD_CONV)), ("attn_out_g", (1, CONV_WIDTH)),
    ("conv_out_g", (1, CONV_WIDTH)), ("w_out", (SHARD_OUT, D_MODEL)), ("final_norm_g", (1, D_MODEL)),
)


def _in_pieces(k):
    lo, hi = SHARD_IN * k, SHARD_IN * (k + 1)
    out = []
    if lo < N_A:
        out.append((0, min(hi, N_A) - lo, lo))
    if hi > N_A:
        s = max(lo, N_A)
        out.append((s - lo, hi - lo, s + GRP_A - N_A))
    return out


def _q_pieces(k):
    lo, hi = SHARD_Q * k, SHARD_Q * (k + 1)
    out = []
    for h in range(N_HEADS):
        base = (D_NOPE + D_ROPE) * h
        s, e = max(lo, base), min(hi, base + D_NOPE)
        if s < e:
            out.append((s - lo, e - lo, D_NOPE * h + s - base))
        s, e = max(lo, base + D_NOPE), min(hi, base + D_NOPE + D_ROPE)
        if s < e:
            out.append((s - lo, e - lo, N_HEADS * D_NOPE + D_ROPE * h + s - base - D_NOPE))
    return out


def _kv_dst(k):
    return D_NOPE * (k // 2) + (N_HEADS * D_NOPE if k % 2 else 0)


def _params(*sem):
    return pltpu.CompilerParams(dimension_semantics=sem, vmem_limit_bytes=VMEM_LIMIT)


def _rms_stats(x):
    r = lax.rsqrt(jnp.mean(x * x, axis=-1, keepdims=True) + EPS)
    return x * r, r


def _rms_bwd(gdy, xhat, r):
    return r * (gdy - xhat * jnp.mean(gdy * xhat, axis=-1, keepdims=True))


def _sigmoid(z):
    return 1.0 / (1.0 + jnp.exp(-z))


def _group_mean(x):
    i0 = lax.broadcasted_iota(jnp.int32, (LANES, LANES), 0) // CONV_GROUP
    i1 = lax.broadcasted_iota(jnp.int32, (LANES, LANES), 1) // CONV_GROUP
    m = jnp.where(i0 == i1, 1.0 / CONV_GROUP, 0.0).astype(BF16)
    hi = x.astype(BF16)
    lo = (x - hi.astype(F32)).astype(BF16)
    return jnp.dot(hi, m, preferred_element_type=F32) + jnp.dot(lo, m, preferred_element_type=F32)


_NT = (((1,), (1,)), ((), ()))
_TN = (((0,), (0,)), ((), ()))


def _dot(a, b, dims=None):
    if dims is None:
        return jnp.dot(a, b, preferred_element_type=F32)
    return lax.dot_general(a, b, dims, preferred_element_type=F32)


def _device_position():
    x, y, c = lax.axis_index("x"), lax.axis_index("y"), lax.axis_index("c")
    return x, y, c, 4 * x + 2 * y + c


def _gather_plan(srcs, slots, send_sems, recv_sems, local_sems):
    x, y, c, _ = _device_position()
    me, sibling = (x, y, c), (x, y, 1 - c)
    flip = lambda v, on: v + on - 2 * v * on
    near = (flip(x, 1 - c), flip(y, c))
    far = (flip(x, c), flip(y, 1 - c))
    diag = (1 - x, 1 - y)
    n = len(srcs)

    def slot(a, px, py, pc):
        return slots[a].at[4 * px + 2 * py + pc]

    def copy(a, k, block, to, own=False):
        return pltpu.make_async_remote_copy(
            src_ref=srcs[a] if own else slot(a, *block),
            dst_ref=slot(a, *block),
            send_sem=send_sems.at[7 * a + k],
            recv_sem=recv_sems.at[7 * a + k],
            device_id=to,
            device_id_type=pl.DeviceIdType.MESH,
        )

    def local(a):
        return pltpu.make_async_copy(srcs[a], slot(a, *me), local_sems.at[a])

    sent = [(me, sibling), (me, (*near, c)), (me, (*far, c)), ((*near, c), (*far, c)),
            ((*near, c), sibling), ((*far, c), sibling), ((*diag, c), sibling)]
    landed = [sibling, (*near, c), (*far, c), (*diag, c), (*far, 1 - c), (*near, 1 - c), (*diag, 1 - c)]

    def send(a, k):
        return copy(a, k, *sent[k], own=k < 3)

    def arrival(a, k):
        return copy(a, k, landed[k], me)

    def start():
        for a in range(n):
            local(a).start()
            for k in range(3):
                send(a, k).start()

    def relay():
        for a in range(n):
            arrival(a, 1).wait_recv()
            send(a, 3).start()
            send(a, 4).start()

    def forward():
        for k in (2, 3):
            for a in range(n):
                arrival(a, k).wait_recv()
                send(a, k + 3).start()

    def finish():
        for a in range(n):
            for k in (0, 4, 5, 6):
                arrival(a, k).wait_recv()
        for a in range(n):
            for k in range(7):
                send(a, k).wait_send()
            local(a).wait()

    return start, relay, forward, finish


def _adam_update(g, w, m, v):
    m_new = ADAM_B1 * m + (1.0 - ADAM_B1) * g
    v_new = ADAM_B2 * v + (1.0 - ADAM_B2) * (g * g)
    m_hat = m_new / (1.0 - ADAM_B1 ** ADAM_STEP)
    v_hat = v_new / (1.0 - ADAM_B2 ** ADAM_STEP)
    return -ADAM_LR * (m_hat / (jnp.sqrt(v_hat) + ADAM_EPS) + ADAM_WD * w), m_new, v_new


def _adamw(grads, params):
    names = [n for n, _ in PARAM_SHAPES]
    n_p = len(names)

    def body(*refs):
        for i in range(n_p):
            g = refs[i][...]
            w, m, v = (refs[n_p + 3 * i + j][...] for j in range(3))
            delta, m_new, v_new = _adam_update(g, w, m, v)
            for j, val in enumerate((g, delta, m_new, v_new)):
                refs[4 * n_p + 4 * i + j][...] = val

    vm = pl.BlockSpec(memory_space=pltpu.VMEM)
    out_shape = []
    for _, shape in PARAM_SHAPES:
        out_shape += [jax.ShapeDtypeStruct(shape, F32)] * 4
    outs = pl.pallas_call(
        body,
        name="adamw",
        out_shape=tuple(out_shape),
        in_specs=[vm] * (4 * n_p),
        out_specs=(vm,) * (4 * n_p),
        compiler_params=pltpu.CompilerParams(vmem_limit_bytes=VMEM_LIMIT),
    )(*[grads[n] for n in names], *[a for n in names for a in params[n]])
    return {n: outs[4 * i:4 * i + 4] for i, n in enumerate(names)}


N_CHIPS = 4


def _reduce_plan(pays, owns, r1s, sums, r2s, send1, recv1, send2, recv2, local_sems):
    x, y, c, _ = _device_position()
    sibling = (x, y, 1 - c)
    chips = [((1 - x if rj & 2 else x), (1 - y if rj & 1 else y)) for rj in range(N_CHIPS)]
    n = len(pays)

    def slot_of(rj, core):
        return 4 * chips[rj][0] + 2 * chips[rj][1] + core

    def to_sibling(a, rj):
        return pltpu.make_async_remote_copy(
            src_ref=pays[a].at[slot_of(rj, 1 - c)], dst_ref=r1s[a].at[rj],
            send_sem=send1.at[N_CHIPS * a + rj], recv_sem=recv1.at[N_CHIPS * a + rj],
            device_id=sibling, device_id_type=pl.DeviceIdType.MESH)

    def load_own(a, rj):
        return pltpu.make_async_copy(pays[a].at[slot_of(rj, c)], owns[a].at[rj], local_sems.at[2 * N_CHIPS * a + rj])

    def to_chip(a, rj):
        return pltpu.make_async_remote_copy(
            src_ref=sums[a].at[rj], dst_ref=r2s[a].at[rj],
            send_sem=send2.at[N_CHIPS * a + rj], recv_sem=recv2.at[N_CHIPS * a + rj],
            device_id=(*chips[rj], c), device_id_type=pl.DeviceIdType.MESH)

    def keep(a):
        return pltpu.make_async_copy(sums[a].at[0], r2s[a].at[0], local_sems.at[2 * N_CHIPS * a + N_CHIPS])

    def start():
        for a in range(n):
            for rj in range(N_CHIPS):
                to_sibling(a, rj).start()
                if owns[a] is not None:
                    load_own(a, rj).start()

    def combine():
        for a in range(n):
            for rj in range(N_CHIPS):
                to_sibling(a, rj).wait_recv()
                if owns[a] is not None:
                    load_own(a, rj).wait()
                    mine = owns[a][rj]
                else:
                    mine = pays[a][slot_of(rj, c)]
                sums[a][rj] = (mine.astype(F32) + r1s[a][rj].astype(F32)).astype(sums[a].dtype)
            keep(a).start()
            for rj in range(1, N_CHIPS):
                to_chip(a, rj).start()

    def finish():
        for a in range(n):
            for rj in range(1, N_CHIPS):
                to_chip(a, rj).wait_recv()
            for rj in range(N_CHIPS):
                to_sibling(a, rj).wait_send()
            for rj in range(1, N_CHIPS):
                to_chip(a, rj).wait_send()
            keep(a).wait()

    return start, combine, finish


def _reduce_scratch(shapes_dtypes, own_flags):
    out = []
    for (shape, dtype), own in zip(shapes_dtypes, own_flags):
        if own:
            out.append(pltpu.VMEM((N_CHIPS,) + shape, dtype))
        out += [pltpu.VMEM((N_CHIPS,) + shape, dtype), pltpu.VMEM((N_CHIPS,) + shape, dtype)]
    n = len(shapes_dtypes)
    out += [pltpu.SemaphoreType.DMA((N_CHIPS * n,))] * 4 + [pltpu.SemaphoreType.DMA((2 * N_CHIPS * n,))]
    return out


def _pack_small(ssmall, dwq, dwkv, dconv, dfinal, dgq, dgkv, dattn, dconvg, loss_part):
    ssmall[...] = jnp.zeros_like(ssmall)
    rep = ssmall.at[0]
    for i in range(D_MODEL // LANES):
        rep[ROW_FINAL + i:ROW_FINAL + i + 1, :] = dfinal[:, LANES * i:LANES * (i + 1)]
    for i in range(Q_RANK // LANES):
        rep[ROW_GQ + i:ROW_GQ + i + 1, :] = dgq[:, LANES * i:LANES * (i + 1)]
    rep[ROW_GKV:ROW_GKV + 1, :] = dgkv[...]
    for i in range(CONV_WIDTH // LANES):
        rep[ROW_ATTN + i:ROW_ATTN + i + 1, :] = dattn[:, LANES * i:LANES * (i + 1)]
        rep[ROW_CONVG + i:ROW_CONVG + i + 1, :] = dconvg[:, LANES * i:LANES * (i + 1)]
    rep[ROW_LOSS:ROW_LOSS + 1, :] = loss_part[...]
    for k in range(N_DEV):
        if k:
            ssmall[k, ROW_REPL:, :] = ssmall[0, ROW_REPL:, :]
        for s, e, d in _q_pieces(k):
            for i in range(Q_RANK // LANES):
                ssmall[k, ROW_Q + SHARD_Q * i + s:ROW_Q + SHARD_Q * i + e, :] = dwq[d:d + e - s, LANES * i:LANES * (i + 1)]
        ssmall[k, ROW_KV:ROW_KV + KV_RANK, :] = dwkv[:, _kv_dst(k):_kv_dst(k) + SHARD_KV]
        ssmall[k, ROW_CONV:ROW_CONV + 3, 0:SHARD_CONV] = dconv[0:3, SHARD_CONV * k:SHARD_CONV * (k + 1)]


TOKEN_TILES = 4
TAIL_ROWS = N_META + D_MODEL // LANES


def _reduce_tail(r_in, r_out, r_small, d_meta, d_norm):
    n_p = len(PARAM_SHAPES)
    names = [n for n, _ in PARAM_SHAPES]

    def body(*refs):
        rin, rout, rsmall, dmeta, dnorm = refs[:5]
        g_out = {n: refs[5 + i] for i, n in enumerate(names)}
        loss_out = refs[5 + n_p]
        stail, rtail, gsum, gtail, send_sems, recv_sems = refs[6 + n_p:]
        x, y, c, me = _device_position()
        my_chip = 2 * x + y

        for k in range(N_DEV):
            stail[k, 0:N_META, :] = dmeta[:, SHARD_META * k:SHARD_META * (k + 1)]
            for i in range(D_MODEL // LANES):
                stail[k, N_META + i:N_META + i + 1, :] = dnorm[:, LANES * i:LANES * (i + 1)]
        copies = []
        for r in range(1, N_DEV):
            peer = (1 - x if r & 4 else x, 1 - y if r & 2 else y, 1 - c if r & 1 else c)
            copies.append(pltpu.make_async_remote_copy(
                src_ref=stail.at[4 * peer[0] + 2 * peer[1] + peer[2]],
                dst_ref=rtail.at[r],
                send_sem=send_sems.at[r - 1],
                recv_sem=recv_sems.at[r - 1],
                device_id=peer,
                device_id_type=pl.DeviceIdType.MESH,
            ))
        for cp in copies:
            cp.start()
        rtail[0] = stail[me]

        g = rin[my_chip].astype(F32)
        for ch in range(1, N_CHIPS):
            g = g + rin[ch ^ my_chip].astype(F32)
        g_out["w_in"][...] = g[:SHARD_IN, :]

        g = rout[my_chip].astype(F32)
        gs = rsmall[my_chip]
        for ch in range(1, N_CHIPS):
            g = g + rout[ch ^ my_chip].astype(F32)
            gs = gs + rsmall[ch ^ my_chip]
        g_out["w_out"][...] = g
        gsum[...] = gs
        for i in range(Q_RANK // LANES):
            g_out["w_q_up"][:, LANES * i:LANES * (i + 1)] = gsum[ROW_Q + SHARD_Q * i:ROW_Q + SHARD_Q * (i + 1), :]
        g_out["w_kv_up"][...] = gsum[ROW_KV:ROW_KV + KV_RANK, :]
        for i in range(3):
            g_out["conv_w"][i] = gsum[ROW_CONV + i:ROW_CONV + i + 1, 0:SHARD_CONV]
        for name, row, width in (("final_norm_g", ROW_FINAL, D_MODEL), ("q_norm_g", ROW_GQ, Q_RANK),
                                 ("kv_norm_g", ROW_GKV, KV_RANK), ("attn_out_g", ROW_ATTN, CONV_WIDTH),
                                 ("conv_out_g", ROW_CONVG, CONV_WIDTH)):
            for i in range(width // LANES):
                g_out[name][:, LANES * i:LANES * (i + 1)] = gsum[row + i:row + i + 1, :]
        loss_out[...] = gsum[ROW_LOSS:ROW_LOSS + 1, :]

        for cp in copies:
            cp.wait_recv()
        gt = rtail[me]
        for d in range(1, N_DEV):
            gt = gt + rtail[d ^ me]
        gtail[...] = gt
        g_out["meta_tokens"][...] = gtail[0:N_META, :]
        for i in range(D_MODEL // LANES):
            g_out["norm_g"][:, LANES * i:LANES * (i + 1)] = gtail[N_META + i:N_META + i + 1, :]
        for cp in copies:
            cp.wait_send()

    vm = pl.BlockSpec(memory_space=pltpu.VMEM)
    out_shape = [jax.ShapeDtypeStruct(shape, F32) for _, shape in PARAM_SHAPES]
    out_shape.append(jax.ShapeDtypeStruct((1, LANES), F32))
    outs = pl.pallas_call(
        body,
        name="reduce_tail",
        out_shape=tuple(out_shape),
        in_specs=[vm] * 5,
        out_specs=(vm,) * len(out_shape),
        scratch_shapes=[
            pltpu.VMEM((N_DEV, TAIL_ROWS, LANES), F32),
            pltpu.VMEM((N_DEV, TAIL_ROWS, LANES), F32),
            pltpu.VMEM((SMALL_ROWS, LANES), F32),
            pltpu.VMEM((TAIL_ROWS, LANES), F32),
            pltpu.SemaphoreType.DMA((N_DEV - 1,)),
            pltpu.SemaphoreType.DMA((N_DEV - 1,)),
        ],
        compiler_params=pltpu.CompilerParams(vmem_limit_bytes=VMEM_LIMIT),
    )(r_in, r_out, r_small, d_meta, d_norm)
    return {n: outs[i] for i, n in enumerate(names)}, outs[-1]


def _prep_gather(x, meta, norm_g, w_in_t):
    nb_seq, s, d = x.shape
    nb = s // LANES + 1
    relay_step = nb_seq * (nb - 1) // 2
    forward_step = nb_seq * (nb - 1) - 1
    finish_step = nb_seq * (nb - 1)

    def body(x_ref, meta_ref, g_ref, win_ref, u_ref, w_in_p, meta_f,
             sbig, ssmall, gbig, gsmall, send_sems, recv_sems, local_sems):
        jj, b = pl.program_id(0), pl.program_id(1)
        t = jj * nb_seq + b

        def plan():
            return _gather_plan((sbig, ssmall), (gbig, gsmall), send_sems, recv_sems, local_sems)

        @pl.when(t == 0)
        def _():
            sbig[0:SHARD_IN, :] = win_ref[...].astype(BF16)
            sbig[SHARD_IN:, :] = jnp.zeros((SHARD_IN_PAD - SHARD_IN, d), BF16)
            ssmall[...] = meta_ref[...]
            plan()[0]()

        @pl.when(t == relay_step)
        def _():
            plan()[1]()

        @pl.when(t == forward_step)
        def _():
            plan()[2]()

        @pl.when(t == finish_step)
        def _():
            plan()[3]()
            w_in_p[N_A:GRP_A, :] = jnp.zeros((GRP_A - N_A, d), BF16)
            for k in range(N_DEV):
                for s0, e0, d0 in _in_pieces(k):
                    w_in_p[d0:d0 + e0 - s0, :] = gbig[k, s0:e0, :]
                meta_f[:, SHARD_META * k:SHARD_META * (k + 1)] = gsmall[k]

        def norm(h):
            hhat, _ = _rms_stats(h)
            return (hhat * g_ref[...]).astype(BF16)

        @pl.when(jj < nb - 1)
        def _():
            u_ref[...] = norm(x_ref[0])

        @pl.when(jj == nb - 1)
        def _():
            u_ref[0:PAD_FRONT, :] = jnp.zeros((PAD_FRONT, d), BF16)
            u_ref[PAD_FRONT:LANES, :] = norm(meta_f[...])

    whole = lambda shape: pl.BlockSpec(shape, lambda jj, b: (0,) * len(shape))
    return pl.pallas_call(
        body,
        name="prep_norm_gather",
        grid=(nb, nb_seq),
        in_specs=[
            pl.BlockSpec((1, LANES, d), lambda jj, b: (b, jnp.minimum(jj, nb - 2), 0)),
            whole(meta.shape), whole(norm_g.shape), whole(w_in_t.shape),
        ],
        out_specs=(pl.BlockSpec((LANES, d), lambda jj, b: (b * nb + (jj + 1) % nb, 0)),
                   whole((IN_PAD, d)), whole((N_META, d))),
        out_shape=(jax.ShapeDtypeStruct((nb_seq * nb * LANES, d), BF16),
                   jax.ShapeDtypeStruct((IN_PAD, d), BF16),
                   jax.ShapeDtypeStruct((N_META, d), F32)),
        scratch_shapes=[
            pltpu.VMEM((SHARD_IN_PAD, d), BF16),
            pltpu.VMEM((N_META, SHARD_META), F32),
            pltpu.VMEM((N_DEV, SHARD_IN_PAD, d), BF16),
            pltpu.VMEM((N_DEV, N_META, SHARD_META), F32),
            pltpu.SemaphoreType.DMA((14,)),
            pltpu.SemaphoreType.DMA((14,)),
            pltpu.SemaphoreType.DMA((2,)),
        ],
        compiler_params=_params("arbitrary", "arbitrary"),
    )(x, meta, norm_g, w_in_t)


def _in_proj_gather(u, w_in_p, w_q, w_kv, w_out, conv_w, bm, bn):
    m, k_dim = u.shape
    n = w_in_p.shape[0]
    steps = (m // bm) * (n // bn)
    relay_step, forward_step = steps // 3, 2 * steps // 3
    qkv_shape = (SHARD_Q + KV_RANK, Q_RANK)

    def body(a_ref, b_ref, wq_ref, wkv_ref, wout_ref, conv_ref, o_ref, w_q_p, w_kv_p, w_out_f, conv_f,
             sqkv, sout, sconv, gqkv, gout, gconv, send_sems, recv_sems, local_sems):
        t = pl.program_id(0) * (n // bn) + pl.program_id(1)

        def plan():
            return _gather_plan((sqkv, sout, sconv), (gqkv, gout, gconv), send_sems, recv_sems, local_sems)

        @pl.when(t == 0)
        def _():
            sqkv[...] = jnp.zeros_like(sqkv)
            sqkv[0:SHARD_Q, :] = wq_ref[...].astype(BF16)
            sqkv[SHARD_Q:, 0:SHARD_KV] = wkv_ref[...].astype(BF16)
            sout[...] = wout_ref[...].astype(BF16)
            sconv[...] = jnp.zeros_like(sconv)
            for i in range(3):
                sconv[i:i + 1, 0:SHARD_CONV] = conv_ref[i]
            plan()[0]()

        @pl.when(t == relay_step)
        def _():
            plan()[1]()

        @pl.when(t == forward_step)
        def _():
            plan()[2]()

        o_ref[...] = _dot(a_ref[...], b_ref[...], _NT).astype(o_ref.dtype)

        @pl.when(t == steps - 1)
        def _():
            plan()[3]()
            conv_f[...] = jnp.zeros_like(conv_f)
            for k in range(N_DEV):
                for s0, e0, d0 in _q_pieces(k):
                    w_q_p[d0:d0 + e0 - s0, :] = gqkv[k, s0:e0, :]
                w_kv_p[:, _kv_dst(k):_kv_dst(k) + SHARD_KV] = gqkv[k, SHARD_Q:, 0:SHARD_KV]
                w_out_f[SHARD_OUT * k:SHARD_OUT * (k + 1), :] = gout[k]
                conv_f[0:3, SHARD_CONV * k:SHARD_CONV * (k + 1)] = gconv[k, 0:3, 0:SHARD_CONV]

    whole = lambda shape: pl.BlockSpec(shape, lambda i, j: (0,) * len(shape))
    return pl.pallas_call(
        body,
        name="in_proj_gather",
        grid=(m // bm, n // bn),
        in_specs=[pl.BlockSpec((bm, k_dim), lambda i, j: (i, 0)), pl.BlockSpec((bn, k_dim), lambda i, j: (j, 0)),
                  whole(w_q.shape), whole(w_kv.shape), whole(w_out.shape), whole(conv_w.shape)],
        out_specs=(pl.BlockSpec((bm, bn), lambda i, j: (i, j)),
                   whole((Q_COLS, Q_RANK)), whole((KV_RANK, KV_COLS)), whole((D_MODEL, D_MODEL)),
                   whole((8, CONV_WIDTH))),
        out_shape=(jax.ShapeDtypeStruct((m, n), BF16),
                   jax.ShapeDtypeStruct((Q_COLS, Q_RANK), BF16),
                   jax.ShapeDtypeStruct((KV_RANK, KV_COLS), BF16),
                   jax.ShapeDtypeStruct((D_MODEL, D_MODEL), BF16),
                   jax.ShapeDtypeStruct((8, CONV_WIDTH), F32)),
        scratch_shapes=[
            pltpu.VMEM(qkv_shape, BF16),
            pltpu.VMEM((SHARD_OUT, D_MODEL), BF16),
            pltpu.VMEM((8, LANES), F32),
            pltpu.VMEM((N_DEV,) + qkv_shape, BF16),
            pltpu.VMEM((N_DEV, SHARD_OUT, D_MODEL), BF16),
            pltpu.VMEM((N_DEV, 8, LANES), F32),
            pltpu.SemaphoreType.DMA((21,)),
            pltpu.SemaphoreType.DMA((21,)),
            pltpu.SemaphoreType.DMA((3,)),
        ],
        compiler_params=_params("arbitrary", "arbitrary"),
    )(u, w_in_p, w_q, w_kv, w_out, conv_w)


def _rope_tables(tp):
    half = D_ROPE // 2
    inv_freq = 1.0 / (ROPE_THETA ** (jnp.arange(half, dtype=F32) / half))
    pos = (jnp.arange(tp) - PAD_FRONT).astype(F32)
    ang = pos[:, None] * inv_freq[None, :]
    cos = jnp.tile(jnp.cos(ang), (1, LANES // half))
    sin = jnp.tile(jnp.sin(ang), (1, LANES // half))
    first = (jnp.arange(LANES) % D_ROPE) < half
    return cos, jnp.where(first, -sin, 0.0), jnp.where(first, 0.0, sin)


def _rope(t, cos, sa, sb):
    return t * cos + pltpu.roll(t, LANES - D_ROPE // 2, 1) * sa + pltpu.roll(t, D_ROPE // 2, 1) * sb


def _rope_t(t, cos, sa, sb):
    return t * cos + pltpu.roll(t * sa, D_ROPE // 2, 1) + pltpu.roll(t * sb, LANES - D_ROPE // 2, 1)


def _qkv_fwd(p, wq, wkv, gq, gkv, tables, nb_seq, tp):
    ht = tp // 2

    def body(pa_ref, wq_ref, wkv_ref, gq_ref, gkv_ref, cos_ref, sa_ref, sb_ref, q_ref, k_ref, v_ref):
        pa = pa_ref[...].astype(F32)
        cq_hat, _ = _rms_stats(pa[:, :Q_RANK])
        ckv_hat, _ = _rms_stats(pa[:, Q_RANK:Q_RANK + KV_RANK])
        q = _dot((cq_hat * gq_ref[...]).astype(BF16), wq_ref[...], _NT) * Q_SCALE
        kv = _dot((ckv_hat * gkv_ref[...]).astype(BF16), wkv_ref[...])
        tabs = (cos_ref[...], sa_ref[...], sb_ref[...])
        lane = lax.broadcasted_iota(jnp.int32, (ht, LANES), 1)
        low = lane < D_ROPE
        mark = lane == D_ROPE
        row = (pl.program_id(0) % 2) * ht + lax.broadcasted_iota(jnp.int32, (ht, LANES), 0)
        k_pe = jnp.where(mark & (row < PAD_FRONT), NEG_INF, _rope(pa[:, Q_RANK + KV_RANK:], *tabs))
        one = jnp.where(mark & (row >= PAD_FRONT), 1.0, 0.0)
        pairs = [_rope(q[:, N_HEADS * D_NOPE + LANES * i:N_HEADS * D_NOPE + LANES * (i + 1)], *tabs) for i in range(2)]
        for h in range(N_HEADS):
            pair = pairs[h // 2]
            if h % 2:
                pair = pltpu.roll(pair, D_ROPE, 1)
            pe = jnp.where(low, pair, one)
            q_ref[0, h] = jnp.concatenate([q[:, D_NOPE * h:D_NOPE * (h + 1)], pe], axis=1).astype(BF16)
            k_ref[0, h] = jnp.concatenate([kv[:, D_NOPE * h:D_NOPE * (h + 1)], k_pe], axis=1).astype(BF16)
            v_ref[0, h] = kv[:, N_HEADS * D_NOPE + D_V * h:N_HEADS * D_NOPE + D_V * (h + 1)].astype(BF16)

    full = lambda a: pl.BlockSpec(a.shape, lambda i: (0,) * a.ndim)
    tab = pl.BlockSpec((ht, LANES), lambda i: (i % 2, 0))
    qk = pl.BlockSpec((1, N_HEADS, ht, 2 * LANES), lambda i: (i // 2, 0, i % 2, 0))
    return pl.pallas_call(
        body,
        name="qkv_fwd",
        grid=(2 * nb_seq,),
        in_specs=[pl.BlockSpec((ht, GRP_A), lambda i: (i, 0)), full(wq), full(wkv), full(gq), full(gkv), tab, tab, tab],
        out_specs=(qk, qk, pl.BlockSpec((1, N_HEADS, ht, D_V), lambda i: (i // 2, 0, i % 2, 0))),
        out_shape=(
            jax.ShapeDtypeStruct((nb_seq, N_HEADS, tp, 2 * LANES), BF16),
            jax.ShapeDtypeStruct((nb_seq, N_HEADS, tp, 2 * LANES), BF16),
            jax.ShapeDtypeStruct((nb_seq, N_HEADS, tp, D_V), BF16),
        ),
        compiler_params=_params("parallel"),
    )(p, wq, wkv, gq, gkv, *tables)


def _attn_fwd(q, k, v, p, g_attn):
    nb_seq, _, tp, _ = q.shape

    def body(q_ref, k_ref, v_ref, z_ref, g_ref, y_ref, o_ref, lse_ref):
        g = g_ref[...]
        for r0 in range(0, tp, Q_TILE):
            nq = min(Q_TILE, tp - r0)
            kend = r0 + nq
            qq = q_ref[0, 0, r0:kend, :]
            sd = _dot(qq, k_ref[0, 0, r0:kend, :], _NT)
            causal = (lax.broadcasted_iota(jnp.int32, (nq, nq), 1) <= lax.broadcasted_iota(jnp.int32, (nq, nq), 0))
            sd = jnp.where(causal, sd, NEG_INF)
            m = jnp.max(sd, axis=-1, keepdims=True)
            if r0:
                so = _dot(qq, k_ref[0, 0, 0:r0, :], _NT)
                m = jnp.maximum(m, jnp.max(so, axis=-1, keepdims=True))
            ed = jnp.exp2(sd - m)
            l = jnp.sum(ed, axis=-1, keepdims=True)
            o = _dot(ed.astype(BF16), v_ref[0, 0, r0:kend, :])
            if r0:
                eo = jnp.exp2(so - m)
                l = l + jnp.sum(eo, axis=-1, keepdims=True)
                o = o + _dot(eo.astype(BF16), v_ref[0, 0, 0:r0, :])
            o = o * (1.0 / l)
            o_ref[0, 0, r0:kend, :] = o
            lse_ref[0, 0, r0:kend, :] = jnp.broadcast_to(m + jnp.log2(l), (nq, LANES))
            ohat, _ = _rms_stats(o)
            z = z_ref[r0:kend, :].astype(F32)
            y_ref[r0:kend, :] = (ohat * g * (z * _sigmoid(z))).astype(BF16)

    qk = pl.BlockSpec((1, 1, tp, 2 * LANES), lambda b, h: (b, h, 0, 0))
    hv = pl.BlockSpec((1, 1, tp, D_V), lambda b, h: (b, h, 0, 0))
    return pl.pallas_call(
        body,
        name="attn_fwd",
        grid=(nb_seq, N_HEADS),
        in_specs=[qk, qk, hv,
                  pl.BlockSpec((tp, LANES), lambda b, h: (b, GRP_A // LANES + h)),
                  pl.BlockSpec((1, LANES), lambda b, h: (0, h))],
        out_specs=(pl.BlockSpec((tp, LANES), lambda b, h: (b, h)), hv, hv),
        out_shape=(
            jax.ShapeDtypeStruct((nb_seq * tp, N_HEADS * D_V), BF16),
            jax.ShapeDtypeStruct((nb_seq, N_HEADS, tp, D_V), F32),
            jax.ShapeDtypeStruct((nb_seq, N_HEADS, tp, LANES), F32),
        ),
        compiler_params=_params("parallel", "parallel"),
    )(q, k, v, p, g_attn)


_CONV_COL0 = (GRP_A + N_HEADS * D_V) // LANES


def _conv_specs(tp, order):
    cols = CONV_WIDTH // LANES
    return [pl.BlockSpec((tp, LANES), functools.partial(
        lambda a, b, off: order(a, b, off), off=_CONV_COL0 + i * cols)) for i in range(4)]


def _conv_fwd(p, conv_w, g_conv, nb_seq, tp):
    def body(b_ref, c_ref, h_ref, z_ref, w_ref, g_ref, y_ref):
        cc = c_ref[...].astype(F32) * h_ref[...].astype(F32)
        row = lax.broadcasted_iota(jnp.int32, (tp, LANES), 0)
        s1 = jnp.where(row >= 1, pltpu.roll(cc, 1, 0), 0.0)
        s2 = jnp.where(row >= 2, pltpu.roll(cc, 2, 0), 0.0)
        yc = b_ref[...].astype(F32) * (w_ref[0:1, :] * s2 + w_ref[1:2, :] * s1 + w_ref[2:3, :] * cc)
        r = lax.rsqrt(_group_mean(yc * yc) + EPS)
        z = z_ref[...].astype(F32)
        y_ref[...] = (yc * r * g_ref[...] * (z * _sigmoid(z))).astype(BF16)

    return pl.pallas_call(
        body,
        name="conv_fwd",
        grid=(nb_seq, CONV_WIDTH // LANES),
        in_specs=_conv_specs(tp, lambda b, t, off: (b, off + t)) + [
            pl.BlockSpec((8, LANES), lambda b, t: (0, t)),
            pl.BlockSpec((1, LANES), lambda b, t: (0, t))],
        out_specs=pl.BlockSpec((tp, LANES), lambda b, t: (b, t)),
        out_shape=jax.ShapeDtypeStruct((nb_seq * tp, CONV_WIDTH), BF16),
        compiler_params=_params("parallel", "parallel"),
    )(p, p, p, p, conv_w, g_conv)


def _token_copy(hbm, b, k, ts, buf, sem, to_hbm=False):
    lo, hi = max(k * ts - LANES, 0), (k + 1) * ts - LANES
    off = lo - (k * ts - LANES)
    src, dst = hbm.at[b, pl.ds(lo, hi - lo)], buf.at[pl.ds(off, hi - lo)]
    if to_hbm:
        src, dst = dst, src
    return pltpu.make_async_copy(src, dst, sem)


def _for_tile(k, nt, fn):
    for kk in range(nt):
        @pl.when(k == kk)
        def _(kk=kk):
            fn(kk)


def _out_proj_loss(ya, yc, w_out, x, target, g_final, nt):
    nb_seq, s, d = x.shape
    r, ka = ya.shape
    n_mix = w_out.shape[0]
    ts = (s + LANES) // nt
    steps = nb_seq * nt

    def body(a_ref, c_ref, w_ref, x_hbm, t_hbm, g_ref, dhb_ref, dcat_ref, dw_ref, dg_ref, loss_ref,
             xbuf, tbuf, acc_ref, dw_acc, sems):
        i = pl.program_id(0)
        b, k = i // nt, i % nt

        @pl.when(i == 0)
        def _():
            acc_ref[...] = jnp.zeros_like(acc_ref)
            dg_ref[...] = jnp.zeros_like(dg_ref)
            dw_acc[...] = jnp.zeros_like(dw_acc)

        slot = i % 2

        def fetch(seq, kk, sl):
            return [_token_copy(x_hbm, seq, kk, ts, xbuf.at[sl], sems.at[sl, 0]),
                    _token_copy(t_hbm, seq, kk, ts, tbuf.at[sl], sems.at[sl, 1])]

        def start(seq, sl, kk):
            if kk == 0:
                xbuf[sl, 0:LANES, :] = jnp.zeros((LANES, d), F32)
                tbuf[sl, 0:LANES, :] = jnp.zeros((LANES, d), F32)
            for cp in fetch(seq, kk, sl):
                cp.start()

        @pl.when(i == 0)
        def _():
            start(0, 0, 0)

        @pl.when(i + 1 < steps)
        def _():
            _for_tile((i + 1) % nt, nt, functools.partial(start, (i + 1) // nt, 1 - slot))

        mix = _dot(a_ref[...], w_ref[0:ka, :]) + _dot(c_ref[...], w_ref[ka:, :])
        _for_tile(k, nt, lambda kk: [cp.wait() for cp in fetch(b, kk, slot)])

        real = (lax.broadcasted_iota(jnp.int32, (ts, d), 0) >= LANES) | (k > 0)
        g = g_ref[...]
        hhat, rstd = _rms_stats(xbuf[slot] + mix)
        e = jnp.where(real, hhat * g - tbuf[slot], 0.0)
        acc_ref[...] += jnp.sum(e * e, axis=0, keepdims=True)
        dy = e * (1.0 / d)
        dg_ref[...] += jnp.sum(dy * hhat, axis=0, keepdims=True)
        dhb = _rms_bwd(g * dy, hhat, rstd).astype(BF16)
        dhb_ref[...] = dhb
        dcat_ref[...] = _dot(dhb, w_ref[...], _NT).astype(BF16)
        dw_acc[0:ka, :] += _dot(a_ref[...], dhb, _TN)
        dw_acc[ka:, :] += _dot(c_ref[...], dhb, _TN)

        @pl.when(i == steps - 1)
        def _():
            total = jnp.sum(acc_ref[...], axis=1, keepdims=True)
            loss_ref[...] = jnp.broadcast_to((0.5 / d) * total, loss_ref.shape)
            dw_ref[...] = dw_acc[...].astype(BF16)

    hbm = pl.BlockSpec(memory_space=pl.ANY)
    row = pl.BlockSpec((ts, d), lambda i: (i, 0))
    vec = pl.BlockSpec((1, d), lambda i: (0, 0))
    return pl.pallas_call(
        body,
        name="out_proj_loss",
        grid=(steps,),
        in_specs=[pl.BlockSpec((ts, ka), lambda i: (i, 0)), pl.BlockSpec((ts, yc.shape[1]), lambda i: (i, 0)),
                  pl.BlockSpec(w_out.shape, lambda i: (0, 0)), hbm, hbm, vec],
        out_specs=(row, pl.BlockSpec((ts, n_mix), lambda i: (i, 0)), pl.BlockSpec((n_mix, d), lambda i: (0, 0)),
                   vec, pl.BlockSpec((1, LANES), lambda i: (0, 0))),
        out_shape=(
            jax.ShapeDtypeStruct((r, d), BF16),
            jax.ShapeDtypeStruct((r, n_mix), BF16),
            jax.ShapeDtypeStruct((n_mix, d), BF16),
            jax.ShapeDtypeStruct((1, d), F32),
            jax.ShapeDtypeStruct((1, LANES), F32),
        ),
        scratch_shapes=[pltpu.VMEM((2, ts, d), F32), pltpu.VMEM((2, ts, d), F32), pltpu.VMEM((1, d), F32),
                        pltpu.VMEM((n_mix, d), F32), pltpu.SemaphoreType.DMA((2, 2))],
        compiler_params=_params("arbitrary"),
    )(ya, yc, w_out, x, target, g_final)


def _attn_bwd(q, k, v, o, lse, dcat, p, g_attn):
    nb_seq, _, tp, _ = q.shape

    def body(q_ref, k_ref, v_ref, o_ref, lse_ref, dy_ref, z_ref, g_ref,
             dq_ref, dk_ref, dv_ref, dz_ref, dg_ref, dq_acc):
        @pl.when(pl.program_id(1) == 0)
        def _():
            dg_ref[...] = jnp.zeros_like(dg_ref)

        g = g_ref[...]
        z = z_ref[...].astype(F32)
        o = o_ref[0, 0]
        dy = dy_ref[...].astype(F32)
        sig = _sigmoid(z)
        ohat, r = _rms_stats(o)
        don = dy * (z * sig)
        dz_ref[...] = (dy * (ohat * g) * (sig * (1.0 + z * (1.0 - sig)))).astype(BF16)
        dg_ref[...] += jnp.sum(don * ohat, axis=0, keepdims=True)
        do = _rms_bwd(g * don, ohat, r)
        dvec = jnp.sum(do * o, axis=-1, keepdims=True)
        dob = do.astype(BF16)
        lse_col = lse_ref[0, 0, :, 0:1]
        dq_acc[...] = jnp.zeros_like(dq_acc)
        for k0 in range(0, tp, K_TILE):
            nk = min(K_TILE, tp - k0)
            nq = tp - k0
            qq = q_ref[0, 0, k0:, :]
            kk = k_ref[0, 0, k0:k0 + nk, :]
            causal = (lax.broadcasted_iota(jnp.int32, (nq, nk), 1) <= lax.broadcasted_iota(jnp.int32, (nq, nk), 0))
            pr = jnp.where(causal, jnp.exp2(_dot(qq, kk, _NT) - lse_col[k0:]), 0.0)
            dp = _dot(dob[k0:], v_ref[0, 0, k0:k0 + nk, :], _NT)
            ds = (pr * (dp - dvec[k0:])).astype(BF16)
            dv_ref[0, 0, k0:k0 + nk, :] = _dot(pr.astype(BF16), dob[k0:], _TN).astype(BF16)
            dk_ref[0, 0, k0:k0 + nk, :] = (_dot(ds, qq, _TN) * (ATTN_SCALE / Q_SCALE)).astype(BF16)
            dq_acc[k0:, :] += _dot(ds, kk)
        dq_ref[0, 0] = (dq_acc[...] * ATTN_SCALE).astype(BF16)

    qk = pl.BlockSpec((1, 1, tp, 2 * LANES), lambda h, b: (b, h, 0, 0))
    hv = pl.BlockSpec((1, 1, tp, D_V), lambda h, b: (b, h, 0, 0))
    col = pl.BlockSpec((tp, LANES), lambda h, b: (b, h))
    return pl.pallas_call(
        body,
        name="attn_bwd",
        grid=(N_HEADS, nb_seq),
        in_specs=[qk, qk, hv, hv, hv, col,
                  pl.BlockSpec((tp, LANES), lambda h, b: (b, GRP_A // LANES + h)),
                  pl.BlockSpec((1, LANES), lambda h, b: (0, h))],
        out_specs=(qk, qk, hv, col, pl.BlockSpec((1, LANES), lambda h, b: (0, h))),
        out_shape=(
            jax.ShapeDtypeStruct((nb_seq, N_HEADS, tp, 2 * LANES), BF16),
            jax.ShapeDtypeStruct((nb_seq, N_HEADS, tp, 2 * LANES), BF16),
            jax.ShapeDtypeStruct((nb_seq, N_HEADS, tp, D_V), BF16),
            jax.ShapeDtypeStruct((nb_seq * tp, N_HEADS * D_V), BF16),
            jax.ShapeDtypeStruct((1, N_HEADS * D_V), F32),
        ),
        scratch_shapes=[pltpu.VMEM((tp, 2 * LANES), F32)],
        compiler_params=_params("arbitrary", "arbitrary"),
    )(q, k, v, o, lse, dcat, p, g_attn)


def _qkv_bwd(p, dq, dk, dv, wq, wkv, gq, gkv, tables):
    nb_seq, _, tp, _ = dq.shape
    ht = tp // 2

    def body(pa_ref, dq_ref, dk_ref, dv_ref, wq_ref, wkv_ref, gq_ref, gkv_ref, cos_ref, sa_ref, sb_ref,
             dpa_ref, dwq_ref, dwkv_ref, dgq_ref, dgkv_ref):
        @pl.when(pl.program_id(0) == 0)
        def _():
            dwq_ref[...] = jnp.zeros_like(dwq_ref)
            dwkv_ref[...] = jnp.zeros_like(dwkv_ref)
            dgq_ref[...] = jnp.zeros_like(dgq_ref)
            dgkv_ref[...] = jnp.zeros_like(dgkv_ref)

        pa = pa_ref[...].astype(F32)
        gq, gkv = gq_ref[...], gkv_ref[...]
        cq_hat, rq = _rms_stats(pa[:, :Q_RANK])
        ckv_hat, rkv = _rms_stats(pa[:, Q_RANK:Q_RANK + KV_RANK])
        tabs = (cos_ref[...], sa_ref[...], sb_ref[...])

        pe = [dq_ref[0, h, :, D_NOPE:].astype(F32) for h in range(N_HEADS)]
        pairs = [_rope_t(pe[2 * i] + pltpu.roll(pe[2 * i + 1], D_ROPE, 1), *tabs).astype(BF16) for i in range(2)]
        dq_flat = jnp.concatenate([dq_ref[0, h, :, :D_NOPE] for h in range(N_HEADS)] + pairs, axis=1)
        dwq_ref[...] += _dot(dq_flat, (cq_hat * gq).astype(BF16), _TN)
        dcqn = _dot(dq_flat, wq_ref[...])
        dgq_ref[...] += jnp.sum(dcqn * cq_hat, axis=0, keepdims=True)
        dcq = _rms_bwd(gq * dcqn, cq_hat, rq)

        dkv_flat = jnp.concatenate([dk_ref[0, h, :, :D_NOPE] for h in range(N_HEADS)]
                                   + [dv_ref[0, h] for h in range(N_HEADS)], axis=1)
        dwkv_ref[...] += _dot((ckv_hat * gkv).astype(BF16), dkv_flat, _TN)
        dckvn = _dot(dkv_flat, wkv_ref[...], _NT)
        dgkv_ref[...] += jnp.sum(dckvn * ckv_hat, axis=0, keepdims=True)
        dckv = _rms_bwd(gkv * dckvn, ckv_hat, rkv)

        dk_pe = dk_ref[0, 0, :, D_NOPE:].astype(F32)
        for h in range(1, N_HEADS):
            dk_pe = dk_pe + dk_ref[0, h, :, D_NOPE:].astype(F32)
        dk_pe = jnp.where(lax.broadcasted_iota(jnp.int32, (ht, LANES), 1) < D_ROPE, dk_pe, 0.0)
        dpa_ref[...] = jnp.concatenate([dcq, dckv, _rope_t(dk_pe, *tabs)], axis=1).astype(BF16)

    full = lambda a: pl.BlockSpec(a.shape, lambda i: (0,) * a.ndim)
    tab = pl.BlockSpec((ht, LANES), lambda i: (i % 2, 0))
    qk = pl.BlockSpec((1, N_HEADS, ht, 2 * LANES), lambda i: (i // 2, 0, i % 2, 0))
    acc = lambda shape: pl.BlockSpec(shape, lambda i: (0, 0))
    return pl.pallas_call(
        body,
        name="qkv_bwd",
        grid=(2 * nb_seq,),
        in_specs=[pl.BlockSpec((ht, GRP_A), lambda i: (i, 0)), qk, qk,
                  pl.BlockSpec((1, N_HEADS, ht, D_V), lambda i: (i // 2, 0, i % 2, 0)),
                  full(wq), full(wkv), full(gq), full(gkv), tab, tab, tab],
        out_specs=(pl.BlockSpec((ht, GRP_A), lambda i: (i, 0)),
                   acc(wq.shape), acc(wkv.shape), acc((1, Q_RANK)), acc((1, KV_RANK))),
        out_shape=(
            jax.ShapeDtypeStruct((nb_seq * tp, GRP_A), BF16),
            jax.ShapeDtypeStruct(wq.shape, F32),
            jax.ShapeDtypeStruct(wkv.shape, F32),
            jax.ShapeDtypeStruct((1, Q_RANK), F32),
            jax.ShapeDtypeStruct((1, KV_RANK), F32),
        ),
        compiler_params=_params("arbitrary"),
    )(p, dq, dk, dv, wq, wkv, gq, gkv, *tables)


def _conv_bwd(p, dcat, conv_w, g_conv, nb_seq, tp):
    cols = CONV_WIDTH // LANES

    def body(b_ref, c_ref, h_ref, z_ref, dy_ref, w_ref, g_ref,
             db_ref, dc_ref, dh_ref, dz_ref, dw_ref, dg_ref):
        @pl.when(pl.program_id(1) == 0)
        def _():
            dw_ref[...] = jnp.zeros_like(dw_ref)
            dg_ref[...] = jnp.zeros_like(dg_ref)

        cb, c, h = b_ref[...].astype(F32), c_ref[...].astype(F32), h_ref[...].astype(F32)
        z, dy = z_ref[...].astype(F32), dy_ref[...].astype(F32)
        g = g_ref[...]
        w0, w1, w2 = w_ref[0:1, :], w_ref[1:2, :], w_ref[2:3, :]
        cc = c * h
        row = lax.broadcasted_iota(jnp.int32, (tp, LANES), 0)
        s1 = jnp.where(row >= 1, pltpu.roll(cc, 1, 0), 0.0)
        s2 = jnp.where(row >= 2, pltpu.roll(cc, 2, 0), 0.0)
        dwc = w0 * s2 + w1 * s1 + w2 * cc
        yc = cb * dwc
        r = lax.rsqrt(_group_mean(yc * yc) + EPS)
        ychat = yc * r
        sig = _sigmoid(z)
        dz_ref[...] = (dy * (ychat * g) * (sig * (1.0 + z * (1.0 - sig)))).astype(BF16)
        dyn = dy * (z * sig)
        dg_ref[...] += jnp.sum(dyn * ychat, axis=0, keepdims=True)
        gd = g * dyn
        dyc = r * (gd - ychat * _group_mean(gd * ychat))
        db_ref[...] = (dyc * dwc).astype(BF16)
        ddw = dyc * cb
        dw_ref[0:1, :] += jnp.sum(ddw * s2, axis=0, keepdims=True)
        dw_ref[1:2, :] += jnp.sum(ddw * s1, axis=0, keepdims=True)
        dw_ref[2:3, :] += jnp.sum(ddw * cc, axis=0, keepdims=True)
        u1 = jnp.where(row <= tp - 2, pltpu.roll(ddw, tp - 1, 0), 0.0)
        u2 = jnp.where(row <= tp - 3, pltpu.roll(ddw, tp - 2, 0), 0.0)
        dcc = w2 * ddw + w1 * u1 + w0 * u2
        dc_ref[...] = (dcc * h).astype(BF16)
        dh_ref[...] = (dcc * c).astype(BF16)

    col = pl.BlockSpec((tp, LANES), lambda t, b: (b, t))
    out = jax.ShapeDtypeStruct((nb_seq * tp, CONV_WIDTH), BF16)
    return pl.pallas_call(
        body,
        name="conv_bwd",
        grid=(cols, nb_seq),
        in_specs=_conv_specs(tp, lambda t, b, off: (b, off + t)) + [
            pl.BlockSpec((tp, LANES), lambda t, b: (b, N_HEADS * D_V // LANES + t)),
            pl.BlockSpec((8, LANES), lambda t, b: (0, t)),
            pl.BlockSpec((1, LANES), lambda t, b: (0, t))],
        out_specs=(col, col, col, col,
                   pl.BlockSpec((8, LANES), lambda t, b: (0, t)), pl.BlockSpec((1, LANES), lambda t, b: (0, t))),
        out_shape=(out, out, out, out,
                   jax.ShapeDtypeStruct((8, CONV_WIDTH), F32), jax.ShapeDtypeStruct((1, CONV_WIDTH), F32)),
        compiler_params=_params("arbitrary", "arbitrary"),
    )(p, p, p, p, dcat, conv_w, g_conv)


def _input_bwd(dps, w_in, x, meta, dh, norm_g, nt, send_in):
    nb_seq, s, d = x.shape
    r, kb = dps[0].shape
    ts = (s + LANES) // nt
    steps = nb_seq * nt
    n_dp = len(dps)
    in_slot = send_in.shape[1:]

    def body(*refs):
        dp_refs, w_ref, x_hbm, meta_ref, dh_ref, g_ref, pay_ref = refs[:n_dp], *refs[n_dp:n_dp + 6]
        o = n_dp + 6
        gx_hbm, dmeta_ref, dg_ref, r2_in = refs[o:o + 4]
        xbuf, gxbuf, tok_sems, own_in, r1_in, sum_in = refs[o + 4:o + 10]
        sems = refs[o + 10:]
        i = pl.program_id(0)
        b, k = i // nt, i % nt

        def plan():
            return _reduce_plan((pay_ref,), (own_in,), (r1_in,), (sum_in,), (r2_in,), *sems)

        @pl.when(i == 0)
        def _():
            dmeta_ref[...] = jnp.zeros_like(dmeta_ref)
            dg_ref[...] = jnp.zeros_like(dg_ref)
            plan()[0]()

        @pl.when(i == 1)
        def _():
            plan()[1]()

        def start(kk):
            if kk == 0:
                xbuf[0:PAD_FRONT, :] = jnp.zeros((PAD_FRONT, d), F32)
                xbuf[PAD_FRONT:LANES, :] = meta_ref[...]
            _token_copy(x_hbm, b, kk, ts, xbuf, tok_sems.at[0]).start()

        _for_tile(k, nt, start)
        du = _dot(dp_refs[0][...], w_ref[0:kb, :])
        for j in range(1, n_dp):
            du = du + _dot(dp_refs[j][...], w_ref[kb * j:kb * (j + 1), :])
        _for_tile(k, nt, lambda kk: _token_copy(x_hbm, b, kk, ts, xbuf, tok_sems.at[0]).wait())

        g = g_ref[...]
        hhat, rstd = _rms_stats(xbuf[...])
        dg_ref[...] += jnp.sum(du * hhat, axis=0, keepdims=True)
        res = _rms_bwd(g * du, hhat, rstd) + dh_ref[...].astype(F32)

        @pl.when(i > 0)
        def _():
            _for_tile(k, nt, lambda kk: _token_copy(gx_hbm, b, (kk - 1) % nt, ts, gxbuf, tok_sems.at[1], True).wait())

        gxbuf[...] = res

        @pl.when(k == 0)
        def _():
            dmeta_ref[...] += gxbuf[PAD_FRONT:LANES, :]

        _for_tile(k, nt, lambda kk: _token_copy(gx_hbm, b, kk, ts, gxbuf, tok_sems.at[1], True).start())

        @pl.when(i == steps - 1)
        def _():
            _token_copy(gx_hbm, b, nt - 1, ts, gxbuf, tok_sems.at[1], True).wait()
            plan()[2]()

    whole = lambda a: pl.BlockSpec(a.shape, lambda i: (0,) * a.ndim)
    hbm = pl.BlockSpec(memory_space=pl.ANY)
    return pl.pallas_call(
        body,
        name="input_bwd",
        grid=(steps,),
        in_specs=[pl.BlockSpec((ts, kb), lambda i: (i, 0)) for _ in dps]
        + [whole(w_in), hbm, whole(meta), pl.BlockSpec((ts, d), lambda i: (i, 0)), whole(norm_g), hbm],
        out_specs=(hbm, pl.BlockSpec((N_META, d), lambda i: (0, 0)), pl.BlockSpec((1, d), lambda i: (0, 0)), hbm),
        out_shape=(jax.ShapeDtypeStruct((nb_seq, s, d), F32),
                   jax.ShapeDtypeStruct((N_META, d), F32),
                   jax.ShapeDtypeStruct((1, d), F32),
                   jax.ShapeDtypeStruct((N_CHIPS,) + in_slot, BF16)),
        scratch_shapes=[pltpu.VMEM((ts, d), F32), pltpu.VMEM((ts, d), F32), pltpu.SemaphoreType.DMA((2,))]
        + _reduce_scratch([(in_slot, BF16)], [True]),
        compiler_params=_params("arbitrary"),
    )(*dps, w_in, x, meta, dh, norm_g, send_in)


def _in_proj_bwd_w(u, dps, bm, small_grads, send_out):
    r, d = u.shape
    kb = dps[0].shape[1]
    steps = r // bm
    n_dp, n_small = len(dps), len(small_grads)
    out_slot, small_slot = send_out.shape[1:], (SMALL_ROWS, LANES)

    def body(*refs):
        u_ref, dp_refs = refs[0], refs[1:1 + n_dp]
        small_refs = refs[1 + n_dp:1 + n_dp + n_small]
        o = 1 + n_dp + n_small
        pay_out, o_ref, r2_out, r2_small = refs[o:o + 4]
        acc_ref, ssmall, r1_out, sum_out, r1_small, sum_small = refs[o + 4:o + 10]
        sems = refs[o + 10:]
        i = pl.program_id(0)

        def plan():
            return _reduce_plan((pay_out, ssmall), (None, None), (r1_out, r1_small), (sum_out, sum_small),
                                (r2_out, r2_small), *sems)

        @pl.when(i == 0)
        def _():
            acc_ref[...] = jnp.zeros_like(acc_ref)
            _pack_small(ssmall, *small_refs)
            plan()[0]()

        @pl.when(i == 1)
        def _():
            plan()[1]()

        uu = u_ref[...]
        for j in range(n_dp):
            acc_ref[kb * j:kb * (j + 1), :] += _dot(dp_refs[j][...], uu, _TN)

        @pl.when(i == steps - 1)
        def _():
            for k in range(N_DEV):
                for s, e, c0 in _in_pieces(k):
                    o_ref[k, s:e, :] = acc_ref[c0:c0 + e - s, :].astype(BF16)
                o_ref[k, SHARD_IN:, :] = jnp.zeros((SHARD_IN_PAD - SHARD_IN, d), BF16)
            plan()[2]()

    whole = lambda a: pl.BlockSpec(a.shape, lambda i: (0,) * a.ndim)
    hbm = pl.BlockSpec(memory_space=pl.ANY)
    return pl.pallas_call(
        body,
        name="in_proj_bwd_w",
        grid=(steps,),
        in_specs=[pl.BlockSpec((bm, d), lambda i: (i, 0))]
        + [pl.BlockSpec((bm, kb), lambda i: (i, 0)) for _ in dps] + [whole(a) for a in small_grads]
        + [whole(send_out)],
        out_specs=(pl.BlockSpec((N_DEV, SHARD_IN_PAD, d), lambda i: (0, 0, 0)), hbm, hbm),
        out_shape=(jax.ShapeDtypeStruct((N_DEV, SHARD_IN_PAD, d), BF16),
                   jax.ShapeDtypeStruct((N_CHIPS,) + out_slot, BF16),
                   jax.ShapeDtypeStruct((N_CHIPS,) + small_slot, F32)),
        scratch_shapes=[pltpu.VMEM((kb * n_dp, d), F32), pltpu.VMEM((N_DEV,) + small_slot, F32)]
        + _reduce_scratch([(out_slot, BF16), (small_slot, F32)], [False, False]),
        compiler_params=_params("arbitrary"),
    )(u, *dps, *small_grads, send_out)


def _local_step(x, loss_target, u, p, meta_f, norm_g, w_in_p, q_norm_g, w_q_p, kv_norm_g, w_kv_p, conv_w_f,
                attn_out_g, conv_out_g, w_out_f, g_final):
    nb_seq, s, d = x.shape
    tp = s + LANES
    ht = tp // 2
    tables = _rope_tables(tp)

    q, k, v = _qkv_fwd(p, w_q_p, w_kv_p, q_norm_g, kv_norm_g, tables, nb_seq, tp)
    ya, o, lse = _attn_fwd(q, k, v, p, attn_out_g)
    yc = _conv_fwd(p, conv_w_f, conv_out_g, nb_seq, tp)
    dhb, dcat, d_w_out, d_final_g, loss_part = _out_proj_loss(
        ya, yc, w_out_f, x, loss_target, g_final, TOKEN_TILES)
    send_out = d_w_out.reshape(N_DEV, SHARD_OUT, d)
    dq, dk, dv, dz_attn, d_attn_g = _attn_bwd(q, k, v, o, lse, dcat, p, attn_out_g)
    dpa, d_wq_p, d_wkv_p, d_gq, d_gkv = _qkv_bwd(p, dq, dk, dv, w_q_p, w_kv_p, q_norm_g, kv_norm_g, tables)
    d_b, d_c, d_h, dz_conv, d_conv_w, d_conv_g = _conv_bwd(p, dcat, conv_w_f, conv_out_g, nb_seq, tp)
    dps = (dpa, dz_attn, d_b, d_c, d_h, dz_conv)
    small = (d_wq_p, d_wkv_p, d_conv_w, d_final_g, d_gq, d_gkv, d_attn_g, d_conv_g, loss_part)
    send_in, r_out, r_small = _in_proj_bwd_w(u, dps, ht, small, send_out)
    grad_x, d_meta, d_norm_g, r_in = _input_bwd(dps, w_in_p, x, meta_f, dhb, norm_g, TOKEN_TILES, send_in)
    return grad_x, r_in, r_out, r_small, d_meta, d_norm_g


def kernel(x, meta_tokens, norm_g, w_in, q_norm_g, w_q_up, kv_norm_g, w_kv_up, conv_w, attn_out_g, conv_out_g, w_out, final_norm_g, loss_target, m_meta_tokens, m_norm_g, m_w_in, m_q_norm_g, m_w_q_up, m_kv_norm_g, m_w_kv_up, m_conv_w, m_attn_out_g, m_conv_out_g, m_w_out, m_final_norm_g, v_meta_tokens, v_norm_g, v_w_in, v_q_norm_g, v_w_q_up, v_kv_norm_g, v_w_kv_up, v_conv_w, v_attn_out_g, v_conv_out_g, v_w_out, v_final_norm_g):
    d = x.shape[-1]
    ht = (x.shape[1] + LANES) // 2
    u, w_in_p, meta_f = _prep_gather(x, meta_tokens, norm_g, w_in[0].T)
    p, w_q_p, w_kv_p, w_out_f, conv_w_f = _in_proj_gather(
        u, w_in_p, w_q_up[0].T, w_kv_up[0], w_out[0], conv_w.transpose(1, 0, 2), ht, 3 * GRP_A)
    g_final = final_norm_g.reshape(1, d)
    grad_x, r_in, r_out, r_small, d_meta, d_norm_g = _local_step(
        x, loss_target, u, p, meta_f, norm_g, w_in_p, q_norm_g, w_q_p, kv_norm_g, w_kv_p, conv_w_f,
        attn_out_g, conv_out_g, w_out_f, g_final)

    flat = lambda a: a.reshape(a.shape[-2:]) if a.ndim == 3 else a.reshape(1, -1) if a.ndim == 1 else a
    transposed = ("w_in", "w_q_up")

    def to_kernel(n, a):
        if n == "conv_w":
            return a.transpose(1, 0, 2)
        return flat(a).T if n in transposed else flat(a)

    def from_kernel(n, a, shape):
        if n == "conv_w":
            return a.transpose(1, 0, 2)
        return (a.T if n in transposed else a).reshape(shape)
    params = {
        "meta_tokens": (meta_tokens, m_meta_tokens, v_meta_tokens),
        "norm_g": (norm_g, m_norm_g, v_norm_g),
        "w_in": (w_in, m_w_in, v_w_in),
        "q_norm_g": (q_norm_g, m_q_norm_g, v_q_norm_g),
        "w_q_up": (w_q_up, m_w_q_up, v_w_q_up),
        "kv_norm_g": (kv_norm_g, m_kv_norm_g, v_kv_norm_g),
        "w_kv_up": (w_kv_up, m_w_kv_up, v_w_kv_up),
        "conv_w": (conv_w, m_conv_w, v_conv_w),
        "attn_out_g": (attn_out_g, m_attn_out_g, v_attn_out_g),
        "conv_out_g": (conv_out_g, m_conv_out_g, v_conv_out_g),
        "w_out": (w_out, m_w_out, v_w_out),
        "final_norm_g": (final_norm_g, m_final_norm_g, v_final_norm_g),
    }
    grads, loss = _reduce_tail(r_in, r_out, r_small, d_meta, d_norm_g)
    updated = _adamw(grads, {n: tuple(to_kernel(n, a) for a in t) for n, t in params.items()})
    outs = [[from_kernel(n, updated[n][i], params[n][0].shape) for n, _ in PARAM_SHAPES] for i in range(4)]
    return (loss[0, 0], grad_x, *outs[0], *outs[1], *outs[2], *outs[3])
```

```python
import functools

import jax
import jax.numpy as jnp
from jax import lax
from jax.experimental import pallas as pl
from jax.experimental.pallas import tpu as pltpu

F32 = jnp.float32
BF16 = jnp.bfloat16

N_META = 16
D_MODEL = 1024
N_HEADS = 4
D_NOPE = 128
D_ROPE = 64
D_V = 128
Q_RANK = 256
KV_RANK = 128
CONV_WIDTH = 512
CONV_GROUP = 64
ROPE_THETA = 10000.0
ATTN_SCALE = (D_NOPE + D_ROPE) ** -0.5
Q_SCALE = ATTN_SCALE * 1.4426950408889634
EPS = 1e-6
NEG_INF = -1e30

ADAM_LR = 0.001
ADAM_B1 = 0.9
ADAM_B2 = 0.999
ADAM_EPS = 1e-08
ADAM_WD = 0.01
ADAM_STEP = 10

LANES = 128
PAD_FRONT = LANES - N_META
K_TILE = 256
Q_TILE = 512
N_DEV = 8
VMEM_LIMIT = 56 * 1024 * 1024

IN_PAD = 3072
GRP_A = 512
N_A = Q_RANK + KV_RANK + D_ROPE
IN_PROJ = 3008
SHARD_IN = IN_PROJ // N_DEV
SHARD_IN_PAD = 384
SHARD_Q = 96
SHARD_KV = 128
SHARD_OUT = 128
SHARD_CONV = 64
SHARD_META = 128
Q_COLS = N_HEADS * (D_NOPE + D_ROPE)
KV_COLS = N_HEADS * (D_NOPE + D_V)

ROW_Q, ROW_KV, ROW_META, ROW_CONV = 0, 256, 384, 400
ROW_REPL = 408
ROW_NORM, ROW_FINAL, ROW_GQ, ROW_GKV, ROW_ATTN, ROW_CONVG, ROW_LOSS = 408, 416, 424, 426, 427, 431, 435
SMALL_ROWS = 440

PARAM_SHAPES = (
    ("meta_tokens", (N_META, SHARD_META)), ("norm_g", (1, D_MODEL)), ("w_in", (SHARD_IN, D_MODEL)),
    ("q_norm_g", (1, Q_RANK)), ("w_q_up", (SHARD_Q, Q_RANK)), ("kv_norm_g", (1, KV_RANK)),
    ("w_kv_up", (KV_RANK, SHARD_KV)), ("conv_w", (3, 1, SHARD_CONV)), ("attn_out_g", (1, CONV_WIDTH)),
    ("conv_out_g", (1, CONV_WIDTH)), ("w_out", (SHARD_OUT, D_MODEL)), ("final_norm_g", (1, D_MODEL)),
)


def _in_pieces(k):
    lo, hi = SHARD_IN * k, SHARD_IN * (k + 1)
    out = []
    if lo < N_A:
        out.append((0, min(hi, N_A) - lo, lo))
    if hi > N_A:
        s = max(lo, N_A)
        out.append((s - lo, hi - lo, s + GRP_A - N_A))
    return out


def _q_pieces(k):
    lo, hi = SHARD_Q * k, SHARD_Q * (k + 1)
    out = []
    for h in range(N_HEADS):
        base = (D_NOPE + D_ROPE) * h
        s, e = max(lo, base), min(hi, base + D_NOPE)
        if s < e:
            out.append((s - lo, e - lo, D_NOPE * h + s - base))
        s, e = max(lo, base + D_NOPE), min(hi, base + D_NOPE + D_ROPE)
        if s < e:
            out.append((s - lo, e - lo, N_HEADS * D_NOPE + D_ROPE * h + s - base - D_NOPE))
    return out


def _kv_dst(k):
    return D_NOPE * (k // 2) + (N_HEADS * D_NOPE if k % 2 else 0)


def _params(*sem):
    return pltpu.CompilerParams(dimension_semantics=sem, vmem_limit_bytes=VMEM_LIMIT)


def _rms_stats(x):
    r = lax.rsqrt(jnp.mean(x * x, axis=-1, keepdims=True) + EPS)
    return x * r, r


def _rms_bwd(gdy, xhat, r):
    return r * (gdy - xhat * jnp.mean(gdy * xhat, axis=-1, keepdims=True))


def _sigmoid(z):
    return 1.0 / (1.0 + jnp.exp(-z))


def _group_mean(x):
    i0 = lax.broadcasted_iota(jnp.int32, (LANES, LANES), 0) // CONV_GROUP
    i1 = lax.broadcasted_iota(jnp.int32, (LANES, LANES), 1) // CONV_GROUP
    m = jnp.where(i0 == i1, 1.0 / CONV_GROUP, 0.0).astype(BF16)
    hi = x.astype(BF16)
    lo = (x - hi.astype(F32)).astype(BF16)
    return jnp.dot(hi, m, preferred_element_type=F32) + jnp.dot(lo, m, preferred_element_type=F32)


_NT = (((1,), (1,)), ((), ()))
_TN = (((0,), (0,)), ((), ()))


def _dot(a, b, dims=None):
    if dims is None:
        return jnp.dot(a, b, preferred_element_type=F32)
    return lax.dot_general(a, b, dims, preferred_element_type=F32)


def _device_position():
    x, y, c = lax.axis_index("x"), lax.axis_index("y"), lax.axis_index("c")
    return x, y, c, 4 * x + 2 * y + c


def _gather_plan(srcs, slots, send_sems, recv_sems, local_sems):
    x, y, c, _ = _device_position()
    me, sibling = (x, y, c), (x, y, 1 - c)
    flip = lambda v, on: v + on - 2 * v * on
    near = (flip(x, 1 - c), flip(y, c))
    far = (flip(x, c), flip(y, 1 - c))
    diag = (1 - x, 1 - y)
    n = len(srcs)

    def slot(a, px, py, pc):
        return slots[a].at[4 * px + 2 * py + pc]

    def copy(a, k, block, to, own=False):
        return pltpu.make_async_remote_copy(
            src_ref=srcs[a] if own else slot(a, *block),
            dst_ref=slot(a, *block),
            send_sem=send_sems.at[7 * a + k],
            recv_sem=recv_sems.at[7 * a + k],
            device_id=to,
            device_id_type=pl.DeviceIdType.MESH,
        )

    def local(a):
        return pltpu.make_async_copy(srcs[a], slot(a, *me), local_sems.at[a])

    sent = [(me, sibling), (me, (*near, c)), (me, (*far, c)), ((*near, c), (*far, c)),
            ((*near, c), sibling), ((*far, c), sibling), ((*diag, c), sibling)]
    landed = [sibling, (*near, c), (*far, c), (*diag, c), (*far, 1 - c), (*near, 1 - c), (*diag, 1 - c)]

    def send(a, k):
        return copy(a, k, *sent[k], own=k < 3)

    def arrival(a, k):
        return copy(a, k, landed[k], me)

    def start():
        for a in range(n):
            local(a).start()
            for k in range(3):
                send(a, k).start()

    def relay():
        for a in range(n):
            arrival(a, 1).wait_recv()
            send(a, 3).start()
            send(a, 4).start()

    def forward():
        for k in (2, 3):
            for a in range(n):
                arrival(a, k).wait_recv()
                send(a, k + 3).start()

    def finish():
        for a in range(n):
            for k in (0, 4, 5, 6):
                arrival(a, k).wait_recv()
        for a in range(n):
            for k in range(7):
                send(a, k).wait_send()
            local(a).wait()

    return start, relay, forward, finish


def _adam_update(g, w, m, v):
    m_new = ADAM_B1 * m + (1.0 - ADAM_B1) * g
    v_new = ADAM_B2 * v + (1.0 - ADAM_B2) * (g * g)
    m_hat = m_new / (1.0 - ADAM_B1 ** ADAM_STEP)
    v_hat = v_new / (1.0 - ADAM_B2 ** ADAM_STEP)
    return -ADAM_LR * (m_hat / (jnp.sqrt(v_hat) + ADAM_EPS) + ADAM_WD * w), m_new, v_new


def _adamw(grads, params):
    names = [n for n, _ in PARAM_SHAPES]
    n_p = len(names)

    def body(*refs):
        for i in range(n_p):
            g = refs[i][...]
            w, m, v = (refs[n_p + 3 * i + j][...] for j in range(3))
            delta, m_new, v_new = _adam_update(g, w, m, v)
            for j, val in enumerate((g, delta, m_new, v_new)):
                refs[4 * n_p + 4 * i + j][...] = val

    vm = pl.BlockSpec(memory_space=pltpu.VMEM)
    out_shape = []
    for _, shape in PARAM_SHAPES:
        out_shape += [jax.ShapeDtypeStruct(shape, F32)] * 4
    outs = pl.pallas_call(
        body,
        name="adamw",
        out_shape=tuple(out_shape),
        in_specs=[vm] * (4 * n_p),
        out_specs=(vm,) * (4 * n_p),
        compiler_params=pltpu.CompilerParams(vmem_limit_bytes=VMEM_LIMIT),
    )(*[grads[n] for n in names], *[a for n in names for a in params[n]])
    return {n: outs[4 * i:4 * i + 4] for i, n in enumerate(names)}


N_CHIPS = 4


def _reduce_plan(pays, owns, r1s, sums, r2s, send1, recv1, send2, recv2, local_sems):
    x, y, c, _ = _device_position()
    sibling = (x, y, 1 - c)
    chips = [((1 - x if rj & 2 else x), (1 - y if rj & 1 else y)) for rj in range(N_CHIPS)]
    n = len(pays)

    def slot_of(rj, core):
        return 4 * chips[rj][0] + 2 * chips[rj][1] + core

    def to_sibling(a, rj):
        return pltpu.make_async_remote_copy(
            src_ref=pays[a].at[slot_of(rj, 1 - c)], dst_ref=r1s[a].at[rj],
            send_sem=send1.at[N_CHIPS * a + rj], recv_sem=recv1.at[N_CHIPS * a + rj],
            device_id=sibling, device_id_type=pl.DeviceIdType.MESH)

    def load_own(a, rj):
        return pltpu.make_async_copy(pays[a].at[slot_of(rj, c)], owns[a].at[rj], local_sems.at[2 * N_CHIPS * a + rj])

    def to_chip(a, rj):
        return pltpu.make_async_remote_copy(
            src_ref=sums[a].at[rj], dst_ref=r2s[a].at[rj],
            send_sem=send2.at[N_CHIPS * a + rj], recv_sem=recv2.at[N_CHIPS * a + rj],
            device_id=(*chips[rj], c), device_id_type=pl.DeviceIdType.MESH)

    def keep(a):
        return pltpu.make_async_copy(sums[a].at[0], r2s[a].at[0], local_sems.at[2 * N_CHIPS * a + N_CHIPS])

    def start():
        for a in range(n):
            for rj in range(N_CHIPS):
                to_sibling(a, rj).start()
                if owns[a] is not None:
                    load_own(a, rj).start()

    def combine():
        for a in range(n):
            for rj in range(N_CHIPS):
                to_sibling(a, rj).wait_recv()
                if owns[a] is not None:
                    load_own(a, rj).wait()
                    mine = owns[a][rj]
                else:
                    mine = pays[a][slot_of(rj, c)]
                sums[a][rj] = (mine.astype(F32) + r1s[a][rj].astype(F32)).astype(sums[a].dtype)
            keep(a).start()
            for rj in range(1, N_CHIPS):
                to_chip(a, rj).start()

    def finish():
        for a in range(n):
            for rj in range(1, N_CHIPS):
                to_chip(a, rj).wait_recv()
            for rj in range(N_CHIPS):
                to_sibling(a, rj).wait_send()
            for rj in range(1, N_CHIPS):
                to_chip(a, rj).wait_send()
            keep(a).wait()

    return start, combine, finish


def _reduce_scratch(shapes_dtypes, own_flags):
    out = []
    for (shape, dtype), own in zip(shapes_dtypes, own_flags):
        if own:
            out.append(pltpu.VMEM((N_CHIPS,) + shape, dtype))
        out += [pltpu.VMEM((N_CHIPS,) + shape, dtype), pltpu.VMEM((N_CHIPS,) + shape, dtype)]
    n = len(shapes_dtypes)
    out += [pltpu.SemaphoreType.DMA((N_CHIPS * n,))] * 4 + [pltpu.SemaphoreType.DMA((2 * N_CHIPS * n,))]
    return out


def _pack_small(ssmall, dwq, dwkv, dconv, dfinal, dgq, dgkv, dattn, dconvg, loss_part):
    ssmall[...] = jnp.zeros_like(ssmall)
    rep = ssmall.at[0]
    for i in range(D_MODEL // LANES):
        rep[ROW_FINAL + i:ROW_FINAL + i + 1, :] = dfinal[:, LANES * i:LANES * (i + 1)]
    for i in range(Q_RANK // LANES):
        rep[ROW_GQ + i:ROW_GQ + i + 1, :] = dgq[:, LANES * i:LANES * (i + 1)]
    rep[ROW_GKV:ROW_GKV + 1, :] = dgkv[...]
    for i in range(CONV_WIDTH // LANES):
        rep[ROW_ATTN + i:ROW_ATTN + i + 1, :] = dattn[:, LANES * i:LANES * (i + 1)]
        rep[ROW_CONVG + i:ROW_CONVG + i + 1, :] = dconvg[:, LANES * i:LANES * (i + 1)]
    rep[ROW_LOSS:ROW_LOSS + 1, :] = loss_part[...]
    for k in range(N_DEV):
        if k:
            ssmall[k, ROW_REPL:, :] = ssmall[0, ROW_REPL:, :]
        for s, e, d in _q_pieces(k):
            for i in range(Q_RANK // LANES):
                ssmall[k, ROW_Q + SHARD_Q * i + s:ROW_Q + SHARD_Q * i + e, :] = dwq[d:d + e - s, LANES * i:LANES * (i + 1)]
        ssmall[k, ROW_KV:ROW_KV + KV_RANK, :] = dwkv[:, _kv_dst(k):_kv_dst(k) + SHARD_KV]
        ssmall[k, ROW_CONV:ROW_CONV + 3, 0:SHARD_CONV] = dconv[0:3, SHARD_CONV * k:SHARD_CONV * (k + 1)]


TOKEN_TILES = 4
TAIL_ROWS = N_META + D_MODEL // LANES


def _reduce_tail(r_in, r_out, r_small, d_meta, d_norm):
    n_p = len(PARAM_SHAPES)
    names = [n for n, _ in PARAM_SHAPES]

    def body(*refs):
        rin, rout, rsmall, dmeta, dnorm = refs[:5]
        g_out = {n: refs[5 + i] for i, n in enumerate(names)}
        loss_out = refs[5 + n_p]
        stail, rtail, gsum, gtail, send_sems, recv_sems = refs[6 + n_p:]
        x, y, c, me = _device_position()
        my_chip = 2 * x + y

        for k in range(N_DEV):
            stail[k, 0:N_META, :] = dmeta[:, SHARD_META * k:SHARD_META * (k + 1)]
            for i in range(D_MODEL // LANES):
                stail[k, N_META + i:N_META + i + 1, :] = dnorm[:, LANES * i:LANES * (i + 1)]
        copies = []
        for r in range(1, N_DEV):
            peer = (1 - x if r & 4 else x, 1 - y if r & 2 else y, 1 - c if r & 1 else c)
            copies.append(pltpu.make_async_remote_copy(
                src_ref=stail.at[4 * peer[0] + 2 * peer[1] + peer[2]],
                dst_ref=rtail.at[r],
                send_sem=send_sems.at[r - 1],
                recv_sem=recv_sems.at[r - 1],
                device_id=peer,
                device_id_type=pl.DeviceIdType.MESH,
            ))
        for cp in copies:
            cp.start()
        rtail[0] = stail[me]

        g = rin[my_chip].astype(F32)
        for ch in range(1, N_CHIPS):
            g = g + rin[ch ^ my_chip].astype(F32)
        g_out["w_in"][...] = g[:SHARD_IN, :]

        g = rout[my_chip].astype(F32)
        gs = rsmall[my_chip]
        for ch in range(1, N_CHIPS):
            g = g + rout[ch ^ my_chip].astype(F32)
            gs = gs + rsmall[ch ^ my_chip]
        g_out["w_out"][...] = g
        gsum[...] = gs
        for i in range(Q_RANK // LANES):
            g_out["w_q_up"][:, LANES * i:LANES * (i + 1)] = gsum[ROW_Q + SHARD_Q * i:ROW_Q + SHARD_Q * (i + 1), :]
        g_out["w_kv_up"][...] = gsum[ROW_KV:ROW_KV + KV_RANK, :]
        for i in range(3):
            g_out["conv_w"][i] = gsum[ROW_CONV + i:ROW_CONV + i + 1, 0:SHARD_CONV]
        for name, row, width in (("final_norm_g", ROW_FINAL, D_MODEL), ("q_norm_g", ROW_GQ, Q_RANK),
                                 ("kv_norm_g", ROW_GKV, KV_RANK), ("attn_out_g", ROW_ATTN, CONV_WIDTH),
                                 ("conv_out_g", ROW_CONVG, CONV_WIDTH)):
            for i in range(width // LANES):
                g_out[name][:, LANES * i:LANES * (i + 1)] = gsum[row + i:row + i + 1, :]
        loss_out[...] = gsum[ROW_LOSS:ROW_LOSS + 1, :]

        for cp in copies:
            cp.wait_recv()
        gt = rtail[me]
        for d in range(1, N_DEV):
            gt = gt + rtail[d ^ me]
        gtail[...] = gt
        g_out["meta_tokens"][...] = gtail[0:N_META, :]
        for i in range(D_MODEL // LANES):
            g_out["norm_g"][:, LANES * i:LANES * (i + 1)] = gtail[N_META + i:N_META + i + 1, :]
        for cp in copies:
            cp.wait_send()

    vm = pl.BlockSpec(memory_space=pltpu.VMEM)
    out_shape = [jax.ShapeDtypeStruct(shape, F32) for _, shape in PARAM_SHAPES]
    out_shape.append(jax.ShapeDtypeStruct((1, LANES), F32))
    outs = pl.pallas_call(
        body,
        name="reduce_tail",
        out_shape=tuple(out_shape),
        in_specs=[vm] * 5,
        out_specs=(vm,) * len(out_shape),
        scratch_shapes=[
            pltpu.VMEM((N_DEV, TAIL_ROWS, LANES), F32),
            pltpu.VMEM((N_DEV, TAIL_ROWS, LANES), F32),
            pltpu.VMEM((SMALL_ROWS, LANES), F32),
            pltpu.VMEM((TAIL_ROWS, LANES), F32),
            pltpu.SemaphoreType.DMA((N_DEV - 1,)),
            pltpu.SemaphoreType.DMA((N_DEV - 1,)),
        ],
        compiler_params=pltpu.CompilerParams(vmem_limit_bytes=VMEM_LIMIT),
    )(r_in, r_out, r_small, d_meta, d_norm)
    return {n: outs[i] for i, n in enumerate(names)}, outs[-1]


def _prep_gather(x, meta, norm_g, w_in_t):
    nb_seq, s, d = x.shape
    nb = s // LANES + 1
    relay_step = nb_seq * (nb - 1) // 2
    forward_step = nb_seq * (nb - 1) - 1
    finish_step = nb_seq * (nb - 1)

    def body(x_ref, meta_ref, g_ref, win_ref, u_ref, w_in_p, meta_f,
             sbig, ssmall, gbig, gsmall, send_sems, recv_sems, local_sems):
        jj, b = pl.program_id(0), pl.program_id(1)
        t = jj * nb_seq + b

        def plan():
            return _gather_plan((sbig, ssmall), (gbig, gsmall), send_sems, recv_sems, local_sems)

        @pl.when(t == 0)
        def _():
            sbig[0:SHARD_IN, :] = win_ref[...].astype(BF16)
            sbig[SHARD_IN:, :] = jnp.zeros((SHARD_IN_PAD - SHARD_IN, d), BF16)
            ssmall[...] = meta_ref[...]
            plan()[0]()

        @pl.when(t == relay_step)
        def _():
            plan()[1]()

        @pl.when(t == forward_step)
        def _():
            plan()[2]()

        @pl.when(t == finish_step)
        def _():
            plan()[3]()
            w_in_p[N_A:GRP_A, :] = jnp.zeros((GRP_A - N_A, d), BF16)
            for k in range(N_DEV):
                for s0, e0, d0 in _in_pieces(k):
                    w_in_p[d0:d0 + e0 - s0, :] = gbig[k, s0:e0, :]
                meta_f[:, SHARD_META * k:SHARD_META * (k + 1)] = gsmall[k]

        def norm(h):
            hhat, _ = _rms_stats(h)
            return (hhat * g_ref[...]).astype(BF16)

        @pl.when(jj < nb - 1)
        def _():
            u_ref[...] = norm(x_ref[0])

        @pl.when(jj == nb - 1)
        def _():
            u_ref[0:PAD_FRONT, :] = jnp.zeros((PAD_FRONT, d), BF16)
            u_ref[PAD_FRONT:LANES, :] = norm(meta_f[...])

    whole = lambda shape: pl.BlockSpec(shape, lambda jj, b: (0,) * len(shape))
    return pl.pallas_call(
        body,
        name="prep_norm_gather",
        grid=(nb, nb_seq),
        in_specs=[
            pl.BlockSpec((1, LANES, d), lambda jj, b: (b, jnp.minimum(jj, nb - 2), 0)),
            whole(meta.shape), whole(norm_g.shape), whole(w_in_t.shape),
        ],
        out_specs=(pl.BlockSpec((LANES, d), lambda jj, b: (b * nb + (jj + 1) % nb, 0)),
                   whole((IN_PAD, d)), whole((N_META, d))),
        out_shape=(jax.ShapeDtypeStruct((nb_seq * nb * LANES, d), BF16),
                   jax.ShapeDtypeStruct((IN_PAD, d), BF16),
                   jax.ShapeDtypeStruct((N_META, d), F32)),
        scratch_shapes=[
            pltpu.VMEM((SHARD_IN_PAD, d), BF16),
            pltpu.VMEM((N_META, SHARD_META), F32),
            pltpu.VMEM((N_DEV, SHARD_IN_PAD, d), BF16),
            pltpu.VMEM((N_DEV, N_META, SHARD_META), F32),
            pltpu.SemaphoreType.DMA((14,)),
            pltpu.SemaphoreType.DMA((14,)),
            pltpu.SemaphoreType.DMA((2,)),
        ],
        compiler_params=_params("arbitrary", "arbitrary"),
    )(x, meta, norm_g, w_in_t)


def _in_proj_gather(u, w_in_p, w_q, w_kv, w_out, conv_w, bm, bn):
    m, k_dim = u.shape
    n = w_in_p.shape[0]
    steps = (m // bm) * (n // bn)
    relay_step, forward_step = steps // 3, 2 * steps // 3
    qkv_shape = (SHARD_Q + KV_RANK, Q_RANK)

    def body(a_ref, b_ref, wq_ref, wkv_ref, wout_ref, conv_ref, o_ref, w_q_p, w_kv_p, w_out_f, conv_f,
             sqkv, sout, sconv, gqkv, gout, gconv, send_sems, recv_sems, local_sems):
        t = pl.program_id(0) * (n // bn) + pl.program_id(1)

        def plan():
            return _gather_plan((sqkv, sout, sconv), (gqkv, gout, gconv), send_sems, recv_sems, local_sems)

        @pl.when(t == 0)
        def _():
            sqkv[...] = jnp.zeros_like(sqkv)
            sqkv[0:SHARD_Q, :] = wq_ref[...].astype(BF16)
            sqkv[SHARD_Q:, 0:SHARD_KV] = wkv_ref[...].astype(BF16)
            sout[...] = wout_ref[...].astype(BF16)
            sconv[...] = jnp.zeros_like(sconv)
            for i in range(3):
                sconv[i:i + 1, 0:SHARD_CONV] = conv_ref[i]
            plan()[0]()

        @pl.when(t == relay_step)
        def _():
            plan()[1]()

        @pl.when(t == forward_step)
        def _():
            plan()[2]()

        o_ref[...] = _dot(a_ref[...], b_ref[...], _NT).astype(o_ref.dtype)

        @pl.when(t == steps - 1)
        def _():
            plan()[3]()
            conv_f[...] = jnp.zeros_like(conv_f)
            for k in range(N_DEV):
                for s0, e0, d0 in _q_pieces(k):
                    w_q_p[d0:d0 + e0 - s0, :] = gqkv[k, s0:e0, :]
                w_kv_p[:, _kv_dst(k):_kv_dst(k) + SHARD_KV] = gqkv[k, SHARD_Q:, 0:SHARD_KV]
                w_out_f[SHARD_OUT * k:SHARD_OUT * (k + 1), :] = gout[k]
                conv_f[0:3, SHARD_CONV * k:SHARD_CONV * (k + 1)] = gconv[k, 0:3, 0:SHARD_CONV]

    whole = lambda shape: pl.BlockSpec(shape, lambda i, j: (0,) * len(shape))
    return pl.pallas_call(
        body,
        name="in_proj_gather",
        grid=(m // bm, n // bn),
        in_specs=[pl.BlockSpec((bm, k_dim), lambda i, j: (i, 0)), pl.BlockSpec((bn, k_dim), lambda i, j: (j, 0)),
                  whole(w_q.shape), whole(w_kv.shape), whole(w_out.shape), whole(conv_w.shape)],
        out_specs=(pl.BlockSpec((bm, bn), lambda i, j: (i, j)),
                   whole((Q_COLS, Q_RANK)), whole((KV_RANK, KV_COLS)), whole((D_MODEL, D_MODEL)),
                   whole((8, CONV_WIDTH))),
        out_shape=(jax.ShapeDtypeStruct((m, n), BF16),
                   jax.ShapeDtypeStruct((Q_COLS, Q_RANK), BF16),
                   jax.ShapeDtypeStruct((KV_RANK, KV_COLS), BF16),
                   jax.ShapeDtypeStruct((D_MODEL, D_MODEL), BF16),
                   jax.ShapeDtypeStruct((8, CONV_WIDTH), F32)),
        scratch_shapes=[
            pltpu.VMEM(qkv_shape, BF16),
            pltpu.VMEM((SHARD_OUT, D_MODEL), BF16),
            pltpu.VMEM((8, LANES), F32),
            pltpu.VMEM((N_DEV,) + qkv_shape, BF16),
            pltpu.VMEM((N_DEV, SHARD_OUT, D_MODEL), BF16),
            pltpu.VMEM((N_DEV, 8, LANES), F32),
            pltpu.SemaphoreType.DMA((21,)),
            pltpu.SemaphoreType.DMA((21,)),
            pltpu.SemaphoreType.DMA((3,)),
        ],
        compiler_params=_params("arbitrary", "arbitrary"),
    )(u, w_in_p, w_q, w_kv, w_out, conv_w)


def _rope_tables(tp):
    half = D_ROPE // 2
    inv_freq = 1.0 / (ROPE_THETA ** (jnp.arange(half, dtype=F32) / half))
    pos = (jnp.arange(tp) - PAD_FRONT).astype(F32)
    ang = pos[:, None] * inv_freq[None, :]
    cos = jnp.tile(jnp.cos(ang), (1, LANES // half))
    sin = jnp.tile(jnp.sin(ang), (1, LANES // half))
    first = (jnp.arange(LANES) % D_ROPE) < half
    return cos, jnp.where(first, -sin, 0.0), jnp.where(first, 0.0, sin)


def _rope(t, cos, sa, sb):
    return t * cos + pltpu.roll(t, LANES - D_ROPE // 2, 1) * sa + pltpu.roll(t, D_ROPE // 2, 1) * sb


def _rope_t(t, cos, sa, sb):
    return t * cos + pltpu.roll(t * sa, D_ROPE // 2, 1) + pltpu.roll(t * sb, LANES - D_ROPE // 2, 1)


def _qkv_fwd(p, wq, wkv, gq, gkv, tables, nb_seq, tp):
    ht = tp // 2

    def body(pa_ref, wq_ref, wkv_ref, gq_ref, gkv_ref, cos_ref, sa_ref, sb_ref, q_ref, k_ref, v_ref):
        pa = pa_ref[...].astype(F32)
        cq_hat, _ = _rms_stats(pa[:, :Q_RANK])
        ckv_hat, _ = _rms_stats(pa[:, Q_RANK:Q_RANK + KV_RANK])
        q = _dot((cq_hat * gq_ref[...]).astype(BF16), wq_ref[...], _NT) * Q_SCALE
        kv = _dot((ckv_hat * gkv_ref[...]).astype(BF16), wkv_ref[...])
        tabs = (cos_ref[...], sa_ref[...], sb_ref[...])
        lane = lax.broadcasted_iota(jnp.int32, (ht, LANES), 1)
        low = lane < D_ROPE
        mark = lane == D_ROPE
        row = (pl.program_id(0) % 2) * ht + lax.broadcasted_iota(jnp.int32, (ht, LANES), 0)
        k_pe = jnp.where(mark & (row < PAD_FRONT), NEG_INF, _rope(pa[:, Q_RANK + KV_RANK:], *tabs))
        one = jnp.where(mark & (row >= PAD_FRONT), 1.0, 0.0)
        pairs = [_rope(q[:, N_HEADS * D_NOPE + LANES * i:N_HEADS * D_NOPE + LANES * (i + 1)], *tabs) for i in range(2)]
        for h in range(N_HEADS):
            pair = pairs[h // 2]
            if h % 2:
                pair = pltpu.roll(pair, D_ROPE, 1)
            pe = jnp.where(low, pair, one)
            q_ref[0, h] = jnp.concatenate([q[:, D_NOPE * h:D_NOPE * (h + 1)], pe], axis=1).astype(BF16)
            k_ref[0, h] = jnp.concatenate([kv[:, D_NOPE * h:D_NOPE * (h + 1)], k_pe], axis=1).astype(BF16)
            v_ref[0, h] = kv[:, N_HEADS * D_NOPE + D_V * h:N_HEADS * D_NOPE + D_V * (h + 1)].astype(BF16)

    full = lambda a: pl.BlockSpec(a.shape, lambda i: (0,) * a.ndim)
    tab = pl.BlockSpec((ht, LANES), lambda i: (i % 2, 0))
    qk = pl.BlockSpec((1, N_HEADS, ht, 2 * LANES), lambda i: (i // 2, 0, i % 2, 0))
    return pl.pallas_call(
        body,
        name="qkv_fwd",
        grid=(2 * nb_seq,),
        in_specs=[pl.BlockSpec((ht, GRP_A), lambda i: (i, 0)), full(wq), full(wkv), full(gq), full(gkv), tab, tab, tab],
        out_specs=(qk, qk, pl.BlockSpec((1, N_HEADS, ht, D_V), lambda i: (i // 2, 0, i % 2, 0))),
        out_shape=(
            jax.ShapeDtypeStruct((nb_seq, N_HEADS, tp, 2 * LANES), BF16),
            jax.ShapeDtypeStruct((nb_seq, N_HEADS, tp, 2 * LANES), BF16),
            jax.ShapeDtypeStruct((nb_seq, N_HEADS, tp, D_V), BF16),
        ),
        compiler_params=_params("parallel"),
    )(p, wq, wkv, gq, gkv, *tables)


def _attn_fwd(q, k, v, p, g_attn):
    nb_seq, _, tp, _ = q.shape

    def body(q_ref, k_ref, v_ref, z_ref, g_ref, y_ref, o_ref, lse_ref):
        g = g_ref[...]
        for r0 in range(0, tp, Q_TILE):
            nq = min(Q_TILE, tp - r0)
            kend = r0 + nq
            qq = q_ref[0, 0, r0:kend, :]
            sd = _dot(qq, k_ref[0, 0, r0:kend, :], _NT)
            causal = (lax.broadcasted_iota(jnp.int32, (nq, nq), 1) <= lax.broadcasted_iota(jnp.int32, (nq, nq), 0))
            sd = jnp.where(causal, sd, NEG_INF)
            m = jnp.max(sd, axis=-1, keepdims=True)
            if r0:
                so = _dot(qq, k_ref[0, 0, 0:r0, :], _NT)
                m = jnp.maximum(m, jnp.max(so, axis=-1, keepdims=True))
            ed = jnp.exp2(sd - m)
            l = jnp.sum(ed, axis=-1, keepdims=True)
            o = _dot(ed.astype(BF16), v_ref[0, 0, r0:kend, :])
            if r0:
                eo = jnp.exp2(so - m)
                l = l + jnp.sum(eo, axis=-1, keepdims=True)
                o = o + _dot(eo.astype(BF16), v_ref[0, 0, 0:r0, :])
            o = o * (1.0 / l)
            o_ref[0, 0, r0:kend, :] = o
            lse_ref[0, 0, r0:kend, :] = jnp.broadcast_to(m + jnp.log2(l), (nq, LANES))
            ohat, _ = _rms_stats(o)
            z = z_ref[r0:kend, :].astype(F32)
            y_ref[r0:kend, :] = (ohat * g * (z * _sigmoid(z))).astype(BF16)

    qk = pl.BlockSpec((1, 1, tp, 2 * LANES), lambda b, h: (b, h, 0, 0))
    hv = pl.BlockSpec((1, 1, tp, D_V), lambda b, h: (b, h, 0, 0))
    return pl.pallas_call(
        body,
        name="attn_fwd",
        grid=(nb_seq, N_HEADS),
        in_specs=[qk, qk, hv,
                  pl.BlockSpec((tp, LANES), lambda b, h: (b, GRP_A // LANES + h)),
                  pl.BlockSpec((1, LANES), lambda b, h: (0, h))],
        out_specs=(pl.BlockSpec((tp, LANES), lambda b, h: (b, h)), hv, hv),
        out_shape=(
            jax.ShapeDtypeStruct((nb_seq * tp, N_HEADS * D_V), BF16),
            jax.ShapeDtypeStruct((nb_seq, N_HEADS, tp, D_V), F32),
            jax.ShapeDtypeStruct((nb_seq, N_HEADS, tp, LANES), F32),
        ),
        compiler_params=_params("parallel", "parallel"),
    )(q, k, v, p, g_attn)


_CONV_COL0 = (GRP_A + N_HEADS * D_V) // LANES


def _conv_specs(tp, order):
    cols = CONV_WIDTH // LANES
    return [pl.BlockSpec((tp, LANES), functools.partial(
        lambda a, b, off: order(a, b, off), off=_CONV_COL0 + i * cols)) for i in range(4)]


def _conv_fwd(p, conv_w, g_conv, nb_seq, tp):
    def body(b_ref, c_ref, h_ref, z_ref, w_ref, g_ref, y_ref):
        cc = c_ref[...].astype(F32) * h_ref[...].astype(F32)
        row = lax.broadcasted_iota(jnp.int32, (tp, LANES), 0)
        s1 = jnp.where(row >= 1, pltpu.roll(cc, 1, 0), 0.0)
        s2 = jnp.where(row >= 2, pltpu.roll(cc, 2, 0), 0.0)
        yc = b_ref[...].astype(F32) * (w_ref[0:1, :] * s2 + w_ref[1:2, :] * s1 + w_ref[2:3, :] * cc)
        r = lax.rsqrt(_group_mean(yc * yc) + EPS)
        z = z_ref[...].astype(F32)
        y_ref[...] = (yc * r * g_ref[...] * (z * _sigmoid(z))).astype(BF16)

    return pl.pallas_call(
        body,
        name="conv_fwd",
        grid=(nb_seq, CONV_WIDTH // LANES),
        in_specs=_conv_specs(tp, lambda b, t, off: (b, off + t)) + [
            pl.BlockSpec((8, LANES), lambda b, t: (0, t)),
            pl.BlockSpec((1, LANES), lambda b, t: (0, t))],
        out_specs=pl.BlockSpec((tp, LANES), lambda b, t: (b, t)),
        out_shape=jax.ShapeDtypeStruct((nb_seq * tp, CONV_WIDTH), BF16),
        compiler_params=_params("parallel", "parallel"),
    )(p, p, p, p, conv_w, g_conv)


def _token_copy(hbm, b, k, ts, buf, sem, to_hbm=False):
    lo, hi = max(k * ts - LANES, 0), (k + 1) * ts - LANES
    off = lo - (k * ts - LANES)
    src, dst = hbm.at[b, pl.ds(lo, hi - lo)], buf.at[pl.ds(off, hi - lo)]
    if to_hbm:
        src, dst = dst, src
    return pltpu.make_async_copy(src, dst, sem)


def _for_tile(k, nt, fn):
    for kk in range(nt):
        @pl.when(k == kk)
        def _(kk=kk):
            fn(kk)


def _out_proj_loss(ya, yc, w_out, x, target, g_final, nt):
    nb_seq, s, d = x.shape
    r, ka = ya.shape
    ts = (s + LANES) // nt
    steps = nb_seq * nt

    def body(a_ref, c_ref, w_ref, x_hbm, t_hbm, g_ref, dhb_ref, dg_ref, loss_ref,
             xbuf, tbuf, acc_ref, sems):
        i = pl.program_id(0)
        b, k = i // nt, i % nt

        @pl.when(i == 0)
        def _():
            acc_ref[...] = jnp.zeros_like(acc_ref)
            dg_ref[...] = jnp.zeros_like(dg_ref)

        slot = i % 2

        def fetch(seq, kk, sl):
            return [_token_copy(x_hbm, seq, kk, ts, xbuf.at[sl], sems.at[sl, 0]),
                    _token_copy(t_hbm, seq, kk, ts, tbuf.at[sl], sems.at[sl, 1])]

        def start(seq, sl, kk):
            if kk == 0:
                xbuf[sl, 0:LANES, :] = jnp.zeros((LANES, d), F32)
                tbuf[sl, 0:LANES, :] = jnp.zeros((LANES, d), F32)
            for cp in fetch(seq, kk, sl):
                cp.start()

        @pl.when(i == 0)
        def _():
            start(0, 0, 0)

        @pl.when(i + 1 < steps)
        def _():
            _for_tile((i + 1) % nt, nt, functools.partial(start, (i + 1) // nt, 1 - slot))

        mix = _dot(a_ref[...], w_ref[0:ka, :]) + _dot(c_ref[...], w_ref[ka:, :])
        _for_tile(k, nt, lambda kk: [cp.wait() for cp in fetch(b, kk, slot)])

        real = (lax.broadcasted_iota(jnp.int32, (ts, d), 0) >= LANES) | (k > 0)
        g = g_ref[...]
        hhat, rstd = _rms_stats(xbuf[slot] + mix)
        e = jnp.where(real, hhat * g - tbuf[slot], 0.0)
        acc_ref[...] += jnp.sum(e * e, axis=0, keepdims=True)
        dy = e * (1.0 / d)
        dg_ref[...] += jnp.sum(dy * hhat, axis=0, keepdims=True)
        dhb_ref[...] = _rms_bwd(g * dy, hhat, rstd).astype(BF16)

        @pl.when(i == steps - 1)
        def _():
            total = jnp.sum(acc_ref[...], axis=1, keepdims=True)
            loss_ref[...] = jnp.broadcast_to((0.5 / d) * total, loss_ref.shape)

    hbm = pl.BlockSpec(memory_space=pl.ANY)
    row = pl.BlockSpec((ts, d), lambda i: (i, 0))
    vec = pl.BlockSpec((1, d), lambda i: (0, 0))
    return pl.pallas_call(
        body,
        name="out_proj_loss",
        grid=(steps,),
        in_specs=[pl.BlockSpec((ts, ka), lambda i: (i, 0)), pl.BlockSpec((ts, yc.shape[1]), lambda i: (i, 0)),
                  pl.BlockSpec(w_out.shape, lambda i: (0, 0)), hbm, hbm, vec],
        out_specs=(row, vec, pl.BlockSpec((1, LANES), lambda i: (0, 0))),
        out_shape=(
            jax.ShapeDtypeStruct((r, d), BF16),
            jax.ShapeDtypeStruct((1, d), F32),
            jax.ShapeDtypeStruct((1, LANES), F32),
        ),
        scratch_shapes=[pltpu.VMEM((2, ts, d), F32), pltpu.VMEM((2, ts, d), F32), pltpu.VMEM((1, d), F32),
                        pltpu.SemaphoreType.DMA((2, 2))],
        compiler_params=_params("arbitrary"),
    )(ya, yc, w_out, x, target, g_final)


def _out_proj_bwd(dhb, w_out, ya, yc, bm):
    r, d = dhb.shape
    ka = ya.shape[1]
    n_mix = w_out.shape[0]
    last = r // bm - 1

    def body(dh_ref, w_ref, a_ref, c_ref, dcat_ref, dw_ref, acc_ref):
        @pl.when(pl.program_id(0) == 0)
        def _():
            acc_ref[...] = jnp.zeros_like(acc_ref)

        dh = dh_ref[...]
        dcat_ref[...] = _dot(dh, w_ref[...], _NT).astype(BF16)
        acc_ref[0:ka, :] += _dot(a_ref[...], dh, _TN)
        acc_ref[ka:, :] += _dot(c_ref[...], dh, _TN)

        @pl.when(pl.program_id(0) == last)
        def _():
            dw_ref[...] = acc_ref[...].astype(BF16)

    return pl.pallas_call(
        body,
        name="out_proj_bwd",
        grid=(r // bm,),
        in_specs=[pl.BlockSpec((bm, d), lambda i: (i, 0)), pl.BlockSpec(w_out.shape, lambda i: (0, 0)),
                  pl.BlockSpec((bm, ka), lambda i: (i, 0)), pl.BlockSpec((bm, yc.shape[1]), lambda i: (i, 0))],
        out_specs=(pl.BlockSpec((bm, n_mix), lambda i: (i, 0)),
                   pl.BlockSpec((n_mix, d), lambda i: (0, 0))),
        out_shape=(jax.ShapeDtypeStruct((r, n_mix), BF16),
                   jax.ShapeDtypeStruct((n_mix, d), BF16)),
        scratch_shapes=[pltpu.VMEM((n_mix, d), F32)],
        compiler_params=_params("arbitrary"),
    )(dhb, w_out, ya, yc)


def _attn_bwd(q, k, v, o, lse, dcat, p, g_attn):
    nb_seq, _, tp, _ = q.shape

    def body(q_ref, k_ref, v_ref, o_ref, lse_ref, dy_ref, z_ref, g_ref,
             dq_ref, dk_ref, dv_ref, dz_ref, dg_ref, dq_acc):
        @pl.when(pl.program_id(1) == 0)
        def _():
            dg_ref[...] = jnp.zeros_like(dg_ref)

        g = g_ref[...]
        z = z_ref[...].astype(F32)
        o = o_ref[0, 0]
        dy = dy_ref[...].astype(F32)
        sig = _sigmoid(z)
        ohat, r = _rms_stats(o)
        don = dy * (z * sig)
        dz_ref[...] = (dy * (ohat * g) * (sig * (1.0 + z * (1.0 - sig)))).astype(BF16)
        dg_ref[...] += jnp.sum(don * ohat, axis=0, keepdims=True)
        do = _rms_bwd(g * don, ohat, r)
        dvec = jnp.sum(do * o, axis=-1, keepdims=True)
        dob = do.astype(BF16)
        lse_col = lse_ref[0, 0, :, 0:1]
        dq_acc[...] = jnp.zeros_like(dq_acc)
        for k0 in range(0, tp, K_TILE):
            nk = min(K_TILE, tp - k0)
            nq = tp - k0
            qq = q_ref[0, 0, k0:, :]
            kk = k_ref[0, 0, k0:k0 + nk, :]
            causal = (lax.broadcasted_iota(jnp.int32, (nq, nk), 1) <= lax.broadcasted_iota(jnp.int32, (nq, nk), 0))
            pr = jnp.where(causal, jnp.exp2(_dot(qq, kk, _NT) - lse_col[k0:]), 0.0)
            dp = _dot(dob[k0:], v_ref[0, 0, k0:k0 + nk, :], _NT)
            ds = (pr * (dp - dvec[k0:])).astype(BF16)
            dv_ref[0, 0, k0:k0 + nk, :] = _dot(pr.astype(BF16), dob[k0:], _TN).astype(BF16)
            dk_ref[0, 0, k0:k0 + nk, :] = (_dot(ds, qq, _TN) * (ATTN_SCALE / Q_SCALE)).astype(BF16)
            dq_acc[k0:, :] += _dot(ds, kk)
        dq_ref[0, 0] = (dq_acc[...] * ATTN_SCALE).astype(BF16)

    qk = pl.BlockSpec((1, 1, tp, 2 * LANES), lambda h, b: (b, h, 0, 0))
    hv = pl.BlockSpec((1, 1, tp, D_V), lambda h, b: (b, h, 0, 0))
    col = pl.BlockSpec((tp, LANES), lambda h, b: (b, h))
    return pl.pallas_call(
        body,
        name="attn_bwd",
        grid=(N_HEADS, nb_seq),
        in_specs=[qk, qk, hv, hv, hv, col,
                  pl.BlockSpec((tp, LANES), lambda h, b: (b, GRP_A // LANES + h)),
                  pl.BlockSpec((1, LANES), lambda h, b: (0, h))],
        out_specs=(qk, qk, hv, col, pl.BlockSpec((1, LANES), lambda h, b: (0, h))),
        out_shape=(
            jax.ShapeDtypeStruct((nb_seq, N_HEADS, tp, 2 * LANES), BF16),
            jax.ShapeDtypeStruct((nb_seq, N_HEADS, tp, 2 * LANES), BF16),
            jax.ShapeDtypeStruct((nb_seq, N_HEADS, tp, D_V), BF16),
            jax.ShapeDtypeStruct((nb_seq * tp, N_HEADS * D_V), BF16),
            jax.ShapeDtypeStruct((1, N_HEADS * D_V), F32),
        ),
        scratch_shapes=[pltpu.VMEM((tp, 2 * LANES), F32)],
        compiler_params=_params("arbitrary", "arbitrary"),
    )(q, k, v, o, lse, dcat, p, g_attn)


def _qkv_bwd(p, dq, dk, dv, wq, wkv, gq, gkv, tables):
    nb_seq, _, tp, _ = dq.shape
    ht = tp // 2

    def body(pa_ref, dq_ref, dk_ref, dv_ref, wq_ref, wkv_ref, gq_ref, gkv_ref, cos_ref, sa_ref, sb_ref,
             dpa_ref, dwq_ref, dwkv_ref, dgq_ref, dgkv_ref):
        @pl.when(pl.program_id(0) == 0)
        def _():
            dwq_ref[...] = jnp.zeros_like(dwq_ref)
            dwkv_ref[...] = jnp.zeros_like(dwkv_ref)
            dgq_ref[...] = jnp.zeros_like(dgq_ref)
            dgkv_ref[...] = jnp.zeros_like(dgkv_ref)

        pa = pa_ref[...].astype(F32)
        gq, gkv = gq_ref[...], gkv_ref[...]
        cq_hat, rq = _rms_stats(pa[:, :Q_RANK])
        ckv_hat, rkv = _rms_stats(pa[:, Q_RANK:Q_RANK + KV_RANK])
        tabs = (cos_ref[...], sa_ref[...], sb_ref[...])

        pe = [dq_ref[0, h, :, D_NOPE:].astype(F32) for h in range(N_HEADS)]
        pairs = [_rope_t(pe[2 * i] + pltpu.roll(pe[2 * i + 1], D_ROPE, 1), *tabs).astype(BF16) for i in range(2)]
        dq_flat = jnp.concatenate([dq_ref[0, h, :, :D_NOPE] for h in range(N_HEADS)] + pairs, axis=1)
        dwq_ref[...] += _dot(dq_flat, (cq_hat * gq).astype(BF16), _TN)
        dcqn = _dot(dq_flat, wq_ref[...])
        dgq_ref[...] += jnp.sum(dcqn * cq_hat, axis=0, keepdims=True)
        dcq = _rms_bwd(gq * dcqn, cq_hat, rq)

        dkv_flat = jnp.concatenate([dk_ref[0, h, :, :D_NOPE] for h in range(N_HEADS)]
                                   + [dv_ref[0, h] for h in range(N_HEADS)], axis=1)
        dwkv_ref[...] += _dot((ckv_hat * gkv).astype(BF16), dkv_flat, _TN)
        dckvn = _dot(dkv_flat, wkv_ref[...], _NT)
        dgkv_ref[...] += jnp.sum(dckvn * ckv_hat, axis=0, keepdims=True)
        dckv = _rms_bwd(gkv * dckvn, ckv_hat, rkv)

        dk_pe = dk_ref[0, 0, :, D_NOPE:].astype(F32)
        for h in range(1, N_HEADS):
            dk_pe = dk_pe + dk_ref[0, h, :, D_NOPE:].astype(F32)
        dk_pe = jnp.where(lax.broadcasted_iota(jnp.int32, (ht, LANES), 1) < D_ROPE, dk_pe, 0.0)
        dpa_ref[...] = jnp.concatenate([dcq, dckv, _rope_t(dk_pe, *tabs)], axis=1).astype(BF16)

    full = lambda a: pl.BlockSpec(a.shape, lambda i: (0,) * a.ndim)
    tab = pl.BlockSpec((ht, LANES), lambda i: (i % 2, 0))
    qk = pl.BlockSpec((1, N_HEADS, ht, 2 * LANES), lambda i: (i // 2, 0, i % 2, 0))
    acc = lambda shape: pl.BlockSpec(shape, lambda i: (0, 0))
    return pl.pallas_call(
        body,
        name="qkv_bwd",
        grid=(2 * nb_seq,),
        in_specs=[pl.BlockSpec((ht, GRP_A), lambda i: (i, 0)), qk, qk,
                  pl.BlockSpec((1, N_HEADS, ht, D_V), lambda i: (i // 2, 0, i % 2, 0)),
                  full(wq), full(wkv), full(gq), full(gkv), tab, tab, tab],
        out_specs=(pl.BlockSpec((ht, GRP_A), lambda i: (i, 0)),
                   acc(wq.shape), acc(wkv.shape), acc((1, Q_RANK)), acc((1, KV_RANK))),
        out_shape=(
            jax.ShapeDtypeStruct((nb_seq * tp, GRP_A), BF16),
            jax.ShapeDtypeStruct(wq.shape, F32),
            jax.ShapeDtypeStruct(wkv.shape, F32),
            jax.ShapeDtypeStruct((1, Q_RANK), F32),
            jax.ShapeDtypeStruct((1, KV_RANK), F32),
        ),
        compiler_params=_params("arbitrary"),
    )(p, dq, dk, dv, wq, wkv, gq, gkv, *tables)


def _conv_bwd(p, dcat, conv_w, g_conv, nb_seq, tp):
    cols = CONV_WIDTH // LANES

    def body(b_ref, c_ref, h_ref, z_ref, dy_ref, w_ref, g_ref,
             db_ref, dc_ref, dh_ref, dz_ref, dw_ref, dg_ref):
        @pl.when(pl.program_id(1) == 0)
        def _():
            dw_ref[...] = jnp.zeros_like(dw_ref)
            dg_ref[...] = jnp.zeros_like(dg_ref)

        cb, c, h = b_ref[...].astype(F32), c_ref[...].astype(F32), h_ref[...].astype(F32)
        z, dy = z_ref[...].astype(F32), dy_ref[...].astype(F32)
        g = g_ref[...]
        w0, w1, w2 = w_ref[0:1, :], w_ref[1:2, :], w_ref[2:3, :]
        cc = c * h
        row = lax.broadcasted_iota(jnp.int32, (tp, LANES), 0)
        s1 = jnp.where(row >= 1, pltpu.roll(cc, 1, 0), 0.0)
        s2 = jnp.where(row >= 2, pltpu.roll(cc, 2, 0), 0.0)
        dwc = w0 * s2 + w1 * s1 + w2 * cc
        yc = cb * dwc
        r = lax.rsqrt(_group_mean(yc * yc) + EPS)
        ychat = yc * r
        sig = _sigmoid(z)
        dz_ref[...] = (dy * (ychat * g) * (sig * (1.0 + z * (1.0 - sig)))).astype(BF16)
        dyn = dy * (z * sig)
        dg_ref[...] += jnp.sum(dyn * ychat, axis=0, keepdims=True)
        gd = g * dyn
        dyc = r * (gd - ychat * _group_mean(gd * ychat))
        db_ref[...] = (dyc * dwc).astype(BF16)
        ddw = dyc * cb
        dw_ref[0:1, :] += jnp.sum(ddw * s2, axis=0, keepdims=True)
        dw_ref[1:2, :] += jnp.sum(ddw * s1, axis=0, keepdims=True)
        dw_ref[2:3, :] += jnp.sum(ddw * cc, axis=0, keepdims=True)
        u1 = jnp.where(row <= tp - 2, pltpu.roll(ddw, tp - 1, 0), 0.0)
        u2 = jnp.where(row <= tp - 3, pltpu.roll(ddw, tp - 2, 0), 0.0)
        dcc = w2 * ddw + w1 * u1 + w0 * u2
        dc_ref[...] = (dcc * h).astype(BF16)
        dh_ref[...] = (dcc * c).astype(BF16)

    col = pl.BlockSpec((tp, LANES), lambda t, b: (b, t))
    out = jax.ShapeDtypeStruct((nb_seq * tp, CONV_WIDTH), BF16)
    return pl.pallas_call(
        body,
        name="conv_bwd",
        grid=(cols, nb_seq),
        in_specs=_conv_specs(tp, lambda t, b, off: (b, off + t)) + [
            pl.BlockSpec((tp, LANES), lambda t, b: (b, N_HEADS * D_V // LANES + t)),
            pl.BlockSpec((8, LANES), lambda t, b: (0, t)),
            pl.BlockSpec((1, LANES), lambda t, b: (0, t))],
        out_specs=(col, col, col, col,
                   pl.BlockSpec((8, LANES), lambda t, b: (0, t)), pl.BlockSpec((1, LANES), lambda t, b: (0, t))),
        out_shape=(out, out, out, out,
                   jax.ShapeDtypeStruct((8, CONV_WIDTH), F32), jax.ShapeDtypeStruct((1, CONV_WIDTH), F32)),
        compiler_params=_params("arbitrary", "arbitrary"),
    )(p, p, p, p, dcat, conv_w, g_conv)


def _input_bwd(dps, w_in, x, meta, dh, norm_g, nt, send_in):
    nb_seq, s, d = x.shape
    r, kb = dps[0].shape
    ts = (s + LANES) // nt
    steps = nb_seq * nt
    n_dp = len(dps)
    in_slot = send_in.shape[1:]

    def body(*refs):
        dp_refs, w_ref, x_hbm, meta_ref, dh_ref, g_ref, pay_ref = refs[:n_dp], *refs[n_dp:n_dp + 6]
        o = n_dp + 6
        gx_hbm, dmeta_ref, dg_ref, r2_in = refs[o:o + 4]
        xbuf, gxbuf, tok_sems, own_in, r1_in, sum_in = refs[o + 4:o + 10]
        sems = refs[o + 10:]
        i = pl.program_id(0)
        b, k = i // nt, i % nt

        def plan():
            return _reduce_plan((pay_ref,), (own_in,), (r1_in,), (sum_in,), (r2_in,), *sems)

        @pl.when(i == 0)
        def _():
            dmeta_ref[...] = jnp.zeros_like(dmeta_ref)
            dg_ref[...] = jnp.zeros_like(dg_ref)
            plan()[0]()

        @pl.when(i == 1)
        def _():
            plan()[1]()

        def start(kk):
            if kk == 0:
                xbuf[0:PAD_FRONT, :] = jnp.zeros((PAD_FRONT, d), F32)
                xbuf[PAD_FRONT:LANES, :] = meta_ref[...]
            _token_copy(x_hbm, b, kk, ts, xbuf, tok_sems.at[0]).start()

        _for_tile(k, nt, start)
        du = _dot(dp_refs[0][...], w_ref[0:kb, :])
        for j in range(1, n_dp):
            du = du + _dot(dp_refs[j][...], w_ref[kb * j:kb * (j + 1), :])
        _for_tile(k, nt, lambda kk: _token_copy(x_hbm, b, kk, ts, xbuf, tok_sems.at[0]).wait())

        g = g_ref[...]
        hhat, rstd = _rms_stats(xbuf[...])
        dg_ref[...] += jnp.sum(du * hhat, axis=0, keepdims=True)
        res = _rms_bwd(g * du, hhat, rstd) + dh_ref[...].astype(F32)

        @pl.when(i > 0)
        def _():
            _for_tile(k, nt, lambda kk: _token_copy(gx_hbm, b, (kk - 1) % nt, ts, gxbuf, tok_sems.at[1], True).wait())

        gxbuf[...] = res

        @pl.when(k == 0)
        def _():
            dmeta_ref[...] += gxbuf[PAD_FRONT:LANES, :]

        _for_tile(k, nt, lambda kk: _token_copy(gx_hbm, b, kk, ts, gxbuf, tok_sems.at[1], True).start())

        @pl.when(i == steps - 1)
        def _():
            _token_copy(gx_hbm, b, nt - 1, ts, gxbuf, tok_sems.at[1], True).wait()
            plan()[2]()

    whole = lambda a: pl.BlockSpec(a.shape, lambda i: (0,) * a.ndim)
    hbm = pl.BlockSpec(memory_space=pl.ANY)
    return pl.pallas_call(
        body,
        name="input_bwd",
        grid=(steps,),
        in_specs=[pl.BlockSpec((ts, kb), lambda i: (i, 0)) for _ in dps]
        + [whole(w_in), hbm, whole(meta), pl.BlockSpec((ts, d), lambda i: (i, 0)), whole(norm_g), hbm],
        out_specs=(hbm, pl.BlockSpec((N_META, d), lambda i: (0, 0)), pl.BlockSpec((1, d), lambda i: (0, 0)), hbm),
        out_shape=(jax.ShapeDtypeStruct((nb_seq, s, d), F32),
                   jax.ShapeDtypeStruct((N_META, d), F32),
                   jax.ShapeDtypeStruct((1, d), F32),
                   jax.ShapeDtypeStruct((N_CHIPS,) + in_slot, BF16)),
        scratch_shapes=[pltpu.VMEM((ts, d), F32), pltpu.VMEM((ts, d), F32), pltpu.SemaphoreType.DMA((2,))]
        + _reduce_scratch([(in_slot, BF16)], [True]),
        compiler_params=_params("arbitrary"),
    )(*dps, w_in, x, meta, dh, norm_g, send_in)


def _in_proj_bwd_w(u, dps, bm, small_grads, send_out):
    r, d = u.shape
    kb = dps[0].shape[1]
    steps = r // bm
    n_dp, n_small = len(dps), len(small_grads)
    out_slot, small_slot = send_out.shape[1:], (SMALL_ROWS, LANES)

    def body(*refs):
        u_ref, dp_refs = refs[0], refs[1:1 + n_dp]
        small_refs = refs[1 + n_dp:1 + n_dp + n_small]
        o = 1 + n_dp + n_small
        pay_out, o_ref, r2_out, r2_small = refs[o:o + 4]
        acc_ref, ssmall, r1_out, sum_out, r1_small, sum_small = refs[o + 4:o + 10]
        sems = refs[o + 10:]
        i = pl.program_id(0)

        def plan():
            return _reduce_plan((pay_out, ssmall), (None, None), (r1_out, r1_small), (sum_out, sum_small),
                                (r2_out, r2_small), *sems)

        @pl.when(i == 0)
        def _():
            acc_ref[...] = jnp.zeros_like(acc_ref)
            _pack_small(ssmall, *small_refs)
            plan()[0]()

        @pl.when(i == 1)
        def _():
            plan()[1]()

        uu = u_ref[...]
        for j in range(n_dp):
            acc_ref[kb * j:kb * (j + 1), :] += _dot(dp_refs[j][...], uu, _TN)

        @pl.when(i == steps - 1)
        def _():
            for k in range(N_DEV):
                for s, e, c0 in _in_pieces(k):
                    o_ref[k, s:e, :] = acc_ref[c0:c0 + e - s, :].astype(BF16)
                o_ref[k, SHARD_IN:, :] = jnp.zeros((SHARD_IN_PAD - SHARD_IN, d), BF16)
            plan()[2]()

    whole = lambda a: pl.BlockSpec(a.shape, lambda i: (0,) * a.ndim)
    hbm = pl.BlockSpec(memory_space=pl.ANY)
    return pl.pallas_call(
        body,
        name="in_proj_bwd_w",
        grid=(steps,),
        in_specs=[pl.BlockSpec((bm, d), lambda i: (i, 0))]
        + [pl.BlockSpec((bm, kb), lambda i: (i, 0)) for _ in dps] + [whole(a) for a in small_grads]
        + [whole(send_out)],
        out_specs=(pl.BlockSpec((N_DEV, SHARD_IN_PAD, d), lambda i: (0, 0, 0)), hbm, hbm),
        out_shape=(jax.ShapeDtypeStruct((N_DEV, SHARD_IN_PAD, d), BF16),
                   jax.ShapeDtypeStruct((N_CHIPS,) + out_slot, BF16),
                   jax.ShapeDtypeStruct((N_CHIPS,) + small_slot, F32)),
        scratch_shapes=[pltpu.VMEM((kb * n_dp, d), F32), pltpu.VMEM((N_DEV,) + small_slot, F32)]
        + _reduce_scratch([(out_slot, BF16), (small_slot, F32)], [False, False]),
        compiler_params=_params("arbitrary"),
    )(u, *dps, *small_grads, send_out)


def _local_step(x, loss_target, u, p, meta_f, norm_g, w_in_p, q_norm_g, w_q_p, kv_norm_g, w_kv_p, conv_w_f,
                attn_out_g, conv_out_g, w_out_f, g_final):
    nb_seq, s, d = x.shape
    tp = s + LANES
    ht = tp // 2
    tables = _rope_tables(tp)

    q, k, v = _qkv_fwd(p, w_q_p, w_kv_p, q_norm_g, kv_norm_g, tables, nb_seq, tp)
    ya, o, lse = _attn_fwd(q, k, v, p, attn_out_g)
    yc = _conv_fwd(p, conv_w_f, conv_out_g, nb_seq, tp)
    dhb, d_final_g, loss_part = _out_proj_loss(ya, yc, w_out_f, x, loss_target, g_final, TOKEN_TILES)

    dcat, d_w_out = _out_proj_bwd(dhb, w_out_f, ya, yc, ht)
    send_out = d_w_out.reshape(N_DEV, SHARD_OUT, d)
    dq, dk, dv, dz_attn, d_attn_g = _attn_bwd(q, k, v, o, lse, dcat, p, attn_out_g)
    dpa, d_wq_p, d_wkv_p, d_gq, d_gkv = _qkv_bwd(p, dq, dk, dv, w_q_p, w_kv_p, q_norm_g, kv_norm_g, tables)
    d_b, d_c, d_h, dz_conv, d_conv_w, d_conv_g = _conv_bwd(p, dcat, conv_w_f, conv_out_g, nb_seq, tp)
    dps = (dpa, dz_attn, d_b, d_c, d_h, dz_conv)
    small = (d_wq_p, d_wkv_p, d_conv_w, d_final_g, d_gq, d_gkv, d_attn_g, d_conv_g, loss_part)
    send_in, r_out, r_small = _in_proj_bwd_w(u, dps, ht, small, send_out)
    grad_x, d_meta, d_norm_g, r_in = _input_bwd(dps, w_in_p, x, meta_f, dhb, norm_g, TOKEN_TILES, send_in)
    return grad_x, r_in, r_out, r_small, d_meta, d_norm_g


def kernel(x, meta_tokens, norm_g, w_in, q_norm_g, w_q_up, kv_norm_g, w_kv_up, conv_w, attn_out_g, conv_out_g, w_out, final_norm_g, loss_target, m_meta_tokens, m_norm_g, m_w_in, m_q_norm_g, m_w_q_up, m_kv_norm_g, m_w_kv_up, m_conv_w, m_attn_out_g, m_conv_out_g, m_w_out, m_final_norm_g, v_meta_tokens, v_norm_g, v_w_in, v_q_norm_g, v_w_q_up, v_kv_norm_g, v_w_kv_up, v_conv_w, v_attn_out_g, v_conv_out_g, v_w_out, v_final_norm_g):
    d = x.shape[-1]
    ht = (x.shape[1] + LANES) // 2
    u, w_in_p, meta_f = _prep_gather(x, meta_tokens, norm_g, w_in[0].T)
    p, w_q_p, w_kv_p, w_out_f, conv_w_f = _in_proj_gather(
        u, w_in_p, w_q_up[0].T, w_kv_up[0], w_out[0], conv_w.transpose(1, 0, 2), ht, 3 * GRP_A)
    g_final = final_norm_g.reshape(1, d)
    grad_x, r_in, r_out, r_small, d_meta, d_norm_g = _local_step(
        x, loss_target, u, p, meta_f, norm_g, w_in_p, q_norm_g, w_q_p, kv_norm_g, w_kv_p, conv_w_f,
        attn_out_g, conv_out_g, w_out_f, g_final)

    flat = lambda a: a.reshape(a.shape[-2:]) if a.ndim == 3 else a.reshape(1, -1) if a.ndim == 1 else a
    transposed = ("w_in", "w_q_up")

    def to_kernel(n, a):
        if n == "conv_w":
            return a.transpose(1, 0, 2)
        return flat(a).T if n in transposed else flat(a)

    def from_kernel(n, a, shape):
        if n == "conv_w":
            return a.transpose(1, 0, 2)
        return (a.T if n in transposed else a).reshape(shape)
    params = {
        "meta_tokens": (meta_tokens, m_meta_tokens, v_meta_tokens),
        "norm_g": (norm_g, m_norm_g, v_norm_g),
        "w_in": (w_in, m_w_in, v_w_in),
        "q_norm_g": (q_norm_g, m_q_norm_g, v_q_norm_g),
        "w_q_up": (w_q_up, m_w_q_up, v_w_q_up),
        "kv_norm_g": (kv_norm_g, m_kv_norm_g, v_kv_norm_g),
        "w_kv_up": (w_kv_up, m_w_kv_up, v_w_kv_up),
        "conv_w": (conv_w, m_conv_w, v_conv_w),
        "attn_out_g": (attn_out_g, m_attn_out_g, v_attn_out_g),
        "conv_out_g": (conv_out_g, m_conv_out_g, v_conv_out_g),
        "w_out": (w_out, m_w_out, v_w_out),
        "final_norm_g": (final_norm_g, m_final_norm_g, v_final_norm_g),
    }
    grads, loss = _reduce_tail(r_in, r_out, r_small, d_meta, d_norm_g)
    updated = _adamw(grads, {n: tuple(to_kernel(n, a) for a in t) for n, t in params.items()})
    outs = [[from_kernel(n, updated[n][i], params[n][0].shape) for n, _ in PARAM_SHAPES] for i in range(4)]
    return (loss[0, 0], grad_x, *outs[0], *outs[1], *outs[2], *outs[3])
```

```python
import functools

import jax
import jax.numpy as jnp
from jax import lax
from jax.experimental import pallas as pl
from jax.experimental.pallas import tpu as pltpu

F32 = jnp.float32
BF16 = jnp.bfloat16

N_META = 16
D_MODEL = 1024
N_HEADS = 4
D_NOPE = 128
D_ROPE = 64
D_V = 128
Q_RANK = 256
KV_RANK = 128
CONV_WIDTH = 512
CONV_GROUP = 64
ROPE_THETA = 10000.0
ATTN_SCALE = (D_NOPE + D_ROPE) ** -0.5
Q_SCALE = ATTN_SCALE * 1.4426950408889634
EPS = 1e-6
NEG_INF = -1e30

ADAM_LR = 0.001
ADAM_B1 = 0.9
ADAM_B2 = 0.999
ADAM_EPS = 1e-08
ADAM_WD = 0.01
ADAM_STEP = 10

LANES = 128
PAD_FRONT = LANES - N_META
K_TILE = 256
Q_TILE = 512
N_DEV = 8
VMEM_LIMIT = 56 * 1024 * 1024

IN_PAD = 3072
GRP_A = 512
N_A = Q_RANK + KV_RANK + D_ROPE
IN_PROJ = 3008
SHARD_IN = IN_PROJ // N_DEV
SHARD_IN_PAD = 384
SHARD_Q = 96
SHARD_KV = 128
SHARD_OUT = 128
SHARD_CONV = 64
SHARD_META = 128
Q_COLS = N_HEADS * (D_NOPE + D_ROPE)
KV_COLS = N_HEADS * (D_NOPE + D_V)

ROW_Q, ROW_KV, ROW_META, ROW_CONV = 0, 256, 384, 400
ROW_REPL = 408
ROW_NORM, ROW_FINAL, ROW_GQ, ROW_GKV, ROW_ATTN, ROW_CONVG, ROW_LOSS = 408, 416, 424, 426, 427, 431, 435
SMALL_ROWS = 440

PARAM_SHAPES = (
    ("meta_tokens", (N_META, SHARD_META)), ("norm_g", (1, D_MODEL)), ("w_in", (SHARD_IN, D_MODEL)),
    ("q_norm_g", (1, Q_RANK)), ("w_q_up", (SHARD_Q, Q_RANK)), ("kv_norm_g", (1, KV_RANK)),
    ("w_kv_up", (KV_RANK, SHARD_KV)), ("conv_w", (3, 1, SHARD_CONV)), ("attn_out_g", (1, CONV_WIDTH)),
    ("conv_out_g", (1, CONV_WIDTH)), ("w_out", (SHARD_OUT, D_MODEL)), ("final_norm_g", (1, D_MODEL)),
)


def _in_pieces(k):
    lo, hi = SHARD_IN * k, SHARD_IN * (k + 1)
    out = []
    if lo < N_A:
        out.append((0, min(hi, N_A) - lo, lo))
    if hi > N_A:
        s = max(lo, N_A)
        out.append((s - lo, hi - lo, s + GRP_A - N_A))
    return out


def _q_pieces(k):
    lo, hi = SHARD_Q * k, SHARD_Q * (k + 1)
    out = []
    for h in range(N_HEADS):
        base = (D_NOPE + D_ROPE) * h
        s, e = max(lo, base), min(hi, base + D_NOPE)
        if s < e:
            out.append((s - lo, e - lo, D_NOPE * h + s - base))
        s, e = max(lo, base + D_NOPE), min(hi, base + D_NOPE + D_ROPE)
        if s < e:
            out.append((s - lo, e - lo, N_HEADS * D_NOPE + D_ROPE * h + s - base - D_NOPE))
    return out


def _kv_dst(k):
    return D_NOPE * (k // 2) + (N_HEADS * D_NOPE if k % 2 else 0)


def _params(*sem):
    return pltpu.CompilerParams(dimension_semantics=sem, vmem_limit_bytes=VMEM_LIMIT)


def _rms_stats(x):
    r = lax.rsqrt(jnp.mean(x * x, axis=-1, keepdims=True) + EPS)
    return x * r, r


def _rms_bwd(gdy, xhat, r):
    return r * (gdy - xhat * jnp.mean(gdy * xhat, axis=-1, keepdims=True))


def _sigmoid(z):
    return 1.0 / (1.0 + jnp.exp(-z))


def _group_mean(x):
    i0 = lax.broadcasted_iota(jnp.int32, (LANES, LANES), 0) // CONV_GROUP
    i1 = lax.broadcasted_iota(jnp.int32, (LANES, LANES), 1) // CONV_GROUP
    m = jnp.where(i0 == i1, 1.0 / CONV_GROUP, 0.0).astype(BF16)
    hi = x.astype(BF16)
    lo = (x - hi.astype(F32)).astype(BF16)
    return jnp.dot(hi, m, preferred_element_type=F32) + jnp.dot(lo, m, preferred_element_type=F32)


_NT = (((1,), (1,)), ((), ()))
_TN = (((0,), (0,)), ((), ()))


def _dot(a, b, dims=None):
    if dims is None:
        return jnp.dot(a, b, preferred_element_type=F32)
    return lax.dot_general(a, b, dims, preferred_element_type=F32)


def _device_position():
    x, y, c = lax.axis_index("x"), lax.axis_index("y"), lax.axis_index("c")
    return x, y, c, 4 * x + 2 * y + c


def _gather_plan(srcs, slots, send_sems, recv_sems, local_sems):
    x, y, c, _ = _device_position()
    me, sibling = (x, y, c), (x, y, 1 - c)
    flip = lambda v, on: v + on - 2 * v * on
    near = (flip(x, 1 - c), flip(y, c))
    far = (flip(x, c), flip(y, 1 - c))
    diag = (1 - x, 1 - y)
    n = len(srcs)

    def slot(a, px, py, pc):
        return slots[a].at[4 * px + 2 * py + pc]

    def copy(a, k, block, to, own=False):
        return pltpu.make_async_remote_copy(
            src_ref=srcs[a] if own else slot(a, *block),
            dst_ref=slot(a, *block),
            send_sem=send_sems.at[7 * a + k],
            recv_sem=recv_sems.at[7 * a + k],
            device_id=to,
            device_id_type=pl.DeviceIdType.MESH,
        )

    def local(a):
        return pltpu.make_async_copy(srcs[a], slot(a, *me), local_sems.at[a])

    sent = [(me, sibling), (me, (*near, c)), (me, (*far, c)), ((*near, c), (*far, c)),
            ((*near, c), sibling), ((*far, c), sibling), ((*diag, c), sibling)]
    landed = [sibling, (*near, c), (*far, c), (*diag, c), (*far, 1 - c), (*near, 1 - c), (*diag, 1 - c)]

    def send(a, k):
        return copy(a, k, *sent[k], own=k < 3)

    def arrival(a, k):
        return copy(a, k, landed[k], me)

    def start():
        for a in range(n):
            local(a).start()
            for k in range(3):
                send(a, k).start()

    def relay():
        for a in range(n):
            arrival(a, 1).wait_recv()
            send(a, 3).start()
            send(a, 4).start()

    def forward():
        for k in (2, 3):
            for a in range(n):
                arrival(a, k).wait_recv()
                send(a, k + 3).start()

    def finish():
        for a in range(n):
            for k in (0, 4, 5, 6):
                arrival(a, k).wait_recv()
        for a in range(n):
            for k in range(7):
                send(a, k).wait_send()
            local(a).wait()

    return start, relay, forward, finish


def _adam_update(g, w, m, v):
    m_new = ADAM_B1 * m + (1.0 - ADAM_B1) * g
    v_new = ADAM_B2 * v + (1.0 - ADAM_B2) * (g * g)
    m_hat = m_new / (1.0 - ADAM_B1 ** ADAM_STEP)
    v_hat = v_new / (1.0 - ADAM_B2 ** ADAM_STEP)
    return -ADAM_LR * (m_hat / (jnp.sqrt(v_hat) + ADAM_EPS) + ADAM_WD * w), m_new, v_new


def _adamw(grads, params, grad_x):
    names = [n for n, _ in PARAM_SHAPES]
    n_p = len(names)

    def body(*refs):
        gx_in, gx_out, sem = refs[4 * n_p], refs[8 * n_p + 1], refs[8 * n_p + 2]
        passing = pltpu.make_async_copy(gx_in, gx_out, sem)
        passing.start()
        for i in range(n_p):
            g = refs[i][...]
            w, m, v = (refs[n_p + 3 * i + j][...] for j in range(3))
            delta, m_new, v_new = _adam_update(g, w, m, v)
            for j, val in enumerate((g, delta, m_new, v_new)):
                refs[4 * n_p + 1 + 4 * i + j][...] = val
        passing.wait()

    vm = pl.BlockSpec(memory_space=pltpu.VMEM)
    hbm = pl.BlockSpec(memory_space=pl.ANY)
    out_shape = []
    for _, shape in PARAM_SHAPES:
        out_shape += [jax.ShapeDtypeStruct(shape, F32)] * 4
    out_shape.append(jax.ShapeDtypeStruct(grad_x.shape, grad_x.dtype))
    outs = pl.pallas_call(
        body,
        name="adamw",
        out_shape=tuple(out_shape),
        in_specs=[vm] * (4 * n_p) + [hbm],
        out_specs=(vm,) * (4 * n_p) + (hbm,),
        scratch_shapes=[pltpu.SemaphoreType.DMA],
        compiler_params=pltpu.CompilerParams(vmem_limit_bytes=VMEM_LIMIT),
    )(*[grads[n] for n in names], *[a for n in names for a in params[n]], grad_x)
    return {n: outs[4 * i:4 * i + 4] for i, n in enumerate(names)}, outs[-1]


N_CHIPS = 4


def _reduce_plan(pays, owns, r1s, sums, r2s, send1, recv1, send2, recv2, local_sems):
    x, y, c, _ = _device_position()
    sibling = (x, y, 1 - c)
    chips = [((1 - x if rj & 2 else x), (1 - y if rj & 1 else y)) for rj in range(N_CHIPS)]
    n = len(pays)

    def slot_of(rj, core):
        return 4 * chips[rj][0] + 2 * chips[rj][1] + core

    def to_sibling(a, rj):
        return pltpu.make_async_remote_copy(
            src_ref=pays[a].at[slot_of(rj, 1 - c)], dst_ref=r1s[a].at[rj],
            send_sem=send1.at[N_CHIPS * a + rj], recv_sem=recv1.at[N_CHIPS * a + rj],
            device_id=sibling, device_id_type=pl.DeviceIdType.MESH)

    def load_own(a, rj):
        return pltpu.make_async_copy(pays[a].at[slot_of(rj, c)], owns[a].at[rj], local_sems.at[2 * N_CHIPS * a + rj])

    def to_chip(a, rj):
        return pltpu.make_async_remote_copy(
            src_ref=sums[a].at[rj], dst_ref=r2s[a].at[rj],
            send_sem=send2.at[N_CHIPS * a + rj], recv_sem=recv2.at[N_CHIPS * a + rj],
            device_id=(*chips[rj], c), device_id_type=pl.DeviceIdType.MESH)

    def keep(a):
        return pltpu.make_async_copy(sums[a].at[0], r2s[a].at[0], local_sems.at[2 * N_CHIPS * a + N_CHIPS])

    def start():
        for a in range(n):
            for rj in range(N_CHIPS):
                to_sibling(a, rj).start()
                if owns[a] is not None:
                    load_own(a, rj).start()

    def combine():
        for a in range(n):
            for rj in range(N_CHIPS):
                to_sibling(a, rj).wait_recv()
                if owns[a] is not None:
                    load_own(a, rj).wait()
                    mine = owns[a][rj]
                else:
                    mine = pays[a][slot_of(rj, c)]
                sums[a][rj] = (mine.astype(F32) + r1s[a][rj].astype(F32)).astype(sums[a].dtype)
            keep(a).start()
            for rj in range(1, N_CHIPS):
                to_chip(a, rj).start()

    def finish():
        for a in range(n):
            for rj in range(1, N_CHIPS):
                to_chip(a, rj).wait_recv()
            for rj in range(N_CHIPS):
                to_sibling(a, rj).wait_send()
            for rj in range(1, N_CHIPS):
                to_chip(a, rj).wait_send()
            keep(a).wait()

    return start, combine, finish


def _reduce_scratch(shapes_dtypes, own_flags):
    out = []
    for (shape, dtype), own in zip(shapes_dtypes, own_flags):
        if own:
            out.append(pltpu.VMEM((N_CHIPS,) + shape, dtype))
        out += [pltpu.VMEM((N_CHIPS,) + shape, dtype), pltpu.VMEM((N_CHIPS,) + shape, dtype)]
    n = len(shapes_dtypes)
    out += [pltpu.SemaphoreType.DMA((N_CHIPS * n,))] * 4 + [pltpu.SemaphoreType.DMA((2 * N_CHIPS * n,))]
    return out


def _pack_small(ssmall, dwq, dwkv, dconv, dfinal, dgq, dgkv, dattn, dconvg, loss_part):
    ssmall[...] = jnp.zeros_like(ssmall)
    rep = ssmall.at[0]
    for i in range(D_MODEL // LANES):
        rep[ROW_FINAL + i:ROW_FINAL + i + 1, :] = dfinal[:, LANES * i:LANES * (i + 1)]
    for i in range(Q_RANK // LANES):
        rep[ROW_GQ + i:ROW_GQ + i + 1, :] = dgq[:, LANES * i:LANES * (i + 1)]
    rep[ROW_GKV:ROW_GKV + 1, :] = dgkv[...]
    for i in range(CONV_WIDTH // LANES):
        rep[ROW_ATTN + i:ROW_ATTN + i + 1, :] = dattn[:, LANES * i:LANES * (i + 1)]
        rep[ROW_CONVG + i:ROW_CONVG + i + 1, :] = dconvg[:, LANES * i:LANES * (i + 1)]
    rep[ROW_LOSS:ROW_LOSS + 1, :] = loss_part[...]
    for k in range(N_DEV):
        if k:
            ssmall[k, ROW_REPL:, :] = ssmall[0, ROW_REPL:, :]
        for s, e, d in _q_pieces(k):
            for i in range(Q_RANK // LANES):
                ssmall[k, ROW_Q + SHARD_Q * i + s:ROW_Q + SHARD_Q * i + e, :] = dwq[d:d + e - s, LANES * i:LANES * (i + 1)]
        ssmall[k, ROW_KV:ROW_KV + KV_RANK, :] = dwkv[:, _kv_dst(k):_kv_dst(k) + SHARD_KV]
        ssmall[k, ROW_CONV:ROW_CONV + 3, 0:SHARD_CONV] = dconv[0:3, SHARD_CONV * k:SHARD_CONV * (k + 1)]


TOKEN_TILES = 4
TAIL_ROWS = N_META + D_MODEL // LANES


def _reduce_tail(r_in, r_out, r_small, d_meta, d_norm):
    n_p = len(PARAM_SHAPES)
    names = [n for n, _ in PARAM_SHAPES]

    def body(*refs):
        rin, rout, rsmall, dmeta, dnorm = refs[:5]
        g_out = {n: refs[5 + i] for i, n in enumerate(names)}
        loss_out = refs[5 + n_p]
        stail, rtail, gsum, gtail, send_sems, recv_sems = refs[6 + n_p:]
        x, y, c, me = _device_position()
        my_chip = 2 * x + y

        for k in range(N_DEV):
            stail[k, 0:N_META, :] = dmeta[:, SHARD_META * k:SHARD_META * (k + 1)]
            for i in range(D_MODEL // LANES):
                stail[k, N_META + i:N_META + i + 1, :] = dnorm[:, LANES * i:LANES * (i + 1)]
        copies = []
        for r in range(1, N_DEV):
            peer = (1 - x if r & 4 else x, 1 - y if r & 2 else y, 1 - c if r & 1 else c)
            copies.append(pltpu.make_async_remote_copy(
                src_ref=stail.at[4 * peer[0] + 2 * peer[1] + peer[2]],
                dst_ref=rtail.at[r],
                send_sem=send_sems.at[r - 1],
                recv_sem=recv_sems.at[r - 1],
                device_id=peer,
                device_id_type=pl.DeviceIdType.MESH,
            ))
        for cp in copies:
            cp.start()
        rtail[0] = stail[me]

        g = rin[my_chip].astype(F32)
        for ch in range(1, N_CHIPS):
            g = g + rin[ch ^ my_chip].astype(F32)
        g_out["w_in"][...] = g[:SHARD_IN, :]

        g = rout[my_chip].astype(F32)
        gs = rsmall[my_chip]
        for ch in range(1, N_CHIPS):
            g = g + rout[ch ^ my_chip].astype(F32)
            gs = gs + rsmall[ch ^ my_chip]
        g_out["w_out"][...] = g
        gsum[...] = gs
        for i in range(Q_RANK // LANES):
            g_out["w_q_up"][:, LANES * i:LANES * (i + 1)] = gsum[ROW_Q + SHARD_Q * i:ROW_Q + SHARD_Q * (i + 1), :]
        g_out["w_kv_up"][...] = gsum[ROW_KV:ROW_KV + KV_RANK, :]
        for i in range(3):
            g_out["conv_w"][i] = gsum[ROW_CONV + i:ROW_CONV + i + 1, 0:SHARD_CONV]
        for name, row, width in (("final_norm_g", ROW_FINAL, D_MODEL), ("q_norm_g", ROW_GQ, Q_RANK),
                                 ("kv_norm_g", ROW_GKV, KV_RANK), ("attn_out_g", ROW_ATTN, CONV_WIDTH),
                                 ("conv_out_g", ROW_CONVG, CONV_WIDTH)):
            for i in range(width // LANES):
                g_out[name][:, LANES * i:LANES * (i + 1)] = gsum[row + i:row + i + 1, :]
        loss_out[...] = gsum[ROW_LOSS:ROW_LOSS + 1, :]

        for cp in copies:
            cp.wait_recv()
        gt = rtail[me]
        for d in range(1, N_DEV):
            gt = gt + rtail[d ^ me]
        gtail[...] = gt
        g_out["meta_tokens"][...] = gtail[0:N_META, :]
        for i in range(D_MODEL // LANES):
            g_out["norm_g"][:, LANES * i:LANES * (i + 1)] = gtail[N_META + i:N_META + i + 1, :]
        for cp in copies:
            cp.wait_send()

    vm = pl.BlockSpec(memory_space=pltpu.VMEM)
    out_shape = [jax.ShapeDtypeStruct(shape, F32) for _, shape in PARAM_SHAPES]
    out_shape.append(jax.ShapeDtypeStruct((1, LANES), F32))
    outs = pl.pallas_call(
        body,
        name="reduce_tail",
        out_shape=tuple(out_shape),
        in_specs=[vm] * 5,
        out_specs=(vm,) * len(out_shape),
        scratch_shapes=[
            pltpu.VMEM((N_DEV, TAIL_ROWS, LANES), F32),
            pltpu.VMEM((N_DEV, TAIL_ROWS, LANES), F32),
            pltpu.VMEM((SMALL_ROWS, LANES), F32),
            pltpu.VMEM((TAIL_ROWS, LANES), F32),
            pltpu.SemaphoreType.DMA((N_DEV - 1,)),
            pltpu.SemaphoreType.DMA((N_DEV - 1,)),
        ],
        compiler_params=pltpu.CompilerParams(vmem_limit_bytes=VMEM_LIMIT),
    )(r_in, r_out, r_small, d_meta, d_norm)
    return {n: outs[i] for i, n in enumerate(names)}, outs[-1]


def _prep_gather(x, meta, norm_g, w_in_t):
    nb_seq, s, d = x.shape
    nb = s // LANES + 1
    relay_step = nb_seq * (nb - 1) // 2
    forward_step = nb_seq * (nb - 1) - 1
    finish_step = nb_seq * (nb - 1)

    def body(x_ref, meta_ref, g_ref, win_ref, u_ref, w_in_p, meta_f,
             sbig, ssmall, gbig, gsmall, send_sems, recv_sems, local_sems):
        jj, b = pl.program_id(0), pl.program_id(1)
        t = jj * nb_seq + b

        def plan():
            return _gather_plan((sbig, ssmall), (gbig, gsmall), send_sems, recv_sems, local_sems)

        @pl.when(t == 0)
        def _():
            sbig[0:SHARD_IN, :] = win_ref[...].astype(BF16)
            sbig[SHARD_IN:, :] = jnp.zeros((SHARD_IN_PAD - SHARD_IN, d), BF16)
            ssmall[...] = meta_ref[...]
            plan()[0]()

        @pl.when(t == relay_step)
        def _():
            plan()[1]()

        @pl.when(t == forward_step)
        def _():
            plan()[2]()

        @pl.when(t == finish_step)
        def _():
            plan()[3]()
            w_in_p[N_A:GRP_A, :] = jnp.zeros((GRP_A - N_A, d), BF16)
            for k in range(N_DEV):
                for s0, e0, d0 in _in_pieces(k):
                    w_in_p[d0:d0 + e0 - s0, :] = gbig[k, s0:e0, :]
                meta_f[:, SHARD_META * k:SHARD_META * (k + 1)] = gsmall[k]

        def norm(h):
            hhat, _ = _rms_stats(h)
            return (hhat * g_ref[...]).astype(BF16)

        @pl.when(jj < nb - 1)
        def _():
            u_ref[...] = norm(x_ref[0])

        @pl.when(jj == nb - 1)
        def _():
            u_ref[0:PAD_FRONT, :] = jnp.zeros((PAD_FRONT, d), BF16)
            u_ref[PAD_FRONT:LANES, :] = norm(meta_f[...])

    whole = lambda shape: pl.BlockSpec(shape, lambda jj, b: (0,) * len(shape))
    return pl.pallas_call(
        body,
        name="prep_norm_gather",
        grid=(nb, nb_seq),
        in_specs=[
            pl.BlockSpec((1, LANES, d), lambda jj, b: (b, jnp.minimum(jj, nb - 2), 0)),
            whole(meta.shape), whole(norm_g.shape), whole(w_in_t.shape),
        ],
        out_specs=(pl.BlockSpec((LANES, d), lambda jj, b: (b * nb + (jj + 1) % nb, 0)),
                   whole((IN_PAD, d)), whole((N_META, d))),
        out_shape=(jax.ShapeDtypeStruct((nb_seq * nb * LANES, d), BF16),
                   jax.ShapeDtypeStruct((IN_PAD, d), BF16),
                   jax.ShapeDtypeStruct((N_META, d), F32)),
        scratch_shapes=[
            pltpu.VMEM((SHARD_IN_PAD, d), BF16),
            pltpu.VMEM((N_META, SHARD_META), F32),
            pltpu.VMEM((N_DEV, SHARD_IN_PAD, d), BF16),
            pltpu.VMEM((N_DEV, N_META, SHARD_META), F32),
            pltpu.SemaphoreType.DMA((14,)),
            pltpu.SemaphoreType.DMA((14,)),
            pltpu.SemaphoreType.DMA((2,)),
        ],
        compiler_params=_params("arbitrary", "arbitrary"),
    )(x, meta, norm_g, w_in_t)


def _in_proj_gather(u, w_in_p, w_q, w_kv, w_out, conv_w, bm, bn):
    m, k_dim = u.shape
    n = w_in_p.shape[0]
    steps = (m // bm) * (n // bn)
    relay_step, forward_step = steps // 3, 2 * steps // 3
    qkv_shape = (SHARD_Q + KV_RANK, Q_RANK)

    def body(a_ref, b_ref, wq_ref, wkv_ref, wout_ref, conv_ref, o_ref, w_q_p, w_kv_p, w_out_f, conv_f,
             sqkv, sout, sconv, gqkv, gout, gconv, send_sems, recv_sems, local_sems):
        t = pl.program_id(0) * (n // bn) + pl.program_id(1)

        def plan():
            return _gather_plan((sqkv, sout, sconv), (gqkv, gout, gconv), send_sems, recv_sems, local_sems)

        @pl.when(t == 0)
        def _():
            sqkv[...] = jnp.zeros_like(sqkv)
            sqkv[0:SHARD_Q, :] = wq_ref[...].astype(BF16)
            sqkv[SHARD_Q:, 0:SHARD_KV] = wkv_ref[...].astype(BF16)
            sout[...] = wout_ref[...].astype(BF16)
            sconv[...] = jnp.zeros_like(sconv)
            for i in range(3):
                sconv[i:i + 1, 0:SHARD_CONV] = conv_ref[i]
            plan()[0]()

        @pl.when(t == relay_step)
        def _():
            plan()[1]()

        @pl.when(t == forward_step)
        def _():
            plan()[2]()

        o_ref[...] = _dot(a_ref[...], b_ref[...], _NT).astype(o_ref.dtype)

        @pl.when(t == steps - 1)
        def _():
            plan()[3]()
            conv_f[...] = jnp.zeros_like(conv_f)
            for k in range(N_DEV):
                for s0, e0, d0 in _q_pieces(k):
                    w_q_p[d0:d0 + e0 - s0, :] = gqkv[k, s0:e0, :]
                w_kv_p[:, _kv_dst(k):_kv_dst(k) + SHARD_KV] = gqkv[k, SHARD_Q:, 0:SHARD_KV]
                w_out_f[SHARD_OUT * k:SHARD_OUT * (k + 1), :] = gout[k]
                conv_f[0:3, SHARD_CONV * k:SHARD_CONV * (k + 1)] = gconv[k, 0:3, 0:SHARD_CONV]

    whole = lambda shape: pl.BlockSpec(shape, lambda i, j: (0,) * len(shape))
    return pl.pallas_call(
        body,
        name="in_proj_gather",
        grid=(m // bm, n // bn),
        in_specs=[pl.BlockSpec((bm, k_dim), lambda i, j: (i, 0)), pl.BlockSpec((bn, k_dim), lambda i, j: (j, 0)),
                  whole(w_q.shape), whole(w_kv.shape), whole(w_out.shape), whole(conv_w.shape)],
        out_specs=(pl.BlockSpec((bm, bn), lambda i, j: (i, j)),
                   whole((Q_COLS, Q_RANK)), whole((KV_RANK, KV_COLS)), whole((D_MODEL, D_MODEL)),
                   whole((8, CONV_WIDTH))),
        out_shape=(jax.ShapeDtypeStruct((m, n), BF16),
                   jax.ShapeDtypeStruct((Q_COLS, Q_RANK), BF16),
                   jax.ShapeDtypeStruct((KV_RANK, KV_COLS), BF16),
                   jax.ShapeDtypeStruct((D_MODEL, D_MODEL), BF16),
                   jax.ShapeDtypeStruct((8, CONV_WIDTH), F32)),
        scratch_shapes=[
            pltpu.VMEM(qkv_shape, BF16),
            pltpu.VMEM((SHARD_OUT, D_MODEL), BF16),
            pltpu.VMEM((8, LANES), F32),
            pltpu.VMEM((N_DEV,) + qkv_shape, BF16),
            pltpu.VMEM((N_DEV, SHARD_OUT, D_MODEL), BF16),
            pltpu.VMEM((N_DEV, 8, LANES), F32),
            pltpu.SemaphoreType.DMA((21,)),
            pltpu.SemaphoreType.DMA((21,)),
            pltpu.SemaphoreType.DMA((3,)),
        ],
        compiler_params=_params("arbitrary", "arbitrary"),
    )(u, w_in_p, w_q, w_kv, w_out, conv_w)


def _rope_tables(tp):
    half = D_ROPE // 2
    inv_freq = 1.0 / (ROPE_THETA ** (jnp.arange(half, dtype=F32) / half))
    pos = (jnp.arange(tp) - PAD_FRONT).astype(F32)
    ang = pos[:, None] * inv_freq[None, :]
    cos = jnp.tile(jnp.cos(ang), (1, LANES // half))
    sin = jnp.tile(jnp.sin(ang), (1, LANES // half))
    first = (jnp.arange(LANES) % D_ROPE) < half
    return cos, jnp.where(first, -sin, 0.0), jnp.where(first, 0.0, sin)


def _rope(t, cos, sa, sb):
    return t * cos + pltpu.roll(t, LANES - D_ROPE // 2, 1) * sa + pltpu.roll(t, D_ROPE // 2, 1) * sb


def _rope_t(t, cos, sa, sb):
    return t * cos + pltpu.roll(t * sa, D_ROPE // 2, 1) + pltpu.roll(t * sb, LANES - D_ROPE // 2, 1)


def _qkv_fwd(p, wq, wkv, gq, gkv, tables, nb_seq, tp):
    ht = tp // 2

    def body(pa_ref, wq_ref, wkv_ref, gq_ref, gkv_ref, cos_ref, sa_ref, sb_ref, q_ref, k_ref, v_ref):
        pa = pa_ref[...].astype(F32)
        cq_hat, _ = _rms_stats(pa[:, :Q_RANK])
        ckv_hat, _ = _rms_stats(pa[:, Q_RANK:Q_RANK + KV_RANK])
        q = _dot((cq_hat * gq_ref[...]).astype(BF16), wq_ref[...], _NT) * Q_SCALE
        kv = _dot((ckv_hat * gkv_ref[...]).astype(BF16), wkv_ref[...])
        tabs = (cos_ref[...], sa_ref[...], sb_ref[...])
        lane = lax.broadcasted_iota(jnp.int32, (ht, LANES), 1)
        low = lane < D_ROPE
        mark = lane == D_ROPE
        row = (pl.program_id(0) % 2) * ht + lax.broadcasted_iota(jnp.int32, (ht, LANES), 0)
        k_pe = jnp.where(mark & (row < PAD_FRONT), NEG_INF, _rope(pa[:, Q_RANK + KV_RANK:], *tabs))
        one = jnp.where(mark & (row >= PAD_FRONT), 1.0, 0.0)
        pairs = [_rope(q[:, N_HEADS * D_NOPE + LANES * i:N_HEADS * D_NOPE + LANES * (i + 1)], *tabs) for i in range(2)]
        for h in range(N_HEADS):
            pair = pairs[h // 2]
            if h % 2:
                pair = pltpu.roll(pair, D_ROPE, 1)
            pe = jnp.where(low, pair, one)
            q_ref[0, h] = jnp.concatenate([q[:, D_NOPE * h:D_NOPE * (h + 1)], pe], axis=1).astype(BF16)
            k_ref[0, h] = jnp.concatenate([kv[:, D_NOPE * h:D_NOPE * (h + 1)], k_pe], axis=1).astype(BF16)
            v_ref[0, h] = kv[:, N_HEADS * D_NOPE + D_V * h:N_HEADS * D_NOPE + D_V * (h + 1)].astype(BF16)

    full = lambda a: pl.BlockSpec(a.shape, lambda i: (0,) * a.ndim)
    tab = pl.BlockSpec((ht, LANES), lambda i: (i % 2, 0))
    qk = pl.BlockSpec((1, N_HEADS, ht, 2 * LANES), lambda i: (i // 2, 0, i % 2, 0))
    return pl.pallas_call(
        body,
        name="qkv_fwd",
        grid=(2 * nb_seq,),
        in_specs=[pl.BlockSpec((ht, GRP_A), lambda i: (i, 0)), full(wq), full(wkv), full(gq), full(gkv), tab, tab, tab],
        out_specs=(qk, qk, pl.BlockSpec((1, N_HEADS, ht, D_V), lambda i: (i // 2, 0, i % 2, 0))),
        out_shape=(
            jax.ShapeDtypeStruct((nb_seq, N_HEADS, tp, 2 * LANES), BF16),
            jax.ShapeDtypeStruct((nb_seq, N_HEADS, tp, 2 * LANES), BF16),
            jax.ShapeDtypeStruct((nb_seq, N_HEADS, tp, D_V), BF16),
        ),
        compiler_params=_params("parallel"),
    )(p, wq, wkv, gq, gkv, *tables)


def _attn_fwd(q, k, v, p, g_attn):
    nb_seq, _, tp, _ = q.shape

    def body(q_ref, k_ref, v_ref, z_ref, g_ref, y_ref, o_ref, lse_ref):
        g = g_ref[...]
        for r0 in range(0, tp, Q_TILE):
            nq = min(Q_TILE, tp - r0)
            kend = r0 + nq
            qq = q_ref[0, 0, r0:kend, :]
            sd = _dot(qq, k_ref[0, 0, r0:kend, :], _NT)
            causal = (lax.broadcasted_iota(jnp.int32, (nq, nq), 1) <= lax.broadcasted_iota(jnp.int32, (nq, nq), 0))
            sd = jnp.where(causal, sd, NEG_INF)
            m = jnp.max(sd, axis=-1, keepdims=True)
            if r0:
                so = _dot(qq, k_ref[0, 0, 0:r0, :], _NT)
                m = jnp.maximum(m, jnp.max(so, axis=-1, keepdims=True))
            ed = jnp.exp2(sd - m)
            l = jnp.sum(ed, axis=-1, keepdims=True)
            o = _dot(ed.astype(BF16), v_ref[0, 0, r0:kend, :])
            if r0:
                eo = jnp.exp2(so - m)
                l = l + jnp.sum(eo, axis=-1, keepdims=True)
                o = o + _dot(eo.astype(BF16), v_ref[0, 0, 0:r0, :])
            o = o * (1.0 / l)
            o_ref[0, 0, r0:kend, :] = o
            lse_ref[0, 0, r0:kend, :] = jnp.broadcast_to(m + jnp.log2(l), (nq, LANES))
            ohat, _ = _rms_stats(o)
            z = z_ref[r0:kend, :].astype(F32)
            y_ref[r0:kend, :] = (ohat * g * (z * _sigmoid(z))).astype(BF16)

    qk = pl.BlockSpec((1, 1, tp, 2 * LANES), lambda b, h: (b, h, 0, 0))
    hv = pl.BlockSpec((1, 1, tp, D_V), lambda b, h: (b, h, 0, 0))
    return pl.pallas_call(
        body,
        name="attn_fwd",
        grid=(nb_seq, N_HEADS),
        in_specs=[qk, qk, hv,
                  pl.BlockSpec((tp, LANES), lambda b, h: (b, GRP_A // LANES + h)),
                  pl.BlockSpec((1, LANES), lambda b, h: (0, h))],
        out_specs=(pl.BlockSpec((tp, LANES), lambda b, h: (b, h)), hv, hv),
        out_shape=(
            jax.ShapeDtypeStruct((nb_seq * tp, N_HEADS * D_V), BF16),
            jax.ShapeDtypeStruct((nb_seq, N_HEADS, tp, D_V), F32),
            jax.ShapeDtypeStruct((nb_seq, N_HEADS, tp, LANES), F32),
        ),
        compiler_params=_params("parallel", "parallel"),
    )(q, k, v, p, g_attn)


_CONV_COL0 = (GRP_A + N_HEADS * D_V) // LANES


def _conv_specs(tp, order):
    cols = CONV_WIDTH // LANES
    return [pl.BlockSpec((tp, LANES), functools.partial(
        lambda a, b, off: order(a, b, off), off=_CONV_COL0 + i * cols)) for i in range(4)]


def _conv_fwd(p, conv_w, g_conv, nb_seq, tp):
    def body(b_ref, c_ref, h_ref, z_ref, w_ref, g_ref, y_ref):
        cc = c_ref[...].astype(F32) * h_ref[...].astype(F32)
        row = lax.broadcasted_iota(jnp.int32, (tp, LANES), 0)
        s1 = jnp.where(row >= 1, pltpu.roll(cc, 1, 0), 0.0)
        s2 = jnp.where(row >= 2, pltpu.roll(cc, 2, 0), 0.0)
        yc = b_ref[...].astype(F32) * (w_ref[0:1, :] * s2 + w_ref[1:2, :] * s1 + w_ref[2:3, :] * cc)
        r = lax.rsqrt(_group_mean(yc * yc) + EPS)
        z = z_ref[...].astype(F32)
        y_ref[...] = (yc * r * g_ref[...] * (z * _sigmoid(z))).astype(BF16)

    return pl.pallas_call(
        body,
        name="conv_fwd",
        grid=(nb_seq, CONV_WIDTH // LANES),
        in_specs=_conv_specs(tp, lambda b, t, off: (b, off + t)) + [
            pl.BlockSpec((8, LANES), lambda b, t: (0, t)),
            pl.BlockSpec((1, LANES), lambda b, t: (0, t))],
        out_specs=pl.BlockSpec((tp, LANES), lambda b, t: (b, t)),
        out_shape=jax.ShapeDtypeStruct((nb_seq * tp, CONV_WIDTH), BF16),
        compiler_params=_params("parallel", "parallel"),
    )(p, p, p, p, conv_w, g_conv)


def _token_copy(hbm, b, k, ts, buf, sem, to_hbm=False):
    lo, hi = max(k * ts - LANES, 0), (k + 1) * ts - LANES
    off = lo - (k * ts - LANES)
    src, dst = hbm.at[b, pl.ds(lo, hi - lo)], buf.at[pl.ds(off, hi - lo)]
    if to_hbm:
        src, dst = dst, src
    return pltpu.make_async_copy(src, dst, sem)


def _for_tile(k, nt, fn):
    for kk in range(nt):
        @pl.when(k == kk)
        def _(kk=kk):
            fn(kk)


def _out_proj_loss(ya, yc, w_out, x, target, g_final, nt):
    nb_seq, s, d = x.shape
    r, ka = ya.shape
    ts = (s + LANES) // nt
    steps = nb_seq * nt

    def body(a_ref, c_ref, w_ref, x_hbm, t_hbm, g_ref, dhb_ref, dg_ref, loss_ref,
             xbuf, tbuf, acc_ref, sems):
        i = pl.program_id(0)
        b, k = i // nt, i % nt

        @pl.when(i == 0)
        def _():
            acc_ref[...] = jnp.zeros_like(acc_ref)
            dg_ref[...] = jnp.zeros_like(dg_ref)

        slot = i % 2

        def fetch(seq, kk, sl):
            return [_token_copy(x_hbm, seq, kk, ts, xbuf.at[sl], sems.at[sl, 0]),
                    _token_copy(t_hbm, seq, kk, ts, tbuf.at[sl], sems.at[sl, 1])]

        def start(seq, sl, kk):
            if kk == 0:
                xbuf[sl, 0:LANES, :] = jnp.zeros((LANES, d), F32)
                tbuf[sl, 0:LANES, :] = jnp.zeros((LANES, d), F32)
            for cp in fetch(seq, kk, sl):
                cp.start()

        @pl.when(i == 0)
        def _():
            start(0, 0, 0)

        @pl.when(i + 1 < steps)
        def _():
            _for_tile((i + 1) % nt, nt, functools.partial(start, (i + 1) // nt, 1 - slot))

        mix = _dot(a_ref[...], w_ref[0:ka, :]) + _dot(c_ref[...], w_ref[ka:, :])
        _for_tile(k, nt, lambda kk: [cp.wait() for cp in fetch(b, kk, slot)])

        real = (lax.broadcasted_iota(jnp.int32, (ts, d), 0) >= LANES) | (k > 0)
        g = g_ref[...]
        hhat, rstd = _rms_stats(xbuf[slot] + mix)
        e = jnp.where(real, hhat * g - tbuf[slot], 0.0)
        acc_ref[...] += jnp.sum(e * e, axis=0, keepdims=True)
        dy = e * (1.0 / d)
        dg_ref[...] += jnp.sum(dy * hhat, axis=0, keepdims=True)
        dhb_ref[...] = _rms_bwd(g * dy, hhat, rstd).astype(BF16)

        @pl.when(i == steps - 1)
        def _():
            total = jnp.sum(acc_ref[...], axis=1, keepdims=True)
            loss_ref[...] = jnp.broadcast_to((0.5 / d) * total, loss_ref.shape)

    hbm = pl.BlockSpec(memory_space=pl.ANY)
    row = pl.BlockSpec((ts, d), lambda i: (i, 0))
    vec = pl.BlockSpec((1, d), lambda i: (0, 0))
    return pl.pallas_call(
        body,
        name="out_proj_loss",
        grid=(steps,),
        in_specs=[pl.BlockSpec((ts, ka), lambda i: (i, 0)), pl.BlockSpec((ts, yc.shape[1]), lambda i: (i, 0)),
                  pl.BlockSpec(w_out.shape, lambda i: (0, 0)), hbm, hbm, vec],
        out_specs=(row, vec, pl.BlockSpec((1, LANES), lambda i: (0, 0))),
        out_shape=(
            jax.ShapeDtypeStruct((r, d), BF16),
            jax.ShapeDtypeStruct((1, d), F32),
            jax.ShapeDtypeStruct((1, LANES), F32),
        ),
        scratch_shapes=[pltpu.VMEM((2, ts, d), F32), pltpu.VMEM((2, ts, d), F32), pltpu.VMEM((1, d), F32),
                        pltpu.SemaphoreType.DMA((2, 2))],
        compiler_params=_params("arbitrary"),
    )(ya, yc, w_out, x, target, g_final)


def _out_proj_bwd(dhb, w_out, ya, yc, bm):
    r, d = dhb.shape
    ka = ya.shape[1]
    n_mix = w_out.shape[0]
    last = r // bm - 1

    def body(dh_ref, w_ref, a_ref, c_ref, dcat_ref, dw_ref, acc_ref):
        @pl.when(pl.program_id(0) == 0)
        def _():
            acc_ref[...] = jnp.zeros_like(acc_ref)

        dh = dh_ref[...]
        dcat_ref[...] = _dot(dh, w_ref[...], _NT).astype(BF16)
        acc_ref[0:ka, :] += _dot(a_ref[...], dh, _TN)
        acc_ref[ka:, :] += _dot(c_ref[...], dh, _TN)

        @pl.when(pl.program_id(0) == last)
        def _():
            dw_ref[...] = acc_ref[...].astype(BF16)

    return pl.pallas_call(
        body,
        name="out_proj_bwd",
        grid=(r // bm,),
        in_specs=[pl.BlockSpec((bm, d), lambda i: (i, 0)), pl.BlockSpec(w_out.shape, lambda i: (0, 0)),
                  pl.BlockSpec((bm, ka), lambda i: (i, 0)), pl.BlockSpec((bm, yc.shape[1]), lambda i: (i, 0))],
        out_specs=(pl.BlockSpec((bm, n_mix), lambda i: (i, 0)),
                   pl.BlockSpec((n_mix, d), lambda i: (0, 0))),
        out_shape=(jax.ShapeDtypeStruct((r, n_mix), BF16),
                   jax.ShapeDtypeStruct((n_mix, d), BF16)),
        scratch_shapes=[pltpu.VMEM((n_mix, d), F32)],
        compiler_params=_params("arbitrary"),
    )(dhb, w_out, ya, yc)


def _attn_bwd(q, k, v, o, lse, dcat, p, g_attn):
    nb_seq, _, tp, _ = q.shape

    def body(q_ref, k_ref, v_ref, o_ref, lse_ref, dy_ref, z_ref, g_ref,
             dq_ref, dk_ref, dv_ref, dz_ref, dg_ref, dq_acc):
        @pl.when(pl.program_id(1) == 0)
        def _():
            dg_ref[...] = jnp.zeros_like(dg_ref)

        g = g_ref[...]
        z = z_ref[...].astype(F32)
        o = o_ref[0, 0]
        dy = dy_ref[...].astype(F32)
        sig = _sigmoid(z)
        ohat, r = _rms_stats(o)
        don = dy * (z * sig)
        dz_ref[...] = (dy * (ohat * g) * (sig * (1.0 + z * (1.0 - sig)))).astype(BF16)
        dg_ref[...] += jnp.sum(don * ohat, axis=0, keepdims=True)
        do = _rms_bwd(g * don, ohat, r)
        dvec = jnp.sum(do * o, axis=-1, keepdims=True)
        dob = do.astype(BF16)
        lse_col = lse_ref[0, 0, :, 0:1]
        dq_acc[...] = jnp.zeros_like(dq_acc)
        for k0 in range(0, tp, K_TILE):
            nk = min(K_TILE, tp - k0)
            nq = tp - k0
            qq = q_ref[0, 0, k0:, :]
            kk = k_ref[0, 0, k0:k0 + nk, :]
            causal = (lax.broadcasted_iota(jnp.int32, (nq, nk), 1) <= lax.broadcasted_iota(jnp.int32, (nq, nk), 0))
            pr = jnp.where(causal, jnp.exp2(_dot(qq, kk, _NT) - lse_col[k0:]), 0.0)
            dp = _dot(dob[k0:], v_ref[0, 0, k0:k0 + nk, :], _NT)
            ds = (pr * (dp - dvec[k0:])).astype(BF16)
            dv_ref[0, 0, k0:k0 + nk, :] = _dot(pr.astype(BF16), dob[k0:], _TN).astype(BF16)
            dk_ref[0, 0, k0:k0 + nk, :] = (_dot(ds, qq, _TN) * (ATTN_SCALE / Q_SCALE)).astype(BF16)
            dq_acc[k0:, :] += _dot(ds, kk)
        dq_ref[0, 0] = (dq_acc[...] * ATTN_SCALE).astype(BF16)

    qk = pl.BlockSpec((1, 1, tp, 2 * LANES), lambda h, b: (b, h, 0, 0))
    hv = pl.BlockSpec((1, 1, tp, D_V), lambda h, b: (b, h, 0, 0))
    col = pl.BlockSpec((tp, LANES), lambda h, b: (b, h))
    return pl.pallas_call(
        body,
        name="attn_bwd",
        grid=(N_HEADS, nb_seq),
        in_specs=[qk, qk, hv, hv, hv, col,
                  pl.BlockSpec((tp, LANES), lambda h, b: (b, GRP_A // LANES + h)),
                  pl.BlockSpec((1, LANES), lambda h, b: (0, h))],
        out_specs=(qk, qk, hv, col, pl.BlockSpec((1, LANES), lambda h, b: (0, h))),
        out_shape=(
            jax.ShapeDtypeStruct((nb_seq, N_HEADS, tp, 2 * LANES), BF16),
            jax.ShapeDtypeStruct((nb_seq, N_HEADS, tp, 2 * LANES), BF16),
            jax.ShapeDtypeStruct((nb_seq, N_HEADS, tp, D_V), BF16),
            jax.ShapeDtypeStruct((nb_seq * tp, N_HEADS * D_V), BF16),
            jax.ShapeDtypeStruct((1, N_HEADS * D_V), F32),
        ),
        scratch_shapes=[pltpu.VMEM((tp, 2 * LANES), F32)],
        compiler_params=_params("arbitrary", "arbitrary"),
    )(q, k, v, o, lse, dcat, p, g_attn)


def _qkv_bwd(p, dq, dk, dv, wq, wkv, gq, gkv, tables):
    nb_seq, _, tp, _ = dq.shape
    ht = tp // 2

    def body(pa_ref, dq_ref, dk_ref, dv_ref, wq_ref, wkv_ref, gq_ref, gkv_ref, cos_ref, sa_ref, sb_ref,
             dpa_ref, dwq_ref, dwkv_ref, dgq_ref, dgkv_ref):
        @pl.when(pl.program_id(0) == 0)
        def _():
            dwq_ref[...] = jnp.zeros_like(dwq_ref)
            dwkv_ref[...] = jnp.zeros_like(dwkv_ref)
            dgq_ref[...] = jnp.zeros_like(dgq_ref)
            dgkv_ref[...] = jnp.zeros_like(dgkv_ref)

        pa = pa_ref[...].astype(F32)
        gq, gkv = gq_ref[...], gkv_ref[...]
        cq_hat, rq = _rms_stats(pa[:, :Q_RANK])
        ckv_hat, rkv = _rms_stats(pa[:, Q_RANK:Q_RANK + KV_RANK])
        tabs = (cos_ref[...], sa_ref[...], sb_ref[...])

        pe = [dq_ref[0, h, :, D_NOPE:].astype(F32) for h in range(N_HEADS)]
        pairs = [_rope_t(pe[2 * i] + pltpu.roll(pe[2 * i + 1], D_ROPE, 1), *tabs).astype(BF16) for i in range(2)]
        dq_flat = jnp.concatenate([dq_ref[0, h, :, :D_NOPE] for h in range(N_HEADS)] + pairs, axis=1)
        dwq_ref[...] += _dot(dq_flat, (cq_hat * gq).astype(BF16), _TN)
        dcqn = _dot(dq_flat, wq_ref[...])
        dgq_ref[...] += jnp.sum(dcqn * cq_hat, axis=0, keepdims=True)
        dcq = _rms_bwd(gq * dcqn, cq_hat, rq)

        dkv_flat = jnp.concatenate([dk_ref[0, h, :, :D_NOPE] for h in range(N_HEADS)]
                                   + [dv_ref[0, h] for h in range(N_HEADS)], axis=1)
        dwkv_ref[...] += _dot((ckv_hat * gkv).astype(BF16), dkv_flat, _TN)
        dckvn = _dot(dkv_flat, wkv_ref[...], _NT)
        dgkv_ref[...] += jnp.sum(dckvn * ckv_hat, axis=0, keepdims=True)
        dckv = _rms_bwd(gkv * dckvn, ckv_hat, rkv)

        dk_pe = dk_ref[0, 0, :, D_NOPE:].astype(F32)
        for h in range(1, N_HEADS):
            dk_pe = dk_pe + dk_ref[0, h, :, D_NOPE:].astype(F32)
        dk_pe = jnp.where(lax.broadcasted_iota(jnp.int32, (ht, LANES), 1) < D_ROPE, dk_pe, 0.0)
        dpa_ref[...] = jnp.concatenate([dcq, dckv, _rope_t(dk_pe, *tabs)], axis=1).astype(BF16)

    full = lambda a: pl.BlockSpec(a.shape, lambda i: (0,) * a.ndim)
    tab = pl.BlockSpec((ht, LANES), lambda i: (i % 2, 0))
    qk = pl.BlockSpec((1, N_HEADS, ht, 2 * LANES), lambda i: (i // 2, 0, i % 2, 0))
    acc = lambda shape: pl.BlockSpec(shape, lambda i: (0, 0))
    return pl.pallas_call(
        body,
        name="qkv_bwd",
        grid=(2 * nb_seq,),
        in_specs=[pl.BlockSpec((ht, GRP_A), lambda i: (i, 0)), qk, qk,
                  pl.BlockSpec((1, N_HEADS, ht, D_V), lambda i: (i // 2, 0, i % 2, 0)),
                  full(wq), full(wkv), full(gq), full(gkv), tab, tab, tab],
        out_specs=(pl.BlockSpec((ht, GRP_A), lambda i: (i, 0)),
                   acc(wq.shape), acc(wkv.shape), acc((1, Q_RANK)), acc((1, KV_RANK))),
        out_shape=(
            jax.ShapeDtypeStruct((nb_seq * tp, GRP_A), BF16),
            jax.ShapeDtypeStruct(wq.shape, F32),
            jax.ShapeDtypeStruct(wkv.shape, F32),
            jax.ShapeDtypeStruct((1, Q_RANK), F32),
            jax.ShapeDtypeStruct((1, KV_RANK), F32),
        ),
        compiler_params=_params("arbitrary"),
    )(p, dq, dk, dv, wq, wkv, gq, gkv, *tables)


def _conv_bwd(p, dcat, conv_w, g_conv, nb_seq, tp):
    cols = CONV_WIDTH // LANES

    def body(b_ref, c_ref, h_ref, z_ref, dy_ref, w_ref, g_ref,
             db_ref, dc_ref, dh_ref, dz_ref, dw_ref, dg_ref):
        @pl.when(pl.program_id(1) == 0)
        def _():
            dw_ref[...] = jnp.zeros_like(dw_ref)
            dg_ref[...] = jnp.zeros_like(dg_ref)

        cb, c, h = b_ref[...].astype(F32), c_ref[...].astype(F32), h_ref[...].astype(F32)
        z, dy = z_ref[...].astype(F32), dy_ref[...].astype(F32)
        g = g_ref[...]
        w0, w1, w2 = w_ref[0:1, :], w_ref[1:2, :], w_ref[2:3, :]
        cc = c * h
        row = lax.broadcasted_iota(jnp.int32, (tp, LANES), 0)
        s1 = jnp.where(row >= 1, pltpu.roll(cc, 1, 0), 0.0)
        s2 = jnp.where(row >= 2, pltpu.roll(cc, 2, 0), 0.0)
        dwc = w0 * s2 + w1 * s1 + w2 * cc
        yc = cb * dwc
        r = lax.rsqrt(_group_mean(yc * yc) + EPS)
        ychat = yc * r
        sig = _sigmoid(z)
        dz_ref[...] = (dy * (ychat * g) * (sig * (1.0 + z * (1.0 - sig)))).astype(BF16)
        dyn = dy * (z * sig)
        dg_ref[...] += jnp.sum(dyn * ychat, axis=0, keepdims=True)
        gd = g * dyn
        dyc = r * (gd - ychat * _group_mean(gd * ychat))
        db_ref[...] = (dyc * dwc).astype(BF16)
        ddw = dyc * cb
        dw_ref[0:1, :] += jnp.sum(ddw * s2, axis=0, keepdims=True)
        dw_ref[1:2, :] += jnp.sum(ddw * s1, axis=0, keepdims=True)
        dw_ref[2:3, :] += jnp.sum(ddw * cc, axis=0, keepdims=True)
        u1 = jnp.where(row <= tp - 2, pltpu.roll(ddw, tp - 1, 0), 0.0)
        u2 = jnp.where(row <= tp - 3, pltpu.roll(ddw, tp - 2, 0), 0.0)
        dcc = w2 * ddw + w1 * u1 + w0 * u2
        dc_ref[...] = (dcc * h).astype(BF16)
        dh_ref[...] = (dcc * c).astype(BF16)

    col = pl.BlockSpec((tp, LANES), lambda t, b: (b, t))
    out = jax.ShapeDtypeStruct((nb_seq * tp, CONV_WIDTH), BF16)
    return pl.pallas_call(
        body,
        name="conv_bwd",
        grid=(cols, nb_seq),
        in_specs=_conv_specs(tp, lambda t, b, off: (b, off + t)) + [
            pl.BlockSpec((tp, LANES), lambda t, b: (b, N_HEADS * D_V // LANES + t)),
            pl.BlockSpec((8, LANES), lambda t, b: (0, t)),
            pl.BlockSpec((1, LANES), lambda t, b: (0, t))],
        out_specs=(col, col, col, col,
                   pl.BlockSpec((8, LANES), lambda t, b: (0, t)), pl.BlockSpec((1, LANES), lambda t, b: (0, t))),
        out_shape=(out, out, out, out,
                   jax.ShapeDtypeStruct((8, CONV_WIDTH), F32), jax.ShapeDtypeStruct((1, CONV_WIDTH), F32)),
        compiler_params=_params("arbitrary", "arbitrary"),
    )(p, p, p, p, dcat, conv_w, g_conv)


def _input_bwd(dps, w_in, x, meta, dh, norm_g, nt, send_in):
    nb_seq, s, d = x.shape
    r, kb = dps[0].shape
    ts = (s + LANES) // nt
    steps = nb_seq * nt
    n_dp = len(dps)
    in_slot = send_in.shape[1:]

    def body(*refs):
        dp_refs, w_ref, x_hbm, meta_ref, dh_ref, g_ref, pay_ref = refs[:n_dp], *refs[n_dp:n_dp + 6]
        o = n_dp + 6
        gx_hbm, dmeta_ref, dg_ref, r2_in = refs[o:o + 4]
        xbuf, gxbuf, tok_sems, own_in, r1_in, sum_in = refs[o + 4:o + 10]
        sems = refs[o + 10:]
        i = pl.program_id(0)
        b, k = i // nt, i % nt

        def plan():
            return _reduce_plan((pay_ref,), (own_in,), (r1_in,), (sum_in,), (r2_in,), *sems)

        @pl.when(i == 0)
        def _():
            dmeta_ref[...] = jnp.zeros_like(dmeta_ref)
            dg_ref[...] = jnp.zeros_like(dg_ref)
            plan()[0]()

        @pl.when(i == 1)
        def _():
            plan()[1]()

        def start(kk):
            if kk == 0:
                xbuf[0:PAD_FRONT, :] = jnp.zeros((PAD_FRONT, d), F32)
                xbuf[PAD_FRONT:LANES, :] = meta_ref[...]
            _token_copy(x_hbm, b, kk, ts, xbuf, tok_sems.at[0]).start()

        _for_tile(k, nt, start)
        du = _dot(dp_refs[0][...], w_ref[0:kb, :])
        for j in range(1, n_dp):
            du = du + _dot(dp_refs[j][...], w_ref[kb * j:kb * (j + 1), :])
        _for_tile(k, nt, lambda kk: _token_copy(x_hbm, b, kk, ts, xbuf, tok_sems.at[0]).wait())

        g = g_ref[...]
        hhat, rstd = _rms_stats(xbuf[...])
        dg_ref[...] += jnp.sum(du * hhat, axis=0, keepdims=True)
        res = _rms_bwd(g * du, hhat, rstd) + dh_ref[...].astype(F32)

        @pl.when(i > 0)
        def _():
            _for_tile(k, nt, lambda kk: _token_copy(gx_hbm, b, (kk - 1) % nt, ts, gxbuf, tok_sems.at[1], True).wait())

        gxbuf[...] = res

        @pl.when(k == 0)
        def _():
            dmeta_ref[...] += gxbuf[PAD_FRONT:LANES, :]

        _for_tile(k, nt, lambda kk: _token_copy(gx_hbm, b, kk, ts, gxbuf, tok_sems.at[1], True).start())

        @pl.when(i == steps - 1)
        def _():
            _token_copy(gx_hbm, b, nt - 1, ts, gxbuf, tok_sems.at[1], True).wait()
            plan()[2]()

    whole = lambda a: pl.BlockSpec(a.shape, lambda i: (0,) * a.ndim)
    hbm = pl.BlockSpec(memory_space=pl.ANY)
    return pl.pallas_call(
        body,
        name="input_bwd",
        grid=(steps,),
        in_specs=[pl.BlockSpec((ts, kb), lambda i: (i, 0)) for _ in dps]
        + [whole(w_in), hbm, whole(meta), pl.BlockSpec((ts, d), lambda i: (i, 0)), whole(norm_g), hbm],
        out_specs=(hbm, pl.BlockSpec((N_META, d), lambda i: (0, 0)), pl.BlockSpec((1, d), lambda i: (0, 0)), hbm),
        out_shape=(jax.ShapeDtypeStruct((nb_seq, s, d), F32),
                   jax.ShapeDtypeStruct((N_META, d), F32),
                   jax.ShapeDtypeStruct((1, d), F32),
                   jax.ShapeDtypeStruct((N_CHIPS,) + in_slot, BF16)),
        scratch_shapes=[pltpu.VMEM((ts, d), F32), pltpu.VMEM((ts, d), F32), pltpu.SemaphoreType.DMA((2,))]
        + _reduce_scratch([(in_slot, BF16)], [True]),
        compiler_params=_params("arbitrary"),
    )(*dps, w_in, x, meta, dh, norm_g, send_in)


def _in_proj_bwd_w(u, dps, bm, small_grads, send_out):
    r, d = u.shape
    kb = dps[0].shape[1]
    steps = r // bm
    n_dp, n_small = len(dps), len(small_grads)
    out_slot, small_slot = send_out.shape[1:], (SMALL_ROWS, LANES)

    def body(*refs):
        u_ref, dp_refs = refs[0], refs[1:1 + n_dp]
        small_refs = refs[1 + n_dp:1 + n_dp + n_small]
        o = 1 + n_dp + n_small
        pay_out, o_ref, r2_out, r2_small = refs[o:o + 4]
        acc_ref, ssmall, r1_out, sum_out, r1_small, sum_small = refs[o + 4:o + 10]
        sems = refs[o + 10:]
        i = pl.program_id(0)

        def plan():
            return _reduce_plan((pay_out, ssmall), (None, None), (r1_out, r1_small), (sum_out, sum_small),
                                (r2_out, r2_small), *sems)

        @pl.when(i == 0)
        def _():
            acc_ref[...] = jnp.zeros_like(acc_ref)
            _pack_small(ssmall, *small_refs)
            plan()[0]()

        @pl.when(i == 1)
        def _():
            plan()[1]()

        uu = u_ref[...]
        for j in range(n_dp):
            acc_ref[kb * j:kb * (j + 1), :] += _dot(dp_refs[j][...], uu, _TN)

        @pl.when(i == steps - 1)
        def _():
            for k in range(N_DEV):
                for s, e, c0 in _in_pieces(k):
                    o_ref[k, s:e, :] = acc_ref[c0:c0 + e - s, :].astype(BF16)
                o_ref[k, SHARD_IN:, :] = jnp.zeros((SHARD_IN_PAD - SHARD_IN, d), BF16)
            plan()[2]()

    whole = lambda a: pl.BlockSpec(a.shape, lambda i: (0,) * a.ndim)
    hbm = pl.BlockSpec(memory_space=pl.ANY)
    return pl.pallas_call(
        body,
        name="in_proj_bwd_w",
        grid=(steps,),
        in_specs=[pl.BlockSpec((bm, d), lambda i: (i, 0))]
        + [pl.BlockSpec((bm, kb), lambda i: (i, 0)) for _ in dps] + [whole(a) for a in small_grads]
        + [whole(send_out)],
        out_specs=(pl.BlockSpec((N_DEV, SHARD_IN_PAD, d), lambda i: (0, 0, 0)), hbm, hbm),
        out_shape=(jax.ShapeDtypeStruct((N_DEV, SHARD_IN_PAD, d), BF16),
                   jax.ShapeDtypeStruct((N_CHIPS,) + out_slot, BF16),
                   jax.ShapeDtypeStruct((N_CHIPS,) + small_slot, F32)),
        scratch_shapes=[pltpu.VMEM((kb * n_dp, d), F32), pltpu.VMEM((N_DEV,) + small_slot, F32)]
        + _reduce_scratch([(out_slot, BF16), (small_slot, F32)], [False, False]),
        compiler_params=_params("arbitrary"),
    )(u, *dps, *small_grads, send_out)


def _local_step(x, loss_target, u, p, meta_f, norm_g, w_in_p, q_norm_g, w_q_p, kv_norm_g, w_kv_p, conv_w_f,
                attn_out_g, conv_out_g, w_out_f, g_final):
    nb_seq, s, d = x.shape
    tp = s + LANES
    ht = tp // 2
    tables = _rope_tables(tp)

    q, k, v = _qkv_fwd(p, w_q_p, w_kv_p, q_norm_g, kv_norm_g, tables, nb_seq, tp)
    ya, o, lse = _attn_fwd(q, k, v, p, attn_out_g)
    yc = _conv_fwd(p, conv_w_f, conv_out_g, nb_seq, tp)
    dhb, d_final_g, loss_part = _out_proj_loss(ya, yc, w_out_f, x, loss_target, g_final, TOKEN_TILES)

    dcat, d_w_out = _out_proj_bwd(dhb, w_out_f, ya, yc, ht)
    send_out = d_w_out.reshape(N_DEV, SHARD_OUT, d)
    dq, dk, dv, dz_attn, d_attn_g = _attn_bwd(q, k, v, o, lse, dcat, p, attn_out_g)
    dpa, d_wq_p, d_wkv_p, d_gq, d_gkv = _qkv_bwd(p, dq, dk, dv, w_q_p, w_kv_p, q_norm_g, kv_norm_g, tables)
    d_b, d_c, d_h, dz_conv, d_conv_w, d_conv_g = _conv_bwd(p, dcat, conv_w_f, conv_out_g, nb_seq, tp)
    dps = (dpa, dz_attn, d_b, d_c, d_h, dz_conv)
    small = (d_wq_p, d_wkv_p, d_conv_w, d_final_g, d_gq, d_gkv, d_attn_g, d_conv_g, loss_part)
    send_in, r_out, r_small = _in_proj_bwd_w(u, dps, ht, small, send_out)
    grad_x, d_meta, d_norm_g, r_in = _input_bwd(dps, w_in_p, x, meta_f, dhb, norm_g, TOKEN_TILES, send_in)
    return grad_x, r_in, r_out, r_small, d_meta, d_norm_g


def kernel(x, meta_tokens, norm_g, w_in, q_norm_g, w_q_up, kv_norm_g, w_kv_up, conv_w, attn_out_g, conv_out_g, w_out, final_norm_g, loss_target, m_meta_tokens, m_norm_g, m_w_in, m_q_norm_g, m_w_q_up, m_kv_norm_g, m_w_kv_up, m_conv_w, m_attn_out_g, m_conv_out_g, m_w_out, m_final_norm_g, v_meta_tokens, v_norm_g, v_w_in, v_q_norm_g, v_w_q_up, v_kv_norm_g, v_w_kv_up, v_conv_w, v_attn_out_g, v_conv_out_g, v_w_out, v_final_norm_g):
    d = x.shape[-1]
    ht = (x.shape[1] + LANES) // 2
    u, w_in_p, meta_f = _prep_gather(x, meta_tokens, norm_g, w_in[0].T)
    p, w_q_p, w_kv_p, w_out_f, conv_w_f = _in_proj_gather(
        u, w_in_p, w_q_up[0].T, w_kv_up[0], w_out[0], conv_w.transpose(1, 0, 2), ht, 3 * GRP_A)
    g_final = final_norm_g.reshape(1, d)
    grad_x, r_in, r_out, r_small, d_meta, d_norm_g = _local_step(
        x, loss_target, u, p, meta_f, norm_g, w_in_p, q_norm_g, w_q_p, kv_norm_g, w_kv_p, conv_w_f,
        attn_out_g, conv_out_g, w_out_f, g_final)

    flat = lambda a: a.reshape(a.shape[-2:]) if a.ndim == 3 else a.reshape(1, -1) if a.ndim == 1 else a
    transposed = ("w_in", "w_q_up")

    def to_kernel(n, a):
        if n == "conv_w":
            return a.transpose(1, 0, 2)
        return flat(a).T if n in transposed else flat(a)

    def from_kernel(n, a, shape):
        if n == "conv_w":
            return a.transpose(1, 0, 2)
        return (a.T if n in transposed else a).reshape(shape)
    params = {
        "meta_tokens": (meta_tokens, m_meta_tokens, v_meta_tokens),
        "norm_g": (norm_g, m_norm_g, v_norm_g),
        "w_in": (w_in, m_w_in, v_w_in),
        "q_norm_g": (q_norm_g, m_q_norm_g, v_q_norm_g),
        "w_q_up": (w_q_up, m_w_q_up, v_w_q_up),
        "kv_norm_g": (kv_norm_g, m_kv_norm_g, v_kv_norm_g),
        "w_kv_up": (w_kv_up, m_w_kv_up, v_w_kv_up),
        "conv_w": (conv_w, m_conv_w, v_conv_w),
        "attn_out_g": (attn_out_g, m_attn_out_g, v_attn_out_g),
        "conv_out_g": (conv_out_g, m_conv_out_g, v_conv_out_g),
        "w_out": (w_out, m_w_out, v_w_out),
        "final_norm_g": (final_norm_g, m_final_norm_g, v_final_norm_g),
    }
    grads, loss = _reduce_tail(r_in, r_out, r_small, d_meta, d_norm_g)
    updated, grad_x = _adamw(grads, {n: tuple(to_kernel(n, a) for a in t) for n, t in params.items()}, grad_x)
    outs = [[from_kernel(n, updated[n][i], params[n][0].shape) for n, _ in PARAM_SHAPES] for i in range(4)]
    return (loss[0, 0], grad_x, *outs[0], *outs[1], *outs[2], *outs[3])
```

```python
import functools

import jax
import jax.numpy as jnp
from jax import lax
from jax.experimental import pallas as pl
from jax.experimental.pallas import tpu as pltpu

F32 = jnp.float32
BF16 = jnp.bfloat16

N_META = 16
D_MODEL = 1024
N_HEADS = 4
D_NOPE = 128
D_ROPE = 64
D_V = 128
Q_RANK = 256
KV_RANK = 128
CONV_WIDTH = 512
CONV_GROUP = 64
ROPE_THETA = 10000.0
ATTN_SCALE = (D_NOPE + D_ROPE) ** -0.5
Q_SCALE = ATTN_SCALE * 1.4426950408889634
EPS = 1e-6
NEG_INF = -1e30

ADAM_LR = 0.001
ADAM_B1 = 0.9
ADAM_B2 = 0.999
ADAM_EPS = 1e-08
ADAM_WD = 0.01
ADAM_STEP = 10

LANES = 128
PAD_FRONT = LANES - N_META
K_TILE = 256
Q_TILE = 512
N_DEV = 8
VMEM_LIMIT = 56 * 1024 * 1024

IN_PAD = 3072
GRP_A = 512
N_A = Q_RANK + KV_RANK + D_ROPE
IN_PROJ = 3008
SHARD_IN = IN_PROJ // N_DEV
SHARD_IN_PAD = 384
SHARD_Q = 96
SHARD_KV = 128
SHARD_OUT = 128
SHARD_CONV = 64
SHARD_META = 128
Q_COLS = N_HEADS * (D_NOPE + D_ROPE)
KV_COLS = N_HEADS * (D_NOPE + D_V)

ROW_Q, ROW_KV, ROW_META, ROW_CONV = 0, 256, 384, 400
ROW_REPL = 408
ROW_NORM, ROW_FINAL, ROW_GQ, ROW_GKV, ROW_ATTN, ROW_CONVG, ROW_LOSS = 408, 416, 424, 426, 427, 431, 435
SMALL_ROWS = 440

PARAM_SHAPES = (
    ("meta_tokens", (N_META, SHARD_META)), ("norm_g", (1, D_MODEL)), ("w_in", (SHARD_IN, D_MODEL)),
    ("q_norm_g", (1, Q_RANK)), ("w_q_up", (SHARD_Q, Q_RANK)), ("kv_norm_g", (1, KV_RANK)),
    ("w_kv_up", (KV_RANK, SHARD_KV)), ("conv_w", (3, 1, SHARD_CONV)), ("attn_out_g", (1, CONV_WIDTH)),
    ("conv_out_g", (1, CONV_WIDTH)), ("w_out", (SHARD_OUT, D_MODEL)), ("final_norm_g", (1, D_MODEL)),
)


def _in_pieces(k):
    lo, hi = SHARD_IN * k, SHARD_IN * (k + 1)
    out = []
    if lo < N_A:
        out.append((0, min(hi, N_A) - lo, lo))
    if hi > N_A:
        s = max(lo, N_A)
        out.append((s - lo, hi - lo, s + GRP_A - N_A))
    return out


def _q_pieces(k):
    lo, hi = SHARD_Q * k, SHARD_Q * (k + 1)
    out = []
    for h in range(N_HEADS):
        base = (D_NOPE + D_ROPE) * h
        s, e = max(lo, base), min(hi, base + D_NOPE)
        if s < e:
            out.append((s - lo, e - lo, D_NOPE * h + s - base))
        s, e = max(lo, base + D_NOPE), min(hi, base + D_NOPE + D_ROPE)
        if s < e:
            out.append((s - lo, e - lo, N_HEADS * D_NOPE + D_ROPE * h + s - base - D_NOPE))
    return out


def _kv_dst(k):
    return D_NOPE * (k // 2) + (N_HEADS * D_NOPE if k % 2 else 0)


def _params(*sem):
    return pltpu.CompilerParams(dimension_semantics=sem, vmem_limit_bytes=VMEM_LIMIT)


def _rms_stats(x):
    r = lax.rsqrt(jnp.mean(x * x, axis=-1, keepdims=True) + EPS)
    return x * r, r


def _rms_bwd(gdy, xhat, r):
    return r * (gdy - xhat * jnp.mean(gdy * xhat, axis=-1, keepdims=True))


def _sigmoid(z):
    return 1.0 / (1.0 + jnp.exp(-z))


def _group_mean(x):
    i0 = lax.broadcasted_iota(jnp.int32, (LANES, LANES), 0) // CONV_GROUP
    i1 = lax.broadcasted_iota(jnp.int32, (LANES, LANES), 1) // CONV_GROUP
    m = jnp.where(i0 == i1, 1.0 / CONV_GROUP, 0.0).astype(BF16)
    hi = x.astype(BF16)
    lo = (x - hi.astype(F32)).astype(BF16)
    return jnp.dot(hi, m, preferred_element_type=F32) + jnp.dot(lo, m, preferred_element_type=F32)


_NT = (((1,), (1,)), ((), ()))
_TN = (((0,), (0,)), ((), ()))


def _dot(a, b, dims=None):
    if dims is None:
        return jnp.dot(a, b, preferred_element_type=F32)
    return lax.dot_general(a, b, dims, preferred_element_type=F32)


def _device_position():
    x, y, c = lax.axis_index("x"), lax.axis_index("y"), lax.axis_index("c")
    return x, y, c, 4 * x + 2 * y + c


def _gather_plan(srcs, slots, send_sems, recv_sems, local_sems):
    x, y, c, _ = _device_position()
    me, sibling = (x, y, c), (x, y, 1 - c)
    flip = lambda v, on: v + on - 2 * v * on
    near = (flip(x, 1 - c), flip(y, c))
    far = (flip(x, c), flip(y, 1 - c))
    diag = (1 - x, 1 - y)
    n = len(srcs)

    def slot(a, px, py, pc):
        return slots[a].at[4 * px + 2 * py + pc]

    def copy(a, k, block, to, own=False):
        return pltpu.make_async_remote_copy(
            src_ref=srcs[a] if own else slot(a, *block),
            dst_ref=slot(a, *block),
            send_sem=send_sems.at[7 * a + k],
            recv_sem=recv_sems.at[7 * a + k],
            device_id=to,
            device_id_type=pl.DeviceIdType.MESH,
        )

    def local(a):
        return pltpu.make_async_copy(srcs[a], slot(a, *me), local_sems.at[a])

    sent = [(me, sibling), (me, (*near, c)), (me, (*far, c)), ((*near, c), (*far, c)),
            ((*near, c), sibling), ((*far, c), sibling), ((*diag, c), sibling)]
    landed = [sibling, (*near, c), (*far, c), (*diag, c), (*far, 1 - c), (*near, 1 - c), (*diag, 1 - c)]

    def send(a, k):
        return copy(a, k, *sent[k], own=k < 3)

    def arrival(a, k):
        return copy(a, k, landed[k], me)

    def start():
        for a in range(n):
            local(a).start()
            for k in range(3):
                send(a, k).start()

    def relay():
        for a in range(n):
            arrival(a, 1).wait_recv()
            send(a, 3).start()
            send(a, 4).start()

    def forward():
        for k in (2, 3):
            for a in range(n):
                arrival(a, k).wait_recv()
                send(a, k + 3).start()

    def finish():
        for a in range(n):
            for k in (0, 4, 5, 6):
                arrival(a, k).wait_recv()
        for a in range(n):
            for k in range(7):
                send(a, k).wait_send()
            local(a).wait()

    return start, relay, forward, finish


def _adam_update(g, w, m, v):
    m_new = ADAM_B1 * m + (1.0 - ADAM_B1) * g
    v_new = ADAM_B2 * v + (1.0 - ADAM_B2) * (g * g)
    m_hat = m_new / (1.0 - ADAM_B1 ** ADAM_STEP)
    v_hat = v_new / (1.0 - ADAM_B2 ** ADAM_STEP)
    return -ADAM_LR * (m_hat / (jnp.sqrt(v_hat) + ADAM_EPS) + ADAM_WD * w), m_new, v_new


GX_ROWS = 512


def _adamw(grads, params, grad_x):
    names = [n for n, _ in PARAM_SHAPES]
    n_p = len(names)
    nb_seq, s, d = grad_x.shape
    rows = GX_ROWS if s % GX_ROWS == 0 else LANES
    per_seq = s // rows

    def body(*refs):
        refs[8 * n_p + 1][...] = refs[4 * n_p][...]

        @pl.when(pl.program_id(0) == 0)
        def _():
            for i in range(n_p):
                g = refs[i][...]
                w, m, v = (refs[n_p + 3 * i + j][...] for j in range(3))
                delta, m_new, v_new = _adam_update(g, w, m, v)
                for j, val in enumerate((g, delta, m_new, v_new)):
                    refs[4 * n_p + 1 + 4 * i + j][...] = val

    whole = lambda shape: pl.BlockSpec(shape, lambda i: (0,) * len(shape))
    tile = pl.BlockSpec((1, rows, d), lambda i: (i // per_seq, i % per_seq, 0))
    shapes = [shape for _, shape in PARAM_SHAPES]
    out_shape = []
    for shape in shapes:
        out_shape += [jax.ShapeDtypeStruct(shape, F32)] * 4
    out_shape.append(jax.ShapeDtypeStruct(grad_x.shape, grad_x.dtype))
    outs = pl.pallas_call(
        body,
        name="adamw",
        grid=(nb_seq * per_seq,),
        out_shape=tuple(out_shape),
        in_specs=[whole(sh) for sh in shapes] + [whole(sh) for sh in shapes for _ in range(3)] + [tile],
        out_specs=tuple(whole(sh) for sh in shapes for _ in range(4)) + (tile,),
        compiler_params=_params("arbitrary"),
    )(*[grads[n] for n in names], *[a for n in names for a in params[n]], grad_x)
    return {n: outs[4 * i:4 * i + 4] for i, n in enumerate(names)}, outs[-1]


N_CHIPS = 4


def _reduce_plan(pays, owns, r1s, sums, r2s, send1, recv1, send2, recv2, local_sems):
    x, y, c, _ = _device_position()
    sibling = (x, y, 1 - c)
    chips = [((1 - x if rj & 2 else x), (1 - y if rj & 1 else y)) for rj in range(N_CHIPS)]
    n = len(pays)

    def slot_of(rj, core):
        return 4 * chips[rj][0] + 2 * chips[rj][1] + core

    def to_sibling(a, rj):
        return pltpu.make_async_remote_copy(
            src_ref=pays[a].at[slot_of(rj, 1 - c)], dst_ref=r1s[a].at[rj],
            send_sem=send1.at[N_CHIPS * a + rj], recv_sem=recv1.at[N_CHIPS * a + rj],
            device_id=sibling, device_id_type=pl.DeviceIdType.MESH)

    def load_own(a, rj):
        return pltpu.make_async_copy(pays[a].at[slot_of(rj, c)], owns[a].at[rj], local_sems.at[2 * N_CHIPS * a + rj])

    def to_chip(a, rj):
        return pltpu.make_async_remote_copy(
            src_ref=sums[a].at[rj], dst_ref=r2s[a].at[rj],
            send_sem=send2.at[N_CHIPS * a + rj], recv_sem=recv2.at[N_CHIPS * a + rj],
            device_id=(*chips[rj], c), device_id_type=pl.DeviceIdType.MESH)

    def keep(a):
        return pltpu.make_async_copy(sums[a].at[0], r2s[a].at[0], local_sems.at[2 * N_CHIPS * a + N_CHIPS])

    def start():
        for a in range(n):
            for rj in range(N_CHIPS):
                to_sibling(a, rj).start()
                if owns[a] is not None:
                    load_own(a, rj).start()

    def combine():
        for a in range(n):
            for rj in range(N_CHIPS):
                to_sibling(a, rj).wait_recv()
                if owns[a] is not None:
                    load_own(a, rj).wait()
                    mine = owns[a][rj]
                else:
                    mine = pays[a][slot_of(rj, c)]
                sums[a][rj] = (mine.astype(F32) + r1s[a][rj].astype(F32)).astype(sums[a].dtype)
            keep(a).start()
            for rj in range(1, N_CHIPS):
                to_chip(a, rj).start()

    def finish():
        for a in range(n):
            for rj in range(1, N_CHIPS):
                to_chip(a, rj).wait_recv()
            for rj in range(N_CHIPS):
                to_sibling(a, rj).wait_send()
            for rj in range(1, N_CHIPS):
                to_chip(a, rj).wait_send()
            keep(a).wait()

    return start, combine, finish


def _reduce_scratch(shapes_dtypes, own_flags):
    out = []
    for (shape, dtype), own in zip(shapes_dtypes, own_flags):
        if own:
            out.append(pltpu.VMEM((N_CHIPS,) + shape, dtype))
        out += [pltpu.VMEM((N_CHIPS,) + shape, dtype), pltpu.VMEM((N_CHIPS,) + shape, dtype)]
    n = len(shapes_dtypes)
    out += [pltpu.SemaphoreType.DMA((N_CHIPS * n,))] * 4 + [pltpu.SemaphoreType.DMA((2 * N_CHIPS * n,))]
    return out


def _pack_small(ssmall, dwq, dwkv, dconv, dfinal, dgq, dgkv, dattn, dconvg, loss_part):
    ssmall[...] = jnp.zeros_like(ssmall)
    rep = ssmall.at[0]
    for i in range(D_MODEL // LANES):
        rep[ROW_FINAL + i:ROW_FINAL + i + 1, :] = dfinal[:, LANES * i:LANES * (i + 1)]
    for i in range(Q_RANK // LANES):
        rep[ROW_GQ + i:ROW_GQ + i + 1, :] = dgq[:, LANES * i:LANES * (i + 1)]
    rep[ROW_GKV:ROW_GKV + 1, :] = dgkv[...]
    for i in range(CONV_WIDTH // LANES):
        rep[ROW_ATTN + i:ROW_ATTN + i + 1, :] = dattn[:, LANES * i:LANES * (i + 1)]
        rep[ROW_CONVG + i:ROW_CONVG + i + 1, :] = dconvg[:, LANES * i:LANES * (i + 1)]
    rep[ROW_LOSS:ROW_LOSS + 1, :] = loss_part[...]
    for k in range(N_DEV):
        if k:
            ssmall[k, ROW_REPL:, :] = ssmall[0, ROW_REPL:, :]
        for s, e, d in _q_pieces(k):
            for i in range(Q_RANK // LANES):
                ssmall[k, ROW_Q + SHARD_Q * i + s:ROW_Q + SHARD_Q * i + e, :] = dwq[d:d + e - s, LANES * i:LANES * (i + 1)]
        ssmall[k, ROW_KV:ROW_KV + KV_RANK, :] = dwkv[:, _kv_dst(k):_kv_dst(k) + SHARD_KV]
        ssmall[k, ROW_CONV:ROW_CONV + 3, 0:SHARD_CONV] = dconv[0:3, SHARD_CONV * k:SHARD_CONV * (k + 1)]


TOKEN_TILES = 4
TAIL_ROWS = N_META + D_MODEL // LANES


def _reduce_tail(r_in, r_out, r_small, d_meta, d_norm):
    n_p = len(PARAM_SHAPES)
    names = [n for n, _ in PARAM_SHAPES]

    def body(*refs):
        rin, rout, rsmall, dmeta, dnorm = refs[:5]
        g_out = {n: refs[5 + i] for i, n in enumerate(names)}
        loss_out = refs[5 + n_p]
        stail, rtail, gsum, gtail, send_sems, recv_sems = refs[6 + n_p:]
        x, y, c, me = _device_position()
        my_chip = 2 * x + y

        for k in range(N_DEV):
            stail[k, 0:N_META, :] = dmeta[:, SHARD_META * k:SHARD_META * (k + 1)]
            for i in range(D_MODEL // LANES):
                stail[k, N_META + i:N_META + i + 1, :] = dnorm[:, LANES * i:LANES * (i + 1)]
        copies = []
        for r in range(1, N_DEV):
            peer = (1 - x if r & 4 else x, 1 - y if r & 2 else y, 1 - c if r & 1 else c)
            copies.append(pltpu.make_async_remote_copy(
                src_ref=stail.at[4 * peer[0] + 2 * peer[1] + peer[2]],
                dst_ref=rtail.at[r],
                send_sem=send_sems.at[r - 1],
                recv_sem=recv_sems.at[r - 1],
                device_id=peer,
                device_id_type=pl.DeviceIdType.MESH,
            ))
        for cp in copies:
            cp.start()
        rtail[0] = stail[me]

        g = rin[my_chip].astype(F32)
        for ch in range(1, N_CHIPS):
            g = g + rin[ch ^ my_chip].astype(F32)
        g_out["w_in"][...] = g[:SHARD_IN, :]

        g = rout[my_chip].astype(F32)
        gs = rsmall[my_chip]
        for ch in range(1, N_CHIPS):
            g = g + rout[ch ^ my_chip].astype(F32)
            gs = gs + rsmall[ch ^ my_chip]
        g_out["w_out"][...] = g
        gsum[...] = gs
        for i in range(Q_RANK // LANES):
            g_out["w_q_up"][:, LANES * i:LANES * (i + 1)] = gsum[ROW_Q + SHARD_Q * i:ROW_Q + SHARD_Q * (i + 1), :]
        g_out["w_kv_up"][...] = gsum[ROW_KV:ROW_KV + KV_RANK, :]
        for i in range(3):
            g_out["conv_w"][i] = gsum[ROW_CONV + i:ROW_CONV + i + 1, 0:SHARD_CONV]
        for name, row, width in (("final_norm_g", ROW_FINAL, D_MODEL), ("q_norm_g", ROW_GQ, Q_RANK),
                                 ("kv_norm_g", ROW_GKV, KV_RANK), ("attn_out_g", ROW_ATTN, CONV_WIDTH),
                                 ("conv_out_g", ROW_CONVG, CONV_WIDTH)):
            for i in range(width // LANES):
                g_out[name][:, LANES * i:LANES * (i + 1)] = gsum[row + i:row + i + 1, :]
        loss_out[...] = gsum[ROW_LOSS:ROW_LOSS + 1, :]

        for cp in copies:
            cp.wait_recv()
        gt = rtail[me]
        for d in range(1, N_DEV):
            gt = gt + rtail[d ^ me]
        gtail[...] = gt
        g_out["meta_tokens"][...] = gtail[0:N_META, :]
        for i in range(D_MODEL // LANES):
            g_out["norm_g"][:, LANES * i:LANES * (i + 1)] = gtail[N_META + i:N_META + i + 1, :]
        for cp in copies:
            cp.wait_send()

    vm = pl.BlockSpec(memory_space=pltpu.VMEM)
    out_shape = [jax.ShapeDtypeStruct(shape, F32) for _, shape in PARAM_SHAPES]
    out_shape.append(jax.ShapeDtypeStruct((1, LANES), F32))
    outs = pl.pallas_call(
        body,
        name="reduce_tail",
        out_shape=tuple(out_shape),
        in_specs=[vm] * 5,
        out_specs=(vm,) * len(out_shape),
        scratch_shapes=[
            pltpu.VMEM((N_DEV, TAIL_ROWS, LANES), F32),
            pltpu.VMEM((N_DEV, TAIL_ROWS, LANES), F32),
            pltpu.VMEM((SMALL_ROWS, LANES), F32),
            pltpu.VMEM((TAIL_ROWS, LANES), F32),
            pltpu.SemaphoreType.DMA((N_DEV - 1,)),
            pltpu.SemaphoreType.DMA((N_DEV - 1,)),
        ],
        compiler_params=pltpu.CompilerParams(vmem_limit_bytes=VMEM_LIMIT),
    )(r_in, r_out, r_small, d_meta, d_norm)
    return {n: outs[i] for i, n in enumerate(names)}, outs[-1]


def _prep_gather(x, meta, norm_g, w_in_t):
    nb_seq, s, d = x.shape
    nb = s // LANES + 1
    relay_step = nb_seq * (nb - 1) // 2
    forward_step = nb_seq * (nb - 1) - 1
    finish_step = nb_seq * (nb - 1)

    def body(x_ref, meta_ref, g_ref, win_ref, u_ref, w_in_p, meta_f,
             sbig, ssmall, gbig, gsmall, send_sems, recv_sems, local_sems):
        jj, b = pl.program_id(0), pl.program_id(1)
        t = jj * nb_seq + b

        def plan():
            return _gather_plan((sbig, ssmall), (gbig, gsmall), send_sems, recv_sems, local_sems)

        @pl.when(t == 0)
        def _():
            sbig[0:SHARD_IN, :] = win_ref[...].astype(BF16)
            sbig[SHARD_IN:, :] = jnp.zeros((SHARD_IN_PAD - SHARD_IN, d), BF16)
            ssmall[...] = meta_ref[...]
            plan()[0]()

        @pl.when(t == relay_step)
        def _():
            plan()[1]()

        @pl.when(t == forward_step)
        def _():
            plan()[2]()

        @pl.when(t == finish_step)
        def _():
            plan()[3]()
            w_in_p[N_A:GRP_A, :] = jnp.zeros((GRP_A - N_A, d), BF16)
            for k in range(N_DEV):
                for s0, e0, d0 in _in_pieces(k):
                    w_in_p[d0:d0 + e0 - s0, :] = gbig[k, s0:e0, :]
                meta_f[:, SHARD_META * k:SHARD_META * (k + 1)] = gsmall[k]

        def norm(h):
            hhat, _ = _rms_stats(h)
            return (hhat * g_ref[...]).astype(BF16)

        @pl.when(jj < nb - 1)
        def _():
            u_ref[...] = norm(x_ref[0])

        @pl.when(jj == nb - 1)
        def _():
            u_ref[0:PAD_FRONT, :] = jnp.zeros((PAD_FRONT, d), BF16)
            u_ref[PAD_FRONT:LANES, :] = norm(meta_f[...])

    whole = lambda shape: pl.BlockSpec(shape, lambda jj, b: (0,) * len(shape))
    return pl.pallas_call(
        body,
        name="prep_norm_gather",
        grid=(nb, nb_seq),
        in_specs=[
            pl.BlockSpec((1, LANES, d), lambda jj, b: (b, jnp.minimum(jj, nb - 2), 0)),
            whole(meta.shape), whole(norm_g.shape), whole(w_in_t.shape),
        ],
        out_specs=(pl.BlockSpec((LANES, d), lambda jj, b: (b * nb + (jj + 1) % nb, 0)),
                   whole((IN_PAD, d)), whole((N_META, d))),
        out_shape=(jax.ShapeDtypeStruct((nb_seq * nb * LANES, d), BF16),
                   jax.ShapeDtypeStruct((IN_PAD, d), BF16),
                   jax.ShapeDtypeStruct((N_META, d), F32)),
        scratch_shapes=[
            pltpu.VMEM((SHARD_IN_PAD, d), BF16),
            pltpu.VMEM((N_META, SHARD_META), F32),
            pltpu.VMEM((N_DEV, SHARD_IN_PAD, d), BF16),
            pltpu.VMEM((N_DEV, N_META, SHARD_META), F32),
            pltpu.SemaphoreType.DMA((14,)),
            pltpu.SemaphoreType.DMA((14,)),
            pltpu.SemaphoreType.DMA((2,)),
        ],
        compiler_params=_params("arbitrary", "arbitrary"),
    )(x, meta, norm_g, w_in_t)


def _in_proj_gather(u, w_in_p, w_q, w_kv, w_out, conv_w, bm, bn):
    m, k_dim = u.shape
    n = w_in_p.shape[0]
    steps = (m // bm) * (n // bn)
    relay_step, forward_step = steps // 3, 2 * steps // 3
    qkv_shape = (SHARD_Q + KV_RANK, Q_RANK)

    def body(a_ref, b_ref, wq_ref, wkv_ref, wout_ref, conv_ref, o_ref, w_q_p, w_kv_p, w_out_f, conv_f,
             sqkv, sout, sconv, gqkv, gout, gconv, send_sems, recv_sems, local_sems):
        t = pl.program_id(0) * (n // bn) + pl.program_id(1)

        def plan():
            return _gather_plan((sqkv, sout, sconv), (gqkv, gout, gconv), send_sems, recv_sems, local_sems)

        @pl.when(t == 0)
        def _():
            sqkv[...] = jnp.zeros_like(sqkv)
            sqkv[0:SHARD_Q, :] = wq_ref[...].astype(BF16)
            sqkv[SHARD_Q:, 0:SHARD_KV] = wkv_ref[...].astype(BF16)
            sout[...] = wout_ref[...].astype(BF16)
            sconv[...] = jnp.zeros_like(sconv)
            for i in range(3):
                sconv[i:i + 1, 0:SHARD_CONV] = conv_ref[i]
            plan()[0]()

        @pl.when(t == relay_step)
        def _():
            plan()[1]()

        @pl.when(t == forward_step)
        def _():
            plan()[2]()

        o_ref[...] = _dot(a_ref[...], b_ref[...], _NT).astype(o_ref.dtype)

        @pl.when(t == steps - 1)
        def _():
            plan()[3]()
            conv_f[...] = jnp.zeros_like(conv_f)
            for k in range(N_DEV):
                for s0, e0, d0 in _q_pieces(k):
                    w_q_p[d0:d0 + e0 - s0, :] = gqkv[k, s0:e0, :]
                w_kv_p[:, _kv_dst(k):_kv_dst(k) + SHARD_KV] = gqkv[k, SHARD_Q:, 0:SHARD_KV]
                w_out_f[SHARD_OUT * k:SHARD_OUT * (k + 1), :] = gout[k]
                conv_f[0:3, SHARD_CONV * k:SHARD_CONV * (k + 1)] = gconv[k, 0:3, 0:SHARD_CONV]

    whole = lambda shape: pl.BlockSpec(shape, lambda i, j: (0,) * len(shape))
    return pl.pallas_call(
        body,
        name="in_proj_gather",
        grid=(m // bm, n // bn),
        in_specs=[pl.BlockSpec((bm, k_dim), lambda i, j: (i, 0)), pl.BlockSpec((bn, k_dim), lambda i, j: (j, 0)),
                  whole(w_q.shape), whole(w_kv.shape), whole(w_out.shape), whole(conv_w.shape)],
        out_specs=(pl.BlockSpec((bm, bn), lambda i, j: (i, j)),
                   whole((Q_COLS, Q_RANK)), whole((KV_RANK, KV_COLS)), whole((D_MODEL, D_MODEL)),
                   whole((8, CONV_WIDTH))),
        out_shape=(jax.ShapeDtypeStruct((m, n), BF16),
                   jax.ShapeDtypeStruct((Q_COLS, Q_RANK), BF16),
                   jax.ShapeDtypeStruct((KV_RANK, KV_COLS), BF16),
                   jax.ShapeDtypeStruct((D_MODEL, D_MODEL), BF16),
                   jax.ShapeDtypeStruct((8, CONV_WIDTH), F32)),
        scratch_shapes=[
            pltpu.VMEM(qkv_shape, BF16),
            pltpu.VMEM((SHARD_OUT, D_MODEL), BF16),
            pltpu.VMEM((8, LANES), F32),
            pltpu.VMEM((N_DEV,) + qkv_shape, BF16),
            pltpu.VMEM((N_DEV, SHARD_OUT, D_MODEL), BF16),
            pltpu.VMEM((N_DEV, 8, LANES), F32),
            pltpu.SemaphoreType.DMA((21,)),
            pltpu.SemaphoreType.DMA((21,)),
            pltpu.SemaphoreType.DMA((3,)),
        ],
        compiler_params=_params("arbitrary", "arbitrary"),
    )(u, w_in_p, w_q, w_kv, w_out, conv_w)


def _rope_tables(tp):
    half = D_ROPE // 2
    inv_freq = 1.0 / (ROPE_THETA ** (jnp.arange(half, dtype=F32) / half))
    pos = (jnp.arange(tp) - PAD_FRONT).astype(F32)
    ang = pos[:, None] * inv_freq[None, :]
    cos = jnp.tile(jnp.cos(ang), (1, LANES // half))
    sin = jnp.tile(jnp.sin(ang), (1, LANES // half))
    first = (jnp.arange(LANES) % D_ROPE) < half
    return cos, jnp.where(first, -sin, 0.0), jnp.where(first, 0.0, sin)


def _rope(t, cos, sa, sb):
    return t * cos + pltpu.roll(t, LANES - D_ROPE // 2, 1) * sa + pltpu.roll(t, D_ROPE // 2, 1) * sb


def _rope_t(t, cos, sa, sb):
    return t * cos + pltpu.roll(t * sa, D_ROPE // 2, 1) + pltpu.roll(t * sb, LANES - D_ROPE // 2, 1)


def _qkv_fwd(p, wq, wkv, gq, gkv, tables, nb_seq, tp):
    ht = tp // 2

    def body(pa_ref, wq_ref, wkv_ref, gq_ref, gkv_ref, cos_ref, sa_ref, sb_ref, q_ref, k_ref, v_ref):
        pa = pa_ref[...].astype(F32)
        cq_hat, _ = _rms_stats(pa[:, :Q_RANK])
        ckv_hat, _ = _rms_stats(pa[:, Q_RANK:Q_RANK + KV_RANK])
        q = _dot((cq_hat * gq_ref[...]).astype(BF16), wq_ref[...], _NT) * Q_SCALE
        kv = _dot((ckv_hat * gkv_ref[...]).astype(BF16), wkv_ref[...])
        tabs = (cos_ref[...], sa_ref[...], sb_ref[...])
        lane = lax.broadcasted_iota(jnp.int32, (ht, LANES), 1)
        low = lane < D_ROPE
        mark = lane == D_ROPE
        row = (pl.program_id(0) % 2) * ht + lax.broadcasted_iota(jnp.int32, (ht, LANES), 0)
        k_pe = jnp.where(mark & (row < PAD_FRONT), NEG_INF, _rope(pa[:, Q_RANK + KV_RANK:], *tabs))
        one = jnp.where(mark & (row >= PAD_FRONT), 1.0, 0.0)
        pairs = [_rope(q[:, N_HEADS * D_NOPE + LANES * i:N_HEADS * D_NOPE + LANES * (i + 1)], *tabs) for i in range(2)]
        for h in range(N_HEADS):
            pair = pairs[h // 2]
            if h % 2:
                pair = pltpu.roll(pair, D_ROPE, 1)
            pe = jnp.where(low, pair, one)
            q_ref[0, h] = jnp.concatenate([q[:, D_NOPE * h:D_NOPE * (h + 1)], pe], axis=1).astype(BF16)
            k_ref[0, h] = jnp.concatenate([kv[:, D_NOPE * h:D_NOPE * (h + 1)], k_pe], axis=1).astype(BF16)
            v_ref[0, h] = kv[:, N_HEADS * D_NOPE + D_V * h:N_HEADS * D_NOPE + D_V * (h + 1)].astype(BF16)

    full = lambda a: pl.BlockSpec(a.shape, lambda i: (0,) * a.ndim)
    tab = pl.BlockSpec((ht, LANES), lambda i: (i % 2, 0))
    qk = pl.BlockSpec((1, N_HEADS, ht, 2 * LANES), lambda i: (i // 2, 0, i % 2, 0))
    return pl.pallas_call(
        body,
        name="qkv_fwd",
        grid=(2 * nb_seq,),
        in_specs=[pl.BlockSpec((ht, GRP_A), lambda i: (i, 0)), full(wq), full(wkv), full(gq), full(gkv), tab, tab, tab],
        out_specs=(qk, qk, pl.BlockSpec((1, N_HEADS, ht, D_V), lambda i: (i // 2, 0, i % 2, 0))),
        out_shape=(
            jax.ShapeDtypeStruct((nb_seq, N_HEADS, tp, 2 * LANES), BF16),
            jax.ShapeDtypeStruct((nb_seq, N_HEADS, tp, 2 * LANES), BF16),
            jax.ShapeDtypeStruct((nb_seq, N_HEADS, tp, D_V), BF16),
        ),
        compiler_params=_params("parallel"),
    )(p, wq, wkv, gq, gkv, *tables)


def _attn_fwd(q, k, v, p, g_attn):
    nb_seq, _, tp, _ = q.shape

    def body(q_ref, k_ref, v_ref, z_ref, g_ref, y_ref, o_ref, lse_ref):
        g = g_ref[...]
        for r0 in range(0, tp, Q_TILE):
            nq = min(Q_TILE, tp - r0)
            kend = r0 + nq
            qq = q_ref[0, 0, r0:kend, :]
            sd = _dot(qq, k_ref[0, 0, r0:kend, :], _NT)
            causal = (lax.broadcasted_iota(jnp.int32, (nq, nq), 1) <= lax.broadcasted_iota(jnp.int32, (nq, nq), 0))
            sd = jnp.where(causal, sd, NEG_INF)
            m = jnp.max(sd, axis=-1, keepdims=True)
            if r0:
                so = _dot(qq, k_ref[0, 0, 0:r0, :], _NT)
                m = jnp.maximum(m, jnp.max(so, axis=-1, keepdims=True))
            ed = jnp.exp2(sd - m)
            l = jnp.sum(ed, axis=-1, keepdims=True)
            o = _dot(ed.astype(BF16), v_ref[0, 0, r0:kend, :])
            if r0:
                eo = jnp.exp2(so - m)
                l = l + jnp.sum(eo, axis=-1, keepdims=True)
                o = o + _dot(eo.astype(BF16), v_ref[0, 0, 0:r0, :])
            o = o * (1.0 / l)
            o_ref[0, 0, r0:kend, :] = o
            lse_ref[0, 0, r0:kend, :] = jnp.broadcast_to(m + jnp.log2(l), (nq, LANES))
            ohat, _ = _rms_stats(o)
            z = z_ref[r0:kend, :].astype(F32)
            y_ref[r0:kend, :] = (ohat * g * (z * _sigmoid(z))).astype(BF16)

    qk = pl.BlockSpec((1, 1, tp, 2 * LANES), lambda b, h: (b, h, 0, 0))
    hv = pl.BlockSpec((1, 1, tp, D_V), lambda b, h: (b, h, 0, 0))
    return pl.pallas_call(
        body,
        name="attn_fwd",
        grid=(nb_seq, N_HEADS),
        in_specs=[qk, qk, hv,
                  pl.BlockSpec((tp, LANES), lambda b, h: (b, GRP_A // LANES + h)),
                  pl.BlockSpec((1, LANES), lambda b, h: (0, h))],
        out_specs=(pl.BlockSpec((tp, LANES), lambda b, h: (b, h)), hv, hv),
        out_shape=(
            jax.ShapeDtypeStruct((nb_seq * tp, N_HEADS * D_V), BF16),
            jax.ShapeDtypeStruct((nb_seq, N_HEADS, tp, D_V), F32),
            jax.ShapeDtypeStruct((nb_seq, N_HEADS, tp, LANES), F32),
        ),
        compiler_params=_params("parallel", "parallel"),
    )(q, k, v, p, g_attn)


_CONV_COL0 = (GRP_A + N_HEADS * D_V) // LANES


def _conv_specs(tp, order):
    cols = CONV_WIDTH // LANES
    return [pl.BlockSpec((tp, LANES), functools.partial(
        lambda a, b, off: order(a, b, off), off=_CONV_COL0 + i * cols)) for i in range(4)]


def _conv_fwd(p, conv_w, g_conv, nb_seq, tp):
    def body(b_ref, c_ref, h_ref, z_ref, w_ref, g_ref, y_ref):
        cc = c_ref[...].astype(F32) * h_ref[...].astype(F32)
        row = lax.broadcasted_iota(jnp.int32, (tp, LANES), 0)
        s1 = jnp.where(row >= 1, pltpu.roll(cc, 1, 0), 0.0)
        s2 = jnp.where(row >= 2, pltpu.roll(cc, 2, 0), 0.0)
        yc = b_ref[...].astype(F32) * (w_ref[0:1, :] * s2 + w_ref[1:2, :] * s1 + w_ref[2:3, :] * cc)
        r = lax.rsqrt(_group_mean(yc * yc) + EPS)
        z = z_ref[...].astype(F32)
        y_ref[...] = (yc * r * g_ref[...] * (z * _sigmoid(z))).astype(BF16)

    return pl.pallas_call(
        body,
        name="conv_fwd",
        grid=(nb_seq, CONV_WIDTH // LANES),
        in_specs=_conv_specs(tp, lambda b, t, off: (b, off + t)) + [
            pl.BlockSpec((8, LANES), lambda b, t: (0, t)),
            pl.BlockSpec((1, LANES), lambda b, t: (0, t))],
        out_specs=pl.BlockSpec((tp, LANES), lambda b, t: (b, t)),
        out_shape=jax.ShapeDtypeStruct((nb_seq * tp, CONV_WIDTH), BF16),
        compiler_params=_params("parallel", "parallel"),
    )(p, p, p, p, conv_w, g_conv)


def _token_copy(hbm, b, k, ts, buf, sem, to_hbm=False):
    lo, hi = max(k * ts - LANES, 0), (k + 1) * ts - LANES
    off = lo - (k * ts - LANES)
    src, dst = hbm.at[b, pl.ds(lo, hi - lo)], buf.at[pl.ds(off, hi - lo)]
    if to_hbm:
        src, dst = dst, src
    return pltpu.make_async_copy(src, dst, sem)


def _for_tile(k, nt, fn):
    for kk in range(nt):
        @pl.when(k == kk)
        def _(kk=kk):
            fn(kk)


def _out_proj_loss(ya, yc, w_out, x, target, g_final, nt):
    nb_seq, s, d = x.shape
    r, ka = ya.shape
    ts = (s + LANES) // nt
    steps = nb_seq * nt

    def body(a_ref, c_ref, w_ref, x_hbm, t_hbm, g_ref, dhb_ref, dg_ref, loss_ref,
             xbuf, tbuf, acc_ref, sems):
        i = pl.program_id(0)
        b, k = i // nt, i % nt

        @pl.when(i == 0)
        def _():
            acc_ref[...] = jnp.zeros_like(acc_ref)
            dg_ref[...] = jnp.zeros_like(dg_ref)

        slot = i % 2

        def fetch(seq, kk, sl):
            return [_token_copy(x_hbm, seq, kk, ts, xbuf.at[sl], sems.at[sl, 0]),
                    _token_copy(t_hbm, seq, kk, ts, tbuf.at[sl], sems.at[sl, 1])]

        def start(seq, sl, kk):
            if kk == 0:
                xbuf[sl, 0:LANES, :] = jnp.zeros((LANES, d), F32)
                tbuf[sl, 0:LANES, :] = jnp.zeros((LANES, d), F32)
            for cp in fetch(seq, kk, sl):
                cp.start()

        @pl.when(i == 0)
        def _():
            start(0, 0, 0)

        @pl.when(i + 1 < steps)
        def _():
            _for_tile((i + 1) % nt, nt, functools.partial(start, (i + 1) // nt, 1 - slot))

        mix = _dot(a_ref[...], w_ref[0:ka, :]) + _dot(c_ref[...], w_ref[ka:, :])
        _for_tile(k, nt, lambda kk: [cp.wait() for cp in fetch(b, kk, slot)])

        real = (lax.broadcasted_iota(jnp.int32, (ts, d), 0) >= LANES) | (k > 0)
        g = g_ref[...]
        hhat, rstd = _rms_stats(xbuf[slot] + mix)
        e = jnp.where(real, hhat * g - tbuf[slot], 0.0)
        acc_ref[...] += jnp.sum(e * e, axis=0, keepdims=True)
        dy = e * (1.0 / d)
        dg_ref[...] += jnp.sum(dy * hhat, axis=0, keepdims=True)
        dhb_ref[...] = _rms_bwd(g * dy, hhat, rstd).astype(BF16)

        @pl.when(i == steps - 1)
        def _():
            total = jnp.sum(acc_ref[...], axis=1, keepdims=True)
            loss_ref[...] = jnp.broadcast_to((0.5 / d) * total, loss_ref.shape)

    hbm = pl.BlockSpec(memory_space=pl.ANY)
    row = pl.BlockSpec((ts, d), lambda i: (i, 0))
    vec = pl.BlockSpec((1, d), lambda i: (0, 0))
    return pl.pallas_call(
        body,
        name="out_proj_loss",
        grid=(steps,),
        in_specs=[pl.BlockSpec((ts, ka), lambda i: (i, 0)), pl.BlockSpec((ts, yc.shape[1]), lambda i: (i, 0)),
                  pl.BlockSpec(w_out.shape, lambda i: (0, 0)), hbm, hbm, vec],
        out_specs=(row, vec, pl.BlockSpec((1, LANES), lambda i: (0, 0))),
        out_shape=(
            jax.ShapeDtypeStruct((r, d), BF16),
            jax.ShapeDtypeStruct((1, d), F32),
            jax.ShapeDtypeStruct((1, LANES), F32),
        ),
        scratch_shapes=[pltpu.VMEM((2, ts, d), F32), pltpu.VMEM((2, ts, d), F32), pltpu.VMEM((1, d), F32),
                        pltpu.SemaphoreType.DMA((2, 2))],
        compiler_params=_params("arbitrary"),
    )(ya, yc, w_out, x, target, g_final)


def _out_proj_bwd(dhb, w_out, ya, yc, bm):
    r, d = dhb.shape
    ka = ya.shape[1]
    n_mix = w_out.shape[0]
    last = r // bm - 1

    def body(dh_ref, w_ref, a_ref, c_ref, dcat_ref, dw_ref, acc_ref):
        @pl.when(pl.program_id(0) == 0)
        def _():
            acc_ref[...] = jnp.zeros_like(acc_ref)

        dh = dh_ref[...]
        dcat_ref[...] = _dot(dh, w_ref[...], _NT).astype(BF16)
        acc_ref[0:ka, :] += _dot(a_ref[...], dh, _TN)
        acc_ref[ka:, :] += _dot(c_ref[...], dh, _TN)

        @pl.when(pl.program_id(0) == last)
        def _():
            dw_ref[...] = acc_ref[...].astype(BF16)

    return pl.pallas_call(
        body,
        name="out_proj_bwd",
        grid=(r // bm,),
        in_specs=[pl.BlockSpec((bm, d), lambda i: (i, 0)), pl.BlockSpec(w_out.shape, lambda i: (0, 0)),
                  pl.BlockSpec((bm, ka), lambda i: (i, 0)), pl.BlockSpec((bm, yc.shape[1]), lambda i: (i, 0))],
        out_specs=(pl.BlockSpec((bm, n_mix), lambda i: (i, 0)),
                   pl.BlockSpec((n_mix, d), lambda i: (0, 0))),
        out_shape=(jax.ShapeDtypeStruct((r, n_mix), BF16),
                   jax.ShapeDtypeStruct((n_mix, d), BF16)),
        scratch_shapes=[pltpu.VMEM((n_mix, d), F32)],
        compiler_params=_params("arbitrary"),
    )(dhb, w_out, ya, yc)


def _attn_bwd(q, k, v, o, lse, dcat, p, g_attn):
    nb_seq, _, tp, _ = q.shape

    def body(q_ref, k_ref, v_ref, o_ref, lse_ref, dy_ref, z_ref, g_ref,
             dq_ref, dk_ref, dv_ref, dz_ref, dg_ref, dq_acc):
        @pl.when(pl.program_id(1) == 0)
        def _():
            dg_ref[...] = jnp.zeros_like(dg_ref)

        g = g_ref[...]
        z = z_ref[...].astype(F32)
        o = o_ref[0, 0]
        dy = dy_ref[...].astype(F32)
        sig = _sigmoid(z)
        ohat, r = _rms_stats(o)
        don = dy * (z * sig)
        dz_ref[...] = (dy * (ohat * g) * (sig * (1.0 + z * (1.0 - sig)))).astype(BF16)
        dg_ref[...] += jnp.sum(don * ohat, axis=0, keepdims=True)
        do = _rms_bwd(g * don, ohat, r)
        dvec = jnp.sum(do * o, axis=-1, keepdims=True)
        dob = do.astype(BF16)
        lse_col = lse_ref[0, 0, :, 0:1]
        dq_acc[...] = jnp.zeros_like(dq_acc)
        for k0 in range(0, tp, K_TILE):
            nk = min(K_TILE, tp - k0)
            nq = tp - k0
            qq = q_ref[0, 0, k0:, :]
            kk = k_ref[0, 0, k0:k0 + nk, :]
            causal = (lax.broadcasted_iota(jnp.int32, (nq, nk), 1) <= lax.broadcasted_iota(jnp.int32, (nq, nk), 0))
            pr = jnp.where(causal, jnp.exp2(_dot(qq, kk, _NT) - lse_col[k0:]), 0.0)
            dp = _dot(dob[k0:], v_ref[0, 0, k0:k0 + nk, :], _NT)
            ds = (pr * (dp - dvec[k0:])).astype(BF16)
            dv_ref[0, 0, k0:k0 + nk, :] = _dot(pr.astype(BF16), dob[k0:], _TN).astype(BF16)
            dk_ref[0, 0, k0:k0 + nk, :] = (_dot(ds, qq, _TN) * (ATTN_SCALE / Q_SCALE)).astype(BF16)
            dq_acc[k0:, :] += _dot(ds, kk)
        dq_ref[0, 0] = (dq_acc[...] * ATTN_SCALE).astype(BF16)

    qk = pl.BlockSpec((1, 1, tp, 2 * LANES), lambda h, b: (b, h, 0, 0))
    hv = pl.BlockSpec((1, 1, tp, D_V), lambda h, b: (b, h, 0, 0))
    col = pl.BlockSpec((tp, LANES), lambda h, b: (b, h))
    return pl.pallas_call(
        body,
        name="attn_bwd",
        grid=(N_HEADS, nb_seq),
        in_specs=[qk, qk, hv, hv, hv, col,
                  pl.BlockSpec((tp, LANES), lambda h, b: (b, GRP_A // LANES + h)),
                  pl.BlockSpec((1, LANES), lambda h, b: (0, h))],
        out_specs=(qk, qk, hv, col, pl.BlockSpec((1, LANES), lambda h, b: (0, h))),
        out_shape=(
            jax.ShapeDtypeStruct((nb_seq, N_HEADS, tp, 2 * LANES), BF16),
            jax.ShapeDtypeStruct((nb_seq, N_HEADS, tp, 2 * LANES), BF16),
            jax.ShapeDtypeStruct((nb_seq, N_HEADS, tp, D_V), BF16),
            jax.ShapeDtypeStruct((nb_seq * tp, N_HEADS * D_V), BF16),
            jax.ShapeDtypeStruct((1, N_HEADS * D_V), F32),
        ),
        scratch_shapes=[pltpu.VMEM((tp, 2 * LANES), F32)],
        compiler_params=_params("arbitrary", "arbitrary"),
    )(q, k, v, o, lse, dcat, p, g_attn)


def _qkv_bwd(p, dq, dk, dv, wq, wkv, gq, gkv, tables):
    nb_seq, _, tp, _ = dq.shape
    ht = tp // 2

    def body(pa_ref, dq_ref, dk_ref, dv_ref, wq_ref, wkv_ref, gq_ref, gkv_ref, cos_ref, sa_ref, sb_ref,
             dpa_ref, dwq_ref, dwkv_ref, dgq_ref, dgkv_ref):
        @pl.when(pl.program_id(0) == 0)
        def _():
            dwq_ref[...] = jnp.zeros_like(dwq_ref)
            dwkv_ref[...] = jnp.zeros_like(dwkv_ref)
            dgq_ref[...] = jnp.zeros_like(dgq_ref)
            dgkv_ref[...] = jnp.zeros_like(dgkv_ref)

        pa = pa_ref[...].astype(F32)
        gq, gkv = gq_ref[...], gkv_ref[...]
        cq_hat, rq = _rms_stats(pa[:, :Q_RANK])
        ckv_hat, rkv = _rms_stats(pa[:, Q_RANK:Q_RANK + KV_RANK])
        tabs = (cos_ref[...], sa_ref[...], sb_ref[...])

        pe = [dq_ref[0, h, :, D_NOPE:].astype(F32) for h in range(N_HEADS)]
        pairs = [_rope_t(pe[2 * i] + pltpu.roll(pe[2 * i + 1], D_ROPE, 1), *tabs).astype(BF16) for i in range(2)]
        dq_flat = jnp.concatenate([dq_ref[0, h, :, :D_NOPE] for h in range(N_HEADS)] + pairs, axis=1)
        dwq_ref[...] += _dot(dq_flat, (cq_hat * gq).astype(BF16), _TN)
        dcqn = _dot(dq_flat, wq_ref[...])
        dgq_ref[...] += jnp.sum(dcqn * cq_hat, axis=0, keepdims=True)
        dcq = _rms_bwd(gq * dcqn, cq_hat, rq)

        dkv_flat = jnp.concatenate([dk_ref[0, h, :, :D_NOPE] for h in range(N_HEADS)]
                                   + [dv_ref[0, h] for h in range(N_HEADS)], axis=1)
        dwkv_ref[...] += _dot((ckv_hat * gkv).astype(BF16), dkv_flat, _TN)
        dckvn = _dot(dkv_flat, wkv_ref[...], _NT)
        dgkv_ref[...] += jnp.sum(dckvn * ckv_hat, axis=0, keepdims=True)
        dckv = _rms_bwd(gkv * dckvn, ckv_hat, rkv)

        dk_pe = dk_ref[0, 0, :, D_NOPE:].astype(F32)
        for h in range(1, N_HEADS):
            dk_pe = dk_pe + dk_ref[0, h, :, D_NOPE:].astype(F32)
        dk_pe = jnp.where(lax.broadcasted_iota(jnp.int32, (ht, LANES), 1) < D_ROPE, dk_pe, 0.0)
        dpa_ref[...] = jnp.concatenate([dcq, dckv, _rope_t(dk_pe, *tabs)], axis=1).astype(BF16)

    full = lambda a: pl.BlockSpec(a.shape, lambda i: (0,) * a.ndim)
    tab = pl.BlockSpec((ht, LANES), lambda i: (i % 2, 0))
    qk = pl.BlockSpec((1, N_HEADS, ht, 2 * LANES), lambda i: (i // 2, 0, i % 2, 0))
    acc = lambda shape: pl.BlockSpec(shape, lambda i: (0, 0))
    return pl.pallas_call(
        body,
        name="qkv_bwd",
        grid=(2 * nb_seq,),
        in_specs=[pl.BlockSpec((ht, GRP_A), lambda i: (i, 0)), qk, qk,
                  pl.BlockSpec((1, N_HEADS, ht, D_V), lambda i: (i // 2, 0, i % 2, 0)),
                  full(wq), full(wkv), full(gq), full(gkv), tab, tab, tab],
        out_specs=(pl.BlockSpec((ht, GRP_A), lambda i: (i, 0)),
                   acc(wq.shape), acc(wkv.shape), acc((1, Q_RANK)), acc((1, KV_RANK))),
        out_shape=(
            jax.ShapeDtypeStruct((nb_seq * tp, GRP_A), BF16),
            jax.ShapeDtypeStruct(wq.shape, F32),
            jax.ShapeDtypeStruct(wkv.shape, F32),
            jax.ShapeDtypeStruct((1, Q_RANK), F32),
            jax.ShapeDtypeStruct((1, KV_RANK), F32),
        ),
        compiler_params=_params("arbitrary"),
    )(p, dq, dk, dv, wq, wkv, gq, gkv, *tables)


def _conv_bwd(p, dcat, conv_w, g_conv, nb_seq, tp):
    cols = CONV_WIDTH // LANES

    def body(b_ref, c_ref, h_ref, z_ref, dy_ref, w_ref, g_ref,
             db_ref, dc_ref, dh_ref, dz_ref, dw_ref, dg_ref):
        @pl.when(pl.program_id(1) == 0)
        def _():
            dw_ref[...] = jnp.zeros_like(dw_ref)
            dg_ref[...] = jnp.zeros_like(dg_ref)

        cb, c, h = b_ref[...].astype(F32), c_ref[...].astype(F32), h_ref[...].astype(F32)
        z, dy = z_ref[...].astype(F32), dy_ref[...].astype(F32)
        g = g_ref[...]
        w0, w1, w2 = w_ref[0:1, :], w_ref[1:2, :], w_ref[2:3, :]
        cc = c * h
        row = lax.broadcasted_iota(jnp.int32, (tp, LANES), 0)
        s1 = jnp.where(row >= 1, pltpu.roll(cc, 1, 0), 0.0)
        s2 = jnp.where(row >= 2, pltpu.roll(cc, 2, 0), 0.0)
        dwc = w0 * s2 + w1 * s1 + w2 * cc
        yc = cb * dwc
        r = lax.rsqrt(_group_mean(yc * yc) + EPS)
        ychat = yc * r
        sig = _sigmoid(z)
        dz_ref[...] = (dy * (ychat * g) * (sig * (1.0 + z * (1.0 - sig)))).astype(BF16)
        dyn = dy * (z * sig)
        dg_ref[...] += jnp.sum(dyn * ychat, axis=0, keepdims=True)
        gd = g * dyn
        dyc = r * (gd - ychat * _group_mean(gd * ychat))
        db_ref[...] = (dyc * dwc).astype(BF16)
        ddw = dyc * cb
        dw_ref[0:1, :] += jnp.sum(ddw * s2, axis=0, keepdims=True)
        dw_ref[1:2, :] += jnp.sum(ddw * s1, axis=0, keepdims=True)
        dw_ref[2:3, :] += jnp.sum(ddw * cc, axis=0, keepdims=True)
        u1 = jnp.where(row <= tp - 2, pltpu.roll(ddw, tp - 1, 0), 0.0)
        u2 = jnp.where(row <= tp - 3, pltpu.roll(ddw, tp - 2, 0), 0.0)
        dcc = w2 * ddw + w1 * u1 + w0 * u2
        dc_ref[...] = (dcc * h).astype(BF16)
        dh_ref[...] = (dcc * c).astype(BF16)

    col = pl.BlockSpec((tp, LANES), lambda t, b: (b, t))
    out = jax.ShapeDtypeStruct((nb_seq * tp, CONV_WIDTH), BF16)
    return pl.pallas_call(
        body,
        name="conv_bwd",
        grid=(cols, nb_seq),
        in_specs=_conv_specs(tp, lambda t, b, off: (b, off + t)) + [
            pl.BlockSpec((tp, LANES), lambda t, b: (b, N_HEADS * D_V // LANES + t)),
            pl.BlockSpec((8, LANES), lambda t, b: (0, t)),
            pl.BlockSpec((1, LANES), lambda t, b: (0, t))],
        out_specs=(col, col, col, col,
                   pl.BlockSpec((8, LANES), lambda t, b: (0, t)), pl.BlockSpec((1, LANES), lambda t, b: (0, t))),
        out_shape=(out, out, out, out,
                   jax.ShapeDtypeStruct((8, CONV_WIDTH), F32), jax.ShapeDtypeStruct((1, CONV_WIDTH), F32)),
        compiler_params=_params("arbitrary", "arbitrary"),
    )(p, p, p, p, dcat, conv_w, g_conv)


def _input_bwd(dps, w_in, x, meta, dh, norm_g, nt, send_in):
    nb_seq, s, d = x.shape
    r, kb = dps[0].shape
    ts = (s + LANES) // nt
    steps = nb_seq * nt
    n_dp = len(dps)
    in_slot = send_in.shape[1:]

    def body(*refs):
        dp_refs, w_ref, x_hbm, meta_ref, dh_ref, g_ref, pay_ref = refs[:n_dp], *refs[n_dp:n_dp + 6]
        o = n_dp + 6
        gx_hbm, dmeta_ref, dg_ref, r2_in = refs[o:o + 4]
        xbuf, gxbuf, tok_sems, own_in, r1_in, sum_in = refs[o + 4:o + 10]
        sems = refs[o + 10:]
        i = pl.program_id(0)
        b, k = i // nt, i % nt

        def plan():
            return _reduce_plan((pay_ref,), (own_in,), (r1_in,), (sum_in,), (r2_in,), *sems)

        @pl.when(i == 0)
        def _():
            dmeta_ref[...] = jnp.zeros_like(dmeta_ref)
            dg_ref[...] = jnp.zeros_like(dg_ref)
            plan()[0]()

        @pl.when(i == 1)
        def _():
            plan()[1]()

        def start(kk):
            if kk == 0:
                xbuf[0:PAD_FRONT, :] = jnp.zeros((PAD_FRONT, d), F32)
                xbuf[PAD_FRONT:LANES, :] = meta_ref[...]
            _token_copy(x_hbm, b, kk, ts, xbuf, tok_sems.at[0]).start()

        _for_tile(k, nt, start)
        du = _dot(dp_refs[0][...], w_ref[0:kb, :])
        for j in range(1, n_dp):
            du = du + _dot(dp_refs[j][...], w_ref[kb * j:kb * (j + 1), :])
        _for_tile(k, nt, lambda kk: _token_copy(x_hbm, b, kk, ts, xbuf, tok_sems.at[0]).wait())

        g = g_ref[...]
        hhat, rstd = _rms_stats(xbuf[...])
        dg_ref[...] += jnp.sum(du * hhat, axis=0, keepdims=True)
        res = _rms_bwd(g * du, hhat, rstd) + dh_ref[...].astype(F32)

        @pl.when(i > 0)
        def _():
            _for_tile(k, nt, lambda kk: _token_copy(gx_hbm, b, (kk - 1) % nt, ts, gxbuf, tok_sems.at[1], True).wait())

        gxbuf[...] = res

        @pl.when(k == 0)
        def _():
            dmeta_ref[...] += gxbuf[PAD_FRONT:LANES, :]

        _for_tile(k, nt, lambda kk: _token_copy(gx_hbm, b, kk, ts, gxbuf, tok_sems.at[1], True).start())

        @pl.when(i == steps - 1)
        def _():
            _token_copy(gx_hbm, b, nt - 1, ts, gxbuf, tok_sems.at[1], True).wait()
            plan()[2]()

    whole = lambda a: pl.BlockSpec(a.shape, lambda i: (0,) * a.ndim)
    hbm = pl.BlockSpec(memory_space=pl.ANY)
    return pl.pallas_call(
        body,
        name="input_bwd",
        grid=(steps,),
        in_specs=[pl.BlockSpec((ts, kb), lambda i: (i, 0)) for _ in dps]
        + [whole(w_in), hbm, whole(meta), pl.BlockSpec((ts, d), lambda i: (i, 0)), whole(norm_g), hbm],
        out_specs=(hbm, pl.BlockSpec((N_META, d), lambda i: (0, 0)), pl.BlockSpec((1, d), lambda i: (0, 0)), hbm),
        out_shape=(jax.ShapeDtypeStruct((nb_seq, s, d), F32),
                   jax.ShapeDtypeStruct((N_META, d), F32),
                   jax.ShapeDtypeStruct((1, d), F32),
                   jax.ShapeDtypeStruct((N_CHIPS,) + in_slot, BF16)),
        scratch_shapes=[pltpu.VMEM((ts, d), F32), pltpu.VMEM((ts, d), F32), pltpu.SemaphoreType.DMA((2,))]
        + _reduce_scratch([(in_slot, BF16)], [True]),
        compiler_params=_params("arbitrary"),
    )(*dps, w_in, x, meta, dh, norm_g, send_in)


def _in_proj_bwd_w(u, dps, bm, small_grads, send_out):
    r, d = u.shape
    kb = dps[0].shape[1]
    steps = r // bm
    n_dp, n_small = len(dps), len(small_grads)
    out_slot, small_slot = send_out.shape[1:], (SMALL_ROWS, LANES)

    def body(*refs):
        u_ref, dp_refs = refs[0], refs[1:1 + n_dp]
        small_refs = refs[1 + n_dp:1 + n_dp + n_small]
        o = 1 + n_dp + n_small
        pay_out, o_ref, r2_out, r2_small = refs[o:o + 4]
        acc_ref, ssmall, r1_out, sum_out, r1_small, sum_small = refs[o + 4:o + 10]
        sems = refs[o + 10:]
        i = pl.program_id(0)

        def plan():
            return _reduce_plan((pay_out, ssmall), (None, None), (r1_out, r1_small), (sum_out, sum_small),
                                (r2_out, r2_small), *sems)

        @pl.when(i == 0)
        def _():
            acc_ref[...] = jnp.zeros_like(acc_ref)
            _pack_small(ssmall, *small_refs)
            plan()[0]()

        @pl.when(i == 1)
        def _():
            plan()[1]()

        uu = u_ref[...]
        for j in range(n_dp):
            acc_ref[kb * j:kb * (j + 1), :] += _dot(dp_refs[j][...], uu, _TN)

        @pl.when(i == steps - 1)
        def _():
            for k in range(N_DEV):
                for s, e, c0 in _in_pieces(k):
                    o_ref[k, s:e, :] = acc_ref[c0:c0 + e - s, :].astype(BF16)
                o_ref[k, SHARD_IN:, :] = jnp.zeros((SHARD_IN_PAD - SHARD_IN, d), BF16)
            plan()[2]()

    whole = lambda a: pl.BlockSpec(a.shape, lambda i: (0,) * a.ndim)
    hbm = pl.BlockSpec(memory_space=pl.ANY)
    return pl.pallas_call(
        body,
        name="in_proj_bwd_w",
        grid=(steps,),
        in_specs=[pl.BlockSpec((bm, d), lambda i: (i, 0))]
        + [pl.BlockSpec((bm, kb), lambda i: (i, 0)) for _ in dps] + [whole(a) for a in small_grads]
        + [whole(send_out)],
        out_specs=(pl.BlockSpec((N_DEV, SHARD_IN_PAD, d), lambda i: (0, 0, 0)), hbm, hbm),
        out_shape=(jax.ShapeDtypeStruct((N_DEV, SHARD_IN_PAD, d), BF16),
                   jax.ShapeDtypeStruct((N_CHIPS,) + out_slot, BF16),
                   jax.ShapeDtypeStruct((N_CHIPS,) + small_slot, F32)),
        scratch_shapes=[pltpu.VMEM((kb * n_dp, d), F32), pltpu.VMEM((N_DEV,) + small_slot, F32)]
        + _reduce_scratch([(out_slot, BF16), (small_slot, F32)], [False, False]),
        compiler_params=_params("arbitrary"),
    )(u, *dps, *small_grads, send_out)


def _local_step(x, loss_target, u, p, meta_f, norm_g, w_in_p, q_norm_g, w_q_p, kv_norm_g, w_kv_p, conv_w_f,
                attn_out_g, conv_out_g, w_out_f, g_final):
    nb_seq, s, d = x.shape
    tp = s + LANES
    ht = tp // 2
    tables = _rope_tables(tp)

    q, k, v = _qkv_fwd(p, w_q_p, w_kv_p, q_norm_g, kv_norm_g, tables, nb_seq, tp)
    ya, o, lse = _attn_fwd(q, k, v, p, attn_out_g)
    yc = _conv_fwd(p, conv_w_f, conv_out_g, nb_seq, tp)
    dhb, d_final_g, loss_part = _out_proj_loss(ya, yc, w_out_f, x, loss_target, g_final, TOKEN_TILES)

    dcat, d_w_out = _out_proj_bwd(dhb, w_out_f, ya, yc, ht)
    send_out = d_w_out.reshape(N_DEV, SHARD_OUT, d)
    dq, dk, dv, dz_attn, d_attn_g = _attn_bwd(q, k, v, o, lse, dcat, p, attn_out_g)
    dpa, d_wq_p, d_wkv_p, d_gq, d_gkv = _qkv_bwd(p, dq, dk, dv, w_q_p, w_kv_p, q_norm_g, kv_norm_g, tables)
    d_b, d_c, d_h, dz_conv, d_conv_w, d_conv_g = _conv_bwd(p, dcat, conv_w_f, conv_out_g, nb_seq, tp)
    dps = (dpa, dz_attn, d_b, d_c, d_h, dz_conv)
    small = (d_wq_p, d_wkv_p, d_conv_w, d_final_g, d_gq, d_gkv, d_attn_g, d_conv_g, loss_part)
    send_in, r_out, r_small = _in_proj_bwd_w(u, dps, ht, small, send_out)
    grad_x, d_meta, d_norm_g, r_in = _input_bwd(dps, w_in_p, x, meta_f, dhb, norm_g, TOKEN_TILES, send_in)
    return grad_x, r_in, r_out, r_small, d_meta, d_norm_g


def kernel(x, meta_tokens, norm_g, w_in, q_norm_g, w_q_up, kv_norm_g, w_kv_up, conv_w, attn_out_g, conv_out_g, w_out, final_norm_g, loss_target, m_meta_tokens, m_norm_g, m_w_in, m_q_norm_g, m_w_q_up, m_kv_norm_g, m_w_kv_up, m_conv_w, m_attn_out_g, m_conv_out_g, m_w_out, m_final_norm_g, v_meta_tokens, v_norm_g, v_w_in, v_q_norm_g, v_w_q_up, v_kv_norm_g, v_w_kv_up, v_conv_w, v_attn_out_g, v_conv_out_g, v_w_out, v_final_norm_g):
    d = x.shape[-1]
    ht = (x.shape[1] + LANES) // 2
    u, w_in_p, meta_f = _prep_gather(x, meta_tokens, norm_g, w_in[0].T)
    p, w_q_p, w_kv_p, w_out_f, conv_w_f = _in_proj_gather(
        u, w_in_p, w_q_up[0].T, w_kv_up[0], w_out[0], conv_w.transpose(1, 0, 2), ht, 3 * GRP_A)
    g_final = final_norm_g.reshape(1, d)
    grad_x, r_in, r_out, r_small, d_meta, d_norm_g = _local_step(
        x, loss_target, u, p, meta_f, norm_g, w_in_p, q_norm_g, w_q_p, kv_norm_g, w_kv_p, conv_w_f,
        attn_out_g, conv_out_g, w_out_f, g_final)

    flat = lambda a: a.reshape(a.shape[-2:]) if a.ndim == 3 else a.reshape(1, -1) if a.ndim == 1 else a
    transposed = ("w_in", "w_q_up")

    def to_kernel(n, a):
        if n == "conv_w":
            return a.transpose(1, 0, 2)
        return flat(a).T if n in transposed else flat(a)

    def from_kernel(n, a, shape):
        if n == "conv_w":
            return a.transpose(1, 0, 2)
        return (a.T if n in transposed else a).reshape(shape)
    params = {
        "meta_tokens": (meta_tokens, m_meta_tokens, v_meta_tokens),
        "norm_g": (norm_g, m_norm_g, v_norm_g),
        "w_in": (w_in, m_w_in, v_w_in),
        "q_norm_g": (q_norm_g, m_q_norm_g, v_q_norm_g),
        "w_q_up": (w_q_up, m_w_q_up, v_w_q_up),
        "kv_norm_g": (kv_norm_g, m_kv_norm_g, v_kv_norm_g),
        "w_kv_up": (w_kv_up, m_w_kv_up, v_w_kv_up),
        "conv_w": (conv_w, m_conv_w, v_conv_w),
        "attn_out_g": (attn_out_g, m_attn_out_g, v_attn_out_g),
        "conv_out_g": (conv_out_g, m_conv_out_g, v_conv_out_g),
        "w_out": (w_out, m_w_out, v_w_out),
        "final_norm_g": (final_norm_g, m_final_norm_g, v_final_norm_g),
    }
    grads, loss = _reduce_tail(r_in, r_out, r_small, d_meta, d_norm_g)
    updated, grad_x = _adamw(grads, {n: tuple(to_kernel(n, a) for a in t) for n, t in params.items()}, grad_x)
    outs = [[from_kernel(n, updated[n][i], params[n][0].shape) for n, _ in PARAM_SHAPES] for i in range(4)]
    return (loss[0, 0], grad_x, *outs[0], *outs[1], *outs[2], *outs[3])
```

```python
import functools

import jax
import jax.numpy as jnp
from jax import lax
from jax.experimental import pallas as pl
from jax.experimental.pallas import tpu as pltpu

F32 = jnp.float32
BF16 = jnp.bfloat16

N_META = 16
D_MODEL = 1024
N_HEADS = 4
D_NOPE = 128
D_ROPE = 64
D_V = 128
Q_RANK = 256
KV_RANK = 128
CONV_WIDTH = 512
CONV_GROUP = 64
ROPE_THETA = 10000.0
ATTN_SCALE = (D_NOPE + D_ROPE) ** -0.5
Q_SCALE = ATTN_SCALE * 1.4426950408889634
EPS = 1e-6
NEG_INF = -1e30

ADAM_LR = 0.001
ADAM_B1 = 0.9
ADAM_B2 = 0.999
ADAM_EPS = 1e-08
ADAM_WD = 0.01
ADAM_STEP = 10

LANES = 128
PAD_FRONT = LANES - N_META
K_TILE = 256
Q_TILE = 512
N_DEV = 8
VMEM_LIMIT = 56 * 1024 * 1024

IN_PAD = 3072
GRP_A = 512
N_A = Q_RANK + KV_RANK + D_ROPE
IN_PROJ = 3008
SHARD_IN = IN_PROJ // N_DEV
SHARD_IN_PAD = 384
SHARD_Q = 96
SHARD_KV = 128
SHARD_OUT = 128
SHARD_CONV = 64
SHARD_META = 128
Q_COLS = N_HEADS * (D_NOPE + D_ROPE)
KV_COLS = N_HEADS * (D_NOPE + D_V)

ROW_Q, ROW_KV, ROW_META, ROW_CONV = 0, 256, 384, 400
ROW_REPL = 408
ROW_NORM, ROW_FINAL, ROW_GQ, ROW_GKV, ROW_ATTN, ROW_CONVG, ROW_LOSS = 408, 416, 424, 426, 427, 431, 435
SMALL_ROWS = 440

PARAM_SHAPES = (
    ("meta_tokens", (N_META, SHARD_META)), ("norm_g", (1, D_MODEL)), ("w_in", (SHARD_IN, D_MODEL)),
    ("q_norm_g", (1, Q_RANK)), ("w_q_up", (SHARD_Q, Q_RANK)), ("kv_norm_g", (1, KV_RANK)),
    ("w_kv_up", (KV_RANK, SHARD_KV)), ("conv_w", (3, 1, SHARD_CONV)), ("attn_out_g", (1, CONV_WIDTH)),
    ("conv_out_g", (1, CONV_WIDTH)), ("w_out", (SHARD_OUT, D_MODEL)), ("final_norm_g", (1, D_MODEL)),
)


def _in_pieces(k):
    lo, hi = SHARD_IN * k, SHARD_IN * (k + 1)
    out = []
    if lo < N_A:
        out.append((0, min(hi, N_A) - lo, lo))
    if hi > N_A:
        s = max(lo, N_A)
        out.append((s - lo, hi - lo, s + GRP_A - N_A))
    return out


def _q_pieces(k):
    lo, hi = SHARD_Q * k, SHARD_Q * (k + 1)
    out = []
    for h in range(N_HEADS):
        base = (D_NOPE + D_ROPE) * h
        s, e = max(lo, base), min(hi, base + D_NOPE)
        if s < e:
            out.append((s - lo, e - lo, D_NOPE * h + s - base))
        s, e = max(lo, base + D_NOPE), min(hi, base + D_NOPE + D_ROPE)
        if s < e:
            out.append((s - lo, e - lo, N_HEADS * D_NOPE + D_ROPE * h + s - base - D_NOPE))
    return out


def _kv_dst(k):
    return D_NOPE * (k // 2) + (N_HEADS * D_NOPE if k % 2 else 0)


def _params(*sem):
    return pltpu.CompilerParams(dimension_semantics=sem, vmem_limit_bytes=VMEM_LIMIT)


def _rms_stats(x):
    r = lax.rsqrt(jnp.mean(x * x, axis=-1, keepdims=True) + EPS)
    return x * r, r


def _rms_bwd(gdy, xhat, r):
    return r * (gdy - xhat * jnp.mean(gdy * xhat, axis=-1, keepdims=True))


def _sigmoid(z):
    return 1.0 / (1.0 + jnp.exp(-z))


def _group_mean(x):
    i0 = lax.broadcasted_iota(jnp.int32, (LANES, LANES), 0) // CONV_GROUP
    i1 = lax.broadcasted_iota(jnp.int32, (LANES, LANES), 1) // CONV_GROUP
    m = jnp.where(i0 == i1, 1.0 / CONV_GROUP, 0.0).astype(BF16)
    hi = x.astype(BF16)
    lo = (x - hi.astype(F32)).astype(BF16)
    return jnp.dot(hi, m, preferred_element_type=F32) + jnp.dot(lo, m, preferred_element_type=F32)


_NT = (((1,), (1,)), ((), ()))
_TN = (((0,), (0,)), ((), ()))


def _dot(a, b, dims=None):
    if dims is None:
        return jnp.dot(a, b, preferred_element_type=F32)
    return lax.dot_general(a, b, dims, preferred_element_type=F32)


def _device_position():
    x, y, c = lax.axis_index("x"), lax.axis_index("y"), lax.axis_index("c")
    return x, y, c, 4 * x + 2 * y + c


def _near_far(x, y, c):
    flip = lambda v, on: v + on - 2 * v * on
    return (flip(x, 1 - c), flip(y, c)), (flip(x, c), flip(y, 1 - c))


def _gather_plan(srcs, slots, send_sems, recv_sems, local_sems):
    x, y, c, _ = _device_position()
    me, sibling = (x, y, c), (x, y, 1 - c)
    near, far = _near_far(x, y, c)
    diag = (1 - x, 1 - y)
    n = len(srcs)

    def slot(a, px, py, pc):
        return slots[a].at[4 * px + 2 * py + pc]

    def copy(a, k, block, to, own=False):
        return pltpu.make_async_remote_copy(
            src_ref=srcs[a] if own else slot(a, *block),
            dst_ref=slot(a, *block),
            send_sem=send_sems.at[7 * a + k],
            recv_sem=recv_sems.at[7 * a + k],
            device_id=to,
            device_id_type=pl.DeviceIdType.MESH,
        )

    def local(a):
        return pltpu.make_async_copy(srcs[a], slot(a, *me), local_sems.at[a])

    sent = [(me, sibling), (me, (*near, c)), (me, (*far, c)), ((*near, c), (*far, c)),
            ((*near, c), sibling), ((*far, c), sibling), ((*diag, c), sibling)]
    landed = [sibling, (*near, c), (*far, c), (*diag, c), (*far, 1 - c), (*near, 1 - c), (*diag, 1 - c)]

    def send(a, k):
        return copy(a, k, *sent[k], own=k < 3)

    def arrival(a, k):
        return copy(a, k, landed[k], me)

    def start():
        for a in range(n):
            local(a).start()
            for k in range(3):
                send(a, k).start()

    def relay():
        for a in range(n):
            arrival(a, 1).wait_recv()
            send(a, 3).start()
            send(a, 4).start()

    def forward():
        for k in (2, 3):
            for a in range(n):
                arrival(a, k).wait_recv()
                send(a, k + 3).start()

    def finish():
        for a in range(n):
            for k in (0, 4, 5, 6):
                arrival(a, k).wait_recv()
        for a in range(n):
            for k in range(7):
                send(a, k).wait_send()
            local(a).wait()

    return start, relay, forward, finish


def _adam_update(g, w, m, v):
    m_new = ADAM_B1 * m + (1.0 - ADAM_B1) * g
    v_new = ADAM_B2 * v + (1.0 - ADAM_B2) * (g * g)
    m_hat = m_new / (1.0 - ADAM_B1 ** ADAM_STEP)
    v_hat = v_new / (1.0 - ADAM_B2 ** ADAM_STEP)
    return -ADAM_LR * (m_hat / (jnp.sqrt(v_hat) + ADAM_EPS) + ADAM_WD * w), m_new, v_new


def _adamw(grads, params):
    names = [n for n, _ in PARAM_SHAPES]
    n_p = len(names)

    def body(*refs):
        for i in range(n_p):
            g = refs[i][...]
            w, m, v = (refs[n_p + 3 * i + j][...] for j in range(3))
            delta, m_new, v_new = _adam_update(g, w, m, v)
            for j, val in enumerate((g, delta, m_new, v_new)):
                refs[4 * n_p + 4 * i + j][...] = val

    vm = pl.BlockSpec(memory_space=pltpu.VMEM)
    out_shape = []
    for _, shape in PARAM_SHAPES:
        out_shape += [jax.ShapeDtypeStruct(shape, F32)] * 4
    outs = pl.pallas_call(
        body,
        name="adamw",
        out_shape=tuple(out_shape),
        in_specs=[vm] * (4 * n_p),
        out_specs=(vm,) * (4 * n_p),
        compiler_params=pltpu.CompilerParams(vmem_limit_bytes=VMEM_LIMIT),
    )(*[grads[n] for n in names], *[a for n in names for a in params[n]])
    return {n: outs[4 * i:4 * i + 4] for i, n in enumerate(names)}


N_CHIPS = 4


def _reduce_plan(pays, owns, r1s, sums, r2s, send1, recv1, send2, recv2, local_sems, relays=None):
    x, y, c, _ = _device_position()
    sibling = (x, y, 1 - c)
    chips = [((1 - x if rj & 2 else x), (1 - y if rj & 1 else y)) for rj in range(N_CHIPS)]
    n = len(pays)

    def slot_of(rj, core):
        return 4 * chips[rj][0] + 2 * chips[rj][1] + core

    def to_sibling(a, rj):
        return pltpu.make_async_remote_copy(
            src_ref=pays[a].at[slot_of(rj, 1 - c)], dst_ref=r1s[a].at[rj],
            send_sem=send1.at[N_CHIPS * a + rj], recv_sem=recv1.at[N_CHIPS * a + rj],
            device_id=sibling, device_id_type=pl.DeviceIdType.MESH)

    def load_own(a, rj):
        return pltpu.make_async_copy(pays[a].at[slot_of(rj, c)], owns[a].at[rj], local_sems.at[2 * N_CHIPS * a + rj])

    def to_chip(a, rj):
        return pltpu.make_async_remote_copy(
            src_ref=sums[a].at[rj], dst_ref=r2s[a].at[rj],
            send_sem=send2.at[N_CHIPS * a + rj], recv_sem=recv2.at[N_CHIPS * a + rj],
            device_id=(*chips[rj], c), device_id_type=pl.DeviceIdType.MESH)

    def keep(a):
        return pltpu.make_async_copy(sums[a].at[0], r2s[a].at[0], local_sems.at[2 * N_CHIPS * a + N_CHIPS])

    near_rj, far_rj = 2 - c, 1 + c
    near, far = _near_far(x, y, c)

    def ordered(a, k):
        src, dst, to = ((sums[a].at[near_rj], r2s[a].at[1], near), (sums[a].at[3], relays[a].at[0], near),
                        (relays[a].at[1], r2s[a].at[2], far))[k]
        return pltpu.make_async_remote_copy(
            src_ref=src, dst_ref=dst, send_sem=send2.at[N_CHIPS * a + 1 + k], recv_sem=recv2.at[N_CHIPS * a + 1 + k],
            device_id=(*to, c), device_id_type=pl.DeviceIdType.MESH)

    def start():
        for a in range(n):
            for rj in range(N_CHIPS):
                to_sibling(a, rj).start()
                if owns[a] is not None:
                    load_own(a, rj).start()

    def combine():
        for a in range(n):
            for rj in range(N_CHIPS):
                to_sibling(a, rj).wait_recv()
                if owns[a] is not None:
                    load_own(a, rj).wait()
                    mine = owns[a][rj]
                else:
                    mine = pays[a][slot_of(rj, c)]
                sums[a][rj] = (mine.astype(F32) + r1s[a][rj].astype(F32)).astype(sums[a].dtype)
            keep(a).start()
            if relays is None:
                for rj in range(1, N_CHIPS):
                    to_chip(a, rj).start()
            else:
                ordered(a, 1).start()
                ordered(a, 0).start()

    def relay():
        for a in range(n):
            ordered(a, 1).wait_recv()
            relays[a][1] = (sums[a][far_rj].astype(F32) + relays[a][0].astype(F32)).astype(sums[a].dtype)
            ordered(a, 2).start()

    def finish():
        for a in range(n):
            if relays is None:
                for rj in range(1, N_CHIPS):
                    to_chip(a, rj).wait_recv()
            else:
                ordered(a, 0).wait_recv()
                ordered(a, 2).wait_recv()
            for rj in range(N_CHIPS):
                to_sibling(a, rj).wait_send()
            if relays is None:
                for rj in range(1, N_CHIPS):
                    to_chip(a, rj).wait_send()
            else:
                for k in range(3):
                    ordered(a, k).wait_send()
            keep(a).wait()

    return (start, combine, finish) if relays is None else (start, combine, relay, finish)


def _reduce_scratch(shapes_dtypes, own_flags):
    out = []
    for (shape, dtype), own in zip(shapes_dtypes, own_flags):
        if own:
            out.append(pltpu.VMEM((N_CHIPS,) + shape, dtype))
        out += [pltpu.VMEM((N_CHIPS,) + shape, dtype), pltpu.VMEM((N_CHIPS,) + shape, dtype)]
    n = len(shapes_dtypes)
    out += [pltpu.SemaphoreType.DMA((N_CHIPS * n,))] * 4 + [pltpu.SemaphoreType.DMA((2 * N_CHIPS * n,))]
    return out


def _pack_small(ssmall, dwq, dwkv, dconv, dfinal, dgq, dgkv, dattn, dconvg, loss_part):
    ssmall[...] = jnp.zeros_like(ssmall)
    rep = ssmall.at[0]
    for i in range(D_MODEL // LANES):
        rep[ROW_FINAL + i:ROW_FINAL + i + 1, :] = dfinal[:, LANES * i:LANES * (i + 1)]
    for i in range(Q_RANK // LANES):
        rep[ROW_GQ + i:ROW_GQ + i + 1, :] = dgq[:, LANES * i:LANES * (i + 1)]
    rep[ROW_GKV:ROW_GKV + 1, :] = dgkv[...]
    for i in range(CONV_WIDTH // LANES):
        rep[ROW_ATTN + i:ROW_ATTN + i + 1, :] = dattn[:, LANES * i:LANES * (i + 1)]
        rep[ROW_CONVG + i:ROW_CONVG + i + 1, :] = dconvg[:, LANES * i:LANES * (i + 1)]
    rep[ROW_LOSS:ROW_LOSS + 1, :] = loss_part[...]
    for k in range(N_DEV):
        if k:
            ssmall[k, ROW_REPL:, :] = ssmall[0, ROW_REPL:, :]
        for s, e, d in _q_pieces(k):
            for i in range(Q_RANK // LANES):
                ssmall[k, ROW_Q + SHARD_Q * i + s:ROW_Q + SHARD_Q * i + e, :] = dwq[d:d + e - s, LANES * i:LANES * (i + 1)]
        ssmall[k, ROW_KV:ROW_KV + KV_RANK, :] = dwkv[:, _kv_dst(k):_kv_dst(k) + SHARD_KV]
        ssmall[k, ROW_CONV:ROW_CONV + 3, 0:SHARD_CONV] = dconv[0:3, SHARD_CONV * k:SHARD_CONV * (k + 1)]


TOKEN_TILES = 4
TAIL_ROWS = N_META + D_MODEL // LANES


def _reduce_tail(r_in, r_out, r_small, d_meta, d_norm):
    n_p = len(PARAM_SHAPES)
    names = [n for n, _ in PARAM_SHAPES]

    def body(*refs):
        rin, rout, rsmall, dmeta, dnorm = refs[:5]
        g_out = {n: refs[5 + i] for i, n in enumerate(names)}
        loss_out = refs[5 + n_p]
        stail, rtail, gsum, gtail, send_sems, recv_sems = refs[6 + n_p:]
        x, y, c, me = _device_position()
        my_chip = 2 * x + y

        for k in range(N_DEV):
            stail[k, 0:N_META, :] = dmeta[:, SHARD_META * k:SHARD_META * (k + 1)]
            for i in range(D_MODEL // LANES):
                stail[k, N_META + i:N_META + i + 1, :] = dnorm[:, LANES * i:LANES * (i + 1)]
        copies = []
        for r in range(1, N_DEV):
            peer = (1 - x if r & 4 else x, 1 - y if r & 2 else y, 1 - c if r & 1 else c)
            copies.append(pltpu.make_async_remote_copy(
                src_ref=stail.at[4 * peer[0] + 2 * peer[1] + peer[2]],
                dst_ref=rtail.at[r],
                send_sem=send_sems.at[r - 1],
                recv_sem=recv_sems.at[r - 1],
                device_id=peer,
                device_id_type=pl.DeviceIdType.MESH,
            ))
        for cp in copies:
            cp.start()
        rtail[0] = stail[me]

        g = rin[0].astype(F32) + rin[1].astype(F32) + rin[2].astype(F32)
        g_out["w_in"][...] = g[:SHARD_IN, :]

        g = rout[my_chip].astype(F32)
        gs = rsmall[my_chip]
        for ch in range(1, N_CHIPS):
            g = g + rout[ch ^ my_chip].astype(F32)
            gs = gs + rsmall[ch ^ my_chip]
        g_out["w_out"][...] = g
        gsum[...] = gs
        for i in range(Q_RANK // LANES):
            g_out["w_q_up"][:, LANES * i:LANES * (i + 1)] = gsum[ROW_Q + SHARD_Q * i:ROW_Q + SHARD_Q * (i + 1), :]
        g_out["w_kv_up"][...] = gsum[ROW_KV:ROW_KV + KV_RANK, :]
        for i in range(3):
            g_out["conv_w"][i] = gsum[ROW_CONV + i:ROW_CONV + i + 1, 0:SHARD_CONV]
        for name, row, width in (("final_norm_g", ROW_FINAL, D_MODEL), ("q_norm_g", ROW_GQ, Q_RANK),
                                 ("kv_norm_g", ROW_GKV, KV_RANK), ("attn_out_g", ROW_ATTN, CONV_WIDTH),
                                 ("conv_out_g", ROW_CONVG, CONV_WIDTH)):
            for i in range(width // LANES):
                g_out[name][:, LANES * i:LANES * (i + 1)] = gsum[row + i:row + i + 1, :]
        loss_out[...] = gsum[ROW_LOSS:ROW_LOSS + 1, :]

        for cp in copies:
            cp.wait_recv()
        gt = rtail[me]
        for d in range(1, N_DEV):
            gt = gt + rtail[d ^ me]
        gtail[...] = gt
        g_out["meta_tokens"][...] = gtail[0:N_META, :]
        for i in range(D_MODEL // LANES):
            g_out["norm_g"][:, LANES * i:LANES * (i + 1)] = gtail[N_META + i:N_META + i + 1, :]
        for cp in copies:
            cp.wait_send()

    vm = pl.BlockSpec(memory_space=pltpu.VMEM)
    out_shape = [jax.ShapeDtypeStruct(shape, F32) for _, shape in PARAM_SHAPES]
    out_shape.append(jax.ShapeDtypeStruct((1, LANES), F32))
    outs = pl.pallas_call(
        body,
        name="reduce_tail",
        out_shape=tuple(out_shape),
        in_specs=[vm] * 5,
        out_specs=(vm,) * len(out_shape),
        scratch_shapes=[
            pltpu.VMEM((N_DEV, TAIL_ROWS, LANES), F32),
            pltpu.VMEM((N_DEV, TAIL_ROWS, LANES), F32),
            pltpu.VMEM((SMALL_ROWS, LANES), F32),
            pltpu.VMEM((TAIL_ROWS, LANES), F32),
            pltpu.SemaphoreType.DMA((N_DEV - 1,)),
            pltpu.SemaphoreType.DMA((N_DEV - 1,)),
        ],
        compiler_params=pltpu.CompilerParams(vmem_limit_bytes=VMEM_LIMIT),
    )(r_in, r_out, r_small, d_meta, d_norm)
    return {n: outs[i] for i, n in enumerate(names)}, outs[-1]


def _prep_gather(x, meta, norm_g, w_in_t):
    nb_seq, s, d = x.shape
    nb = s // LANES + 1
    relay_step = nb_seq * (nb - 1) // 2
    forward_step = nb_seq * (nb - 1) - 1
    finish_step = nb_seq * (nb - 1)

    def body(x_ref, meta_ref, g_ref, win_ref, u_ref, w_in_p, meta_f,
             sbig, ssmall, gbig, gsmall, send_sems, recv_sems, local_sems):
        jj, b = pl.program_id(0), pl.program_id(1)
        t = jj * nb_seq + b

        def plan():
            return _gather_plan((sbig, ssmall), (gbig, gsmall), send_sems, recv_sems, local_sems)

        @pl.when(t == 0)
        def _():
            sbig[0:SHARD_IN, :] = win_ref[...].astype(BF16)
            sbig[SHARD_IN:, :] = jnp.zeros((SHARD_IN_PAD - SHARD_IN, d), BF16)
            ssmall[...] = meta_ref[...]
            plan()[0]()

        @pl.when(t == relay_step)
        def _():
            plan()[1]()

        @pl.when(t == forward_step)
        def _():
            plan()[2]()

        @pl.when(t == finish_step)
        def _():
            plan()[3]()
            w_in_p[N_A:GRP_A, :] = jnp.zeros((GRP_A - N_A, d), BF16)
            for k in range(N_DEV):
                for s0, e0, d0 in _in_pieces(k):
                    w_in_p[d0:d0 + e0 - s0, :] = gbig[k, s0:e0, :]
                meta_f[:, SHARD_META * k:SHARD_META * (k + 1)] = gsmall[k]

        def norm(h):
            hhat, _ = _rms_stats(h)
            return (hhat * g_ref[...]).astype(BF16)

        @pl.when(jj < nb - 1)
        def _():
            u_ref[...] = norm(x_ref[0])

        @pl.when(jj == nb - 1)
        def _():
            u_ref[0:PAD_FRONT, :] = jnp.zeros((PAD_FRONT, d), BF16)
            u_ref[PAD_FRONT:LANES, :] = norm(meta_f[...])

    whole = lambda shape: pl.BlockSpec(shape, lambda jj, b: (0,) * len(shape))
    return pl.pallas_call(
        body,
        name="prep_norm_gather",
        grid=(nb, nb_seq),
        in_specs=[
            pl.BlockSpec((1, LANES, d), lambda jj, b: (b, jnp.minimum(jj, nb - 2), 0)),
            whole(meta.shape), whole(norm_g.shape), whole(w_in_t.shape),
        ],
        out_specs=(pl.BlockSpec((LANES, d), lambda jj, b: (b * nb + (jj + 1) % nb, 0)),
                   whole((IN_PAD, d)), whole((N_META, d))),
        out_shape=(jax.ShapeDtypeStruct((nb_seq * nb * LANES, d), BF16),
                   jax.ShapeDtypeStruct((IN_PAD, d), BF16),
                   jax.ShapeDtypeStruct((N_META, d), F32)),
        scratch_shapes=[
            pltpu.VMEM((SHARD_IN_PAD, d), BF16),
            pltpu.VMEM((N_META, SHARD_META), F32),
            pltpu.VMEM((N_DEV, SHARD_IN_PAD, d), BF16),
            pltpu.VMEM((N_DEV, N_META, SHARD_META), F32),
            pltpu.SemaphoreType.DMA((14,)),
            pltpu.SemaphoreType.DMA((14,)),
            pltpu.SemaphoreType.DMA((2,)),
        ],
        compiler_params=_params("arbitrary", "arbitrary"),
    )(x, meta, norm_g, w_in_t)


def _in_proj_gather(u, w_in_p, w_q, w_kv, w_out, conv_w, bm, bn):
    m, k_dim = u.shape
    n = w_in_p.shape[0]
    steps = (m // bm) * (n // bn)
    relay_step, forward_step = steps // 3, 2 * steps // 3
    qkv_shape = (SHARD_Q + KV_RANK, Q_RANK)

    def body(a_ref, b_ref, wq_ref, wkv_ref, wout_ref, conv_ref, o_ref, w_q_p, w_kv_p, w_out_f, conv_f,
             sqkv, sout, sconv, gqkv, gout, gconv, send_sems, recv_sems, local_sems):
        t = pl.program_id(0) * (n // bn) + pl.program_id(1)

        def plan():
            return _gather_plan((sqkv, sout, sconv), (gqkv, gout, gconv), send_sems, recv_sems, local_sems)

        @pl.when(t == 0)
        def _():
            sqkv[...] = jnp.zeros_like(sqkv)
            sqkv[0:SHARD_Q, :] = wq_ref[...].astype(BF16)
            sqkv[SHARD_Q:, 0:SHARD_KV] = wkv_ref[...].astype(BF16)
            sout[...] = wout_ref[...].astype(BF16)
            sconv[...] = jnp.zeros_like(sconv)
            for i in range(3):
                sconv[i:i + 1, 0:SHARD_CONV] = conv_ref[i]
            plan()[0]()

        @pl.when(t == relay_step)
        def _():
            plan()[1]()

        @pl.when(t == forward_step)
        def _():
            plan()[2]()

        o_ref[...] = _dot(a_ref[...], b_ref[...], _NT).astype(o_ref.dtype)

        @pl.when(t == steps - 1)
        def _():
            plan()[3]()
            conv_f[...] = jnp.zeros_like(conv_f)
            for k in range(N_DEV):
                for s0, e0, d0 in _q_pieces(k):
                    w_q_p[d0:d0 + e0 - s0, :] = gqkv[k, s0:e0, :]
                w_kv_p[:, _kv_dst(k):_kv_dst(k) + SHARD_KV] = gqkv[k, SHARD_Q:, 0:SHARD_KV]
                w_out_f[SHARD_OUT * k:SHARD_OUT * (k + 1), :] = gout[k]
                conv_f[0:3, SHARD_CONV * k:SHARD_CONV * (k + 1)] = gconv[k, 0:3, 0:SHARD_CONV]

    whole = lambda shape: pl.BlockSpec(shape, lambda i, j: (0,) * len(shape))
    return pl.pallas_call(
        body,
        name="in_proj_gather",
        grid=(m // bm, n // bn),
        in_specs=[pl.BlockSpec((bm, k_dim), lambda i, j: (i, 0)), pl.BlockSpec((bn, k_dim), lambda i, j: (j, 0)),
                  whole(w_q.shape), whole(w_kv.shape), whole(w_out.shape), whole(conv_w.shape)],
        out_specs=(pl.BlockSpec((bm, bn), lambda i, j: (i, j)),
                   whole((Q_COLS, Q_RANK)), whole((KV_RANK, KV_COLS)), whole((D_MODEL, D_MODEL)),
                   whole((8, CONV_WIDTH))),
        out_shape=(jax.ShapeDtypeStruct((m, n), BF16),
                   jax.ShapeDtypeStruct((Q_COLS, Q_RANK), BF16),
                   jax.ShapeDtypeStruct((KV_RANK, KV_COLS), BF16),
                   jax.ShapeDtypeStruct((D_MODEL, D_MODEL), BF16),
                   jax.ShapeDtypeStruct((8, CONV_WIDTH), F32)),
        scratch_shapes=[
            pltpu.VMEM(qkv_shape, BF16),
            pltpu.VMEM((SHARD_OUT, D_MODEL), BF16),
            pltpu.VMEM((8, LANES), F32),
            pltpu.VMEM((N_DEV,) + qkv_shape, BF16),
            pltpu.VMEM((N_DEV, SHARD_OUT, D_MODEL), BF16),
            pltpu.VMEM((N_DEV, 8, LANES), F32),
            pltpu.SemaphoreType.DMA((21,)),
            pltpu.SemaphoreType.DMA((21,)),
            pltpu.SemaphoreType.DMA((3,)),
        ],
        compiler_params=_params("arbitrary", "arbitrary"),
    )(u, w_in_p, w_q, w_kv, w_out, conv_w)


def _rope_tables(tp):
    half = D_ROPE // 2
    inv_freq = 1.0 / (ROPE_THETA ** (jnp.arange(half, dtype=F32) / half))
    pos = (jnp.arange(tp) - PAD_FRONT).astype(F32)
    ang = pos[:, None] * inv_freq[None, :]
    cos = jnp.tile(jnp.cos(ang), (1, LANES // half))
    sin = jnp.tile(jnp.sin(ang), (1, LANES // half))
    first = (jnp.arange(LANES) % D_ROPE) < half
    return cos, jnp.where(first, -sin, 0.0), jnp.where(first, 0.0, sin)


def _rope(t, cos, sa, sb):
    return t * cos + pltpu.roll(t, LANES - D_ROPE // 2, 1) * sa + pltpu.roll(t, D_ROPE // 2, 1) * sb


def _rope_t(t, cos, sa, sb):
    return t * cos + pltpu.roll(t * sa, D_ROPE // 2, 1) + pltpu.roll(t * sb, LANES - D_ROPE // 2, 1)


def _qkv_fwd(p, wq, wkv, gq, gkv, tables, nb_seq, tp):
    ht = tp // 2

    def body(pa_ref, wq_ref, wkv_ref, gq_ref, gkv_ref, cos_ref, sa_ref, sb_ref, q_ref, k_ref, v_ref):
        pa = pa_ref[...].astype(F32)
        cq_hat, _ = _rms_stats(pa[:, :Q_RANK])
        ckv_hat, _ = _rms_stats(pa[:, Q_RANK:Q_RANK + KV_RANK])
        q = _dot((cq_hat * gq_ref[...]).astype(BF16), wq_ref[...], _NT) * Q_SCALE
        kv = _dot((ckv_hat * gkv_ref[...]).astype(BF16), wkv_ref[...])
        tabs = (cos_ref[...], sa_ref[...], sb_ref[...])
        lane = lax.broadcasted_iota(jnp.int32, (ht, LANES), 1)
        low = lane < D_ROPE
        mark = lane == D_ROPE
        row = (pl.program_id(0) % 2) * ht + lax.broadcasted_iota(jnp.int32, (ht, LANES), 0)
        k_pe = jnp.where(mark & (row < PAD_FRONT), NEG_INF, _rope(pa[:, Q_RANK + KV_RANK:], *tabs))
        one = jnp.where(mark & (row >= PAD_FRONT), 1.0, 0.0)
        pairs = [_rope(q[:, N_HEADS * D_NOPE + LANES * i:N_HEADS * D_NOPE + LANES * (i + 1)], *tabs) for i in range(2)]
        for h in range(N_HEADS):
            pair = pairs[h // 2]
            if h % 2:
                pair = pltpu.roll(pair, D_ROPE, 1)
            pe = jnp.where(low, pair, one)
            q_ref[0, h] = jnp.concatenate([q[:, D_NOPE * h:D_NOPE * (h + 1)], pe], axis=1).astype(BF16)
            k_ref[0, h] = jnp.concatenate([kv[:, D_NOPE * h:D_NOPE * (h + 1)], k_pe], axis=1).astype(BF16)
            v_ref[0, h] = kv[:, N_HEADS * D_NOPE + D_V * h:N_HEADS * D_NOPE + D_V * (h + 1)].astype(BF16)

    full = lambda a: pl.BlockSpec(a.shape, lambda i: (0,) * a.ndim)
    tab = pl.BlockSpec((ht, LANES), lambda i: (i % 2, 0))
    qk = pl.BlockSpec((1, N_HEADS, ht, 2 * LANES), lambda i: (i // 2, 0, i % 2, 0))
    return pl.pallas_call(
        body,
        name="qkv_fwd",
        grid=(2 * nb_seq,),
        in_specs=[pl.BlockSpec((ht, GRP_A), lambda i: (i, 0)), full(wq), full(wkv), full(gq), full(gkv), tab, tab, tab],
        out_specs=(qk, qk, pl.BlockSpec((1, N_HEADS, ht, D_V), lambda i: (i // 2, 0, i % 2, 0))),
        out_shape=(
            jax.ShapeDtypeStruct((nb_seq, N_HEADS, tp, 2 * LANES), BF16),
            jax.ShapeDtypeStruct((nb_seq, N_HEADS, tp, 2 * LANES), BF16),
            jax.ShapeDtypeStruct((nb_seq, N_HEADS, tp, D_V), BF16),
        ),
        compiler_params=_params("parallel"),
    )(p, wq, wkv, gq, gkv, *tables)


def _attn_fwd(q, k, v, p, g_attn):
    nb_seq, _, tp, _ = q.shape

    def body(q_ref, k_ref, v_ref, z_ref, g_ref, y_ref, o_ref, lse_ref):
        g = g_ref[...]
        for r0 in range(0, tp, Q_TILE):
            nq = min(Q_TILE, tp - r0)
            kend = r0 + nq
            qq = q_ref[0, 0, r0:kend, :]
            sd = _dot(qq, k_ref[0, 0, r0:kend, :], _NT)
            causal = (lax.broadcasted_iota(jnp.int32, (nq, nq), 1) <= lax.broadcasted_iota(jnp.int32, (nq, nq), 0))
            sd = jnp.where(causal, sd, NEG_INF)
            m = jnp.max(sd, axis=-1, keepdims=True)
            if r0:
                so = _dot(qq, k_ref[0, 0, 0:r0, :], _NT)
                m = jnp.maximum(m, jnp.max(so, axis=-1, keepdims=True))
            ed = jnp.exp2(sd - m)
            l = jnp.sum(ed, axis=-1, keepdims=True)
            o = _dot(ed.astype(BF16), v_ref[0, 0, r0:kend, :])
            if r0:
                eo = jnp.exp2(so - m)
                l = l + jnp.sum(eo, axis=-1, keepdims=True)
                o = o + _dot(eo.astype(BF16), v_ref[0, 0, 0:r0, :])
            o = o * (1.0 / l)
            o_ref[0, 0, r0:kend, :] = o
            lse_ref[0, 0, r0:kend, :] = jnp.broadcast_to(m + jnp.log2(l), (nq, LANES))
            ohat, _ = _rms_stats(o)
            z = z_ref[r0:kend, :].astype(F32)
            y_ref[r0:kend, :] = (ohat * g * (z * _sigmoid(z))).astype(BF16)

    qk = pl.BlockSpec((1, 1, tp, 2 * LANES), lambda b, h: (b, h, 0, 0))
    hv = pl.BlockSpec((1, 1, tp, D_V), lambda b, h: (b, h, 0, 0))
    return pl.pallas_call(
        body,
        name="attn_fwd",
        grid=(nb_seq, N_HEADS),
        in_specs=[qk, qk, hv,
                  pl.BlockSpec((tp, LANES), lambda b, h: (b, GRP_A // LANES + h)),
                  pl.BlockSpec((1, LANES), lambda b, h: (0, h))],
        out_specs=(pl.BlockSpec((tp, LANES), lambda b, h: (b, h)), hv, hv),
        out_shape=(
            jax.ShapeDtypeStruct((nb_seq * tp, N_HEADS * D_V), BF16),
            jax.ShapeDtypeStruct((nb_seq, N_HEADS, tp, D_V), F32),
            jax.ShapeDtypeStruct((nb_seq, N_HEADS, tp, LANES), F32),
        ),
        compiler_params=_params("parallel", "parallel"),
    )(q, k, v, p, g_attn)


_CONV_COL0 = (GRP_A + N_HEADS * D_V) // LANES


def _conv_specs(tp, order):
    cols = CONV_WIDTH // LANES
    return [pl.BlockSpec((tp, LANES), functools.partial(
        lambda a, b, off: order(a, b, off), off=_CONV_COL0 + i * cols)) for i in range(4)]


def _conv_fwd(p, conv_w, g_conv, nb_seq, tp):
    def body(b_ref, c_ref, h_ref, z_ref, w_ref, g_ref, y_ref):
        cc = c_ref[...].astype(F32) * h_ref[...].astype(F32)
        row = lax.broadcasted_iota(jnp.int32, (tp, LANES), 0)
        s1 = jnp.where(row >= 1, pltpu.roll(cc, 1, 0), 0.0)
        s2 = jnp.where(row >= 2, pltpu.roll(cc, 2, 0), 0.0)
        yc = b_ref[...].astype(F32) * (w_ref[0:1, :] * s2 + w_ref[1:2, :] * s1 + w_ref[2:3, :] * cc)
        r = lax.rsqrt(_group_mean(yc * yc) + EPS)
        z = z_ref[...].astype(F32)
        y_ref[...] = (yc * r * g_ref[...] * (z * _sigmoid(z))).astype(BF16)

    return pl.pallas_call(
        body,
        name="conv_fwd",
        grid=(nb_seq, CONV_WIDTH // LANES),
        in_specs=_conv_specs(tp, lambda b, t, off: (b, off + t)) + [
            pl.BlockSpec((8, LANES), lambda b, t: (0, t)),
            pl.BlockSpec((1, LANES), lambda b, t: (0, t))],
        out_specs=pl.BlockSpec((tp, LANES), lambda b, t: (b, t)),
        out_shape=jax.ShapeDtypeStruct((nb_seq * tp, CONV_WIDTH), BF16),
        compiler_params=_params("parallel", "parallel"),
    )(p, p, p, p, conv_w, g_conv)


def _token_copy(hbm, b, k, ts, buf, sem, to_hbm=False):
    lo, hi = max(k * ts - LANES, 0), (k + 1) * ts - LANES
    off = lo - (k * ts - LANES)
    src, dst = hbm.at[b, pl.ds(lo, hi - lo)], buf.at[pl.ds(off, hi - lo)]
    if to_hbm:
        src, dst = dst, src
    return pltpu.make_async_copy(src, dst, sem)


def _for_tile(k, nt, fn):
    for kk in range(nt):
        @pl.when(k == kk)
        def _(kk=kk):
            fn(kk)


def _out_proj_loss(ya, yc, w_out, x, target, g_final, nt):
    nb_seq, s, d = x.shape
    r, ka = ya.shape
    ts = (s + LANES) // nt
    steps = nb_seq * nt

    def body(a_ref, c_ref, w_ref, x_hbm, t_hbm, g_ref, dhb_ref, dg_ref, loss_ref,
             xbuf, tbuf, acc_ref, sems):
        i = pl.program_id(0)
        b, k = i // nt, i % nt

        @pl.when(i == 0)
        def _():
            acc_ref[...] = jnp.zeros_like(acc_ref)
            dg_ref[...] = jnp.zeros_like(dg_ref)

        slot = i % 2

        def fetch(seq, kk, sl):
            return [_token_copy(x_hbm, seq, kk, ts, xbuf.at[sl], sems.at[sl, 0]),
                    _token_copy(t_hbm, seq, kk, ts, tbuf.at[sl], sems.at[sl, 1])]

        def start(seq, sl, kk):
            if kk == 0:
                xbuf[sl, 0:LANES, :] = jnp.zeros((LANES, d), F32)
                tbuf[sl, 0:LANES, :] = jnp.zeros((LANES, d), F32)
            for cp in fetch(seq, kk, sl):
                cp.start()

        @pl.when(i == 0)
        def _():
            start(0, 0, 0)

        @pl.when(i + 1 < steps)
        def _():
            _for_tile((i + 1) % nt, nt, functools.partial(start, (i + 1) // nt, 1 - slot))

        mix = _dot(a_ref[...], w_ref[0:ka, :]) + _dot(c_ref[...], w_ref[ka:, :])
        _for_tile(k, nt, lambda kk: [cp.wait() for cp in fetch(b, kk, slot)])

        real = (lax.broadcasted_iota(jnp.int32, (ts, d), 0) >= LANES) | (k > 0)
        g = g_ref[...]
        hhat, rstd = _rms_stats(xbuf[slot] + mix)
        e = jnp.where(real, hhat * g - tbuf[slot], 0.0)
        acc_ref[...] += jnp.sum(e * e, axis=0, keepdims=True)
        dy = e * (1.0 / d)
        dg_ref[...] += jnp.sum(dy * hhat, axis=0, keepdims=True)
        dhb_ref[...] = _rms_bwd(g * dy, hhat, rstd).astype(BF16)

        @pl.when(i == steps - 1)
        def _():
            total = jnp.sum(acc_ref[...], axis=1, keepdims=True)
            loss_ref[...] = jnp.broadcast_to((0.5 / d) * total, loss_ref.shape)

    hbm = pl.BlockSpec(memory_space=pl.ANY)
    row = pl.BlockSpec((ts, d), lambda i: (i, 0))
    vec = pl.BlockSpec((1, d), lambda i: (0, 0))
    return pl.pallas_call(
        body,
        name="out_proj_loss",
        grid=(steps,),
        in_specs=[pl.BlockSpec((ts, ka), lambda i: (i, 0)), pl.BlockSpec((ts, yc.shape[1]), lambda i: (i, 0)),
                  pl.BlockSpec(w_out.shape, lambda i: (0, 0)), hbm, hbm, vec],
        out_specs=(row, vec, pl.BlockSpec((1, LANES), lambda i: (0, 0))),
        out_shape=(
            jax.ShapeDtypeStruct((r, d), BF16),
            jax.ShapeDtypeStruct((1, d), F32),
            jax.ShapeDtypeStruct((1, LANES), F32),
        ),
        scratch_shapes=[pltpu.VMEM((2, ts, d), F32), pltpu.VMEM((2, ts, d), F32), pltpu.VMEM((1, d), F32),
                        pltpu.SemaphoreType.DMA((2, 2))],
        compiler_params=_params("arbitrary"),
    )(ya, yc, w_out, x, target, g_final)


def _out_proj_bwd(dhb, w_out, ya, yc, bm):
    r, d = dhb.shape
    ka = ya.shape[1]
    n_mix = w_out.shape[0]
    last = r // bm - 1

    def body(dh_ref, w_ref, a_ref, c_ref, dcat_ref, dw_ref, acc_ref):
        @pl.when(pl.program_id(0) == 0)
        def _():
            acc_ref[...] = jnp.zeros_like(acc_ref)

        dh = dh_ref[...]
        dcat_ref[...] = _dot(dh, w_ref[...], _NT).astype(BF16)
        acc_ref[0:ka, :] += _dot(a_ref[...], dh, _TN)
        acc_ref[ka:, :] += _dot(c_ref[...], dh, _TN)

        @pl.when(pl.program_id(0) == last)
        def _():
            dw_ref[...] = acc_ref[...].astype(BF16)

    return pl.pallas_call(
        body,
        name="out_proj_bwd",
        grid=(r // bm,),
        in_specs=[pl.BlockSpec((bm, d), lambda i: (i, 0)), pl.BlockSpec(w_out.shape, lambda i: (0, 0)),
                  pl.BlockSpec((bm, ka), lambda i: (i, 0)), pl.BlockSpec((bm, yc.shape[1]), lambda i: (i, 0))],
        out_specs=(pl.BlockSpec((bm, n_mix), lambda i: (i, 0)),
                   pl.BlockSpec((n_mix, d), lambda i: (0, 0))),
        out_shape=(jax.ShapeDtypeStruct((r, n_mix), BF16),
                   jax.ShapeDtypeStruct((n_mix, d), BF16)),
        scratch_shapes=[pltpu.VMEM((n_mix, d), F32)],
        compiler_params=_params("arbitrary"),
    )(dhb, w_out, ya, yc)


def _attn_bwd(q, k, v, o, lse, dcat, p, g_attn):
    nb_seq, _, tp, _ = q.shape

    def body(q_ref, k_ref, v_ref, o_ref, lse_ref, dy_ref, z_ref, g_ref,
             dq_ref, dk_ref, dv_ref, dz_ref, dg_ref, dq_acc):
        @pl.when(pl.program_id(1) == 0)
        def _():
            dg_ref[...] = jnp.zeros_like(dg_ref)

        g = g_ref[...]
        z = z_ref[...].astype(F32)
        o = o_ref[0, 0]
        dy = dy_ref[...].astype(F32)
        sig = _sigmoid(z)
        ohat, r = _rms_stats(o)
        don = dy * (z * sig)
        dz_ref[...] = (dy * (ohat * g) * (sig * (1.0 + z * (1.0 - sig)))).astype(BF16)
        dg_ref[...] += jnp.sum(don * ohat, axis=0, keepdims=True)
        do = _rms_bwd(g * don, ohat, r)
        dvec = jnp.sum(do * o, axis=-1, keepdims=True)
        dob = do.astype(BF16)
        lse_col = lse_ref[0, 0, :, 0:1]
        dq_acc[...] = jnp.zeros_like(dq_acc)
        for k0 in range(0, tp, K_TILE):
            nk = min(K_TILE, tp - k0)
            nq = tp - k0
            qq = q_ref[0, 0, k0:, :]
            kk = k_ref[0, 0, k0:k0 + nk, :]
            causal = (lax.broadcasted_iota(jnp.int32, (nq, nk), 1) <= lax.broadcasted_iota(jnp.int32, (nq, nk), 0))
            pr = jnp.where(causal, jnp.exp2(_dot(qq, kk, _NT) - lse_col[k0:]), 0.0)
            dp = _dot(dob[k0:], v_ref[0, 0, k0:k0 + nk, :], _NT)
            ds = (pr * (dp - dvec[k0:])).astype(BF16)
            dv_ref[0, 0, k0:k0 + nk, :] = _dot(pr.astype(BF16), dob[k0:], _TN).astype(BF16)
            dk_ref[0, 0, k0:k0 + nk, :] = (_dot(ds, qq, _TN) * (ATTN_SCALE / Q_SCALE)).astype(BF16)
            dq_acc[k0:, :] += _dot(ds, kk)
        dq_ref[0, 0] = (dq_acc[...] * ATTN_SCALE).astype(BF16)

    qk = pl.BlockSpec((1, 1, tp, 2 * LANES), lambda h, b: (b, h, 0, 0))
    hv = pl.BlockSpec((1, 1, tp, D_V), lambda h, b: (b, h, 0, 0))
    col = pl.BlockSpec((tp, LANES), lambda h, b: (b, h))
    return pl.pallas_call(
        body,
        name="attn_bwd",
        grid=(N_HEADS, nb_seq),
        in_specs=[qk, qk, hv, hv, hv, col,
                  pl.BlockSpec((tp, LANES), lambda h, b: (b, GRP_A // LANES + h)),
                  pl.BlockSpec((1, LANES), lambda h, b: (0, h))],
        out_specs=(qk, qk, hv, col, pl.BlockSpec((1, LANES), lambda h, b: (0, h))),
        out_shape=(
            jax.ShapeDtypeStruct((nb_seq, N_HEADS, tp, 2 * LANES), BF16),
            jax.ShapeDtypeStruct((nb_seq, N_HEADS, tp, 2 * LANES), BF16),
            jax.ShapeDtypeStruct((nb_seq, N_HEADS, tp, D_V), BF16),
            jax.ShapeDtypeStruct((nb_seq * tp, N_HEADS * D_V), BF16),
            jax.ShapeDtypeStruct((1, N_HEADS * D_V), F32),
        ),
        scratch_shapes=[pltpu.VMEM((tp, 2 * LANES), F32)],
        compiler_params=_params("arbitrary", "arbitrary"),
    )(q, k, v, o, lse, dcat, p, g_attn)


def _qkv_bwd(p, dq, dk, dv, wq, wkv, gq, gkv, tables):
    nb_seq, _, tp, _ = dq.shape
    ht = tp // 2

    def body(pa_ref, dq_ref, dk_ref, dv_ref, wq_ref, wkv_ref, gq_ref, gkv_ref, cos_ref, sa_ref, sb_ref,
             dpa_ref, dwq_ref, dwkv_ref, dgq_ref, dgkv_ref):
        @pl.when(pl.program_id(0) == 0)
        def _():
            dwq_ref[...] = jnp.zeros_like(dwq_ref)
            dwkv_ref[...] = jnp.zeros_like(dwkv_ref)
            dgq_ref[...] = jnp.zeros_like(dgq_ref)
            dgkv_ref[...] = jnp.zeros_like(dgkv_ref)

        pa = pa_ref[...].astype(F32)
        gq, gkv = gq_ref[...], gkv_ref[...]
        cq_hat, rq = _rms_stats(pa[:, :Q_RANK])
        ckv_hat, rkv = _rms_stats(pa[:, Q_RANK:Q_RANK + KV_RANK])
        tabs = (cos_ref[...], sa_ref[...], sb_ref[...])

        pe = [dq_ref[0, h, :, D_NOPE:].astype(F32) for h in range(N_HEADS)]
        pairs = [_rope_t(pe[2 * i] + pltpu.roll(pe[2 * i + 1], D_ROPE, 1), *tabs).astype(BF16) for i in range(2)]
        dq_flat = jnp.concatenate([dq_ref[0, h, :, :D_NOPE] for h in range(N_HEADS)] + pairs, axis=1)
        dwq_ref[...] += _dot(dq_flat, (cq_hat * gq).astype(BF16), _TN)
        dcqn = _dot(dq_flat, wq_ref[...])
        dgq_ref[...] += jnp.sum(dcqn * cq_hat, axis=0, keepdims=True)
        dcq = _rms_bwd(gq * dcqn, cq_hat, rq)

        dkv_flat = jnp.concatenate([dk_ref[0, h, :, :D_NOPE] for h in range(N_HEADS)]
                                   + [dv_ref[0, h] for h in range(N_HEADS)], axis=1)
        dwkv_ref[...] += _dot((ckv_hat * gkv).astype(BF16), dkv_flat, _TN)
        dckvn = _dot(dkv_flat, wkv_ref[...], _NT)
        dgkv_ref[...] += jnp.sum(dckvn * ckv_hat, axis=0, keepdims=True)
        dckv = _rms_bwd(gkv * dckvn, ckv_hat, rkv)

        dk_pe = dk_ref[0, 0, :, D_NOPE:].astype(F32)
        for h in range(1, N_HEADS):
            dk_pe = dk_pe + dk_ref[0, h, :, D_NOPE:].astype(F32)
        dk_pe = jnp.where(lax.broadcasted_iota(jnp.int32, (ht, LANES), 1) < D_ROPE, dk_pe, 0.0)
        dpa_ref[...] = jnp.concatenate([dcq, dckv, _rope_t(dk_pe, *tabs)], axis=1).astype(BF16)

    full = lambda a: pl.BlockSpec(a.shape, lambda i: (0,) * a.ndim)
    tab = pl.BlockSpec((ht, LANES), lambda i: (i % 2, 0))
    qk = pl.BlockSpec((1, N_HEADS, ht, 2 * LANES), lambda i: (i // 2, 0, i % 2, 0))
    acc = lambda shape: pl.BlockSpec(shape, lambda i: (0, 0))
    return pl.pallas_call(
        body,
        name="qkv_bwd",
        grid=(2 * nb_seq,),
        in_specs=[pl.BlockSpec((ht, GRP_A), lambda i: (i, 0)), qk, qk,
                  pl.BlockSpec((1, N_HEADS, ht, D_V), lambda i: (i // 2, 0, i % 2, 0)),
                  full(wq), full(wkv), full(gq), full(gkv), tab, tab, tab],
        out_specs=(pl.BlockSpec((ht, GRP_A), lambda i: (i, 0)),
                   acc(wq.shape), acc(wkv.shape), acc((1, Q_RANK)), acc((1, KV_RANK))),
        out_shape=(
            jax.ShapeDtypeStruct((nb_seq * tp, GRP_A), BF16),
            jax.ShapeDtypeStruct(wq.shape, F32),
            jax.ShapeDtypeStruct(wkv.shape, F32),
            jax.ShapeDtypeStruct((1, Q_RANK), F32),
            jax.ShapeDtypeStruct((1, KV_RANK), F32),
        ),
        compiler_params=_params("arbitrary"),
    )(p, dq, dk, dv, wq, wkv, gq, gkv, *tables)


def _conv_bwd(p, dcat, conv_w, g_conv, nb_seq, tp):
    cols = CONV_WIDTH // LANES

    def body(b_ref, c_ref, h_ref, z_ref, dy_ref, w_ref, g_ref,
             db_ref, dc_ref, dh_ref, dz_ref, dw_ref, dg_ref):
        @pl.when(pl.program_id(1) == 0)
        def _():
            dw_ref[...] = jnp.zeros_like(dw_ref)
            dg_ref[...] = jnp.zeros_like(dg_ref)

        cb, c, h = b_ref[...].astype(F32), c_ref[...].astype(F32), h_ref[...].astype(F32)
        z, dy = z_ref[...].astype(F32), dy_ref[...].astype(F32)
        g = g_ref[...]
        w0, w1, w2 = w_ref[0:1, :], w_ref[1:2, :], w_ref[2:3, :]
        cc = c * h
        row = lax.broadcasted_iota(jnp.int32, (tp, LANES), 0)
        s1 = jnp.where(row >= 1, pltpu.roll(cc, 1, 0), 0.0)
        s2 = jnp.where(row >= 2, pltpu.roll(cc, 2, 0), 0.0)
        dwc = w0 * s2 + w1 * s1 + w2 * cc
        yc = cb * dwc
        r = lax.rsqrt(_group_mean(yc * yc) + EPS)
        ychat = yc * r
        sig = _sigmoid(z)
        dz_ref[...] = (dy * (ychat * g) * (sig * (1.0 + z * (1.0 - sig)))).astype(BF16)
        dyn = dy * (z * sig)
        dg_ref[...] += jnp.sum(dyn * ychat, axis=0, keepdims=True)
        gd = g * dyn
        dyc = r * (gd - ychat * _group_mean(gd * ychat))
        db_ref[...] = (dyc * dwc).astype(BF16)
        ddw = dyc * cb
        dw_ref[0:1, :] += jnp.sum(ddw * s2, axis=0, keepdims=True)
        dw_ref[1:2, :] += jnp.sum(ddw * s1, axis=0, keepdims=True)
        dw_ref[2:3, :] += jnp.sum(ddw * cc, axis=0, keepdims=True)
        u1 = jnp.where(row <= tp - 2, pltpu.roll(ddw, tp - 1, 0), 0.0)
        u2 = jnp.where(row <= tp - 3, pltpu.roll(ddw, tp - 2, 0), 0.0)
        dcc = w2 * ddw + w1 * u1 + w0 * u2
        dc_ref[...] = (dcc * h).astype(BF16)
        dh_ref[...] = (dcc * c).astype(BF16)

    col = pl.BlockSpec((tp, LANES), lambda t, b: (b, t))
    out = jax.ShapeDtypeStruct((nb_seq * tp, CONV_WIDTH), BF16)
    return pl.pallas_call(
        body,
        name="conv_bwd",
        grid=(cols, nb_seq),
        in_specs=_conv_specs(tp, lambda t, b, off: (b, off + t)) + [
            pl.BlockSpec((tp, LANES), lambda t, b: (b, N_HEADS * D_V // LANES + t)),
            pl.BlockSpec((8, LANES), lambda t, b: (0, t)),
            pl.BlockSpec((1, LANES), lambda t, b: (0, t))],
        out_specs=(col, col, col, col,
                   pl.BlockSpec((8, LANES), lambda t, b: (0, t)), pl.BlockSpec((1, LANES), lambda t, b: (0, t))),
        out_shape=(out, out, out, out,
                   jax.ShapeDtypeStruct((8, CONV_WIDTH), F32), jax.ShapeDtypeStruct((1, CONV_WIDTH), F32)),
        compiler_params=_params("arbitrary", "arbitrary"),
    )(p, p, p, p, dcat, conv_w, g_conv)


def _input_bwd(dps, w_in, x, meta, dh, norm_g, nt, send_in):
    nb_seq, s, d = x.shape
    r, kb = dps[0].shape
    ts = (s + LANES) // nt
    steps = nb_seq * nt
    n_dp = len(dps)
    in_slot = send_in.shape[1:]

    def body(*refs):
        dp_refs, w_ref, x_hbm, meta_ref, dh_ref, g_ref, pay_ref = refs[:n_dp], *refs[n_dp:n_dp + 6]
        o = n_dp + 6
        gx_hbm, dmeta_ref, dg_ref, r2_in = refs[o:o + 4]
        xbuf, gxbuf, tok_sems, relay_in, own_in, r1_in, sum_in = refs[o + 4:o + 11]
        sems = refs[o + 11:]
        i = pl.program_id(0)
        b, k = i // nt, i % nt

        def plan():
            return _reduce_plan((pay_ref,), (own_in,), (r1_in,), (sum_in,), (r2_in,), *sems, relays=(relay_in,))

        @pl.when(i == 0)
        def _():
            dmeta_ref[...] = jnp.zeros_like(dmeta_ref)
            dg_ref[...] = jnp.zeros_like(dg_ref)
            plan()[0]()

        @pl.when(i == 1)
        def _():
            plan()[1]()

        @pl.when(i == steps // 2)
        def _():
            plan()[2]()

        def start(kk):
            if kk == 0:
                xbuf[0:PAD_FRONT, :] = jnp.zeros((PAD_FRONT, d), F32)
                xbuf[PAD_FRONT:LANES, :] = meta_ref[...]
            _token_copy(x_hbm, b, kk, ts, xbuf, tok_sems.at[0]).start()

        _for_tile(k, nt, start)
        du = _dot(dp_refs[0][...], w_ref[0:kb, :])
        for j in range(1, n_dp):
            du = du + _dot(dp_refs[j][...], w_ref[kb * j:kb * (j + 1), :])
        _for_tile(k, nt, lambda kk: _token_copy(x_hbm, b, kk, ts, xbuf, tok_sems.at[0]).wait())

        g = g_ref[...]
        hhat, rstd = _rms_stats(xbuf[...])
        dg_ref[...] += jnp.sum(du * hhat, axis=0, keepdims=True)
        res = _rms_bwd(g * du, hhat, rstd) + dh_ref[...].astype(F32)

        @pl.when(i > 0)
        def _():
            _for_tile(k, nt, lambda kk: _token_copy(gx_hbm, b, (kk - 1) % nt, ts, gxbuf, tok_sems.at[1], True).wait())

        gxbuf[...] = res

        @pl.when(k == 0)
        def _():
            dmeta_ref[...] += gxbuf[PAD_FRONT:LANES, :]

        _for_tile(k, nt, lambda kk: _token_copy(gx_hbm, b, kk, ts, gxbuf, tok_sems.at[1], True).start())

        @pl.when(i == steps - 1)
        def _():
            _token_copy(gx_hbm, b, nt - 1, ts, gxbuf, tok_sems.at[1], True).wait()
            plan()[3]()

    whole = lambda a: pl.BlockSpec(a.shape, lambda i: (0,) * a.ndim)
    hbm = pl.BlockSpec(memory_space=pl.ANY)
    return pl.pallas_call(
        body,
        name="input_bwd",
        grid=(steps,),
        in_specs=[pl.BlockSpec((ts, kb), lambda i: (i, 0)) for _ in dps]
        + [whole(w_in), hbm, whole(meta), pl.BlockSpec((ts, d), lambda i: (i, 0)), whole(norm_g), hbm],
        out_specs=(hbm, pl.BlockSpec((N_META, d), lambda i: (0, 0)), pl.BlockSpec((1, d), lambda i: (0, 0)), hbm),
        out_shape=(jax.ShapeDtypeStruct((nb_seq, s, d), F32),
                   jax.ShapeDtypeStruct((N_META, d), F32),
                   jax.ShapeDtypeStruct((1, d), F32),
                   jax.ShapeDtypeStruct((3,) + in_slot, BF16)),
        scratch_shapes=[pltpu.VMEM((ts, d), F32), pltpu.VMEM((ts, d), F32), pltpu.SemaphoreType.DMA((2,)),
                        pltpu.VMEM((2,) + in_slot, BF16)]
        + _reduce_scratch([(in_slot, BF16)], [True]),
        compiler_params=_params("arbitrary"),
    )(*dps, w_in, x, meta, dh, norm_g, send_in)


def _in_proj_bwd_w(u, dps, bm, small_grads, send_out):
    r, d = u.shape
    kb = dps[0].shape[1]
    steps = r // bm
    n_dp, n_small = len(dps), len(small_grads)
    out_slot, small_slot = send_out.shape[1:], (SMALL_ROWS, LANES)

    def body(*refs):
        u_ref, dp_refs = refs[0], refs[1:1 + n_dp]
        small_refs = refs[1 + n_dp:1 + n_dp + n_small]
        o = 1 + n_dp + n_small
        pay_out, o_ref, r2_out, r2_small = refs[o:o + 4]
        acc_ref, ssmall, r1_out, sum_out, r1_small, sum_small = refs[o + 4:o + 10]
        sems = refs[o + 10:]
        i = pl.program_id(0)

        def plan():
            return _reduce_plan((pay_out, ssmall), (None, None), (r1_out, r1_small), (sum_out, sum_small),
                                (r2_out, r2_small), *sems)

        @pl.when(i == 0)
        def _():
            acc_ref[...] = jnp.zeros_like(acc_ref)
            _pack_small(ssmall, *small_refs)
            plan()[0]()

        @pl.when(i == 1)
        def _():
            plan()[1]()

        uu = u_ref[...]
        for j in range(n_dp):
            acc_ref[kb * j:kb * (j + 1), :] += _dot(dp_refs[j][...], uu, _TN)

        @pl.when(i == steps - 1)
        def _():
            for k in range(N_DEV):
                for s, e, c0 in _in_pieces(k):
                    o_ref[k, s:e, :] = acc_ref[c0:c0 + e - s, :].astype(BF16)
                o_ref[k, SHARD_IN:, :] = jnp.zeros((SHARD_IN_PAD - SHARD_IN, d), BF16)
            plan()[2]()

    whole = lambda a: pl.BlockSpec(a.shape, lambda i: (0,) * a.ndim)
    hbm = pl.BlockSpec(memory_space=pl.ANY)
    return pl.pallas_call(
        body,
        name="in_proj_bwd_w",
        grid=(steps,),
        in_specs=[pl.BlockSpec((bm, d), lambda i: (i, 0))]
        + [pl.BlockSpec((bm, kb), lambda i: (i, 0)) for _ in dps] + [whole(a) for a in small_grads]
        + [whole(send_out)],
        out_specs=(pl.BlockSpec((N_DEV, SHARD_IN_PAD, d), lambda i: (0, 0, 0)), hbm, hbm),
        out_shape=(jax.ShapeDtypeStruct((N_DEV, SHARD_IN_PAD, d), BF16),
                   jax.ShapeDtypeStruct((N_CHIPS,) + out_slot, BF16),
                   jax.ShapeDtypeStruct((N_CHIPS,) + small_slot, F32)),
        scratch_shapes=[pltpu.VMEM((kb * n_dp, d), F32), pltpu.VMEM((N_DEV,) + small_slot, F32)]
        + _reduce_scratch([(out_slot, BF16), (small_slot, F32)], [False, False]),
        compiler_params=_params("arbitrary"),
    )(u, *dps, *small_grads, send_out)


def _local_step(x, loss_target, u, p, meta_f, norm_g, w_in_p, q_norm_g, w_q_p, kv_norm_g, w_kv_p, conv_w_f,
                attn_out_g, conv_out_g, w_out_f, g_final):
    nb_seq, s, d = x.shape
    tp = s + LANES
    ht = tp // 2
    tables = _rope_tables(tp)

    q, k, v = _qkv_fwd(p, w_q_p, w_kv_p, q_norm_g, kv_norm_g, tables, nb_seq, tp)
    ya, o, lse = _attn_fwd(q, k, v, p, attn_out_g)
    yc = _conv_fwd(p, conv_w_f, conv_out_g, nb_seq, tp)
    dhb, d_final_g, loss_part = _out_proj_loss(ya, yc, w_out_f, x, loss_target, g_final, TOKEN_TILES)

    dcat, d_w_out = _out_proj_bwd(dhb, w_out_f, ya, yc, ht)
    send_out = d_w_out.reshape(N_DEV, SHARD_OUT, d)
    dq, dk, dv, dz_attn, d_attn_g = _attn_bwd(q, k, v, o, lse, dcat, p, attn_out_g)
    dpa, d_wq_p, d_wkv_p, d_gq, d_gkv = _qkv_bwd(p, dq, dk, dv, w_q_p, w_kv_p, q_norm_g, kv_norm_g, tables)
    d_b, d_c, d_h, dz_conv, d_conv_w, d_conv_g = _conv_bwd(p, dcat, conv_w_f, conv_out_g, nb_seq, tp)
    dps = (dpa, dz_attn, d_b, d_c, d_h, dz_conv)
    small = (d_wq_p, d_wkv_p, d_conv_w, d_final_g, d_gq, d_gkv, d_attn_g, d_conv_g, loss_part)
    send_in, r_out, r_small = _in_proj_bwd_w(u, dps, ht, small, send_out)
    grad_x, d_meta, d_norm_g, r_in = _input_bwd(dps, w_in_p, x, meta_f, dhb, norm_g, TOKEN_TILES, send_in)
    return grad_x, r_in, r_out, r_small, d_meta, d_norm_g


def kernel(x, meta_tokens, norm_g, w_in, q_norm_g, w_q_up, kv_norm_g, w_kv_up, conv_w, attn_out_g, conv_out_g, w_out, final_norm_g, loss_target, m_meta_tokens, m_norm_g, m_w_in, m_q_norm_g, m_w_q_up, m_kv_norm_g, m_w_kv_up, m_conv_w, m_attn_out_g, m_conv_out_g, m_w_out, m_final_norm_g, v_meta_tokens, v_norm_g, v_w_in, v_q_norm_g, v_w_q_up, v_kv_norm_g, v_w_kv_up, v_conv_w, v_attn_out_g, v_conv_out_g, v_w_out, v_final_norm_g):
    d = x.shape[-1]
    ht = (x.shape[1] + LANES) // 2
    u, w_in_p, meta_f = _prep_gather(x, meta_tokens, norm_g, w_in[0].T)
    p, w_q_p, w_kv_p, w_out_f, conv_w_f = _in_proj_gather(
        u, w_in_p, w_q_up[0].T, w_kv_up[0], w_out[0], conv_w.transpose(1, 0, 2), ht, 3 * GRP_A)
    g_final = final_norm_g.reshape(1, d)
    grad_x, r_in, r_out, r_small, d_meta, d_norm_g = _local_step(
        x, loss_target, u, p, meta_f, norm_g, w_in_p, q_norm_g, w_q_p, kv_norm_g, w_kv_p, conv_w_f,
        attn_out_g, conv_out_g, w_out_f, g_final)

    flat = lambda a: a.reshape(a.shape[-2:]) if a.ndim == 3 else a.reshape(1, -1) if a.ndim == 1 else a
    transposed = ("w_in", "w_q_up")

    def to_kernel(n, a):
        if n == "conv_w":
            return a.transpose(1, 0, 2)
        return flat(a).T if n in transposed else flat(a)

    def from_kernel(n, a, shape):
        if n == "conv_w":
            return a.transpose(1, 0, 2)
        return (a.T if n in transposed else a).reshape(shape)
    params = {
        "meta_tokens": (meta_tokens, m_meta_tokens, v_meta_tokens),
        "norm_g": (norm_g, m_norm_g, v_norm_g),
        "w_in": (w_in, m_w_in, v_w_in),
        "q_norm_g": (q_norm_g, m_q_norm_g, v_q_norm_g),
        "w_q_up": (w_q_up, m_w_q_up, v_w_q_up),
        "kv_norm_g": (kv_norm_g, m_kv_norm_g, v_kv_norm_g),
        "w_kv_up": (w_kv_up, m_w_kv_up, v_w_kv_up),
        "conv_w": (conv_w, m_conv_w, v_conv_w),
        "attn_out_g": (attn_out_g, m_attn_out_g, v_attn_out_g),
        "conv_out_g": (conv_out_g, m_conv_out_g, v_conv_out_g),
        "w_out": (w_out, m_w_out, v_w_out),
        "final_norm_g": (final_norm_g, m_final_norm_g, v_final_norm_g),
    }
    grads, loss = _reduce_tail(r_in, r_out, r_small, d_meta, d_norm_g)
    updated = _adamw(grads, {n: tuple(to_kernel(n, a) for a in t) for n, t in params.items()})
    outs = [[from_kernel(n, updated[n][i], params[n][0].shape) for n, _ in PARAM_SHAPES] for i in range(4)]
    return (loss[0, 0], grad_x, *outs[0], *outs[1], *outs[2], *outs[3])
```

```python
import functools

import jax
import jax.numpy as jnp
from jax import lax
from jax.experimental import pallas as pl
from jax.experimental.pallas import tpu as pltpu

F32 = jnp.float32
BF16 = jnp.bfloat16

N_META = 16
D_MODEL = 1024
N_HEADS = 4
D_NOPE = 128
D_ROPE = 64
D_V = 128
Q_RANK = 256
KV_RANK = 128
CONV_WIDTH = 512
CONV_GROUP = 64
ROPE_THETA = 10000.0
ATTN_SCALE = (D_NOPE + D_ROPE) ** -0.5
Q_SCALE = ATTN_SCALE * 1.4426950408889634
EPS = 1e-6
NEG_INF = -1e30

ADAM_LR = 0.001
ADAM_B1 = 0.9
ADAM_B2 = 0.999
ADAM_EPS = 1e-08
ADAM_WD = 0.01
ADAM_STEP = 10

LANES = 128
PAD_FRONT = LANES - N_META
K_TILE = 256
Q_TILE = 512
N_DEV = 8
VMEM_LIMIT = 56 * 1024 * 1024

IN_PAD = 3072
GRP_A = 512
N_A = Q_RANK + KV_RANK + D_ROPE
IN_PROJ = 3008
SHARD_IN = IN_PROJ // N_DEV
SHARD_IN_PAD = 384
SHARD_Q = 96
SHARD_KV = 128
SHARD_OUT = 128
SHARD_CONV = 64
SHARD_META = 128
Q_COLS = N_HEADS * (D_NOPE + D_ROPE)
KV_COLS = N_HEADS * (D_NOPE + D_V)

ROW_Q, ROW_KV, ROW_META, ROW_CONV = 0, 256, 384, 400
ROW_REPL = 408
ROW_NORM, ROW_FINAL, ROW_GQ, ROW_GKV, ROW_ATTN, ROW_CONVG, ROW_LOSS = 408, 416, 424, 426, 427, 431, 435
SMALL_ROWS = 440

PARAM_SHAPES = (
    ("meta_tokens", (N_META, SHARD_META)), ("norm_g", (1, D_MODEL)), ("w_in", (SHARD_IN, D_MODEL)),
    ("q_norm_g", (1, Q_RANK)), ("w_q_up", (SHARD_Q, Q_RANK)), ("kv_norm_g", (1, KV_RANK)),
    ("w_kv_up", (KV_RANK, SHARD_KV)), ("conv_w", (3, 1, SHARD_CONV)), ("attn_out_g", (1, CONV_WIDTH)),
    ("conv_out_g", (1, CONV_WIDTH)), ("w_out", (SHARD_OUT, D_MODEL)), ("final_norm_g", (1, D_MODEL)),
)


def _in_pieces(k):
    lo, hi = SHARD_IN * k, SHARD_IN * (k + 1)
    out = []
    if lo < N_A:
        out.append((0, min(hi, N_A) - lo, lo))
    if hi > N_A:
        s = max(lo, N_A)
        out.append((s - lo, hi - lo, s + GRP_A - N_A))
    return out


def _q_pieces(k):
    lo, hi = SHARD_Q * k, SHARD_Q * (k + 1)
    out = []
    for h in range(N_HEADS):
        base = (D_NOPE + D_ROPE) * h
        s, e = max(lo, base), min(hi, base + D_NOPE)
        if s < e:
            out.append((s - lo, e - lo, D_NOPE * h + s - base))
        s, e = max(lo, base + D_NOPE), min(hi, base + D_NOPE + D_ROPE)
        if s < e:
            out.append((s - lo, e - lo, N_HEADS * D_NOPE + D_ROPE * h + s - base - D_NOPE))
    return out


def _kv_dst(k):
    return D_NOPE * (k // 2) + (N_HEADS * D_NOPE if k % 2 else 0)


def _params(*sem):
    return pltpu.CompilerParams(dimension_semantics=sem, vmem_limit_bytes=VMEM_LIMIT)


def _rms_stats(x):
    r = lax.rsqrt(jnp.mean(x * x, axis=-1, keepdims=True) + EPS)
    return x * r, r


def _rms_bwd(gdy, xhat, r):
    return r * (gdy - xhat * jnp.mean(gdy * xhat, axis=-1, keepdims=True))


def _sigmoid(z):
    return 1.0 / (1.0 + jnp.exp(-z))


def _group_mean(x):
    i0 = lax.broadcasted_iota(jnp.int32, (LANES, LANES), 0) // CONV_GROUP
    i1 = lax.broadcasted_iota(jnp.int32, (LANES, LANES), 1) // CONV_GROUP
    m = jnp.where(i0 == i1, 1.0 / CONV_GROUP, 0.0).astype(BF16)
    hi = x.astype(BF16)
    lo = (x - hi.astype(F32)).astype(BF16)
    return jnp.dot(hi, m, preferred_element_type=F32) + jnp.dot(lo, m, preferred_element_type=F32)


_NT = (((1,), (1,)), ((), ()))
_TN = (((0,), (0,)), ((), ()))


def _dot(a, b, dims=None):
    if dims is None:
        return jnp.dot(a, b, preferred_element_type=F32)
    return lax.dot_general(a, b, dims, preferred_element_type=F32)


def _device_position():
    x, y, c = lax.axis_index("x"), lax.axis_index("y"), lax.axis_index("c")
    return x, y, c, 4 * x + 2 * y + c


def _near_far(x, y, c):
    flip = lambda v, on: v + on - 2 * v * on
    return (flip(x, 1 - c), flip(y, c)), (flip(x, c), flip(y, 1 - c))


def _gather_plan(srcs, slots, send_sems, recv_sems, local_sems):
    x, y, c, _ = _device_position()
    me, sibling = (x, y, c), (x, y, 1 - c)
    near, far = _near_far(x, y, c)
    diag = (1 - x, 1 - y)
    n = len(srcs)

    def slot(a, px, py, pc):
        return slots[a].at[4 * px + 2 * py + pc]

    def copy(a, k, block, to, own=False):
        return pltpu.make_async_remote_copy(
            src_ref=srcs[a] if own else slot(a, *block),
            dst_ref=slot(a, *block),
            send_sem=send_sems.at[7 * a + k],
            recv_sem=recv_sems.at[7 * a + k],
            device_id=to,
            device_id_type=pl.DeviceIdType.MESH,
        )

    def local(a):
        return pltpu.make_async_copy(srcs[a], slot(a, *me), local_sems.at[a])

    sent = [(me, sibling), (me, (*near, c)), (me, (*far, c)), ((*near, c), (*far, c)),
            ((*near, c), sibling), ((*far, c), sibling), ((*diag, c), sibling)]
    landed = [sibling, (*near, c), (*far, c), (*diag, c), (*far, 1 - c), (*near, 1 - c), (*diag, 1 - c)]

    def send(a, k):
        return copy(a, k, *sent[k], own=k < 3)

    def arrival(a, k):
        return copy(a, k, landed[k], me)

    def start():
        for a in range(n):
            local(a).start()
            for k in range(3):
                send(a, k).start()

    def relay():
        for a in range(n):
            arrival(a, 1).wait_recv()
            send(a, 3).start()
            send(a, 4).start()

    def forward():
        for k in (2, 3):
            for a in range(n):
                arrival(a, k).wait_recv()
                send(a, k + 3).start()

    def finish():
        for a in range(n):
            for k in (0, 4, 5, 6):
                arrival(a, k).wait_recv()
        for a in range(n):
            for k in range(7):
                send(a, k).wait_send()
            local(a).wait()

    return start, relay, forward, finish


def _adam_update(g, w, m, v):
    m_new = ADAM_B1 * m + (1.0 - ADAM_B1) * g
    v_new = ADAM_B2 * v + (1.0 - ADAM_B2) * (g * g)
    m_hat = m_new / (1.0 - ADAM_B1 ** ADAM_STEP)
    v_hat = v_new / (1.0 - ADAM_B2 ** ADAM_STEP)
    return -ADAM_LR * (m_hat / (jnp.sqrt(v_hat) + ADAM_EPS) + ADAM_WD * w), m_new, v_new


def _adamw(grads, params):
    names = [n for n, _ in PARAM_SHAPES]
    n_p = len(names)

    def body(*refs):
        for i in range(n_p):
            g = refs[i][...]
            w, m, v = (refs[n_p + 3 * i + j][...] for j in range(3))
            delta, m_new, v_new = _adam_update(g, w, m, v)
            for j, val in enumerate((g, delta, m_new, v_new)):
                refs[4 * n_p + 4 * i + j][...] = val

    vm = pl.BlockSpec(memory_space=pltpu.VMEM)
    out_shape = []
    for _, shape in PARAM_SHAPES:
        out_shape += [jax.ShapeDtypeStruct(shape, F32)] * 4
    outs = pl.pallas_call(
        body,
        name="adamw",
        out_shape=tuple(out_shape),
        in_specs=[vm] * (4 * n_p),
        out_specs=(vm,) * (4 * n_p),
        compiler_params=pltpu.CompilerParams(vmem_limit_bytes=VMEM_LIMIT),
    )(*[grads[n] for n in names], *[a for n in names for a in params[n]])
    return {n: outs[4 * i:4 * i + 4] for i, n in enumerate(names)}


N_CHIPS = 4


def _reduce_plan(pays, owns, r1s, sums, r2s, send1, recv1, send2, recv2, local_sems, relays=None):
    x, y, c, _ = _device_position()
    sibling = (x, y, 1 - c)
    chips = [((1 - x if rj & 2 else x), (1 - y if rj & 1 else y)) for rj in range(N_CHIPS)]
    n = len(pays)

    def slot_of(rj, core):
        return 4 * chips[rj][0] + 2 * chips[rj][1] + core

    def to_sibling(a, rj):
        return pltpu.make_async_remote_copy(
            src_ref=pays[a].at[slot_of(rj, 1 - c)], dst_ref=r1s[a].at[rj],
            send_sem=send1.at[N_CHIPS * a + rj], recv_sem=recv1.at[N_CHIPS * a + rj],
            device_id=sibling, device_id_type=pl.DeviceIdType.MESH)

    def load_own(a, rj):
        return pltpu.make_async_copy(pays[a].at[slot_of(rj, c)], owns[a].at[rj], local_sems.at[2 * N_CHIPS * a + rj])

    def to_chip(a, rj):
        return pltpu.make_async_remote_copy(
            src_ref=sums[a].at[rj], dst_ref=r2s[a].at[rj],
            send_sem=send2.at[N_CHIPS * a + rj], recv_sem=recv2.at[N_CHIPS * a + rj],
            device_id=(*chips[rj], c), device_id_type=pl.DeviceIdType.MESH)

    def keep(a):
        return pltpu.make_async_copy(sums[a].at[0], r2s[a].at[0], local_sems.at[2 * N_CHIPS * a + N_CHIPS])

    near_rj, far_rj = 2 - c, 1 + c
    near, far = _near_far(x, y, c)

    def ordered(a, k):
        src, dst, to = ((sums[a].at[near_rj], r2s[a].at[1], near), (sums[a].at[3], relays[a].at[0], near),
                        (relays[a].at[1], r2s[a].at[2], far))[k]
        return pltpu.make_async_remote_copy(
            src_ref=src, dst_ref=dst, send_sem=send2.at[N_CHIPS * a + 1 + k], recv_sem=recv2.at[N_CHIPS * a + 1 + k],
            device_id=(*to, c), device_id_type=pl.DeviceIdType.MESH)

    def start():
        for a in range(n):
            for rj in range(N_CHIPS):
                to_sibling(a, rj).start()
                if owns[a] is not None:
                    load_own(a, rj).start()

    def combine():
        for a in range(n):
            for rj in range(N_CHIPS):
                to_sibling(a, rj).wait_recv()
                if owns[a] is not None:
                    load_own(a, rj).wait()
                    mine = owns[a][rj]
                else:
                    mine = pays[a][slot_of(rj, c)]
                sums[a][rj] = (mine.astype(F32) + r1s[a][rj].astype(F32)).astype(sums[a].dtype)
            keep(a).start()
            if relays is None:
                for rj in range(1, N_CHIPS):
                    to_chip(a, rj).start()
            else:
                ordered(a, 1).start()
                ordered(a, 0).start()

    def relay():
        for a in range(n):
            ordered(a, 1).wait_recv()
            relays[a][1] = (sums[a][far_rj].astype(F32) + relays[a][0].astype(F32)).astype(sums[a].dtype)
            ordered(a, 2).start()

    def finish():
        for a in range(n):
            if relays is None:
                for rj in range(1, N_CHIPS):
                    to_chip(a, rj).wait_recv()
            else:
                ordered(a, 0).wait_recv()
                ordered(a, 2).wait_recv()
            for rj in range(N_CHIPS):
                to_sibling(a, rj).wait_send()
            if relays is None:
                for rj in range(1, N_CHIPS):
                    to_chip(a, rj).wait_send()
            else:
                for k in range(3):
                    ordered(a, k).wait_send()
            keep(a).wait()

    return (start, combine, finish) if relays is None else (start, combine, relay, finish)


def _reduce_scratch(shapes_dtypes, own_flags):
    out = []
    for (shape, dtype), own in zip(shapes_dtypes, own_flags):
        if own:
            out.append(pltpu.VMEM((N_CHIPS,) + shape, dtype))
        out += [pltpu.VMEM((N_CHIPS,) + shape, dtype), pltpu.VMEM((N_CHIPS,) + shape, dtype)]
    n = len(shapes_dtypes)
    out += [pltpu.SemaphoreType.DMA((N_CHIPS * n,))] * 4 + [pltpu.SemaphoreType.DMA((2 * N_CHIPS * n,))]
    return out


def _pack_small(ssmall, dwq, dwkv, dconv, dfinal, dgq, dgkv, dattn, dconvg, loss_part):
    ssmall[...] = jnp.zeros_like(ssmall)
    rep = ssmall.at[0]
    for i in range(D_MODEL // LANES):
        rep[ROW_FINAL + i:ROW_FINAL + i + 1, :] = dfinal[:, LANES * i:LANES * (i + 1)]
    for i in range(Q_RANK // LANES):
        rep[ROW_GQ + i:ROW_GQ + i + 1, :] = dgq[:, LANES * i:LANES * (i + 1)]
    rep[ROW_GKV:ROW_GKV + 1, :] = dgkv[...]
    for i in range(CONV_WIDTH // LANES):
        rep[ROW_ATTN + i:ROW_ATTN + i + 1, :] = dattn[:, LANES * i:LANES * (i + 1)]
        rep[ROW_CONVG + i:ROW_CONVG + i + 1, :] = dconvg[:, LANES * i:LANES * (i + 1)]
    rep[ROW_LOSS:ROW_LOSS + 1, :] = loss_part[...]
    for k in range(N_DEV):
        if k:
            ssmall[k, ROW_REPL:, :] = ssmall[0, ROW_REPL:, :]
        for s, e, d in _q_pieces(k):
            for i in range(Q_RANK // LANES):
                ssmall[k, ROW_Q + SHARD_Q * i + s:ROW_Q + SHARD_Q * i + e, :] = dwq[d:d + e - s, LANES * i:LANES * (i + 1)]
        ssmall[k, ROW_KV:ROW_KV + KV_RANK, :] = dwkv[:, _kv_dst(k):_kv_dst(k) + SHARD_KV]
        ssmall[k, ROW_CONV:ROW_CONV + 3, 0:SHARD_CONV] = dconv[0:3, SHARD_CONV * k:SHARD_CONV * (k + 1)]


TOKEN_TILES = 4
TAIL_ROWS = N_META + D_MODEL // LANES


def _reduce_tail(r_in, r_out, r_small, d_meta, d_norm):
    n_p = len(PARAM_SHAPES)
    names = [n for n, _ in PARAM_SHAPES]

    def body(*refs):
        rin, rout, rsmall, dmeta, dnorm = refs[:5]
        g_out = {n: refs[5 + i] for i, n in enumerate(names)}
        loss_out = refs[5 + n_p]
        stail, rtail, gsum, gtail, send_sems, recv_sems = refs[6 + n_p:]
        x, y, c, me = _device_position()
        my_chip = 2 * x + y

        for k in range(N_DEV):
            stail[k, 0:N_META, :] = dmeta[:, SHARD_META * k:SHARD_META * (k + 1)]
            for i in range(D_MODEL // LANES):
                stail[k, N_META + i:N_META + i + 1, :] = dnorm[:, LANES * i:LANES * (i + 1)]
        copies = []
        for r in range(1, N_DEV):
            peer = (1 - x if r & 4 else x, 1 - y if r & 2 else y, 1 - c if r & 1 else c)
            copies.append(pltpu.make_async_remote_copy(
                src_ref=stail.at[4 * peer[0] + 2 * peer[1] + peer[2]],
                dst_ref=rtail.at[r],
                send_sem=send_sems.at[r - 1],
                recv_sem=recv_sems.at[r - 1],
                device_id=peer,
                device_id_type=pl.DeviceIdType.MESH,
            ))
        for cp in copies:
            cp.start()
        rtail[0] = stail[me]

        g = rin[0].astype(F32) + rin[1].astype(F32) + rin[2].astype(F32)
        g_out["w_in"][...] = g[:SHARD_IN, :]

        g = rout[my_chip].astype(F32)
        gs = rsmall[my_chip]
        for ch in range(1, N_CHIPS):
            g = g + rout[ch ^ my_chip].astype(F32)
            gs = gs + rsmall[ch ^ my_chip]
        g_out["w_out"][...] = g
        gsum[...] = gs
        for i in range(Q_RANK // LANES):
            g_out["w_q_up"][:, LANES * i:LANES * (i + 1)] = gsum[ROW_Q + SHARD_Q * i:ROW_Q + SHARD_Q * (i + 1), :]
        g_out["w_kv_up"][...] = gsum[ROW_KV:ROW_KV + KV_RANK, :]
        for i in range(3):
            g_out["conv_w"][i] = gsum[ROW_CONV + i:ROW_CONV + i + 1, 0:SHARD_CONV]
        for name, row, width in (("final_norm_g", ROW_FINAL, D_MODEL), ("q_norm_g", ROW_GQ, Q_RANK),
                                 ("kv_norm_g", ROW_GKV, KV_RANK), ("attn_out_g", ROW_ATTN, CONV_WIDTH),
                                 ("conv_out_g", ROW_CONVG, CONV_WIDTH)):
            for i in range(width // LANES):
                g_out[name][:, LANES * i:LANES * (i + 1)] = gsum[row + i:row + i + 1, :]
        loss_out[...] = gsum[ROW_LOSS:ROW_LOSS + 1, :]

        for cp in copies:
            cp.wait_recv()
        gt = rtail[me]
        for d in range(1, N_DEV):
            gt = gt + rtail[d ^ me]
        gtail[...] = gt
        g_out["meta_tokens"][...] = gtail[0:N_META, :]
        for i in range(D_MODEL // LANES):
            g_out["norm_g"][:, LANES * i:LANES * (i + 1)] = gtail[N_META + i:N_META + i + 1, :]
        for cp in copies:
            cp.wait_send()

    vm = pl.BlockSpec(memory_space=pltpu.VMEM)
    out_shape = [jax.ShapeDtypeStruct(shape, F32) for _, shape in PARAM_SHAPES]
    out_shape.append(jax.ShapeDtypeStruct((1, LANES), F32))
    outs = pl.pallas_call(
        body,
        name="reduce_tail",
        out_shape=tuple(out_shape),
        in_specs=[vm] * 5,
        out_specs=(vm,) * len(out_shape),
        scratch_shapes=[
            pltpu.VMEM((N_DEV, TAIL_ROWS, LANES), F32),
            pltpu.VMEM((N_DEV, TAIL_ROWS, LANES), F32),
            pltpu.VMEM((SMALL_ROWS, LANES), F32),
            pltpu.VMEM((TAIL_ROWS, LANES), F32),
            pltpu.SemaphoreType.DMA((N_DEV - 1,)),
            pltpu.SemaphoreType.DMA((N_DEV - 1,)),
        ],
        compiler_params=pltpu.CompilerParams(vmem_limit_bytes=VMEM_LIMIT),
    )(r_in, r_out, r_small, d_meta, d_norm)
    return {n: outs[i] for i, n in enumerate(names)}, outs[-1]


def _prep_gather(x, meta, norm_g, w_in_t):
    nb_seq, s, d = x.shape
    nb = s // LANES + 1
    relay_step = nb_seq * (nb - 1) // 2
    forward_step = nb_seq * (nb - 1) - 1
    finish_step = nb_seq * (nb - 1)

    def body(x_ref, meta_ref, g_ref, win_ref, u_ref, w_in_p, meta_f,
             sbig, ssmall, gbig, gsmall, send_sems, recv_sems, local_sems):
        jj, b = pl.program_id(0), pl.program_id(1)
        t = jj * nb_seq + b

        def plan():
            return _gather_plan((sbig, ssmall), (gbig, gsmall), send_sems, recv_sems, local_sems)

        @pl.when(t == 0)
        def _():
            sbig[0:SHARD_IN, :] = win_ref[...].astype(BF16)
            sbig[SHARD_IN:, :] = jnp.zeros((SHARD_IN_PAD - SHARD_IN, d), BF16)
            ssmall[...] = meta_ref[...]
            plan()[0]()

        @pl.when(t == relay_step)
        def _():
            plan()[1]()

        @pl.when(t == forward_step)
        def _():
            plan()[2]()

        @pl.when(t == finish_step)
        def _():
            plan()[3]()
            w_in_p[N_A:GRP_A, :] = jnp.zeros((GRP_A - N_A, d), BF16)
            for k in range(N_DEV):
                for s0, e0, d0 in _in_pieces(k):
                    w_in_p[d0:d0 + e0 - s0, :] = gbig[k, s0:e0, :]
                meta_f[:, SHARD_META * k:SHARD_META * (k + 1)] = gsmall[k]

        def norm(h):
            hhat, _ = _rms_stats(h)
            return (hhat * g_ref[...]).astype(BF16)

        @pl.when(jj < nb - 1)
        def _():
            u_ref[...] = norm(x_ref[0])

        @pl.when(jj == nb - 1)
        def _():
            u_ref[0:PAD_FRONT, :] = jnp.zeros((PAD_FRONT, d), BF16)
            u_ref[PAD_FRONT:LANES, :] = norm(meta_f[...])

    whole = lambda shape: pl.BlockSpec(shape, lambda jj, b: (0,) * len(shape))
    return pl.pallas_call(
        body,
        name="prep_norm_gather",
        grid=(nb, nb_seq),
        in_specs=[
            pl.BlockSpec((1, LANES, d), lambda jj, b: (b, jnp.minimum(jj, nb - 2), 0)),
            whole(meta.shape), whole(norm_g.shape), whole(w_in_t.shape),
        ],
        out_specs=(pl.BlockSpec((LANES, d), lambda jj, b: (b * nb + (jj + 1) % nb, 0)),
                   whole((IN_PAD, d)), whole((N_META, d))),
        out_shape=(jax.ShapeDtypeStruct((nb_seq * nb * LANES, d), BF16),
                   jax.ShapeDtypeStruct((IN_PAD, d), BF16),
                   jax.ShapeDtypeStruct((N_META, d), F32)),
        scratch_shapes=[
            pltpu.VMEM((SHARD_IN_PAD, d), BF16),
            pltpu.VMEM((N_META, SHARD_META), F32),
            pltpu.VMEM((N_DEV, SHARD_IN_PAD, d), BF16),
            pltpu.VMEM((N_DEV, N_META, SHARD_META), F32),
            pltpu.SemaphoreType.DMA((14,)),
            pltpu.SemaphoreType.DMA((14,)),
            pltpu.SemaphoreType.DMA((2,)),
        ],
        compiler_params=_params("arbitrary", "arbitrary"),
    )(x, meta, norm_g, w_in_t)


def _in_proj_gather(u, w_in_p, w_q, w_kv, w_out, conv_w, bm, bn):
    m, k_dim = u.shape
    n = w_in_p.shape[0]
    steps = (m // bm) * (n // bn)
    relay_step, forward_step = steps // 3, 2 * steps // 3
    qkv_shape = (SHARD_Q + KV_RANK, Q_RANK)

    def body(a_ref, b_ref, wq_ref, wkv_ref, wout_ref, conv_ref, o_ref, w_q_p, w_kv_p, w_out_f, conv_f,
             sqkv, sout, sconv, gqkv, gout, gconv, send_sems, recv_sems, local_sems):
        t = pl.program_id(0) * (n // bn) + pl.program_id(1)

        def plan():
            return _gather_plan((sqkv, sout, sconv), (gqkv, gout, gconv), send_sems, recv_sems, local_sems)

        @pl.when(t == 0)
        def _():
            sqkv[...] = jnp.zeros_like(sqkv)
            sqkv[0:SHARD_Q, :] = wq_ref[...].astype(BF16)
            sqkv[SHARD_Q:, 0:SHARD_KV] = wkv_ref[...].astype(BF16)
            sout[...] = wout_ref[...].astype(BF16)
            sconv[...] = jnp.zeros_like(sconv)
            for i in range(3):
                sconv[i:i + 1, 0:SHARD_CONV] = conv_ref[i]
            plan()[0]()

        @pl.when(t == relay_step)
        def _():
            plan()[1]()

        @pl.when(t == forward_step)
        def _():
            plan()[2]()

        o_ref[...] = _dot(a_ref[...], b_ref[...], _NT).astype(o_ref.dtype)

        @pl.when(t == steps - 1)
        def _():
            plan()[3]()
            conv_f[...] = jnp.zeros_like(conv_f)
            for k in range(N_DEV):
                for s0, e0, d0 in _q_pieces(k):
                    w_q_p[d0:d0 + e0 - s0, :] = gqkv[k, s0:e0, :]
                w_kv_p[:, _kv_dst(k):_kv_dst(k) + SHARD_KV] = gqkv[k, SHARD_Q:, 0:SHARD_KV]
                w_out_f[SHARD_OUT * k:SHARD_OUT * (k + 1), :] = gout[k]
                conv_f[0:3, SHARD_CONV * k:SHARD_CONV * (k + 1)] = gconv[k, 0:3, 0:SHARD_CONV]

    whole = lambda shape: pl.BlockSpec(shape, lambda i, j: (0,) * len(shape))
    return pl.pallas_call(
        body,
        name="in_proj_gather",
        grid=(m // bm, n // bn),
        in_specs=[pl.BlockSpec((bm, k_dim), lambda i, j: (i, 0)), pl.BlockSpec((bn, k_dim), lambda i, j: (j, 0)),
                  whole(w_q.shape), whole(w_kv.shape), whole(w_out.shape), whole(conv_w.shape)],
        out_specs=(pl.BlockSpec((bm, bn), lambda i, j: (i, j)),
                   whole((Q_COLS, Q_RANK)), whole((KV_RANK, KV_COLS)), whole((D_MODEL, D_MODEL)),
                   whole((8, CONV_WIDTH))),
        out_shape=(jax.ShapeDtypeStruct((m, n), BF16),
                   jax.ShapeDtypeStruct((Q_COLS, Q_RANK), BF16),
                   jax.ShapeDtypeStruct((KV_RANK, KV_COLS), BF16),
                   jax.ShapeDtypeStruct((D_MODEL, D_MODEL), BF16),
                   jax.ShapeDtypeStruct((8, CONV_WIDTH), F32)),
        scratch_shapes=[
            pltpu.VMEM(qkv_shape, BF16),
            pltpu.VMEM((SHARD_OUT, D_MODEL), BF16),
            pltpu.VMEM((8, LANES), F32),
            pltpu.VMEM((N_DEV,) + qkv_shape, BF16),
            pltpu.VMEM((N_DEV, SHARD_OUT, D_MODEL), BF16),
            pltpu.VMEM((N_DEV, 8, LANES), F32),
            pltpu.SemaphoreType.DMA((21,)),
            pltpu.SemaphoreType.DMA((21,)),
            pltpu.SemaphoreType.DMA((3,)),
        ],
        compiler_params=_params("arbitrary", "arbitrary"),
    )(u, w_in_p, w_q, w_kv, w_out, conv_w)


def _rope_tables(tp):
    half = D_ROPE // 2
    inv_freq = 1.0 / (ROPE_THETA ** (jnp.arange(half, dtype=F32) / half))
    pos = (jnp.arange(tp) - PAD_FRONT).astype(F32)
    ang = pos[:, None] * inv_freq[None, :]
    cos = jnp.tile(jnp.cos(ang), (1, LANES // half))
    sin = jnp.tile(jnp.sin(ang), (1, LANES // half))
    first = (jnp.arange(LANES) % D_ROPE) < half
    return cos, jnp.where(first, -sin, 0.0), jnp.where(first, 0.0, sin)


def _rope(t, cos, sa, sb):
    return t * cos + pltpu.roll(t, LANES - D_ROPE // 2, 1) * sa + pltpu.roll(t, D_ROPE // 2, 1) * sb


def _rope_t(t, cos, sa, sb):
    return t * cos + pltpu.roll(t * sa, D_ROPE // 2, 1) + pltpu.roll(t * sb, LANES - D_ROPE // 2, 1)


def _qkv_fwd(p, wq, wkv, gq, gkv, tables, nb_seq, tp):
    ht = tp // 2

    def body(pa_ref, wq_ref, wkv_ref, gq_ref, gkv_ref, cos_ref, sa_ref, sb_ref, q_ref, k_ref, v_ref):
        pa = pa_ref[...].astype(F32)
        cq_hat, _ = _rms_stats(pa[:, :Q_RANK])
        ckv_hat, _ = _rms_stats(pa[:, Q_RANK:Q_RANK + KV_RANK])
        q = _dot((cq_hat * gq_ref[...]).astype(BF16), wq_ref[...], _NT) * Q_SCALE
        kv = _dot((ckv_hat * gkv_ref[...]).astype(BF16), wkv_ref[...])
        tabs = (cos_ref[...], sa_ref[...], sb_ref[...])
        lane = lax.broadcasted_iota(jnp.int32, (ht, LANES), 1)
        low = lane < D_ROPE
        mark = lane == D_ROPE
        row = (pl.program_id(0) % 2) * ht + lax.broadcasted_iota(jnp.int32, (ht, LANES), 0)
        k_pe = jnp.where(mark & (row < PAD_FRONT), NEG_INF, _rope(pa[:, Q_RANK + KV_RANK:], *tabs))
        one = jnp.where(mark & (row >= PAD_FRONT), 1.0, 0.0)
        pairs = [_rope(q[:, N_HEADS * D_NOPE + LANES * i:N_HEADS * D_NOPE + LANES * (i + 1)], *tabs) for i in range(2)]
        for h in range(N_HEADS):
            pair = pairs[h // 2]
            if h % 2:
                pair = pltpu.roll(pair, D_ROPE, 1)
            pe = jnp.where(low, pair, one)
            q_ref[0, h] = jnp.concatenate([q[:, D_NOPE * h:D_NOPE * (h + 1)], pe], axis=1).astype(BF16)
            k_ref[0, h] = jnp.concatenate([kv[:, D_NOPE * h:D_NOPE * (h + 1)], k_pe], axis=1).astype(BF16)
            v_ref[0, h] = kv[:, N_HEADS * D_NOPE + D_V * h:N_HEADS * D_NOPE + D_V * (h + 1)].astype(BF16)

    full = lambda a: pl.BlockSpec(a.shape, lambda i: (0,) * a.ndim)
    tab = pl.BlockSpec((ht, LANES), lambda i: (i % 2, 0))
    qk = pl.BlockSpec((1, N_HEADS, ht, 2 * LANES), lambda i: (i // 2, 0, i % 2, 0))
    return pl.pallas_call(
        body,
        name="qkv_fwd",
        grid=(2 * nb_seq,),
        in_specs=[pl.BlockSpec((ht, GRP_A), lambda i: (i, 0)), full(wq), full(wkv), full(gq), full(gkv), tab, tab, tab],
        out_specs=(qk, qk, pl.BlockSpec((1, N_HEADS, ht, D_V), lambda i: (i // 2, 0, i % 2, 0))),
        out_shape=(
            jax.ShapeDtypeStruct((nb_seq, N_HEADS, tp, 2 * LANES), BF16),
            jax.ShapeDtypeStruct((nb_seq, N_HEADS, tp, 2 * LANES), BF16),
            jax.ShapeDtypeStruct((nb_seq, N_HEADS, tp, D_V), BF16),
        ),
        compiler_params=_params("parallel"),
    )(p, wq, wkv, gq, gkv, *tables)


def _attn_fwd(q, k, v, p, g_attn):
    nb_seq, _, tp, _ = q.shape

    def body(q_ref, k_ref, v_ref, z_ref, g_ref, y_ref, o_ref, lse_ref):
        g = g_ref[...]
        for r0 in range(0, tp, Q_TILE):
            nq = min(Q_TILE, tp - r0)
            kend = r0 + nq
            qq = q_ref[0, 0, r0:kend, :]
            sd = _dot(qq, k_ref[0, 0, r0:kend, :], _NT)
            causal = (lax.broadcasted_iota(jnp.int32, (nq, nq), 1) <= lax.broadcasted_iota(jnp.int32, (nq, nq), 0))
            sd = jnp.where(causal, sd, NEG_INF)
            m = jnp.max(sd, axis=-1, keepdims=True)
            if r0:
                so = _dot(qq, k_ref[0, 0, 0:r0, :], _NT)
                m = jnp.maximum(m, jnp.max(so, axis=-1, keepdims=True))
            ed = jnp.exp2(sd - m)
            l = jnp.sum(ed, axis=-1, keepdims=True)
            o = _dot(ed.astype(BF16), v_ref[0, 0, r0:kend, :])
            if r0:
                eo = jnp.exp2(so - m)
                l = l + jnp.sum(eo, axis=-1, keepdims=True)
                o = o + _dot(eo.astype(BF16), v_ref[0, 0, 0:r0, :])
            o = o * (1.0 / l)
            o_ref[0, 0, r0:kend, :] = o
            lse_ref[0, 0, r0:kend, :] = jnp.broadcast_to(m + jnp.log2(l), (nq, LANES))
            ohat, _ = _rms_stats(o)
            z = z_ref[r0:kend, :].astype(F32)
            y_ref[r0:kend, :] = (ohat * g * (z * _sigmoid(z))).astype(BF16)

    qk = pl.BlockSpec((1, 1, tp, 2 * LANES), lambda b, h: (b, h, 0, 0))
    hv = pl.BlockSpec((1, 1, tp, D_V), lambda b, h: (b, h, 0, 0))
    return pl.pallas_call(
        body,
        name="attn_fwd",
        grid=(nb_seq, N_HEADS),
        in_specs=[qk, qk, hv,
                  pl.BlockSpec((tp, LANES), lambda b, h: (b, GRP_A // LANES + h)),
                  pl.BlockSpec((1, LANES), lambda b, h: (0, h))],
        out_specs=(pl.BlockSpec((tp, LANES), lambda b, h: (b, h)), hv, hv),
        out_shape=(
            jax.ShapeDtypeStruct((nb_seq * tp, N_HEADS * D_V), BF16),
            jax.ShapeDtypeStruct((nb_seq, N_HEADS, tp, D_V), F32),
            jax.ShapeDtypeStruct((nb_seq, N_HEADS, tp, LANES), F32),
        ),
        compiler_params=_params("parallel", "parallel"),
    )(q, k, v, p, g_attn)


_CONV_COL0 = (GRP_A + N_HEADS * D_V) // LANES


def _conv_specs(tp, order):
    cols = CONV_WIDTH // LANES
    return [pl.BlockSpec((tp, LANES), functools.partial(
        lambda a, b, off: order(a, b, off), off=_CONV_COL0 + i * cols)) for i in range(4)]


def _conv_fwd(p, conv_w, g_conv, nb_seq, tp):
    def body(b_ref, c_ref, h_ref, z_ref, w_ref, g_ref, y_ref):
        cc = c_ref[...].astype(F32) * h_ref[...].astype(F32)
        row = lax.broadcasted_iota(jnp.int32, (tp, LANES), 0)
        s1 = jnp.where(row >= 1, pltpu.roll(cc, 1, 0), 0.0)
        s2 = jnp.where(row >= 2, pltpu.roll(cc, 2, 0), 0.0)
        yc = b_ref[...].astype(F32) * (w_ref[0:1, :] * s2 + w_ref[1:2, :] * s1 + w_ref[2:3, :] * cc)
        r = lax.rsqrt(_group_mean(yc * yc) + EPS)
        z = z_ref[...].astype(F32)
        y_ref[...] = (yc * r * g_ref[...] * (z * _sigmoid(z))).astype(BF16)

    return pl.pallas_call(
        body,
        name="conv_fwd",
        grid=(nb_seq, CONV_WIDTH // LANES),
        in_specs=_conv_specs(tp, lambda b, t, off: (b, off + t)) + [
            pl.BlockSpec((8, LANES), lambda b, t: (0, t)),
            pl.BlockSpec((1, LANES), lambda b, t: (0, t))],
        out_specs=pl.BlockSpec((tp, LANES), lambda b, t: (b, t)),
        out_shape=jax.ShapeDtypeStruct((nb_seq * tp, CONV_WIDTH), BF16),
        compiler_params=_params("parallel", "parallel"),
    )(p, p, p, p, conv_w, g_conv)


def _token_copy(hbm, b, k, ts, buf, sem, to_hbm=False):
    lo, hi = max(k * ts - LANES, 0), (k + 1) * ts - LANES
    off = lo - (k * ts - LANES)
    src, dst = hbm.at[b, pl.ds(lo, hi - lo)], buf.at[pl.ds(off, hi - lo)]
    if to_hbm:
        src, dst = dst, src
    return pltpu.make_async_copy(src, dst, sem)


def _for_tile(k, nt, fn):
    for kk in range(nt):
        @pl.when(k == kk)
        def _(kk=kk):
            fn(kk)


def _out_proj_loss(ya, yc, w_out, x, target, g_final, nt):
    nb_seq, s, d = x.shape
    r, ka = ya.shape
    ts = (s + LANES) // nt
    steps = nb_seq * nt

    def body(a_ref, c_ref, w_ref, x_hbm, t_hbm, g_ref, dhb_ref, dg_ref, loss_ref,
             xbuf, tbuf, acc_ref, sems):
        i = pl.program_id(0)
        b, k = i // nt, i % nt

        @pl.when(i == 0)
        def _():
            acc_ref[...] = jnp.zeros_like(acc_ref)
            dg_ref[...] = jnp.zeros_like(dg_ref)

        slot = i % 2

        def fetch(seq, kk, sl):
            return [_token_copy(x_hbm, seq, kk, ts, xbuf.at[sl], sems.at[sl, 0]),
                    _token_copy(t_hbm, seq, kk, ts, tbuf.at[sl], sems.at[sl, 1])]

        def start(seq, sl, kk):
            if kk == 0:
                xbuf[sl, 0:LANES, :] = jnp.zeros((LANES, d), F32)
                tbuf[sl, 0:LANES, :] = jnp.zeros((LANES, d), F32)
            for cp in fetch(seq, kk, sl):
                cp.start()

        @pl.when(i == 0)
        def _():
            start(0, 0, 0)

        @pl.when(i + 1 < steps)
        def _():
            _for_tile((i + 1) % nt, nt, functools.partial(start, (i + 1) // nt, 1 - slot))

        mix = _dot(a_ref[...], w_ref[0:ka, :]) + _dot(c_ref[...], w_ref[ka:, :])
        _for_tile(k, nt, lambda kk: [cp.wait() for cp in fetch(b, kk, slot)])

        real = (lax.broadcasted_iota(jnp.int32, (ts, d), 0) >= LANES) | (k > 0)
        g = g_ref[...]
        hhat, rstd = _rms_stats(xbuf[slot] + mix)
        e = jnp.where(real, hhat * g - tbuf[slot], 0.0)
        acc_ref[...] += jnp.sum(e * e, axis=0, keepdims=True)
        dy = e * (1.0 / d)
        dg_ref[...] += jnp.sum(dy * hhat, axis=0, keepdims=True)
        dhb_ref[...] = _rms_bwd(g * dy, hhat, rstd).astype(BF16)

        @pl.when(i == steps - 1)
        def _():
            total = jnp.sum(acc_ref[...], axis=1, keepdims=True)
            loss_ref[...] = jnp.broadcast_to((0.5 / d) * total, loss_ref.shape)

    hbm = pl.BlockSpec(memory_space=pl.ANY)
    row = pl.BlockSpec((ts, d), lambda i: (i, 0))
    vec = pl.BlockSpec((1, d), lambda i: (0, 0))
    return pl.pallas_call(
        body,
        name="out_proj_loss",
        grid=(steps,),
        in_specs=[pl.BlockSpec((ts, ka), lambda i: (i, 0)), pl.BlockSpec((ts, yc.shape[1]), lambda i: (i, 0)),
                  pl.BlockSpec(w_out.shape, lambda i: (0, 0)), hbm, hbm, vec],
        out_specs=(row, vec, pl.BlockSpec((1, LANES), lambda i: (0, 0))),
        out_shape=(
            jax.ShapeDtypeStruct((r, d), BF16),
            jax.ShapeDtypeStruct((1, d), F32),
            jax.ShapeDtypeStruct((1, LANES), F32),
        ),
        scratch_shapes=[pltpu.VMEM((2, ts, d), F32), pltpu.VMEM((2, ts, d), F32), pltpu.VMEM((1, d), F32),
                        pltpu.SemaphoreType.DMA((2, 2))],
        compiler_params=_params("arbitrary"),
    )(ya, yc, w_out, x, target, g_final)


def _out_proj_bwd(dhb, w_out, ya, yc, bm):
    r, d = dhb.shape
    ka = ya.shape[1]
    n_mix = w_out.shape[0]
    last = r // bm - 1

    def body(dh_ref, w_ref, a_ref, c_ref, dcat_ref, dw_ref, acc_ref):
        @pl.when(pl.program_id(0) == 0)
        def _():
            acc_ref[...] = jnp.zeros_like(acc_ref)

        dh = dh_ref[...]
        dcat_ref[...] = _dot(dh, w_ref[...], _NT).astype(BF16)
        acc_ref[0:ka, :] += _dot(a_ref[...], dh, _TN)
        acc_ref[ka:, :] += _dot(c_ref[...], dh, _TN)

        @pl.when(pl.program_id(0) == last)
        def _():
            dw_ref[...] = acc_ref[...].astype(BF16)

    return pl.pallas_call(
        body,
        name="out_proj_bwd",
        grid=(r // bm,),
        in_specs=[pl.BlockSpec((bm, d), lambda i: (i, 0)), pl.BlockSpec(w_out.shape, lambda i: (0, 0)),
                  pl.BlockSpec((bm, ka), lambda i: (i, 0)), pl.BlockSpec((bm, yc.shape[1]), lambda i: (i, 0))],
        out_specs=(pl.BlockSpec((bm, n_mix), lambda i: (i, 0)),
                   pl.BlockSpec((n_mix, d), lambda i: (0, 0))),
        out_shape=(jax.ShapeDtypeStruct((r, n_mix), BF16),
                   jax.ShapeDtypeStruct((n_mix, d), BF16)),
        scratch_shapes=[pltpu.VMEM((n_mix, d), F32)],
        compiler_params=_params("arbitrary"),
    )(dhb, w_out, ya, yc)


def _attn_bwd(q, k, v, o, lse, dcat, p, g_attn):
    nb_seq, _, tp, _ = q.shape

    def body(q_ref, k_ref, v_ref, o_ref, lse_ref, dy_ref, z_ref, g_ref,
             dq_ref, dk_ref, dv_ref, dz_ref, dg_ref, dq_acc):
        @pl.when(pl.program_id(1) == 0)
        def _():
            dg_ref[...] = jnp.zeros_like(dg_ref)

        g = g_ref[...]
        z = z_ref[...].astype(F32)
        o = o_ref[0, 0]
        dy = dy_ref[...].astype(F32)
        sig = _sigmoid(z)
        ohat, r = _rms_stats(o)
        don = dy * (z * sig)
        dz_ref[...] = (dy * (ohat * g) * (sig * (1.0 + z * (1.0 - sig)))).astype(BF16)
        dg_ref[...] += jnp.sum(don * ohat, axis=0, keepdims=True)
        do = _rms_bwd(g * don, ohat, r)
        dvec = jnp.sum(do * o, axis=-1, keepdims=True)
        dob = do.astype(BF16)
        lse_col = lse_ref[0, 0, :, 0:1]
        dq_acc[...] = jnp.zeros_like(dq_acc)
        for k0 in range(0, tp, K_TILE):
            nk = min(K_TILE, tp - k0)
            nq = tp - k0
            qq = q_ref[0, 0, k0:, :]
            kk = k_ref[0, 0, k0:k0 + nk, :]
            causal = (lax.broadcasted_iota(jnp.int32, (nq, nk), 1) <= lax.broadcasted_iota(jnp.int32, (nq, nk), 0))
            pr = jnp.where(causal, jnp.exp2(_dot(qq, kk, _NT) - lse_col[k0:]), 0.0)
            dp = _dot(dob[k0:], v_ref[0, 0, k0:k0 + nk, :], _NT)
            ds = (pr * (dp - dvec[k0:])).astype(BF16)
            dv_ref[0, 0, k0:k0 + nk, :] = _dot(pr.astype(BF16), dob[k0:], _TN).astype(BF16)
            dk_ref[0, 0, k0:k0 + nk, :] = (_dot(ds, qq, _TN) * (ATTN_SCALE / Q_SCALE)).astype(BF16)
            dq_acc[k0:, :] += _dot(ds, kk)
        dq_ref[0, 0] = (dq_acc[...] * ATTN_SCALE).astype(BF16)

    qk = pl.BlockSpec((1, 1, tp, 2 * LANES), lambda h, b: (b, h, 0, 0))
    hv = pl.BlockSpec((1, 1, tp, D_V), lambda h, b: (b, h, 0, 0))
    col = pl.BlockSpec((tp, LANES), lambda h, b: (b, h))
    return pl.pallas_call(
        body,
        name="attn_bwd",
        grid=(N_HEADS, nb_seq),
        in_specs=[qk, qk, hv, hv, hv, col,
                  pl.BlockSpec((tp, LANES), lambda h, b: (b, GRP_A // LANES + h)),
                  pl.BlockSpec((1, LANES), lambda h, b: (0, h))],
        out_specs=(qk, qk, hv, col, pl.BlockSpec((1, LANES), lambda h, b: (0, h))),
        out_shape=(
            jax.ShapeDtypeStruct((nb_seq, N_HEADS, tp, 2 * LANES), BF16),
            jax.ShapeDtypeStruct((nb_seq, N_HEADS, tp, 2 * LANES), BF16),
            jax.ShapeDtypeStruct((nb_seq, N_HEADS, tp, D_V), BF16),
            jax.ShapeDtypeStruct((nb_seq * tp, N_HEADS * D_V), BF16),
            jax.ShapeDtypeStruct((1, N_HEADS * D_V), F32),
        ),
        scratch_shapes=[pltpu.VMEM((tp, 2 * LANES), F32)],
        compiler_params=_params("arbitrary", "arbitrary"),
    )(q, k, v, o, lse, dcat, p, g_attn)


def _qkv_bwd(p, dq, dk, dv, wq, wkv, gq, gkv, tables):
    nb_seq, _, tp, _ = dq.shape
    ht = tp // 2

    def body(pa_ref, dq_ref, dk_ref, dv_ref, wq_ref, wkv_ref, gq_ref, gkv_ref, cos_ref, sa_ref, sb_ref,
             dpa_ref, dwq_ref, dwkv_ref, dgq_ref, dgkv_ref):
        @pl.when(pl.program_id(0) == 0)
        def _():
            dwq_ref[...] = jnp.zeros_like(dwq_ref)
            dwkv_ref[...] = jnp.zeros_like(dwkv_ref)
            dgq_ref[...] = jnp.zeros_like(dgq_ref)
            dgkv_ref[...] = jnp.zeros_like(dgkv_ref)

        pa = pa_ref[...].astype(F32)
        gq, gkv = gq_ref[...], gkv_ref[...]
        cq_hat, rq = _rms_stats(pa[:, :Q_RANK])
        ckv_hat, rkv = _rms_stats(pa[:, Q_RANK:Q_RANK + KV_RANK])
        tabs = (cos_ref[...], sa_ref[...], sb_ref[...])

        pe = [dq_ref[0, h, :, D_NOPE:].astype(F32) for h in range(N_HEADS)]
        pairs = [_rope_t(pe[2 * i] + pltpu.roll(pe[2 * i + 1], D_ROPE, 1), *tabs).astype(BF16) for i in range(2)]
        dq_flat = jnp.concatenate([dq_ref[0, h, :, :D_NOPE] for h in range(N_HEADS)] + pairs, axis=1)
        dwq_ref[...] += _dot(dq_flat, (cq_hat * gq).astype(BF16), _TN)
        dcqn = _dot(dq_flat, wq_ref[...])
        dgq_ref[...] += jnp.sum(dcqn * cq_hat, axis=0, keepdims=True)
        dcq = _rms_bwd(gq * dcqn, cq_hat, rq)

        dkv_flat = jnp.concatenate([dk_ref[0, h, :, :D_NOPE] for h in range(N_HEADS)]
                                   + [dv_ref[0, h] for h in range(N_HEADS)], axis=1)
        dwkv_ref[...] += _dot((ckv_hat * gkv).astype(BF16), dkv_flat, _TN)
        dckvn = _dot(dkv_flat, wkv_ref[...], _NT)
        dgkv_ref[...] += jnp.sum(dckvn * ckv_hat, axis=0, keepdims=True)
        dckv = _rms_bwd(gkv * dckvn, ckv_hat, rkv)

        dk_pe = dk_ref[0, 0, :, D_NOPE:].astype(F32)
        for h in range(1, N_HEADS):
            dk_pe = dk_pe + dk_ref[0, h, :, D_NOPE:].astype(F32)
        dk_pe = jnp.where(lax.broadcasted_iota(jnp.int32, (ht, LANES), 1) < D_ROPE, dk_pe, 0.0)
        dpa_ref[...] = jnp.concatenate([dcq, dckv, _rope_t(dk_pe, *tabs)], axis=1).astype(BF16)

    full = lambda a: pl.BlockSpec(a.shape, lambda i: (0,) * a.ndim)
    tab = pl.BlockSpec((ht, LANES), lambda i: (i % 2, 0))
    qk = pl.BlockSpec((1, N_HEADS, ht, 2 * LANES), lambda i: (i // 2, 0, i % 2, 0))
    acc = lambda shape: pl.BlockSpec(shape, lambda i: (0, 0))
    return pl.pallas_call(
        body,
        name="qkv_bwd",
        grid=(2 * nb_seq,),
        in_specs=[pl.BlockSpec((ht, GRP_A), lambda i: (i, 0)), qk, qk,
                  pl.BlockSpec((1, N_HEADS, ht, D_V), lambda i: (i // 2, 0, i % 2, 0)),
                  full(wq), full(wkv), full(gq), full(gkv), tab, tab, tab],
        out_specs=(pl.BlockSpec((ht, GRP_A), lambda i: (i, 0)),
                   acc(wq.shape), acc(wkv.shape), acc((1, Q_RANK)), acc((1, KV_RANK))),
        out_shape=(
            jax.ShapeDtypeStruct((nb_seq * tp, GRP_A), BF16),
            jax.ShapeDtypeStruct(wq.shape, F32),
            jax.ShapeDtypeStruct(wkv.shape, F32),
            jax.ShapeDtypeStruct((1, Q_RANK), F32),
            jax.ShapeDtypeStruct((1, KV_RANK), F32),
        ),
        compiler_params=_params("arbitrary"),
    )(p, dq, dk, dv, wq, wkv, gq, gkv, *tables)


def _conv_bwd(p, dcat, conv_w, g_conv, nb_seq, tp):
    cols = CONV_WIDTH // LANES

    def body(b_ref, c_ref, h_ref, z_ref, dy_ref, w_ref, g_ref,
             db_ref, dc_ref, dh_ref, dz_ref, dw_ref, dg_ref):
        @pl.when(pl.program_id(1) == 0)
        def _():
            dw_ref[...] = jnp.zeros_like(dw_ref)
            dg_ref[...] = jnp.zeros_like(dg_ref)

        cb, c, h = b_ref[...].astype(F32), c_ref[...].astype(F32), h_ref[...].astype(F32)
        z, dy = z_ref[...].astype(F32), dy_ref[...].astype(F32)
        g = g_ref[...]
        w0, w1, w2 = w_ref[0:1, :], w_ref[1:2, :], w_ref[2:3, :]
        cc = c * h
        row = lax.broadcasted_iota(jnp.int32, (tp, LANES), 0)
        s1 = jnp.where(row >= 1, pltpu.roll(cc, 1, 0), 0.0)
        s2 = jnp.where(row >= 2, pltpu.roll(cc, 2, 0), 0.0)
        dwc = w0 * s2 + w1 * s1 + w2 * cc
        yc = cb * dwc
        r = lax.rsqrt(_group_mean(yc * yc) + EPS)
        ychat = yc * r
        sig = _sigmoid(z)
        dz_ref[...] = (dy * (ychat * g) * (sig * (1.0 + z * (1.0 - sig)))).astype(BF16)
        dyn = dy * (z * sig)
        dg_ref[...] += jnp.sum(dyn * ychat, axis=0, keepdims=True)
        gd = g * dyn
        dyc = r * (gd - ychat * _group_mean(gd * ychat))
        db_ref[...] = (dyc * dwc).astype(BF16)
        ddw = dyc * cb
        dw_ref[0:1, :] += jnp.sum(ddw * s2, axis=0, keepdims=True)
        dw_ref[1:2, :] += jnp.sum(ddw * s1, axis=0, keepdims=True)
        dw_ref[2:3, :] += jnp.sum(ddw * cc, axis=0, keepdims=True)
        u1 = jnp.where(row <= tp - 2, pltpu.roll(ddw, tp - 1, 0), 0.0)
        u2 = jnp.where(row <= tp - 3, pltpu.roll(ddw, tp - 2, 0), 0.0)
        dcc = w2 * ddw + w1 * u1 + w0 * u2
        dc_ref[...] = (dcc * h).astype(BF16)
        dh_ref[...] = (dcc * c).astype(BF16)

    col = pl.BlockSpec((tp, LANES), lambda t, b: (b, t))
    out = jax.ShapeDtypeStruct((nb_seq * tp, CONV_WIDTH), BF16)
    return pl.pallas_call(
        body,
        name="conv_bwd",
        grid=(cols, nb_seq),
        in_specs=_conv_specs(tp, lambda t, b, off: (b, off + t)) + [
            pl.BlockSpec((tp, LANES), lambda t, b: (b, N_HEADS * D_V // LANES + t)),
            pl.BlockSpec((8, LANES), lambda t, b: (0, t)),
            pl.BlockSpec((1, LANES), lambda t, b: (0, t))],
        out_specs=(col, col, col, col,
                   pl.BlockSpec((8, LANES), lambda t, b: (0, t)), pl.BlockSpec((1, LANES), lambda t, b: (0, t))),
        out_shape=(out, out, out, out,
                   jax.ShapeDtypeStruct((8, CONV_WIDTH), F32), jax.ShapeDtypeStruct((1, CONV_WIDTH), F32)),
        compiler_params=_params("arbitrary", "arbitrary"),
    )(p, p, p, p, dcat, conv_w, g_conv)


def _input_bwd(dps, w_in, x, meta, dh, norm_g, nt, send_in):
    nb_seq, s, d = x.shape
    r, kb = dps[0].shape
    ts = (s + LANES) // nt
    steps = nb_seq * nt
    n_dp = len(dps)
    in_slot = send_in.shape[1:]

    def body(*refs):
        dp_refs, w_ref, x_hbm, meta_ref, dh_ref, g_ref, pay_ref = refs[:n_dp], *refs[n_dp:n_dp + 6]
        o = n_dp + 6
        gx_hbm, dmeta_ref, dg_ref, r2_in = refs[o:o + 4]
        xbuf, gxbuf, tok_sems, relay_in, own_in, r1_in, sum_in = refs[o + 4:o + 11]
        sems = refs[o + 11:]
        i = pl.program_id(0)
        b, k = i // nt, i % nt

        def plan():
            return _reduce_plan((pay_ref,), (own_in,), (r1_in,), (sum_in,), (r2_in,), *sems, relays=(relay_in,))

        @pl.when(i == 0)
        def _():
            dmeta_ref[...] = jnp.zeros_like(dmeta_ref)
            dg_ref[...] = jnp.zeros_like(dg_ref)
            plan()[0]()

        @pl.when(i == 1)
        def _():
            plan()[1]()

        @pl.when(i == steps // 2)
        def _():
            plan()[2]()

        def start(kk):
            if kk == 0:
                xbuf[0:PAD_FRONT, :] = jnp.zeros((PAD_FRONT, d), F32)
                xbuf[PAD_FRONT:LANES, :] = meta_ref[...]
            _token_copy(x_hbm, b, kk, ts, xbuf, tok_sems.at[0]).start()

        _for_tile(k, nt, start)
        du = _dot(dp_refs[0][...], w_ref[0:kb, :])
        for j in range(1, n_dp):
            du = du + _dot(dp_refs[j][...], w_ref[kb * j:kb * (j + 1), :])
        _for_tile(k, nt, lambda kk: _token_copy(x_hbm, b, kk, ts, xbuf, tok_sems.at[0]).wait())

        g = g_ref[...]
        hhat, rstd = _rms_stats(xbuf[...])
        dg_ref[...] += jnp.sum(du * hhat, axis=0, keepdims=True)
        res = _rms_bwd(g * du, hhat, rstd) + dh_ref[...].astype(F32)

        @pl.when(i > 0)
        def _():
            _for_tile(k, nt, lambda kk: _token_copy(gx_hbm, b, (kk - 1) % nt, ts, gxbuf, tok_sems.at[1], True).wait())

        gxbuf[...] = res

        @pl.when(k == 0)
        def _():
            dmeta_ref[...] += gxbuf[PAD_FRONT:LANES, :]

        _for_tile(k, nt, lambda kk: _token_copy(gx_hbm, b, kk, ts, gxbuf, tok_sems.at[1], True).start())

        @pl.when(i == steps - 1)
        def _():
            _token_copy(gx_hbm, b, nt - 1, ts, gxbuf, tok_sems.at[1], True).wait()
            plan()[3]()

    whole = lambda a: pl.BlockSpec(a.shape, lambda i: (0,) * a.ndim)
    hbm = pl.BlockSpec(memory_space=pl.ANY)
    return pl.pallas_call(
        body,
        name="input_bwd",
        grid=(steps,),
        in_specs=[pl.BlockSpec((ts, kb), lambda i: (i, 0)) for _ in dps]
        + [whole(w_in), hbm, whole(meta), pl.BlockSpec((ts, d), lambda i: (i, 0)), whole(norm_g), hbm],
        out_specs=(hbm, pl.BlockSpec((N_META, d), lambda i: (0, 0)), pl.BlockSpec((1, d), lambda i: (0, 0)), hbm),
        out_shape=(jax.ShapeDtypeStruct((nb_seq, s, d), F32),
                   jax.ShapeDtypeStruct((N_META, d), F32),
                   jax.ShapeDtypeStruct((1, d), F32),
                   jax.ShapeDtypeStruct((3,) + in_slot, BF16)),
        scratch_shapes=[pltpu.VMEM((ts, d), F32), pltpu.VMEM((ts, d), F32), pltpu.SemaphoreType.DMA((2,)),
                        pltpu.VMEM((2,) + in_slot, BF16)]
        + _reduce_scratch([(in_slot, BF16)], [True]),
        compiler_params=_params("arbitrary"),
    )(*dps, w_in, x, meta, dh, norm_g, send_in)


def _in_proj_bwd_w(u, dps, bm, small_grads, send_out):
    r, d = u.shape
    kb = dps[0].shape[1]
    steps = r // bm
    n_dp, n_small = len(dps), len(small_grads)
    out_slot, small_slot = send_out.shape[1:], (SMALL_ROWS, LANES)

    def body(*refs):
        u_ref, dp_refs = refs[0], refs[1:1 + n_dp]
        small_refs = refs[1 + n_dp:1 + n_dp + n_small]
        o = 1 + n_dp + n_small
        pay_out, o_ref, r2_out, r2_small = refs[o:o + 4]
        acc_ref, ssmall, r1_out, sum_out, r1_small, sum_small = refs[o + 4:o + 10]
        sems = refs[o + 10:]
        i = pl.program_id(0)

        def plan():
            return _reduce_plan((pay_out, ssmall), (None, None), (r1_out, r1_small), (sum_out, sum_small),
                                (r2_out, r2_small), *sems)

        @pl.when(i == 0)
        def _():
            acc_ref[...] = jnp.zeros_like(acc_ref)
            _pack_small(ssmall, *small_refs)
            plan()[0]()

        @pl.when(i == 1)
        def _():
            plan()[1]()

        uu = u_ref[...]
        for j in range(n_dp):
            acc_ref[kb * j:kb * (j + 1), :] += _dot(dp_refs[j][...], uu, _TN)

        @pl.when(i == steps - 1)
        def _():
            for k in range(N_DEV):
                for s, e, c0 in _in_pieces(k):
                    o_ref[k, s:e, :] = acc_ref[c0:c0 + e - s, :].astype(BF16)
                o_ref[k, SHARD_IN:, :] = jnp.zeros((SHARD_IN_PAD - SHARD_IN, d), BF16)
            plan()[2]()

    whole = lambda a: pl.BlockSpec(a.shape, lambda i: (0,) * a.ndim)
    hbm = pl.BlockSpec(memory_space=pl.ANY)
    return pl.pallas_call(
        body,
        name="in_proj_bwd_w",
        grid=(steps,),
        in_specs=[pl.BlockSpec((bm, d), lambda i: (i, 0))]
        + [pl.BlockSpec((bm, kb), lambda i: (i, 0)) for _ in dps] + [whole(a) for a in small_grads]
        + [whole(send_out)],
        out_specs=(pl.BlockSpec((N_DEV, SHARD_IN_PAD, d), lambda i: (0, 0, 0)), hbm, hbm),
        out_shape=(jax.ShapeDtypeStruct((N_DEV, SHARD_IN_PAD, d), BF16),
                   jax.ShapeDtypeStruct((N_CHIPS,) + out_slot, BF16),
                   jax.ShapeDtypeStruct((N_CHIPS,) + small_slot, F32)),
        scratch_shapes=[pltpu.VMEM((kb * n_dp, d), F32), pltpu.VMEM((N_DEV,) + small_slot, F32)]
        + _reduce_scratch([(out_slot, BF16), (small_slot, F32)], [False, False]),
        compiler_params=_params("arbitrary"),
    )(u, *dps, *small_grads, send_out)


def _local_step(x, loss_target, u, p, meta_f, norm_g, w_in_p, q_norm_g, w_q_p, kv_norm_g, w_kv_p, conv_w_f,
                attn_out_g, conv_out_g, w_out_f, g_final):
    nb_seq, s, d = x.shape
    tp = s + LANES
    ht = tp // 2
    tables = _rope_tables(tp)

    q, k, v = _qkv_fwd(p, w_q_p, w_kv_p, q_norm_g, kv_norm_g, tables, nb_seq, tp)
    ya, o, lse = _attn_fwd(q, k, v, p, attn_out_g)
    yc = _conv_fwd(p, conv_w_f, conv_out_g, nb_seq, tp)
    dhb, d_final_g, loss_part = _out_proj_loss(ya, yc, w_out_f, x, loss_target, g_final, TOKEN_TILES)

    dcat, d_w_out = _out_proj_bwd(dhb, w_out_f, ya, yc, ht)
    send_out = d_w_out.reshape(N_DEV, SHARD_OUT, d)
    dq, dk, dv, dz_attn, d_attn_g = _attn_bwd(q, k, v, o, lse, dcat, p, attn_out_g)
    dpa, d_wq_p, d_wkv_p, d_gq, d_gkv = _qkv_bwd(p, dq, dk, dv, w_q_p, w_kv_p, q_norm_g, kv_norm_g, tables)
    d_b, d_c, d_h, dz_conv, d_conv_w, d_conv_g = _conv_bwd(p, dcat, conv_w_f, conv_out_g, nb_seq, tp)
    dps = (dpa, dz_attn, d_b, d_c, d_h, dz_conv)
    small = (d_wq_p, d_wkv_p, d_conv_w, d_final_g, d_gq, d_gkv, d_attn_g, d_conv_g, loss_part)
    send_in, r_out, r_small = _in_proj_bwd_w(u, dps, ht, small, send_out)
    grad_x, d_meta, d_norm_g, r_in = _input_bwd(dps, w_in_p, x, meta_f, dhb, norm_g, TOKEN_TILES, send_in)
    return grad_x, r_in, r_out, r_small, d_meta, d_norm_g


def kernel(x, meta_tokens, norm_g, w_in, q_norm_g, w_q_up, kv_norm_g, w_kv_up, conv_w, attn_out_g, conv_out_g, w_out, final_norm_g, loss_target, m_meta_tokens, m_norm_g, m_w_in, m_q_norm_g, m_w_q_up, m_kv_norm_g, m_w_kv_up, m_conv_w, m_attn_out_g, m_conv_out_g, m_w_out, m_final_norm_g, v_meta_tokens, v_norm_g, v_w_in, v_q_norm_g, v_w_q_up, v_kv_norm_g, v_w_kv_up, v_conv_w, v_attn_out_g, v_conv_out_g, v_w_out, v_final_norm_g):
    d = x.shape[-1]
    ht = (x.shape[1] + LANES) // 2
    u, w_in_p, meta_f = _prep_gather(x, meta_tokens, norm_g, w_in[0].T)
    p, w_q_p, w_kv_p, w_out_f, conv_w_f = _in_proj_gather(
        u, w_in_p, w_q_up[0].T, w_kv_up[0], w_out[0], conv_w.transpose(1, 0, 2), ht, IN_PAD)
    g_final = final_norm_g.reshape(1, d)
    grad_x, r_in, r_out, r_small, d_meta, d_norm_g = _local_step(
        x, loss_target, u, p, meta_f, norm_g, w_in_p, q_norm_g, w_q_p, kv_norm_g, w_kv_p, conv_w_f,
        attn_out_g, conv_out_g, w_out_f, g_final)

    flat = lambda a: a.reshape(a.shape[-2:]) if a.ndim == 3 else a.reshape(1, -1) if a.ndim == 1 else a
    transposed = ("w_in", "w_q_up")

    def to_kernel(n, a):
        if n == "conv_w":
            return a.transpose(1, 0, 2)
        return flat(a).T if n in transposed else flat(a)

    def from_kernel(n, a, shape):
        if n == "conv_w":
            return a.transpose(1, 0, 2)
        return (a.T if n in transposed else a).reshape(shape)
    params = {
        "meta_tokens": (meta_tokens, m_meta_tokens, v_meta_tokens),
        "norm_g": (norm_g, m_norm_g, v_norm_g),
        "w_in": (w_in, m_w_in, v_w_in),
        "q_norm_g": (q_norm_g, m_q_norm_g, v_q_norm_g),
        "w_q_up": (w_q_up, m_w_q_up, v_w_q_up),
        "kv_norm_g": (kv_norm_g, m_kv_norm_g, v_kv_norm_g),
        "w_kv_up": (w_kv_up, m_w_kv_up, v_w_kv_up),
        "conv_w": (conv_w, m_conv_w, v_conv_w),
        "attn_out_g": (attn_out_g, m_attn_out_g, v_attn_out_g),
        "conv_out_g": (conv_out_g, m_conv_out_g, v_conv_out_g),
        "w_out": (w_out, m_w_out, v_w_out),
        "final_norm_g": (final_norm_g, m_final_norm_g, v_final_norm_g),
    }
    grads, loss = _reduce_tail(r_in, r_out, r_small, d_meta, d_norm_g)
    updated = _adamw(grads, {n: tuple(to_kernel(n, a) for a in t) for n, t in params.items()})
    outs = [[from_kernel(n, updated[n][i], params[n][0].shape) for n, _ in PARAM_SHAPES] for i in range(4)]
    return (loss[0, 0], grad_x, *outs[0], *outs[1], *outs[2], *outs[3])
```

```python
import functools

import jax
import jax.numpy as jnp
import numpy as np
from jax import lax
from jax.experimental import pallas as pl
from jax.experimental.pallas import tpu as pltpu

F32 = jnp.float32
BF16 = jnp.bfloat16

N_META = 16
D_MODEL = 1024
N_HEADS = 4
D_NOPE = 128
D_ROPE = 64
D_V = 128
Q_RANK = 256
KV_RANK = 128
CONV_WIDTH = 512
CONV_GROUP = 64
ROPE_THETA = 10000.0
ATTN_SCALE = (D_NOPE + D_ROPE) ** -0.5
Q_SCALE = ATTN_SCALE * 1.4426950408889634
EPS = 1e-6
NEG_INF = -1e30

ADAM_LR = 0.001
ADAM_B1 = 0.9
ADAM_B2 = 0.999
ADAM_EPS = 1e-08
ADAM_WD = 0.01
ADAM_STEP = 10

LANES = 128
PAD_FRONT = LANES - N_META
K_TILE = 256
Q_TILE = 512
N_DEV = 8
VMEM_LIMIT = 56 * 1024 * 1024

IN_PAD = 3072
GRP_A = 512
N_A = Q_RANK + KV_RANK + D_ROPE
IN_PROJ = 3008
SHARD_IN = IN_PROJ // N_DEV
SHARD_IN_PAD = 384
SHARD_Q = 96
SHARD_KV = 128
SHARD_OUT = 128
SHARD_CONV = 64
SHARD_META = 128
Q_COLS = N_HEADS * (D_NOPE + D_ROPE)
KV_COLS = N_HEADS * (D_NOPE + D_V)

ROW_Q, ROW_KV, ROW_META, ROW_CONV = 0, 256, 384, 400
ROW_REPL = 408
ROW_NORM, ROW_FINAL, ROW_GQ, ROW_GKV, ROW_ATTN, ROW_CONVG, ROW_LOSS = 408, 416, 424, 426, 427, 431, 435
SMALL_ROWS = 440

PARAM_SHAPES = (
    ("meta_tokens", (N_META, SHARD_META)), ("norm_g", (1, D_MODEL)), ("w_in", (SHARD_IN, D_MODEL)),
    ("q_norm_g", (1, Q_RANK)), ("w_q_up", (SHARD_Q, Q_RANK)), ("kv_norm_g", (1, KV_RANK)),
    ("w_kv_up", (KV_RANK, SHARD_KV)), ("conv_w", (3, 1, SHARD_CONV)), ("attn_out_g", (1, CONV_WIDTH)),
    ("conv_out_g", (1, CONV_WIDTH)), ("w_out", (SHARD_OUT, D_MODEL)), ("final_norm_g", (1, D_MODEL)),
)


def _in_pieces(k):
    lo, hi = SHARD_IN * k, SHARD_IN * (k + 1)
    out = []
    if lo < N_A:
        out.append((0, min(hi, N_A) - lo, lo))
    if hi > N_A:
        s = max(lo, N_A)
        out.append((s - lo, hi - lo, s + GRP_A - N_A))
    return out


def _q_pieces(k):
    lo, hi = SHARD_Q * k, SHARD_Q * (k + 1)
    out = []
    for h in range(N_HEADS):
        base = (D_NOPE + D_ROPE) * h
        s, e = max(lo, base), min(hi, base + D_NOPE)
        if s < e:
            out.append((s - lo, e - lo, D_NOPE * h + s - base))
        s, e = max(lo, base + D_NOPE), min(hi, base + D_NOPE + D_ROPE)
        if s < e:
            out.append((s - lo, e - lo, N_HEADS * D_NOPE + D_ROPE * h + s - base - D_NOPE))
    return out


def _kv_dst(k):
    return D_NOPE * (k // 2) + (N_HEADS * D_NOPE if k % 2 else 0)


def _params(*sem):
    return pltpu.CompilerParams(dimension_semantics=sem, vmem_limit_bytes=VMEM_LIMIT)


def _rms_stats(x):
    r = lax.rsqrt(jnp.mean(x * x, axis=-1, keepdims=True) + EPS)
    return x * r, r


def _rms_bwd(gdy, xhat, r):
    return r * (gdy - xhat * jnp.mean(gdy * xhat, axis=-1, keepdims=True))


def _sigmoid(z):
    return 1.0 / (1.0 + jnp.exp(-z))


def _group_mean(x):
    i0 = lax.broadcasted_iota(jnp.int32, (LANES, LANES), 0) // CONV_GROUP
    i1 = lax.broadcasted_iota(jnp.int32, (LANES, LANES), 1) // CONV_GROUP
    m = jnp.where(i0 == i1, 1.0 / CONV_GROUP, 0.0).astype(BF16)
    hi = x.astype(BF16)
    lo = (x - hi.astype(F32)).astype(BF16)
    return jnp.dot(hi, m, preferred_element_type=F32) + jnp.dot(lo, m, preferred_element_type=F32)


_NT = (((1,), (1,)), ((), ()))
_TN = (((0,), (0,)), ((), ()))


def _dot(a, b, dims=None):
    if dims is None:
        return jnp.dot(a, b, preferred_element_type=F32)
    return lax.dot_general(a, b, dims, preferred_element_type=F32)


def _device_position():
    x, y, c = lax.axis_index("x"), lax.axis_index("y"), lax.axis_index("c")
    return x, y, c, 4 * x + 2 * y + c


def _gather_plan(srcs, slots, send_sems, recv_sems, local_sems):
    x, y, c, _ = _device_position()
    me, sibling = (x, y, c), (x, y, 1 - c)
    flip = lambda v, on: v + on - 2 * v * on
    near = (flip(x, 1 - c), flip(y, c))
    far = (flip(x, c), flip(y, 1 - c))
    diag = (1 - x, 1 - y)
    n = len(srcs)

    def slot(a, px, py, pc):
        return slots[a].at[4 * px + 2 * py + pc]

    def copy(a, k, block, to, own=False):
        return pltpu.make_async_remote_copy(
            src_ref=srcs[a] if own else slot(a, *block),
            dst_ref=slot(a, *block),
            send_sem=send_sems.at[7 * a + k],
            recv_sem=recv_sems.at[7 * a + k],
            device_id=to,
            device_id_type=pl.DeviceIdType.MESH,
        )

    def local(a):
        return pltpu.make_async_copy(srcs[a], slot(a, *me), local_sems.at[a])

    sent = [(me, sibling), (me, (*near, c)), (me, (*far, c)), ((*near, c), (*far, c)),
            ((*near, c), sibling), ((*far, c), sibling), ((*diag, c), sibling)]
    landed = [sibling, (*near, c), (*far, c), (*diag, c), (*far, 1 - c), (*near, 1 - c), (*diag, 1 - c)]

    def send(a, k):
        return copy(a, k, *sent[k], own=k < 3)

    def arrival(a, k):
        return copy(a, k, landed[k], me)

    def start():
        for a in range(n):
            local(a).start()
            for k in range(3):
                send(a, k).start()

    def relay():
        for a in range(n):
            arrival(a, 1).wait_recv()
            send(a, 3).start()
            send(a, 4).start()

    def forward():
        for k in (2, 3):
            for a in range(n):
                arrival(a, k).wait_recv()
                send(a, k + 3).start()

    def finish():
        for a in range(n):
            for k in (0, 4, 5, 6):
                arrival(a, k).wait_recv()
        for a in range(n):
            for k in range(7):
                send(a, k).wait_send()
            local(a).wait()

    return start, relay, forward, finish


def _adam_update(g, w, m, v):
    m_new = ADAM_B1 * m + (1.0 - ADAM_B1) * g
    v_new = ADAM_B2 * v + (1.0 - ADAM_B2) * (g * g)
    m_hat = m_new / (1.0 - ADAM_B1 ** ADAM_STEP)
    v_hat = v_new / (1.0 - ADAM_B2 ** ADAM_STEP)
    return -ADAM_LR * (m_hat / (jnp.sqrt(v_hat) + ADAM_EPS) + ADAM_WD * w), m_new, v_new


def _adamw(grads, params):
    names = [n for n, _ in PARAM_SHAPES]
    n_p = len(names)

    def body(*refs):
        for i in range(n_p):
            g = refs[i][...]
            w, m, v = (refs[n_p + 3 * i + j][...] for j in range(3))
            delta, m_new, v_new = _adam_update(g, w, m, v)
            for j, val in enumerate((g, delta, m_new, v_new)):
                refs[4 * n_p + 4 * i + j][...] = val

    vm = pl.BlockSpec(memory_space=pltpu.VMEM)
    out_shape = []
    for _, shape in PARAM_SHAPES:
        out_shape += [jax.ShapeDtypeStruct(shape, F32)] * 4
    outs = pl.pallas_call(
        body,
        name="adamw",
        out_shape=tuple(out_shape),
        in_specs=[vm] * (4 * n_p),
        out_specs=(vm,) * (4 * n_p),
        compiler_params=pltpu.CompilerParams(vmem_limit_bytes=VMEM_LIMIT),
    )(*[grads[n] for n in names], *[a for n in names for a in params[n]])
    return {n: outs[4 * i:4 * i + 4] for i, n in enumerate(names)}


N_CHIPS = 4


def _reduce_plan(pays, owns, r1s, sums, r2s, send1, recv1, send2, recv2, local_sems):
    x, y, c, _ = _device_position()
    sibling = (x, y, 1 - c)
    chips = [((1 - x if rj & 2 else x), (1 - y if rj & 1 else y)) for rj in range(N_CHIPS)]
    n = len(pays)

    def slot_of(rj, core):
        return 4 * chips[rj][0] + 2 * chips[rj][1] + core

    def to_sibling(a, rj):
        return pltpu.make_async_remote_copy(
            src_ref=pays[a].at[slot_of(rj, 1 - c)], dst_ref=r1s[a].at[rj],
            send_sem=send1.at[N_CHIPS * a + rj], recv_sem=recv1.at[N_CHIPS * a + rj],
            device_id=sibling, device_id_type=pl.DeviceIdType.MESH)

    def load_own(a, rj):
        return pltpu.make_async_copy(pays[a].at[slot_of(rj, c)], owns[a].at[rj], local_sems.at[2 * N_CHIPS * a + rj])

    def to_chip(a, rj):
        return pltpu.make_async_remote_copy(
            src_ref=sums[a].at[rj], dst_ref=r2s[a].at[rj],
            send_sem=send2.at[N_CHIPS * a + rj], recv_sem=recv2.at[N_CHIPS * a + rj],
            device_id=(*chips[rj], c), device_id_type=pl.DeviceIdType.MESH)

    def keep(a):
        return pltpu.make_async_copy(sums[a].at[0], r2s[a].at[0], local_sems.at[2 * N_CHIPS * a + N_CHIPS])

    def start():
        for a in range(n):
            for rj in range(N_CHIPS):
                to_sibling(a, rj).start()
                if owns[a] is not None:
                    load_own(a, rj).start()

    def combine():
        for a in range(n):
            for rj in range(N_CHIPS):
                to_sibling(a, rj).wait_recv()
                if owns[a] is not None:
                    load_own(a, rj).wait()
                    mine = owns[a][rj]
                else:
                    mine = pays[a][slot_of(rj, c)]
                sums[a][rj] = (mine.astype(F32) + r1s[a][rj].astype(F32)).astype(sums[a].dtype)
            keep(a).start()
            for rj in range(1, N_CHIPS):
                to_chip(a, rj).start()

    def finish():
        for a in range(n):
            for rj in range(1, N_CHIPS):
                to_chip(a, rj).wait_recv()
            for rj in range(N_CHIPS):
                to_sibling(a, rj).wait_send()
            for rj in range(1, N_CHIPS):
                to_chip(a, rj).wait_send()
            keep(a).wait()

    return start, combine, finish


def _reduce_scratch(shapes_dtypes, own_flags):
    out = []
    for (shape, dtype), own in zip(shapes_dtypes, own_flags):
        if own:
            out.append(pltpu.VMEM((N_CHIPS,) + shape, dtype))
        out += [pltpu.VMEM((N_CHIPS,) + shape, dtype), pltpu.VMEM((N_CHIPS,) + shape, dtype)]
    n = len(shapes_dtypes)
    out += [pltpu.SemaphoreType.DMA((N_CHIPS * n,))] * 4 + [pltpu.SemaphoreType.DMA((2 * N_CHIPS * n,))]
    return out


def _pack_small(ssmall, dwq, dwkv, dconv, dfinal, dgq, dgkv, dattn, dconvg, loss_part):
    ssmall[...] = jnp.zeros_like(ssmall)
    rep = ssmall.at[0]
    for i in range(D_MODEL // LANES):
        rep[ROW_FINAL + i:ROW_FINAL + i + 1, :] = dfinal[:, LANES * i:LANES * (i + 1)]
    for i in range(Q_RANK // LANES):
        rep[ROW_GQ + i:ROW_GQ + i + 1, :] = dgq[:, LANES * i:LANES * (i + 1)]
    rep[ROW_GKV:ROW_GKV + 1, :] = dgkv[...]
    for i in range(CONV_WIDTH // LANES):
        rep[ROW_ATTN + i:ROW_ATTN + i + 1, :] = dattn[:, LANES * i:LANES * (i + 1)]
        rep[ROW_CONVG + i:ROW_CONVG + i + 1, :] = dconvg[:, LANES * i:LANES * (i + 1)]
    rep[ROW_LOSS:ROW_LOSS + 1, :] = loss_part[...]
    for k in range(N_DEV):
        if k:
            ssmall[k, ROW_REPL:, :] = ssmall[0, ROW_REPL:, :]
        for s, e, d in _q_pieces(k):
            for i in range(Q_RANK // LANES):
                ssmall[k, ROW_Q + SHARD_Q * i + s:ROW_Q + SHARD_Q * i + e, :] = dwq[d:d + e - s, LANES * i:LANES * (i + 1)]
        ssmall[k, ROW_KV:ROW_KV + KV_RANK, :] = dwkv[:, _kv_dst(k):_kv_dst(k) + SHARD_KV]
        ssmall[k, ROW_CONV:ROW_CONV + 3, 0:SHARD_CONV] = dconv[0:3, SHARD_CONV * k:SHARD_CONV * (k + 1)]


TOKEN_TILES = 4
TAIL_ROWS = N_META + D_MODEL // LANES


def _reduce_tail(r_in, r_out, r_small, d_meta, d_norm):
    n_p = len(PARAM_SHAPES)
    names = [n for n, _ in PARAM_SHAPES]

    def body(*refs):
        rin, rout, rsmall, dmeta, dnorm = refs[:5]
        g_out = {n: refs[5 + i] for i, n in enumerate(names)}
        loss_out = refs[5 + n_p]
        stail, rtail, gsum, gtail, send_sems, recv_sems = refs[6 + n_p:]
        x, y, c, me = _device_position()
        my_chip = 2 * x + y

        for k in range(N_DEV):
            stail[k, 0:N_META, :] = dmeta[:, SHARD_META * k:SHARD_META * (k + 1)]
            for i in range(D_MODEL // LANES):
                stail[k, N_META + i:N_META + i + 1, :] = dnorm[:, LANES * i:LANES * (i + 1)]
        copies = []
        for r in range(1, N_DEV):
            peer = (1 - x if r & 4 else x, 1 - y if r & 2 else y, 1 - c if r & 1 else c)
            copies.append(pltpu.make_async_remote_copy(
                src_ref=stail.at[4 * peer[0] + 2 * peer[1] + peer[2]],
                dst_ref=rtail.at[r],
                send_sem=send_sems.at[r - 1],
                recv_sem=recv_sems.at[r - 1],
                device_id=peer,
                device_id_type=pl.DeviceIdType.MESH,
            ))
        for cp in copies:
            cp.start()
        rtail[0] = stail[me]

        g = rin[my_chip].astype(F32)
        for ch in range(1, N_CHIPS):
            g = g + rin[ch ^ my_chip].astype(F32)
        g_out["w_in"][...] = g[:SHARD_IN, :]

        g = rout[my_chip].astype(F32)
        gs = rsmall[my_chip]
        for ch in range(1, N_CHIPS):
            g = g + rout[ch ^ my_chip].astype(F32)
            gs = gs + rsmall[ch ^ my_chip]
        g_out["w_out"][...] = g
        gsum[...] = gs
        for i in range(Q_RANK // LANES):
            g_out["w_q_up"][:, LANES * i:LANES * (i + 1)] = gsum[ROW_Q + SHARD_Q * i:ROW_Q + SHARD_Q * (i + 1), :]
        g_out["w_kv_up"][...] = gsum[ROW_KV:ROW_KV + KV_RANK, :]
        for i in range(3):
            g_out["conv_w"][i] = gsum[ROW_CONV + i:ROW_CONV + i + 1, 0:SHARD_CONV]
        for name, row, width in (("final_norm_g", ROW_FINAL, D_MODEL), ("q_norm_g", ROW_GQ, Q_RANK),
                                 ("kv_norm_g", ROW_GKV, KV_RANK), ("attn_out_g", ROW_ATTN, CONV_WIDTH),
                                 ("conv_out_g", ROW_CONVG, CONV_WIDTH)):
            for i in range(width // LANES):
                g_out[name][:, LANES * i:LANES * (i + 1)] = gsum[row + i:row + i + 1, :]
        loss_out[...] = gsum[ROW_LOSS:ROW_LOSS + 1, :]

        for cp in copies:
            cp.wait_recv()
        gt = rtail[me]
        for d in range(1, N_DEV):
            gt = gt + rtail[d ^ me]
        gtail[...] = gt
        g_out["meta_tokens"][...] = gtail[0:N_META, :]
        for i in range(D_MODEL // LANES):
            g_out["norm_g"][:, LANES * i:LANES * (i + 1)] = gtail[N_META + i:N_META + i + 1, :]
        for cp in copies:
            cp.wait_send()

    vm = pl.BlockSpec(memory_space=pltpu.VMEM)
    out_shape = [jax.ShapeDtypeStruct(shape, F32) for _, shape in PARAM_SHAPES]
    out_shape.append(jax.ShapeDtypeStruct((1, LANES), F32))
    outs = pl.pallas_call(
        body,
        name="reduce_tail",
        out_shape=tuple(out_shape),
        in_specs=[vm] * 5,
        out_specs=(vm,) * len(out_shape),
        scratch_shapes=[
            pltpu.VMEM((N_DEV, TAIL_ROWS, LANES), F32),
            pltpu.VMEM((N_DEV, TAIL_ROWS, LANES), F32),
            pltpu.VMEM((SMALL_ROWS, LANES), F32),
            pltpu.VMEM((TAIL_ROWS, LANES), F32),
            pltpu.SemaphoreType.DMA((N_DEV - 1,)),
            pltpu.SemaphoreType.DMA((N_DEV - 1,)),
        ],
        compiler_params=pltpu.CompilerParams(vmem_limit_bytes=VMEM_LIMIT),
    )(r_in, r_out, r_small, d_meta, d_norm)
    return {n: outs[i] for i, n in enumerate(names)}, outs[-1]


def _prep_gather(x, meta, norm_g, w_in_t):
    nb_seq, s, d = x.shape
    nb = s // LANES + 1
    relay_step = nb_seq * (nb - 1) // 2
    forward_step = nb_seq * (nb - 1) - 1
    finish_step = nb_seq * (nb - 1)

    def body(x_ref, meta_ref, g_ref, win_ref, u_ref, w_in_p, meta_f,
             sbig, ssmall, gbig, gsmall, send_sems, recv_sems, local_sems):
        jj, b = pl.program_id(0), pl.program_id(1)
        t = jj * nb_seq + b

        def plan():
            return _gather_plan((sbig, ssmall), (gbig, gsmall), send_sems, recv_sems, local_sems)

        @pl.when(t == 0)
        def _():
            sbig[0:SHARD_IN, :] = win_ref[...].astype(BF16)
            sbig[SHARD_IN:, :] = jnp.zeros((SHARD_IN_PAD - SHARD_IN, d), BF16)
            ssmall[...] = meta_ref[...]
            plan()[0]()

        @pl.when(t == relay_step)
        def _():
            plan()[1]()

        @pl.when(t == forward_step)
        def _():
            plan()[2]()

        @pl.when(t == finish_step)
        def _():
            plan()[3]()
            w_in_p[N_A:GRP_A, :] = jnp.zeros((GRP_A - N_A, d), BF16)
            for k in range(N_DEV):
                for s0, e0, d0 in _in_pieces(k):
                    w_in_p[d0:d0 + e0 - s0, :] = gbig[k, s0:e0, :]
                meta_f[:, SHARD_META * k:SHARD_META * (k + 1)] = gsmall[k]

        def norm(h):
            hhat, _ = _rms_stats(h)
            return (hhat * g_ref[...]).astype(BF16)

        @pl.when(jj < nb - 1)
        def _():
            u_ref[...] = norm(x_ref[0])

        @pl.when(jj == nb - 1)
        def _():
            u_ref[0:PAD_FRONT, :] = jnp.zeros((PAD_FRONT, d), BF16)
            u_ref[PAD_FRONT:LANES, :] = norm(meta_f[...])

    whole = lambda shape: pl.BlockSpec(shape, lambda jj, b: (0,) * len(shape))
    return pl.pallas_call(
        body,
        name="prep_norm_gather",
        grid=(nb, nb_seq),
        in_specs=[
            pl.BlockSpec((1, LANES, d), lambda jj, b: (b, jnp.minimum(jj, nb - 2), 0)),
            whole(meta.shape), whole(norm_g.shape), whole(w_in_t.shape),
        ],
        out_specs=(pl.BlockSpec((LANES, d), lambda jj, b: (b * nb + (jj + 1) % nb, 0)),
                   whole((IN_PAD, d)), whole((N_META, d))),
        out_shape=(jax.ShapeDtypeStruct((nb_seq * nb * LANES, d), BF16),
                   jax.ShapeDtypeStruct((IN_PAD, d), BF16),
                   jax.ShapeDtypeStruct((N_META, d), F32)),
        scratch_shapes=[
            pltpu.VMEM((SHARD_IN_PAD, d), BF16),
            pltpu.VMEM((N_META, SHARD_META), F32),
            pltpu.VMEM((N_DEV, SHARD_IN_PAD, d), BF16),
            pltpu.VMEM((N_DEV, N_META, SHARD_META), F32),
            pltpu.SemaphoreType.DMA((14,)),
            pltpu.SemaphoreType.DMA((14,)),
            pltpu.SemaphoreType.DMA((2,)),
        ],
        compiler_params=_params("arbitrary", "arbitrary"),
    )(x, meta, norm_g, w_in_t)


def _in_proj_gather(u, w_in_p, w_q, w_kv, w_out, conv_w, bm, bn):
    m, k_dim = u.shape
    n = w_in_p.shape[0]
    steps = (m // bm) * (n // bn)
    relay_step, forward_step = steps // 3, 2 * steps // 3
    qkv_shape = (SHARD_Q + KV_RANK, Q_RANK)

    def body(a_ref, b_ref, wq_ref, wkv_ref, wout_ref, conv_ref, o_ref, w_q_p, w_kv_p, w_out_f, conv_f,
             sqkv, sout, sconv, gqkv, gout, gconv, send_sems, recv_sems, local_sems):
        t = pl.program_id(0) * (n // bn) + pl.program_id(1)

        def plan():
            return _gather_plan((sqkv, sout, sconv), (gqkv, gout, gconv), send_sems, recv_sems, local_sems)

        @pl.when(t == 0)
        def _():
            sqkv[...] = jnp.zeros_like(sqkv)
            sqkv[0:SHARD_Q, :] = wq_ref[...].astype(BF16)
            sqkv[SHARD_Q:, 0:SHARD_KV] = wkv_ref[...].astype(BF16)
            sout[...] = wout_ref[...].astype(BF16)
            sconv[...] = jnp.zeros_like(sconv)
            for i in range(3):
                sconv[i:i + 1, 0:SHARD_CONV] = conv_ref[i]
            plan()[0]()

        @pl.when(t == relay_step)
        def _():
            plan()[1]()

        @pl.when(t == forward_step)
        def _():
            plan()[2]()

        o_ref[...] = _dot(a_ref[...], b_ref[...], _NT).astype(o_ref.dtype)

        @pl.when(t == steps - 1)
        def _():
            plan()[3]()
            conv_f[...] = jnp.zeros_like(conv_f)
            for k in range(N_DEV):
                for s0, e0, d0 in _q_pieces(k):
                    w_q_p[d0:d0 + e0 - s0, :] = gqkv[k, s0:e0, :]
                w_kv_p[:, _kv_dst(k):_kv_dst(k) + SHARD_KV] = gqkv[k, SHARD_Q:, 0:SHARD_KV]
                w_out_f[SHARD_OUT * k:SHARD_OUT * (k + 1), :] = gout[k]
                conv_f[0:3, SHARD_CONV * k:SHARD_CONV * (k + 1)] = gconv[k, 0:3, 0:SHARD_CONV]

    whole = lambda shape: pl.BlockSpec(shape, lambda i, j: (0,) * len(shape))
    return pl.pallas_call(
        body,
        name="in_proj_gather",
        grid=(m // bm, n // bn),
        in_specs=[pl.BlockSpec((bm, k_dim), lambda i, j: (i, 0)), pl.BlockSpec((bn, k_dim), lambda i, j: (j, 0)),
                  whole(w_q.shape), whole(w_kv.shape), whole(w_out.shape), whole(conv_w.shape)],
        out_specs=(pl.BlockSpec((bm, bn), lambda i, j: (i, j)),
                   whole((Q_COLS, Q_RANK)), whole((KV_RANK, KV_COLS)), whole((D_MODEL, D_MODEL)),
                   whole((8, CONV_WIDTH))),
        out_shape=(jax.ShapeDtypeStruct((m, n), BF16),
                   jax.ShapeDtypeStruct((Q_COLS, Q_RANK), BF16),
                   jax.ShapeDtypeStruct((KV_RANK, KV_COLS), BF16),
                   jax.ShapeDtypeStruct((D_MODEL, D_MODEL), BF16),
                   jax.ShapeDtypeStruct((8, CONV_WIDTH), F32)),
        scratch_shapes=[
            pltpu.VMEM(qkv_shape, BF16),
            pltpu.VMEM((SHARD_OUT, D_MODEL), BF16),
            pltpu.VMEM((8, LANES), F32),
            pltpu.VMEM((N_DEV,) + qkv_shape, BF16),
            pltpu.VMEM((N_DEV, SHARD_OUT, D_MODEL), BF16),
            pltpu.VMEM((N_DEV, 8, LANES), F32),
            pltpu.SemaphoreType.DMA((21,)),
            pltpu.SemaphoreType.DMA((21,)),
            pltpu.SemaphoreType.DMA((3,)),
        ],
        compiler_params=_params("arbitrary", "arbitrary"),
    )(u, w_in_p, w_q, w_kv, w_out, conv_w)


def _rope_tables(tp):
    half = D_ROPE // 2
    inv_freq = (1.0 / (ROPE_THETA ** (np.arange(half, dtype=np.float32) / half))).astype(np.float32)
    pos = (np.arange(tp) - PAD_FRONT).astype(np.float32)
    ang = pos[:, None] * inv_freq[None, :]
    cos = np.tile(np.cos(ang), (1, LANES // half))
    sin = np.tile(np.sin(ang), (1, LANES // half))
    first = (np.arange(LANES) % D_ROPE) < half
    zero = np.float32(0.0)
    return tuple(jnp.asarray(t, F32) for t in (cos, np.where(first, -sin, zero), np.where(first, zero, sin)))


def _rope(t, cos, sa, sb):
    return t * cos + pltpu.roll(t, LANES - D_ROPE // 2, 1) * sa + pltpu.roll(t, D_ROPE // 2, 1) * sb


def _rope_t(t, cos, sa, sb):
    return t * cos + pltpu.roll(t * sa, D_ROPE // 2, 1) + pltpu.roll(t * sb, LANES - D_ROPE // 2, 1)


def _qkv_fwd(p, wq, wkv, gq, gkv, tables, nb_seq, tp):
    ht = tp // 2

    def body(pa_ref, wq_ref, wkv_ref, gq_ref, gkv_ref, cos_ref, sa_ref, sb_ref, q_ref, k_ref, v_ref):
        pa = pa_ref[...].astype(F32)
        cq_hat, _ = _rms_stats(pa[:, :Q_RANK])
        ckv_hat, _ = _rms_stats(pa[:, Q_RANK:Q_RANK + KV_RANK])
        q = _dot((cq_hat * gq_ref[...]).astype(BF16), wq_ref[...], _NT) * Q_SCALE
        kv = _dot((ckv_hat * gkv_ref[...]).astype(BF16), wkv_ref[...])
        tabs = (cos_ref[...], sa_ref[...], sb_ref[...])
        lane = lax.broadcasted_iota(jnp.int32, (ht, LANES), 1)
        low = lane < D_ROPE
        mark = lane == D_ROPE
        row = (pl.program_id(0) % 2) * ht + lax.broadcasted_iota(jnp.int32, (ht, LANES), 0)
        k_pe = jnp.where(mark & (row < PAD_FRONT), NEG_INF, _rope(pa[:, Q_RANK + KV_RANK:], *tabs))
        one = jnp.where(mark & (row >= PAD_FRONT), 1.0, 0.0)
        pairs = [_rope(q[:, N_HEADS * D_NOPE + LANES * i:N_HEADS * D_NOPE + LANES * (i + 1)], *tabs) for i in range(2)]
        for h in range(N_HEADS):
            pair = pairs[h // 2]
            if h % 2:
                pair = pltpu.roll(pair, D_ROPE, 1)
            pe = jnp.where(low, pair, one)
            q_ref[0, h] = jnp.concatenate([q[:, D_NOPE * h:D_NOPE * (h + 1)], pe], axis=1).astype(BF16)
            k_ref[0, h] = jnp.concatenate([kv[:, D_NOPE * h:D_NOPE * (h + 1)], k_pe], axis=1).astype(BF16)
            v_ref[0, h] = kv[:, N_HEADS * D_NOPE + D_V * h:N_HEADS * D_NOPE + D_V * (h + 1)].astype(BF16)

    full = lambda a: pl.BlockSpec(a.shape, lambda i: (0,) * a.ndim)
    tab = pl.BlockSpec((ht, LANES), lambda i: (i % 2, 0))
    qk = pl.BlockSpec((1, N_HEADS, ht, 2 * LANES), lambda i: (i // 2, 0, i % 2, 0))
    return pl.pallas_call(
        body,
        name="qkv_fwd",
        grid=(2 * nb_seq,),
        in_specs=[pl.BlockSpec((ht, GRP_A), lambda i: (i, 0)), full(wq), full(wkv), full(gq), full(gkv), tab, tab, tab],
        out_specs=(qk, qk, pl.BlockSpec((1, N_HEADS, ht, D_V), lambda i: (i // 2, 0, i % 2, 0))),
        out_shape=(
            jax.ShapeDtypeStruct((nb_seq, N_HEADS, tp, 2 * LANES), BF16),
            jax.ShapeDtypeStruct((nb_seq, N_HEADS, tp, 2 * LANES), BF16),
            jax.ShapeDtypeStruct((nb_seq, N_HEADS, tp, D_V), BF16),
        ),
        compiler_params=_params("parallel"),
    )(p, wq, wkv, gq, gkv, *tables)


def _attn_fwd(q, k, v, p, g_attn):
    nb_seq, _, tp, _ = q.shape

    def body(q_ref, k_ref, v_ref, z_ref, g_ref, y_ref, o_ref, lse_ref):
        g = g_ref[...]
        for r0 in range(0, tp, Q_TILE):
            nq = min(Q_TILE, tp - r0)
            kend = r0 + nq
            qq = q_ref[0, 0, r0:kend, :]
            sd = _dot(qq, k_ref[0, 0, r0:kend, :], _NT)
            causal = (lax.broadcasted_iota(jnp.int32, (nq, nq), 1) <= lax.broadcasted_iota(jnp.int32, (nq, nq), 0))
            sd = jnp.where(causal, sd, NEG_INF)
            m = jnp.max(sd, axis=-1, keepdims=True)
            if r0:
                so = _dot(qq, k_ref[0, 0, 0:r0, :], _NT)
                m = jnp.maximum(m, jnp.max(so, axis=-1, keepdims=True))
            ed = jnp.exp2(sd - m)
            l = jnp.sum(ed, axis=-1, keepdims=True)
            o = _dot(ed.astype(BF16), v_ref[0, 0, r0:kend, :])
            if r0:
                eo = jnp.exp2(so - m)
                l = l + jnp.sum(eo, axis=-1, keepdims=True)
                o = o + _dot(eo.astype(BF16), v_ref[0, 0, 0:r0, :])
            o = o * (1.0 / l)
            o_ref[0, 0, r0:kend, :] = o
            lse_ref[0, 0, r0:kend, :] = jnp.broadcast_to(m + jnp.log2(l), (nq, LANES))
            ohat, _ = _rms_stats(o)
            z = z_ref[r0:kend, :].astype(F32)
            y_ref[r0:kend, :] = (ohat * g * (z * _sigmoid(z))).astype(BF16)

    qk = pl.BlockSpec((1, 1, tp, 2 * LANES), lambda b, h: (b, h, 0, 0))
    hv = pl.BlockSpec((1, 1, tp, D_V), lambda b, h: (b, h, 0, 0))
    return pl.pallas_call(
        body,
        name="attn_fwd",
        grid=(nb_seq, N_HEADS),
        in_specs=[qk, qk, hv,
                  pl.BlockSpec((tp, LANES), lambda b, h: (b, GRP_A // LANES + h)),
                  pl.BlockSpec((1, LANES), lambda b, h: (0, h))],
        out_specs=(pl.BlockSpec((tp, LANES), lambda b, h: (b, h)), hv, hv),
        out_shape=(
            jax.ShapeDtypeStruct((nb_seq * tp, N_HEADS * D_V), BF16),
            jax.ShapeDtypeStruct((nb_seq, N_HEADS, tp, D_V), F32),
            jax.ShapeDtypeStruct((nb_seq, N_HEADS, tp, LANES), F32),
        ),
        compiler_params=_params("parallel", "parallel"),
    )(q, k, v, p, g_attn)


_CONV_COL0 = (GRP_A + N_HEADS * D_V) // LANES


def _conv_specs(tp, order):
    cols = CONV_WIDTH // LANES
    return [pl.BlockSpec((tp, LANES), functools.partial(
        lambda a, b, off: order(a, b, off), off=_CONV_COL0 + i * cols)) for i in range(4)]


def _conv_fwd(p, conv_w, g_conv, nb_seq, tp):
    def body(b_ref, c_ref, h_ref, z_ref, w_ref, g_ref, y_ref):
        cc = c_ref[...].astype(F32) * h_ref[...].astype(F32)
        row = lax.broadcasted_iota(jnp.int32, (tp, LANES), 0)
        s1 = jnp.where(row >= 1, pltpu.roll(cc, 1, 0), 0.0)
        s2 = jnp.where(row >= 2, pltpu.roll(cc, 2, 0), 0.0)
        yc = b_ref[...].astype(F32) * (w_ref[0:1, :] * s2 + w_ref[1:2, :] * s1 + w_ref[2:3, :] * cc)
        r = lax.rsqrt(_group_mean(yc * yc) + EPS)
        z = z_ref[...].astype(F32)
        y_ref[...] = (yc * r * g_ref[...] * (z * _sigmoid(z))).astype(BF16)

    return pl.pallas_call(
        body,
        name="conv_fwd",
        grid=(nb_seq, CONV_WIDTH // LANES),
        in_specs=_conv_specs(tp, lambda b, t, off: (b, off + t)) + [
            pl.BlockSpec((8, LANES), lambda b, t: (0, t)),
            pl.BlockSpec((1, LANES), lambda b, t: (0, t))],
        out_specs=pl.BlockSpec((tp, LANES), lambda b, t: (b, t)),
        out_shape=jax.ShapeDtypeStruct((nb_seq * tp, CONV_WIDTH), BF16),
        compiler_params=_params("parallel", "parallel"),
    )(p, p, p, p, conv_w, g_conv)


def _token_copy(hbm, b, k, ts, buf, sem, to_hbm=False):
    lo, hi = max(k * ts - LANES, 0), (k + 1) * ts - LANES
    off = lo - (k * ts - LANES)
    src, dst = hbm.at[b, pl.ds(lo, hi - lo)], buf.at[pl.ds(off, hi - lo)]
    if to_hbm:
        src, dst = dst, src
    return pltpu.make_async_copy(src, dst, sem)


def _for_tile(k, nt, fn):
    for kk in range(nt):
        @pl.when(k == kk)
        def _(kk=kk):
            fn(kk)


def _out_proj_loss(ya, yc, w_out, x, target, g_final, nt):
    nb_seq, s, d = x.shape
    r, ka = ya.shape
    ts = (s + LANES) // nt
    steps = nb_seq * nt

    def body(a_ref, c_ref, w_ref, x_hbm, t_hbm, g_ref, dhb_ref, dg_ref, loss_ref,
             xbuf, tbuf, acc_ref, sems):
        i = pl.program_id(0)
        b, k = i // nt, i % nt

        @pl.when(i == 0)
        def _():
            acc_ref[...] = jnp.zeros_like(acc_ref)
            dg_ref[...] = jnp.zeros_like(dg_ref)

        slot = i % 2

        def fetch(seq, kk, sl):
            return [_token_copy(x_hbm, seq, kk, ts, xbuf.at[sl], sems.at[sl, 0]),
                    _token_copy(t_hbm, seq, kk, ts, tbuf.at[sl], sems.at[sl, 1])]

        def start(seq, sl, kk):
            if kk == 0:
                xbuf[sl, 0:LANES, :] = jnp.zeros((LANES, d), F32)
                tbuf[sl, 0:LANES, :] = jnp.zeros((LANES, d), F32)
            for cp in fetch(seq, kk, sl):
                cp.start()

        @pl.when(i == 0)
        def _():
            start(0, 0, 0)

        @pl.when(i + 1 < steps)
        def _():
            _for_tile((i + 1) % nt, nt, functools.partial(start, (i + 1) // nt, 1 - slot))

        mix = _dot(a_ref[...], w_ref[0:ka, :]) + _dot(c_ref[...], w_ref[ka:, :])
        _for_tile(k, nt, lambda kk: [cp.wait() for cp in fetch(b, kk, slot)])

        real = (lax.broadcasted_iota(jnp.int32, (ts, d), 0) >= LANES) | (k > 0)
        g = g_ref[...]
        hhat, rstd = _rms_stats(xbuf[slot] + mix)
        e = jnp.where(real, hhat * g - tbuf[slot], 0.0)
        acc_ref[...] += jnp.sum(e * e, axis=0, keepdims=True)
        dy = e * (1.0 / d)
        dg_ref[...] += jnp.sum(dy * hhat, axis=0, keepdims=True)
        dhb_ref[...] = _rms_bwd(g * dy, hhat, rstd).astype(BF16)

        @pl.when(i == steps - 1)
        def _():
            total = jnp.sum(acc_ref[...], axis=1, keepdims=True)
            loss_ref[...] = jnp.broadcast_to((0.5 / d) * total, loss_ref.shape)

    hbm = pl.BlockSpec(memory_space=pl.ANY)
    row = pl.BlockSpec((ts, d), lambda i: (i, 0))
    vec = pl.BlockSpec((1, d), lambda i: (0, 0))
    return pl.pallas_call(
        body,
        name="out_proj_loss",
        grid=(steps,),
        in_specs=[pl.BlockSpec((ts, ka), lambda i: (i, 0)), pl.BlockSpec((ts, yc.shape[1]), lambda i: (i, 0)),
                  pl.BlockSpec(w_out.shape, lambda i: (0, 0)), hbm, hbm, vec],
        out_specs=(row, vec, pl.BlockSpec((1, LANES), lambda i: (0, 0))),
        out_shape=(
            jax.ShapeDtypeStruct((r, d), BF16),
            jax.ShapeDtypeStruct((1, d), F32),
            jax.ShapeDtypeStruct((1, LANES), F32),
        ),
        scratch_shapes=[pltpu.VMEM((2, ts, d), F32), pltpu.VMEM((2, ts, d), F32), pltpu.VMEM((1, d), F32),
                        pltpu.SemaphoreType.DMA((2, 2))],
        compiler_params=_params("arbitrary"),
    )(ya, yc, w_out, x, target, g_final)


def _out_proj_bwd(dhb, w_out, ya, yc, bm):
    r, d = dhb.shape
    ka = ya.shape[1]
    n_mix = w_out.shape[0]
    last = r // bm - 1

    def body(dh_ref, w_ref, a_ref, c_ref, dcat_ref, dw_ref, acc_ref):
        @pl.when(pl.program_id(0) == 0)
        def _():
            acc_ref[...] = jnp.zeros_like(acc_ref)

        dh = dh_ref[...]
        dcat_ref[...] = _dot(dh, w_ref[...], _NT).astype(BF16)
        acc_ref[0:ka, :] += _dot(a_ref[...], dh, _TN)
        acc_ref[ka:, :] += _dot(c_ref[...], dh, _TN)

        @pl.when(pl.program_id(0) == last)
        def _():
            dw_ref[...] = acc_ref[...].astype(BF16)

    return pl.pallas_call(
        body,
        name="out_proj_bwd",
        grid=(r // bm,),
        in_specs=[pl.BlockSpec((bm, d), lambda i: (i, 0)), pl.BlockSpec(w_out.shape, lambda i: (0, 0)),
                  pl.BlockSpec((bm, ka), lambda i: (i, 0)), pl.BlockSpec((bm, yc.shape[1]), lambda i: (i, 0))],
        out_specs=(pl.BlockSpec((bm, n_mix), lambda i: (i, 0)),
                   pl.BlockSpec((n_mix, d), lambda i: (0, 0))),
        out_shape=(jax.ShapeDtypeStruct((r, n_mix), BF16),
                   jax.ShapeDtypeStruct((n_mix, d), BF16)),
        scratch_shapes=[pltpu.VMEM((n_mix, d), F32)],
        compiler_params=_params("arbitrary"),
    )(dhb, w_out, ya, yc)


def _attn_bwd(q, k, v, o, lse, dcat, p, g_attn):
    nb_seq, _, tp, _ = q.shape

    def body(q_ref, k_ref, v_ref, o_ref, lse_ref, dy_ref, z_ref, g_ref,
             dq_ref, dk_ref, dv_ref, dz_ref, dg_ref, dq_acc):
        @pl.when(pl.program_id(1) == 0)
        def _():
            dg_ref[...] = jnp.zeros_like(dg_ref)

        g = g_ref[...]
        z = z_ref[...].astype(F32)
        o = o_ref[0, 0]
        dy = dy_ref[...].astype(F32)
        sig = _sigmoid(z)
        ohat, r = _rms_stats(o)
        don = dy * (z * sig)
        dz_ref[...] = (dy * (ohat * g) * (sig * (1.0 + z * (1.0 - sig)))).astype(BF16)
        dg_ref[...] += jnp.sum(don * ohat, axis=0, keepdims=True)
        do = _rms_bwd(g * don, ohat, r)
        dvec = jnp.sum(do * o, axis=-1, keepdims=True)
        dob = do.astype(BF16)
        lse_col = lse_ref[0, 0, :, 0:1]
        dq_acc[...] = jnp.zeros_like(dq_acc)
        for k0 in range(0, tp, K_TILE):
            nk = min(K_TILE, tp - k0)
            nq = tp - k0
            qq = q_ref[0, 0, k0:, :]
            kk = k_ref[0, 0, k0:k0 + nk, :]
            causal = (lax.broadcasted_iota(jnp.int32, (nq, nk), 1) <= lax.broadcasted_iota(jnp.int32, (nq, nk), 0))
            pr = jnp.where(causal, jnp.exp2(_dot(qq, kk, _NT) - lse_col[k0:]), 0.0)
            dp = _dot(dob[k0:], v_ref[0, 0, k0:k0 + nk, :], _NT)
            ds = (pr * (dp - dvec[k0:])).astype(BF16)
            dv_ref[0, 0, k0:k0 + nk, :] = _dot(pr.astype(BF16), dob[k0:], _TN).astype(BF16)
            dk_ref[0, 0, k0:k0 + nk, :] = (_dot(ds, qq, _TN) * (ATTN_SCALE / Q_SCALE)).astype(BF16)
            dq_acc[k0:, :] += _dot(ds, kk)
        dq_ref[0, 0] = (dq_acc[...] * ATTN_SCALE).astype(BF16)

    qk = pl.BlockSpec((1, 1, tp, 2 * LANES), lambda h, b: (b, h, 0, 0))
    hv = pl.BlockSpec((1, 1, tp, D_V), lambda h, b: (b, h, 0, 0))
    col = pl.BlockSpec((tp, LANES), lambda h, b: (b, h))
    return pl.pallas_call(
        body,
        name="attn_bwd",
        grid=(N_HEADS, nb_seq),
        in_specs=[qk, qk, hv, hv, hv, col,
                  pl.BlockSpec((tp, LANES), lambda h, b: (b, GRP_A // LANES + h)),
                  pl.BlockSpec((1, LANES), lambda h, b: (0, h))],
        out_specs=(qk, qk, hv, col, pl.BlockSpec((1, LANES), lambda h, b: (0, h))),
        out_shape=(
            jax.ShapeDtypeStruct((nb_seq, N_HEADS, tp, 2 * LANES), BF16),
            jax.ShapeDtypeStruct((nb_seq, N_HEADS, tp, 2 * LANES), BF16),
            jax.ShapeDtypeStruct((nb_seq, N_HEADS, tp, D_V), BF16),
            jax.ShapeDtypeStruct((nb_seq * tp, N_HEADS * D_V), BF16),
            jax.ShapeDtypeStruct((1, N_HEADS * D_V), F32),
        ),
        scratch_shapes=[pltpu.VMEM((tp, 2 * LANES), F32)],
        compiler_params=_params("arbitrary", "arbitrary"),
    )(q, k, v, o, lse, dcat, p, g_attn)


def _qkv_bwd(p, dq, dk, dv, wq, wkv, gq, gkv, tables):
    nb_seq, _, tp, _ = dq.shape
    ht = tp // 2

    def body(pa_ref, dq_ref, dk_ref, dv_ref, wq_ref, wkv_ref, gq_ref, gkv_ref, cos_ref, sa_ref, sb_ref,
             dpa_ref, dwq_ref, dwkv_ref, dgq_ref, dgkv_ref):
        @pl.when(pl.program_id(0) == 0)
        def _():
            dwq_ref[...] = jnp.zeros_like(dwq_ref)
            dwkv_ref[...] = jnp.zeros_like(dwkv_ref)
            dgq_ref[...] = jnp.zeros_like(dgq_ref)
            dgkv_ref[...] = jnp.zeros_like(dgkv_ref)

        pa = pa_ref[...].astype(F32)
        gq, gkv = gq_ref[...], gkv_ref[...]
        cq_hat, rq = _rms_stats(pa[:, :Q_RANK])
        ckv_hat, rkv = _rms_stats(pa[:, Q_RANK:Q_RANK + KV_RANK])
        tabs = (cos_ref[...], sa_ref[...], sb_ref[...])

        pe = [dq_ref[0, h, :, D_NOPE:].astype(F32) for h in range(N_HEADS)]
        pairs = [_rope_t(pe[2 * i] + pltpu.roll(pe[2 * i + 1], D_ROPE, 1), *tabs).astype(BF16) for i in range(2)]
        dq_flat = jnp.concatenate([dq_ref[0, h, :, :D_NOPE] for h in range(N_HEADS)] + pairs, axis=1)
        dwq_ref[...] += _dot(dq_flat, (cq_hat * gq).astype(BF16), _TN)
        dcqn = _dot(dq_flat, wq_ref[...])
        dgq_ref[...] += jnp.sum(dcqn * cq_hat, axis=0, keepdims=True)
        dcq = _rms_bwd(gq * dcqn, cq_hat, rq)

        dkv_flat = jnp.concatenate([dk_ref[0, h, :, :D_NOPE] for h in range(N_HEADS)]
                                   + [dv_ref[0, h] for h in range(N_HEADS)], axis=1)
        dwkv_ref[...] += _dot((ckv_hat * gkv).astype(BF16), dkv_flat, _TN)
        dckvn = _dot(dkv_flat, wkv_ref[...], _NT)
        dgkv_ref[...] += jnp.sum(dckvn * ckv_hat, axis=0, keepdims=True)
        dckv = _rms_bwd(gkv * dckvn, ckv_hat, rkv)

        dk_pe = dk_ref[0, 0, :, D_NOPE:].astype(F32)
        for h in range(1, N_HEADS):
            dk_pe = dk_pe + dk_ref[0, h, :, D_NOPE:].astype(F32)
        dk_pe = jnp.where(lax.broadcasted_iota(jnp.int32, (ht, LANES), 1) < D_ROPE, dk_pe, 0.0)
        dpa_ref[...] = jnp.concatenate([dcq, dckv, _rope_t(dk_pe, *tabs)], axis=1).astype(BF16)

    full = lambda a: pl.BlockSpec(a.shape, lambda i: (0,) * a.ndim)
    tab = pl.BlockSpec((ht, LANES), lambda i: (i % 2, 0))
    qk = pl.BlockSpec((1, N_HEADS, ht, 2 * LANES), lambda i: (i // 2, 0, i % 2, 0))
    acc = lambda shape: pl.BlockSpec(shape, lambda i: (0, 0))
    return pl.pallas_call(
        body,
        name="qkv_bwd",
        grid=(2 * nb_seq,),
        in_specs=[pl.BlockSpec((ht, GRP_A), lambda i: (i, 0)), qk, qk,
                  pl.BlockSpec((1, N_HEADS, ht, D_V), lambda i: (i // 2, 0, i % 2, 0)),
                  full(wq), full(wkv), full(gq), full(gkv), tab, tab, tab],
        out_specs=(pl.BlockSpec((ht, GRP_A), lambda i: (i, 0)),
                   acc(wq.shape), acc(wkv.shape), acc((1, Q_RANK)), acc((1, KV_RANK))),
        out_shape=(
            jax.ShapeDtypeStruct((nb_seq * tp, GRP_A), BF16),
            jax.ShapeDtypeStruct(wq.shape, F32),
            jax.ShapeDtypeStruct(wkv.shape, F32),
            jax.ShapeDtypeStruct((1, Q_RANK), F32),
            jax.ShapeDtypeStruct((1, KV_RANK), F32),
        ),
        compiler_params=_params("arbitrary"),
    )(p, dq, dk, dv, wq, wkv, gq, gkv, *tables)


def _conv_bwd(p, dcat, conv_w, g_conv, nb_seq, tp):
    cols = CONV_WIDTH // LANES

    def body(b_ref, c_ref, h_ref, z_ref, dy_ref, w_ref, g_ref,
             db_ref, dc_ref, dh_ref, dz_ref, dw_ref, dg_ref):
        @pl.when(pl.program_id(1) == 0)
        def _():
            dw_ref[...] = jnp.zeros_like(dw_ref)
            dg_ref[...] = jnp.zeros_like(dg_ref)

        cb, c, h = b_ref[...].astype(F32), c_ref[...].astype(F32), h_ref[...].astype(F32)
        z, dy = z_ref[...].astype(F32), dy_ref[...].astype(F32)
        g = g_ref[...]
        w0, w1, w2 = w_ref[0:1, :], w_ref[1:2, :], w_ref[2:3, :]
        cc = c * h
        row = lax.broadcasted_iota(jnp.int32, (tp, LANES), 0)
        s1 = jnp.where(row >= 1, pltpu.roll(cc, 1, 0), 0.0)
        s2 = jnp.where(row >= 2, pltpu.roll(cc, 2, 0), 0.0)
        dwc = w0 * s2 + w1 * s1 + w2 * cc
        yc = cb * dwc
        r = lax.rsqrt(_group_mean(yc * yc) + EPS)
        ychat = yc * r
        sig = _sigmoid(z)
        dz_ref[...] = (dy * (ychat * g) * (sig * (1.0 + z * (1.0 - sig)))).astype(BF16)
        dyn = dy * (z * sig)
        dg_ref[...] += jnp.sum(dyn * ychat, axis=0, keepdims=True)
        gd = g * dyn
        dyc = r * (gd - ychat * _group_mean(gd * ychat))
        db_ref[...] = (dyc * dwc).astype(BF16)
        ddw = dyc * cb
        dw_ref[0:1, :] += jnp.sum(ddw * s2, axis=0, keepdims=True)
        dw_ref[1:2, :] += jnp.sum(ddw * s1, axis=0, keepdims=True)
        dw_ref[2:3, :] += jnp.sum(ddw * cc, axis=0, keepdims=True)
        u1 = jnp.where(row <= tp - 2, pltpu.roll(ddw, tp - 1, 0), 0.0)
        u2 = jnp.where(row <= tp - 3, pltpu.roll(ddw, tp - 2, 0), 0.0)
        dcc = w2 * ddw + w1 * u1 + w0 * u2
        dc_ref[...] = (dcc * h).astype(BF16)
        dh_ref[...] = (dcc * c).astype(BF16)

    col = pl.BlockSpec((tp, LANES), lambda t, b: (b, t))
    out = jax.ShapeDtypeStruct((nb_seq * tp, CONV_WIDTH), BF16)
    return pl.pallas_call(
        body,
        name="conv_bwd",
        grid=(cols, nb_seq),
        in_specs=_conv_specs(tp, lambda t, b, off: (b, off + t)) + [
            pl.BlockSpec((tp, LANES), lambda t, b: (b, N_HEADS * D_V // LANES + t)),
            pl.BlockSpec((8, LANES), lambda t, b: (0, t)),
            pl.BlockSpec((1, LANES), lambda t, b: (0, t))],
        out_specs=(col, col, col, col,
                   pl.BlockSpec((8, LANES), lambda t, b: (0, t)), pl.BlockSpec((1, LANES), lambda t, b: (0, t))),
        out_shape=(out, out, out, out,
                   jax.ShapeDtypeStruct((8, CONV_WIDTH), F32), jax.ShapeDtypeStruct((1, CONV_WIDTH), F32)),
        compiler_params=_params("arbitrary", "arbitrary"),
    )(p, p, p, p, dcat, conv_w, g_conv)


def _input_bwd(dps, w_in, x, meta, dh, norm_g, nt, send_in):
    nb_seq, s, d = x.shape
    r, kb = dps[0].shape
    ts = (s + LANES) // nt
    steps = nb_seq * nt
    n_dp = len(dps)
    in_slot = send_in.shape[1:]

    def body(*refs):
        dp_refs, w_ref, x_hbm, meta_ref, dh_ref, g_ref, pay_ref = refs[:n_dp], *refs[n_dp:n_dp + 6]
        o = n_dp + 6
        gx_hbm, dmeta_ref, dg_ref, r2_in = refs[o:o + 4]
        xbuf, gxbuf, tok_sems, own_in, r1_in, sum_in = refs[o + 4:o + 10]
        sems = refs[o + 10:]
        i = pl.program_id(0)
        b, k = i // nt, i % nt

        def plan():
            return _reduce_plan((pay_ref,), (own_in,), (r1_in,), (sum_in,), (r2_in,), *sems)

        @pl.when(i == 0)
        def _():
            dmeta_ref[...] = jnp.zeros_like(dmeta_ref)
            dg_ref[...] = jnp.zeros_like(dg_ref)
            plan()[0]()

        @pl.when(i == 1)
        def _():
            plan()[1]()

        def start(kk):
            if kk == 0:
                xbuf[0:PAD_FRONT, :] = jnp.zeros((PAD_FRONT, d), F32)
                xbuf[PAD_FRONT:LANES, :] = meta_ref[...]
            _token_copy(x_hbm, b, kk, ts, xbuf, tok_sems.at[0]).start()

        _for_tile(k, nt, start)
        du = _dot(dp_refs[0][...], w_ref[0:kb, :])
        for j in range(1, n_dp):
            du = du + _dot(dp_refs[j][...], w_ref[kb * j:kb * (j + 1), :])
        _for_tile(k, nt, lambda kk: _token_copy(x_hbm, b, kk, ts, xbuf, tok_sems.at[0]).wait())

        g = g_ref[...]
        hhat, rstd = _rms_stats(xbuf[...])
        dg_ref[...] += jnp.sum(du * hhat, axis=0, keepdims=True)
        res = _rms_bwd(g * du, hhat, rstd) + dh_ref[...].astype(F32)

        @pl.when(i > 0)
        def _():
            _for_tile(k, nt, lambda kk: _token_copy(gx_hbm, b, (kk - 1) % nt, ts, gxbuf, tok_sems.at[1], True).wait())

        gxbuf[...] = res

        @pl.when(k == 0)
        def _():
            dmeta_ref[...] += gxbuf[PAD_FRONT:LANES, :]

        _for_tile(k, nt, lambda kk: _token_copy(gx_hbm, b, kk, ts, gxbuf, tok_sems.at[1], True).start())

        @pl.when(i == steps - 1)
        def _():
            _token_copy(gx_hbm, b, nt - 1, ts, gxbuf, tok_sems.at[1], True).wait()
            plan()[2]()

    whole = lambda a: pl.BlockSpec(a.shape, lambda i: (0,) * a.ndim)
    hbm = pl.BlockSpec(memory_space=pl.ANY)
    return pl.pallas_call(
        body,
        name="input_bwd",
        grid=(steps,),
        in_specs=[pl.BlockSpec((ts, kb), lambda i: (i, 0)) for _ in dps]
        + [whole(w_in), hbm, whole(meta), pl.BlockSpec((ts, d), lambda i: (i, 0)), whole(norm_g), hbm],
        out_specs=(hbm, pl.BlockSpec((N_META, d), lambda i: (0, 0)), pl.BlockSpec((1, d), lambda i: (0, 0)), hbm),
        out_shape=(jax.ShapeDtypeStruct((nb_seq, s, d), F32),
                   jax.ShapeDtypeStruct((N_META, d), F32),
                   jax.ShapeDtypeStruct((1, d), F32),
                   jax.ShapeDtypeStruct((N_CHIPS,) + in_slot, BF16)),
        scratch_shapes=[pltpu.VMEM((ts, d), F32), pltpu.VMEM((ts, d), F32), pltpu.SemaphoreType.DMA((2,))]
        + _reduce_scratch([(in_slot, BF16)], [True]),
        compiler_params=_params("arbitrary"),
    )(*dps, w_in, x, meta, dh, norm_g, send_in)


def _in_proj_bwd_w(u, dps, bm, small_grads, send_out):
    r, d = u.shape
    kb = dps[0].shape[1]
    steps = r // bm
    n_dp, n_small = len(dps), len(small_grads)
    out_slot, small_slot = send_out.shape[1:], (SMALL_ROWS, LANES)

    def body(*refs):
        u_ref, dp_refs = refs[0], refs[1:1 + n_dp]
        small_refs = refs[1 + n_dp:1 + n_dp + n_small]
        o = 1 + n_dp + n_small
        pay_out, o_ref, r2_out, r2_small = refs[o:o + 4]
        acc_ref, ssmall, r1_out, sum_out, r1_small, sum_small = refs[o + 4:o + 10]
        sems = refs[o + 10:]
        i = pl.program_id(0)

        def plan():
            return _reduce_plan((pay_out, ssmall), (None, None), (r1_out, r1_small), (sum_out, sum_small),
                                (r2_out, r2_small), *sems)

        @pl.when(i == 0)
        def _():
            acc_ref[...] = jnp.zeros_like(acc_ref)
            _pack_small(ssmall, *small_refs)
            plan()[0]()

        @pl.when(i == 1)
        def _():
            plan()[1]()

        uu = u_ref[...]
        for j in range(n_dp):
            acc_ref[kb * j:kb * (j + 1), :] += _dot(dp_refs[j][...], uu, _TN)

        @pl.when(i == steps - 1)
        def _():
            for k in range(N_DEV):
                for s, e, c0 in _in_pieces(k):
                    o_ref[k, s:e, :] = acc_ref[c0:c0 + e - s, :].astype(BF16)
                o_ref[k, SHARD_IN:, :] = jnp.zeros((SHARD_IN_PAD - SHARD_IN, d), BF16)
            plan()[2]()

    whole = lambda a: pl.BlockSpec(a.shape, lambda i: (0,) * a.ndim)
    hbm = pl.BlockSpec(memory_space=pl.ANY)
    return pl.pallas_call(
        body,
        name="in_proj_bwd_w",
        grid=(steps,),
        in_specs=[pl.BlockSpec((bm, d), lambda i: (i, 0))]
        + [pl.BlockSpec((bm, kb), lambda i: (i, 0)) for _ in dps] + [whole(a) for a in small_grads]
        + [whole(send_out)],
        out_specs=(pl.BlockSpec((N_DEV, SHARD_IN_PAD, d), lambda i: (0, 0, 0)), hbm, hbm),
        out_shape=(jax.ShapeDtypeStruct((N_DEV, SHARD_IN_PAD, d), BF16),
                   jax.ShapeDtypeStruct((N_CHIPS,) + out_slot, BF16),
                   jax.ShapeDtypeStruct((N_CHIPS,) + small_slot, F32)),
        scratch_shapes=[pltpu.VMEM((kb * n_dp, d), F32), pltpu.VMEM((N_DEV,) + small_slot, F32)]
        + _reduce_scratch([(out_slot, BF16), (small_slot, F32)], [False, False]),
        compiler_params=_params("arbitrary"),
    )(u, *dps, *small_grads, send_out)


def _local_step(x, loss_target, u, p, meta_f, norm_g, w_in_p, q_norm_g, w_q_p, kv_norm_g, w_kv_p, conv_w_f,
                attn_out_g, conv_out_g, w_out_f, g_final):
    nb_seq, s, d = x.shape
    tp = s + LANES
    ht = tp // 2
    tables = _rope_tables(tp)

    q, k, v = _qkv_fwd(p, w_q_p, w_kv_p, q_norm_g, kv_norm_g, tables, nb_seq, tp)
    ya, o, lse = _attn_fwd(q, k, v, p, attn_out_g)
    yc = _conv_fwd(p, conv_w_f, conv_out_g, nb_seq, tp)
    dhb, d_final_g, loss_part = _out_proj_loss(ya, yc, w_out_f, x, loss_target, g_final, TOKEN_TILES)

    dcat, d_w_out = _out_proj_bwd(dhb, w_out_f, ya, yc, ht)
    send_out = d_w_out.reshape(N_DEV, SHARD_OUT, d)
    dq, dk, dv, dz_attn, d_attn_g = _attn_bwd(q, k, v, o, lse, dcat, p, attn_out_g)
    dpa, d_wq_p, d_wkv_p, d_gq, d_gkv = _qkv_bwd(p, dq, dk, dv, w_q_p, w_kv_p, q_norm_g, kv_norm_g, tables)
    d_b, d_c, d_h, dz_conv, d_conv_w, d_conv_g = _conv_bwd(p, dcat, conv_w_f, conv_out_g, nb_seq, tp)
    dps = (dpa, dz_attn, d_b, d_c, d_h, dz_conv)
    small = (d_wq_p, d_wkv_p, d_conv_w, d_final_g, d_gq, d_gkv, d_attn_g, d_conv_g, loss_part)
    send_in, r_out, r_small = _in_proj_bwd_w(u, dps, ht, small, send_out)
    grad_x, d_meta, d_norm_g, r_in = _input_bwd(dps, w_in_p, x, meta_f, dhb, norm_g, TOKEN_TILES, send_in)
    return grad_x, r_in, r_out, r_small, d_meta, d_norm_g


def kernel(x, meta_tokens, norm_g, w_in, q_norm_g, w_q_up, kv_norm_g, w_kv_up, conv_w, attn_out_g, conv_out_g, w_out, final_norm_g, loss_target, m_meta_tokens, m_norm_g, m_w_in, m_q_norm_g, m_w_q_up, m_kv_norm_g, m_w_kv_up, m_conv_w, m_attn_out_g, m_conv_out_g, m_w_out, m_final_norm_g, v_meta_tokens, v_norm_g, v_w_in, v_q_norm_g, v_w_q_up, v_kv_norm_g, v_w_kv_up, v_conv_w, v_attn_out_g, v_conv_out_g, v_w_out, v_final_norm_g):
    d = x.shape[-1]
    ht = (x.shape[1] + LANES) // 2
    u, w_in_p, meta_f = _prep_gather(x, meta_tokens, norm_g, w_in[0].T)
    p, w_q_p, w_kv_p, w_out_f, conv_w_f = _in_proj_gather(
        u, w_in_p, w_q_up[0].T, w_kv_up[0], w_out[0], conv_w.transpose(1, 0, 2), ht, 3 * GRP_A)
    g_final = final_norm_g.reshape(1, d)
    grad_x, r_in, r_out, r_small, d_meta, d_norm_g = _local_step(
        x, loss_target, u, p, meta_f, norm_g, w_in_p, q_norm_g, w_q_p, kv_norm_g, w_kv_p, conv_w_f,
        attn_out_g, conv_out_g, w_out_f, g_final)

    flat = lambda a: a.reshape(a.shape[-2:]) if a.ndim == 3 else a.reshape(1, -1) if a.ndim == 1 else a
    transposed = ("w_in", "w_q_up")

    def to_kernel(n, a):
        if n == "conv_w":
            return a.transpose(1, 0, 2)
        return flat(a).T if n in transposed else flat(a)

    def from_kernel(n, a, shape):
        if n == "conv_w":
            return a.transpose(1, 0, 2)
        return (a.T if n in transposed else a).reshape(shape)
    params = {
        "meta_tokens": (meta_tokens, m_meta_tokens, v_meta_tokens),
        "norm_g": (norm_g, m_norm_g, v_norm_g),
        "w_in": (w_in, m_w_in, v_w_in),
        "q_norm_g": (q_norm_g, m_q_norm_g, v_q_norm_g),
        "w_q_up": (w_q_up, m_w_q_up, v_w_q_up),
        "kv_norm_g": (kv_norm_g, m_kv_norm_g, v_kv_norm_g),
        "w_kv_up": (w_kv_up, m_w_kv_up, v_w_kv_up),
        "conv_w": (conv_w, m_conv_w, v_conv_w),
        "attn_out_g": (attn_out_g, m_attn_out_g, v_attn_out_g),
        "conv_out_g": (conv_out_g, m_conv_out_g, v_conv_out_g),
        "w_out": (w_out, m_w_out, v_w_out),
        "final_norm_g": (final_norm_g, m_final_norm_g, v_final_norm_g),
    }
    grads, loss = _reduce_tail(r_in, r_out, r_small, d_meta, d_norm_g)
    updated = _adamw(grads, {n: tuple(to_kernel(n, a) for a in t) for n, t in params.items()})
    outs = [[from_kernel(n, updated[n][i], params[n][0].shape) for n, _ in PARAM_SHAPES] for i in range(4)]
    return (loss[0, 0], grad_x, *outs[0], *outs[1], *outs[2], *outs[3])
```

```python
import functools

import jax
import jax.numpy as jnp
import numpy as np
from jax import lax
from jax.experimental import pallas as pl
from jax.experimental.pallas import tpu as pltpu

F32 = jnp.float32
BF16 = jnp.bfloat16

N_META = 16
D_MODEL = 1024
N_HEADS = 4
D_NOPE = 128
D_ROPE = 64
D_V = 128
Q_RANK = 256
KV_RANK = 128
CONV_WIDTH = 512
CONV_GROUP = 64
ROPE_THETA = 10000.0
ATTN_SCALE = (D_NOPE + D_ROPE) ** -0.5
Q_SCALE = ATTN_SCALE * 1.4426950408889634
EPS = 1e-6
NEG_INF = -1e30

ADAM_LR = 0.001
ADAM_B1 = 0.9
ADAM_B2 = 0.999
ADAM_EPS = 1e-08
ADAM_WD = 0.01
ADAM_STEP = 10

LANES = 128
PAD_FRONT = LANES - N_META
K_TILE = 256
Q_TILE = 512
N_DEV = 8
VMEM_LIMIT = 56 * 1024 * 1024

IN_PAD = 3072
GRP_A = 512
N_A = Q_RANK + KV_RANK + D_ROPE
IN_PROJ = 3008
SHARD_IN = IN_PROJ // N_DEV
SHARD_IN_PAD = 384
SHARD_Q = 96
SHARD_KV = 128
SHARD_OUT = 128
SHARD_CONV = 64
SHARD_META = 128
Q_COLS = N_HEADS * (D_NOPE + D_ROPE)
KV_COLS = N_HEADS * (D_NOPE + D_V)

ROW_Q, ROW_KV, ROW_META, ROW_CONV = 0, 256, 384, 400
ROW_REPL = 408
ROW_NORM, ROW_FINAL, ROW_GQ, ROW_GKV, ROW_ATTN, ROW_CONVG, ROW_LOSS = 408, 416, 424, 426, 427, 431, 435
SMALL_ROWS = 440

PARAM_SHAPES = (
    ("meta_tokens", (N_META, SHARD_META)), ("norm_g", (1, D_MODEL)), ("w_in", (SHARD_IN, D_MODEL)),
    ("q_norm_g", (1, Q_RANK)), ("w_q_up", (SHARD_Q, Q_RANK)), ("kv_norm_g", (1, KV_RANK)),
    ("w_kv_up", (KV_RANK, SHARD_KV)), ("conv_w", (3, 1, SHARD_CONV)), ("attn_out_g", (1, CONV_WIDTH)),
    ("conv_out_g", (1, CONV_WIDTH)), ("w_out", (SHARD_OUT, D_MODEL)), ("final_norm_g", (1, D_MODEL)),
)


def _in_pieces(k):
    lo, hi = SHARD_IN * k, SHARD_IN * (k + 1)
    out = []
    if lo < N_A:
        out.append((0, min(hi, N_A) - lo, lo))
    if hi > N_A:
        s = max(lo, N_A)
        out.append((s - lo, hi - lo, s + GRP_A - N_A))
    return out


def _group_pieces(g):
    lo, hi = GRP_A * g, GRP_A * (g + 1)
    out = []
    for k in range(N_DEV):
        for s, e, d in _in_pieces(k):
            a, b = max(d, lo), min(d + e - s, hi)
            if a < b:
                out.append((k, s + a - d, s + b - d, a - lo))
    return out


def _group_orders():
    n_grp = IN_PAD // GRP_A
    rows = []
    for chip in range(N_CHIPS):
        diagonal = {2 * (N_CHIPS - 1 - chip), 2 * (N_CHIPS - 1 - chip) + 1}
        free = [g for g in range(n_grp) if not diagonal & {k for k, _, _, _ in _group_pieces(g)}][:n_grp // 2]
        assert len(free) == n_grp // 2
        rows.append(free + [g for g in range(n_grp) if g not in free])
    return np.asarray(rows, np.int32)


def _q_pieces(k):
    lo, hi = SHARD_Q * k, SHARD_Q * (k + 1)
    out = []
    for h in range(N_HEADS):
        base = (D_NOPE + D_ROPE) * h
        s, e = max(lo, base), min(hi, base + D_NOPE)
        if s < e:
            out.append((s - lo, e - lo, D_NOPE * h + s - base))
        s, e = max(lo, base + D_NOPE), min(hi, base + D_NOPE + D_ROPE)
        if s < e:
            out.append((s - lo, e - lo, N_HEADS * D_NOPE + D_ROPE * h + s - base - D_NOPE))
    return out


def _kv_dst(k):
    return D_NOPE * (k // 2) + (N_HEADS * D_NOPE if k % 2 else 0)


def _params(*sem):
    return pltpu.CompilerParams(dimension_semantics=sem, vmem_limit_bytes=VMEM_LIMIT)


def _rms_stats(x):
    r = lax.rsqrt(jnp.mean(x * x, axis=-1, keepdims=True) + EPS)
    return x * r, r


def _rms_bwd(gdy, xhat, r):
    return r * (gdy - xhat * jnp.mean(gdy * xhat, axis=-1, keepdims=True))


def _sigmoid(z):
    return 1.0 / (1.0 + jnp.exp(-z))


def _group_mean(x):
    i0 = lax.broadcasted_iota(jnp.int32, (LANES, LANES), 0) // CONV_GROUP
    i1 = lax.broadcasted_iota(jnp.int32, (LANES, LANES), 1) // CONV_GROUP
    m = jnp.where(i0 == i1, 1.0 / CONV_GROUP, 0.0).astype(BF16)
    hi = x.astype(BF16)
    lo = (x - hi.astype(F32)).astype(BF16)
    return jnp.dot(hi, m, preferred_element_type=F32) + jnp.dot(lo, m, preferred_element_type=F32)


_NT = (((1,), (1,)), ((), ()))
_TN = (((0,), (0,)), ((), ()))


def _dot(a, b, dims=None):
    if dims is None:
        return jnp.dot(a, b, preferred_element_type=F32)
    return lax.dot_general(a, b, dims, preferred_element_type=F32)


def _device_position():
    x, y, c = lax.axis_index("x"), lax.axis_index("y"), lax.axis_index("c")
    return x, y, c, 4 * x + 2 * y + c


def _gather_plan(srcs, slots, send_sems, recv_sems, local_sems):
    x, y, c, _ = _device_position()
    me, sibling = (x, y, c), (x, y, 1 - c)
    flip = lambda v, on: v + on - 2 * v * on
    near = (flip(x, 1 - c), flip(y, c))
    far = (flip(x, c), flip(y, 1 - c))
    diag = (1 - x, 1 - y)
    n = len(srcs)

    def slot(a, px, py, pc):
        return slots[a].at[4 * px + 2 * py + pc]

    def copy(a, k, block, to, own=False):
        return pltpu.make_async_remote_copy(
            src_ref=srcs[a] if own else slot(a, *block),
            dst_ref=slot(a, *block),
            send_sem=send_sems.at[7 * a + k],
            recv_sem=recv_sems.at[7 * a + k],
            device_id=to,
            device_id_type=pl.DeviceIdType.MESH,
        )

    def local(a):
        return pltpu.make_async_copy(srcs[a], slot(a, *me), local_sems.at[a])

    sent = [(me, sibling), (me, (*near, c)), (me, (*far, c)), ((*near, c), (*far, c)),
            ((*near, c), sibling), ((*far, c), sibling), ((*diag, c), sibling)]
    landed = [sibling, (*near, c), (*far, c), (*diag, c), (*far, 1 - c), (*near, 1 - c), (*diag, 1 - c)]

    def send(a, k):
        return copy(a, k, *sent[k], own=k < 3)

    def arrival(a, k):
        return copy(a, k, landed[k], me)

    def start():
        for a in range(n):
            local(a).start()
            for k in range(3):
                send(a, k).start()

    def mid():
        for a in range(n):
            arrival(a, 1).wait_recv()
            send(a, 3).start()
            send(a, 4).start()
        for a in range(n):
            arrival(a, 2).wait_recv()
            send(a, 5).start()
        for a in range(n):
            local(a).wait()
            for k in (0, 4, 5):
                arrival(a, k).wait_recv()

    def late():
        for a in range(n):
            arrival(a, 3).wait_recv()
            send(a, 6).start()
        for a in range(n):
            arrival(a, 6).wait_recv()

    def finish():
        for a in range(n):
            for k in range(7):
                send(a, k).wait_send()

    return start, mid, late, finish


def _adam_update(g, w, m, v):
    m_new = ADAM_B1 * m + (1.0 - ADAM_B1) * g
    v_new = ADAM_B2 * v + (1.0 - ADAM_B2) * (g * g)
    m_hat = m_new / (1.0 - ADAM_B1 ** ADAM_STEP)
    v_hat = v_new / (1.0 - ADAM_B2 ** ADAM_STEP)
    return -ADAM_LR * (m_hat / (jnp.sqrt(v_hat) + ADAM_EPS) + ADAM_WD * w), m_new, v_new


def _adamw(grads, params):
    names = [n for n, _ in PARAM_SHAPES]
    n_p = len(names)

    def body(*refs):
        for i in range(n_p):
            g = refs[i][...]
            w, m, v = (refs[n_p + 3 * i + j][...] for j in range(3))
            delta, m_new, v_new = _adam_update(g, w, m, v)
            for j, val in enumerate((g, delta, m_new, v_new)):
                refs[4 * n_p + 4 * i + j][...] = val

    vm = pl.BlockSpec(memory_space=pltpu.VMEM)
    out_shape = []
    for _, shape in PARAM_SHAPES:
        out_shape += [jax.ShapeDtypeStruct(shape, F32)] * 4
    outs = pl.pallas_call(
        body,
        name="adamw",
        out_shape=tuple(out_shape),
        in_specs=[vm] * (4 * n_p),
        out_specs=(vm,) * (4 * n_p),
        compiler_params=pltpu.CompilerParams(vmem_limit_bytes=VMEM_LIMIT),
    )(*[grads[n] for n in names], *[a for n in names for a in params[n]])
    return {n: outs[4 * i:4 * i + 4] for i, n in enumerate(names)}


N_CHIPS = 4


def _reduce_plan(pays, owns, r1s, sums, r2s, send1, recv1, send2, recv2, local_sems):
    x, y, c, _ = _device_position()
    sibling = (x, y, 1 - c)
    chips = [((1 - x if rj & 2 else x), (1 - y if rj & 1 else y)) for rj in range(N_CHIPS)]
    n = len(pays)

    def slot_of(rj, core):
        return 4 * chips[rj][0] + 2 * chips[rj][1] + core

    def to_sibling(a, rj):
        return pltpu.make_async_remote_copy(
            src_ref=pays[a].at[slot_of(rj, 1 - c)], dst_ref=r1s[a].at[rj],
            send_sem=send1.at[N_CHIPS * a + rj], recv_sem=recv1.at[N_CHIPS * a + rj],
            device_id=sibling, device_id_type=pl.DeviceIdType.MESH)

    def load_own(a, rj):
        return pltpu.make_async_copy(pays[a].at[slot_of(rj, c)], owns[a].at[rj], local_sems.at[2 * N_CHIPS * a + rj])

    def to_chip(a, rj):
        return pltpu.make_async_remote_copy(
            src_ref=sums[a].at[rj], dst_ref=r2s[a].at[rj],
            send_sem=send2.at[N_CHIPS * a + rj], recv_sem=recv2.at[N_CHIPS * a + rj],
            device_id=(*chips[rj], c), device_id_type=pl.DeviceIdType.MESH)

    def keep(a):
        return pltpu.make_async_copy(sums[a].at[0], r2s[a].at[0], local_sems.at[2 * N_CHIPS * a + N_CHIPS])

    def start():
        for a in range(n):
            for rj in range(N_CHIPS):
                to_sibling(a, rj).start()
                if owns[a] is not None:
                    load_own(a, rj).start()

    def combine():
        for a in range(n):
            for rj in range(N_CHIPS):
                to_sibling(a, rj).wait_recv()
                if owns[a] is not None:
                    load_own(a, rj).wait()
                    mine = owns[a][rj]
                else:
                    mine = pays[a][slot_of(rj, c)]
                sums[a][rj] = (mine.astype(F32) + r1s[a][rj].astype(F32)).astype(sums[a].dtype)
            keep(a).start()
            for rj in range(1, N_CHIPS):
                to_chip(a, rj).start()

    def finish():
        for a in range(n):
            for rj in range(1, N_CHIPS):
                to_chip(a, rj).wait_recv()
            for rj in range(N_CHIPS):
                to_sibling(a, rj).wait_send()
            for rj in range(1, N_CHIPS):
                to_chip(a, rj).wait_send()
            keep(a).wait()

    return start, combine, finish


def _reduce_scratch(shapes_dtypes, own_flags):
    out = []
    for (shape, dtype), own in zip(shapes_dtypes, own_flags):
        if own:
            out.append(pltpu.VMEM((N_CHIPS,) + shape, dtype))
        out += [pltpu.VMEM((N_CHIPS,) + shape, dtype), pltpu.VMEM((N_CHIPS,) + shape, dtype)]
    n = len(shapes_dtypes)
    out += [pltpu.SemaphoreType.DMA((N_CHIPS * n,))] * 4 + [pltpu.SemaphoreType.DMA((2 * N_CHIPS * n,))]
    return out


def _pack_small(ssmall, dwq, dwkv, dconv, dfinal, dgq, dgkv, dattn, dconvg, loss_part):
    ssmall[...] = jnp.zeros_like(ssmall)
    rep = ssmall.at[0]
    for i in range(D_MODEL // LANES):
        rep[ROW_FINAL + i:ROW_FINAL + i + 1, :] = dfinal[:, LANES * i:LANES * (i + 1)]
    for i in range(Q_RANK // LANES):
        rep[ROW_GQ + i:ROW_GQ + i + 1, :] = dgq[:, LANES * i:LANES * (i + 1)]
    rep[ROW_GKV:ROW_GKV + 1, :] = dgkv[...]
    for i in range(CONV_WIDTH // LANES):
        rep[ROW_ATTN + i:ROW_ATTN + i + 1, :] = dattn[:, LANES * i:LANES * (i + 1)]
        rep[ROW_CONVG + i:ROW_CONVG + i + 1, :] = dconvg[:, LANES * i:LANES * (i + 1)]
    rep[ROW_LOSS:ROW_LOSS + 1, :] = loss_part[...]
    for k in range(N_DEV):
        if k:
            ssmall[k, ROW_REPL:, :] = ssmall[0, ROW_REPL:, :]
        for s, e, d in _q_pieces(k):
            for i in range(Q_RANK // LANES):
                ssmall[k, ROW_Q + SHARD_Q * i + s:ROW_Q + SHARD_Q * i + e, :] = dwq[d:d + e - s, LANES * i:LANES * (i + 1)]
        ssmall[k, ROW_KV:ROW_KV + KV_RANK, :] = dwkv[:, _kv_dst(k):_kv_dst(k) + SHARD_KV]
        ssmall[k, ROW_CONV:ROW_CONV + 3, 0:SHARD_CONV] = dconv[0:3, SHARD_CONV * k:SHARD_CONV * (k + 1)]


TOKEN_TILES = 4
TAIL_ROWS = N_META + D_MODEL // LANES


def _reduce_tail(r_in, r_out, r_small, d_meta, d_norm):
    n_p = len(PARAM_SHAPES)
    names = [n for n, _ in PARAM_SHAPES]

    def body(*refs):
        rin, rout, rsmall, dmeta, dnorm = refs[:5]
        g_out = {n: refs[5 + i] for i, n in enumerate(names)}
        loss_out = refs[5 + n_p]
        stail, rtail, gsum, gtail, send_sems, recv_sems = refs[6 + n_p:]
        x, y, c, me = _device_position()
        my_chip = 2 * x + y

        for k in range(N_DEV):
            stail[k, 0:N_META, :] = dmeta[:, SHARD_META * k:SHARD_META * (k + 1)]
            for i in range(D_MODEL // LANES):
                stail[k, N_META + i:N_META + i + 1, :] = dnorm[:, LANES * i:LANES * (i + 1)]
        copies = []
        for r in range(1, N_DEV):
            peer = (1 - x if r & 4 else x, 1 - y if r & 2 else y, 1 - c if r & 1 else c)
            copies.append(pltpu.make_async_remote_copy(
                src_ref=stail.at[4 * peer[0] + 2 * peer[1] + peer[2]],
                dst_ref=rtail.at[r],
                send_sem=send_sems.at[r - 1],
                recv_sem=recv_sems.at[r - 1],
                device_id=peer,
                device_id_type=pl.DeviceIdType.MESH,
            ))
        for cp in copies:
            cp.start()
        rtail[0] = stail[me]

        g = rin[my_chip].astype(F32)
        for ch in range(1, N_CHIPS):
            g = g + rin[ch ^ my_chip].astype(F32)
        g_out["w_in"][...] = g[:SHARD_IN, :]

        g = rout[my_chip].astype(F32)
        gs = rsmall[my_chip]
        for ch in range(1, N_CHIPS):
            g = g + rout[ch ^ my_chip].astype(F32)
            gs = gs + rsmall[ch ^ my_chip]
        g_out["w_out"][...] = g
        gsum[...] = gs
        for i in range(Q_RANK // LANES):
            g_out["w_q_up"][:, LANES * i:LANES * (i + 1)] = gsum[ROW_Q + SHARD_Q * i:ROW_Q + SHARD_Q * (i + 1), :]
        g_out["w_kv_up"][...] = gsum[ROW_KV:ROW_KV + KV_RANK, :]
        for i in range(3):
            g_out["conv_w"][i] = gsum[ROW_CONV + i:ROW_CONV + i + 1, 0:SHARD_CONV]
        for name, row, width in (("final_norm_g", ROW_FINAL, D_MODEL), ("q_norm_g", ROW_GQ, Q_RANK),
                                 ("kv_norm_g", ROW_GKV, KV_RANK), ("attn_out_g", ROW_ATTN, CONV_WIDTH),
                                 ("conv_out_g", ROW_CONVG, CONV_WIDTH)):
            for i in range(width // LANES):
                g_out[name][:, LANES * i:LANES * (i + 1)] = gsum[row + i:row + i + 1, :]
        loss_out[...] = gsum[ROW_LOSS:ROW_LOSS + 1, :]

        for cp in copies:
            cp.wait_recv()
        gt = rtail[me]
        for d in range(1, N_DEV):
            gt = gt + rtail[d ^ me]
        gtail[...] = gt
        g_out["meta_tokens"][...] = gtail[0:N_META, :]
        for i in range(D_MODEL // LANES):
            g_out["norm_g"][:, LANES * i:LANES * (i + 1)] = gtail[N_META + i:N_META + i + 1, :]
        for cp in copies:
            cp.wait_send()

    vm = pl.BlockSpec(memory_space=pltpu.VMEM)
    out_shape = [jax.ShapeDtypeStruct(shape, F32) for _, shape in PARAM_SHAPES]
    out_shape.append(jax.ShapeDtypeStruct((1, LANES), F32))
    outs = pl.pallas_call(
        body,
        name="reduce_tail",
        out_shape=tuple(out_shape),
        in_specs=[vm] * 5,
        out_specs=(vm,) * len(out_shape),
        scratch_shapes=[
            pltpu.VMEM((N_DEV, TAIL_ROWS, LANES), F32),
            pltpu.VMEM((N_DEV, TAIL_ROWS, LANES), F32),
            pltpu.VMEM((SMALL_ROWS, LANES), F32),
            pltpu.VMEM((TAIL_ROWS, LANES), F32),
            pltpu.SemaphoreType.DMA((N_DEV - 1,)),
            pltpu.SemaphoreType.DMA((N_DEV - 1,)),
        ],
        compiler_params=pltpu.CompilerParams(vmem_limit_bytes=VMEM_LIMIT),
    )(r_in, r_out, r_small, d_meta, d_norm)
    return {n: outs[i] for i, n in enumerate(names)}, outs[-1]


def _prep_in_proj(x, meta, norm_g, w_in_t, w_q, w_kv, w_out, conv_w, order):
    nb_seq, s, d = x.shape
    nb = s // LANES + 1
    m = nb_seq * nb * LANES
    n_real, n_norm = nb_seq * (nb - 1), nb_seq * nb
    n_grp = IN_PAD // GRP_A
    steps = n_norm + n_grp
    dot_rows = m // 4
    qkv_shape = (SHARD_Q + KV_RANK, Q_RANK)

    def tile(t):
        t = jnp.minimum(t, n_norm - 1)
        jj, b = t // nb_seq, t % nb_seq
        return b, jnp.minimum(jj, nb - 2), b * nb + (jj + 1) % nb

    def group(t, order_ref):
        return order_ref[jnp.maximum(t - n_norm, 0)]

    def body(order_ref, x_ref, meta_ref, g_ref, win_ref, wq_ref, wkv_ref, wout_ref, conv_ref,
             u_ref, p_ref, w_in_p, meta_f, w_q_p, w_kv_p, w_out_f, conv_f,
             u_all, sbig, gbig, smeta, gmeta, sqkv, sout, sconv, gqkv, gout, gconv,
             send_in, recv_in, local_in, send_meta, recv_meta, send_rest, recv_rest, local_rest):
        t = pl.program_id(0)
        px, py, pc, me = _device_position()

        def plan_in():
            return _gather_plan((sbig,), (gbig,), send_in, recv_in, local_in)

        def plan_rest():
            return _gather_plan((sqkv, sout, sconv), (gqkv, gout, gconv), send_rest, recv_rest, local_rest)

        def meta_copies():
            out = []
            for r in range(1, N_DEV):
                peer = (1 - px if r & 4 else px, 1 - py if r & 2 else py, 1 - pc if r & 1 else pc)
                out.append(pltpu.make_async_remote_copy(
                    src_ref=smeta,
                    dst_ref=gmeta.at[r],
                    send_sem=send_meta.at[r - 1],
                    recv_sem=recv_meta.at[r - 1],
                    device_id=peer,
                    device_id_type=pl.DeviceIdType.MESH,
                ))
            return out

        @pl.when(t == 0)
        def _():
            sbig[0:SHARD_IN, :] = win_ref[...].astype(BF16)
            sbig[SHARD_IN:, :] = jnp.zeros((SHARD_IN_PAD - SHARD_IN, d), BF16)
            smeta[...] = meta_ref[...]
            plan_in()[0]()
            for cp in meta_copies():
                cp.start()
            sqkv[...] = jnp.zeros_like(sqkv)
            sqkv[0:SHARD_Q, :] = wq_ref[...].astype(BF16)
            sqkv[SHARD_Q:, 0:SHARD_KV] = wkv_ref[...].astype(BF16)
            sout[...] = wout_ref[...].astype(BF16)
            sconv[...] = jnp.zeros_like(sconv)
            for i in range(3):
                sconv[i:i + 1, 0:SHARD_CONV] = conv_ref[i]

        def norm(h):
            hhat, _ = _rms_stats(h)
            return (hhat * g_ref[...]).astype(BF16)

        row0 = pl.multiple_of(tile(t)[2] * LANES, LANES)

        @pl.when(t < n_real)
        def _():
            un = norm(x_ref[0])
            u_ref[...] = un
            u_all[pl.ds(row0, LANES), :] = un

        @pl.when(t == n_real)
        def _():
            for cp in meta_copies():
                cp.wait_recv()
            gmeta[0] = smeta[...]
            for k in range(N_DEV):
                meta_f[:, SHARD_META * k:SHARD_META * (k + 1)] = gmeta[k ^ me]

        @pl.when((t >= n_real) & (t < n_norm))
        def _():
            un = jnp.concatenate([jnp.zeros((PAD_FRONT, d), BF16), norm(meta_f[...])], axis=0)
            u_ref[...] = un
            u_all[pl.ds(row0, LANES), :] = un

        @pl.when(t == n_norm)
        def _():
            plan_in()[1]()
            plan_rest()[0]()

        @pl.when(t == n_norm + n_grp // 2 - 1)
        def _():
            plan_rest()[1]()

        @pl.when(t == n_norm + n_grp // 2)
        def _():
            plan_in()[2]()

        @pl.when(t == n_norm + n_grp - 2)
        def _():
            plan_rest()[2]()

        @pl.when(t >= n_norm)
        def _():
            g = group(t, order_ref)
            for gg in range(n_grp):
                @pl.when(g == gg)
                def _(gg=gg):
                    if gg == 0:
                        w_in_p[N_A:GRP_A, :] = jnp.zeros((GRP_A - N_A, d), BF16)
                    for k, s0, e0, d0 in _group_pieces(gg):
                        w_in_p[d0:d0 + e0 - s0, :] = gbig[k, s0:e0, :]
            for r in range(m // dot_rows):
                rows = slice(dot_rows * r, dot_rows * (r + 1))
                p_ref[rows, :] = _dot(u_all[rows, :], w_in_p[...], _NT).astype(BF16)

        @pl.when(t == steps - 1)
        def _():
            plan_in()[3]()
            plan_rest()[3]()
            for cp in meta_copies():
                cp.wait_send()
            conv_f[...] = jnp.zeros_like(conv_f)
            for k in range(N_DEV):
                for s0, e0, d0 in _q_pieces(k):
                    w_q_p[d0:d0 + e0 - s0, :] = gqkv[k, s0:e0, :]
                w_kv_p[:, _kv_dst(k):_kv_dst(k) + SHARD_KV] = gqkv[k, SHARD_Q:, 0:SHARD_KV]
                w_out_f[SHARD_OUT * k:SHARD_OUT * (k + 1), :] = gout[k]
                conv_f[0:3, SHARD_CONV * k:SHARD_CONV * (k + 1)] = gconv[k, 0:3, 0:SHARD_CONV]

    whole = lambda shape: pl.BlockSpec(shape, lambda t, o: (0,) * len(shape))
    return pl.pallas_call(
        body,
        name="prep_in_proj_gather",
        grid_spec=pltpu.PrefetchScalarGridSpec(
            num_scalar_prefetch=1,
            grid=(steps,),
            in_specs=[
                pl.BlockSpec((1, LANES, d), lambda t, o: (tile(t)[0], tile(t)[1], 0)),
                whole(meta.shape), whole(norm_g.shape), whole(w_in_t.shape),
                whole(w_q.shape), whole(w_kv.shape), whole(w_out.shape), whole(conv_w.shape),
            ],
            out_specs=(pl.BlockSpec((LANES, d), lambda t, o: (tile(t)[2], 0)),
                       pl.BlockSpec((m, GRP_A), lambda t, o: (0, group(t, o))),
                       pl.BlockSpec((GRP_A, d), lambda t, o: (group(t, o), 0)),
                       whole((N_META, d)),
                       whole((Q_COLS, Q_RANK)), whole((KV_RANK, KV_COLS)), whole((D_MODEL, D_MODEL)),
                       whole((8, CONV_WIDTH))),
            scratch_shapes=[
                pltpu.VMEM((m, d), BF16),
                pltpu.VMEM((SHARD_IN_PAD, d), BF16),
                pltpu.VMEM((N_DEV, SHARD_IN_PAD, d), BF16),
                pltpu.VMEM((N_META, SHARD_META), F32),
                pltpu.VMEM((N_DEV, N_META, SHARD_META), F32),
                pltpu.VMEM(qkv_shape, BF16),
                pltpu.VMEM((SHARD_OUT, D_MODEL), BF16),
                pltpu.VMEM((8, LANES), F32),
                pltpu.VMEM((N_DEV,) + qkv_shape, BF16),
                pltpu.VMEM((N_DEV, SHARD_OUT, D_MODEL), BF16),
                pltpu.VMEM((N_DEV, 8, LANES), F32),
                pltpu.SemaphoreType.DMA((7,)),
                pltpu.SemaphoreType.DMA((7,)),
                pltpu.SemaphoreType.DMA((1,)),
                pltpu.SemaphoreType.DMA((N_DEV - 1,)),
                pltpu.SemaphoreType.DMA((N_DEV - 1,)),
                pltpu.SemaphoreType.DMA((21,)),
                pltpu.SemaphoreType.DMA((21,)),
                pltpu.SemaphoreType.DMA((3,)),
            ],
        ),
        out_shape=(jax.ShapeDtypeStruct((m, d), BF16),
                   jax.ShapeDtypeStruct((m, IN_PAD), BF16),
                   jax.ShapeDtypeStruct((IN_PAD, d), BF16),
                   jax.ShapeDtypeStruct((N_META, d), F32),
                   jax.ShapeDtypeStruct((Q_COLS, Q_RANK), BF16),
                   jax.ShapeDtypeStruct((KV_RANK, KV_COLS), BF16),
                   jax.ShapeDtypeStruct((D_MODEL, D_MODEL), BF16),
                   jax.ShapeDtypeStruct((8, CONV_WIDTH), F32)),
        compiler_params=_params("arbitrary"),
    )(order, x, meta, norm_g, w_in_t, w_q, w_kv, w_out, conv_w)


def _rope_tables(tp):
    half = D_ROPE // 2
    inv_freq = (1.0 / (ROPE_THETA ** (np.arange(half, dtype=np.float32) / half))).astype(np.float32)
    pos = (np.arange(tp) - PAD_FRONT).astype(np.float32)
    ang = pos[:, None] * inv_freq[None, :]
    cos = np.tile(np.cos(ang), (1, LANES // half))
    sin = np.tile(np.sin(ang), (1, LANES // half))
    first = (np.arange(LANES) % D_ROPE) < half
    zero = np.float32(0.0)
    return tuple(jnp.asarray(t, F32) for t in (cos, np.where(first, -sin, zero), np.where(first, zero, sin)))


def _rope(t, cos, sa, sb):
    return t * cos + pltpu.roll(t, LANES - D_ROPE // 2, 1) * sa + pltpu.roll(t, D_ROPE // 2, 1) * sb


def _rope_t(t, cos, sa, sb):
    return t * cos + pltpu.roll(t * sa, D_ROPE // 2, 1) + pltpu.roll(t * sb, LANES - D_ROPE // 2, 1)


def _qkv_fwd(p, wq, wkv, gq, gkv, tables, nb_seq, tp):
    ht = tp // 2

    def body(pa_ref, wq_ref, wkv_ref, gq_ref, gkv_ref, cos_ref, sa_ref, sb_ref, q_ref, k_ref, v_ref):
        pa = pa_ref[...].astype(F32)
        cq_hat, _ = _rms_stats(pa[:, :Q_RANK])
        ckv_hat, _ = _rms_stats(pa[:, Q_RANK:Q_RANK + KV_RANK])
        q = _dot((cq_hat * gq_ref[...]).astype(BF16), wq_ref[...], _NT) * Q_SCALE
        kv = _dot((ckv_hat * gkv_ref[...]).astype(BF16), wkv_ref[...])
        tabs = (cos_ref[...], sa_ref[...], sb_ref[...])
        lane = lax.broadcasted_iota(jnp.int32, (ht, LANES), 1)
        low = lane < D_ROPE
        mark = lane == D_ROPE
        row = (pl.program_id(0) % 2) * ht + lax.broadcasted_iota(jnp.int32, (ht, LANES), 0)
        k_pe = jnp.where(mark & (row < PAD_FRONT), NEG_INF, _rope(pa[:, Q_RANK + KV_RANK:], *tabs))
        one = jnp.where(mark & (row >= PAD_FRONT), 1.0, 0.0)
        pairs = [_rope(q[:, N_HEADS * D_NOPE + LANES * i:N_HEADS * D_NOPE + LANES * (i + 1)], *tabs) for i in range(2)]
        for h in range(N_HEADS):
            pair = pairs[h // 2]
            if h % 2:
                pair = pltpu.roll(pair, D_ROPE, 1)
            pe = jnp.where(low, pair, one)
            q_ref[0, h] = jnp.concatenate([q[:, D_NOPE * h:D_NOPE * (h + 1)], pe], axis=1).astype(BF16)
            k_ref[0, h] = jnp.concatenate([kv[:, D_NOPE * h:D_NOPE * (h + 1)], k_pe], axis=1).astype(BF16)
            v_ref[0, h] = kv[:, N_HEADS * D_NOPE + D_V * h:N_HEADS * D_NOPE + D_V * (h + 1)].astype(BF16)

    full = lambda a: pl.BlockSpec(a.shape, lambda i: (0,) * a.ndim)
    tab = pl.BlockSpec((ht, LANES), lambda i: (i % 2, 0))
    qk = pl.BlockSpec((1, N_HEADS, ht, 2 * LANES), lambda i: (i // 2, 0, i % 2, 0))
    return pl.pallas_call(
        body,
        name="qkv_fwd",
        grid=(2 * nb_seq,),
        in_specs=[pl.BlockSpec((ht, GRP_A), lambda i: (i, 0)), full(wq), full(wkv), full(gq), full(gkv), tab, tab, tab],
        out_specs=(qk, qk, pl.BlockSpec((1, N_HEADS, ht, D_V), lambda i: (i // 2, 0, i % 2, 0))),
        out_shape=(
            jax.ShapeDtypeStruct((nb_seq, N_HEADS, tp, 2 * LANES), BF16),
            jax.ShapeDtypeStruct((nb_seq, N_HEADS, tp, 2 * LANES), BF16),
            jax.ShapeDtypeStruct((nb_seq, N_HEADS, tp, D_V), BF16),
        ),
        compiler_params=_params("parallel"),
    )(p, wq, wkv, gq, gkv, *tables)


def _attn_fwd(q, k, v, p, g_attn):
    nb_seq, _, tp, _ = q.shape

    def body(q_ref, k_ref, v_ref, z_ref, g_ref, y_ref, o_ref, lse_ref):
        g = g_ref[...]
        for r0 in range(0, tp, Q_TILE):
            nq = min(Q_TILE, tp - r0)
            kend = r0 + nq
            qq = q_ref[0, 0, r0:kend, :]
            sd = _dot(qq, k_ref[0, 0, r0:kend, :], _NT)
            causal = (lax.broadcasted_iota(jnp.int32, (nq, nq), 1) <= lax.broadcasted_iota(jnp.int32, (nq, nq), 0))
            sd = jnp.where(causal, sd, NEG_INF)
            m = jnp.max(sd, axis=-1, keepdims=True)
            if r0:
                so = _dot(qq, k_ref[0, 0, 0:r0, :], _NT)
                m = jnp.maximum(m, jnp.max(so, axis=-1, keepdims=True))
            ed = jnp.exp2(sd - m)
            l = jnp.sum(ed, axis=-1, keepdims=True)
            o = _dot(ed.astype(BF16), v_ref[0, 0, r0:kend, :])
            if r0:
                eo = jnp.exp2(so - m)
                l = l + jnp.sum(eo, axis=-1, keepdims=True)
                o = o + _dot(eo.astype(BF16), v_ref[0, 0, 0:r0, :])
            o = o * (1.0 / l)
            o_ref[0, 0, r0:kend, :] = o
            lse_ref[0, 0, r0:kend, :] = jnp.broadcast_to(m + jnp.log2(l), (nq, LANES))
            ohat, _ = _rms_stats(o)
            z = z_ref[r0:kend, :].astype(F32)
            y_ref[r0:kend, :] = (ohat * g * (z * _sigmoid(z))).astype(BF16)

    qk = pl.BlockSpec((1, 1, tp, 2 * LANES), lambda b, h: (b, h, 0, 0))
    hv = pl.BlockSpec((1, 1, tp, D_V), lambda b, h: (b, h, 0, 0))
    return pl.pallas_call(
        body,
        name="attn_fwd",
        grid=(nb_seq, N_HEADS),
        in_specs=[qk, qk, hv,
                  pl.BlockSpec((tp, LANES), lambda b, h: (b, GRP_A // LANES + h)),
                  pl.BlockSpec((1, LANES), lambda b, h: (0, h))],
        out_specs=(pl.BlockSpec((tp, LANES), lambda b, h: (b, h)), hv, hv),
        out_shape=(
            jax.ShapeDtypeStruct((nb_seq * tp, N_HEADS * D_V), BF16),
            jax.ShapeDtypeStruct((nb_seq, N_HEADS, tp, D_V), F32),
            jax.ShapeDtypeStruct((nb_seq, N_HEADS, tp, LANES), F32),
        ),
        compiler_params=_params("parallel", "parallel"),
    )(q, k, v, p, g_attn)


_CONV_COL0 = (GRP_A + N_HEADS * D_V) // LANES


def _conv_specs(tp, order):
    cols = CONV_WIDTH // LANES
    return [pl.BlockSpec((tp, LANES), functools.partial(
        lambda a, b, off: order(a, b, off), off=_CONV_COL0 + i * cols)) for i in range(4)]


def _conv_fwd(p, conv_w, g_conv, nb_seq, tp):
    def body(b_ref, c_ref, h_ref, z_ref, w_ref, g_ref, y_ref):
        cc = c_ref[...].astype(F32) * h_ref[...].astype(F32)
        row = lax.broadcasted_iota(jnp.int32, (tp, LANES), 0)
        s1 = jnp.where(row >= 1, pltpu.roll(cc, 1, 0), 0.0)
        s2 = jnp.where(row >= 2, pltpu.roll(cc, 2, 0), 0.0)
        yc = b_ref[...].astype(F32) * (w_ref[0:1, :] * s2 + w_ref[1:2, :] * s1 + w_ref[2:3, :] * cc)
        r = lax.rsqrt(_group_mean(yc * yc) + EPS)
        z = z_ref[...].astype(F32)
        y_ref[...] = (yc * r * g_ref[...] * (z * _sigmoid(z))).astype(BF16)

    return pl.pallas_call(
        body,
        name="conv_fwd",
        grid=(nb_seq, CONV_WIDTH // LANES),
        in_specs=_conv_specs(tp, lambda b, t, off: (b, off + t)) + [
            pl.BlockSpec((8, LANES), lambda b, t: (0, t)),
            pl.BlockSpec((1, LANES), lambda b, t: (0, t))],
        out_specs=pl.BlockSpec((tp, LANES), lambda b, t: (b, t)),
        out_shape=jax.ShapeDtypeStruct((nb_seq * tp, CONV_WIDTH), BF16),
        compiler_params=_params("parallel", "parallel"),
    )(p, p, p, p, conv_w, g_conv)


def _token_copy(hbm, b, k, ts, buf, sem, to_hbm=False):
    lo, hi = max(k * ts - LANES, 0), (k + 1) * ts - LANES
    off = lo - (k * ts - LANES)
    src, dst = hbm.at[b, pl.ds(lo, hi - lo)], buf.at[pl.ds(off, hi - lo)]
    if to_hbm:
        src, dst = dst, src
    return pltpu.make_async_copy(src, dst, sem)


def _for_tile(k, nt, fn):
    for kk in range(nt):
        @pl.when(k == kk)
        def _(kk=kk):
            fn(kk)


def _out_proj_loss(ya, yc, w_out, x, target, g_final, nt):
    nb_seq, s, d = x.shape
    r, ka = ya.shape
    ts = (s + LANES) // nt
    steps = nb_seq * nt

    def body(a_ref, c_ref, w_ref, x_hbm, t_hbm, g_ref, dhb_ref, dg_ref, loss_ref,
             xbuf, tbuf, acc_ref, sems):
        i = pl.program_id(0)
        b, k = i // nt, i % nt

        @pl.when(i == 0)
        def _():
            acc_ref[...] = jnp.zeros_like(acc_ref)
            dg_ref[...] = jnp.zeros_like(dg_ref)

        slot = i % 2

        def fetch(seq, kk, sl):
            return [_token_copy(x_hbm, seq, kk, ts, xbuf.at[sl], sems.at[sl, 0]),
                    _token_copy(t_hbm, seq, kk, ts, tbuf.at[sl], sems.at[sl, 1])]

        def start(seq, sl, kk):
            if kk == 0:
                xbuf[sl, 0:LANES, :] = jnp.zeros((LANES, d), F32)
                tbuf[sl, 0:LANES, :] = jnp.zeros((LANES, d), F32)
            for cp in fetch(seq, kk, sl):
                cp.start()

        @pl.when(i == 0)
        def _():
            start(0, 0, 0)

        @pl.when(i + 1 < steps)
        def _():
            _for_tile((i + 1) % nt, nt, functools.partial(start, (i + 1) // nt, 1 - slot))

        mix = _dot(a_ref[...], w_ref[0:ka, :]) + _dot(c_ref[...], w_ref[ka:, :])
        _for_tile(k, nt, lambda kk: [cp.wait() for cp in fetch(b, kk, slot)])

        real = (lax.broadcasted_iota(jnp.int32, (ts, d), 0) >= LANES) | (k > 0)
        g = g_ref[...]
        hhat, rstd = _rms_stats(xbuf[slot] + mix)
        e = jnp.where(real, hhat * g - tbuf[slot], 0.0)
        acc_ref[...] += jnp.sum(e * e, axis=0, keepdims=True)
        dy = e * (1.0 / d)
        dg_ref[...] += jnp.sum(dy * hhat, axis=0, keepdims=True)
        dhb_ref[...] = _rms_bwd(g * dy, hhat, rstd).astype(BF16)

        @pl.when(i == steps - 1)
        def _():
            total = jnp.sum(acc_ref[...], axis=1, keepdims=True)
            loss_ref[...] = jnp.broadcast_to((0.5 / d) * total, loss_ref.shape)

    hbm = pl.BlockSpec(memory_space=pl.ANY)
    row = pl.BlockSpec((ts, d), lambda i: (i, 0))
    vec = pl.BlockSpec((1, d), lambda i: (0, 0))
    return pl.pallas_call(
        body,
        name="out_proj_loss",
        grid=(steps,),
        in_specs=[pl.BlockSpec((ts, ka), lambda i: (i, 0)), pl.BlockSpec((ts, yc.shape[1]), lambda i: (i, 0)),
                  pl.BlockSpec(w_out.shape, lambda i: (0, 0)), hbm, hbm, vec],
        out_specs=(row, vec, pl.BlockSpec((1, LANES), lambda i: (0, 0))),
        out_shape=(
            jax.ShapeDtypeStruct((r, d), BF16),
            jax.ShapeDtypeStruct((1, d), F32),
            jax.ShapeDtypeStruct((1, LANES), F32),
        ),
        scratch_shapes=[pltpu.VMEM((2, ts, d), F32), pltpu.VMEM((2, ts, d), F32), pltpu.VMEM((1, d), F32),
                        pltpu.SemaphoreType.DMA((2, 2))],
        compiler_params=_params("arbitrary"),
    )(ya, yc, w_out, x, target, g_final)


def _out_proj_bwd(dhb, w_out, ya, yc, bm):
    r, d = dhb.shape
    ka = ya.shape[1]
    n_mix = w_out.shape[0]
    last = r // bm - 1

    def body(dh_ref, w_ref, a_ref, c_ref, dcat_ref, dw_ref, acc_ref):
        @pl.when(pl.program_id(0) == 0)
        def _():
            acc_ref[...] = jnp.zeros_like(acc_ref)

        dh = dh_ref[...]
        dcat_ref[...] = _dot(dh, w_ref[...], _NT).astype(BF16)
        acc_ref[0:ka, :] += _dot(a_ref[...], dh, _TN)
        acc_ref[ka:, :] += _dot(c_ref[...], dh, _TN)

        @pl.when(pl.program_id(0) == last)
        def _():
            dw_ref[...] = acc_ref[...].astype(BF16)

    return pl.pallas_call(
        body,
        name="out_proj_bwd",
        grid=(r // bm,),
        in_specs=[pl.BlockSpec((bm, d), lambda i: (i, 0)), pl.BlockSpec(w_out.shape, lambda i: (0, 0)),
                  pl.BlockSpec((bm, ka), lambda i: (i, 0)), pl.BlockSpec((bm, yc.shape[1]), lambda i: (i, 0))],
        out_specs=(pl.BlockSpec((bm, n_mix), lambda i: (i, 0)),
                   pl.BlockSpec((n_mix, d), lambda i: (0, 0))),
        out_shape=(jax.ShapeDtypeStruct((r, n_mix), BF16),
                   jax.ShapeDtypeStruct((n_mix, d), BF16)),
        scratch_shapes=[pltpu.VMEM((n_mix, d), F32)],
        compiler_params=_params("arbitrary"),
    )(dhb, w_out, ya, yc)


def _attn_bwd(q, k, v, o, lse, dcat, p, g_attn):
    nb_seq, _, tp, _ = q.shape

    def body(q_ref, k_ref, v_ref, o_ref, lse_ref, dy_ref, z_ref, g_ref,
             dq_ref, dk_ref, dv_ref, dz_ref, dg_ref, dq_acc):
        @pl.when(pl.program_id(1) == 0)
        def _():
            dg_ref[...] = jnp.zeros_like(dg_ref)

        g = g_ref[...]
        z = z_ref[...].astype(F32)
        o = o_ref[0, 0]
        dy = dy_ref[...].astype(F32)
        sig = _sigmoid(z)
        ohat, r = _rms_stats(o)
        don = dy * (z * sig)
        dz_ref[...] = (dy * (ohat * g) * (sig * (1.0 + z * (1.0 - sig)))).astype(BF16)
        dg_ref[...] += jnp.sum(don * ohat, axis=0, keepdims=True)
        do = _rms_bwd(g * don, ohat, r)
        dvec = jnp.sum(do * o, axis=-1, keepdims=True)
        dob = do.astype(BF16)
        lse_col = lse_ref[0, 0, :, 0:1]
        dq_acc[...] = jnp.zeros_like(dq_acc)
        for k0 in range(0, tp, K_TILE):
            nk = min(K_TILE, tp - k0)
            nq = tp - k0
            qq = q_ref[0, 0, k0:, :]
            kk = k_ref[0, 0, k0:k0 + nk, :]
            causal = (lax.broadcasted_iota(jnp.int32, (nq, nk), 1) <= lax.broadcasted_iota(jnp.int32, (nq, nk), 0))
            pr = jnp.where(causal, jnp.exp2(_dot(qq, kk, _NT) - lse_col[k0:]), 0.0)
            dp = _dot(dob[k0:], v_ref[0, 0, k0:k0 + nk, :], _NT)
            ds = (pr * (dp - dvec[k0:])).astype(BF16)
            dv_ref[0, 0, k0:k0 + nk, :] = _dot(pr.astype(BF16), dob[k0:], _TN).astype(BF16)
            dk_ref[0, 0, k0:k0 + nk, :] = (_dot(ds, qq, _TN) * (ATTN_SCALE / Q_SCALE)).astype(BF16)
            dq_acc[k0:, :] += _dot(ds, kk)
        dq_ref[0, 0] = (dq_acc[...] * ATTN_SCALE).astype(BF16)

    qk = pl.BlockSpec((1, 1, tp, 2 * LANES), lambda h, b: (b, h, 0, 0))
    hv = pl.BlockSpec((1, 1, tp, D_V), lambda h, b: (b, h, 0, 0))
    col = pl.BlockSpec((tp, LANES), lambda h, b: (b, h))
    return pl.pallas_call(
        body,
        name="attn_bwd",
        grid=(N_HEADS, nb_seq),
        in_specs=[qk, qk, hv, hv, hv, col,
                  pl.BlockSpec((tp, LANES), lambda h, b: (b, GRP_A // LANES + h)),
                  pl.BlockSpec((1, LANES), lambda h, b: (0, h))],
        out_specs=(qk, qk, hv, col, pl.BlockSpec((1, LANES), lambda h, b: (0, h))),
        out_shape=(
            jax.ShapeDtypeStruct((nb_seq, N_HEADS, tp, 2 * LANES), BF16),
            jax.ShapeDtypeStruct((nb_seq, N_HEADS, tp, 2 * LANES), BF16),
            jax.ShapeDtypeStruct((nb_seq, N_HEADS, tp, D_V), BF16),
            jax.ShapeDtypeStruct((nb_seq * tp, N_HEADS * D_V), BF16),
            jax.ShapeDtypeStruct((1, N_HEADS * D_V), F32),
        ),
        scratch_shapes=[pltpu.VMEM((tp, 2 * LANES), F32)],
        compiler_params=_params("arbitrary", "arbitrary"),
    )(q, k, v, o, lse, dcat, p, g_attn)


def _qkv_bwd(p, dq, dk, dv, wq, wkv, gq, gkv, tables):
    nb_seq, _, tp, _ = dq.shape
    ht = tp // 2

    def body(pa_ref, dq_ref, dk_ref, dv_ref, wq_ref, wkv_ref, gq_ref, gkv_ref, cos_ref, sa_ref, sb_ref,
             dpa_ref, dwq_ref, dwkv_ref, dgq_ref, dgkv_ref):
        @pl.when(pl.program_id(0) == 0)
        def _():
            dwq_ref[...] = jnp.zeros_like(dwq_ref)
            dwkv_ref[...] = jnp.zeros_like(dwkv_ref)
            dgq_ref[...] = jnp.zeros_like(dgq_ref)
            dgkv_ref[...] = jnp.zeros_like(dgkv_ref)

        pa = pa_ref[...].astype(F32)
        gq, gkv = gq_ref[...], gkv_ref[...]
        cq_hat, rq = _rms_stats(pa[:, :Q_RANK])
        ckv_hat, rkv = _rms_stats(pa[:, Q_RANK:Q_RANK + KV_RANK])
        tabs = (cos_ref[...], sa_ref[...], sb_ref[...])

        pe = [dq_ref[0, h, :, D_NOPE:].astype(F32) for h in range(N_HEADS)]
        pairs = [_rope_t(pe[2 * i] + pltpu.roll(pe[2 * i + 1], D_ROPE, 1), *tabs).astype(BF16) for i in range(2)]
        dq_flat = jnp.concatenate([dq_ref[0, h, :, :D_NOPE] for h in range(N_HEADS)] + pairs, axis=1)
        dwq_ref[...] += _dot(dq_flat, (cq_hat * gq).astype(BF16), _TN)
        dcqn = _dot(dq_flat, wq_ref[...])
        dgq_ref[...] += jnp.sum(dcqn * cq_hat, axis=0, keepdims=True)
        dcq = _rms_bwd(gq * dcqn, cq_hat, rq)

        dkv_flat = jnp.concatenate([dk_ref[0, h, :, :D_NOPE] for h in range(N_HEADS)]
                                   + [dv_ref[0, h] for h in range(N_HEADS)], axis=1)
        dwkv_ref[...] += _dot((ckv_hat * gkv).astype(BF16), dkv_flat, _TN)
        dckvn = _dot(dkv_flat, wkv_ref[...], _NT)
        dgkv_ref[...] += jnp.sum(dckvn * ckv_hat, axis=0, keepdims=True)
        dckv = _rms_bwd(gkv * dckvn, ckv_hat, rkv)

        dk_pe = dk_ref[0, 0, :, D_NOPE:].astype(F32)
        for h in range(1, N_HEADS):
            dk_pe = dk_pe + dk_ref[0, h, :, D_NOPE:].astype(F32)
        dk_pe = jnp.where(lax.broadcasted_iota(jnp.int32, (ht, LANES), 1) < D_ROPE, dk_pe, 0.0)
        dpa_ref[...] = jnp.concatenate([dcq, dckv, _rope_t(dk_pe, *tabs)], axis=1).astype(BF16)

    full = lambda a: pl.BlockSpec(a.shape, lambda i: (0,) * a.ndim)
    tab = pl.BlockSpec((ht, LANES), lambda i: (i % 2, 0))
    qk = pl.BlockSpec((1, N_HEADS, ht, 2 * LANES), lambda i: (i // 2, 0, i % 2, 0))
    acc = lambda shape: pl.BlockSpec(shape, lambda i: (0, 0))
    return pl.pallas_call(
        body,
        name="qkv_bwd",
        grid=(2 * nb_seq,),
        in_specs=[pl.BlockSpec((ht, GRP_A), lambda i: (i, 0)), qk, qk,
                  pl.BlockSpec((1, N_HEADS, ht, D_V), lambda i: (i // 2, 0, i % 2, 0)),
                  full(wq), full(wkv), full(gq), full(gkv), tab, tab, tab],
        out_specs=(pl.BlockSpec((ht, GRP_A), lambda i: (i, 0)),
                   acc(wq.shape), acc(wkv.shape), acc((1, Q_RANK)), acc((1, KV_RANK))),
        out_shape=(
            jax.ShapeDtypeStruct((nb_seq * tp, GRP_A), BF16),
            jax.ShapeDtypeStruct(wq.shape, F32),
            jax.ShapeDtypeStruct(wkv.shape, F32),
            jax.ShapeDtypeStruct((1, Q_RANK), F32),
            jax.ShapeDtypeStruct((1, KV_RANK), F32),
        ),
        compiler_params=_params("arbitrary"),
    )(p, dq, dk, dv, wq, wkv, gq, gkv, *tables)


def _conv_bwd(p, dcat, conv_w, g_conv, nb_seq, tp):
    cols = CONV_WIDTH // LANES

    def body(b_ref, c_ref, h_ref, z_ref, dy_ref, w_ref, g_ref,
             db_ref, dc_ref, dh_ref, dz_ref, dw_ref, dg_ref):
        @pl.when(pl.program_id(1) == 0)
        def _():
            dw_ref[...] = jnp.zeros_like(dw_ref)
            dg_ref[...] = jnp.zeros_like(dg_ref)

        cb, c, h = b_ref[...].astype(F32), c_ref[...].astype(F32), h_ref[...].astype(F32)
        z, dy = z_ref[...].astype(F32), dy_ref[...].astype(F32)
        g = g_ref[...]
        w0, w1, w2 = w_ref[0:1, :], w_ref[1:2, :], w_ref[2:3, :]
        cc = c * h
        row = lax.broadcasted_iota(jnp.int32, (tp, LANES), 0)
        s1 = jnp.where(row >= 1, pltpu.roll(cc, 1, 0), 0.0)
        s2 = jnp.where(row >= 2, pltpu.roll(cc, 2, 0), 0.0)
        dwc = w0 * s2 + w1 * s1 + w2 * cc
        yc = cb * dwc
        r = lax.rsqrt(_group_mean(yc * yc) + EPS)
        ychat = yc * r
        sig = _sigmoid(z)
        dz_ref[...] = (dy * (ychat * g) * (sig * (1.0 + z * (1.0 - sig)))).astype(BF16)
        dyn = dy * (z * sig)
        dg_ref[...] += jnp.sum(dyn * ychat, axis=0, keepdims=True)
        gd = g * dyn
        dyc = r * (gd - ychat * _group_mean(gd * ychat))
        db_ref[...] = (dyc * dwc).astype(BF16)
        ddw = dyc * cb
        dw_ref[0:1, :] += jnp.sum(ddw * s2, axis=0, keepdims=True)
        dw_ref[1:2, :] += jnp.sum(ddw * s1, axis=0, keepdims=True)
        dw_ref[2:3, :] += jnp.sum(ddw * cc, axis=0, keepdims=True)
        u1 = jnp.where(row <= tp - 2, pltpu.roll(ddw, tp - 1, 0), 0.0)
        u2 = jnp.where(row <= tp - 3, pltpu.roll(ddw, tp - 2, 0), 0.0)
        dcc = w2 * ddw + w1 * u1 + w0 * u2
        dc_ref[...] = (dcc * h).astype(BF16)
        dh_ref[...] = (dcc * c).astype(BF16)

    col = pl.BlockSpec((tp, LANES), lambda t, b: (b, t))
    out = jax.ShapeDtypeStruct((nb_seq * tp, CONV_WIDTH), BF16)
    return pl.pallas_call(
        body,
        name="conv_bwd",
        grid=(cols, nb_seq),
        in_specs=_conv_specs(tp, lambda t, b, off: (b, off + t)) + [
            pl.BlockSpec((tp, LANES), lambda t, b: (b, N_HEADS * D_V // LANES + t)),
            pl.BlockSpec((8, LANES), lambda t, b: (0, t)),
            pl.BlockSpec((1, LANES), lambda t, b: (0, t))],
        out_specs=(col, col, col, col,
                   pl.BlockSpec((8, LANES), lambda t, b: (0, t)), pl.BlockSpec((1, LANES), lambda t, b: (0, t))),
        out_shape=(out, out, out, out,
                   jax.ShapeDtypeStruct((8, CONV_WIDTH), F32), jax.ShapeDtypeStruct((1, CONV_WIDTH), F32)),
        compiler_params=_params("arbitrary", "arbitrary"),
    )(p, p, p, p, dcat, conv_w, g_conv)


def _input_bwd(dps, w_in, x, meta, dh, norm_g, nt, send_in):
    nb_seq, s, d = x.shape
    r, kb = dps[0].shape
    ts = (s + LANES) // nt
    steps = nb_seq * nt
    n_dp = len(dps)
    in_slot = send_in.shape[1:]

    def body(*refs):
        dp_refs, w_ref, x_hbm, meta_ref, dh_ref, g_ref, pay_ref = refs[:n_dp], *refs[n_dp:n_dp + 6]
        o = n_dp + 6
        gx_hbm, dmeta_ref, dg_ref, r2_in = refs[o:o + 4]
        xbuf, gxbuf, tok_sems, own_in, r1_in, sum_in = refs[o + 4:o + 10]
        sems = refs[o + 10:]
        i = pl.program_id(0)
        b, k = i // nt, i % nt

        def plan():
            return _reduce_plan((pay_ref,), (own_in,), (r1_in,), (sum_in,), (r2_in,), *sems)

        @pl.when(i == 0)
        def _():
            dmeta_ref[...] = jnp.zeros_like(dmeta_ref)
            dg_ref[...] = jnp.zeros_like(dg_ref)
            plan()[0]()

        @pl.when(i == 1)
        def _():
            plan()[1]()

        def start(kk):
            if kk == 0:
                xbuf[0:PAD_FRONT, :] = jnp.zeros((PAD_FRONT, d), F32)
                xbuf[PAD_FRONT:LANES, :] = meta_ref[...]
            _token_copy(x_hbm, b, kk, ts, xbuf, tok_sems.at[0]).start()

        _for_tile(k, nt, start)
        du = _dot(dp_refs[0][...], w_ref[0:kb, :])
        for j in range(1, n_dp):
            du = du + _dot(dp_refs[j][...], w_ref[kb * j:kb * (j + 1), :])
        _for_tile(k, nt, lambda kk: _token_copy(x_hbm, b, kk, ts, xbuf, tok_sems.at[0]).wait())

        g = g_ref[...]
        hhat, rstd = _rms_stats(xbuf[...])
        dg_ref[...] += jnp.sum(du * hhat, axis=0, keepdims=True)
        res = _rms_bwd(g * du, hhat, rstd) + dh_ref[...].astype(F32)

        @pl.when(i > 0)
        def _():
            _for_tile(k, nt, lambda kk: _token_copy(gx_hbm, b, (kk - 1) % nt, ts, gxbuf, tok_sems.at[1], True).wait())

        gxbuf[...] = res

        @pl.when(k == 0)
        def _():
            dmeta_ref[...] += gxbuf[PAD_FRONT:LANES, :]

        _for_tile(k, nt, lambda kk: _token_copy(gx_hbm, b, kk, ts, gxbuf, tok_sems.at[1], True).start())

        @pl.when(i == steps - 1)
        def _():
            _token_copy(gx_hbm, b, nt - 1, ts, gxbuf, tok_sems.at[1], True).wait()
            plan()[2]()

    whole = lambda a: pl.BlockSpec(a.shape, lambda i: (0,) * a.ndim)
    hbm = pl.BlockSpec(memory_space=pl.ANY)
    return pl.pallas_call(
        body,
        name="input_bwd",
        grid=(steps,),
        in_specs=[pl.BlockSpec((ts, kb), lambda i: (i, 0)) for _ in dps]
        + [whole(w_in), hbm, whole(meta), pl.BlockSpec((ts, d), lambda i: (i, 0)), whole(norm_g), hbm],
        out_specs=(hbm, pl.BlockSpec((N_META, d), lambda i: (0, 0)), pl.BlockSpec((1, d), lambda i: (0, 0)), hbm),
        out_shape=(jax.ShapeDtypeStruct((nb_seq, s, d), F32),
                   jax.ShapeDtypeStruct((N_META, d), F32),
                   jax.ShapeDtypeStruct((1, d), F32),
                   jax.ShapeDtypeStruct((N_CHIPS,) + in_slot, BF16)),
        scratch_shapes=[pltpu.VMEM((ts, d), F32), pltpu.VMEM((ts, d), F32), pltpu.SemaphoreType.DMA((2,))]
        + _reduce_scratch([(in_slot, BF16)], [True]),
        compiler_params=_params("arbitrary"),
    )(*dps, w_in, x, meta, dh, norm_g, send_in)


def _in_proj_bwd_w(u, dps, bm, small_grads, send_out):
    r, d = u.shape
    kb = dps[0].shape[1]
    steps = r // bm
    n_dp, n_small = len(dps), len(small_grads)
    out_slot, small_slot = send_out.shape[1:], (SMALL_ROWS, LANES)

    def body(*refs):
        u_ref, dp_refs = refs[0], refs[1:1 + n_dp]
        small_refs = refs[1 + n_dp:1 + n_dp + n_small]
        o = 1 + n_dp + n_small
        pay_out, o_ref, r2_out, r2_small = refs[o:o + 4]
        acc_ref, ssmall, r1_out, sum_out, r1_small, sum_small = refs[o + 4:o + 10]
        sems = refs[o + 10:]
        i = pl.program_id(0)

        def plan():
            return _reduce_plan((pay_out, ssmall), (None, None), (r1_out, r1_small), (sum_out, sum_small),
                                (r2_out, r2_small), *sems)

        @pl.when(i == 0)
        def _():
            acc_ref[...] = jnp.zeros_like(acc_ref)
            _pack_small(ssmall, *small_refs)
            plan()[0]()

        @pl.when(i == 1)
        def _():
            plan()[1]()

        uu = u_ref[...]
        for j in range(n_dp):
            acc_ref[kb * j:kb * (j + 1), :] += _dot(dp_refs[j][...], uu, _TN)

        @pl.when(i == steps - 1)
        def _():
            for k in range(N_DEV):
                for s, e, c0 in _in_pieces(k):
                    o_ref[k, s:e, :] = acc_ref[c0:c0 + e - s, :].astype(BF16)
                o_ref[k, SHARD_IN:, :] = jnp.zeros((SHARD_IN_PAD - SHARD_IN, d), BF16)
            plan()[2]()

    whole = lambda a: pl.BlockSpec(a.shape, lambda i: (0,) * a.ndim)
    hbm = pl.BlockSpec(memory_space=pl.ANY)
    return pl.pallas_call(
        body,
        name="in_proj_bwd_w",
        grid=(steps,),
        in_specs=[pl.BlockSpec((bm, d), lambda i: (i, 0))]
        + [pl.BlockSpec((bm, kb), lambda i: (i, 0)) for _ in dps] + [whole(a) for a in small_grads]
        + [whole(send_out)],
        out_specs=(pl.BlockSpec((N_DEV, SHARD_IN_PAD, d), lambda i: (0, 0, 0)), hbm, hbm),
        out_shape=(jax.ShapeDtypeStruct((N_DEV, SHARD_IN_PAD, d), BF16),
                   jax.ShapeDtypeStruct((N_CHIPS,) + out_slot, BF16),
                   jax.ShapeDtypeStruct((N_CHIPS,) + small_slot, F32)),
        scratch_shapes=[pltpu.VMEM((kb * n_dp, d), F32), pltpu.VMEM((N_DEV,) + small_slot, F32)]
        + _reduce_scratch([(out_slot, BF16), (small_slot, F32)], [False, False]),
        compiler_params=_params("arbitrary"),
    )(u, *dps, *small_grads, send_out)


def _local_step(x, loss_target, u, p, meta_f, norm_g, w_in_p, q_norm_g, w_q_p, kv_norm_g, w_kv_p, conv_w_f,
                attn_out_g, conv_out_g, w_out_f, g_final):
    nb_seq, s, d = x.shape
    tp = s + LANES
    ht = tp // 2
    tables = _rope_tables(tp)

    q, k, v = _qkv_fwd(p, w_q_p, w_kv_p, q_norm_g, kv_norm_g, tables, nb_seq, tp)
    ya, o, lse = _attn_fwd(q, k, v, p, attn_out_g)
    yc = _conv_fwd(p, conv_w_f, conv_out_g, nb_seq, tp)
    dhb, d_final_g, loss_part = _out_proj_loss(ya, yc, w_out_f, x, loss_target, g_final, TOKEN_TILES)

    dcat, d_w_out = _out_proj_bwd(dhb, w_out_f, ya, yc, ht)
    send_out = d_w_out.reshape(N_DEV, SHARD_OUT, d)
    dq, dk, dv, dz_attn, d_attn_g = _attn_bwd(q, k, v, o, lse, dcat, p, attn_out_g)
    dpa, d_wq_p, d_wkv_p, d_gq, d_gkv = _qkv_bwd(p, dq, dk, dv, w_q_p, w_kv_p, q_norm_g, kv_norm_g, tables)
    d_b, d_c, d_h, dz_conv, d_conv_w, d_conv_g = _conv_bwd(p, dcat, conv_w_f, conv_out_g, nb_seq, tp)
    dps = (dpa, dz_attn, d_b, d_c, d_h, dz_conv)
    small = (d_wq_p, d_wkv_p, d_conv_w, d_final_g, d_gq, d_gkv, d_attn_g, d_conv_g, loss_part)
    send_in, r_out, r_small = _in_proj_bwd_w(u, dps, ht, small, send_out)
    grad_x, d_meta, d_norm_g, r_in = _input_bwd(dps, w_in_p, x, meta_f, dhb, norm_g, TOKEN_TILES, send_in)
    return grad_x, r_in, r_out, r_small, d_meta, d_norm_g


def kernel(x, meta_tokens, norm_g, w_in, q_norm_g, w_q_up, kv_norm_g, w_kv_up, conv_w, attn_out_g, conv_out_g, w_out, final_norm_g, loss_target, m_meta_tokens, m_norm_g, m_w_in, m_q_norm_g, m_w_q_up, m_kv_norm_g, m_w_kv_up, m_conv_w, m_attn_out_g, m_conv_out_g, m_w_out, m_final_norm_g, v_meta_tokens, v_norm_g, v_w_in, v_q_norm_g, v_w_q_up, v_kv_norm_g, v_w_kv_up, v_conv_w, v_attn_out_g, v_conv_out_g, v_w_out, v_final_norm_g):
    d = x.shape[-1]
    order = jnp.asarray(_group_orders())[2 * lax.axis_index("x") + lax.axis_index("y")]
    u, p, w_in_p, meta_f, w_q_p, w_kv_p, w_out_f, conv_w_f = _prep_in_proj(
        x, meta_tokens, norm_g, w_in[0].T, w_q_up[0].T, w_kv_up[0], w_out[0], conv_w.transpose(1, 0, 2), order)
    g_final = final_norm_g.reshape(1, d)
    grad_x, r_in, r_out, r_small, d_meta, d_norm_g = _local_step(
        x, loss_target, u, p, meta_f, norm_g, w_in_p, q_norm_g, w_q_p, kv_norm_g, w_kv_p, conv_w_f,
        attn_out_g, conv_out_g, w_out_f, g_final)

    flat = lambda a: a.reshape(a.shape[-2:]) if a.ndim == 3 else a.reshape(1, -1) if a.ndim == 1 else a
    transposed = ("w_in", "w_q_up")

    def to_kernel(n, a):
        if n == "conv_w":
            return a.transpose(1, 0, 2)
        return flat(a).T if n in transposed else flat(a)

    def from_kernel(n, a, shape):
        if n == "conv_w":
            return a.transpose(1, 0, 2)
        return (a.T if n in transposed else a).reshape(shape)
    params = {
        "meta_tokens": (meta_tokens, m_meta_tokens, v_meta_tokens),
        "norm_g": (norm_g, m_norm_g, v_norm_g),
        "w_in": (w_in, m_w_in, v_w_in),
        "q_norm_g": (q_norm_g, m_q_norm_g, v_q_norm_g),
        "w_q_up": (w_q_up, m_w_q_up, v_w_q_up),
        "kv_norm_g": (kv_norm_g, m_kv_norm_g, v_kv_norm_g),
        "w_kv_up": (w_kv_up, m_w_kv_up, v_w_kv_up),
        "conv_w": (conv_w, m_conv_w, v_conv_w),
        "attn_out_g": (attn_out_g, m_attn_out_g, v_attn_out_g),
        "conv_out_g": (conv_out_g, m_conv_out_g, v_conv_out_g),
        "w_out": (w_out, m_w_out, v_w_out),
        "final_norm_g": (final_norm_g, m_final_norm_g, v_final_norm_g),
    }
    grads, loss = _reduce_tail(r_in, r_out, r_small, d_meta, d_norm_g)
    updated = _adamw(grads, {n: tuple(to_kernel(n, a) for a in t) for n, t in params.items()})
    outs = [[from_kernel(n, updated[n][i], params[n][0].shape) for n, _ in PARAM_SHAPES] for i in range(4)]
    return (loss[0, 0], grad_x, *outs[0], *outs[1], *outs[2], *outs[3])
```

```python
import functools

import jax
import jax.numpy as jnp
import numpy as np
from jax import lax
from jax.experimental import pallas as pl
from jax.experimental.pallas import tpu as pltpu

F32 = jnp.float32
BF16 = jnp.bfloat16

N_META = 16
D_MODEL = 1024
N_HEADS = 4
D_NOPE = 128
D_ROPE = 64
D_V = 128
Q_RANK = 256
KV_RANK = 128
CONV_WIDTH = 512
CONV_GROUP = 64
ROPE_THETA = 10000.0
ATTN_SCALE = (D_NOPE + D_ROPE) ** -0.5
Q_SCALE = ATTN_SCALE * 1.4426950408889634
EPS = 1e-6
NEG_INF = -1e30

ADAM_LR = 0.001
ADAM_B1 = 0.9
ADAM_B2 = 0.999
ADAM_EPS = 1e-08
ADAM_WD = 0.01
ADAM_STEP = 10

LANES = 128
PAD_FRONT = LANES - N_META
K_TILE = 256
Q_TILE = 512
N_DEV = 8
VMEM_LIMIT = 56 * 1024 * 1024

IN_PAD = 3072
GRP_A = 512
N_A = Q_RANK + KV_RANK + D_ROPE
IN_PROJ = 3008
SHARD_IN = IN_PROJ // N_DEV
SHARD_IN_PAD = 384
SHARD_Q = 96
SHARD_KV = 128
SHARD_OUT = 128
SHARD_CONV = 64
SHARD_META = 128
Q_COLS = N_HEADS * (D_NOPE + D_ROPE)
KV_COLS = N_HEADS * (D_NOPE + D_V)

ROW_Q, ROW_KV, ROW_META, ROW_CONV = 0, 256, 384, 400
ROW_REPL = 408
ROW_NORM, ROW_FINAL, ROW_GQ, ROW_GKV, ROW_ATTN, ROW_CONVG, ROW_LOSS = 408, 416, 424, 426, 427, 431, 435
SMALL_ROWS = 440

PARAM_SHAPES = (
    ("meta_tokens", (N_META, SHARD_META)), ("norm_g", (1, D_MODEL)), ("w_in", (SHARD_IN, D_MODEL)),
    ("q_norm_g", (1, Q_RANK)), ("w_q_up", (SHARD_Q, Q_RANK)), ("kv_norm_g", (1, KV_RANK)),
    ("w_kv_up", (KV_RANK, SHARD_KV)), ("conv_w", (3, 1, SHARD_CONV)), ("attn_out_g", (1, CONV_WIDTH)),
    ("conv_out_g", (1, CONV_WIDTH)), ("w_out", (SHARD_OUT, D_MODEL)), ("final_norm_g", (1, D_MODEL)),
)


def _in_pieces(k):
    lo, hi = SHARD_IN * k, SHARD_IN * (k + 1)
    out = []
    if lo < N_A:
        out.append((0, min(hi, N_A) - lo, lo))
    if hi > N_A:
        s = max(lo, N_A)
        out.append((s - lo, hi - lo, s + GRP_A - N_A))
    return out


P_TILE = 256


def _tile_pieces(j):
    lo, hi = P_TILE * j, P_TILE * (j + 1)
    out = []
    for k in range(N_DEV):
        for s, e, d in _in_pieces(k):
            a, b = max(d, lo), min(d + e - s, hi)
            if a < b:
                out.append((k, s + a - d, s + b - d, a - lo))
    return out


def _tile_orders():
    n_tiles = IN_PAD // P_TILE
    sources = [{k for k, _, _, _ in _tile_pieces(j)} for j in range(n_tiles)]
    rows, n_early, n_free = [], n_tiles, n_tiles
    for chip in range(N_CHIPS):
        own = {2 * chip, 2 * chip + 1}
        diagonal = {2 * (N_CHIPS - 1 - chip), 2 * (N_CHIPS - 1 - chip) + 1}
        early = [j for j in range(n_tiles) if sources[j] <= own]
        late = [j for j in range(n_tiles) if sources[j] & diagonal]
        mid = [j for j in range(n_tiles) if j not in early and j not in late]
        rows.append(early + mid + late)
        n_early, n_free = min(n_early, len(early)), min(n_free, len(early) + len(mid))
    return np.asarray(rows, np.int32), n_early, n_free


def _q_pieces(k):
    lo, hi = SHARD_Q * k, SHARD_Q * (k + 1)
    out = []
    for h in range(N_HEADS):
        base = (D_NOPE + D_ROPE) * h
        s, e = max(lo, base), min(hi, base + D_NOPE)
        if s < e:
            out.append((s - lo, e - lo, D_NOPE * h + s - base))
        s, e = max(lo, base + D_NOPE), min(hi, base + D_NOPE + D_ROPE)
        if s < e:
            out.append((s - lo, e - lo, N_HEADS * D_NOPE + D_ROPE * h + s - base - D_NOPE))
    return out


def _kv_dst(k):
    return D_NOPE * (k // 2) + (N_HEADS * D_NOPE if k % 2 else 0)


def _params(*sem):
    return pltpu.CompilerParams(dimension_semantics=sem, vmem_limit_bytes=VMEM_LIMIT)


def _rms_stats(x):
    r = lax.rsqrt(jnp.mean(x * x, axis=-1, keepdims=True) + EPS)
    return x * r, r


def _rms_bwd(gdy, xhat, r):
    return r * (gdy - xhat * jnp.mean(gdy * xhat, axis=-1, keepdims=True))


def _sigmoid(z):
    return 1.0 / (1.0 + jnp.exp(-z))


def _group_mean(x):
    i0 = lax.broadcasted_iota(jnp.int32, (LANES, LANES), 0) // CONV_GROUP
    i1 = lax.broadcasted_iota(jnp.int32, (LANES, LANES), 1) // CONV_GROUP
    m = jnp.where(i0 == i1, 1.0 / CONV_GROUP, 0.0).astype(BF16)
    hi = x.astype(BF16)
    lo = (x - hi.astype(F32)).astype(BF16)
    return jnp.dot(hi, m, preferred_element_type=F32) + jnp.dot(lo, m, preferred_element_type=F32)


_NT = (((1,), (1,)), ((), ()))
_TN = (((0,), (0,)), ((), ()))


def _dot(a, b, dims=None):
    if dims is None:
        return jnp.dot(a, b, preferred_element_type=F32)
    return lax.dot_general(a, b, dims, preferred_element_type=F32)


def _device_position():
    x, y, c = lax.axis_index("x"), lax.axis_index("y"), lax.axis_index("c")
    return x, y, c, 4 * x + 2 * y + c


def _gather_plan(srcs, slots, send_sems, recv_sems, local_sems):
    x, y, c, _ = _device_position()
    me, sibling = (x, y, c), (x, y, 1 - c)
    flip = lambda v, on: v + on - 2 * v * on
    near = (flip(x, 1 - c), flip(y, c))
    far = (flip(x, c), flip(y, 1 - c))
    diag = (1 - x, 1 - y)
    n = len(srcs)

    def slot(a, px, py, pc):
        return slots[a].at[4 * px + 2 * py + pc]

    def copy(a, k, block, to, own=False):
        return pltpu.make_async_remote_copy(
            src_ref=srcs[a] if own else slot(a, *block),
            dst_ref=slot(a, *block),
            send_sem=send_sems.at[7 * a + k],
            recv_sem=recv_sems.at[7 * a + k],
            device_id=to,
            device_id_type=pl.DeviceIdType.MESH,
        )

    def local(a):
        return pltpu.make_async_copy(srcs[a], slot(a, *me), local_sems.at[a])

    sent = [(me, sibling), (me, (*near, c)), (me, (*far, c)), ((*near, c), (*far, c)),
            ((*near, c), sibling), ((*far, c), sibling), ((*diag, c), sibling)]
    landed = [sibling, (*near, c), (*far, c), (*diag, c), (*far, 1 - c), (*near, 1 - c), (*diag, 1 - c)]

    def send(a, k):
        return copy(a, k, *sent[k], own=k < 3)

    def arrival(a, k):
        return copy(a, k, landed[k], me)

    def start():
        for a in range(n):
            local(a).start()
            for k in range(3):
                send(a, k).start()

    def own():
        for a in range(n):
            local(a).wait()
            arrival(a, 0).wait_recv()

    def mid():
        for a in range(n):
            arrival(a, 1).wait_recv()
            send(a, 3).start()
            send(a, 4).start()
        for a in range(n):
            arrival(a, 2).wait_recv()
            send(a, 5).start()
        for a in range(n):
            for k in (4, 5):
                arrival(a, k).wait_recv()

    def late():
        for a in range(n):
            arrival(a, 3).wait_recv()
            send(a, 6).start()
        for a in range(n):
            arrival(a, 6).wait_recv()

    def finish():
        for a in range(n):
            for k in range(7):
                send(a, k).wait_send()

    return start, own, mid, late, finish


def _adam_update(g, w, m, v):
    m_new = ADAM_B1 * m + (1.0 - ADAM_B1) * g
    v_new = ADAM_B2 * v + (1.0 - ADAM_B2) * (g * g)
    m_hat = m_new / (1.0 - ADAM_B1 ** ADAM_STEP)
    v_hat = v_new / (1.0 - ADAM_B2 ** ADAM_STEP)
    return -ADAM_LR * (m_hat / (jnp.sqrt(v_hat) + ADAM_EPS) + ADAM_WD * w), m_new, v_new


def _adamw(grads, params):
    names = [n for n, _ in PARAM_SHAPES]
    n_p = len(names)

    def body(*refs):
        for i in range(n_p):
            g = refs[i][...]
            w, m, v = (refs[n_p + 3 * i + j][...] for j in range(3))
            delta, m_new, v_new = _adam_update(g, w, m, v)
            for j, val in enumerate((g, delta, m_new, v_new)):
                refs[4 * n_p + 4 * i + j][...] = val

    vm = pl.BlockSpec(memory_space=pltpu.VMEM)
    out_shape = []
    for _, shape in PARAM_SHAPES:
        out_shape += [jax.ShapeDtypeStruct(shape, F32)] * 4
    outs = pl.pallas_call(
        body,
        name="adamw",
        out_shape=tuple(out_shape),
        in_specs=[vm] * (4 * n_p),
        out_specs=(vm,) * (4 * n_p),
        compiler_params=pltpu.CompilerParams(vmem_limit_bytes=VMEM_LIMIT),
    )(*[grads[n] for n in names], *[a for n in names for a in params[n]])
    return {n: outs[4 * i:4 * i + 4] for i, n in enumerate(names)}


N_CHIPS = 4


def _reduce_plan(pays, owns, r1s, sums, r2s, send1, recv1, send2, recv2, local_sems):
    x, y, c, _ = _device_position()
    sibling = (x, y, 1 - c)
    chips = [((1 - x if rj & 2 else x), (1 - y if rj & 1 else y)) for rj in range(N_CHIPS)]
    n = len(pays)

    def slot_of(rj, core):
        return 4 * chips[rj][0] + 2 * chips[rj][1] + core

    def to_sibling(a, rj):
        return pltpu.make_async_remote_copy(
            src_ref=pays[a].at[slot_of(rj, 1 - c)], dst_ref=r1s[a].at[rj],
            send_sem=send1.at[N_CHIPS * a + rj], recv_sem=recv1.at[N_CHIPS * a + rj],
            device_id=sibling, device_id_type=pl.DeviceIdType.MESH)

    def load_own(a, rj):
        return pltpu.make_async_copy(pays[a].at[slot_of(rj, c)], owns[a].at[rj], local_sems.at[2 * N_CHIPS * a + rj])

    def to_chip(a, rj):
        return pltpu.make_async_remote_copy(
            src_ref=sums[a].at[rj], dst_ref=r2s[a].at[rj],
            send_sem=send2.at[N_CHIPS * a + rj], recv_sem=recv2.at[N_CHIPS * a + rj],
            device_id=(*chips[rj], c), device_id_type=pl.DeviceIdType.MESH)

    def keep(a):
        return pltpu.make_async_copy(sums[a].at[0], r2s[a].at[0], local_sems.at[2 * N_CHIPS * a + N_CHIPS])

    def start():
        for a in range(n):
            for rj in range(N_CHIPS):
                to_sibling(a, rj).start()
                if owns[a] is not None:
                    load_own(a, rj).start()

    def combine():
        for a in range(n):
            for rj in range(N_CHIPS):
                to_sibling(a, rj).wait_recv()
                if owns[a] is not None:
                    load_own(a, rj).wait()
                    mine = owns[a][rj]
                else:
                    mine = pays[a][slot_of(rj, c)]
                sums[a][rj] = (mine.astype(F32) + r1s[a][rj].astype(F32)).astype(sums[a].dtype)
            keep(a).start()
            for rj in range(1, N_CHIPS):
                to_chip(a, rj).start()

    def finish():
        for a in range(n):
            for rj in range(1, N_CHIPS):
                to_chip(a, rj).wait_recv()
            for rj in range(N_CHIPS):
                to_sibling(a, rj).wait_send()
            for rj in range(1, N_CHIPS):
                to_chip(a, rj).wait_send()
            keep(a).wait()

    return start, combine, finish


def _reduce_scratch(shapes_dtypes, own_flags):
    out = []
    for (shape, dtype), own in zip(shapes_dtypes, own_flags):
        if own:
            out.append(pltpu.VMEM((N_CHIPS,) + shape, dtype))
        out += [pltpu.VMEM((N_CHIPS,) + shape, dtype), pltpu.VMEM((N_CHIPS,) + shape, dtype)]
    n = len(shapes_dtypes)
    out += [pltpu.SemaphoreType.DMA((N_CHIPS * n,))] * 4 + [pltpu.SemaphoreType.DMA((2 * N_CHIPS * n,))]
    return out


def _pack_small(ssmall, dwq, dwkv, dconv, dfinal, dgq, dgkv, dattn, dconvg, loss_part):
    ssmall[...] = jnp.zeros_like(ssmall)
    rep = ssmall.at[0]
    for i in range(D_MODEL // LANES):
        rep[ROW_FINAL + i:ROW_FINAL + i + 1, :] = dfinal[:, LANES * i:LANES * (i + 1)]
    for i in range(Q_RANK // LANES):
        rep[ROW_GQ + i:ROW_GQ + i + 1, :] = dgq[:, LANES * i:LANES * (i + 1)]
    rep[ROW_GKV:ROW_GKV + 1, :] = dgkv[...]
    for i in range(CONV_WIDTH // LANES):
        rep[ROW_ATTN + i:ROW_ATTN + i + 1, :] = dattn[:, LANES * i:LANES * (i + 1)]
        rep[ROW_CONVG + i:ROW_CONVG + i + 1, :] = dconvg[:, LANES * i:LANES * (i + 1)]
    rep[ROW_LOSS:ROW_LOSS + 1, :] = loss_part[...]
    for k in range(N_DEV):
        if k:
            ssmall[k, ROW_REPL:, :] = ssmall[0, ROW_REPL:, :]
        for s, e, d in _q_pieces(k):
            for i in range(Q_RANK // LANES):
                ssmall[k, ROW_Q + SHARD_Q * i + s:ROW_Q + SHARD_Q * i + e, :] = dwq[d:d + e - s, LANES * i:LANES * (i + 1)]
        ssmall[k, ROW_KV:ROW_KV + KV_RANK, :] = dwkv[:, _kv_dst(k):_kv_dst(k) + SHARD_KV]
        ssmall[k, ROW_CONV:ROW_CONV + 3, 0:SHARD_CONV] = dconv[0:3, SHARD_CONV * k:SHARD_CONV * (k + 1)]


TOKEN_TILES = 4
TAIL_ROWS = N_META + D_MODEL // LANES


def _reduce_tail(r_in, r_out, r_small, d_meta, d_norm):
    n_p = len(PARAM_SHAPES)
    names = [n for n, _ in PARAM_SHAPES]

    def body(*refs):
        rin, rout, rsmall, dmeta, dnorm = refs[:5]
        g_out = {n: refs[5 + i] for i, n in enumerate(names)}
        loss_out = refs[5 + n_p]
        stail, rtail, gsum, gtail, send_sems, recv_sems = refs[6 + n_p:]
        x, y, c, me = _device_position()
        my_chip = 2 * x + y

        for k in range(N_DEV):
            stail[k, 0:N_META, :] = dmeta[:, SHARD_META * k:SHARD_META * (k + 1)]
            for i in range(D_MODEL // LANES):
                stail[k, N_META + i:N_META + i + 1, :] = dnorm[:, LANES * i:LANES * (i + 1)]
        copies = []
        for r in range(1, N_DEV):
            peer = (1 - x if r & 4 else x, 1 - y if r & 2 else y, 1 - c if r & 1 else c)
            copies.append(pltpu.make_async_remote_copy(
                src_ref=stail.at[4 * peer[0] + 2 * peer[1] + peer[2]],
                dst_ref=rtail.at[r],
                send_sem=send_sems.at[r - 1],
                recv_sem=recv_sems.at[r - 1],
                device_id=peer,
                device_id_type=pl.DeviceIdType.MESH,
            ))
        for cp in copies:
            cp.start()
        rtail[0] = stail[me]

        g = rin[my_chip].astype(F32)
        for ch in range(1, N_CHIPS):
            g = g + rin[ch ^ my_chip].astype(F32)
        g_out["w_in"][...] = g[:SHARD_IN, :]

        g = rout[my_chip].astype(F32)
        gs = rsmall[my_chip]
        for ch in range(1, N_CHIPS):
            g = g + rout[ch ^ my_chip].astype(F32)
            gs = gs + rsmall[ch ^ my_chip]
        g_out["w_out"][...] = g
        gsum[...] = gs
        for i in range(Q_RANK // LANES):
            g_out["w_q_up"][:, LANES * i:LANES * (i + 1)] = gsum[ROW_Q + SHARD_Q * i:ROW_Q + SHARD_Q * (i + 1), :]
        g_out["w_kv_up"][...] = gsum[ROW_KV:ROW_KV + KV_RANK, :]
        for i in range(3):
            g_out["conv_w"][i] = gsum[ROW_CONV + i:ROW_CONV + i + 1, 0:SHARD_CONV]
        for name, row, width in (("final_norm_g", ROW_FINAL, D_MODEL), ("q_norm_g", ROW_GQ, Q_RANK),
                                 ("kv_norm_g", ROW_GKV, KV_RANK), ("attn_out_g", ROW_ATTN, CONV_WIDTH),
                                 ("conv_out_g", ROW_CONVG, CONV_WIDTH)):
            for i in range(width // LANES):
                g_out[name][:, LANES * i:LANES * (i + 1)] = gsum[row + i:row + i + 1, :]
        loss_out[...] = gsum[ROW_LOSS:ROW_LOSS + 1, :]

        for cp in copies:
            cp.wait_recv()
        gt = rtail[me]
        for d in range(1, N_DEV):
            gt = gt + rtail[d ^ me]
        gtail[...] = gt
        g_out["meta_tokens"][...] = gtail[0:N_META, :]
        for i in range(D_MODEL // LANES):
            g_out["norm_g"][:, LANES * i:LANES * (i + 1)] = gtail[N_META + i:N_META + i + 1, :]
        for cp in copies:
            cp.wait_send()

    vm = pl.BlockSpec(memory_space=pltpu.VMEM)
    out_shape = [jax.ShapeDtypeStruct(shape, F32) for _, shape in PARAM_SHAPES]
    out_shape.append(jax.ShapeDtypeStruct((1, LANES), F32))
    outs = pl.pallas_call(
        body,
        name="reduce_tail",
        out_shape=tuple(out_shape),
        in_specs=[vm] * 5,
        out_specs=(vm,) * len(out_shape),
        scratch_shapes=[
            pltpu.VMEM((N_DEV, TAIL_ROWS, LANES), F32),
            pltpu.VMEM((N_DEV, TAIL_ROWS, LANES), F32),
            pltpu.VMEM((SMALL_ROWS, LANES), F32),
            pltpu.VMEM((TAIL_ROWS, LANES), F32),
            pltpu.SemaphoreType.DMA((N_DEV - 1,)),
            pltpu.SemaphoreType.DMA((N_DEV - 1,)),
        ],
        compiler_params=pltpu.CompilerParams(vmem_limit_bytes=VMEM_LIMIT),
    )(r_in, r_out, r_small, d_meta, d_norm)
    return {n: outs[i] for i, n in enumerate(names)}, outs[-1]


def _prep_in_proj(x, meta, norm_g, w_in_t, w_q, w_kv, w_out, conv_w, order):
    nb_seq, s, d = x.shape
    nb = s // LANES + 1
    m = nb_seq * nb * LANES
    n_real, n_norm = nb_seq * (nb - 1), nb_seq * nb
    n_tiles = IN_PAD // P_TILE
    _, n_early, n_free = _tile_orders()
    steps = n_norm + n_tiles
    dot_rows = m // 4
    qkv_shape = (SHARD_Q + KV_RANK, Q_RANK)

    def tile(t):
        t = jnp.minimum(t, n_norm - 1)
        jj, b = t // nb_seq, t % nb_seq
        return b, jnp.minimum(jj, nb - 2), b * nb + (jj + 1) % nb

    def column_tile(t, order_ref):
        return order_ref[jnp.maximum(t - n_norm, 0)]

    def body(order_ref, x_ref, meta_ref, g_ref, win_ref, wq_ref, wkv_ref, wout_ref, conv_ref,
             u_ref, p_ref, w_in_p, meta_f, w_q_p, w_kv_p, w_out_f, conv_f,
             u_all, sbig, gbig, smeta, gmeta, sqkv, sout, sconv, gqkv, gout, gconv,
             send_in, recv_in, local_in, send_meta, recv_meta, send_rest, recv_rest, local_rest):
        t = pl.program_id(0)
        px, py, pc, me = _device_position()

        def plan_in():
            return _gather_plan((sbig,), (gbig,), send_in, recv_in, local_in)

        def plan_rest():
            return _gather_plan((sqkv, sout, sconv), (gqkv, gout, gconv), send_rest, recv_rest, local_rest)

        def meta_copies():
            out = []
            for r in range(1, N_DEV):
                peer = (1 - px if r & 4 else px, 1 - py if r & 2 else py, 1 - pc if r & 1 else pc)
                out.append(pltpu.make_async_remote_copy(
                    src_ref=smeta,
                    dst_ref=gmeta.at[r],
                    send_sem=send_meta.at[r - 1],
                    recv_sem=recv_meta.at[r - 1],
                    device_id=peer,
                    device_id_type=pl.DeviceIdType.MESH,
                ))
            return out

        @pl.when(t == 0)
        def _():
            sbig[0:SHARD_IN, :] = win_ref[...].astype(BF16)
            sbig[SHARD_IN:, :] = jnp.zeros((SHARD_IN_PAD - SHARD_IN, d), BF16)
            smeta[...] = meta_ref[...]
            plan_in()[0]()
            for cp in meta_copies():
                cp.start()
            sqkv[...] = jnp.zeros_like(sqkv)
            sqkv[0:SHARD_Q, :] = wq_ref[...].astype(BF16)
            sqkv[SHARD_Q:, 0:SHARD_KV] = wkv_ref[...].astype(BF16)
            sout[...] = wout_ref[...].astype(BF16)
            sconv[...] = jnp.zeros_like(sconv)
            for i in range(3):
                sconv[i:i + 1, 0:SHARD_CONV] = conv_ref[i]

        def norm(h):
            hhat, _ = _rms_stats(h)
            return (hhat * g_ref[...]).astype(BF16)

        row0 = pl.multiple_of(tile(t)[2] * LANES, LANES)

        @pl.when(t < n_real)
        def _():
            un = norm(x_ref[0])
            u_ref[...] = un
            u_all[pl.ds(row0, LANES), :] = un

        @pl.when(t == n_real)
        def _():
            for cp in meta_copies():
                cp.wait_recv()
            gmeta[0] = smeta[...]
            for k in range(N_DEV):
                meta_f[:, SHARD_META * k:SHARD_META * (k + 1)] = gmeta[k ^ me]

        @pl.when((t >= n_real) & (t < n_norm))
        def _():
            un = jnp.concatenate([jnp.zeros((PAD_FRONT, d), BF16), norm(meta_f[...])], axis=0)
            u_ref[...] = un
            u_all[pl.ds(row0, LANES), :] = un

        @pl.when(t == n_norm)
        def _():
            plan_in()[1]()

        @pl.when(t == n_norm + n_early)
        def _():
            plan_in()[2]()
            plan_rest()[0]()

        @pl.when(t == n_norm + (n_early + n_free) // 2)
        def _():
            plan_rest()[1]()
            plan_rest()[2]()

        @pl.when(t == n_norm + n_free)
        def _():
            plan_in()[3]()

        @pl.when(t == n_norm + n_free + 2)
        def _():
            plan_rest()[3]()

        @pl.when(t >= n_norm)
        def _():
            j = column_tile(t, order_ref)
            for jj in range(n_tiles):
                @pl.when(j == jj)
                def _(jj=jj):
                    if jj == N_A // P_TILE:
                        w_in_p[N_A % P_TILE:, :] = jnp.zeros((P_TILE - N_A % P_TILE, d), BF16)
                    for k, s0, e0, d0 in _tile_pieces(jj):
                        w_in_p[d0:d0 + e0 - s0, :] = gbig[k, s0:e0, :]
            for r in range(m // dot_rows):
                rows = slice(dot_rows * r, dot_rows * (r + 1))
                p_ref[rows, :] = _dot(u_all[rows, :], w_in_p[...], _NT).astype(BF16)

        @pl.when(t == steps - 1)
        def _():
            plan_in()[4]()
            plan_rest()[4]()
            for cp in meta_copies():
                cp.wait_send()
            conv_f[...] = jnp.zeros_like(conv_f)
            for k in range(N_DEV):
                for s0, e0, d0 in _q_pieces(k):
                    w_q_p[d0:d0 + e0 - s0, :] = gqkv[k, s0:e0, :]
                w_kv_p[:, _kv_dst(k):_kv_dst(k) + SHARD_KV] = gqkv[k, SHARD_Q:, 0:SHARD_KV]
                w_out_f[SHARD_OUT * k:SHARD_OUT * (k + 1), :] = gout[k]
                conv_f[0:3, SHARD_CONV * k:SHARD_CONV * (k + 1)] = gconv[k, 0:3, 0:SHARD_CONV]

    whole = lambda shape: pl.BlockSpec(shape, lambda t, o: (0,) * len(shape))
    return pl.pallas_call(
        body,
        name="prep_in_proj_gather",
        grid_spec=pltpu.PrefetchScalarGridSpec(
            num_scalar_prefetch=1,
            grid=(steps,),
            in_specs=[
                pl.BlockSpec((1, LANES, d), lambda t, o: (tile(t)[0], tile(t)[1], 0)),
                whole(meta.shape), whole(norm_g.shape), whole(w_in_t.shape),
                whole(w_q.shape), whole(w_kv.shape), whole(w_out.shape), whole(conv_w.shape),
            ],
            out_specs=(pl.BlockSpec((LANES, d), lambda t, o: (tile(t)[2], 0)),
                       pl.BlockSpec((m, P_TILE), lambda t, o: (0, column_tile(t, o))),
                       pl.BlockSpec((P_TILE, d), lambda t, o: (column_tile(t, o), 0)),
                       whole((N_META, d)),
                       whole((Q_COLS, Q_RANK)), whole((KV_RANK, KV_COLS)), whole((D_MODEL, D_MODEL)),
                       whole((8, CONV_WIDTH))),
            scratch_shapes=[
                pltpu.VMEM((m, d), BF16),
                pltpu.VMEM((SHARD_IN_PAD, d), BF16),
                pltpu.VMEM((N_DEV, SHARD_IN_PAD, d), BF16),
                pltpu.VMEM((N_META, SHARD_META), F32),
                pltpu.VMEM((N_DEV, N_META, SHARD_META), F32),
                pltpu.VMEM(qkv_shape, BF16),
                pltpu.VMEM((SHARD_OUT, D_MODEL), BF16),
                pltpu.VMEM((8, LANES), F32),
                pltpu.VMEM((N_DEV,) + qkv_shape, BF16),
                pltpu.VMEM((N_DEV, SHARD_OUT, D_MODEL), BF16),
                pltpu.VMEM((N_DEV, 8, LANES), F32),
                pltpu.SemaphoreType.DMA((7,)),
                pltpu.SemaphoreType.DMA((7,)),
                pltpu.SemaphoreType.DMA((1,)),
                pltpu.SemaphoreType.DMA((N_DEV - 1,)),
                pltpu.SemaphoreType.DMA((N_DEV - 1,)),
                pltpu.SemaphoreType.DMA((21,)),
                pltpu.SemaphoreType.DMA((21,)),
                pltpu.SemaphoreType.DMA((3,)),
            ],
        ),
        out_shape=(jax.ShapeDtypeStruct((m, d), BF16),
                   jax.ShapeDtypeStruct((m, IN_PAD), BF16),
                   jax.ShapeDtypeStruct((IN_PAD, d), BF16),
                   jax.ShapeDtypeStruct((N_META, d), F32),
                   jax.ShapeDtypeStruct((Q_COLS, Q_RANK), BF16),
                   jax.ShapeDtypeStruct((KV_RANK, KV_COLS), BF16),
                   jax.ShapeDtypeStruct((D_MODEL, D_MODEL), BF16),
                   jax.ShapeDtypeStruct((8, CONV_WIDTH), F32)),
        compiler_params=_params("arbitrary"),
    )(order, x, meta, norm_g, w_in_t, w_q, w_kv, w_out, conv_w)


def _rope_tables(tp):
    half = D_ROPE // 2
    inv_freq = (1.0 / (ROPE_THETA ** (np.arange(half, dtype=np.float32) / half))).astype(np.float32)
    pos = (np.arange(tp) - PAD_FRONT).astype(np.float32)
    ang = pos[:, None] * inv_freq[None, :]
    cos = np.tile(np.cos(ang), (1, LANES // half))
    sin = np.tile(np.sin(ang), (1, LANES // half))
    first = (np.arange(LANES) % D_ROPE) < half
    zero = np.float32(0.0)
    return tuple(jnp.asarray(t, F32) for t in (cos, np.where(first, -sin, zero), np.where(first, zero, sin)))


def _rope(t, cos, sa, sb):
    return t * cos + pltpu.roll(t, LANES - D_ROPE // 2, 1) * sa + pltpu.roll(t, D_ROPE // 2, 1) * sb


def _rope_t(t, cos, sa, sb):
    return t * cos + pltpu.roll(t * sa, D_ROPE // 2, 1) + pltpu.roll(t * sb, LANES - D_ROPE // 2, 1)


def _qkv_fwd(p, wq, wkv, gq, gkv, tables, nb_seq, tp):
    ht = tp // 2

    def body(pa_ref, wq_ref, wkv_ref, gq_ref, gkv_ref, cos_ref, sa_ref, sb_ref, q_ref, k_ref, v_ref):
        pa = pa_ref[...].astype(F32)
        cq_hat, _ = _rms_stats(pa[:, :Q_RANK])
        ckv_hat, _ = _rms_stats(pa[:, Q_RANK:Q_RANK + KV_RANK])
        q = _dot((cq_hat * gq_ref[...]).astype(BF16), wq_ref[...], _NT) * Q_SCALE
        kv = _dot((ckv_hat * gkv_ref[...]).astype(BF16), wkv_ref[...])
        tabs = (cos_ref[...], sa_ref[...], sb_ref[...])
        lane = lax.broadcasted_iota(jnp.int32, (ht, LANES), 1)
        low = lane < D_ROPE
        mark = lane == D_ROPE
        row = (pl.program_id(0) % 2) * ht + lax.broadcasted_iota(jnp.int32, (ht, LANES), 0)
        k_pe = jnp.where(mark & (row < PAD_FRONT), NEG_INF, _rope(pa[:, Q_RANK + KV_RANK:], *tabs))
        one = jnp.where(mark & (row >= PAD_FRONT), 1.0, 0.0)
        pairs = [_rope(q[:, N_HEADS * D_NOPE + LANES * i:N_HEADS * D_NOPE + LANES * (i + 1)], *tabs) for i in range(2)]
        for h in range(N_HEADS):
            pair = pairs[h // 2]
            if h % 2:
                pair = pltpu.roll(pair, D_ROPE, 1)
            pe = jnp.where(low, pair, one)
            q_ref[0, h] = jnp.concatenate([q[:, D_NOPE * h:D_NOPE * (h + 1)], pe], axis=1).astype(BF16)
            k_ref[0, h] = jnp.concatenate([kv[:, D_NOPE * h:D_NOPE * (h + 1)], k_pe], axis=1).astype(BF16)
            v_ref[0, h] = kv[:, N_HEADS * D_NOPE + D_V * h:N_HEADS * D_NOPE + D_V * (h + 1)].astype(BF16)

    full = lambda a: pl.BlockSpec(a.shape, lambda i: (0,) * a.ndim)
    tab = pl.BlockSpec((ht, LANES), lambda i: (i % 2, 0))
    qk = pl.BlockSpec((1, N_HEADS, ht, 2 * LANES), lambda i: (i // 2, 0, i % 2, 0))
    return pl.pallas_call(
        body,
        name="qkv_fwd",
        grid=(2 * nb_seq,),
        in_specs=[pl.BlockSpec((ht, GRP_A), lambda i: (i, 0)), full(wq), full(wkv), full(gq), full(gkv), tab, tab, tab],
        out_specs=(qk, qk, pl.BlockSpec((1, N_HEADS, ht, D_V), lambda i: (i // 2, 0, i % 2, 0))),
        out_shape=(
            jax.ShapeDtypeStruct((nb_seq, N_HEADS, tp, 2 * LANES), BF16),
            jax.ShapeDtypeStruct((nb_seq, N_HEADS, tp, 2 * LANES), BF16),
            jax.ShapeDtypeStruct((nb_seq, N_HEADS, tp, D_V), BF16),
        ),
        compiler_params=_params("parallel"),
    )(p, wq, wkv, gq, gkv, *tables)


def _attn_fwd(q, k, v, p, g_attn):
    nb_seq, _, tp, _ = q.shape

    def body(q_ref, k_ref, v_ref, z_ref, g_ref, y_ref, o_ref, lse_ref):
        g = g_ref[...]
        for r0 in range(0, tp, Q_TILE):
            nq = min(Q_TILE, tp - r0)
            kend = r0 + nq
            qq = q_ref[0, 0, r0:kend, :]
            sd = _dot(qq, k_ref[0, 0, r0:kend, :], _NT)
            causal = (lax.broadcasted_iota(jnp.int32, (nq, nq), 1) <= lax.broadcasted_iota(jnp.int32, (nq, nq), 0))
            sd = jnp.where(causal, sd, NEG_INF)
            m = jnp.max(sd, axis=-1, keepdims=True)
            if r0:
                so = _dot(qq, k_ref[0, 0, 0:r0, :], _NT)
                m = jnp.maximum(m, jnp.max(so, axis=-1, keepdims=True))
            ed = jnp.exp2(sd - m)
            l = jnp.sum(ed, axis=-1, keepdims=True)
            o = _dot(ed.astype(BF16), v_ref[0, 0, r0:kend, :])
            if r0:
                eo = jnp.exp2(so - m)
                l = l + jnp.sum(eo, axis=-1, keepdims=True)
                o = o + _dot(eo.astype(BF16), v_ref[0, 0, 0:r0, :])
            o = o * (1.0 / l)
            o_ref[0, 0, r0:kend, :] = o
            lse_ref[0, 0, r0:kend, :] = jnp.broadcast_to(m + jnp.log2(l), (nq, LANES))
            ohat, _ = _rms_stats(o)
            z = z_ref[r0:kend, :].astype(F32)
            y_ref[r0:kend, :] = (ohat * g * (z * _sigmoid(z))).astype(BF16)

    qk = pl.BlockSpec((1, 1, tp, 2 * LANES), lambda b, h: (b, h, 0, 0))
    hv = pl.BlockSpec((1, 1, tp, D_V), lambda b, h: (b, h, 0, 0))
    return pl.pallas_call(
        body,
        name="attn_fwd",
        grid=(nb_seq, N_HEADS),
        in_specs=[qk, qk, hv,
                  pl.BlockSpec((tp, LANES), lambda b, h: (b, GRP_A // LANES + h)),
                  pl.BlockSpec((1, LANES), lambda b, h: (0, h))],
        out_specs=(pl.BlockSpec((tp, LANES), lambda b, h: (b, h)), hv, hv),
        out_shape=(
            jax.ShapeDtypeStruct((nb_seq * tp, N_HEADS * D_V), BF16),
            jax.ShapeDtypeStruct((nb_seq, N_HEADS, tp, D_V), F32),
            jax.ShapeDtypeStruct((nb_seq, N_HEADS, tp, LANES), F32),
        ),
        compiler_params=_params("parallel", "parallel"),
    )(q, k, v, p, g_attn)


_CONV_COL0 = (GRP_A + N_HEADS * D_V) // LANES


def _conv_specs(tp, order):
    cols = CONV_WIDTH // LANES
    return [pl.BlockSpec((tp, LANES), functools.partial(
        lambda a, b, off: order(a, b, off), off=_CONV_COL0 + i * cols)) for i in range(4)]


def _conv_fwd(p, conv_w, g_conv, nb_seq, tp):
    def body(b_ref, c_ref, h_ref, z_ref, w_ref, g_ref, y_ref):
        cc = c_ref[...].astype(F32) * h_ref[...].astype(F32)
        row = lax.broadcasted_iota(jnp.int32, (tp, LANES), 0)
        s1 = jnp.where(row >= 1, pltpu.roll(cc, 1, 0), 0.0)
        s2 = jnp.where(row >= 2, pltpu.roll(cc, 2, 0), 0.0)
        yc = b_ref[...].astype(F32) * (w_ref[0:1, :] * s2 + w_ref[1:2, :] * s1 + w_ref[2:3, :] * cc)
        r = lax.rsqrt(_group_mean(yc * yc) + EPS)
        z = z_ref[...].astype(F32)
        y_ref[...] = (yc * r * g_ref[...] * (z * _sigmoid(z))).astype(BF16)

    return pl.pallas_call(
        body,
        name="conv_fwd",
        grid=(nb_seq, CONV_WIDTH // LANES),
        in_specs=_conv_specs(tp, lambda b, t, off: (b, off + t)) + [
            pl.BlockSpec((8, LANES), lambda b, t: (0, t)),
            pl.BlockSpec((1, LANES), lambda b, t: (0, t))],
        out_specs=pl.BlockSpec((tp, LANES), lambda b, t: (b, t)),
        out_shape=jax.ShapeDtypeStruct((nb_seq * tp, CONV_WIDTH), BF16),
        compiler_params=_params("parallel", "parallel"),
    )(p, p, p, p, conv_w, g_conv)


def _token_copy(hbm, b, k, ts, buf, sem, to_hbm=False):
    lo, hi = max(k * ts - LANES, 0), (k + 1) * ts - LANES
    off = lo - (k * ts - LANES)
    src, dst = hbm.at[b, pl.ds(lo, hi - lo)], buf.at[pl.ds(off, hi - lo)]
    if to_hbm:
        src, dst = dst, src
    return pltpu.make_async_copy(src, dst, sem)


def _for_tile(k, nt, fn):
    for kk in range(nt):
        @pl.when(k == kk)
        def _(kk=kk):
            fn(kk)


def _out_proj_loss(ya, yc, w_out, x, target, g_final, nt):
    nb_seq, s, d = x.shape
    r, ka = ya.shape
    ts = (s + LANES) // nt
    steps = nb_seq * nt

    def body(a_ref, c_ref, w_ref, x_hbm, t_hbm, g_ref, dhb_ref, dg_ref, loss_ref,
             xbuf, tbuf, acc_ref, sems):
        i = pl.program_id(0)
        b, k = i // nt, i % nt

        @pl.when(i == 0)
        def _():
            acc_ref[...] = jnp.zeros_like(acc_ref)
            dg_ref[...] = jnp.zeros_like(dg_ref)

        slot = i % 2

        def fetch(seq, kk, sl):
            return [_token_copy(x_hbm, seq, kk, ts, xbuf.at[sl], sems.at[sl, 0]),
                    _token_copy(t_hbm, seq, kk, ts, tbuf.at[sl], sems.at[sl, 1])]

        def start(seq, sl, kk):
            if kk == 0:
                xbuf[sl, 0:LANES, :] = jnp.zeros((LANES, d), F32)
                tbuf[sl, 0:LANES, :] = jnp.zeros((LANES, d), F32)
            for cp in fetch(seq, kk, sl):
                cp.start()

        @pl.when(i == 0)
        def _():
            start(0, 0, 0)

        @pl.when(i + 1 < steps)
        def _():
            _for_tile((i + 1) % nt, nt, functools.partial(start, (i + 1) // nt, 1 - slot))

        mix = _dot(a_ref[...], w_ref[0:ka, :]) + _dot(c_ref[...], w_ref[ka:, :])
        _for_tile(k, nt, lambda kk: [cp.wait() for cp in fetch(b, kk, slot)])

        real = (lax.broadcasted_iota(jnp.int32, (ts, d), 0) >= LANES) | (k > 0)
        g = g_ref[...]
        hhat, rstd = _rms_stats(xbuf[slot] + mix)
        e = jnp.where(real, hhat * g - tbuf[slot], 0.0)
        acc_ref[...] += jnp.sum(e * e, axis=0, keepdims=True)
        dy = e * (1.0 / d)
        dg_ref[...] += jnp.sum(dy * hhat, axis=0, keepdims=True)
        dhb_ref[...] = _rms_bwd(g * dy, hhat, rstd).astype(BF16)

        @pl.when(i == steps - 1)
        def _():
            total = jnp.sum(acc_ref[...], axis=1, keepdims=True)
            loss_ref[...] = jnp.broadcast_to((0.5 / d) * total, loss_ref.shape)

    hbm = pl.BlockSpec(memory_space=pl.ANY)
    row = pl.BlockSpec((ts, d), lambda i: (i, 0))
    vec = pl.BlockSpec((1, d), lambda i: (0, 0))
    return pl.pallas_call(
        body,
        name="out_proj_loss",
        grid=(steps,),
        in_specs=[pl.BlockSpec((ts, ka), lambda i: (i, 0)), pl.BlockSpec((ts, yc.shape[1]), lambda i: (i, 0)),
                  pl.BlockSpec(w_out.shape, lambda i: (0, 0)), hbm, hbm, vec],
        out_specs=(row, vec, pl.BlockSpec((1, LANES), lambda i: (0, 0))),
        out_shape=(
            jax.ShapeDtypeStruct((r, d), BF16),
            jax.ShapeDtypeStruct((1, d), F32),
            jax.ShapeDtypeStruct((1, LANES), F32),
        ),
        scratch_shapes=[pltpu.VMEM((2, ts, d), F32), pltpu.VMEM((2, ts, d), F32), pltpu.VMEM((1, d), F32),
                        pltpu.SemaphoreType.DMA((2, 2))],
        compiler_params=_params("arbitrary"),
    )(ya, yc, w_out, x, target, g_final)


def _out_proj_bwd(dhb, w_out, ya, yc, bm):
    r, d = dhb.shape
    ka = ya.shape[1]
    n_mix = w_out.shape[0]
    last = r // bm - 1

    def body(dh_ref, w_ref, a_ref, c_ref, dcat_ref, dw_ref, acc_ref):
        @pl.when(pl.program_id(0) == 0)
        def _():
            acc_ref[...] = jnp.zeros_like(acc_ref)

        dh = dh_ref[...]
        dcat_ref[...] = _dot(dh, w_ref[...], _NT).astype(BF16)
        acc_ref[0:ka, :] += _dot(a_ref[...], dh, _TN)
        acc_ref[ka:, :] += _dot(c_ref[...], dh, _TN)

        @pl.when(pl.program_id(0) == last)
        def _():
            dw_ref[...] = acc_ref[...].astype(BF16)

    return pl.pallas_call(
        body,
        name="out_proj_bwd",
        grid=(r // bm,),
        in_specs=[pl.BlockSpec((bm, d), lambda i: (i, 0)), pl.BlockSpec(w_out.shape, lambda i: (0, 0)),
                  pl.BlockSpec((bm, ka), lambda i: (i, 0)), pl.BlockSpec((bm, yc.shape[1]), lambda i: (i, 0))],
        out_specs=(pl.BlockSpec((bm, n_mix), lambda i: (i, 0)),
                   pl.BlockSpec((n_mix, d), lambda i: (0, 0))),
        out_shape=(jax.ShapeDtypeStruct((r, n_mix), BF16),
                   jax.ShapeDtypeStruct((n_mix, d), BF16)),
        scratch_shapes=[pltpu.VMEM((n_mix, d), F32)],
        compiler_params=_params("arbitrary"),
    )(dhb, w_out, ya, yc)


def _attn_bwd(q, k, v, o, lse, dcat, p, g_attn):
    nb_seq, _, tp, _ = q.shape

    def body(q_ref, k_ref, v_ref, o_ref, lse_ref, dy_ref, z_ref, g_ref,
             dq_ref, dk_ref, dv_ref, dz_ref, dg_ref, dq_acc):
        @pl.when(pl.program_id(1) == 0)
        def _():
            dg_ref[...] = jnp.zeros_like(dg_ref)

        g = g_ref[...]
        z = z_ref[...].astype(F32)
        o = o_ref[0, 0]
        dy = dy_ref[...].astype(F32)
        sig = _sigmoid(z)
        ohat, r = _rms_stats(o)
        don = dy * (z * sig)
        dz_ref[...] = (dy * (ohat * g) * (sig * (1.0 + z * (1.0 - sig)))).astype(BF16)
        dg_ref[...] += jnp.sum(don * ohat, axis=0, keepdims=True)
        do = _rms_bwd(g * don, ohat, r)
        dvec = jnp.sum(do * o, axis=-1, keepdims=True)
        dob = do.astype(BF16)
        lse_col = lse_ref[0, 0, :, 0:1]
        dq_acc[...] = jnp.zeros_like(dq_acc)
        for k0 in range(0, tp, K_TILE):
            nk = min(K_TILE, tp - k0)
            nq = tp - k0
            qq = q_ref[0, 0, k0:, :]
            kk = k_ref[0, 0, k0:k0 + nk, :]
            causal = (lax.broadcasted_iota(jnp.int32, (nq, nk), 1) <= lax.broadcasted_iota(jnp.int32, (nq, nk), 0))
            pr = jnp.where(causal, jnp.exp2(_dot(qq, kk, _NT) - lse_col[k0:]), 0.0)
            dp = _dot(dob[k0:], v_ref[0, 0, k0:k0 + nk, :], _NT)
            ds = (pr * (dp - dvec[k0:])).astype(BF16)
            dv_ref[0, 0, k0:k0 + nk, :] = _dot(pr.astype(BF16), dob[k0:], _TN).astype(BF16)
            dk_ref[0, 0, k0:k0 + nk, :] = (_dot(ds, qq, _TN) * (ATTN_SCALE / Q_SCALE)).astype(BF16)
            dq_acc[k0:, :] += _dot(ds, kk)
        dq_ref[0, 0] = (dq_acc[...] * ATTN_SCALE).astype(BF16)

    qk = pl.BlockSpec((1, 1, tp, 2 * LANES), lambda h, b: (b, h, 0, 0))
    hv = pl.BlockSpec((1, 1, tp, D_V), lambda h, b: (b, h, 0, 0))
    col = pl.BlockSpec((tp, LANES), lambda h, b: (b, h))
    return pl.pallas_call(
        body,
        name="attn_bwd",
        grid=(N_HEADS, nb_seq),
        in_specs=[qk, qk, hv, hv, hv, col,
                  pl.BlockSpec((tp, LANES), lambda h, b: (b, GRP_A // LANES + h)),
                  pl.BlockSpec((1, LANES), lambda h, b: (0, h))],
        out_specs=(qk, qk, hv, col, pl.BlockSpec((1, LANES), lambda h, b: (0, h))),
        out_shape=(
            jax.ShapeDtypeStruct((nb_seq, N_HEADS, tp, 2 * LANES), BF16),
            jax.ShapeDtypeStruct((nb_seq, N_HEADS, tp, 2 * LANES), BF16),
            jax.ShapeDtypeStruct((nb_seq, N_HEADS, tp, D_V), BF16),
            jax.ShapeDtypeStruct((nb_seq * tp, N_HEADS * D_V), BF16),
            jax.ShapeDtypeStruct((1, N_HEADS * D_V), F32),
        ),
        scratch_shapes=[pltpu.VMEM((tp, 2 * LANES), F32)],
        compiler_params=_params("arbitrary", "arbitrary"),
    )(q, k, v, o, lse, dcat, p, g_attn)


def _qkv_bwd(p, dq, dk, dv, wq, wkv, gq, gkv, tables):
    nb_seq, _, tp, _ = dq.shape
    ht = tp // 2

    def body(pa_ref, dq_ref, dk_ref, dv_ref, wq_ref, wkv_ref, gq_ref, gkv_ref, cos_ref, sa_ref, sb_ref,
             dpa_ref, dwq_ref, dwkv_ref, dgq_ref, dgkv_ref):
        @pl.when(pl.program_id(0) == 0)
        def _():
            dwq_ref[...] = jnp.zeros_like(dwq_ref)
            dwkv_ref[...] = jnp.zeros_like(dwkv_ref)
            dgq_ref[...] = jnp.zeros_like(dgq_ref)
            dgkv_ref[...] = jnp.zeros_like(dgkv_ref)

        pa = pa_ref[...].astype(F32)
        gq, gkv = gq_ref[...], gkv_ref[...]
        cq_hat, rq = _rms_stats(pa[:, :Q_RANK])
        ckv_hat, rkv = _rms_stats(pa[:, Q_RANK:Q_RANK + KV_RANK])
        tabs = (cos_ref[...], sa_ref[...], sb_ref[...])

        pe = [dq_ref[0, h, :, D_NOPE:].astype(F32) for h in range(N_HEADS)]
        pairs = [_rope_t(pe[2 * i] + pltpu.roll(pe[2 * i + 1], D_ROPE, 1), *tabs).astype(BF16) for i in range(2)]
        dq_flat = jnp.concatenate([dq_ref[0, h, :, :D_NOPE] for h in range(N_HEADS)] + pairs, axis=1)
        dwq_ref[...] += _dot(dq_flat, (cq_hat * gq).astype(BF16), _TN)
        dcqn = _dot(dq_flat, wq_ref[...])
        dgq_ref[...] += jnp.sum(dcqn * cq_hat, axis=0, keepdims=True)
        dcq = _rms_bwd(gq * dcqn, cq_hat, rq)

        dkv_flat = jnp.concatenate([dk_ref[0, h, :, :D_NOPE] for h in range(N_HEADS)]
                                   + [dv_ref[0, h] for h in range(N_HEADS)], axis=1)
        dwkv_ref[...] += _dot((ckv_hat * gkv).astype(BF16), dkv_flat, _TN)
        dckvn = _dot(dkv_flat, wkv_ref[...], _NT)
        dgkv_ref[...] += jnp.sum(dckvn * ckv_hat, axis=0, keepdims=True)
        dckv = _rms_bwd(gkv * dckvn, ckv_hat, rkv)

        dk_pe = dk_ref[0, 0, :, D_NOPE:].astype(F32)
        for h in range(1, N_HEADS):
            dk_pe = dk_pe + dk_ref[0, h, :, D_NOPE:].astype(F32)
        dk_pe = jnp.where(lax.broadcasted_iota(jnp.int32, (ht, LANES), 1) < D_ROPE, dk_pe, 0.0)
        dpa_ref[...] = jnp.concatenate([dcq, dckv, _rope_t(dk_pe, *tabs)], axis=1).astype(BF16)

    full = lambda a: pl.BlockSpec(a.shape, lambda i: (0,) * a.ndim)
    tab = pl.BlockSpec((ht, LANES), lambda i: (i % 2, 0))
    qk = pl.BlockSpec((1, N_HEADS, ht, 2 * LANES), lambda i: (i // 2, 0, i % 2, 0))
    acc = lambda shape: pl.BlockSpec(shape, lambda i: (0, 0))
    return pl.pallas_call(
        body,
        name="qkv_bwd",
        grid=(2 * nb_seq,),
        in_specs=[pl.BlockSpec((ht, GRP_A), lambda i: (i, 0)), qk, qk,
                  pl.BlockSpec((1, N_HEADS, ht, D_V), lambda i: (i // 2, 0, i % 2, 0)),
                  full(wq), full(wkv), full(gq), full(gkv), tab, tab, tab],
        out_specs=(pl.BlockSpec((ht, GRP_A), lambda i: (i, 0)),
                   acc(wq.shape), acc(wkv.shape), acc((1, Q_RANK)), acc((1, KV_RANK))),
        out_shape=(
            jax.ShapeDtypeStruct((nb_seq * tp, GRP_A), BF16),
            jax.ShapeDtypeStruct(wq.shape, F32),
            jax.ShapeDtypeStruct(wkv.shape, F32),
            jax.ShapeDtypeStruct((1, Q_RANK), F32),
            jax.ShapeDtypeStruct((1, KV_RANK), F32),
        ),
        compiler_params=_params("arbitrary"),
    )(p, dq, dk, dv, wq, wkv, gq, gkv, *tables)


def _conv_bwd(p, dcat, conv_w, g_conv, nb_seq, tp):
    cols = CONV_WIDTH // LANES

    def body(b_ref, c_ref, h_ref, z_ref, dy_ref, w_ref, g_ref,
             db_ref, dc_ref, dh_ref, dz_ref, dw_ref, dg_ref):
        @pl.when(pl.program_id(1) == 0)
        def _():
            dw_ref[...] = jnp.zeros_like(dw_ref)
            dg_ref[...] = jnp.zeros_like(dg_ref)

        cb, c, h = b_ref[...].astype(F32), c_ref[...].astype(F32), h_ref[...].astype(F32)
        z, dy = z_ref[...].astype(F32), dy_ref[...].astype(F32)
        g = g_ref[...]
        w0, w1, w2 = w_ref[0:1, :], w_ref[1:2, :], w_ref[2:3, :]
        cc = c * h
        row = lax.broadcasted_iota(jnp.int32, (tp, LANES), 0)
        s1 = jnp.where(row >= 1, pltpu.roll(cc, 1, 0), 0.0)
        s2 = jnp.where(row >= 2, pltpu.roll(cc, 2, 0), 0.0)
        dwc = w0 * s2 + w1 * s1 + w2 * cc
        yc = cb * dwc
        r = lax.rsqrt(_group_mean(yc * yc) + EPS)
        ychat = yc * r
        sig = _sigmoid(z)
        dz_ref[...] = (dy * (ychat * g) * (sig * (1.0 + z * (1.0 - sig)))).astype(BF16)
        dyn = dy * (z * sig)
        dg_ref[...] += jnp.sum(dyn * ychat, axis=0, keepdims=True)
        gd = g * dyn
        dyc = r * (gd - ychat * _group_mean(gd * ychat))
        db_ref[...] = (dyc * dwc).astype(BF16)
        ddw = dyc * cb
        dw_ref[0:1, :] += jnp.sum(ddw * s2, axis=0, keepdims=True)
        dw_ref[1:2, :] += jnp.sum(ddw * s1, axis=0, keepdims=True)
        dw_ref[2:3, :] += jnp.sum(ddw * cc, axis=0, keepdims=True)
        u1 = jnp.where(row <= tp - 2, pltpu.roll(ddw, tp - 1, 0), 0.0)
        u2 = jnp.where(row <= tp - 3, pltpu.roll(ddw, tp - 2, 0), 0.0)
        dcc = w2 * ddw + w1 * u1 + w0 * u2
        dc_ref[...] = (dcc * h).astype(BF16)
        dh_ref[...] = (dcc * c).astype(BF16)

    col = pl.BlockSpec((tp, LANES), lambda t, b: (b, t))
    out = jax.ShapeDtypeStruct((nb_seq * tp, CONV_WIDTH), BF16)
    return pl.pallas_call(
        body,
        name="conv_bwd",
        grid=(cols, nb_seq),
        in_specs=_conv_specs(tp, lambda t, b, off: (b, off + t)) + [
            pl.BlockSpec((tp, LANES), lambda t, b: (b, N_HEADS * D_V // LANES + t)),
            pl.BlockSpec((8, LANES), lambda t, b: (0, t)),
            pl.BlockSpec((1, LANES), lambda t, b: (0, t))],
        out_specs=(col, col, col, col,
                   pl.BlockSpec((8, LANES), lambda t, b: (0, t)), pl.BlockSpec((1, LANES), lambda t, b: (0, t))),
        out_shape=(out, out, out, out,
                   jax.ShapeDtypeStruct((8, CONV_WIDTH), F32), jax.ShapeDtypeStruct((1, CONV_WIDTH), F32)),
        compiler_params=_params("arbitrary", "arbitrary"),
    )(p, p, p, p, dcat, conv_w, g_conv)


def _input_bwd(dps, w_in, x, meta, dh, norm_g, nt, send_in):
    nb_seq, s, d = x.shape
    r, kb = dps[0].shape
    ts = (s + LANES) // nt
    steps = nb_seq * nt
    n_dp = len(dps)
    in_slot = send_in.shape[1:]

    def body(*refs):
        dp_refs, w_ref, x_hbm, meta_ref, dh_ref, g_ref, pay_ref = refs[:n_dp], *refs[n_dp:n_dp + 6]
        o = n_dp + 6
        gx_hbm, dmeta_ref, dg_ref, r2_in = refs[o:o + 4]
        xbuf, gxbuf, tok_sems, own_in, r1_in, sum_in = refs[o + 4:o + 10]
        sems = refs[o + 10:]
        i = pl.program_id(0)
        b, k = i // nt, i % nt

        def plan():
            return _reduce_plan((pay_ref,), (own_in,), (r1_in,), (sum_in,), (r2_in,), *sems)

        @pl.when(i == 0)
        def _():
            dmeta_ref[...] = jnp.zeros_like(dmeta_ref)
            dg_ref[...] = jnp.zeros_like(dg_ref)
            plan()[0]()

        @pl.when(i == 1)
        def _():
            plan()[1]()

        def start(kk):
            if kk == 0:
                xbuf[0:PAD_FRONT, :] = jnp.zeros((PAD_FRONT, d), F32)
                xbuf[PAD_FRONT:LANES, :] = meta_ref[...]
            _token_copy(x_hbm, b, kk, ts, xbuf, tok_sems.at[0]).start()

        _for_tile(k, nt, start)
        du = _dot(dp_refs[0][...], w_ref[0:kb, :])
        for j in range(1, n_dp):
            du = du + _dot(dp_refs[j][...], w_ref[kb * j:kb * (j + 1), :])
        _for_tile(k, nt, lambda kk: _token_copy(x_hbm, b, kk, ts, xbuf, tok_sems.at[0]).wait())

        g = g_ref[...]
        hhat, rstd = _rms_stats(xbuf[...])
        dg_ref[...] += jnp.sum(du * hhat, axis=0, keepdims=True)
        res = _rms_bwd(g * du, hhat, rstd) + dh_ref[...].astype(F32)

        @pl.when(i > 0)
        def _():
            _for_tile(k, nt, lambda kk: _token_copy(gx_hbm, b, (kk - 1) % nt, ts, gxbuf, tok_sems.at[1], True).wait())

        gxbuf[...] = res

        @pl.when(k == 0)
        def _():
            dmeta_ref[...] += gxbuf[PAD_FRONT:LANES, :]

        _for_tile(k, nt, lambda kk: _token_copy(gx_hbm, b, kk, ts, gxbuf, tok_sems.at[1], True).start())

        @pl.when(i == steps - 1)
        def _():
            _token_copy(gx_hbm, b, nt - 1, ts, gxbuf, tok_sems.at[1], True).wait()
            plan()[2]()

    whole = lambda a: pl.BlockSpec(a.shape, lambda i: (0,) * a.ndim)
    hbm = pl.BlockSpec(memory_space=pl.ANY)
    return pl.pallas_call(
        body,
        name="input_bwd",
        grid=(steps,),
        in_specs=[pl.BlockSpec((ts, kb), lambda i: (i, 0)) for _ in dps]
        + [whole(w_in), hbm, whole(meta), pl.BlockSpec((ts, d), lambda i: (i, 0)), whole(norm_g), hbm],
        out_specs=(hbm, pl.BlockSpec((N_META, d), lambda i: (0, 0)), pl.BlockSpec((1, d), lambda i: (0, 0)), hbm),
        out_shape=(jax.ShapeDtypeStruct((nb_seq, s, d), F32),
                   jax.ShapeDtypeStruct((N_META, d), F32),
                   jax.ShapeDtypeStruct((1, d), F32),
                   jax.ShapeDtypeStruct((N_CHIPS,) + in_slot, BF16)),
        scratch_shapes=[pltpu.VMEM((ts, d), F32), pltpu.VMEM((ts, d), F32), pltpu.SemaphoreType.DMA((2,))]
        + _reduce_scratch([(in_slot, BF16)], [True]),
        compiler_params=_params("arbitrary"),
    )(*dps, w_in, x, meta, dh, norm_g, send_in)


def _in_proj_bwd_w(u, dps, bm, small_grads, send_out):
    r, d = u.shape
    kb = dps[0].shape[1]
    steps = r // bm
    n_dp, n_small = len(dps), len(small_grads)
    out_slot, small_slot = send_out.shape[1:], (SMALL_ROWS, LANES)

    def body(*refs):
        u_ref, dp_refs = refs[0], refs[1:1 + n_dp]
        small_refs = refs[1 + n_dp:1 + n_dp + n_small]
        o = 1 + n_dp + n_small
        pay_out, o_ref, r2_out, r2_small = refs[o:o + 4]
        acc_ref, ssmall, r1_out, sum_out, r1_small, sum_small = refs[o + 4:o + 10]
        sems = refs[o + 10:]
        i = pl.program_id(0)

        def plan():
            return _reduce_plan((pay_out, ssmall), (None, None), (r1_out, r1_small), (sum_out, sum_small),
                                (r2_out, r2_small), *sems)

        @pl.when(i == 0)
        def _():
            acc_ref[...] = jnp.zeros_like(acc_ref)
            _pack_small(ssmall, *small_refs)
            plan()[0]()

        @pl.when(i == 1)
        def _():
            plan()[1]()

        uu = u_ref[...]
        for j in range(n_dp):
            acc_ref[kb * j:kb * (j + 1), :] += _dot(dp_refs[j][...], uu, _TN)

        @pl.when(i == steps - 1)
        def _():
            for k in range(N_DEV):
                for s, e, c0 in _in_pieces(k):
                    o_ref[k, s:e, :] = acc_ref[c0:c0 + e - s, :].astype(BF16)
                o_ref[k, SHARD_IN:, :] = jnp.zeros((SHARD_IN_PAD - SHARD_IN, d), BF16)
            plan()[2]()

    whole = lambda a: pl.BlockSpec(a.shape, lambda i: (0,) * a.ndim)
    hbm = pl.BlockSpec(memory_space=pl.ANY)
    return pl.pallas_call(
        body,
        name="in_proj_bwd_w",
        grid=(steps,),
        in_specs=[pl.BlockSpec((bm, d), lambda i: (i, 0))]
        + [pl.BlockSpec((bm, kb), lambda i: (i, 0)) for _ in dps] + [whole(a) for a in small_grads]
        + [whole(send_out)],
        out_specs=(pl.BlockSpec((N_DEV, SHARD_IN_PAD, d), lambda i: (0, 0, 0)), hbm, hbm),
        out_shape=(jax.ShapeDtypeStruct((N_DEV, SHARD_IN_PAD, d), BF16),
                   jax.ShapeDtypeStruct((N_CHIPS,) + out_slot, BF16),
                   jax.ShapeDtypeStruct((N_CHIPS,) + small_slot, F32)),
        scratch_shapes=[pltpu.VMEM((kb * n_dp, d), F32), pltpu.VMEM((N_DEV,) + small_slot, F32)]
        + _reduce_scratch([(out_slot, BF16), (small_slot, F32)], [False, False]),
        compiler_params=_params("arbitrary"),
    )(u, *dps, *small_grads, send_out)


def _local_step(x, loss_target, u, p, meta_f, norm_g, w_in_p, q_norm_g, w_q_p, kv_norm_g, w_kv_p, conv_w_f,
                attn_out_g, conv_out_g, w_out_f, g_final):
    nb_seq, s, d = x.shape
    tp = s + LANES
    ht = tp // 2
    tables = _rope_tables(tp)

    q, k, v = _qkv_fwd(p, w_q_p, w_kv_p, q_norm_g, kv_norm_g, tables, nb_seq, tp)
    ya, o, lse = _attn_fwd(q, k, v, p, attn_out_g)
    yc = _conv_fwd(p, conv_w_f, conv_out_g, nb_seq, tp)
    dhb, d_final_g, loss_part = _out_proj_loss(ya, yc, w_out_f, x, loss_target, g_final, TOKEN_TILES)

    dcat, d_w_out = _out_proj_bwd(dhb, w_out_f, ya, yc, ht)
    send_out = d_w_out.reshape(N_DEV, SHARD_OUT, d)
    dq, dk, dv, dz_attn, d_attn_g = _attn_bwd(q, k, v, o, lse, dcat, p, attn_out_g)
    dpa, d_wq_p, d_wkv_p, d_gq, d_gkv = _qkv_bwd(p, dq, dk, dv, w_q_p, w_kv_p, q_norm_g, kv_norm_g, tables)
    d_b, d_c, d_h, dz_conv, d_conv_w, d_conv_g = _conv_bwd(p, dcat, conv_w_f, conv_out_g, nb_seq, tp)
    dps = (dpa, dz_attn, d_b, d_c, d_h, dz_conv)
    small = (d_wq_p, d_wkv_p, d_conv_w, d_final_g, d_gq, d_gkv, d_attn_g, d_conv_g, loss_part)
    send_in, r_out, r_small = _in_proj_bwd_w(u, dps, ht, small, send_out)
    grad_x, d_meta, d_norm_g, r_in = _input_bwd(dps, w_in_p, x, meta_f, dhb, norm_g, TOKEN_TILES, send_in)
    return grad_x, r_in, r_out, r_small, d_meta, d_norm_g


def kernel(x, meta_tokens, norm_g, w_in, q_norm_g, w_q_up, kv_norm_g, w_kv_up, conv_w, attn_out_g, conv_out_g, w_out, final_norm_g, loss_target, m_meta_tokens, m_norm_g, m_w_in, m_q_norm_g, m_w_q_up, m_kv_norm_g, m_w_kv_up, m_conv_w, m_attn_out_g, m_conv_out_g, m_w_out, m_final_norm_g, v_meta_tokens, v_norm_g, v_w_in, v_q_norm_g, v_w_q_up, v_kv_norm_g, v_w_kv_up, v_conv_w, v_attn_out_g, v_conv_out_g, v_w_out, v_final_norm_g):
    d = x.shape[-1]
    order = jnp.asarray(_tile_orders()[0])[2 * lax.axis_index("x") + lax.axis_index("y")]
    u, p, w_in_p, meta_f, w_q_p, w_kv_p, w_out_f, conv_w_f = _prep_in_proj(
        x, meta_tokens, norm_g, w_in[0].T, w_q_up[0].T, w_kv_up[0], w_out[0], conv_w.transpose(1, 0, 2), order)
    g_final = final_norm_g.reshape(1, d)
    grad_x, r_in, r_out, r_small, d_meta, d_norm_g = _local_step(
        x, loss_target, u, p, meta_f, norm_g, w_in_p, q_norm_g, w_q_p, kv_norm_g, w_kv_p, conv_w_f,
        attn_out_g, conv_out_g, w_out_f, g_final)

    flat = lambda a: a.reshape(a.shape[-2:]) if a.ndim == 3 else a.reshape(1, -1) if a.ndim == 1 else a
    transposed = ("w_in", "w_q_up")

    def to_kernel(n, a):
        if n == "conv_w":
            return a.transpose(1, 0, 2)
        return flat(a).T if n in transposed else flat(a)

    def from_kernel(n, a, shape):
        if n == "conv_w":
            return a.transpose(1, 0, 2)
        return (a.T if n in transposed else a).reshape(shape)
    params = {
        "meta_tokens": (meta_tokens, m_meta_tokens, v_meta_tokens),
        "norm_g": (norm_g, m_norm_g, v_norm_g),
        "w_in": (w_in, m_w_in, v_w_in),
        "q_norm_g": (q_norm_g, m_q_norm_g, v_q_norm_g),
        "w_q_up": (w_q_up, m_w_q_up, v_w_q_up),
        "kv_norm_g": (kv_norm_g, m_kv_norm_g, v_kv_norm_g),
        "w_kv_up": (w_kv_up, m_w_kv_up, v_w_kv_up),
        "conv_w": (conv_w, m_conv_w, v_conv_w),
        "attn_out_g": (attn_out_g, m_attn_out_g, v_attn_out_g),
        "conv_out_g": (conv_out_g, m_conv_out_g, v_conv_out_g),
        "w_out": (w_out, m_w_out, v_w_out),
        "final_norm_g": (final_norm_g, m_final_norm_g, v_final_norm_g),
    }
    grads, loss = _reduce_tail(r_in, r_out, r_small, d_meta, d_norm_g)
    updated = _adamw(grads, {n: tuple(to_kernel(n, a) for a in t) for n, t in params.items()})
    outs = [[from_kernel(n, updated[n][i], params[n][0].shape) for n, _ in PARAM_SHAPES] for i in range(4)]
    return (loss[0, 0], grad_x, *outs[0], *outs[1], *outs[2], *outs[3])
```

```python
import functools

import jax
import jax.numpy as jnp
import numpy as np
from jax import lax
from jax.experimental import pallas as pl
from jax.experimental.pallas import tpu as pltpu

F32 = jnp.float32
BF16 = jnp.bfloat16

N_META = 16
D_MODEL = 1024
N_HEADS = 4
D_NOPE = 128
D_ROPE = 64
D_V = 128
Q_RANK = 256
KV_RANK = 128
CONV_WIDTH = 512
CONV_GROUP = 64
ROPE_THETA = 10000.0
ATTN_SCALE = (D_NOPE + D_ROPE) ** -0.5
Q_SCALE = ATTN_SCALE * 1.4426950408889634
EPS = 1e-6
NEG_INF = -1e30

ADAM_LR = 0.001
ADAM_B1 = 0.9
ADAM_B2 = 0.999
ADAM_EPS = 1e-08
ADAM_WD = 0.01
ADAM_STEP = 10

LANES = 128
PAD_FRONT = LANES - N_META
K_TILE = 256
Q_TILE = 512
N_DEV = 8
VMEM_LIMIT = 56 * 1024 * 1024

IN_PAD = 3072
GRP_A = 512
N_A = Q_RANK + KV_RANK + D_ROPE
IN_PROJ = 3008
SHARD_IN = IN_PROJ // N_DEV
SHARD_IN_PAD = 384
SHARD_Q = 96
SHARD_KV = 128
SHARD_OUT = 128
SHARD_CONV = 64
SHARD_META = 128
Q_COLS = N_HEADS * (D_NOPE + D_ROPE)
KV_COLS = N_HEADS * (D_NOPE + D_V)

ROW_Q, ROW_KV, ROW_META, ROW_CONV = 0, 256, 384, 400
ROW_REPL = 408
ROW_NORM, ROW_FINAL, ROW_GQ, ROW_GKV, ROW_ATTN, ROW_CONVG, ROW_LOSS = 408, 416, 424, 426, 427, 431, 435
SMALL_ROWS = 440

PARAM_SHAPES = (
    ("meta_tokens", (N_META, SHARD_META)), ("norm_g", (1, D_MODEL)), ("w_in", (SHARD_IN, D_MODEL)),
    ("q_norm_g", (1, Q_RANK)), ("w_q_up", (SHARD_Q, Q_RANK)), ("kv_norm_g", (1, KV_RANK)),
    ("w_kv_up", (KV_RANK, SHARD_KV)), ("conv_w", (3, 1, SHARD_CONV)), ("attn_out_g", (1, CONV_WIDTH)),
    ("conv_out_g", (1, CONV_WIDTH)), ("w_out", (SHARD_OUT, D_MODEL)), ("final_norm_g", (1, D_MODEL)),
)


def _in_pieces(k):
    lo, hi = SHARD_IN * k, SHARD_IN * (k + 1)
    out = []
    if lo < N_A:
        out.append((0, min(hi, N_A) - lo, lo))
    if hi > N_A:
        s = max(lo, N_A)
        out.append((s - lo, hi - lo, s + GRP_A - N_A))
    return out


P_TILE = 256


def _tile_pieces(j):
    lo, hi = P_TILE * j, P_TILE * (j + 1)
    out = []
    for k in range(N_DEV):
        for s, e, d in _in_pieces(k):
            a, b = max(d, lo), min(d + e - s, hi)
            if a < b:
                out.append((k, s + a - d, s + b - d, a - lo))
    return out


def _tile_orders():
    n_tiles = IN_PAD // P_TILE
    sources = [{k for k, _, _, _ in _tile_pieces(j)} for j in range(n_tiles)]
    rows, n_early, n_free = [], n_tiles, n_tiles
    for chip in range(N_CHIPS):
        own = {2 * chip, 2 * chip + 1}
        diagonal = {2 * (N_CHIPS - 1 - chip), 2 * (N_CHIPS - 1 - chip) + 1}
        early = [j for j in range(n_tiles) if sources[j] <= own]
        late = [j for j in range(n_tiles) if sources[j] & diagonal]
        mid = [j for j in range(n_tiles) if j not in early and j not in late]
        rows.append(early + mid + late)
        n_early, n_free = min(n_early, len(early)), min(n_free, len(early) + len(mid))
    return np.asarray(rows, np.int32), n_early, n_free


def _q_pieces(k):
    lo, hi = SHARD_Q * k, SHARD_Q * (k + 1)
    out = []
    for h in range(N_HEADS):
        base = (D_NOPE + D_ROPE) * h
        s, e = max(lo, base), min(hi, base + D_NOPE)
        if s < e:
            out.append((s - lo, e - lo, D_NOPE * h + s - base))
        s, e = max(lo, base + D_NOPE), min(hi, base + D_NOPE + D_ROPE)
        if s < e:
            out.append((s - lo, e - lo, N_HEADS * D_NOPE + D_ROPE * h + s - base - D_NOPE))
    return out


def _kv_dst(k):
    return D_NOPE * (k // 2) + (N_HEADS * D_NOPE if k % 2 else 0)


def _params(*sem):
    return pltpu.CompilerParams(dimension_semantics=sem, vmem_limit_bytes=VMEM_LIMIT)


def _rms_stats(x):
    r = lax.rsqrt(jnp.mean(x * x, axis=-1, keepdims=True) + EPS)
    return x * r, r


def _rms_bwd(gdy, xhat, r):
    return r * (gdy - xhat * jnp.mean(gdy * xhat, axis=-1, keepdims=True))


def _sigmoid(z):
    return 1.0 / (1.0 + jnp.exp(-z))


def _group_mean(x):
    i0 = lax.broadcasted_iota(jnp.int32, (LANES, LANES), 0) // CONV_GROUP
    i1 = lax.broadcasted_iota(jnp.int32, (LANES, LANES), 1) // CONV_GROUP
    m = jnp.where(i0 == i1, 1.0 / CONV_GROUP, 0.0).astype(BF16)
    hi = x.astype(BF16)
    lo = (x - hi.astype(F32)).astype(BF16)
    return jnp.dot(hi, m, preferred_element_type=F32) + jnp.dot(lo, m, preferred_element_type=F32)


_NT = (((1,), (1,)), ((), ()))
_TN = (((0,), (0,)), ((), ()))


def _dot(a, b, dims=None):
    if dims is None:
        return jnp.dot(a, b, preferred_element_type=F32)
    return lax.dot_general(a, b, dims, preferred_element_type=F32)


def _device_position():
    x, y, c = lax.axis_index("x"), lax.axis_index("y"), lax.axis_index("c")
    return x, y, c, 4 * x + 2 * y + c


def _gather_plan(srcs, slots, send_sems, recv_sems, local_sems):
    x, y, c, _ = _device_position()
    me, sibling = (x, y, c), (x, y, 1 - c)
    flip = lambda v, on: v + on - 2 * v * on
    near = (flip(x, 1 - c), flip(y, c))
    far = (flip(x, c), flip(y, 1 - c))
    diag = (1 - x, 1 - y)
    n = len(srcs)

    def slot(a, px, py, pc):
        return slots[a].at[4 * px + 2 * py + pc]

    def copy(a, k, block, to, own=False):
        return pltpu.make_async_remote_copy(
            src_ref=srcs[a] if own else slot(a, *block),
            dst_ref=slot(a, *block),
            send_sem=send_sems.at[7 * a + k],
            recv_sem=recv_sems.at[7 * a + k],
            device_id=to,
            device_id_type=pl.DeviceIdType.MESH,
        )

    def local(a):
        return pltpu.make_async_copy(srcs[a], slot(a, *me), local_sems.at[a])

    sent = [(me, sibling), (me, (*near, c)), (me, (*far, c)), ((*near, c), (*far, c)),
            ((*near, c), sibling), ((*far, c), sibling), ((*diag, c), sibling)]
    landed = [sibling, (*near, c), (*far, c), (*diag, c), (*far, 1 - c), (*near, 1 - c), (*diag, 1 - c)]

    def send(a, k):
        return copy(a, k, *sent[k], own=k < 3)

    def arrival(a, k):
        return copy(a, k, landed[k], me)

    def start():
        for a in range(n):
            local(a).start()
            for k in range(3):
                send(a, k).start()

    def own():
        for a in range(n):
            local(a).wait()
            arrival(a, 0).wait_recv()

    def mid():
        for a in range(n):
            arrival(a, 1).wait_recv()
            send(a, 3).start()
            send(a, 4).start()
        for a in range(n):
            arrival(a, 2).wait_recv()
            send(a, 5).start()
        for a in range(n):
            for k in (4, 5):
                arrival(a, k).wait_recv()

    def late():
        for a in range(n):
            arrival(a, 3).wait_recv()
            send(a, 6).start()
        for a in range(n):
            arrival(a, 6).wait_recv()

    def finish():
        for a in range(n):
            for k in range(7):
                send(a, k).wait_send()

    return start, own, mid, late, finish


def _adam_update(g, w, m, v):
    m_new = ADAM_B1 * m + (1.0 - ADAM_B1) * g
    v_new = ADAM_B2 * v + (1.0 - ADAM_B2) * (g * g)
    m_hat = m_new / (1.0 - ADAM_B1 ** ADAM_STEP)
    v_hat = v_new / (1.0 - ADAM_B2 ** ADAM_STEP)
    return -ADAM_LR * (m_hat / (jnp.sqrt(v_hat) + ADAM_EPS) + ADAM_WD * w), m_new, v_new


def _adamw(grads, params):
    names = [n for n, _ in PARAM_SHAPES]
    n_p = len(names)

    def body(*refs):
        for i in range(n_p):
            g = refs[i][...]
            w, m, v = (refs[n_p + 3 * i + j][...] for j in range(3))
            delta, m_new, v_new = _adam_update(g, w, m, v)
            for j, val in enumerate((g, delta, m_new, v_new)):
                refs[4 * n_p + 4 * i + j][...] = val

    vm = pl.BlockSpec(memory_space=pltpu.VMEM)
    out_shape = []
    for _, shape in PARAM_SHAPES:
        out_shape += [jax.ShapeDtypeStruct(shape, F32)] * 4
    outs = pl.pallas_call(
        body,
        name="adamw",
        out_shape=tuple(out_shape),
        in_specs=[vm] * (4 * n_p),
        out_specs=(vm,) * (4 * n_p),
        compiler_params=pltpu.CompilerParams(vmem_limit_bytes=VMEM_LIMIT),
    )(*[grads[n] for n in names], *[a for n in names for a in params[n]])
    return {n: outs[4 * i:4 * i + 4] for i, n in enumerate(names)}


N_CHIPS = 4


def _reduce_plan(pays, owns, r1s, sums, r2s, send1, recv1, send2, recv2, local_sems):
    x, y, c, _ = _device_position()
    sibling = (x, y, 1 - c)
    chips = [((1 - x if rj & 2 else x), (1 - y if rj & 1 else y)) for rj in range(N_CHIPS)]
    n = len(pays)

    def slot_of(rj, core):
        return 4 * chips[rj][0] + 2 * chips[rj][1] + core

    def to_sibling(a, rj):
        return pltpu.make_async_remote_copy(
            src_ref=pays[a].at[slot_of(rj, 1 - c)], dst_ref=r1s[a].at[rj],
            send_sem=send1.at[N_CHIPS * a + rj], recv_sem=recv1.at[N_CHIPS * a + rj],
            device_id=sibling, device_id_type=pl.DeviceIdType.MESH)

    def load_own(a, rj):
        return pltpu.make_async_copy(pays[a].at[slot_of(rj, c)], owns[a].at[rj], local_sems.at[2 * N_CHIPS * a + rj])

    def to_chip(a, rj):
        return pltpu.make_async_remote_copy(
            src_ref=sums[a].at[rj], dst_ref=r2s[a].at[rj],
            send_sem=send2.at[N_CHIPS * a + rj], recv_sem=recv2.at[N_CHIPS * a + rj],
            device_id=(*chips[rj], c), device_id_type=pl.DeviceIdType.MESH)

    def keep(a):
        return pltpu.make_async_copy(sums[a].at[0], r2s[a].at[0], local_sems.at[2 * N_CHIPS * a + N_CHIPS])

    def start():
        for a in range(n):
            for rj in range(N_CHIPS):
                to_sibling(a, rj).start()
                if owns[a] is not None:
                    load_own(a, rj).start()

    def combine():
        for a in range(n):
            for rj in range(N_CHIPS):
                to_sibling(a, rj).wait_recv()
                if owns[a] is not None:
                    load_own(a, rj).wait()
                    mine = owns[a][rj]
                else:
                    mine = pays[a][slot_of(rj, c)]
                sums[a][rj] = (mine.astype(F32) + r1s[a][rj].astype(F32)).astype(sums[a].dtype)
            keep(a).start()
            for rj in range(1, N_CHIPS):
                to_chip(a, rj).start()

    def finish():
        for a in range(n):
            for rj in range(1, N_CHIPS):
                to_chip(a, rj).wait_recv()
            for rj in range(N_CHIPS):
                to_sibling(a, rj).wait_send()
            for rj in range(1, N_CHIPS):
                to_chip(a, rj).wait_send()
            keep(a).wait()

    return start, combine, finish


def _reduce_scratch(shapes_dtypes, own_flags):
    out = []
    for (shape, dtype), own in zip(shapes_dtypes, own_flags):
        if own:
            out.append(pltpu.VMEM((N_CHIPS,) + shape, dtype))
        out += [pltpu.VMEM((N_CHIPS,) + shape, dtype), pltpu.VMEM((N_CHIPS,) + shape, dtype)]
    n = len(shapes_dtypes)
    out += [pltpu.SemaphoreType.DMA((N_CHIPS * n,))] * 4 + [pltpu.SemaphoreType.DMA((2 * N_CHIPS * n,))]
    return out


def _pack_small(ssmall, dwq, dwkv, dconv, dfinal, dgq, dgkv, dattn, dconvg, loss_part):
    ssmall[...] = jnp.zeros_like(ssmall)
    rep = ssmall.at[0]
    for i in range(D_MODEL // LANES):
        rep[ROW_FINAL + i:ROW_FINAL + i + 1, :] = dfinal[:, LANES * i:LANES * (i + 1)]
    for i in range(Q_RANK // LANES):
        rep[ROW_GQ + i:ROW_GQ + i + 1, :] = dgq[:, LANES * i:LANES * (i + 1)]
    rep[ROW_GKV:ROW_GKV + 1, :] = dgkv[...]
    for i in range(CONV_WIDTH // LANES):
        rep[ROW_ATTN + i:ROW_ATTN + i + 1, :] = dattn[:, LANES * i:LANES * (i + 1)]
        rep[ROW_CONVG + i:ROW_CONVG + i + 1, :] = dconvg[:, LANES * i:LANES * (i + 1)]
    rep[ROW_LOSS:ROW_LOSS + 1, :] = loss_part[...]
    for k in range(N_DEV):
        if k:
            ssmall[k, ROW_REPL:, :] = ssmall[0, ROW_REPL:, :]
        for s, e, d in _q_pieces(k):
            for i in range(Q_RANK // LANES):
                ssmall[k, ROW_Q + SHARD_Q * i + s:ROW_Q + SHARD_Q * i + e, :] = dwq[d:d + e - s, LANES * i:LANES * (i + 1)]
        ssmall[k, ROW_KV:ROW_KV + KV_RANK, :] = dwkv[:, _kv_dst(k):_kv_dst(k) + SHARD_KV]
        ssmall[k, ROW_CONV:ROW_CONV + 3, 0:SHARD_CONV] = dconv[0:3, SHARD_CONV * k:SHARD_CONV * (k + 1)]


TOKEN_TILES = 4
TAIL_ROWS = N_META + D_MODEL // LANES


def _reduce_tail(r_in, r_out, r_small, d_meta, d_norm):
    n_p = len(PARAM_SHAPES)
    names = [n for n, _ in PARAM_SHAPES]

    def body(*refs):
        rin, rout, rsmall, dmeta, dnorm = refs[:5]
        g_out = {n: refs[5 + i] for i, n in enumerate(names)}
        loss_out = refs[5 + n_p]
        stail, rtail, gsum, gtail, send_sems, recv_sems = refs[6 + n_p:]
        x, y, c, me = _device_position()
        my_chip = 2 * x + y

        for k in range(N_DEV):
            stail[k, 0:N_META, :] = dmeta[:, SHARD_META * k:SHARD_META * (k + 1)]
            for i in range(D_MODEL // LANES):
                stail[k, N_META + i:N_META + i + 1, :] = dnorm[:, LANES * i:LANES * (i + 1)]
        copies = []
        for r in range(1, N_DEV):
            peer = (1 - x if r & 4 else x, 1 - y if r & 2 else y, 1 - c if r & 1 else c)
            copies.append(pltpu.make_async_remote_copy(
                src_ref=stail.at[4 * peer[0] + 2 * peer[1] + peer[2]],
                dst_ref=rtail.at[r],
                send_sem=send_sems.at[r - 1],
                recv_sem=recv_sems.at[r - 1],
                device_id=peer,
                device_id_type=pl.DeviceIdType.MESH,
            ))
        for cp in copies:
            cp.start()
        rtail[0] = stail[me]

        g = rin[my_chip].astype(F32)
        for ch in range(1, N_CHIPS):
            g = g + rin[ch ^ my_chip].astype(F32)
        g_out["w_in"][...] = g[:SHARD_IN, :]

        g = rout[my_chip].astype(F32)
        gs = rsmall[my_chip]
        for ch in range(1, N_CHIPS):
            g = g + rout[ch ^ my_chip].astype(F32)
            gs = gs + rsmall[ch ^ my_chip]
        g_out["w_out"][...] = g
        gsum[...] = gs
        for i in range(Q_RANK // LANES):
            g_out["w_q_up"][:, LANES * i:LANES * (i + 1)] = gsum[ROW_Q + SHARD_Q * i:ROW_Q + SHARD_Q * (i + 1), :]
        g_out["w_kv_up"][...] = gsum[ROW_KV:ROW_KV + KV_RANK, :]
        for i in range(3):
            g_out["conv_w"][i] = gsum[ROW_CONV + i:ROW_CONV + i + 1, 0:SHARD_CONV]
        for name, row, width in (("final_norm_g", ROW_FINAL, D_MODEL), ("q_norm_g", ROW_GQ, Q_RANK),
                                 ("kv_norm_g", ROW_GKV, KV_RANK), ("attn_out_g", ROW_ATTN, CONV_WIDTH),
                                 ("conv_out_g", ROW_CONVG, CONV_WIDTH)):
            for i in range(width // LANES):
                g_out[name][:, LANES * i:LANES * (i + 1)] = gsum[row + i:row + i + 1, :]
        loss_out[...] = gsum[ROW_LOSS:ROW_LOSS + 1, :]

        for cp in copies:
            cp.wait_recv()
        gt = rtail[me]
        for d in range(1, N_DEV):
            gt = gt + rtail[d ^ me]
        gtail[...] = gt
        g_out["meta_tokens"][...] = gtail[0:N_META, :]
        for i in range(D_MODEL // LANES):
            g_out["norm_g"][:, LANES * i:LANES * (i + 1)] = gtail[N_META + i:N_META + i + 1, :]
        for cp in copies:
            cp.wait_send()

    vm = pl.BlockSpec(memory_space=pltpu.VMEM)
    out_shape = [jax.ShapeDtypeStruct(shape, F32) for _, shape in PARAM_SHAPES]
    out_shape.append(jax.ShapeDtypeStruct((1, LANES), F32))
    outs = pl.pallas_call(
        body,
        name="reduce_tail",
        out_shape=tuple(out_shape),
        in_specs=[vm] * 5,
        out_specs=(vm,) * len(out_shape),
        scratch_shapes=[
            pltpu.VMEM((N_DEV, TAIL_ROWS, LANES), F32),
            pltpu.VMEM((N_DEV, TAIL_ROWS, LANES), F32),
            pltpu.VMEM((SMALL_ROWS, LANES), F32),
            pltpu.VMEM((TAIL_ROWS, LANES), F32),
            pltpu.SemaphoreType.DMA((N_DEV - 1,)),
            pltpu.SemaphoreType.DMA((N_DEV - 1,)),
        ],
        compiler_params=pltpu.CompilerParams(vmem_limit_bytes=VMEM_LIMIT),
    )(r_in, r_out, r_small, d_meta, d_norm)
    return {n: outs[i] for i, n in enumerate(names)}, outs[-1]


def _prep_in_proj(x, meta, norm_g, w_in_t, w_q, w_kv, w_out, conv_w, order):
    nb_seq, s, d = x.shape
    tp = s + LANES
    m = nb_seq * tp
    ts = s // TOKEN_TILES
    n_real = nb_seq * TOKEN_TILES
    n_norm = n_real + 1
    n_tiles = IN_PAD // P_TILE
    _, n_early, n_free = _tile_orders()
    steps = n_norm + n_tiles
    dot_rows = m // 4
    qkv_shape = (SHARD_Q + KV_RANK, Q_RANK)

    def tile(t):
        t = jnp.minimum(t, n_real - 1)
        return t // TOKEN_TILES, t % TOKEN_TILES

    def column_tile(t, order_ref):
        return order_ref[jnp.maximum(t - n_norm, 0)]

    def body(order_ref, x_ref, meta_ref, g_ref, win_ref, wq_ref, wkv_ref, wout_ref, conv_ref,
             u_hbm, p_ref, w_in_p, meta_f, w_q_p, w_kv_p, w_out_f, conv_f,
             u_all, sbig, gbig, smeta, gmeta, sqkv, sout, sconv, gqkv, gout, gconv,
             send_in, recv_in, local_in, send_meta, recv_meta, send_rest, recv_rest, local_rest, u_sem):
        t = pl.program_id(0)
        px, py, pc, me = _device_position()
        u_copy = pltpu.make_async_copy(u_all, u_hbm, u_sem)

        def plan_in():
            return _gather_plan((sbig,), (gbig,), send_in, recv_in, local_in)

        def plan_rest():
            return _gather_plan((sqkv, sout, sconv), (gqkv, gout, gconv), send_rest, recv_rest, local_rest)

        def meta_copies():
            out = []
            for r in range(1, N_DEV):
                peer = (1 - px if r & 4 else px, 1 - py if r & 2 else py, 1 - pc if r & 1 else pc)
                out.append(pltpu.make_async_remote_copy(
                    src_ref=smeta,
                    dst_ref=gmeta.at[r],
                    send_sem=send_meta.at[r - 1],
                    recv_sem=recv_meta.at[r - 1],
                    device_id=peer,
                    device_id_type=pl.DeviceIdType.MESH,
                ))
            return out

        @pl.when(t == 0)
        def _():
            sbig[0:SHARD_IN, :] = win_ref[...].astype(BF16)
            sbig[SHARD_IN:, :] = jnp.zeros((SHARD_IN_PAD - SHARD_IN, d), BF16)
            smeta[...] = meta_ref[...]
            plan_in()[0]()
            for cp in meta_copies():
                cp.start()
            sqkv[...] = jnp.zeros_like(sqkv)
            sqkv[0:SHARD_Q, :] = wq_ref[...].astype(BF16)
            sqkv[SHARD_Q:, 0:SHARD_KV] = wkv_ref[...].astype(BF16)
            sout[...] = wout_ref[...].astype(BF16)
            sconv[...] = jnp.zeros_like(sconv)
            for i in range(3):
                sconv[i:i + 1, 0:SHARD_CONV] = conv_ref[i]

        def norm(h):
            hhat, _ = _rms_stats(h)
            return (hhat * g_ref[...]).astype(BF16)

        @pl.when(t < n_real)
        def _():
            b, k = tile(t)
            row0 = pl.multiple_of(b * tp + LANES + k * ts, 16)
            u_all[pl.ds(row0, ts), :] = norm(x_ref[0])

        @pl.when(t == n_real)
        def _():
            for cp in meta_copies():
                cp.wait_recv()
            gmeta[0] = smeta[...]
            for k in range(N_DEV):
                meta_f[:, SHARD_META * k:SHARD_META * (k + 1)] = gmeta[k ^ me]
            um = norm(meta_f[...])
            for b in range(nb_seq):
                u_all[b * tp:b * tp + PAD_FRONT, :] = jnp.zeros((PAD_FRONT, d), BF16)
                u_all[b * tp + PAD_FRONT:b * tp + LANES, :] = um
            u_copy.start()

        @pl.when(t == n_norm)
        def _():
            plan_in()[1]()

        @pl.when(t == n_norm + n_early)
        def _():
            plan_in()[2]()
            plan_rest()[0]()

        @pl.when(t == n_norm + (n_early + n_free) // 2)
        def _():
            plan_rest()[1]()
            plan_rest()[2]()

        @pl.when(t == n_norm + n_free)
        def _():
            plan_in()[3]()

        @pl.when(t == n_norm + n_free + 2)
        def _():
            plan_rest()[3]()

        @pl.when(t >= n_norm)
        def _():
            j = column_tile(t, order_ref)
            for jj in range(n_tiles):
                @pl.when(j == jj)
                def _(jj=jj):
                    if jj == N_A // P_TILE:
                        w_in_p[N_A % P_TILE:, :] = jnp.zeros((P_TILE - N_A % P_TILE, d), BF16)
                    for k, s0, e0, d0 in _tile_pieces(jj):
                        w_in_p[d0:d0 + e0 - s0, :] = gbig[k, s0:e0, :]
            for r in range(m // dot_rows):
                rows = slice(dot_rows * r, dot_rows * (r + 1))
                p_ref[rows, :] = _dot(u_all[rows, :], w_in_p[...], _NT).astype(BF16)

        @pl.when(t == steps - 1)
        def _():
            plan_in()[4]()
            plan_rest()[4]()
            for cp in meta_copies():
                cp.wait_send()
            u_copy.wait()
            conv_f[...] = jnp.zeros_like(conv_f)
            for k in range(N_DEV):
                for s0, e0, d0 in _q_pieces(k):
                    w_q_p[d0:d0 + e0 - s0, :] = gqkv[k, s0:e0, :]
                w_kv_p[:, _kv_dst(k):_kv_dst(k) + SHARD_KV] = gqkv[k, SHARD_Q:, 0:SHARD_KV]
                w_out_f[SHARD_OUT * k:SHARD_OUT * (k + 1), :] = gout[k]
                conv_f[0:3, SHARD_CONV * k:SHARD_CONV * (k + 1)] = gconv[k, 0:3, 0:SHARD_CONV]

    whole = lambda shape: pl.BlockSpec(shape, lambda t, o: (0,) * len(shape))
    return pl.pallas_call(
        body,
        name="prep_in_proj_gather",
        grid_spec=pltpu.PrefetchScalarGridSpec(
            num_scalar_prefetch=1,
            grid=(steps,),
            in_specs=[
                pl.BlockSpec((1, ts, d), lambda t, o: (*tile(t), 0)),
                whole(meta.shape), whole(norm_g.shape), whole(w_in_t.shape),
                whole(w_q.shape), whole(w_kv.shape), whole(w_out.shape), whole(conv_w.shape),
            ],
            out_specs=(pl.BlockSpec(memory_space=pl.ANY),
                       pl.BlockSpec((m, P_TILE), lambda t, o: (0, column_tile(t, o))),
                       pl.BlockSpec((P_TILE, d), lambda t, o: (column_tile(t, o), 0)),
                       whole((N_META, d)),
                       whole((Q_COLS, Q_RANK)), whole((KV_RANK, KV_COLS)), whole((D_MODEL, D_MODEL)),
                       whole((8, CONV_WIDTH))),
            scratch_shapes=[
                pltpu.VMEM((m, d), BF16),
                pltpu.VMEM((SHARD_IN_PAD, d), BF16),
                pltpu.VMEM((N_DEV, SHARD_IN_PAD, d), BF16),
                pltpu.VMEM((N_META, SHARD_META), F32),
                pltpu.VMEM((N_DEV, N_META, SHARD_META), F32),
                pltpu.VMEM(qkv_shape, BF16),
                pltpu.VMEM((SHARD_OUT, D_MODEL), BF16),
                pltpu.VMEM((8, LANES), F32),
                pltpu.VMEM((N_DEV,) + qkv_shape, BF16),
                pltpu.VMEM((N_DEV, SHARD_OUT, D_MODEL), BF16),
                pltpu.VMEM((N_DEV, 8, LANES), F32),
                pltpu.SemaphoreType.DMA((7,)),
                pltpu.SemaphoreType.DMA((7,)),
                pltpu.SemaphoreType.DMA((1,)),
                pltpu.SemaphoreType.DMA((N_DEV - 1,)),
                pltpu.SemaphoreType.DMA((N_DEV - 1,)),
                pltpu.SemaphoreType.DMA((21,)),
                pltpu.SemaphoreType.DMA((21,)),
                pltpu.SemaphoreType.DMA((3,)),
                pltpu.SemaphoreType.DMA,
            ],
        ),
        out_shape=(jax.ShapeDtypeStruct((m, d), BF16),
                   jax.ShapeDtypeStruct((m, IN_PAD), BF16),
                   jax.ShapeDtypeStruct((IN_PAD, d), BF16),
                   jax.ShapeDtypeStruct((N_META, d), F32),
                   jax.ShapeDtypeStruct((Q_COLS, Q_RANK), BF16),
                   jax.ShapeDtypeStruct((KV_RANK, KV_COLS), BF16),
                   jax.ShapeDtypeStruct((D_MODEL, D_MODEL), BF16),
                   jax.ShapeDtypeStruct((8, CONV_WIDTH), F32)),
        compiler_params=_params("arbitrary"),
    )(order, x, meta, norm_g, w_in_t, w_q, w_kv, w_out, conv_w)


def _rope_tables(tp):
    half = D_ROPE // 2
    inv_freq = (1.0 / (ROPE_THETA ** (np.arange(half, dtype=np.float32) / half))).astype(np.float32)
    pos = (np.arange(tp) - PAD_FRONT).astype(np.float32)
    ang = pos[:, None] * inv_freq[None, :]
    cos = np.tile(np.cos(ang), (1, LANES // half))
    sin = np.tile(np.sin(ang), (1, LANES // half))
    first = (np.arange(LANES) % D_ROPE) < half
    zero = np.float32(0.0)
    return tuple(jnp.asarray(t, F32) for t in (cos, np.where(first, -sin, zero), np.where(first, zero, sin)))


def _rope(t, cos, sa, sb):
    return t * cos + pltpu.roll(t, LANES - D_ROPE // 2, 1) * sa + pltpu.roll(t, D_ROPE // 2, 1) * sb


def _rope_t(t, cos, sa, sb):
    return t * cos + pltpu.roll(t * sa, D_ROPE // 2, 1) + pltpu.roll(t * sb, LANES - D_ROPE // 2, 1)


def _qkv_fwd(p, wq, wkv, gq, gkv, tables, nb_seq, tp):
    ht = tp // 2

    def body(pa_ref, wq_ref, wkv_ref, gq_ref, gkv_ref, cos_ref, sa_ref, sb_ref, q_ref, k_ref, v_ref):
        pa = pa_ref[...].astype(F32)
        cq_hat, _ = _rms_stats(pa[:, :Q_RANK])
        ckv_hat, _ = _rms_stats(pa[:, Q_RANK:Q_RANK + KV_RANK])
        q = _dot((cq_hat * gq_ref[...]).astype(BF16), wq_ref[...], _NT) * Q_SCALE
        kv = _dot((ckv_hat * gkv_ref[...]).astype(BF16), wkv_ref[...])
        tabs = (cos_ref[...], sa_ref[...], sb_ref[...])
        lane = lax.broadcasted_iota(jnp.int32, (ht, LANES), 1)
        low = lane < D_ROPE
        mark = lane == D_ROPE
        row = (pl.program_id(0) % 2) * ht + lax.broadcasted_iota(jnp.int32, (ht, LANES), 0)
        k_pe = jnp.where(mark & (row < PAD_FRONT), NEG_INF, _rope(pa[:, Q_RANK + KV_RANK:], *tabs))
        one = jnp.where(mark & (row >= PAD_FRONT), 1.0, 0.0)
        pairs = [_rope(q[:, N_HEADS * D_NOPE + LANES * i:N_HEADS * D_NOPE + LANES * (i + 1)], *tabs) for i in range(2)]
        for h in range(N_HEADS):
            pair = pairs[h // 2]
            if h % 2:
                pair = pltpu.roll(pair, D_ROPE, 1)
            pe = jnp.where(low, pair, one)
            q_ref[0, h] = jnp.concatenate([q[:, D_NOPE * h:D_NOPE * (h + 1)], pe], axis=1).astype(BF16)
            k_ref[0, h] = jnp.concatenate([kv[:, D_NOPE * h:D_NOPE * (h + 1)], k_pe], axis=1).astype(BF16)
            v_ref[0, h] = kv[:, N_HEADS * D_NOPE + D_V * h:N_HEADS * D_NOPE + D_V * (h + 1)].astype(BF16)

    full = lambda a: pl.BlockSpec(a.shape, lambda i: (0,) * a.ndim)
    tab = pl.BlockSpec((ht, LANES), lambda i: (i % 2, 0))
    qk = pl.BlockSpec((1, N_HEADS, ht, 2 * LANES), lambda i: (i // 2, 0, i % 2, 0))
    return pl.pallas_call(
        body,
        name="qkv_fwd",
        grid=(2 * nb_seq,),
        in_specs=[pl.BlockSpec((ht, GRP_A), lambda i: (i, 0)), full(wq), full(wkv), full(gq), full(gkv), tab, tab, tab],
        out_specs=(qk, qk, pl.BlockSpec((1, N_HEADS, ht, D_V), lambda i: (i // 2, 0, i % 2, 0))),
        out_shape=(
            jax.ShapeDtypeStruct((nb_seq, N_HEADS, tp, 2 * LANES), BF16),
            jax.ShapeDtypeStruct((nb_seq, N_HEADS, tp, 2 * LANES), BF16),
            jax.ShapeDtypeStruct((nb_seq, N_HEADS, tp, D_V), BF16),
        ),
        compiler_params=_params("parallel"),
    )(p, wq, wkv, gq, gkv, *tables)


def _attn_fwd(q, k, v, p, g_attn):
    nb_seq, _, tp, _ = q.shape

    def body(q_ref, k_ref, v_ref, z_ref, g_ref, y_ref, o_ref, lse_ref):
        g = g_ref[...]
        for r0 in range(0, tp, Q_TILE):
            nq = min(Q_TILE, tp - r0)
            kend = r0 + nq
            qq = q_ref[0, 0, r0:kend, :]
            sd = _dot(qq, k_ref[0, 0, r0:kend, :], _NT)
            causal = (lax.broadcasted_iota(jnp.int32, (nq, nq), 1) <= lax.broadcasted_iota(jnp.int32, (nq, nq), 0))
            sd = jnp.where(causal, sd, NEG_INF)
            m = jnp.max(sd, axis=-1, keepdims=True)
            if r0:
                so = _dot(qq, k_ref[0, 0, 0:r0, :], _NT)
                m = jnp.maximum(m, jnp.max(so, axis=-1, keepdims=True))
            ed = jnp.exp2(sd - m)
            l = jnp.sum(ed, axis=-1, keepdims=True)
            o = _dot(ed.astype(BF16), v_ref[0, 0, r0:kend, :])
            if r0:
                eo = jnp.exp2(so - m)
                l = l + jnp.sum(eo, axis=-1, keepdims=True)
                o = o + _dot(eo.astype(BF16), v_ref[0, 0, 0:r0, :])
            o = o * (1.0 / l)
            o_ref[0, 0, r0:kend, :] = o
            lse_ref[0, 0, r0:kend, :] = jnp.broadcast_to(m + jnp.log2(l), (nq, LANES))
            ohat, _ = _rms_stats(o)
            z = z_ref[r0:kend, :].astype(F32)
            y_ref[r0:kend, :] = (ohat * g * (z * _sigmoid(z))).astype(BF16)

    qk = pl.BlockSpec((1, 1, tp, 2 * LANES), lambda b, h: (b, h, 0, 0))
    hv = pl.BlockSpec((1, 1, tp, D_V), lambda b, h: (b, h, 0, 0))
    return pl.pallas_call(
        body,
        name="attn_fwd",
        grid=(nb_seq, N_HEADS),
        in_specs=[qk, qk, hv,
                  pl.BlockSpec((tp, LANES), lambda b, h: (b, GRP_A // LANES + h)),
                  pl.BlockSpec((1, LANES), lambda b, h: (0, h))],
        out_specs=(pl.BlockSpec((tp, LANES), lambda b, h: (b, h)), hv, hv),
        out_shape=(
            jax.ShapeDtypeStruct((nb_seq * tp, N_HEADS * D_V), BF16),
            jax.ShapeDtypeStruct((nb_seq, N_HEADS, tp, D_V), F32),
            jax.ShapeDtypeStruct((nb_seq, N_HEADS, tp, LANES), F32),
        ),
        compiler_params=_params("parallel", "parallel"),
    )(q, k, v, p, g_attn)


_CONV_COL0 = (GRP_A + N_HEADS * D_V) // LANES


def _conv_specs(tp, order):
    cols = CONV_WIDTH // LANES
    return [pl.BlockSpec((tp, LANES), functools.partial(
        lambda a, b, off: order(a, b, off), off=_CONV_COL0 + i * cols)) for i in range(4)]


def _conv_fwd(p, conv_w, g_conv, nb_seq, tp):
    def body(b_ref, c_ref, h_ref, z_ref, w_ref, g_ref, y_ref):
        cc = c_ref[...].astype(F32) * h_ref[...].astype(F32)
        row = lax.broadcasted_iota(jnp.int32, (tp, LANES), 0)
        s1 = jnp.where(row >= 1, pltpu.roll(cc, 1, 0), 0.0)
        s2 = jnp.where(row >= 2, pltpu.roll(cc, 2, 0), 0.0)
        yc = b_ref[...].astype(F32) * (w_ref[0:1, :] * s2 + w_ref[1:2, :] * s1 + w_ref[2:3, :] * cc)
        r = lax.rsqrt(_group_mean(yc * yc) + EPS)
        z = z_ref[...].astype(F32)
        y_ref[...] = (yc * r * g_ref[...] * (z * _sigmoid(z))).astype(BF16)

    return pl.pallas_call(
        body,
        name="conv_fwd",
        grid=(nb_seq, CONV_WIDTH // LANES),
        in_specs=_conv_specs(tp, lambda b, t, off: (b, off + t)) + [
            pl.BlockSpec((8, LANES), lambda b, t: (0, t)),
            pl.BlockSpec((1, LANES), lambda b, t: (0, t))],
        out_specs=pl.BlockSpec((tp, LANES), lambda b, t: (b, t)),
        out_shape=jax.ShapeDtypeStruct((nb_seq * tp, CONV_WIDTH), BF16),
        compiler_params=_params("parallel", "parallel"),
    )(p, p, p, p, conv_w, g_conv)


def _token_copy(hbm, b, k, ts, buf, sem, to_hbm=False):
    lo, hi = max(k * ts - LANES, 0), (k + 1) * ts - LANES
    off = lo - (k * ts - LANES)
    src, dst = hbm.at[b, pl.ds(lo, hi - lo)], buf.at[pl.ds(off, hi - lo)]
    if to_hbm:
        src, dst = dst, src
    return pltpu.make_async_copy(src, dst, sem)


def _for_tile(k, nt, fn):
    for kk in range(nt):
        @pl.when(k == kk)
        def _(kk=kk):
            fn(kk)


def _out_proj_loss(ya, yc, w_out, x, target, g_final, nt):
    nb_seq, s, d = x.shape
    r, ka = ya.shape
    ts = (s + LANES) // nt
    steps = nb_seq * nt

    def body(a_ref, c_ref, w_ref, x_hbm, t_hbm, g_ref, dhb_ref, dg_ref, loss_ref,
             xbuf, tbuf, acc_ref, sems):
        i = pl.program_id(0)
        b, k = i // nt, i % nt

        @pl.when(i == 0)
        def _():
            acc_ref[...] = jnp.zeros_like(acc_ref)
            dg_ref[...] = jnp.zeros_like(dg_ref)

        slot = i % 2

        def fetch(seq, kk, sl):
            return [_token_copy(x_hbm, seq, kk, ts, xbuf.at[sl], sems.at[sl, 0]),
                    _token_copy(t_hbm, seq, kk, ts, tbuf.at[sl], sems.at[sl, 1])]

        def start(seq, sl, kk):
            if kk == 0:
                xbuf[sl, 0:LANES, :] = jnp.zeros((LANES, d), F32)
                tbuf[sl, 0:LANES, :] = jnp.zeros((LANES, d), F32)
            for cp in fetch(seq, kk, sl):
                cp.start()

        @pl.when(i == 0)
        def _():
            start(0, 0, 0)

        @pl.when(i + 1 < steps)
        def _():
            _for_tile((i + 1) % nt, nt, functools.partial(start, (i + 1) // nt, 1 - slot))

        mix = _dot(a_ref[...], w_ref[0:ka, :]) + _dot(c_ref[...], w_ref[ka:, :])
        _for_tile(k, nt, lambda kk: [cp.wait() for cp in fetch(b, kk, slot)])

        real = (lax.broadcasted_iota(jnp.int32, (ts, d), 0) >= LANES) | (k > 0)
        g = g_ref[...]
        hhat, rstd = _rms_stats(xbuf[slot] + mix)
        e = jnp.where(real, hhat * g - tbuf[slot], 0.0)
        acc_ref[...] += jnp.sum(e * e, axis=0, keepdims=True)
        dy = e * (1.0 / d)
        dg_ref[...] += jnp.sum(dy * hhat, axis=0, keepdims=True)
        dhb_ref[...] = _rms_bwd(g * dy, hhat, rstd).astype(BF16)

        @pl.when(i == steps - 1)
        def _():
            total = jnp.sum(acc_ref[...], axis=1, keepdims=True)
            loss_ref[...] = jnp.broadcast_to((0.5 / d) * total, loss_ref.shape)

    hbm = pl.BlockSpec(memory_space=pl.ANY)
    row = pl.BlockSpec((ts, d), lambda i: (i, 0))
    vec = pl.BlockSpec((1, d), lambda i: (0, 0))
    return pl.pallas_call(
        body,
        name="out_proj_loss",
        grid=(steps,),
        in_specs=[pl.BlockSpec((ts, ka), lambda i: (i, 0)), pl.BlockSpec((ts, yc.shape[1]), lambda i: (i, 0)),
                  pl.BlockSpec(w_out.shape, lambda i: (0, 0)), hbm, hbm, vec],
        out_specs=(row, vec, pl.BlockSpec((1, LANES), lambda i: (0, 0))),
        out_shape=(
            jax.ShapeDtypeStruct((r, d), BF16),
            jax.ShapeDtypeStruct((1, d), F32),
            jax.ShapeDtypeStruct((1, LANES), F32),
        ),
        scratch_shapes=[pltpu.VMEM((2, ts, d), F32), pltpu.VMEM((2, ts, d), F32), pltpu.VMEM((1, d), F32),
                        pltpu.SemaphoreType.DMA((2, 2))],
        compiler_params=_params("arbitrary"),
    )(ya, yc, w_out, x, target, g_final)


def _out_proj_bwd(dhb, w_out, ya, yc, bm):
    r, d = dhb.shape
    ka = ya.shape[1]
    n_mix = w_out.shape[0]
    last = r // bm - 1

    def body(dh_ref, w_ref, a_ref, c_ref, dcat_ref, dw_ref, acc_ref):
        @pl.when(pl.program_id(0) == 0)
        def _():
            acc_ref[...] = jnp.zeros_like(acc_ref)

        dh = dh_ref[...]
        dcat_ref[...] = _dot(dh, w_ref[...], _NT).astype(BF16)
        acc_ref[0:ka, :] += _dot(a_ref[...], dh, _TN)
        acc_ref[ka:, :] += _dot(c_ref[...], dh, _TN)

        @pl.when(pl.program_id(0) == last)
        def _():
            dw_ref[...] = acc_ref[...].astype(BF16)

    return pl.pallas_call(
        body,
        name="out_proj_bwd",
        grid=(r // bm,),
        in_specs=[pl.BlockSpec((bm, d), lambda i: (i, 0)), pl.BlockSpec(w_out.shape, lambda i: (0, 0)),
                  pl.BlockSpec((bm, ka), lambda i: (i, 0)), pl.BlockSpec((bm, yc.shape[1]), lambda i: (i, 0))],
        out_specs=(pl.BlockSpec((bm, n_mix), lambda i: (i, 0)),
                   pl.BlockSpec((n_mix, d), lambda i: (0, 0))),
        out_shape=(jax.ShapeDtypeStruct((r, n_mix), BF16),
                   jax.ShapeDtypeStruct((n_mix, d), BF16)),
        scratch_shapes=[pltpu.VMEM((n_mix, d), F32)],
        compiler_params=_params("arbitrary"),
    )(dhb, w_out, ya, yc)


def _attn_bwd(q, k, v, o, lse, dcat, p, g_attn):
    nb_seq, _, tp, _ = q.shape

    def body(q_ref, k_ref, v_ref, o_ref, lse_ref, dy_ref, z_ref, g_ref,
             dq_ref, dk_ref, dv_ref, dz_ref, dg_ref, dq_acc):
        @pl.when(pl.program_id(1) == 0)
        def _():
            dg_ref[...] = jnp.zeros_like(dg_ref)

        g = g_ref[...]
        z = z_ref[...].astype(F32)
        o = o_ref[0, 0]
        dy = dy_ref[...].astype(F32)
        sig = _sigmoid(z)
        ohat, r = _rms_stats(o)
        don = dy * (z * sig)
        dz_ref[...] = (dy * (ohat * g) * (sig * (1.0 + z * (1.0 - sig)))).astype(BF16)
        dg_ref[...] += jnp.sum(don * ohat, axis=0, keepdims=True)
        do = _rms_bwd(g * don, ohat, r)
        dvec = jnp.sum(do * o, axis=-1, keepdims=True)
        dob = do.astype(BF16)
        lse_col = lse_ref[0, 0, :, 0:1]
        dq_acc[...] = jnp.zeros_like(dq_acc)
        for k0 in range(0, tp, K_TILE):
            nk = min(K_TILE, tp - k0)
            nq = tp - k0
            qq = q_ref[0, 0, k0:, :]
            kk = k_ref[0, 0, k0:k0 + nk, :]
            causal = (lax.broadcasted_iota(jnp.int32, (nq, nk), 1) <= lax.broadcasted_iota(jnp.int32, (nq, nk), 0))
            pr = jnp.where(causal, jnp.exp2(_dot(qq, kk, _NT) - lse_col[k0:]), 0.0)
            dp = _dot(dob[k0:], v_ref[0, 0, k0:k0 + nk, :], _NT)
            ds = (pr * (dp - dvec[k0:])).astype(BF16)
            dv_ref[0, 0, k0:k0 + nk, :] = _dot(pr.astype(BF16), dob[k0:], _TN).astype(BF16)
            dk_ref[0, 0, k0:k0 + nk, :] = (_dot(ds, qq, _TN) * (ATTN_SCALE / Q_SCALE)).astype(BF16)
            dq_acc[k0:, :] += _dot(ds, kk)
        dq_ref[0, 0] = (dq_acc[...] * ATTN_SCALE).astype(BF16)

    qk = pl.BlockSpec((1, 1, tp, 2 * LANES), lambda h, b: (b, h, 0, 0))
    hv = pl.BlockSpec((1, 1, tp, D_V), lambda h, b: (b, h, 0, 0))
    col = pl.BlockSpec((tp, LANES), lambda h, b: (b, h))
    return pl.pallas_call(
        body,
        name="attn_bwd",
        grid=(N_HEADS, nb_seq),
        in_specs=[qk, qk, hv, hv, hv, col,
                  pl.BlockSpec((tp, LANES), lambda h, b: (b, GRP_A // LANES + h)),
                  pl.BlockSpec((1, LANES), lambda h, b: (0, h))],
        out_specs=(qk, qk, hv, col, pl.BlockSpec((1, LANES), lambda h, b: (0, h))),
        out_shape=(
            jax.ShapeDtypeStruct((nb_seq, N_HEADS, tp, 2 * LANES), BF16),
            jax.ShapeDtypeStruct((nb_seq, N_HEADS, tp, 2 * LANES), BF16),
            jax.ShapeDtypeStruct((nb_seq, N_HEADS, tp, D_V), BF16),
            jax.ShapeDtypeStruct((nb_seq * tp, N_HEADS * D_V), BF16),
            jax.ShapeDtypeStruct((1, N_HEADS * D_V), F32),
        ),
        scratch_shapes=[pltpu.VMEM((tp, 2 * LANES), F32)],
        compiler_params=_params("arbitrary", "arbitrary"),
    )(q, k, v, o, lse, dcat, p, g_attn)


def _qkv_bwd(p, dq, dk, dv, wq, wkv, gq, gkv, tables):
    nb_seq, _, tp, _ = dq.shape
    ht = tp // 2

    def body(pa_ref, dq_ref, dk_ref, dv_ref, wq_ref, wkv_ref, gq_ref, gkv_ref, cos_ref, sa_ref, sb_ref,
             dpa_ref, dwq_ref, dwkv_ref, dgq_ref, dgkv_ref):
        @pl.when(pl.program_id(0) == 0)
        def _():
            dwq_ref[...] = jnp.zeros_like(dwq_ref)
            dwkv_ref[...] = jnp.zeros_like(dwkv_ref)
            dgq_ref[...] = jnp.zeros_like(dgq_ref)
            dgkv_ref[...] = jnp.zeros_like(dgkv_ref)

        pa = pa_ref[...].astype(F32)
        gq, gkv = gq_ref[...], gkv_ref[...]
        cq_hat, rq = _rms_stats(pa[:, :Q_RANK])
        ckv_hat, rkv = _rms_stats(pa[:, Q_RANK:Q_RANK + KV_RANK])
        tabs = (cos_ref[...], sa_ref[...], sb_ref[...])

        pe = [dq_ref[0, h, :, D_NOPE:].astype(F32) for h in range(N_HEADS)]
        pairs = [_rope_t(pe[2 * i] + pltpu.roll(pe[2 * i + 1], D_ROPE, 1), *tabs).astype(BF16) for i in range(2)]
        dq_flat = jnp.concatenate([dq_ref[0, h, :, :D_NOPE] for h in range(N_HEADS)] + pairs, axis=1)
        dwq_ref[...] += _dot(dq_flat, (cq_hat * gq).astype(BF16), _TN)
        dcqn = _dot(dq_flat, wq_ref[...])
        dgq_ref[...] += jnp.sum(dcqn * cq_hat, axis=0, keepdims=True)
        dcq = _rms_bwd(gq * dcqn, cq_hat, rq)

        dkv_flat = jnp.concatenate([dk_ref[0, h, :, :D_NOPE] for h in range(N_HEADS)]
                                   + [dv_ref[0, h] for h in range(N_HEADS)], axis=1)
        dwkv_ref[...] += _dot((ckv_hat * gkv).astype(BF16), dkv_flat, _TN)
        dckvn = _dot(dkv_flat, wkv_ref[...], _NT)
        dgkv_ref[...] += jnp.sum(dckvn * ckv_hat, axis=0, keepdims=True)
        dckv = _rms_bwd(gkv * dckvn, ckv_hat, rkv)

        dk_pe = dk_ref[0, 0, :, D_NOPE:].astype(F32)
        for h in range(1, N_HEADS):
            dk_pe = dk_pe + dk_ref[0, h, :, D_NOPE:].astype(F32)
        dk_pe = jnp.where(lax.broadcasted_iota(jnp.int32, (ht, LANES), 1) < D_ROPE, dk_pe, 0.0)
        dpa_ref[...] = jnp.concatenate([dcq, dckv, _rope_t(dk_pe, *tabs)], axis=1).astype(BF16)

    full = lambda a: pl.BlockSpec(a.shape, lambda i: (0,) * a.ndim)
    tab = pl.BlockSpec((ht, LANES), lambda i: (i % 2, 0))
    qk = pl.BlockSpec((1, N_HEADS, ht, 2 * LANES), lambda i: (i // 2, 0, i % 2, 0))
    acc = lambda shape: pl.BlockSpec(shape, lambda i: (0, 0))
    return pl.pallas_call(
        body,
        name="qkv_bwd",
        grid=(2 * nb_seq,),
        in_specs=[pl.BlockSpec((ht, GRP_A), lambda i: (i, 0)), qk, qk,
                  pl.BlockSpec((1, N_HEADS, ht, D_V), lambda i: (i // 2, 0, i % 2, 0)),
                  full(wq), full(wkv), full(gq), full(gkv), tab, tab, tab],
        out_specs=(pl.BlockSpec((ht, GRP_A), lambda i: (i, 0)),
                   acc(wq.shape), acc(wkv.shape), acc((1, Q_RANK)), acc((1, KV_RANK))),
        out_shape=(
            jax.ShapeDtypeStruct((nb_seq * tp, GRP_A), BF16),
            jax.ShapeDtypeStruct(wq.shape, F32),
            jax.ShapeDtypeStruct(wkv.shape, F32),
            jax.ShapeDtypeStruct((1, Q_RANK), F32),
            jax.ShapeDtypeStruct((1, KV_RANK), F32),
        ),
        compiler_params=_params("arbitrary"),
    )(p, dq, dk, dv, wq, wkv, gq, gkv, *tables)


def _conv_bwd(p, dcat, conv_w, g_conv, nb_seq, tp):
    cols = CONV_WIDTH // LANES

    def body(b_ref, c_ref, h_ref, z_ref, dy_ref, w_ref, g_ref,
             db_ref, dc_ref, dh_ref, dz_ref, dw_ref, dg_ref):
        @pl.when(pl.program_id(1) == 0)
        def _():
            dw_ref[...] = jnp.zeros_like(dw_ref)
            dg_ref[...] = jnp.zeros_like(dg_ref)

        cb, c, h = b_ref[...].astype(F32), c_ref[...].astype(F32), h_ref[...].astype(F32)
        z, dy = z_ref[...].astype(F32), dy_ref[...].astype(F32)
        g = g_ref[...]
        w0, w1, w2 = w_ref[0:1, :], w_ref[1:2, :], w_ref[2:3, :]
        cc = c * h
        row = lax.broadcasted_iota(jnp.int32, (tp, LANES), 0)
        s1 = jnp.where(row >= 1, pltpu.roll(cc, 1, 0), 0.0)
        s2 = jnp.where(row >= 2, pltpu.roll(cc, 2, 0), 0.0)
        dwc = w0 * s2 + w1 * s1 + w2 * cc
        yc = cb * dwc
        r = lax.rsqrt(_group_mean(yc * yc) + EPS)
        ychat = yc * r
        sig = _sigmoid(z)
        dz_ref[...] = (dy * (ychat * g) * (sig * (1.0 + z * (1.0 - sig)))).astype(BF16)
        dyn = dy * (z * sig)
        dg_ref[...] += jnp.sum(dyn * ychat, axis=0, keepdims=True)
        gd = g * dyn
        dyc = r * (gd - ychat * _group_mean(gd * ychat))
        db_ref[...] = (dyc * dwc).astype(BF16)
        ddw = dyc * cb
        dw_ref[0:1, :] += jnp.sum(ddw * s2, axis=0, keepdims=True)
        dw_ref[1:2, :] += jnp.sum(ddw * s1, axis=0, keepdims=True)
        dw_ref[2:3, :] += jnp.sum(ddw * cc, axis=0, keepdims=True)
        u1 = jnp.where(row <= tp - 2, pltpu.roll(ddw, tp - 1, 0), 0.0)
        u2 = jnp.where(row <= tp - 3, pltpu.roll(ddw, tp - 2, 0), 0.0)
        dcc = w2 * ddw + w1 * u1 + w0 * u2
        dc_ref[...] = (dcc * h).astype(BF16)
        dh_ref[...] = (dcc * c).astype(BF16)

    col = pl.BlockSpec((tp, LANES), lambda t, b: (b, t))
    out = jax.ShapeDtypeStruct((nb_seq * tp, CONV_WIDTH), BF16)
    return pl.pallas_call(
        body,
        name="conv_bwd",
        grid=(cols, nb_seq),
        in_specs=_conv_specs(tp, lambda t, b, off: (b, off + t)) + [
            pl.BlockSpec((tp, LANES), lambda t, b: (b, N_HEADS * D_V // LANES + t)),
            pl.BlockSpec((8, LANES), lambda t, b: (0, t)),
            pl.BlockSpec((1, LANES), lambda t, b: (0, t))],
        out_specs=(col, col, col, col,
                   pl.BlockSpec((8, LANES), lambda t, b: (0, t)), pl.BlockSpec((1, LANES), lambda t, b: (0, t))),
        out_shape=(out, out, out, out,
                   jax.ShapeDtypeStruct((8, CONV_WIDTH), F32), jax.ShapeDtypeStruct((1, CONV_WIDTH), F32)),
        compiler_params=_params("arbitrary", "arbitrary"),
    )(p, p, p, p, dcat, conv_w, g_conv)


def _input_bwd(dps, w_in, x, meta, dh, norm_g, nt, send_in):
    nb_seq, s, d = x.shape
    r, kb = dps[0].shape
    ts = (s + LANES) // nt
    steps = nb_seq * nt
    n_dp = len(dps)
    in_slot = send_in.shape[1:]

    def body(*refs):
        dp_refs, w_ref, x_hbm, meta_ref, dh_ref, g_ref, pay_ref = refs[:n_dp], *refs[n_dp:n_dp + 6]
        o = n_dp + 6
        gx_hbm, dmeta_ref, dg_ref, r2_in = refs[o:o + 4]
        xbuf, gxbuf, tok_sems, own_in, r1_in, sum_in = refs[o + 4:o + 10]
        sems = refs[o + 10:]
        i = pl.program_id(0)
        b, k = i // nt, i % nt

        def plan():
            return _reduce_plan((pay_ref,), (own_in,), (r1_in,), (sum_in,), (r2_in,), *sems)

        @pl.when(i == 0)
        def _():
            dmeta_ref[...] = jnp.zeros_like(dmeta_ref)
            dg_ref[...] = jnp.zeros_like(dg_ref)
            plan()[0]()

        @pl.when(i == 1)
        def _():
            plan()[1]()

        def start(kk):
            if kk == 0:
                xbuf[0:PAD_FRONT, :] = jnp.zeros((PAD_FRONT, d), F32)
                xbuf[PAD_FRONT:LANES, :] = meta_ref[...]
            _token_copy(x_hbm, b, kk, ts, xbuf, tok_sems.at[0]).start()

        _for_tile(k, nt, start)
        du = _dot(dp_refs[0][...], w_ref[0:kb, :])
        for j in range(1, n_dp):
            du = du + _dot(dp_refs[j][...], w_ref[kb * j:kb * (j + 1), :])
        _for_tile(k, nt, lambda kk: _token_copy(x_hbm, b, kk, ts, xbuf, tok_sems.at[0]).wait())

        g = g_ref[...]
        hhat, rstd = _rms_stats(xbuf[...])
        dg_ref[...] += jnp.sum(du * hhat, axis=0, keepdims=True)
        res = _rms_bwd(g * du, hhat, rstd) + dh_ref[...].astype(F32)

        @pl.when(i > 0)
        def _():
            _for_tile(k, nt, lambda kk: _token_copy(gx_hbm, b, (kk - 1) % nt, ts, gxbuf, tok_sems.at[1], True).wait())

        gxbuf[...] = res

        @pl.when(k == 0)
        def _():
            dmeta_ref[...] += gxbuf[PAD_FRONT:LANES, :]

        _for_tile(k, nt, lambda kk: _token_copy(gx_hbm, b, kk, ts, gxbuf, tok_sems.at[1], True).start())

        @pl.when(i == steps - 1)
        def _():
            _token_copy(gx_hbm, b, nt - 1, ts, gxbuf, tok_sems.at[1], True).wait()
            plan()[2]()

    whole = lambda a: pl.BlockSpec(a.shape, lambda i: (0,) * a.ndim)
    hbm = pl.BlockSpec(memory_space=pl.ANY)
    return pl.pallas_call(
        body,
        name="input_bwd",
        grid=(steps,),
        in_specs=[pl.BlockSpec((ts, kb), lambda i: (i, 0)) for _ in dps]
        + [whole(w_in), hbm, whole(meta), pl.BlockSpec((ts, d), lambda i: (i, 0)), whole(norm_g), hbm],
        out_specs=(hbm, pl.BlockSpec((N_META, d), lambda i: (0, 0)), pl.BlockSpec((1, d), lambda i: (0, 0)), hbm),
        out_shape=(jax.ShapeDtypeStruct((nb_seq, s, d), F32),
                   jax.ShapeDtypeStruct((N_META, d), F32),
                   jax.ShapeDtypeStruct((1, d), F32),
                   jax.ShapeDtypeStruct((N_CHIPS,) + in_slot, BF16)),
        scratch_shapes=[pltpu.VMEM((ts, d), F32), pltpu.VMEM((ts, d), F32), pltpu.SemaphoreType.DMA((2,))]
        + _reduce_scratch([(in_slot, BF16)], [True]),
        compiler_params=_params("arbitrary"),
    )(*dps, w_in, x, meta, dh, norm_g, send_in)


def _in_proj_bwd_w(u, dps, bm, small_grads, send_out):
    r, d = u.shape
    kb = dps[0].shape[1]
    steps = r // bm
    n_dp, n_small = len(dps), len(small_grads)
    out_slot, small_slot = send_out.shape[1:], (SMALL_ROWS, LANES)

    def body(*refs):
        u_ref, dp_refs = refs[0], refs[1:1 + n_dp]
        small_refs = refs[1 + n_dp:1 + n_dp + n_small]
        o = 1 + n_dp + n_small
        pay_out, o_ref, r2_out, r2_small = refs[o:o + 4]
        acc_ref, ssmall, r1_out, sum_out, r1_small, sum_small = refs[o + 4:o + 10]
        sems = refs[o + 10:]
        i = pl.program_id(0)

        def plan():
            return _reduce_plan((pay_out, ssmall), (None, None), (r1_out, r1_small), (sum_out, sum_small),
                                (r2_out, r2_small), *sems)

        @pl.when(i == 0)
        def _():
            acc_ref[...] = jnp.zeros_like(acc_ref)
            _pack_small(ssmall, *small_refs)
            plan()[0]()

        @pl.when(i == 1)
        def _():
            plan()[1]()

        uu = u_ref[...]
        for j in range(n_dp):
            acc_ref[kb * j:kb * (j + 1), :] += _dot(dp_refs[j][...], uu, _TN)

        @pl.when(i == steps - 1)
        def _():
            for k in range(N_DEV):
                for s, e, c0 in _in_pieces(k):
                    o_ref[k, s:e, :] = acc_ref[c0:c0 + e - s, :].astype(BF16)
                o_ref[k, SHARD_IN:, :] = jnp.zeros((SHARD_IN_PAD - SHARD_IN, d), BF16)
            plan()[2]()

    whole = lambda a: pl.BlockSpec(a.shape, lambda i: (0,) * a.ndim)
    hbm = pl.BlockSpec(memory_space=pl.ANY)
    return pl.pallas_call(
        body,
        name="in_proj_bwd_w",
        grid=(steps,),
        in_specs=[pl.BlockSpec((bm, d), lambda i: (i, 0))]
        + [pl.BlockSpec((bm, kb), lambda i: (i, 0)) for _ in dps] + [whole(a) for a in small_grads]
        + [whole(send_out)],
        out_specs=(pl.BlockSpec((N_DEV, SHARD_IN_PAD, d), lambda i: (0, 0, 0)), hbm, hbm),
        out_shape=(jax.ShapeDtypeStruct((N_DEV, SHARD_IN_PAD, d), BF16),
                   jax.ShapeDtypeStruct((N_CHIPS,) + out_slot, BF16),
                   jax.ShapeDtypeStruct((N_CHIPS,) + small_slot, F32)),
        scratch_shapes=[pltpu.VMEM((kb * n_dp, d), F32), pltpu.VMEM((N_DEV,) + small_slot, F32)]
        + _reduce_scratch([(out_slot, BF16), (small_slot, F32)], [False, False]),
        compiler_params=_params("arbitrary"),
    )(u, *dps, *small_grads, send_out)


def _local_step(x, loss_target, u, p, meta_f, norm_g, w_in_p, q_norm_g, w_q_p, kv_norm_g, w_kv_p, conv_w_f,
                attn_out_g, conv_out_g, w_out_f, g_final):
    nb_seq, s, d = x.shape
    tp = s + LANES
    ht = tp // 2
    tables = _rope_tables(tp)

    q, k, v = _qkv_fwd(p, w_q_p, w_kv_p, q_norm_g, kv_norm_g, tables, nb_seq, tp)
    ya, o, lse = _attn_fwd(q, k, v, p, attn_out_g)
    yc = _conv_fwd(p, conv_w_f, conv_out_g, nb_seq, tp)
    dhb, d_final_g, loss_part = _out_proj_loss(ya, yc, w_out_f, x, loss_target, g_final, TOKEN_TILES)

    dcat, d_w_out = _out_proj_bwd(dhb, w_out_f, ya, yc, ht)
    send_out = d_w_out.reshape(N_DEV, SHARD_OUT, d)
    dq, dk, dv, dz_attn, d_attn_g = _attn_bwd(q, k, v, o, lse, dcat, p, attn_out_g)
    dpa, d_wq_p, d_wkv_p, d_gq, d_gkv = _qkv_bwd(p, dq, dk, dv, w_q_p, w_kv_p, q_norm_g, kv_norm_g, tables)
    d_b, d_c, d_h, dz_conv, d_conv_w, d_conv_g = _conv_bwd(p, dcat, conv_w_f, conv_out_g, nb_seq, tp)
    dps = (dpa, dz_attn, d_b, d_c, d_h, dz_conv)
    small = (d_wq_p, d_wkv_p, d_conv_w, d_final_g, d_gq, d_gkv, d_attn_g, d_conv_g, loss_part)
    send_in, r_out, r_small = _in_proj_bwd_w(u, dps, ht, small, send_out)
    grad_x, d_meta, d_norm_g, r_in = _input_bwd(dps, w_in_p, x, meta_f, dhb, norm_g, TOKEN_TILES, send_in)
    return grad_x, r_in, r_out, r_small, d_meta, d_norm_g


def kernel(x, meta_tokens, norm_g, w_in, q_norm_g, w_q_up, kv_norm_g, w_kv_up, conv_w, attn_out_g, conv_out_g, w_out, final_norm_g, loss_target, m_meta_tokens, m_norm_g, m_w_in, m_q_norm_g, m_w_q_up, m_kv_norm_g, m_w_kv_up, m_conv_w, m_attn_out_g, m_conv_out_g, m_w_out, m_final_norm_g, v_meta_tokens, v_norm_g, v_w_in, v_q_norm_g, v_w_q_up, v_kv_norm_g, v_w_kv_up, v_conv_w, v_attn_out_g, v_conv_out_g, v_w_out, v_final_norm_g):
    d = x.shape[-1]
    order = jnp.asarray(_tile_orders()[0])[2 * lax.axis_index("x") + lax.axis_index("y")]
    u, p, w_in_p, meta_f, w_q_p, w_kv_p, w_out_f, conv_w_f = _prep_in_proj(
        x, meta_tokens, norm_g, w_in[0].T, w_q_up[0].T, w_kv_up[0], w_out[0], conv_w.transpose(1, 0, 2), order)
    g_final = final_norm_g.reshape(1, d)
    grad_x, r_in, r_out, r_small, d_meta, d_norm_g = _local_step(
        x, loss_target, u, p, meta_f, norm_g, w_in_p, q_norm_g, w_q_p, kv_norm_g, w_kv_p, conv_w_f,
        attn_out_g, conv_out_g, w_out_f, g_final)

    flat = lambda a: a.reshape(a.shape[-2:]) if a.ndim == 3 else a.reshape(1, -1) if a.ndim == 1 else a
    transposed = ("w_in", "w_q_up")

    def to_kernel(n, a):
        if n == "conv_w":
            return a.transpose(1, 0, 2)
        return flat(a).T if n in transposed else flat(a)

    def from_kernel(n, a, shape):
        if n == "conv_w":
            return a.transpose(1, 0, 2)
        return (a.T if n in transposed else a).reshape(shape)
    params = {
        "meta_tokens": (meta_tokens, m_meta_tokens, v_meta_tokens),
        "norm_g": (norm_g, m_norm_g, v_norm_g),
        "w_in": (w_in, m_w_in, v_w_in),
        "q_norm_g": (q_norm_g, m_q_norm_g, v_q_norm_g),
        "w_q_up": (w_q_up, m_w_q_up, v_w_q_up),
        "kv_norm_g": (kv_norm_g, m_kv_norm_g, v_kv_norm_g),
        "w_kv_up": (w_kv_up, m_w_kv_up, v_w_kv_up),
        "conv_w": (conv_w, m_conv_w, v_conv_w),
        "attn_out_g": (attn_out_g, m_attn_out_g, v_attn_out_g),
        "conv_out_g": (conv_out_g, m_conv_out_g, v_conv_out_g),
        "w_out": (w_out, m_w_out, v_w_out),
        "final_norm_g": (final_norm_g, m_final_norm_g, v_final_norm_g),
    }
    grads, loss = _reduce_tail(r_in, r_out, r_small, d_meta, d_norm_g)
    updated = _adamw(grads, {n: tuple(to_kernel(n, a) for a in t) for n, t in params.items()})
    outs = [[from_kernel(n, updated[n][i], params[n][0].shape) for n, _ in PARAM_SHAPES] for i in range(4)]
    return (loss[0, 0], grad_x, *outs[0], *outs[1], *outs[2], *outs[3])
```

```python
import functools

import jax
import jax.numpy as jnp
import numpy as np
from jax import lax
from jax.experimental import pallas as pl
from jax.experimental.pallas import tpu as pltpu

F32 = jnp.float32
BF16 = jnp.bfloat16

N_META = 16
D_MODEL = 1024
N_HEADS = 4
D_NOPE = 128
D_ROPE = 64
D_V = 128
Q_RANK = 256
KV_RANK = 128
CONV_WIDTH = 512
CONV_GROUP = 64
ROPE_THETA = 10000.0
ATTN_SCALE = (D_NOPE + D_ROPE) ** -0.5
Q_SCALE = ATTN_SCALE * 1.4426950408889634
EPS = 1e-6
NEG_INF = -1e30

ADAM_LR = 0.001
ADAM_B1 = 0.9
ADAM_B2 = 0.999
ADAM_EPS = 1e-08
ADAM_WD = 0.01
ADAM_STEP = 10

LANES = 128
PAD_FRONT = LANES - N_META
K_TILE = 256
Q_TILE = 512
N_DEV = 8
VMEM_LIMIT = 56 * 1024 * 1024

IN_PAD = 3072
GRP_A = 512
N_A = Q_RANK + KV_RANK + D_ROPE
IN_PROJ = 3008
SHARD_IN = IN_PROJ // N_DEV
SHARD_IN_PAD = 384
SHARD_Q = 96
SHARD_KV = 128
SHARD_OUT = 128
SHARD_CONV = 64
SHARD_META = 128
Q_COLS = N_HEADS * (D_NOPE + D_ROPE)
KV_COLS = N_HEADS * (D_NOPE + D_V)

ROW_Q, ROW_KV, ROW_META, ROW_CONV = 0, 256, 384, 400
ROW_REPL = 408
ROW_NORM, ROW_FINAL, ROW_GQ, ROW_GKV, ROW_ATTN, ROW_CONVG, ROW_LOSS = 408, 416, 424, 426, 427, 431, 435
SMALL_ROWS = 440

PARAM_SHAPES = (
    ("meta_tokens", (N_META, SHARD_META)), ("norm_g", (1, D_MODEL)), ("w_in", (SHARD_IN, D_MODEL)),
    ("q_norm_g", (1, Q_RANK)), ("w_q_up", (SHARD_Q, Q_RANK)), ("kv_norm_g", (1, KV_RANK)),
    ("w_kv_up", (KV_RANK, SHARD_KV)), ("conv_w", (3, 1, SHARD_CONV)), ("attn_out_g", (1, CONV_WIDTH)),
    ("conv_out_g", (1, CONV_WIDTH)), ("w_out", (SHARD_OUT, D_MODEL)), ("final_norm_g", (1, D_MODEL)),
)


def _in_pieces(k):
    lo, hi = SHARD_IN * k, SHARD_IN * (k + 1)
    out = []
    if lo < N_A:
        out.append((0, min(hi, N_A) - lo, lo))
    if hi > N_A:
        s = max(lo, N_A)
        out.append((s - lo, hi - lo, s + GRP_A - N_A))
    return out


P_TILE = 512


def _tile_pieces(j):
    lo, hi = P_TILE * j, P_TILE * (j + 1)
    out = []
    for k in range(N_DEV):
        for s, e, d in _in_pieces(k):
            a, b = max(d, lo), min(d + e - s, hi)
            if a < b:
                out.append((k, s + a - d, s + b - d, a - lo))
    return out


def _tile_orders():
    n_tiles = IN_PAD // P_TILE
    sources = [{k for k, _, _, _ in _tile_pieces(j)} for j in range(n_tiles)]
    rows, n_early, n_free = [], n_tiles, n_tiles
    for chip in range(N_CHIPS):
        own = {2 * chip, 2 * chip + 1}
        diagonal = {2 * (N_CHIPS - 1 - chip), 2 * (N_CHIPS - 1 - chip) + 1}
        early = [j for j in range(n_tiles) if sources[j] <= own]
        late = [j for j in range(n_tiles) if sources[j] & diagonal]
        mid = [j for j in range(n_tiles) if j not in early and j not in late]
        rows.append(early + mid + late)
        n_early, n_free = min(n_early, len(early)), min(n_free, len(early) + len(mid))
    return np.asarray(rows, np.int32), n_early, n_free


def _q_pieces(k):
    lo, hi = SHARD_Q * k, SHARD_Q * (k + 1)
    out = []
    for h in range(N_HEADS):
        base = (D_NOPE + D_ROPE) * h
        s, e = max(lo, base), min(hi, base + D_NOPE)
        if s < e:
            out.append((s - lo, e - lo, D_NOPE * h + s - base))
        s, e = max(lo, base + D_NOPE), min(hi, base + D_NOPE + D_ROPE)
        if s < e:
            out.append((s - lo, e - lo, N_HEADS * D_NOPE + D_ROPE * h + s - base - D_NOPE))
    return out


def _kv_dst(k):
    return D_NOPE * (k // 2) + (N_HEADS * D_NOPE if k % 2 else 0)


def _params(*sem):
    return pltpu.CompilerParams(dimension_semantics=sem, vmem_limit_bytes=VMEM_LIMIT)


def _rms_stats(x):
    r = lax.rsqrt(jnp.mean(x * x, axis=-1, keepdims=True) + EPS)
    return x * r, r


def _rms_bwd(gdy, xhat, r):
    return r * (gdy - xhat * jnp.mean(gdy * xhat, axis=-1, keepdims=True))


def _sigmoid(z):
    return 1.0 / (1.0 + jnp.exp(-z))


def _group_mean(x):
    i0 = lax.broadcasted_iota(jnp.int32, (LANES, LANES), 0) // CONV_GROUP
    i1 = lax.broadcasted_iota(jnp.int32, (LANES, LANES), 1) // CONV_GROUP
    m = jnp.where(i0 == i1, 1.0 / CONV_GROUP, 0.0).astype(BF16)
    hi = x.astype(BF16)
    lo = (x - hi.astype(F32)).astype(BF16)
    return jnp.dot(hi, m, preferred_element_type=F32) + jnp.dot(lo, m, preferred_element_type=F32)


_NT = (((1,), (1,)), ((), ()))
_TN = (((0,), (0,)), ((), ()))


def _dot(a, b, dims=None):
    if dims is None:
        return jnp.dot(a, b, preferred_element_type=F32)
    return lax.dot_general(a, b, dims, preferred_element_type=F32)


def _device_position():
    x, y, c = lax.axis_index("x"), lax.axis_index("y"), lax.axis_index("c")
    return x, y, c, 4 * x + 2 * y + c


def _gather_plan(srcs, slots, send_sems, recv_sems, local_sems):
    x, y, c, _ = _device_position()
    me, sibling = (x, y, c), (x, y, 1 - c)
    flip = lambda v, on: v + on - 2 * v * on
    near = (flip(x, 1 - c), flip(y, c))
    far = (flip(x, c), flip(y, 1 - c))
    diag = (1 - x, 1 - y)
    n = len(srcs)

    def slot(a, px, py, pc):
        return slots[a].at[4 * px + 2 * py + pc]

    def copy(a, k, block, to, own=False):
        return pltpu.make_async_remote_copy(
            src_ref=srcs[a] if own else slot(a, *block),
            dst_ref=slot(a, *block),
            send_sem=send_sems.at[7 * a + k],
            recv_sem=recv_sems.at[7 * a + k],
            device_id=to,
            device_id_type=pl.DeviceIdType.MESH,
        )

    def local(a):
        return pltpu.make_async_copy(srcs[a], slot(a, *me), local_sems.at[a])

    sent = [(me, sibling), (me, (*near, c)), (me, (*far, c)), ((*near, c), (*far, c)),
            ((*near, c), sibling), ((*far, c), sibling), ((*diag, c), sibling)]
    landed = [sibling, (*near, c), (*far, c), (*diag, c), (*far, 1 - c), (*near, 1 - c), (*diag, 1 - c)]

    def send(a, k):
        return copy(a, k, *sent[k], own=k < 3)

    def arrival(a, k):
        return copy(a, k, landed[k], me)

    def start():
        for a in range(n):
            local(a).start()
            for k in range(3):
                send(a, k).start()

    def own():
        for a in range(n):
            local(a).wait()
            arrival(a, 0).wait_recv()

    def mid():
        for a in range(n):
            arrival(a, 1).wait_recv()
            send(a, 3).start()
            send(a, 4).start()
        for a in range(n):
            arrival(a, 2).wait_recv()
            send(a, 5).start()
        for a in range(n):
            for k in (4, 5):
                arrival(a, k).wait_recv()

    def late():
        for a in range(n):
            arrival(a, 3).wait_recv()
            send(a, 6).start()
        for a in range(n):
            arrival(a, 6).wait_recv()

    def finish():
        for a in range(n):
            for k in range(7):
                send(a, k).wait_send()

    return start, own, mid, late, finish


def _adam_update(g, w, m, v):
    m_new = ADAM_B1 * m + (1.0 - ADAM_B1) * g
    v_new = ADAM_B2 * v + (1.0 - ADAM_B2) * (g * g)
    m_hat = m_new / (1.0 - ADAM_B1 ** ADAM_STEP)
    v_hat = v_new / (1.0 - ADAM_B2 ** ADAM_STEP)
    return -ADAM_LR * (m_hat / (jnp.sqrt(v_hat) + ADAM_EPS) + ADAM_WD * w), m_new, v_new


def _adamw(grads, params):
    names = [n for n, _ in PARAM_SHAPES]
    n_p = len(names)

    def body(*refs):
        for i in range(n_p):
            g = refs[i][...]
            w, m, v = (refs[n_p + 3 * i + j][...] for j in range(3))
            delta, m_new, v_new = _adam_update(g, w, m, v)
            for j, val in enumerate((g, delta, m_new, v_new)):
                refs[4 * n_p + 4 * i + j][...] = val

    vm = pl.BlockSpec(memory_space=pltpu.VMEM)
    out_shape = []
    for _, shape in PARAM_SHAPES:
        out_shape += [jax.ShapeDtypeStruct(shape, F32)] * 4
    outs = pl.pallas_call(
        body,
        name="adamw",
        out_shape=tuple(out_shape),
        in_specs=[vm] * (4 * n_p),
        out_specs=(vm,) * (4 * n_p),
        compiler_params=pltpu.CompilerParams(vmem_limit_bytes=VMEM_LIMIT),
    )(*[grads[n] for n in names], *[a for n in names for a in params[n]])
    return {n: outs[4 * i:4 * i + 4] for i, n in enumerate(names)}


N_CHIPS = 4


def _reduce_plan(pays, owns, r1s, sums, r2s, send1, recv1, send2, recv2, local_sems):
    x, y, c, _ = _device_position()
    sibling = (x, y, 1 - c)
    chips = [((1 - x if rj & 2 else x), (1 - y if rj & 1 else y)) for rj in range(N_CHIPS)]
    n = len(pays)

    def slot_of(rj, core):
        return 4 * chips[rj][0] + 2 * chips[rj][1] + core

    def to_sibling(a, rj):
        return pltpu.make_async_remote_copy(
            src_ref=pays[a].at[slot_of(rj, 1 - c)], dst_ref=r1s[a].at[rj],
            send_sem=send1.at[N_CHIPS * a + rj], recv_sem=recv1.at[N_CHIPS * a + rj],
            device_id=sibling, device_id_type=pl.DeviceIdType.MESH)

    def load_own(a, rj):
        return pltpu.make_async_copy(pays[a].at[slot_of(rj, c)], owns[a].at[rj], local_sems.at[2 * N_CHIPS * a + rj])

    def to_chip(a, rj):
        return pltpu.make_async_remote_copy(
            src_ref=sums[a].at[rj], dst_ref=r2s[a].at[rj],
            send_sem=send2.at[N_CHIPS * a + rj], recv_sem=recv2.at[N_CHIPS * a + rj],
            device_id=(*chips[rj], c), device_id_type=pl.DeviceIdType.MESH)

    def keep(a):
        return pltpu.make_async_copy(sums[a].at[0], r2s[a].at[0], local_sems.at[2 * N_CHIPS * a + N_CHIPS])

    def start():
        for a in range(n):
            for rj in range(N_CHIPS):
                to_sibling(a, rj).start()
                if owns[a] is not None:
                    load_own(a, rj).start()

    def combine():
        for a in range(n):
            for rj in range(N_CHIPS):
                to_sibling(a, rj).wait_recv()
                if owns[a] is not None:
                    load_own(a, rj).wait()
                    mine = owns[a][rj]
                else:
                    mine = pays[a][slot_of(rj, c)]
                sums[a][rj] = (mine.astype(F32) + r1s[a][rj].astype(F32)).astype(sums[a].dtype)
            keep(a).start()
            for rj in range(1, N_CHIPS):
                to_chip(a, rj).start()

    def finish():
        for a in range(n):
            for rj in range(1, N_CHIPS):
                to_chip(a, rj).wait_recv()
            for rj in range(N_CHIPS):
                to_sibling(a, rj).wait_send()
            for rj in range(1, N_CHIPS):
                to_chip(a, rj).wait_send()
            keep(a).wait()

    return start, combine, finish


def _reduce_scratch(shapes_dtypes, own_flags):
    out = []
    for (shape, dtype), own in zip(shapes_dtypes, own_flags):
        if own:
            out.append(pltpu.VMEM((N_CHIPS,) + shape, dtype))
        out += [pltpu.VMEM((N_CHIPS,) + shape, dtype), pltpu.VMEM((N_CHIPS,) + shape, dtype)]
    n = len(shapes_dtypes)
    out += [pltpu.SemaphoreType.DMA((N_CHIPS * n,))] * 4 + [pltpu.SemaphoreType.DMA((2 * N_CHIPS * n,))]
    return out


def _pack_small(ssmall, dwq, dwkv, dconv, dfinal, dgq, dgkv, dattn, dconvg, loss_part):
    ssmall[...] = jnp.zeros_like(ssmall)
    rep = ssmall.at[0]
    for i in range(D_MODEL // LANES):
        rep[ROW_FINAL + i:ROW_FINAL + i + 1, :] = dfinal[:, LANES * i:LANES * (i + 1)]
    for i in range(Q_RANK // LANES):
        rep[ROW_GQ + i:ROW_GQ + i + 1, :] = dgq[:, LANES * i:LANES * (i + 1)]
    rep[ROW_GKV:ROW_GKV + 1, :] = dgkv[...]
    for i in range(CONV_WIDTH // LANES):
        rep[ROW_ATTN + i:ROW_ATTN + i + 1, :] = dattn[:, LANES * i:LANES * (i + 1)]
        rep[ROW_CONVG + i:ROW_CONVG + i + 1, :] = dconvg[:, LANES * i:LANES * (i + 1)]
    rep[ROW_LOSS:ROW_LOSS + 1, :] = loss_part[...]
    for k in range(N_DEV):
        if k:
            ssmall[k, ROW_REPL:, :] = ssmall[0, ROW_REPL:, :]
        for s, e, d in _q_pieces(k):
            for i in range(Q_RANK // LANES):
                ssmall[k, ROW_Q + SHARD_Q * i + s:ROW_Q + SHARD_Q * i + e, :] = dwq[d:d + e - s, LANES * i:LANES * (i + 1)]
        ssmall[k, ROW_KV:ROW_KV + KV_RANK, :] = dwkv[:, _kv_dst(k):_kv_dst(k) + SHARD_KV]
        ssmall[k, ROW_CONV:ROW_CONV + 3, 0:SHARD_CONV] = dconv[0:3, SHARD_CONV * k:SHARD_CONV * (k + 1)]


TOKEN_TILES = 4
TAIL_ROWS = N_META + D_MODEL // LANES


def _reduce_tail(r_in, r_out, r_small, d_meta, d_norm):
    n_p = len(PARAM_SHAPES)
    names = [n for n, _ in PARAM_SHAPES]

    def body(*refs):
        rin, rout, rsmall, dmeta, dnorm = refs[:5]
        g_out = {n: refs[5 + i] for i, n in enumerate(names)}
        loss_out = refs[5 + n_p]
        stail, rtail, gsum, gtail, send_sems, recv_sems = refs[6 + n_p:]
        x, y, c, me = _device_position()
        my_chip = 2 * x + y

        for k in range(N_DEV):
            stail[k, 0:N_META, :] = dmeta[:, SHARD_META * k:SHARD_META * (k + 1)]
            for i in range(D_MODEL // LANES):
                stail[k, N_META + i:N_META + i + 1, :] = dnorm[:, LANES * i:LANES * (i + 1)]
        copies = []
        for r in range(1, N_DEV):
            peer = (1 - x if r & 4 else x, 1 - y if r & 2 else y, 1 - c if r & 1 else c)
            copies.append(pltpu.make_async_remote_copy(
                src_ref=stail.at[4 * peer[0] + 2 * peer[1] + peer[2]],
                dst_ref=rtail.at[r],
                send_sem=send_sems.at[r - 1],
                recv_sem=recv_sems.at[r - 1],
                device_id=peer,
                device_id_type=pl.DeviceIdType.MESH,
            ))
        for cp in copies:
            cp.start()
        rtail[0] = stail[me]

        g = rin[my_chip].astype(F32)
        for ch in range(1, N_CHIPS):
            g = g + rin[ch ^ my_chip].astype(F32)
        g_out["w_in"][...] = g[:SHARD_IN, :]

        g = rout[my_chip].astype(F32)
        gs = rsmall[my_chip]
        for ch in range(1, N_CHIPS):
            g = g + rout[ch ^ my_chip].astype(F32)
            gs = gs + rsmall[ch ^ my_chip]
        g_out["w_out"][...] = g
        gsum[...] = gs
        for i in range(Q_RANK // LANES):
            g_out["w_q_up"][:, LANES * i:LANES * (i + 1)] = gsum[ROW_Q + SHARD_Q * i:ROW_Q + SHARD_Q * (i + 1), :]
        g_out["w_kv_up"][...] = gsum[ROW_KV:ROW_KV + KV_RANK, :]
        for i in range(3):
            g_out["conv_w"][i] = gsum[ROW_CONV + i:ROW_CONV + i + 1, 0:SHARD_CONV]
        for name, row, width in (("final_norm_g", ROW_FINAL, D_MODEL), ("q_norm_g", ROW_GQ, Q_RANK),
                                 ("kv_norm_g", ROW_GKV, KV_RANK), ("attn_out_g", ROW_ATTN, CONV_WIDTH),
                                 ("conv_out_g", ROW_CONVG, CONV_WIDTH)):
            for i in range(width // LANES):
                g_out[name][:, LANES * i:LANES * (i + 1)] = gsum[row + i:row + i + 1, :]
        loss_out[...] = gsum[ROW_LOSS:ROW_LOSS + 1, :]

        for cp in copies:
            cp.wait_recv()
        gt = rtail[me]
        for d in range(1, N_DEV):
            gt = gt + rtail[d ^ me]
        gtail[...] = gt
        g_out["meta_tokens"][...] = gtail[0:N_META, :]
        for i in range(D_MODEL // LANES):
            g_out["norm_g"][:, LANES * i:LANES * (i + 1)] = gtail[N_META + i:N_META + i + 1, :]
        for cp in copies:
            cp.wait_send()

    vm = pl.BlockSpec(memory_space=pltpu.VMEM)
    out_shape = [jax.ShapeDtypeStruct(shape, F32) for _, shape in PARAM_SHAPES]
    out_shape.append(jax.ShapeDtypeStruct((1, LANES), F32))
    outs = pl.pallas_call(
        body,
        name="reduce_tail",
        out_shape=tuple(out_shape),
        in_specs=[vm] * 5,
        out_specs=(vm,) * len(out_shape),
        scratch_shapes=[
            pltpu.VMEM((N_DEV, TAIL_ROWS, LANES), F32),
            pltpu.VMEM((N_DEV, TAIL_ROWS, LANES), F32),
            pltpu.VMEM((SMALL_ROWS, LANES), F32),
            pltpu.VMEM((TAIL_ROWS, LANES), F32),
            pltpu.SemaphoreType.DMA((N_DEV - 1,)),
            pltpu.SemaphoreType.DMA((N_DEV - 1,)),
        ],
        compiler_params=pltpu.CompilerParams(vmem_limit_bytes=VMEM_LIMIT),
    )(r_in, r_out, r_small, d_meta, d_norm)
    return {n: outs[i] for i, n in enumerate(names)}, outs[-1]


def _prep_in_proj(x, meta, norm_g, w_in_t, w_q, w_kv, w_out, conv_w, order):
    nb_seq, s, d = x.shape
    tp = s + LANES
    m = nb_seq * tp
    ts = s // TOKEN_TILES
    n_real = nb_seq * TOKEN_TILES
    n_norm = n_real + 1
    n_tiles = IN_PAD // P_TILE
    _, n_early, n_free = _tile_orders()
    steps = n_norm + n_tiles
    dot_rows = m // 4
    qkv_shape = (SHARD_Q + KV_RANK, Q_RANK)

    def tile(t):
        t = jnp.minimum(t, n_real - 1)
        return t // TOKEN_TILES, t % TOKEN_TILES

    def column_tile(t, order_ref):
        return order_ref[jnp.maximum(t - n_norm, 0)]

    def body(order_ref, x_ref, meta_ref, g_ref, win_ref, wq_ref, wkv_ref, wout_ref, conv_ref,
             u_hbm, p_ref, w_in_p, meta_f, w_q_p, w_kv_p, w_out_f, conv_f,
             u_all, sbig, gbig, smeta, gmeta, sqkv, sout, sconv, gqkv, gout, gconv,
             send_in, recv_in, local_in, send_meta, recv_meta, send_rest, recv_rest, local_rest, u_sem):
        t = pl.program_id(0)
        px, py, pc, me = _device_position()
        u_copy = pltpu.make_async_copy(u_all, u_hbm, u_sem)

        def plan_in():
            return _gather_plan((sbig,), (gbig,), send_in, recv_in, local_in)

        def plan_rest():
            return _gather_plan((sqkv, sout, sconv), (gqkv, gout, gconv), send_rest, recv_rest, local_rest)

        def meta_copies():
            out = []
            for r in range(1, N_DEV):
                peer = (1 - px if r & 4 else px, 1 - py if r & 2 else py, 1 - pc if r & 1 else pc)
                out.append(pltpu.make_async_remote_copy(
                    src_ref=smeta,
                    dst_ref=gmeta.at[r],
                    send_sem=send_meta.at[r - 1],
                    recv_sem=recv_meta.at[r - 1],
                    device_id=peer,
                    device_id_type=pl.DeviceIdType.MESH,
                ))
            return out

        @pl.when(t == 0)
        def _():
            sbig[0:SHARD_IN, :] = win_ref[...].astype(BF16)
            sbig[SHARD_IN:, :] = jnp.zeros((SHARD_IN_PAD - SHARD_IN, d), BF16)
            smeta[...] = meta_ref[...]
            plan_in()[0]()
            for cp in meta_copies():
                cp.start()
            sqkv[...] = jnp.zeros_like(sqkv)
            sqkv[0:SHARD_Q, :] = wq_ref[...].astype(BF16)
            sqkv[SHARD_Q:, 0:SHARD_KV] = wkv_ref[...].astype(BF16)
            sout[...] = wout_ref[...].astype(BF16)
            sconv[...] = jnp.zeros_like(sconv)
            for i in range(3):
                sconv[i:i + 1, 0:SHARD_CONV] = conv_ref[i]

        def norm(h):
            hhat, _ = _rms_stats(h)
            return (hhat * g_ref[...]).astype(BF16)

        @pl.when(t < n_real)
        def _():
            b, k = tile(t)
            row0 = pl.multiple_of(b * tp + LANES + k * ts, 16)
            u_all[pl.ds(row0, ts), :] = norm(x_ref[0])

        @pl.when(t == n_real)
        def _():
            for cp in meta_copies():
                cp.wait_recv()
            gmeta[0] = smeta[...]
            for k in range(N_DEV):
                meta_f[:, SHARD_META * k:SHARD_META * (k + 1)] = gmeta[k ^ me]
            um = norm(meta_f[...])
            for b in range(nb_seq):
                u_all[b * tp:b * tp + PAD_FRONT, :] = jnp.zeros((PAD_FRONT, d), BF16)
                u_all[b * tp + PAD_FRONT:b * tp + LANES, :] = um
            u_copy.start()

        @pl.when(t == n_norm)
        def _():
            plan_in()[1]()

        @pl.when(t == n_norm + n_early)
        def _():
            plan_in()[2]()
            plan_rest()[0]()

        @pl.when(t == n_norm + (n_early + n_free) // 2)
        def _():
            plan_rest()[1]()
            plan_rest()[2]()

        @pl.when(t == n_norm + n_free)
        def _():
            plan_in()[3]()

        @pl.when(t == n_norm + n_free + 2)
        def _():
            plan_rest()[3]()

        @pl.when(t >= n_norm)
        def _():
            j = column_tile(t, order_ref)
            for jj in range(n_tiles):
                @pl.when(j == jj)
                def _(jj=jj):
                    if jj == N_A // P_TILE:
                        w_in_p[N_A % P_TILE:, :] = jnp.zeros((P_TILE - N_A % P_TILE, d), BF16)
                    for k, s0, e0, d0 in _tile_pieces(jj):
                        w_in_p[d0:d0 + e0 - s0, :] = gbig[k, s0:e0, :]
            for r in range(m // dot_rows):
                rows = slice(dot_rows * r, dot_rows * (r + 1))
                p_ref[rows, :] = _dot(u_all[rows, :], w_in_p[...], _NT).astype(BF16)

        @pl.when(t == steps - 1)
        def _():
            plan_in()[4]()
            plan_rest()[4]()
            for cp in meta_copies():
                cp.wait_send()
            u_copy.wait()
            conv_f[...] = jnp.zeros_like(conv_f)
            for k in range(N_DEV):
                for s0, e0, d0 in _q_pieces(k):
                    w_q_p[d0:d0 + e0 - s0, :] = gqkv[k, s0:e0, :]
                w_kv_p[:, _kv_dst(k):_kv_dst(k) + SHARD_KV] = gqkv[k, SHARD_Q:, 0:SHARD_KV]
                w_out_f[SHARD_OUT * k:SHARD_OUT * (k + 1), :] = gout[k]
                conv_f[0:3, SHARD_CONV * k:SHARD_CONV * (k + 1)] = gconv[k, 0:3, 0:SHARD_CONV]

    whole = lambda shape: pl.BlockSpec(shape, lambda t, o: (0,) * len(shape))
    return pl.pallas_call(
        body,
        name="prep_in_proj_gather",
        grid_spec=pltpu.PrefetchScalarGridSpec(
            num_scalar_prefetch=1,
            grid=(steps,),
            in_specs=[
                pl.BlockSpec((1, ts, d), lambda t, o: (*tile(t), 0)),
                whole(meta.shape), whole(norm_g.shape), whole(w_in_t.shape),
                whole(w_q.shape), whole(w_kv.shape), whole(w_out.shape), whole(conv_w.shape),
            ],
            out_specs=(pl.BlockSpec(memory_space=pl.ANY),
                       pl.BlockSpec((m, P_TILE), lambda t, o: (0, column_tile(t, o))),
                       pl.BlockSpec((P_TILE, d), lambda t, o: (column_tile(t, o), 0)),
                       whole((N_META, d)),
                       whole((Q_COLS, Q_RANK)), whole((KV_RANK, KV_COLS)), whole((D_MODEL, D_MODEL)),
                       whole((8, CONV_WIDTH))),
            scratch_shapes=[
                pltpu.VMEM((m, d), BF16),
                pltpu.VMEM((SHARD_IN_PAD, d), BF16),
                pltpu.VMEM((N_DEV, SHARD_IN_PAD, d), BF16),
                pltpu.VMEM((N_META, SHARD_META), F32),
                pltpu.VMEM((N_DEV, N_META, SHARD_META), F32),
                pltpu.VMEM(qkv_shape, BF16),
                pltpu.VMEM((SHARD_OUT, D_MODEL), BF16),
                pltpu.VMEM((8, LANES), F32),
                pltpu.VMEM((N_DEV,) + qkv_shape, BF16),
                pltpu.VMEM((N_DEV, SHARD_OUT, D_MODEL), BF16),
                pltpu.VMEM((N_DEV, 8, LANES), F32),
                pltpu.SemaphoreType.DMA((7,)),
                pltpu.SemaphoreType.DMA((7,)),
                pltpu.SemaphoreType.DMA((1,)),
                pltpu.SemaphoreType.DMA((N_DEV - 1,)),
                pltpu.SemaphoreType.DMA((N_DEV - 1,)),
                pltpu.SemaphoreType.DMA((21,)),
                pltpu.SemaphoreType.DMA((21,)),
                pltpu.SemaphoreType.DMA((3,)),
                pltpu.SemaphoreType.DMA,
            ],
        ),
        out_shape=(jax.ShapeDtypeStruct((m, d), BF16),
                   jax.ShapeDtypeStruct((m, IN_PAD), BF16),
                   jax.ShapeDtypeStruct((IN_PAD, d), BF16),
                   jax.ShapeDtypeStruct((N_META, d), F32),
                   jax.ShapeDtypeStruct((Q_COLS, Q_RANK), BF16),
                   jax.ShapeDtypeStruct((KV_RANK, KV_COLS), BF16),
                   jax.ShapeDtypeStruct((D_MODEL, D_MODEL), BF16),
                   jax.ShapeDtypeStruct((8, CONV_WIDTH), F32)),
        compiler_params=_params("arbitrary"),
    )(order, x, meta, norm_g, w_in_t, w_q, w_kv, w_out, conv_w)


def _rope_tables(tp):
    half = D_ROPE // 2
    inv_freq = (1.0 / (ROPE_THETA ** (np.arange(half, dtype=np.float32) / half))).astype(np.float32)
    pos = (np.arange(tp) - PAD_FRONT).astype(np.float32)
    ang = pos[:, None] * inv_freq[None, :]
    cos = np.tile(np.cos(ang), (1, LANES // half))
    sin = np.tile(np.sin(ang), (1, LANES // half))
    first = (np.arange(LANES) % D_ROPE) < half
    zero = np.float32(0.0)
    return tuple(jnp.asarray(t, F32) for t in (cos, np.where(first, -sin, zero), np.where(first, zero, sin)))


def _rope(t, cos, sa, sb):
    return t * cos + pltpu.roll(t, LANES - D_ROPE // 2, 1) * sa + pltpu.roll(t, D_ROPE // 2, 1) * sb


def _rope_t(t, cos, sa, sb):
    return t * cos + pltpu.roll(t * sa, D_ROPE // 2, 1) + pltpu.roll(t * sb, LANES - D_ROPE // 2, 1)


def _qkv_fwd(p, wq, wkv, gq, gkv, tables, nb_seq, tp):
    ht = tp // 2

    def body(pa_ref, wq_ref, wkv_ref, gq_ref, gkv_ref, cos_ref, sa_ref, sb_ref, q_ref, k_ref, v_ref):
        pa = pa_ref[...].astype(F32)
        cq_hat, _ = _rms_stats(pa[:, :Q_RANK])
        ckv_hat, _ = _rms_stats(pa[:, Q_RANK:Q_RANK + KV_RANK])
        q = _dot((cq_hat * gq_ref[...]).astype(BF16), wq_ref[...], _NT) * Q_SCALE
        kv = _dot((ckv_hat * gkv_ref[...]).astype(BF16), wkv_ref[...])
        tabs = (cos_ref[...], sa_ref[...], sb_ref[...])
        lane = lax.broadcasted_iota(jnp.int32, (ht, LANES), 1)
        low = lane < D_ROPE
        mark = lane == D_ROPE
        row = (pl.program_id(0) % 2) * ht + lax.broadcasted_iota(jnp.int32, (ht, LANES), 0)
        k_pe = jnp.where(mark & (row < PAD_FRONT), NEG_INF, _rope(pa[:, Q_RANK + KV_RANK:], *tabs))
        one = jnp.where(mark & (row >= PAD_FRONT), 1.0, 0.0)
        pairs = [_rope(q[:, N_HEADS * D_NOPE + LANES * i:N_HEADS * D_NOPE + LANES * (i + 1)], *tabs) for i in range(2)]
        for h in range(N_HEADS):
            pair = pairs[h // 2]
            if h % 2:
                pair = pltpu.roll(pair, D_ROPE, 1)
            pe = jnp.where(low, pair, one)
            q_ref[0, h] = jnp.concatenate([q[:, D_NOPE * h:D_NOPE * (h + 1)], pe], axis=1).astype(BF16)
            k_ref[0, h] = jnp.concatenate([kv[:, D_NOPE * h:D_NOPE * (h + 1)], k_pe], axis=1).astype(BF16)
            v_ref[0, h] = kv[:, N_HEADS * D_NOPE + D_V * h:N_HEADS * D_NOPE + D_V * (h + 1)].astype(BF16)

    full = lambda a: pl.BlockSpec(a.shape, lambda i: (0,) * a.ndim)
    tab = pl.BlockSpec((ht, LANES), lambda i: (i % 2, 0))
    qk = pl.BlockSpec((1, N_HEADS, ht, 2 * LANES), lambda i: (i // 2, 0, i % 2, 0))
    return pl.pallas_call(
        body,
        name="qkv_fwd",
        grid=(2 * nb_seq,),
        in_specs=[pl.BlockSpec((ht, GRP_A), lambda i: (i, 0)), full(wq), full(wkv), full(gq), full(gkv), tab, tab, tab],
        out_specs=(qk, qk, pl.BlockSpec((1, N_HEADS, ht, D_V), lambda i: (i // 2, 0, i % 2, 0))),
        out_shape=(
            jax.ShapeDtypeStruct((nb_seq, N_HEADS, tp, 2 * LANES), BF16),
            jax.ShapeDtypeStruct((nb_seq, N_HEADS, tp, 2 * LANES), BF16),
            jax.ShapeDtypeStruct((nb_seq, N_HEADS, tp, D_V), BF16),
        ),
        compiler_params=_params("parallel"),
    )(p, wq, wkv, gq, gkv, *tables)


def _attn_fwd(q, k, v, p, g_attn):
    nb_seq, _, tp, _ = q.shape

    def body(q_ref, k_ref, v_ref, z_ref, g_ref, y_ref, o_ref, lse_ref):
        g = g_ref[...]
        for r0 in range(0, tp, Q_TILE):
            nq = min(Q_TILE, tp - r0)
            kend = r0 + nq
            qq = q_ref[0, 0, r0:kend, :]
            sd = _dot(qq, k_ref[0, 0, r0:kend, :], _NT)
            causal = (lax.broadcasted_iota(jnp.int32, (nq, nq), 1) <= lax.broadcasted_iota(jnp.int32, (nq, nq), 0))
            sd = jnp.where(causal, sd, NEG_INF)
            m = jnp.max(sd, axis=-1, keepdims=True)
            if r0:
                so = _dot(qq, k_ref[0, 0, 0:r0, :], _NT)
                m = jnp.maximum(m, jnp.max(so, axis=-1, keepdims=True))
            ed = jnp.exp2(sd - m)
            l = jnp.sum(ed, axis=-1, keepdims=True)
            o = _dot(ed.astype(BF16), v_ref[0, 0, r0:kend, :])
            if r0:
                eo = jnp.exp2(so - m)
                l = l + jnp.sum(eo, axis=-1, keepdims=True)
                o = o + _dot(eo.astype(BF16), v_ref[0, 0, 0:r0, :])
            o = o * (1.0 / l)
            o_ref[0, 0, r0:kend, :] = o
            lse_ref[0, 0, r0:kend, :] = jnp.broadcast_to(m + jnp.log2(l), (nq, LANES))
            ohat, _ = _rms_stats(o)
            z = z_ref[r0:kend, :].astype(F32)
            y_ref[r0:kend, :] = (ohat * g * (z * _sigmoid(z))).astype(BF16)

    qk = pl.BlockSpec((1, 1, tp, 2 * LANES), lambda b, h: (b, h, 0, 0))
    hv = pl.BlockSpec((1, 1, tp, D_V), lambda b, h: (b, h, 0, 0))
    return pl.pallas_call(
        body,
        name="attn_fwd",
        grid=(nb_seq, N_HEADS),
        in_specs=[qk, qk, hv,
                  pl.BlockSpec((tp, LANES), lambda b, h: (b, GRP_A // LANES + h)),
                  pl.BlockSpec((1, LANES), lambda b, h: (0, h))],
        out_specs=(pl.BlockSpec((tp, LANES), lambda b, h: (b, h)), hv, hv),
        out_shape=(
            jax.ShapeDtypeStruct((nb_seq * tp, N_HEADS * D_V), BF16),
            jax.ShapeDtypeStruct((nb_seq, N_HEADS, tp, D_V), F32),
            jax.ShapeDtypeStruct((nb_seq, N_HEADS, tp, LANES), F32),
        ),
        compiler_params=_params("parallel", "parallel"),
    )(q, k, v, p, g_attn)


_CONV_COL0 = (GRP_A + N_HEADS * D_V) // LANES


def _conv_specs(tp, order):
    cols = CONV_WIDTH // LANES
    return [pl.BlockSpec((tp, LANES), functools.partial(
        lambda a, b, off: order(a, b, off), off=_CONV_COL0 + i * cols)) for i in range(4)]


def _conv_fwd(p, conv_w, g_conv, nb_seq, tp):
    def body(b_ref, c_ref, h_ref, z_ref, w_ref, g_ref, y_ref):
        cc = c_ref[...].astype(F32) * h_ref[...].astype(F32)
        row = lax.broadcasted_iota(jnp.int32, (tp, LANES), 0)
        s1 = jnp.where(row >= 1, pltpu.roll(cc, 1, 0), 0.0)
        s2 = jnp.where(row >= 2, pltpu.roll(cc, 2, 0), 0.0)
        yc = b_ref[...].astype(F32) * (w_ref[0:1, :] * s2 + w_ref[1:2, :] * s1 + w_ref[2:3, :] * cc)
        r = lax.rsqrt(_group_mean(yc * yc) + EPS)
        z = z_ref[...].astype(F32)
        y_ref[...] = (yc * r * g_ref[...] * (z * _sigmoid(z))).astype(BF16)

    return pl.pallas_call(
        body,
        name="conv_fwd",
        grid=(nb_seq, CONV_WIDTH // LANES),
        in_specs=_conv_specs(tp, lambda b, t, off: (b, off + t)) + [
            pl.BlockSpec((8, LANES), lambda b, t: (0, t)),
            pl.BlockSpec((1, LANES), lambda b, t: (0, t))],
        out_specs=pl.BlockSpec((tp, LANES), lambda b, t: (b, t)),
        out_shape=jax.ShapeDtypeStruct((nb_seq * tp, CONV_WIDTH), BF16),
        compiler_params=_params("parallel", "parallel"),
    )(p, p, p, p, conv_w, g_conv)


def _token_copy(hbm, b, k, ts, buf, sem, to_hbm=False):
    lo, hi = max(k * ts - LANES, 0), (k + 1) * ts - LANES
    off = lo - (k * ts - LANES)
    src, dst = hbm.at[b, pl.ds(lo, hi - lo)], buf.at[pl.ds(off, hi - lo)]
    if to_hbm:
        src, dst = dst, src
    return pltpu.make_async_copy(src, dst, sem)


def _for_tile(k, nt, fn):
    for kk in range(nt):
        @pl.when(k == kk)
        def _(kk=kk):
            fn(kk)


def _out_proj_loss(ya, yc, w_out, x, target, g_final, nt):
    nb_seq, s, d = x.shape
    r, ka = ya.shape
    ts = (s + LANES) // nt
    steps = nb_seq * nt

    def body(a_ref, c_ref, w_ref, x_hbm, t_hbm, g_ref, dhb_ref, dg_ref, loss_ref,
             xbuf, tbuf, acc_ref, sems):
        i = pl.program_id(0)
        b, k = i // nt, i % nt

        @pl.when(i == 0)
        def _():
            acc_ref[...] = jnp.zeros_like(acc_ref)
            dg_ref[...] = jnp.zeros_like(dg_ref)

        slot = i % 2

        def fetch(seq, kk, sl):
            return [_token_copy(x_hbm, seq, kk, ts, xbuf.at[sl], sems.at[sl, 0]),
                    _token_copy(t_hbm, seq, kk, ts, tbuf.at[sl], sems.at[sl, 1])]

        def start(seq, sl, kk):
            if kk == 0:
                xbuf[sl, 0:LANES, :] = jnp.zeros((LANES, d), F32)
                tbuf[sl, 0:LANES, :] = jnp.zeros((LANES, d), F32)
            for cp in fetch(seq, kk, sl):
                cp.start()

        @pl.when(i == 0)
        def _():
            start(0, 0, 0)

        @pl.when(i + 1 < steps)
        def _():
            _for_tile((i + 1) % nt, nt, functools.partial(start, (i + 1) // nt, 1 - slot))

        mix = _dot(a_ref[...], w_ref[0:ka, :]) + _dot(c_ref[...], w_ref[ka:, :])
        _for_tile(k, nt, lambda kk: [cp.wait() for cp in fetch(b, kk, slot)])

        real = (lax.broadcasted_iota(jnp.int32, (ts, d), 0) >= LANES) | (k > 0)
        g = g_ref[...]
        hhat, rstd = _rms_stats(xbuf[slot] + mix)
        e = jnp.where(real, hhat * g - tbuf[slot], 0.0)
        acc_ref[...] += jnp.sum(e * e, axis=0, keepdims=True)
        dy = e * (1.0 / d)
        dg_ref[...] += jnp.sum(dy * hhat, axis=0, keepdims=True)
        dhb_ref[...] = _rms_bwd(g * dy, hhat, rstd).astype(BF16)

        @pl.when(i == steps - 1)
        def _():
            total = jnp.sum(acc_ref[...], axis=1, keepdims=True)
            loss_ref[...] = jnp.broadcast_to((0.5 / d) * total, loss_ref.shape)

    hbm = pl.BlockSpec(memory_space=pl.ANY)
    row = pl.BlockSpec((ts, d), lambda i: (i, 0))
    vec = pl.BlockSpec((1, d), lambda i: (0, 0))
    return pl.pallas_call(
        body,
        name="out_proj_loss",
        grid=(steps,),
        in_specs=[pl.BlockSpec((ts, ka), lambda i: (i, 0)), pl.BlockSpec((ts, yc.shape[1]), lambda i: (i, 0)),
                  pl.BlockSpec(w_out.shape, lambda i: (0, 0)), hbm, hbm, vec],
        out_specs=(row, vec, pl.BlockSpec((1, LANES), lambda i: (0, 0))),
        out_shape=(
            jax.ShapeDtypeStruct((r, d), BF16),
            jax.ShapeDtypeStruct((1, d), F32),
            jax.ShapeDtypeStruct((1, LANES), F32),
        ),
        scratch_shapes=[pltpu.VMEM((2, ts, d), F32), pltpu.VMEM((2, ts, d), F32), pltpu.VMEM((1, d), F32),
                        pltpu.SemaphoreType.DMA((2, 2))],
        compiler_params=_params("arbitrary"),
    )(ya, yc, w_out, x, target, g_final)


def _out_proj_bwd(dhb, w_out, ya, yc, bm):
    r, d = dhb.shape
    ka = ya.shape[1]
    n_mix = w_out.shape[0]
    last = r // bm - 1

    def body(dh_ref, w_ref, a_ref, c_ref, dcat_ref, dw_ref, acc_ref):
        @pl.when(pl.program_id(0) == 0)
        def _():
            acc_ref[...] = jnp.zeros_like(acc_ref)

        dh = dh_ref[...]
        dcat_ref[...] = _dot(dh, w_ref[...], _NT).astype(BF16)
        acc_ref[0:ka, :] += _dot(a_ref[...], dh, _TN)
        acc_ref[ka:, :] += _dot(c_ref[...], dh, _TN)

        @pl.when(pl.program_id(0) == last)
        def _():
            dw_ref[...] = acc_ref[...].astype(BF16)

    return pl.pallas_call(
        body,
        name="out_proj_bwd",
        grid=(r // bm,),
        in_specs=[pl.BlockSpec((bm, d), lambda i: (i, 0)), pl.BlockSpec(w_out.shape, lambda i: (0, 0)),
                  pl.BlockSpec((bm, ka), lambda i: (i, 0)), pl.BlockSpec((bm, yc.shape[1]), lambda i: (i, 0))],
        out_specs=(pl.BlockSpec((bm, n_mix), lambda i: (i, 0)),
                   pl.BlockSpec((n_mix, d), lambda i: (0, 0))),
        out_shape=(jax.ShapeDtypeStruct((r, n_mix), BF16),
                   jax.ShapeDtypeStruct((n_mix, d), BF16)),
        scratch_shapes=[pltpu.VMEM((n_mix, d), F32)],
        compiler_params=_params("arbitrary"),
    )(dhb, w_out, ya, yc)


def _attn_bwd(q, k, v, o, lse, dcat, p, g_attn):
    nb_seq, _, tp, _ = q.shape

    def body(q_ref, k_ref, v_ref, o_ref, lse_ref, dy_ref, z_ref, g_ref,
             dq_ref, dk_ref, dv_ref, dz_ref, dg_ref, dq_acc):
        @pl.when(pl.program_id(1) == 0)
        def _():
            dg_ref[...] = jnp.zeros_like(dg_ref)

        g = g_ref[...]
        z = z_ref[...].astype(F32)
        o = o_ref[0, 0]
        dy = dy_ref[...].astype(F32)
        sig = _sigmoid(z)
        ohat, r = _rms_stats(o)
        don = dy * (z * sig)
        dz_ref[...] = (dy * (ohat * g) * (sig * (1.0 + z * (1.0 - sig)))).astype(BF16)
        dg_ref[...] += jnp.sum(don * ohat, axis=0, keepdims=True)
        do = _rms_bwd(g * don, ohat, r)
        dvec = jnp.sum(do * o, axis=-1, keepdims=True)
        dob = do.astype(BF16)
        lse_col = lse_ref[0, 0, :, 0:1]
        dq_acc[...] = jnp.zeros_like(dq_acc)
        for k0 in range(0, tp, K_TILE):
            nk = min(K_TILE, tp - k0)
            nq = tp - k0
            qq = q_ref[0, 0, k0:, :]
            kk = k_ref[0, 0, k0:k0 + nk, :]
            causal = (lax.broadcasted_iota(jnp.int32, (nq, nk), 1) <= lax.broadcasted_iota(jnp.int32, (nq, nk), 0))
            pr = jnp.where(causal, jnp.exp2(_dot(qq, kk, _NT) - lse_col[k0:]), 0.0)
            dp = _dot(dob[k0:], v_ref[0, 0, k0:k0 + nk, :], _NT)
            ds = (pr * (dp - dvec[k0:])).astype(BF16)
            dv_ref[0, 0, k0:k0 + nk, :] = _dot(pr.astype(BF16), dob[k0:], _TN).astype(BF16)
            dk_ref[0, 0, k0:k0 + nk, :] = (_dot(ds, qq, _TN) * (ATTN_SCALE / Q_SCALE)).astype(BF16)
            dq_acc[k0:, :] += _dot(ds, kk)
        dq_ref[0, 0] = (dq_acc[...] * ATTN_SCALE).astype(BF16)

    qk = pl.BlockSpec((1, 1, tp, 2 * LANES), lambda h, b: (b, h, 0, 0))
    hv = pl.BlockSpec((1, 1, tp, D_V), lambda h, b: (b, h, 0, 0))
    col = pl.BlockSpec((tp, LANES), lambda h, b: (b, h))
    return pl.pallas_call(
        body,
        name="attn_bwd",
        grid=(N_HEADS, nb_seq),
        in_specs=[qk, qk, hv, hv, hv, col,
                  pl.BlockSpec((tp, LANES), lambda h, b: (b, GRP_A // LANES + h)),
                  pl.BlockSpec((1, LANES), lambda h, b: (0, h))],
        out_specs=(qk, qk, hv, col, pl.BlockSpec((1, LANES), lambda h, b: (0, h))),
        out_shape=(
            jax.ShapeDtypeStruct((nb_seq, N_HEADS, tp, 2 * LANES), BF16),
            jax.ShapeDtypeStruct((nb_seq, N_HEADS, tp, 2 * LANES), BF16),
            jax.ShapeDtypeStruct((nb_seq, N_HEADS, tp, D_V), BF16),
            jax.ShapeDtypeStruct((nb_seq * tp, N_HEADS * D_V), BF16),
            jax.ShapeDtypeStruct((1, N_HEADS * D_V), F32),
        ),
        scratch_shapes=[pltpu.VMEM((tp, 2 * LANES), F32)],
        compiler_params=_params("arbitrary", "arbitrary"),
    )(q, k, v, o, lse, dcat, p, g_attn)


def _qkv_bwd(p, dq, dk, dv, wq, wkv, gq, gkv, tables):
    nb_seq, _, tp, _ = dq.shape
    ht = tp // 2

    def body(pa_ref, dq_ref, dk_ref, dv_ref, wq_ref, wkv_ref, gq_ref, gkv_ref, cos_ref, sa_ref, sb_ref,
             dpa_ref, dwq_ref, dwkv_ref, dgq_ref, dgkv_ref):
        @pl.when(pl.program_id(0) == 0)
        def _():
            dwq_ref[...] = jnp.zeros_like(dwq_ref)
            dwkv_ref[...] = jnp.zeros_like(dwkv_ref)
            dgq_ref[...] = jnp.zeros_like(dgq_ref)
            dgkv_ref[...] = jnp.zeros_like(dgkv_ref)

        pa = pa_ref[...].astype(F32)
        gq, gkv = gq_ref[...], gkv_ref[...]
        cq_hat, rq = _rms_stats(pa[:, :Q_RANK])
        ckv_hat, rkv = _rms_stats(pa[:, Q_RANK:Q_RANK + KV_RANK])
        tabs = (cos_ref[...], sa_ref[...], sb_ref[...])

        pe = [dq_ref[0, h, :, D_NOPE:].astype(F32) for h in range(N_HEADS)]
        pairs = [_rope_t(pe[2 * i] + pltpu.roll(pe[2 * i + 1], D_ROPE, 1), *tabs).astype(BF16) for i in range(2)]
        dq_flat = jnp.concatenate([dq_ref[0, h, :, :D_NOPE] for h in range(N_HEADS)] + pairs, axis=1)
        dwq_ref[...] += _dot(dq_flat, (cq_hat * gq).astype(BF16), _TN)
        dcqn = _dot(dq_flat, wq_ref[...])
        dgq_ref[...] += jnp.sum(dcqn * cq_hat, axis=0, keepdims=True)
        dcq = _rms_bwd(gq * dcqn, cq_hat, rq)

        dkv_flat = jnp.concatenate([dk_ref[0, h, :, :D_NOPE] for h in range(N_HEADS)]
                                   + [dv_ref[0, h] for h in range(N_HEADS)], axis=1)
        dwkv_ref[...] += _dot((ckv_hat * gkv).astype(BF16), dkv_flat, _TN)
        dckvn = _dot(dkv_flat, wkv_ref[...], _NT)
        dgkv_ref[...] += jnp.sum(dckvn * ckv_hat, axis=0, keepdims=True)
        dckv = _rms_bwd(gkv * dckvn, ckv_hat, rkv)

        dk_pe = dk_ref[0, 0, :, D_NOPE:].astype(F32)
        for h in range(1, N_HEADS):
            dk_pe = dk_pe + dk_ref[0, h, :, D_NOPE:].astype(F32)
        dk_pe = jnp.where(lax.broadcasted_iota(jnp.int32, (ht, LANES), 1) < D_ROPE, dk_pe, 0.0)
        dpa_ref[...] = jnp.concatenate([dcq, dckv, _rope_t(dk_pe, *tabs)], axis=1).astype(BF16)

    full = lambda a: pl.BlockSpec(a.shape, lambda i: (0,) * a.ndim)
    tab = pl.BlockSpec((ht, LANES), lambda i: (i % 2, 0))
    qk = pl.BlockSpec((1, N_HEADS, ht, 2 * LANES), lambda i: (i // 2, 0, i % 2, 0))
    acc = lambda shape: pl.BlockSpec(shape, lambda i: (0, 0))
    return pl.pallas_call(
        body,
        name="qkv_bwd",
        grid=(2 * nb_seq,),
        in_specs=[pl.BlockSpec((ht, GRP_A), lambda i: (i, 0)), qk, qk,
                  pl.BlockSpec((1, N_HEADS, ht, D_V), lambda i: (i // 2, 0, i % 2, 0)),
                  full(wq), full(wkv), full(gq), full(gkv), tab, tab, tab],
        out_specs=(pl.BlockSpec((ht, GRP_A), lambda i: (i, 0)),
                   acc(wq.shape), acc(wkv.shape), acc((1, Q_RANK)), acc((1, KV_RANK))),
        out_shape=(
            jax.ShapeDtypeStruct((nb_seq * tp, GRP_A), BF16),
            jax.ShapeDtypeStruct(wq.shape, F32),
            jax.ShapeDtypeStruct(wkv.shape, F32),
            jax.ShapeDtypeStruct((1, Q_RANK), F32),
            jax.ShapeDtypeStruct((1, KV_RANK), F32),
        ),
        compiler_params=_params("arbitrary"),
    )(p, dq, dk, dv, wq, wkv, gq, gkv, *tables)


def _conv_bwd(p, dcat, conv_w, g_conv, nb_seq, tp):
    cols = CONV_WIDTH // LANES

    def body(b_ref, c_ref, h_ref, z_ref, dy_ref, w_ref, g_ref,
             db_ref, dc_ref, dh_ref, dz_ref, dw_ref, dg_ref):
        @pl.when(pl.program_id(1) == 0)
        def _():
            dw_ref[...] = jnp.zeros_like(dw_ref)
            dg_ref[...] = jnp.zeros_like(dg_ref)

        cb, c, h = b_ref[...].astype(F32), c_ref[...].astype(F32), h_ref[...].astype(F32)
        z, dy = z_ref[...].astype(F32), dy_ref[...].astype(F32)
        g = g_ref[...]
        w0, w1, w2 = w_ref[0:1, :], w_ref[1:2, :], w_ref[2:3, :]
        cc = c * h
        row = lax.broadcasted_iota(jnp.int32, (tp, LANES), 0)
        s1 = jnp.where(row >= 1, pltpu.roll(cc, 1, 0), 0.0)
        s2 = jnp.where(row >= 2, pltpu.roll(cc, 2, 0), 0.0)
        dwc = w0 * s2 + w1 * s1 + w2 * cc
        yc = cb * dwc
        r = lax.rsqrt(_group_mean(yc * yc) + EPS)
        ychat = yc * r
        sig = _sigmoid(z)
        dz_ref[...] = (dy * (ychat * g) * (sig * (1.0 + z * (1.0 - sig)))).astype(BF16)
        dyn = dy * (z * sig)
        dg_ref[...] += jnp.sum(dyn * ychat, axis=0, keepdims=True)
        gd = g * dyn
        dyc = r * (gd - ychat * _group_mean(gd * ychat))
        db_ref[...] = (dyc * dwc).astype(BF16)
        ddw = dyc * cb
        dw_ref[0:1, :] += jnp.sum(ddw * s2, axis=0, keepdims=True)
        dw_ref[1:2, :] += jnp.sum(ddw * s1, axis=0, keepdims=True)
        dw_ref[2:3, :] += jnp.sum(ddw * cc, axis=0, keepdims=True)
        u1 = jnp.where(row <= tp - 2, pltpu.roll(ddw, tp - 1, 0), 0.0)
        u2 = jnp.where(row <= tp - 3, pltpu.roll(ddw, tp - 2, 0), 0.0)
        dcc = w2 * ddw + w1 * u1 + w0 * u2
        dc_ref[...] = (dcc * h).astype(BF16)
        dh_ref[...] = (dcc * c).astype(BF16)

    col = pl.BlockSpec((tp, LANES), lambda t, b: (b, t))
    out = jax.ShapeDtypeStruct((nb_seq * tp, CONV_WIDTH), BF16)
    return pl.pallas_call(
        body,
        name="conv_bwd",
        grid=(cols, nb_seq),
        in_specs=_conv_specs(tp, lambda t, b, off: (b, off + t)) + [
            pl.BlockSpec((tp, LANES), lambda t, b: (b, N_HEADS * D_V // LANES + t)),
            pl.BlockSpec((8, LANES), lambda t, b: (0, t)),
            pl.BlockSpec((1, LANES), lambda t, b: (0, t))],
        out_specs=(col, col, col, col,
                   pl.BlockSpec((8, LANES), lambda t, b: (0, t)), pl.BlockSpec((1, LANES), lambda t, b: (0, t))),
        out_shape=(out, out, out, out,
                   jax.ShapeDtypeStruct((8, CONV_WIDTH), F32), jax.ShapeDtypeStruct((1, CONV_WIDTH), F32)),
        compiler_params=_params("arbitrary", "arbitrary"),
    )(p, p, p, p, dcat, conv_w, g_conv)


def _input_bwd(dps, w_in, x, meta, dh, norm_g, nt, send_in):
    nb_seq, s, d = x.shape
    r, kb = dps[0].shape
    ts = (s + LANES) // nt
    steps = nb_seq * nt
    n_dp = len(dps)
    in_slot = send_in.shape[1:]

    def body(*refs):
        dp_refs, w_ref, x_hbm, meta_ref, dh_ref, g_ref, pay_ref = refs[:n_dp], *refs[n_dp:n_dp + 6]
        o = n_dp + 6
        gx_hbm, dmeta_ref, dg_ref, r2_in = refs[o:o + 4]
        xbuf, gxbuf, tok_sems, own_in, r1_in, sum_in = refs[o + 4:o + 10]
        sems = refs[o + 10:]
        i = pl.program_id(0)
        b, k = i // nt, i % nt

        def plan():
            return _reduce_plan((pay_ref,), (own_in,), (r1_in,), (sum_in,), (r2_in,), *sems)

        @pl.when(i == 0)
        def _():
            dmeta_ref[...] = jnp.zeros_like(dmeta_ref)
            dg_ref[...] = jnp.zeros_like(dg_ref)
            plan()[0]()

        @pl.when(i == 1)
        def _():
            plan()[1]()

        def start(kk):
            if kk == 0:
                xbuf[0:PAD_FRONT, :] = jnp.zeros((PAD_FRONT, d), F32)
                xbuf[PAD_FRONT:LANES, :] = meta_ref[...]
            _token_copy(x_hbm, b, kk, ts, xbuf, tok_sems.at[0]).start()

        _for_tile(k, nt, start)
        du = _dot(dp_refs[0][...], w_ref[0:kb, :])
        for j in range(1, n_dp):
            du = du + _dot(dp_refs[j][...], w_ref[kb * j:kb * (j + 1), :])
        _for_tile(k, nt, lambda kk: _token_copy(x_hbm, b, kk, ts, xbuf, tok_sems.at[0]).wait())

        g = g_ref[...]
        hhat, rstd = _rms_stats(xbuf[...])
        dg_ref[...] += jnp.sum(du * hhat, axis=0, keepdims=True)
        res = _rms_bwd(g * du, hhat, rstd) + dh_ref[...].astype(F32)

        @pl.when(i > 0)
        def _():
            _for_tile(k, nt, lambda kk: _token_copy(gx_hbm, b, (kk - 1) % nt, ts, gxbuf, tok_sems.at[1], True).wait())

        gxbuf[...] = res

        @pl.when(k == 0)
        def _():
            dmeta_ref[...] += gxbuf[PAD_FRONT:LANES, :]

        _for_tile(k, nt, lambda kk: _token_copy(gx_hbm, b, kk, ts, gxbuf, tok_sems.at[1], True).start())

        @pl.when(i == steps - 1)
        def _():
            _token_copy(gx_hbm, b, nt - 1, ts, gxbuf, tok_sems.at[1], True).wait()
            plan()[2]()

    whole = lambda a: pl.BlockSpec(a.shape, lambda i: (0,) * a.ndim)
    hbm = pl.BlockSpec(memory_space=pl.ANY)
    return pl.pallas_call(
        body,
        name="input_bwd",
        grid=(steps,),
        in_specs=[pl.BlockSpec((ts, kb), lambda i: (i, 0)) for _ in dps]
        + [whole(w_in), hbm, whole(meta), pl.BlockSpec((ts, d), lambda i: (i, 0)), whole(norm_g), hbm],
        out_specs=(hbm, pl.BlockSpec((N_META, d), lambda i: (0, 0)), pl.BlockSpec((1, d), lambda i: (0, 0)), hbm),
        out_shape=(jax.ShapeDtypeStruct((nb_seq, s, d), F32),
                   jax.ShapeDtypeStruct((N_META, d), F32),
                   jax.ShapeDtypeStruct((1, d), F32),
                   jax.ShapeDtypeStruct((N_CHIPS,) + in_slot, BF16)),
        scratch_shapes=[pltpu.VMEM((ts, d), F32), pltpu.VMEM((ts, d), F32), pltpu.SemaphoreType.DMA((2,))]
        + _reduce_scratch([(in_slot, BF16)], [True]),
        compiler_params=_params("arbitrary"),
    )(*dps, w_in, x, meta, dh, norm_g, send_in)


def _in_proj_bwd_w(u, dps, bm, small_grads, send_out):
    r, d = u.shape
    kb = dps[0].shape[1]
    steps = r // bm
    n_dp, n_small = len(dps), len(small_grads)
    out_slot, small_slot = send_out.shape[1:], (SMALL_ROWS, LANES)

    def body(*refs):
        u_ref, dp_refs = refs[0], refs[1:1 + n_dp]
        small_refs = refs[1 + n_dp:1 + n_dp + n_small]
        o = 1 + n_dp + n_small
        pay_out, o_ref, r2_out, r2_small = refs[o:o + 4]
        acc_ref, ssmall, r1_out, sum_out, r1_small, sum_small = refs[o + 4:o + 10]
        sems = refs[o + 10:]
        i = pl.program_id(0)

        def plan():
            return _reduce_plan((pay_out, ssmall), (None, None), (r1_out, r1_small), (sum_out, sum_small),
                                (r2_out, r2_small), *sems)

        @pl.when(i == 0)
        def _():
            acc_ref[...] = jnp.zeros_like(acc_ref)
            _pack_small(ssmall, *small_refs)
            plan()[0]()

        @pl.when(i == 1)
        def _():
            plan()[1]()

        uu = u_ref[...]
        for j in range(n_dp):
            acc_ref[kb * j:kb * (j + 1), :] += _dot(dp_refs[j][...], uu, _TN)

        @pl.when(i == steps - 1)
        def _():
            for k in range(N_DEV):
                for s, e, c0 in _in_pieces(k):
                    o_ref[k, s:e, :] = acc_ref[c0:c0 + e - s, :].astype(BF16)
                o_ref[k, SHARD_IN:, :] = jnp.zeros((SHARD_IN_PAD - SHARD_IN, d), BF16)
            plan()[2]()

    whole = lambda a: pl.BlockSpec(a.shape, lambda i: (0,) * a.ndim)
    hbm = pl.BlockSpec(memory_space=pl.ANY)
    return pl.pallas_call(
        body,
        name="in_proj_bwd_w",
        grid=(steps,),
        in_specs=[pl.BlockSpec((bm, d), lambda i: (i, 0))]
        + [pl.BlockSpec((bm, kb), lambda i: (i, 0)) for _ in dps] + [whole(a) for a in small_grads]
        + [whole(send_out)],
        out_specs=(pl.BlockSpec((N_DEV, SHARD_IN_PAD, d), lambda i: (0, 0, 0)), hbm, hbm),
        out_shape=(jax.ShapeDtypeStruct((N_DEV, SHARD_IN_PAD, d), BF16),
                   jax.ShapeDtypeStruct((N_CHIPS,) + out_slot, BF16),
                   jax.ShapeDtypeStruct((N_CHIPS,) + small_slot, F32)),
        scratch_shapes=[pltpu.VMEM((kb * n_dp, d), F32), pltpu.VMEM((N_DEV,) + small_slot, F32)]
        + _reduce_scratch([(out_slot, BF16), (small_slot, F32)], [False, False]),
        compiler_params=_params("arbitrary"),
    )(u, *dps, *small_grads, send_out)


def _local_step(x, loss_target, u, p, meta_f, norm_g, w_in_p, q_norm_g, w_q_p, kv_norm_g, w_kv_p, conv_w_f,
                attn_out_g, conv_out_g, w_out_f, g_final):
    nb_seq, s, d = x.shape
    tp = s + LANES
    ht = tp // 2
    tables = _rope_tables(tp)

    q, k, v = _qkv_fwd(p, w_q_p, w_kv_p, q_norm_g, kv_norm_g, tables, nb_seq, tp)
    ya, o, lse = _attn_fwd(q, k, v, p, attn_out_g)
    yc = _conv_fwd(p, conv_w_f, conv_out_g, nb_seq, tp)
    dhb, d_final_g, loss_part = _out_proj_loss(ya, yc, w_out_f, x, loss_target, g_final, TOKEN_TILES)

    dcat, d_w_out = _out_proj_bwd(dhb, w_out_f, ya, yc, ht)
    send_out = d_w_out.reshape(N_DEV, SHARD_OUT, d)
    dq, dk, dv, dz_attn, d_attn_g = _attn_bwd(q, k, v, o, lse, dcat, p, attn_out_g)
    dpa, d_wq_p, d_wkv_p, d_gq, d_gkv = _qkv_bwd(p, dq, dk, dv, w_q_p, w_kv_p, q_norm_g, kv_norm_g, tables)
    d_b, d_c, d_h, dz_conv, d_conv_w, d_conv_g = _conv_bwd(p, dcat, conv_w_f, conv_out_g, nb_seq, tp)
    dps = (dpa, dz_attn, d_b, d_c, d_h, dz_conv)
    small = (d_wq_p, d_wkv_p, d_conv_w, d_final_g, d_gq, d_gkv, d_attn_g, d_conv_g, loss_part)
    send_in, r_out, r_small = _in_proj_bwd_w(u, dps, ht, small, send_out)
    grad_x, d_meta, d_norm_g, r_in = _input_bwd(dps, w_in_p, x, meta_f, dhb, norm_g, TOKEN_TILES, send_in)
    return grad_x, r_in, r_out, r_small, d_meta, d_norm_g


def kernel(x, meta_tokens, norm_g, w_in, q_norm_g, w_q_up, kv_norm_g, w_kv_up, conv_w, attn_out_g, conv_out_g, w_out, final_norm_g, loss_target, m_meta_tokens, m_norm_g, m_w_in, m_q_norm_g, m_w_q_up, m_kv_norm_g, m_w_kv_up, m_conv_w, m_attn_out_g, m_conv_out_g, m_w_out, m_final_norm_g, v_meta_tokens, v_norm_g, v_w_in, v_q_norm_g, v_w_q_up, v_kv_norm_g, v_w_kv_up, v_conv_w, v_attn_out_g, v_conv_out_g, v_w_out, v_final_norm_g):
    d = x.shape[-1]
    order = jnp.asarray(_tile_orders()[0])[2 * lax.axis_index("x") + lax.axis_index("y")]
    u, p, w_in_p, meta_f, w_q_p, w_kv_p, w_out_f, conv_w_f = _prep_in_proj(
        x, meta_tokens, norm_g, w_in[0].T, w_q_up[0].T, w_kv_up[0], w_out[0], conv_w.transpose(1, 0, 2), order)
    g_final = final_norm_g.reshape(1, d)
    grad_x, r_in, r_out, r_small, d_meta, d_norm_g = _local_step(
        x, loss_target, u, p, meta_f, norm_g, w_in_p, q_norm_g, w_q_p, kv_norm_g, w_kv_p, conv_w_f,
        attn_out_g, conv_out_g, w_out_f, g_final)

    flat = lambda a: a.reshape(a.shape[-2:]) if a.ndim == 3 else a.reshape(1, -1) if a.ndim == 1 else a
    transposed = ("w_in", "w_q_up")

    def to_kernel(n, a):
        if n == "conv_w":
            return a.transpose(1, 0, 2)
        return flat(a).T if n in transposed else flat(a)

    def from_kernel(n, a, shape):
        if n == "conv_w":
            return a.transpose(1, 0, 2)
        return (a.T if n in transposed else a).reshape(shape)
    params = {
        "meta_tokens": (meta_tokens, m_meta_tokens, v_meta_tokens),
        "norm_g": (norm_g, m_norm_g, v_norm_g),
        "w_in": (w_in, m_w_in, v_w_in),
        "q_norm_g": (q_norm_g, m_q_norm_g, v_q_norm_g),
        "w_q_up": (w_q_up, m_w_q_up, v_w_q_up),
        "kv_norm_g": (kv_norm_g, m_kv_norm_g, v_kv_norm_g),
        "w_kv_up": (w_kv_up, m_w_kv_up, v_w_kv_up),
        "conv_w": (conv_w, m_conv_w, v_conv_w),
        "attn_out_g": (attn_out_g, m_attn_out_g, v_attn_out_g),
        "conv_out_g": (conv_out_g, m_conv_out_g, v_conv_out_g),
        "w_out": (w_out, m_w_out, v_w_out),
        "final_norm_g": (final_norm_g, m_final_norm_g, v_final_norm_g),
    }
    grads, loss = _reduce_tail(r_in, r_out, r_small, d_meta, d_norm_g)
    updated = _adamw(grads, {n: tuple(to_kernel(n, a) for a in t) for n, t in params.items()})
    outs = [[from_kernel(n, updated[n][i], params[n][0].shape) for n, _ in PARAM_SHAPES] for i in range(4)]
    return (loss[0, 0], grad_x, *outs[0], *outs[1], *outs[2], *outs[3])
```

```python
import functools

import jax
import jax.numpy as jnp
import numpy as np
from jax import lax
from jax.experimental import pallas as pl
from jax.experimental.pallas import tpu as pltpu

F32 = jnp.float32
BF16 = jnp.bfloat16

N_META = 16
D_MODEL = 1024
N_HEADS = 4
D_NOPE = 128
D_ROPE = 64
D_V = 128
Q_RANK = 256
KV_RANK = 128
CONV_WIDTH = 512
CONV_GROUP = 64
ROPE_THETA = 10000.0
ATTN_SCALE = (D_NOPE + D_ROPE) ** -0.5
Q_SCALE = ATTN_SCALE * 1.4426950408889634
EPS = 1e-6
NEG_INF = -1e30

ADAM_LR = 0.001
ADAM_B1 = 0.9
ADAM_B2 = 0.999
ADAM_EPS = 1e-08
ADAM_WD = 0.01
ADAM_STEP = 10

LANES = 128
PAD_FRONT = LANES - N_META
K_TILE = 256
Q_TILE = 512
N_DEV = 8
VMEM_LIMIT = 56 * 1024 * 1024

IN_PAD = 3072
GRP_A = 512
N_A = Q_RANK + KV_RANK + D_ROPE
IN_PROJ = 3008
SHARD_IN = IN_PROJ // N_DEV
SHARD_IN_PAD = 384
SHARD_Q = 96
SHARD_KV = 128
SHARD_OUT = 128
SHARD_CONV = 64
SHARD_META = 128
Q_COLS = N_HEADS * (D_NOPE + D_ROPE)
KV_COLS = N_HEADS * (D_NOPE + D_V)

ROW_Q, ROW_KV, ROW_META, ROW_CONV = 0, 256, 384, 400
ROW_REPL = 408
ROW_NORM, ROW_FINAL, ROW_GQ, ROW_GKV, ROW_ATTN, ROW_CONVG, ROW_LOSS = 408, 416, 424, 426, 427, 431, 435
SMALL_ROWS = 440

PARAM_SHAPES = (
    ("meta_tokens", (N_META, SHARD_META)), ("norm_g", (1, D_MODEL)), ("w_in", (SHARD_IN, D_MODEL)),
    ("q_norm_g", (1, Q_RANK)), ("w_q_up", (SHARD_Q, Q_RANK)), ("kv_norm_g", (1, KV_RANK)),
    ("w_kv_up", (KV_RANK, SHARD_KV)), ("conv_w", (3, 1, SHARD_CONV)), ("attn_out_g", (1, CONV_WIDTH)),
    ("conv_out_g", (1, CONV_WIDTH)), ("w_out", (SHARD_OUT, D_MODEL)), ("final_norm_g", (1, D_MODEL)),
)


def _in_pieces(k):
    lo, hi = SHARD_IN * k, SHARD_IN * (k + 1)
    out = []
    if lo < N_A:
        out.append((0, min(hi, N_A) - lo, lo))
    if hi > N_A:
        s = max(lo, N_A)
        out.append((s - lo, hi - lo, s + GRP_A - N_A))
    return out


P_TILE = 256


def _tile_pieces(j):
    lo, hi = P_TILE * j, P_TILE * (j + 1)
    out = []
    for k in range(N_DEV):
        for s, e, d in _in_pieces(k):
            a, b = max(d, lo), min(d + e - s, hi)
            if a < b:
                out.append((k, s + a - d, s + b - d, a - lo))
    return out


def _tile_orders():
    n_tiles = IN_PAD // P_TILE
    sources = [{k for k, _, _, _ in _tile_pieces(j)} for j in range(n_tiles)]
    rows, n_early, n_free = [], n_tiles, n_tiles
    for chip in range(N_CHIPS):
        own = {2 * chip, 2 * chip + 1}
        diagonal = {2 * (N_CHIPS - 1 - chip), 2 * (N_CHIPS - 1 - chip) + 1}
        early = [j for j in range(n_tiles) if sources[j] <= own]
        late = [j for j in range(n_tiles) if sources[j] & diagonal]
        mid = [j for j in range(n_tiles) if j not in early and j not in late]
        rows.append(early + mid + late)
        n_early, n_free = min(n_early, len(early)), min(n_free, len(early) + len(mid))
    return np.asarray(rows, np.int32), n_early, n_free


def _q_pieces(k):
    lo, hi = SHARD_Q * k, SHARD_Q * (k + 1)
    out = []
    for h in range(N_HEADS):
        base = (D_NOPE + D_ROPE) * h
        s, e = max(lo, base), min(hi, base + D_NOPE)
        if s < e:
            out.append((s - lo, e - lo, D_NOPE * h + s - base))
        s, e = max(lo, base + D_NOPE), min(hi, base + D_NOPE + D_ROPE)
        if s < e:
            out.append((s - lo, e - lo, N_HEADS * D_NOPE + D_ROPE * h + s - base - D_NOPE))
    return out


def _kv_dst(k):
    return D_NOPE * (k // 2) + (N_HEADS * D_NOPE if k % 2 else 0)


def _params(*sem):
    return pltpu.CompilerParams(dimension_semantics=sem, vmem_limit_bytes=VMEM_LIMIT)


def _rms_stats(x):
    r = lax.rsqrt(jnp.mean(x * x, axis=-1, keepdims=True) + EPS)
    return x * r, r


def _rms_bwd(gdy, xhat, r):
    return r * (gdy - xhat * jnp.mean(gdy * xhat, axis=-1, keepdims=True))


def _sigmoid(z):
    return 1.0 / (1.0 + jnp.exp(-z))


def _group_mean(x):
    i0 = lax.broadcasted_iota(jnp.int32, (LANES, LANES), 0) // CONV_GROUP
    i1 = lax.broadcasted_iota(jnp.int32, (LANES, LANES), 1) // CONV_GROUP
    m = jnp.where(i0 == i1, 1.0 / CONV_GROUP, 0.0).astype(BF16)
    hi = x.astype(BF16)
    lo = (x - hi.astype(F32)).astype(BF16)
    return jnp.dot(hi, m, preferred_element_type=F32) + jnp.dot(lo, m, preferred_element_type=F32)


_NT = (((1,), (1,)), ((), ()))
_TN = (((0,), (0,)), ((), ()))


def _dot(a, b, dims=None):
    if dims is None:
        return jnp.dot(a, b, preferred_element_type=F32)
    return lax.dot_general(a, b, dims, preferred_element_type=F32)


def _device_position():
    x, y, c = lax.axis_index("x"), lax.axis_index("y"), lax.axis_index("c")
    return x, y, c, 4 * x + 2 * y + c


def _gather_plan(srcs, slots, send_sems, recv_sems, local_sems):
    x, y, c, _ = _device_position()
    me, sibling = (x, y, c), (x, y, 1 - c)
    flip = lambda v, on: v + on - 2 * v * on
    near = (flip(x, 1 - c), flip(y, c))
    far = (flip(x, c), flip(y, 1 - c))
    diag = (1 - x, 1 - y)
    n = len(srcs)

    def slot(a, px, py, pc):
        return slots[a].at[4 * px + 2 * py + pc]

    def copy(a, k, block, to, own=False):
        return pltpu.make_async_remote_copy(
            src_ref=srcs[a] if own else slot(a, *block),
            dst_ref=slot(a, *block),
            send_sem=send_sems.at[7 * a + k],
            recv_sem=recv_sems.at[7 * a + k],
            device_id=to,
            device_id_type=pl.DeviceIdType.MESH,
        )

    def local(a):
        return pltpu.make_async_copy(srcs[a], slot(a, *me), local_sems.at[a])

    sent = [(me, sibling), (me, (*near, c)), (me, (*far, c)), ((*near, c), (*far, c)),
            ((*near, c), sibling), ((*far, c), sibling), ((*diag, c), sibling)]
    landed = [sibling, (*near, c), (*far, c), (*diag, c), (*far, 1 - c), (*near, 1 - c), (*diag, 1 - c)]

    def send(a, k):
        return copy(a, k, *sent[k], own=k < 3)

    def arrival(a, k):
        return copy(a, k, landed[k], me)

    def start():
        for a in range(n):
            local(a).start()
            for k in range(3):
                send(a, k).start()

    def own():
        for a in range(n):
            local(a).wait()
            arrival(a, 0).wait_recv()

    def mid():
        for a in range(n):
            arrival(a, 1).wait_recv()
            send(a, 3).start()
            send(a, 4).start()
        for a in range(n):
            arrival(a, 2).wait_recv()
            send(a, 5).start()
        for a in range(n):
            for k in (4, 5):
                arrival(a, k).wait_recv()

    def late():
        for a in range(n):
            arrival(a, 3).wait_recv()
            send(a, 6).start()
        for a in range(n):
            arrival(a, 6).wait_recv()

    def finish():
        for a in range(n):
            for k in range(7):
                send(a, k).wait_send()

    return start, own, mid, late, finish


def _adam_update(g, w, m, v):
    m_new = ADAM_B1 * m + (1.0 - ADAM_B1) * g
    v_new = ADAM_B2 * v + (1.0 - ADAM_B2) * (g * g)
    m_hat = m_new / (1.0 - ADAM_B1 ** ADAM_STEP)
    v_hat = v_new / (1.0 - ADAM_B2 ** ADAM_STEP)
    return -ADAM_LR * (m_hat / (jnp.sqrt(v_hat) + ADAM_EPS) + ADAM_WD * w), m_new, v_new


def _adamw(grads, params):
    names = [n for n, _ in PARAM_SHAPES]
    n_p = len(names)

    def body(*refs):
        for i in range(n_p):
            g = refs[i][...]
            w, m, v = (refs[n_p + 3 * i + j][...] for j in range(3))
            delta, m_new, v_new = _adam_update(g, w, m, v)
            for j, val in enumerate((g, delta, m_new, v_new)):
                refs[4 * n_p + 4 * i + j][...] = val

    vm = pl.BlockSpec(memory_space=pltpu.VMEM)
    out_shape = []
    for _, shape in PARAM_SHAPES:
        out_shape += [jax.ShapeDtypeStruct(shape, F32)] * 4
    outs = pl.pallas_call(
        body,
        name="adamw",
        out_shape=tuple(out_shape),
        in_specs=[vm] * (4 * n_p),
        out_specs=(vm,) * (4 * n_p),
        compiler_params=pltpu.CompilerParams(vmem_limit_bytes=VMEM_LIMIT),
    )(*[grads[n] for n in names], *[a for n in names for a in params[n]])
    return {n: outs[4 * i:4 * i + 4] for i, n in enumerate(names)}


N_CHIPS = 4


def _reduce_plan(pays, owns, r1s, sums, r2s, send1, recv1, send2, recv2, local_sems):
    x, y, c, _ = _device_position()
    sibling = (x, y, 1 - c)
    chips = [((1 - x if rj & 2 else x), (1 - y if rj & 1 else y)) for rj in range(N_CHIPS)]
    n = len(pays)

    def slot_of(rj, core):
        return 4 * chips[rj][0] + 2 * chips[rj][1] + core

    def to_sibling(a, rj):
        return pltpu.make_async_remote_copy(
            src_ref=pays[a].at[slot_of(rj, 1 - c)], dst_ref=r1s[a].at[rj],
            send_sem=send1.at[N_CHIPS * a + rj], recv_sem=recv1.at[N_CHIPS * a + rj],
            device_id=sibling, device_id_type=pl.DeviceIdType.MESH)

    def load_own(a, rj):
        return pltpu.make_async_copy(pays[a].at[slot_of(rj, c)], owns[a].at[rj], local_sems.at[2 * N_CHIPS * a + rj])

    def to_chip(a, rj):
        return pltpu.make_async_remote_copy(
            src_ref=sums[a].at[rj], dst_ref=r2s[a].at[rj],
            send_sem=send2.at[N_CHIPS * a + rj], recv_sem=recv2.at[N_CHIPS * a + rj],
            device_id=(*chips[rj], c), device_id_type=pl.DeviceIdType.MESH)

    def keep(a):
        return pltpu.make_async_copy(sums[a].at[0], r2s[a].at[0], local_sems.at[2 * N_CHIPS * a + N_CHIPS])

    def start():
        for a in range(n):
            for rj in range(N_CHIPS):
                to_sibling(a, rj).start()
                if owns[a] is not None:
                    load_own(a, rj).start()

    def combine():
        for a in range(n):
            for rj in range(N_CHIPS):
                to_sibling(a, rj).wait_recv()
                if owns[a] is not None:
                    load_own(a, rj).wait()
                    mine = owns[a][rj]
                else:
                    mine = pays[a][slot_of(rj, c)]
                sums[a][rj] = (mine.astype(F32) + r1s[a][rj].astype(F32)).astype(sums[a].dtype)
            keep(a).start()
            for rj in range(1, N_CHIPS):
                to_chip(a, rj).start()

    def finish():
        for a in range(n):
            for rj in range(1, N_CHIPS):
                to_chip(a, rj).wait_recv()
            for rj in range(N_CHIPS):
                to_sibling(a, rj).wait_send()
            for rj in range(1, N_CHIPS):
                to_chip(a, rj).wait_send()
            keep(a).wait()

    return start, combine, finish


def _reduce_scratch(shapes_dtypes, own_flags):
    out = []
    for (shape, dtype), own in zip(shapes_dtypes, own_flags):
        if own:
            out.append(pltpu.VMEM((N_CHIPS,) + shape, dtype))
        out += [pltpu.VMEM((N_CHIPS,) + shape, dtype), pltpu.VMEM((N_CHIPS,) + shape, dtype)]
    n = len(shapes_dtypes)
    out += [pltpu.SemaphoreType.DMA((N_CHIPS * n,))] * 4 + [pltpu.SemaphoreType.DMA((2 * N_CHIPS * n,))]
    return out


def _pack_small(ssmall, dwq, dwkv, dconv, dfinal, dgq, dgkv, dattn, dconvg, loss_part):
    ssmall[...] = jnp.zeros_like(ssmall)
    rep = ssmall.at[0]
    for i in range(D_MODEL // LANES):
        rep[ROW_FINAL + i:ROW_FINAL + i + 1, :] = dfinal[:, LANES * i:LANES * (i + 1)]
    for i in range(Q_RANK // LANES):
        rep[ROW_GQ + i:ROW_GQ + i + 1, :] = dgq[:, LANES * i:LANES * (i + 1)]
    rep[ROW_GKV:ROW_GKV + 1, :] = dgkv[...]
    for i in range(CONV_WIDTH // LANES):
        rep[ROW_ATTN + i:ROW_ATTN + i + 1, :] = dattn[:, LANES * i:LANES * (i + 1)]
        rep[ROW_CONVG + i:ROW_CONVG + i + 1, :] = dconvg[:, LANES * i:LANES * (i + 1)]
    rep[ROW_LOSS:ROW_LOSS + 1, :] = loss_part[...]
    for k in range(N_DEV):
        if k:
            ssmall[k, ROW_REPL:, :] = ssmall[0, ROW_REPL:, :]
        for s, e, d in _q_pieces(k):
            for i in range(Q_RANK // LANES):
                ssmall[k, ROW_Q + SHARD_Q * i + s:ROW_Q + SHARD_Q * i + e, :] = dwq[d:d + e - s, LANES * i:LANES * (i + 1)]
        ssmall[k, ROW_KV:ROW_KV + KV_RANK, :] = dwkv[:, _kv_dst(k):_kv_dst(k) + SHARD_KV]
        ssmall[k, ROW_CONV:ROW_CONV + 3, 0:SHARD_CONV] = dconv[0:3, SHARD_CONV * k:SHARD_CONV * (k + 1)]


TOKEN_TILES = 4
TAIL_ROWS = N_META + D_MODEL // LANES


def _reduce_tail(r_in, r_out, r_small, d_meta, d_norm):
    n_p = len(PARAM_SHAPES)
    names = [n for n, _ in PARAM_SHAPES]

    def body(*refs):
        rin, rout, rsmall, dmeta, dnorm = refs[:5]
        g_out = {n: refs[5 + i] for i, n in enumerate(names)}
        loss_out = refs[5 + n_p]
        stail, rtail, gsum, gtail, send_sems, recv_sems = refs[6 + n_p:]
        x, y, c, me = _device_position()
        my_chip = 2 * x + y

        for k in range(N_DEV):
            stail[k, 0:N_META, :] = dmeta[:, SHARD_META * k:SHARD_META * (k + 1)]
            for i in range(D_MODEL // LANES):
                stail[k, N_META + i:N_META + i + 1, :] = dnorm[:, LANES * i:LANES * (i + 1)]
        copies = []
        for r in range(1, N_DEV):
            peer = (1 - x if r & 4 else x, 1 - y if r & 2 else y, 1 - c if r & 1 else c)
            copies.append(pltpu.make_async_remote_copy(
                src_ref=stail.at[4 * peer[0] + 2 * peer[1] + peer[2]],
                dst_ref=rtail.at[r],
                send_sem=send_sems.at[r - 1],
                recv_sem=recv_sems.at[r - 1],
                device_id=peer,
                device_id_type=pl.DeviceIdType.MESH,
            ))
        for cp in copies:
            cp.start()
        rtail[0] = stail[me]

        g = rin[my_chip].astype(F32)
        for ch in range(1, N_CHIPS):
            g = g + rin[ch ^ my_chip].astype(F32)
        g_out["w_in"][...] = g[:SHARD_IN, :]

        g = rout[my_chip].astype(F32)
        gs = rsmall[my_chip]
        for ch in range(1, N_CHIPS):
            g = g + rout[ch ^ my_chip].astype(F32)
            gs = gs + rsmall[ch ^ my_chip]
        g_out["w_out"][...] = g
        gsum[...] = gs
        for i in range(Q_RANK // LANES):
            g_out["w_q_up"][:, LANES * i:LANES * (i + 1)] = gsum[ROW_Q + SHARD_Q * i:ROW_Q + SHARD_Q * (i + 1), :]
        g_out["w_kv_up"][...] = gsum[ROW_KV:ROW_KV + KV_RANK, :]
        for i in range(3):
            g_out["conv_w"][i] = gsum[ROW_CONV + i:ROW_CONV + i + 1, 0:SHARD_CONV]
        for name, row, width in (("final_norm_g", ROW_FINAL, D_MODEL), ("q_norm_g", ROW_GQ, Q_RANK),
                                 ("kv_norm_g", ROW_GKV, KV_RANK), ("attn_out_g", ROW_ATTN, CONV_WIDTH),
                                 ("conv_out_g", ROW_CONVG, CONV_WIDTH)):
            for i in range(width // LANES):
                g_out[name][:, LANES * i:LANES * (i + 1)] = gsum[row + i:row + i + 1, :]
        loss_out[...] = gsum[ROW_LOSS:ROW_LOSS + 1, :]

        for cp in copies:
            cp.wait_recv()
        gt = rtail[me]
        for d in range(1, N_DEV):
            gt = gt + rtail[d ^ me]
        gtail[...] = gt
        g_out["meta_tokens"][...] = gtail[0:N_META, :]
        for i in range(D_MODEL // LANES):
            g_out["norm_g"][:, LANES * i:LANES * (i + 1)] = gtail[N_META + i:N_META + i + 1, :]
        for cp in copies:
            cp.wait_send()

    vm = pl.BlockSpec(memory_space=pltpu.VMEM)
    out_shape = [jax.ShapeDtypeStruct(shape, F32) for _, shape in PARAM_SHAPES]
    out_shape.append(jax.ShapeDtypeStruct((1, LANES), F32))
    outs = pl.pallas_call(
        body,
        name="reduce_tail",
        out_shape=tuple(out_shape),
        in_specs=[vm] * 5,
        out_specs=(vm,) * len(out_shape),
        scratch_shapes=[
            pltpu.VMEM((N_DEV, TAIL_ROWS, LANES), F32),
            pltpu.VMEM((N_DEV, TAIL_ROWS, LANES), F32),
            pltpu.VMEM((SMALL_ROWS, LANES), F32),
            pltpu.VMEM((TAIL_ROWS, LANES), F32),
            pltpu.SemaphoreType.DMA((N_DEV - 1,)),
            pltpu.SemaphoreType.DMA((N_DEV - 1,)),
        ],
        compiler_params=pltpu.CompilerParams(vmem_limit_bytes=VMEM_LIMIT),
    )(r_in, r_out, r_small, d_meta, d_norm)
    return {n: outs[i] for i, n in enumerate(names)}, outs[-1]


def _prep_in_proj(x, meta, norm_g, w_in_t, w_q, w_kv, conv_w, order):
    nb_seq, s, d = x.shape
    tp = s + LANES
    m = nb_seq * tp
    ts = s // TOKEN_TILES
    n_real = nb_seq * TOKEN_TILES
    n_norm = n_real + 1
    n_tiles = IN_PAD // P_TILE
    _, n_early, n_free = _tile_orders()
    steps = n_norm + n_tiles
    dot_rows = m // 4
    qkv_shape = (SHARD_Q + KV_RANK, Q_RANK)

    def tile(t):
        t = jnp.minimum(t, n_real - 1)
        return t // TOKEN_TILES, t % TOKEN_TILES

    def column_tile(t, order_ref):
        return order_ref[jnp.maximum(t - n_norm, 0)]

    def body(order_ref, x_ref, meta_ref, g_ref, win_ref, wq_ref, wkv_ref, conv_ref,
             u_hbm, p_ref, w_in_p, meta_f, w_q_p, w_kv_p, conv_f,
             u_all, sbig, gbig, smeta, gmeta, sqkv, sconv, gqkv, gconv,
             send_in, recv_in, local_in, send_meta, recv_meta, send_rest, recv_rest, local_rest, u_sem):
        t = pl.program_id(0)
        px, py, pc, me = _device_position()
        u_copy = pltpu.make_async_copy(u_all, u_hbm, u_sem)

        def plan_in():
            return _gather_plan((sbig,), (gbig,), send_in, recv_in, local_in)

        def plan_rest():
            return _gather_plan((sqkv, sconv), (gqkv, gconv), send_rest, recv_rest, local_rest)

        def meta_copies():
            out = []
            for r in range(1, N_DEV):
                peer = (1 - px if r & 4 else px, 1 - py if r & 2 else py, 1 - pc if r & 1 else pc)
                out.append(pltpu.make_async_remote_copy(
                    src_ref=smeta,
                    dst_ref=gmeta.at[r],
                    send_sem=send_meta.at[r - 1],
                    recv_sem=recv_meta.at[r - 1],
                    device_id=peer,
                    device_id_type=pl.DeviceIdType.MESH,
                ))
            return out

        @pl.when(t == 0)
        def _():
            sbig[0:SHARD_IN, :] = win_ref[...].astype(BF16)
            sbig[SHARD_IN:, :] = jnp.zeros((SHARD_IN_PAD - SHARD_IN, d), BF16)
            smeta[...] = meta_ref[...]
            plan_in()[0]()
            for cp in meta_copies():
                cp.start()
            sqkv[...] = jnp.zeros_like(sqkv)
            sqkv[0:SHARD_Q, :] = wq_ref[...].astype(BF16)
            sqkv[SHARD_Q:, 0:SHARD_KV] = wkv_ref[...].astype(BF16)
            sconv[...] = jnp.zeros_like(sconv)
            for i in range(3):
                sconv[i:i + 1, 0:SHARD_CONV] = conv_ref[i]

        def norm(h):
            hhat, _ = _rms_stats(h)
            return (hhat * g_ref[...]).astype(BF16)

        @pl.when(t < n_real)
        def _():
            b, k = tile(t)
            row0 = pl.multiple_of(b * tp + LANES + k * ts, 16)
            u_all[pl.ds(row0, ts), :] = norm(x_ref[0])

        @pl.when(t == n_real)
        def _():
            for cp in meta_copies():
                cp.wait_recv()
            gmeta[0] = smeta[...]
            for k in range(N_DEV):
                meta_f[:, SHARD_META * k:SHARD_META * (k + 1)] = gmeta[k ^ me]
            um = norm(meta_f[...])
            for b in range(nb_seq):
                u_all[b * tp:b * tp + PAD_FRONT, :] = jnp.zeros((PAD_FRONT, d), BF16)
                u_all[b * tp + PAD_FRONT:b * tp + LANES, :] = um
            u_copy.start()

        @pl.when(t == n_norm)
        def _():
            plan_in()[1]()

        @pl.when(t == n_norm + n_early)
        def _():
            plan_in()[2]()
            plan_rest()[0]()

        @pl.when(t == n_norm + (n_early + n_free) // 2)
        def _():
            plan_rest()[1]()
            plan_rest()[2]()

        @pl.when(t == n_norm + n_free)
        def _():
            plan_in()[3]()

        @pl.when(t == n_norm + n_free + 2)
        def _():
            plan_rest()[3]()

        @pl.when(t >= n_norm)
        def _():
            j = column_tile(t, order_ref)
            for jj in range(n_tiles):
                @pl.when(j == jj)
                def _(jj=jj):
                    if jj == N_A // P_TILE:
                        w_in_p[N_A % P_TILE:, :] = jnp.zeros((P_TILE - N_A % P_TILE, d), BF16)
                    for k, s0, e0, d0 in _tile_pieces(jj):
                        w_in_p[d0:d0 + e0 - s0, :] = gbig[k, s0:e0, :]
            for r in range(m // dot_rows):
                rows = slice(dot_rows * r, dot_rows * (r + 1))
                p_ref[rows, :] = _dot(u_all[rows, :], w_in_p[...], _NT).astype(BF16)

        @pl.when(t == steps - 1)
        def _():
            plan_in()[4]()
            plan_rest()[4]()
            for cp in meta_copies():
                cp.wait_send()
            u_copy.wait()
            conv_f[...] = jnp.zeros_like(conv_f)
            for k in range(N_DEV):
                for s0, e0, d0 in _q_pieces(k):
                    w_q_p[d0:d0 + e0 - s0, :] = gqkv[k, s0:e0, :]
                w_kv_p[:, _kv_dst(k):_kv_dst(k) + SHARD_KV] = gqkv[k, SHARD_Q:, 0:SHARD_KV]
                conv_f[0:3, SHARD_CONV * k:SHARD_CONV * (k + 1)] = gconv[k, 0:3, 0:SHARD_CONV]

    whole = lambda shape: pl.BlockSpec(shape, lambda t, o: (0,) * len(shape))
    return pl.pallas_call(
        body,
        name="prep_in_proj_gather",
        grid_spec=pltpu.PrefetchScalarGridSpec(
            num_scalar_prefetch=1,
            grid=(steps,),
            in_specs=[
                pl.BlockSpec((1, ts, d), lambda t, o: (*tile(t), 0)),
                whole(meta.shape), whole(norm_g.shape), whole(w_in_t.shape),
                whole(w_q.shape), whole(w_kv.shape), whole(conv_w.shape),
            ],
            out_specs=(pl.BlockSpec(memory_space=pl.ANY),
                       pl.BlockSpec((m, P_TILE), lambda t, o: (0, column_tile(t, o))),
                       pl.BlockSpec((P_TILE, d), lambda t, o: (column_tile(t, o), 0)),
                       whole((N_META, d)),
                       whole((Q_COLS, Q_RANK)), whole((KV_RANK, KV_COLS)), whole((8, CONV_WIDTH))),
            scratch_shapes=[
                pltpu.VMEM((m, d), BF16),
                pltpu.VMEM((SHARD_IN_PAD, d), BF16),
                pltpu.VMEM((N_DEV, SHARD_IN_PAD, d), BF16),
                pltpu.VMEM((N_META, SHARD_META), F32),
                pltpu.VMEM((N_DEV, N_META, SHARD_META), F32),
                pltpu.VMEM(qkv_shape, BF16),
                pltpu.VMEM((8, LANES), F32),
                pltpu.VMEM((N_DEV,) + qkv_shape, BF16),
                pltpu.VMEM((N_DEV, 8, LANES), F32),
                pltpu.SemaphoreType.DMA((7,)),
                pltpu.SemaphoreType.DMA((7,)),
                pltpu.SemaphoreType.DMA((1,)),
                pltpu.SemaphoreType.DMA((N_DEV - 1,)),
                pltpu.SemaphoreType.DMA((N_DEV - 1,)),
                pltpu.SemaphoreType.DMA((14,)),
                pltpu.SemaphoreType.DMA((14,)),
                pltpu.SemaphoreType.DMA((2,)),
                pltpu.SemaphoreType.DMA,
            ],
        ),
        out_shape=(jax.ShapeDtypeStruct((m, d), BF16),
                   jax.ShapeDtypeStruct((m, IN_PAD), BF16),
                   jax.ShapeDtypeStruct((IN_PAD, d), BF16),
                   jax.ShapeDtypeStruct((N_META, d), F32),
                   jax.ShapeDtypeStruct((Q_COLS, Q_RANK), BF16),
                   jax.ShapeDtypeStruct((KV_RANK, KV_COLS), BF16),
                   jax.ShapeDtypeStruct((8, CONV_WIDTH), F32)),
        compiler_params=_params("arbitrary"),
    )(order, x, meta, norm_g, w_in_t, w_q, w_kv, conv_w)


def _rope_tables(tp):
    half = D_ROPE // 2
    inv_freq = (1.0 / (ROPE_THETA ** (np.arange(half, dtype=np.float32) / half))).astype(np.float32)
    pos = (np.arange(tp) - PAD_FRONT).astype(np.float32)
    ang = pos[:, None] * inv_freq[None, :]
    cos = np.tile(np.cos(ang), (1, LANES // half))
    sin = np.tile(np.sin(ang), (1, LANES // half))
    first = (np.arange(LANES) % D_ROPE) < half
    zero = np.float32(0.0)
    return tuple(jnp.asarray(t, F32) for t in (cos, np.where(first, -sin, zero), np.where(first, zero, sin)))


def _rope(t, cos, sa, sb):
    return t * cos + pltpu.roll(t, LANES - D_ROPE // 2, 1) * sa + pltpu.roll(t, D_ROPE // 2, 1) * sb


def _rope_t(t, cos, sa, sb):
    return t * cos + pltpu.roll(t * sa, D_ROPE // 2, 1) + pltpu.roll(t * sb, LANES - D_ROPE // 2, 1)


def _qkv_fwd(p, wq, wkv, gq, gkv, tables, nb_seq, tp):
    ht = tp // 2

    def body(pa_ref, wq_ref, wkv_ref, gq_ref, gkv_ref, cos_ref, sa_ref, sb_ref, q_ref, k_ref, v_ref):
        pa = pa_ref[...].astype(F32)
        cq_hat, _ = _rms_stats(pa[:, :Q_RANK])
        ckv_hat, _ = _rms_stats(pa[:, Q_RANK:Q_RANK + KV_RANK])
        q = _dot((cq_hat * gq_ref[...]).astype(BF16), wq_ref[...], _NT) * Q_SCALE
        kv = _dot((ckv_hat * gkv_ref[...]).astype(BF16), wkv_ref[...])
        tabs = (cos_ref[...], sa_ref[...], sb_ref[...])
        lane = lax.broadcasted_iota(jnp.int32, (ht, LANES), 1)
        low = lane < D_ROPE
        mark = lane == D_ROPE
        row = (pl.program_id(0) % 2) * ht + lax.broadcasted_iota(jnp.int32, (ht, LANES), 0)
        k_pe = jnp.where(mark & (row < PAD_FRONT), NEG_INF, _rope(pa[:, Q_RANK + KV_RANK:], *tabs))
        one = jnp.where(mark & (row >= PAD_FRONT), 1.0, 0.0)
        pairs = [_rope(q[:, N_HEADS * D_NOPE + LANES * i:N_HEADS * D_NOPE + LANES * (i + 1)], *tabs) for i in range(2)]
        for h in range(N_HEADS):
            pair = pairs[h // 2]
            if h % 2:
                pair = pltpu.roll(pair, D_ROPE, 1)
            pe = jnp.where(low, pair, one)
            q_ref[0, h] = jnp.concatenate([q[:, D_NOPE * h:D_NOPE * (h + 1)], pe], axis=1).astype(BF16)
            k_ref[0, h] = jnp.concatenate([kv[:, D_NOPE * h:D_NOPE * (h + 1)], k_pe], axis=1).astype(BF16)
            v_ref[0, h] = kv[:, N_HEADS * D_NOPE + D_V * h:N_HEADS * D_NOPE + D_V * (h + 1)].astype(BF16)

    full = lambda a: pl.BlockSpec(a.shape, lambda i: (0,) * a.ndim)
    tab = pl.BlockSpec((ht, LANES), lambda i: (i % 2, 0))
    qk = pl.BlockSpec((1, N_HEADS, ht, 2 * LANES), lambda i: (i // 2, 0, i % 2, 0))
    return pl.pallas_call(
        body,
        name="qkv_fwd",
        grid=(2 * nb_seq,),
        in_specs=[pl.BlockSpec((ht, GRP_A), lambda i: (i, 0)), full(wq), full(wkv), full(gq), full(gkv), tab, tab, tab],
        out_specs=(qk, qk, pl.BlockSpec((1, N_HEADS, ht, D_V), lambda i: (i // 2, 0, i % 2, 0))),
        out_shape=(
            jax.ShapeDtypeStruct((nb_seq, N_HEADS, tp, 2 * LANES), BF16),
            jax.ShapeDtypeStruct((nb_seq, N_HEADS, tp, 2 * LANES), BF16),
            jax.ShapeDtypeStruct((nb_seq, N_HEADS, tp, D_V), BF16),
        ),
        compiler_params=_params("parallel"),
    )(p, wq, wkv, gq, gkv, *tables)


def _attn_fwd(q, k, v, p, g_attn, w_out):
    nb_seq, _, tp, _ = q.shape
    steps = nb_seq * N_HEADS

    def body(q_ref, k_ref, v_ref, z_ref, g_ref, wout_ref, y_ref, o_ref, lse_ref, w_out_f,
             sout, gout, send_sems, recv_sems, local_sems):
        t = pl.program_id(0) * N_HEADS + pl.program_id(1)

        def plan():
            return _gather_plan((sout,), (gout,), send_sems, recv_sems, local_sems)

        @pl.when(t == 0)
        def _():
            sout[...] = wout_ref[...].astype(BF16)
            plan()[0]()

        @pl.when(t == steps // 3)
        def _():
            plan()[1]()
            plan()[2]()

        @pl.when(t == 2 * steps // 3)
        def _():
            plan()[3]()

        g = g_ref[...]
        for r0 in range(0, tp, Q_TILE):
            nq = min(Q_TILE, tp - r0)
            kend = r0 + nq
            qq = q_ref[0, 0, r0:kend, :]
            sd = _dot(qq, k_ref[0, 0, r0:kend, :], _NT)
            causal = (lax.broadcasted_iota(jnp.int32, (nq, nq), 1) <= lax.broadcasted_iota(jnp.int32, (nq, nq), 0))
            sd = jnp.where(causal, sd, NEG_INF)
            m = jnp.max(sd, axis=-1, keepdims=True)
            if r0:
                so = _dot(qq, k_ref[0, 0, 0:r0, :], _NT)
                m = jnp.maximum(m, jnp.max(so, axis=-1, keepdims=True))
            ed = jnp.exp2(sd - m)
            l = jnp.sum(ed, axis=-1, keepdims=True)
            o = _dot(ed.astype(BF16), v_ref[0, 0, r0:kend, :])
            if r0:
                eo = jnp.exp2(so - m)
                l = l + jnp.sum(eo, axis=-1, keepdims=True)
                o = o + _dot(eo.astype(BF16), v_ref[0, 0, 0:r0, :])
            o = o * (1.0 / l)
            o_ref[0, 0, r0:kend, :] = o
            lse_ref[0, 0, r0:kend, :] = jnp.broadcast_to(m + jnp.log2(l), (nq, LANES))
            ohat, _ = _rms_stats(o)
            z = z_ref[r0:kend, :].astype(F32)
            y_ref[r0:kend, :] = (ohat * g * (z * _sigmoid(z))).astype(BF16)

        @pl.when(t == steps - 1)
        def _():
            plan()[4]()
            for dev in range(N_DEV):
                w_out_f[SHARD_OUT * dev:SHARD_OUT * (dev + 1), :] = gout[dev]

    qk = pl.BlockSpec((1, 1, tp, 2 * LANES), lambda b, h: (b, h, 0, 0))
    hv = pl.BlockSpec((1, 1, tp, D_V), lambda b, h: (b, h, 0, 0))
    return pl.pallas_call(
        body,
        name="attn_fwd",
        grid=(nb_seq, N_HEADS),
        in_specs=[qk, qk, hv,
                  pl.BlockSpec((tp, LANES), lambda b, h: (b, GRP_A // LANES + h)),
                  pl.BlockSpec((1, LANES), lambda b, h: (0, h)),
                  pl.BlockSpec(w_out.shape, lambda b, h: (0, 0))],
        out_specs=(pl.BlockSpec((tp, LANES), lambda b, h: (b, h)), hv, hv,
                   pl.BlockSpec((D_MODEL, D_MODEL), lambda b, h: (0, 0))),
        out_shape=(
            jax.ShapeDtypeStruct((nb_seq * tp, N_HEADS * D_V), BF16),
            jax.ShapeDtypeStruct((nb_seq, N_HEADS, tp, D_V), F32),
            jax.ShapeDtypeStruct((nb_seq, N_HEADS, tp, LANES), F32),
            jax.ShapeDtypeStruct((D_MODEL, D_MODEL), BF16),
        ),
        scratch_shapes=[
            pltpu.VMEM((SHARD_OUT, D_MODEL), BF16),
            pltpu.VMEM((N_DEV, SHARD_OUT, D_MODEL), BF16),
            pltpu.SemaphoreType.DMA((7,)),
            pltpu.SemaphoreType.DMA((7,)),
            pltpu.SemaphoreType.DMA((1,)),
        ],
        compiler_params=_params("arbitrary", "arbitrary"),
    )(q, k, v, p, g_attn, w_out)


_CONV_COL0 = (GRP_A + N_HEADS * D_V) // LANES


def _conv_specs(tp, order):
    cols = CONV_WIDTH // LANES
    return [pl.BlockSpec((tp, LANES), functools.partial(
        lambda a, b, off: order(a, b, off), off=_CONV_COL0 + i * cols)) for i in range(4)]


def _conv_fwd(p, conv_w, g_conv, nb_seq, tp):
    def body(b_ref, c_ref, h_ref, z_ref, w_ref, g_ref, y_ref):
        cc = c_ref[...].astype(F32) * h_ref[...].astype(F32)
        row = lax.broadcasted_iota(jnp.int32, (tp, LANES), 0)
        s1 = jnp.where(row >= 1, pltpu.roll(cc, 1, 0), 0.0)
        s2 = jnp.where(row >= 2, pltpu.roll(cc, 2, 0), 0.0)
        yc = b_ref[...].astype(F32) * (w_ref[0:1, :] * s2 + w_ref[1:2, :] * s1 + w_ref[2:3, :] * cc)
        r = lax.rsqrt(_group_mean(yc * yc) + EPS)
        z = z_ref[...].astype(F32)
        y_ref[...] = (yc * r * g_ref[...] * (z * _sigmoid(z))).astype(BF16)

    return pl.pallas_call(
        body,
        name="conv_fwd",
        grid=(nb_seq, CONV_WIDTH // LANES),
        in_specs=_conv_specs(tp, lambda b, t, off: (b, off + t)) + [
            pl.BlockSpec((8, LANES), lambda b, t: (0, t)),
            pl.BlockSpec((1, LANES), lambda b, t: (0, t))],
        out_specs=pl.BlockSpec((tp, LANES), lambda b, t: (b, t)),
        out_shape=jax.ShapeDtypeStruct((nb_seq * tp, CONV_WIDTH), BF16),
        compiler_params=_params("parallel", "parallel"),
    )(p, p, p, p, conv_w, g_conv)


def _token_copy(hbm, b, k, ts, buf, sem, to_hbm=False):
    lo, hi = max(k * ts - LANES, 0), (k + 1) * ts - LANES
    off = lo - (k * ts - LANES)
    src, dst = hbm.at[b, pl.ds(lo, hi - lo)], buf.at[pl.ds(off, hi - lo)]
    if to_hbm:
        src, dst = dst, src
    return pltpu.make_async_copy(src, dst, sem)


def _for_tile(k, nt, fn):
    for kk in range(nt):
        @pl.when(k == kk)
        def _(kk=kk):
            fn(kk)


def _out_proj_loss(ya, yc, w_out, x, target, g_final, nt):
    nb_seq, s, d = x.shape
    r, ka = ya.shape
    ts = (s + LANES) // nt
    steps = nb_seq * nt

    def body(a_ref, c_ref, w_ref, x_hbm, t_hbm, g_ref, dhb_ref, dg_ref, loss_ref,
             xbuf, tbuf, acc_ref, sems):
        i = pl.program_id(0)
        b, k = i // nt, i % nt

        @pl.when(i == 0)
        def _():
            acc_ref[...] = jnp.zeros_like(acc_ref)
            dg_ref[...] = jnp.zeros_like(dg_ref)

        slot = i % 2

        def fetch(seq, kk, sl):
            return [_token_copy(x_hbm, seq, kk, ts, xbuf.at[sl], sems.at[sl, 0]),
                    _token_copy(t_hbm, seq, kk, ts, tbuf.at[sl], sems.at[sl, 1])]

        def start(seq, sl, kk):
            if kk == 0:
                xbuf[sl, 0:LANES, :] = jnp.zeros((LANES, d), F32)
                tbuf[sl, 0:LANES, :] = jnp.zeros((LANES, d), F32)
            for cp in fetch(seq, kk, sl):
                cp.start()

        @pl.when(i == 0)
        def _():
            start(0, 0, 0)

        @pl.when(i + 1 < steps)
        def _():
            _for_tile((i + 1) % nt, nt, functools.partial(start, (i + 1) // nt, 1 - slot))

        mix = _dot(a_ref[...], w_ref[0:ka, :]) + _dot(c_ref[...], w_ref[ka:, :])
        _for_tile(k, nt, lambda kk: [cp.wait() for cp in fetch(b, kk, slot)])

        real = (lax.broadcasted_iota(jnp.int32, (ts, d), 0) >= LANES) | (k > 0)
        g = g_ref[...]
        hhat, rstd = _rms_stats(xbuf[slot] + mix)
        e = jnp.where(real, hhat * g - tbuf[slot], 0.0)
        acc_ref[...] += jnp.sum(e * e, axis=0, keepdims=True)
        dy = e * (1.0 / d)
        dg_ref[...] += jnp.sum(dy * hhat, axis=0, keepdims=True)
        dhb_ref[...] = _rms_bwd(g * dy, hhat, rstd).astype(BF16)

        @pl.when(i == steps - 1)
        def _():
            total = jnp.sum(acc_ref[...], axis=1, keepdims=True)
            loss_ref[...] = jnp.broadcast_to((0.5 / d) * total, loss_ref.shape)

    hbm = pl.BlockSpec(memory_space=pl.ANY)
    row = pl.BlockSpec((ts, d), lambda i: (i, 0))
    vec = pl.BlockSpec((1, d), lambda i: (0, 0))
    return pl.pallas_call(
        body,
        name="out_proj_loss",
        grid=(steps,),
        in_specs=[pl.BlockSpec((ts, ka), lambda i: (i, 0)), pl.BlockSpec((ts, yc.shape[1]), lambda i: (i, 0)),
                  pl.BlockSpec(w_out.shape, lambda i: (0, 0)), hbm, hbm, vec],
        out_specs=(row, vec, pl.BlockSpec((1, LANES), lambda i: (0, 0))),
        out_shape=(
            jax.ShapeDtypeStruct((r, d), BF16),
            jax.ShapeDtypeStruct((1, d), F32),
            jax.ShapeDtypeStruct((1, LANES), F32),
        ),
        scratch_shapes=[pltpu.VMEM((2, ts, d), F32), pltpu.VMEM((2, ts, d), F32), pltpu.VMEM((1, d), F32),
                        pltpu.SemaphoreType.DMA((2, 2))],
        compiler_params=_params("arbitrary"),
    )(ya, yc, w_out, x, target, g_final)


def _out_proj_bwd(dhb, w_out, ya, yc, bm):
    r, d = dhb.shape
    ka = ya.shape[1]
    n_mix = w_out.shape[0]
    last = r // bm - 1

    def body(dh_ref, w_ref, a_ref, c_ref, dcat_ref, dw_ref, acc_ref):
        @pl.when(pl.program_id(0) == 0)
        def _():
            acc_ref[...] = jnp.zeros_like(acc_ref)

        dh = dh_ref[...]
        dcat_ref[...] = _dot(dh, w_ref[...], _NT).astype(BF16)
        acc_ref[0:ka, :] += _dot(a_ref[...], dh, _TN)
        acc_ref[ka:, :] += _dot(c_ref[...], dh, _TN)

        @pl.when(pl.program_id(0) == last)
        def _():
            dw_ref[...] = acc_ref[...].astype(BF16)

    return pl.pallas_call(
        body,
        name="out_proj_bwd",
        grid=(r // bm,),
        in_specs=[pl.BlockSpec((bm, d), lambda i: (i, 0)), pl.BlockSpec(w_out.shape, lambda i: (0, 0)),
                  pl.BlockSpec((bm, ka), lambda i: (i, 0)), pl.BlockSpec((bm, yc.shape[1]), lambda i: (i, 0))],
        out_specs=(pl.BlockSpec((bm, n_mix), lambda i: (i, 0)),
                   pl.BlockSpec((n_mix, d), lambda i: (0, 0))),
        out_shape=(jax.ShapeDtypeStruct((r, n_mix), BF16),
                   jax.ShapeDtypeStruct((n_mix, d), BF16)),
        scratch_shapes=[pltpu.VMEM((n_mix, d), F32)],
        compiler_params=_params("arbitrary"),
    )(dhb, w_out, ya, yc)


def _attn_bwd(q, k, v, o, lse, dcat, p, g_attn):
    nb_seq, _, tp, _ = q.shape

    def body(q_ref, k_ref, v_ref, o_ref, lse_ref, dy_ref, z_ref, g_ref,
             dq_ref, dk_ref, dv_ref, dz_ref, dg_ref, dq_acc):
        @pl.when(pl.program_id(1) == 0)
        def _():
            dg_ref[...] = jnp.zeros_like(dg_ref)

        g = g_ref[...]
        z = z_ref[...].astype(F32)
        o = o_ref[0, 0]
        dy = dy_ref[...].astype(F32)
        sig = _sigmoid(z)
        ohat, r = _rms_stats(o)
        don = dy * (z * sig)
        dz_ref[...] = (dy * (ohat * g) * (sig * (1.0 + z * (1.0 - sig)))).astype(BF16)
        dg_ref[...] += jnp.sum(don * ohat, axis=0, keepdims=True)
        do = _rms_bwd(g * don, ohat, r)
        dvec = jnp.sum(do * o, axis=-1, keepdims=True)
        dob = do.astype(BF16)
        lse_col = lse_ref[0, 0, :, 0:1]
        dq_acc[...] = jnp.zeros_like(dq_acc)
        for k0 in range(0, tp, K_TILE):
            nk = min(K_TILE, tp - k0)
            nq = tp - k0
            qq = q_ref[0, 0, k0:, :]
            kk = k_ref[0, 0, k0:k0 + nk, :]
            causal = (lax.broadcasted_iota(jnp.int32, (nq, nk), 1) <= lax.broadcasted_iota(jnp.int32, (nq, nk), 0))
            pr = jnp.where(causal, jnp.exp2(_dot(qq, kk, _NT) - lse_col[k0:]), 0.0)
            dp = _dot(dob[k0:], v_ref[0, 0, k0:k0 + nk, :], _NT)
            ds = (pr * (dp - dvec[k0:])).astype(BF16)
            dv_ref[0, 0, k0:k0 + nk, :] = _dot(pr.astype(BF16), dob[k0:], _TN).astype(BF16)
            dk_ref[0, 0, k0:k0 + nk, :] = (_dot(ds, qq, _TN) * (ATTN_SCALE / Q_SCALE)).astype(BF16)
            dq_acc[k0:, :] += _dot(ds, kk)
        dq_ref[0, 0] = (dq_acc[...] * ATTN_SCALE).astype(BF16)

    qk = pl.BlockSpec((1, 1, tp, 2 * LANES), lambda h, b: (b, h, 0, 0))
    hv = pl.BlockSpec((1, 1, tp, D_V), lambda h, b: (b, h, 0, 0))
    col = pl.BlockSpec((tp, LANES), lambda h, b: (b, h))
    return pl.pallas_call(
        body,
        name="attn_bwd",
        grid=(N_HEADS, nb_seq),
        in_specs=[qk, qk, hv, hv, hv, col,
                  pl.BlockSpec((tp, LANES), lambda h, b: (b, GRP_A // LANES + h)),
                  pl.BlockSpec((1, LANES), lambda h, b: (0, h))],
        out_specs=(qk, qk, hv, col, pl.BlockSpec((1, LANES), lambda h, b: (0, h))),
        out_shape=(
            jax.ShapeDtypeStruct((nb_seq, N_HEADS, tp, 2 * LANES), BF16),
            jax.ShapeDtypeStruct((nb_seq, N_HEADS, tp, 2 * LANES), BF16),
            jax.ShapeDtypeStruct((nb_seq, N_HEADS, tp, D_V), BF16),
            jax.ShapeDtypeStruct((nb_seq * tp, N_HEADS * D_V), BF16),
            jax.ShapeDtypeStruct((1, N_HEADS * D_V), F32),
        ),
        scratch_shapes=[pltpu.VMEM((tp, 2 * LANES), F32)],
        compiler_params=_params("arbitrary", "arbitrary"),
    )(q, k, v, o, lse, dcat, p, g_attn)


def _qkv_bwd(p, dq, dk, dv, wq, wkv, gq, gkv, tables):
    nb_seq, _, tp, _ = dq.shape
    ht = tp // 2

    def body(pa_ref, dq_ref, dk_ref, dv_ref, wq_ref, wkv_ref, gq_ref, gkv_ref, cos_ref, sa_ref, sb_ref,
             dpa_ref, dwq_ref, dwkv_ref, dgq_ref, dgkv_ref):
        @pl.when(pl.program_id(0) == 0)
        def _():
            dwq_ref[...] = jnp.zeros_like(dwq_ref)
            dwkv_ref[...] = jnp.zeros_like(dwkv_ref)
            dgq_ref[...] = jnp.zeros_like(dgq_ref)
            dgkv_ref[...] = jnp.zeros_like(dgkv_ref)

        pa = pa_ref[...].astype(F32)
        gq, gkv = gq_ref[...], gkv_ref[...]
        cq_hat, rq = _rms_stats(pa[:, :Q_RANK])
        ckv_hat, rkv = _rms_stats(pa[:, Q_RANK:Q_RANK + KV_RANK])
        tabs = (cos_ref[...], sa_ref[...], sb_ref[...])

        pe = [dq_ref[0, h, :, D_NOPE:].astype(F32) for h in range(N_HEADS)]
        pairs = [_rope_t(pe[2 * i] + pltpu.roll(pe[2 * i + 1], D_ROPE, 1), *tabs).astype(BF16) for i in range(2)]
        dq_flat = jnp.concatenate([dq_ref[0, h, :, :D_NOPE] for h in range(N_HEADS)] + pairs, axis=1)
        dwq_ref[...] += _dot(dq_flat, (cq_hat * gq).astype(BF16), _TN)
        dcqn = _dot(dq_flat, wq_ref[...])
        dgq_ref[...] += jnp.sum(dcqn * cq_hat, axis=0, keepdims=True)
        dcq = _rms_bwd(gq * dcqn, cq_hat, rq)

        dkv_flat = jnp.concatenate([dk_ref[0, h, :, :D_NOPE] for h in range(N_HEADS)]
                                   + [dv_ref[0, h] for h in range(N_HEADS)], axis=1)
        dwkv_ref[...] += _dot((ckv_hat * gkv).astype(BF16), dkv_flat, _TN)
        dckvn = _dot(dkv_flat, wkv_ref[...], _NT)
        dgkv_ref[...] += jnp.sum(dckvn * ckv_hat, axis=0, keepdims=True)
        dckv = _rms_bwd(gkv * dckvn, ckv_hat, rkv)

        dk_pe = dk_ref[0, 0, :, D_NOPE:].astype(F32)
        for h in range(1, N_HEADS):
            dk_pe = dk_pe + dk_ref[0, h, :, D_NOPE:].astype(F32)
        dk_pe = jnp.where(lax.broadcasted_iota(jnp.int32, (ht, LANES), 1) < D_ROPE, dk_pe, 0.0)
        dpa_ref[...] = jnp.concatenate([dcq, dckv, _rope_t(dk_pe, *tabs)], axis=1).astype(BF16)

    full = lambda a: pl.BlockSpec(a.shape, lambda i: (0,) * a.ndim)
    tab = pl.BlockSpec((ht, LANES), lambda i: (i % 2, 0))
    qk = pl.BlockSpec((1, N_HEADS, ht, 2 * LANES), lambda i: (i // 2, 0, i % 2, 0))
    acc = lambda shape: pl.BlockSpec(shape, lambda i: (0, 0))
    return pl.pallas_call(
        body,
        name="qkv_bwd",
        grid=(2 * nb_seq,),
        in_specs=[pl.BlockSpec((ht, GRP_A), lambda i: (i, 0)), qk, qk,
                  pl.BlockSpec((1, N_HEADS, ht, D_V), lambda i: (i // 2, 0, i % 2, 0)),
                  full(wq), full(wkv), full(gq), full(gkv), tab, tab, tab],
        out_specs=(pl.BlockSpec((ht, GRP_A), lambda i: (i, 0)),
                   acc(wq.shape), acc(wkv.shape), acc((1, Q_RANK)), acc((1, KV_RANK))),
        out_shape=(
            jax.ShapeDtypeStruct((nb_seq * tp, GRP_A), BF16),
            jax.ShapeDtypeStruct(wq.shape, F32),
            jax.ShapeDtypeStruct(wkv.shape, F32),
            jax.ShapeDtypeStruct((1, Q_RANK), F32),
            jax.ShapeDtypeStruct((1, KV_RANK), F32),
        ),
        compiler_params=_params("arbitrary"),
    )(p, dq, dk, dv, wq, wkv, gq, gkv, *tables)


def _conv_bwd(p, dcat, conv_w, g_conv, nb_seq, tp):
    cols = CONV_WIDTH // LANES

    def body(b_ref, c_ref, h_ref, z_ref, dy_ref, w_ref, g_ref,
             db_ref, dc_ref, dh_ref, dz_ref, dw_ref, dg_ref):
        @pl.when(pl.program_id(1) == 0)
        def _():
            dw_ref[...] = jnp.zeros_like(dw_ref)
            dg_ref[...] = jnp.zeros_like(dg_ref)

        cb, c, h = b_ref[...].astype(F32), c_ref[...].astype(F32), h_ref[...].astype(F32)
        z, dy = z_ref[...].astype(F32), dy_ref[...].astype(F32)
        g = g_ref[...]
        w0, w1, w2 = w_ref[0:1, :], w_ref[1:2, :], w_ref[2:3, :]
        cc = c * h
        row = lax.broadcasted_iota(jnp.int32, (tp, LANES), 0)
        s1 = jnp.where(row >= 1, pltpu.roll(cc, 1, 0), 0.0)
        s2 = jnp.where(row >= 2, pltpu.roll(cc, 2, 0), 0.0)
        dwc = w0 * s2 + w1 * s1 + w2 * cc
        yc = cb * dwc
        r = lax.rsqrt(_group_mean(yc * yc) + EPS)
        ychat = yc * r
        sig = _sigmoid(z)
        dz_ref[...] = (dy * (ychat * g) * (sig * (1.0 + z * (1.0 - sig)))).astype(BF16)
        dyn = dy * (z * sig)
        dg_ref[...] += jnp.sum(dyn * ychat, axis=0, keepdims=True)
        gd = g * dyn
        dyc = r * (gd - ychat * _group_mean(gd * ychat))
        db_ref[...] = (dyc * dwc).astype(BF16)
        ddw = dyc * cb
        dw_ref[0:1, :] += jnp.sum(ddw * s2, axis=0, keepdims=True)
        dw_ref[1:2, :] += jnp.sum(ddw * s1, axis=0, keepdims=True)
        dw_ref[2:3, :] += jnp.sum(ddw * cc, axis=0, keepdims=True)
        u1 = jnp.where(row <= tp - 2, pltpu.roll(ddw, tp - 1, 0), 0.0)
        u2 = jnp.where(row <= tp - 3, pltpu.roll(ddw, tp - 2, 0), 0.0)
        dcc = w2 * ddw + w1 * u1 + w0 * u2
        dc_ref[...] = (dcc * h).astype(BF16)
        dh_ref[...] = (dcc * c).astype(BF16)

    col = pl.BlockSpec((tp, LANES), lambda t, b: (b, t))
    out = jax.ShapeDtypeStruct((nb_seq * tp, CONV_WIDTH), BF16)
    return pl.pallas_call(
        body,
        name="conv_bwd",
        grid=(cols, nb_seq),
        in_specs=_conv_specs(tp, lambda t, b, off: (b, off + t)) + [
            pl.BlockSpec((tp, LANES), lambda t, b: (b, N_HEADS * D_V // LANES + t)),
            pl.BlockSpec((8, LANES), lambda t, b: (0, t)),
            pl.BlockSpec((1, LANES), lambda t, b: (0, t))],
        out_specs=(col, col, col, col,
                   pl.BlockSpec((8, LANES), lambda t, b: (0, t)), pl.BlockSpec((1, LANES), lambda t, b: (0, t))),
        out_shape=(out, out, out, out,
                   jax.ShapeDtypeStruct((8, CONV_WIDTH), F32), jax.ShapeDtypeStruct((1, CONV_WIDTH), F32)),
        compiler_params=_params("arbitrary", "arbitrary"),
    )(p, p, p, p, dcat, conv_w, g_conv)


def _input_bwd(dps, w_in, x, meta, dh, norm_g, nt, send_in):
    nb_seq, s, d = x.shape
    r, kb = dps[0].shape
    ts = (s + LANES) // nt
    steps = nb_seq * nt
    n_dp = len(dps)
    in_slot = send_in.shape[1:]

    def body(*refs):
        dp_refs, w_ref, x_hbm, meta_ref, dh_ref, g_ref, pay_ref = refs[:n_dp], *refs[n_dp:n_dp + 6]
        o = n_dp + 6
        gx_hbm, dmeta_ref, dg_ref, r2_in = refs[o:o + 4]
        xbuf, gxbuf, tok_sems, own_in, r1_in, sum_in = refs[o + 4:o + 10]
        sems = refs[o + 10:]
        i = pl.program_id(0)
        b, k = i // nt, i % nt

        def plan():
            return _reduce_plan((pay_ref,), (own_in,), (r1_in,), (sum_in,), (r2_in,), *sems)

        @pl.when(i == 0)
        def _():
            dmeta_ref[...] = jnp.zeros_like(dmeta_ref)
            dg_ref[...] = jnp.zeros_like(dg_ref)
            plan()[0]()

        @pl.when(i == 1)
        def _():
            plan()[1]()

        def start(kk):
            if kk == 0:
                xbuf[0:PAD_FRONT, :] = jnp.zeros((PAD_FRONT, d), F32)
                xbuf[PAD_FRONT:LANES, :] = meta_ref[...]
            _token_copy(x_hbm, b, kk, ts, xbuf, tok_sems.at[0]).start()

        _for_tile(k, nt, start)
        du = _dot(dp_refs[0][...], w_ref[0:kb, :])
        for j in range(1, n_dp):
            du = du + _dot(dp_refs[j][...], w_ref[kb * j:kb * (j + 1), :])
        _for_tile(k, nt, lambda kk: _token_copy(x_hbm, b, kk, ts, xbuf, tok_sems.at[0]).wait())

        g = g_ref[...]
        hhat, rstd = _rms_stats(xbuf[...])
        dg_ref[...] += jnp.sum(du * hhat, axis=0, keepdims=True)
        res = _rms_bwd(g * du, hhat, rstd) + dh_ref[...].astype(F32)

        @pl.when(i > 0)
        def _():
            _for_tile(k, nt, lambda kk: _token_copy(gx_hbm, b, (kk - 1) % nt, ts, gxbuf, tok_sems.at[1], True).wait())

        gxbuf[...] = res

        @pl.when(k == 0)
        def _():
            dmeta_ref[...] += gxbuf[PAD_FRONT:LANES, :]

        _for_tile(k, nt, lambda kk: _token_copy(gx_hbm, b, kk, ts, gxbuf, tok_sems.at[1], True).start())

        @pl.when(i == steps - 1)
        def _():
            _token_copy(gx_hbm, b, nt - 1, ts, gxbuf, tok_sems.at[1], True).wait()
            plan()[2]()

    whole = lambda a: pl.BlockSpec(a.shape, lambda i: (0,) * a.ndim)
    hbm = pl.BlockSpec(memory_space=pl.ANY)
    return pl.pallas_call(
        body,
        name="input_bwd",
        grid=(steps,),
        in_specs=[pl.BlockSpec((ts, kb), lambda i: (i, 0)) for _ in dps]
        + [whole(w_in), hbm, whole(meta), pl.BlockSpec((ts, d), lambda i: (i, 0)), whole(norm_g), hbm],
        out_specs=(hbm, pl.BlockSpec((N_META, d), lambda i: (0, 0)), pl.BlockSpec((1, d), lambda i: (0, 0)), hbm),
        out_shape=(jax.ShapeDtypeStruct((nb_seq, s, d), F32),
                   jax.ShapeDtypeStruct((N_META, d), F32),
                   jax.ShapeDtypeStruct((1, d), F32),
                   jax.ShapeDtypeStruct((N_CHIPS,) + in_slot, BF16)),
        scratch_shapes=[pltpu.VMEM((ts, d), F32), pltpu.VMEM((ts, d), F32), pltpu.SemaphoreType.DMA((2,))]
        + _reduce_scratch([(in_slot, BF16)], [True]),
        compiler_params=_params("arbitrary"),
    )(*dps, w_in, x, meta, dh, norm_g, send_in)


def _in_proj_bwd_w(u, dps, bm, small_grads, send_out):
    r, d = u.shape
    kb = dps[0].shape[1]
    steps = r // bm
    n_dp, n_small = len(dps), len(small_grads)
    out_slot, small_slot = send_out.shape[1:], (SMALL_ROWS, LANES)

    def body(*refs):
        u_ref, dp_refs = refs[0], refs[1:1 + n_dp]
        small_refs = refs[1 + n_dp:1 + n_dp + n_small]
        o = 1 + n_dp + n_small
        pay_out, o_ref, r2_out, r2_small = refs[o:o + 4]
        acc_ref, ssmall, r1_out, sum_out, r1_small, sum_small = refs[o + 4:o + 10]
        sems = refs[o + 10:]
        i = pl.program_id(0)

        def plan():
            return _reduce_plan((pay_out, ssmall), (None, None), (r1_out, r1_small), (sum_out, sum_small),
                                (r2_out, r2_small), *sems)

        @pl.when(i == 0)
        def _():
            acc_ref[...] = jnp.zeros_like(acc_ref)
            _pack_small(ssmall, *small_refs)
            plan()[0]()

        @pl.when(i == 1)
        def _():
            plan()[1]()

        uu = u_ref[...]
        for j in range(n_dp):
            acc_ref[kb * j:kb * (j + 1), :] += _dot(dp_refs[j][...], uu, _TN)

        @pl.when(i == steps - 1)
        def _():
            for k in range(N_DEV):
                for s, e, c0 in _in_pieces(k):
                    o_ref[k, s:e, :] = acc_ref[c0:c0 + e - s, :].astype(BF16)
                o_ref[k, SHARD_IN:, :] = jnp.zeros((SHARD_IN_PAD - SHARD_IN, d), BF16)
            plan()[2]()

    whole = lambda a: pl.BlockSpec(a.shape, lambda i: (0,) * a.ndim)
    hbm = pl.BlockSpec(memory_space=pl.ANY)
    return pl.pallas_call(
        body,
        name="in_proj_bwd_w",
        grid=(steps,),
        in_specs=[pl.BlockSpec((bm, d), lambda i: (i, 0))]
        + [pl.BlockSpec((bm, kb), lambda i: (i, 0)) for _ in dps] + [whole(a) for a in small_grads]
        + [whole(send_out)],
        out_specs=(pl.BlockSpec((N_DEV, SHARD_IN_PAD, d), lambda i: (0, 0, 0)), hbm, hbm),
        out_shape=(jax.ShapeDtypeStruct((N_DEV, SHARD_IN_PAD, d), BF16),
                   jax.ShapeDtypeStruct((N_CHIPS,) + out_slot, BF16),
                   jax.ShapeDtypeStruct((N_CHIPS,) + small_slot, F32)),
        scratch_shapes=[pltpu.VMEM((kb * n_dp, d), F32), pltpu.VMEM((N_DEV,) + small_slot, F32)]
        + _reduce_scratch([(out_slot, BF16), (small_slot, F32)], [False, False]),
        compiler_params=_params("arbitrary"),
    )(u, *dps, *small_grads, send_out)


def _local_step(x, loss_target, u, p, meta_f, norm_g, w_in_p, q_norm_g, w_q_p, kv_norm_g, w_kv_p, conv_w_f,
                attn_out_g, conv_out_g, w_out_s, g_final):
    nb_seq, s, d = x.shape
    tp = s + LANES
    ht = tp // 2
    tables = _rope_tables(tp)

    q, k, v = _qkv_fwd(p, w_q_p, w_kv_p, q_norm_g, kv_norm_g, tables, nb_seq, tp)
    ya, o, lse, w_out_f = _attn_fwd(q, k, v, p, attn_out_g, w_out_s)
    yc = _conv_fwd(p, conv_w_f, conv_out_g, nb_seq, tp)
    dhb, d_final_g, loss_part = _out_proj_loss(ya, yc, w_out_f, x, loss_target, g_final, TOKEN_TILES)

    dcat, d_w_out = _out_proj_bwd(dhb, w_out_f, ya, yc, ht)
    send_out = d_w_out.reshape(N_DEV, SHARD_OUT, d)
    dq, dk, dv, dz_attn, d_attn_g = _attn_bwd(q, k, v, o, lse, dcat, p, attn_out_g)
    dpa, d_wq_p, d_wkv_p, d_gq, d_gkv = _qkv_bwd(p, dq, dk, dv, w_q_p, w_kv_p, q_norm_g, kv_norm_g, tables)
    d_b, d_c, d_h, dz_conv, d_conv_w, d_conv_g = _conv_bwd(p, dcat, conv_w_f, conv_out_g, nb_seq, tp)
    dps = (dpa, dz_attn, d_b, d_c, d_h, dz_conv)
    small = (d_wq_p, d_wkv_p, d_conv_w, d_final_g, d_gq, d_gkv, d_attn_g, d_conv_g, loss_part)
    send_in, r_out, r_small = _in_proj_bwd_w(u, dps, ht, small, send_out)
    grad_x, d_meta, d_norm_g, r_in = _input_bwd(dps, w_in_p, x, meta_f, dhb, norm_g, TOKEN_TILES, send_in)
    return grad_x, r_in, r_out, r_small, d_meta, d_norm_g


def kernel(x, meta_tokens, norm_g, w_in, q_norm_g, w_q_up, kv_norm_g, w_kv_up, conv_w, attn_out_g, conv_out_g, w_out, final_norm_g, loss_target, m_meta_tokens, m_norm_g, m_w_in, m_q_norm_g, m_w_q_up, m_kv_norm_g, m_w_kv_up, m_conv_w, m_attn_out_g, m_conv_out_g, m_w_out, m_final_norm_g, v_meta_tokens, v_norm_g, v_w_in, v_q_norm_g, v_w_q_up, v_kv_norm_g, v_w_kv_up, v_conv_w, v_attn_out_g, v_conv_out_g, v_w_out, v_final_norm_g):
    d = x.shape[-1]
    order = jnp.asarray(_tile_orders()[0])[2 * lax.axis_index("x") + lax.axis_index("y")]
    u, p, w_in_p, meta_f, w_q_p, w_kv_p, conv_w_f = _prep_in_proj(
        x, meta_tokens, norm_g, w_in[0].T, w_q_up[0].T, w_kv_up[0], conv_w.transpose(1, 0, 2), order)
    g_final = final_norm_g.reshape(1, d)
    grad_x, r_in, r_out, r_small, d_meta, d_norm_g = _local_step(
        x, loss_target, u, p, meta_f, norm_g, w_in_p, q_norm_g, w_q_p, kv_norm_g, w_kv_p, conv_w_f,
        attn_out_g, conv_out_g, w_out[0], g_final)

    flat = lambda a: a.reshape(a.shape[-2:]) if a.ndim == 3 else a.reshape(1, -1) if a.ndim == 1 else a
    transposed = ("w_in", "w_q_up")

    def to_kernel(n, a):
        if n == "conv_w":
            return a.transpose(1, 0, 2)
        return flat(a).T if n in transposed else flat(a)

    def from_kernel(n, a, shape):
        if n == "conv_w":
            return a.transpose(1, 0, 2)
        return (a.T if n in transposed else a).reshape(shape)
    params = {
        "meta_tokens": (meta_tokens, m_meta_tokens, v_meta_tokens),
        "norm_g": (norm_g, m_norm_g, v_norm_g),
        "w_in": (w_in, m_w_in, v_w_in),
        "q_norm_g": (q_norm_g, m_q_norm_g, v_q_norm_g),
        "w_q_up": (w_q_up, m_w_q_up, v_w_q_up),
        "kv_norm_g": (kv_norm_g, m_kv_norm_g, v_kv_norm_g),
        "w_kv_up": (w_kv_up, m_w_kv_up, v_w_kv_up),
        "conv_w": (conv_w, m_conv_w, v_conv_w),
        "attn_out_g": (attn_out_g, m_attn_out_g, v_attn_out_g),
        "conv_out_g": (conv_out_g, m_conv_out_g, v_conv_out_g),
        "w_out": (w_out, m_w_out, v_w_out),
        "final_norm_g": (final_norm_g, m_final_norm_g, v_final_norm_g),
    }
    grads, loss = _reduce_tail(r_in, r_out, r_small, d_meta, d_norm_g)
    updated = _adamw(grads, {n: tuple(to_kernel(n, a) for a in t) for n, t in params.items()})
    outs = [[from_kernel(n, updated[n][i], params[n][0].shape) for n, _ in PARAM_SHAPES] for i in range(4)]
    return (loss[0, 0], grad_x, *outs[0], *outs[1], *outs[2], *outs[3])
```

```python
import functools

import jax
import jax.numpy as jnp
import numpy as np
from jax import lax
from jax.experimental import pallas as pl
from jax.experimental.pallas import tpu as pltpu

F32 = jnp.float32
BF16 = jnp.bfloat16

N_META = 16
D_MODEL = 1024
N_HEADS = 4
D_NOPE = 128
D_ROPE = 64
D_V = 128
Q_RANK = 256
KV_RANK = 128
CONV_WIDTH = 512
CONV_GROUP = 64
ROPE_THETA = 10000.0
ATTN_SCALE = (D_NOPE + D_ROPE) ** -0.5
Q_SCALE = ATTN_SCALE * 1.4426950408889634
EPS = 1e-6
NEG_INF = -1e30

ADAM_LR = 0.001
ADAM_B1 = 0.9
ADAM_B2 = 0.999
ADAM_EPS = 1e-08
ADAM_WD = 0.01
ADAM_STEP = 10

LANES = 128
PAD_FRONT = LANES - N_META
K_TILE = 256
Q_TILE = 512
N_DEV = 8
VMEM_LIMIT = 56 * 1024 * 1024

IN_PAD = 3072
GRP_A = 512
N_A = Q_RANK + KV_RANK + D_ROPE
IN_PROJ = 3008
SHARD_IN = IN_PROJ // N_DEV
SHARD_IN_PAD = 384
SHARD_Q = 96
SHARD_KV = 128
SHARD_OUT = 128
SHARD_CONV = 64
SHARD_META = 128
Q_COLS = N_HEADS * (D_NOPE + D_ROPE)
KV_COLS = N_HEADS * (D_NOPE + D_V)

ROW_Q, ROW_KV, ROW_META, ROW_CONV = 0, 256, 384, 400
ROW_REPL = 408
ROW_NORM, ROW_FINAL, ROW_GQ, ROW_GKV, ROW_ATTN, ROW_CONVG, ROW_LOSS = 408, 416, 424, 426, 427, 431, 435
SMALL_ROWS = 440

PARAM_SHAPES = (
    ("meta_tokens", (N_META, SHARD_META)), ("norm_g", (1, D_MODEL)), ("w_in", (SHARD_IN, D_MODEL)),
    ("q_norm_g", (1, Q_RANK)), ("w_q_up", (SHARD_Q, Q_RANK)), ("kv_norm_g", (1, KV_RANK)),
    ("w_kv_up", (KV_RANK, SHARD_KV)), ("conv_w", (3, 1, SHARD_CONV)), ("attn_out_g", (1, CONV_WIDTH)),
    ("conv_out_g", (1, CONV_WIDTH)), ("w_out", (SHARD_OUT, D_MODEL)), ("final_norm_g", (1, D_MODEL)),
)


def _in_pieces(k):
    lo, hi = SHARD_IN * k, SHARD_IN * (k + 1)
    out = []
    if lo < N_A:
        out.append((0, min(hi, N_A) - lo, lo))
    if hi > N_A:
        s = max(lo, N_A)
        out.append((s - lo, hi - lo, s + GRP_A - N_A))
    return out


P_TILE = 256


def _tile_pieces(j):
    lo, hi = P_TILE * j, P_TILE * (j + 1)
    out = []
    for k in range(N_DEV):
        for s, e, d in _in_pieces(k):
            a, b = max(d, lo), min(d + e - s, hi)
            if a < b:
                out.append((k, s + a - d, s + b - d, a - lo))
    return out


def _tile_orders():
    n_tiles = IN_PAD // P_TILE
    sources = [{k for k, _, _, _ in _tile_pieces(j)} for j in range(n_tiles)]
    rows, n_early, n_free = [], n_tiles, n_tiles
    for chip in range(N_CHIPS):
        own = {2 * chip, 2 * chip + 1}
        diagonal = {2 * (N_CHIPS - 1 - chip), 2 * (N_CHIPS - 1 - chip) + 1}
        early = [j for j in range(n_tiles) if sources[j] <= own]
        late = [j for j in range(n_tiles) if sources[j] & diagonal]
        mid = [j for j in range(n_tiles) if j not in early and j not in late]
        rows.append(early + mid + late)
        n_early, n_free = min(n_early, len(early)), min(n_free, len(early) + len(mid))
    return np.asarray(rows, np.int32), n_early, n_free


def _q_pieces(k):
    lo, hi = SHARD_Q * k, SHARD_Q * (k + 1)
    out = []
    for h in range(N_HEADS):
        base = (D_NOPE + D_ROPE) * h
        s, e = max(lo, base), min(hi, base + D_NOPE)
        if s < e:
            out.append((s - lo, e - lo, D_NOPE * h + s - base))
        s, e = max(lo, base + D_NOPE), min(hi, base + D_NOPE + D_ROPE)
        if s < e:
            out.append((s - lo, e - lo, N_HEADS * D_NOPE + D_ROPE * h + s - base - D_NOPE))
    return out


def _kv_dst(k):
    return D_NOPE * (k // 2) + (N_HEADS * D_NOPE if k % 2 else 0)


def _params(*sem):
    return pltpu.CompilerParams(dimension_semantics=sem, vmem_limit_bytes=VMEM_LIMIT)


def _rms_stats(x):
    r = lax.rsqrt(jnp.mean(x * x, axis=-1, keepdims=True) + EPS)
    return x * r, r


def _rms_bwd(gdy, xhat, r):
    return r * (gdy - xhat * jnp.mean(gdy * xhat, axis=-1, keepdims=True))


def _sigmoid(z):
    return 1.0 / (1.0 + jnp.exp(-z))


def _group_mean(x):
    i0 = lax.broadcasted_iota(jnp.int32, (LANES, LANES), 0) // CONV_GROUP
    i1 = lax.broadcasted_iota(jnp.int32, (LANES, LANES), 1) // CONV_GROUP
    m = jnp.where(i0 == i1, 1.0 / CONV_GROUP, 0.0).astype(BF16)
    hi = x.astype(BF16)
    lo = (x - hi.astype(F32)).astype(BF16)
    return jnp.dot(hi, m, preferred_element_type=F32) + jnp.dot(lo, m, preferred_element_type=F32)


_NT = (((1,), (1,)), ((), ()))
_TN = (((0,), (0,)), ((), ()))


def _dot(a, b, dims=None):
    if dims is None:
        return jnp.dot(a, b, preferred_element_type=F32)
    return lax.dot_general(a, b, dims, preferred_element_type=F32)


def _device_position():
    x, y, c = lax.axis_index("x"), lax.axis_index("y"), lax.axis_index("c")
    return x, y, c, 4 * x + 2 * y + c


def _gather_plan(srcs, slots, send_sems, recv_sems, local_sems):
    x, y, c, _ = _device_position()
    me, sibling = (x, y, c), (x, y, 1 - c)
    flip = lambda v, on: v + on - 2 * v * on
    near = (flip(x, 1 - c), flip(y, c))
    far = (flip(x, c), flip(y, 1 - c))
    diag = (1 - x, 1 - y)
    n = len(srcs)

    def slot(a, px, py, pc):
        return slots[a].at[4 * px + 2 * py + pc]

    def copy(a, k, block, to, own=False):
        return pltpu.make_async_remote_copy(
            src_ref=srcs[a] if own else slot(a, *block),
            dst_ref=slot(a, *block),
            send_sem=send_sems.at[7 * a + k],
            recv_sem=recv_sems.at[7 * a + k],
            device_id=to,
            device_id_type=pl.DeviceIdType.MESH,
        )

    def local(a):
        return pltpu.make_async_copy(srcs[a], slot(a, *me), local_sems.at[a])

    sent = [(me, sibling), (me, (*near, c)), (me, (*far, c)), ((*near, c), (*far, c)),
            ((*near, c), sibling), ((*far, c), sibling), ((*diag, c), sibling)]
    landed = [sibling, (*near, c), (*far, c), (*diag, c), (*far, 1 - c), (*near, 1 - c), (*diag, 1 - c)]

    def send(a, k):
        return copy(a, k, *sent[k], own=k < 3)

    def arrival(a, k):
        return copy(a, k, landed[k], me)

    def start():
        for a in range(n):
            local(a).start()
            for k in range(3):
                send(a, k).start()

    def own():
        for a in range(n):
            local(a).wait()
            arrival(a, 0).wait_recv()

    def mid():
        for a in range(n):
            arrival(a, 1).wait_recv()
            send(a, 3).start()
            send(a, 4).start()
        for a in range(n):
            arrival(a, 2).wait_recv()
            send(a, 5).start()
        for a in range(n):
            for k in (4, 5):
                arrival(a, k).wait_recv()

    def late():
        for a in range(n):
            arrival(a, 3).wait_recv()
            send(a, 6).start()
        for a in range(n):
            arrival(a, 6).wait_recv()

    def finish():
        for a in range(n):
            for k in range(7):
                send(a, k).wait_send()

    return start, own, mid, late, finish


def _adam_update(g, w, m, v):
    m_new = ADAM_B1 * m + (1.0 - ADAM_B1) * g
    v_new = ADAM_B2 * v + (1.0 - ADAM_B2) * (g * g)
    m_hat = m_new / (1.0 - ADAM_B1 ** ADAM_STEP)
    v_hat = v_new / (1.0 - ADAM_B2 ** ADAM_STEP)
    return -ADAM_LR * (m_hat / (jnp.sqrt(v_hat) + ADAM_EPS) + ADAM_WD * w), m_new, v_new


def _adamw(grads, params):
    names = [n for n, _ in PARAM_SHAPES]
    n_p = len(names)

    def body(*refs):
        for i in range(n_p):
            g = refs[i][...]
            w, m, v = (refs[n_p + 3 * i + j][...] for j in range(3))
            delta, m_new, v_new = _adam_update(g, w, m, v)
            for j, val in enumerate((g, delta, m_new, v_new)):
                refs[4 * n_p + 4 * i + j][...] = val

    vm = pl.BlockSpec(memory_space=pltpu.VMEM)
    out_shape = []
    for _, shape in PARAM_SHAPES:
        out_shape += [jax.ShapeDtypeStruct(shape, F32)] * 4
    outs = pl.pallas_call(
        body,
        name="adamw",
        out_shape=tuple(out_shape),
        in_specs=[vm] * (4 * n_p),
        out_specs=(vm,) * (4 * n_p),
        compiler_params=pltpu.CompilerParams(vmem_limit_bytes=VMEM_LIMIT),
    )(*[grads[n] for n in names], *[a for n in names for a in params[n]])
    return {n: outs[4 * i:4 * i + 4] for i, n in enumerate(names)}


N_CHIPS = 4


def _reduce_plan(pays, owns, r1s, sums, r2s, send1, recv1, send2, recv2, local_sems):
    x, y, c, _ = _device_position()
    sibling = (x, y, 1 - c)
    chips = [((1 - x if rj & 2 else x), (1 - y if rj & 1 else y)) for rj in range(N_CHIPS)]
    n = len(pays)

    def slot_of(rj, core):
        return 4 * chips[rj][0] + 2 * chips[rj][1] + core

    def to_sibling(a, rj):
        return pltpu.make_async_remote_copy(
            src_ref=pays[a].at[slot_of(rj, 1 - c)], dst_ref=r1s[a].at[rj],
            send_sem=send1.at[N_CHIPS * a + rj], recv_sem=recv1.at[N_CHIPS * a + rj],
            device_id=sibling, device_id_type=pl.DeviceIdType.MESH)

    def load_own(a, rj):
        return pltpu.make_async_copy(pays[a].at[slot_of(rj, c)], owns[a].at[rj], local_sems.at[2 * N_CHIPS * a + rj])

    def to_chip(a, rj):
        return pltpu.make_async_remote_copy(
            src_ref=sums[a].at[rj], dst_ref=r2s[a].at[rj],
            send_sem=send2.at[N_CHIPS * a + rj], recv_sem=recv2.at[N_CHIPS * a + rj],
            device_id=(*chips[rj], c), device_id_type=pl.DeviceIdType.MESH)

    def keep(a):
        return pltpu.make_async_copy(sums[a].at[0], r2s[a].at[0], local_sems.at[2 * N_CHIPS * a + N_CHIPS])

    def start():
        for a in range(n):
            for rj in range(N_CHIPS):
                to_sibling(a, rj).start()
                if owns[a] is not None:
                    load_own(a, rj).start()

    def combine():
        for a in range(n):
            for rj in range(N_CHIPS):
                to_sibling(a, rj).wait_recv()
                if owns[a] is not None:
                    load_own(a, rj).wait()
                    mine = owns[a][rj]
                else:
                    mine = pays[a][slot_of(rj, c)]
                sums[a][rj] = (mine.astype(F32) + r1s[a][rj].astype(F32)).astype(sums[a].dtype)
            keep(a).start()
            for rj in range(1, N_CHIPS):
                to_chip(a, rj).start()

    def finish():
        for a in range(n):
            for rj in range(1, N_CHIPS):
                to_chip(a, rj).wait_recv()
            for rj in range(N_CHIPS):
                to_sibling(a, rj).wait_send()
            for rj in range(1, N_CHIPS):
                to_chip(a, rj).wait_send()
            keep(a).wait()

    return start, combine, finish


def _reduce_scratch(shapes_dtypes, own_flags):
    out = []
    for (shape, dtype), own in zip(shapes_dtypes, own_flags):
        if own:
            out.append(pltpu.VMEM((N_CHIPS,) + shape, dtype))
        out += [pltpu.VMEM((N_CHIPS,) + shape, dtype), pltpu.VMEM((N_CHIPS,) + shape, dtype)]
    n = len(shapes_dtypes)
    out += [pltpu.SemaphoreType.DMA((N_CHIPS * n,))] * 4 + [pltpu.SemaphoreType.DMA((2 * N_CHIPS * n,))]
    return out


def _pack_small(ssmall, dwq, dwkv, dconv, dfinal, dgq, dgkv, dattn, dconvg, loss_part):
    ssmall[...] = jnp.zeros_like(ssmall)
    rep = ssmall.at[0]
    for i in range(D_MODEL // LANES):
        rep[ROW_FINAL + i:ROW_FINAL + i + 1, :] = dfinal[:, LANES * i:LANES * (i + 1)]
    for i in range(Q_RANK // LANES):
        rep[ROW_GQ + i:ROW_GQ + i + 1, :] = dgq[:, LANES * i:LANES * (i + 1)]
    rep[ROW_GKV:ROW_GKV + 1, :] = dgkv[...]
    for i in range(CONV_WIDTH // LANES):
        rep[ROW_ATTN + i:ROW_ATTN + i + 1, :] = dattn[:, LANES * i:LANES * (i + 1)]
        rep[ROW_CONVG + i:ROW_CONVG + i + 1, :] = dconvg[:, LANES * i:LANES * (i + 1)]
    rep[ROW_LOSS:ROW_LOSS + 1, :] = loss_part[...]
    for k in range(N_DEV):
        if k:
            ssmall[k, ROW_REPL:, :] = ssmall[0, ROW_REPL:, :]
        for s, e, d in _q_pieces(k):
            for i in range(Q_RANK // LANES):
                ssmall[k, ROW_Q + SHARD_Q * i + s:ROW_Q + SHARD_Q * i + e, :] = dwq[d:d + e - s, LANES * i:LANES * (i + 1)]
        ssmall[k, ROW_KV:ROW_KV + KV_RANK, :] = dwkv[:, _kv_dst(k):_kv_dst(k) + SHARD_KV]
        ssmall[k, ROW_CONV:ROW_CONV + 3, 0:SHARD_CONV] = dconv[0:3, SHARD_CONV * k:SHARD_CONV * (k + 1)]


TOKEN_TILES = 4
TAIL_ROWS = N_META + D_MODEL // LANES


def _reduce_tail(r_in, r_out, r_small, d_meta, d_norm):
    n_p = len(PARAM_SHAPES)
    names = [n for n, _ in PARAM_SHAPES]

    def body(*refs):
        rin, rout, rsmall, dmeta, dnorm = refs[:5]
        g_out = {n: refs[5 + i] for i, n in enumerate(names)}
        loss_out = refs[5 + n_p]
        stail, rtail, gsum, gtail, send_sems, recv_sems = refs[6 + n_p:]
        x, y, c, me = _device_position()
        my_chip = 2 * x + y

        for k in range(N_DEV):
            stail[k, 0:N_META, :] = dmeta[:, SHARD_META * k:SHARD_META * (k + 1)]
            for i in range(D_MODEL // LANES):
                stail[k, N_META + i:N_META + i + 1, :] = dnorm[:, LANES * i:LANES * (i + 1)]
        copies = []
        for r in range(1, N_DEV):
            peer = (1 - x if r & 4 else x, 1 - y if r & 2 else y, 1 - c if r & 1 else c)
            copies.append(pltpu.make_async_remote_copy(
                src_ref=stail.at[4 * peer[0] + 2 * peer[1] + peer[2]],
                dst_ref=rtail.at[r],
                send_sem=send_sems.at[r - 1],
                recv_sem=recv_sems.at[r - 1],
                device_id=peer,
                device_id_type=pl.DeviceIdType.MESH,
            ))
        for cp in copies:
            cp.start()
        rtail[0] = stail[me]

        g = rin[my_chip].astype(F32)
        for ch in range(1, N_CHIPS):
            g = g + rin[ch ^ my_chip].astype(F32)
        g_out["w_in"][...] = g[:SHARD_IN, :]

        g = rout[my_chip].astype(F32)
        gs = rsmall[my_chip]
        for ch in range(1, N_CHIPS):
            g = g + rout[ch ^ my_chip].astype(F32)
            gs = gs + rsmall[ch ^ my_chip]
        g_out["w_out"][...] = g
        gsum[...] = gs
        for i in range(Q_RANK // LANES):
            g_out["w_q_up"][:, LANES * i:LANES * (i + 1)] = gsum[ROW_Q + SHARD_Q * i:ROW_Q + SHARD_Q * (i + 1), :]
        g_out["w_kv_up"][...] = gsum[ROW_KV:ROW_KV + KV_RANK, :]
        for i in range(3):
            g_out["conv_w"][i] = gsum[ROW_CONV + i:ROW_CONV + i + 1, 0:SHARD_CONV]
        for name, row, width in (("final_norm_g", ROW_FINAL, D_MODEL), ("q_norm_g", ROW_GQ, Q_RANK),
                                 ("kv_norm_g", ROW_GKV, KV_RANK), ("attn_out_g", ROW_ATTN, CONV_WIDTH),
                                 ("conv_out_g", ROW_CONVG, CONV_WIDTH)):
            for i in range(width // LANES):
                g_out[name][:, LANES * i:LANES * (i + 1)] = gsum[row + i:row + i + 1, :]
        loss_out[...] = gsum[ROW_LOSS:ROW_LOSS + 1, :]

        for cp in copies:
            cp.wait_recv()
        gt = rtail[me]
        for d in range(1, N_DEV):
            gt = gt + rtail[d ^ me]
        gtail[...] = gt
        g_out["meta_tokens"][...] = gtail[0:N_META, :]
        for i in range(D_MODEL // LANES):
            g_out["norm_g"][:, LANES * i:LANES * (i + 1)] = gtail[N_META + i:N_META + i + 1, :]
        for cp in copies:
            cp.wait_send()

    vm = pl.BlockSpec(memory_space=pltpu.VMEM)
    out_shape = [jax.ShapeDtypeStruct(shape, F32) for _, shape in PARAM_SHAPES]
    out_shape.append(jax.ShapeDtypeStruct((1, LANES), F32))
    outs = pl.pallas_call(
        body,
        name="reduce_tail",
        out_shape=tuple(out_shape),
        in_specs=[vm] * 5,
        out_specs=(vm,) * len(out_shape),
        scratch_shapes=[
            pltpu.VMEM((N_DEV, TAIL_ROWS, LANES), F32),
            pltpu.VMEM((N_DEV, TAIL_ROWS, LANES), F32),
            pltpu.VMEM((SMALL_ROWS, LANES), F32),
            pltpu.VMEM((TAIL_ROWS, LANES), F32),
            pltpu.SemaphoreType.DMA((N_DEV - 1,)),
            pltpu.SemaphoreType.DMA((N_DEV - 1,)),
        ],
        compiler_params=pltpu.CompilerParams(vmem_limit_bytes=VMEM_LIMIT),
    )(r_in, r_out, r_small, d_meta, d_norm)
    return {n: outs[i] for i, n in enumerate(names)}, outs[-1]


def _prep_in_proj(x, meta, norm_g, w_in_t, w_q, w_kv, conv_w, order):
    nb_seq, s, d = x.shape
    tp = s + LANES
    m = nb_seq * tp
    ts = s // TOKEN_TILES
    n_real = nb_seq * TOKEN_TILES
    n_norm = n_real + 1
    n_tiles = IN_PAD // P_TILE
    _, n_early, n_free = _tile_orders()
    steps = n_norm + n_tiles
    dot_rows = m // 2
    qkv_shape = (SHARD_Q + KV_RANK, Q_RANK)

    def tile(t):
        t = jnp.minimum(t, n_real - 1)
        return t // TOKEN_TILES, t % TOKEN_TILES

    def column_tile(t, order_ref):
        return order_ref[jnp.maximum(t - n_norm, 0)]

    def body(order_ref, x_ref, meta_ref, g_ref, win_ref, wq_ref, wkv_ref, conv_ref,
             u_hbm, p_ref, w_in_p, meta_f, w_q_p, w_kv_p, conv_f,
             u_all, sbig, gbig, smeta, gmeta, sqkv, sconv, gqkv, gconv,
             send_in, recv_in, local_in, send_meta, recv_meta, send_rest, recv_rest, local_rest, u_sem):
        t = pl.program_id(0)
        px, py, pc, me = _device_position()
        u_copy = pltpu.make_async_copy(u_all, u_hbm, u_sem)

        def plan_in():
            return _gather_plan((sbig,), (gbig,), send_in, recv_in, local_in)

        def plan_rest():
            return _gather_plan((sqkv, sconv), (gqkv, gconv), send_rest, recv_rest, local_rest)

        def meta_copies():
            out = []
            for r in range(1, N_DEV):
                peer = (1 - px if r & 4 else px, 1 - py if r & 2 else py, 1 - pc if r & 1 else pc)
                out.append(pltpu.make_async_remote_copy(
                    src_ref=smeta,
                    dst_ref=gmeta.at[r],
                    send_sem=send_meta.at[r - 1],
                    recv_sem=recv_meta.at[r - 1],
                    device_id=peer,
                    device_id_type=pl.DeviceIdType.MESH,
                ))
            return out

        @pl.when(t == 0)
        def _():
            sbig[0:SHARD_IN, :] = win_ref[...].astype(BF16)
            sbig[SHARD_IN:, :] = jnp.zeros((SHARD_IN_PAD - SHARD_IN, d), BF16)
            smeta[...] = meta_ref[...]
            plan_in()[0]()
            for cp in meta_copies():
                cp.start()
            sqkv[...] = jnp.zeros_like(sqkv)
            sqkv[0:SHARD_Q, :] = wq_ref[...].astype(BF16)
            sqkv[SHARD_Q:, 0:SHARD_KV] = wkv_ref[...].astype(BF16)
            sconv[...] = jnp.zeros_like(sconv)
            for i in range(3):
                sconv[i:i + 1, 0:SHARD_CONV] = conv_ref[i]

        def norm(h):
            hhat, _ = _rms_stats(h)
            return (hhat * g_ref[...]).astype(BF16)

        @pl.when(t < n_real)
        def _():
            b, k = tile(t)
            row0 = pl.multiple_of(b * tp + LANES + k * ts, 16)
            u_all[pl.ds(row0, ts), :] = norm(x_ref[0])

        @pl.when(t == n_real)
        def _():
            for cp in meta_copies():
                cp.wait_recv()
            gmeta[0] = smeta[...]
            for k in range(N_DEV):
                meta_f[:, SHARD_META * k:SHARD_META * (k + 1)] = gmeta[k ^ me]
            um = norm(meta_f[...])
            for b in range(nb_seq):
                u_all[b * tp:b * tp + PAD_FRONT, :] = jnp.zeros((PAD_FRONT, d), BF16)
                u_all[b * tp + PAD_FRONT:b * tp + LANES, :] = um
            u_copy.start()

        @pl.when(t == n_norm)
        def _():
            plan_in()[1]()

        @pl.when(t == n_norm + n_early)
        def _():
            plan_in()[2]()
            plan_rest()[0]()

        @pl.when(t == n_norm + (n_early + n_free) // 2)
        def _():
            plan_rest()[1]()
            plan_rest()[2]()

        @pl.when(t == n_norm + n_free)
        def _():
            plan_in()[3]()

        @pl.when(t == n_norm + n_free + 2)
        def _():
            plan_rest()[3]()

        @pl.when(t >= n_norm)
        def _():
            j = column_tile(t, order_ref)
            for jj in range(n_tiles):
                @pl.when(j == jj)
                def _(jj=jj):
                    if jj == N_A // P_TILE:
                        w_in_p[N_A % P_TILE:, :] = jnp.zeros((P_TILE - N_A % P_TILE, d), BF16)
                    for k, s0, e0, d0 in _tile_pieces(jj):
                        w_in_p[d0:d0 + e0 - s0, :] = gbig[k, s0:e0, :]
            for r in range(m // dot_rows):
                rows = slice(dot_rows * r, dot_rows * (r + 1))
                p_ref[rows, :] = _dot(u_all[rows, :], w_in_p[...], _NT).astype(BF16)

        @pl.when(t == steps - 1)
        def _():
            plan_in()[4]()
            plan_rest()[4]()
            for cp in meta_copies():
                cp.wait_send()
            u_copy.wait()
            conv_f[...] = jnp.zeros_like(conv_f)
            for k in range(N_DEV):
                for s0, e0, d0 in _q_pieces(k):
                    w_q_p[d0:d0 + e0 - s0, :] = gqkv[k, s0:e0, :]
                w_kv_p[:, _kv_dst(k):_kv_dst(k) + SHARD_KV] = gqkv[k, SHARD_Q:, 0:SHARD_KV]
                conv_f[0:3, SHARD_CONV * k:SHARD_CONV * (k + 1)] = gconv[k, 0:3, 0:SHARD_CONV]

    whole = lambda shape: pl.BlockSpec(shape, lambda t, o: (0,) * len(shape))
    return pl.pallas_call(
        body,
        name="prep_in_proj_gather",
        grid_spec=pltpu.PrefetchScalarGridSpec(
            num_scalar_prefetch=1,
            grid=(steps,),
            in_specs=[
                pl.BlockSpec((1, ts, d), lambda t, o: (*tile(t), 0)),
                whole(meta.shape), whole(norm_g.shape), whole(w_in_t.shape),
                whole(w_q.shape), whole(w_kv.shape), whole(conv_w.shape),
            ],
            out_specs=(pl.BlockSpec(memory_space=pl.ANY),
                       pl.BlockSpec((m, P_TILE), lambda t, o: (0, column_tile(t, o))),
                       pl.BlockSpec((P_TILE, d), lambda t, o: (column_tile(t, o), 0)),
                       whole((N_META, d)),
                       whole((Q_COLS, Q_RANK)), whole((KV_RANK, KV_COLS)), whole((8, CONV_WIDTH))),
            scratch_shapes=[
                pltpu.VMEM((m, d), BF16),
                pltpu.VMEM((SHARD_IN_PAD, d), BF16),
                pltpu.VMEM((N_DEV, SHARD_IN_PAD, d), BF16),
                pltpu.VMEM((N_META, SHARD_META), F32),
                pltpu.VMEM((N_DEV, N_META, SHARD_META), F32),
                pltpu.VMEM(qkv_shape, BF16),
                pltpu.VMEM((8, LANES), F32),
                pltpu.VMEM((N_DEV,) + qkv_shape, BF16),
                pltpu.VMEM((N_DEV, 8, LANES), F32),
                pltpu.SemaphoreType.DMA((7,)),
                pltpu.SemaphoreType.DMA((7,)),
                pltpu.SemaphoreType.DMA((1,)),
                pltpu.SemaphoreType.DMA((N_DEV - 1,)),
                pltpu.SemaphoreType.DMA((N_DEV - 1,)),
                pltpu.SemaphoreType.DMA((14,)),
                pltpu.SemaphoreType.DMA((14,)),
                pltpu.SemaphoreType.DMA((2,)),
                pltpu.SemaphoreType.DMA,
            ],
        ),
        out_shape=(jax.ShapeDtypeStruct((m, d), BF16),
                   jax.ShapeDtypeStruct((m, IN_PAD), BF16),
                   jax.ShapeDtypeStruct((IN_PAD, d), BF16),
                   jax.ShapeDtypeStruct((N_META, d), F32),
                   jax.ShapeDtypeStruct((Q_COLS, Q_RANK), BF16),
                   jax.ShapeDtypeStruct((KV_RANK, KV_COLS), BF16),
                   jax.ShapeDtypeStruct((8, CONV_WIDTH), F32)),
        compiler_params=_params("arbitrary"),
    )(order, x, meta, norm_g, w_in_t, w_q, w_kv, conv_w)


def _rope_tables(tp):
    half = D_ROPE // 2
    inv_freq = (1.0 / (ROPE_THETA ** (np.arange(half, dtype=np.float32) / half))).astype(np.float32)
    pos = (np.arange(tp) - PAD_FRONT).astype(np.float32)
    ang = pos[:, None] * inv_freq[None, :]
    cos = np.tile(np.cos(ang), (1, LANES // half))
    sin = np.tile(np.sin(ang), (1, LANES // half))
    first = (np.arange(LANES) % D_ROPE) < half
    zero = np.float32(0.0)
    return tuple(jnp.asarray(t, F32) for t in (cos, np.where(first, -sin, zero), np.where(first, zero, sin)))


def _rope(t, cos, sa, sb):
    return t * cos + pltpu.roll(t, LANES - D_ROPE // 2, 1) * sa + pltpu.roll(t, D_ROPE // 2, 1) * sb


def _rope_t(t, cos, sa, sb):
    return t * cos + pltpu.roll(t * sa, D_ROPE // 2, 1) + pltpu.roll(t * sb, LANES - D_ROPE // 2, 1)


def _qkv_fwd(p, wq, wkv, gq, gkv, tables, nb_seq, tp):
    ht = tp // 2

    def body(pa_ref, wq_ref, wkv_ref, gq_ref, gkv_ref, cos_ref, sa_ref, sb_ref, q_ref, k_ref, v_ref):
        pa = pa_ref[...].astype(F32)
        cq_hat, _ = _rms_stats(pa[:, :Q_RANK])
        ckv_hat, _ = _rms_stats(pa[:, Q_RANK:Q_RANK + KV_RANK])
        q = _dot((cq_hat * gq_ref[...]).astype(BF16), wq_ref[...], _NT) * Q_SCALE
        kv = _dot((ckv_hat * gkv_ref[...]).astype(BF16), wkv_ref[...])
        tabs = (cos_ref[...], sa_ref[...], sb_ref[...])
        lane = lax.broadcasted_iota(jnp.int32, (ht, LANES), 1)
        low = lane < D_ROPE
        mark = lane == D_ROPE
        row = (pl.program_id(0) % 2) * ht + lax.broadcasted_iota(jnp.int32, (ht, LANES), 0)
        k_pe = jnp.where(mark & (row < PAD_FRONT), NEG_INF, _rope(pa[:, Q_RANK + KV_RANK:], *tabs))
        one = jnp.where(mark & (row >= PAD_FRONT), 1.0, 0.0)
        pairs = [_rope(q[:, N_HEADS * D_NOPE + LANES * i:N_HEADS * D_NOPE + LANES * (i + 1)], *tabs) for i in range(2)]
        for h in range(N_HEADS):
            pair = pairs[h // 2]
            if h % 2:
                pair = pltpu.roll(pair, D_ROPE, 1)
            pe = jnp.where(low, pair, one)
            q_ref[0, h] = jnp.concatenate([q[:, D_NOPE * h:D_NOPE * (h + 1)], pe], axis=1).astype(BF16)
            k_ref[0, h] = jnp.concatenate([kv[:, D_NOPE * h:D_NOPE * (h + 1)], k_pe], axis=1).astype(BF16)
            v_ref[0, h] = kv[:, N_HEADS * D_NOPE + D_V * h:N_HEADS * D_NOPE + D_V * (h + 1)].astype(BF16)

    full = lambda a: pl.BlockSpec(a.shape, lambda i: (0,) * a.ndim)
    tab = pl.BlockSpec((ht, LANES), lambda i: (i % 2, 0))
    qk = pl.BlockSpec((1, N_HEADS, ht, 2 * LANES), lambda i: (i // 2, 0, i % 2, 0))
    return pl.pallas_call(
        body,
        name="qkv_fwd",
        grid=(2 * nb_seq,),
        in_specs=[pl.BlockSpec((ht, GRP_A), lambda i: (i, 0)), full(wq), full(wkv), full(gq), full(gkv), tab, tab, tab],
        out_specs=(qk, qk, pl.BlockSpec((1, N_HEADS, ht, D_V), lambda i: (i // 2, 0, i % 2, 0))),
        out_shape=(
            jax.ShapeDtypeStruct((nb_seq, N_HEADS, tp, 2 * LANES), BF16),
            jax.ShapeDtypeStruct((nb_seq, N_HEADS, tp, 2 * LANES), BF16),
            jax.ShapeDtypeStruct((nb_seq, N_HEADS, tp, D_V), BF16),
        ),
        compiler_params=_params("parallel"),
    )(p, wq, wkv, gq, gkv, *tables)


def _attn_fwd(q, k, v, p, g_attn, w_out):
    nb_seq, _, tp, _ = q.shape
    steps = nb_seq * N_HEADS

    def body(q_ref, k_ref, v_ref, z_ref, g_ref, wout_ref, y_ref, o_ref, lse_ref, w_out_f,
             sout, gout, send_sems, recv_sems, local_sems):
        t = pl.program_id(0) * N_HEADS + pl.program_id(1)

        def plan():
            return _gather_plan((sout,), (gout,), send_sems, recv_sems, local_sems)

        @pl.when(t == 0)
        def _():
            sout[...] = wout_ref[...].astype(BF16)
            plan()[0]()

        @pl.when(t == steps // 2)
        def _():
            plan()[1]()
            plan()[2]()

        @pl.when(t == steps - 2)
        def _():
            plan()[3]()

        g = g_ref[...]
        for r0 in range(0, tp, Q_TILE):
            nq = min(Q_TILE, tp - r0)
            kend = r0 + nq
            qq = q_ref[0, 0, r0:kend, :]
            sd = _dot(qq, k_ref[0, 0, r0:kend, :], _NT)
            causal = (lax.broadcasted_iota(jnp.int32, (nq, nq), 1) <= lax.broadcasted_iota(jnp.int32, (nq, nq), 0))
            sd = jnp.where(causal, sd, NEG_INF)
            m = jnp.max(sd, axis=-1, keepdims=True)
            if r0:
                so = _dot(qq, k_ref[0, 0, 0:r0, :], _NT)
                m = jnp.maximum(m, jnp.max(so, axis=-1, keepdims=True))
            ed = jnp.exp2(sd - m)
            l = jnp.sum(ed, axis=-1, keepdims=True)
            o = _dot(ed.astype(BF16), v_ref[0, 0, r0:kend, :])
            if r0:
                eo = jnp.exp2(so - m)
                l = l + jnp.sum(eo, axis=-1, keepdims=True)
                o = o + _dot(eo.astype(BF16), v_ref[0, 0, 0:r0, :])
            o = o * (1.0 / l)
            o_ref[0, 0, r0:kend, :] = o
            lse_ref[0, 0, r0:kend, :] = jnp.broadcast_to(m + jnp.log2(l), (nq, LANES))
            ohat, _ = _rms_stats(o)
            z = z_ref[r0:kend, :].astype(F32)
            y_ref[r0:kend, :] = (ohat * g * (z * _sigmoid(z))).astype(BF16)

        @pl.when(t == steps - 1)
        def _():
            plan()[4]()
            for dev in range(N_DEV):
                w_out_f[SHARD_OUT * dev:SHARD_OUT * (dev + 1), :] = gout[dev]

    qk = pl.BlockSpec((1, 1, tp, 2 * LANES), lambda b, h: (b, h, 0, 0))
    hv = pl.BlockSpec((1, 1, tp, D_V), lambda b, h: (b, h, 0, 0))
    return pl.pallas_call(
        body,
        name="attn_fwd",
        grid=(nb_seq, N_HEADS),
        in_specs=[qk, qk, hv,
                  pl.BlockSpec((tp, LANES), lambda b, h: (b, GRP_A // LANES + h)),
                  pl.BlockSpec((1, LANES), lambda b, h: (0, h)),
                  pl.BlockSpec(w_out.shape, lambda b, h: (0, 0))],
        out_specs=(pl.BlockSpec((tp, LANES), lambda b, h: (b, h)), hv, hv,
                   pl.BlockSpec((D_MODEL, D_MODEL), lambda b, h: (0, 0))),
        out_shape=(
            jax.ShapeDtypeStruct((nb_seq * tp, N_HEADS * D_V), BF16),
            jax.ShapeDtypeStruct((nb_seq, N_HEADS, tp, D_V), F32),
            jax.ShapeDtypeStruct((nb_seq, N_HEADS, tp, LANES), F32),
            jax.ShapeDtypeStruct((D_MODEL, D_MODEL), BF16),
        ),
        scratch_shapes=[
            pltpu.VMEM((SHARD_OUT, D_MODEL), BF16),
            pltpu.VMEM((N_DEV, SHARD_OUT, D_MODEL), BF16),
            pltpu.SemaphoreType.DMA((7,)),
            pltpu.SemaphoreType.DMA((7,)),
            pltpu.SemaphoreType.DMA((1,)),
        ],
        compiler_params=_params("arbitrary", "arbitrary"),
    )(q, k, v, p, g_attn, w_out)


_CONV_COL0 = (GRP_A + N_HEADS * D_V) // LANES


def _conv_specs(tp, order):
    cols = CONV_WIDTH // LANES
    return [pl.BlockSpec((tp, LANES), functools.partial(
        lambda a, b, off: order(a, b, off), off=_CONV_COL0 + i * cols)) for i in range(4)]


def _conv_fwd(p, conv_w, g_conv, nb_seq, tp):
    def body(b_ref, c_ref, h_ref, z_ref, w_ref, g_ref, y_ref):
        cc = c_ref[...].astype(F32) * h_ref[...].astype(F32)
        row = lax.broadcasted_iota(jnp.int32, (tp, LANES), 0)
        s1 = jnp.where(row >= 1, pltpu.roll(cc, 1, 0), 0.0)
        s2 = jnp.where(row >= 2, pltpu.roll(cc, 2, 0), 0.0)
        yc = b_ref[...].astype(F32) * (w_ref[0:1, :] * s2 + w_ref[1:2, :] * s1 + w_ref[2:3, :] * cc)
        r = lax.rsqrt(_group_mean(yc * yc) + EPS)
        z = z_ref[...].astype(F32)
        y_ref[...] = (yc * r * g_ref[...] * (z * _sigmoid(z))).astype(BF16)

    return pl.pallas_call(
        body,
        name="conv_fwd",
        grid=(nb_seq, CONV_WIDTH // LANES),
        in_specs=_conv_specs(tp, lambda b, t, off: (b, off + t)) + [
            pl.BlockSpec((8, LANES), lambda b, t: (0, t)),
            pl.BlockSpec((1, LANES), lambda b, t: (0, t))],
        out_specs=pl.BlockSpec((tp, LANES), lambda b, t: (b, t)),
        out_shape=jax.ShapeDtypeStruct((nb_seq * tp, CONV_WIDTH), BF16),
        compiler_params=_params("parallel", "parallel"),
    )(p, p, p, p, conv_w, g_conv)


def _token_copy(hbm, b, k, ts, buf, sem, to_hbm=False):
    lo, hi = max(k * ts - LANES, 0), (k + 1) * ts - LANES
    off = lo - (k * ts - LANES)
    src, dst = hbm.at[b, pl.ds(lo, hi - lo)], buf.at[pl.ds(off, hi - lo)]
    if to_hbm:
        src, dst = dst, src
    return pltpu.make_async_copy(src, dst, sem)


def _for_tile(k, nt, fn):
    for kk in range(nt):
        @pl.when(k == kk)
        def _(kk=kk):
            fn(kk)


def _out_proj_loss(ya, yc, w_out, x, target, g_final, nt):
    nb_seq, s, d = x.shape
    r, ka = ya.shape
    ts = (s + LANES) // nt
    steps = nb_seq * nt

    def body(a_ref, c_ref, w_ref, x_hbm, t_hbm, g_ref, dhb_ref, dg_ref, loss_ref,
             xbuf, tbuf, acc_ref, sems):
        i = pl.program_id(0)
        b, k = i // nt, i % nt

        @pl.when(i == 0)
        def _():
            acc_ref[...] = jnp.zeros_like(acc_ref)
            dg_ref[...] = jnp.zeros_like(dg_ref)

        slot = i % 2

        def fetch(seq, kk, sl):
            return [_token_copy(x_hbm, seq, kk, ts, xbuf.at[sl], sems.at[sl, 0]),
                    _token_copy(t_hbm, seq, kk, ts, tbuf.at[sl], sems.at[sl, 1])]

        def start(seq, sl, kk):
            if kk == 0:
                xbuf[sl, 0:LANES, :] = jnp.zeros((LANES, d), F32)
                tbuf[sl, 0:LANES, :] = jnp.zeros((LANES, d), F32)
            for cp in fetch(seq, kk, sl):
                cp.start()

        @pl.when(i == 0)
        def _():
            start(0, 0, 0)

        @pl.when(i + 1 < steps)
        def _():
            _for_tile((i + 1) % nt, nt, functools.partial(start, (i + 1) // nt, 1 - slot))

        mix = _dot(a_ref[...], w_ref[0:ka, :]) + _dot(c_ref[...], w_ref[ka:, :])
        _for_tile(k, nt, lambda kk: [cp.wait() for cp in fetch(b, kk, slot)])

        real = (lax.broadcasted_iota(jnp.int32, (ts, d), 0) >= LANES) | (k > 0)
        g = g_ref[...]
        hhat, rstd = _rms_stats(xbuf[slot] + mix)
        e = jnp.where(real, hhat * g - tbuf[slot], 0.0)
        acc_ref[...] += jnp.sum(e * e, axis=0, keepdims=True)
        dy = e * (1.0 / d)
        dg_ref[...] += jnp.sum(dy * hhat, axis=0, keepdims=True)
        dhb_ref[...] = _rms_bwd(g * dy, hhat, rstd).astype(BF16)

        @pl.when(i == steps - 1)
        def _():
            total = jnp.sum(acc_ref[...], axis=1, keepdims=True)
            loss_ref[...] = jnp.broadcast_to((0.5 / d) * total, loss_ref.shape)

    hbm = pl.BlockSpec(memory_space=pl.ANY)
    row = pl.BlockSpec((ts, d), lambda i: (i, 0))
    vec = pl.BlockSpec((1, d), lambda i: (0, 0))
    return pl.pallas_call(
        body,
        name="out_proj_loss",
        grid=(steps,),
        in_specs=[pl.BlockSpec((ts, ka), lambda i: (i, 0)), pl.BlockSpec((ts, yc.shape[1]), lambda i: (i, 0)),
                  pl.BlockSpec(w_out.shape, lambda i: (0, 0)), hbm, hbm, vec],
        out_specs=(row, vec, pl.BlockSpec((1, LANES), lambda i: (0, 0))),
        out_shape=(
            jax.ShapeDtypeStruct((r, d), BF16),
            jax.ShapeDtypeStruct((1, d), F32),
            jax.ShapeDtypeStruct((1, LANES), F32),
        ),
        scratch_shapes=[pltpu.VMEM((2, ts, d), F32), pltpu.VMEM((2, ts, d), F32), pltpu.VMEM((1, d), F32),
                        pltpu.SemaphoreType.DMA((2, 2))],
        compiler_params=_params("arbitrary"),
    )(ya, yc, w_out, x, target, g_final)


def _out_proj_bwd(dhb, w_out, ya, yc, bm):
    r, d = dhb.shape
    ka = ya.shape[1]
    n_mix = w_out.shape[0]
    last = r // bm - 1

    def body(dh_ref, w_ref, a_ref, c_ref, dcat_ref, dw_ref, acc_ref):
        @pl.when(pl.program_id(0) == 0)
        def _():
            acc_ref[...] = jnp.zeros_like(acc_ref)

        dh = dh_ref[...]
        dcat_ref[...] = _dot(dh, w_ref[...], _NT).astype(BF16)
        acc_ref[0:ka, :] += _dot(a_ref[...], dh, _TN)
        acc_ref[ka:, :] += _dot(c_ref[...], dh, _TN)

        @pl.when(pl.program_id(0) == last)
        def _():
            dw_ref[...] = acc_ref[...].astype(BF16)

    return pl.pallas_call(
        body,
        name="out_proj_bwd",
        grid=(r // bm,),
        in_specs=[pl.BlockSpec((bm, d), lambda i: (i, 0)), pl.BlockSpec(w_out.shape, lambda i: (0, 0)),
                  pl.BlockSpec((bm, ka), lambda i: (i, 0)), pl.BlockSpec((bm, yc.shape[1]), lambda i: (i, 0))],
        out_specs=(pl.BlockSpec((bm, n_mix), lambda i: (i, 0)),
                   pl.BlockSpec((n_mix, d), lambda i: (0, 0))),
        out_shape=(jax.ShapeDtypeStruct((r, n_mix), BF16),
                   jax.ShapeDtypeStruct((n_mix, d), BF16)),
        scratch_shapes=[pltpu.VMEM((n_mix, d), F32)],
        compiler_params=_params("arbitrary"),
    )(dhb, w_out, ya, yc)


def _attn_bwd(q, k, v, o, lse, dcat, p, g_attn):
    nb_seq, _, tp, _ = q.shape

    def body(q_ref, k_ref, v_ref, o_ref, lse_ref, dy_ref, z_ref, g_ref,
             dq_ref, dk_ref, dv_ref, dz_ref, dg_ref, dq_acc):
        @pl.when(pl.program_id(1) == 0)
        def _():
            dg_ref[...] = jnp.zeros_like(dg_ref)

        g = g_ref[...]
        z = z_ref[...].astype(F32)
        o = o_ref[0, 0]
        dy = dy_ref[...].astype(F32)
        sig = _sigmoid(z)
        ohat, r = _rms_stats(o)
        don = dy * (z * sig)
        dz_ref[...] = (dy * (ohat * g) * (sig * (1.0 + z * (1.0 - sig)))).astype(BF16)
        dg_ref[...] += jnp.sum(don * ohat, axis=0, keepdims=True)
        do = _rms_bwd(g * don, ohat, r)
        dvec = jnp.sum(do * o, axis=-1, keepdims=True)
        dob = do.astype(BF16)
        lse_col = lse_ref[0, 0, :, 0:1]
        dq_acc[...] = jnp.zeros_like(dq_acc)
        for k0 in range(0, tp, K_TILE):
            nk = min(K_TILE, tp - k0)
            nq = tp - k0
            qq = q_ref[0, 0, k0:, :]
            kk = k_ref[0, 0, k0:k0 + nk, :]
            causal = (lax.broadcasted_iota(jnp.int32, (nq, nk), 1) <= lax.broadcasted_iota(jnp.int32, (nq, nk), 0))
            pr = jnp.where(causal, jnp.exp2(_dot(qq, kk, _NT) - lse_col[k0:]), 0.0)
            dp = _dot(dob[k0:], v_ref[0, 0, k0:k0 + nk, :], _NT)
            ds = (pr * (dp - dvec[k0:])).astype(BF16)
            dv_ref[0, 0, k0:k0 + nk, :] = _dot(pr.astype(BF16), dob[k0:], _TN).astype(BF16)
            dk_ref[0, 0, k0:k0 + nk, :] = (_dot(ds, qq, _TN) * (ATTN_SCALE / Q_SCALE)).astype(BF16)
            dq_acc[k0:, :] += _dot(ds, kk)
        dq_ref[0, 0] = (dq_acc[...] * ATTN_SCALE).astype(BF16)

    qk = pl.BlockSpec((1, 1, tp, 2 * LANES), lambda h, b: (b, h, 0, 0))
    hv = pl.BlockSpec((1, 1, tp, D_V), lambda h, b: (b, h, 0, 0))
    col = pl.BlockSpec((tp, LANES), lambda h, b: (b, h))
    return pl.pallas_call(
        body,
        name="attn_bwd",
        grid=(N_HEADS, nb_seq),
        in_specs=[qk, qk, hv, hv, hv, col,
                  pl.BlockSpec((tp, LANES), lambda h, b: (b, GRP_A // LANES + h)),
                  pl.BlockSpec((1, LANES), lambda h, b: (0, h))],
        out_specs=(qk, qk, hv, col, pl.BlockSpec((1, LANES), lambda h, b: (0, h))),
        out_shape=(
            jax.ShapeDtypeStruct((nb_seq, N_HEADS, tp, 2 * LANES), BF16),
            jax.ShapeDtypeStruct((nb_seq, N_HEADS, tp, 2 * LANES), BF16),
            jax.ShapeDtypeStruct((nb_seq, N_HEADS, tp, D_V), BF16),
            jax.ShapeDtypeStruct((nb_seq * tp, N_HEADS * D_V), BF16),
            jax.ShapeDtypeStruct((1, N_HEADS * D_V), F32),
        ),
        scratch_shapes=[pltpu.VMEM((tp, 2 * LANES), F32)],
        compiler_params=_params("arbitrary", "arbitrary"),
    )(q, k, v, o, lse, dcat, p, g_attn)


def _qkv_bwd(p, dq, dk, dv, wq, wkv, gq, gkv, tables):
    nb_seq, _, tp, _ = dq.shape
    ht = tp // 2

    def body(pa_ref, dq_ref, dk_ref, dv_ref, wq_ref, wkv_ref, gq_ref, gkv_ref, cos_ref, sa_ref, sb_ref,
             dpa_ref, dwq_ref, dwkv_ref, dgq_ref, dgkv_ref):
        @pl.when(pl.program_id(0) == 0)
        def _():
            dwq_ref[...] = jnp.zeros_like(dwq_ref)
            dwkv_ref[...] = jnp.zeros_like(dwkv_ref)
            dgq_ref[...] = jnp.zeros_like(dgq_ref)
            dgkv_ref[...] = jnp.zeros_like(dgkv_ref)

        pa = pa_ref[...].astype(F32)
        gq, gkv = gq_ref[...], gkv_ref[...]
        cq_hat, rq = _rms_stats(pa[:, :Q_RANK])
        ckv_hat, rkv = _rms_stats(pa[:, Q_RANK:Q_RANK + KV_RANK])
        tabs = (cos_ref[...], sa_ref[...], sb_ref[...])

        pe = [dq_ref[0, h, :, D_NOPE:].astype(F32) for h in range(N_HEADS)]
        pairs = [_rope_t(pe[2 * i] + pltpu.roll(pe[2 * i + 1], D_ROPE, 1), *tabs).astype(BF16) for i in range(2)]
        dq_flat = jnp.concatenate([dq_ref[0, h, :, :D_NOPE] for h in range(N_HEADS)] + pairs, axis=1)
        dwq_ref[...] += _dot(dq_flat, (cq_hat * gq).astype(BF16), _TN)
        dcqn = _dot(dq_flat, wq_ref[...])
        dgq_ref[...] += jnp.sum(dcqn * cq_hat, axis=0, keepdims=True)
        dcq = _rms_bwd(gq * dcqn, cq_hat, rq)

        dkv_flat = jnp.concatenate([dk_ref[0, h, :, :D_NOPE] for h in range(N_HEADS)]
                                   + [dv_ref[0, h] for h in range(N_HEADS)], axis=1)
        dwkv_ref[...] += _dot((ckv_hat * gkv).astype(BF16), dkv_flat, _TN)
        dckvn = _dot(dkv_flat, wkv_ref[...], _NT)
        dgkv_ref[...] += jnp.sum(dckvn * ckv_hat, axis=0, keepdims=True)
        dckv = _rms_bwd(gkv * dckvn, ckv_hat, rkv)

        dk_pe = dk_ref[0, 0, :, D_NOPE:].astype(F32)
        for h in range(1, N_HEADS):
            dk_pe = dk_pe + dk_ref[0, h, :, D_NOPE:].astype(F32)
        dk_pe = jnp.where(lax.broadcasted_iota(jnp.int32, (ht, LANES), 1) < D_ROPE, dk_pe, 0.0)
        dpa_ref[...] = jnp.concatenate([dcq, dckv, _rope_t(dk_pe, *tabs)], axis=1).astype(BF16)

    full = lambda a: pl.BlockSpec(a.shape, lambda i: (0,) * a.ndim)
    tab = pl.BlockSpec((ht, LANES), lambda i: (i % 2, 0))
    qk = pl.BlockSpec((1, N_HEADS, ht, 2 * LANES), lambda i: (i // 2, 0, i % 2, 0))
    acc = lambda shape: pl.BlockSpec(shape, lambda i: (0, 0))
    return pl.pallas_call(
        body,
        name="qkv_bwd",
        grid=(2 * nb_seq,),
        in_specs=[pl.BlockSpec((ht, GRP_A), lambda i: (i, 0)), qk, qk,
                  pl.BlockSpec((1, N_HEADS, ht, D_V), lambda i: (i // 2, 0, i % 2, 0)),
                  full(wq), full(wkv), full(gq), full(gkv), tab, tab, tab],
        out_specs=(pl.BlockSpec((ht, GRP_A), lambda i: (i, 0)),
                   acc(wq.shape), acc(wkv.shape), acc((1, Q_RANK)), acc((1, KV_RANK))),
        out_shape=(
            jax.ShapeDtypeStruct((nb_seq * tp, GRP_A), BF16),
            jax.ShapeDtypeStruct(wq.shape, F32),
            jax.ShapeDtypeStruct(wkv.shape, F32),
            jax.ShapeDtypeStruct((1, Q_RANK), F32),
            jax.ShapeDtypeStruct((1, KV_RANK), F32),
        ),
        compiler_params=_params("arbitrary"),
    )(p, dq, dk, dv, wq, wkv, gq, gkv, *tables)


def _conv_bwd(p, dcat, conv_w, g_conv, nb_seq, tp):
    cols = CONV_WIDTH // LANES

    def body(b_ref, c_ref, h_ref, z_ref, dy_ref, w_ref, g_ref,
             db_ref, dc_ref, dh_ref, dz_ref, dw_ref, dg_ref):
        @pl.when(pl.program_id(1) == 0)
        def _():
            dw_ref[...] = jnp.zeros_like(dw_ref)
            dg_ref[...] = jnp.zeros_like(dg_ref)

        cb, c, h = b_ref[...].astype(F32), c_ref[...].astype(F32), h_ref[...].astype(F32)
        z, dy = z_ref[...].astype(F32), dy_ref[...].astype(F32)
        g = g_ref[...]
        w0, w1, w2 = w_ref[0:1, :], w_ref[1:2, :], w_ref[2:3, :]
        cc = c * h
        row = lax.broadcasted_iota(jnp.int32, (tp, LANES), 0)
        s1 = jnp.where(row >= 1, pltpu.roll(cc, 1, 0), 0.0)
        s2 = jnp.where(row >= 2, pltpu.roll(cc, 2, 0), 0.0)
        dwc = w0 * s2 + w1 * s1 + w2 * cc
        yc = cb * dwc
        r = lax.rsqrt(_group_mean(yc * yc) + EPS)
        ychat = yc * r
        sig = _sigmoid(z)
        dz_ref[...] = (dy * (ychat * g) * (sig * (1.0 + z * (1.0 - sig)))).astype(BF16)
        dyn = dy * (z * sig)
        dg_ref[...] += jnp.sum(dyn * ychat, axis=0, keepdims=True)
        gd = g * dyn
        dyc = r * (gd - ychat * _group_mean(gd * ychat))
        db_ref[...] = (dyc * dwc).astype(BF16)
        ddw = dyc * cb
        dw_ref[0:1, :] += jnp.sum(ddw * s2, axis=0, keepdims=True)
        dw_ref[1:2, :] += jnp.sum(ddw * s1, axis=0, keepdims=True)
        dw_ref[2:3, :] += jnp.sum(ddw * cc, axis=0, keepdims=True)
        u1 = jnp.where(row <= tp - 2, pltpu.roll(ddw, tp - 1, 0), 0.0)
        u2 = jnp.where(row <= tp - 3, pltpu.roll(ddw, tp - 2, 0), 0.0)
        dcc = w2 * ddw + w1 * u1 + w0 * u2
        dc_ref[...] = (dcc * h).astype(BF16)
        dh_ref[...] = (dcc * c).astype(BF16)

    col = pl.BlockSpec((tp, LANES), lambda t, b: (b, t))
    out = jax.ShapeDtypeStruct((nb_seq * tp, CONV_WIDTH), BF16)
    return pl.pallas_call(
        body,
        name="conv_bwd",
        grid=(cols, nb_seq),
        in_specs=_conv_specs(tp, lambda t, b, off: (b, off + t)) + [
            pl.BlockSpec((tp, LANES), lambda t, b: (b, N_HEADS * D_V // LANES + t)),
            pl.BlockSpec((8, LANES), lambda t, b: (0, t)),
            pl.BlockSpec((1, LANES), lambda t, b: (0, t))],
        out_specs=(col, col, col, col,
                   pl.BlockSpec((8, LANES), lambda t, b: (0, t)), pl.BlockSpec((1, LANES), lambda t, b: (0, t))),
        out_shape=(out, out, out, out,
                   jax.ShapeDtypeStruct((8, CONV_WIDTH), F32), jax.ShapeDtypeStruct((1, CONV_WIDTH), F32)),
        compiler_params=_params("arbitrary", "arbitrary"),
    )(p, p, p, p, dcat, conv_w, g_conv)


def _input_bwd(dps, w_in, x, meta, dh, norm_g, nt, send_in):
    nb_seq, s, d = x.shape
    r, kb = dps[0].shape
    ts = (s + LANES) // nt
    steps = nb_seq * nt
    n_dp = len(dps)
    in_slot = send_in.shape[1:]

    def body(*refs):
        dp_refs, w_ref, x_hbm, meta_ref, dh_ref, g_ref, pay_ref = refs[:n_dp], *refs[n_dp:n_dp + 6]
        o = n_dp + 6
        gx_hbm, dmeta_ref, dg_ref, r2_in = refs[o:o + 4]
        xbuf, gxbuf, tok_sems, own_in, r1_in, sum_in = refs[o + 4:o + 10]
        sems = refs[o + 10:]
        i = pl.program_id(0)
        b, k = i // nt, i % nt

        def plan():
            return _reduce_plan((pay_ref,), (own_in,), (r1_in,), (sum_in,), (r2_in,), *sems)

        @pl.when(i == 0)
        def _():
            dmeta_ref[...] = jnp.zeros_like(dmeta_ref)
            dg_ref[...] = jnp.zeros_like(dg_ref)
            plan()[0]()

        @pl.when(i == 1)
        def _():
            plan()[1]()

        def start(kk):
            if kk == 0:
                xbuf[0:PAD_FRONT, :] = jnp.zeros((PAD_FRONT, d), F32)
                xbuf[PAD_FRONT:LANES, :] = meta_ref[...]
            _token_copy(x_hbm, b, kk, ts, xbuf, tok_sems.at[0]).start()

        _for_tile(k, nt, start)
        du = _dot(dp_refs[0][...], w_ref[0:kb, :])
        for j in range(1, n_dp):
            du = du + _dot(dp_refs[j][...], w_ref[kb * j:kb * (j + 1), :])
        _for_tile(k, nt, lambda kk: _token_copy(x_hbm, b, kk, ts, xbuf, tok_sems.at[0]).wait())

        g = g_ref[...]
        hhat, rstd = _rms_stats(xbuf[...])
        dg_ref[...] += jnp.sum(du * hhat, axis=0, keepdims=True)
        res = _rms_bwd(g * du, hhat, rstd) + dh_ref[...].astype(F32)

        @pl.when(i > 0)
        def _():
            _for_tile(k, nt, lambda kk: _token_copy(gx_hbm, b, (kk - 1) % nt, ts, gxbuf, tok_sems.at[1], True).wait())

        gxbuf[...] = res

        @pl.when(k == 0)
        def _():
            dmeta_ref[...] += gxbuf[PAD_FRONT:LANES, :]

        _for_tile(k, nt, lambda kk: _token_copy(gx_hbm, b, kk, ts, gxbuf, tok_sems.at[1], True).start())

        @pl.when(i == steps - 1)
        def _():
            _token_copy(gx_hbm, b, nt - 1, ts, gxbuf, tok_sems.at[1], True).wait()
            plan()[2]()

    whole = lambda a: pl.BlockSpec(a.shape, lambda i: (0,) * a.ndim)
    hbm = pl.BlockSpec(memory_space=pl.ANY)
    return pl.pallas_call(
        body,
        name="input_bwd",
        grid=(steps,),
        in_specs=[pl.BlockSpec((ts, kb), lambda i: (i, 0)) for _ in dps]
        + [whole(w_in), hbm, whole(meta), pl.BlockSpec((ts, d), lambda i: (i, 0)), whole(norm_g), hbm],
        out_specs=(hbm, pl.BlockSpec((N_META, d), lambda i: (0, 0)), pl.BlockSpec((1, d), lambda i: (0, 0)), hbm),
        out_shape=(jax.ShapeDtypeStruct((nb_seq, s, d), F32),
                   jax.ShapeDtypeStruct((N_META, d), F32),
                   jax.ShapeDtypeStruct((1, d), F32),
                   jax.ShapeDtypeStruct((N_CHIPS,) + in_slot, BF16)),
        scratch_shapes=[pltpu.VMEM((ts, d), F32), pltpu.VMEM((ts, d), F32), pltpu.SemaphoreType.DMA((2,))]
        + _reduce_scratch([(in_slot, BF16)], [True]),
        compiler_params=_params("arbitrary"),
    )(*dps, w_in, x, meta, dh, norm_g, send_in)


def _in_proj_bwd_w(u, dps, bm, small_grads, send_out):
    r, d = u.shape
    kb = dps[0].shape[1]
    steps = r // bm
    n_dp, n_small = len(dps), len(small_grads)
    out_slot, small_slot = send_out.shape[1:], (SMALL_ROWS, LANES)

    def body(*refs):
        u_ref, dp_refs = refs[0], refs[1:1 + n_dp]
        small_refs = refs[1 + n_dp:1 + n_dp + n_small]
        o = 1 + n_dp + n_small
        pay_out, o_ref, r2_out, r2_small = refs[o:o + 4]
        acc_ref, ssmall, r1_out, sum_out, r1_small, sum_small = refs[o + 4:o + 10]
        sems = refs[o + 10:]
        i = pl.program_id(0)

        def plan():
            return _reduce_plan((pay_out, ssmall), (None, None), (r1_out, r1_small), (sum_out, sum_small),
                                (r2_out, r2_small), *sems)

        @pl.when(i == 0)
        def _():
            acc_ref[...] = jnp.zeros_like(acc_ref)
            _pack_small(ssmall, *small_refs)
            plan()[0]()

        @pl.when(i == 1)
        def _():
            plan()[1]()

        uu = u_ref[...]
        for j in range(n_dp):
            acc_ref[kb * j:kb * (j + 1), :] += _dot(dp_refs[j][...], uu, _TN)

        @pl.when(i == steps - 1)
        def _():
            for k in range(N_DEV):
                for s, e, c0 in _in_pieces(k):
                    o_ref[k, s:e, :] = acc_ref[c0:c0 + e - s, :].astype(BF16)
                o_ref[k, SHARD_IN:, :] = jnp.zeros((SHARD_IN_PAD - SHARD_IN, d), BF16)
            plan()[2]()

    whole = lambda a: pl.BlockSpec(a.shape, lambda i: (0,) * a.ndim)
    hbm = pl.BlockSpec(memory_space=pl.ANY)
    return pl.pallas_call(
        body,
        name="in_proj_bwd_w",
        grid=(steps,),
        in_specs=[pl.BlockSpec((bm, d), lambda i: (i, 0))]
        + [pl.BlockSpec((bm, kb), lambda i: (i, 0)) for _ in dps] + [whole(a) for a in small_grads]
        + [whole(send_out)],
        out_specs=(pl.BlockSpec((N_DEV, SHARD_IN_PAD, d), lambda i: (0, 0, 0)), hbm, hbm),
        out_shape=(jax.ShapeDtypeStruct((N_DEV, SHARD_IN_PAD, d), BF16),
                   jax.ShapeDtypeStruct((N_CHIPS,) + out_slot, BF16),
                   jax.ShapeDtypeStruct((N_CHIPS,) + small_slot, F32)),
        scratch_shapes=[pltpu.VMEM((kb * n_dp, d), F32), pltpu.VMEM((N_DEV,) + small_slot, F32)]
        + _reduce_scratch([(out_slot, BF16), (small_slot, F32)], [False, False]),
        compiler_params=_params("arbitrary"),
    )(u, *dps, *small_grads, send_out)


def _local_step(x, loss_target, u, p, meta_f, norm_g, w_in_p, q_norm_g, w_q_p, kv_norm_g, w_kv_p, conv_w_f,
                attn_out_g, conv_out_g, w_out_s, g_final):
    nb_seq, s, d = x.shape
    tp = s + LANES
    ht = tp // 2
    tables = _rope_tables(tp)

    q, k, v = _qkv_fwd(p, w_q_p, w_kv_p, q_norm_g, kv_norm_g, tables, nb_seq, tp)
    ya, o, lse, w_out_f = _attn_fwd(q, k, v, p, attn_out_g, w_out_s)
    yc = _conv_fwd(p, conv_w_f, conv_out_g, nb_seq, tp)
    dhb, d_final_g, loss_part = _out_proj_loss(ya, yc, w_out_f, x, loss_target, g_final, TOKEN_TILES)

    dcat, d_w_out = _out_proj_bwd(dhb, w_out_f, ya, yc, ht)
    send_out = d_w_out.reshape(N_DEV, SHARD_OUT, d)
    dq, dk, dv, dz_attn, d_attn_g = _attn_bwd(q, k, v, o, lse, dcat, p, attn_out_g)
    dpa, d_wq_p, d_wkv_p, d_gq, d_gkv = _qkv_bwd(p, dq, dk, dv, w_q_p, w_kv_p, q_norm_g, kv_norm_g, tables)
    d_b, d_c, d_h, dz_conv, d_conv_w, d_conv_g = _conv_bwd(p, dcat, conv_w_f, conv_out_g, nb_seq, tp)
    dps = (dpa, dz_attn, d_b, d_c, d_h, dz_conv)
    small = (d_wq_p, d_wkv_p, d_conv_w, d_final_g, d_gq, d_gkv, d_attn_g, d_conv_g, loss_part)
    send_in, r_out, r_small = _in_proj_bwd_w(u, dps, ht, small, send_out)
    grad_x, d_meta, d_norm_g, r_in = _input_bwd(dps, w_in_p, x, meta_f, dhb, norm_g, TOKEN_TILES, send_in)
    return grad_x, r_in, r_out, r_small, d_meta, d_norm_g


def kernel(x, meta_tokens, norm_g, w_in, q_norm_g, w_q_up, kv_norm_g, w_kv_up, conv_w, attn_out_g, conv_out_g, w_out, final_norm_g, loss_target, m_meta_tokens, m_norm_g, m_w_in, m_q_norm_g, m_w_q_up, m_kv_norm_g, m_w_kv_up, m_conv_w, m_attn_out_g, m_conv_out_g, m_w_out, m_final_norm_g, v_meta_tokens, v_norm_g, v_w_in, v_q_norm_g, v_w_q_up, v_kv_norm_g, v_w_kv_up, v_conv_w, v_attn_out_g, v_conv_out_g, v_w_out, v_final_norm_g):
    d = x.shape[-1]
    order = jnp.asarray(_tile_orders()[0])[2 * lax.axis_index("x") + lax.axis_index("y")]
    u, p, w_in_p, meta_f, w_q_p, w_kv_p, conv_w_f = _prep_in_proj(
        x, meta_tokens, norm_g, w_in[0].T, w_q_up[0].T, w_kv_up[0], conv_w.transpose(1, 0, 2), order)
    g_final = final_norm_g.reshape(1, d)
    grad_x, r_in, r_out, r_small, d_meta, d_norm_g = _local_step(
        x, loss_target, u, p, meta_f, norm_g, w_in_p, q_norm_g, w_q_p, kv_norm_g, w_kv_p, conv_w_f,
        attn_out_g, conv_out_g, w_out[0], g_final)

    flat = lambda a: a.reshape(a.shape[-2:]) if a.ndim == 3 else a.reshape(1, -1) if a.ndim == 1 else a
    transposed = ("w_in", "w_q_up")

    def to_kernel(n, a):
        if n == "conv_w":
            return a.transpose(1, 0, 2)
        return flat(a).T if n in transposed else flat(a)

    def from_kernel(n, a, shape):
        if n == "conv_w":
            return a.transpose(1, 0, 2)
        return (a.T if n in transposed else a).reshape(shape)
    params = {
        "meta_tokens": (meta_tokens, m_meta_tokens, v_meta_tokens),
        "norm_g": (norm_g, m_norm_g, v_norm_g),
        "w_in": (w_in, m_w_in, v_w_in),
        "q_norm_g": (q_norm_g, m_q_norm_g, v_q_norm_g),
        "w_q_up": (w_q_up, m_w_q_up, v_w_q_up),
        "kv_norm_g": (kv_norm_g, m_kv_norm_g, v_kv_norm_g),
        "w_kv_up": (w_kv_up, m_w_kv_up, v_w_kv_up),
        "conv_w": (conv_w, m_conv_w, v_conv_w),
        "attn_out_g": (attn_out_g, m_attn_out_g, v_attn_out_g),
        "conv_out_g": (conv_out_g, m_conv_out_g, v_conv_out_g),
        "w_out": (w_out, m_w_out, v_w_out),
        "final_norm_g": (final_norm_g, m_final_norm_g, v_final_norm_g),
    }
    grads, loss = _reduce_tail(r_in, r_out, r_small, d_meta, d_norm_g)
    updated = _adamw(grads, {n: tuple(to_kernel(n, a) for a in t) for n, t in params.items()})
    outs = [[from_kernel(n, updated[n][i], params[n][0].shape) for n, _ in PARAM_SHAPES] for i in range(4)]
    return (loss[0, 0], grad_x, *outs[0], *outs[1], *outs[2], *outs[3])
```

```python
import functools

import jax
import jax.numpy as jnp
import numpy as np
from jax import lax
from jax.experimental import pallas as pl
from jax.experimental.pallas import tpu as pltpu

F32 = jnp.float32
BF16 = jnp.bfloat16

N_META = 16
D_MODEL = 1024
N_HEADS = 4
D_NOPE = 128
D_ROPE = 64
D_V = 128
Q_RANK = 256
KV_RANK = 128
CONV_WIDTH = 512
CONV_GROUP = 64
ROPE_THETA = 10000.0
ATTN_SCALE = (D_NOPE + D_ROPE) ** -0.5
Q_SCALE = ATTN_SCALE * 1.4426950408889634
EPS = 1e-6
NEG_INF = -1e30

ADAM_LR = 0.001
ADAM_B1 = 0.9
ADAM_B2 = 0.999
ADAM_EPS = 1e-08
ADAM_WD = 0.01
ADAM_STEP = 10

LANES = 128
PAD_FRONT = LANES - N_META
K_TILE = 256
Q_TILE = 512
N_DEV = 8
VMEM_LIMIT = 56 * 1024 * 1024

IN_PAD = 3072
GRP_A = 512
N_A = Q_RANK + KV_RANK + D_ROPE
IN_PROJ = 3008
SHARD_IN = IN_PROJ // N_DEV
SHARD_IN_PAD = 384
SHARD_Q = 96
SHARD_KV = 128
SHARD_OUT = 128
SHARD_CONV = 64
SHARD_META = 128
Q_COLS = N_HEADS * (D_NOPE + D_ROPE)
KV_COLS = N_HEADS * (D_NOPE + D_V)

ROW_Q, ROW_KV, ROW_META, ROW_CONV = 0, 256, 384, 400
ROW_REPL = 408
ROW_NORM, ROW_FINAL, ROW_GQ, ROW_GKV, ROW_ATTN, ROW_CONVG, ROW_LOSS = 408, 416, 424, 426, 427, 431, 435
SMALL_ROWS = 440

PARAM_SHAPES = (
    ("meta_tokens", (N_META, SHARD_META)), ("norm_g", (1, D_MODEL)), ("w_in", (SHARD_IN, D_MODEL)),
    ("q_norm_g", (1, Q_RANK)), ("w_q_up", (SHARD_Q, Q_RANK)), ("kv_norm_g", (1, KV_RANK)),
    ("w_kv_up", (KV_RANK, SHARD_KV)), ("conv_w", (3, 1, SHARD_CONV)), ("attn_out_g", (1, CONV_WIDTH)),
    ("conv_out_g", (1, CONV_WIDTH)), ("w_out", (SHARD_OUT, D_MODEL)), ("final_norm_g", (1, D_MODEL)),
)


def _in_pieces(k):
    lo, hi = SHARD_IN * k, SHARD_IN * (k + 1)
    out = []
    if lo < N_A:
        out.append((0, min(hi, N_A) - lo, lo))
    if hi > N_A:
        s = max(lo, N_A)
        out.append((s - lo, hi - lo, s + GRP_A - N_A))
    return out


P_TILE = 256


def _tile_pieces(j):
    lo, hi = P_TILE * j, P_TILE * (j + 1)
    out = []
    for k in range(N_DEV):
        for s, e, d in _in_pieces(k):
            a, b = max(d, lo), min(d + e - s, hi)
            if a < b:
                out.append((k, s + a - d, s + b - d, a - lo))
    return out


def _tile_orders():
    n_tiles = IN_PAD // P_TILE
    sources = [{k for k, _, _, _ in _tile_pieces(j)} for j in range(n_tiles)]
    rows, n_early, n_free = [], n_tiles, n_tiles
    for chip in range(N_CHIPS):
        own = {2 * chip, 2 * chip + 1}
        diagonal = {2 * (N_CHIPS - 1 - chip), 2 * (N_CHIPS - 1 - chip) + 1}
        early = [j for j in range(n_tiles) if sources[j] <= own]
        late = [j for j in range(n_tiles) if sources[j] & diagonal]
        mid = [j for j in range(n_tiles) if j not in early and j not in late]
        rows.append(early + mid + late)
        n_early, n_free = min(n_early, len(early)), min(n_free, len(early) + len(mid))
    return np.asarray(rows, np.int32), n_early, n_free


def _p_spec(rows, index):
    per = P_TILE // LANES

    def index_map(*g):
        r, c = index(*g)
        return c // per, r, c % per
    return pl.BlockSpec((None, rows, LANES), index_map)


def _q_pieces(k):
    lo, hi = SHARD_Q * k, SHARD_Q * (k + 1)
    out = []
    for h in range(N_HEADS):
        base = (D_NOPE + D_ROPE) * h
        s, e = max(lo, base), min(hi, base + D_NOPE)
        if s < e:
            out.append((s - lo, e - lo, D_NOPE * h + s - base))
        s, e = max(lo, base + D_NOPE), min(hi, base + D_NOPE + D_ROPE)
        if s < e:
            out.append((s - lo, e - lo, N_HEADS * D_NOPE + D_ROPE * h + s - base - D_NOPE))
    return out


def _kv_dst(k):
    return D_NOPE * (k // 2) + (N_HEADS * D_NOPE if k % 2 else 0)


def _params(*sem):
    return pltpu.CompilerParams(dimension_semantics=sem, vmem_limit_bytes=VMEM_LIMIT)


def _rms_stats(x):
    r = lax.rsqrt(jnp.mean(x * x, axis=-1, keepdims=True) + EPS)
    return x * r, r


def _rms_bwd(gdy, xhat, r):
    return r * (gdy - xhat * jnp.mean(gdy * xhat, axis=-1, keepdims=True))


def _sigmoid(z):
    return 1.0 / (1.0 + jnp.exp(-z))


def _group_mean(x):
    i0 = lax.broadcasted_iota(jnp.int32, (LANES, LANES), 0) // CONV_GROUP
    i1 = lax.broadcasted_iota(jnp.int32, (LANES, LANES), 1) // CONV_GROUP
    m = jnp.where(i0 == i1, 1.0 / CONV_GROUP, 0.0).astype(BF16)
    hi = x.astype(BF16)
    lo = (x - hi.astype(F32)).astype(BF16)
    return jnp.dot(hi, m, preferred_element_type=F32) + jnp.dot(lo, m, preferred_element_type=F32)


_NT = (((1,), (1,)), ((), ()))
_TN = (((0,), (0,)), ((), ()))


def _dot(a, b, dims=None):
    if dims is None:
        return jnp.dot(a, b, preferred_element_type=F32)
    return lax.dot_general(a, b, dims, preferred_element_type=F32)


def _device_position():
    x, y, c = lax.axis_index("x"), lax.axis_index("y"), lax.axis_index("c")
    return x, y, c, 4 * x + 2 * y + c


def _gather_plan(srcs, slots, send_sems, recv_sems, local_sems):
    x, y, c, _ = _device_position()
    me, sibling = (x, y, c), (x, y, 1 - c)
    flip = lambda v, on: v + on - 2 * v * on
    near = (flip(x, 1 - c), flip(y, c))
    far = (flip(x, c), flip(y, 1 - c))
    diag = (1 - x, 1 - y)
    n = len(srcs)

    def slot(a, px, py, pc):
        return slots[a].at[4 * px + 2 * py + pc]

    def copy(a, k, block, to, own=False):
        return pltpu.make_async_remote_copy(
            src_ref=srcs[a] if own else slot(a, *block),
            dst_ref=slot(a, *block),
            send_sem=send_sems.at[7 * a + k],
            recv_sem=recv_sems.at[7 * a + k],
            device_id=to,
            device_id_type=pl.DeviceIdType.MESH,
        )

    def local(a):
        return pltpu.make_async_copy(srcs[a], slot(a, *me), local_sems.at[a])

    sent = [(me, sibling), (me, (*near, c)), (me, (*far, c)), ((*near, c), (*far, c)),
            ((*near, c), sibling), ((*far, c), sibling), ((*diag, c), sibling)]
    landed = [sibling, (*near, c), (*far, c), (*diag, c), (*far, 1 - c), (*near, 1 - c), (*diag, 1 - c)]

    def send(a, k):
        return copy(a, k, *sent[k], own=k < 3)

    def arrival(a, k):
        return copy(a, k, landed[k], me)

    def start():
        for a in range(n):
            local(a).start()
            for k in range(3):
                send(a, k).start()

    def own():
        for a in range(n):
            local(a).wait()
            arrival(a, 0).wait_recv()

    def mid():
        for a in range(n):
            arrival(a, 1).wait_recv()
            send(a, 3).start()
            send(a, 4).start()
        for a in range(n):
            arrival(a, 2).wait_recv()
            send(a, 5).start()
        for a in range(n):
            for k in (4, 5):
                arrival(a, k).wait_recv()

    def late():
        for a in range(n):
            arrival(a, 3).wait_recv()
            send(a, 6).start()
        for a in range(n):
            arrival(a, 6).wait_recv()

    def finish():
        for a in range(n):
            for k in range(7):
                send(a, k).wait_send()

    return start, own, mid, late, finish


def _adam_update(g, w, m, v):
    m_new = ADAM_B1 * m + (1.0 - ADAM_B1) * g
    v_new = ADAM_B2 * v + (1.0 - ADAM_B2) * (g * g)
    m_hat = m_new / (1.0 - ADAM_B1 ** ADAM_STEP)
    v_hat = v_new / (1.0 - ADAM_B2 ** ADAM_STEP)
    return -ADAM_LR * (m_hat / (jnp.sqrt(v_hat) + ADAM_EPS) + ADAM_WD * w), m_new, v_new


def _adamw(grads, params):
    names = [n for n, _ in PARAM_SHAPES]
    n_p = len(names)

    def body(*refs):
        for i in range(n_p):
            g = refs[i][...]
            w, m, v = (refs[n_p + 3 * i + j][...] for j in range(3))
            delta, m_new, v_new = _adam_update(g, w, m, v)
            for j, val in enumerate((g, delta, m_new, v_new)):
                refs[4 * n_p + 4 * i + j][...] = val

    vm = pl.BlockSpec(memory_space=pltpu.VMEM)
    out_shape = []
    for _, shape in PARAM_SHAPES:
        out_shape += [jax.ShapeDtypeStruct(shape, F32)] * 4
    outs = pl.pallas_call(
        body,
        name="adamw",
        out_shape=tuple(out_shape),
        in_specs=[vm] * (4 * n_p),
        out_specs=(vm,) * (4 * n_p),
        compiler_params=pltpu.CompilerParams(vmem_limit_bytes=VMEM_LIMIT),
    )(*[grads[n] for n in names], *[a for n in names for a in params[n]])
    return {n: outs[4 * i:4 * i + 4] for i, n in enumerate(names)}


N_CHIPS = 4


def _reduce_plan(pays, owns, r1s, sums, r2s, send1, recv1, send2, recv2, local_sems):
    x, y, c, _ = _device_position()
    sibling = (x, y, 1 - c)
    chips = [((1 - x if rj & 2 else x), (1 - y if rj & 1 else y)) for rj in range(N_CHIPS)]
    n = len(pays)

    def slot_of(rj, core):
        return 4 * chips[rj][0] + 2 * chips[rj][1] + core

    def to_sibling(a, rj):
        return pltpu.make_async_remote_copy(
            src_ref=pays[a].at[slot_of(rj, 1 - c)], dst_ref=r1s[a].at[rj],
            send_sem=send1.at[N_CHIPS * a + rj], recv_sem=recv1.at[N_CHIPS * a + rj],
            device_id=sibling, device_id_type=pl.DeviceIdType.MESH)

    def load_own(a, rj):
        return pltpu.make_async_copy(pays[a].at[slot_of(rj, c)], owns[a].at[rj], local_sems.at[2 * N_CHIPS * a + rj])

    def to_chip(a, rj):
        return pltpu.make_async_remote_copy(
            src_ref=sums[a].at[rj], dst_ref=r2s[a].at[rj],
            send_sem=send2.at[N_CHIPS * a + rj], recv_sem=recv2.at[N_CHIPS * a + rj],
            device_id=(*chips[rj], c), device_id_type=pl.DeviceIdType.MESH)

    def keep(a):
        return pltpu.make_async_copy(sums[a].at[0], r2s[a].at[0], local_sems.at[2 * N_CHIPS * a + N_CHIPS])

    def start():
        for a in range(n):
            for rj in range(N_CHIPS):
                to_sibling(a, rj).start()
                if owns[a] is not None:
                    load_own(a, rj).start()

    def combine():
        for a in range(n):
            for rj in range(N_CHIPS):
                to_sibling(a, rj).wait_recv()
                if owns[a] is not None:
                    load_own(a, rj).wait()
                    mine = owns[a][rj]
                else:
                    mine = pays[a][slot_of(rj, c)]
                sums[a][rj] = (mine.astype(F32) + r1s[a][rj].astype(F32)).astype(sums[a].dtype)
            keep(a).start()
            for rj in range(1, N_CHIPS):
                to_chip(a, rj).start()

    def finish():
        for a in range(n):
            for rj in range(1, N_CHIPS):
                to_chip(a, rj).wait_recv()
            for rj in range(N_CHIPS):
                to_sibling(a, rj).wait_send()
            for rj in range(1, N_CHIPS):
                to_chip(a, rj).wait_send()
            keep(a).wait()

    return start, combine, finish


def _reduce_scratch(shapes_dtypes, own_flags):
    out = []
    for (shape, dtype), own in zip(shapes_dtypes, own_flags):
        if own:
            out.append(pltpu.VMEM((N_CHIPS,) + shape, dtype))
        out += [pltpu.VMEM((N_CHIPS,) + shape, dtype), pltpu.VMEM((N_CHIPS,) + shape, dtype)]
    n = len(shapes_dtypes)
    out += [pltpu.SemaphoreType.DMA((N_CHIPS * n,))] * 4 + [pltpu.SemaphoreType.DMA((2 * N_CHIPS * n,))]
    return out


def _pack_small(ssmall, dwq, dwkv, dconv, dfinal, dgq, dgkv, dattn, dconvg, loss_part):
    ssmall[...] = jnp.zeros_like(ssmall)
    rep = ssmall.at[0]
    for i in range(D_MODEL // LANES):
        rep[ROW_FINAL + i:ROW_FINAL + i + 1, :] = dfinal[:, LANES * i:LANES * (i + 1)]
    for i in range(Q_RANK // LANES):
        rep[ROW_GQ + i:ROW_GQ + i + 1, :] = dgq[:, LANES * i:LANES * (i + 1)]
    rep[ROW_GKV:ROW_GKV + 1, :] = dgkv[...]
    for i in range(CONV_WIDTH // LANES):
        rep[ROW_ATTN + i:ROW_ATTN + i + 1, :] = dattn[:, LANES * i:LANES * (i + 1)]
        rep[ROW_CONVG + i:ROW_CONVG + i + 1, :] = dconvg[:, LANES * i:LANES * (i + 1)]
    rep[ROW_LOSS:ROW_LOSS + 1, :] = loss_part[...]
    for k in range(N_DEV):
        if k:
            ssmall[k, ROW_REPL:, :] = ssmall[0, ROW_REPL:, :]
        for s, e, d in _q_pieces(k):
            for i in range(Q_RANK // LANES):
                ssmall[k, ROW_Q + SHARD_Q * i + s:ROW_Q + SHARD_Q * i + e, :] = dwq[d:d + e - s, LANES * i:LANES * (i + 1)]
        ssmall[k, ROW_KV:ROW_KV + KV_RANK, :] = dwkv[:, _kv_dst(k):_kv_dst(k) + SHARD_KV]
        ssmall[k, ROW_CONV:ROW_CONV + 3, 0:SHARD_CONV] = dconv[0:3, SHARD_CONV * k:SHARD_CONV * (k + 1)]


TOKEN_TILES = 4
TAIL_ROWS = N_META + D_MODEL // LANES


def _reduce_tail(r_in, r_out, r_small, d_meta, d_norm):
    n_p = len(PARAM_SHAPES)
    names = [n for n, _ in PARAM_SHAPES]

    def body(*refs):
        rin, rout, rsmall, dmeta, dnorm = refs[:5]
        g_out = {n: refs[5 + i] for i, n in enumerate(names)}
        loss_out = refs[5 + n_p]
        stail, rtail, gsum, gtail, send_sems, recv_sems = refs[6 + n_p:]
        x, y, c, me = _device_position()
        my_chip = 2 * x + y

        for k in range(N_DEV):
            stail[k, 0:N_META, :] = dmeta[:, SHARD_META * k:SHARD_META * (k + 1)]
            for i in range(D_MODEL // LANES):
                stail[k, N_META + i:N_META + i + 1, :] = dnorm[:, LANES * i:LANES * (i + 1)]
        copies = []
        for r in range(1, N_DEV):
            peer = (1 - x if r & 4 else x, 1 - y if r & 2 else y, 1 - c if r & 1 else c)
            copies.append(pltpu.make_async_remote_copy(
                src_ref=stail.at[4 * peer[0] + 2 * peer[1] + peer[2]],
                dst_ref=rtail.at[r],
                send_sem=send_sems.at[r - 1],
                recv_sem=recv_sems.at[r - 1],
                device_id=peer,
                device_id_type=pl.DeviceIdType.MESH,
            ))
        for cp in copies:
            cp.start()
        rtail[0] = stail[me]

        g = rin[my_chip].astype(F32)
        for ch in range(1, N_CHIPS):
            g = g + rin[ch ^ my_chip].astype(F32)
        g_out["w_in"][...] = g[:SHARD_IN, :]

        g = rout[my_chip].astype(F32)
        gs = rsmall[my_chip]
        for ch in range(1, N_CHIPS):
            g = g + rout[ch ^ my_chip].astype(F32)
            gs = gs + rsmall[ch ^ my_chip]
        g_out["w_out"][...] = g
        gsum[...] = gs
        for i in range(Q_RANK // LANES):
            g_out["w_q_up"][:, LANES * i:LANES * (i + 1)] = gsum[ROW_Q + SHARD_Q * i:ROW_Q + SHARD_Q * (i + 1), :]
        g_out["w_kv_up"][...] = gsum[ROW_KV:ROW_KV + KV_RANK, :]
        for i in range(3):
            g_out["conv_w"][i] = gsum[ROW_CONV + i:ROW_CONV + i + 1, 0:SHARD_CONV]
        for name, row, width in (("final_norm_g", ROW_FINAL, D_MODEL), ("q_norm_g", ROW_GQ, Q_RANK),
                                 ("kv_norm_g", ROW_GKV, KV_RANK), ("attn_out_g", ROW_ATTN, CONV_WIDTH),
                                 ("conv_out_g", ROW_CONVG, CONV_WIDTH)):
            for i in range(width // LANES):
                g_out[name][:, LANES * i:LANES * (i + 1)] = gsum[row + i:row + i + 1, :]
        loss_out[...] = gsum[ROW_LOSS:ROW_LOSS + 1, :]

        for cp in copies:
            cp.wait_recv()
        gt = rtail[me]
        for d in range(1, N_DEV):
            gt = gt + rtail[d ^ me]
        gtail[...] = gt
        g_out["meta_tokens"][...] = gtail[0:N_META, :]
        for i in range(D_MODEL // LANES):
            g_out["norm_g"][:, LANES * i:LANES * (i + 1)] = gtail[N_META + i:N_META + i + 1, :]
        for cp in copies:
            cp.wait_send()

    vm = pl.BlockSpec(memory_space=pltpu.VMEM)
    out_shape = [jax.ShapeDtypeStruct(shape, F32) for _, shape in PARAM_SHAPES]
    out_shape.append(jax.ShapeDtypeStruct((1, LANES), F32))
    outs = pl.pallas_call(
        body,
        name="reduce_tail",
        out_shape=tuple(out_shape),
        in_specs=[vm] * 5,
        out_specs=(vm,) * len(out_shape),
        scratch_shapes=[
            pltpu.VMEM((N_DEV, TAIL_ROWS, LANES), F32),
            pltpu.VMEM((N_DEV, TAIL_ROWS, LANES), F32),
            pltpu.VMEM((SMALL_ROWS, LANES), F32),
            pltpu.VMEM((TAIL_ROWS, LANES), F32),
            pltpu.SemaphoreType.DMA((N_DEV - 1,)),
            pltpu.SemaphoreType.DMA((N_DEV - 1,)),
        ],
        compiler_params=pltpu.CompilerParams(vmem_limit_bytes=VMEM_LIMIT),
    )(r_in, r_out, r_small, d_meta, d_norm)
    return {n: outs[i] for i, n in enumerate(names)}, outs[-1]


def _prep_in_proj(x, meta, norm_g, w_in_t, w_q, w_kv, w_out, conv_w, order):
    nb_seq, s, d = x.shape
    tp = s + LANES
    m = nb_seq * tp
    ts = s // TOKEN_TILES
    n_real = nb_seq * TOKEN_TILES
    n_norm = n_real + 1
    n_tiles = IN_PAD // P_TILE
    _, n_early, n_free = _tile_orders()
    steps = n_norm + n_tiles
    dot_rows = m // 4
    qkv_shape = (SHARD_Q + KV_RANK, Q_RANK)

    def tile(t):
        t = jnp.minimum(t, n_real - 1)
        return t // TOKEN_TILES, t % TOKEN_TILES

    def column_tile(t, order_ref):
        return order_ref[jnp.maximum(t - n_norm, 0)]

    def body(order_ref, x_ref, meta_ref, g_ref, win_ref, wq_ref, wkv_ref, wout_ref, conv_ref,
             u_hbm, p_hbm, w_in_p, meta_f, w_q_p, w_kv_p, w_out_f, conv_f,
             u_all, sbig, gbig, smeta, gmeta, sqkv, sout, sconv, gqkv, gout, gconv,
             send_in, recv_in, local_in, send_meta, recv_meta, send_rest, recv_rest, local_rest, u_sem,
             pbuf, p_sems):
        t = pl.program_id(0)
        px, py, pc, me = _device_position()
        u_copy = pltpu.make_async_copy(u_all, u_hbm, u_sem)

        def p_copy(slot, i):
            return pltpu.make_async_copy(pbuf.at[slot], p_hbm.at[order_ref[i]], p_sems.at[slot])

        def plan_in():
            return _gather_plan((sbig,), (gbig,), send_in, recv_in, local_in)

        def plan_rest():
            return _gather_plan((sqkv, sout, sconv), (gqkv, gout, gconv), send_rest, recv_rest, local_rest)

        def meta_copies():
            out = []
            for r in range(1, N_DEV):
                peer = (1 - px if r & 4 else px, 1 - py if r & 2 else py, 1 - pc if r & 1 else pc)
                out.append(pltpu.make_async_remote_copy(
                    src_ref=smeta,
                    dst_ref=gmeta.at[r],
                    send_sem=send_meta.at[r - 1],
                    recv_sem=recv_meta.at[r - 1],
                    device_id=peer,
                    device_id_type=pl.DeviceIdType.MESH,
                ))
            return out

        @pl.when(t == 0)
        def _():
            sbig[0:SHARD_IN, :] = win_ref[...].astype(BF16)
            sbig[SHARD_IN:, :] = jnp.zeros((SHARD_IN_PAD - SHARD_IN, d), BF16)
            smeta[...] = meta_ref[...]
            plan_in()[0]()
            for cp in meta_copies():
                cp.start()
            sqkv[...] = jnp.zeros_like(sqkv)
            sqkv[0:SHARD_Q, :] = wq_ref[...].astype(BF16)
            sqkv[SHARD_Q:, 0:SHARD_KV] = wkv_ref[...].astype(BF16)
            sout[...] = wout_ref[...].astype(BF16)
            sconv[...] = jnp.zeros_like(sconv)
            for i in range(3):
                sconv[i:i + 1, 0:SHARD_CONV] = conv_ref[i]

        def norm(h):
            hhat, _ = _rms_stats(h)
            return (hhat * g_ref[...]).astype(BF16)

        @pl.when(t < n_real)
        def _():
            b, k = tile(t)
            row0 = pl.multiple_of(b * tp + LANES + k * ts, 16)
            u_all[pl.ds(row0, ts), :] = norm(x_ref[0])

        @pl.when(t == n_real)
        def _():
            for cp in meta_copies():
                cp.wait_recv()
            gmeta[0] = smeta[...]
            for k in range(N_DEV):
                meta_f[:, SHARD_META * k:SHARD_META * (k + 1)] = gmeta[k ^ me]
            um = norm(meta_f[...])
            for b in range(nb_seq):
                u_all[b * tp:b * tp + PAD_FRONT, :] = jnp.zeros((PAD_FRONT, d), BF16)
                u_all[b * tp + PAD_FRONT:b * tp + LANES, :] = um
            u_copy.start()

        @pl.when(t == n_norm)
        def _():
            plan_in()[1]()

        @pl.when(t == n_norm + n_early)
        def _():
            plan_in()[2]()
            plan_rest()[0]()

        @pl.when(t == n_norm + (n_early + n_free) // 2)
        def _():
            plan_rest()[1]()
            plan_rest()[2]()

        @pl.when(t == n_norm + n_free)
        def _():
            plan_in()[3]()

        @pl.when(t == n_norm + n_free + 2)
        def _():
            plan_rest()[3]()

        @pl.when(t >= n_norm)
        def _():
            j = column_tile(t, order_ref)
            for jj in range(n_tiles):
                @pl.when(j == jj)
                def _(jj=jj):
                    if jj == N_A // P_TILE:
                        w_in_p[N_A % P_TILE:, :] = jnp.zeros((P_TILE - N_A % P_TILE, d), BF16)
                    for k, s0, e0, d0 in _tile_pieces(jj):
                        w_in_p[d0:d0 + e0 - s0, :] = gbig[k, s0:e0, :]
            i = t - n_norm
            slot = i % 2

            @pl.when(i >= 2)
            def _():
                p_copy(slot, i - 2).wait()
            for r in range(m // dot_rows):
                rows = slice(dot_rows * r, dot_rows * (r + 1))
                pbuf[slot, rows, :] = _dot(u_all[rows, :], w_in_p[...], _NT).astype(BF16)
            p_copy(slot, i).start()

        @pl.when(t == steps - 1)
        def _():
            plan_in()[4]()
            plan_rest()[4]()
            for cp in meta_copies():
                cp.wait_send()
            u_copy.wait()
            for i in (n_tiles - 2, n_tiles - 1):
                p_copy(i % 2, i).wait()
            conv_f[...] = jnp.zeros_like(conv_f)
            for k in range(N_DEV):
                for s0, e0, d0 in _q_pieces(k):
                    w_q_p[d0:d0 + e0 - s0, :] = gqkv[k, s0:e0, :]
                w_kv_p[:, _kv_dst(k):_kv_dst(k) + SHARD_KV] = gqkv[k, SHARD_Q:, 0:SHARD_KV]
                w_out_f[SHARD_OUT * k:SHARD_OUT * (k + 1), :] = gout[k]
                conv_f[0:3, SHARD_CONV * k:SHARD_CONV * (k + 1)] = gconv[k, 0:3, 0:SHARD_CONV]

    whole = lambda shape: pl.BlockSpec(shape, lambda t, o: (0,) * len(shape))
    return pl.pallas_call(
        body,
        name="prep_in_proj_gather",
        grid_spec=pltpu.PrefetchScalarGridSpec(
            num_scalar_prefetch=1,
            grid=(steps,),
            in_specs=[
                pl.BlockSpec((1, ts, d), lambda t, o: (*tile(t), 0)),
                whole(meta.shape), whole(norm_g.shape), whole(w_in_t.shape),
                whole(w_q.shape), whole(w_kv.shape), whole(w_out.shape), whole(conv_w.shape),
            ],
            out_specs=(pl.BlockSpec(memory_space=pl.ANY),
                       pl.BlockSpec(memory_space=pl.ANY),
                       pl.BlockSpec((P_TILE, d), lambda t, o: (column_tile(t, o), 0)),
                       whole((N_META, d)),
                       whole((Q_COLS, Q_RANK)), whole((KV_RANK, KV_COLS)), whole((D_MODEL, D_MODEL)),
                       whole((8, CONV_WIDTH))),
            scratch_shapes=[
                pltpu.VMEM((m, d), BF16),
                pltpu.VMEM((SHARD_IN_PAD, d), BF16),
                pltpu.VMEM((N_DEV, SHARD_IN_PAD, d), BF16),
                pltpu.VMEM((N_META, SHARD_META), F32),
                pltpu.VMEM((N_DEV, N_META, SHARD_META), F32),
                pltpu.VMEM(qkv_shape, BF16),
                pltpu.VMEM((SHARD_OUT, D_MODEL), BF16),
                pltpu.VMEM((8, LANES), F32),
                pltpu.VMEM((N_DEV,) + qkv_shape, BF16),
                pltpu.VMEM((N_DEV, SHARD_OUT, D_MODEL), BF16),
                pltpu.VMEM((N_DEV, 8, LANES), F32),
                pltpu.SemaphoreType.DMA((7,)),
                pltpu.SemaphoreType.DMA((7,)),
                pltpu.SemaphoreType.DMA((1,)),
                pltpu.SemaphoreType.DMA((N_DEV - 1,)),
                pltpu.SemaphoreType.DMA((N_DEV - 1,)),
                pltpu.SemaphoreType.DMA((21,)),
                pltpu.SemaphoreType.DMA((21,)),
                pltpu.SemaphoreType.DMA((3,)),
                pltpu.SemaphoreType.DMA,
                pltpu.VMEM((2, m, P_TILE), BF16),
                pltpu.SemaphoreType.DMA((2,)),
            ],
        ),
        out_shape=(jax.ShapeDtypeStruct((m, d), BF16),
                   jax.ShapeDtypeStruct((n_tiles, m, P_TILE), BF16),
                   jax.ShapeDtypeStruct((IN_PAD, d), BF16),
                   jax.ShapeDtypeStruct((N_META, d), F32),
                   jax.ShapeDtypeStruct((Q_COLS, Q_RANK), BF16),
                   jax.ShapeDtypeStruct((KV_RANK, KV_COLS), BF16),
                   jax.ShapeDtypeStruct((D_MODEL, D_MODEL), BF16),
                   jax.ShapeDtypeStruct((8, CONV_WIDTH), F32)),
        compiler_params=_params("arbitrary"),
    )(order, x, meta, norm_g, w_in_t, w_q, w_kv, w_out, conv_w)


def _rope_tables(tp):
    half = D_ROPE // 2
    inv_freq = (1.0 / (ROPE_THETA ** (np.arange(half, dtype=np.float32) / half))).astype(np.float32)
    pos = (np.arange(tp) - PAD_FRONT).astype(np.float32)
    ang = pos[:, None] * inv_freq[None, :]
    cos = np.tile(np.cos(ang), (1, LANES // half))
    sin = np.tile(np.sin(ang), (1, LANES // half))
    first = (np.arange(LANES) % D_ROPE) < half
    zero = np.float32(0.0)
    return tuple(jnp.asarray(t, F32) for t in (cos, np.where(first, -sin, zero), np.where(first, zero, sin)))


def _rope(t, cos, sa, sb):
    return t * cos + pltpu.roll(t, LANES - D_ROPE // 2, 1) * sa + pltpu.roll(t, D_ROPE // 2, 1) * sb


def _rope_t(t, cos, sa, sb):
    return t * cos + pltpu.roll(t * sa, D_ROPE // 2, 1) + pltpu.roll(t * sb, LANES - D_ROPE // 2, 1)


def _qkv_fwd(p, wq, wkv, gq, gkv, tables, nb_seq, tp):
    ht = tp // 2

    def body(pa_ref, wq_ref, wkv_ref, gq_ref, gkv_ref, cos_ref, sa_ref, sb_ref, q_ref, k_ref, v_ref):
        pa = jnp.concatenate([pa_ref[i] for i in range(GRP_A // P_TILE)], axis=1).astype(F32)
        cq_hat, _ = _rms_stats(pa[:, :Q_RANK])
        ckv_hat, _ = _rms_stats(pa[:, Q_RANK:Q_RANK + KV_RANK])
        q = _dot((cq_hat * gq_ref[...]).astype(BF16), wq_ref[...], _NT) * Q_SCALE
        kv = _dot((ckv_hat * gkv_ref[...]).astype(BF16), wkv_ref[...])
        tabs = (cos_ref[...], sa_ref[...], sb_ref[...])
        lane = lax.broadcasted_iota(jnp.int32, (ht, LANES), 1)
        low = lane < D_ROPE
        mark = lane == D_ROPE
        row = (pl.program_id(0) % 2) * ht + lax.broadcasted_iota(jnp.int32, (ht, LANES), 0)
        k_pe = jnp.where(mark & (row < PAD_FRONT), NEG_INF, _rope(pa[:, Q_RANK + KV_RANK:], *tabs))
        one = jnp.where(mark & (row >= PAD_FRONT), 1.0, 0.0)
        pairs = [_rope(q[:, N_HEADS * D_NOPE + LANES * i:N_HEADS * D_NOPE + LANES * (i + 1)], *tabs) for i in range(2)]
        for h in range(N_HEADS):
            pair = pairs[h // 2]
            if h % 2:
                pair = pltpu.roll(pair, D_ROPE, 1)
            pe = jnp.where(low, pair, one)
            q_ref[0, h] = jnp.concatenate([q[:, D_NOPE * h:D_NOPE * (h + 1)], pe], axis=1).astype(BF16)
            k_ref[0, h] = jnp.concatenate([kv[:, D_NOPE * h:D_NOPE * (h + 1)], k_pe], axis=1).astype(BF16)
            v_ref[0, h] = kv[:, N_HEADS * D_NOPE + D_V * h:N_HEADS * D_NOPE + D_V * (h + 1)].astype(BF16)

    full = lambda a: pl.BlockSpec(a.shape, lambda i: (0,) * a.ndim)
    tab = pl.BlockSpec((ht, LANES), lambda i: (i % 2, 0))
    qk = pl.BlockSpec((1, N_HEADS, ht, 2 * LANES), lambda i: (i // 2, 0, i % 2, 0))
    return pl.pallas_call(
        body,
        name="qkv_fwd",
        grid=(2 * nb_seq,),
        in_specs=[pl.BlockSpec((GRP_A // P_TILE, ht, P_TILE), lambda i: (0, i, 0)),
                  full(wq), full(wkv), full(gq), full(gkv), tab, tab, tab],
        out_specs=(qk, qk, pl.BlockSpec((1, N_HEADS, ht, D_V), lambda i: (i // 2, 0, i % 2, 0))),
        out_shape=(
            jax.ShapeDtypeStruct((nb_seq, N_HEADS, tp, 2 * LANES), BF16),
            jax.ShapeDtypeStruct((nb_seq, N_HEADS, tp, 2 * LANES), BF16),
            jax.ShapeDtypeStruct((nb_seq, N_HEADS, tp, D_V), BF16),
        ),
        compiler_params=_params("parallel"),
    )(p, wq, wkv, gq, gkv, *tables)


def _attn_fwd(q, k, v, p, g_attn):
    nb_seq, _, tp, _ = q.shape

    def body(q_ref, k_ref, v_ref, z_ref, g_ref, y_ref, o_ref, lse_ref):
        g = g_ref[...]
        for r0 in range(0, tp, Q_TILE):
            nq = min(Q_TILE, tp - r0)
            kend = r0 + nq
            qq = q_ref[0, 0, r0:kend, :]
            sd = _dot(qq, k_ref[0, 0, r0:kend, :], _NT)
            causal = (lax.broadcasted_iota(jnp.int32, (nq, nq), 1) <= lax.broadcasted_iota(jnp.int32, (nq, nq), 0))
            sd = jnp.where(causal, sd, NEG_INF)
            m = jnp.max(sd, axis=-1, keepdims=True)
            if r0:
                so = _dot(qq, k_ref[0, 0, 0:r0, :], _NT)
                m = jnp.maximum(m, jnp.max(so, axis=-1, keepdims=True))
            ed = jnp.exp2(sd - m)
            l = jnp.sum(ed, axis=-1, keepdims=True)
            o = _dot(ed.astype(BF16), v_ref[0, 0, r0:kend, :])
            if r0:
                eo = jnp.exp2(so - m)
                l = l + jnp.sum(eo, axis=-1, keepdims=True)
                o = o + _dot(eo.astype(BF16), v_ref[0, 0, 0:r0, :])
            o = o * (1.0 / l)
            o_ref[0, 0, r0:kend, :] = o
            lse_ref[0, 0, r0:kend, :] = jnp.broadcast_to(m + jnp.log2(l), (nq, LANES))
            ohat, _ = _rms_stats(o)
            z = z_ref[r0:kend, :].astype(F32)
            y_ref[r0:kend, :] = (ohat * g * (z * _sigmoid(z))).astype(BF16)

    qk = pl.BlockSpec((1, 1, tp, 2 * LANES), lambda b, h: (b, h, 0, 0))
    hv = pl.BlockSpec((1, 1, tp, D_V), lambda b, h: (b, h, 0, 0))
    return pl.pallas_call(
        body,
        name="attn_fwd",
        grid=(nb_seq, N_HEADS),
        in_specs=[qk, qk, hv,
                  _p_spec(tp, lambda b, h: (b, GRP_A // LANES + h)),
                  pl.BlockSpec((1, LANES), lambda b, h: (0, h))],
        out_specs=(pl.BlockSpec((tp, LANES), lambda b, h: (b, h)), hv, hv),
        out_shape=(
            jax.ShapeDtypeStruct((nb_seq * tp, N_HEADS * D_V), BF16),
            jax.ShapeDtypeStruct((nb_seq, N_HEADS, tp, D_V), F32),
            jax.ShapeDtypeStruct((nb_seq, N_HEADS, tp, LANES), F32),
        ),
        compiler_params=_params("parallel", "parallel"),
    )(q, k, v, p, g_attn)


_CONV_COL0 = (GRP_A + N_HEADS * D_V) // LANES


def _conv_specs(tp, order):
    cols = CONV_WIDTH // LANES
    return [_p_spec(tp, functools.partial(
        lambda a, b, off: order(a, b, off), off=_CONV_COL0 + i * cols)) for i in range(4)]


def _conv_fwd(p, conv_w, g_conv, nb_seq, tp):
    def body(b_ref, c_ref, h_ref, z_ref, w_ref, g_ref, y_ref):
        cc = c_ref[...].astype(F32) * h_ref[...].astype(F32)
        row = lax.broadcasted_iota(jnp.int32, (tp, LANES), 0)
        s1 = jnp.where(row >= 1, pltpu.roll(cc, 1, 0), 0.0)
        s2 = jnp.where(row >= 2, pltpu.roll(cc, 2, 0), 0.0)
        yc = b_ref[...].astype(F32) * (w_ref[0:1, :] * s2 + w_ref[1:2, :] * s1 + w_ref[2:3, :] * cc)
        r = lax.rsqrt(_group_mean(yc * yc) + EPS)
        z = z_ref[...].astype(F32)
        y_ref[...] = (yc * r * g_ref[...] * (z * _sigmoid(z))).astype(BF16)

    return pl.pallas_call(
        body,
        name="conv_fwd",
        grid=(nb_seq, CONV_WIDTH // LANES),
        in_specs=_conv_specs(tp, lambda b, t, off: (b, off + t)) + [
            pl.BlockSpec((8, LANES), lambda b, t: (0, t)),
            pl.BlockSpec((1, LANES), lambda b, t: (0, t))],
        out_specs=pl.BlockSpec((tp, LANES), lambda b, t: (b, t)),
        out_shape=jax.ShapeDtypeStruct((nb_seq * tp, CONV_WIDTH), BF16),
        compiler_params=_params("parallel", "parallel"),
    )(p, p, p, p, conv_w, g_conv)


def _token_copy(hbm, b, k, ts, buf, sem, to_hbm=False):
    lo, hi = max(k * ts - LANES, 0), (k + 1) * ts - LANES
    off = lo - (k * ts - LANES)
    src, dst = hbm.at[b, pl.ds(lo, hi - lo)], buf.at[pl.ds(off, hi - lo)]
    if to_hbm:
        src, dst = dst, src
    return pltpu.make_async_copy(src, dst, sem)


def _for_tile(k, nt, fn):
    for kk in range(nt):
        @pl.when(k == kk)
        def _(kk=kk):
            fn(kk)


def _out_proj_loss(ya, yc, w_out, x, target, g_final, nt):
    nb_seq, s, d = x.shape
    r, ka = ya.shape
    ts = (s + LANES) // nt
    steps = nb_seq * nt

    def body(a_ref, c_ref, w_ref, x_hbm, t_hbm, g_ref, dhb_ref, dg_ref, loss_ref,
             xbuf, tbuf, acc_ref, sems):
        i = pl.program_id(0)
        b, k = i // nt, i % nt

        @pl.when(i == 0)
        def _():
            acc_ref[...] = jnp.zeros_like(acc_ref)
            dg_ref[...] = jnp.zeros_like(dg_ref)

        slot = i % 2

        def fetch(seq, kk, sl):
            return [_token_copy(x_hbm, seq, kk, ts, xbuf.at[sl], sems.at[sl, 0]),
                    _token_copy(t_hbm, seq, kk, ts, tbuf.at[sl], sems.at[sl, 1])]

        def start(seq, sl, kk):
            if kk == 0:
                xbuf[sl, 0:LANES, :] = jnp.zeros((LANES, d), F32)
                tbuf[sl, 0:LANES, :] = jnp.zeros((LANES, d), F32)
            for cp in fetch(seq, kk, sl):
                cp.start()

        @pl.when(i == 0)
        def _():
            start(0, 0, 0)

        @pl.when(i + 1 < steps)
        def _():
            _for_tile((i + 1) % nt, nt, functools.partial(start, (i + 1) // nt, 1 - slot))

        mix = _dot(a_ref[...], w_ref[0:ka, :]) + _dot(c_ref[...], w_ref[ka:, :])
        _for_tile(k, nt, lambda kk: [cp.wait() for cp in fetch(b, kk, slot)])

        real = (lax.broadcasted_iota(jnp.int32, (ts, d), 0) >= LANES) | (k > 0)
        g = g_ref[...]
        hhat, rstd = _rms_stats(xbuf[slot] + mix)
        e = jnp.where(real, hhat * g - tbuf[slot], 0.0)
        acc_ref[...] += jnp.sum(e * e, axis=0, keepdims=True)
        dy = e * (1.0 / d)
        dg_ref[...] += jnp.sum(dy * hhat, axis=0, keepdims=True)
        dhb_ref[...] = _rms_bwd(g * dy, hhat, rstd).astype(BF16)

        @pl.when(i == steps - 1)
        def _():
            total = jnp.sum(acc_ref[...], axis=1, keepdims=True)
            loss_ref[...] = jnp.broadcast_to((0.5 / d) * total, loss_ref.shape)

    hbm = pl.BlockSpec(memory_space=pl.ANY)
    row = pl.BlockSpec((ts, d), lambda i: (i, 0))
    vec = pl.BlockSpec((1, d), lambda i: (0, 0))
    return pl.pallas_call(
        body,
        name="out_proj_loss",
        grid=(steps,),
        in_specs=[pl.BlockSpec((ts, ka), lambda i: (i, 0)), pl.BlockSpec((ts, yc.shape[1]), lambda i: (i, 0)),
                  pl.BlockSpec(w_out.shape, lambda i: (0, 0)), hbm, hbm, vec],
        out_specs=(row, vec, pl.BlockSpec((1, LANES), lambda i: (0, 0))),
        out_shape=(
            jax.ShapeDtypeStruct((r, d), BF16),
            jax.ShapeDtypeStruct((1, d), F32),
            jax.ShapeDtypeStruct((1, LANES), F32),
        ),
        scratch_shapes=[pltpu.VMEM((2, ts, d), F32), pltpu.VMEM((2, ts, d), F32), pltpu.VMEM((1, d), F32),
                        pltpu.SemaphoreType.DMA((2, 2))],
        compiler_params=_params("arbitrary"),
    )(ya, yc, w_out, x, target, g_final)


def _out_proj_bwd(dhb, w_out, ya, yc, bm):
    r, d = dhb.shape
    ka = ya.shape[1]
    n_mix = w_out.shape[0]
    last = r // bm - 1

    def body(dh_ref, w_ref, a_ref, c_ref, dcat_ref, dw_ref, acc_ref):
        @pl.when(pl.program_id(0) == 0)
        def _():
            acc_ref[...] = jnp.zeros_like(acc_ref)

        dh = dh_ref[...]
        dcat_ref[...] = _dot(dh, w_ref[...], _NT).astype(BF16)
        acc_ref[0:ka, :] += _dot(a_ref[...], dh, _TN)
        acc_ref[ka:, :] += _dot(c_ref[...], dh, _TN)

        @pl.when(pl.program_id(0) == last)
        def _():
            dw_ref[...] = acc_ref[...].astype(BF16)

    return pl.pallas_call(
        body,
        name="out_proj_bwd",
        grid=(r // bm,),
        in_specs=[pl.BlockSpec((bm, d), lambda i: (i, 0)), pl.BlockSpec(w_out.shape, lambda i: (0, 0)),
                  pl.BlockSpec((bm, ka), lambda i: (i, 0)), pl.BlockSpec((bm, yc.shape[1]), lambda i: (i, 0))],
        out_specs=(pl.BlockSpec((bm, n_mix), lambda i: (i, 0)),
                   pl.BlockSpec((n_mix, d), lambda i: (0, 0))),
        out_shape=(jax.ShapeDtypeStruct((r, n_mix), BF16),
                   jax.ShapeDtypeStruct((n_mix, d), BF16)),
        scratch_shapes=[pltpu.VMEM((n_mix, d), F32)],
        compiler_params=_params("arbitrary"),
    )(dhb, w_out, ya, yc)


def _attn_bwd(q, k, v, o, lse, dcat, p, g_attn):
    nb_seq, _, tp, _ = q.shape

    def body(q_ref, k_ref, v_ref, o_ref, lse_ref, dy_ref, z_ref, g_ref,
             dq_ref, dk_ref, dv_ref, dz_ref, dg_ref, dq_acc):
        @pl.when(pl.program_id(1) == 0)
        def _():
            dg_ref[...] = jnp.zeros_like(dg_ref)

        g = g_ref[...]
        z = z_ref[...].astype(F32)
        o = o_ref[0, 0]
        dy = dy_ref[...].astype(F32)
        sig = _sigmoid(z)
        ohat, r = _rms_stats(o)
        don = dy * (z * sig)
        dz_ref[...] = (dy * (ohat * g) * (sig * (1.0 + z * (1.0 - sig)))).astype(BF16)
        dg_ref[...] += jnp.sum(don * ohat, axis=0, keepdims=True)
        do = _rms_bwd(g * don, ohat, r)
        dvec = jnp.sum(do * o, axis=-1, keepdims=True)
        dob = do.astype(BF16)
        lse_col = lse_ref[0, 0, :, 0:1]
        dq_acc[...] = jnp.zeros_like(dq_acc)
        for k0 in range(0, tp, K_TILE):
            nk = min(K_TILE, tp - k0)
            nq = tp - k0
            qq = q_ref[0, 0, k0:, :]
            kk = k_ref[0, 0, k0:k0 + nk, :]
            causal = (lax.broadcasted_iota(jnp.int32, (nq, nk), 1) <= lax.broadcasted_iota(jnp.int32, (nq, nk), 0))
            pr = jnp.where(causal, jnp.exp2(_dot(qq, kk, _NT) - lse_col[k0:]), 0.0)
            dp = _dot(dob[k0:], v_ref[0, 0, k0:k0 + nk, :], _NT)
            ds = (pr * (dp - dvec[k0:])).astype(BF16)
            dv_ref[0, 0, k0:k0 + nk, :] = _dot(pr.astype(BF16), dob[k0:], _TN).astype(BF16)
            dk_ref[0, 0, k0:k0 + nk, :] = (_dot(ds, qq, _TN) * (ATTN_SCALE / Q_SCALE)).astype(BF16)
            dq_acc[k0:, :] += _dot(ds, kk)
        dq_ref[0, 0] = (dq_acc[...] * ATTN_SCALE).astype(BF16)

    qk = pl.BlockSpec((1, 1, tp, 2 * LANES), lambda h, b: (b, h, 0, 0))
    hv = pl.BlockSpec((1, 1, tp, D_V), lambda h, b: (b, h, 0, 0))
    col = pl.BlockSpec((tp, LANES), lambda h, b: (b, h))
    return pl.pallas_call(
        body,
        name="attn_bwd",
        grid=(N_HEADS, nb_seq),
        in_specs=[qk, qk, hv, hv, hv, col,
                  _p_spec(tp, lambda h, b: (b, GRP_A // LANES + h)),
                  pl.BlockSpec((1, LANES), lambda h, b: (0, h))],
        out_specs=(qk, qk, hv, col, pl.BlockSpec((1, LANES), lambda h, b: (0, h))),
        out_shape=(
            jax.ShapeDtypeStruct((nb_seq, N_HEADS, tp, 2 * LANES), BF16),
            jax.ShapeDtypeStruct((nb_seq, N_HEADS, tp, 2 * LANES), BF16),
            jax.ShapeDtypeStruct((nb_seq, N_HEADS, tp, D_V), BF16),
            jax.ShapeDtypeStruct((nb_seq * tp, N_HEADS * D_V), BF16),
            jax.ShapeDtypeStruct((1, N_HEADS * D_V), F32),
        ),
        scratch_shapes=[pltpu.VMEM((tp, 2 * LANES), F32)],
        compiler_params=_params("arbitrary", "arbitrary"),
    )(q, k, v, o, lse, dcat, p, g_attn)


def _qkv_bwd(p, dq, dk, dv, wq, wkv, gq, gkv, tables):
    nb_seq, _, tp, _ = dq.shape
    ht = tp // 2

    def body(pa_ref, dq_ref, dk_ref, dv_ref, wq_ref, wkv_ref, gq_ref, gkv_ref, cos_ref, sa_ref, sb_ref,
             dpa_ref, dwq_ref, dwkv_ref, dgq_ref, dgkv_ref):
        @pl.when(pl.program_id(0) == 0)
        def _():
            dwq_ref[...] = jnp.zeros_like(dwq_ref)
            dwkv_ref[...] = jnp.zeros_like(dwkv_ref)
            dgq_ref[...] = jnp.zeros_like(dgq_ref)
            dgkv_ref[...] = jnp.zeros_like(dgkv_ref)

        pa = jnp.concatenate([pa_ref[i] for i in range(GRP_A // P_TILE)], axis=1).astype(F32)
        gq, gkv = gq_ref[...], gkv_ref[...]
        cq_hat, rq = _rms_stats(pa[:, :Q_RANK])
        ckv_hat, rkv = _rms_stats(pa[:, Q_RANK:Q_RANK + KV_RANK])
        tabs = (cos_ref[...], sa_ref[...], sb_ref[...])

        pe = [dq_ref[0, h, :, D_NOPE:].astype(F32) for h in range(N_HEADS)]
        pairs = [_rope_t(pe[2 * i] + pltpu.roll(pe[2 * i + 1], D_ROPE, 1), *tabs).astype(BF16) for i in range(2)]
        dq_flat = jnp.concatenate([dq_ref[0, h, :, :D_NOPE] for h in range(N_HEADS)] + pairs, axis=1)
        dwq_ref[...] += _dot(dq_flat, (cq_hat * gq).astype(BF16), _TN)
        dcqn = _dot(dq_flat, wq_ref[...])
        dgq_ref[...] += jnp.sum(dcqn * cq_hat, axis=0, keepdims=True)
        dcq = _rms_bwd(gq * dcqn, cq_hat, rq)

        dkv_flat = jnp.concatenate([dk_ref[0, h, :, :D_NOPE] for h in range(N_HEADS)]
                                   + [dv_ref[0, h] for h in range(N_HEADS)], axis=1)
        dwkv_ref[...] += _dot((ckv_hat * gkv).astype(BF16), dkv_flat, _TN)
        dckvn = _dot(dkv_flat, wkv_ref[...], _NT)
        dgkv_ref[...] += jnp.sum(dckvn * ckv_hat, axis=0, keepdims=True)
        dckv = _rms_bwd(gkv * dckvn, ckv_hat, rkv)

        dk_pe = dk_ref[0, 0, :, D_NOPE:].astype(F32)
        for h in range(1, N_HEADS):
            dk_pe = dk_pe + dk_ref[0, h, :, D_NOPE:].astype(F32)
        dk_pe = jnp.where(lax.broadcasted_iota(jnp.int32, (ht, LANES), 1) < D_ROPE, dk_pe, 0.0)
        dpa_ref[...] = jnp.concatenate([dcq, dckv, _rope_t(dk_pe, *tabs)], axis=1).astype(BF16)

    full = lambda a: pl.BlockSpec(a.shape, lambda i: (0,) * a.ndim)
    tab = pl.BlockSpec((ht, LANES), lambda i: (i % 2, 0))
    qk = pl.BlockSpec((1, N_HEADS, ht, 2 * LANES), lambda i: (i // 2, 0, i % 2, 0))
    acc = lambda shape: pl.BlockSpec(shape, lambda i: (0, 0))
    return pl.pallas_call(
        body,
        name="qkv_bwd",
        grid=(2 * nb_seq,),
        in_specs=[pl.BlockSpec((GRP_A // P_TILE, ht, P_TILE), lambda i: (0, i, 0)), qk, qk,
                  pl.BlockSpec((1, N_HEADS, ht, D_V), lambda i: (i // 2, 0, i % 2, 0)),
                  full(wq), full(wkv), full(gq), full(gkv), tab, tab, tab],
        out_specs=(pl.BlockSpec((ht, GRP_A), lambda i: (i, 0)),
                   acc(wq.shape), acc(wkv.shape), acc((1, Q_RANK)), acc((1, KV_RANK))),
        out_shape=(
            jax.ShapeDtypeStruct((nb_seq * tp, GRP_A), BF16),
            jax.ShapeDtypeStruct(wq.shape, F32),
            jax.ShapeDtypeStruct(wkv.shape, F32),
            jax.ShapeDtypeStruct((1, Q_RANK), F32),
            jax.ShapeDtypeStruct((1, KV_RANK), F32),
        ),
        compiler_params=_params("arbitrary"),
    )(p, dq, dk, dv, wq, wkv, gq, gkv, *tables)


def _conv_bwd(p, dcat, conv_w, g_conv, nb_seq, tp):
    cols = CONV_WIDTH // LANES

    def body(b_ref, c_ref, h_ref, z_ref, dy_ref, w_ref, g_ref,
             db_ref, dc_ref, dh_ref, dz_ref, dw_ref, dg_ref):
        @pl.when(pl.program_id(1) == 0)
        def _():
            dw_ref[...] = jnp.zeros_like(dw_ref)
            dg_ref[...] = jnp.zeros_like(dg_ref)

        cb, c, h = b_ref[...].astype(F32), c_ref[...].astype(F32), h_ref[...].astype(F32)
        z, dy = z_ref[...].astype(F32), dy_ref[...].astype(F32)
        g = g_ref[...]
        w0, w1, w2 = w_ref[0:1, :], w_ref[1:2, :], w_ref[2:3, :]
        cc = c * h
        row = lax.broadcasted_iota(jnp.int32, (tp, LANES), 0)
        s1 = jnp.where(row >= 1, pltpu.roll(cc, 1, 0), 0.0)
        s2 = jnp.where(row >= 2, pltpu.roll(cc, 2, 0), 0.0)
        dwc = w0 * s2 + w1 * s1 + w2 * cc
        yc = cb * dwc
        r = lax.rsqrt(_group_mean(yc * yc) + EPS)
        ychat = yc * r
        sig = _sigmoid(z)
        dz_ref[...] = (dy * (ychat * g) * (sig * (1.0 + z * (1.0 - sig)))).astype(BF16)
        dyn = dy * (z * sig)
        dg_ref[...] += jnp.sum(dyn * ychat, axis=0, keepdims=True)
        gd = g * dyn
        dyc = r * (gd - ychat * _group_mean(gd * ychat))
        db_ref[...] = (dyc * dwc).astype(BF16)
        ddw = dyc * cb
        dw_ref[0:1, :] += jnp.sum(ddw * s2, axis=0, keepdims=True)
        dw_ref[1:2, :] += jnp.sum(ddw * s1, axis=0, keepdims=True)
        dw_ref[2:3, :] += jnp.sum(ddw * cc, axis=0, keepdims=True)
        u1 = jnp.where(row <= tp - 2, pltpu.roll(ddw, tp - 1, 0), 0.0)
        u2 = jnp.where(row <= tp - 3, pltpu.roll(ddw, tp - 2, 0), 0.0)
        dcc = w2 * ddw + w1 * u1 + w0 * u2
        dc_ref[...] = (dcc * h).astype(BF16)
        dh_ref[...] = (dcc * c).astype(BF16)

    col = pl.BlockSpec((tp, LANES), lambda t, b: (b, t))
    out = jax.ShapeDtypeStruct((nb_seq * tp, CONV_WIDTH), BF16)
    return pl.pallas_call(
        body,
        name="conv_bwd",
        grid=(cols, nb_seq),
        in_specs=_conv_specs(tp, lambda t, b, off: (b, off + t)) + [
            pl.BlockSpec((tp, LANES), lambda t, b: (b, N_HEADS * D_V // LANES + t)),
            pl.BlockSpec((8, LANES), lambda t, b: (0, t)),
            pl.BlockSpec((1, LANES), lambda t, b: (0, t))],
        out_specs=(col, col, col, col,
                   pl.BlockSpec((8, LANES), lambda t, b: (0, t)), pl.BlockSpec((1, LANES), lambda t, b: (0, t))),
        out_shape=(out, out, out, out,
                   jax.ShapeDtypeStruct((8, CONV_WIDTH), F32), jax.ShapeDtypeStruct((1, CONV_WIDTH), F32)),
        compiler_params=_params("arbitrary", "arbitrary"),
    )(p, p, p, p, dcat, conv_w, g_conv)


def _input_bwd(dps, w_in, x, meta, dh, norm_g, nt, send_in):
    nb_seq, s, d = x.shape
    r, kb = dps[0].shape
    ts = (s + LANES) // nt
    steps = nb_seq * nt
    n_dp = len(dps)
    in_slot = send_in.shape[1:]

    def body(*refs):
        dp_refs, w_ref, x_hbm, meta_ref, dh_ref, g_ref, pay_ref = refs[:n_dp], *refs[n_dp:n_dp + 6]
        o = n_dp + 6
        gx_hbm, dmeta_ref, dg_ref, r2_in = refs[o:o + 4]
        xbuf, gxbuf, tok_sems, own_in, r1_in, sum_in = refs[o + 4:o + 10]
        sems = refs[o + 10:]
        i = pl.program_id(0)
        b, k = i // nt, i % nt

        def plan():
            return _reduce_plan((pay_ref,), (own_in,), (r1_in,), (sum_in,), (r2_in,), *sems)

        @pl.when(i == 0)
        def _():
            dmeta_ref[...] = jnp.zeros_like(dmeta_ref)
            dg_ref[...] = jnp.zeros_like(dg_ref)
            plan()[0]()

        @pl.when(i == 1)
        def _():
            plan()[1]()

        def start(kk):
            if kk == 0:
                xbuf[0:PAD_FRONT, :] = jnp.zeros((PAD_FRONT, d), F32)
                xbuf[PAD_FRONT:LANES, :] = meta_ref[...]
            _token_copy(x_hbm, b, kk, ts, xbuf, tok_sems.at[0]).start()

        _for_tile(k, nt, start)
        du = _dot(dp_refs[0][...], w_ref[0:kb, :])
        for j in range(1, n_dp):
            du = du + _dot(dp_refs[j][...], w_ref[kb * j:kb * (j + 1), :])
        _for_tile(k, nt, lambda kk: _token_copy(x_hbm, b, kk, ts, xbuf, tok_sems.at[0]).wait())

        g = g_ref[...]
        hhat, rstd = _rms_stats(xbuf[...])
        dg_ref[...] += jnp.sum(du * hhat, axis=0, keepdims=True)
        res = _rms_bwd(g * du, hhat, rstd) + dh_ref[...].astype(F32)

        @pl.when(i > 0)
        def _():
            _for_tile(k, nt, lambda kk: _token_copy(gx_hbm, b, (kk - 1) % nt, ts, gxbuf, tok_sems.at[1], True).wait())

        gxbuf[...] = res

        @pl.when(k == 0)
        def _():
            dmeta_ref[...] += gxbuf[PAD_FRONT:LANES, :]

        _for_tile(k, nt, lambda kk: _token_copy(gx_hbm, b, kk, ts, gxbuf, tok_sems.at[1], True).start())

        @pl.when(i == steps - 1)
        def _():
            _token_copy(gx_hbm, b, nt - 1, ts, gxbuf, tok_sems.at[1], True).wait()
            plan()[2]()

    whole = lambda a: pl.BlockSpec(a.shape, lambda i: (0,) * a.ndim)
    hbm = pl.BlockSpec(memory_space=pl.ANY)
    return pl.pallas_call(
        body,
        name="input_bwd",
        grid=(steps,),
        in_specs=[pl.BlockSpec((ts, kb), lambda i: (i, 0)) for _ in dps]
        + [whole(w_in), hbm, whole(meta), pl.BlockSpec((ts, d), lambda i: (i, 0)), whole(norm_g), hbm],
        out_specs=(hbm, pl.BlockSpec((N_META, d), lambda i: (0, 0)), pl.BlockSpec((1, d), lambda i: (0, 0)), hbm),
        out_shape=(jax.ShapeDtypeStruct((nb_seq, s, d), F32),
                   jax.ShapeDtypeStruct((N_META, d), F32),
                   jax.ShapeDtypeStruct((1, d), F32),
                   jax.ShapeDtypeStruct((N_CHIPS,) + in_slot, BF16)),
        scratch_shapes=[pltpu.VMEM((ts, d), F32), pltpu.VMEM((ts, d), F32), pltpu.SemaphoreType.DMA((2,))]
        + _reduce_scratch([(in_slot, BF16)], [True]),
        compiler_params=_params("arbitrary"),
    )(*dps, w_in, x, meta, dh, norm_g, send_in)


def _in_proj_bwd_w(u, dps, bm, small_grads, send_out):
    r, d = u.shape
    kb = dps[0].shape[1]
    steps = r // bm
    n_dp, n_small = len(dps), len(small_grads)
    out_slot, small_slot = send_out.shape[1:], (SMALL_ROWS, LANES)

    def body(*refs):
        u_ref, dp_refs = refs[0], refs[1:1 + n_dp]
        small_refs = refs[1 + n_dp:1 + n_dp + n_small]
        o = 1 + n_dp + n_small
        pay_out, o_ref, r2_out, r2_small = refs[o:o + 4]
        acc_ref, ssmall, r1_out, sum_out, r1_small, sum_small = refs[o + 4:o + 10]
        sems = refs[o + 10:]
        i = pl.program_id(0)

        def plan():
            return _reduce_plan((pay_out, ssmall), (None, None), (r1_out, r1_small), (sum_out, sum_small),
                                (r2_out, r2_small), *sems)

        @pl.when(i == 0)
        def _():
            acc_ref[...] = jnp.zeros_like(acc_ref)
            _pack_small(ssmall, *small_refs)
            plan()[0]()

        @pl.when(i == 1)
        def _():
            plan()[1]()

        uu = u_ref[...]
        for j in range(n_dp):
            acc_ref[kb * j:kb * (j + 1), :] += _dot(dp_refs[j][...], uu, _TN)

        @pl.when(i == steps - 1)
        def _():
            for k in range(N_DEV):
                for s, e, c0 in _in_pieces(k):
                    o_ref[k, s:e, :] = acc_ref[c0:c0 + e - s, :].astype(BF16)
                o_ref[k, SHARD_IN:, :] = jnp.zeros((SHARD_IN_PAD - SHARD_IN, d), BF16)
            plan()[2]()

    whole = lambda a: pl.BlockSpec(a.shape, lambda i: (0,) * a.ndim)
    hbm = pl.BlockSpec(memory_space=pl.ANY)
    return pl.pallas_call(
        body,
        name="in_proj_bwd_w",
        grid=(steps,),
        in_specs=[pl.BlockSpec((bm, d), lambda i: (i, 0))]
        + [pl.BlockSpec((bm, kb), lambda i: (i, 0)) for _ in dps] + [whole(a) for a in small_grads]
        + [whole(send_out)],
        out_specs=(pl.BlockSpec((N_DEV, SHARD_IN_PAD, d), lambda i: (0, 0, 0)), hbm, hbm),
        out_shape=(jax.ShapeDtypeStruct((N_DEV, SHARD_IN_PAD, d), BF16),
                   jax.ShapeDtypeStruct((N_CHIPS,) + out_slot, BF16),
                   jax.ShapeDtypeStruct((N_CHIPS,) + small_slot, F32)),
        scratch_shapes=[pltpu.VMEM((kb * n_dp, d), F32), pltpu.VMEM((N_DEV,) + small_slot, F32)]
        + _reduce_scratch([(out_slot, BF16), (small_slot, F32)], [False, False]),
        compiler_params=_params("arbitrary"),
    )(u, *dps, *small_grads, send_out)


def _local_step(x, loss_target, u, p, meta_f, norm_g, w_in_p, q_norm_g, w_q_p, kv_norm_g, w_kv_p, conv_w_f,
                attn_out_g, conv_out_g, w_out_f, g_final):
    nb_seq, s, d = x.shape
    tp = s + LANES
    ht = tp // 2
    tables = _rope_tables(tp)

    q, k, v = _qkv_fwd(p, w_q_p, w_kv_p, q_norm_g, kv_norm_g, tables, nb_seq, tp)
    ya, o, lse = _attn_fwd(q, k, v, p, attn_out_g)
    yc = _conv_fwd(p, conv_w_f, conv_out_g, nb_seq, tp)
    dhb, d_final_g, loss_part = _out_proj_loss(ya, yc, w_out_f, x, loss_target, g_final, TOKEN_TILES)

    dcat, d_w_out = _out_proj_bwd(dhb, w_out_f, ya, yc, ht)
    send_out = d_w_out.reshape(N_DEV, SHARD_OUT, d)
    dq, dk, dv, dz_attn, d_attn_g = _attn_bwd(q, k, v, o, lse, dcat, p, attn_out_g)
    dpa, d_wq_p, d_wkv_p, d_gq, d_gkv = _qkv_bwd(p, dq, dk, dv, w_q_p, w_kv_p, q_norm_g, kv_norm_g, tables)
    d_b, d_c, d_h, dz_conv, d_conv_w, d_conv_g = _conv_bwd(p, dcat, conv_w_f, conv_out_g, nb_seq, tp)
    dps = (dpa, dz_attn, d_b, d_c, d_h, dz_conv)
    small = (d_wq_p, d_wkv_p, d_conv_w, d_final_g, d_gq, d_gkv, d_attn_g, d_conv_g, loss_part)
    send_in, r_out, r_small = _in_proj_bwd_w(u, dps, ht, small, send_out)
    grad_x, d_meta, d_norm_g, r_in = _input_bwd(dps, w_in_p, x, meta_f, dhb, norm_g, TOKEN_TILES, send_in)
    return grad_x, r_in, r_out, r_small, d_meta, d_norm_g


def kernel(x, meta_tokens, norm_g, w_in, q_norm_g, w_q_up, kv_norm_g, w_kv_up, conv_w, attn_out_g, conv_out_g, w_out, final_norm_g, loss_target, m_meta_tokens, m_norm_g, m_w_in, m_q_norm_g, m_w_q_up, m_kv_norm_g, m_w_kv_up, m_conv_w, m_attn_out_g, m_conv_out_g, m_w_out, m_final_norm_g, v_meta_tokens, v_norm_g, v_w_in, v_q_norm_g, v_w_q_up, v_kv_norm_g, v_w_kv_up, v_conv_w, v_attn_out_g, v_conv_out_g, v_w_out, v_final_norm_g):
    d = x.shape[-1]
    order = jnp.asarray(_tile_orders()[0])[2 * lax.axis_index("x") + lax.axis_index("y")]
    u, p, w_in_p, meta_f, w_q_p, w_kv_p, w_out_f, conv_w_f = _prep_in_proj(
        x, meta_tokens, norm_g, w_in[0].T, w_q_up[0].T, w_kv_up[0], w_out[0], conv_w.transpose(1, 0, 2), order)
    g_final = final_norm_g.reshape(1, d)
    grad_x, r_in, r_out, r_small, d_meta, d_norm_g = _local_step(
        x, loss_target, u, p, meta_f, norm_g, w_in_p, q_norm_g, w_q_p, kv_norm_g, w_kv_p, conv_w_f,
        attn_out_g, conv_out_g, w_out_f, g_final)

    flat = lambda a: a.reshape(a.shape[-2:]) if a.ndim == 3 else a.reshape(1, -1) if a.ndim == 1 else a
    transposed = ("w_in", "w_q_up")

    def to_kernel(n, a):
        if n == "conv_w":
            return a.transpose(1, 0, 2)
        return flat(a).T if n in transposed else flat(a)

    def from_kernel(n, a, shape):
        if n == "conv_w":
            return a.transpose(1, 0, 2)
        return (a.T if n in transposed else a).reshape(shape)
    params = {
        "meta_tokens": (meta_tokens, m_meta_tokens, v_meta_tokens),
        "norm_g": (norm_g, m_norm_g, v_norm_g),
        "w_in": (w_in, m_w_in, v_w_in),
        "q_norm_g": (q_norm_g, m_q_norm_g, v_q_norm_g),
        "w_q_up": (w_q_up, m_w_q_up, v_w_q_up),
        "kv_norm_g": (kv_norm_g, m_kv_norm_g, v_kv_norm_g),
        "w_kv_up": (w_kv_up, m_w_kv_up, v_w_kv_up),
        "conv_w": (conv_w, m_conv_w, v_conv_w),
        "attn_out_g": (attn_out_g, m_attn_out_g, v_attn_out_g),
        "conv_out_g": (conv_out_g, m_conv_out_g, v_conv_out_g),
        "w_out": (w_out, m_w_out, v_w_out),
        "final_norm_g": (final_norm_g, m_final_norm_g, v_final_norm_g),
    }
    grads, loss = _reduce_tail(r_in, r_out, r_small, d_meta, d_norm_g)
    updated = _adamw(grads, {n: tuple(to_kernel(n, a) for a in t) for n, t in params.items()})
    outs = [[from_kernel(n, updated[n][i], params[n][0].shape) for n, _ in PARAM_SHAPES] for i in range(4)]
    return (loss[0, 0], grad_x, *outs[0], *outs[1], *outs[2], *outs[3])
```

```python
import functools

import jax
import jax.numpy as jnp
import numpy as np
from jax import lax
from jax.experimental import pallas as pl
from jax.experimental.pallas import tpu as pltpu

F32 = jnp.float32
BF16 = jnp.bfloat16

N_META = 16
D_MODEL = 1024
N_HEADS = 4
D_NOPE = 128
D_ROPE = 64
D_V = 128
Q_RANK = 256
KV_RANK = 128
CONV_WIDTH = 512
CONV_GROUP = 64
ROPE_THETA = 10000.0
ATTN_SCALE = (D_NOPE + D_ROPE) ** -0.5
Q_SCALE = ATTN_SCALE * 1.4426950408889634
EPS = 1e-6
NEG_INF = -1e30

ADAM_LR = 0.001
ADAM_B1 = 0.9
ADAM_B2 = 0.999
ADAM_EPS = 1e-08
ADAM_WD = 0.01
ADAM_STEP = 10

LANES = 128
PAD_FRONT = LANES - N_META
K_TILE = 256
Q_TILE = 512
N_DEV = 8
VMEM_LIMIT = 56 * 1024 * 1024

IN_PAD = 3072
GRP_A = 512
N_A = Q_RANK + KV_RANK + D_ROPE
IN_PROJ = 3008
SHARD_IN = IN_PROJ // N_DEV
SHARD_IN_PAD = 384
SHARD_Q = 96
SHARD_KV = 128
SHARD_OUT = 128
SHARD_CONV = 64
SHARD_META = 128
Q_COLS = N_HEADS * (D_NOPE + D_ROPE)
KV_COLS = N_HEADS * (D_NOPE + D_V)

ROW_Q, ROW_KV, ROW_META, ROW_CONV = 0, 256, 384, 400
ROW_REPL = 408
ROW_NORM, ROW_FINAL, ROW_GQ, ROW_GKV, ROW_ATTN, ROW_CONVG, ROW_LOSS = 408, 416, 424, 426, 427, 431, 435
SMALL_ROWS = 440

PARAM_SHAPES = (
    ("meta_tokens", (N_META, SHARD_META)), ("norm_g", (1, D_MODEL)), ("w_in", (SHARD_IN, D_MODEL)),
    ("q_norm_g", (1, Q_RANK)), ("w_q_up", (SHARD_Q, Q_RANK)), ("kv_norm_g", (1, KV_RANK)),
    ("w_kv_up", (KV_RANK, SHARD_KV)), ("conv_w", (3, 1, SHARD_CONV)), ("attn_out_g", (1, CONV_WIDTH)),
    ("conv_out_g", (1, CONV_WIDTH)), ("w_out", (SHARD_OUT, D_MODEL)), ("final_norm_g", (1, D_MODEL)),
)


def _in_pieces(k):
    lo, hi = SHARD_IN * k, SHARD_IN * (k + 1)
    out = []
    if lo < N_A:
        out.append((0, min(hi, N_A) - lo, lo))
    if hi > N_A:
        s = max(lo, N_A)
        out.append((s - lo, hi - lo, s + GRP_A - N_A))
    return out


P_TILE = 256


def _tile_pieces(j):
    lo, hi = P_TILE * j, P_TILE * (j + 1)
    out = []
    for k in range(N_DEV):
        for s, e, d in _in_pieces(k):
            a, b = max(d, lo), min(d + e - s, hi)
            if a < b:
                out.append((k, s + a - d, s + b - d, a - lo))
    return out


def _tile_orders():
    n_tiles = IN_PAD // P_TILE
    sources = [{k for k, _, _, _ in _tile_pieces(j)} for j in range(n_tiles)]
    rows, n_early, n_free = [], n_tiles, n_tiles
    for chip in range(N_CHIPS):
        own = {2 * chip, 2 * chip + 1}
        diagonal = {2 * (N_CHIPS - 1 - chip), 2 * (N_CHIPS - 1 - chip) + 1}
        early = [j for j in range(n_tiles) if sources[j] <= own]
        late = [j for j in range(n_tiles) if sources[j] & diagonal]
        mid = [j for j in range(n_tiles) if j not in early and j not in late]
        rows.append(early + mid + late)
        n_early, n_free = min(n_early, len(early)), min(n_free, len(early) + len(mid))
    return np.asarray(rows, np.int32), n_early, n_free


def _q_pieces(k):
    lo, hi = SHARD_Q * k, SHARD_Q * (k + 1)
    out = []
    for h in range(N_HEADS):
        base = (D_NOPE + D_ROPE) * h
        s, e = max(lo, base), min(hi, base + D_NOPE)
        if s < e:
            out.append((s - lo, e - lo, D_NOPE * h + s - base))
        s, e = max(lo, base + D_NOPE), min(hi, base + D_NOPE + D_ROPE)
        if s < e:
            out.append((s - lo, e - lo, N_HEADS * D_NOPE + D_ROPE * h + s - base - D_NOPE))
    return out


def _kv_dst(k):
    return D_NOPE * (k // 2) + (N_HEADS * D_NOPE if k % 2 else 0)


def _params(*sem):
    return pltpu.CompilerParams(dimension_semantics=sem, vmem_limit_bytes=VMEM_LIMIT)


def _rms_stats(x):
    r = lax.rsqrt(jnp.mean(x * x, axis=-1, keepdims=True) + EPS)
    return x * r, r


def _rms_bwd(gdy, xhat, r):
    return r * (gdy - xhat * jnp.mean(gdy * xhat, axis=-1, keepdims=True))


def _sigmoid(z):
    return 1.0 / (1.0 + jnp.exp(-z))


def _group_mean(x):
    i0 = lax.broadcasted_iota(jnp.int32, (LANES, LANES), 0) // CONV_GROUP
    i1 = lax.broadcasted_iota(jnp.int32, (LANES, LANES), 1) // CONV_GROUP
    m = jnp.where(i0 == i1, 1.0 / CONV_GROUP, 0.0).astype(BF16)
    hi = x.astype(BF16)
    lo = (x - hi.astype(F32)).astype(BF16)
    return jnp.dot(hi, m, preferred_element_type=F32) + jnp.dot(lo, m, preferred_element_type=F32)


_NT = (((1,), (1,)), ((), ()))
_TN = (((0,), (0,)), ((), ()))


def _dot(a, b, dims=None):
    if dims is None:
        return jnp.dot(a, b, preferred_element_type=F32)
    return lax.dot_general(a, b, dims, preferred_element_type=F32)


def _device_position():
    x, y, c = lax.axis_index("x"), lax.axis_index("y"), lax.axis_index("c")
    return x, y, c, 4 * x + 2 * y + c


def _gather_plan(srcs, slots, send_sems, recv_sems, local_sems):
    x, y, c, _ = _device_position()
    me, sibling = (x, y, c), (x, y, 1 - c)
    flip = lambda v, on: v + on - 2 * v * on
    near = (flip(x, 1 - c), flip(y, c))
    far = (flip(x, c), flip(y, 1 - c))
    diag = (1 - x, 1 - y)
    n = len(srcs)

    def slot(a, px, py, pc):
        return slots[a].at[4 * px + 2 * py + pc]

    def copy(a, k, block, to, own=False):
        return pltpu.make_async_remote_copy(
            src_ref=srcs[a] if own else slot(a, *block),
            dst_ref=slot(a, *block),
            send_sem=send_sems.at[7 * a + k],
            recv_sem=recv_sems.at[7 * a + k],
            device_id=to,
            device_id_type=pl.DeviceIdType.MESH,
        )

    def local(a):
        return pltpu.make_async_copy(srcs[a], slot(a, *me), local_sems.at[a])

    sent = [(me, sibling), (me, (*near, c)), (me, (*far, c)), ((*near, c), (*far, c)),
            ((*near, c), sibling), ((*far, c), sibling), ((*diag, c), sibling)]
    landed = [sibling, (*near, c), (*far, c), (*diag, c), (*far, 1 - c), (*near, 1 - c), (*diag, 1 - c)]

    def send(a, k):
        return copy(a, k, *sent[k], own=k < 3)

    def arrival(a, k):
        return copy(a, k, landed[k], me)

    def start():
        for a in range(n):
            local(a).start()
            for k in range(3):
                send(a, k).start()

    def own():
        for a in range(n):
            local(a).wait()
            arrival(a, 0).wait_recv()

    def mid():
        for a in range(n):
            arrival(a, 1).wait_recv()
            send(a, 3).start()
            send(a, 4).start()
        for a in range(n):
            arrival(a, 2).wait_recv()
            send(a, 5).start()
        for a in range(n):
            for k in (4, 5):
                arrival(a, k).wait_recv()

    def late():
        for a in range(n):
            arrival(a, 3).wait_recv()
            send(a, 6).start()
        for a in range(n):
            arrival(a, 6).wait_recv()

    def finish():
        for a in range(n):
            for k in range(7):
                send(a, k).wait_send()

    return start, own, mid, late, finish


def _adam_update(g, w, m, v):
    m_new = ADAM_B1 * m + (1.0 - ADAM_B1) * g
    v_new = ADAM_B2 * v + (1.0 - ADAM_B2) * (g * g)
    m_hat = m_new / (1.0 - ADAM_B1 ** ADAM_STEP)
    v_hat = v_new / (1.0 - ADAM_B2 ** ADAM_STEP)
    return -ADAM_LR * (m_hat / (jnp.sqrt(v_hat) + ADAM_EPS) + ADAM_WD * w), m_new, v_new


def _adamw(grads, params):
    names = [n for n, _ in PARAM_SHAPES]
    n_p = len(names)

    def body(*refs):
        for i in range(n_p):
            g = refs[i][...]
            w, m, v = (refs[n_p + 3 * i + j][...] for j in range(3))
            delta, m_new, v_new = _adam_update(g, w, m, v)
            for j, val in enumerate((g, delta, m_new, v_new)):
                refs[4 * n_p + 4 * i + j][...] = val

    vm = pl.BlockSpec(memory_space=pltpu.VMEM)
    out_shape = []
    for _, shape in PARAM_SHAPES:
        out_shape += [jax.ShapeDtypeStruct(shape, F32)] * 4
    outs = pl.pallas_call(
        body,
        name="adamw",
        out_shape=tuple(out_shape),
        in_specs=[vm] * (4 * n_p),
        out_specs=(vm,) * (4 * n_p),
        compiler_params=pltpu.CompilerParams(vmem_limit_bytes=VMEM_LIMIT),
    )(*[grads[n] for n in names], *[a for n in names for a in params[n]])
    return {n: outs[4 * i:4 * i + 4] for i, n in enumerate(names)}


N_CHIPS = 4


def _reduce_plan(pays, owns, r1s, sums, r2s, send1, recv1, send2, recv2, local_sems):
    x, y, c, _ = _device_position()
    sibling = (x, y, 1 - c)
    chips = [((1 - x if rj & 2 else x), (1 - y if rj & 1 else y)) for rj in range(N_CHIPS)]
    n = len(pays)

    def slot_of(rj, core):
        return 4 * chips[rj][0] + 2 * chips[rj][1] + core

    def to_sibling(a, rj):
        return pltpu.make_async_remote_copy(
            src_ref=pays[a].at[slot_of(rj, 1 - c)], dst_ref=r1s[a].at[rj],
            send_sem=send1.at[N_CHIPS * a + rj], recv_sem=recv1.at[N_CHIPS * a + rj],
            device_id=sibling, device_id_type=pl.DeviceIdType.MESH)

    def load_own(a, rj):
        return pltpu.make_async_copy(pays[a].at[slot_of(rj, c)], owns[a].at[rj], local_sems.at[2 * N_CHIPS * a + rj])

    def to_chip(a, rj):
        return pltpu.make_async_remote_copy(
            src_ref=sums[a].at[rj], dst_ref=r2s[a].at[rj],
            send_sem=send2.at[N_CHIPS * a + rj], recv_sem=recv2.at[N_CHIPS * a + rj],
            device_id=(*chips[rj], c), device_id_type=pl.DeviceIdType.MESH)

    def keep(a):
        return pltpu.make_async_copy(sums[a].at[0], r2s[a].at[0], local_sems.at[2 * N_CHIPS * a + N_CHIPS])

    def start():
        for a in range(n):
            for rj in range(N_CHIPS):
                to_sibling(a, rj).start()
                if owns[a] is not None:
                    load_own(a, rj).start()

    def combine():
        for a in range(n):
            for rj in range(N_CHIPS):
                to_sibling(a, rj).wait_recv()
                if owns[a] is not None:
                    load_own(a, rj).wait()
                    mine = owns[a][rj]
                else:
                    mine = pays[a][slot_of(rj, c)]
                sums[a][rj] = (mine.astype(F32) + r1s[a][rj].astype(F32)).astype(sums[a].dtype)
            keep(a).start()
            for rj in range(1, N_CHIPS):
                to_chip(a, rj).start()

    def finish():
        for a in range(n):
            for rj in range(1, N_CHIPS):
                to_chip(a, rj).wait_recv()
            for rj in range(N_CHIPS):
                to_sibling(a, rj).wait_send()
            for rj in range(1, N_CHIPS):
                to_chip(a, rj).wait_send()
            keep(a).wait()

    return start, combine, finish


def _reduce_scratch(shapes_dtypes, own_flags):
    out = []
    for (shape, dtype), own in zip(shapes_dtypes, own_flags):
        if own:
            out.append(pltpu.VMEM((N_CHIPS,) + shape, dtype))
        out += [pltpu.VMEM((N_CHIPS,) + shape, dtype), pltpu.VMEM((N_CHIPS,) + shape, dtype)]
    n = len(shapes_dtypes)
    out += [pltpu.SemaphoreType.DMA((N_CHIPS * n,))] * 4 + [pltpu.SemaphoreType.DMA((2 * N_CHIPS * n,))]
    return out


def _pack_small(ssmall, dwq, dwkv, dconv, dfinal, dgq, dgkv, dattn, dconvg, loss_part):
    ssmall[...] = jnp.zeros_like(ssmall)
    rep = ssmall.at[0]
    for i in range(D_MODEL // LANES):
        rep[ROW_FINAL + i:ROW_FINAL + i + 1, :] = dfinal[:, LANES * i:LANES * (i + 1)]
    for i in range(Q_RANK // LANES):
        rep[ROW_GQ + i:ROW_GQ + i + 1, :] = dgq[:, LANES * i:LANES * (i + 1)]
    rep[ROW_GKV:ROW_GKV + 1, :] = dgkv[...]
    for i in range(CONV_WIDTH // LANES):
        rep[ROW_ATTN + i:ROW_ATTN + i + 1, :] = dattn[:, LANES * i:LANES * (i + 1)]
        rep[ROW_CONVG + i:ROW_CONVG + i + 1, :] = dconvg[:, LANES * i:LANES * (i + 1)]
    rep[ROW_LOSS:ROW_LOSS + 1, :] = loss_part[...]
    for k in range(N_DEV):
        if k:
            ssmall[k, ROW_REPL:, :] = ssmall[0, ROW_REPL:, :]
        for s, e, d in _q_pieces(k):
            for i in range(Q_RANK // LANES):
                ssmall[k, ROW_Q + SHARD_Q * i + s:ROW_Q + SHARD_Q * i + e, :] = dwq[d:d + e - s, LANES * i:LANES * (i + 1)]
        ssmall[k, ROW_KV:ROW_KV + KV_RANK, :] = dwkv[:, _kv_dst(k):_kv_dst(k) + SHARD_KV]
        ssmall[k, ROW_CONV:ROW_CONV + 3, 0:SHARD_CONV] = dconv[0:3, SHARD_CONV * k:SHARD_CONV * (k + 1)]


TOKEN_TILES = 4
TAIL_ROWS = N_META + D_MODEL // LANES


def _reduce_tail(r_in, r_out, r_small, d_meta, d_norm):
    n_p = len(PARAM_SHAPES)
    names = [n for n, _ in PARAM_SHAPES]

    def body(*refs):
        rin, rout, rsmall, dmeta, dnorm = refs[:5]
        g_out = {n: refs[5 + i] for i, n in enumerate(names)}
        loss_out = refs[5 + n_p]
        stail, rtail, gsum, gtail, send_sems, recv_sems = refs[6 + n_p:]
        x, y, c, me = _device_position()
        my_chip = 2 * x + y

        for k in range(N_DEV):
            stail[k, 0:N_META, :] = dmeta[:, SHARD_META * k:SHARD_META * (k + 1)]
            for i in range(D_MODEL // LANES):
                stail[k, N_META + i:N_META + i + 1, :] = dnorm[:, LANES * i:LANES * (i + 1)]
        copies = []
        for r in range(1, N_DEV):
            peer = (1 - x if r & 4 else x, 1 - y if r & 2 else y, 1 - c if r & 1 else c)
            copies.append(pltpu.make_async_remote_copy(
                src_ref=stail.at[4 * peer[0] + 2 * peer[1] + peer[2]],
                dst_ref=rtail.at[r],
                send_sem=send_sems.at[r - 1],
                recv_sem=recv_sems.at[r - 1],
                device_id=peer,
                device_id_type=pl.DeviceIdType.MESH,
            ))
        for cp in copies:
            cp.start()
        rtail[0] = stail[me]

        g = rin[my_chip].astype(F32)
        for ch in range(1, N_CHIPS):
            g = g + rin[ch ^ my_chip].astype(F32)
        g_out["w_in"][...] = g[:SHARD_IN, :]

        g = rout[my_chip].astype(F32)
        gs = rsmall[my_chip]
        for ch in range(1, N_CHIPS):
            g = g + rout[ch ^ my_chip].astype(F32)
            gs = gs + rsmall[ch ^ my_chip]
        g_out["w_out"][...] = g
        gsum[...] = gs
        for i in range(Q_RANK // LANES):
            g_out["w_q_up"][:, LANES * i:LANES * (i + 1)] = gsum[ROW_Q + SHARD_Q * i:ROW_Q + SHARD_Q * (i + 1), :]
        g_out["w_kv_up"][...] = gsum[ROW_KV:ROW_KV + KV_RANK, :]
        for i in range(3):
            g_out["conv_w"][i] = gsum[ROW_CONV + i:ROW_CONV + i + 1, 0:SHARD_CONV]
        for name, row, width in (("final_norm_g", ROW_FINAL, D_MODEL), ("q_norm_g", ROW_GQ, Q_RANK),
                                 ("kv_norm_g", ROW_GKV, KV_RANK), ("attn_out_g", ROW_ATTN, CONV_WIDTH),
                                 ("conv_out_g", ROW_CONVG, CONV_WIDTH)):
            for i in range(width // LANES):
                g_out[name][:, LANES * i:LANES * (i + 1)] = gsum[row + i:row + i + 1, :]
        loss_out[...] = gsum[ROW_LOSS:ROW_LOSS + 1, :]

        for cp in copies:
            cp.wait_recv()
        gt = rtail[me]
        for d in range(1, N_DEV):
            gt = gt + rtail[d ^ me]
        gtail[...] = gt
        g_out["meta_tokens"][...] = gtail[0:N_META, :]
        for i in range(D_MODEL // LANES):
            g_out["norm_g"][:, LANES * i:LANES * (i + 1)] = gtail[N_META + i:N_META + i + 1, :]
        for cp in copies:
            cp.wait_send()

    vm = pl.BlockSpec(memory_space=pltpu.VMEM)
    out_shape = [jax.ShapeDtypeStruct(shape, F32) for _, shape in PARAM_SHAPES]
    out_shape.append(jax.ShapeDtypeStruct((1, LANES), F32))
    outs = pl.pallas_call(
        body,
        name="reduce_tail",
        out_shape=tuple(out_shape),
        in_specs=[vm] * 5,
        out_specs=(vm,) * len(out_shape),
        scratch_shapes=[
            pltpu.VMEM((N_DEV, TAIL_ROWS, LANES), F32),
            pltpu.VMEM((N_DEV, TAIL_ROWS, LANES), F32),
            pltpu.VMEM((SMALL_ROWS, LANES), F32),
            pltpu.VMEM((TAIL_ROWS, LANES), F32),
            pltpu.SemaphoreType.DMA((N_DEV - 1,)),
            pltpu.SemaphoreType.DMA((N_DEV - 1,)),
        ],
        compiler_params=pltpu.CompilerParams(vmem_limit_bytes=VMEM_LIMIT),
    )(r_in, r_out, r_small, d_meta, d_norm)
    return {n: outs[i] for i, n in enumerate(names)}, outs[-1]


def _prep_in_proj(x, meta, norm_g, w_in_t, w_q, w_kv, w_out, conv_w, order):
    nb_seq, s, d = x.shape
    tp = s + LANES
    m = nb_seq * tp
    ts = s // TOKEN_TILES
    n_real = nb_seq * TOKEN_TILES
    n_norm = n_real + 1
    n_tiles = IN_PAD // P_TILE
    _, n_early, n_free = _tile_orders()
    steps = n_norm + n_tiles
    dot_rows = m // 4
    qkv_shape = (SHARD_Q + KV_RANK, Q_RANK)

    def tile(t):
        t = jnp.minimum(t, n_real - 1)
        return t // TOKEN_TILES, t % TOKEN_TILES

    def column_tile(t, order_ref):
        chip = 2 * lax.axis_index("x") + lax.axis_index("y")
        return order_ref[chip * n_tiles + jnp.maximum(t - n_norm, 0)]

    def body(order_ref, x_ref, meta_ref, g_ref, win_ref, wq_ref, wkv_ref, wout_ref, conv_ref,
             u_hbm, p_ref, w_in_p, meta_f, w_q_p, w_kv_p, w_out_f, conv_f,
             u_all, sbig, gbig, smeta, gmeta, sqkv, sout, sconv, gqkv, gout, gconv,
             send_in, recv_in, local_in, send_meta, recv_meta, send_rest, recv_rest, local_rest, u_sem):
        t = pl.program_id(0)
        px, py, pc, me = _device_position()
        u_copy = pltpu.make_async_copy(u_all, u_hbm, u_sem)

        def plan_in():
            return _gather_plan((sbig,), (gbig,), send_in, recv_in, local_in)

        def plan_rest():
            return _gather_plan((sqkv, sout, sconv), (gqkv, gout, gconv), send_rest, recv_rest, local_rest)

        def meta_copies():
            out = []
            for r in range(1, N_DEV):
                peer = (1 - px if r & 4 else px, 1 - py if r & 2 else py, 1 - pc if r & 1 else pc)
                out.append(pltpu.make_async_remote_copy(
                    src_ref=smeta,
                    dst_ref=gmeta.at[r],
                    send_sem=send_meta.at[r - 1],
                    recv_sem=recv_meta.at[r - 1],
                    device_id=peer,
                    device_id_type=pl.DeviceIdType.MESH,
                ))
            return out

        @pl.when(t == 0)
        def _():
            sbig[0:SHARD_IN, :] = win_ref[...].astype(BF16)
            sbig[SHARD_IN:, :] = jnp.zeros((SHARD_IN_PAD - SHARD_IN, d), BF16)
            smeta[...] = meta_ref[...]
            plan_in()[0]()
            for cp in meta_copies():
                cp.start()
            sqkv[...] = jnp.zeros_like(sqkv)
            sqkv[0:SHARD_Q, :] = wq_ref[...].astype(BF16)
            sqkv[SHARD_Q:, 0:SHARD_KV] = wkv_ref[...].astype(BF16)
            sout[...] = wout_ref[...].astype(BF16)
            sconv[...] = jnp.zeros_like(sconv)
            for i in range(3):
                sconv[i:i + 1, 0:SHARD_CONV] = conv_ref[i]

        def norm(h):
            hhat, _ = _rms_stats(h)
            return (hhat * g_ref[...]).astype(BF16)

        @pl.when(t < n_real)
        def _():
            b, k = tile(t)
            row0 = pl.multiple_of(b * tp + LANES + k * ts, 16)
            u_all[pl.ds(row0, ts), :] = norm(x_ref[0])

        @pl.when(t == n_real)
        def _():
            for cp in meta_copies():
                cp.wait_recv()
            gmeta[0] = smeta[...]
            for k in range(N_DEV):
                meta_f[:, SHARD_META * k:SHARD_META * (k + 1)] = gmeta[k ^ me]
            um = norm(meta_f[...])
            for b in range(nb_seq):
                u_all[b * tp:b * tp + PAD_FRONT, :] = jnp.zeros((PAD_FRONT, d), BF16)
                u_all[b * tp + PAD_FRONT:b * tp + LANES, :] = um
            u_copy.start()

        @pl.when(t == n_norm)
        def _():
            plan_in()[1]()

        @pl.when(t == n_norm + n_early)
        def _():
            plan_in()[2]()
            plan_rest()[0]()

        @pl.when(t == n_norm + (n_early + n_free) // 2)
        def _():
            plan_rest()[1]()
            plan_rest()[2]()

        @pl.when(t == n_norm + n_free)
        def _():
            plan_in()[3]()

        @pl.when(t == n_norm + n_free + 2)
        def _():
            plan_rest()[3]()

        @pl.when(t >= n_norm)
        def _():
            j = column_tile(t, order_ref)
            for jj in range(n_tiles):
                @pl.when(j == jj)
                def _(jj=jj):
                    if jj == N_A // P_TILE:
                        w_in_p[N_A % P_TILE:, :] = jnp.zeros((P_TILE - N_A % P_TILE, d), BF16)
                    for k, s0, e0, d0 in _tile_pieces(jj):
                        w_in_p[d0:d0 + e0 - s0, :] = gbig[k, s0:e0, :]
            for r in range(m // dot_rows):
                rows = slice(dot_rows * r, dot_rows * (r + 1))
                p_ref[rows, :] = _dot(u_all[rows, :], w_in_p[...], _NT).astype(BF16)

        @pl.when(t == steps - 1)
        def _():
            plan_in()[4]()
            plan_rest()[4]()
            for cp in meta_copies():
                cp.wait_send()
            u_copy.wait()
            conv_f[...] = jnp.zeros_like(conv_f)
            for k in range(N_DEV):
                for s0, e0, d0 in _q_pieces(k):
                    w_q_p[d0:d0 + e0 - s0, :] = gqkv[k, s0:e0, :]
                w_kv_p[:, _kv_dst(k):_kv_dst(k) + SHARD_KV] = gqkv[k, SHARD_Q:, 0:SHARD_KV]
                w_out_f[SHARD_OUT * k:SHARD_OUT * (k + 1), :] = gout[k]
                conv_f[0:3, SHARD_CONV * k:SHARD_CONV * (k + 1)] = gconv[k, 0:3, 0:SHARD_CONV]

    whole = lambda shape: pl.BlockSpec(shape, lambda t, o: (0,) * len(shape))
    return pl.pallas_call(
        body,
        name="prep_in_proj_gather",
        grid_spec=pltpu.PrefetchScalarGridSpec(
            num_scalar_prefetch=1,
            grid=(steps,),
            in_specs=[
                pl.BlockSpec((1, ts, d), lambda t, o: (*tile(t), 0)),
                whole(meta.shape), whole(norm_g.shape), whole(w_in_t.shape),
                whole(w_q.shape), whole(w_kv.shape), whole(w_out.shape), whole(conv_w.shape),
            ],
            out_specs=(pl.BlockSpec(memory_space=pl.ANY),
                       pl.BlockSpec((m, P_TILE), lambda t, o: (0, column_tile(t, o))),
                       pl.BlockSpec((P_TILE, d), lambda t, o: (column_tile(t, o), 0)),
                       whole((N_META, d)),
                       whole((Q_COLS, Q_RANK)), whole((KV_RANK, KV_COLS)), whole((D_MODEL, D_MODEL)),
                       whole((8, CONV_WIDTH))),
            scratch_shapes=[
                pltpu.VMEM((m, d), BF16),
                pltpu.VMEM((SHARD_IN_PAD, d), BF16),
                pltpu.VMEM((N_DEV, SHARD_IN_PAD, d), BF16),
                pltpu.VMEM((N_META, SHARD_META), F32),
                pltpu.VMEM((N_DEV, N_META, SHARD_META), F32),
                pltpu.VMEM(qkv_shape, BF16),
                pltpu.VMEM((SHARD_OUT, D_MODEL), BF16),
                pltpu.VMEM((8, LANES), F32),
                pltpu.VMEM((N_DEV,) + qkv_shape, BF16),
                pltpu.VMEM((N_DEV, SHARD_OUT, D_MODEL), BF16),
                pltpu.VMEM((N_DEV, 8, LANES), F32),
                pltpu.SemaphoreType.DMA((7,)),
                pltpu.SemaphoreType.DMA((7,)),
                pltpu.SemaphoreType.DMA((1,)),
                pltpu.SemaphoreType.DMA((N_DEV - 1,)),
                pltpu.SemaphoreType.DMA((N_DEV - 1,)),
                pltpu.SemaphoreType.DMA((21,)),
                pltpu.SemaphoreType.DMA((21,)),
                pltpu.SemaphoreType.DMA((3,)),
                pltpu.SemaphoreType.DMA,
            ],
        ),
        out_shape=(jax.ShapeDtypeStruct((m, d), BF16),
                   jax.ShapeDtypeStruct((m, IN_PAD), BF16),
                   jax.ShapeDtypeStruct((IN_PAD, d), BF16),
                   jax.ShapeDtypeStruct((N_META, d), F32),
                   jax.ShapeDtypeStruct((Q_COLS, Q_RANK), BF16),
                   jax.ShapeDtypeStruct((KV_RANK, KV_COLS), BF16),
                   jax.ShapeDtypeStruct((D_MODEL, D_MODEL), BF16),
                   jax.ShapeDtypeStruct((8, CONV_WIDTH), F32)),
        compiler_params=_params("arbitrary"),
    )(order, x, meta, norm_g, w_in_t, w_q, w_kv, w_out, conv_w)


def _rope_tables(tp):
    half = D_ROPE // 2
    inv_freq = (1.0 / (ROPE_THETA ** (np.arange(half, dtype=np.float32) / half))).astype(np.float32)
    pos = (np.arange(tp) - PAD_FRONT).astype(np.float32)
    ang = pos[:, None] * inv_freq[None, :]
    cos = np.tile(np.cos(ang), (1, LANES // half))
    sin = np.tile(np.sin(ang), (1, LANES // half))
    first = (np.arange(LANES) % D_ROPE) < half
    zero = np.float32(0.0)
    return tuple(jnp.asarray(t, F32) for t in (cos, np.where(first, -sin, zero), np.where(first, zero, sin)))


def _rope(t, cos, sa, sb):
    return t * cos + pltpu.roll(t, LANES - D_ROPE // 2, 1) * sa + pltpu.roll(t, D_ROPE // 2, 1) * sb


def _rope_t(t, cos, sa, sb):
    return t * cos + pltpu.roll(t * sa, D_ROPE // 2, 1) + pltpu.roll(t * sb, LANES - D_ROPE // 2, 1)


def _qkv_fwd(p, wq, wkv, gq, gkv, tables, nb_seq, tp):
    ht = tp // 2

    def body(pa_ref, wq_ref, wkv_ref, gq_ref, gkv_ref, cos_ref, sa_ref, sb_ref, q_ref, k_ref, v_ref):
        pa = pa_ref[...].astype(F32)
        cq_hat, _ = _rms_stats(pa[:, :Q_RANK])
        ckv_hat, _ = _rms_stats(pa[:, Q_RANK:Q_RANK + KV_RANK])
        q = _dot((cq_hat * gq_ref[...]).astype(BF16), wq_ref[...], _NT) * Q_SCALE
        kv = _dot((ckv_hat * gkv_ref[...]).astype(BF16), wkv_ref[...])
        tabs = (cos_ref[...], sa_ref[...], sb_ref[...])
        lane = lax.broadcasted_iota(jnp.int32, (ht, LANES), 1)
        low = lane < D_ROPE
        mark = lane == D_ROPE
        row = (pl.program_id(0) % 2) * ht + lax.broadcasted_iota(jnp.int32, (ht, LANES), 0)
        k_pe = jnp.where(mark & (row < PAD_FRONT), NEG_INF, _rope(pa[:, Q_RANK + KV_RANK:], *tabs))
        one = jnp.where(mark & (row >= PAD_FRONT), 1.0, 0.0)
        pairs = [_rope(q[:, N_HEADS * D_NOPE + LANES * i:N_HEADS * D_NOPE + LANES * (i + 1)], *tabs) for i in range(2)]
        for h in range(N_HEADS):
            pair = pairs[h // 2]
            if h % 2:
                pair = pltpu.roll(pair, D_ROPE, 1)
            pe = jnp.where(low, pair, one)
            q_ref[0, h] = jnp.concatenate([q[:, D_NOPE * h:D_NOPE * (h + 1)], pe], axis=1).astype(BF16)
            k_ref[0, h] = jnp.concatenate([kv[:, D_NOPE * h:D_NOPE * (h + 1)], k_pe], axis=1).astype(BF16)
            v_ref[0, h] = kv[:, N_HEADS * D_NOPE + D_V * h:N_HEADS * D_NOPE + D_V * (h + 1)].astype(BF16)

    full = lambda a: pl.BlockSpec(a.shape, lambda i: (0,) * a.ndim)
    tab = pl.BlockSpec((ht, LANES), lambda i: (i % 2, 0))
    qk = pl.BlockSpec((1, N_HEADS, ht, 2 * LANES), lambda i: (i // 2, 0, i % 2, 0))
    return pl.pallas_call(
        body,
        name="qkv_fwd",
        grid=(2 * nb_seq,),
        in_specs=[pl.BlockSpec((ht, GRP_A), lambda i: (i, 0)), full(wq), full(wkv), full(gq), full(gkv), tab, tab, tab],
        out_specs=(qk, qk, pl.BlockSpec((1, N_HEADS, ht, D_V), lambda i: (i // 2, 0, i % 2, 0))),
        out_shape=(
            jax.ShapeDtypeStruct((nb_seq, N_HEADS, tp, 2 * LANES), BF16),
            jax.ShapeDtypeStruct((nb_seq, N_HEADS, tp, 2 * LANES), BF16),
            jax.ShapeDtypeStruct((nb_seq, N_HEADS, tp, D_V), BF16),
        ),
        compiler_params=_params("parallel"),
    )(p, wq, wkv, gq, gkv, *tables)


def _attn_fwd(q, k, v, p, g_attn):
    nb_seq, _, tp, _ = q.shape

    def body(q_ref, k_ref, v_ref, z_ref, g_ref, y_ref, o_ref, lse_ref):
        g = g_ref[...]
        for r0 in range(0, tp, Q_TILE):
            nq = min(Q_TILE, tp - r0)
            kend = r0 + nq
            qq = q_ref[0, 0, r0:kend, :]
            sd = _dot(qq, k_ref[0, 0, r0:kend, :], _NT)
            causal = (lax.broadcasted_iota(jnp.int32, (nq, nq), 1) <= lax.broadcasted_iota(jnp.int32, (nq, nq), 0))
            sd = jnp.where(causal, sd, NEG_INF)
            m = jnp.max(sd, axis=-1, keepdims=True)
            if r0:
                so = _dot(qq, k_ref[0, 0, 0:r0, :], _NT)
                m = jnp.maximum(m, jnp.max(so, axis=-1, keepdims=True))
            ed = jnp.exp2(sd - m)
            l = jnp.sum(ed, axis=-1, keepdims=True)
            o = _dot(ed.astype(BF16), v_ref[0, 0, r0:kend, :])
            if r0:
                eo = jnp.exp2(so - m)
                l = l + jnp.sum(eo, axis=-1, keepdims=True)
                o = o + _dot(eo.astype(BF16), v_ref[0, 0, 0:r0, :])
            o = o * (1.0 / l)
            o_ref[0, 0, r0:kend, :] = o
            lse_ref[0, 0, r0:kend, :] = jnp.broadcast_to(m + jnp.log2(l), (nq, LANES))
            ohat, _ = _rms_stats(o)
            z = z_ref[r0:kend, :].astype(F32)
            y_ref[r0:kend, :] = (ohat * g * (z * _sigmoid(z))).astype(BF16)

    qk = pl.BlockSpec((1, 1, tp, 2 * LANES), lambda b, h: (b, h, 0, 0))
    hv = pl.BlockSpec((1, 1, tp, D_V), lambda b, h: (b, h, 0, 0))
    return pl.pallas_call(
        body,
        name="attn_fwd",
        grid=(nb_seq, N_HEADS),
        in_specs=[qk, qk, hv,
                  pl.BlockSpec((tp, LANES), lambda b, h: (b, GRP_A // LANES + h)),
                  pl.BlockSpec((1, LANES), lambda b, h: (0, h))],
        out_specs=(pl.BlockSpec((tp, LANES), lambda b, h: (b, h)), hv, hv),
        out_shape=(
            jax.ShapeDtypeStruct((nb_seq * tp, N_HEADS * D_V), BF16),
            jax.ShapeDtypeStruct((nb_seq, N_HEADS, tp, D_V), F32),
            jax.ShapeDtypeStruct((nb_seq, N_HEADS, tp, LANES), F32),
        ),
        compiler_params=_params("parallel", "parallel"),
    )(q, k, v, p, g_attn)


_CONV_COL0 = (GRP_A + N_HEADS * D_V) // LANES


def _conv_specs(tp, order):
    cols = CONV_WIDTH // LANES
    return [pl.BlockSpec((tp, LANES), functools.partial(
        lambda a, b, off: order(a, b, off), off=_CONV_COL0 + i * cols)) for i in range(4)]


def _conv_fwd(p, conv_w, g_conv, nb_seq, tp):
    def body(b_ref, c_ref, h_ref, z_ref, w_ref, g_ref, y_ref):
        cc = c_ref[...].astype(F32) * h_ref[...].astype(F32)
        row = lax.broadcasted_iota(jnp.int32, (tp, LANES), 0)
        s1 = jnp.where(row >= 1, pltpu.roll(cc, 1, 0), 0.0)
        s2 = jnp.where(row >= 2, pltpu.roll(cc, 2, 0), 0.0)
        yc = b_ref[...].astype(F32) * (w_ref[0:1, :] * s2 + w_ref[1:2, :] * s1 + w_ref[2:3, :] * cc)
        r = lax.rsqrt(_group_mean(yc * yc) + EPS)
        z = z_ref[...].astype(F32)
        y_ref[...] = (yc * r * g_ref[...] * (z * _sigmoid(z))).astype(BF16)

    return pl.pallas_call(
        body,
        name="conv_fwd",
        grid=(nb_seq, CONV_WIDTH // LANES),
        in_specs=_conv_specs(tp, lambda b, t, off: (b, off + t)) + [
            pl.BlockSpec((8, LANES), lambda b, t: (0, t)),
            pl.BlockSpec((1, LANES), lambda b, t: (0, t))],
        out_specs=pl.BlockSpec((tp, LANES), lambda b, t: (b, t)),
        out_shape=jax.ShapeDtypeStruct((nb_seq * tp, CONV_WIDTH), BF16),
        compiler_params=_params("parallel", "parallel"),
    )(p, p, p, p, conv_w, g_conv)


def _token_copy(hbm, b, k, ts, buf, sem, to_hbm=False):
    lo, hi = max(k * ts - LANES, 0), (k + 1) * ts - LANES
    off = lo - (k * ts - LANES)
    src, dst = hbm.at[b, pl.ds(lo, hi - lo)], buf.at[pl.ds(off, hi - lo)]
    if to_hbm:
        src, dst = dst, src
    return pltpu.make_async_copy(src, dst, sem)


def _for_tile(k, nt, fn):
    for kk in range(nt):
        @pl.when(k == kk)
        def _(kk=kk):
            fn(kk)


def _out_proj_loss(ya, yc, w_out, x, target, g_final, nt):
    nb_seq, s, d = x.shape
    r, ka = ya.shape
    ts = (s + LANES) // nt
    steps = nb_seq * nt

    def body(a_ref, c_ref, w_ref, x_hbm, t_hbm, g_ref, dhb_ref, dg_ref, loss_ref,
             xbuf, tbuf, acc_ref, sems):
        i = pl.program_id(0)
        b, k = i // nt, i % nt

        @pl.when(i == 0)
        def _():
            acc_ref[...] = jnp.zeros_like(acc_ref)
            dg_ref[...] = jnp.zeros_like(dg_ref)

        slot = i % 2

        def fetch(seq, kk, sl):
            return [_token_copy(x_hbm, seq, kk, ts, xbuf.at[sl], sems.at[sl, 0]),
                    _token_copy(t_hbm, seq, kk, ts, tbuf.at[sl], sems.at[sl, 1])]

        def start(seq, sl, kk):
            if kk == 0:
                xbuf[sl, 0:LANES, :] = jnp.zeros((LANES, d), F32)
                tbuf[sl, 0:LANES, :] = jnp.zeros((LANES, d), F32)
            for cp in fetch(seq, kk, sl):
                cp.start()

        @pl.when(i == 0)
        def _():
            start(0, 0, 0)

        @pl.when(i + 1 < steps)
        def _():
            _for_tile((i + 1) % nt, nt, functools.partial(start, (i + 1) // nt, 1 - slot))

        mix = _dot(a_ref[...], w_ref[0:ka, :]) + _dot(c_ref[...], w_ref[ka:, :])
        _for_tile(k, nt, lambda kk: [cp.wait() for cp in fetch(b, kk, slot)])

        real = (lax.broadcasted_iota(jnp.int32, (ts, d), 0) >= LANES) | (k > 0)
        g = g_ref[...]
        hhat, rstd = _rms_stats(xbuf[slot] + mix)
        e = jnp.where(real, hhat * g - tbuf[slot], 0.0)
        acc_ref[...] += jnp.sum(e * e, axis=0, keepdims=True)
        dy = e * (1.0 / d)
        dg_ref[...] += jnp.sum(dy * hhat, axis=0, keepdims=True)
        dhb_ref[...] = _rms_bwd(g * dy, hhat, rstd).astype(BF16)

        @pl.when(i == steps - 1)
        def _():
            total = jnp.sum(acc_ref[...], axis=1, keepdims=True)
            loss_ref[...] = jnp.broadcast_to((0.5 / d) * total, loss_ref.shape)

    hbm = pl.BlockSpec(memory_space=pl.ANY)
    row = pl.BlockSpec((ts, d), lambda i: (i, 0))
    vec = pl.BlockSpec((1, d), lambda i: (0, 0))
    return pl.pallas_call(
        body,
        name="out_proj_loss",
        grid=(steps,),
        in_specs=[pl.BlockSpec((ts, ka), lambda i: (i, 0)), pl.BlockSpec((ts, yc.shape[1]), lambda i: (i, 0)),
                  pl.BlockSpec(w_out.shape, lambda i: (0, 0)), hbm, hbm, vec],
        out_specs=(row, vec, pl.BlockSpec((1, LANES), lambda i: (0, 0))),
        out_shape=(
            jax.ShapeDtypeStruct((r, d), BF16),
            jax.ShapeDtypeStruct((1, d), F32),
            jax.ShapeDtypeStruct((1, LANES), F32),
        ),
        scratch_shapes=[pltpu.VMEM((2, ts, d), F32), pltpu.VMEM((2, ts, d), F32), pltpu.VMEM((1, d), F32),
                        pltpu.SemaphoreType.DMA((2, 2))],
        compiler_params=_params("arbitrary"),
    )(ya, yc, w_out, x, target, g_final)


def _out_proj_bwd(dhb, w_out, ya, yc, bm):
    r, d = dhb.shape
    ka = ya.shape[1]
    n_mix = w_out.shape[0]
    last = r // bm - 1

    def body(dh_ref, w_ref, a_ref, c_ref, dcat_ref, dw_ref, acc_ref):
        @pl.when(pl.program_id(0) == 0)
        def _():
            acc_ref[...] = jnp.zeros_like(acc_ref)

        dh = dh_ref[...]
        dcat_ref[...] = _dot(dh, w_ref[...], _NT).astype(BF16)
        acc_ref[0:ka, :] += _dot(a_ref[...], dh, _TN)
        acc_ref[ka:, :] += _dot(c_ref[...], dh, _TN)

        @pl.when(pl.program_id(0) == last)
        def _():
            dw_ref[...] = acc_ref[...].astype(BF16)

    return pl.pallas_call(
        body,
        name="out_proj_bwd",
        grid=(r // bm,),
        in_specs=[pl.BlockSpec((bm, d), lambda i: (i, 0)), pl.BlockSpec(w_out.shape, lambda i: (0, 0)),
                  pl.BlockSpec((bm, ka), lambda i: (i, 0)), pl.BlockSpec((bm, yc.shape[1]), lambda i: (i, 0))],
        out_specs=(pl.BlockSpec((bm, n_mix), lambda i: (i, 0)),
                   pl.BlockSpec((n_mix, d), lambda i: (0, 0))),
        out_shape=(jax.ShapeDtypeStruct((r, n_mix), BF16),
                   jax.ShapeDtypeStruct((n_mix, d), BF16)),
        scratch_shapes=[pltpu.VMEM((n_mix, d), F32)],
        compiler_params=_params("arbitrary"),
    )(dhb, w_out, ya, yc)


def _attn_bwd(q, k, v, o, lse, dcat, p, g_attn):
    nb_seq, _, tp, _ = q.shape

    def body(q_ref, k_ref, v_ref, o_ref, lse_ref, dy_ref, z_ref, g_ref,
             dq_ref, dk_ref, dv_ref, dz_ref, dg_ref, dq_acc):
        @pl.when(pl.program_id(1) == 0)
        def _():
            dg_ref[...] = jnp.zeros_like(dg_ref)

        g = g_ref[...]
        z = z_ref[...].astype(F32)
        o = o_ref[0, 0]
        dy = dy_ref[...].astype(F32)
        sig = _sigmoid(z)
        ohat, r = _rms_stats(o)
        don = dy * (z * sig)
        dz_ref[...] = (dy * (ohat * g) * (sig * (1.0 + z * (1.0 - sig)))).astype(BF16)
        dg_ref[...] += jnp.sum(don * ohat, axis=0, keepdims=True)
        do = _rms_bwd(g * don, ohat, r)
        dvec = jnp.sum(do * o, axis=-1, keepdims=True)
        dob = do.astype(BF16)
        lse_col = lse_ref[0, 0, :, 0:1]
        dq_acc[...] = jnp.zeros_like(dq_acc)
        for k0 in range(0, tp, K_TILE):
            nk = min(K_TILE, tp - k0)
            nq = tp - k0
            qq = q_ref[0, 0, k0:, :]
            kk = k_ref[0, 0, k0:k0 + nk, :]
            causal = (lax.broadcasted_iota(jnp.int32, (nq, nk), 1) <= lax.broadcasted_iota(jnp.int32, (nq, nk), 0))
            pr = jnp.where(causal, jnp.exp2(_dot(qq, kk, _NT) - lse_col[k0:]), 0.0)
            dp = _dot(dob[k0:], v_ref[0, 0, k0:k0 + nk, :], _NT)
            ds = (pr * (dp - dvec[k0:])).astype(BF16)
            dv_ref[0, 0, k0:k0 + nk, :] = _dot(pr.astype(BF16), dob[k0:], _TN).astype(BF16)
            dk_ref[0, 0, k0:k0 + nk, :] = (_dot(ds, qq, _TN) * (ATTN_SCALE / Q_SCALE)).astype(BF16)
            dq_acc[k0:, :] += _dot(ds, kk)
        dq_ref[0, 0] = (dq_acc[...] * ATTN_SCALE).astype(BF16)

    qk = pl.BlockSpec((1, 1, tp, 2 * LANES), lambda h, b: (b, h, 0, 0))
    hv = pl.BlockSpec((1, 1, tp, D_V), lambda h, b: (b, h, 0, 0))
    col = pl.BlockSpec((tp, LANES), lambda h, b: (b, h))
    return pl.pallas_call(
        body,
        name="attn_bwd",
        grid=(N_HEADS, nb_seq),
        in_specs=[qk, qk, hv, hv, hv, col,
                  pl.BlockSpec((tp, LANES), lambda h, b: (b, GRP_A // LANES + h)),
                  pl.BlockSpec((1, LANES), lambda h, b: (0, h))],
        out_specs=(qk, qk, hv, col, pl.BlockSpec((1, LANES), lambda h, b: (0, h))),
        out_shape=(
            jax.ShapeDtypeStruct((nb_seq, N_HEADS, tp, 2 * LANES), BF16),
            jax.ShapeDtypeStruct((nb_seq, N_HEADS, tp, 2 * LANES), BF16),
            jax.ShapeDtypeStruct((nb_seq, N_HEADS, tp, D_V), BF16),
            jax.ShapeDtypeStruct((nb_seq * tp, N_HEADS * D_V), BF16),
            jax.ShapeDtypeStruct((1, N_HEADS * D_V), F32),
        ),
        scratch_shapes=[pltpu.VMEM((tp, 2 * LANES), F32)],
        compiler_params=_params("arbitrary", "arbitrary"),
    )(q, k, v, o, lse, dcat, p, g_attn)


def _qkv_bwd(p, dq, dk, dv, wq, wkv, gq, gkv, tables):
    nb_seq, _, tp, _ = dq.shape
    ht = tp // 2

    def body(pa_ref, dq_ref, dk_ref, dv_ref, wq_ref, wkv_ref, gq_ref, gkv_ref, cos_ref, sa_ref, sb_ref,
             dpa_ref, dwq_ref, dwkv_ref, dgq_ref, dgkv_ref):
        @pl.when(pl.program_id(0) == 0)
        def _():
            dwq_ref[...] = jnp.zeros_like(dwq_ref)
            dwkv_ref[...] = jnp.zeros_like(dwkv_ref)
            dgq_ref[...] = jnp.zeros_like(dgq_ref)
            dgkv_ref[...] = jnp.zeros_like(dgkv_ref)

        pa = pa_ref[...].astype(F32)
        gq, gkv = gq_ref[...], gkv_ref[...]
        cq_hat, rq = _rms_stats(pa[:, :Q_RANK])
        ckv_hat, rkv = _rms_stats(pa[:, Q_RANK:Q_RANK + KV_RANK])
        tabs = (cos_ref[...], sa_ref[...], sb_ref[...])

        pe = [dq_ref[0, h, :, D_NOPE:].astype(F32) for h in range(N_HEADS)]
        pairs = [_rope_t(pe[2 * i] + pltpu.roll(pe[2 * i + 1], D_ROPE, 1), *tabs).astype(BF16) for i in range(2)]
        dq_flat = jnp.concatenate([dq_ref[0, h, :, :D_NOPE] for h in range(N_HEADS)] + pairs, axis=1)
        dwq_ref[...] += _dot(dq_flat, (cq_hat * gq).astype(BF16), _TN)
        dcqn = _dot(dq_flat, wq_ref[...])
        dgq_ref[...] += jnp.sum(dcqn * cq_hat, axis=0, keepdims=True)
        dcq = _rms_bwd(gq * dcqn, cq_hat, rq)

        dkv_flat = jnp.concatenate([dk_ref[0, h, :, :D_NOPE] for h in range(N_HEADS)]
                                   + [dv_ref[0, h] for h in range(N_HEADS)], axis=1)
        dwkv_ref[...] += _dot((ckv_hat * gkv).astype(BF16), dkv_flat, _TN)
        dckvn = _dot(dkv_flat, wkv_ref[...], _NT)
        dgkv_ref[...] += jnp.sum(dckvn * ckv_hat, axis=0, keepdims=True)
        dckv = _rms_bwd(gkv * dckvn, ckv_hat, rkv)

        dk_pe = dk_ref[0, 0, :, D_NOPE:].astype(F32)
        for h in range(1, N_HEADS):
            dk_pe = dk_pe + dk_ref[0, h, :, D_NOPE:].astype(F32)
        dk_pe = jnp.where(lax.broadcasted_iota(jnp.int32, (ht, LANES), 1) < D_ROPE, dk_pe, 0.0)
        dpa_ref[...] = jnp.concatenate([dcq, dckv, _rope_t(dk_pe, *tabs)], axis=1).astype(BF16)

    full = lambda a: pl.BlockSpec(a.shape, lambda i: (0,) * a.ndim)
    tab = pl.BlockSpec((ht, LANES), lambda i: (i % 2, 0))
    qk = pl.BlockSpec((1, N_HEADS, ht, 2 * LANES), lambda i: (i // 2, 0, i % 2, 0))
    acc = lambda shape: pl.BlockSpec(shape, lambda i: (0, 0))
    return pl.pallas_call(
        body,
        name="qkv_bwd",
        grid=(2 * nb_seq,),
        in_specs=[pl.BlockSpec((ht, GRP_A), lambda i: (i, 0)), qk, qk,
                  pl.BlockSpec((1, N_HEADS, ht, D_V), lambda i: (i // 2, 0, i % 2, 0)),
                  full(wq), full(wkv), full(gq), full(gkv), tab, tab, tab],
        out_specs=(pl.BlockSpec((ht, GRP_A), lambda i: (i, 0)),
                   acc(wq.shape), acc(wkv.shape), acc((1, Q_RANK)), acc((1, KV_RANK))),
        out_shape=(
            jax.ShapeDtypeStruct((nb_seq * tp, GRP_A), BF16),
            jax.ShapeDtypeStruct(wq.shape, F32),
            jax.ShapeDtypeStruct(wkv.shape, F32),
            jax.ShapeDtypeStruct((1, Q_RANK), F32),
            jax.ShapeDtypeStruct((1, KV_RANK), F32),
        ),
        compiler_params=_params("arbitrary"),
    )(p, dq, dk, dv, wq, wkv, gq, gkv, *tables)


def _conv_bwd(p, dcat, conv_w, g_conv, nb_seq, tp):
    cols = CONV_WIDTH // LANES

    def body(b_ref, c_ref, h_ref, z_ref, dy_ref, w_ref, g_ref,
             db_ref, dc_ref, dh_ref, dz_ref, dw_ref, dg_ref):
        @pl.when(pl.program_id(1) == 0)
        def _():
            dw_ref[...] = jnp.zeros_like(dw_ref)
            dg_ref[...] = jnp.zeros_like(dg_ref)

        cb, c, h = b_ref[...].astype(F32), c_ref[...].astype(F32), h_ref[...].astype(F32)
        z, dy = z_ref[...].astype(F32), dy_ref[...].astype(F32)
        g = g_ref[...]
        w0, w1, w2 = w_ref[0:1, :], w_ref[1:2, :], w_ref[2:3, :]
        cc = c * h
        row = lax.broadcasted_iota(jnp.int32, (tp, LANES), 0)
        s1 = jnp.where(row >= 1, pltpu.roll(cc, 1, 0), 0.0)
        s2 = jnp.where(row >= 2, pltpu.roll(cc, 2, 0), 0.0)
        dwc = w0 * s2 + w1 * s1 + w2 * cc
        yc = cb * dwc
        r = lax.rsqrt(_group_mean(yc * yc) + EPS)
        ychat = yc * r
        sig = _sigmoid(z)
        dz_ref[...] = (dy * (ychat * g) * (sig * (1.0 + z * (1.0 - sig)))).astype(BF16)
        dyn = dy * (z * sig)
        dg_ref[...] += jnp.sum(dyn * ychat, axis=0, keepdims=True)
        gd = g * dyn
        dyc = r * (gd - ychat * _group_mean(gd * ychat))
        db_ref[...] = (dyc * dwc).astype(BF16)
        ddw = dyc * cb
        dw_ref[0:1, :] += jnp.sum(ddw * s2, axis=0, keepdims=True)
        dw_ref[1:2, :] += jnp.sum(ddw * s1, axis=0, keepdims=True)
        dw_ref[2:3, :] += jnp.sum(ddw * cc, axis=0, keepdims=True)
        u1 = jnp.where(row <= tp - 2, pltpu.roll(ddw, tp - 1, 0), 0.0)
        u2 = jnp.where(row <= tp - 3, pltpu.roll(ddw, tp - 2, 0), 0.0)
        dcc = w2 * ddw + w1 * u1 + w0 * u2
        dc_ref[...] = (dcc * h).astype(BF16)
        dh_ref[...] = (dcc * c).astype(BF16)

    col = pl.BlockSpec((tp, LANES), lambda t, b: (b, t))
    out = jax.ShapeDtypeStruct((nb_seq * tp, CONV_WIDTH), BF16)
    return pl.pallas_call(
        body,
        name="conv_bwd",
        grid=(cols, nb_seq),
        in_specs=_conv_specs(tp, lambda t, b, off: (b, off + t)) + [
            pl.BlockSpec((tp, LANES), lambda t, b: (b, N_HEADS * D_V // LANES + t)),
            pl.BlockSpec((8, LANES), lambda t, b: (0, t)),
            pl.BlockSpec((1, LANES), lambda t, b: (0, t))],
        out_specs=(col, col, col, col,
                   pl.BlockSpec((8, LANES), lambda t, b: (0, t)), pl.BlockSpec((1, LANES), lambda t, b: (0, t))),
        out_shape=(out, out, out, out,
                   jax.ShapeDtypeStruct((8, CONV_WIDTH), F32), jax.ShapeDtypeStruct((1, CONV_WIDTH), F32)),
        compiler_params=_params("arbitrary", "arbitrary"),
    )(p, p, p, p, dcat, conv_w, g_conv)


def _input_bwd(dps, w_in, x, meta, dh, norm_g, nt, send_in):
    nb_seq, s, d = x.shape
    r, kb = dps[0].shape
    ts = (s + LANES) // nt
    steps = nb_seq * nt
    n_dp = len(dps)
    in_slot = send_in.shape[1:]

    def body(*refs):
        dp_refs, w_ref, x_hbm, meta_ref, dh_ref, g_ref, pay_ref = refs[:n_dp], *refs[n_dp:n_dp + 6]
        o = n_dp + 6
        gx_hbm, dmeta_ref, dg_ref, r2_in = refs[o:o + 4]
        xbuf, gxbuf, tok_sems, own_in, r1_in, sum_in = refs[o + 4:o + 10]
        sems = refs[o + 10:]
        i = pl.program_id(0)
        b, k = i // nt, i % nt

        def plan():
            return _reduce_plan((pay_ref,), (own_in,), (r1_in,), (sum_in,), (r2_in,), *sems)

        @pl.when(i == 0)
        def _():
            dmeta_ref[...] = jnp.zeros_like(dmeta_ref)
            dg_ref[...] = jnp.zeros_like(dg_ref)
            plan()[0]()

        @pl.when(i == 1)
        def _():
            plan()[1]()

        def start(kk):
            if kk == 0:
                xbuf[0:PAD_FRONT, :] = jnp.zeros((PAD_FRONT, d), F32)
                xbuf[PAD_FRONT:LANES, :] = meta_ref[...]
            _token_copy(x_hbm, b, kk, ts, xbuf, tok_sems.at[0]).start()

        _for_tile(k, nt, start)
        du = _dot(dp_refs[0][...], w_ref[0:kb, :])
        for j in range(1, n_dp):
            du = du + _dot(dp_refs[j][...], w_ref[kb * j:kb * (j + 1), :])
        _for_tile(k, nt, lambda kk: _token_copy(x_hbm, b, kk, ts, xbuf, tok_sems.at[0]).wait())

        g = g_ref[...]
        hhat, rstd = _rms_stats(xbuf[...])
        dg_ref[...] += jnp.sum(du * hhat, axis=0, keepdims=True)
        res = _rms_bwd(g * du, hhat, rstd) + dh_ref[...].astype(F32)

        @pl.when(i > 0)
        def _():
            _for_tile(k, nt, lambda kk: _token_copy(gx_hbm, b, (kk - 1) % nt, ts, gxbuf, tok_sems.at[1], True).wait())

        gxbuf[...] = res

        @pl.when(k == 0)
        def _():
            dmeta_ref[...] += gxbuf[PAD_FRONT:LANES, :]

        _for_tile(k, nt, lambda kk: _token_copy(gx_hbm, b, kk, ts, gxbuf, tok_sems.at[1], True).start())

        @pl.when(i == steps - 1)
        def _():
            _token_copy(gx_hbm, b, nt - 1, ts, gxbuf, tok_sems.at[1], True).wait()
            plan()[2]()

    whole = lambda a: pl.BlockSpec(a.shape, lambda i: (0,) * a.ndim)
    hbm = pl.BlockSpec(memory_space=pl.ANY)
    return pl.pallas_call(
        body,
        name="input_bwd",
        grid=(steps,),
        in_specs=[pl.BlockSpec((ts, kb), lambda i: (i, 0)) for _ in dps]
        + [whole(w_in), hbm, whole(meta), pl.BlockSpec((ts, d), lambda i: (i, 0)), whole(norm_g), hbm],
        out_specs=(hbm, pl.BlockSpec((N_META, d), lambda i: (0, 0)), pl.BlockSpec((1, d), lambda i: (0, 0)), hbm),
        out_shape=(jax.ShapeDtypeStruct((nb_seq, s, d), F32),
                   jax.ShapeDtypeStruct((N_META, d), F32),
                   jax.ShapeDtypeStruct((1, d), F32),
                   jax.ShapeDtypeStruct((N_CHIPS,) + in_slot, BF16)),
        scratch_shapes=[pltpu.VMEM((ts, d), F32), pltpu.VMEM((ts, d), F32), pltpu.SemaphoreType.DMA((2,))]
        + _reduce_scratch([(in_slot, BF16)], [True]),
        compiler_params=_params("arbitrary"),
    )(*dps, w_in, x, meta, dh, norm_g, send_in)


def _in_proj_bwd_w(u, dps, bm, small_grads, send_out):
    r, d = u.shape
    kb = dps[0].shape[1]
    steps = r // bm
    n_dp, n_small = len(dps), len(small_grads)
    out_slot, small_slot = send_out.shape[1:], (SMALL_ROWS, LANES)

    def body(*refs):
        u_ref, dp_refs = refs[0], refs[1:1 + n_dp]
        small_refs = refs[1 + n_dp:1 + n_dp + n_small]
        o = 1 + n_dp + n_small
        pay_out, o_ref, r2_out, r2_small = refs[o:o + 4]
        acc_ref, ssmall, r1_out, sum_out, r1_small, sum_small = refs[o + 4:o + 10]
        sems = refs[o + 10:]
        i = pl.program_id(0)

        def plan():
            return _reduce_plan((pay_out, ssmall), (None, None), (r1_out, r1_small), (sum_out, sum_small),
                                (r2_out, r2_small), *sems)

        @pl.when(i == 0)
        def _():
            acc_ref[...] = jnp.zeros_like(acc_ref)
            _pack_small(ssmall, *small_refs)
            plan()[0]()

        @pl.when(i == 1)
        def _():
            plan()[1]()

        uu = u_ref[...]
        for j in range(n_dp):
            acc_ref[kb * j:kb * (j + 1), :] += _dot(dp_refs[j][...], uu, _TN)

        @pl.when(i == steps - 1)
        def _():
            for k in range(N_DEV):
                for s, e, c0 in _in_pieces(k):
                    o_ref[k, s:e, :] = acc_ref[c0:c0 + e - s, :].astype(BF16)
                o_ref[k, SHARD_IN:, :] = jnp.zeros((SHARD_IN_PAD - SHARD_IN, d), BF16)
            plan()[2]()

    whole = lambda a: pl.BlockSpec(a.shape, lambda i: (0,) * a.ndim)
    hbm = pl.BlockSpec(memory_space=pl.ANY)
    return pl.pallas_call(
        body,
        name="in_proj_bwd_w",
        grid=(steps,),
        in_specs=[pl.BlockSpec((bm, d), lambda i: (i, 0))]
        + [pl.BlockSpec((bm, kb), lambda i: (i, 0)) for _ in dps] + [whole(a) for a in small_grads]
        + [whole(send_out)],
        out_specs=(pl.BlockSpec((N_DEV, SHARD_IN_PAD, d), lambda i: (0, 0, 0)), hbm, hbm),
        out_shape=(jax.ShapeDtypeStruct((N_DEV, SHARD_IN_PAD, d), BF16),
                   jax.ShapeDtypeStruct((N_CHIPS,) + out_slot, BF16),
                   jax.ShapeDtypeStruct((N_CHIPS,) + small_slot, F32)),
        scratch_shapes=[pltpu.VMEM((kb * n_dp, d), F32), pltpu.VMEM((N_DEV,) + small_slot, F32)]
        + _reduce_scratch([(out_slot, BF16), (small_slot, F32)], [False, False]),
        compiler_params=_params("arbitrary"),
    )(u, *dps, *small_grads, send_out)


def _local_step(x, loss_target, u, p, meta_f, norm_g, w_in_p, q_norm_g, w_q_p, kv_norm_g, w_kv_p, conv_w_f,
                attn_out_g, conv_out_g, w_out_f, g_final):
    nb_seq, s, d = x.shape
    tp = s + LANES
    ht = tp // 2
    tables = _rope_tables(tp)

    q, k, v = _qkv_fwd(p, w_q_p, w_kv_p, q_norm_g, kv_norm_g, tables, nb_seq, tp)
    ya, o, lse = _attn_fwd(q, k, v, p, attn_out_g)
    yc = _conv_fwd(p, conv_w_f, conv_out_g, nb_seq, tp)
    dhb, d_final_g, loss_part = _out_proj_loss(ya, yc, w_out_f, x, loss_target, g_final, TOKEN_TILES)

    dcat, d_w_out = _out_proj_bwd(dhb, w_out_f, ya, yc, ht)
    send_out = d_w_out.reshape(N_DEV, SHARD_OUT, d)
    dq, dk, dv, dz_attn, d_attn_g = _attn_bwd(q, k, v, o, lse, dcat, p, attn_out_g)
    dpa, d_wq_p, d_wkv_p, d_gq, d_gkv = _qkv_bwd(p, dq, dk, dv, w_q_p, w_kv_p, q_norm_g, kv_norm_g, tables)
    d_b, d_c, d_h, dz_conv, d_conv_w, d_conv_g = _conv_bwd(p, dcat, conv_w_f, conv_out_g, nb_seq, tp)
    dps = (dpa, dz_attn, d_b, d_c, d_h, dz_conv)
    small = (d_wq_p, d_wkv_p, d_conv_w, d_final_g, d_gq, d_gkv, d_attn_g, d_conv_g, loss_part)
    send_in, r_out, r_small = _in_proj_bwd_w(u, dps, ht, small, send_out)
    grad_x, d_meta, d_norm_g, r_in = _input_bwd(dps, w_in_p, x, meta_f, dhb, norm_g, TOKEN_TILES, send_in)
    return grad_x, r_in, r_out, r_small, d_meta, d_norm_g


def kernel(x, meta_tokens, norm_g, w_in, q_norm_g, w_q_up, kv_norm_g, w_kv_up, conv_w, attn_out_g, conv_out_g, w_out, final_norm_g, loss_target, m_meta_tokens, m_norm_g, m_w_in, m_q_norm_g, m_w_q_up, m_kv_norm_g, m_w_kv_up, m_conv_w, m_attn_out_g, m_conv_out_g, m_w_out, m_final_norm_g, v_meta_tokens, v_norm_g, v_w_in, v_q_norm_g, v_w_q_up, v_kv_norm_g, v_w_kv_up, v_conv_w, v_attn_out_g, v_conv_out_g, v_w_out, v_final_norm_g):
    d = x.shape[-1]
    order = jnp.asarray(_tile_orders()[0].reshape(-1))
    u, p, w_in_p, meta_f, w_q_p, w_kv_p, w_out_f, conv_w_f = _prep_in_proj(
        x, meta_tokens, norm_g, w_in[0].T, w_q_up[0].T, w_kv_up[0], w_out[0], conv_w.transpose(1, 0, 2), order)
    g_final = final_norm_g.reshape(1, d)
    grad_x, r_in, r_out, r_small, d_meta, d_norm_g = _local_step(
        x, loss_target, u, p, meta_f, norm_g, w_in_p, q_norm_g, w_q_p, kv_norm_g, w_kv_p, conv_w_f,
        attn_out_g, conv_out_g, w_out_f, g_final)

    flat = lambda a: a.reshape(a.shape[-2:]) if a.ndim == 3 else a.reshape(1, -1) if a.ndim == 1 else a
    transposed = ("w_in", "w_q_up")

    def to_kernel(n, a):
        if n == "conv_w":
            return a.transpose(1, 0, 2)
        return flat(a).T if n in transposed else flat(a)

    def from_kernel(n, a, shape):
        if n == "conv_w":
            return a.transpose(1, 0, 2)
        return (a.T if n in transposed else a).reshape(shape)
    params = {
        "meta_tokens": (meta_tokens, m_meta_tokens, v_meta_tokens),
        "norm_g": (norm_g, m_norm_g, v_norm_g),
        "w_in": (w_in, m_w_in, v_w_in),
        "q_norm_g": (q_norm_g, m_q_norm_g, v_q_norm_g),
        "w_q_up": (w_q_up, m_w_q_up, v_w_q_up),
        "kv_norm_g": (kv_norm_g, m_kv_norm_g, v_kv_norm_g),
        "w_kv_up": (w_kv_up, m_w_kv_up, v_w_kv_up),
        "conv_w": (conv_w, m_conv_w, v_conv_w),
        "attn_out_g": (attn_out_g, m_attn_out_g, v_attn_out_g),
        "conv_out_g": (conv_out_g, m_conv_out_g, v_conv_out_g),
        "w_out": (w_out, m_w_out, v_w_out),
        "final_norm_g": (final_norm_g, m_final_norm_g, v_final_norm_g),
    }
    grads, loss = _reduce_tail(r_in, r_out, r_small, d_meta, d_norm_g)
    updated = _adamw(grads, {n: tuple(to_kernel(n, a) for a in t) for n, t in params.items()})
    outs = [[from_kernel(n, updated[n][i], params[n][0].shape) for n, _ in PARAM_SHAPES] for i in range(4)]
    return (loss[0, 0], grad_x, *outs[0], *outs[1], *outs[2], *outs[3])
```

```python
import functools

import jax
import jax.numpy as jnp
import numpy as np
from jax import lax
from jax.experimental import pallas as pl
from jax.experimental.pallas import tpu as pltpu

F32 = jnp.float32
BF16 = jnp.bfloat16

N_META = 16
D_MODEL = 1024
N_HEADS = 4
D_NOPE = 128
D_ROPE = 64
D_V = 128
Q_RANK = 256
KV_RANK = 128
CONV_WIDTH = 512
CONV_GROUP = 64
ROPE_THETA = 10000.0
ATTN_SCALE = (D_NOPE + D_ROPE) ** -0.5
Q_SCALE = ATTN_SCALE * 1.4426950408889634
EPS = 1e-6
NEG_INF = -1e30

ADAM_LR = 0.001
ADAM_B1 = 0.9
ADAM_B2 = 0.999
ADAM_EPS = 1e-08
ADAM_WD = 0.01
ADAM_STEP = 10

LANES = 128
PAD_FRONT = LANES - N_META
K_TILE = 256
Q_TILE = 512
N_DEV = 8
VMEM_LIMIT = 56 * 1024 * 1024

IN_PAD = 3072
GRP_A = 512
N_A = Q_RANK + KV_RANK + D_ROPE
IN_PROJ = 3008
SHARD_IN = IN_PROJ // N_DEV
SHARD_IN_PAD = 384
SHARD_Q = 96
SHARD_KV = 128
SHARD_OUT = 128
SHARD_CONV = 64
SHARD_META = 128
Q_COLS = N_HEADS * (D_NOPE + D_ROPE)
KV_COLS = N_HEADS * (D_NOPE + D_V)

ROW_Q, ROW_KV, ROW_META, ROW_CONV = 0, 256, 384, 400
ROW_REPL = 408
ROW_NORM, ROW_FINAL, ROW_GQ, ROW_GKV, ROW_ATTN, ROW_CONVG, ROW_LOSS = 408, 416, 424, 426, 427, 431, 435
SMALL_ROWS = 440

PARAM_SHAPES = (
    ("meta_tokens", (N_META, SHARD_META)), ("norm_g", (1, D_MODEL)), ("w_in", (SHARD_IN, D_MODEL)),
    ("q_norm_g", (1, Q_RANK)), ("w_q_up", (SHARD_Q, Q_RANK)), ("kv_norm_g", (1, KV_RANK)),
    ("w_kv_up", (KV_RANK, SHARD_KV)), ("conv_w", (3, 1, SHARD_CONV)), ("attn_out_g", (1, CONV_WIDTH)),
    ("conv_out_g", (1, CONV_WIDTH)), ("w_out", (SHARD_OUT, D_MODEL)), ("final_norm_g", (1, D_MODEL)),
)


def _in_pieces(k):
    lo, hi = SHARD_IN * k, SHARD_IN * (k + 1)
    out = []
    if lo < N_A:
        out.append((0, min(hi, N_A) - lo, lo))
    if hi > N_A:
        s = max(lo, N_A)
        out.append((s - lo, hi - lo, s + GRP_A - N_A))
    return out


P_TILE = 256


def _tile_pieces(j):
    lo, hi = P_TILE * j, P_TILE * (j + 1)
    out = []
    for k in range(N_DEV):
        for s, e, d in _in_pieces(k):
            a, b = max(d, lo), min(d + e - s, hi)
            if a < b:
                out.append((k, s + a - d, s + b - d, a - lo))
    return out


def _tile_orders():
    n_tiles = IN_PAD // P_TILE
    sources = [{k for k, _, _, _ in _tile_pieces(j)} for j in range(n_tiles)]
    rows, n_early, n_free = [], n_tiles, n_tiles
    for chip in range(N_CHIPS):
        own = {2 * chip, 2 * chip + 1}
        diagonal = {2 * (N_CHIPS - 1 - chip), 2 * (N_CHIPS - 1 - chip) + 1}
        early = [j for j in range(n_tiles) if sources[j] <= own]
        late = [j for j in range(n_tiles) if sources[j] & diagonal]
        mid = [j for j in range(n_tiles) if j not in early and j not in late]
        rows.append(early + mid + late)
        n_early, n_free = min(n_early, len(early)), min(n_free, len(early) + len(mid))
    return np.asarray(rows, np.int32), n_early, n_free


def _q_pieces(k):
    lo, hi = SHARD_Q * k, SHARD_Q * (k + 1)
    out = []
    for h in range(N_HEADS):
        base = (D_NOPE + D_ROPE) * h
        s, e = max(lo, base), min(hi, base + D_NOPE)
        if s < e:
            out.append((s - lo, e - lo, D_NOPE * h + s - base))
        s, e = max(lo, base + D_NOPE), min(hi, base + D_NOPE + D_ROPE)
        if s < e:
            out.append((s - lo, e - lo, N_HEADS * D_NOPE + D_ROPE * h + s - base - D_NOPE))
    return out


def _kv_dst(k):
    return D_NOPE * (k // 2) + (N_HEADS * D_NOPE if k % 2 else 0)


def _params(*sem):
    return pltpu.CompilerParams(dimension_semantics=sem, vmem_limit_bytes=VMEM_LIMIT)


def _rms_stats(x):
    r = lax.rsqrt(jnp.mean(x * x, axis=-1, keepdims=True) + EPS)
    return x * r, r


def _rms_bwd(gdy, xhat, r):
    return r * (gdy - xhat * jnp.mean(gdy * xhat, axis=-1, keepdims=True))


def _sigmoid(z):
    return 1.0 / (1.0 + jnp.exp(-z))


def _group_mean(x):
    i0 = lax.broadcasted_iota(jnp.int32, (LANES, LANES), 0) // CONV_GROUP
    i1 = lax.broadcasted_iota(jnp.int32, (LANES, LANES), 1) // CONV_GROUP
    m = jnp.where(i0 == i1, 1.0 / CONV_GROUP, 0.0).astype(BF16)
    hi = x.astype(BF16)
    lo = (x - hi.astype(F32)).astype(BF16)
    return jnp.dot(hi, m, preferred_element_type=F32) + jnp.dot(lo, m, preferred_element_type=F32)


_NT = (((1,), (1,)), ((), ()))
_TN = (((0,), (0,)), ((), ()))


def _dot(a, b, dims=None):
    if dims is None:
        return jnp.dot(a, b, preferred_element_type=F32)
    return lax.dot_general(a, b, dims, preferred_element_type=F32)


def _device_position():
    x, y, c = lax.axis_index("x"), lax.axis_index("y"), lax.axis_index("c")
    return x, y, c, 4 * x + 2 * y + c


def _gather_plan(srcs, slots, send_sems, recv_sems, local_sems):
    x, y, c, _ = _device_position()
    me, sibling = (x, y, c), (x, y, 1 - c)
    flip = lambda v, on: v + on - 2 * v * on
    near = (flip(x, 1 - c), flip(y, c))
    far = (flip(x, c), flip(y, 1 - c))
    diag = (1 - x, 1 - y)
    n = len(srcs)

    def slot(a, px, py, pc):
        return slots[a].at[4 * px + 2 * py + pc]

    def copy(a, k, block, to, own=False):
        return pltpu.make_async_remote_copy(
            src_ref=srcs[a] if own else slot(a, *block),
            dst_ref=slot(a, *block),
            send_sem=send_sems.at[7 * a + k],
            recv_sem=recv_sems.at[7 * a + k],
            device_id=to,
            device_id_type=pl.DeviceIdType.MESH,
        )

    def local(a):
        return pltpu.make_async_copy(srcs[a], slot(a, *me), local_sems.at[a])

    sent = [(me, sibling), (me, (*near, c)), (me, (*far, c)), ((*near, c), (*far, c)),
            ((*near, c), sibling), ((*far, c), sibling), ((*diag, c), sibling)]
    landed = [sibling, (*near, c), (*far, c), (*diag, c), (*far, 1 - c), (*near, 1 - c), (*diag, 1 - c)]

    def send(a, k):
        return copy(a, k, *sent[k], own=k < 3)

    def arrival(a, k):
        return copy(a, k, landed[k], me)

    def start():
        for a in range(n):
            local(a).start()
            for k in range(3):
                send(a, k).start()

    def own():
        for a in range(n):
            local(a).wait()
            arrival(a, 0).wait_recv()

    def mid():
        for a in range(n):
            arrival(a, 1).wait_recv()
            send(a, 3).start()
            send(a, 4).start()
        for a in range(n):
            arrival(a, 2).wait_recv()
            send(a, 5).start()
        for a in range(n):
            for k in (4, 5):
                arrival(a, k).wait_recv()

    def late():
        for a in range(n):
            arrival(a, 3).wait_recv()
            send(a, 6).start()
        for a in range(n):
            arrival(a, 6).wait_recv()

    def finish():
        for a in range(n):
            for k in range(7):
                send(a, k).wait_send()

    return start, own, mid, late, finish


def _adam_update(g, w, m, v):
    m_new = ADAM_B1 * m + (1.0 - ADAM_B1) * g
    v_new = ADAM_B2 * v + (1.0 - ADAM_B2) * (g * g)
    m_hat = m_new / (1.0 - ADAM_B1 ** ADAM_STEP)
    v_hat = v_new / (1.0 - ADAM_B2 ** ADAM_STEP)
    return -ADAM_LR * (m_hat / (jnp.sqrt(v_hat) + ADAM_EPS) + ADAM_WD * w), m_new, v_new


def _adamw(grads, params):
    names = [n for n, _ in PARAM_SHAPES]
    n_p = len(names)

    def body(*refs):
        for i in range(n_p):
            g = refs[i][...]
            w, m, v = (refs[n_p + 3 * i + j][...] for j in range(3))
            delta, m_new, v_new = _adam_update(g, w, m, v)
            for j, val in enumerate((g, delta, m_new, v_new)):
                refs[4 * n_p + 4 * i + j][...] = val

    vm = pl.BlockSpec(memory_space=pltpu.VMEM)
    out_shape = []
    for _, shape in PARAM_SHAPES:
        out_shape += [jax.ShapeDtypeStruct(shape, F32)] * 4
    outs = pl.pallas_call(
        body,
        name="adamw",
        out_shape=tuple(out_shape),
        in_specs=[vm] * (4 * n_p),
        out_specs=(vm,) * (4 * n_p),
        compiler_params=pltpu.CompilerParams(vmem_limit_bytes=VMEM_LIMIT),
    )(*[grads[n] for n in names], *[a for n in names for a in params[n]])
    return {n: outs[4 * i:4 * i + 4] for i, n in enumerate(names)}


N_CHIPS = 4


def _reduce_plan(pays, owns, r1s, sums, r2s, send1, recv1, send2, recv2, local_sems):
    x, y, c, _ = _device_position()
    sibling = (x, y, 1 - c)
    chips = [((1 - x if rj & 2 else x), (1 - y if rj & 1 else y)) for rj in range(N_CHIPS)]
    n = len(pays)

    def slot_of(rj, core):
        return 4 * chips[rj][0] + 2 * chips[rj][1] + core

    def to_sibling(a, rj):
        return pltpu.make_async_remote_copy(
            src_ref=pays[a].at[slot_of(rj, 1 - c)], dst_ref=r1s[a].at[rj],
            send_sem=send1.at[N_CHIPS * a + rj], recv_sem=recv1.at[N_CHIPS * a + rj],
            device_id=sibling, device_id_type=pl.DeviceIdType.MESH)

    def load_own(a, rj):
        return pltpu.make_async_copy(pays[a].at[slot_of(rj, c)], owns[a].at[rj], local_sems.at[2 * N_CHIPS * a + rj])

    def to_chip(a, rj):
        return pltpu.make_async_remote_copy(
            src_ref=sums[a].at[rj], dst_ref=r2s[a].at[rj],
            send_sem=send2.at[N_CHIPS * a + rj], recv_sem=recv2.at[N_CHIPS * a + rj],
            device_id=(*chips[rj], c), device_id_type=pl.DeviceIdType.MESH)

    def keep(a):
        return pltpu.make_async_copy(sums[a].at[0], r2s[a].at[0], local_sems.at[2 * N_CHIPS * a + N_CHIPS])

    def start():
        for a in range(n):
            for rj in range(N_CHIPS):
                to_sibling(a, rj).start()
                if owns[a] is not None:
                    load_own(a, rj).start()

    def combine():
        for a in range(n):
            for rj in range(N_CHIPS):
                to_sibling(a, rj).wait_recv()
                if owns[a] is not None:
                    load_own(a, rj).wait()
                    mine = owns[a][rj]
                else:
                    mine = pays[a][slot_of(rj, c)]
                sums[a][rj] = (mine.astype(F32) + r1s[a][rj].astype(F32)).astype(sums[a].dtype)
            keep(a).start()
            for rj in range(1, N_CHIPS):
                to_chip(a, rj).start()

    def finish():
        for a in range(n):
            for rj in range(1, N_CHIPS):
                to_chip(a, rj).wait_recv()
            for rj in range(N_CHIPS):
                to_sibling(a, rj).wait_send()
            for rj in range(1, N_CHIPS):
                to_chip(a, rj).wait_send()
            keep(a).wait()

    return start, combine, finish


def _reduce_scratch(shapes_dtypes, own_flags):
    out = []
    for (shape, dtype), own in zip(shapes_dtypes, own_flags):
        if own:
            out.append(pltpu.VMEM((N_CHIPS,) + shape, dtype))
        out += [pltpu.VMEM((N_CHIPS,) + shape, dtype), pltpu.VMEM((N_CHIPS,) + shape, dtype)]
    n = len(shapes_dtypes)
    out += [pltpu.SemaphoreType.DMA((N_CHIPS * n,))] * 4 + [pltpu.SemaphoreType.DMA((2 * N_CHIPS * n,))]
    return out


def _pack_small(ssmall, dwq, dwkv, dconv, dfinal, dgq, dgkv, dattn, dconvg, loss_part):
    ssmall[...] = jnp.zeros_like(ssmall)
    rep = ssmall.at[0]
    for i in range(D_MODEL // LANES):
        rep[ROW_FINAL + i:ROW_FINAL + i + 1, :] = dfinal[:, LANES * i:LANES * (i + 1)]
    for i in range(Q_RANK // LANES):
        rep[ROW_GQ + i:ROW_GQ + i + 1, :] = dgq[:, LANES * i:LANES * (i + 1)]
    rep[ROW_GKV:ROW_GKV + 1, :] = dgkv[...]
    for i in range(CONV_WIDTH // LANES):
        rep[ROW_ATTN + i:ROW_ATTN + i + 1, :] = dattn[:, LANES * i:LANES * (i + 1)]
        rep[ROW_CONVG + i:ROW_CONVG + i + 1, :] = dconvg[:, LANES * i:LANES * (i + 1)]
    rep[ROW_LOSS:ROW_LOSS + 1, :] = loss_part[...]
    for k in range(N_DEV):
        if k:
            ssmall[k, ROW_REPL:, :] = ssmall[0, ROW_REPL:, :]
        for s, e, d in _q_pieces(k):
            for i in range(Q_RANK // LANES):
                ssmall[k, ROW_Q + SHARD_Q * i + s:ROW_Q + SHARD_Q * i + e, :] = dwq[d:d + e - s, LANES * i:LANES * (i + 1)]
        ssmall[k, ROW_KV:ROW_KV + KV_RANK, :] = dwkv[:, _kv_dst(k):_kv_dst(k) + SHARD_KV]
        ssmall[k, ROW_CONV:ROW_CONV + 3, 0:SHARD_CONV] = dconv[0:3, SHARD_CONV * k:SHARD_CONV * (k + 1)]


TOKEN_TILES = 4
TAIL_ROWS = N_META + D_MODEL // LANES


def _reduce_tail(r_in, r_out, r_small, g_meta, g_norm):
    n_p = len(PARAM_SHAPES)
    names = [n for n, _ in PARAM_SHAPES]

    def body(*refs):
        rin, rout, rsmall, gmeta, gnorm = refs[:5]
        g_out = {n: refs[5 + i] for i, n in enumerate(names)}
        loss_out = refs[5 + n_p]
        gsum = refs[6 + n_p]
        x, y, c, me = _device_position()
        my_chip = 2 * x + y

        g = rin[my_chip].astype(F32)
        for ch in range(1, N_CHIPS):
            g = g + rin[ch ^ my_chip].astype(F32)
        g_out["w_in"][...] = g[:SHARD_IN, :]

        g = rout[my_chip].astype(F32)
        gs = rsmall[my_chip]
        for ch in range(1, N_CHIPS):
            g = g + rout[ch ^ my_chip].astype(F32)
            gs = gs + rsmall[ch ^ my_chip]
        g_out["w_out"][...] = g
        gsum[...] = gs
        for i in range(Q_RANK // LANES):
            g_out["w_q_up"][:, LANES * i:LANES * (i + 1)] = gsum[ROW_Q + SHARD_Q * i:ROW_Q + SHARD_Q * (i + 1), :]
        g_out["w_kv_up"][...] = gsum[ROW_KV:ROW_KV + KV_RANK, :]
        for i in range(3):
            g_out["conv_w"][i] = gsum[ROW_CONV + i:ROW_CONV + i + 1, 0:SHARD_CONV]
        for name, row, width in (("final_norm_g", ROW_FINAL, D_MODEL), ("q_norm_g", ROW_GQ, Q_RANK),
                                 ("kv_norm_g", ROW_GKV, KV_RANK), ("attn_out_g", ROW_ATTN, CONV_WIDTH),
                                 ("conv_out_g", ROW_CONVG, CONV_WIDTH)):
            for i in range(width // LANES):
                g_out[name][:, LANES * i:LANES * (i + 1)] = gsum[row + i:row + i + 1, :]
        loss_out[...] = gsum[ROW_LOSS:ROW_LOSS + 1, :]
        g_out["meta_tokens"][...] = gmeta[...]
        g_out["norm_g"][...] = gnorm[...]

    vm = pl.BlockSpec(memory_space=pltpu.VMEM)
    out_shape = [jax.ShapeDtypeStruct(shape, F32) for _, shape in PARAM_SHAPES]
    out_shape.append(jax.ShapeDtypeStruct((1, LANES), F32))
    outs = pl.pallas_call(
        body,
        name="reduce_tail",
        out_shape=tuple(out_shape),
        in_specs=[vm] * 5,
        out_specs=(vm,) * len(out_shape),
        scratch_shapes=[pltpu.VMEM((SMALL_ROWS, LANES), F32)],
        compiler_params=pltpu.CompilerParams(vmem_limit_bytes=VMEM_LIMIT),
    )(r_in, r_out, r_small, g_meta, g_norm)
    return {n: outs[i] for i, n in enumerate(names)}, outs[-1]


def _prep_in_proj(x, meta, norm_g, w_in_t, w_q, w_kv, w_out, conv_w, order):
    nb_seq, s, d = x.shape
    tp = s + LANES
    m = nb_seq * tp
    ts = s // TOKEN_TILES
    n_real = nb_seq * TOKEN_TILES
    n_norm = n_real + 1
    n_tiles = IN_PAD // P_TILE
    _, n_early, n_free = _tile_orders()
    steps = n_norm + n_tiles
    dot_rows = m // 4
    qkv_shape = (SHARD_Q + KV_RANK, Q_RANK)

    def tile(t):
        t = jnp.minimum(t, n_real - 1)
        return t // TOKEN_TILES, t % TOKEN_TILES

    def column_tile(t, order_ref):
        chip = 2 * lax.axis_index("x") + lax.axis_index("y")
        return order_ref[chip * n_tiles + jnp.maximum(t - n_norm, 0)]

    def body(order_ref, x_ref, meta_ref, g_ref, win_ref, wq_ref, wkv_ref, wout_ref, conv_ref,
             u_hbm, p_ref, w_in_p, meta_f, w_q_p, w_kv_p, w_out_f, conv_f,
             u_all, sbig, gbig, smeta, gmeta, sqkv, sout, sconv, gqkv, gout, gconv,
             send_in, recv_in, local_in, send_meta, recv_meta, send_rest, recv_rest, local_rest, u_sem):
        t = pl.program_id(0)
        px, py, pc, me = _device_position()
        u_copy = pltpu.make_async_copy(u_all, u_hbm, u_sem)

        def plan_in():
            return _gather_plan((sbig,), (gbig,), send_in, recv_in, local_in)

        def plan_rest():
            return _gather_plan((sqkv, sout, sconv), (gqkv, gout, gconv), send_rest, recv_rest, local_rest)

        def meta_copies():
            out = []
            for r in range(1, N_DEV):
                peer = (1 - px if r & 4 else px, 1 - py if r & 2 else py, 1 - pc if r & 1 else pc)
                out.append(pltpu.make_async_remote_copy(
                    src_ref=smeta,
                    dst_ref=gmeta.at[r],
                    send_sem=send_meta.at[r - 1],
                    recv_sem=recv_meta.at[r - 1],
                    device_id=peer,
                    device_id_type=pl.DeviceIdType.MESH,
                ))
            return out

        @pl.when(t == 0)
        def _():
            sbig[0:SHARD_IN, :] = win_ref[...].astype(BF16)
            sbig[SHARD_IN:, :] = jnp.zeros((SHARD_IN_PAD - SHARD_IN, d), BF16)
            smeta[...] = meta_ref[...]
            plan_in()[0]()
            for cp in meta_copies():
                cp.start()
            sqkv[...] = jnp.zeros_like(sqkv)
            sqkv[0:SHARD_Q, :] = wq_ref[...].astype(BF16)
            sqkv[SHARD_Q:, 0:SHARD_KV] = wkv_ref[...].astype(BF16)
            sout[...] = wout_ref[...].astype(BF16)
            sconv[...] = jnp.zeros_like(sconv)
            for i in range(3):
                sconv[i:i + 1, 0:SHARD_CONV] = conv_ref[i]

        def norm(h):
            hhat, _ = _rms_stats(h)
            return (hhat * g_ref[...]).astype(BF16)

        @pl.when(t < n_real)
        def _():
            b, k = tile(t)
            row0 = pl.multiple_of(b * tp + LANES + k * ts, 16)
            u_all[pl.ds(row0, ts), :] = norm(x_ref[0])

        @pl.when(t == n_real)
        def _():
            for cp in meta_copies():
                cp.wait_recv()
            gmeta[0] = smeta[...]
            for k in range(N_DEV):
                meta_f[:, SHARD_META * k:SHARD_META * (k + 1)] = gmeta[k ^ me]
            um = norm(meta_f[...])
            for b in range(nb_seq):
                u_all[b * tp:b * tp + PAD_FRONT, :] = jnp.zeros((PAD_FRONT, d), BF16)
                u_all[b * tp + PAD_FRONT:b * tp + LANES, :] = um
            u_copy.start()

        @pl.when(t == n_norm)
        def _():
            plan_in()[1]()

        @pl.when(t == n_norm + n_early)
        def _():
            plan_in()[2]()
            plan_rest()[0]()

        @pl.when(t == n_norm + (n_early + n_free) // 2)
        def _():
            plan_rest()[1]()
            plan_rest()[2]()

        @pl.when(t == n_norm + n_free)
        def _():
            plan_in()[3]()

        @pl.when(t == n_norm + n_free + 2)
        def _():
            plan_rest()[3]()

        @pl.when(t >= n_norm)
        def _():
            j = column_tile(t, order_ref)
            for jj in range(n_tiles):
                @pl.when(j == jj)
                def _(jj=jj):
                    if jj == N_A // P_TILE:
                        w_in_p[N_A % P_TILE:, :] = jnp.zeros((P_TILE - N_A % P_TILE, d), BF16)
                    for k, s0, e0, d0 in _tile_pieces(jj):
                        w_in_p[d0:d0 + e0 - s0, :] = gbig[k, s0:e0, :]
            for r in range(m // dot_rows):
                rows = slice(dot_rows * r, dot_rows * (r + 1))
                p_ref[rows, :] = _dot(u_all[rows, :], w_in_p[...], _NT).astype(BF16)

        @pl.when(t == steps - 1)
        def _():
            plan_in()[4]()
            plan_rest()[4]()
            for cp in meta_copies():
                cp.wait_send()
            u_copy.wait()
            conv_f[...] = jnp.zeros_like(conv_f)
            for k in range(N_DEV):
                for s0, e0, d0 in _q_pieces(k):
                    w_q_p[d0:d0 + e0 - s0, :] = gqkv[k, s0:e0, :]
                w_kv_p[:, _kv_dst(k):_kv_dst(k) + SHARD_KV] = gqkv[k, SHARD_Q:, 0:SHARD_KV]
                w_out_f[SHARD_OUT * k:SHARD_OUT * (k + 1), :] = gout[k]
                conv_f[0:3, SHARD_CONV * k:SHARD_CONV * (k + 1)] = gconv[k, 0:3, 0:SHARD_CONV]

    whole = lambda shape: pl.BlockSpec(shape, lambda t, o: (0,) * len(shape))
    return pl.pallas_call(
        body,
        name="prep_in_proj_gather",
        grid_spec=pltpu.PrefetchScalarGridSpec(
            num_scalar_prefetch=1,
            grid=(steps,),
            in_specs=[
                pl.BlockSpec((1, ts, d), lambda t, o: (*tile(t), 0)),
                whole(meta.shape), whole(norm_g.shape), whole(w_in_t.shape),
                whole(w_q.shape), whole(w_kv.shape), whole(w_out.shape), whole(conv_w.shape),
            ],
            out_specs=(pl.BlockSpec(memory_space=pl.ANY),
                       pl.BlockSpec((m, P_TILE), lambda t, o: (0, column_tile(t, o))),
                       pl.BlockSpec((P_TILE, d), lambda t, o: (column_tile(t, o), 0)),
                       whole((N_META, d)),
                       whole((Q_COLS, Q_RANK)), whole((KV_RANK, KV_COLS)), whole((D_MODEL, D_MODEL)),
                       whole((8, CONV_WIDTH))),
            scratch_shapes=[
                pltpu.VMEM((m, d), BF16),
                pltpu.VMEM((SHARD_IN_PAD, d), BF16),
                pltpu.VMEM((N_DEV, SHARD_IN_PAD, d), BF16),
                pltpu.VMEM((N_META, SHARD_META), F32),
                pltpu.VMEM((N_DEV, N_META, SHARD_META), F32),
                pltpu.VMEM(qkv_shape, BF16),
                pltpu.VMEM((SHARD_OUT, D_MODEL), BF16),
                pltpu.VMEM((8, LANES), F32),
                pltpu.VMEM((N_DEV,) + qkv_shape, BF16),
                pltpu.VMEM((N_DEV, SHARD_OUT, D_MODEL), BF16),
                pltpu.VMEM((N_DEV, 8, LANES), F32),
                pltpu.SemaphoreType.DMA((7,)),
                pltpu.SemaphoreType.DMA((7,)),
                pltpu.SemaphoreType.DMA((1,)),
                pltpu.SemaphoreType.DMA((N_DEV - 1,)),
                pltpu.SemaphoreType.DMA((N_DEV - 1,)),
                pltpu.SemaphoreType.DMA((21,)),
                pltpu.SemaphoreType.DMA((21,)),
                pltpu.SemaphoreType.DMA((3,)),
                pltpu.SemaphoreType.DMA,
            ],
        ),
        out_shape=(jax.ShapeDtypeStruct((m, d), BF16),
                   jax.ShapeDtypeStruct((m, IN_PAD), BF16),
                   jax.ShapeDtypeStruct((IN_PAD, d), BF16),
                   jax.ShapeDtypeStruct((N_META, d), F32),
                   jax.ShapeDtypeStruct((Q_COLS, Q_RANK), BF16),
                   jax.ShapeDtypeStruct((KV_RANK, KV_COLS), BF16),
                   jax.ShapeDtypeStruct((D_MODEL, D_MODEL), BF16),
                   jax.ShapeDtypeStruct((8, CONV_WIDTH), F32)),
        compiler_params=_params("arbitrary"),
    )(order, x, meta, norm_g, w_in_t, w_q, w_kv, w_out, conv_w)


def _rope_tables(tp):
    half = D_ROPE // 2
    inv_freq = (1.0 / (ROPE_THETA ** (np.arange(half, dtype=np.float32) / half))).astype(np.float32)
    pos = (np.arange(tp) - PAD_FRONT).astype(np.float32)
    ang = pos[:, None] * inv_freq[None, :]
    cos = np.tile(np.cos(ang), (1, LANES // half))
    sin = np.tile(np.sin(ang), (1, LANES // half))
    first = (np.arange(LANES) % D_ROPE) < half
    zero = np.float32(0.0)
    return tuple(jnp.asarray(t, F32) for t in (cos, np.where(first, -sin, zero), np.where(first, zero, sin)))


def _rope(t, cos, sa, sb):
    return t * cos + pltpu.roll(t, LANES - D_ROPE // 2, 1) * sa + pltpu.roll(t, D_ROPE // 2, 1) * sb


def _rope_t(t, cos, sa, sb):
    return t * cos + pltpu.roll(t * sa, D_ROPE // 2, 1) + pltpu.roll(t * sb, LANES - D_ROPE // 2, 1)


def _qkv_fwd(p, wq, wkv, gq, gkv, tables, nb_seq, tp):
    ht = tp // 2

    def body(pa_ref, wq_ref, wkv_ref, gq_ref, gkv_ref, cos_ref, sa_ref, sb_ref, q_ref, k_ref, v_ref):
        pa = pa_ref[...].astype(F32)
        cq_hat, _ = _rms_stats(pa[:, :Q_RANK])
        ckv_hat, _ = _rms_stats(pa[:, Q_RANK:Q_RANK + KV_RANK])
        q = _dot((cq_hat * gq_ref[...]).astype(BF16), wq_ref[...], _NT) * Q_SCALE
        kv = _dot((ckv_hat * gkv_ref[...]).astype(BF16), wkv_ref[...])
        tabs = (cos_ref[...], sa_ref[...], sb_ref[...])
        lane = lax.broadcasted_iota(jnp.int32, (ht, LANES), 1)
        low = lane < D_ROPE
        mark = lane == D_ROPE
        row = (pl.program_id(0) % 2) * ht + lax.broadcasted_iota(jnp.int32, (ht, LANES), 0)
        k_pe = jnp.where(mark & (row < PAD_FRONT), NEG_INF, _rope(pa[:, Q_RANK + KV_RANK:], *tabs))
        one = jnp.where(mark & (row >= PAD_FRONT), 1.0, 0.0)
        pairs = [_rope(q[:, N_HEADS * D_NOPE + LANES * i:N_HEADS * D_NOPE + LANES * (i + 1)], *tabs) for i in range(2)]
        for h in range(N_HEADS):
            pair = pairs[h // 2]
            if h % 2:
                pair = pltpu.roll(pair, D_ROPE, 1)
            pe = jnp.where(low, pair, one)
            q_ref[0, h] = jnp.concatenate([q[:, D_NOPE * h:D_NOPE * (h + 1)], pe], axis=1).astype(BF16)
            k_ref[0, h] = jnp.concatenate([kv[:, D_NOPE * h:D_NOPE * (h + 1)], k_pe], axis=1).astype(BF16)
            v_ref[0, h] = kv[:, N_HEADS * D_NOPE + D_V * h:N_HEADS * D_NOPE + D_V * (h + 1)].astype(BF16)

    full = lambda a: pl.BlockSpec(a.shape, lambda i: (0,) * a.ndim)
    tab = pl.BlockSpec((ht, LANES), lambda i: (i % 2, 0))
    qk = pl.BlockSpec((1, N_HEADS, ht, 2 * LANES), lambda i: (i // 2, 0, i % 2, 0))
    return pl.pallas_call(
        body,
        name="qkv_fwd",
        grid=(2 * nb_seq,),
        in_specs=[pl.BlockSpec((ht, GRP_A), lambda i: (i, 0)), full(wq), full(wkv), full(gq), full(gkv), tab, tab, tab],
        out_specs=(qk, qk, pl.BlockSpec((1, N_HEADS, ht, D_V), lambda i: (i // 2, 0, i % 2, 0))),
        out_shape=(
            jax.ShapeDtypeStruct((nb_seq, N_HEADS, tp, 2 * LANES), BF16),
            jax.ShapeDtypeStruct((nb_seq, N_HEADS, tp, 2 * LANES), BF16),
            jax.ShapeDtypeStruct((nb_seq, N_HEADS, tp, D_V), BF16),
        ),
        compiler_params=_params("parallel"),
    )(p, wq, wkv, gq, gkv, *tables)


def _attn_fwd(q, k, v, p, g_attn):
    nb_seq, _, tp, _ = q.shape

    def body(q_ref, k_ref, v_ref, z_ref, g_ref, y_ref, o_ref, lse_ref):
        g = g_ref[...]
        for r0 in range(0, tp, Q_TILE):
            nq = min(Q_TILE, tp - r0)
            kend = r0 + nq
            qq = q_ref[0, 0, r0:kend, :]
            sd = _dot(qq, k_ref[0, 0, r0:kend, :], _NT)
            causal = (lax.broadcasted_iota(jnp.int32, (nq, nq), 1) <= lax.broadcasted_iota(jnp.int32, (nq, nq), 0))
            sd = jnp.where(causal, sd, NEG_INF)
            m = jnp.max(sd, axis=-1, keepdims=True)
            if r0:
                so = _dot(qq, k_ref[0, 0, 0:r0, :], _NT)
                m = jnp.maximum(m, jnp.max(so, axis=-1, keepdims=True))
            ed = jnp.exp2(sd - m)
            l = jnp.sum(ed, axis=-1, keepdims=True)
            o = _dot(ed.astype(BF16), v_ref[0, 0, r0:kend, :])
            if r0:
                eo = jnp.exp2(so - m)
                l = l + jnp.sum(eo, axis=-1, keepdims=True)
                o = o + _dot(eo.astype(BF16), v_ref[0, 0, 0:r0, :])
            o = o * (1.0 / l)
            o_ref[0, 0, r0:kend, :] = o
            lse_ref[0, 0, r0:kend, :] = jnp.broadcast_to(m + jnp.log2(l), (nq, LANES))
            ohat, _ = _rms_stats(o)
            z = z_ref[r0:kend, :].astype(F32)
            y_ref[r0:kend, :] = (ohat * g * (z * _sigmoid(z))).astype(BF16)

    qk = pl.BlockSpec((1, 1, tp, 2 * LANES), lambda b, h: (b, h, 0, 0))
    hv = pl.BlockSpec((1, 1, tp, D_V), lambda b, h: (b, h, 0, 0))
    return pl.pallas_call(
        body,
        name="attn_fwd",
        grid=(nb_seq, N_HEADS),
        in_specs=[qk, qk, hv,
                  pl.BlockSpec((tp, LANES), lambda b, h: (b, GRP_A // LANES + h)),
                  pl.BlockSpec((1, LANES), lambda b, h: (0, h))],
        out_specs=(pl.BlockSpec((tp, LANES), lambda b, h: (b, h)), hv, hv),
        out_shape=(
            jax.ShapeDtypeStruct((nb_seq * tp, N_HEADS * D_V), BF16),
            jax.ShapeDtypeStruct((nb_seq, N_HEADS, tp, D_V), F32),
            jax.ShapeDtypeStruct((nb_seq, N_HEADS, tp, LANES), F32),
        ),
        compiler_params=_params("parallel", "parallel"),
    )(q, k, v, p, g_attn)


_CONV_COL0 = (GRP_A + N_HEADS * D_V) // LANES


def _conv_specs(tp, order):
    cols = CONV_WIDTH // LANES
    return [pl.BlockSpec((tp, LANES), functools.partial(
        lambda a, b, off: order(a, b, off), off=_CONV_COL0 + i * cols)) for i in range(4)]


def _conv_fwd(p, conv_w, g_conv, nb_seq, tp):
    def body(b_ref, c_ref, h_ref, z_ref, w_ref, g_ref, y_ref):
        cc = c_ref[...].astype(F32) * h_ref[...].astype(F32)
        row = lax.broadcasted_iota(jnp.int32, (tp, LANES), 0)
        s1 = jnp.where(row >= 1, pltpu.roll(cc, 1, 0), 0.0)
        s2 = jnp.where(row >= 2, pltpu.roll(cc, 2, 0), 0.0)
        yc = b_ref[...].astype(F32) * (w_ref[0:1, :] * s2 + w_ref[1:2, :] * s1 + w_ref[2:3, :] * cc)
        r = lax.rsqrt(_group_mean(yc * yc) + EPS)
        z = z_ref[...].astype(F32)
        y_ref[...] = (yc * r * g_ref[...] * (z * _sigmoid(z))).astype(BF16)

    return pl.pallas_call(
        body,
        name="conv_fwd",
        grid=(nb_seq, CONV_WIDTH // LANES),
        in_specs=_conv_specs(tp, lambda b, t, off: (b, off + t)) + [
            pl.BlockSpec((8, LANES), lambda b, t: (0, t)),
            pl.BlockSpec((1, LANES), lambda b, t: (0, t))],
        out_specs=pl.BlockSpec((tp, LANES), lambda b, t: (b, t)),
        out_shape=jax.ShapeDtypeStruct((nb_seq * tp, CONV_WIDTH), BF16),
        compiler_params=_params("parallel", "parallel"),
    )(p, p, p, p, conv_w, g_conv)


def _token_copy(hbm, b, k, ts, buf, sem, to_hbm=False):
    lo, hi = max(k * ts - LANES, 0), (k + 1) * ts - LANES
    off = lo - (k * ts - LANES)
    src, dst = hbm.at[b, pl.ds(lo, hi - lo)], buf.at[pl.ds(off, hi - lo)]
    if to_hbm:
        src, dst = dst, src
    return pltpu.make_async_copy(src, dst, sem)


def _for_tile(k, nt, fn):
    for kk in range(nt):
        @pl.when(k == kk)
        def _(kk=kk):
            fn(kk)


def _out_proj_loss(ya, yc, w_out, x, target, g_final, nt):
    nb_seq, s, d = x.shape
    r, ka = ya.shape
    ts = (s + LANES) // nt
    steps = nb_seq * nt

    def body(a_ref, c_ref, w_ref, x_hbm, t_hbm, g_ref, dhb_ref, dg_ref, loss_ref,
             xbuf, tbuf, acc_ref, sems):
        i = pl.program_id(0)
        b, k = i // nt, i % nt

        @pl.when(i == 0)
        def _():
            acc_ref[...] = jnp.zeros_like(acc_ref)
            dg_ref[...] = jnp.zeros_like(dg_ref)

        slot = i % 2

        def fetch(seq, kk, sl):
            return [_token_copy(x_hbm, seq, kk, ts, xbuf.at[sl], sems.at[sl, 0]),
                    _token_copy(t_hbm, seq, kk, ts, tbuf.at[sl], sems.at[sl, 1])]

        def start(seq, sl, kk):
            if kk == 0:
                xbuf[sl, 0:LANES, :] = jnp.zeros((LANES, d), F32)
                tbuf[sl, 0:LANES, :] = jnp.zeros((LANES, d), F32)
            for cp in fetch(seq, kk, sl):
                cp.start()

        @pl.when(i == 0)
        def _():
            start(0, 0, 0)

        @pl.when(i + 1 < steps)
        def _():
            _for_tile((i + 1) % nt, nt, functools.partial(start, (i + 1) // nt, 1 - slot))

        mix = _dot(a_ref[...], w_ref[0:ka, :]) + _dot(c_ref[...], w_ref[ka:, :])
        _for_tile(k, nt, lambda kk: [cp.wait() for cp in fetch(b, kk, slot)])

        real = (lax.broadcasted_iota(jnp.int32, (ts, d), 0) >= LANES) | (k > 0)
        g = g_ref[...]
        hhat, rstd = _rms_stats(xbuf[slot] + mix)
        e = jnp.where(real, hhat * g - tbuf[slot], 0.0)
        acc_ref[...] += jnp.sum(e * e, axis=0, keepdims=True)
        dy = e * (1.0 / d)
        dg_ref[...] += jnp.sum(dy * hhat, axis=0, keepdims=True)
        dhb_ref[...] = _rms_bwd(g * dy, hhat, rstd).astype(BF16)

        @pl.when(i == steps - 1)
        def _():
            total = jnp.sum(acc_ref[...], axis=1, keepdims=True)
            loss_ref[...] = jnp.broadcast_to((0.5 / d) * total, loss_ref.shape)

    hbm = pl.BlockSpec(memory_space=pl.ANY)
    row = pl.BlockSpec((ts, d), lambda i: (i, 0))
    vec = pl.BlockSpec((1, d), lambda i: (0, 0))
    return pl.pallas_call(
        body,
        name="out_proj_loss",
        grid=(steps,),
        in_specs=[pl.BlockSpec((ts, ka), lambda i: (i, 0)), pl.BlockSpec((ts, yc.shape[1]), lambda i: (i, 0)),
                  pl.BlockSpec(w_out.shape, lambda i: (0, 0)), hbm, hbm, vec],
        out_specs=(row, vec, pl.BlockSpec((1, LANES), lambda i: (0, 0))),
        out_shape=(
            jax.ShapeDtypeStruct((r, d), BF16),
            jax.ShapeDtypeStruct((1, d), F32),
            jax.ShapeDtypeStruct((1, LANES), F32),
        ),
        scratch_shapes=[pltpu.VMEM((2, ts, d), F32), pltpu.VMEM((2, ts, d), F32), pltpu.VMEM((1, d), F32),
                        pltpu.SemaphoreType.DMA((2, 2))],
        compiler_params=_params("arbitrary"),
    )(ya, yc, w_out, x, target, g_final)


def _out_proj_bwd(dhb, w_out, ya, yc, bm):
    r, d = dhb.shape
    ka = ya.shape[1]
    n_mix = w_out.shape[0]
    last = r // bm - 1

    def body(dh_ref, w_ref, a_ref, c_ref, dcat_ref, dw_ref, acc_ref):
        @pl.when(pl.program_id(0) == 0)
        def _():
            acc_ref[...] = jnp.zeros_like(acc_ref)

        dh = dh_ref[...]
        dcat_ref[...] = _dot(dh, w_ref[...], _NT).astype(BF16)
        acc_ref[0:ka, :] += _dot(a_ref[...], dh, _TN)
        acc_ref[ka:, :] += _dot(c_ref[...], dh, _TN)

        @pl.when(pl.program_id(0) == last)
        def _():
            dw_ref[...] = acc_ref[...].astype(BF16)

    return pl.pallas_call(
        body,
        name="out_proj_bwd",
        grid=(r // bm,),
        in_specs=[pl.BlockSpec((bm, d), lambda i: (i, 0)), pl.BlockSpec(w_out.shape, lambda i: (0, 0)),
                  pl.BlockSpec((bm, ka), lambda i: (i, 0)), pl.BlockSpec((bm, yc.shape[1]), lambda i: (i, 0))],
        out_specs=(pl.BlockSpec((bm, n_mix), lambda i: (i, 0)),
                   pl.BlockSpec((n_mix, d), lambda i: (0, 0))),
        out_shape=(jax.ShapeDtypeStruct((r, n_mix), BF16),
                   jax.ShapeDtypeStruct((n_mix, d), BF16)),
        scratch_shapes=[pltpu.VMEM((n_mix, d), F32)],
        compiler_params=_params("arbitrary"),
    )(dhb, w_out, ya, yc)


def _attn_bwd(q, k, v, o, lse, dcat, p, g_attn):
    nb_seq, _, tp, _ = q.shape

    def body(q_ref, k_ref, v_ref, o_ref, lse_ref, dy_ref, z_ref, g_ref,
             dq_ref, dk_ref, dv_ref, dz_ref, dg_ref, dq_acc):
        @pl.when(pl.program_id(1) == 0)
        def _():
            dg_ref[...] = jnp.zeros_like(dg_ref)

        g = g_ref[...]
        z = z_ref[...].astype(F32)
        o = o_ref[0, 0]
        dy = dy_ref[...].astype(F32)
        sig = _sigmoid(z)
        ohat, r = _rms_stats(o)
        don = dy * (z * sig)
        dz_ref[...] = (dy * (ohat * g) * (sig * (1.0 + z * (1.0 - sig)))).astype(BF16)
        dg_ref[...] += jnp.sum(don * ohat, axis=0, keepdims=True)
        do = _rms_bwd(g * don, ohat, r)
        dvec = jnp.sum(do * o, axis=-1, keepdims=True)
        dob = do.astype(BF16)
        lse_col = lse_ref[0, 0, :, 0:1]
        dq_acc[...] = jnp.zeros_like(dq_acc)
        for k0 in range(0, tp, K_TILE):
            nk = min(K_TILE, tp - k0)
            nq = tp - k0
            qq = q_ref[0, 0, k0:, :]
            kk = k_ref[0, 0, k0:k0 + nk, :]
            causal = (lax.broadcasted_iota(jnp.int32, (nq, nk), 1) <= lax.broadcasted_iota(jnp.int32, (nq, nk), 0))
            pr = jnp.where(causal, jnp.exp2(_dot(qq, kk, _NT) - lse_col[k0:]), 0.0)
            dp = _dot(dob[k0:], v_ref[0, 0, k0:k0 + nk, :], _NT)
            ds = (pr * (dp - dvec[k0:])).astype(BF16)
            dv_ref[0, 0, k0:k0 + nk, :] = _dot(pr.astype(BF16), dob[k0:], _TN).astype(BF16)
            dk_ref[0, 0, k0:k0 + nk, :] = (_dot(ds, qq, _TN) * (ATTN_SCALE / Q_SCALE)).astype(BF16)
            dq_acc[k0:, :] += _dot(ds, kk)
        dq_ref[0, 0] = (dq_acc[...] * ATTN_SCALE).astype(BF16)

    qk = pl.BlockSpec((1, 1, tp, 2 * LANES), lambda h, b: (b, h, 0, 0))
    hv = pl.BlockSpec((1, 1, tp, D_V), lambda h, b: (b, h, 0, 0))
    col = pl.BlockSpec((tp, LANES), lambda h, b: (b, h))
    return pl.pallas_call(
        body,
        name="attn_bwd",
        grid=(N_HEADS, nb_seq),
        in_specs=[qk, qk, hv, hv, hv, col,
                  pl.BlockSpec((tp, LANES), lambda h, b: (b, GRP_A // LANES + h)),
                  pl.BlockSpec((1, LANES), lambda h, b: (0, h))],
        out_specs=(qk, qk, hv, col, pl.BlockSpec((1, LANES), lambda h, b: (0, h))),
        out_shape=(
            jax.ShapeDtypeStruct((nb_seq, N_HEADS, tp, 2 * LANES), BF16),
            jax.ShapeDtypeStruct((nb_seq, N_HEADS, tp, 2 * LANES), BF16),
            jax.ShapeDtypeStruct((nb_seq, N_HEADS, tp, D_V), BF16),
            jax.ShapeDtypeStruct((nb_seq * tp, N_HEADS * D_V), BF16),
            jax.ShapeDtypeStruct((1, N_HEADS * D_V), F32),
        ),
        scratch_shapes=[pltpu.VMEM((tp, 2 * LANES), F32)],
        compiler_params=_params("arbitrary", "arbitrary"),
    )(q, k, v, o, lse, dcat, p, g_attn)


def _qkv_bwd(p, dq, dk, dv, wq, wkv, gq, gkv, tables):
    nb_seq, _, tp, _ = dq.shape
    ht = tp // 2

    def body(pa_ref, dq_ref, dk_ref, dv_ref, wq_ref, wkv_ref, gq_ref, gkv_ref, cos_ref, sa_ref, sb_ref,
             dpa_ref, dwq_ref, dwkv_ref, dgq_ref, dgkv_ref):
        @pl.when(pl.program_id(0) == 0)
        def _():
            dwq_ref[...] = jnp.zeros_like(dwq_ref)
            dwkv_ref[...] = jnp.zeros_like(dwkv_ref)
            dgq_ref[...] = jnp.zeros_like(dgq_ref)
            dgkv_ref[...] = jnp.zeros_like(dgkv_ref)

        pa = pa_ref[...].astype(F32)
        gq, gkv = gq_ref[...], gkv_ref[...]
        cq_hat, rq = _rms_stats(pa[:, :Q_RANK])
        ckv_hat, rkv = _rms_stats(pa[:, Q_RANK:Q_RANK + KV_RANK])
        tabs = (cos_ref[...], sa_ref[...], sb_ref[...])

        pe = [dq_ref[0, h, :, D_NOPE:].astype(F32) for h in range(N_HEADS)]
        pairs = [_rope_t(pe[2 * i] + pltpu.roll(pe[2 * i + 1], D_ROPE, 1), *tabs).astype(BF16) for i in range(2)]
        dq_flat = jnp.concatenate([dq_ref[0, h, :, :D_NOPE] for h in range(N_HEADS)] + pairs, axis=1)
        dwq_ref[...] += _dot(dq_flat, (cq_hat * gq).astype(BF16), _TN)
        dcqn = _dot(dq_flat, wq_ref[...])
        dgq_ref[...] += jnp.sum(dcqn * cq_hat, axis=0, keepdims=True)
        dcq = _rms_bwd(gq * dcqn, cq_hat, rq)

        dkv_flat = jnp.concatenate([dk_ref[0, h, :, :D_NOPE] for h in range(N_HEADS)]
                                   + [dv_ref[0, h] for h in range(N_HEADS)], axis=1)
        dwkv_ref[...] += _dot((ckv_hat * gkv).astype(BF16), dkv_flat, _TN)
        dckvn = _dot(dkv_flat, wkv_ref[...], _NT)
        dgkv_ref[...] += jnp.sum(dckvn * ckv_hat, axis=0, keepdims=True)
        dckv = _rms_bwd(gkv * dckvn, ckv_hat, rkv)

        dk_pe = dk_ref[0, 0, :, D_NOPE:].astype(F32)
        for h in range(1, N_HEADS):
            dk_pe = dk_pe + dk_ref[0, h, :, D_NOPE:].astype(F32)
        dk_pe = jnp.where(lax.broadcasted_iota(jnp.int32, (ht, LANES), 1) < D_ROPE, dk_pe, 0.0)
        dpa_ref[...] = jnp.concatenate([dcq, dckv, _rope_t(dk_pe, *tabs)], axis=1).astype(BF16)

    full = lambda a: pl.BlockSpec(a.shape, lambda i: (0,) * a.ndim)
    tab = pl.BlockSpec((ht, LANES), lambda i: (i % 2, 0))
    qk = pl.BlockSpec((1, N_HEADS, ht, 2 * LANES), lambda i: (i // 2, 0, i % 2, 0))
    acc = lambda shape: pl.BlockSpec(shape, lambda i: (0, 0))
    return pl.pallas_call(
        body,
        name="qkv_bwd",
        grid=(2 * nb_seq,),
        in_specs=[pl.BlockSpec((ht, GRP_A), lambda i: (i, 0)), qk, qk,
                  pl.BlockSpec((1, N_HEADS, ht, D_V), lambda i: (i // 2, 0, i % 2, 0)),
                  full(wq), full(wkv), full(gq), full(gkv), tab, tab, tab],
        out_specs=(pl.BlockSpec((ht, GRP_A), lambda i: (i, 0)),
                   acc(wq.shape), acc(wkv.shape), acc((1, Q_RANK)), acc((1, KV_RANK))),
        out_shape=(
            jax.ShapeDtypeStruct((nb_seq * tp, GRP_A), BF16),
            jax.ShapeDtypeStruct(wq.shape, F32),
            jax.ShapeDtypeStruct(wkv.shape, F32),
            jax.ShapeDtypeStruct((1, Q_RANK), F32),
            jax.ShapeDtypeStruct((1, KV_RANK), F32),
        ),
        compiler_params=_params("arbitrary"),
    )(p, dq, dk, dv, wq, wkv, gq, gkv, *tables)


def _conv_bwd(p, dcat, conv_w, g_conv, nb_seq, tp):
    cols = CONV_WIDTH // LANES

    def body(b_ref, c_ref, h_ref, z_ref, dy_ref, w_ref, g_ref,
             db_ref, dc_ref, dh_ref, dz_ref, dw_ref, dg_ref):
        @pl.when(pl.program_id(1) == 0)
        def _():
            dw_ref[...] = jnp.zeros_like(dw_ref)
            dg_ref[...] = jnp.zeros_like(dg_ref)

        cb, c, h = b_ref[...].astype(F32), c_ref[...].astype(F32), h_ref[...].astype(F32)
        z, dy = z_ref[...].astype(F32), dy_ref[...].astype(F32)
        g = g_ref[...]
        w0, w1, w2 = w_ref[0:1, :], w_ref[1:2, :], w_ref[2:3, :]
        cc = c * h
        row = lax.broadcasted_iota(jnp.int32, (tp, LANES), 0)
        s1 = jnp.where(row >= 1, pltpu.roll(cc, 1, 0), 0.0)
        s2 = jnp.where(row >= 2, pltpu.roll(cc, 2, 0), 0.0)
        dwc = w0 * s2 + w1 * s1 + w2 * cc
        yc = cb * dwc
        r = lax.rsqrt(_group_mean(yc * yc) + EPS)
        ychat = yc * r
        sig = _sigmoid(z)
        dz_ref[...] = (dy * (ychat * g) * (sig * (1.0 + z * (1.0 - sig)))).astype(BF16)
        dyn = dy * (z * sig)
        dg_ref[...] += jnp.sum(dyn * ychat, axis=0, keepdims=True)
        gd = g * dyn
        dyc = r * (gd - ychat * _group_mean(gd * ychat))
        db_ref[...] = (dyc * dwc).astype(BF16)
        ddw = dyc * cb
        dw_ref[0:1, :] += jnp.sum(ddw * s2, axis=0, keepdims=True)
        dw_ref[1:2, :] += jnp.sum(ddw * s1, axis=0, keepdims=True)
        dw_ref[2:3, :] += jnp.sum(ddw * cc, axis=0, keepdims=True)
        u1 = jnp.where(row <= tp - 2, pltpu.roll(ddw, tp - 1, 0), 0.0)
        u2 = jnp.where(row <= tp - 3, pltpu.roll(ddw, tp - 2, 0), 0.0)
        dcc = w2 * ddw + w1 * u1 + w0 * u2
        dc_ref[...] = (dcc * h).astype(BF16)
        dh_ref[...] = (dcc * c).astype(BF16)

    col = pl.BlockSpec((tp, LANES), lambda t, b: (b, t))
    out = jax.ShapeDtypeStruct((nb_seq * tp, CONV_WIDTH), BF16)
    return pl.pallas_call(
        body,
        name="conv_bwd",
        grid=(cols, nb_seq),
        in_specs=_conv_specs(tp, lambda t, b, off: (b, off + t)) + [
            pl.BlockSpec((tp, LANES), lambda t, b: (b, N_HEADS * D_V // LANES + t)),
            pl.BlockSpec((8, LANES), lambda t, b: (0, t)),
            pl.BlockSpec((1, LANES), lambda t, b: (0, t))],
        out_specs=(col, col, col, col,
                   pl.BlockSpec((8, LANES), lambda t, b: (0, t)), pl.BlockSpec((1, LANES), lambda t, b: (0, t))),
        out_shape=(out, out, out, out,
                   jax.ShapeDtypeStruct((8, CONV_WIDTH), F32), jax.ShapeDtypeStruct((1, CONV_WIDTH), F32)),
        compiler_params=_params("arbitrary", "arbitrary"),
    )(p, p, p, p, dcat, conv_w, g_conv)


def _input_bwd(dps, w_in, x, meta, dh, norm_g, nt, send_in):
    nb_seq, s, d = x.shape
    r, kb = dps[0].shape
    ts = (s + LANES) // nt
    steps = nb_seq * nt
    n_dp = len(dps)
    in_slot = send_in.shape[1:]

    def body(*refs):
        dp_refs, w_ref, x_hbm, meta_ref, dh_ref, g_ref, pay_ref = refs[:n_dp], *refs[n_dp:n_dp + 6]
        o = n_dp + 6
        gx_hbm, dmeta_ref, dg_ref, r2_in, gmeta_ref, gnorm_ref = refs[o:o + 6]
        xbuf, gxbuf, tok_sems, own_in, r1_in, sum_in = refs[o + 6:o + 12]
        sems = refs[o + 12:-4]
        stail, rtail, tail_send, tail_recv = refs[-4:]
        i = pl.program_id(0)
        b, k = i // nt, i % nt

        def plan():
            return _reduce_plan((pay_ref,), (own_in,), (r1_in,), (sum_in,), (r2_in,), *sems)

        @pl.when(i == 0)
        def _():
            dmeta_ref[...] = jnp.zeros_like(dmeta_ref)
            dg_ref[...] = jnp.zeros_like(dg_ref)
            plan()[0]()

        @pl.when(i == 1)
        def _():
            plan()[1]()

        def start(kk):
            if kk == 0:
                xbuf[0:PAD_FRONT, :] = jnp.zeros((PAD_FRONT, d), F32)
                xbuf[PAD_FRONT:LANES, :] = meta_ref[...]
            _token_copy(x_hbm, b, kk, ts, xbuf, tok_sems.at[0]).start()

        _for_tile(k, nt, start)
        du = _dot(dp_refs[0][...], w_ref[0:kb, :])
        for j in range(1, n_dp):
            du = du + _dot(dp_refs[j][...], w_ref[kb * j:kb * (j + 1), :])
        _for_tile(k, nt, lambda kk: _token_copy(x_hbm, b, kk, ts, xbuf, tok_sems.at[0]).wait())

        g = g_ref[...]
        hhat, rstd = _rms_stats(xbuf[...])
        dg_ref[...] += jnp.sum(du * hhat, axis=0, keepdims=True)
        res = _rms_bwd(g * du, hhat, rstd) + dh_ref[...].astype(F32)

        @pl.when(i > 0)
        def _():
            _for_tile(k, nt, lambda kk: _token_copy(gx_hbm, b, (kk - 1) % nt, ts, gxbuf, tok_sems.at[1], True).wait())

        gxbuf[...] = res

        @pl.when(k == 0)
        def _():
            dmeta_ref[...] += gxbuf[PAD_FRONT:LANES, :]

        _for_tile(k, nt, lambda kk: _token_copy(gx_hbm, b, kk, ts, gxbuf, tok_sems.at[1], True).start())

        @pl.when(i == steps - 1)
        def _():
            _token_copy(gx_hbm, b, nt - 1, ts, gxbuf, tok_sems.at[1], True).wait()
            px, py, pc, me = _device_position()
            for dev in range(N_DEV):
                stail[dev, 0:N_META, :] = dmeta_ref[:, SHARD_META * dev:SHARD_META * (dev + 1)]
                for j in range(d // LANES):
                    stail[dev, N_META + j:N_META + j + 1, :] = dg_ref[:, LANES * j:LANES * (j + 1)]
            copies = []
            for r in range(1, N_DEV):
                peer = (1 - px if r & 4 else px, 1 - py if r & 2 else py, 1 - pc if r & 1 else pc)
                copies.append(pltpu.make_async_remote_copy(
                    src_ref=stail.at[4 * peer[0] + 2 * peer[1] + peer[2]],
                    dst_ref=rtail.at[r],
                    send_sem=tail_send.at[r - 1],
                    recv_sem=tail_recv.at[r - 1],
                    device_id=peer,
                    device_id_type=pl.DeviceIdType.MESH,
                ))
            for cp in copies:
                cp.start()
            rtail[0] = stail[me]
            plan()[2]()
            for cp in copies:
                cp.wait_recv()
            gt = rtail[me]
            for dev in range(1, N_DEV):
                gt = gt + rtail[dev ^ me]
            gmeta_ref[...] = gt[0:N_META, :]
            for j in range(d // LANES):
                gnorm_ref[:, LANES * j:LANES * (j + 1)] = gt[N_META + j:N_META + j + 1, :]
            for cp in copies:
                cp.wait_send()

    whole = lambda a: pl.BlockSpec(a.shape, lambda i: (0,) * a.ndim)
    hbm = pl.BlockSpec(memory_space=pl.ANY)
    return pl.pallas_call(
        body,
        name="input_bwd",
        grid=(steps,),
        in_specs=[pl.BlockSpec((ts, kb), lambda i: (i, 0)) for _ in dps]
        + [whole(w_in), hbm, whole(meta), pl.BlockSpec((ts, d), lambda i: (i, 0)), whole(norm_g), hbm],
        out_specs=(hbm, pl.BlockSpec((N_META, d), lambda i: (0, 0)), pl.BlockSpec((1, d), lambda i: (0, 0)), hbm,
                   pl.BlockSpec((N_META, SHARD_META), lambda i: (0, 0)), pl.BlockSpec((1, d), lambda i: (0, 0))),
        out_shape=(jax.ShapeDtypeStruct((nb_seq, s, d), F32),
                   jax.ShapeDtypeStruct((N_META, d), F32),
                   jax.ShapeDtypeStruct((1, d), F32),
                   jax.ShapeDtypeStruct((N_CHIPS,) + in_slot, BF16),
                   jax.ShapeDtypeStruct((N_META, SHARD_META), F32),
                   jax.ShapeDtypeStruct((1, d), F32)),
        scratch_shapes=[pltpu.VMEM((ts, d), F32), pltpu.VMEM((ts, d), F32), pltpu.SemaphoreType.DMA((2,))]
        + _reduce_scratch([(in_slot, BF16)], [True])
        + [pltpu.VMEM((N_DEV, TAIL_ROWS, LANES), F32), pltpu.VMEM((N_DEV, TAIL_ROWS, LANES), F32),
           pltpu.SemaphoreType.DMA((N_DEV - 1,)), pltpu.SemaphoreType.DMA((N_DEV - 1,))],
        compiler_params=_params("arbitrary"),
    )(*dps, w_in, x, meta, dh, norm_g, send_in)


def _in_proj_bwd_w(u, dps, bm, small_grads, send_out):
    r, d = u.shape
    kb = dps[0].shape[1]
    steps = r // bm
    n_dp, n_small = len(dps), len(small_grads)
    out_slot, small_slot = send_out.shape[1:], (SMALL_ROWS, LANES)

    def body(*refs):
        u_ref, dp_refs = refs[0], refs[1:1 + n_dp]
        small_refs = refs[1 + n_dp:1 + n_dp + n_small]
        o = 1 + n_dp + n_small
        pay_out, o_ref, r2_out, r2_small = refs[o:o + 4]
        acc_ref, ssmall, r1_out, sum_out, r1_small, sum_small = refs[o + 4:o + 10]
        sems = refs[o + 10:]
        i = pl.program_id(0)

        def plan():
            return _reduce_plan((pay_out, ssmall), (None, None), (r1_out, r1_small), (sum_out, sum_small),
                                (r2_out, r2_small), *sems)

        @pl.when(i == 0)
        def _():
            acc_ref[...] = jnp.zeros_like(acc_ref)
            _pack_small(ssmall, *small_refs)
            plan()[0]()

        @pl.when(i == 1)
        def _():
            plan()[1]()

        uu = u_ref[...]
        for j in range(n_dp):
            acc_ref[kb * j:kb * (j + 1), :] += _dot(dp_refs[j][...], uu, _TN)

        @pl.when(i == steps - 1)
        def _():
            for k in range(N_DEV):
                for s, e, c0 in _in_pieces(k):
                    o_ref[k, s:e, :] = acc_ref[c0:c0 + e - s, :].astype(BF16)
                o_ref[k, SHARD_IN:, :] = jnp.zeros((SHARD_IN_PAD - SHARD_IN, d), BF16)
            plan()[2]()

    whole = lambda a: pl.BlockSpec(a.shape, lambda i: (0,) * a.ndim)
    hbm = pl.BlockSpec(memory_space=pl.ANY)
    return pl.pallas_call(
        body,
        name="in_proj_bwd_w",
        grid=(steps,),
        in_specs=[pl.BlockSpec((bm, d), lambda i: (i, 0))]
        + [pl.BlockSpec((bm, kb), lambda i: (i, 0)) for _ in dps] + [whole(a) for a in small_grads]
        + [whole(send_out)],
        out_specs=(pl.BlockSpec((N_DEV, SHARD_IN_PAD, d), lambda i: (0, 0, 0)), hbm, hbm),
        out_shape=(jax.ShapeDtypeStruct((N_DEV, SHARD_IN_PAD, d), BF16),
                   jax.ShapeDtypeStruct((N_CHIPS,) + out_slot, BF16),
                   jax.ShapeDtypeStruct((N_CHIPS,) + small_slot, F32)),
        scratch_shapes=[pltpu.VMEM((kb * n_dp, d), F32), pltpu.VMEM((N_DEV,) + small_slot, F32)]
        + _reduce_scratch([(out_slot, BF16), (small_slot, F32)], [False, False]),
        compiler_params=_params("arbitrary"),
    )(u, *dps, *small_grads, send_out)


def _local_step(x, loss_target, u, p, meta_f, norm_g, w_in_p, q_norm_g, w_q_p, kv_norm_g, w_kv_p, conv_w_f,
                attn_out_g, conv_out_g, w_out_f, g_final):
    nb_seq, s, d = x.shape
    tp = s + LANES
    ht = tp // 2
    tables = _rope_tables(tp)

    q, k, v = _qkv_fwd(p, w_q_p, w_kv_p, q_norm_g, kv_norm_g, tables, nb_seq, tp)
    ya, o, lse = _attn_fwd(q, k, v, p, attn_out_g)
    yc = _conv_fwd(p, conv_w_f, conv_out_g, nb_seq, tp)
    dhb, d_final_g, loss_part = _out_proj_loss(ya, yc, w_out_f, x, loss_target, g_final, TOKEN_TILES)

    dcat, d_w_out = _out_proj_bwd(dhb, w_out_f, ya, yc, ht)
    send_out = d_w_out.reshape(N_DEV, SHARD_OUT, d)
    dq, dk, dv, dz_attn, d_attn_g = _attn_bwd(q, k, v, o, lse, dcat, p, attn_out_g)
    dpa, d_wq_p, d_wkv_p, d_gq, d_gkv = _qkv_bwd(p, dq, dk, dv, w_q_p, w_kv_p, q_norm_g, kv_norm_g, tables)
    d_b, d_c, d_h, dz_conv, d_conv_w, d_conv_g = _conv_bwd(p, dcat, conv_w_f, conv_out_g, nb_seq, tp)
    dps = (dpa, dz_attn, d_b, d_c, d_h, dz_conv)
    small = (d_wq_p, d_wkv_p, d_conv_w, d_final_g, d_gq, d_gkv, d_attn_g, d_conv_g, loss_part)
    send_in, r_out, r_small = _in_proj_bwd_w(u, dps, ht, small, send_out)
    grad_x, _, _, r_in, g_meta, g_norm = _input_bwd(dps, w_in_p, x, meta_f, dhb, norm_g, TOKEN_TILES, send_in)
    return grad_x, r_in, r_out, r_small, g_meta, g_norm


def kernel(x, meta_tokens, norm_g, w_in, q_norm_g, w_q_up, kv_norm_g, w_kv_up, conv_w, attn_out_g, conv_out_g, w_out, final_norm_g, loss_target, m_meta_tokens, m_norm_g, m_w_in, m_q_norm_g, m_w_q_up, m_kv_norm_g, m_w_kv_up, m_conv_w, m_attn_out_g, m_conv_out_g, m_w_out, m_final_norm_g, v_meta_tokens, v_norm_g, v_w_in, v_q_norm_g, v_w_q_up, v_kv_norm_g, v_w_kv_up, v_conv_w, v_attn_out_g, v_conv_out_g, v_w_out, v_final_norm_g):
    d = x.shape[-1]
    order = jnp.asarray(_tile_orders()[0].reshape(-1))
    u, p, w_in_p, meta_f, w_q_p, w_kv_p, w_out_f, conv_w_f = _prep_in_proj(
        x, meta_tokens, norm_g, w_in[0].T, w_q_up[0].T, w_kv_up[0], w_out[0], conv_w.transpose(1, 0, 2), order)
    g_final = final_norm_g.reshape(1, d)
    grad_x, r_in, r_out, r_small, g_meta, g_norm = _local_step(
        x, loss_target, u, p, meta_f, norm_g, w_in_p, q_norm_g, w_q_p, kv_norm_g, w_kv_p, conv_w_f,
        attn_out_g, conv_out_g, w_out_f, g_final)

    flat = lambda a: a.reshape(a.shape[-2:]) if a.ndim == 3 else a.reshape(1, -1) if a.ndim == 1 else a
    transposed = ("w_in", "w_q_up")

    def to_kernel(n, a):
        if n == "conv_w":
            return a.transpose(1, 0, 2)
        return flat(a).T if n in transposed else flat(a)

    def from_kernel(n, a, shape):
        if n == "conv_w":
            return a.transpose(1, 0, 2)
        return (a.T if n in transposed else a).reshape(shape)
    params = {
        "meta_tokens": (meta_tokens, m_meta_tokens, v_meta_tokens),
        "norm_g": (norm_g, m_norm_g, v_norm_g),
        "w_in": (w_in, m_w_in, v_w_in),
        "q_norm_g": (q_norm_g, m_q_norm_g, v_q_norm_g),
        "w_q_up": (w_q_up, m_w_q_up, v_w_q_up),
        "kv_norm_g": (kv_norm_g, m_kv_norm_g, v_kv_norm_g),
        "w_kv_up": (w_kv_up, m_w_kv_up, v_w_kv_up),
        "conv_w": (conv_w, m_conv_w, v_conv_w),
        "attn_out_g": (attn_out_g, m_attn_out_g, v_attn_out_g),
        "conv_out_g": (conv_out_g, m_conv_out_g, v_conv_out_g),
        "w_out": (w_out, m_w_out, v_w_out),
        "final_norm_g": (final_norm_g, m_final_norm_g, v_final_norm_g),
    }
    grads, loss = _reduce_tail(r_in, r_out, r_small, g_meta, g_norm)
    updated = _adamw(grads, {n: tuple(to_kernel(n, a) for a in t) for n, t in params.items()})
    outs = [[from_kernel(n, updated[n][i], params[n][0].shape) for n, _ in PARAM_SHAPES] for i in range(4)]
    return (loss[0, 0], grad_x, *outs[0], *outs[1], *outs[2], *outs[3])
```

```python
import functools

import jax
import jax.numpy as jnp
import numpy as np
from jax import lax
from jax.experimental import pallas as pl
from jax.experimental.pallas import tpu as pltpu

F32 = jnp.float32
BF16 = jnp.bfloat16

N_META = 16
D_MODEL = 1024
N_HEADS = 4
D_NOPE = 128
D_ROPE = 64
D_V = 128
Q_RANK = 256
KV_RANK = 128
CONV_WIDTH = 512
CONV_GROUP = 64
ROPE_THETA = 10000.0
ATTN_SCALE = (D_NOPE + D_ROPE) ** -0.5
Q_SCALE = ATTN_SCALE * 1.4426950408889634
EPS = 1e-6
NEG_INF = -1e30

ADAM_LR = 0.001
ADAM_B1 = 0.9
ADAM_B2 = 0.999
ADAM_EPS = 1e-08
ADAM_WD = 0.01
ADAM_STEP = 10

LANES = 128
PAD_FRONT = LANES - N_META
K_TILE = 256
Q_TILE = 512
N_DEV = 8
VMEM_LIMIT = 56 * 1024 * 1024

IN_PAD = 3072
GRP_A = 512
N_A = Q_RANK + KV_RANK + D_ROPE
IN_PROJ = 3008
SHARD_IN = IN_PROJ // N_DEV
SHARD_IN_PAD = 384
SHARD_Q = 96
SHARD_KV = 128
SHARD_OUT = 128
SHARD_CONV = 64
SHARD_META = 128
Q_COLS = N_HEADS * (D_NOPE + D_ROPE)
KV_COLS = N_HEADS * (D_NOPE + D_V)

ROW_Q, ROW_KV, ROW_META, ROW_CONV = 0, 256, 384, 400
ROW_REPL = 408
ROW_NORM, ROW_FINAL, ROW_GQ, ROW_GKV, ROW_ATTN, ROW_CONVG, ROW_LOSS = 408, 416, 424, 426, 427, 431, 435
SMALL_ROWS = 440

PARAM_SHAPES = (
    ("meta_tokens", (N_META, SHARD_META)), ("norm_g", (1, D_MODEL)), ("w_in", (SHARD_IN, D_MODEL)),
    ("q_norm_g", (1, Q_RANK)), ("w_q_up", (SHARD_Q, Q_RANK)), ("kv_norm_g", (1, KV_RANK)),
    ("w_kv_up", (KV_RANK, SHARD_KV)), ("conv_w", (3, 1, SHARD_CONV)), ("attn_out_g", (1, CONV_WIDTH)),
    ("conv_out_g", (1, CONV_WIDTH)), ("w_out", (SHARD_OUT, D_MODEL)), ("final_norm_g", (1, D_MODEL)),
)


def _in_pieces(k):
    lo, hi = SHARD_IN * k, SHARD_IN * (k + 1)
    out = []
    if lo < N_A:
        out.append((0, min(hi, N_A) - lo, lo))
    if hi > N_A:
        s = max(lo, N_A)
        out.append((s - lo, hi - lo, s + GRP_A - N_A))
    return out


P_TILE = 256


def _tile_pieces(j):
    lo, hi = P_TILE * j, P_TILE * (j + 1)
    out = []
    for k in range(N_DEV):
        for s, e, d in _in_pieces(k):
            a, b = max(d, lo), min(d + e - s, hi)
            if a < b:
                out.append((k, s + a - d, s + b - d, a - lo))
    return out


def _tile_orders():
    n_tiles = IN_PAD // P_TILE
    sources = [{k for k, _, _, _ in _tile_pieces(j)} for j in range(n_tiles)]
    rows, n_early, n_free = [], n_tiles, n_tiles
    for chip in range(N_CHIPS):
        own = {2 * chip, 2 * chip + 1}
        diagonal = {2 * (N_CHIPS - 1 - chip), 2 * (N_CHIPS - 1 - chip) + 1}
        early = [j for j in range(n_tiles) if sources[j] <= own]
        late = [j for j in range(n_tiles) if sources[j] & diagonal]
        mid = [j for j in range(n_tiles) if j not in early and j not in late]
        rows.append(early + mid + late)
        n_early, n_free = min(n_early, len(early)), min(n_free, len(early) + len(mid))
    return np.asarray(rows, np.int32), n_early, n_free


def _q_pieces(k):
    lo, hi = SHARD_Q * k, SHARD_Q * (k + 1)
    out = []
    for h in range(N_HEADS):
        base = (D_NOPE + D_ROPE) * h
        s, e = max(lo, base), min(hi, base + D_NOPE)
        if s < e:
            out.append((s - lo, e - lo, D_NOPE * h + s - base))
        s, e = max(lo, base + D_NOPE), min(hi, base + D_NOPE + D_ROPE)
        if s < e:
            out.append((s - lo, e - lo, N_HEADS * D_NOPE + D_ROPE * h + s - base - D_NOPE))
    return out


def _kv_dst(k):
    return D_NOPE * (k // 2) + (N_HEADS * D_NOPE if k % 2 else 0)


def _params(*sem):
    return pltpu.CompilerParams(dimension_semantics=sem, vmem_limit_bytes=VMEM_LIMIT)


def _rms_stats(x):
    r = lax.rsqrt(jnp.mean(x * x, axis=-1, keepdims=True) + EPS)
    return x * r, r


def _rms_bwd(gdy, xhat, r):
    return r * (gdy - xhat * jnp.mean(gdy * xhat, axis=-1, keepdims=True))


def _sigmoid(z):
    return 1.0 / (1.0 + jnp.exp(-z))


def _group_mean(x):
    i0 = lax.broadcasted_iota(jnp.int32, (LANES, LANES), 0) // CONV_GROUP
    i1 = lax.broadcasted_iota(jnp.int32, (LANES, LANES), 1) // CONV_GROUP
    m = jnp.where(i0 == i1, 1.0 / CONV_GROUP, 0.0).astype(BF16)
    hi = x.astype(BF16)
    lo = (x - hi.astype(F32)).astype(BF16)
    return jnp.dot(hi, m, preferred_element_type=F32) + jnp.dot(lo, m, preferred_element_type=F32)


_NT = (((1,), (1,)), ((), ()))
_TN = (((0,), (0,)), ((), ()))


def _dot(a, b, dims=None):
    if dims is None:
        return jnp.dot(a, b, preferred_element_type=F32)
    return lax.dot_general(a, b, dims, preferred_element_type=F32)


def _device_position():
    x, y, c = lax.axis_index("x"), lax.axis_index("y"), lax.axis_index("c")
    return x, y, c, 4 * x + 2 * y + c


def _gather_plan(srcs, slots, send_sems, recv_sems, local_sems):
    x, y, c, _ = _device_position()
    me, sibling = (x, y, c), (x, y, 1 - c)
    flip = lambda v, on: v + on - 2 * v * on
    near = (flip(x, 1 - c), flip(y, c))
    far = (flip(x, c), flip(y, 1 - c))
    diag = (1 - x, 1 - y)
    n = len(srcs)

    def slot(a, px, py, pc):
        return slots[a].at[4 * px + 2 * py + pc]

    def copy(a, k, block, to, own=False):
        return pltpu.make_async_remote_copy(
            src_ref=srcs[a] if own else slot(a, *block),
            dst_ref=slot(a, *block),
            send_sem=send_sems.at[7 * a + k],
            recv_sem=recv_sems.at[7 * a + k],
            device_id=to,
            device_id_type=pl.DeviceIdType.MESH,
        )

    def local(a):
        return pltpu.make_async_copy(srcs[a], slot(a, *me), local_sems.at[a])

    sent = [(me, sibling), (me, (*near, c)), (me, (*far, c)), ((*near, c), (*far, c)),
            ((*near, c), sibling), ((*far, c), sibling), ((*diag, c), sibling)]
    landed = [sibling, (*near, c), (*far, c), (*diag, c), (*far, 1 - c), (*near, 1 - c), (*diag, 1 - c)]

    def send(a, k):
        return copy(a, k, *sent[k], own=k < 3)

    def arrival(a, k):
        return copy(a, k, landed[k], me)

    def start():
        for a in range(n):
            local(a).start()
            for k in range(3):
                send(a, k).start()

    def own():
        for a in range(n):
            local(a).wait()
            arrival(a, 0).wait_recv()

    def mid():
        for a in range(n):
            arrival(a, 1).wait_recv()
            send(a, 3).start()
            send(a, 4).start()
        for a in range(n):
            arrival(a, 2).wait_recv()
            send(a, 5).start()
        for a in range(n):
            for k in (4, 5):
                arrival(a, k).wait_recv()

    def late():
        for a in range(n):
            arrival(a, 3).wait_recv()
            send(a, 6).start()
        for a in range(n):
            arrival(a, 6).wait_recv()

    def finish():
        for a in range(n):
            for k in range(7):
                send(a, k).wait_send()

    return start, own, mid, late, finish


def _adam_update(g, w, m, v):
    m_new = ADAM_B1 * m + (1.0 - ADAM_B1) * g
    v_new = ADAM_B2 * v + (1.0 - ADAM_B2) * (g * g)
    m_hat = m_new / (1.0 - ADAM_B1 ** ADAM_STEP)
    v_hat = v_new / (1.0 - ADAM_B2 ** ADAM_STEP)
    return -ADAM_LR * (m_hat / (jnp.sqrt(v_hat) + ADAM_EPS) + ADAM_WD * w), m_new, v_new


N_CHIPS = 4


def _reduce_plan(pays, owns, r1s, sums, r2s, send1, recv1, send2, recv2, local_sems):
    x, y, c, _ = _device_position()
    sibling = (x, y, 1 - c)
    chips = [((1 - x if rj & 2 else x), (1 - y if rj & 1 else y)) for rj in range(N_CHIPS)]
    n = len(pays)

    def slot_of(rj, core):
        return 4 * chips[rj][0] + 2 * chips[rj][1] + core

    def to_sibling(a, rj):
        return pltpu.make_async_remote_copy(
            src_ref=pays[a].at[slot_of(rj, 1 - c)], dst_ref=r1s[a].at[rj],
            send_sem=send1.at[N_CHIPS * a + rj], recv_sem=recv1.at[N_CHIPS * a + rj],
            device_id=sibling, device_id_type=pl.DeviceIdType.MESH)

    def load_own(a, rj):
        return pltpu.make_async_copy(pays[a].at[slot_of(rj, c)], owns[a].at[rj], local_sems.at[2 * N_CHIPS * a + rj])

    def to_chip(a, rj):
        return pltpu.make_async_remote_copy(
            src_ref=sums[a].at[rj], dst_ref=r2s[a].at[rj],
            send_sem=send2.at[N_CHIPS * a + rj], recv_sem=recv2.at[N_CHIPS * a + rj],
            device_id=(*chips[rj], c), device_id_type=pl.DeviceIdType.MESH)

    def keep(a):
        return pltpu.make_async_copy(sums[a].at[0], r2s[a].at[0], local_sems.at[2 * N_CHIPS * a + N_CHIPS])

    def start():
        for a in range(n):
            for rj in range(N_CHIPS):
                to_sibling(a, rj).start()
                if owns[a] is not None:
                    load_own(a, rj).start()

    def combine():
        for a in range(n):
            for rj in range(N_CHIPS):
                to_sibling(a, rj).wait_recv()
                if owns[a] is not None:
                    load_own(a, rj).wait()
                    mine = owns[a][rj]
                else:
                    mine = pays[a][slot_of(rj, c)]
                sums[a][rj] = (mine.astype(F32) + r1s[a][rj].astype(F32)).astype(sums[a].dtype)
            keep(a).start()
            for rj in range(1, N_CHIPS):
                to_chip(a, rj).start()

    def finish():
        for a in range(n):
            for rj in range(1, N_CHIPS):
                to_chip(a, rj).wait_recv()
            for rj in range(N_CHIPS):
                to_sibling(a, rj).wait_send()
            for rj in range(1, N_CHIPS):
                to_chip(a, rj).wait_send()
            keep(a).wait()

    return start, combine, finish


def _reduce_scratch(shapes_dtypes, own_flags):
    out = []
    for (shape, dtype), own in zip(shapes_dtypes, own_flags):
        if own:
            out.append(pltpu.VMEM((N_CHIPS,) + shape, dtype))
        out += [pltpu.VMEM((N_CHIPS,) + shape, dtype), pltpu.VMEM((N_CHIPS,) + shape, dtype)]
    n = len(shapes_dtypes)
    out += [pltpu.SemaphoreType.DMA((N_CHIPS * n,))] * 4 + [pltpu.SemaphoreType.DMA((2 * N_CHIPS * n,))]
    return out


def _pack_small(ssmall, dwq, dwkv, dconv, dfinal, dgq, dgkv, dattn, dconvg, loss_part):
    ssmall[...] = jnp.zeros_like(ssmall)
    rep = ssmall.at[0]
    for i in range(D_MODEL // LANES):
        rep[ROW_FINAL + i:ROW_FINAL + i + 1, :] = dfinal[:, LANES * i:LANES * (i + 1)]
    for i in range(Q_RANK // LANES):
        rep[ROW_GQ + i:ROW_GQ + i + 1, :] = dgq[:, LANES * i:LANES * (i + 1)]
    rep[ROW_GKV:ROW_GKV + 1, :] = dgkv[...]
    for i in range(CONV_WIDTH // LANES):
        rep[ROW_ATTN + i:ROW_ATTN + i + 1, :] = dattn[:, LANES * i:LANES * (i + 1)]
        rep[ROW_CONVG + i:ROW_CONVG + i + 1, :] = dconvg[:, LANES * i:LANES * (i + 1)]
    rep[ROW_LOSS:ROW_LOSS + 1, :] = loss_part[...]
    for k in range(N_DEV):
        if k:
            ssmall[k, ROW_REPL:, :] = ssmall[0, ROW_REPL:, :]
        for s, e, d in _q_pieces(k):
            for i in range(Q_RANK // LANES):
                ssmall[k, ROW_Q + SHARD_Q * i + s:ROW_Q + SHARD_Q * i + e, :] = dwq[d:d + e - s, LANES * i:LANES * (i + 1)]
        ssmall[k, ROW_KV:ROW_KV + KV_RANK, :] = dwkv[:, _kv_dst(k):_kv_dst(k) + SHARD_KV]
        ssmall[k, ROW_CONV:ROW_CONV + 3, 0:SHARD_CONV] = dconv[0:3, SHARD_CONV * k:SHARD_CONV * (k + 1)]


TOKEN_TILES = 4
TAIL_ROWS = N_META + D_MODEL // LANES


def _reduce_update(r_in, r_out, r_small, g_meta, g_norm, params):
    n_p = len(PARAM_SHAPES)
    names = [n for n, _ in PARAM_SHAPES]

    def body(*refs):
        rin, rout, rsmall, gmeta, gnorm = refs[:5]
        par = refs[5:5 + 3 * n_p]
        o = 5 + 3 * n_p
        g_out = {n: refs[o + i] for i, n in enumerate(names)}
        loss_out = refs[o + n_p]
        upd = refs[o + n_p + 1:o + 4 * n_p + 1]
        gsum = refs[o + 4 * n_p + 1]
        x, y, c, me = _device_position()
        my_chip = 2 * x + y

        g = rin[my_chip].astype(F32)
        for ch in range(1, N_CHIPS):
            g = g + rin[ch ^ my_chip].astype(F32)
        g_out["w_in"][...] = g[:SHARD_IN, :]

        g = rout[my_chip].astype(F32)
        gs = rsmall[my_chip]
        for ch in range(1, N_CHIPS):
            g = g + rout[ch ^ my_chip].astype(F32)
            gs = gs + rsmall[ch ^ my_chip]
        g_out["w_out"][...] = g
        gsum[...] = gs
        for i in range(Q_RANK // LANES):
            g_out["w_q_up"][:, LANES * i:LANES * (i + 1)] = gsum[ROW_Q + SHARD_Q * i:ROW_Q + SHARD_Q * (i + 1), :]
        g_out["w_kv_up"][...] = gsum[ROW_KV:ROW_KV + KV_RANK, :]
        for i in range(3):
            g_out["conv_w"][i] = gsum[ROW_CONV + i:ROW_CONV + i + 1, 0:SHARD_CONV]
        for name, row, width in (("final_norm_g", ROW_FINAL, D_MODEL), ("q_norm_g", ROW_GQ, Q_RANK),
                                 ("kv_norm_g", ROW_GKV, KV_RANK), ("attn_out_g", ROW_ATTN, CONV_WIDTH),
                                 ("conv_out_g", ROW_CONVG, CONV_WIDTH)):
            for i in range(width // LANES):
                g_out[name][:, LANES * i:LANES * (i + 1)] = gsum[row + i:row + i + 1, :]
        loss_out[...] = gsum[ROW_LOSS:ROW_LOSS + 1, :]
        g_out["meta_tokens"][...] = gmeta[...]
        g_out["norm_g"][...] = gnorm[...]

        for i, n in enumerate(names):
            w, m, v = (par[3 * i + j][...] for j in range(3))
            for j, val in enumerate(_adam_update(g_out[n][...], w, m, v)):
                upd[3 * i + j][...] = val

    vm = pl.BlockSpec(memory_space=pltpu.VMEM)
    out_shape = [jax.ShapeDtypeStruct(shape, F32) for _, shape in PARAM_SHAPES]
    out_shape.append(jax.ShapeDtypeStruct((1, LANES), F32))
    for _, shape in PARAM_SHAPES:
        out_shape += [jax.ShapeDtypeStruct(shape, F32)] * 3
    outs = pl.pallas_call(
        body,
        name="reduce_update",
        out_shape=tuple(out_shape),
        in_specs=[vm] * (5 + 3 * n_p),
        out_specs=(vm,) * len(out_shape),
        scratch_shapes=[pltpu.VMEM((SMALL_ROWS, LANES), F32)],
        compiler_params=pltpu.CompilerParams(vmem_limit_bytes=VMEM_LIMIT),
    )(r_in, r_out, r_small, g_meta, g_norm, *[a for n in names for a in params[n]])
    return {n: (outs[i], *outs[n_p + 1 + 3 * i:n_p + 4 + 3 * i]) for i, n in enumerate(names)}, outs[n_p]


def _prep_in_proj(x, meta, norm_g, w_in_t, w_q, w_kv, w_out, conv_w, order):
    nb_seq, s, d = x.shape
    tp = s + LANES
    m = nb_seq * tp
    ts = s // TOKEN_TILES
    n_real = nb_seq * TOKEN_TILES
    n_norm = n_real + 1
    n_tiles = IN_PAD // P_TILE
    _, n_early, n_free = _tile_orders()
    steps = n_norm + n_tiles
    dot_rows = m // 4
    qkv_shape = (SHARD_Q + KV_RANK, Q_RANK)

    def tile(t):
        t = jnp.minimum(t, n_real - 1)
        return t // TOKEN_TILES, t % TOKEN_TILES

    def column_tile(t, order_ref):
        chip = 2 * lax.axis_index("x") + lax.axis_index("y")
        return order_ref[chip * n_tiles + jnp.maximum(t - n_norm, 0)]

    def body(order_ref, x_ref, meta_ref, g_ref, win_ref, wq_ref, wkv_ref, wout_ref, conv_ref,
             u_hbm, p_ref, w_in_p, meta_f, w_q_p, w_kv_p, w_out_f, conv_f,
             u_all, sbig, gbig, smeta, gmeta, sqkv, sout, sconv, gqkv, gout, gconv,
             send_in, recv_in, local_in, send_meta, recv_meta, send_rest, recv_rest, local_rest, u_sem):
        t = pl.program_id(0)
        px, py, pc, me = _device_position()
        u_copy = pltpu.make_async_copy(u_all, u_hbm, u_sem)

        def plan_in():
            return _gather_plan((sbig,), (gbig,), send_in, recv_in, local_in)

        def plan_rest():
            return _gather_plan((sqkv, sout, sconv), (gqkv, gout, gconv), send_rest, recv_rest, local_rest)

        def meta_copies():
            out = []
            for r in range(1, N_DEV):
                peer = (1 - px if r & 4 else px, 1 - py if r & 2 else py, 1 - pc if r & 1 else pc)
                out.append(pltpu.make_async_remote_copy(
                    src_ref=smeta,
                    dst_ref=gmeta.at[r],
                    send_sem=send_meta.at[r - 1],
                    recv_sem=recv_meta.at[r - 1],
                    device_id=peer,
                    device_id_type=pl.DeviceIdType.MESH,
                ))
            return out

        @pl.when(t == 0)
        def _():
            sbig[0:SHARD_IN, :] = win_ref[...].astype(BF16)
            sbig[SHARD_IN:, :] = jnp.zeros((SHARD_IN_PAD - SHARD_IN, d), BF16)
            smeta[...] = meta_ref[...]
            plan_in()[0]()
            for cp in meta_copies():
                cp.start()
            sqkv[...] = jnp.zeros_like(sqkv)
            sqkv[0:SHARD_Q, :] = wq_ref[...].astype(BF16)
            sqkv[SHARD_Q:, 0:SHARD_KV] = wkv_ref[...].astype(BF16)
            sout[...] = wout_ref[...].astype(BF16)
            sconv[...] = jnp.zeros_like(sconv)
            for i in range(3):
                sconv[i:i + 1, 0:SHARD_CONV] = conv_ref[i]

        def norm(h):
            hhat, _ = _rms_stats(h)
            return (hhat * g_ref[...]).astype(BF16)

        @pl.when(t < n_real)
        def _():
            b, k = tile(t)
            row0 = pl.multiple_of(b * tp + LANES + k * ts, 16)
            u_all[pl.ds(row0, ts), :] = norm(x_ref[0])

        @pl.when(t == n_real)
        def _():
            for cp in meta_copies():
                cp.wait_recv()
            gmeta[0] = smeta[...]
            for k in range(N_DEV):
                meta_f[:, SHARD_META * k:SHARD_META * (k + 1)] = gmeta[k ^ me]
            um = norm(meta_f[...])
            for b in range(nb_seq):
                u_all[b * tp:b * tp + PAD_FRONT, :] = jnp.zeros((PAD_FRONT, d), BF16)
                u_all[b * tp + PAD_FRONT:b * tp + LANES, :] = um
            u_copy.start()

        @pl.when(t == n_norm)
        def _():
            plan_in()[1]()

        @pl.when(t == n_norm + n_early)
        def _():
            plan_in()[2]()
            plan_rest()[0]()

        @pl.when(t == n_norm + (n_early + n_free) // 2)
        def _():
            plan_rest()[1]()
            plan_rest()[2]()

        @pl.when(t == n_norm + n_free)
        def _():
            plan_in()[3]()

        @pl.when(t == n_norm + n_free + 2)
        def _():
            plan_rest()[3]()

        @pl.when(t >= n_norm)
        def _():
            j = column_tile(t, order_ref)
            for jj in range(n_tiles):
                @pl.when(j == jj)
                def _(jj=jj):
                    if jj == N_A // P_TILE:
                        w_in_p[N_A % P_TILE:, :] = jnp.zeros((P_TILE - N_A % P_TILE, d), BF16)
                    for k, s0, e0, d0 in _tile_pieces(jj):
                        w_in_p[d0:d0 + e0 - s0, :] = gbig[k, s0:e0, :]
            for r in range(m // dot_rows):
                rows = slice(dot_rows * r, dot_rows * (r + 1))
                p_ref[rows, :] = _dot(u_all[rows, :], w_in_p[...], _NT).astype(BF16)

        @pl.when(t == steps - 1)
        def _():
            plan_in()[4]()
            plan_rest()[4]()
            for cp in meta_copies():
                cp.wait_send()
            u_copy.wait()
            conv_f[...] = jnp.zeros_like(conv_f)
            for k in range(N_DEV):
                for s0, e0, d0 in _q_pieces(k):
                    w_q_p[d0:d0 + e0 - s0, :] = gqkv[k, s0:e0, :]
                w_kv_p[:, _kv_dst(k):_kv_dst(k) + SHARD_KV] = gqkv[k, SHARD_Q:, 0:SHARD_KV]
                w_out_f[SHARD_OUT * k:SHARD_OUT * (k + 1), :] = gout[k]
                conv_f[0:3, SHARD_CONV * k:SHARD_CONV * (k + 1)] = gconv[k, 0:3, 0:SHARD_CONV]

    whole = lambda shape: pl.BlockSpec(shape, lambda t, o: (0,) * len(shape))
    return pl.pallas_call(
        body,
        name="prep_in_proj_gather",
        grid_spec=pltpu.PrefetchScalarGridSpec(
            num_scalar_prefetch=1,
            grid=(steps,),
            in_specs=[
                pl.BlockSpec((1, ts, d), lambda t, o: (*tile(t), 0)),
                whole(meta.shape), whole(norm_g.shape), whole(w_in_t.shape),
                whole(w_q.shape), whole(w_kv.shape), whole(w_out.shape), whole(conv_w.shape),
            ],
            out_specs=(pl.BlockSpec(memory_space=pl.ANY),
                       pl.BlockSpec((m, P_TILE), lambda t, o: (0, column_tile(t, o))),
                       pl.BlockSpec((P_TILE, d), lambda t, o: (column_tile(t, o), 0)),
                       whole((N_META, d)),
                       whole((Q_COLS, Q_RANK)), whole((KV_RANK, KV_COLS)), whole((D_MODEL, D_MODEL)),
                       whole((8, CONV_WIDTH))),
            scratch_shapes=[
                pltpu.VMEM((m, d), BF16),
                pltpu.VMEM((SHARD_IN_PAD, d), BF16),
                pltpu.VMEM((N_DEV, SHARD_IN_PAD, d), BF16),
                pltpu.VMEM((N_META, SHARD_META), F32),
                pltpu.VMEM((N_DEV, N_META, SHARD_META), F32),
                pltpu.VMEM(qkv_shape, BF16),
                pltpu.VMEM((SHARD_OUT, D_MODEL), BF16),
                pltpu.VMEM((8, LANES), F32),
                pltpu.VMEM((N_DEV,) + qkv_shape, BF16),
                pltpu.VMEM((N_DEV, SHARD_OUT, D_MODEL), BF16),
                pltpu.VMEM((N_DEV, 8, LANES), F32),
                pltpu.SemaphoreType.DMA((7,)),
                pltpu.SemaphoreType.DMA((7,)),
                pltpu.SemaphoreType.DMA((1,)),
                pltpu.SemaphoreType.DMA((N_DEV - 1,)),
                pltpu.SemaphoreType.DMA((N_DEV - 1,)),
                pltpu.SemaphoreType.DMA((21,)),
                pltpu.SemaphoreType.DMA((21,)),
                pltpu.SemaphoreType.DMA((3,)),
                pltpu.SemaphoreType.DMA,
            ],
        ),
        out_shape=(jax.ShapeDtypeStruct((m, d), BF16),
                   jax.ShapeDtypeStruct((m, IN_PAD), BF16),
                   jax.ShapeDtypeStruct((IN_PAD, d), BF16),
                   jax.ShapeDtypeStruct((N_META, d), F32),
                   jax.ShapeDtypeStruct((Q_COLS, Q_RANK), BF16),
                   jax.ShapeDtypeStruct((KV_RANK, KV_COLS), BF16),
                   jax.ShapeDtypeStruct((D_MODEL, D_MODEL), BF16),
                   jax.ShapeDtypeStruct((8, CONV_WIDTH), F32)),
        compiler_params=_params("arbitrary"),
    )(order, x, meta, norm_g, w_in_t, w_q, w_kv, w_out, conv_w)


def _rope_tables(tp):
    half = D_ROPE // 2
    inv_freq = (1.0 / (ROPE_THETA ** (np.arange(half, dtype=np.float32) / half))).astype(np.float32)
    pos = (np.arange(tp) - PAD_FRONT).astype(np.float32)
    ang = pos[:, None] * inv_freq[None, :]
    cos = np.tile(np.cos(ang), (1, LANES // half))
    sin = np.tile(np.sin(ang), (1, LANES // half))
    first = (np.arange(LANES) % D_ROPE) < half
    zero = np.float32(0.0)
    return tuple(jnp.asarray(t, F32) for t in (cos, np.where(first, -sin, zero), np.where(first, zero, sin)))


def _rope(t, cos, sa, sb):
    return t * cos + pltpu.roll(t, LANES - D_ROPE // 2, 1) * sa + pltpu.roll(t, D_ROPE // 2, 1) * sb


def _rope_t(t, cos, sa, sb):
    return t * cos + pltpu.roll(t * sa, D_ROPE // 2, 1) + pltpu.roll(t * sb, LANES - D_ROPE // 2, 1)


def _qkv_fwd(p, wq, wkv, gq, gkv, tables, nb_seq, tp):
    ht = tp // 2

    def body(pa_ref, wq_ref, wkv_ref, gq_ref, gkv_ref, cos_ref, sa_ref, sb_ref, q_ref, k_ref, v_ref):
        pa = pa_ref[...].astype(F32)
        cq_hat, _ = _rms_stats(pa[:, :Q_RANK])
        ckv_hat, _ = _rms_stats(pa[:, Q_RANK:Q_RANK + KV_RANK])
        q = _dot((cq_hat * gq_ref[...]).astype(BF16), wq_ref[...], _NT) * Q_SCALE
        kv = _dot((ckv_hat * gkv_ref[...]).astype(BF16), wkv_ref[...])
        tabs = (cos_ref[...], sa_ref[...], sb_ref[...])
        lane = lax.broadcasted_iota(jnp.int32, (ht, LANES), 1)
        low = lane < D_ROPE
        mark = lane == D_ROPE
        row = (pl.program_id(0) % 2) * ht + lax.broadcasted_iota(jnp.int32, (ht, LANES), 0)
        k_pe = jnp.where(mark & (row < PAD_FRONT), NEG_INF, _rope(pa[:, Q_RANK + KV_RANK:], *tabs))
        one = jnp.where(mark & (row >= PAD_FRONT), 1.0, 0.0)
        pairs = [_rope(q[:, N_HEADS * D_NOPE + LANES * i:N_HEADS * D_NOPE + LANES * (i + 1)], *tabs) for i in range(2)]
        for h in range(N_HEADS):
            pair = pairs[h // 2]
            if h % 2:
                pair = pltpu.roll(pair, D_ROPE, 1)
            pe = jnp.where(low, pair, one)
            q_ref[0, h] = jnp.concatenate([q[:, D_NOPE * h:D_NOPE * (h + 1)], pe], axis=1).astype(BF16)
            k_ref[0, h] = jnp.concatenate([kv[:, D_NOPE * h:D_NOPE * (h + 1)], k_pe], axis=1).astype(BF16)
            v_ref[0, h] = kv[:, N_HEADS * D_NOPE + D_V * h:N_HEADS * D_NOPE + D_V * (h + 1)].astype(BF16)

    full = lambda a: pl.BlockSpec(a.shape, lambda i: (0,) * a.ndim)
    tab = pl.BlockSpec((ht, LANES), lambda i: (i % 2, 0))
    qk = pl.BlockSpec((1, N_HEADS, ht, 2 * LANES), lambda i: (i // 2, 0, i % 2, 0))
    return pl.pallas_call(
        body,
        name="qkv_fwd",
        grid=(2 * nb_seq,),
        in_specs=[pl.BlockSpec((ht, GRP_A), lambda i: (i, 0)), full(wq), full(wkv), full(gq), full(gkv), tab, tab, tab],
        out_specs=(qk, qk, pl.BlockSpec((1, N_HEADS, ht, D_V), lambda i: (i // 2, 0, i % 2, 0))),
        out_shape=(
            jax.ShapeDtypeStruct((nb_seq, N_HEADS, tp, 2 * LANES), BF16),
            jax.ShapeDtypeStruct((nb_seq, N_HEADS, tp, 2 * LANES), BF16),
            jax.ShapeDtypeStruct((nb_seq, N_HEADS, tp, D_V), BF16),
        ),
        compiler_params=_params("parallel"),
    )(p, wq, wkv, gq, gkv, *tables)


def _attn_fwd(q, k, v, p, g_attn):
    nb_seq, _, tp, _ = q.shape

    def body(q_ref, k_ref, v_ref, z_ref, g_ref, y_ref, o_ref, lse_ref):
        g = g_ref[...]
        for r0 in range(0, tp, Q_TILE):
            nq = min(Q_TILE, tp - r0)
            kend = r0 + nq
            qq = q_ref[0, 0, r0:kend, :]
            sd = _dot(qq, k_ref[0, 0, r0:kend, :], _NT)
            causal = (lax.broadcasted_iota(jnp.int32, (nq, nq), 1) <= lax.broadcasted_iota(jnp.int32, (nq, nq), 0))
            sd = jnp.where(causal, sd, NEG_INF)
            m = jnp.max(sd, axis=-1, keepdims=True)
            if r0:
                so = _dot(qq, k_ref[0, 0, 0:r0, :], _NT)
                m = jnp.maximum(m, jnp.max(so, axis=-1, keepdims=True))
            ed = jnp.exp2(sd - m)
            l = jnp.sum(ed, axis=-1, keepdims=True)
            o = _dot(ed.astype(BF16), v_ref[0, 0, r0:kend, :])
            if r0:
                eo = jnp.exp2(so - m)
                l = l + jnp.sum(eo, axis=-1, keepdims=True)
                o = o + _dot(eo.astype(BF16), v_ref[0, 0, 0:r0, :])
            o = o * (1.0 / l)
            o_ref[0, 0, r0:kend, :] = o
            lse_ref[0, 0, r0:kend, :] = jnp.broadcast_to(m + jnp.log2(l), (nq, LANES))
            ohat, _ = _rms_stats(o)
            z = z_ref[r0:kend, :].astype(F32)
            y_ref[r0:kend, :] = (ohat * g * (z * _sigmoid(z))).astype(BF16)

    qk = pl.BlockSpec((1, 1, tp, 2 * LANES), lambda b, h: (b, h, 0, 0))
    hv = pl.BlockSpec((1, 1, tp, D_V), lambda b, h: (b, h, 0, 0))
    return pl.pallas_call(
        body,
        name="attn_fwd",
        grid=(nb_seq, N_HEADS),
        in_specs=[qk, qk, hv,
                  pl.BlockSpec((tp, LANES), lambda b, h: (b, GRP_A // LANES + h)),
                  pl.BlockSpec((1, LANES), lambda b, h: (0, h))],
        out_specs=(pl.BlockSpec((tp, LANES), lambda b, h: (b, h)), hv, hv),
        out_shape=(
            jax.ShapeDtypeStruct((nb_seq * tp, N_HEADS * D_V), BF16),
            jax.ShapeDtypeStruct((nb_seq, N_HEADS, tp, D_V), F32),
            jax.ShapeDtypeStruct((nb_seq, N_HEADS, tp, LANES), F32),
        ),
        compiler_params=_params("parallel", "parallel"),
    )(q, k, v, p, g_attn)


_CONV_COL0 = (GRP_A + N_HEADS * D_V) // LANES


def _conv_specs(tp, order):
    cols = CONV_WIDTH // LANES
    return [pl.BlockSpec((tp, LANES), functools.partial(
        lambda a, b, off: order(a, b, off), off=_CONV_COL0 + i * cols)) for i in range(4)]


def _conv_fwd(p, conv_w, g_conv, nb_seq, tp):
    def body(b_ref, c_ref, h_ref, z_ref, w_ref, g_ref, y_ref):
        cc = c_ref[...].astype(F32) * h_ref[...].astype(F32)
        row = lax.broadcasted_iota(jnp.int32, (tp, LANES), 0)
        s1 = jnp.where(row >= 1, pltpu.roll(cc, 1, 0), 0.0)
        s2 = jnp.where(row >= 2, pltpu.roll(cc, 2, 0), 0.0)
        yc = b_ref[...].astype(F32) * (w_ref[0:1, :] * s2 + w_ref[1:2, :] * s1 + w_ref[2:3, :] * cc)
        r = lax.rsqrt(_group_mean(yc * yc) + EPS)
        z = z_ref[...].astype(F32)
        y_ref[...] = (yc * r * g_ref[...] * (z * _sigmoid(z))).astype(BF16)

    return pl.pallas_call(
        body,
        name="conv_fwd",
        grid=(nb_seq, CONV_WIDTH // LANES),
        in_specs=_conv_specs(tp, lambda b, t, off: (b, off + t)) + [
            pl.BlockSpec((8, LANES), lambda b, t: (0, t)),
            pl.BlockSpec((1, LANES), lambda b, t: (0, t))],
        out_specs=pl.BlockSpec((tp, LANES), lambda b, t: (b, t)),
        out_shape=jax.ShapeDtypeStruct((nb_seq * tp, CONV_WIDTH), BF16),
        compiler_params=_params("parallel", "parallel"),
    )(p, p, p, p, conv_w, g_conv)


def _token_copy(hbm, b, k, ts, buf, sem, to_hbm=False):
    lo, hi = max(k * ts - LANES, 0), (k + 1) * ts - LANES
    off = lo - (k * ts - LANES)
    src, dst = hbm.at[b, pl.ds(lo, hi - lo)], buf.at[pl.ds(off, hi - lo)]
    if to_hbm:
        src, dst = dst, src
    return pltpu.make_async_copy(src, dst, sem)


def _for_tile(k, nt, fn):
    for kk in range(nt):
        @pl.when(k == kk)
        def _(kk=kk):
            fn(kk)


def _out_proj_loss(ya, yc, w_out, x, target, g_final, nt):
    nb_seq, s, d = x.shape
    r, ka = ya.shape
    ts = (s + LANES) // nt
    steps = nb_seq * nt

    def body(a_ref, c_ref, w_ref, x_hbm, t_hbm, g_ref, dhb_ref, dg_ref, loss_ref,
             xbuf, tbuf, acc_ref, sems):
        i = pl.program_id(0)
        b, k = i // nt, i % nt

        @pl.when(i == 0)
        def _():
            acc_ref[...] = jnp.zeros_like(acc_ref)
            dg_ref[...] = jnp.zeros_like(dg_ref)

        slot = i % 2

        def fetch(seq, kk, sl):
            return [_token_copy(x_hbm, seq, kk, ts, xbuf.at[sl], sems.at[sl, 0]),
                    _token_copy(t_hbm, seq, kk, ts, tbuf.at[sl], sems.at[sl, 1])]

        def start(seq, sl, kk):
            if kk == 0:
                xbuf[sl, 0:LANES, :] = jnp.zeros((LANES, d), F32)
                tbuf[sl, 0:LANES, :] = jnp.zeros((LANES, d), F32)
            for cp in fetch(seq, kk, sl):
                cp.start()

        @pl.when(i == 0)
        def _():
            start(0, 0, 0)

        @pl.when(i + 1 < steps)
        def _():
            _for_tile((i + 1) % nt, nt, functools.partial(start, (i + 1) // nt, 1 - slot))

        mix = _dot(a_ref[...], w_ref[0:ka, :]) + _dot(c_ref[...], w_ref[ka:, :])
        _for_tile(k, nt, lambda kk: [cp.wait() for cp in fetch(b, kk, slot)])

        real = (lax.broadcasted_iota(jnp.int32, (ts, d), 0) >= LANES) | (k > 0)
        g = g_ref[...]
        hhat, rstd = _rms_stats(xbuf[slot] + mix)
        e = jnp.where(real, hhat * g - tbuf[slot], 0.0)
        acc_ref[...] += jnp.sum(e * e, axis=0, keepdims=True)
        dy = e * (1.0 / d)
        dg_ref[...] += jnp.sum(dy * hhat, axis=0, keepdims=True)
        dhb_ref[...] = _rms_bwd(g * dy, hhat, rstd).astype(BF16)

        @pl.when(i == steps - 1)
        def _():
            total = jnp.sum(acc_ref[...], axis=1, keepdims=True)
            loss_ref[...] = jnp.broadcast_to((0.5 / d) * total, loss_ref.shape)

    hbm = pl.BlockSpec(memory_space=pl.ANY)
    row = pl.BlockSpec((ts, d), lambda i: (i, 0))
    vec = pl.BlockSpec((1, d), lambda i: (0, 0))
    return pl.pallas_call(
        body,
        name="out_proj_loss",
        grid=(steps,),
        in_specs=[pl.BlockSpec((ts, ka), lambda i: (i, 0)), pl.BlockSpec((ts, yc.shape[1]), lambda i: (i, 0)),
                  pl.BlockSpec(w_out.shape, lambda i: (0, 0)), hbm, hbm, vec],
        out_specs=(row, vec, pl.BlockSpec((1, LANES), lambda i: (0, 0))),
        out_shape=(
            jax.ShapeDtypeStruct((r, d), BF16),
            jax.ShapeDtypeStruct((1, d), F32),
            jax.ShapeDtypeStruct((1, LANES), F32),
        ),
        scratch_shapes=[pltpu.VMEM((2, ts, d), F32), pltpu.VMEM((2, ts, d), F32), pltpu.VMEM((1, d), F32),
                        pltpu.SemaphoreType.DMA((2, 2))],
        compiler_params=_params("arbitrary"),
    )(ya, yc, w_out, x, target, g_final)


def _out_proj_bwd(dhb, w_out, ya, yc, bm):
    r, d = dhb.shape
    ka = ya.shape[1]
    n_mix = w_out.shape[0]
    last = r // bm - 1

    def body(dh_ref, w_ref, a_ref, c_ref, dcat_ref, dw_ref, acc_ref):
        @pl.when(pl.program_id(0) == 0)
        def _():
            acc_ref[...] = jnp.zeros_like(acc_ref)

        dh = dh_ref[...]
        dcat_ref[...] = _dot(dh, w_ref[...], _NT).astype(BF16)
        acc_ref[0:ka, :] += _dot(a_ref[...], dh, _TN)
        acc_ref[ka:, :] += _dot(c_ref[...], dh, _TN)

        @pl.when(pl.program_id(0) == last)
        def _():
            dw_ref[...] = acc_ref[...].astype(BF16)

    return pl.pallas_call(
        body,
        name="out_proj_bwd",
        grid=(r // bm,),
        in_specs=[pl.BlockSpec((bm, d), lambda i: (i, 0)), pl.BlockSpec(w_out.shape, lambda i: (0, 0)),
                  pl.BlockSpec((bm, ka), lambda i: (i, 0)), pl.BlockSpec((bm, yc.shape[1]), lambda i: (i, 0))],
        out_specs=(pl.BlockSpec((bm, n_mix), lambda i: (i, 0)),
                   pl.BlockSpec((n_mix, d), lambda i: (0, 0))),
        out_shape=(jax.ShapeDtypeStruct((r, n_mix), BF16),
                   jax.ShapeDtypeStruct((n_mix, d), BF16)),
        scratch_shapes=[pltpu.VMEM((n_mix, d), F32)],
        compiler_params=_params("arbitrary"),
    )(dhb, w_out, ya, yc)


def _attn_bwd(q, k, v, o, lse, dcat, p, g_attn):
    nb_seq, _, tp, _ = q.shape

    def body(q_ref, k_ref, v_ref, o_ref, lse_ref, dy_ref, z_ref, g_ref,
             dq_ref, dk_ref, dv_ref, dz_ref, dg_ref, dq_acc):
        @pl.when(pl.program_id(1) == 0)
        def _():
            dg_ref[...] = jnp.zeros_like(dg_ref)

        g = g_ref[...]
        z = z_ref[...].astype(F32)
        o = o_ref[0, 0]
        dy = dy_ref[...].astype(F32)
        sig = _sigmoid(z)
        ohat, r = _rms_stats(o)
        don = dy * (z * sig)
        dz_ref[...] = (dy * (ohat * g) * (sig * (1.0 + z * (1.0 - sig)))).astype(BF16)
        dg_ref[...] += jnp.sum(don * ohat, axis=0, keepdims=True)
        do = _rms_bwd(g * don, ohat, r)
        dvec = jnp.sum(do * o, axis=-1, keepdims=True)
        dob = do.astype(BF16)
        lse_col = lse_ref[0, 0, :, 0:1]
        dq_acc[...] = jnp.zeros_like(dq_acc)
        for k0 in range(0, tp, K_TILE):
            nk = min(K_TILE, tp - k0)
            nq = tp - k0
            qq = q_ref[0, 0, k0:, :]
            kk = k_ref[0, 0, k0:k0 + nk, :]
            causal = (lax.broadcasted_iota(jnp.int32, (nq, nk), 1) <= lax.broadcasted_iota(jnp.int32, (nq, nk), 0))
            pr = jnp.where(causal, jnp.exp2(_dot(qq, kk, _NT) - lse_col[k0:]), 0.0)
            dp = _dot(dob[k0:], v_ref[0, 0, k0:k0 + nk, :], _NT)
            ds = (pr * (dp - dvec[k0:])).astype(BF16)
            dv_ref[0, 0, k0:k0 + nk, :] = _dot(pr.astype(BF16), dob[k0:], _TN).astype(BF16)
            dk_ref[0, 0, k0:k0 + nk, :] = (_dot(ds, qq, _TN) * (ATTN_SCALE / Q_SCALE)).astype(BF16)
            dq_acc[k0:, :] += _dot(ds, kk)
        dq_ref[0, 0] = (dq_acc[...] * ATTN_SCALE).astype(BF16)

    qk = pl.BlockSpec((1, 1, tp, 2 * LANES), lambda h, b: (b, h, 0, 0))
    hv = pl.BlockSpec((1, 1, tp, D_V), lambda h, b: (b, h, 0, 0))
    col = pl.BlockSpec((tp, LANES), lambda h, b: (b, h))
    return pl.pallas_call(
        body,
        name="attn_bwd",
        grid=(N_HEADS, nb_seq),
        in_specs=[qk, qk, hv, hv, hv, col,
                  pl.BlockSpec((tp, LANES), lambda h, b: (b, GRP_A // LANES + h)),
                  pl.BlockSpec((1, LANES), lambda h, b: (0, h))],
        out_specs=(qk, qk, hv, col, pl.BlockSpec((1, LANES), lambda h, b: (0, h))),
        out_shape=(
            jax.ShapeDtypeStruct((nb_seq, N_HEADS, tp, 2 * LANES), BF16),
            jax.ShapeDtypeStruct((nb_seq, N_HEADS, tp, 2 * LANES), BF16),
            jax.ShapeDtypeStruct((nb_seq, N_HEADS, tp, D_V), BF16),
            jax.ShapeDtypeStruct((nb_seq * tp, N_HEADS * D_V), BF16),
            jax.ShapeDtypeStruct((1, N_HEADS * D_V), F32),
        ),
        scratch_shapes=[pltpu.VMEM((tp, 2 * LANES), F32)],
        compiler_params=_params("arbitrary", "arbitrary"),
    )(q, k, v, o, lse, dcat, p, g_attn)


def _qkv_bwd(p, dq, dk, dv, wq, wkv, gq, gkv, tables):
    nb_seq, _, tp, _ = dq.shape
    ht = tp // 2

    def body(pa_ref, dq_ref, dk_ref, dv_ref, wq_ref, wkv_ref, gq_ref, gkv_ref, cos_ref, sa_ref, sb_ref,
             dpa_ref, dwq_ref, dwkv_ref, dgq_ref, dgkv_ref):
        @pl.when(pl.program_id(0) == 0)
        def _():
            dwq_ref[...] = jnp.zeros_like(dwq_ref)
            dwkv_ref[...] = jnp.zeros_like(dwkv_ref)
            dgq_ref[...] = jnp.zeros_like(dgq_ref)
            dgkv_ref[...] = jnp.zeros_like(dgkv_ref)

        pa = pa_ref[...].astype(F32)
        gq, gkv = gq_ref[...], gkv_ref[...]
        cq_hat, rq = _rms_stats(pa[:, :Q_RANK])
        ckv_hat, rkv = _rms_stats(pa[:, Q_RANK:Q_RANK + KV_RANK])
        tabs = (cos_ref[...], sa_ref[...], sb_ref[...])

        pe = [dq_ref[0, h, :, D_NOPE:].astype(F32) for h in range(N_HEADS)]
        pairs = [_rope_t(pe[2 * i] + pltpu.roll(pe[2 * i + 1], D_ROPE, 1), *tabs).astype(BF16) for i in range(2)]
        dq_flat = jnp.concatenate([dq_ref[0, h, :, :D_NOPE] for h in range(N_HEADS)] + pairs, axis=1)
        dwq_ref[...] += _dot(dq_flat, (cq_hat * gq).astype(BF16), _TN)
        dcqn = _dot(dq_flat, wq_ref[...])
        dgq_ref[...] += jnp.sum(dcqn * cq_hat, axis=0, keepdims=True)
        dcq = _rms_bwd(gq * dcqn, cq_hat, rq)

        dkv_flat = jnp.concatenate([dk_ref[0, h, :, :D_NOPE] for h in range(N_HEADS)]
                                   + [dv_ref[0, h] for h in range(N_HEADS)], axis=1)
        dwkv_ref[...] += _dot((ckv_hat * gkv).astype(BF16), dkv_flat, _TN)
        dckvn = _dot(dkv_flat, wkv_ref[...], _NT)
        dgkv_ref[...] += jnp.sum(dckvn * ckv_hat, axis=0, keepdims=True)
        dckv = _rms_bwd(gkv * dckvn, ckv_hat, rkv)

        dk_pe = dk_ref[0, 0, :, D_NOPE:].astype(F32)
        for h in range(1, N_HEADS):
            dk_pe = dk_pe + dk_ref[0, h, :, D_NOPE:].astype(F32)
        dk_pe = jnp.where(lax.broadcasted_iota(jnp.int32, (ht, LANES), 1) < D_ROPE, dk_pe, 0.0)
        dpa_ref[...] = jnp.concatenate([dcq, dckv, _rope_t(dk_pe, *tabs)], axis=1).astype(BF16)

    full = lambda a: pl.BlockSpec(a.shape, lambda i: (0,) * a.ndim)
    tab = pl.BlockSpec((ht, LANES), lambda i: (i % 2, 0))
    qk = pl.BlockSpec((1, N_HEADS, ht, 2 * LANES), lambda i: (i // 2, 0, i % 2, 0))
    acc = lambda shape: pl.BlockSpec(shape, lambda i: (0, 0))
    return pl.pallas_call(
        body,
        name="qkv_bwd",
        grid=(2 * nb_seq,),
        in_specs=[pl.BlockSpec((ht, GRP_A), lambda i: (i, 0)), qk, qk,
                  pl.BlockSpec((1, N_HEADS, ht, D_V), lambda i: (i // 2, 0, i % 2, 0)),
                  full(wq), full(wkv), full(gq), full(gkv), tab, tab, tab],
        out_specs=(pl.BlockSpec((ht, GRP_A), lambda i: (i, 0)),
                   acc(wq.shape), acc(wkv.shape), acc((1, Q_RANK)), acc((1, KV_RANK))),
        out_shape=(
            jax.ShapeDtypeStruct((nb_seq * tp, GRP_A), BF16),
            jax.ShapeDtypeStruct(wq.shape, F32),
            jax.ShapeDtypeStruct(wkv.shape, F32),
            jax.ShapeDtypeStruct((1, Q_RANK), F32),
            jax.ShapeDtypeStruct((1, KV_RANK), F32),
        ),
        compiler_params=_params("arbitrary"),
    )(p, dq, dk, dv, wq, wkv, gq, gkv, *tables)


def _conv_bwd(p, dcat, conv_w, g_conv, nb_seq, tp):
    cols = CONV_WIDTH // LANES

    def body(b_ref, c_ref, h_ref, z_ref, dy_ref, w_ref, g_ref,
             db_ref, dc_ref, dh_ref, dz_ref, dw_ref, dg_ref):
        @pl.when(pl.program_id(1) == 0)
        def _():
            dw_ref[...] = jnp.zeros_like(dw_ref)
            dg_ref[...] = jnp.zeros_like(dg_ref)

        cb, c, h = b_ref[...].astype(F32), c_ref[...].astype(F32), h_ref[...].astype(F32)
        z, dy = z_ref[...].astype(F32), dy_ref[...].astype(F32)
        g = g_ref[...]
        w0, w1, w2 = w_ref[0:1, :], w_ref[1:2, :], w_ref[2:3, :]
        cc = c * h
        row = lax.broadcasted_iota(jnp.int32, (tp, LANES), 0)
        s1 = jnp.where(row >= 1, pltpu.roll(cc, 1, 0), 0.0)
        s2 = jnp.where(row >= 2, pltpu.roll(cc, 2, 0), 0.0)
        dwc = w0 * s2 + w1 * s1 + w2 * cc
        yc = cb * dwc
        r = lax.rsqrt(_group_mean(yc * yc) + EPS)
        ychat = yc * r
        sig = _sigmoid(z)
        dz_ref[...] = (dy * (ychat * g) * (sig * (1.0 + z * (1.0 - sig)))).astype(BF16)
        dyn = dy * (z * sig)
        dg_ref[...] += jnp.sum(dyn * ychat, axis=0, keepdims=True)
        gd = g * dyn
        dyc = r * (gd - ychat * _group_mean(gd * ychat))
        db_ref[...] = (dyc * dwc).astype(BF16)
        ddw = dyc * cb
        dw_ref[0:1, :] += jnp.sum(ddw * s2, axis=0, keepdims=True)
        dw_ref[1:2, :] += jnp.sum(ddw * s1, axis=0, keepdims=True)
        dw_ref[2:3, :] += jnp.sum(ddw * cc, axis=0, keepdims=True)
        u1 = jnp.where(row <= tp - 2, pltpu.roll(ddw, tp - 1, 0), 0.0)
        u2 = jnp.where(row <= tp - 3, pltpu.roll(ddw, tp - 2, 0), 0.0)
        dcc = w2 * ddw + w1 * u1 + w0 * u2
        dc_ref[...] = (dcc * h).astype(BF16)
        dh_ref[...] = (dcc * c).astype(BF16)

    col = pl.BlockSpec((tp, LANES), lambda t, b: (b, t))
    out = jax.ShapeDtypeStruct((nb_seq * tp, CONV_WIDTH), BF16)
    return pl.pallas_call(
        body,
        name="conv_bwd",
        grid=(cols, nb_seq),
        in_specs=_conv_specs(tp, lambda t, b, off: (b, off + t)) + [
            pl.BlockSpec((tp, LANES), lambda t, b: (b, N_HEADS * D_V // LANES + t)),
            pl.BlockSpec((8, LANES), lambda t, b: (0, t)),
            pl.BlockSpec((1, LANES), lambda t, b: (0, t))],
        out_specs=(col, col, col, col,
                   pl.BlockSpec((8, LANES), lambda t, b: (0, t)), pl.BlockSpec((1, LANES), lambda t, b: (0, t))),
        out_shape=(out, out, out, out,
                   jax.ShapeDtypeStruct((8, CONV_WIDTH), F32), jax.ShapeDtypeStruct((1, CONV_WIDTH), F32)),
        compiler_params=_params("arbitrary", "arbitrary"),
    )(p, p, p, p, dcat, conv_w, g_conv)


def _input_bwd(dps, w_in, x, meta, dh, norm_g, nt, send_in):
    nb_seq, s, d = x.shape
    r, kb = dps[0].shape
    ts = (s + LANES) // nt
    steps = nb_seq * nt
    n_dp = len(dps)
    in_slot = send_in.shape[1:]

    def body(*refs):
        dp_refs, w_ref, x_hbm, meta_ref, dh_ref, g_ref, pay_ref = refs[:n_dp], *refs[n_dp:n_dp + 6]
        o = n_dp + 6
        gx_hbm, dmeta_ref, dg_ref, r2_in, gmeta_ref, gnorm_ref = refs[o:o + 6]
        xbuf, gxbuf, tok_sems, own_in, r1_in, sum_in = refs[o + 6:o + 12]
        sems = refs[o + 12:-4]
        stail, rtail, tail_send, tail_recv = refs[-4:]
        i = pl.program_id(0)
        b, k = i // nt, i % nt

        def plan():
            return _reduce_plan((pay_ref,), (own_in,), (r1_in,), (sum_in,), (r2_in,), *sems)

        @pl.when(i == 0)
        def _():
            dmeta_ref[...] = jnp.zeros_like(dmeta_ref)
            dg_ref[...] = jnp.zeros_like(dg_ref)
            plan()[0]()

        @pl.when(i == 1)
        def _():
            plan()[1]()

        def start(kk):
            if kk == 0:
                xbuf[0:PAD_FRONT, :] = jnp.zeros((PAD_FRONT, d), F32)
                xbuf[PAD_FRONT:LANES, :] = meta_ref[...]
            _token_copy(x_hbm, b, kk, ts, xbuf, tok_sems.at[0]).start()

        _for_tile(k, nt, start)
        du = _dot(dp_refs[0][...], w_ref[0:kb, :])
        for j in range(1, n_dp):
            du = du + _dot(dp_refs[j][...], w_ref[kb * j:kb * (j + 1), :])
        _for_tile(k, nt, lambda kk: _token_copy(x_hbm, b, kk, ts, xbuf, tok_sems.at[0]).wait())

        g = g_ref[...]
        hhat, rstd = _rms_stats(xbuf[...])
        dg_ref[...] += jnp.sum(du * hhat, axis=0, keepdims=True)
        res = _rms_bwd(g * du, hhat, rstd) + dh_ref[...].astype(F32)

        @pl.when(i > 0)
        def _():
            _for_tile(k, nt, lambda kk: _token_copy(gx_hbm, b, (kk - 1) % nt, ts, gxbuf, tok_sems.at[1], True).wait())

        gxbuf[...] = res

        @pl.when(k == 0)
        def _():
            dmeta_ref[...] += gxbuf[PAD_FRONT:LANES, :]

        _for_tile(k, nt, lambda kk: _token_copy(gx_hbm, b, kk, ts, gxbuf, tok_sems.at[1], True).start())

        @pl.when(i == steps - 1)
        def _():
            _token_copy(gx_hbm, b, nt - 1, ts, gxbuf, tok_sems.at[1], True).wait()
            px, py, pc, me = _device_position()
            for dev in range(N_DEV):
                stail[dev, 0:N_META, :] = dmeta_ref[:, SHARD_META * dev:SHARD_META * (dev + 1)]
                for j in range(d // LANES):
                    stail[dev, N_META + j:N_META + j + 1, :] = dg_ref[:, LANES * j:LANES * (j + 1)]
            copies = []
            for r in range(1, N_DEV):
                peer = (1 - px if r & 4 else px, 1 - py if r & 2 else py, 1 - pc if r & 1 else pc)
                copies.append(pltpu.make_async_remote_copy(
                    src_ref=stail.at[4 * peer[0] + 2 * peer[1] + peer[2]],
                    dst_ref=rtail.at[r],
                    send_sem=tail_send.at[r - 1],
                    recv_sem=tail_recv.at[r - 1],
                    device_id=peer,
                    device_id_type=pl.DeviceIdType.MESH,
                ))
            for cp in copies:
                cp.start()
            rtail[0] = stail[me]
            plan()[2]()
            for cp in copies:
                cp.wait_recv()
            gt = rtail[me]
            for dev in range(1, N_DEV):
                gt = gt + rtail[dev ^ me]
            gmeta_ref[...] = gt[0:N_META, :]
            for j in range(d // LANES):
                gnorm_ref[:, LANES * j:LANES * (j + 1)] = gt[N_META + j:N_META + j + 1, :]
            for cp in copies:
                cp.wait_send()

    whole = lambda a: pl.BlockSpec(a.shape, lambda i: (0,) * a.ndim)
    hbm = pl.BlockSpec(memory_space=pl.ANY)
    return pl.pallas_call(
        body,
        name="input_bwd",
        grid=(steps,),
        in_specs=[pl.BlockSpec((ts, kb), lambda i: (i, 0)) for _ in dps]
        + [whole(w_in), hbm, whole(meta), pl.BlockSpec((ts, d), lambda i: (i, 0)), whole(norm_g), hbm],
        out_specs=(hbm, pl.BlockSpec((N_META, d), lambda i: (0, 0)), pl.BlockSpec((1, d), lambda i: (0, 0)), hbm,
                   pl.BlockSpec((N_META, SHARD_META), lambda i: (0, 0)), pl.BlockSpec((1, d), lambda i: (0, 0))),
        out_shape=(jax.ShapeDtypeStruct((nb_seq, s, d), F32),
                   jax.ShapeDtypeStruct((N_META, d), F32),
                   jax.ShapeDtypeStruct((1, d), F32),
                   jax.ShapeDtypeStruct((N_CHIPS,) + in_slot, BF16),
                   jax.ShapeDtypeStruct((N_META, SHARD_META), F32),
                   jax.ShapeDtypeStruct((1, d), F32)),
        scratch_shapes=[pltpu.VMEM((ts, d), F32), pltpu.VMEM((ts, d), F32), pltpu.SemaphoreType.DMA((2,))]
        + _reduce_scratch([(in_slot, BF16)], [True])
        + [pltpu.VMEM((N_DEV, TAIL_ROWS, LANES), F32), pltpu.VMEM((N_DEV, TAIL_ROWS, LANES), F32),
           pltpu.SemaphoreType.DMA((N_DEV - 1,)), pltpu.SemaphoreType.DMA((N_DEV - 1,))],
        compiler_params=_params("arbitrary"),
    )(*dps, w_in, x, meta, dh, norm_g, send_in)


def _in_proj_bwd_w(u, dps, bm, small_grads, send_out):
    r, d = u.shape
    kb = dps[0].shape[1]
    steps = r // bm
    n_dp, n_small = len(dps), len(small_grads)
    out_slot, small_slot = send_out.shape[1:], (SMALL_ROWS, LANES)

    def body(*refs):
        u_ref, dp_refs = refs[0], refs[1:1 + n_dp]
        small_refs = refs[1 + n_dp:1 + n_dp + n_small]
        o = 1 + n_dp + n_small
        pay_out, o_ref, r2_out, r2_small = refs[o:o + 4]
        acc_ref, ssmall, r1_out, sum_out, r1_small, sum_small = refs[o + 4:o + 10]
        sems = refs[o + 10:]
        i = pl.program_id(0)

        def plan():
            return _reduce_plan((pay_out, ssmall), (None, None), (r1_out, r1_small), (sum_out, sum_small),
                                (r2_out, r2_small), *sems)

        @pl.when(i == 0)
        def _():
            acc_ref[...] = jnp.zeros_like(acc_ref)
            _pack_small(ssmall, *small_refs)
            plan()[0]()

        @pl.when(i == 1)
        def _():
            plan()[1]()

        uu = u_ref[...]
        for j in range(n_dp):
            acc_ref[kb * j:kb * (j + 1), :] += _dot(dp_refs[j][...], uu, _TN)

        @pl.when(i == steps - 1)
        def _():
            for k in range(N_DEV):
                for s, e, c0 in _in_pieces(k):
                    o_ref[k, s:e, :] = acc_ref[c0:c0 + e - s, :].astype(BF16)
                o_ref[k, SHARD_IN:, :] = jnp.zeros((SHARD_IN_PAD - SHARD_IN, d), BF16)
            plan()[2]()

    whole = lambda a: pl.BlockSpec(a.shape, lambda i: (0,) * a.ndim)
    hbm = pl.BlockSpec(memory_space=pl.ANY)
    return pl.pallas_call(
        body,
        name="in_proj_bwd_w",
        grid=(steps,),
        in_specs=[pl.BlockSpec((bm, d), lambda i: (i, 0))]
        + [pl.BlockSpec((bm, kb), lambda i: (i, 0)) for _ in dps] + [whole(a) for a in small_grads]
        + [whole(send_out)],
        out_specs=(pl.BlockSpec((N_DEV, SHARD_IN_PAD, d), lambda i: (0, 0, 0)), hbm, hbm),
        out_shape=(jax.ShapeDtypeStruct((N_DEV, SHARD_IN_PAD, d), BF16),
                   jax.ShapeDtypeStruct((N_CHIPS,) + out_slot, BF16),
                   jax.ShapeDtypeStruct((N_CHIPS,) + small_slot, F32)),
        scratch_shapes=[pltpu.VMEM((kb * n_dp, d), F32), pltpu.VMEM((N_DEV,) + small_slot, F32)]
        + _reduce_scratch([(out_slot, BF16), (small_slot, F32)], [False, False]),
        compiler_params=_params("arbitrary"),
    )(u, *dps, *small_grads, send_out)


def _local_step(x, loss_target, u, p, meta_f, norm_g, w_in_p, q_norm_g, w_q_p, kv_norm_g, w_kv_p, conv_w_f,
                attn_out_g, conv_out_g, w_out_f, g_final):
    nb_seq, s, d = x.shape
    tp = s + LANES
    ht = tp // 2
    tables = _rope_tables(tp)

    q, k, v = _qkv_fwd(p, w_q_p, w_kv_p, q_norm_g, kv_norm_g, tables, nb_seq, tp)
    ya, o, lse = _attn_fwd(q, k, v, p, attn_out_g)
    yc = _conv_fwd(p, conv_w_f, conv_out_g, nb_seq, tp)
    dhb, d_final_g, loss_part = _out_proj_loss(ya, yc, w_out_f, x, loss_target, g_final, TOKEN_TILES)

    dcat, d_w_out = _out_proj_bwd(dhb, w_out_f, ya, yc, ht)
    send_out = d_w_out.reshape(N_DEV, SHARD_OUT, d)
    dq, dk, dv, dz_attn, d_attn_g = _attn_bwd(q, k, v, o, lse, dcat, p, attn_out_g)
    dpa, d_wq_p, d_wkv_p, d_gq, d_gkv = _qkv_bwd(p, dq, dk, dv, w_q_p, w_kv_p, q_norm_g, kv_norm_g, tables)
    d_b, d_c, d_h, dz_conv, d_conv_w, d_conv_g = _conv_bwd(p, dcat, conv_w_f, conv_out_g, nb_seq, tp)
    dps = (dpa, dz_attn, d_b, d_c, d_h, dz_conv)
    small = (d_wq_p, d_wkv_p, d_conv_w, d_final_g, d_gq, d_gkv, d_attn_g, d_conv_g, loss_part)
    send_in, r_out, r_small = _in_proj_bwd_w(u, dps, ht, small, send_out)
    grad_x, _, _, r_in, g_meta, g_norm = _input_bwd(dps, w_in_p, x, meta_f, dhb, norm_g, TOKEN_TILES, send_in)
    return grad_x, r_in, r_out, r_small, g_meta, g_norm


def kernel(x, meta_tokens, norm_g, w_in, q_norm_g, w_q_up, kv_norm_g, w_kv_up, conv_w, attn_out_g, conv_out_g, w_out, final_norm_g, loss_target, m_meta_tokens, m_norm_g, m_w_in, m_q_norm_g, m_w_q_up, m_kv_norm_g, m_w_kv_up, m_conv_w, m_attn_out_g, m_conv_out_g, m_w_out, m_final_norm_g, v_meta_tokens, v_norm_g, v_w_in, v_q_norm_g, v_w_q_up, v_kv_norm_g, v_w_kv_up, v_conv_w, v_attn_out_g, v_conv_out_g, v_w_out, v_final_norm_g):
    d = x.shape[-1]
    order = jnp.asarray(_tile_orders()[0].reshape(-1))
    u, p, w_in_p, meta_f, w_q_p, w_kv_p, w_out_f, conv_w_f = _prep_in_proj(
        x, meta_tokens, norm_g, w_in[0].T, w_q_up[0].T, w_kv_up[0], w_out[0], conv_w.transpose(1, 0, 2), order)
    g_final = final_norm_g.reshape(1, d)
    grad_x, r_in, r_out, r_small, g_meta, g_norm = _local_step(
        x, loss_target, u, p, meta_f, norm_g, w_in_p, q_norm_g, w_q_p, kv_norm_g, w_kv_p, conv_w_f,
        attn_out_g, conv_out_g, w_out_f, g_final)

    flat = lambda a: a.reshape(a.shape[-2:]) if a.ndim == 3 else a.reshape(1, -1) if a.ndim == 1 else a
    transposed = ("w_in", "w_q_up")

    def to_kernel(n, a):
        if n == "conv_w":
            return a.transpose(1, 0, 2)
        return flat(a).T if n in transposed else flat(a)

    def from_kernel(n, a, shape):
        if n == "conv_w":
            return a.transpose(1, 0, 2)
        return (a.T if n in transposed else a).reshape(shape)
    params = {
        "meta_tokens": (meta_tokens, m_meta_tokens, v_meta_tokens),
        "norm_g": (norm_g, m_norm_g, v_norm_g),
        "w_in": (w_in, m_w_in, v_w_in),
        "q_norm_g": (q_norm_g, m_q_norm_g, v_q_norm_g),
        "w_q_up": (w_q_up, m_w_q_up, v_w_q_up),
        "kv_norm_g": (kv_norm_g, m_kv_norm_g, v_kv_norm_g),
        "w_kv_up": (w_kv_up, m_w_kv_up, v_w_kv_up),
        "conv_w": (conv_w, m_conv_w, v_conv_w),
        "attn_out_g": (attn_out_g, m_attn_out_g, v_attn_out_g),
        "conv_out_g": (conv_out_g, m_conv_out_g, v_conv_out_g),
        "w_out": (w_out, m_w_out, v_w_out),
        "final_norm_g": (final_norm_g, m_final_norm_g, v_final_norm_g),
    }
    updated, loss = _reduce_update(r_in, r_out, r_small, g_meta, g_norm,
                                   {n: tuple(to_kernel(n, a) for a in t) for n, t in params.items()})
    outs = [[from_kernel(n, updated[n][i], params[n][0].shape) for n, _ in PARAM_SHAPES] for i in range(4)]
    return (loss[0, 0], grad_x, *outs[0], *outs[1], *outs[2], *outs[3])
```

```python
import functools

import jax
import jax.numpy as jnp
import numpy as np
from jax import lax
from jax.experimental import pallas as pl
from jax.experimental.pallas import tpu as pltpu

F32 = jnp.float32
BF16 = jnp.bfloat16

N_META = 16
D_MODEL = 1024
N_HEADS = 4
D_NOPE = 128
D_ROPE = 64
D_V = 128
Q_RANK = 256
KV_RANK = 128
CONV_WIDTH = 512
CONV_GROUP = 64
ROPE_THETA = 10000.0
ATTN_SCALE = (D_NOPE + D_ROPE) ** -0.5
Q_SCALE = ATTN_SCALE * 1.4426950408889634
EPS = 1e-6
NEG_INF = -1e30

ADAM_LR = 0.001
ADAM_B1 = 0.9
ADAM_B2 = 0.999
ADAM_EPS = 1e-08
ADAM_WD = 0.01
ADAM_STEP = 10

LANES = 128
PAD_FRONT = LANES - N_META
K_TILE = 256
Q_TILE = 512
N_DEV = 8
VMEM_LIMIT = 56 * 1024 * 1024

IN_PAD = 3072
GRP_A = 512
N_A = Q_RANK + KV_RANK + D_ROPE
IN_PROJ = 3008
SHARD_IN = IN_PROJ // N_DEV
SHARD_IN_PAD = 384
SHARD_Q = 96
SHARD_KV = 128
SHARD_OUT = 128
SHARD_CONV = 64
SHARD_META = 128
Q_COLS = N_HEADS * (D_NOPE + D_ROPE)
KV_COLS = N_HEADS * (D_NOPE + D_V)

ROW_Q, ROW_KV, ROW_META, ROW_CONV = 0, 256, 384, 400
ROW_REPL = 408
ROW_NORM, ROW_FINAL, ROW_GQ, ROW_GKV, ROW_ATTN, ROW_CONVG, ROW_LOSS = 408, 416, 424, 426, 427, 431, 435
SMALL_ROWS = 440

PARAM_SHAPES = (
    ("meta_tokens", (N_META, SHARD_META)), ("norm_g", (1, D_MODEL)), ("w_in", (SHARD_IN, D_MODEL)),
    ("q_norm_g", (1, Q_RANK)), ("w_q_up", (SHARD_Q, Q_RANK)), ("kv_norm_g", (1, KV_RANK)),
    ("w_kv_up", (KV_RANK, SHARD_KV)), ("conv_w", (3, 1, SHARD_CONV)), ("attn_out_g", (1, CONV_WIDTH)),
    ("conv_out_g", (1, CONV_WIDTH)), ("w_out", (SHARD_OUT, D_MODEL)), ("final_norm_g", (1, D_MODEL)),
)


def _in_pieces(k):
    lo, hi = SHARD_IN * k, SHARD_IN * (k + 1)
    out = []
    if lo < N_A:
        out.append((0, min(hi, N_A) - lo, lo))
    if hi > N_A:
        s = max(lo, N_A)
        out.append((s - lo, hi - lo, s + GRP_A - N_A))
    return out


P_TILE = 256


def _tile_pieces(j):
    lo, hi = P_TILE * j, P_TILE * (j + 1)
    out = []
    for k in range(N_DEV):
        for s, e, d in _in_pieces(k):
            a, b = max(d, lo), min(d + e - s, hi)
            if a < b:
                out.append((k, s + a - d, s + b - d, a - lo))
    return out


def _tile_orders():
    n_tiles = IN_PAD // P_TILE
    sources = [{k for k, _, _, _ in _tile_pieces(j)} for j in range(n_tiles)]
    rows, n_early, n_free = [], n_tiles, n_tiles
    for chip in range(N_CHIPS):
        own = {2 * chip, 2 * chip + 1}
        diagonal = {2 * (N_CHIPS - 1 - chip), 2 * (N_CHIPS - 1 - chip) + 1}
        early = [j for j in range(n_tiles) if sources[j] <= own]
        late = [j for j in range(n_tiles) if sources[j] & diagonal]
        mid = [j for j in range(n_tiles) if j not in early and j not in late]
        rows.append(early + mid + late)
        n_early, n_free = min(n_early, len(early)), min(n_free, len(early) + len(mid))
    return np.asarray(rows, np.int32), n_early, n_free


def _q_pieces(k):
    lo, hi = SHARD_Q * k, SHARD_Q * (k + 1)
    out = []
    for h in range(N_HEADS):
        base = (D_NOPE + D_ROPE) * h
        s, e = max(lo, base), min(hi, base + D_NOPE)
        if s < e:
            out.append((s - lo, e - lo, D_NOPE * h + s - base))
        s, e = max(lo, base + D_NOPE), min(hi, base + D_NOPE + D_ROPE)
        if s < e:
            out.append((s - lo, e - lo, N_HEADS * D_NOPE + D_ROPE * h + s - base - D_NOPE))
    return out


def _kv_dst(k):
    return D_NOPE * (k // 2) + (N_HEADS * D_NOPE if k % 2 else 0)


def _params(*sem):
    return pltpu.CompilerParams(dimension_semantics=sem, vmem_limit_bytes=VMEM_LIMIT)


def _rms_stats(x):
    r = lax.rsqrt(jnp.mean(x * x, axis=-1, keepdims=True) + EPS)
    return x * r, r


def _rms_bwd(gdy, xhat, r):
    return r * (gdy - xhat * jnp.mean(gdy * xhat, axis=-1, keepdims=True))


def _sigmoid(z):
    return 1.0 / (1.0 + jnp.exp(-z))


def _group_mean(x):
    i0 = lax.broadcasted_iota(jnp.int32, (LANES, LANES), 0) // CONV_GROUP
    i1 = lax.broadcasted_iota(jnp.int32, (LANES, LANES), 1) // CONV_GROUP
    m = jnp.where(i0 == i1, 1.0 / CONV_GROUP, 0.0).astype(BF16)
    hi = x.astype(BF16)
    lo = (x - hi.astype(F32)).astype(BF16)
    return jnp.dot(hi, m, preferred_element_type=F32) + jnp.dot(lo, m, preferred_element_type=F32)


_NT = (((1,), (1,)), ((), ()))
_TN = (((0,), (0,)), ((), ()))


def _dot(a, b, dims=None):
    if dims is None:
        return jnp.dot(a, b, preferred_element_type=F32)
    return lax.dot_general(a, b, dims, preferred_element_type=F32)


def _device_position():
    x, y, c = lax.axis_index("x"), lax.axis_index("y"), lax.axis_index("c")
    return x, y, c, 4 * x + 2 * y + c


def _gather_plan(srcs, slots, send_sems, recv_sems, local_sems):
    x, y, c, _ = _device_position()
    me, sibling = (x, y, c), (x, y, 1 - c)
    flip = lambda v, on: v + on - 2 * v * on
    near = (flip(x, 1 - c), flip(y, c))
    far = (flip(x, c), flip(y, 1 - c))
    diag = (1 - x, 1 - y)
    n = len(srcs)

    def slot(a, px, py, pc):
        return slots[a].at[4 * px + 2 * py + pc]

    def copy(a, k, block, to, own=False):
        return pltpu.make_async_remote_copy(
            src_ref=srcs[a] if own else slot(a, *block),
            dst_ref=slot(a, *block),
            send_sem=send_sems.at[7 * a + k],
            recv_sem=recv_sems.at[7 * a + k],
            device_id=to,
            device_id_type=pl.DeviceIdType.MESH,
        )

    def local(a):
        return pltpu.make_async_copy(srcs[a], slot(a, *me), local_sems.at[a])

    sent = [(me, sibling), (me, (*near, c)), (me, (*far, c)), ((*near, c), (*far, c)),
            ((*near, c), sibling), ((*far, c), sibling), ((*diag, c), sibling)]
    landed = [sibling, (*near, c), (*far, c), (*diag, c), (*far, 1 - c), (*near, 1 - c), (*diag, 1 - c)]

    def send(a, k):
        return copy(a, k, *sent[k], own=k < 3)

    def arrival(a, k):
        return copy(a, k, landed[k], me)

    def start():
        for a in range(n):
            local(a).start()
            for k in range(3):
                send(a, k).start()

    def own():
        for a in range(n):
            local(a).wait()
            arrival(a, 0).wait_recv()

    def mid():
        for a in range(n):
            arrival(a, 1).wait_recv()
            send(a, 3).start()
            send(a, 4).start()
        for a in range(n):
            arrival(a, 2).wait_recv()
            send(a, 5).start()
        for a in range(n):
            for k in (4, 5):
                arrival(a, k).wait_recv()

    def late():
        for a in range(n):
            arrival(a, 3).wait_recv()
            send(a, 6).start()
        for a in range(n):
            arrival(a, 6).wait_recv()

    def finish():
        for a in range(n):
            for k in range(7):
                send(a, k).wait_send()

    return start, own, mid, late, finish


def _adam_update(g, w, m, v):
    m_new = ADAM_B1 * m + (1.0 - ADAM_B1) * g
    v_new = ADAM_B2 * v + (1.0 - ADAM_B2) * (g * g)
    m_hat = m_new / (1.0 - ADAM_B1 ** ADAM_STEP)
    v_hat = v_new / (1.0 - ADAM_B2 ** ADAM_STEP)
    return -ADAM_LR * (m_hat / (jnp.sqrt(v_hat) + ADAM_EPS) + ADAM_WD * w), m_new, v_new


N_CHIPS = 4


def _reduce_plan(pays, owns, r1s, sums, r2s, send1, recv1, send2, recv2, local_sems):
    x, y, c, _ = _device_position()
    sibling = (x, y, 1 - c)
    chips = [((1 - x if rj & 2 else x), (1 - y if rj & 1 else y)) for rj in range(N_CHIPS)]
    n = len(pays)

    def slot_of(rj, core):
        return 4 * chips[rj][0] + 2 * chips[rj][1] + core

    def to_sibling(a, rj):
        return pltpu.make_async_remote_copy(
            src_ref=pays[a].at[slot_of(rj, 1 - c)], dst_ref=r1s[a].at[rj],
            send_sem=send1.at[N_CHIPS * a + rj], recv_sem=recv1.at[N_CHIPS * a + rj],
            device_id=sibling, device_id_type=pl.DeviceIdType.MESH)

    def load_own(a, rj):
        return pltpu.make_async_copy(pays[a].at[slot_of(rj, c)], owns[a].at[rj], local_sems.at[2 * N_CHIPS * a + rj])

    def to_chip(a, rj):
        return pltpu.make_async_remote_copy(
            src_ref=sums[a].at[rj], dst_ref=r2s[a].at[rj],
            send_sem=send2.at[N_CHIPS * a + rj], recv_sem=recv2.at[N_CHIPS * a + rj],
            device_id=(*chips[rj], c), device_id_type=pl.DeviceIdType.MESH)

    def keep(a):
        return pltpu.make_async_copy(sums[a].at[0], r2s[a].at[0], local_sems.at[2 * N_CHIPS * a + N_CHIPS])

    def start():
        for a in range(n):
            for rj in range(N_CHIPS):
                to_sibling(a, rj).start()
                if owns[a] is not None:
                    load_own(a, rj).start()

    def combine():
        for a in range(n):
            for rj in range(N_CHIPS):
                to_sibling(a, rj).wait_recv()
                if owns[a] is not None:
                    load_own(a, rj).wait()
                    mine = owns[a][rj]
                else:
                    mine = pays[a][slot_of(rj, c)]
                sums[a][rj] = (mine.astype(F32) + r1s[a][rj].astype(F32)).astype(sums[a].dtype)
            keep(a).start()
            for rj in range(1, N_CHIPS):
                to_chip(a, rj).start()

    def finish():
        for a in range(n):
            for rj in range(1, N_CHIPS):
                to_chip(a, rj).wait_recv()
            for rj in range(N_CHIPS):
                to_sibling(a, rj).wait_send()
            for rj in range(1, N_CHIPS):
                to_chip(a, rj).wait_send()
            keep(a).wait()

    return start, combine, finish


def _reduce_scratch(shapes_dtypes, own_flags):
    out = []
    for (shape, dtype), own in zip(shapes_dtypes, own_flags):
        if own:
            out.append(pltpu.VMEM((N_CHIPS,) + shape, dtype))
        out += [pltpu.VMEM((N_CHIPS,) + shape, dtype), pltpu.VMEM((N_CHIPS,) + shape, dtype)]
    n = len(shapes_dtypes)
    out += [pltpu.SemaphoreType.DMA((N_CHIPS * n,))] * 4 + [pltpu.SemaphoreType.DMA((2 * N_CHIPS * n,))]
    return out


def _pack_small(ssmall, dwq, dwkv, dconv, dfinal, dgq, dgkv, dattn, dconvg, loss_part):
    ssmall[...] = jnp.zeros_like(ssmall)
    rep = ssmall.at[0]
    for i in range(D_MODEL // LANES):
        rep[ROW_FINAL + i:ROW_FINAL + i + 1, :] = dfinal[:, LANES * i:LANES * (i + 1)]
    for i in range(Q_RANK // LANES):
        rep[ROW_GQ + i:ROW_GQ + i + 1, :] = dgq[:, LANES * i:LANES * (i + 1)]
    rep[ROW_GKV:ROW_GKV + 1, :] = dgkv[...]
    for i in range(CONV_WIDTH // LANES):
        rep[ROW_ATTN + i:ROW_ATTN + i + 1, :] = dattn[:, LANES * i:LANES * (i + 1)]
        rep[ROW_CONVG + i:ROW_CONVG + i + 1, :] = dconvg[:, LANES * i:LANES * (i + 1)]
    rep[ROW_LOSS:ROW_LOSS + 1, :] = loss_part[...]
    for k in range(N_DEV):
        if k:
            ssmall[k, ROW_REPL:, :] = ssmall[0, ROW_REPL:, :]
        for s, e, d in _q_pieces(k):
            for i in range(Q_RANK // LANES):
                ssmall[k, ROW_Q + SHARD_Q * i + s:ROW_Q + SHARD_Q * i + e, :] = dwq[d:d + e - s, LANES * i:LANES * (i + 1)]
        ssmall[k, ROW_KV:ROW_KV + KV_RANK, :] = dwkv[:, _kv_dst(k):_kv_dst(k) + SHARD_KV]
        ssmall[k, ROW_CONV:ROW_CONV + 3, 0:SHARD_CONV] = dconv[0:3, SHARD_CONV * k:SHARD_CONV * (k + 1)]


TOKEN_TILES = 4
TAIL_ROWS = N_META + D_MODEL // LANES


def _reduce_update(r_in, r_out, r_small, g_meta, g_norm, params):
    n_p = len(PARAM_SHAPES)
    names = [n for n, _ in PARAM_SHAPES]

    def body(*refs):
        rin, rout, rsmall, gmeta, gnorm = refs[:5]
        par = refs[5:5 + 3 * n_p]
        o = 5 + 3 * n_p
        g_out = {n: refs[o + i] for i, n in enumerate(names)}
        loss_out = refs[o + n_p]
        upd = refs[o + n_p + 1:o + 4 * n_p + 1]
        gsum = refs[o + 4 * n_p + 1]
        x, y, c, me = _device_position()
        my_chip = 2 * x + y

        g = rin[my_chip].astype(F32)
        for ch in range(1, N_CHIPS):
            g = g + rin[ch ^ my_chip].astype(F32)
        g_out["w_in"][...] = g[:SHARD_IN, :]

        g = rout[my_chip].astype(F32)
        gs = rsmall[my_chip]
        for ch in range(1, N_CHIPS):
            g = g + rout[ch ^ my_chip].astype(F32)
            gs = gs + rsmall[ch ^ my_chip]
        g_out["w_out"][...] = g
        gsum[...] = gs
        for i in range(Q_RANK // LANES):
            g_out["w_q_up"][:, LANES * i:LANES * (i + 1)] = gsum[ROW_Q + SHARD_Q * i:ROW_Q + SHARD_Q * (i + 1), :]
        g_out["w_kv_up"][...] = gsum[ROW_KV:ROW_KV + KV_RANK, :]
        for i in range(3):
            g_out["conv_w"][i] = gsum[ROW_CONV + i:ROW_CONV + i + 1, 0:SHARD_CONV]
        for name, row, width in (("final_norm_g", ROW_FINAL, D_MODEL), ("q_norm_g", ROW_GQ, Q_RANK),
                                 ("kv_norm_g", ROW_GKV, KV_RANK), ("attn_out_g", ROW_ATTN, CONV_WIDTH),
                                 ("conv_out_g", ROW_CONVG, CONV_WIDTH)):
            for i in range(width // LANES):
                g_out[name][:, LANES * i:LANES * (i + 1)] = gsum[row + i:row + i + 1, :]
        loss_out[...] = gsum[ROW_LOSS:ROW_LOSS + 1, :]
        g_out["meta_tokens"][...] = gmeta[...]
        g_out["norm_g"][...] = gnorm[...]

        for i, n in enumerate(names):
            w, m, v = (par[3 * i + j][...] for j in range(3))
            for j, val in enumerate(_adam_update(g_out[n][...], w, m, v)):
                upd[3 * i + j][...] = val

    vm = pl.BlockSpec(memory_space=pltpu.VMEM)
    out_shape = [jax.ShapeDtypeStruct(shape, F32) for _, shape in PARAM_SHAPES]
    out_shape.append(jax.ShapeDtypeStruct((1, LANES), F32))
    for _, shape in PARAM_SHAPES:
        out_shape += [jax.ShapeDtypeStruct(shape, F32)] * 3
    outs = pl.pallas_call(
        body,
        name="reduce_update",
        out_shape=tuple(out_shape),
        in_specs=[vm] * (5 + 3 * n_p),
        out_specs=(vm,) * len(out_shape),
        scratch_shapes=[pltpu.VMEM((SMALL_ROWS, LANES), F32)],
        compiler_params=pltpu.CompilerParams(vmem_limit_bytes=VMEM_LIMIT),
    )(r_in, r_out, r_small, g_meta, g_norm, *[a for n in names for a in params[n]])
    return {n: (outs[i], *outs[n_p + 1 + 3 * i:n_p + 4 + 3 * i]) for i, n in enumerate(names)}, outs[n_p]


def _prep_in_proj(x, meta, norm_g, w_in_t, w_q, w_kv, w_out, conv_w, order):
    nb_seq, s, d = x.shape
    tp = s + LANES
    m = nb_seq * tp
    ts = s // TOKEN_TILES
    n_real = nb_seq * TOKEN_TILES
    n_norm = n_real + 1
    n_tiles = IN_PAD // P_TILE
    _, n_early, n_free = _tile_orders()
    steps = n_norm + n_tiles
    dot_rows = m // 4
    qkv_shape = (SHARD_Q + KV_RANK, Q_RANK)

    def tile(t):
        t = jnp.minimum(t, n_real - 1)
        return t // TOKEN_TILES, t % TOKEN_TILES

    def column_tile(t, order_ref):
        chip = 2 * lax.axis_index("x") + lax.axis_index("y")
        return order_ref[chip * n_tiles + jnp.maximum(t - n_norm, 0)]

    def body(order_ref, x_ref, meta_ref, g_ref, win_ref, wq_ref, wkv_ref, wout_ref, conv_ref,
             u_hbm, p_ref, w_in_p, meta_f, w_q_p, w_kv_p, w_out_f, conv_f,
             u_all, sbig, gbig, smeta, gmeta, sqkv, sout, sconv, gqkv, gout, gconv,
             send_in, recv_in, local_in, send_meta, recv_meta, send_rest, recv_rest, local_rest, u_sem):
        t = pl.program_id(0)
        px, py, pc, me = _device_position()
        u_copy = pltpu.make_async_copy(u_all, u_hbm, u_sem)

        def plan_in():
            return _gather_plan((sbig,), (gbig,), send_in, recv_in, local_in)

        def plan_rest():
            return _gather_plan((sqkv, sout, sconv), (gqkv, gout, gconv), send_rest, recv_rest, local_rest)

        def meta_copies():
            out = []
            for r in range(1, N_DEV):
                peer = (1 - px if r & 4 else px, 1 - py if r & 2 else py, 1 - pc if r & 1 else pc)
                out.append(pltpu.make_async_remote_copy(
                    src_ref=smeta,
                    dst_ref=gmeta.at[r],
                    send_sem=send_meta.at[r - 1],
                    recv_sem=recv_meta.at[r - 1],
                    device_id=peer,
                    device_id_type=pl.DeviceIdType.MESH,
                ))
            return out

        @pl.when(t == 0)
        def _():
            sbig[0:SHARD_IN, :] = win_ref[...].astype(BF16)
            sbig[SHARD_IN:, :] = jnp.zeros((SHARD_IN_PAD - SHARD_IN, d), BF16)
            smeta[...] = meta_ref[...]
            plan_in()[0]()
            for cp in meta_copies():
                cp.start()
            sqkv[...] = jnp.zeros_like(sqkv)
            sqkv[0:SHARD_Q, :] = wq_ref[...].astype(BF16)
            sqkv[SHARD_Q:, 0:SHARD_KV] = wkv_ref[...].astype(BF16)
            sout[...] = wout_ref[...].astype(BF16)
            sconv[...] = jnp.zeros_like(sconv)
            for i in range(3):
                sconv[i:i + 1, 0:SHARD_CONV] = conv_ref[i]

        def norm(h):
            hhat, _ = _rms_stats(h)
            return (hhat * g_ref[...]).astype(BF16)

        @pl.when(t < n_real)
        def _():
            b, k = tile(t)
            row0 = pl.multiple_of(b * tp + LANES + k * ts, 16)
            u_all[pl.ds(row0, ts), :] = norm(x_ref[0])

        @pl.when(t == n_real)
        def _():
            for cp in meta_copies():
                cp.wait_recv()
            gmeta[0] = smeta[...]
            for k in range(N_DEV):
                meta_f[:, SHARD_META * k:SHARD_META * (k + 1)] = gmeta[k ^ me]
            um = norm(meta_f[...])
            for b in range(nb_seq):
                u_all[b * tp:b * tp + PAD_FRONT, :] = jnp.zeros((PAD_FRONT, d), BF16)
                u_all[b * tp + PAD_FRONT:b * tp + LANES, :] = um
            u_copy.start()

        @pl.when(t == n_norm)
        def _():
            plan_in()[1]()

        @pl.when(t == n_norm + n_early)
        def _():
            plan_in()[2]()
            plan_rest()[0]()

        @pl.when(t == n_norm + (n_early + n_free) // 2)
        def _():
            plan_rest()[1]()
            plan_rest()[2]()

        @pl.when(t == n_norm + n_free)
        def _():
            plan_in()[3]()

        @pl.when(t == n_norm + n_free + 2)
        def _():
            plan_rest()[3]()

        @pl.when(t >= n_norm)
        def _():
            j = column_tile(t, order_ref)
            for jj in range(n_tiles):
                @pl.when(j == jj)
                def _(jj=jj):
                    if jj == N_A // P_TILE:
                        w_in_p[N_A % P_TILE:, :] = jnp.zeros((P_TILE - N_A % P_TILE, d), BF16)
                    for k, s0, e0, d0 in _tile_pieces(jj):
                        w_in_p[d0:d0 + e0 - s0, :] = gbig[k, s0:e0, :]
            for r in range(m // dot_rows):
                rows = slice(dot_rows * r, dot_rows * (r + 1))
                p_ref[rows, :] = _dot(u_all[rows, :], w_in_p[...], _NT).astype(BF16)

        @pl.when(t == steps - 1)
        def _():
            plan_in()[4]()
            plan_rest()[4]()
            for cp in meta_copies():
                cp.wait_send()
            u_copy.wait()
            conv_f[...] = jnp.zeros_like(conv_f)
            for k in range(N_DEV):
                for s0, e0, d0 in _q_pieces(k):
                    w_q_p[d0:d0 + e0 - s0, :] = gqkv[k, s0:e0, :]
                w_kv_p[:, _kv_dst(k):_kv_dst(k) + SHARD_KV] = gqkv[k, SHARD_Q:, 0:SHARD_KV]
                w_out_f[SHARD_OUT * k:SHARD_OUT * (k + 1), :] = gout[k]
                conv_f[0:3, SHARD_CONV * k:SHARD_CONV * (k + 1)] = gconv[k, 0:3, 0:SHARD_CONV]

    whole = lambda shape: pl.BlockSpec(shape, lambda t, o: (0,) * len(shape))
    return pl.pallas_call(
        body,
        name="prep_in_proj_gather",
        grid_spec=pltpu.PrefetchScalarGridSpec(
            num_scalar_prefetch=1,
            grid=(steps,),
            in_specs=[
                pl.BlockSpec((1, ts, d), lambda t, o: (*tile(t), 0)),
                whole(meta.shape), whole(norm_g.shape), whole(w_in_t.shape),
                whole(w_q.shape), whole(w_kv.shape), whole(w_out.shape), whole(conv_w.shape),
            ],
            out_specs=(pl.BlockSpec(memory_space=pl.ANY),
                       pl.BlockSpec((m, P_TILE), lambda t, o: (0, column_tile(t, o))),
                       pl.BlockSpec((P_TILE, d), lambda t, o: (column_tile(t, o), 0)),
                       whole((N_META, d)),
                       whole((Q_COLS, Q_RANK)), whole((KV_RANK, KV_COLS)), whole((D_MODEL, D_MODEL)),
                       whole((8, CONV_WIDTH))),
            scratch_shapes=[
                pltpu.VMEM((m, d), BF16),
                pltpu.VMEM((SHARD_IN_PAD, d), BF16),
                pltpu.VMEM((N_DEV, SHARD_IN_PAD, d), BF16),
                pltpu.VMEM((N_META, SHARD_META), F32),
                pltpu.VMEM((N_DEV, N_META, SHARD_META), F32),
                pltpu.VMEM(qkv_shape, BF16),
                pltpu.VMEM((SHARD_OUT, D_MODEL), BF16),
                pltpu.VMEM((8, LANES), F32),
                pltpu.VMEM((N_DEV,) + qkv_shape, BF16),
                pltpu.VMEM((N_DEV, SHARD_OUT, D_MODEL), BF16),
                pltpu.VMEM((N_DEV, 8, LANES), F32),
                pltpu.SemaphoreType.DMA((7,)),
                pltpu.SemaphoreType.DMA((7,)),
                pltpu.SemaphoreType.DMA((1,)),
                pltpu.SemaphoreType.DMA((N_DEV - 1,)),
                pltpu.SemaphoreType.DMA((N_DEV - 1,)),
                pltpu.SemaphoreType.DMA((21,)),
                pltpu.SemaphoreType.DMA((21,)),
                pltpu.SemaphoreType.DMA((3,)),
                pltpu.SemaphoreType.DMA,
            ],
        ),
        out_shape=(jax.ShapeDtypeStruct((m, d), BF16),
                   jax.ShapeDtypeStruct((m, IN_PAD), BF16),
                   jax.ShapeDtypeStruct((IN_PAD, d), BF16),
                   jax.ShapeDtypeStruct((N_META, d), F32),
                   jax.ShapeDtypeStruct((Q_COLS, Q_RANK), BF16),
                   jax.ShapeDtypeStruct((KV_RANK, KV_COLS), BF16),
                   jax.ShapeDtypeStruct((D_MODEL, D_MODEL), BF16),
                   jax.ShapeDtypeStruct((8, CONV_WIDTH), F32)),
        compiler_params=_params("arbitrary"),
    )(order, x, meta, norm_g, w_in_t, w_q, w_kv, w_out, conv_w)


def _rope_tables(tp):
    half = D_ROPE // 2
    inv_freq = (1.0 / (ROPE_THETA ** (np.arange(half, dtype=np.float32) / half))).astype(np.float32)
    pos = (np.arange(tp) - PAD_FRONT).astype(np.float32)
    ang = pos[:, None] * inv_freq[None, :]
    cos = np.tile(np.cos(ang), (1, LANES // half))
    sin = np.tile(np.sin(ang), (1, LANES // half))
    first = (np.arange(LANES) % D_ROPE) < half
    zero = np.float32(0.0)
    return tuple(jnp.asarray(t, F32) for t in (cos, np.where(first, -sin, zero), np.where(first, zero, sin)))


def _rope(t, cos, sa, sb):
    return t * cos + pltpu.roll(t, LANES - D_ROPE // 2, 1) * sa + pltpu.roll(t, D_ROPE // 2, 1) * sb


def _rope_t(t, cos, sa, sb):
    return t * cos + pltpu.roll(t * sa, D_ROPE // 2, 1) + pltpu.roll(t * sb, LANES - D_ROPE // 2, 1)


def _qkv_fwd(p, wq, wkv, gq, gkv, tables, nb_seq, tp):
    ht = tp // 2

    def body(pa_ref, wq_ref, wkv_ref, gq_ref, gkv_ref, cos_ref, sa_ref, sb_ref, q_ref, k_ref, v_ref):
        pa = pa_ref[...].astype(F32)
        cq_hat, _ = _rms_stats(pa[:, :Q_RANK])
        ckv_hat, _ = _rms_stats(pa[:, Q_RANK:Q_RANK + KV_RANK])
        q = _dot((cq_hat * gq_ref[...]).astype(BF16), wq_ref[...], _NT) * Q_SCALE
        kv = _dot((ckv_hat * gkv_ref[...]).astype(BF16), wkv_ref[...])
        tabs = (cos_ref[...], sa_ref[...], sb_ref[...])
        lane = lax.broadcasted_iota(jnp.int32, (ht, LANES), 1)
        low = lane < D_ROPE
        mark = lane == D_ROPE
        row = (pl.program_id(0) % 2) * ht + lax.broadcasted_iota(jnp.int32, (ht, LANES), 0)
        k_pe = jnp.where(mark & (row < PAD_FRONT), NEG_INF, _rope(pa[:, Q_RANK + KV_RANK:], *tabs))
        one = jnp.where(mark & (row >= PAD_FRONT), 1.0, 0.0)
        pairs = [_rope(q[:, N_HEADS * D_NOPE + LANES * i:N_HEADS * D_NOPE + LANES * (i + 1)], *tabs) for i in range(2)]
        for h in range(N_HEADS):
            pair = pairs[h // 2]
            if h % 2:
                pair = pltpu.roll(pair, D_ROPE, 1)
            pe = jnp.where(low, pair, one)
            q_ref[0, h] = jnp.concatenate([q[:, D_NOPE * h:D_NOPE * (h + 1)], pe], axis=1).astype(BF16)
            k_ref[0, h] = jnp.concatenate([kv[:, D_NOPE * h:D_NOPE * (h + 1)], k_pe], axis=1).astype(BF16)
            v_ref[0, h] = kv[:, N_HEADS * D_NOPE + D_V * h:N_HEADS * D_NOPE + D_V * (h + 1)].astype(BF16)

    full = lambda a: pl.BlockSpec(a.shape, lambda i: (0,) * a.ndim)
    tab = pl.BlockSpec((ht, LANES), lambda i: (i % 2, 0))
    qk = pl.BlockSpec((1, N_HEADS, ht, 2 * LANES), lambda i: (i // 2, 0, i % 2, 0))
    return pl.pallas_call(
        body,
        name="qkv_fwd",
        grid=(2 * nb_seq,),
        in_specs=[pl.BlockSpec((ht, GRP_A), lambda i: (i, 0)), full(wq), full(wkv), full(gq), full(gkv), tab, tab, tab],
        out_specs=(qk, qk, pl.BlockSpec((1, N_HEADS, ht, D_V), lambda i: (i // 2, 0, i % 2, 0))),
        out_shape=(
            jax.ShapeDtypeStruct((nb_seq, N_HEADS, tp, 2 * LANES), BF16),
            jax.ShapeDtypeStruct((nb_seq, N_HEADS, tp, 2 * LANES), BF16),
            jax.ShapeDtypeStruct((nb_seq, N_HEADS, tp, D_V), BF16),
        ),
        compiler_params=_params("parallel"),
    )(p, wq, wkv, gq, gkv, *tables)


def _attn_fwd(q, k, v, p, g_attn):
    nb_seq, _, tp, _ = q.shape

    def body(q_ref, k_ref, v_ref, z_ref, g_ref, y_ref, o_ref, lse_ref):
        g = g_ref[...]
        for r0 in range(0, tp, Q_TILE):
            nq = min(Q_TILE, tp - r0)
            kend = r0 + nq
            qq = q_ref[0, 0, r0:kend, :]
            sd = _dot(qq, k_ref[0, 0, r0:kend, :], _NT)
            causal = (lax.broadcasted_iota(jnp.int32, (nq, nq), 1) <= lax.broadcasted_iota(jnp.int32, (nq, nq), 0))
            sd = jnp.where(causal, sd, NEG_INF)
            m = jnp.max(sd, axis=-1, keepdims=True)
            if r0:
                so = _dot(qq, k_ref[0, 0, 0:r0, :], _NT)
                m = jnp.maximum(m, jnp.max(so, axis=-1, keepdims=True))
            ed = jnp.exp2(sd - m)
            l = jnp.sum(ed, axis=-1, keepdims=True)
            o = _dot(ed.astype(BF16), v_ref[0, 0, r0:kend, :])
            if r0:
                eo = jnp.exp2(so - m)
                l = l + jnp.sum(eo, axis=-1, keepdims=True)
                o = o + _dot(eo.astype(BF16), v_ref[0, 0, 0:r0, :])
            o = o * (1.0 / l)
            o_ref[0, 0, r0:kend, :] = o
            lse_ref[0, 0, r0:kend, :] = jnp.broadcast_to(m + jnp.log2(l), (nq, LANES))
            ohat, _ = _rms_stats(o)
            z = z_ref[r0:kend, :].astype(F32)
            y_ref[r0:kend, :] = (ohat * g * (z * _sigmoid(z))).astype(BF16)

    qk = pl.BlockSpec((1, 1, tp, 2 * LANES), lambda b, h: (b, h, 0, 0))
    hv = pl.BlockSpec((1, 1, tp, D_V), lambda b, h: (b, h, 0, 0))
    return pl.pallas_call(
        body,
        name="attn_fwd",
        grid=(nb_seq, N_HEADS),
        in_specs=[qk, qk, hv,
                  pl.BlockSpec((tp, LANES), lambda b, h: (b, GRP_A // LANES + h)),
                  pl.BlockSpec((1, LANES), lambda b, h: (0, h))],
        out_specs=(pl.BlockSpec((tp, LANES), lambda b, h: (b, h)), hv, hv),
        out_shape=(
            jax.ShapeDtypeStruct((nb_seq * tp, N_HEADS * D_V), BF16),
            jax.ShapeDtypeStruct((nb_seq, N_HEADS, tp, D_V), F32),
            jax.ShapeDtypeStruct((nb_seq, N_HEADS, tp, LANES), F32),
        ),
        compiler_params=_params("parallel", "parallel"),
    )(q, k, v, p, g_attn)


_CONV_COL0 = (GRP_A + N_HEADS * D_V) // LANES


def _conv_specs(tp, order):
    cols = CONV_WIDTH // LANES
    return [pl.BlockSpec((tp, LANES), functools.partial(
        lambda a, b, off: order(a, b, off), off=_CONV_COL0 + i * cols)) for i in range(4)]


def _conv_fwd(p, conv_w, g_conv, nb_seq, tp):
    def body(b_ref, c_ref, h_ref, z_ref, w_ref, g_ref, y_ref):
        cc = c_ref[...].astype(F32) * h_ref[...].astype(F32)
        row = lax.broadcasted_iota(jnp.int32, (tp, LANES), 0)
        s1 = jnp.where(row >= 1, pltpu.roll(cc, 1, 0), 0.0)
        s2 = jnp.where(row >= 2, pltpu.roll(cc, 2, 0), 0.0)
        yc = b_ref[...].astype(F32) * (w_ref[0:1, :] * s2 + w_ref[1:2, :] * s1 + w_ref[2:3, :] * cc)
        r = lax.rsqrt(_group_mean(yc * yc) + EPS)
        z = z_ref[...].astype(F32)
        y_ref[...] = (yc * r * g_ref[...] * (z * _sigmoid(z))).astype(BF16)

    return pl.pallas_call(
        body,
        name="conv_fwd",
        grid=(nb_seq, CONV_WIDTH // LANES),
        in_specs=_conv_specs(tp, lambda b, t, off: (b, off + t)) + [
            pl.BlockSpec((8, LANES), lambda b, t: (0, t)),
            pl.BlockSpec((1, LANES), lambda b, t: (0, t))],
        out_specs=pl.BlockSpec((tp, LANES), lambda b, t: (b, t)),
        out_shape=jax.ShapeDtypeStruct((nb_seq * tp, CONV_WIDTH), BF16),
        compiler_params=_params("parallel", "parallel"),
    )(p, p, p, p, conv_w, g_conv)


def _token_copy(hbm, b, k, ts, buf, sem, to_hbm=False):
    lo, hi = max(k * ts - LANES, 0), (k + 1) * ts - LANES
    off = lo - (k * ts - LANES)
    src, dst = hbm.at[b, pl.ds(lo, hi - lo)], buf.at[pl.ds(off, hi - lo)]
    if to_hbm:
        src, dst = dst, src
    return pltpu.make_async_copy(src, dst, sem)


def _for_tile(k, nt, fn):
    for kk in range(nt):
        @pl.when(k == kk)
        def _(kk=kk):
            fn(kk)


def _out_proj_loss(ya, yc, w_out, x, target, g_final, nt):
    nb_seq, s, d = x.shape
    r, ka = ya.shape
    ts = (s + LANES) // nt
    steps = nb_seq * nt

    def body(a_ref, c_ref, w_ref, x_hbm, t_hbm, g_ref, dhb_ref, dg_ref, loss_ref,
             xbuf, tbuf, acc_ref, sems):
        i = pl.program_id(0)
        b, k = i // nt, i % nt

        @pl.when(i == 0)
        def _():
            acc_ref[...] = jnp.zeros_like(acc_ref)
            dg_ref[...] = jnp.zeros_like(dg_ref)

        slot = i % 2

        def fetch(seq, kk, sl):
            return [_token_copy(x_hbm, seq, kk, ts, xbuf.at[sl], sems.at[sl, 0]),
                    _token_copy(t_hbm, seq, kk, ts, tbuf.at[sl], sems.at[sl, 1])]

        def start(seq, sl, kk):
            if kk == 0:
                xbuf[sl, 0:LANES, :] = jnp.zeros((LANES, d), F32)
                tbuf[sl, 0:LANES, :] = jnp.zeros((LANES, d), F32)
            for cp in fetch(seq, kk, sl):
                cp.start()

        @pl.when(i == 0)
        def _():
            start(0, 0, 0)

        @pl.when(i + 1 < steps)
        def _():
            _for_tile((i + 1) % nt, nt, functools.partial(start, (i + 1) // nt, 1 - slot))

        mix = _dot(a_ref[...], w_ref[0:ka, :]) + _dot(c_ref[...], w_ref[ka:, :])
        _for_tile(k, nt, lambda kk: [cp.wait() for cp in fetch(b, kk, slot)])

        real = (lax.broadcasted_iota(jnp.int32, (ts, d), 0) >= LANES) | (k > 0)
        g = g_ref[...]
        hhat, rstd = _rms_stats(xbuf[slot] + mix)
        e = jnp.where(real, hhat * g - tbuf[slot], 0.0)
        acc_ref[...] += jnp.sum(e * e, axis=0, keepdims=True)
        dy = e * (1.0 / d)
        dg_ref[...] += jnp.sum(dy * hhat, axis=0, keepdims=True)
        dhb_ref[...] = _rms_bwd(g * dy, hhat, rstd).astype(BF16)

        @pl.when(i == steps - 1)
        def _():
            total = jnp.sum(acc_ref[...], axis=1, keepdims=True)
            loss_ref[...] = jnp.broadcast_to((0.5 / d) * total, loss_ref.shape)

    hbm = pl.BlockSpec(memory_space=pl.ANY)
    row = pl.BlockSpec((ts, d), lambda i: (i, 0))
    vec = pl.BlockSpec((1, d), lambda i: (0, 0))
    return pl.pallas_call(
        body,
        name="out_proj_loss",
        grid=(steps,),
        in_specs=[pl.BlockSpec((ts, ka), lambda i: (i, 0)), pl.BlockSpec((ts, yc.shape[1]), lambda i: (i, 0)),
                  pl.BlockSpec(w_out.shape, lambda i: (0, 0)), hbm, hbm, vec],
        out_specs=(row, vec, pl.BlockSpec((1, LANES), lambda i: (0, 0))),
        out_shape=(
            jax.ShapeDtypeStruct((r, d), BF16),
            jax.ShapeDtypeStruct((1, d), F32),
            jax.ShapeDtypeStruct((1, LANES), F32),
        ),
        scratch_shapes=[pltpu.VMEM((2, ts, d), F32), pltpu.VMEM((2, ts, d), F32), pltpu.VMEM((1, d), F32),
                        pltpu.SemaphoreType.DMA((2, 2))],
        compiler_params=_params("arbitrary"),
    )(ya, yc, w_out, x, target, g_final)


def _out_proj_bwd(dhb, w_out, ya, yc, bm):
    r, d = dhb.shape
    ka = ya.shape[1]
    n_mix = w_out.shape[0]
    last = r // bm - 1

    def body(dh_ref, w_ref, a_ref, c_ref, dcat_ref, dw_ref, acc_ref):
        @pl.when(pl.program_id(0) == 0)
        def _():
            acc_ref[...] = jnp.zeros_like(acc_ref)

        dh = dh_ref[...]
        dcat_ref[...] = _dot(dh, w_ref[...], _NT).astype(BF16)
        acc_ref[0:ka, :] += _dot(a_ref[...], dh, _TN)
        acc_ref[ka:, :] += _dot(c_ref[...], dh, _TN)

        @pl.when(pl.program_id(0) == last)
        def _():
            dw_ref[...] = acc_ref[...].astype(BF16)

    return pl.pallas_call(
        body,
        name="out_proj_bwd",
        grid=(r // bm,),
        in_specs=[pl.BlockSpec((bm, d), lambda i: (i, 0)), pl.BlockSpec(w_out.shape, lambda i: (0, 0)),
                  pl.BlockSpec((bm, ka), lambda i: (i, 0)), pl.BlockSpec((bm, yc.shape[1]), lambda i: (i, 0))],
        out_specs=(pl.BlockSpec((bm, n_mix), lambda i: (i, 0)),
                   pl.BlockSpec((n_mix, d), lambda i: (0, 0))),
        out_shape=(jax.ShapeDtypeStruct((r, n_mix), BF16),
                   jax.ShapeDtypeStruct((n_mix, d), BF16)),
        scratch_shapes=[pltpu.VMEM((n_mix, d), F32)],
        compiler_params=_params("arbitrary"),
    )(dhb, w_out, ya, yc)


def _attn_bwd(q, k, v, o, lse, dcat, p, g_attn):
    nb_seq, _, tp, _ = q.shape

    def body(q_ref, k_ref, v_ref, o_ref, lse_ref, dy_ref, z_ref, g_ref,
             dq_ref, dk_ref, dv_ref, dz_ref, dg_ref, dq_acc):
        @pl.when(pl.program_id(1) == 0)
        def _():
            dg_ref[...] = jnp.zeros_like(dg_ref)

        g = g_ref[...]
        z = z_ref[...].astype(F32)
        o = o_ref[0, 0]
        dy = dy_ref[...].astype(F32)
        sig = _sigmoid(z)
        ohat, r = _rms_stats(o)
        don = dy * (z * sig)
        dz_ref[...] = (dy * (ohat * g) * (sig * (1.0 + z * (1.0 - sig)))).astype(BF16)
        dg_ref[...] += jnp.sum(don * ohat, axis=0, keepdims=True)
        do = _rms_bwd(g * don, ohat, r)
        dvec = jnp.sum(do * o, axis=-1, keepdims=True)
        dob = do.astype(BF16)
        lse_col = lse_ref[0, 0, :, 0:1]
        dq_acc[...] = jnp.zeros_like(dq_acc)
        for k0 in range(0, tp, K_TILE):
            nk = min(K_TILE, tp - k0)
            nq = tp - k0
            qq = q_ref[0, 0, k0:, :]
            kk = k_ref[0, 0, k0:k0 + nk, :]
            causal = (lax.broadcasted_iota(jnp.int32, (nq, nk), 1) <= lax.broadcasted_iota(jnp.int32, (nq, nk), 0))
            pr = jnp.where(causal, jnp.exp2(_dot(qq, kk, _NT) - lse_col[k0:]), 0.0)
            dp = _dot(dob[k0:], v_ref[0, 0, k0:k0 + nk, :], _NT)
            ds = (pr * (dp - dvec[k0:])).astype(BF16)
            dv_ref[0, 0, k0:k0 + nk, :] = _dot(pr.astype(BF16), dob[k0:], _TN).astype(BF16)
            dk_ref[0, 0, k0:k0 + nk, :] = (_dot(ds, qq, _TN) * (ATTN_SCALE / Q_SCALE)).astype(BF16)
            dq_acc[k0:, :] += _dot(ds, kk)
        dq_ref[0, 0] = (dq_acc[...] * ATTN_SCALE).astype(BF16)

    qk = pl.BlockSpec((1, 1, tp, 2 * LANES), lambda h, b: (b, h, 0, 0))
    hv = pl.BlockSpec((1, 1, tp, D_V), lambda h, b: (b, h, 0, 0))
    col = pl.BlockSpec((tp, LANES), lambda h, b: (b, h))
    return pl.pallas_call(
        body,
        name="attn_bwd",
        grid=(N_HEADS, nb_seq),
        in_specs=[qk, qk, hv, hv, hv, col,
                  pl.BlockSpec((tp, LANES), lambda h, b: (b, GRP_A // LANES + h)),
                  pl.BlockSpec((1, LANES), lambda h, b: (0, h))],
        out_specs=(qk, qk, hv, col, pl.BlockSpec((1, LANES), lambda h, b: (0, h))),
        out_shape=(
            jax.ShapeDtypeStruct((nb_seq, N_HEADS, tp, 2 * LANES), BF16),
            jax.ShapeDtypeStruct((nb_seq, N_HEADS, tp, 2 * LANES), BF16),
            jax.ShapeDtypeStruct((nb_seq, N_HEADS, tp, D_V), BF16),
            jax.ShapeDtypeStruct((nb_seq * tp, N_HEADS * D_V), BF16),
            jax.ShapeDtypeStruct((1, N_HEADS * D_V), F32),
        ),
        scratch_shapes=[pltpu.VMEM((tp, 2 * LANES), F32)],
        compiler_params=_params("arbitrary", "arbitrary"),
    )(q, k, v, o, lse, dcat, p, g_attn)


def _qkv_bwd(p, dq, dk, dv, wq, wkv, gq, gkv, tables):
    nb_seq, _, tp, _ = dq.shape
    ht = tp // 2

    def body(pa_ref, dq_ref, dk_ref, dv_ref, wq_ref, wkv_ref, gq_ref, gkv_ref, cos_ref, sa_ref, sb_ref,
             dpa_ref, dwq_ref, dwkv_ref, dgq_ref, dgkv_ref):
        @pl.when(pl.program_id(0) == 0)
        def _():
            dwq_ref[...] = jnp.zeros_like(dwq_ref)
            dwkv_ref[...] = jnp.zeros_like(dwkv_ref)
            dgq_ref[...] = jnp.zeros_like(dgq_ref)
            dgkv_ref[...] = jnp.zeros_like(dgkv_ref)

        pa = pa_ref[...].astype(F32)
        gq, gkv = gq_ref[...], gkv_ref[...]
        cq_hat, rq = _rms_stats(pa[:, :Q_RANK])
        ckv_hat, rkv = _rms_stats(pa[:, Q_RANK:Q_RANK + KV_RANK])
        tabs = (cos_ref[...], sa_ref[...], sb_ref[...])

        pe = [dq_ref[0, h, :, D_NOPE:].astype(F32) for h in range(N_HEADS)]
        pairs = [_rope_t(pe[2 * i] + pltpu.roll(pe[2 * i + 1], D_ROPE, 1), *tabs).astype(BF16) for i in range(2)]
        dq_flat = jnp.concatenate([dq_ref[0, h, :, :D_NOPE] for h in range(N_HEADS)] + pairs, axis=1)
        dwq_ref[...] += _dot(dq_flat, (cq_hat * gq).astype(BF16), _TN)
        dcqn = _dot(dq_flat, wq_ref[...])
        dgq_ref[...] += jnp.sum(dcqn * cq_hat, axis=0, keepdims=True)
        dcq = _rms_bwd(gq * dcqn, cq_hat, rq)

        dkv_flat = jnp.concatenate([dk_ref[0, h, :, :D_NOPE] for h in range(N_HEADS)]
                                   + [dv_ref[0, h] for h in range(N_HEADS)], axis=1)
        dwkv_ref[...] += _dot((ckv_hat * gkv).astype(BF16), dkv_flat, _TN)
        dckvn = _dot(dkv_flat, wkv_ref[...], _NT)
        dgkv_ref[...] += jnp.sum(dckvn * ckv_hat, axis=0, keepdims=True)
        dckv = _rms_bwd(gkv * dckvn, ckv_hat, rkv)

        dk_pe = dk_ref[0, 0, :, D_NOPE:].astype(F32)
        for h in range(1, N_HEADS):
            dk_pe = dk_pe + dk_ref[0, h, :, D_NOPE:].astype(F32)
        dk_pe = jnp.where(lax.broadcasted_iota(jnp.int32, (ht, LANES), 1) < D_ROPE, dk_pe, 0.0)
        dpa_ref[...] = jnp.concatenate([dcq, dckv, _rope_t(dk_pe, *tabs)], axis=1).astype(BF16)

    full = lambda a: pl.BlockSpec(a.shape, lambda i: (0,) * a.ndim)
    tab = pl.BlockSpec((ht, LANES), lambda i: (i % 2, 0))
    qk = pl.BlockSpec((1, N_HEADS, ht, 2 * LANES), lambda i: (i // 2, 0, i % 2, 0))
    acc = lambda shape: pl.BlockSpec(shape, lambda i: (0, 0))
    return pl.pallas_call(
        body,
        name="qkv_bwd",
        grid=(2 * nb_seq,),
        in_specs=[pl.BlockSpec((ht, GRP_A), lambda i: (i, 0)), qk, qk,
                  pl.BlockSpec((1, N_HEADS, ht, D_V), lambda i: (i // 2, 0, i % 2, 0)),
                  full(wq), full(wkv), full(gq), full(gkv), tab, tab, tab],
        out_specs=(pl.BlockSpec((ht, GRP_A), lambda i: (i, 0)),
                   acc(wq.shape), acc(wkv.shape), acc((1, Q_RANK)), acc((1, KV_RANK))),
        out_shape=(
            jax.ShapeDtypeStruct((nb_seq * tp, GRP_A), BF16),
            jax.ShapeDtypeStruct(wq.shape, F32),
            jax.ShapeDtypeStruct(wkv.shape, F32),
            jax.ShapeDtypeStruct((1, Q_RANK), F32),
            jax.ShapeDtypeStruct((1, KV_RANK), F32),
        ),
        compiler_params=_params("arbitrary"),
    )(p, dq, dk, dv, wq, wkv, gq, gkv, *tables)


def _conv_bwd(p, dcat, conv_w, g_conv, nb_seq, tp):
    cols = CONV_WIDTH // LANES

    n_in, n_slots = 5, 3
    steps = cols * nb_seq
    in_cols = [_CONV_COL0 + i * cols for i in range(4)] + [N_HEADS * D_V // LANES]

    def body(p_hbm, dcat_hbm, w_ref, g_ref, db_ref, dc_ref, dh_ref, dz_ref, dw_ref, dg_ref, bufs, sems):
        s = pl.program_id(0) * nb_seq + pl.program_id(1)

        def fetch(step):
            t, b = step // nb_seq, step % nb_seq
            slot = step % n_slots
            rows = pl.ds(pl.multiple_of(b * tp, LANES), tp)
            return [pltpu.make_async_copy(
                (dcat_hbm if i == n_in - 1 else p_hbm).at[rows, pl.ds(pl.multiple_of((in_cols[i] + t) * LANES, LANES), LANES)],
                bufs.at[slot, i], sems.at[slot, i]) for i in range(n_in)]

        @pl.when(s == 0)
        def _():
            for step in range(min(n_slots - 1, steps)):
                for cp in fetch(step):
                    cp.start()

        @pl.when(s + n_slots - 1 < steps)
        def _():
            for cp in fetch(s + n_slots - 1):
                cp.start()

        for cp in fetch(s):
            cp.wait()

        @pl.when(pl.program_id(1) == 0)
        def _():
            dw_ref[...] = jnp.zeros_like(dw_ref)
            dg_ref[...] = jnp.zeros_like(dg_ref)

        tiles = bufs.at[s % n_slots]
        cb, c, h = tiles[0].astype(F32), tiles[1].astype(F32), tiles[2].astype(F32)
        z, dy = tiles[3].astype(F32), tiles[4].astype(F32)
        g = g_ref[...]
        w0, w1, w2 = w_ref[0:1, :], w_ref[1:2, :], w_ref[2:3, :]
        cc = c * h
        row = lax.broadcasted_iota(jnp.int32, (tp, LANES), 0)
        s1 = jnp.where(row >= 1, pltpu.roll(cc, 1, 0), 0.0)
        s2 = jnp.where(row >= 2, pltpu.roll(cc, 2, 0), 0.0)
        dwc = w0 * s2 + w1 * s1 + w2 * cc
        yc = cb * dwc
        r = lax.rsqrt(_group_mean(yc * yc) + EPS)
        ychat = yc * r
        sig = _sigmoid(z)
        dz_ref[...] = (dy * (ychat * g) * (sig * (1.0 + z * (1.0 - sig)))).astype(BF16)
        dyn = dy * (z * sig)
        dg_ref[...] += jnp.sum(dyn * ychat, axis=0, keepdims=True)
        gd = g * dyn
        dyc = r * (gd - ychat * _group_mean(gd * ychat))
        db_ref[...] = (dyc * dwc).astype(BF16)
        ddw = dyc * cb
        dw_ref[0:1, :] += jnp.sum(ddw * s2, axis=0, keepdims=True)
        dw_ref[1:2, :] += jnp.sum(ddw * s1, axis=0, keepdims=True)
        dw_ref[2:3, :] += jnp.sum(ddw * cc, axis=0, keepdims=True)
        u1 = jnp.where(row <= tp - 2, pltpu.roll(ddw, tp - 1, 0), 0.0)
        u2 = jnp.where(row <= tp - 3, pltpu.roll(ddw, tp - 2, 0), 0.0)
        dcc = w2 * ddw + w1 * u1 + w0 * u2
        dc_ref[...] = (dcc * h).astype(BF16)
        dh_ref[...] = (dcc * c).astype(BF16)

    col = pl.BlockSpec((tp, LANES), lambda t, b: (b, t))
    out = jax.ShapeDtypeStruct((nb_seq * tp, CONV_WIDTH), BF16)
    return pl.pallas_call(
        body,
        name="conv_bwd",
        grid=(cols, nb_seq),
        in_specs=[pl.BlockSpec(memory_space=pl.ANY), pl.BlockSpec(memory_space=pl.ANY),
                  pl.BlockSpec((8, LANES), lambda t, b: (0, t)),
                  pl.BlockSpec((1, LANES), lambda t, b: (0, t))],
        out_specs=(col, col, col, col,
                   pl.BlockSpec((8, LANES), lambda t, b: (0, t)), pl.BlockSpec((1, LANES), lambda t, b: (0, t))),
        out_shape=(out, out, out, out,
                   jax.ShapeDtypeStruct((8, CONV_WIDTH), F32), jax.ShapeDtypeStruct((1, CONV_WIDTH), F32)),
        scratch_shapes=[pltpu.VMEM((n_slots, n_in, tp, LANES), BF16), pltpu.SemaphoreType.DMA((n_slots, n_in))],
        compiler_params=_params("arbitrary", "arbitrary"),
    )(p, dcat, conv_w, g_conv)


def _input_bwd(dps, w_in, x, meta, dh, norm_g, nt, send_in):
    nb_seq, s, d = x.shape
    r, kb = dps[0].shape
    ts = (s + LANES) // nt
    steps = nb_seq * nt
    n_dp = len(dps)
    in_slot = send_in.shape[1:]

    def body(*refs):
        dp_refs, w_ref, x_hbm, meta_ref, dh_ref, g_ref, pay_ref = refs[:n_dp], *refs[n_dp:n_dp + 6]
        o = n_dp + 6
        gx_hbm, dmeta_ref, dg_ref, r2_in, gmeta_ref, gnorm_ref = refs[o:o + 6]
        xbuf, gxbuf, tok_sems, own_in, r1_in, sum_in = refs[o + 6:o + 12]
        sems = refs[o + 12:-4]
        stail, rtail, tail_send, tail_recv = refs[-4:]
        i = pl.program_id(0)
        b, k = i // nt, i % nt

        def plan():
            return _reduce_plan((pay_ref,), (own_in,), (r1_in,), (sum_in,), (r2_in,), *sems)

        @pl.when(i == 0)
        def _():
            dmeta_ref[...] = jnp.zeros_like(dmeta_ref)
            dg_ref[...] = jnp.zeros_like(dg_ref)
            plan()[0]()

        @pl.when(i == 1)
        def _():
            plan()[1]()

        def start(kk):
            if kk == 0:
                xbuf[0:PAD_FRONT, :] = jnp.zeros((PAD_FRONT, d), F32)
                xbuf[PAD_FRONT:LANES, :] = meta_ref[...]
            _token_copy(x_hbm, b, kk, ts, xbuf, tok_sems.at[0]).start()

        _for_tile(k, nt, start)
        du = _dot(dp_refs[0][...], w_ref[0:kb, :])
        for j in range(1, n_dp):
            du = du + _dot(dp_refs[j][...], w_ref[kb * j:kb * (j + 1), :])
        _for_tile(k, nt, lambda kk: _token_copy(x_hbm, b, kk, ts, xbuf, tok_sems.at[0]).wait())

        g = g_ref[...]
        hhat, rstd = _rms_stats(xbuf[...])
        dg_ref[...] += jnp.sum(du * hhat, axis=0, keepdims=True)
        res = _rms_bwd(g * du, hhat, rstd) + dh_ref[...].astype(F32)

        @pl.when(i > 0)
        def _():
            _for_tile(k, nt, lambda kk: _token_copy(gx_hbm, b, (kk - 1) % nt, ts, gxbuf, tok_sems.at[1], True).wait())

        gxbuf[...] = res

        @pl.when(k == 0)
        def _():
            dmeta_ref[...] += gxbuf[PAD_FRONT:LANES, :]

        _for_tile(k, nt, lambda kk: _token_copy(gx_hbm, b, kk, ts, gxbuf, tok_sems.at[1], True).start())

        @pl.when(i == steps - 1)
        def _():
            _token_copy(gx_hbm, b, nt - 1, ts, gxbuf, tok_sems.at[1], True).wait()
            px, py, pc, me = _device_position()
            for dev in range(N_DEV):
                stail[dev, 0:N_META, :] = dmeta_ref[:, SHARD_META * dev:SHARD_META * (dev + 1)]
                for j in range(d // LANES):
                    stail[dev, N_META + j:N_META + j + 1, :] = dg_ref[:, LANES * j:LANES * (j + 1)]
            copies = []
            for r in range(1, N_DEV):
                peer = (1 - px if r & 4 else px, 1 - py if r & 2 else py, 1 - pc if r & 1 else pc)
                copies.append(pltpu.make_async_remote_copy(
                    src_ref=stail.at[4 * peer[0] + 2 * peer[1] + peer[2]],
                    dst_ref=rtail.at[r],
                    send_sem=tail_send.at[r - 1],
                    recv_sem=tail_recv.at[r - 1],
                    device_id=peer,
                    device_id_type=pl.DeviceIdType.MESH,
                ))
            for cp in copies:
                cp.start()
            rtail[0] = stail[me]
            plan()[2]()
            for cp in copies:
                cp.wait_recv()
            gt = rtail[me]
            for dev in range(1, N_DEV):
                gt = gt + rtail[dev ^ me]
            gmeta_ref[...] = gt[0:N_META, :]
            for j in range(d // LANES):
                gnorm_ref[:, LANES * j:LANES * (j + 1)] = gt[N_META + j:N_META + j + 1, :]
            for cp in copies:
                cp.wait_send()

    whole = lambda a: pl.BlockSpec(a.shape, lambda i: (0,) * a.ndim)
    hbm = pl.BlockSpec(memory_space=pl.ANY)
    return pl.pallas_call(
        body,
        name="input_bwd",
        grid=(steps,),
        in_specs=[pl.BlockSpec((ts, kb), lambda i: (i, 0)) for _ in dps]
        + [whole(w_in), hbm, whole(meta), pl.BlockSpec((ts, d), lambda i: (i, 0)), whole(norm_g), hbm],
        out_specs=(hbm, pl.BlockSpec((N_META, d), lambda i: (0, 0)), pl.BlockSpec((1, d), lambda i: (0, 0)), hbm,
                   pl.BlockSpec((N_META, SHARD_META), lambda i: (0, 0)), pl.BlockSpec((1, d), lambda i: (0, 0))),
        out_shape=(jax.ShapeDtypeStruct((nb_seq, s, d), F32),
                   jax.ShapeDtypeStruct((N_META, d), F32),
                   jax.ShapeDtypeStruct((1, d), F32),
                   jax.ShapeDtypeStruct((N_CHIPS,) + in_slot, BF16),
                   jax.ShapeDtypeStruct((N_META, SHARD_META), F32),
                   jax.ShapeDtypeStruct((1, d), F32)),
        scratch_shapes=[pltpu.VMEM((ts, d), F32), pltpu.VMEM((ts, d), F32), pltpu.SemaphoreType.DMA((2,))]
        + _reduce_scratch([(in_slot, BF16)], [True])
        + [pltpu.VMEM((N_DEV, TAIL_ROWS, LANES), F32), pltpu.VMEM((N_DEV, TAIL_ROWS, LANES), F32),
           pltpu.SemaphoreType.DMA((N_DEV - 1,)), pltpu.SemaphoreType.DMA((N_DEV - 1,))],
        compiler_params=_params("arbitrary"),
    )(*dps, w_in, x, meta, dh, norm_g, send_in)


def _in_proj_bwd_w(u, dps, bm, small_grads, send_out):
    r, d = u.shape
    kb = dps[0].shape[1]
    steps = r // bm
    n_dp, n_small = len(dps), len(small_grads)
    out_slot, small_slot = send_out.shape[1:], (SMALL_ROWS, LANES)

    def body(*refs):
        u_ref, dp_refs = refs[0], refs[1:1 + n_dp]
        small_refs = refs[1 + n_dp:1 + n_dp + n_small]
        o = 1 + n_dp + n_small
        pay_out, o_ref, r2_out, r2_small = refs[o:o + 4]
        acc_ref, ssmall, r1_out, sum_out, r1_small, sum_small = refs[o + 4:o + 10]
        sems = refs[o + 10:]
        i = pl.program_id(0)

        def plan():
            return _reduce_plan((pay_out, ssmall), (None, None), (r1_out, r1_small), (sum_out, sum_small),
                                (r2_out, r2_small), *sems)

        @pl.when(i == 0)
        def _():
            acc_ref[...] = jnp.zeros_like(acc_ref)
            _pack_small(ssmall, *small_refs)
            plan()[0]()

        @pl.when(i == 1)
        def _():
            plan()[1]()

        uu = u_ref[...]
        for j in range(n_dp):
            acc_ref[kb * j:kb * (j + 1), :] += _dot(dp_refs[j][...], uu, _TN)

        @pl.when(i == steps - 1)
        def _():
            for k in range(N_DEV):
                for s, e, c0 in _in_pieces(k):
                    o_ref[k, s:e, :] = acc_ref[c0:c0 + e - s, :].astype(BF16)
                o_ref[k, SHARD_IN:, :] = jnp.zeros((SHARD_IN_PAD - SHARD_IN, d), BF16)
            plan()[2]()

    whole = lambda a: pl.BlockSpec(a.shape, lambda i: (0,) * a.ndim)
    hbm = pl.BlockSpec(memory_space=pl.ANY)
    return pl.pallas_call(
        body,
        name="in_proj_bwd_w",
        grid=(steps,),
        in_specs=[pl.BlockSpec((bm, d), lambda i: (i, 0))]
        + [pl.BlockSpec((bm, kb), lambda i: (i, 0)) for _ in dps] + [whole(a) for a in small_grads]
        + [whole(send_out)],
        out_specs=(pl.BlockSpec((N_DEV, SHARD_IN_PAD, d), lambda i: (0, 0, 0)), hbm, hbm),
        out_shape=(jax.ShapeDtypeStruct((N_DEV, SHARD_IN_PAD, d), BF16),
                   jax.ShapeDtypeStruct((N_CHIPS,) + out_slot, BF16),
                   jax.ShapeDtypeStruct((N_CHIPS,) + small_slot, F32)),
        scratch_shapes=[pltpu.VMEM((kb * n_dp, d), F32), pltpu.VMEM((N_DEV,) + small_slot, F32)]
        + _reduce_scratch([(out_slot, BF16), (small_slot, F32)], [False, False]),
        compiler_params=_params("arbitrary"),
    )(u, *dps, *small_grads, send_out)


def _local_step(x, loss_target, u, p, meta_f, norm_g, w_in_p, q_norm_g, w_q_p, kv_norm_g, w_kv_p, conv_w_f,
                attn_out_g, conv_out_g, w_out_f, g_final):
    nb_seq, s, d = x.shape
    tp = s + LANES
    ht = tp // 2
    tables = _rope_tables(tp)

    q, k, v = _qkv_fwd(p, w_q_p, w_kv_p, q_norm_g, kv_norm_g, tables, nb_seq, tp)
    ya, o, lse = _attn_fwd(q, k, v, p, attn_out_g)
    yc = _conv_fwd(p, conv_w_f, conv_out_g, nb_seq, tp)
    dhb, d_final_g, loss_part = _out_proj_loss(ya, yc, w_out_f, x, loss_target, g_final, TOKEN_TILES)

    dcat, d_w_out = _out_proj_bwd(dhb, w_out_f, ya, yc, ht)
    send_out = d_w_out.reshape(N_DEV, SHARD_OUT, d)
    dq, dk, dv, dz_attn, d_attn_g = _attn_bwd(q, k, v, o, lse, dcat, p, attn_out_g)
    dpa, d_wq_p, d_wkv_p, d_gq, d_gkv = _qkv_bwd(p, dq, dk, dv, w_q_p, w_kv_p, q_norm_g, kv_norm_g, tables)
    d_b, d_c, d_h, dz_conv, d_conv_w, d_conv_g = _conv_bwd(p, dcat, conv_w_f, conv_out_g, nb_seq, tp)
    dps = (dpa, dz_attn, d_b, d_c, d_h, dz_conv)
    small = (d_wq_p, d_wkv_p, d_conv_w, d_final_g, d_gq, d_gkv, d_attn_g, d_conv_g, loss_part)
    send_in, r_out, r_small = _in_proj_bwd_w(u, dps, ht, small, send_out)
    grad_x, _, _, r_in, g_meta, g_norm = _input_bwd(dps, w_in_p, x, meta_f, dhb, norm_g, TOKEN_TILES, send_in)
    return grad_x, r_in, r_out, r_small, g_meta, g_norm


def kernel(x, meta_tokens, norm_g, w_in, q_norm_g, w_q_up, kv_norm_g, w_kv_up, conv_w, attn_out_g, conv_out_g, w_out, final_norm_g, loss_target, m_meta_tokens, m_norm_g, m_w_in, m_q_norm_g, m_w_q_up, m_kv_norm_g, m_w_kv_up, m_conv_w, m_attn_out_g, m_conv_out_g, m_w_out, m_final_norm_g, v_meta_tokens, v_norm_g, v_w_in, v_q_norm_g, v_w_q_up, v_kv_norm_g, v_w_kv_up, v_conv_w, v_attn_out_g, v_conv_out_g, v_w_out, v_final_norm_g):
    d = x.shape[-1]
    order = jnp.asarray(_tile_orders()[0].reshape(-1))
    u, p, w_in_p, meta_f, w_q_p, w_kv_p, w_out_f, conv_w_f = _prep_in_proj(
        x, meta_tokens, norm_g, w_in[0].T, w_q_up[0].T, w_kv_up[0], w_out[0], conv_w.transpose(1, 0, 2), order)
    g_final = final_norm_g.reshape(1, d)
    grad_x, r_in, r_out, r_small, g_meta, g_norm = _local_step(
        x, loss_target, u, p, meta_f, norm_g, w_in_p, q_norm_g, w_q_p, kv_norm_g, w_kv_p, conv_w_f,
        attn_out_g, conv_out_g, w_out_f, g_final)

    flat = lambda a: a.reshape(a.shape[-2:]) if a.ndim == 3 else a.reshape(1, -1) if a.ndim == 1 else a
    transposed = ("w_in", "w_q_up")

    def to_kernel(n, a):
        if n == "conv_w":
            return a.transpose(1, 0, 2)
        return flat(a).T if n in transposed else flat(a)

    def from_kernel(n, a, shape):
        if n == "conv_w":
            return a.transpose(1, 0, 2)
        return (a.T if n in transposed else a).reshape(shape)
    params = {
        "meta_tokens": (meta_tokens, m_meta_tokens, v_meta_tokens),
        "norm_g": (norm_g, m_norm_g, v_norm_g),
        "w_in": (w_in, m_w_in, v_w_in),
        "q_norm_g": (q_norm_g, m_q_norm_g, v_q_norm_g),
        "w_q_up": (w_q_up, m_w_q_up, v_w_q_up),
        "kv_norm_g": (kv_norm_g, m_kv_norm_g, v_kv_norm_g),
        "w_kv_up": (w_kv_up, m_w_kv_up, v_w_kv_up),
        "conv_w": (conv_w, m_conv_w, v_conv_w),
        "attn_out_g": (attn_out_g, m_attn_out_g, v_attn_out_g),
        "conv_out_g": (conv_out_g, m_conv_out_g, v_conv_out_g),
        "w_out": (w_out, m_w_out, v_w_out),
        "final_norm_g": (final_norm_g, m_final_norm_g, v_final_norm_g),
    }
    updated, loss = _reduce_update(r_in, r_out, r_small, g_meta, g_norm,
                                   {n: tuple(to_kernel(n, a) for a in t) for n, t in params.items()})
    outs = [[from_kernel(n, updated[n][i], params[n][0].shape) for n, _ in PARAM_SHAPES] for i in range(4)]
    return (loss[0, 0], grad_x, *outs[0], *outs[1], *outs[2], *outs[3])
```

```python
import functools

import jax
import jax.numpy as jnp
import numpy as np
from jax import lax
from jax.experimental import pallas as pl
from jax.experimental.pallas import tpu as pltpu

F32 = jnp.float32
BF16 = jnp.bfloat16

N_META = 16
D_MODEL = 1024
N_HEADS = 4
D_NOPE = 128
D_ROPE = 64
D_V = 128
Q_RANK = 256
KV_RANK = 128
CONV_WIDTH = 512
CONV_GROUP = 64
ROPE_THETA = 10000.0
ATTN_SCALE = (D_NOPE + D_ROPE) ** -0.5
Q_SCALE = ATTN_SCALE * 1.4426950408889634
EPS = 1e-6
NEG_INF = -1e30

ADAM_LR = 0.001
ADAM_B1 = 0.9
ADAM_B2 = 0.999
ADAM_EPS = 1e-08
ADAM_WD = 0.01
ADAM_STEP = 10

LANES = 128
PAD_FRONT = LANES - N_META
K_TILE = 256
Q_TILE = 512
N_DEV = 8
VMEM_LIMIT = 56 * 1024 * 1024

IN_PAD = 3072
GRP_A = 512
N_A = Q_RANK + KV_RANK + D_ROPE
IN_PROJ = 3008
SHARD_IN = IN_PROJ // N_DEV
SHARD_IN_PAD = 384
SHARD_Q = 96
SHARD_KV = 128
SHARD_OUT = 128
SHARD_CONV = 64
SHARD_META = 128
Q_COLS = N_HEADS * (D_NOPE + D_ROPE)
KV_COLS = N_HEADS * (D_NOPE + D_V)

ROW_Q, ROW_KV, ROW_META, ROW_CONV = 0, 256, 384, 400
ROW_REPL = 408
ROW_NORM, ROW_FINAL, ROW_GQ, ROW_GKV, ROW_ATTN, ROW_CONVG, ROW_LOSS = 408, 416, 424, 426, 427, 431, 435
SMALL_ROWS = 440

PARAM_SHAPES = (
    ("meta_tokens", (N_META, SHARD_META)), ("norm_g", (1, D_MODEL)), ("w_in", (SHARD_IN, D_MODEL)),
    ("q_norm_g", (1, Q_RANK)), ("w_q_up", (SHARD_Q, Q_RANK)), ("kv_norm_g", (1, KV_RANK)),
    ("w_kv_up", (KV_RANK, SHARD_KV)), ("conv_w", (3, 1, SHARD_CONV)), ("attn_out_g", (1, CONV_WIDTH)),
    ("conv_out_g", (1, CONV_WIDTH)), ("w_out", (SHARD_OUT, D_MODEL)), ("final_norm_g", (1, D_MODEL)),
)


def _in_pieces(k):
    lo, hi = SHARD_IN * k, SHARD_IN * (k + 1)
    out = []
    if lo < N_A:
        out.append((0, min(hi, N_A) - lo, lo))
    if hi > N_A:
        s = max(lo, N_A)
        out.append((s - lo, hi - lo, s + GRP_A - N_A))
    return out


P_TILE = 256


def _tile_pieces(j):
    lo, hi = P_TILE * j, P_TILE * (j + 1)
    out = []
    for k in range(N_DEV):
        for s, e, d in _in_pieces(k):
            a, b = max(d, lo), min(d + e - s, hi)
            if a < b:
                out.append((k, s + a - d, s + b - d, a - lo))
    return out


def _tile_orders():
    n_tiles = IN_PAD // P_TILE
    sources = [{k for k, _, _, _ in _tile_pieces(j)} for j in range(n_tiles)]
    rows, n_early, n_free = [], n_tiles, n_tiles
    for chip in range(N_CHIPS):
        own = {2 * chip, 2 * chip + 1}
        diagonal = {2 * (N_CHIPS - 1 - chip), 2 * (N_CHIPS - 1 - chip) + 1}
        early = [j for j in range(n_tiles) if sources[j] <= own]
        late = [j for j in range(n_tiles) if sources[j] & diagonal]
        mid = [j for j in range(n_tiles) if j not in early and j not in late]
        rows.append(early + mid + late)
        n_early, n_free = min(n_early, len(early)), min(n_free, len(early) + len(mid))
    return np.asarray(rows, np.int32), n_early, n_free


def _q_pieces(k):
    lo, hi = SHARD_Q * k, SHARD_Q * (k + 1)
    out = []
    for h in range(N_HEADS):
        base = (D_NOPE + D_ROPE) * h
        s, e = max(lo, base), min(hi, base + D_NOPE)
        if s < e:
            out.append((s - lo, e - lo, D_NOPE * h + s - base))
        s, e = max(lo, base + D_NOPE), min(hi, base + D_NOPE + D_ROPE)
        if s < e:
            out.append((s - lo, e - lo, N_HEADS * D_NOPE + D_ROPE * h + s - base - D_NOPE))
    return out


def _kv_dst(k):
    return D_NOPE * (k // 2) + (N_HEADS * D_NOPE if k % 2 else 0)


def _params(*sem):
    return pltpu.CompilerParams(dimension_semantics=sem, vmem_limit_bytes=VMEM_LIMIT)


def _rms_stats(x):
    r = lax.rsqrt(jnp.mean(x * x, axis=-1, keepdims=True) + EPS)
    return x * r, r


def _rms_bwd(gdy, xhat, r):
    return r * (gdy - xhat * jnp.mean(gdy * xhat, axis=-1, keepdims=True))


def _sigmoid(z):
    return 1.0 / (1.0 + jnp.exp(-z))


def _group_mean(x):
    i0 = lax.broadcasted_iota(jnp.int32, (LANES, LANES), 0) // CONV_GROUP
    i1 = lax.broadcasted_iota(jnp.int32, (LANES, LANES), 1) // CONV_GROUP
    m = jnp.where(i0 == i1, 1.0 / CONV_GROUP, 0.0).astype(BF16)
    hi = x.astype(BF16)
    lo = (x - hi.astype(F32)).astype(BF16)
    return jnp.dot(hi, m, preferred_element_type=F32) + jnp.dot(lo, m, preferred_element_type=F32)


_NT = (((1,), (1,)), ((), ()))
_TN = (((0,), (0,)), ((), ()))


def _dot(a, b, dims=None):
    if dims is None:
        return jnp.dot(a, b, preferred_element_type=F32)
    return lax.dot_general(a, b, dims, preferred_element_type=F32)


def _device_position():
    x, y, c = lax.axis_index("x"), lax.axis_index("y"), lax.axis_index("c")
    return x, y, c, 4 * x + 2 * y + c


def _gather_plan(srcs, slots, send_sems, recv_sems, local_sems):
    x, y, c, _ = _device_position()
    me, sibling = (x, y, c), (x, y, 1 - c)
    flip = lambda v, on: v + on - 2 * v * on
    near = (flip(x, 1 - c), flip(y, c))
    far = (flip(x, c), flip(y, 1 - c))
    diag = (1 - x, 1 - y)
    n = len(srcs)

    def slot(a, px, py, pc):
        return slots[a].at[4 * px + 2 * py + pc]

    def copy(a, k, block, to, own=False):
        return pltpu.make_async_remote_copy(
            src_ref=srcs[a] if own else slot(a, *block),
            dst_ref=slot(a, *block),
            send_sem=send_sems.at[7 * a + k],
            recv_sem=recv_sems.at[7 * a + k],
            device_id=to,
            device_id_type=pl.DeviceIdType.MESH,
        )

    def local(a):
        return pltpu.make_async_copy(srcs[a], slot(a, *me), local_sems.at[a])

    sent = [(me, sibling), (me, (*near, c)), (me, (*far, c)), ((*near, c), (*far, c)),
            ((*near, c), sibling), ((*far, c), sibling), ((*diag, c), sibling)]
    landed = [sibling, (*near, c), (*far, c), (*diag, c), (*far, 1 - c), (*near, 1 - c), (*diag, 1 - c)]

    def send(a, k):
        return copy(a, k, *sent[k], own=k < 3)

    def arrival(a, k):
        return copy(a, k, landed[k], me)

    def start():
        for a in range(n):
            local(a).start()
            for k in range(3):
                send(a, k).start()

    def own():
        for a in range(n):
            local(a).wait()
            arrival(a, 0).wait_recv()

    def mid():
        for a in range(n):
            arrival(a, 1).wait_recv()
            send(a, 3).start()
            send(a, 4).start()
        for a in range(n):
            arrival(a, 2).wait_recv()
            send(a, 5).start()
        for a in range(n):
            for k in (4, 5):
                arrival(a, k).wait_recv()

    def late():
        for a in range(n):
            arrival(a, 3).wait_recv()
            send(a, 6).start()
        for a in range(n):
            arrival(a, 6).wait_recv()

    def finish():
        for a in range(n):
            for k in range(7):
                send(a, k).wait_send()

    return start, own, mid, late, finish


def _adam_update(g, w, m, v):
    m_new = ADAM_B1 * m + (1.0 - ADAM_B1) * g
    v_new = ADAM_B2 * v + (1.0 - ADAM_B2) * (g * g)
    m_hat = m_new / (1.0 - ADAM_B1 ** ADAM_STEP)
    v_hat = v_new / (1.0 - ADAM_B2 ** ADAM_STEP)
    return -ADAM_LR * (m_hat / (jnp.sqrt(v_hat) + ADAM_EPS) + ADAM_WD * w), m_new, v_new


N_CHIPS = 4


def _reduce_plan(pays, owns, r1s, sums, r2s, send1, recv1, send2, recv2, local_sems):
    x, y, c, _ = _device_position()
    sibling = (x, y, 1 - c)
    chips = [((1 - x if rj & 2 else x), (1 - y if rj & 1 else y)) for rj in range(N_CHIPS)]
    n = len(pays)

    def slot_of(rj, core):
        return 4 * chips[rj][0] + 2 * chips[rj][1] + core

    def to_sibling(a, rj):
        return pltpu.make_async_remote_copy(
            src_ref=pays[a].at[slot_of(rj, 1 - c)], dst_ref=r1s[a].at[rj],
            send_sem=send1.at[N_CHIPS * a + rj], recv_sem=recv1.at[N_CHIPS * a + rj],
            device_id=sibling, device_id_type=pl.DeviceIdType.MESH)

    def load_own(a, rj):
        return pltpu.make_async_copy(pays[a].at[slot_of(rj, c)], owns[a].at[rj], local_sems.at[2 * N_CHIPS * a + rj])

    def to_chip(a, rj):
        return pltpu.make_async_remote_copy(
            src_ref=sums[a].at[rj], dst_ref=r2s[a].at[rj],
            send_sem=send2.at[N_CHIPS * a + rj], recv_sem=recv2.at[N_CHIPS * a + rj],
            device_id=(*chips[rj], c), device_id_type=pl.DeviceIdType.MESH)

    def keep(a):
        return pltpu.make_async_copy(sums[a].at[0], r2s[a].at[0], local_sems.at[2 * N_CHIPS * a + N_CHIPS])

    def start():
        for a in range(n):
            for rj in range(N_CHIPS):
                to_sibling(a, rj).start()
                if owns[a] is not None:
                    load_own(a, rj).start()

    def combine():
        for a in range(n):
            for rj in range(N_CHIPS):
                to_sibling(a, rj).wait_recv()
                if owns[a] is not None:
                    load_own(a, rj).wait()
                    mine = owns[a][rj]
                else:
                    mine = pays[a][slot_of(rj, c)]
                sums[a][rj] = (mine.astype(F32) + r1s[a][rj].astype(F32)).astype(sums[a].dtype)
            keep(a).start()
            for rj in range(1, N_CHIPS):
                to_chip(a, rj).start()

    def finish():
        for a in range(n):
            for rj in range(1, N_CHIPS):
                to_chip(a, rj).wait_recv()
            for rj in range(N_CHIPS):
                to_sibling(a, rj).wait_send()
            for rj in range(1, N_CHIPS):
                to_chip(a, rj).wait_send()
            keep(a).wait()

    return start, combine, finish


def _reduce_scratch(shapes_dtypes, own_flags):
    out = []
    for (shape, dtype), own in zip(shapes_dtypes, own_flags):
        if own:
            out.append(pltpu.VMEM((N_CHIPS,) + shape, dtype))
        out += [pltpu.VMEM((N_CHIPS,) + shape, dtype), pltpu.VMEM((N_CHIPS,) + shape, dtype)]
    n = len(shapes_dtypes)
    out += [pltpu.SemaphoreType.DMA((N_CHIPS * n,))] * 4 + [pltpu.SemaphoreType.DMA((2 * N_CHIPS * n,))]
    return out


def _pack_small(ssmall, dwq, dwkv, dconv, dfinal, dgq, dgkv, dattn, dconvg, loss_part):
    ssmall[...] = jnp.zeros_like(ssmall)
    rep = ssmall.at[0]
    for i in range(D_MODEL // LANES):
        rep[ROW_FINAL + i:ROW_FINAL + i + 1, :] = dfinal[:, LANES * i:LANES * (i + 1)]
    for i in range(Q_RANK // LANES):
        rep[ROW_GQ + i:ROW_GQ + i + 1, :] = dgq[:, LANES * i:LANES * (i + 1)]
    rep[ROW_GKV:ROW_GKV + 1, :] = dgkv[...]
    for i in range(CONV_WIDTH // LANES):
        rep[ROW_ATTN + i:ROW_ATTN + i + 1, :] = dattn[:, LANES * i:LANES * (i + 1)]
        rep[ROW_CONVG + i:ROW_CONVG + i + 1, :] = dconvg[:, LANES * i:LANES * (i + 1)]
    rep[ROW_LOSS:ROW_LOSS + 1, :] = loss_part[...]
    for k in range(N_DEV):
        if k:
            ssmall[k, ROW_REPL:, :] = ssmall[0, ROW_REPL:, :]
        for s, e, d in _q_pieces(k):
            for i in range(Q_RANK // LANES):
                ssmall[k, ROW_Q + SHARD_Q * i + s:ROW_Q + SHARD_Q * i + e, :] = dwq[d:d + e - s, LANES * i:LANES * (i + 1)]
        ssmall[k, ROW_KV:ROW_KV + KV_RANK, :] = dwkv[:, _kv_dst(k):_kv_dst(k) + SHARD_KV]
        ssmall[k, ROW_CONV:ROW_CONV + 3, 0:SHARD_CONV] = dconv[0:3, SHARD_CONV * k:SHARD_CONV * (k + 1)]


TOKEN_TILES = 4
TAIL_ROWS = N_META + D_MODEL // LANES


def _reduce_update(r_in, r_out, r_small, g_meta, g_norm, params):
    n_p = len(PARAM_SHAPES)
    names = [n for n, _ in PARAM_SHAPES]

    def body(*refs):
        rin, rout, rsmall, gmeta, gnorm = refs[:5]
        par = refs[5:5 + 3 * n_p]
        o = 5 + 3 * n_p
        g_out = {n: refs[o + i] for i, n in enumerate(names)}
        loss_out = refs[o + n_p]
        upd = refs[o + n_p + 1:o + 4 * n_p + 1]
        gsum = refs[o + 4 * n_p + 1]
        x, y, c, me = _device_position()
        my_chip = 2 * x + y

        g = rin[my_chip].astype(F32)
        for ch in range(1, N_CHIPS):
            g = g + rin[ch ^ my_chip].astype(F32)
        g_out["w_in"][...] = g[:SHARD_IN, :]

        g = rout[my_chip].astype(F32)
        gs = rsmall[my_chip]
        for ch in range(1, N_CHIPS):
            g = g + rout[ch ^ my_chip].astype(F32)
            gs = gs + rsmall[ch ^ my_chip]
        g_out["w_out"][...] = g
        gsum[...] = gs
        for i in range(Q_RANK // LANES):
            g_out["w_q_up"][:, LANES * i:LANES * (i + 1)] = gsum[ROW_Q + SHARD_Q * i:ROW_Q + SHARD_Q * (i + 1), :]
        g_out["w_kv_up"][...] = gsum[ROW_KV:ROW_KV + KV_RANK, :]
        for i in range(3):
            g_out["conv_w"][i] = gsum[ROW_CONV + i:ROW_CONV + i + 1, 0:SHARD_CONV]
        for name, row, width in (("final_norm_g", ROW_FINAL, D_MODEL), ("q_norm_g", ROW_GQ, Q_RANK),
                                 ("kv_norm_g", ROW_GKV, KV_RANK), ("attn_out_g", ROW_ATTN, CONV_WIDTH),
                                 ("conv_out_g", ROW_CONVG, CONV_WIDTH)):
            for i in range(width // LANES):
                g_out[name][:, LANES * i:LANES * (i + 1)] = gsum[row + i:row + i + 1, :]
        loss_out[...] = gsum[ROW_LOSS:ROW_LOSS + 1, :]
        g_out["meta_tokens"][...] = gmeta[...]
        g_out["norm_g"][...] = gnorm[...]

        for i, n in enumerate(names):
            w, m, v = (par[3 * i + j][...] for j in range(3))
            for j, val in enumerate(_adam_update(g_out[n][...], w, m, v)):
                upd[3 * i + j][...] = val

    vm = pl.BlockSpec(memory_space=pltpu.VMEM)
    out_shape = [jax.ShapeDtypeStruct(shape, F32) for _, shape in PARAM_SHAPES]
    out_shape.append(jax.ShapeDtypeStruct((1, LANES), F32))
    for _, shape in PARAM_SHAPES:
        out_shape += [jax.ShapeDtypeStruct(shape, F32)] * 3
    outs = pl.pallas_call(
        body,
        name="reduce_update",
        out_shape=tuple(out_shape),
        in_specs=[vm] * (5 + 3 * n_p),
        out_specs=(vm,) * len(out_shape),
        scratch_shapes=[pltpu.VMEM((SMALL_ROWS, LANES), F32)],
        compiler_params=pltpu.CompilerParams(vmem_limit_bytes=VMEM_LIMIT),
    )(r_in, r_out, r_small, g_meta, g_norm, *[a for n in names for a in params[n]])
    return {n: (outs[i], *outs[n_p + 1 + 3 * i:n_p + 4 + 3 * i]) for i, n in enumerate(names)}, outs[n_p]


def _prep_in_proj(x, meta, norm_g, w_in_t, w_q, w_kv, w_out, conv_w, order):
    nb_seq, s, d = x.shape
    tp = s + LANES
    m = nb_seq * tp
    ts = s // TOKEN_TILES
    n_real = nb_seq * TOKEN_TILES
    n_norm = n_real + 1
    n_tiles = IN_PAD // P_TILE
    _, n_early, n_free = _tile_orders()
    steps = n_norm + n_tiles
    dot_rows = m // 4
    qkv_shape = (SHARD_Q + KV_RANK, Q_RANK)

    def tile(t):
        t = jnp.minimum(t, n_real - 1)
        return t // TOKEN_TILES, t % TOKEN_TILES

    def column_tile(t, order_ref):
        chip = 2 * lax.axis_index("x") + lax.axis_index("y")
        return order_ref[chip * n_tiles + jnp.maximum(t - n_norm, 0)]

    def body(order_ref, x_ref, meta_ref, g_ref, win_ref, wq_ref, wkv_ref, wout_ref, conv_ref,
             u_hbm, p_ref, w_in_p, meta_f, w_q_p, w_kv_p, w_out_f, conv_f,
             u_all, sbig, gbig, smeta, gmeta, sqkv, sout, sconv, gqkv, gout, gconv,
             send_in, recv_in, local_in, send_meta, recv_meta, send_rest, recv_rest, local_rest, u_sem):
        t = pl.program_id(0)
        px, py, pc, me = _device_position()
        u_copy = pltpu.make_async_copy(u_all, u_hbm, u_sem)

        def plan_in():
            return _gather_plan((sbig,), (gbig,), send_in, recv_in, local_in)

        def plan_rest():
            return _gather_plan((sqkv, sout, sconv), (gqkv, gout, gconv), send_rest, recv_rest, local_rest)

        def meta_copies():
            out = []
            for r in range(1, N_DEV):
                peer = (1 - px if r & 4 else px, 1 - py if r & 2 else py, 1 - pc if r & 1 else pc)
                out.append(pltpu.make_async_remote_copy(
                    src_ref=smeta,
                    dst_ref=gmeta.at[r],
                    send_sem=send_meta.at[r - 1],
                    recv_sem=recv_meta.at[r - 1],
                    device_id=peer,
                    device_id_type=pl.DeviceIdType.MESH,
                ))
            return out

        @pl.when(t == 0)
        def _():
            sbig[0:SHARD_IN, :] = win_ref[...].astype(BF16)
            sbig[SHARD_IN:, :] = jnp.zeros((SHARD_IN_PAD - SHARD_IN, d), BF16)
            smeta[...] = meta_ref[...]
            plan_in()[0]()
            for cp in meta_copies():
                cp.start()
            sqkv[...] = jnp.zeros_like(sqkv)
            sqkv[0:SHARD_Q, :] = wq_ref[...].astype(BF16)
            sqkv[SHARD_Q:, 0:SHARD_KV] = wkv_ref[...].astype(BF16)
            sout[...] = wout_ref[...].astype(BF16)
            sconv[...] = jnp.zeros_like(sconv)
            for i in range(3):
                sconv[i:i + 1, 0:SHARD_CONV] = conv_ref[i]

        def norm(h):
            hhat, _ = _rms_stats(h)
            return (hhat * g_ref[...]).astype(BF16)

        @pl.when(t < n_real)
        def _():
            b, k = tile(t)
            row0 = pl.multiple_of(b * tp + LANES + k * ts, 16)
            u_all[pl.ds(row0, ts), :] = norm(x_ref[0])

        @pl.when(t == n_real)
        def _():
            for cp in meta_copies():
                cp.wait_recv()
            gmeta[0] = smeta[...]
            for k in range(N_DEV):
                meta_f[:, SHARD_META * k:SHARD_META * (k + 1)] = gmeta[k ^ me]
            um = norm(meta_f[...])
            for b in range(nb_seq):
                u_all[b * tp:b * tp + PAD_FRONT, :] = jnp.zeros((PAD_FRONT, d), BF16)
                u_all[b * tp + PAD_FRONT:b * tp + LANES, :] = um
            u_copy.start()

        @pl.when(t == n_norm)
        def _():
            plan_in()[1]()

        @pl.when(t == n_norm + n_early)
        def _():
            plan_in()[2]()
            plan_rest()[0]()

        @pl.when(t == n_norm + (n_early + n_free) // 2)
        def _():
            plan_rest()[1]()
            plan_rest()[2]()

        @pl.when(t == n_norm + n_free)
        def _():
            plan_in()[3]()

        @pl.when(t == n_norm + n_free + 2)
        def _():
            plan_rest()[3]()

        @pl.when(t >= n_norm)
        def _():
            j = column_tile(t, order_ref)
            for jj in range(n_tiles):
                @pl.when(j == jj)
                def _(jj=jj):
                    if jj == N_A // P_TILE:
                        w_in_p[N_A % P_TILE:, :] = jnp.zeros((P_TILE - N_A % P_TILE, d), BF16)
                    for k, s0, e0, d0 in _tile_pieces(jj):
                        w_in_p[d0:d0 + e0 - s0, :] = gbig[k, s0:e0, :]
            for r in range(m // dot_rows):
                rows = slice(dot_rows * r, dot_rows * (r + 1))
                p_ref[rows, :] = _dot(u_all[rows, :], w_in_p[...], _NT).astype(BF16)

        @pl.when(t == steps - 1)
        def _():
            plan_in()[4]()
            plan_rest()[4]()
            for cp in meta_copies():
                cp.wait_send()
            u_copy.wait()
            conv_f[...] = jnp.zeros_like(conv_f)
            for k in range(N_DEV):
                for s0, e0, d0 in _q_pieces(k):
                    w_q_p[d0:d0 + e0 - s0, :] = gqkv[k, s0:e0, :]
                w_kv_p[:, _kv_dst(k):_kv_dst(k) + SHARD_KV] = gqkv[k, SHARD_Q:, 0:SHARD_KV]
                w_out_f[SHARD_OUT * k:SHARD_OUT * (k + 1), :] = gout[k]
                conv_f[0:3, SHARD_CONV * k:SHARD_CONV * (k + 1)] = gconv[k, 0:3, 0:SHARD_CONV]

    whole = lambda shape: pl.BlockSpec(shape, lambda t, o: (0,) * len(shape))
    return pl.pallas_call(
        body,
        name="prep_in_proj_gather",
        grid_spec=pltpu.PrefetchScalarGridSpec(
            num_scalar_prefetch=1,
            grid=(steps,),
            in_specs=[
                pl.BlockSpec((1, ts, d), lambda t, o: (*tile(t), 0)),
                whole(meta.shape), whole(norm_g.shape), whole(w_in_t.shape),
                whole(w_q.shape), whole(w_kv.shape), whole(w_out.shape), whole(conv_w.shape),
            ],
            out_specs=(pl.BlockSpec(memory_space=pl.ANY),
                       pl.BlockSpec((m, P_TILE), lambda t, o: (0, column_tile(t, o))),
                       pl.BlockSpec((P_TILE, d), lambda t, o: (column_tile(t, o), 0)),
                       whole((N_META, d)),
                       whole((Q_COLS, Q_RANK)), whole((KV_RANK, KV_COLS)), whole((D_MODEL, D_MODEL)),
                       whole((8, CONV_WIDTH))),
            scratch_shapes=[
                pltpu.VMEM((m, d), BF16),
                pltpu.VMEM((SHARD_IN_PAD, d), BF16),
                pltpu.VMEM((N_DEV, SHARD_IN_PAD, d), BF16),
                pltpu.VMEM((N_META, SHARD_META), F32),
                pltpu.VMEM((N_DEV, N_META, SHARD_META), F32),
                pltpu.VMEM(qkv_shape, BF16),
                pltpu.VMEM((SHARD_OUT, D_MODEL), BF16),
                pltpu.VMEM((8, LANES), F32),
                pltpu.VMEM((N_DEV,) + qkv_shape, BF16),
                pltpu.VMEM((N_DEV, SHARD_OUT, D_MODEL), BF16),
                pltpu.VMEM((N_DEV, 8, LANES), F32),
                pltpu.SemaphoreType.DMA((7,)),
                pltpu.SemaphoreType.DMA((7,)),
                pltpu.SemaphoreType.DMA((1,)),
                pltpu.SemaphoreType.DMA((N_DEV - 1,)),
                pltpu.SemaphoreType.DMA((N_DEV - 1,)),
                pltpu.SemaphoreType.DMA((21,)),
                pltpu.SemaphoreType.DMA((21,)),
                pltpu.SemaphoreType.DMA((3,)),
                pltpu.SemaphoreType.DMA,
            ],
        ),
        out_shape=(jax.ShapeDtypeStruct((m, d), BF16),
                   jax.ShapeDtypeStruct((m, IN_PAD), BF16),
                   jax.ShapeDtypeStruct((IN_PAD, d), BF16),
                   jax.ShapeDtypeStruct((N_META, d), F32),
                   jax.ShapeDtypeStruct((Q_COLS, Q_RANK), BF16),
                   jax.ShapeDtypeStruct((KV_RANK, KV_COLS), BF16),
                   jax.ShapeDtypeStruct((D_MODEL, D_MODEL), BF16),
                   jax.ShapeDtypeStruct((8, CONV_WIDTH), F32)),
        compiler_params=_params("arbitrary"),
    )(order, x, meta, norm_g, w_in_t, w_q, w_kv, w_out, conv_w)


def _rope_tables(tp):
    half = D_ROPE // 2
    inv_freq = (1.0 / (ROPE_THETA ** (np.arange(half, dtype=np.float32) / half))).astype(np.float32)
    pos = (np.arange(tp) - PAD_FRONT).astype(np.float32)
    ang = pos[:, None] * inv_freq[None, :]
    cos = np.tile(np.cos(ang), (1, LANES // half))
    sin = np.tile(np.sin(ang), (1, LANES // half))
    first = (np.arange(LANES) % D_ROPE) < half
    zero = np.float32(0.0)
    return tuple(jnp.asarray(t, F32) for t in (cos, np.where(first, -sin, zero), np.where(first, zero, sin)))


def _rope(t, cos, sa, sb):
    return t * cos + pltpu.roll(t, LANES - D_ROPE // 2, 1) * sa + pltpu.roll(t, D_ROPE // 2, 1) * sb


def _rope_t(t, cos, sa, sb):
    return t * cos + pltpu.roll(t * sa, D_ROPE // 2, 1) + pltpu.roll(t * sb, LANES - D_ROPE // 2, 1)


def _qkv_fwd(p, wq, wkv, gq, gkv, tables, nb_seq, tp):
    ht = tp // 2

    def body(pa_ref, wq_ref, wkv_ref, gq_ref, gkv_ref, cos_ref, sa_ref, sb_ref, q_ref, k_ref, v_ref):
        pa = pa_ref[...].astype(F32)
        cq_hat, _ = _rms_stats(pa[:, :Q_RANK])
        ckv_hat, _ = _rms_stats(pa[:, Q_RANK:Q_RANK + KV_RANK])
        q = _dot((cq_hat * gq_ref[...]).astype(BF16), wq_ref[...], _NT) * Q_SCALE
        kv = _dot((ckv_hat * gkv_ref[...]).astype(BF16), wkv_ref[...])
        tabs = (cos_ref[...], sa_ref[...], sb_ref[...])
        lane = lax.broadcasted_iota(jnp.int32, (ht, LANES), 1)
        low = lane < D_ROPE
        mark = lane == D_ROPE
        row = (pl.program_id(0) % 2) * ht + lax.broadcasted_iota(jnp.int32, (ht, LANES), 0)
        k_pe = jnp.where(mark & (row < PAD_FRONT), NEG_INF, _rope(pa[:, Q_RANK + KV_RANK:], *tabs))
        one = jnp.where(mark & (row >= PAD_FRONT), 1.0, 0.0)
        pairs = [_rope(q[:, N_HEADS * D_NOPE + LANES * i:N_HEADS * D_NOPE + LANES * (i + 1)], *tabs) for i in range(2)]
        for h in range(N_HEADS):
            pair = pairs[h // 2]
            if h % 2:
                pair = pltpu.roll(pair, D_ROPE, 1)
            pe = jnp.where(low, pair, one)
            q_ref[0, h] = jnp.concatenate([q[:, D_NOPE * h:D_NOPE * (h + 1)], pe], axis=1).astype(BF16)
            k_ref[0, h] = jnp.concatenate([kv[:, D_NOPE * h:D_NOPE * (h + 1)], k_pe], axis=1).astype(BF16)
            v_ref[0, h] = kv[:, N_HEADS * D_NOPE + D_V * h:N_HEADS * D_NOPE + D_V * (h + 1)].astype(BF16)

    full = lambda a: pl.BlockSpec(a.shape, lambda i: (0,) * a.ndim)
    tab = pl.BlockSpec((ht, LANES), lambda i: (i % 2, 0))
    qk = pl.BlockSpec((1, N_HEADS, ht, 2 * LANES), lambda i: (i // 2, 0, i % 2, 0))
    return pl.pallas_call(
        body,
        name="qkv_fwd",
        grid=(2 * nb_seq,),
        in_specs=[pl.BlockSpec((ht, GRP_A), lambda i: (i, 0)), full(wq), full(wkv), full(gq), full(gkv), tab, tab, tab],
        out_specs=(qk, qk, pl.BlockSpec((1, N_HEADS, ht, D_V), lambda i: (i // 2, 0, i % 2, 0))),
        out_shape=(
            jax.ShapeDtypeStruct((nb_seq, N_HEADS, tp, 2 * LANES), BF16),
            jax.ShapeDtypeStruct((nb_seq, N_HEADS, tp, 2 * LANES), BF16),
            jax.ShapeDtypeStruct((nb_seq, N_HEADS, tp, D_V), BF16),
        ),
        compiler_params=_params("parallel"),
    )(p, wq, wkv, gq, gkv, *tables)


def _attn_fwd(q, k, v, p, g_attn):
    nb_seq, _, tp, _ = q.shape

    def body(q_ref, k_ref, v_ref, z_ref, g_ref, y_ref, o_ref, lse_ref):
        g = g_ref[...]
        for r0 in range(0, tp, Q_TILE):
            nq = min(Q_TILE, tp - r0)
            kend = r0 + nq
            qq = q_ref[0, 0, r0:kend, :]
            sd = _dot(qq, k_ref[0, 0, r0:kend, :], _NT)
            causal = (lax.broadcasted_iota(jnp.int32, (nq, nq), 1) <= lax.broadcasted_iota(jnp.int32, (nq, nq), 0))
            sd = jnp.where(causal, sd, NEG_INF)
            m = jnp.max(sd, axis=-1, keepdims=True)
            if r0:
                so = _dot(qq, k_ref[0, 0, 0:r0, :], _NT)
                m = jnp.maximum(m, jnp.max(so, axis=-1, keepdims=True))
            ed = jnp.exp2(sd - m)
            l = jnp.sum(ed, axis=-1, keepdims=True)
            o = _dot(ed.astype(BF16), v_ref[0, 0, r0:kend, :])
            if r0:
                eo = jnp.exp2(so - m)
                l = l + jnp.sum(eo, axis=-1, keepdims=True)
                o = o + _dot(eo.astype(BF16), v_ref[0, 0, 0:r0, :])
            o = o * (1.0 / l)
            o_ref[0, 0, r0:kend, :] = o
            lse_ref[0, 0, r0:kend, :] = jnp.broadcast_to(m + jnp.log2(l), (nq, LANES))
            ohat, _ = _rms_stats(o)
            z = z_ref[r0:kend, :].astype(F32)
            y_ref[r0:kend, :] = (ohat * g * (z * _sigmoid(z))).astype(BF16)

    qk = pl.BlockSpec((1, 1, tp, 2 * LANES), lambda b, h: (b, h, 0, 0))
    hv = pl.BlockSpec((1, 1, tp, D_V), lambda b, h: (b, h, 0, 0))
    return pl.pallas_call(
        body,
        name="attn_fwd",
        grid=(nb_seq, N_HEADS),
        in_specs=[qk, qk, hv,
                  pl.BlockSpec((tp, LANES), lambda b, h: (b, GRP_A // LANES + h)),
                  pl.BlockSpec((1, LANES), lambda b, h: (0, h))],
        out_specs=(pl.BlockSpec((tp, LANES), lambda b, h: (b, h)), hv, hv),
        out_shape=(
            jax.ShapeDtypeStruct((nb_seq * tp, N_HEADS * D_V), BF16),
            jax.ShapeDtypeStruct((nb_seq, N_HEADS, tp, D_V), F32),
            jax.ShapeDtypeStruct((nb_seq, N_HEADS, tp, LANES), F32),
        ),
        compiler_params=_params("parallel", "parallel"),
    )(q, k, v, p, g_attn)


_CONV_COL0 = (GRP_A + N_HEADS * D_V) // LANES


def _conv_specs(tp, order):
    cols = CONV_WIDTH // LANES
    return [pl.BlockSpec((tp, LANES), functools.partial(
        lambda a, b, off: order(a, b, off), off=_CONV_COL0 + i * cols)) for i in range(4)]


def _conv_fwd(p, conv_w, g_conv, nb_seq, tp):
    def body(b_ref, c_ref, h_ref, z_ref, w_ref, g_ref, y_ref):
        cc = c_ref[...].astype(F32) * h_ref[...].astype(F32)
        row = lax.broadcasted_iota(jnp.int32, (tp, LANES), 0)
        s1 = jnp.where(row >= 1, pltpu.roll(cc, 1, 0), 0.0)
        s2 = jnp.where(row >= 2, pltpu.roll(cc, 2, 0), 0.0)
        yc = b_ref[...].astype(F32) * (w_ref[0:1, :] * s2 + w_ref[1:2, :] * s1 + w_ref[2:3, :] * cc)
        r = lax.rsqrt(_group_mean(yc * yc) + EPS)
        z = z_ref[...].astype(F32)
        y_ref[...] = (yc * r * g_ref[...] * (z * _sigmoid(z))).astype(BF16)

    return pl.pallas_call(
        body,
        name="conv_fwd",
        grid=(nb_seq, CONV_WIDTH // LANES),
        in_specs=_conv_specs(tp, lambda b, t, off: (b, off + t)) + [
            pl.BlockSpec((8, LANES), lambda b, t: (0, t)),
            pl.BlockSpec((1, LANES), lambda b, t: (0, t))],
        out_specs=pl.BlockSpec((tp, LANES), lambda b, t: (b, t)),
        out_shape=jax.ShapeDtypeStruct((nb_seq * tp, CONV_WIDTH), BF16),
        compiler_params=_params("parallel", "parallel"),
    )(p, p, p, p, conv_w, g_conv)


def _token_copy(hbm, b, k, ts, buf, sem, to_hbm=False):
    lo, hi = max(k * ts - LANES, 0), (k + 1) * ts - LANES
    off = lo - (k * ts - LANES)
    src, dst = hbm.at[b, pl.ds(lo, hi - lo)], buf.at[pl.ds(off, hi - lo)]
    if to_hbm:
        src, dst = dst, src
    return pltpu.make_async_copy(src, dst, sem)


def _for_tile(k, nt, fn):
    for kk in range(nt):
        @pl.when(k == kk)
        def _(kk=kk):
            fn(kk)


def _out_proj_loss(ya, yc, w_out, x, target, g_final, nt):
    nb_seq, s, d = x.shape
    r, ka = ya.shape
    ts = (s + LANES) // nt
    steps = nb_seq * nt

    def body(a_ref, c_ref, w_ref, x_hbm, t_hbm, g_ref, dhb_ref, dg_ref, loss_ref,
             xbuf, tbuf, acc_ref, sems):
        i = pl.program_id(0)
        b, k = i // nt, i % nt

        @pl.when(i == 0)
        def _():
            acc_ref[...] = jnp.zeros_like(acc_ref)
            dg_ref[...] = jnp.zeros_like(dg_ref)

        slot = i % 2

        def fetch(seq, kk, sl):
            return [_token_copy(x_hbm, seq, kk, ts, xbuf.at[sl], sems.at[sl, 0]),
                    _token_copy(t_hbm, seq, kk, ts, tbuf.at[sl], sems.at[sl, 1])]

        def start(seq, sl, kk):
            if kk == 0:
                xbuf[sl, 0:LANES, :] = jnp.zeros((LANES, d), F32)
                tbuf[sl, 0:LANES, :] = jnp.zeros((LANES, d), F32)
            for cp in fetch(seq, kk, sl):
                cp.start()

        @pl.when(i == 0)
        def _():
            start(0, 0, 0)

        @pl.when(i + 1 < steps)
        def _():
            _for_tile((i + 1) % nt, nt, functools.partial(start, (i + 1) // nt, 1 - slot))

        mix = _dot(a_ref[...], w_ref[0:ka, :]) + _dot(c_ref[...], w_ref[ka:, :])
        _for_tile(k, nt, lambda kk: [cp.wait() for cp in fetch(b, kk, slot)])

        real = (lax.broadcasted_iota(jnp.int32, (ts, d), 0) >= LANES) | (k > 0)
        g = g_ref[...]
        hhat, rstd = _rms_stats(xbuf[slot] + mix)
        e = jnp.where(real, hhat * g - tbuf[slot], 0.0)
        acc_ref[...] += jnp.sum(e * e, axis=0, keepdims=True)
        dy = e * (1.0 / d)
        dg_ref[...] += jnp.sum(dy * hhat, axis=0, keepdims=True)
        dhb_ref[...] = _rms_bwd(g * dy, hhat, rstd).astype(BF16)

        @pl.when(i == steps - 1)
        def _():
            total = jnp.sum(acc_ref[...], axis=1, keepdims=True)
            loss_ref[...] = jnp.broadcast_to((0.5 / d) * total, loss_ref.shape)

    hbm = pl.BlockSpec(memory_space=pl.ANY)
    row = pl.BlockSpec((ts, d), lambda i: (i, 0))
    vec = pl.BlockSpec((1, d), lambda i: (0, 0))
    return pl.pallas_call(
        body,
        name="out_proj_loss",
        grid=(steps,),
        in_specs=[pl.BlockSpec((ts, ka), lambda i: (i, 0)), pl.BlockSpec((ts, yc.shape[1]), lambda i: (i, 0)),
                  pl.BlockSpec(w_out.shape, lambda i: (0, 0)), hbm, hbm, vec],
        out_specs=(row, vec, pl.BlockSpec((1, LANES), lambda i: (0, 0))),
        out_shape=(
            jax.ShapeDtypeStruct((r, d), BF16),
            jax.ShapeDtypeStruct((1, d), F32),
            jax.ShapeDtypeStruct((1, LANES), F32),
        ),
        scratch_shapes=[pltpu.VMEM((2, ts, d), F32), pltpu.VMEM((2, ts, d), F32), pltpu.VMEM((1, d), F32),
                        pltpu.SemaphoreType.DMA((2, 2))],
        compiler_params=_params("arbitrary"),
    )(ya, yc, w_out, x, target, g_final)


def _out_proj_bwd(dhb, w_out, ya, yc, bm):
    r, d = dhb.shape
    ka = ya.shape[1]
    n_mix = w_out.shape[0]
    last = r // bm - 1

    def body(dh_ref, w_ref, a_ref, c_ref, dcat_ref, dw_ref, acc_ref):
        @pl.when(pl.program_id(0) == 0)
        def _():
            acc_ref[...] = jnp.zeros_like(acc_ref)

        dh = dh_ref[...]
        dcat_ref[...] = _dot(dh, w_ref[...], _NT).astype(BF16)
        acc_ref[0:ka, :] += _dot(a_ref[...], dh, _TN)
        acc_ref[ka:, :] += _dot(c_ref[...], dh, _TN)

        @pl.when(pl.program_id(0) == last)
        def _():
            dw_ref[...] = acc_ref[...].astype(BF16)

    return pl.pallas_call(
        body,
        name="out_proj_bwd",
        grid=(r // bm,),
        in_specs=[pl.BlockSpec((bm, d), lambda i: (i, 0)), pl.BlockSpec(w_out.shape, lambda i: (0, 0)),
                  pl.BlockSpec((bm, ka), lambda i: (i, 0)), pl.BlockSpec((bm, yc.shape[1]), lambda i: (i, 0))],
        out_specs=(pl.BlockSpec((bm, n_mix), lambda i: (i, 0)),
                   pl.BlockSpec((n_mix, d), lambda i: (0, 0))),
        out_shape=(jax.ShapeDtypeStruct((r, n_mix), BF16),
                   jax.ShapeDtypeStruct((n_mix, d), BF16)),
        scratch_shapes=[pltpu.VMEM((n_mix, d), F32)],
        compiler_params=_params("arbitrary"),
    )(dhb, w_out, ya, yc)


def _attn_bwd(q, k, v, o, lse, dcat, p, g_attn):
    nb_seq, _, tp, _ = q.shape

    def body(q_ref, k_ref, v_ref, o_ref, lse_ref, dy_ref, z_ref, g_ref,
             dq_ref, dk_ref, dv_ref, dz_ref, dg_ref, dq_acc):
        @pl.when(pl.program_id(1) == 0)
        def _():
            dg_ref[...] = jnp.zeros_like(dg_ref)

        g = g_ref[...]
        z = z_ref[...].astype(F32)
        o = o_ref[0, 0]
        dy = dy_ref[...].astype(F32)
        sig = _sigmoid(z)
        ohat, r = _rms_stats(o)
        don = dy * (z * sig)
        dz_ref[...] = (dy * (ohat * g) * (sig * (1.0 + z * (1.0 - sig)))).astype(BF16)
        dg_ref[...] += jnp.sum(don * ohat, axis=0, keepdims=True)
        do = _rms_bwd(g * don, ohat, r)
        dvec = jnp.sum(do * o, axis=-1, keepdims=True)
        dob = do.astype(BF16)
        lse_col = lse_ref[0, 0, :, 0:1]
        dq_acc[...] = jnp.zeros_like(dq_acc)
        for k0 in range(0, tp, K_TILE):
            nk = min(K_TILE, tp - k0)
            nq = tp - k0
            qq = q_ref[0, 0, k0:, :]
            kk = k_ref[0, 0, k0:k0 + nk, :]
            causal = (lax.broadcasted_iota(jnp.int32, (nq, nk), 1) <= lax.broadcasted_iota(jnp.int32, (nq, nk), 0))
            pr = jnp.where(causal, jnp.exp2(_dot(qq, kk, _NT) - lse_col[k0:]), 0.0)
            dp = _dot(dob[k0:], v_ref[0, 0, k0:k0 + nk, :], _NT)
            ds = (pr * (dp - dvec[k0:])).astype(BF16)
            dv_ref[0, 0, k0:k0 + nk, :] = _dot(pr.astype(BF16), dob[k0:], _TN).astype(BF16)
            dk_ref[0, 0, k0:k0 + nk, :] = (_dot(ds, qq, _TN) * (ATTN_SCALE / Q_SCALE)).astype(BF16)
            dq_acc[k0:, :] += _dot(ds, kk)
        dq_ref[0, 0] = (dq_acc[...] * ATTN_SCALE).astype(BF16)

    qk = pl.BlockSpec((1, 1, tp, 2 * LANES), lambda h, b: (b, h, 0, 0))
    hv = pl.BlockSpec((1, 1, tp, D_V), lambda h, b: (b, h, 0, 0))
    col = pl.BlockSpec((tp, LANES), lambda h, b: (b, h))
    return pl.pallas_call(
        body,
        name="attn_bwd",
        grid=(N_HEADS, nb_seq),
        in_specs=[qk, qk, hv, hv, hv, col,
                  pl.BlockSpec((tp, LANES), lambda h, b: (b, GRP_A // LANES + h)),
                  pl.BlockSpec((1, LANES), lambda h, b: (0, h))],
        out_specs=(qk, qk, hv, col, pl.BlockSpec((1, LANES), lambda h, b: (0, h))),
        out_shape=(
            jax.ShapeDtypeStruct((nb_seq, N_HEADS, tp, 2 * LANES), BF16),
            jax.ShapeDtypeStruct((nb_seq, N_HEADS, tp, 2 * LANES), BF16),
            jax.ShapeDtypeStruct((nb_seq, N_HEADS, tp, D_V), BF16),
            jax.ShapeDtypeStruct((nb_seq * tp, N_HEADS * D_V), BF16),
            jax.ShapeDtypeStruct((1, N_HEADS * D_V), F32),
        ),
        scratch_shapes=[pltpu.VMEM((tp, 2 * LANES), F32)],
        compiler_params=_params("arbitrary", "arbitrary"),
    )(q, k, v, o, lse, dcat, p, g_attn)


def _qkv_bwd(p, dq, dk, dv, wq, wkv, gq, gkv, tables):
    nb_seq, _, tp, _ = dq.shape
    ht = tp // 2

    def body(pa_ref, dq_ref, dk_ref, dv_ref, wq_ref, wkv_ref, gq_ref, gkv_ref, cos_ref, sa_ref, sb_ref,
             dpa_ref, dwq_ref, dwkv_ref, dgq_ref, dgkv_ref):
        @pl.when(pl.program_id(0) == 0)
        def _():
            dwq_ref[...] = jnp.zeros_like(dwq_ref)
            dwkv_ref[...] = jnp.zeros_like(dwkv_ref)
            dgq_ref[...] = jnp.zeros_like(dgq_ref)
            dgkv_ref[...] = jnp.zeros_like(dgkv_ref)

        pa = pa_ref[...].astype(F32)
        gq, gkv = gq_ref[...], gkv_ref[...]
        cq_hat, rq = _rms_stats(pa[:, :Q_RANK])
        ckv_hat, rkv = _rms_stats(pa[:, Q_RANK:Q_RANK + KV_RANK])
        tabs = (cos_ref[...], sa_ref[...], sb_ref[...])

        pe = [dq_ref[0, h, :, D_NOPE:].astype(F32) for h in range(N_HEADS)]
        pairs = [_rope_t(pe[2 * i] + pltpu.roll(pe[2 * i + 1], D_ROPE, 1), *tabs).astype(BF16) for i in range(2)]
        dq_flat = jnp.concatenate([dq_ref[0, h, :, :D_NOPE] for h in range(N_HEADS)] + pairs, axis=1)
        dwq_ref[...] += _dot(dq_flat, (cq_hat * gq).astype(BF16), _TN)
        dcqn = _dot(dq_flat, wq_ref[...])
        dgq_ref[...] += jnp.sum(dcqn * cq_hat, axis=0, keepdims=True)
        dcq = _rms_bwd(gq * dcqn, cq_hat, rq)

        dkv_flat = jnp.concatenate([dk_ref[0, h, :, :D_NOPE] for h in range(N_HEADS)]
                                   + [dv_ref[0, h] for h in range(N_HEADS)], axis=1)
        dwkv_ref[...] += _dot((ckv_hat * gkv).astype(BF16), dkv_flat, _TN)
        dckvn = _dot(dkv_flat, wkv_ref[...], _NT)
        dgkv_ref[...] += jnp.sum(dckvn * ckv_hat, axis=0, keepdims=True)
        dckv = _rms_bwd(gkv * dckvn, ckv_hat, rkv)

        dk_pe = dk_ref[0, 0, :, D_NOPE:].astype(F32)
        for h in range(1, N_HEADS):
            dk_pe = dk_pe + dk_ref[0, h, :, D_NOPE:].astype(F32)
        dk_pe = jnp.where(lax.broadcasted_iota(jnp.int32, (ht, LANES), 1) < D_ROPE, dk_pe, 0.0)
        dpa_ref[...] = jnp.concatenate([dcq, dckv, _rope_t(dk_pe, *tabs)], axis=1).astype(BF16)

    full = lambda a: pl.BlockSpec(a.shape, lambda i: (0,) * a.ndim)
    tab = pl.BlockSpec((ht, LANES), lambda i: (i % 2, 0))
    qk = pl.BlockSpec((1, N_HEADS, ht, 2 * LANES), lambda i: (i // 2, 0, i % 2, 0))
    acc = lambda shape: pl.BlockSpec(shape, lambda i: (0, 0))
    return pl.pallas_call(
        body,
        name="qkv_bwd",
        grid=(2 * nb_seq,),
        in_specs=[pl.BlockSpec((ht, GRP_A), lambda i: (i, 0)), qk, qk,
                  pl.BlockSpec((1, N_HEADS, ht, D_V), lambda i: (i // 2, 0, i % 2, 0)),
                  full(wq), full(wkv), full(gq), full(gkv), tab, tab, tab],
        out_specs=(pl.BlockSpec((ht, GRP_A), lambda i: (i, 0)),
                   acc(wq.shape), acc(wkv.shape), acc((1, Q_RANK)), acc((1, KV_RANK))),
        out_shape=(
            jax.ShapeDtypeStruct((nb_seq * tp, GRP_A), BF16),
            jax.ShapeDtypeStruct(wq.shape, F32),
            jax.ShapeDtypeStruct(wkv.shape, F32),
            jax.ShapeDtypeStruct((1, Q_RANK), F32),
            jax.ShapeDtypeStruct((1, KV_RANK), F32),
        ),
        compiler_params=_params("arbitrary"),
    )(p, dq, dk, dv, wq, wkv, gq, gkv, *tables)


def _conv_bwd(p, dcat, conv_w, g_conv, nb_seq, tp):
    cols = CONV_WIDTH // LANES

    def body(b_ref, c_ref, h_ref, z_ref, dy_ref, w_ref, g_ref,
             db_ref, dc_ref, dh_ref, dz_ref, dw_ref, dg_ref):
        @pl.when(pl.program_id(1) == 0)
        def _():
            dw_ref[...] = jnp.zeros_like(dw_ref)
            dg_ref[...] = jnp.zeros_like(dg_ref)

        cb, c, h = b_ref[...].astype(F32), c_ref[...].astype(F32), h_ref[...].astype(F32)
        z, dy = z_ref[...].astype(F32), dy_ref[...].astype(F32)
        g = g_ref[...]
        w0, w1, w2 = w_ref[0:1, :], w_ref[1:2, :], w_ref[2:3, :]
        cc = c * h
        row = lax.broadcasted_iota(jnp.int32, (tp, LANES), 0)
        s1 = jnp.where(row >= 1, pltpu.roll(cc, 1, 0), 0.0)
        s2 = jnp.where(row >= 2, pltpu.roll(cc, 2, 0), 0.0)
        dwc = w0 * s2 + w1 * s1 + w2 * cc
        yc = cb * dwc
        r = lax.rsqrt(_group_mean(yc * yc) + EPS)
        ychat = yc * r
        sig = _sigmoid(z)
        dz_ref[...] = (dy * (ychat * g) * (sig * (1.0 + z * (1.0 - sig)))).astype(BF16)
        dyn = dy * (z * sig)
        dg_ref[...] += jnp.sum(dyn * ychat, axis=0, keepdims=True)
        gd = g * dyn
        dyc = r * (gd - ychat * _group_mean(gd * ychat))
        db_ref[...] = (dyc * dwc).astype(BF16)
        ddw = dyc * cb
        dw_ref[0:1, :] += jnp.sum(ddw * s2, axis=0, keepdims=True)
        dw_ref[1:2, :] += jnp.sum(ddw * s1, axis=0, keepdims=True)
        dw_ref[2:3, :] += jnp.sum(ddw * cc, axis=0, keepdims=True)
        u1 = jnp.where(row <= tp - 2, pltpu.roll(ddw, tp - 1, 0), 0.0)
        u2 = jnp.where(row <= tp - 3, pltpu.roll(ddw, tp - 2, 0), 0.0)
        dcc = w2 * ddw + w1 * u1 + w0 * u2
        dc_ref[...] = (dcc * h).astype(BF16)
        dh_ref[...] = (dcc * c).astype(BF16)

    col = pl.BlockSpec((tp, LANES), lambda t, b: (b, t))
    out = jax.ShapeDtypeStruct((nb_seq * tp, CONV_WIDTH), BF16)
    return pl.pallas_call(
        body,
        name="conv_bwd",
        grid=(cols, nb_seq),
        in_specs=_conv_specs(tp, lambda t, b, off: (b, off + t)) + [
            pl.BlockSpec((tp, LANES), lambda t, b: (b, N_HEADS * D_V // LANES + t)),
            pl.BlockSpec((8, LANES), lambda t, b: (0, t)),
            pl.BlockSpec((1, LANES), lambda t, b: (0, t))],
        out_specs=(col, col, col, col,
                   pl.BlockSpec((8, LANES), lambda t, b: (0, t)), pl.BlockSpec((1, LANES), lambda t, b: (0, t))),
        out_shape=(out, out, out, out,
                   jax.ShapeDtypeStruct((8, CONV_WIDTH), F32), jax.ShapeDtypeStruct((1, CONV_WIDTH), F32)),
        compiler_params=_params("arbitrary", "arbitrary"),
    )(p, p, p, p, dcat, conv_w, g_conv)


def _input_bwd(dps, w_in, x, meta, dh, norm_g, nt, send_in):
    nb_seq, s, d = x.shape
    r, kb = dps[0].shape
    ts = (s + LANES) // nt
    steps = nb_seq * nt
    n_dp = len(dps)
    in_slot = send_in.shape[1:]

    def body(*refs):
        dp_refs, w_ref, x_hbm, meta_ref, dh_ref, g_ref, pay_ref = refs[:n_dp], *refs[n_dp:n_dp + 6]
        o = n_dp + 6
        gx_hbm, dmeta_ref, dg_ref, r2_in, gmeta_ref, gnorm_ref = refs[o:o + 6]
        xbuf, gxbuf, tok_sems, own_in, r1_in, sum_in = refs[o + 6:o + 12]
        sems = refs[o + 12:-4]
        stail, rtail, tail_send, tail_recv = refs[-4:]
        i = pl.program_id(0)
        b, k = i // nt, i % nt

        def plan():
            return _reduce_plan((pay_ref,), (own_in,), (r1_in,), (sum_in,), (r2_in,), *sems)

        @pl.when(i == 0)
        def _():
            dmeta_ref[...] = jnp.zeros_like(dmeta_ref)
            dg_ref[...] = jnp.zeros_like(dg_ref)
            plan()[0]()

        @pl.when(i == 1)
        def _():
            plan()[1]()

        def start(kk):
            if kk == 0:
                xbuf[0:PAD_FRONT, :] = jnp.zeros((PAD_FRONT, d), F32)
                xbuf[PAD_FRONT:LANES, :] = meta_ref[...]
            _token_copy(x_hbm, b, kk, ts, xbuf, tok_sems.at[0]).start()

        _for_tile(k, nt, start)
        du = _dot(dp_refs[0][...], w_ref[0:kb, :])
        for j in range(1, n_dp):
            du = du + _dot(dp_refs[j][...], w_ref[kb * j:kb * (j + 1), :])
        _for_tile(k, nt, lambda kk: _token_copy(x_hbm, b, kk, ts, xbuf, tok_sems.at[0]).wait())

        g = g_ref[...]
        hhat, rstd = _rms_stats(xbuf[...])
        dg_ref[...] += jnp.sum(du * hhat, axis=0, keepdims=True)
        res = _rms_bwd(g * du, hhat, rstd) + dh_ref[...].astype(F32)

        @pl.when(i > 0)
        def _():
            _for_tile(k, nt, lambda kk: _token_copy(gx_hbm, b, (kk - 1) % nt, ts, gxbuf, tok_sems.at[1], True).wait())

        gxbuf[...] = res

        @pl.when(k == 0)
        def _():
            dmeta_ref[...] += gxbuf[PAD_FRONT:LANES, :]

        _for_tile(k, nt, lambda kk: _token_copy(gx_hbm, b, kk, ts, gxbuf, tok_sems.at[1], True).start(priority=1))

        @pl.when(i == steps - 1)
        def _():
            _token_copy(gx_hbm, b, nt - 1, ts, gxbuf, tok_sems.at[1], True).wait()
            px, py, pc, me = _device_position()
            for dev in range(N_DEV):
                stail[dev, 0:N_META, :] = dmeta_ref[:, SHARD_META * dev:SHARD_META * (dev + 1)]
                for j in range(d // LANES):
                    stail[dev, N_META + j:N_META + j + 1, :] = dg_ref[:, LANES * j:LANES * (j + 1)]
            copies = []
            for r in range(1, N_DEV):
                peer = (1 - px if r & 4 else px, 1 - py if r & 2 else py, 1 - pc if r & 1 else pc)
                copies.append(pltpu.make_async_remote_copy(
                    src_ref=stail.at[4 * peer[0] + 2 * peer[1] + peer[2]],
                    dst_ref=rtail.at[r],
                    send_sem=tail_send.at[r - 1],
                    recv_sem=tail_recv.at[r - 1],
                    device_id=peer,
                    device_id_type=pl.DeviceIdType.MESH,
                ))
            for cp in copies:
                cp.start()
            rtail[0] = stail[me]
            plan()[2]()
            for cp in copies:
                cp.wait_recv()
            gt = rtail[me]
            for dev in range(1, N_DEV):
                gt = gt + rtail[dev ^ me]
            gmeta_ref[...] = gt[0:N_META, :]
            for j in range(d // LANES):
                gnorm_ref[:, LANES * j:LANES * (j + 1)] = gt[N_META + j:N_META + j + 1, :]
            for cp in copies:
                cp.wait_send()

    whole = lambda a: pl.BlockSpec(a.shape, lambda i: (0,) * a.ndim)
    hbm = pl.BlockSpec(memory_space=pl.ANY)
    return pl.pallas_call(
        body,
        name="input_bwd",
        grid=(steps,),
        in_specs=[pl.BlockSpec((ts, kb), lambda i: (i, 0)) for _ in dps]
        + [whole(w_in), hbm, whole(meta), pl.BlockSpec((ts, d), lambda i: (i, 0)), whole(norm_g), hbm],
        out_specs=(hbm, pl.BlockSpec((N_META, d), lambda i: (0, 0)), pl.BlockSpec((1, d), lambda i: (0, 0)), hbm,
                   pl.BlockSpec((N_META, SHARD_META), lambda i: (0, 0)), pl.BlockSpec((1, d), lambda i: (0, 0))),
        out_shape=(jax.ShapeDtypeStruct((nb_seq, s, d), F32),
                   jax.ShapeDtypeStruct((N_META, d), F32),
                   jax.ShapeDtypeStruct((1, d), F32),
                   jax.ShapeDtypeStruct((N_CHIPS,) + in_slot, BF16),
                   jax.ShapeDtypeStruct((N_META, SHARD_META), F32),
                   jax.ShapeDtypeStruct((1, d), F32)),
        scratch_shapes=[pltpu.VMEM((ts, d), F32), pltpu.VMEM((ts, d), F32), pltpu.SemaphoreType.DMA((2,))]
        + _reduce_scratch([(in_slot, BF16)], [True])
        + [pltpu.VMEM((N_DEV, TAIL_ROWS, LANES), F32), pltpu.VMEM((N_DEV, TAIL_ROWS, LANES), F32),
           pltpu.SemaphoreType.DMA((N_DEV - 1,)), pltpu.SemaphoreType.DMA((N_DEV - 1,))],
        compiler_params=_params("arbitrary"),
    )(*dps, w_in, x, meta, dh, norm_g, send_in)


def _in_proj_bwd_w(u, dps, bm, small_grads, send_out):
    r, d = u.shape
    kb = dps[0].shape[1]
    steps = r // bm
    n_dp, n_small = len(dps), len(small_grads)
    out_slot, small_slot = send_out.shape[1:], (SMALL_ROWS, LANES)

    def body(*refs):
        u_ref, dp_refs = refs[0], refs[1:1 + n_dp]
        small_refs = refs[1 + n_dp:1 + n_dp + n_small]
        o = 1 + n_dp + n_small
        pay_out, o_ref, r2_out, r2_small = refs[o:o + 4]
        acc_ref, ssmall, r1_out, sum_out, r1_small, sum_small = refs[o + 4:o + 10]
        sems = refs[o + 10:]
        i = pl.program_id(0)

        def plan():
            return _reduce_plan((pay_out, ssmall), (None, None), (r1_out, r1_small), (sum_out, sum_small),
                                (r2_out, r2_small), *sems)

        @pl.when(i == 0)
        def _():
            acc_ref[...] = jnp.zeros_like(acc_ref)
            _pack_small(ssmall, *small_refs)
            plan()[0]()

        @pl.when(i == 1)
        def _():
            plan()[1]()

        uu = u_ref[...]
        for j in range(n_dp):
            acc_ref[kb * j:kb * (j + 1), :] += _dot(dp_refs[j][...], uu, _TN)

        @pl.when(i == steps - 1)
        def _():
            for k in range(N_DEV):
                for s, e, c0 in _in_pieces(k):
                    o_ref[k, s:e, :] = acc_ref[c0:c0 + e - s, :].astype(BF16)
                o_ref[k, SHARD_IN:, :] = jnp.zeros((SHARD_IN_PAD - SHARD_IN, d), BF16)
            plan()[2]()

    whole = lambda a: pl.BlockSpec(a.shape, lambda i: (0,) * a.ndim)
    hbm = pl.BlockSpec(memory_space=pl.ANY)
    return pl.pallas_call(
        body,
        name="in_proj_bwd_w",
        grid=(steps,),
        in_specs=[pl.BlockSpec((bm, d), lambda i: (i, 0))]
        + [pl.BlockSpec((bm, kb), lambda i: (i, 0)) for _ in dps] + [whole(a) for a in small_grads]
        + [whole(send_out)],
        out_specs=(pl.BlockSpec((N_DEV, SHARD_IN_PAD, d), lambda i: (0, 0, 0)), hbm, hbm),
        out_shape=(jax.ShapeDtypeStruct((N_DEV, SHARD_IN_PAD, d), BF16),
                   jax.ShapeDtypeStruct((N_CHIPS,) + out_slot, BF16),
                   jax.ShapeDtypeStruct((N_CHIPS,) + small_slot, F32)),
        scratch_shapes=[pltpu.VMEM((kb * n_dp, d), F32), pltpu.VMEM((N_DEV,) + small_slot, F32)]
        + _reduce_scratch([(out_slot, BF16), (small_slot, F32)], [False, False]),
        compiler_params=_params("arbitrary"),
    )(u, *dps, *small_grads, send_out)


def _local_step(x, loss_target, u, p, meta_f, norm_g, w_in_p, q_norm_g, w_q_p, kv_norm_g, w_kv_p, conv_w_f,
                attn_out_g, conv_out_g, w_out_f, g_final):
    nb_seq, s, d = x.shape
    tp = s + LANES
    ht = tp // 2
    tables = _rope_tables(tp)

    q, k, v = _qkv_fwd(p, w_q_p, w_kv_p, q_norm_g, kv_norm_g, tables, nb_seq, tp)
    ya, o, lse = _attn_fwd(q, k, v, p, attn_out_g)
    yc = _conv_fwd(p, conv_w_f, conv_out_g, nb_seq, tp)
    dhb, d_final_g, loss_part = _out_proj_loss(ya, yc, w_out_f, x, loss_target, g_final, TOKEN_TILES)

    dcat, d_w_out = _out_proj_bwd(dhb, w_out_f, ya, yc, ht)
    send_out = d_w_out.reshape(N_DEV, SHARD_OUT, d)
    dq, dk, dv, dz_attn, d_attn_g = _attn_bwd(q, k, v, o, lse, dcat, p, attn_out_g)
    dpa, d_wq_p, d_wkv_p, d_gq, d_gkv = _qkv_bwd(p, dq, dk, dv, w_q_p, w_kv_p, q_norm_g, kv_norm_g, tables)
    d_b, d_c, d_h, dz_conv, d_conv_w, d_conv_g = _conv_bwd(p, dcat, conv_w_f, conv_out_g, nb_seq, tp)
    dps = (dpa, dz_attn, d_b, d_c, d_h, dz_conv)
    small = (d_wq_p, d_wkv_p, d_conv_w, d_final_g, d_gq, d_gkv, d_attn_g, d_conv_g, loss_part)
    send_in, r_out, r_small = _in_proj_bwd_w(u, dps, ht, small, send_out)
    grad_x, _, _, r_in, g_meta, g_norm = _input_bwd(dps, w_in_p, x, meta_f, dhb, norm_g, TOKEN_TILES, send_in)
    return grad_x, r_in, r_out, r_small, g_meta, g_norm


def kernel(x, meta_tokens, norm_g, w_in, q_norm_g, w_q_up, kv_norm_g, w_kv_up, conv_w, attn_out_g, conv_out_g, w_out, final_norm_g, loss_target, m_meta_tokens, m_norm_g, m_w_in, m_q_norm_g, m_w_q_up, m_kv_norm_g, m_w_kv_up, m_conv_w, m_attn_out_g, m_conv_out_g, m_w_out, m_final_norm_g, v_meta_tokens, v_norm_g, v_w_in, v_q_norm_g, v_w_q_up, v_kv_norm_g, v_w_kv_up, v_conv_w, v_attn_out_g, v_conv_out_g, v_w_out, v_final_norm_g):
    d = x.shape[-1]
    order = jnp.asarray(_tile_orders()[0].reshape(-1))
    u, p, w_in_p, meta_f, w_q_p, w_kv_p, w_out_f, conv_w_f = _prep_in_proj(
        x, meta_tokens, norm_g, w_in[0].T, w_q_up[0].T, w_kv_up[0], w_out[0], conv_w.transpose(1, 0, 2), order)
    g_final = final_norm_g.reshape(1, d)
    grad_x, r_in, r_out, r_small, g_meta, g_norm = _local_step(
        x, loss_target, u, p, meta_f, norm_g, w_in_p, q_norm_g, w_q_p, kv_norm_g, w_kv_p, conv_w_f,
        attn_out_g, conv_out_g, w_out_f, g_final)

    flat = lambda a: a.reshape(a.shape[-2:]) if a.ndim == 3 else a.reshape(1, -1) if a.ndim == 1 else a
    transposed = ("w_in", "w_q_up")

    def to_kernel(n, a):
        if n == "conv_w":
            return a.transpose(1, 0, 2)
        return flat(a).T if n in transposed else flat(a)

    def from_kernel(n, a, shape):
        if n == "conv_w":
            return a.transpose(1, 0, 2)
        return (a.T if n in transposed else a).reshape(shape)
    params = {
        "meta_tokens": (meta_tokens, m_meta_tokens, v_meta_tokens),
        "norm_g": (norm_g, m_norm_g, v_norm_g),
        "w_in": (w_in, m_w_in, v_w_in),
        "q_norm_g": (q_norm_g, m_q_norm_g, v_q_norm_g),
        "w_q_up": (w_q_up, m_w_q_up, v_w_q_up),
        "kv_norm_g": (kv_norm_g, m_kv_norm_g, v_kv_norm_g),
        "w_kv_up": (w_kv_up, m_w_kv_up, v_w_kv_up),
        "conv_w": (conv_w, m_conv_w, v_conv_w),
        "attn_out_g": (attn_out_g, m_attn_out_g, v_attn_out_g),
        "conv_out_g": (conv_out_g, m_conv_out_g, v_conv_out_g),
        "w_out": (w_out, m_w_out, v_w_out),
        "final_norm_g": (final_norm_g, m_final_norm_g, v_final_norm_g),
    }
    updated, loss = _reduce_update(r_in, r_out, r_small, g_meta, g_norm,
                                   {n: tuple(to_kernel(n, a) for a in t) for n, t in params.items()})
    outs = [[from_kernel(n, updated[n][i], params[n][0].shape) for n, _ in PARAM_SHAPES] for i in range(4)]
    return (loss[0, 0], grad_x, *outs[0], *outs[1], *outs[2], *outs[3])
```

```python
import functools

import jax
import jax.numpy as jnp
import numpy as np
from jax import lax
from jax.experimental import pallas as pl
from jax.experimental.pallas import tpu as pltpu

F32 = jnp.float32
BF16 = jnp.bfloat16

N_META = 16
D_MODEL = 1024
N_HEADS = 4
D_NOPE = 128
D_ROPE = 64
D_V = 128
Q_RANK = 256
KV_RANK = 128
CONV_WIDTH = 512
CONV_GROUP = 64
ROPE_THETA = 10000.0
ATTN_SCALE = (D_NOPE + D_ROPE) ** -0.5
Q_SCALE = ATTN_SCALE * 1.4426950408889634
EPS = 1e-6
NEG_INF = -1e30

ADAM_LR = 0.001
ADAM_B1 = 0.9
ADAM_B2 = 0.999
ADAM_EPS = 1e-08
ADAM_WD = 0.01
ADAM_STEP = 10

LANES = 128
PAD_FRONT = LANES - N_META
K_TILE = 256
Q_TILE = 512
N_DEV = 8
VMEM_LIMIT = 56 * 1024 * 1024

IN_PAD = 3072
GRP_A = 512
N_A = Q_RANK + KV_RANK + D_ROPE
IN_PROJ = 3008
SHARD_IN = IN_PROJ // N_DEV
SHARD_IN_PAD = 384
SHARD_Q = 96
SHARD_KV = 128
SHARD_OUT = 128
SHARD_CONV = 64
SHARD_META = 128
Q_COLS = N_HEADS * (D_NOPE + D_ROPE)
KV_COLS = N_HEADS * (D_NOPE + D_V)

ROW_Q, ROW_KV, ROW_META, ROW_CONV = 0, 256, 384, 400
ROW_REPL = 408
ROW_NORM, ROW_FINAL, ROW_GQ, ROW_GKV, ROW_ATTN, ROW_CONVG, ROW_LOSS = 408, 416, 424, 426, 427, 431, 435
SMALL_ROWS = 440

PARAM_SHAPES = (
    ("meta_tokens", (N_META, SHARD_META)), ("norm_g", (1, D_MODEL)), ("w_in", (SHARD_IN, D_MODEL)),
    ("q_norm_g", (1, Q_RANK)), ("w_q_up", (SHARD_Q, Q_RANK)), ("kv_norm_g", (1, KV_RANK)),
    ("w_kv_up", (KV_RANK, SHARD_KV)), ("conv_w", (3, 1, SHARD_CONV)), ("attn_out_g", (1, CONV_WIDTH)),
    ("conv_out_g", (1, CONV_WIDTH)), ("w_out", (SHARD_OUT, D_MODEL)), ("final_norm_g", (1, D_MODEL)),
)


def _in_pieces(k):
    lo, hi = SHARD_IN * k, SHARD_IN * (k + 1)
    out = []
    if lo < N_A:
        out.append((0, min(hi, N_A) - lo, lo))
    if hi > N_A:
        s = max(lo, N_A)
        out.append((s - lo, hi - lo, s + GRP_A - N_A))
    return out


P_TILE = 256


def _tile_pieces(j):
    lo, hi = P_TILE * j, P_TILE * (j + 1)
    out = []
    for k in range(N_DEV):
        for s, e, d in _in_pieces(k):
            a, b = max(d, lo), min(d + e - s, hi)
            if a < b:
                out.append((k, s + a - d, s + b - d, a - lo))
    return out


def _tile_orders():
    n_tiles = IN_PAD // P_TILE
    sources = [{k for k, _, _, _ in _tile_pieces(j)} for j in range(n_tiles)]
    rows, n_early, n_free = [], n_tiles, n_tiles
    for chip in range(N_CHIPS):
        own = {2 * chip, 2 * chip + 1}
        diagonal = {2 * (N_CHIPS - 1 - chip), 2 * (N_CHIPS - 1 - chip) + 1}
        early = [j for j in range(n_tiles) if sources[j] <= own]
        late = [j for j in range(n_tiles) if sources[j] & diagonal]
        mid = [j for j in range(n_tiles) if j not in early and j not in late]
        rows.append(early + mid + late)
        n_early, n_free = min(n_early, len(early)), min(n_free, len(early) + len(mid))
    return np.asarray(rows, np.int32), n_early, n_free


def _q_pieces(k):
    lo, hi = SHARD_Q * k, SHARD_Q * (k + 1)
    out = []
    for h in range(N_HEADS):
        base = (D_NOPE + D_ROPE) * h
        s, e = max(lo, base), min(hi, base + D_NOPE)
        if s < e:
            out.append((s - lo, e - lo, D_NOPE * h + s - base))
        s, e = max(lo, base + D_NOPE), min(hi, base + D_NOPE + D_ROPE)
        if s < e:
            out.append((s - lo, e - lo, N_HEADS * D_NOPE + D_ROPE * h + s - base - D_NOPE))
    return out


def _kv_dst(k):
    return D_NOPE * (k // 2) + (N_HEADS * D_NOPE if k % 2 else 0)


def _params(*sem):
    return pltpu.CompilerParams(dimension_semantics=sem, vmem_limit_bytes=VMEM_LIMIT)


def _rms_stats(x):
    r = lax.rsqrt(jnp.mean(x * x, axis=-1, keepdims=True) + EPS)
    return x * r, r


def _rms_bwd(gdy, xhat, r):
    return r * (gdy - xhat * jnp.mean(gdy * xhat, axis=-1, keepdims=True))


def _sigmoid(z):
    return 1.0 / (1.0 + jnp.exp(-z))


def _group_mean(x):
    i0 = lax.broadcasted_iota(jnp.int32, (LANES, LANES), 0) // CONV_GROUP
    i1 = lax.broadcasted_iota(jnp.int32, (LANES, LANES), 1) // CONV_GROUP
    m = jnp.where(i0 == i1, 1.0 / CONV_GROUP, 0.0).astype(BF16)
    hi = x.astype(BF16)
    lo = (x - hi.astype(F32)).astype(BF16)
    return jnp.dot(hi, m, preferred_element_type=F32) + jnp.dot(lo, m, preferred_element_type=F32)


_NT = (((1,), (1,)), ((), ()))
_TN = (((0,), (0,)), ((), ()))


def _dot(a, b, dims=None):
    if dims is None:
        return jnp.dot(a, b, preferred_element_type=F32)
    return lax.dot_general(a, b, dims, preferred_element_type=F32)


def _device_position():
    x, y, c = lax.axis_index("x"), lax.axis_index("y"), lax.axis_index("c")
    return x, y, c, 4 * x + 2 * y + c


def _gather_plan(srcs, slots, send_sems, recv_sems, local_sems):
    x, y, c, _ = _device_position()
    me, sibling = (x, y, c), (x, y, 1 - c)
    flip = lambda v, on: v + on - 2 * v * on
    near = (flip(x, 1 - c), flip(y, c))
    far = (flip(x, c), flip(y, 1 - c))
    diag = (1 - x, 1 - y)
    n = len(srcs)

    def slot(a, px, py, pc):
        return slots[a].at[4 * px + 2 * py + pc]

    def copy(a, k, block, to, own=False):
        return pltpu.make_async_remote_copy(
            src_ref=srcs[a] if own else slot(a, *block),
            dst_ref=slot(a, *block),
            send_sem=send_sems.at[7 * a + k],
            recv_sem=recv_sems.at[7 * a + k],
            device_id=to,
            device_id_type=pl.DeviceIdType.MESH,
        )

    def local(a):
        return pltpu.make_async_copy(srcs[a], slot(a, *me), local_sems.at[a])

    sent = [(me, sibling), (me, (*near, c)), (me, (*far, c)), ((*near, c), (*far, c)),
            ((*near, c), sibling), ((*far, c), sibling), ((*diag, c), sibling)]
    landed = [sibling, (*near, c), (*far, c), (*diag, c), (*far, 1 - c), (*near, 1 - c), (*diag, 1 - c)]

    def send(a, k):
        return copy(a, k, *sent[k], own=k < 3)

    def arrival(a, k):
        return copy(a, k, landed[k], me)

    def start():
        for a in range(n):
            local(a).start()
            for k in range(3):
                send(a, k).start()

    def own():
        for a in range(n):
            local(a).wait()
            arrival(a, 0).wait_recv()

    def mid():
        for a in range(n):
            arrival(a, 1).wait_recv()
            send(a, 3).start()
            send(a, 4).start()
        for a in range(n):
            arrival(a, 2).wait_recv()
            send(a, 5).start()
        for a in range(n):
            for k in (4, 5):
                arrival(a, k).wait_recv()

    def late():
        for a in range(n):
            arrival(a, 3).wait_recv()
            send(a, 6).start()
        for a in range(n):
            arrival(a, 6).wait_recv()

    def finish():
        for a in range(n):
            for k in range(7):
                send(a, k).wait_send()

    return start, own, mid, late, finish


def _adam_update(g, w, m, v):
    m_new = ADAM_B1 * m + (1.0 - ADAM_B1) * g
    v_new = ADAM_B2 * v + (1.0 - ADAM_B2) * (g * g)
    m_hat = m_new / (1.0 - ADAM_B1 ** ADAM_STEP)
    v_hat = v_new / (1.0 - ADAM_B2 ** ADAM_STEP)
    return -ADAM_LR * (m_hat / (jnp.sqrt(v_hat) + ADAM_EPS) + ADAM_WD * w), m_new, v_new


N_CHIPS = 4


def _reduce_plan(pays, owns, r1s, sums, r2s, send1, recv1, send2, recv2, local_sems):
    x, y, c, _ = _device_position()
    sibling = (x, y, 1 - c)
    chips = [((1 - x if rj & 2 else x), (1 - y if rj & 1 else y)) for rj in range(N_CHIPS)]
    n = len(pays)

    def slot_of(rj, core):
        return 4 * chips[rj][0] + 2 * chips[rj][1] + core

    def to_sibling(a, rj):
        return pltpu.make_async_remote_copy(
            src_ref=pays[a].at[slot_of(rj, 1 - c)], dst_ref=r1s[a].at[rj],
            send_sem=send1.at[N_CHIPS * a + rj], recv_sem=recv1.at[N_CHIPS * a + rj],
            device_id=sibling, device_id_type=pl.DeviceIdType.MESH)

    def load_own(a, rj):
        return pltpu.make_async_copy(pays[a].at[slot_of(rj, c)], owns[a].at[rj], local_sems.at[2 * N_CHIPS * a + rj])

    def to_chip(a, rj):
        return pltpu.make_async_remote_copy(
            src_ref=sums[a].at[rj], dst_ref=r2s[a].at[rj],
            send_sem=send2.at[N_CHIPS * a + rj], recv_sem=recv2.at[N_CHIPS * a + rj],
            device_id=(*chips[rj], c), device_id_type=pl.DeviceIdType.MESH)

    def keep(a):
        return pltpu.make_async_copy(sums[a].at[0], r2s[a].at[0], local_sems.at[2 * N_CHIPS * a + N_CHIPS])

    def start():
        for a in range(n):
            for rj in range(N_CHIPS):
                to_sibling(a, rj).start()
                if owns[a] is not None:
                    load_own(a, rj).start()

    def combine():
        for a in range(n):
            for rj in range(N_CHIPS):
                to_sibling(a, rj).wait_recv()
                if owns[a] is not None:
                    load_own(a, rj).wait()
                    mine = owns[a][rj]
                else:
                    mine = pays[a][slot_of(rj, c)]
                sums[a][rj] = (mine.astype(F32) + r1s[a][rj].astype(F32)).astype(sums[a].dtype)
            keep(a).start()
            for rj in range(1, N_CHIPS):
                to_chip(a, rj).start()

    def finish():
        for a in range(n):
            for rj in range(1, N_CHIPS):
                to_chip(a, rj).wait_recv()
            for rj in range(N_CHIPS):
                to_sibling(a, rj).wait_send()
            for rj in range(1, N_CHIPS):
                to_chip(a, rj).wait_send()
            keep(a).wait()

    return start, combine, finish


def _reduce_scratch(shapes_dtypes, own_flags):
    out = []
    for (shape, dtype), own in zip(shapes_dtypes, own_flags):
        if own:
            out.append(pltpu.VMEM((N_CHIPS,) + shape, dtype))
        out += [pltpu.VMEM((N_CHIPS,) + shape, dtype), pltpu.VMEM((N_CHIPS,) + shape, dtype)]
    n = len(shapes_dtypes)
    out += [pltpu.SemaphoreType.DMA((N_CHIPS * n,))] * 4 + [pltpu.SemaphoreType.DMA((2 * N_CHIPS * n,))]
    return out


def _pack_small(ssmall, dwq, dwkv, dconv, dfinal, dgq, dgkv, dattn, dconvg, loss_part):
    ssmall[...] = jnp.zeros_like(ssmall)
    rep = ssmall.at[0]
    for i in range(D_MODEL // LANES):
        rep[ROW_FINAL + i:ROW_FINAL + i + 1, :] = dfinal[:, LANES * i:LANES * (i + 1)]
    for i in range(Q_RANK // LANES):
        rep[ROW_GQ + i:ROW_GQ + i + 1, :] = dgq[:, LANES * i:LANES * (i + 1)]
    rep[ROW_GKV:ROW_GKV + 1, :] = dgkv[...]
    for i in range(CONV_WIDTH // LANES):
        rep[ROW_ATTN + i:ROW_ATTN + i + 1, :] = dattn[:, LANES * i:LANES * (i + 1)]
        rep[ROW_CONVG + i:ROW_CONVG + i + 1, :] = dconvg[:, LANES * i:LANES * (i + 1)]
    rep[ROW_LOSS:ROW_LOSS + 1, :] = loss_part[...]
    for k in range(N_DEV):
        if k:
            ssmall[k, ROW_REPL:, :] = ssmall[0, ROW_REPL:, :]
        for s, e, d in _q_pieces(k):
            for i in range(Q_RANK // LANES):
                ssmall[k, ROW_Q + SHARD_Q * i + s:ROW_Q + SHARD_Q * i + e, :] = dwq[d:d + e - s, LANES * i:LANES * (i + 1)]
        ssmall[k, ROW_KV:ROW_KV + KV_RANK, :] = dwkv[:, _kv_dst(k):_kv_dst(k) + SHARD_KV]
        ssmall[k, ROW_CONV:ROW_CONV + 3, 0:SHARD_CONV] = dconv[0:3, SHARD_CONV * k:SHARD_CONV * (k + 1)]


TOKEN_TILES = 4
TAIL_ROWS = N_META + D_MODEL // LANES


def _reduce_update(r_in, r_out, r_small, g_meta, g_norm, params):
    n_p = len(PARAM_SHAPES)
    names = [n for n, _ in PARAM_SHAPES]

    def body(*refs):
        rin, rout, rsmall, gmeta, gnorm = refs[:5]
        par = refs[5:5 + 3 * n_p]
        o = 5 + 3 * n_p
        g_out = {n: refs[o + i] for i, n in enumerate(names)}
        loss_out = refs[o + n_p]
        upd = refs[o + n_p + 1:o + 4 * n_p + 1]
        gsum = refs[o + 4 * n_p + 1]
        x, y, c, me = _device_position()
        my_chip = 2 * x + y

        g = rin[my_chip].astype(F32)
        for ch in range(1, N_CHIPS):
            g = g + rin[ch ^ my_chip].astype(F32)
        g_out["w_in"][...] = g[:SHARD_IN, :]

        g = rout[my_chip].astype(F32)
        gs = rsmall[my_chip]
        for ch in range(1, N_CHIPS):
            g = g + rout[ch ^ my_chip].astype(F32)
            gs = gs + rsmall[ch ^ my_chip]
        g_out["w_out"][...] = g
        gsum[...] = gs
        for i in range(Q_RANK // LANES):
            g_out["w_q_up"][:, LANES * i:LANES * (i + 1)] = gsum[ROW_Q + SHARD_Q * i:ROW_Q + SHARD_Q * (i + 1), :]
        g_out["w_kv_up"][...] = gsum[ROW_KV:ROW_KV + KV_RANK, :]
        for i in range(3):
            g_out["conv_w"][i] = gsum[ROW_CONV + i:ROW_CONV + i + 1, 0:SHARD_CONV]
        for name, row, width in (("final_norm_g", ROW_FINAL, D_MODEL), ("q_norm_g", ROW_GQ, Q_RANK),
                                 ("kv_norm_g", ROW_GKV, KV_RANK), ("attn_out_g", ROW_ATTN, CONV_WIDTH),
                                 ("conv_out_g", ROW_CONVG, CONV_WIDTH)):
            for i in range(width // LANES):
                g_out[name][:, LANES * i:LANES * (i + 1)] = gsum[row + i:row + i + 1, :]
        loss_out[...] = gsum[ROW_LOSS:ROW_LOSS + 1, :]
        g_out["meta_tokens"][...] = gmeta[...]
        g_out["norm_g"][...] = gnorm[...]

        for i, n in enumerate(names):
            w, m, v = (par[3 * i + j][...] for j in range(3))
            for j, val in enumerate(_adam_update(g_out[n][...], w, m, v)):
                upd[3 * i + j][...] = val

    vm = pl.BlockSpec(memory_space=pltpu.VMEM)
    out_shape = [jax.ShapeDtypeStruct(shape, F32) for _, shape in PARAM_SHAPES]
    out_shape.append(jax.ShapeDtypeStruct((1, LANES), F32))
    for _, shape in PARAM_SHAPES:
        out_shape += [jax.ShapeDtypeStruct(shape, F32)] * 3
    outs = pl.pallas_call(
        body,
        name="reduce_update",
        out_shape=tuple(out_shape),
        in_specs=[vm] * (5 + 3 * n_p),
        out_specs=(vm,) * len(out_shape),
        scratch_shapes=[pltpu.VMEM((SMALL_ROWS, LANES), F32)],
        compiler_params=pltpu.CompilerParams(vmem_limit_bytes=VMEM_LIMIT),
    )(r_in, r_out, r_small, g_meta, g_norm, *[a for n in names for a in params[n]])
    return {n: (outs[i], *outs[n_p + 1 + 3 * i:n_p + 4 + 3 * i]) for i, n in enumerate(names)}, outs[n_p]


def _prep_in_proj(x, meta, norm_g, w_in_t, w_q, w_kv, w_out, conv_w, order):
    nb_seq, s, d = x.shape
    tp = s + LANES
    m = nb_seq * tp
    ts = s // TOKEN_TILES
    n_real = nb_seq * TOKEN_TILES
    n_norm = n_real + 1
    n_tiles = IN_PAD // P_TILE
    _, n_early, n_free = _tile_orders()
    steps = n_norm + n_tiles
    dot_rows = m // 4
    qkv_shape = (SHARD_Q + KV_RANK, Q_RANK)

    def tile(t):
        t = jnp.minimum(t, n_real - 1)
        return t // TOKEN_TILES, t % TOKEN_TILES

    def column_tile(t, order_ref):
        chip = 2 * lax.axis_index("x") + lax.axis_index("y")
        return order_ref[chip * n_tiles + jnp.maximum(t - n_norm, 0)]

    def body(order_ref, x_ref, meta_ref, g_ref, win_ref, wq_ref, wkv_ref, wout_ref, conv_ref,
             u_hbm, p_ref, w_in_p, meta_f, w_q_p, w_kv_p, w_out_f, conv_f,
             u_all, sbig, gbig, smeta, gmeta, sqkv, sout, sconv, gqkv, gout, gconv,
             send_in, recv_in, local_in, send_meta, recv_meta, send_rest, recv_rest, local_rest, u_sem):
        t = pl.program_id(0)
        px, py, pc, me = _device_position()
        u_copy = pltpu.make_async_copy(u_all, u_hbm, u_sem)

        def plan_in():
            return _gather_plan((sbig,), (gbig,), send_in, recv_in, local_in)

        def plan_rest():
            return _gather_plan((sqkv, sout, sconv), (gqkv, gout, gconv), send_rest, recv_rest, local_rest)

        def meta_copies():
            out = []
            for r in range(1, N_DEV):
                peer = (1 - px if r & 4 else px, 1 - py if r & 2 else py, 1 - pc if r & 1 else pc)
                out.append(pltpu.make_async_remote_copy(
                    src_ref=smeta,
                    dst_ref=gmeta.at[r],
                    send_sem=send_meta.at[r - 1],
                    recv_sem=recv_meta.at[r - 1],
                    device_id=peer,
                    device_id_type=pl.DeviceIdType.MESH,
                ))
            return out

        @pl.when(t == 0)
        def _():
            sbig[0:SHARD_IN, :] = win_ref[...].astype(BF16)
            sbig[SHARD_IN:, :] = jnp.zeros((SHARD_IN_PAD - SHARD_IN, d), BF16)
            smeta[...] = meta_ref[...]
            plan_in()[0]()
            for cp in meta_copies():
                cp.start()
            sqkv[...] = jnp.zeros_like(sqkv)
            sqkv[0:SHARD_Q, :] = wq_ref[...].astype(BF16)
            sqkv[SHARD_Q:, 0:SHARD_KV] = wkv_ref[...].astype(BF16)
            sout[...] = wout_ref[...].astype(BF16)
            sconv[...] = jnp.zeros_like(sconv)
            for i in range(3):
                sconv[i:i + 1, 0:SHARD_CONV] = conv_ref[i]

        def norm(h):
            hhat, _ = _rms_stats(h)
            return (hhat * g_ref[...]).astype(BF16)

        @pl.when(t < n_real)
        def _():
            b, k = tile(t)
            row0 = pl.multiple_of(b * tp + LANES + k * ts, 16)
            u_all[pl.ds(row0, ts), :] = norm(x_ref[0])

        @pl.when(t == n_real)
        def _():
            for cp in meta_copies():
                cp.wait_recv()
            gmeta[0] = smeta[...]
            for k in range(N_DEV):
                meta_f[:, SHARD_META * k:SHARD_META * (k + 1)] = gmeta[k ^ me]
            um = norm(meta_f[...])
            for b in range(nb_seq):
                u_all[b * tp:b * tp + PAD_FRONT, :] = jnp.zeros((PAD_FRONT, d), BF16)
                u_all[b * tp + PAD_FRONT:b * tp + LANES, :] = um
            u_copy.start(priority=1)

        @pl.when(t == n_norm)
        def _():
            plan_in()[1]()

        @pl.when(t == n_norm + n_early)
        def _():
            plan_in()[2]()
            plan_rest()[0]()

        @pl.when(t == n_norm + (n_early + n_free) // 2)
        def _():
            plan_rest()[1]()
            plan_rest()[2]()

        @pl.when(t == n_norm + n_free)
        def _():
            plan_in()[3]()

        @pl.when(t == n_norm + n_free + 2)
        def _():
            plan_rest()[3]()

        @pl.when(t >= n_norm)
        def _():
            j = column_tile(t, order_ref)
            for jj in range(n_tiles):
                @pl.when(j == jj)
                def _(jj=jj):
                    if jj == N_A // P_TILE:
                        w_in_p[N_A % P_TILE:, :] = jnp.zeros((P_TILE - N_A % P_TILE, d), BF16)
                    for k, s0, e0, d0 in _tile_pieces(jj):
                        w_in_p[d0:d0 + e0 - s0, :] = gbig[k, s0:e0, :]
            for r in range(m // dot_rows):
                rows = slice(dot_rows * r, dot_rows * (r + 1))
                p_ref[rows, :] = _dot(u_all[rows, :], w_in_p[...], _NT).astype(BF16)

        @pl.when(t == steps - 1)
        def _():
            plan_in()[4]()
            plan_rest()[4]()
            for cp in meta_copies():
                cp.wait_send()
            u_copy.wait()
            conv_f[...] = jnp.zeros_like(conv_f)
            for k in range(N_DEV):
                for s0, e0, d0 in _q_pieces(k):
                    w_q_p[d0:d0 + e0 - s0, :] = gqkv[k, s0:e0, :]
                w_kv_p[:, _kv_dst(k):_kv_dst(k) + SHARD_KV] = gqkv[k, SHARD_Q:, 0:SHARD_KV]
                w_out_f[SHARD_OUT * k:SHARD_OUT * (k + 1), :] = gout[k]
                conv_f[0:3, SHARD_CONV * k:SHARD_CONV * (k + 1)] = gconv[k, 0:3, 0:SHARD_CONV]

    whole = lambda shape: pl.BlockSpec(shape, lambda t, o: (0,) * len(shape))
    return pl.pallas_call(
        body,
        name="prep_in_proj_gather",
        grid_spec=pltpu.PrefetchScalarGridSpec(
            num_scalar_prefetch=1,
            grid=(steps,),
            in_specs=[
                pl.BlockSpec((1, ts, d), lambda t, o: (*tile(t), 0)),
                whole(meta.shape), whole(norm_g.shape), whole(w_in_t.shape),
                whole(w_q.shape), whole(w_kv.shape), whole(w_out.shape), whole(conv_w.shape),
            ],
            out_specs=(pl.BlockSpec(memory_space=pl.ANY),
                       pl.BlockSpec((m, P_TILE), lambda t, o: (0, column_tile(t, o))),
                       pl.BlockSpec((P_TILE, d), lambda t, o: (column_tile(t, o), 0)),
                       whole((N_META, d)),
                       whole((Q_COLS, Q_RANK)), whole((KV_RANK, KV_COLS)), whole((D_MODEL, D_MODEL)),
                       whole((8, CONV_WIDTH))),
            scratch_shapes=[
                pltpu.VMEM((m, d), BF16),
                pltpu.VMEM((SHARD_IN_PAD, d), BF16),
                pltpu.VMEM((N_DEV, SHARD_IN_PAD, d), BF16),
                pltpu.VMEM((N_META, SHARD_META), F32),
                pltpu.VMEM((N_DEV, N_META, SHARD_META), F32),
                pltpu.VMEM(qkv_shape, BF16),
                pltpu.VMEM((SHARD_OUT, D_MODEL), BF16),
                pltpu.VMEM((8, LANES), F32),
                pltpu.VMEM((N_DEV,) + qkv_shape, BF16),
                pltpu.VMEM((N_DEV, SHARD_OUT, D_MODEL), BF16),
                pltpu.VMEM((N_DEV, 8, LANES), F32),
                pltpu.SemaphoreType.DMA((7,)),
                pltpu.SemaphoreType.DMA((7,)),
                pltpu.SemaphoreType.DMA((1,)),
                pltpu.SemaphoreType.DMA((N_DEV - 1,)),
                pltpu.SemaphoreType.DMA((N_DEV - 1,)),
                pltpu.SemaphoreType.DMA((21,)),
                pltpu.SemaphoreType.DMA((21,)),
                pltpu.SemaphoreType.DMA((3,)),
                pltpu.SemaphoreType.DMA,
            ],
        ),
        out_shape=(jax.ShapeDtypeStruct((m, d), BF16),
                   jax.ShapeDtypeStruct((m, IN_PAD), BF16),
                   jax.ShapeDtypeStruct((IN_PAD, d), BF16),
                   jax.ShapeDtypeStruct((N_META, d), F32),
                   jax.ShapeDtypeStruct((Q_COLS, Q_RANK), BF16),
                   jax.ShapeDtypeStruct((KV_RANK, KV_COLS), BF16),
                   jax.ShapeDtypeStruct((D_MODEL, D_MODEL), BF16),
                   jax.ShapeDtypeStruct((8, CONV_WIDTH), F32)),
        compiler_params=_params("arbitrary"),
    )(order, x, meta, norm_g, w_in_t, w_q, w_kv, w_out, conv_w)


def _rope_tables(tp):
    half = D_ROPE // 2
    inv_freq = (1.0 / (ROPE_THETA ** (np.arange(half, dtype=np.float32) / half))).astype(np.float32)
    pos = (np.arange(tp) - PAD_FRONT).astype(np.float32)
    ang = pos[:, None] * inv_freq[None, :]
    cos = np.tile(np.cos(ang), (1, LANES // half))
    sin = np.tile(np.sin(ang), (1, LANES // half))
    first = (np.arange(LANES) % D_ROPE) < half
    zero = np.float32(0.0)
    return tuple(jnp.asarray(t, F32) for t in (cos, np.where(first, -sin, zero), np.where(first, zero, sin)))


def _rope(t, cos, sa, sb):
    return t * cos + pltpu.roll(t, LANES - D_ROPE // 2, 1) * sa + pltpu.roll(t, D_ROPE // 2, 1) * sb


def _rope_t(t, cos, sa, sb):
    return t * cos + pltpu.roll(t * sa, D_ROPE // 2, 1) + pltpu.roll(t * sb, LANES - D_ROPE // 2, 1)


def _qkv_fwd(p, wq, wkv, gq, gkv, tables, nb_seq, tp):
    ht = tp // 2

    def body(pa_ref, wq_ref, wkv_ref, gq_ref, gkv_ref, cos_ref, sa_ref, sb_ref, q_ref, k_ref, v_ref):
        pa = pa_ref[...].astype(F32)
        cq_hat, _ = _rms_stats(pa[:, :Q_RANK])
        ckv_hat, _ = _rms_stats(pa[:, Q_RANK:Q_RANK + KV_RANK])
        q = _dot((cq_hat * gq_ref[...]).astype(BF16), wq_ref[...], _NT) * Q_SCALE
        kv = _dot((ckv_hat * gkv_ref[...]).astype(BF16), wkv_ref[...])
        tabs = (cos_ref[...], sa_ref[...], sb_ref[...])
        lane = lax.broadcasted_iota(jnp.int32, (ht, LANES), 1)
        low = lane < D_ROPE
        mark = lane == D_ROPE
        row = (pl.program_id(0) % 2) * ht + lax.broadcasted_iota(jnp.int32, (ht, LANES), 0)
        k_pe = jnp.where(mark & (row < PAD_FRONT), NEG_INF, _rope(pa[:, Q_RANK + KV_RANK:], *tabs))
        one = jnp.where(mark & (row >= PAD_FRONT), 1.0, 0.0)
        pairs = [_rope(q[:, N_HEADS * D_NOPE + LANES * i:N_HEADS * D_NOPE + LANES * (i + 1)], *tabs) for i in range(2)]
        for h in range(N_HEADS):
            pair = pairs[h // 2]
            if h % 2:
                pair = pltpu.roll(pair, D_ROPE, 1)
            pe = jnp.where(low, pair, one)
            q_ref[0, h] = jnp.concatenate([q[:, D_NOPE * h:D_NOPE * (h + 1)], pe], axis=1).astype(BF16)
            k_ref[0, h] = jnp.concatenate([kv[:, D_NOPE * h:D_NOPE * (h + 1)], k_pe], axis=1).astype(BF16)
            v_ref[0, h] = kv[:, N_HEADS * D_NOPE + D_V * h:N_HEADS * D_NOPE + D_V * (h + 1)].astype(BF16)

    full = lambda a: pl.BlockSpec(a.shape, lambda i: (0,) * a.ndim)
    tab = pl.BlockSpec((ht, LANES), lambda i: (i % 2, 0))
    qk = pl.BlockSpec((1, N_HEADS, ht, 2 * LANES), lambda i: (i // 2, 0, i % 2, 0))
    return pl.pallas_call(
        body,
        name="qkv_fwd",
        grid=(2 * nb_seq,),
        in_specs=[pl.BlockSpec((ht, GRP_A), lambda i: (i, 0)), full(wq), full(wkv), full(gq), full(gkv), tab, tab, tab],
        out_specs=(qk, qk, pl.BlockSpec((1, N_HEADS, ht, D_V), lambda i: (i // 2, 0, i % 2, 0))),
        out_shape=(
            jax.ShapeDtypeStruct((nb_seq, N_HEADS, tp, 2 * LANES), BF16),
            jax.ShapeDtypeStruct((nb_seq, N_HEADS, tp, 2 * LANES), BF16),
            jax.ShapeDtypeStruct((nb_seq, N_HEADS, tp, D_V), BF16),
        ),
        compiler_params=_params("parallel"),
    )(p, wq, wkv, gq, gkv, *tables)


def _attn_fwd(q, k, v, p, g_attn):
    nb_seq, _, tp, _ = q.shape

    def body(q_ref, k_ref, v_ref, z_ref, g_ref, y_ref, o_ref, lse_ref):
        g = g_ref[...]
        for r0 in range(0, tp, Q_TILE):
            nq = min(Q_TILE, tp - r0)
            kend = r0 + nq
            qq = q_ref[0, 0, r0:kend, :]
            sd = _dot(qq, k_ref[0, 0, r0:kend, :], _NT)
            causal = (lax.broadcasted_iota(jnp.int32, (nq, nq), 1) <= lax.broadcasted_iota(jnp.int32, (nq, nq), 0))
            sd = jnp.where(causal, sd, NEG_INF)
            m = jnp.max(sd, axis=-1, keepdims=True)
            if r0:
                so = _dot(qq, k_ref[0, 0, 0:r0, :], _NT)
                m = jnp.maximum(m, jnp.max(so, axis=-1, keepdims=True))
            ed = jnp.exp2(sd - m)
            l = jnp.sum(ed, axis=-1, keepdims=True)
            o = _dot(ed.astype(BF16), v_ref[0, 0, r0:kend, :])
            if r0:
                eo = jnp.exp2(so - m)
                l = l + jnp.sum(eo, axis=-1, keepdims=True)
                o = o + _dot(eo.astype(BF16), v_ref[0, 0, 0:r0, :])
            o = o * (1.0 / l)
            o_ref[0, 0, r0:kend, :] = o
            lse_ref[0, 0, r0:kend, :] = jnp.broadcast_to(m + jnp.log2(l), (nq, LANES))
            ohat, _ = _rms_stats(o)
            z = z_ref[r0:kend, :].astype(F32)
            y_ref[r0:kend, :] = (ohat * g * (z * _sigmoid(z))).astype(BF16)

    qk = pl.BlockSpec((1, 1, tp, 2 * LANES), lambda b, h: (b, h, 0, 0))
    hv = pl.BlockSpec((1, 1, tp, D_V), lambda b, h: (b, h, 0, 0))
    return pl.pallas_call(
        body,
        name="attn_fwd",
        grid=(nb_seq, N_HEADS),
        in_specs=[qk, qk, hv,
                  pl.BlockSpec((tp, LANES), lambda b, h: (b, GRP_A // LANES + h)),
                  pl.BlockSpec((1, LANES), lambda b, h: (0, h))],
        out_specs=(pl.BlockSpec((tp, LANES), lambda b, h: (b, h)), hv, hv),
        out_shape=(
            jax.ShapeDtypeStruct((nb_seq * tp, N_HEADS * D_V), BF16),
            jax.ShapeDtypeStruct((nb_seq, N_HEADS, tp, D_V), F32),
            jax.ShapeDtypeStruct((nb_seq, N_HEADS, tp, LANES), F32),
        ),
        compiler_params=_params("parallel", "parallel"),
    )(q, k, v, p, g_attn)


_CONV_COL0 = (GRP_A + N_HEADS * D_V) // LANES


def _conv_specs(tp, order):
    cols = CONV_WIDTH // LANES
    return [pl.BlockSpec((tp, LANES), functools.partial(
        lambda a, b, off: order(a, b, off), off=_CONV_COL0 + i * cols)) for i in range(4)]


def _conv_fwd(p, conv_w, g_conv, nb_seq, tp):
    def body(b_ref, c_ref, h_ref, z_ref, w_ref, g_ref, y_ref):
        cc = c_ref[...].astype(F32) * h_ref[...].astype(F32)
        row = lax.broadcasted_iota(jnp.int32, (tp, LANES), 0)
        s1 = jnp.where(row >= 1, pltpu.roll(cc, 1, 0), 0.0)
        s2 = jnp.where(row >= 2, pltpu.roll(cc, 2, 0), 0.0)
        yc = b_ref[...].astype(F32) * (w_ref[0:1, :] * s2 + w_ref[1:2, :] * s1 + w_ref[2:3, :] * cc)
        r = lax.rsqrt(_group_mean(yc * yc) + EPS)
        z = z_ref[...].astype(F32)
        y_ref[...] = (yc * r * g_ref[...] * (z * _sigmoid(z))).astype(BF16)

    return pl.pallas_call(
        body,
        name="conv_fwd",
        grid=(nb_seq, CONV_WIDTH // LANES),
        in_specs=_conv_specs(tp, lambda b, t, off: (b, off + t)) + [
            pl.BlockSpec((8, LANES), lambda b, t: (0, t)),
            pl.BlockSpec((1, LANES), lambda b, t: (0, t))],
        out_specs=pl.BlockSpec((tp, LANES), lambda b, t: (b, t)),
        out_shape=jax.ShapeDtypeStruct((nb_seq * tp, CONV_WIDTH), BF16),
        compiler_params=_params("parallel", "parallel"),
    )(p, p, p, p, conv_w, g_conv)


def _token_copy(hbm, b, k, ts, buf, sem, to_hbm=False):
    lo, hi = max(k * ts - LANES, 0), (k + 1) * ts - LANES
    off = lo - (k * ts - LANES)
    src, dst = hbm.at[b, pl.ds(lo, hi - lo)], buf.at[pl.ds(off, hi - lo)]
    if to_hbm:
        src, dst = dst, src
    return pltpu.make_async_copy(src, dst, sem)


def _for_tile(k, nt, fn):
    for kk in range(nt):
        @pl.when(k == kk)
        def _(kk=kk):
            fn(kk)


def _out_proj_loss(ya, yc, w_out, x, target, g_final, nt):
    nb_seq, s, d = x.shape
    r, ka = ya.shape
    ts = (s + LANES) // nt
    steps = nb_seq * nt

    def body(a_ref, c_ref, w_ref, x_hbm, t_hbm, g_ref, dhb_ref, dg_ref, loss_ref,
             xbuf, tbuf, acc_ref, sems):
        i = pl.program_id(0)
        b, k = i // nt, i % nt

        @pl.when(i == 0)
        def _():
            acc_ref[...] = jnp.zeros_like(acc_ref)
            dg_ref[...] = jnp.zeros_like(dg_ref)

        slot = i % 2

        def fetch(seq, kk, sl):
            return [_token_copy(x_hbm, seq, kk, ts, xbuf.at[sl], sems.at[sl, 0]),
                    _token_copy(t_hbm, seq, kk, ts, tbuf.at[sl], sems.at[sl, 1])]

        def start(seq, sl, kk):
            if kk == 0:
                xbuf[sl, 0:LANES, :] = jnp.zeros((LANES, d), F32)
                tbuf[sl, 0:LANES, :] = jnp.zeros((LANES, d), F32)
            for cp in fetch(seq, kk, sl):
                cp.start()

        @pl.when(i == 0)
        def _():
            start(0, 0, 0)

        @pl.when(i + 1 < steps)
        def _():
            _for_tile((i + 1) % nt, nt, functools.partial(start, (i + 1) // nt, 1 - slot))

        mix = _dot(a_ref[...], w_ref[0:ka, :]) + _dot(c_ref[...], w_ref[ka:, :])
        _for_tile(k, nt, lambda kk: [cp.wait() for cp in fetch(b, kk, slot)])

        real = (lax.broadcasted_iota(jnp.int32, (ts, d), 0) >= LANES) | (k > 0)
        g = g_ref[...]
        hhat, rstd = _rms_stats(xbuf[slot] + mix)
        e = jnp.where(real, hhat * g - tbuf[slot], 0.0)
        acc_ref[...] += jnp.sum(e * e, axis=0, keepdims=True)
        dy = e * (1.0 / d)
        dg_ref[...] += jnp.sum(dy * hhat, axis=0, keepdims=True)
        dhb_ref[...] = _rms_bwd(g * dy, hhat, rstd).astype(BF16)

        @pl.when(i == steps - 1)
        def _():
            total = jnp.sum(acc_ref[...], axis=1, keepdims=True)
            loss_ref[...] = jnp.broadcast_to((0.5 / d) * total, loss_ref.shape)

    hbm = pl.BlockSpec(memory_space=pl.ANY)
    row = pl.BlockSpec((ts, d), lambda i: (i, 0))
    vec = pl.BlockSpec((1, d), lambda i: (0, 0))
    return pl.pallas_call(
        body,
        name="out_proj_loss",
        grid=(steps,),
        in_specs=[pl.BlockSpec((ts, ka), lambda i: (i, 0)), pl.BlockSpec((ts, yc.shape[1]), lambda i: (i, 0)),
                  pl.BlockSpec(w_out.shape, lambda i: (0, 0)), hbm, hbm, vec],
        out_specs=(row, vec, pl.BlockSpec((1, LANES), lambda i: (0, 0))),
        out_shape=(
            jax.ShapeDtypeStruct((r, d), BF16),
            jax.ShapeDtypeStruct((1, d), F32),
            jax.ShapeDtypeStruct((1, LANES), F32),
        ),
        scratch_shapes=[pltpu.VMEM((2, ts, d), F32), pltpu.VMEM((2, ts, d), F32), pltpu.VMEM((1, d), F32),
                        pltpu.SemaphoreType.DMA((2, 2))],
        compiler_params=_params("arbitrary"),
    )(ya, yc, w_out, x, target, g_final)


def _out_proj_bwd(dhb, w_out, ya, yc, bm):
    r, d = dhb.shape
    ka = ya.shape[1]
    n_mix = w_out.shape[0]
    last = r // bm - 1

    def body(dh_ref, w_ref, a_ref, c_ref, dcat_ref, dw_ref, acc_ref):
        @pl.when(pl.program_id(0) == 0)
        def _():
            acc_ref[...] = jnp.zeros_like(acc_ref)

        dh = dh_ref[...]
        dcat_ref[...] = _dot(dh, w_ref[...], _NT).astype(BF16)
        acc_ref[0:ka, :] += _dot(a_ref[...], dh, _TN)
        acc_ref[ka:, :] += _dot(c_ref[...], dh, _TN)

        @pl.when(pl.program_id(0) == last)
        def _():
            dw_ref[...] = acc_ref[...].astype(BF16)

    return pl.pallas_call(
        body,
        name="out_proj_bwd",
        grid=(r // bm,),
        in_specs=[pl.BlockSpec((bm, d), lambda i: (i, 0)), pl.BlockSpec(w_out.shape, lambda i: (0, 0)),
                  pl.BlockSpec((bm, ka), lambda i: (i, 0)), pl.BlockSpec((bm, yc.shape[1]), lambda i: (i, 0))],
        out_specs=(pl.BlockSpec((bm, n_mix), lambda i: (i, 0)),
                   pl.BlockSpec((n_mix, d), lambda i: (0, 0))),
        out_shape=(jax.ShapeDtypeStruct((r, n_mix), BF16),
                   jax.ShapeDtypeStruct((n_mix, d), BF16)),
        scratch_shapes=[pltpu.VMEM((n_mix, d), F32)],
        compiler_params=_params("arbitrary"),
    )(dhb, w_out, ya, yc)


def _attn_bwd(q, k, v, o, lse, dcat, p, g_attn):
    nb_seq, _, tp, _ = q.shape

    def body(q_ref, k_ref, v_ref, o_ref, lse_ref, dy_ref, z_ref, g_ref,
             dq_ref, dk_ref, dv_ref, dz_ref, dg_ref, dq_acc):
        @pl.when(pl.program_id(1) == 0)
        def _():
            dg_ref[...] = jnp.zeros_like(dg_ref)

        g = g_ref[...]
        z = z_ref[...].astype(F32)
        o = o_ref[0, 0]
        dy = dy_ref[...].astype(F32)
        sig = _sigmoid(z)
        ohat, r = _rms_stats(o)
        don = dy * (z * sig)
        dz_ref[...] = (dy * (ohat * g) * (sig * (1.0 + z * (1.0 - sig)))).astype(BF16)
        dg_ref[...] += jnp.sum(don * ohat, axis=0, keepdims=True)
        do = _rms_bwd(g * don, ohat, r)
        dvec = jnp.sum(do * o, axis=-1, keepdims=True)
        dob = do.astype(BF16)
        lse_col = lse_ref[0, 0, :, 0:1]
        dq_acc[...] = jnp.zeros_like(dq_acc)
        for k0 in range(0, tp, K_TILE):
            nk = min(K_TILE, tp - k0)
            nq = tp - k0
            qq = q_ref[0, 0, k0:, :]
            kk = k_ref[0, 0, k0:k0 + nk, :]
            causal = (lax.broadcasted_iota(jnp.int32, (nq, nk), 1) <= lax.broadcasted_iota(jnp.int32, (nq, nk), 0))
            pr = jnp.where(causal, jnp.exp2(_dot(qq, kk, _NT) - lse_col[k0:]), 0.0)
            dp = _dot(dob[k0:], v_ref[0, 0, k0:k0 + nk, :], _NT)
            ds = (pr * (dp - dvec[k0:])).astype(BF16)
            dv_ref[0, 0, k0:k0 + nk, :] = _dot(pr.astype(BF16), dob[k0:], _TN).astype(BF16)
            dk_ref[0, 0, k0:k0 + nk, :] = (_dot(ds, qq, _TN) * (ATTN_SCALE / Q_SCALE)).astype(BF16)
            dq_acc[k0:, :] += _dot(ds, kk)
        dq_ref[0, 0] = (dq_acc[...] * ATTN_SCALE).astype(BF16)

    qk = pl.BlockSpec((1, 1, tp, 2 * LANES), lambda h, b: (b, h, 0, 0))
    hv = pl.BlockSpec((1, 1, tp, D_V), lambda h, b: (b, h, 0, 0))
    col = pl.BlockSpec((tp, LANES), lambda h, b: (b, h))
    return pl.pallas_call(
        body,
        name="attn_bwd",
        grid=(N_HEADS, nb_seq),
        in_specs=[qk, qk, hv, hv, hv, col,
                  pl.BlockSpec((tp, LANES), lambda h, b: (b, GRP_A // LANES + h)),
                  pl.BlockSpec((1, LANES), lambda h, b: (0, h))],
        out_specs=(qk, qk, hv, col, pl.BlockSpec((1, LANES), lambda h, b: (0, h))),
        out_shape=(
            jax.ShapeDtypeStruct((nb_seq, N_HEADS, tp, 2 * LANES), BF16),
            jax.ShapeDtypeStruct((nb_seq, N_HEADS, tp, 2 * LANES), BF16),
            jax.ShapeDtypeStruct((nb_seq, N_HEADS, tp, D_V), BF16),
            jax.ShapeDtypeStruct((nb_seq * tp, N_HEADS * D_V), BF16),
            jax.ShapeDtypeStruct((1, N_HEADS * D_V), F32),
        ),
        scratch_shapes=[pltpu.VMEM((tp, 2 * LANES), F32)],
        compiler_params=_params("arbitrary", "arbitrary"),
    )(q, k, v, o, lse, dcat, p, g_attn)


def _qkv_bwd(p, dq, dk, dv, wq, wkv, gq, gkv, tables):
    nb_seq, _, tp, _ = dq.shape
    ht = tp // 2

    def body(pa_ref, dq_ref, dk_ref, dv_ref, wq_ref, wkv_ref, gq_ref, gkv_ref, cos_ref, sa_ref, sb_ref,
             dpa_ref, dwq_ref, dwkv_ref, dgq_ref, dgkv_ref):
        @pl.when(pl.program_id(0) == 0)
        def _():
            dwq_ref[...] = jnp.zeros_like(dwq_ref)
            dwkv_ref[...] = jnp.zeros_like(dwkv_ref)
            dgq_ref[...] = jnp.zeros_like(dgq_ref)
            dgkv_ref[...] = jnp.zeros_like(dgkv_ref)

        pa = pa_ref[...].astype(F32)
        gq, gkv = gq_ref[...], gkv_ref[...]
        cq_hat, rq = _rms_stats(pa[:, :Q_RANK])
        ckv_hat, rkv = _rms_stats(pa[:, Q_RANK:Q_RANK + KV_RANK])
        tabs = (cos_ref[...], sa_ref[...], sb_ref[...])

        pe = [dq_ref[0, h, :, D_NOPE:].astype(F32) for h in range(N_HEADS)]
        pairs = [_rope_t(pe[2 * i] + pltpu.roll(pe[2 * i + 1], D_ROPE, 1), *tabs).astype(BF16) for i in range(2)]
        dq_flat = jnp.concatenate([dq_ref[0, h, :, :D_NOPE] for h in range(N_HEADS)] + pairs, axis=1)
        dwq_ref[...] += _dot(dq_flat, (cq_hat * gq).astype(BF16), _TN)
        dcqn = _dot(dq_flat, wq_ref[...])
        dgq_ref[...] += jnp.sum(dcqn * cq_hat, axis=0, keepdims=True)
        dcq = _rms_bwd(gq * dcqn, cq_hat, rq)

        dkv_flat = jnp.concatenate([dk_ref[0, h, :, :D_NOPE] for h in range(N_HEADS)]
                                   + [dv_ref[0, h] for h in range(N_HEADS)], axis=1)
        dwkv_ref[...] += _dot((ckv_hat * gkv).astype(BF16), dkv_flat, _TN)
        dckvn = _dot(dkv_flat, wkv_ref[...], _NT)
        dgkv_ref[...] += jnp.sum(dckvn * ckv_hat, axis=0, keepdims=True)
        dckv = _rms_bwd(gkv * dckvn, ckv_hat, rkv)

        dk_pe = dk_ref[0, 0, :, D_NOPE:].astype(F32)
        for h in range(1, N_HEADS):
            dk_pe = dk_pe + dk_ref[0, h, :, D_NOPE:].astype(F32)
        dk_pe = jnp.where(lax.broadcasted_iota(jnp.int32, (ht, LANES), 1) < D_ROPE, dk_pe, 0.0)
        dpa_ref[...] = jnp.concatenate([dcq, dckv, _rope_t(dk_pe, *tabs)], axis=1).astype(BF16)

    full = lambda a: pl.BlockSpec(a.shape, lambda i: (0,) * a.ndim)
    tab = pl.BlockSpec((ht, LANES), lambda i: (i % 2, 0))
    qk = pl.BlockSpec((1, N_HEADS, ht, 2 * LANES), lambda i: (i // 2, 0, i % 2, 0))
    acc = lambda shape: pl.BlockSpec(shape, lambda i: (0, 0))
    return pl.pallas_call(
        body,
        name="qkv_bwd",
        grid=(2 * nb_seq,),
        in_specs=[pl.BlockSpec((ht, GRP_A), lambda i: (i, 0)), qk, qk,
                  pl.BlockSpec((1, N_HEADS, ht, D_V), lambda i: (i // 2, 0, i % 2, 0)),
                  full(wq), full(wkv), full(gq), full(gkv), tab, tab, tab],
        out_specs=(pl.BlockSpec((ht, GRP_A), lambda i: (i, 0)),
                   acc(wq.shape), acc(wkv.shape), acc((1, Q_RANK)), acc((1, KV_RANK))),
        out_shape=(
            jax.ShapeDtypeStruct((nb_seq * tp, GRP_A), BF16),
            jax.ShapeDtypeStruct(wq.shape, F32),
            jax.ShapeDtypeStruct(wkv.shape, F32),
            jax.ShapeDtypeStruct((1, Q_RANK), F32),
            jax.ShapeDtypeStruct((1, KV_RANK), F32),
        ),
        compiler_params=_params("arbitrary"),
    )(p, dq, dk, dv, wq, wkv, gq, gkv, *tables)


def _conv_bwd(p, dcat, conv_w, g_conv, nb_seq, tp):
    cols = CONV_WIDTH // LANES

    def body(b_ref, c_ref, h_ref, z_ref, dy_ref, w_ref, g_ref,
             db_ref, dc_ref, dh_ref, dz_ref, dw_ref, dg_ref):
        @pl.when(pl.program_id(1) == 0)
        def _():
            dw_ref[...] = jnp.zeros_like(dw_ref)
            dg_ref[...] = jnp.zeros_like(dg_ref)

        cb, c, h = b_ref[...].astype(F32), c_ref[...].astype(F32), h_ref[...].astype(F32)
        z, dy = z_ref[...].astype(F32), dy_ref[...].astype(F32)
        g = g_ref[...]
        w0, w1, w2 = w_ref[0:1, :], w_ref[1:2, :], w_ref[2:3, :]
        cc = c * h
        row = lax.broadcasted_iota(jnp.int32, (tp, LANES), 0)
        s1 = jnp.where(row >= 1, pltpu.roll(cc, 1, 0), 0.0)
        s2 = jnp.where(row >= 2, pltpu.roll(cc, 2, 0), 0.0)
        dwc = w0 * s2 + w1 * s1 + w2 * cc
        yc = cb * dwc
        r = lax.rsqrt(_group_mean(yc * yc) + EPS)
        ychat = yc * r
        sig = _sigmoid(z)
        dz_ref[...] = (dy * (ychat * g) * (sig * (1.0 + z * (1.0 - sig)))).astype(BF16)
        dyn = dy * (z * sig)
        dg_ref[...] += jnp.sum(dyn * ychat, axis=0, keepdims=True)
        gd = g * dyn
        dyc = r * (gd - ychat * _group_mean(gd * ychat))
        db_ref[...] = (dyc * dwc).astype(BF16)
        ddw = dyc * cb
        dw_ref[0:1, :] += jnp.sum(ddw * s2, axis=0, keepdims=True)
        dw_ref[1:2, :] += jnp.sum(ddw * s1, axis=0, keepdims=True)
        dw_ref[2:3, :] += jnp.sum(ddw * cc, axis=0, keepdims=True)
        u1 = jnp.where(row <= tp - 2, pltpu.roll(ddw, tp - 1, 0), 0.0)
        u2 = jnp.where(row <= tp - 3, pltpu.roll(ddw, tp - 2, 0), 0.0)
        dcc = w2 * ddw + w1 * u1 + w0 * u2
        dc_ref[...] = (dcc * h).astype(BF16)
        dh_ref[...] = (dcc * c).astype(BF16)

    col = pl.BlockSpec((tp, LANES), lambda t, b: (b, t))
    out = jax.ShapeDtypeStruct((nb_seq * tp, CONV_WIDTH), BF16)
    return pl.pallas_call(
        body,
        name="conv_bwd",
        grid=(cols, nb_seq),
        in_specs=_conv_specs(tp, lambda t, b, off: (b, off + t)) + [
            pl.BlockSpec((tp, LANES), lambda t, b: (b, N_HEADS * D_V // LANES + t)),
            pl.BlockSpec((8, LANES), lambda t, b: (0, t)),
            pl.BlockSpec((1, LANES), lambda t, b: (0, t))],
        out_specs=(col, col, col, col,
                   pl.BlockSpec((8, LANES), lambda t, b: (0, t)), pl.BlockSpec((1, LANES), lambda t, b: (0, t))),
        out_shape=(out, out, out, out,
                   jax.ShapeDtypeStruct((8, CONV_WIDTH), F32), jax.ShapeDtypeStruct((1, CONV_WIDTH), F32)),
        compiler_params=_params("arbitrary", "arbitrary"),
    )(p, p, p, p, dcat, conv_w, g_conv)


def _input_bwd(dps, w_in, x, meta, dh, norm_g, nt, send_in):
    nb_seq, s, d = x.shape
    r, kb = dps[0].shape
    ts = (s + LANES) // nt
    steps = nb_seq * nt
    n_dp = len(dps)
    in_slot = send_in.shape[1:]

    def body(*refs):
        dp_refs, w_ref, x_hbm, meta_ref, dh_ref, g_ref, pay_ref = refs[:n_dp], *refs[n_dp:n_dp + 6]
        o = n_dp + 6
        gx_hbm, dmeta_ref, dg_ref, r2_in, gmeta_ref, gnorm_ref = refs[o:o + 6]
        xbuf, gxbuf, tok_sems, own_in, r1_in, sum_in = refs[o + 6:o + 12]
        sems = refs[o + 12:-4]
        stail, rtail, tail_send, tail_recv = refs[-4:]
        i = pl.program_id(0)
        b, k = i // nt, i % nt

        def plan():
            return _reduce_plan((pay_ref,), (own_in,), (r1_in,), (sum_in,), (r2_in,), *sems)

        @pl.when(i == 0)
        def _():
            dmeta_ref[...] = jnp.zeros_like(dmeta_ref)
            dg_ref[...] = jnp.zeros_like(dg_ref)
            plan()[0]()

        @pl.when(i == 1)
        def _():
            plan()[1]()

        def start(kk):
            if kk == 0:
                xbuf[0:PAD_FRONT, :] = jnp.zeros((PAD_FRONT, d), F32)
                xbuf[PAD_FRONT:LANES, :] = meta_ref[...]
            _token_copy(x_hbm, b, kk, ts, xbuf, tok_sems.at[0]).start()

        _for_tile(k, nt, start)
        du = _dot(dp_refs[0][...], w_ref[0:kb, :])
        for j in range(1, n_dp):
            du = du + _dot(dp_refs[j][...], w_ref[kb * j:kb * (j + 1), :])
        _for_tile(k, nt, lambda kk: _token_copy(x_hbm, b, kk, ts, xbuf, tok_sems.at[0]).wait())

        g = g_ref[...]
        hhat, rstd = _rms_stats(xbuf[...])
        dg_ref[...] += jnp.sum(du * hhat, axis=0, keepdims=True)
        res = _rms_bwd(g * du, hhat, rstd) + dh_ref[...].astype(F32)

        @pl.when(i > 0)
        def _():
            _for_tile(k, nt, lambda kk: _token_copy(gx_hbm, b, (kk - 1) % nt, ts, gxbuf, tok_sems.at[1], True).wait())

        gxbuf[...] = res

        @pl.when(k == 0)
        def _():
            dmeta_ref[...] += gxbuf[PAD_FRONT:LANES, :]

        _for_tile(k, nt, lambda kk: _token_copy(gx_hbm, b, kk, ts, gxbuf, tok_sems.at[1], True).start())

        @pl.when(i == steps - 1)
        def _():
            _token_copy(gx_hbm, b, nt - 1, ts, gxbuf, tok_sems.at[1], True).wait()
            px, py, pc, me = _device_position()
            for dev in range(N_DEV):
                stail[dev, 0:N_META, :] = dmeta_ref[:, SHARD_META * dev:SHARD_META * (dev + 1)]
                for j in range(d // LANES):
                    stail[dev, N_META + j:N_META + j + 1, :] = dg_ref[:, LANES * j:LANES * (j + 1)]
            copies = []
            for r in range(1, N_DEV):
                peer = (1 - px if r & 4 else px, 1 - py if r & 2 else py, 1 - pc if r & 1 else pc)
                copies.append(pltpu.make_async_remote_copy(
                    src_ref=stail.at[4 * peer[0] + 2 * peer[1] + peer[2]],
                    dst_ref=rtail.at[r],
                    send_sem=tail_send.at[r - 1],
                    recv_sem=tail_recv.at[r - 1],
                    device_id=peer,
                    device_id_type=pl.DeviceIdType.MESH,
                ))
            for cp in copies:
                cp.start()
            rtail[0] = stail[me]
            plan()[2]()
            for cp in copies:
                cp.wait_recv()
            gt = rtail[me]
            for dev in range(1, N_DEV):
                gt = gt + rtail[dev ^ me]
            gmeta_ref[...] = gt[0:N_META, :]
            for j in range(d // LANES):
                gnorm_ref[:, LANES * j:LANES * (j + 1)] = gt[N_META + j:N_META + j + 1, :]
            for cp in copies:
                cp.wait_send()

    whole = lambda a: pl.BlockSpec(a.shape, lambda i: (0,) * a.ndim)
    hbm = pl.BlockSpec(memory_space=pl.ANY)
    return pl.pallas_call(
        body,
        name="input_bwd",
        grid=(steps,),
        in_specs=[pl.BlockSpec((ts, kb), lambda i: (i, 0)) for _ in dps]
        + [whole(w_in), hbm, whole(meta), pl.BlockSpec((ts, d), lambda i: (i, 0)), whole(norm_g), hbm],
        out_specs=(hbm, pl.BlockSpec((N_META, d), lambda i: (0, 0)), pl.BlockSpec((1, d), lambda i: (0, 0)), hbm,
                   pl.BlockSpec((N_META, SHARD_META), lambda i: (0, 0)), pl.BlockSpec((1, d), lambda i: (0, 0))),
        out_shape=(jax.ShapeDtypeStruct((nb_seq, s, d), F32),
                   jax.ShapeDtypeStruct((N_META, d), F32),
                   jax.ShapeDtypeStruct((1, d), F32),
                   jax.ShapeDtypeStruct((N_CHIPS,) + in_slot, BF16),
                   jax.ShapeDtypeStruct((N_META, SHARD_META), F32),
                   jax.ShapeDtypeStruct((1, d), F32)),
        scratch_shapes=[pltpu.VMEM((ts, d), F32), pltpu.VMEM((ts, d), F32), pltpu.SemaphoreType.DMA((2,))]
        + _reduce_scratch([(in_slot, BF16)], [True])
        + [pltpu.VMEM((N_DEV, TAIL_ROWS, LANES), F32), pltpu.VMEM((N_DEV, TAIL_ROWS, LANES), F32),
           pltpu.SemaphoreType.DMA((N_DEV - 1,)), pltpu.SemaphoreType.DMA((N_DEV - 1,))],
        compiler_params=_params("arbitrary"),
    )(*dps, w_in, x, meta, dh, norm_g, send_in)


def _in_proj_bwd_w(u, dps, bm, small_grads, send_out):
    r, d = u.shape
    kb = dps[0].shape[1]
    steps = r // bm
    n_dp, n_small = len(dps), len(small_grads)
    out_slot, small_slot = send_out.shape[1:], (SMALL_ROWS, LANES)

    def body(*refs):
        u_ref, dp_refs = refs[0], refs[1:1 + n_dp]
        small_refs = refs[1 + n_dp:1 + n_dp + n_small]
        o = 1 + n_dp + n_small
        pay_out, o_ref, r2_out, r2_small = refs[o:o + 4]
        acc_ref, ssmall, r1_out, sum_out, r1_small, sum_small = refs[o + 4:o + 10]
        sems = refs[o + 10:]
        i = pl.program_id(0)

        def plan():
            return _reduce_plan((pay_out, ssmall), (None, None), (r1_out, r1_small), (sum_out, sum_small),
                                (r2_out, r2_small), *sems)

        @pl.when(i == 0)
        def _():
            acc_ref[...] = jnp.zeros_like(acc_ref)
            _pack_small(ssmall, *small_refs)
            plan()[0]()

        @pl.when(i == 1)
        def _():
            plan()[1]()

        uu = u_ref[...]
        for j in range(n_dp):
            acc_ref[kb * j:kb * (j + 1), :] += _dot(dp_refs[j][...], uu, _TN)

        @pl.when(i == steps - 1)
        def _():
            for k in range(N_DEV):
                for s, e, c0 in _in_pieces(k):
                    o_ref[k, s:e, :] = acc_ref[c0:c0 + e - s, :].astype(BF16)
                o_ref[k, SHARD_IN:, :] = jnp.zeros((SHARD_IN_PAD - SHARD_IN, d), BF16)
            plan()[2]()

    whole = lambda a: pl.BlockSpec(a.shape, lambda i: (0,) * a.ndim)
    hbm = pl.BlockSpec(memory_space=pl.ANY)
    return pl.pallas_call(
        body,
        name="in_proj_bwd_w",
        grid=(steps,),
        in_specs=[pl.BlockSpec((bm, d), lambda i: (i, 0))]
        + [pl.BlockSpec((bm, kb), lambda i: (i, 0)) for _ in dps] + [whole(a) for a in small_grads]
        + [whole(send_out)],
        out_specs=(pl.BlockSpec((N_DEV, SHARD_IN_PAD, d), lambda i: (0, 0, 0)), hbm, hbm),
        out_shape=(jax.ShapeDtypeStruct((N_DEV, SHARD_IN_PAD, d), BF16),
                   jax.ShapeDtypeStruct((N_CHIPS,) + out_slot, BF16),
                   jax.ShapeDtypeStruct((N_CHIPS,) + small_slot, F32)),
        scratch_shapes=[pltpu.VMEM((kb * n_dp, d), F32), pltpu.VMEM((N_DEV,) + small_slot, F32)]
        + _reduce_scratch([(out_slot, BF16), (small_slot, F32)], [False, False]),
        compiler_params=_params("arbitrary"),
    )(u, *dps, *small_grads, send_out)


def _local_step(x, loss_target, u, p, meta_f, norm_g, w_in_p, q_norm_g, w_q_p, kv_norm_g, w_kv_p, conv_w_f,
                attn_out_g, conv_out_g, w_out_f, g_final):
    nb_seq, s, d = x.shape
    tp = s + LANES
    ht = tp // 2
    tables = _rope_tables(tp)

    q, k, v = _qkv_fwd(p, w_q_p, w_kv_p, q_norm_g, kv_norm_g, tables, nb_seq, tp)
    ya, o, lse = _attn_fwd(q, k, v, p, attn_out_g)
    yc = _conv_fwd(p, conv_w_f, conv_out_g, nb_seq, tp)
    dhb, d_final_g, loss_part = _out_proj_loss(ya, yc, w_out_f, x, loss_target, g_final, TOKEN_TILES)

    dcat, d_w_out = _out_proj_bwd(dhb, w_out_f, ya, yc, ht)
    send_out = d_w_out.reshape(N_DEV, SHARD_OUT, d)
    dq, dk, dv, dz_attn, d_attn_g = _attn_bwd(q, k, v, o, lse, dcat, p, attn_out_g)
    dpa, d_wq_p, d_wkv_p, d_gq, d_gkv = _qkv_bwd(p, dq, dk, dv, w_q_p, w_kv_p, q_norm_g, kv_norm_g, tables)
    d_b, d_c, d_h, dz_conv, d_conv_w, d_conv_g = _conv_bwd(p, dcat, conv_w_f, conv_out_g, nb_seq, tp)
    dps = (dpa, dz_attn, d_b, d_c, d_h, dz_conv)
    small = (d_wq_p, d_wkv_p, d_conv_w, d_final_g, d_gq, d_gkv, d_attn_g, d_conv_g, loss_part)
    send_in, r_out, r_small = _in_proj_bwd_w(u, dps, ht, small, send_out)
    grad_x, _, _, r_in, g_meta, g_norm = _input_bwd(dps, w_in_p, x, meta_f, dhb, norm_g, TOKEN_TILES, send_in)
    return grad_x, r_in, r_out, r_small, g_meta, g_norm


def kernel(x, meta_tokens, norm_g, w_in, q_norm_g, w_q_up, kv_norm_g, w_kv_up, conv_w, attn_out_g, conv_out_g, w_out, final_norm_g, loss_target, m_meta_tokens, m_norm_g, m_w_in, m_q_norm_g, m_w_q_up, m_kv_norm_g, m_w_kv_up, m_conv_w, m_attn_out_g, m_conv_out_g, m_w_out, m_final_norm_g, v_meta_tokens, v_norm_g, v_w_in, v_q_norm_g, v_w_q_up, v_kv_norm_g, v_w_kv_up, v_conv_w, v_attn_out_g, v_conv_out_g, v_w_out, v_final_norm_g):
    d = x.shape[-1]
    order = jnp.asarray(_tile_orders()[0].reshape(-1))
    u, p, w_in_p, meta_f, w_q_p, w_kv_p, w_out_f, conv_w_f = _prep_in_proj(
        x, meta_tokens, norm_g, w_in[0].T, w_q_up[0].T, w_kv_up[0], w_out[0], conv_w.transpose(1, 0, 2), order)
    g_final = final_norm_g.reshape(1, d)
    grad_x, r_in, r_out, r_small, g_meta, g_norm = _local_step(
        x, loss_target, u, p, meta_f, norm_g, w_in_p, q_norm_g, w_q_p, kv_norm_g, w_kv_p, conv_w_f,
        attn_out_g, conv_out_g, w_out_f, g_final)

    flat = lambda a: a.reshape(a.shape[-2:]) if a.ndim == 3 else a.reshape(1, -1) if a.ndim == 1 else a
    transposed = ("w_in", "w_q_up")

    def to_kernel(n, a):
        if n == "conv_w":
            return a.transpose(1, 0, 2)
        return flat(a).T if n in transposed else flat(a)

    def from_kernel(n, a, shape):
        if n == "conv_w":
            return a.transpose(1, 0, 2)
        return (a.T if n in transposed else a).reshape(shape)
    params = {
        "meta_tokens": (meta_tokens, m_meta_tokens, v_meta_tokens),
        "norm_g": (norm_g, m_norm_g, v_norm_g),
        "w_in": (w_in, m_w_in, v_w_in),
        "q_norm_g": (q_norm_g, m_q_norm_g, v_q_norm_g),
        "w_q_up": (w_q_up, m_w_q_up, v_w_q_up),
        "kv_norm_g": (kv_norm_g, m_kv_norm_g, v_kv_norm_g),
        "w_kv_up": (w_kv_up, m_w_kv_up, v_w_kv_up),
        "conv_w": (conv_w, m_conv_w, v_conv_w),
        "attn_out_g": (attn_out_g, m_attn_out_g, v_attn_out_g),
        "conv_out_g": (conv_out_g, m_conv_out_g, v_conv_out_g),
        "w_out": (w_out, m_w_out, v_w_out),
        "final_norm_g": (final_norm_g, m_final_norm_g, v_final_norm_g),
    }
    updated, loss = _reduce_update(r_in, r_out, r_small, g_meta, g_norm,
                                   {n: tuple(to_kernel(n, a) for a in t) for n, t in params.items()})
    outs = [[from_kernel(n, updated[n][i], params[n][0].shape) for n, _ in PARAM_SHAPES] for i in range(4)]
    return (loss[0, 0], grad_x, *outs[0], *outs[1], *outs[2], *outs[3])
```
